```python
import math
import jax, jax.numpy as jnp
from jax import lax
import numpy as np

D_MODEL = 1024
BATCH = 8
SEQ = 4096
DEPTH = 4

HEAD_DIM = 64
D_ATTN = D_MODEL // 2
ATTN_HEADS = D_ATTN // HEAD_DIM
Q_BLOCK = 128
D_CONV = D_MODEL // 4
CONV_K = 3
D_POOL = D_MODEL // 4
POOL_WINDOWS = (2, 4, 8, 16)
POOL_GROUPS = len(POOL_WINDOWS)
POOL_GROUP_DIM = D_POOL // POOL_GROUPS
POOL_OUT_DIM = D_MODEL // POOL_GROUPS
N_BRANCHES = 3
D_FF = -(-8 * D_MODEL // (3 * 256)) * 256
EPS = 1e-6

IN_SIZES = (D_ATTN, D_ATTN, D_ATTN, ATTN_HEADS, D_CONV, D_CONV, D_CONV, D_POOL, N_BRANCHES * D_MODEL)
D_IN = sum(IN_SIZES)
IN_SPLITS = tuple(int(v) for v in np.cumsum(IN_SIZES)[:-1])

kernel_name = "fox_conv_pool_gated_hybrid"


def rmsnorm(x, g):
    xf = x.astype(jnp.float32)
    y = xf * lax.rsqrt(jnp.mean(xf * xf, axis=-1, keepdims=True) + EPS)
    return (y * g.astype(jnp.float32)).astype(x.dtype)


def forgetting_attention(q, k, v, logf):
    S = q.shape[2]
    c = jnp.cumsum(logf, axis=-1)
    scale = HEAD_DIM ** -0.5
    outs = []
    for i in range(S // Q_BLOCK):
        q0, q1 = i * Q_BLOCK, (i + 1) * Q_BLOCK
        qb = q[:, :, q0:q1]
        kb = k[:, :, :q1]
        vb = v[:, :, :q1]
        logits = jnp.einsum('bhqd,bhkd->bhqk', qb, kb, preferred_element_type=jnp.float32) * scale
        logits = logits + (c[:, :, q0:q1, None] - c[:, :, None, :q1])
        causal = (q0 + jnp.arange(Q_BLOCK))[:, None] >= jnp.arange(q1)[None, :]
        logits = jnp.where(causal, logits, -jnp.inf)
        p = jax.nn.softmax(logits, axis=-1)
        outs.append(jnp.einsum('bhqk,bhkd->bhqd', p.astype(vb.dtype), vb))
    return jnp.concatenate(outs, axis=2)


def short_conv_mixer(u, b_gate, c_gate, w):
    S = u.shape[1]
    z = c_gate * u
    zp = jnp.pad(z, ((0, 0), (CONV_K - 1, 0), (0, 0)))
    conv = w[0] * zp[:, 0:S]
    for j in range(1, CONV_K):
        conv = conv + w[j] * zp[:, j:j + S]
    return b_gate * conv


def pooling_mixer(u, w_grp, scale):
    Bsz, S, _ = u.shape
    uf = u.astype(jnp.float32)
    cs = jnp.cumsum(uf, axis=1)
    t = jnp.arange(S, dtype=jnp.float32)
    groups = []
    for g, w in enumerate(POOL_WINDOWS):
        sl = slice(g * POOL_GROUP_DIM, (g + 1) * POOL_GROUP_DIM)
        csg = cs[:, :, sl]
        lagged = jnp.pad(csg[:, :S - w], ((0, 0), (w, 0), (0, 0)))
        counts = jnp.minimum(t + 1.0, float(w))[None, :, None]
        groups.append((csg - lagged) / counts - uf[:, :, sl])
    d = jnp.stack(groups, axis=2).astype(u.dtype)
    out = jnp.einsum('bsgc,gcd->bsgd', d, w_grp).reshape(Bsz, S, D_MODEL)
    return out * scale


def _fwd_setup_inputs(seed: int = 0) -> dict:
    key = jax.random.key(seed)
    ks = jax.random.split(key, 16)
    nrm = lambda k, shape, fan_in: jax.random.normal(k, shape, jnp.float32) * (fan_in ** -0.5)
    return {
        "x": jax.random.normal(ks[0], (BATCH, SEQ, D_MODEL), jnp.float32),
        "norm_mix_g": 1.0 + 0.02 * jax.random.normal(ks[1], (DEPTH, D_MODEL), jnp.float32),
        "w_in": nrm(ks[2], (DEPTH, D_MODEL, D_IN), D_MODEL),
        "forget_b": jax.random.uniform(ks[3], (DEPTH, ATTN_HEADS), jnp.float32, 2.0, 5.0),
        "q_norm_g": 1.0 + 0.02 * jax.random.normal(ks[4], (DEPTH, HEAD_DIM), jnp.float32),
        "k_norm_g": 1.0 + 0.02 * jax.random.normal(ks[5], (DEPTH, HEAD_DIM), jnp.float32),
        "w_attn_out": nrm(ks[6], (DEPTH, D_ATTN, D_MODEL), D_ATTN),
        "conv_w": nrm(ks[7], (DEPTH, CONV_K, D_CONV), CONV_K),
        "w_conv_out": nrm(ks[8], (DEPTH, D_CONV, D_MODEL), D_CONV),
        "pool_w": nrm(ks[9], (DEPTH, POOL_GROUPS, POOL_GROUP_DIM, POOL_OUT_DIM), POOL_GROUP_DIM),
        "pool_scale": 1.0 + 0.1 * jax.random.normal(ks[10], (DEPTH, D_MODEL), jnp.float32),
        "w_o": nrm(ks[11], (DEPTH, D_MODEL, D_MODEL), D_MODEL),
        "norm_ffn_g": 1.0 + 0.02 * jax.random.normal(ks[12], (DEPTH, D_MODEL), jnp.float32),
        "w_ffn_in": nrm(ks[13], (DEPTH, D_MODEL, 2 * D_FF), D_MODEL),
        "w_ffn_out": nrm(ks[14], (DEPTH, D_FF, D_MODEL), D_FF),
    }


def _fwd_reference(x, norm_mix_g, w_in, forget_b, q_norm_g, k_norm_g, w_attn_out, conv_w,
              w_conv_out, pool_w, pool_scale, w_o, norm_ffn_g, w_ffn_in, w_ffn_out):
    Bsz, S, _ = x.shape

    def heads(t):
        return t.reshape(Bsz, S, ATTN_HEADS, HEAD_DIM).transpose(0, 2, 1, 3)

    for l in range(DEPTH):
        h = rmsnorm(x, norm_mix_g[l])
        proj = h @ w_in[l]
        q, k, v, f_logit, cx, cb, cc, px, gate_logit = jnp.split(proj, IN_SPLITS, axis=-1)

        qh = rmsnorm(heads(q), q_norm_g[l])
        kh = rmsnorm(heads(k), k_norm_g[l])
        vh = heads(v)
        logf = jax.nn.log_sigmoid((f_logit + forget_b[l]).astype(jnp.float32)).transpose(0, 2, 1)
        a = forgetting_attention(qh, kh, vh, logf).transpose(0, 2, 1, 3).reshape(Bsz, S, D_ATTN)
        y_attn = a @ w_attn_out[l]

        y_conv = short_conv_mixer(cx, cb, cc, conv_w[l]) @ w_conv_out[l]

        y_pool = pooling_mixer(px, pool_w[l], pool_scale[l])

        g = jax.nn.sigmoid(gate_logit).reshape(Bsz, S, N_BRANCHES, D_MODEL)
        merged = g[:, :, 0] * y_attn + g[:, :, 1] * y_conv + g[:, :, 2] * y_pool
        x = x + merged @ w_o[l]

        h = rmsnorm(x, norm_ffn_g[l])
        gt, up = jnp.split(h @ w_ffn_in[l], 2, axis=-1)
        x = x + (jax.nn.silu(gt) * up) @ w_ffn_out[l]
    return x


import jax as _jax
import jax.numpy as _jnp

TWIN_FORMAT = 'train_step'
FWD_PARAMS = ['x', 'norm_mix_g', 'w_in', 'forget_b', 'q_norm_g', 'k_norm_g', 'w_attn_out', 'conv_w', 'w_conv_out', 'pool_w', 'pool_scale', 'w_o', 'norm_ffn_g', 'w_ffn_in', 'w_ffn_out']
TWIN_WEIGHTS = ['norm_mix_g', 'w_in', 'forget_b', 'q_norm_g', 'k_norm_g', 'w_attn_out', 'conv_w', 'w_conv_out', 'pool_w', 'pool_scale', 'w_o', 'norm_ffn_g', 'w_ffn_in', 'w_ffn_out']
TWIN_DIFF_INPUT = 'x'
TWIN_INPUTS = ['x', 'norm_mix_g', 'w_in', 'forget_b', 'q_norm_g', 'k_norm_g', 'w_attn_out', 'conv_w', 'w_conv_out', 'pool_w', 'pool_scale', 'w_o', 'norm_ffn_g', 'w_ffn_in', 'w_ffn_out', 'loss_target', 'm_norm_mix_g', 'm_w_in', 'm_forget_b', 'm_q_norm_g', 'm_k_norm_g', 'm_w_attn_out', 'm_conv_w', 'm_w_conv_out', 'm_pool_w', 'm_pool_scale', 'm_w_o', 'm_norm_ffn_g', 'm_w_ffn_in', 'm_w_ffn_out', 'v_norm_mix_g', 'v_w_in', 'v_forget_b', 'v_q_norm_g', 'v_k_norm_g', 'v_w_attn_out', 'v_conv_w', 'v_w_conv_out', 'v_pool_w', 'v_pool_scale', 'v_w_o', 'v_norm_ffn_g', 'v_w_ffn_in', 'v_w_ffn_out']
TWIN_OUTPUTS = ['loss', 'grad_x', 'grad_norm_mix_g', 'grad_w_in', 'grad_forget_b', 'grad_q_norm_g', 'grad_k_norm_g', 'grad_w_attn_out', 'grad_conv_w', 'grad_w_conv_out', 'grad_pool_w', 'grad_pool_scale', 'grad_w_o', 'grad_norm_ffn_g', 'grad_w_ffn_in', 'grad_w_ffn_out', 'delta_norm_mix_g', 'delta_w_in', 'delta_forget_b', 'delta_q_norm_g', 'delta_k_norm_g', 'delta_w_attn_out', 'delta_conv_w', 'delta_w_conv_out', 'delta_pool_w', 'delta_pool_scale', 'delta_w_o', 'delta_norm_ffn_g', 'delta_w_ffn_in', 'delta_w_ffn_out', 'new_m_norm_mix_g', 'new_m_w_in', 'new_m_forget_b', 'new_m_q_norm_g', 'new_m_k_norm_g', 'new_m_w_attn_out', 'new_m_conv_w', 'new_m_w_conv_out', 'new_m_pool_w', 'new_m_pool_scale', 'new_m_w_o', 'new_m_norm_ffn_g', 'new_m_w_ffn_in', 'new_m_w_ffn_out', 'new_v_norm_mix_g', 'new_v_w_in', 'new_v_forget_b', 'new_v_q_norm_g', 'new_v_k_norm_g', 'new_v_w_attn_out', 'new_v_conv_w', 'new_v_w_conv_out', 'new_v_pool_w', 'new_v_pool_scale', 'new_v_w_o', 'new_v_norm_ffn_g', 'new_v_w_ffn_in', 'new_v_w_ffn_out']
TWIN_LEAF_KINDS = {'loss': 'loss', 'grad_x': 'grad_x', 'grad_norm_mix_g': 'grad_w', 'grad_w_in': 'grad_w', 'grad_forget_b': 'grad_w', 'grad_q_norm_g': 'grad_w', 'grad_k_norm_g': 'grad_w', 'grad_w_attn_out': 'grad_w', 'grad_conv_w': 'grad_w', 'grad_w_conv_out': 'grad_w', 'grad_pool_w': 'grad_w', 'grad_pool_scale': 'grad_w', 'grad_w_o': 'grad_w', 'grad_norm_ffn_g': 'grad_w', 'grad_w_ffn_in': 'grad_w', 'grad_w_ffn_out': 'grad_w', 'delta_norm_mix_g': 'delta_w', 'delta_w_in': 'delta_w', 'delta_forget_b': 'delta_w', 'delta_q_norm_g': 'delta_w', 'delta_k_norm_g': 'delta_w', 'delta_w_attn_out': 'delta_w', 'delta_conv_w': 'delta_w', 'delta_w_conv_out': 'delta_w', 'delta_pool_w': 'delta_w', 'delta_pool_scale': 'delta_w', 'delta_w_o': 'delta_w', 'delta_norm_ffn_g': 'delta_w', 'delta_w_ffn_in': 'delta_w', 'delta_w_ffn_out': 'delta_w', 'new_m_norm_mix_g': 'new_m', 'new_m_w_in': 'new_m', 'new_m_forget_b': 'new_m', 'new_m_q_norm_g': 'new_m', 'new_m_k_norm_g': 'new_m', 'new_m_w_attn_out': 'new_m', 'new_m_conv_w': 'new_m', 'new_m_w_conv_out': 'new_m', 'new_m_pool_w': 'new_m', 'new_m_pool_scale': 'new_m', 'new_m_w_o': 'new_m', 'new_m_norm_ffn_g': 'new_m', 'new_m_w_ffn_in': 'new_m', 'new_m_w_ffn_out': 'new_m', 'new_v_norm_mix_g': 'new_v', 'new_v_w_in': 'new_v', 'new_v_forget_b': 'new_v', 'new_v_q_norm_g': 'new_v', 'new_v_k_norm_g': 'new_v', 'new_v_w_attn_out': 'new_v', 'new_v_conv_w': 'new_v', 'new_v_w_conv_out': 'new_v', 'new_v_pool_w': 'new_v', 'new_v_pool_scale': 'new_v', 'new_v_w_o': 'new_v', 'new_v_norm_ffn_g': 'new_v', 'new_v_w_ffn_in': 'new_v', 'new_v_w_ffn_out': 'new_v'}


def _forward(args):
    return _fwd_reference(*[args[k] for k in FWD_PARAMS])


def _output_shape():
    out = _jax.eval_shape(lambda: _forward(_fwd_setup_inputs(0)))
    return out.shape, out.dtype

N_MICROBATCH = 1
ADAM_LR = 0.001
ADAM_B1 = 0.9
ADAM_B2 = 0.999
ADAM_EPS = 1e-08
ADAM_WD = 0.01
ADAM_STEP = 10
PER_EXAMPLE_BATCH_AXIS = {'x': 0, 'loss_target': 0}
SHARED_INPUTS = []
_WEIGHT_DTYPES = {'norm_mix_g': _jnp.float32, 'w_in': _jnp.float32, 'forget_b': _jnp.float32, 'q_norm_g': _jnp.float32, 'k_norm_g': _jnp.float32, 'w_attn_out': _jnp.float32, 'conv_w': _jnp.float32, 'w_conv_out': _jnp.float32, 'pool_w': _jnp.float32, 'pool_scale': _jnp.float32, 'w_o': _jnp.float32, 'norm_ffn_g': _jnp.float32, 'w_ffn_in': _jnp.float32, 'w_ffn_out': _jnp.float32}
MOMENT_SCALE = {'norm_mix_g': 3.782491e+01, 'w_in': 7.542179e-01, 'forget_b': 3.343737e+01, 'q_norm_g': 4.800964e+00, 'k_norm_g': 4.807718e+00, 'w_attn_out': 1.887175e-01, 'conv_w': 2.101889e+01, 'w_conv_out': 9.070979e-01, 'pool_w': 1.046649e+00, 'pool_scale': 7.188030e+00, 'w_o': 1.212542e+00, 'norm_ffn_g': 2.442751e+01, 'w_ffn_in': 3.906548e-01, 'w_ffn_out': 6.610571e-01}


def _to_microbatches(a, axis):
    t = _jnp.moveaxis(a, axis, 0)
    t = t.reshape((N_MICROBATCH, t.shape[0] // N_MICROBATCH) + t.shape[1:])
    return _jnp.moveaxis(t, 1, axis + 1)


def setup_inputs(seed: int = 0) -> dict:
    inp = _fwd_setup_inputs(seed)
    key = _jax.random.fold_in(_jax.random.key(seed), 7919)
    shape, _ = _output_shape()
    out = dict(inp)
    out["loss_target"] = _jax.random.normal(_jax.random.fold_in(key, 0), shape, _jnp.float32)
    for i, name in enumerate(TWIN_WEIGHTS):
        w = inp[name].astype(_jnp.float32)
        if MOMENT_SCALE is None:
            s = _jnp.sqrt(_jnp.mean(_jnp.square(w)) + 1e-30)
        else:
            s = MOMENT_SCALE[name]
        km, kv = _jax.random.split(_jax.random.fold_in(key, i + 1))
        out[name] = w
        out["m_" + name] = s * _jax.random.normal(km, w.shape, _jnp.float32)
        out["v_" + name] = (s * s) * _jax.random.uniform(kv, w.shape, _jnp.float32, 0.5, 1.5)
    if N_MICROBATCH > 1:
        for name, axis in PER_EXAMPLE_BATCH_AXIS.items():
            out[name] = _to_microbatches(out[name], axis)
    return {'x': out['x'], 'norm_mix_g': out['norm_mix_g'], 'w_in': out['w_in'], 'forget_b': out['forget_b'], 'q_norm_g': out['q_norm_g'], 'k_norm_g': out['k_norm_g'], 'w_attn_out': out['w_attn_out'], 'conv_w': out['conv_w'], 'w_conv_out': out['w_conv_out'], 'pool_w': out['pool_w'], 'pool_scale': out['pool_scale'], 'w_o': out['w_o'], 'norm_ffn_g': out['norm_ffn_g'], 'w_ffn_in': out['w_ffn_in'], 'w_ffn_out': out['w_ffn_out'], 'loss_target': out['loss_target'], 'm_norm_mix_g': out['m_norm_mix_g'], 'm_w_in': out['m_w_in'], 'm_forget_b': out['m_forget_b'], 'm_q_norm_g': out['m_q_norm_g'], 'm_k_norm_g': out['m_k_norm_g'], 'm_w_attn_out': out['m_w_attn_out'], 'm_conv_w': out['m_conv_w'], 'm_w_conv_out': out['m_w_conv_out'], 'm_pool_w': out['m_pool_w'], 'm_pool_scale': out['m_pool_scale'], 'm_w_o': out['m_w_o'], 'm_norm_ffn_g': out['m_norm_ffn_g'], 'm_w_ffn_in': out['m_w_ffn_in'], 'm_w_ffn_out': out['m_w_ffn_out'], 'v_norm_mix_g': out['v_norm_mix_g'], 'v_w_in': out['v_w_in'], 'v_forget_b': out['v_forget_b'], 'v_q_norm_g': out['v_q_norm_g'], 'v_k_norm_g': out['v_k_norm_g'], 'v_w_attn_out': out['v_w_attn_out'], 'v_conv_w': out['v_conv_w'], 'v_w_conv_out': out['v_w_conv_out'], 'v_pool_w': out['v_pool_w'], 'v_pool_scale': out['v_pool_scale'], 'v_w_o': out['v_w_o'], 'v_norm_ffn_g': out['v_norm_ffn_g'], 'v_w_ffn_in': out['v_w_ffn_in'], 'v_w_ffn_out': out['v_w_ffn_out']}


def _loss(weights, diff, rest, loss_target):
    with _jax.named_scope("forward"):
        args = {**rest, TWIN_DIFF_INPUT: diff, **{k: w.astype(_WEIGHT_DTYPES[k]) for k, w in weights.items()}}
        y = _forward(args)
    with _jax.named_scope("loss_head"):
        err = _jnp.square(y.astype(_jnp.float32) - loss_target)
        return 0.5 * _jnp.sum(_jnp.mean(err, axis=-1)) if err.ndim else 0.5 * err


def _adamw(w, g, m, v):
    m = ADAM_B1 * m + (1.0 - ADAM_B1) * g
    v = ADAM_B2 * v + (1.0 - ADAM_B2) * _jnp.square(g)
    m_hat = m / (1.0 - ADAM_B1 ** ADAM_STEP)
    v_hat = v / (1.0 - ADAM_B2 ** ADAM_STEP)
    delta = -ADAM_LR * (m_hat / (_jnp.sqrt(v_hat) + ADAM_EPS) + ADAM_WD * w)
    return delta, m, v


def reference(x, norm_mix_g, w_in, forget_b, q_norm_g, k_norm_g, w_attn_out, conv_w, w_conv_out, pool_w, pool_scale, w_o, norm_ffn_g, w_ffn_in, w_ffn_out, loss_target, m_norm_mix_g, m_w_in, m_forget_b, m_q_norm_g, m_k_norm_g, m_w_attn_out, m_conv_w, m_w_conv_out, m_pool_w, m_pool_scale, m_w_o, m_norm_ffn_g, m_w_ffn_in, m_w_ffn_out, v_norm_mix_g, v_w_in, v_forget_b, v_q_norm_g, v_k_norm_g, v_w_attn_out, v_conv_w, v_w_conv_out, v_pool_w, v_pool_scale, v_w_o, v_norm_ffn_g, v_w_ffn_in, v_w_ffn_out):
    given = dict(x=x, norm_mix_g=norm_mix_g, w_in=w_in, forget_b=forget_b, q_norm_g=q_norm_g, k_norm_g=k_norm_g, w_attn_out=w_attn_out, conv_w=conv_w, w_conv_out=w_conv_out, pool_w=pool_w, pool_scale=pool_scale, w_o=w_o, norm_ffn_g=norm_ffn_g, w_ffn_in=w_ffn_in, w_ffn_out=w_ffn_out, loss_target=loss_target, m_norm_mix_g=m_norm_mix_g, m_w_in=m_w_in, m_forget_b=m_forget_b, m_q_norm_g=m_q_norm_g, m_k_norm_g=m_k_norm_g, m_w_attn_out=m_w_attn_out, m_conv_w=m_conv_w, m_w_conv_out=m_w_conv_out, m_pool_w=m_pool_w, m_pool_scale=m_pool_scale, m_w_o=m_w_o, m_norm_ffn_g=m_norm_ffn_g, m_w_ffn_in=m_w_ffn_in, m_w_ffn_out=m_w_ffn_out, v_norm_mix_g=v_norm_mix_g, v_w_in=v_w_in, v_forget_b=v_forget_b, v_q_norm_g=v_q_norm_g, v_k_norm_g=v_k_norm_g, v_w_attn_out=v_w_attn_out, v_conv_w=v_conv_w, v_w_conv_out=v_w_conv_out, v_pool_w=v_pool_w, v_pool_scale=v_pool_scale, v_w_o=v_w_o, v_norm_ffn_g=v_norm_ffn_g, v_w_ffn_in=v_w_ffn_in, v_w_ffn_out=v_w_ffn_out)
    weights = {n: given[n] for n in TWIN_WEIGHTS}
    shared = {n: given[n] for n in SHARED_INPUTS}
    per_example = {n: given[n] for n in ['x']}
    grad_fn = _jax.value_and_grad(_loss, argnums=(0, 1))

    def one_microbatch(ex, loss_target):
        ex = dict(ex)
        diff = ex.pop(TWIN_DIFF_INPUT)
        return grad_fn(weights, diff, {**shared, **ex}, loss_target)

    if N_MICROBATCH == 1:
        loss, (grad_w, grad_x) = one_microbatch(per_example, given["loss_target"])
    else:
        def body(carry, xs):
            loss_sum, grad_sum = carry
            l_k, (gw_k, gx_k) = one_microbatch(xs[0], xs[1])
            with _jax.named_scope("update"):
                return (loss_sum + l_k, _jax.tree.map(_jnp.add, grad_sum, gw_k)), gx_k

        init = (_jnp.zeros((), _jnp.float32), _jax.tree.map(_jnp.zeros_like, weights))
        (loss, grad_w), grad_x = _jax.lax.scan(body, init, (per_example, given["loss_target"]))
    with _jax.named_scope("update"):
        delta_w, new_m, new_v = {}, {}, {}
        for n in TWIN_WEIGHTS:
            delta_w[n], new_m[n], new_v[n] = _adamw(weights[n], grad_w[n], given["m_" + n], given["v_" + n])
    return (loss, grad_x, *[grad_w[n] for n in TWIN_WEIGHTS], *[delta_w[n] for n in TWIN_WEIGHTS],
            *[new_m[n] for n in TWIN_WEIGHTS], *[new_v[n] for n in TWIN_WEIGHTS])
```

```python
import functools

import numpy as np
import jax
import jax.numpy as jnp
from jax import lax
from jax.experimental import pallas as pl
from jax.experimental.pallas import tpu as pltpu

F32 = jnp.float32
BF16 = jnp.bfloat16

D_MODEL = 1024
DEPTH = 4
HEAD_DIM = 64
N_HEADS = 8
D_ATTN = 512
D_CONV = 256
D_POOL = 256
D_FF = 2816
D_IN = 5640
POOL_WINDOWS = (2, 4, 8, 16)
EPS = 1e-6
ADAM_LR, ADAM_B1, ADAM_B2, ADAM_EPS, ADAM_WD, ADAM_STEP = 0.001, 0.9, 0.999, 1e-08, 0.01, 10

D_QKV = 3 * D_ATTN
D_F = 128
D_B = 3 * D_CONV + D_POOL + 3 * D_MODEL
D_LOCAL = 3 * D_CONV + D_POOL

LANES = 128
HALO = 16
VMEM_LIMIT = 56 * 1024 * 1024
NEG = -1e30

TM = 256
TM_MIX = 128
TQ = 256


def _dot(a, b):
    return jnp.dot(a, b, preferred_element_type=F32)


def _dot_nt(a, b):
    return lax.dot_general(a, b, (((1,), (1,)), ((), ())), preferred_element_type=F32)


def _dot_tn(a, b):
    return lax.dot_general(a, b, (((0,), (0,)), ((), ())), preferred_element_type=F32)


def _params(*sem):
    return pltpu.CompilerParams(dimension_semantics=sem, vmem_limit_bytes=VMEM_LIMIT)


def _rows(tm, n):
    return pl.BlockSpec((tm, n), lambda i: (i, 0))


def _whole(a):
    nd = a.ndim
    return pl.BlockSpec(a.shape, lambda *_: (0,) * nd)


def _split_bf16(x):
    hi = x.astype(BF16)
    lo = (x - hi.astype(F32)).astype(BF16)
    return hi, lo


def _sigmoid(x):
    return 1.0 / (1.0 + jnp.exp(-x))


def in_proj_fwd(x, g, wa, wf, wb):
    S = x.shape[0]

    def body(x_ref, g_ref, wa_ref, wf_ref, wb_ref, h_ref, pa_ref, pf_ref, pb_ref):
        xf = x_ref[...]
        r = lax.rsqrt(jnp.mean(xf * xf, axis=-1, keepdims=True) + EPS)
        h = (xf * r * g_ref[...]).astype(BF16)
        h_ref[...] = h
        pa_ref[...] = _dot(h, wa_ref[...]).astype(BF16)
        pf_ref[...] = _dot(h, wf_ref[...])
        pb_ref[...] = _dot(h, wb_ref[...]).astype(BF16)

    return pl.pallas_call(
        body,
        grid=(S // TM,),
        in_specs=[_rows(TM, D_MODEL), _whole(g), _whole(wa), _whole(wf), _whole(wb)],
        out_specs=[_rows(TM, D_MODEL), _rows(TM, D_QKV), _rows(TM, D_F), _rows(TM, D_B)],
        out_shape=[
            jax.ShapeDtypeStruct((S, D_MODEL), BF16),
            jax.ShapeDtypeStruct((S, D_QKV), BF16),
            jax.ShapeDtypeStruct((S, D_F), F32),
            jax.ShapeDtypeStruct((S, D_B), BF16),
        ],
        compiler_params=_params("parallel"),
        name="in_proj_fwd",
    )(x, g, wa, wf, wb)


def in_proj_bwd(x, g, dxm, dpa, dpf, dpb, wa, wf, wb):
    S = x.shape[0]

    def body(x_ref, g_ref, dxm_ref, dpa_ref, dpf_ref, dpb_ref, wa_ref, wf_ref, wb_ref, dx_ref, dg_ref):
        @pl.when(pl.program_id(0) == 0)
        def _():
            dg_ref[...] = jnp.zeros_like(dg_ref)

        dh = _dot_nt(dpa_ref[...], wa_ref[...]) + _dot_nt(dpf_ref[...], wf_ref[...]) + _dot_nt(dpb_ref[...], wb_ref[...])
        xf = x_ref[...]
        r = lax.rsqrt(jnp.mean(xf * xf, axis=-1, keepdims=True) + EPS)
        xhat = xf * r
        dg_ref[...] += jnp.sum(dh * xhat, axis=0, keepdims=True)
        gdh = dh * g_ref[...]
        dx_ref[...] = dxm_ref[...] + r * (gdh - xhat * jnp.mean(xhat * gdh, axis=-1, keepdims=True))

    return pl.pallas_call(
        body,
        grid=(S // TM,),
        in_specs=[_rows(TM, D_MODEL), _whole(g), _rows(TM, D_MODEL), _rows(TM, D_QKV), _rows(TM, D_F), _rows(TM, D_B),
                  _whole(wa), _whole(wf), _whole(wb)],
        out_specs=[_rows(TM, D_MODEL), pl.BlockSpec((1, D_MODEL), lambda i: (0, 0))],
        out_shape=[jax.ShapeDtypeStruct((S, D_MODEL), F32), jax.ShapeDtypeStruct((1, D_MODEL), F32)],
        compiler_params=_params("arbitrary"),
        name="in_proj_bwd",
    )(x, g, dxm, dpa, dpf, dpb, wa, wf, wb)


def wgrad(xa, dy, name, tn=512, ts=512):
    S, K = xa.shape
    N = dy.shape[1]
    tn = min(tn, N)
    n_steps = S // ts

    def body(x_ref, dy_ref, o_ref):
        @pl.when(pl.program_id(1) == 0)
        def _():
            o_ref[...] = jnp.zeros_like(o_ref)

        o_ref[...] += _dot_tn(x_ref[...], dy_ref[...])

    return pl.pallas_call(
        body,
        grid=(N // tn, n_steps),
        in_specs=[pl.BlockSpec((ts, K), lambda j, k: (k, 0)), pl.BlockSpec((ts, tn), lambda j, k: (k, j))],
        out_specs=pl.BlockSpec((K, tn), lambda j, k: (0, j)),
        out_shape=jax.ShapeDtypeStruct((K, N), F32),
        compiler_params=_params("parallel", "arbitrary"),
        name=name,
    )(xa, dy)


def _head_mean_matrix():
    h = np.arange(D_ATTN) // HEAD_DIM
    return jnp.asarray((h[:, None] == h[None, :]).astype(np.float32) / HEAD_DIM, BF16)


def _head_spread_matrix():
    src = np.arange(LANES)[:, None]
    dst = np.arange(N_HEADS * LANES)[None, :] // LANES
    return jnp.asarray((src == dst).astype(np.float32), BF16)


def _tri(n, upper):
    r = np.arange(n)
    m = (r[None, :] >= r[:, None]) if upper else (r[None, :] <= r[:, None])
    return jnp.asarray(m.astype(np.float32), BF16)


def qk_prep(pa, pf, gq, gk, fb):
    S = pa.shape[0]
    bd = _head_mean_matrix()
    spread = _head_spread_matrix()
    tri = _tri(TM, upper=False)

    def body(q_ref, k_ref, pf_ref, gq_ref, gk_ref, fb_ref, bd_ref, sp_ref, tri_ref, qn_ref, kn_ref, cp_ref, ct_ref,
             carry, carry_p):
        @pl.when(pl.program_id(0) == 0)
        def _():
            carry[...] = jnp.zeros_like(carry)
            carry_p[...] = jnp.zeros_like(carry_p)

        def head_norm(x_ref, g_ref, scale):
            xf = x_ref[...].astype(F32)
            ms = _dot((xf * xf).astype(BF16), bd_ref[...])
            return (xf * lax.rsqrt(ms + EPS) * g_ref[...] * scale).astype(BF16)

        qn_ref[...] = head_norm(q_ref, gq_ref, HEAD_DIM ** -0.5)
        kn_ref[...] = head_norm(k_ref, gk_ref, 1.0)

        z = pf_ref[...] + fb_ref[...]
        logf = jnp.minimum(z, 0.0) - jnp.log(1.0 + jnp.exp(-jnp.abs(z)))
        hi, lo = _split_bf16(logf)
        c = _dot(tri_ref[...], hi) + _dot(tri_ref[...], lo) + carry[...]
        carry[...] += jnp.sum(hi.astype(F32) + lo.astype(F32), axis=0, keepdims=True)
        ct_ref[...] = c.T[0:N_HEADS, :]
        hi_p = _dot(hi, sp_ref[...]).astype(BF16)
        lo_p = _dot(lo, sp_ref[...]).astype(BF16)
        c_p = _dot(tri_ref[...], hi_p) + _dot(tri_ref[...], lo_p) + carry_p[...]
        carry_p[...] += jnp.sum(hi_p.astype(F32) + lo_p.astype(F32), axis=0, keepdims=True)
        for h in range(N_HEADS):
            cp_ref[h] = c_p[:, h * LANES:(h + 1) * LANES]

    return pl.pallas_call(
        body,
        grid=(S // TM,),
        in_specs=[pl.BlockSpec((TM, D_ATTN), lambda i: (i, 0)), pl.BlockSpec((TM, D_ATTN), lambda i: (i, 1)),
                  _rows(TM, D_F), _whole(gq), _whole(gk), _whole(fb), _whole(bd), _whole(spread), _whole(tri)],
        out_specs=[_rows(TM, D_ATTN), _rows(TM, D_ATTN), pl.BlockSpec((N_HEADS, TM, LANES), lambda i: (0, i, 0)),
                   pl.BlockSpec((N_HEADS, TM), lambda i: (0, i))],
        out_shape=[jax.ShapeDtypeStruct((S, D_ATTN), BF16), jax.ShapeDtypeStruct((S, D_ATTN), BF16),
                   jax.ShapeDtypeStruct((N_HEADS, S, LANES), F32), jax.ShapeDtypeStruct((N_HEADS, S), F32)],
        scratch_shapes=[pltpu.VMEM((1, D_F), F32), pltpu.VMEM((1, N_HEADS * LANES), F32)],
        compiler_params=_params("arbitrary"),
        name="qk_prep",
    )(pa, pa, pf, gq, gk, fb, bd, spread, tri)


def _wide(col):
    return jnp.concatenate([col] * (TQ // LANES), axis=1)


def attn_fwd(qn, kn, pa, cp, ct):
    S = qn.shape[0]
    nq = S // TQ
    v_blk0 = 2 * D_ATTN // LANES

    def body(q_ref, k_ref, v_ref, cp_ref, ct_ref, o_ref, lse_ref):
        p = pl.program_id(0)
        i = pl.program_id(1)
        lane = lax.broadcasted_iota(jnp.int32, (TQ, LANES), 1)
        row = lax.broadcasted_iota(jnp.int32, (TQ, TQ), 0)
        col = lax.broadcasted_iota(jnp.int32, (TQ, TQ), 1)
        q = q_ref[...]
        outs = []
        for j in range(2):
            in_head = (lane >= HEAD_DIM * j) & (lane < HEAD_DIM * (j + 1))
            qm = jnp.where(in_head, q, jnp.zeros_like(q))
            ct_wide = _wide(cp_ref[j])

            def step(kt, carry, diagonal):
                m, l, acc = carry
                ks = pl.multiple_of(kt * TQ, TQ)
                k = k_ref[pl.ds(ks, TQ), :]
                v = v_ref[pl.ds(ks, TQ), :]
                cs = ct_ref[pl.ds(2 * p + j, 1), pl.ds(ks, TQ)]
                z = _dot_nt(qm, k) + (ct_wide - cs)
                if diagonal:
                    z = jnp.where(row >= col, z, NEG)
                m_new = jnp.maximum(m, jnp.max(z, axis=1, keepdims=True))
                alpha = jnp.exp(m - m_new)
                pr = jnp.exp(z - m_new)
                l = alpha * l + jnp.sum(pr, axis=1, keepdims=True)
                acc = alpha * acc + _dot(pr.astype(BF16), v)
                return m_new, l, acc

            init = (jnp.full((TQ, 1), NEG, F32), jnp.zeros((TQ, 1), F32), jnp.zeros((TQ, LANES), F32))
            carry = lax.fori_loop(0, i, functools.partial(step, diagonal=False), init)
            m, l, acc = step(i, carry, True)
            outs.append(acc / l)
            lse_ref[j] = jnp.broadcast_to(m + jnp.log(l), (TQ, LANES))
        o_ref[...] = jnp.where(lane < HEAD_DIM, outs[0], outs[1])

    return pl.pallas_call(
        body,
        grid=(N_HEADS // 2, nq),
        in_specs=[pl.BlockSpec((TQ, LANES), lambda p, i: (i, p)),
                  pl.BlockSpec((S, LANES), lambda p, i: (0, p)),
                  pl.BlockSpec((S, LANES), lambda p, i: (0, v_blk0 + p)),
                  pl.BlockSpec((2, TQ, LANES), lambda p, i: (p, i, 0)),
                  pl.BlockSpec((N_HEADS, S), lambda p, i: (0, 0))],
        out_specs=[pl.BlockSpec((TQ, LANES), lambda p, i: (i, p)),
                   pl.BlockSpec((2, TQ, LANES), lambda p, i: (p, i, 0))],
        out_shape=[jax.ShapeDtypeStruct((S, D_ATTN), F32), jax.ShapeDtypeStruct((N_HEADS, S, LANES), F32)],
        compiler_params=_params("parallel", "parallel"),
        name="attn_fwd",
    )(qn, kn, pa, cp, ct)


def attn_bwd(qn, kn, pa, do, o, cp, ct, lse):
    S = qn.shape[0]
    nq = S // TQ
    v_blk0 = 2 * D_ATTN // LANES

    def body(q_ref, k_ref, v_ref, do_ref, o_ref, cp_ref, ct_ref, lse_ref, dq_ref, dk_ref, dv_ref, dct_ref, dcq_ref):
        p = pl.program_id(0)
        kt = pl.program_id(1)

        @pl.when(kt == 0)
        def _():
            dq_ref[...] = jnp.zeros_like(dq_ref)

        @pl.when((kt == 0) & (p == 0))
        def _():
            dcq_ref[...] = jnp.zeros_like(dcq_ref)

        lane = lax.broadcasted_iota(jnp.int32, (TQ, LANES), 1)
        row = lax.broadcasted_iota(jnp.int32, (TQ, TQ), 0)
        col = lax.broadcasted_iota(jnp.int32, (TQ, TQ), 1)
        k = k_ref[...]
        v = v_ref[...]
        ks = pl.multiple_of(kt * TQ, TQ)
        dk = jnp.zeros((TQ, LANES), F32)
        dv = jnp.zeros((TQ, LANES), F32)
        for j in range(2):
            in_head = (lane >= HEAD_DIM * j) & (lane < HEAD_DIM * (j + 1))
            km = jnp.where(in_head, k, jnp.zeros_like(k))
            cs = ct_ref[pl.ds(2 * p + j, 1), pl.ds(ks, TQ)]

            def step(qi, carry, diagonal):
                dk, dv, dcs = carry
                qs = pl.multiple_of(qi * TQ, TQ)
                q = q_ref[pl.ds(qs, TQ), :]
                dout = do_ref[pl.ds(qs, TQ), :]
                out = o_ref[pl.ds(qs, TQ), :]
                qm = jnp.where(in_head, q, jnp.zeros_like(q))
                dom = jnp.where(in_head, dout, jnp.zeros_like(dout))
                delta = jnp.sum(jnp.where(in_head, dout.astype(F32) * out, 0.0), axis=1, keepdims=True)
                z = _dot_nt(qm, k) + (_wide(cp_ref[j, pl.ds(qs, TQ), :]) - cs)
                if diagonal:
                    z = jnp.where(row >= col, z, NEG)
                pr = jnp.exp(z - _wide(lse_ref[j, pl.ds(qs, TQ), :]))
                dv = dv + _dot_tn(pr.astype(BF16), dom)
                ds = pr * (_dot_nt(dom, v) - delta)
                dsb = ds.astype(BF16)
                dk = dk + _dot_tn(dsb, qm)
                dq_ref[pl.ds(qs, TQ), :] += _dot(dsb, km)
                dcs = dcs + jnp.sum(ds, axis=0, keepdims=True)
                dcq_ref[pl.ds(qs, TQ), :] += jnp.where(lane == 2 * p + j, jnp.sum(ds, axis=1, keepdims=True), 0.0)
                return dk, dv, dcs

            carry = step(kt, (dk, dv, jnp.zeros((1, TQ), F32)), True)
            dk, dv, dcs = lax.fori_loop(kt + 1, nq, functools.partial(step, diagonal=False), carry)
            dct_ref[pl.ds(2 * p + j, 1), pl.ds(ks, TQ)] = -dcs
        dk_ref[...] = dk
        dv_ref[...] = dv

    pair_cols = lambda p, kt: (0, p)
    return pl.pallas_call(
        body,
        grid=(N_HEADS // 2, nq),
        in_specs=[pl.BlockSpec((S, LANES), pair_cols),
                  pl.BlockSpec((TQ, LANES), lambda p, kt: (kt, p)),
                  pl.BlockSpec((TQ, LANES), lambda p, kt: (kt, v_blk0 + p)),
                  pl.BlockSpec((S, LANES), pair_cols),
                  pl.BlockSpec((S, LANES), pair_cols),
                  pl.BlockSpec((2, S, LANES), lambda p, kt: (p, 0, 0)),
                  pl.BlockSpec((N_HEADS, S), lambda p, kt: (0, 0)),
                  pl.BlockSpec((2, S, LANES), lambda p, kt: (p, 0, 0))],
        out_specs=[pl.BlockSpec((S, LANES), pair_cols),
                   pl.BlockSpec((TQ, LANES), lambda p, kt: (kt, p)),
                   pl.BlockSpec((TQ, LANES), lambda p, kt: (kt, p)),
                   pl.BlockSpec((N_HEADS, S), lambda p, kt: (0, 0)),
                   pl.BlockSpec((S, LANES), lambda p, kt: (0, 0))],
        out_shape=[jax.ShapeDtypeStruct((S, D_ATTN), F32), jax.ShapeDtypeStruct((S, D_ATTN), F32),
                   jax.ShapeDtypeStruct((S, D_ATTN), F32), jax.ShapeDtypeStruct((N_HEADS, S), F32),
                   jax.ShapeDtypeStruct((S, LANES), F32)],
        compiler_params=_params("arbitrary", "arbitrary"),
        name="attn_bwd",
    )(qn, kn, pa, do, o, cp, ct, lse)


def attn_bwd_post(pa, pf, dqn, dkn, dv, dct, dcq, gq, gk, fb):
    S = pa.shape[0]
    nt = S // TM
    bd = _head_mean_matrix()
    triu = _tri(TM, upper=True)
    rev = lambda i: nt - 1 - i

    def body(q_ref, k_ref, pf_ref, dqn_ref, dkn_ref, dv_ref, dct_ref, dcq_ref, gq_ref, gk_ref, fb_ref, bd_ref, triu_ref,
             dpa_ref, dpf_ref, dgq_ref, dgk_ref, dfb_ref, carry):
        @pl.when(pl.program_id(0) == 0)
        def _():
            carry[...] = jnp.zeros_like(carry)
            dgq_ref[...] = jnp.zeros_like(dgq_ref)
            dgk_ref[...] = jnp.zeros_like(dgk_ref)
            dfb_ref[...] = jnp.zeros_like(dfb_ref)

        def head_norm_bwd(x_ref, dy, g_ref, dg_ref):
            xf = x_ref[...].astype(F32)
            r = lax.rsqrt(_dot((xf * xf).astype(BF16), bd_ref[...]) + EPS)
            xhat = xf * r
            dg_ref[...] += jnp.sum(dy * xhat, axis=0, keepdims=True)
            gdy = dy * g_ref[...]
            return (r * (gdy - xhat * _dot((xhat * gdy).astype(BF16), bd_ref[...]))).astype(BF16)

        dpa_ref[:, 0:D_ATTN] = head_norm_bwd(q_ref, dqn_ref[...] * HEAD_DIM ** -0.5, gq_ref, dgq_ref)
        dpa_ref[:, D_ATTN:2 * D_ATTN] = head_norm_bwd(k_ref, dkn_ref[...], gk_ref, dgk_ref)
        dpa_ref[:, 2 * D_ATTN:3 * D_ATTN] = dv_ref[...].astype(BF16)

        dc = jnp.concatenate([dct_ref[...], jnp.zeros((LANES - N_HEADS, TM), F32)], axis=0)
        hi, lo = _split_bf16(dc)
        qhi, qlo = _split_bf16(dcq_ref[...])
        dlogf = (_dot_nt(triu_ref[...], hi) + _dot(triu_ref[...], qhi)
                 + (_dot_nt(triu_ref[...], lo) + _dot(triu_ref[...], qlo)) + carry[...])
        first = lax.broadcasted_iota(jnp.int32, (TM, D_F), 0) == 0
        carry[...] = jnp.sum(jnp.where(first, dlogf, 0.0), axis=0, keepdims=True)
        df = dlogf * _sigmoid(-(pf_ref[...] + fb_ref[...]))
        dfb_ref[...] += jnp.sum(df, axis=0, keepdims=True)
        dpf_ref[...] = df.astype(BF16)

    return pl.pallas_call(
        body,
        grid=(nt,),
        in_specs=[pl.BlockSpec((TM, D_ATTN), lambda i: (rev(i), 0)), pl.BlockSpec((TM, D_ATTN), lambda i: (rev(i), 1)),
                  pl.BlockSpec((TM, D_F), lambda i: (rev(i), 0)),
                  pl.BlockSpec((TM, D_ATTN), lambda i: (rev(i), 0)), pl.BlockSpec((TM, D_ATTN), lambda i: (rev(i), 0)),
                  pl.BlockSpec((TM, D_ATTN), lambda i: (rev(i), 0)),
                  pl.BlockSpec((N_HEADS, TM), lambda i: (0, rev(i))),
                  pl.BlockSpec((TM, LANES), lambda i: (rev(i), 0)),
                  _whole(gq), _whole(gk), _whole(fb), _whole(bd), _whole(triu)],
        out_specs=[pl.BlockSpec((TM, D_QKV), lambda i: (rev(i), 0)), pl.BlockSpec((TM, D_F), lambda i: (rev(i), 0)),
                   pl.BlockSpec((1, D_ATTN), lambda i: (0, 0)), pl.BlockSpec((1, D_ATTN), lambda i: (0, 0)),
                   pl.BlockSpec((1, D_F), lambda i: (0, 0))],
        out_shape=[jax.ShapeDtypeStruct((S, D_QKV), BF16), jax.ShapeDtypeStruct((S, D_F), BF16),
                   jax.ShapeDtypeStruct((1, D_ATTN), F32), jax.ShapeDtypeStruct((1, D_ATTN), F32),
                   jax.ShapeDtypeStruct((1, D_F), F32)],
        scratch_shapes=[pltpu.VMEM((1, D_F), F32)],
        compiler_params=_params("arbitrary"),
        name="attn_bwd_post",
    )(pa, pa, pf, dqn, dkn, dv, dct, dcq, gq, gk, fb, bd, triu)


def _shift_down(ext, k):
    return pltpu.roll(ext, k, 0)[HALO:]


def _shift_up(ext, k, n):
    return pltpu.roll(ext, n + HALO - k, 0)[:n]


def _pool_lane_select(a2, a4, a8, a16, lane):
    return jnp.where(lane < 64, a2, jnp.where(lane < 128, a4, jnp.where(lane < 192, a8, a16)))


def _local_branches(o, pb, halo, have_prev, row0, wao, convw, wco, wpool, pscale):
    n = pb.shape[0]
    cx = pb[:, 0:D_CONV].astype(F32)
    cb = pb[:, D_CONV:2 * D_CONV].astype(F32)
    cc = pb[:, 2 * D_CONV:3 * D_CONV].astype(F32)
    px = pb[:, 3 * D_CONV:D_LOCAL].astype(F32)
    keep = have_prev.astype(F32)
    z = cc * cx
    z_ext = jnp.concatenate([halo[:, 2 * D_CONV:3 * D_CONV].astype(F32) * halo[:, 0:D_CONV].astype(F32) * keep, z], axis=0)
    z1 = _shift_down(z_ext, 1)
    z2 = _shift_down(z_ext, 2)
    conv = convw[0:1, :] * z2 + convw[1:2, :] * z1 + convw[2:3, :] * z
    cm = cb * conv

    u_ext = jnp.concatenate([halo[:, 3 * D_CONV:D_LOCAL].astype(F32) * keep, px], axis=0)
    s2 = u_ext + pltpu.roll(u_ext, 1, 0)
    s4 = s2 + pltpu.roll(s2, 2, 0)
    s8 = s4 + pltpu.roll(s4, 4, 0)
    s16 = s8 + pltpu.roll(s8, 8, 0)
    lane = lax.broadcasted_iota(jnp.int32, (n, D_POOL), 1)
    win = _pool_lane_select(2.0, 4.0, 8.0, 16.0, lane)
    t = (row0 + lax.broadcasted_iota(jnp.int32, (n, D_POOL), 0)).astype(F32)
    cnt = jnp.minimum(t + 1.0, win)
    feat = _pool_lane_select(s2[HALO:], s4[HALO:], s8[HALO:], s16[HALO:], lane) / cnt - px

    ya = _dot(o, wao)
    yc = _dot(cm.astype(BF16), wco)
    yp_pre = _dot(feat.astype(BF16), wpool)
    yp = yp_pre * pscale
    return dict(cx=cx, cb=cb, cc=cc, z=z, z1=z1, z2=z2, conv=conv, cm=cm, feat=feat, cnt=cnt, lane=lane,
                ya=ya, yc=yc, yp_pre=yp_pre, yp=yp)


def _halo_spec(tm, tile_of):
    per = tm // HALO
    return pl.BlockSpec((HALO, D_LOCAL), lambda i: (jnp.maximum(tile_of(i) * per - 1, 0), 0))


def mix_out_fwd(x, o, pb, wao, convw, wco, wpool, pscale, wo):
    S = x.shape[0]
    tm = TM_MIX

    def body(x_ref, o_ref, pb_ref, halo_ref, wao_ref, cw_ref, wco_ref, wp_ref, ps_ref, wo_ref, y_ref):
        i = pl.program_id(0)
        pb = pb_ref[...]
        b = _local_branches(o_ref[...].astype(BF16), pb, halo_ref[...], i > 0, i * tm, wao_ref[...], cw_ref[...], wco_ref[...],
                            wp_ref[...], ps_ref[...])
        g0 = _sigmoid(pb[:, D_LOCAL:D_LOCAL + D_MODEL].astype(F32))
        g1 = _sigmoid(pb[:, D_LOCAL + D_MODEL:D_LOCAL + 2 * D_MODEL].astype(F32))
        g2 = _sigmoid(pb[:, D_LOCAL + 2 * D_MODEL:D_B].astype(F32))
        merged = g0 * b["ya"] + g1 * b["yc"] + g2 * b["yp"]
        y_ref[...] = x_ref[...] + _dot(merged.astype(BF16), wo_ref[...])

    return pl.pallas_call(
        body,
        grid=(S // tm,),
        in_specs=[_rows(tm, D_MODEL), _rows(tm, D_ATTN), _rows(tm, D_B), _halo_spec(tm, lambda i: i),
                  _whole(wao), _whole(convw), _whole(wco), _whole(wpool), _whole(pscale), _whole(wo)],
        out_specs=_rows(tm, D_MODEL),
        out_shape=jax.ShapeDtypeStruct((S, D_MODEL), F32),
        compiler_params=_params("parallel"),
        name="mix_out_fwd",
    )(x, o, pb, pb, wao, convw, wco, wpool, pscale, wo)


def mix_out_bwd(dxm, o, pb, wao, convw, wco, wpool, pscale, wo):
    S = dxm.shape[0]
    tm = TM_MIX
    nt = S // tm
    rev = lambda i: nt - 1 - i
    rows = lambda n: pl.BlockSpec((tm, n), lambda i: (rev(i), 0))
    acc = lambda r, c: pl.BlockSpec((r, c), lambda i: (0, 0))

    def body(dxm_ref, o_ref, pb_ref, halo_ref, wao_ref, cw_ref, wco_ref, wp_ref, ps_ref, wo_ref,
             dpb_ref, da_ref, dwo_ref, dwao_ref, dwco_ref, dwp_ref, dcw_ref, dps_ref, next_dconv, next_e):
        i = pl.program_id(0)
        r = rev(i)

        @pl.when(i == 0)
        def _():
            for ref in (dwo_ref, dwao_ref, dwco_ref, dwp_ref, dcw_ref, dps_ref, next_dconv, next_e):
                ref[...] = jnp.zeros_like(ref)

        pb = pb_ref[...]
        o = o_ref[...].astype(BF16)
        cw = cw_ref[...]
        b = _local_branches(o, pb, halo_ref[...], r > 0, r * tm, wao_ref[...], cw, wco_ref[...], wp_ref[...], ps_ref[...])
        g0 = _sigmoid(pb[:, D_LOCAL:D_LOCAL + D_MODEL].astype(F32))
        g1 = _sigmoid(pb[:, D_LOCAL + D_MODEL:D_LOCAL + 2 * D_MODEL].astype(F32))
        g2 = _sigmoid(pb[:, D_LOCAL + 2 * D_MODEL:D_B].astype(F32))
        dxb = dxm_ref[...].astype(BF16)
        merged = g0 * b["ya"] + g1 * b["yc"] + g2 * b["yp"]
        dwo_ref[...] += _dot_tn(merged.astype(BF16), dxb)
        dmer = _dot_nt(dxb, wo_ref[...])
        dpb_ref[:, D_LOCAL:D_LOCAL + D_MODEL] = (dmer * b["ya"] * (g0 * (1.0 - g0))).astype(BF16)
        dpb_ref[:, D_LOCAL + D_MODEL:D_LOCAL + 2 * D_MODEL] = (dmer * b["yc"] * (g1 * (1.0 - g1))).astype(BF16)
        dpb_ref[:, D_LOCAL + 2 * D_MODEL:D_B] = (dmer * b["yp"] * (g2 * (1.0 - g2))).astype(BF16)

        dya = (dmer * g0).astype(BF16)
        dwao_ref[...] += _dot_tn(o, dya)
        da_ref[...] = _dot_nt(dya, wao_ref[...]).astype(BF16)

        dyc = (dmer * g1).astype(BF16)
        dwco_ref[...] += _dot_tn(b["cm"].astype(BF16), dyc)
        dcm = _dot_nt(dyc, wco_ref[...])
        dconv = dcm * b["cb"]
        dcw_ref[0:1, :] += jnp.sum(dconv * b["z2"], axis=0, keepdims=True)
        dcw_ref[1:2, :] += jnp.sum(dconv * b["z1"], axis=0, keepdims=True)
        dcw_ref[2:3, :] += jnp.sum(dconv * b["z"], axis=0, keepdims=True)
        d_ext = jnp.concatenate([dconv, next_dconv[...]], axis=0)
        dz = cw[2:3, :] * dconv + cw[1:2, :] * _shift_up(d_ext, 1, tm) + cw[0:1, :] * _shift_up(d_ext, 2, tm)
        next_dconv[...] = dconv[0:HALO]
        dpb_ref[:, 0:D_CONV] = (dz * b["cc"]).astype(BF16)
        dpb_ref[:, D_CONV:2 * D_CONV] = (dcm * b["conv"]).astype(BF16)
        dpb_ref[:, 2 * D_CONV:3 * D_CONV] = (dz * b["cx"]).astype(BF16)

        dyp = dmer * g2
        dps_ref[...] += jnp.sum(dyp * b["yp_pre"], axis=0, keepdims=True)
        dyps = (dyp * ps_ref[...]).astype(BF16)
        dwp_ref[...] += _dot_tn(b["feat"].astype(BF16), dyps)
        dfeat = _dot_nt(dyps, wp_ref[...])
        e = dfeat / b["cnt"]
        e_ext = jnp.concatenate([e, next_e[...]], axis=0)
        up = lambda a, k: pltpu.roll(a, tm + HALO - k, 0)
        f2 = e_ext + up(e_ext, 1)
        f4 = f2 + up(f2, 2)
        f8 = f4 + up(f4, 4)
        f16 = f8 + up(f8, 8)
        next_e[...] = e[0:HALO]
        dpb_ref[:, 3 * D_CONV:D_LOCAL] = (_pool_lane_select(f2[:tm], f4[:tm], f8[:tm], f16[:tm], b["lane"]) - dfeat).astype(BF16)

    return pl.pallas_call(
        body,
        grid=(nt,),
        in_specs=[rows(D_MODEL), rows(D_ATTN), rows(D_B), _halo_spec(tm, rev),
                  _whole(wao), _whole(convw), _whole(wco), _whole(wpool), _whole(pscale), _whole(wo)],
        out_specs=[rows(D_B), rows(D_ATTN), acc(D_MODEL, D_MODEL), acc(D_ATTN, D_MODEL), acc(D_CONV, D_MODEL),
                   acc(D_POOL, D_MODEL), acc(8, D_CONV), acc(1, D_MODEL)],
        out_shape=[jax.ShapeDtypeStruct((S, D_B), BF16), jax.ShapeDtypeStruct((S, D_ATTN), BF16),
                   jax.ShapeDtypeStruct((D_MODEL, D_MODEL), F32), jax.ShapeDtypeStruct((D_ATTN, D_MODEL), F32),
                   jax.ShapeDtypeStruct((D_CONV, D_MODEL), F32), jax.ShapeDtypeStruct((D_POOL, D_MODEL), F32),
                   jax.ShapeDtypeStruct((8, D_CONV), F32), jax.ShapeDtypeStruct((1, D_MODEL), F32)],
        scratch_shapes=[pltpu.VMEM((HALO, D_CONV), F32), pltpu.VMEM((HALO, D_POOL), F32)],
        compiler_params=_params("arbitrary"),
        name="mix_out_bwd",
    )(dxm, o, pb, pb, wao, convw, wco, wpool, pscale, wo)


def ffn_fwd(x, g, w1, w2):
    S = x.shape[0]

    def body(x_ref, g_ref, w1_ref, w2_ref, y_ref, u_ref):
        xf = x_ref[...]
        r = lax.rsqrt(jnp.mean(xf * xf, axis=-1, keepdims=True) + EPS)
        h = (xf * r * g_ref[...]).astype(BF16)
        u = _dot(h, w1_ref[...])
        u_ref[...] = u.astype(BF16)
        gt = u[:, 0:D_FF]
        act = gt * _sigmoid(gt) * u[:, D_FF:2 * D_FF]
        y_ref[...] = xf + _dot(act.astype(BF16), w2_ref[...])

    return pl.pallas_call(
        body,
        grid=(S // TM,),
        in_specs=[_rows(TM, D_MODEL), _whole(g), _whole(w1), _whole(w2)],
        out_specs=[_rows(TM, D_MODEL), _rows(TM, 2 * D_FF)],
        out_shape=[jax.ShapeDtypeStruct((S, D_MODEL), F32), jax.ShapeDtypeStruct((S, 2 * D_FF), BF16)],
        compiler_params=_params("parallel"),
        name="ffn_fwd",
    )(x, g, w1, w2)


def ffn_bwd(x, g, dy, u, w1, w2):
    S = x.shape[0]

    def body(x_ref, g_ref, dy_ref, u_ref, w1_ref, w2_ref, dx_ref, du_ref, act_ref, h_ref, dg_ref):
        @pl.when(pl.program_id(0) == 0)
        def _():
            dg_ref[...] = jnp.zeros_like(dg_ref)

        dyf = dy_ref[...]
        dact = _dot_nt(dyf.astype(BF16), w2_ref[...])
        gt = u_ref[:, 0:D_FF].astype(F32)
        up = u_ref[:, D_FF:2 * D_FF].astype(F32)
        sg = _sigmoid(gt)
        silu = gt * sg
        act_ref[...] = (silu * up).astype(BF16)
        du_ref[:, 0:D_FF] = (dact * up * (sg * (1.0 + gt * (1.0 - sg)))).astype(BF16)
        du_ref[:, D_FF:2 * D_FF] = (dact * silu).astype(BF16)
        dh = _dot_nt(du_ref[...], w1_ref[...])
        xf = x_ref[...]
        r = lax.rsqrt(jnp.mean(xf * xf, axis=-1, keepdims=True) + EPS)
        xhat = xf * r
        h_ref[...] = (xhat * g_ref[...]).astype(BF16)
        dg_ref[...] += jnp.sum(dh * xhat, axis=0, keepdims=True)
        gdh = dh * g_ref[...]
        dx_ref[...] = dyf + r * (gdh - xhat * jnp.mean(xhat * gdh, axis=-1, keepdims=True))

    return pl.pallas_call(
        body,
        grid=(S // TM,),
        in_specs=[_rows(TM, D_MODEL), _whole(g), _rows(TM, D_MODEL), _rows(TM, 2 * D_FF), _whole(w1), _whole(w2)],
        out_specs=[_rows(TM, D_MODEL), _rows(TM, 2 * D_FF), _rows(TM, D_FF), _rows(TM, D_MODEL),
                   pl.BlockSpec((1, D_MODEL), lambda i: (0, 0))],
        out_shape=[jax.ShapeDtypeStruct((S, D_MODEL), F32), jax.ShapeDtypeStruct((S, 2 * D_FF), BF16),
                   jax.ShapeDtypeStruct((S, D_FF), BF16), jax.ShapeDtypeStruct((S, D_MODEL), BF16),
                   jax.ShapeDtypeStruct((1, D_MODEL), F32)],
        compiler_params=_params("arbitrary"),
        name="ffn_bwd",
    )(x, g, dy, u, w1, w2)


def loss_head(y, target):
    S = y.shape[0]

    def body(y_ref, t_ref, loss_ref, dy_ref):
        @pl.when(pl.program_id(0) == 0)
        def _():
            loss_ref[0, 0] = 0.0

        err = y_ref[...] - t_ref[...]
        dy_ref[...] = err * (1.0 / D_MODEL)
        loss_ref[0, 0] += 0.5 * jnp.sum(jnp.mean(err * err, axis=-1))

    return pl.pallas_call(
        body,
        grid=(S // TM,),
        in_specs=[_rows(TM, D_MODEL), _rows(TM, D_MODEL)],
        out_specs=[pl.BlockSpec((1, 1), lambda i: (0, 0), memory_space=pltpu.SMEM), _rows(TM, D_MODEL)],
        out_shape=[jax.ShapeDtypeStruct((1, 1), F32), jax.ShapeDtypeStruct((S, D_MODEL), F32)],
        compiler_params=_params("arbitrary"),
        name="loss_head",
    )(y, target)


def layer_weights(w_in, forget_b, q_norm_g, k_norm_g, w_attn_out, conv_w, w_conv_out, pool_w, pool_scale, w_o,
                  norm_mix_g, norm_ffn_g, w_ffn_in, w_ffn_out):
    f0 = D_QKV
    b0 = D_QKV + N_HEADS
    wpool = jnp.zeros((D_POOL, D_MODEL), BF16)
    gd = D_POOL // len(POOL_WINDOWS)
    od = D_MODEL // len(POOL_WINDOWS)
    for g in range(len(POOL_WINDOWS)):
        wpool = wpool.at[g * gd:(g + 1) * gd, g * od:(g + 1) * od].set(pool_w[g].astype(BF16))
    return dict(
        wa=w_in[:, 0:f0].astype(BF16),
        wf=jnp.pad(w_in[:, f0:b0], ((0, 0), (0, D_F - N_HEADS))).astype(BF16),
        wb=w_in[:, b0:D_IN].astype(BF16),
        fb=jnp.pad(forget_b.astype(F32), (0, D_F - N_HEADS)).reshape(1, D_F),
        gq=jnp.tile(q_norm_g.astype(F32), N_HEADS).reshape(1, D_ATTN),
        gk=jnp.tile(k_norm_g.astype(F32), N_HEADS).reshape(1, D_ATTN),
        wao=w_attn_out.astype(BF16),
        convw=jnp.pad(conv_w.astype(F32), ((0, 8 - conv_w.shape[0]), (0, 0))),
        wco=w_conv_out.astype(BF16),
        wpool=wpool,
        pscale=pool_scale.astype(F32).reshape(1, D_MODEL),
        wo=w_o.astype(BF16),
        g1=norm_mix_g.astype(F32).reshape(1, D_MODEL),
        g2=norm_ffn_g.astype(F32).reshape(1, D_MODEL),
        w1=w_ffn_in.astype(BF16),
        w2=w_ffn_out.astype(BF16),
    )


def local_step(x, target, layers):
    saved = []
    for w in layers:
        h, pa, pf, pb = in_proj_fwd(x, w["g1"], w["wa"], w["wf"], w["wb"])
        qn, kn, cp, ct = qk_prep(pa, pf, w["gq"], w["gk"], w["fb"])
        o, lse = attn_fwd(qn, kn, pa, cp, ct)
        xm = mix_out_fwd(x, o, pb, w["wao"], w["convw"], w["wco"], w["wpool"], w["pscale"], w["wo"])
        y, u = ffn_fwd(xm, w["g2"], w["w1"], w["w2"])
        saved.append(dict(x=x, h=h, pa=pa, pf=pf, pb=pb, qn=qn, kn=kn, cp=cp, ct=ct, o=o, lse=lse, xm=xm, u=u))
        x = y
    loss, dx = loss_head(x, target)
    grads = [None] * len(layers)
    for li in reversed(range(len(layers))):
        w, s = layers[li], saved[li]
        dxm, du, act, h2, dg2 = ffn_bwd(s["xm"], w["g2"], dx, s["u"], w["w1"], w["w2"])
        dw2 = wgrad(act, dx.astype(BF16), "wgrad_ffn_out")
        dw1 = wgrad(h2, du, "wgrad_ffn_in")
        dpb, da, dwo, dwao, dwco, dwpool, dconvw, dpscale = mix_out_bwd(
            dxm, s["o"], s["pb"], w["wao"], w["convw"], w["wco"], w["wpool"], w["pscale"], w["wo"])
        dqn, dkn, dv, dct, dcq = attn_bwd(s["qn"], s["kn"], s["pa"], da, s["o"], s["cp"], s["ct"], s["lse"])
        dpa, dpf, dgq, dgk, dfb = attn_bwd_post(s["pa"], s["pf"], dqn, dkn, dv, dct, dcq, w["gq"], w["gk"], w["fb"])
        dx, dg1 = in_proj_bwd(s["x"], w["g1"], dxm, dpa, dpf, dpb, w["wa"], w["wf"], w["wb"])
        dwa = wgrad(s["h"], dpa, "wgrad_in_qkv")
        dwf = wgrad(s["h"], dpf, "wgrad_in_f")
        dwb = wgrad(s["h"], dpb, "wgrad_in_b")
        gd = D_POOL // len(POOL_WINDOWS)
        od = D_MODEL // len(POOL_WINDOWS)
        grads[li] = dict(
            norm_mix_g=dg1[0],
            w_in=jnp.concatenate([dwa, dwf[:, 0:N_HEADS], dwb], axis=1),
            forget_b=dfb[0, 0:N_HEADS],
            q_norm_g=dgq.reshape(N_HEADS, HEAD_DIM).sum(0),
            k_norm_g=dgk.reshape(N_HEADS, HEAD_DIM).sum(0),
            w_attn_out=dwao,
            conv_w=dconvw[0:3],
            w_conv_out=dwco,
            pool_w=jnp.stack([dwpool[g * gd:(g + 1) * gd, g * od:(g + 1) * od] for g in range(len(POOL_WINDOWS))]),
            pool_scale=dpscale[0],
            w_o=dwo,
            norm_ffn_g=dg2[0],
            w_ffn_in=dw1,
            w_ffn_out=dw2,
        )
    return loss, dx, grads


def adamw(w, g, m, v, name):
    R, C = w.shape
    tm = 256 if R % 256 == 0 else R

    def body(w_ref, g_ref, m_ref, v_ref, d_ref, nm_ref, nv_ref):
        gr = g_ref[...]
        m_new = ADAM_B1 * m_ref[...] + (1.0 - ADAM_B1) * gr
        v_new = ADAM_B2 * v_ref[...] + (1.0 - ADAM_B2) * jnp.square(gr)
        nm_ref[...] = m_new
        nv_ref[...] = v_new
        m_hat = m_new / (1.0 - ADAM_B1 ** ADAM_STEP)
        v_hat = v_new / (1.0 - ADAM_B2 ** ADAM_STEP)
        d_ref[...] = -ADAM_LR * (m_hat / (jnp.sqrt(v_hat) + ADAM_EPS) + ADAM_WD * w_ref[...])

    spec = _rows(tm, C)
    out = jax.ShapeDtypeStruct((R, C), F32)
    return pl.pallas_call(
        body,
        grid=(R // tm,),
        in_specs=[spec] * 4,
        out_specs=[spec] * 3,
        out_shape=[out] * 3,
        compiler_params=_params("parallel"),
        name=name,
    )(w, g, m, v)


MESH = pl.DeviceIdType.MESH
HBM_REF = pl.BlockSpec(memory_space=pl.ANY)
N_CHIPS = 4
N_DEV = 8
PACK_COLS = 1024
PACK_ROWS = 16384
HALF_ROWS = PACK_ROWS // 2
SMALL_ROWS = 104


def _mesh_pos():
    return lax.axis_index("x"), lax.axis_index("y"), lax.axis_index("c")


def _other_chips(x, y):
    return [(1 - x, y), (x, 1 - y), (1 - x, 1 - y)]


def gather_shards(packed):
    def body(in_ref, out_ref, send_sems, recv_sems, local_sem):
        x, y, c = _mesh_pos()
        me = 2 * x + y
        sibling = (x, y, 1 - c)
        chips = _other_chips(x, y)

        def half(chip, h):
            return out_ref.at[chip, pl.ds(h * HALF_ROWS, HALF_ROWS), :]

        def copy(j, src, dst, to):
            return pltpu.make_async_remote_copy(src_ref=src, dst_ref=dst, send_sem=send_sems.at[j],
                                                recv_sem=recv_sems.at[j], device_id=to, device_id_type=MESH)

        mine = pltpu.make_async_copy(in_ref, out_ref.at[me], local_sem)
        mine.start()
        my_half = in_ref.at[pl.ds(c * HALF_ROWS, HALF_ROWS), :]
        sent = [copy(j, my_half, half(me, c), (*chip, c)) for j, chip in enumerate(chips)]
        for cp in sent:
            cp.start()
        for j, (px, py) in enumerate(chips):
            k = 2 * px + py
            copy(j, my_half, half(k, c), (px, py, c)).wait_recv()
            passed = copy(3 + j, half(k, c), half(k, c), sibling)
            passed.start()
            sent.append(passed)
        for j, (px, py) in enumerate(chips):
            k = 2 * px + py
            copy(3 + j, half(k, 1 - c), half(k, 1 - c), sibling).wait_recv()
        for cp in sent:
            cp.wait_send()
        mine.wait()

    return pl.pallas_call(
        body,
        in_specs=[HBM_REF],
        out_specs=HBM_REF,
        out_shape=jax.ShapeDtypeStruct((N_CHIPS, PACK_ROWS, PACK_COLS), packed.dtype),
        scratch_shapes=[pltpu.SemaphoreType.DMA((6,)), pltpu.SemaphoreType.DMA((6,)), pltpu.SemaphoreType.DMA],
        name="gather_shards",
    )(packed)


def pair_exchange(grads):
    def body(g_ref, t_ref, send_sem, recv_sem):
        x, y, c = _mesh_pos()
        cp = pltpu.make_async_remote_copy(
            src_ref=g_ref.at[pl.ds(0, N_CHIPS), pl.ds((1 - c) * HALF_ROWS, HALF_ROWS), :], dst_ref=t_ref,
            send_sem=send_sem, recv_sem=recv_sem, device_id=(x, y, 1 - c), device_id_type=MESH)
        cp.start()
        cp.wait()

    return pl.pallas_call(
        body,
        in_specs=[HBM_REF],
        out_specs=HBM_REF,
        out_shape=jax.ShapeDtypeStruct((N_CHIPS, HALF_ROWS, PACK_COLS), grads.dtype),
        scratch_shapes=[pltpu.SemaphoreType.DMA, pltpu.SemaphoreType.DMA],
        name="pair_exchange",
    )(grads)


def pair_sum(core, grads, theirs, tm=512):
    per_half = HALF_ROWS // tm

    def body(c_ref, g_ref, t_ref, o_ref):
        o_ref[...] = (g_ref[...] + t_ref[...]).astype(BF16)

    return pl.pallas_call(
        body,
        grid_spec=pltpu.PrefetchScalarGridSpec(
            num_scalar_prefetch=1,
            grid=(N_CHIPS, per_half),
            in_specs=[pl.BlockSpec((None, tm, PACK_COLS), lambda j, i, c: (j, c[0] * per_half + i, 0)),
                      pl.BlockSpec((None, tm, PACK_COLS), lambda j, i, c: (j, i, 0))],
            out_specs=pl.BlockSpec((None, tm, PACK_COLS), lambda j, i, c: (j, i, 0)),
        ),
        out_shape=jax.ShapeDtypeStruct((N_CHIPS, HALF_ROWS, PACK_COLS), BF16),
        compiler_params=_params("parallel", "parallel"),
        name="pair_sum",
    )(core, grads, theirs)


def chip_exchange(sums):
    def body(p_ref, r_ref, send_sems, recv_sems, local_sem):
        x, y, c = _mesh_pos()
        me = 2 * x + y
        chips = _other_chips(x, y)
        own = pltpu.make_async_copy(p_ref.at[me], r_ref.at[me], local_sem)
        own.start()

        def copy(j, src, dst, to):
            return pltpu.make_async_remote_copy(src_ref=src, dst_ref=dst, send_sem=send_sems.at[j],
                                                recv_sem=recv_sems.at[j], device_id=to, device_id_type=MESH)

        sent = []
        for j, (px, py) in enumerate(chips):
            cp = copy(j, p_ref.at[2 * px + py], r_ref.at[me], (px, py, c))
            cp.start()
            sent.append(cp)
        for j, (px, py) in enumerate(chips):
            k = 2 * px + py
            copy(j, p_ref.at[k], r_ref.at[k], (px, py, c)).wait_recv()
        for cp in sent:
            cp.wait_send()
        own.wait()

    return pl.pallas_call(
        body,
        in_specs=[HBM_REF],
        out_specs=HBM_REF,
        out_shape=jax.ShapeDtypeStruct((N_CHIPS, HALF_ROWS, PACK_COLS), sums.dtype),
        scratch_shapes=[pltpu.SemaphoreType.DMA((3,)), pltpu.SemaphoreType.DMA((3,)), pltpu.SemaphoreType.DMA],
        name="chip_exchange",
    )(sums)


def chip_sum(recv, tm=512):
    def body(r_ref, o_ref):
        acc = r_ref[0].astype(F32)
        for s in range(1, N_CHIPS):
            acc = acc + r_ref[s].astype(F32)
        o_ref[...] = acc

    return pl.pallas_call(
        body,
        grid=(HALF_ROWS // tm,),
        in_specs=[pl.BlockSpec((N_CHIPS, tm, PACK_COLS), lambda i: (0, i, 0))],
        out_specs=_rows(tm, PACK_COLS),
        out_shape=jax.ShapeDtypeStruct((HALF_ROWS, PACK_COLS), F32),
        compiler_params=_params("parallel"),
        name="chip_sum",
    )(recv)


def sibling_share(total):
    def body(t_ref, f_ref, send_sem, recv_sem, local_sem):
        x, y, c = _mesh_pos()

        def half(h):
            return f_ref.at[pl.ds(h * HALF_ROWS, HALF_ROWS), :]

        own = pltpu.make_async_copy(t_ref, half(c), local_sem)
        own.start()
        cp = pltpu.make_async_remote_copy(src_ref=t_ref, dst_ref=half(c), send_sem=send_sem, recv_sem=recv_sem,
                                          device_id=(x, y, 1 - c), device_id_type=MESH)
        cp.start()
        pltpu.make_async_remote_copy(src_ref=t_ref, dst_ref=half(1 - c), send_sem=send_sem, recv_sem=recv_sem,
                                     device_id=(x, y, 1 - c), device_id_type=MESH).wait_recv()
        cp.wait_send()
        own.wait()

    return pl.pallas_call(
        body,
        in_specs=[HBM_REF],
        out_specs=HBM_REF,
        out_shape=jax.ShapeDtypeStruct((PACK_ROWS, PACK_COLS), total.dtype),
        scratch_shapes=[pltpu.SemaphoreType.DMA, pltpu.SemaphoreType.DMA, pltpu.SemaphoreType.DMA],
        name="sibling_share",
    )(total)


def small_allgather(small):
    def body(s_ref, a_ref, send_sems, recv_sems, local_sem):
        x, y, c = _mesh_pos()
        me = 4 * x + 2 * y + c
        own = pltpu.make_async_copy(s_ref, a_ref.at[me], local_sem)
        own.start()
        sent = []
        for k in range(1, N_DEV):
            px, py, pc = x ^ (k >> 2), y ^ ((k >> 1) & 1), c ^ (k & 1)
            cp = pltpu.make_async_remote_copy(src_ref=s_ref, dst_ref=a_ref.at[me], send_sem=send_sems.at[k - 1],
                                              recv_sem=recv_sems.at[k - 1], device_id=(px, py, pc), device_id_type=MESH)
            cp.start()
            sent.append(cp)
        for k in range(1, N_DEV):
            px, py, pc = x ^ (k >> 2), y ^ ((k >> 1) & 1), c ^ (k & 1)
            pltpu.make_async_remote_copy(src_ref=s_ref, dst_ref=a_ref.at[4 * px + 2 * py + pc], send_sem=send_sems.at[k - 1],
                                         recv_sem=recv_sems.at[k - 1], device_id=(px, py, pc), device_id_type=MESH).wait_recv()
        for cp in sent:
            cp.wait_send()
        own.wait()

    return pl.pallas_call(
        body,
        in_specs=[HBM_REF],
        out_specs=HBM_REF,
        out_shape=jax.ShapeDtypeStruct((N_DEV, SMALL_ROWS, LANES), small.dtype),
        scratch_shapes=[pltpu.SemaphoreType.DMA((N_DEV - 1,)), pltpu.SemaphoreType.DMA((N_DEV - 1,)), pltpu.SemaphoreType.DMA],
        name="small_allgather",
    )(small)


def small_sum(blocks):
    def body(a_ref, o_ref):
        acc = a_ref[0]
        for d in range(1, N_DEV):
            acc = acc + a_ref[d]
        o_ref[...] = acc

    return pl.pallas_call(
        body,
        in_specs=[pl.BlockSpec(memory_space=pltpu.VMEM)],
        out_specs=pl.BlockSpec(memory_space=pltpu.VMEM),
        out_shape=jax.ShapeDtypeStruct((SMALL_ROWS, LANES), F32),
        name="small_sum",
    )(blocks)


SPLIT = {
    "w_in": ((D_MODEL, D_IN), 1),
    "w_attn_out": ((D_ATTN, D_MODEL), 1),
    "w_conv_out": ((D_CONV, D_MODEL), 1),
    "pool_w": ((len(POOL_WINDOWS), D_POOL // len(POOL_WINDOWS), D_MODEL // len(POOL_WINDOWS)), 2),
    "w_o": ((D_MODEL, D_MODEL), 0),
    "w_ffn_in": ((D_MODEL, 2 * D_FF), 1),
    "w_ffn_out": ((D_FF, D_MODEL), 0),
    "conv_w": ((3, D_CONV), 1),
}
SMALL = {"norm_mix_g": D_MODEL, "forget_b": N_HEADS, "q_norm_g": HEAD_DIM, "k_norm_g": HEAD_DIM,
         "pool_scale": D_MODEL, "norm_ffn_g": D_MODEL}
WEIGHTS = ["norm_mix_g", "w_in", "forget_b", "q_norm_g", "k_norm_g", "w_attn_out", "conv_w", "w_conv_out", "pool_w",
           "pool_scale", "w_o", "norm_ffn_g", "w_ffn_in", "w_ffn_out"]


def _shard_shape(name):
    shape, axis = SPLIT[name]
    return tuple(n // N_CHIPS if a == axis else n for a, n in enumerate(shape))


def _to_rows(flat, rows):
    pad = rows * PACK_COLS - flat.shape[-1]
    flat = jnp.pad(flat, [(0, 0)] * (flat.ndim - 1) + [(0, pad)])
    return flat.reshape(flat.shape[:-1] + (rows, PACK_COLS))


def pack_weight_shards(shards):
    parts = [shards[n].astype(BF16).reshape(-1) for n in SPLIT if n != "conv_w"]
    parts.append(lax.bitcast_convert_type(shards["conv_w"], BF16).reshape(-1))
    return _to_rows(jnp.concatenate(parts), PACK_ROWS)


def unpack_gathered(gathered):
    flat = gathered.reshape(N_CHIPS, -1)
    full, off = {}, 0
    for n, (_, axis) in SPLIT.items():
        shape = (N_CHIPS, DEPTH) + _shard_shape(n)
        if n == "conv_w":
            size = 2 * int(np.prod(shape[1:]))
            piece = lax.bitcast_convert_type(flat[:, off:off + size].reshape(shape + (2,)), F32)
        else:
            size = int(np.prod(shape[1:]))
            piece = flat[:, off:off + size].reshape(shape)
        full[n] = jnp.concatenate([piece[j] for j in range(N_CHIPS)], axis=axis + 1)
        off += size
    return full


def pack_grads(grads):
    parts = []
    for n, (_, axis) in SPLIT.items():
        parts.append(jnp.stack(jnp.split(grads[n], N_CHIPS, axis=axis + 1)).reshape(N_CHIPS, -1))
    return _to_rows(jnp.concatenate(parts, axis=1), PACK_ROWS)


def unpack_reduced(packed):
    flat = packed.reshape(-1)
    out, off = {}, 0
    for n in SPLIT:
        shape = (DEPTH,) + _shard_shape(n)
        size = int(np.prod(shape))
        out[n] = flat[off:off + size].reshape(shape)
        off += size
    return out


def pack_small(grads, loss):
    parts = [grads[n].reshape(-1) for n in SMALL] + [loss.reshape(-1)]
    return _to_rows(jnp.concatenate(parts), SMALL_ROWS * LANES // PACK_COLS).reshape(SMALL_ROWS, LANES)


def unpack_small(packed):
    flat = packed.reshape(-1)
    out, off = {}, 0
    for n, size in SMALL.items():
        out[n] = flat[off:off + DEPTH * size].reshape(DEPTH, size)
        off += DEPTH * size
    return out, flat[off]


def kernel(x, norm_mix_g, w_in, forget_b, q_norm_g, k_norm_g, w_attn_out, conv_w, w_conv_out, pool_w, pool_scale, w_o, norm_ffn_g, w_ffn_in, w_ffn_out, loss_target, m_norm_mix_g, m_w_in, m_forget_b, m_q_norm_g, m_k_norm_g, m_w_attn_out, m_conv_w, m_w_conv_out, m_pool_w, m_pool_scale, m_w_o, m_norm_ffn_g, m_w_ffn_in, m_w_ffn_out, v_norm_mix_g, v_w_in, v_forget_b, v_q_norm_g, v_k_norm_g, v_w_attn_out, v_conv_w, v_w_conv_out, v_pool_w, v_pool_scale, v_w_o, v_norm_ffn_g, v_w_ffn_in, v_w_ffn_out):
    given = dict(locals())
    weights = {n: given[n] for n in WEIGHTS}

    full = unpack_gathered(gather_shards(pack_weight_shards(weights)))
    full.update({n: weights[n] for n in SMALL})
    order = ["w_in", "forget_b", "q_norm_g", "k_norm_g", "w_attn_out", "conv_w", "w_conv_out", "pool_w", "pool_scale", "w_o",
             "norm_mix_g", "norm_ffn_g", "w_ffn_in", "w_ffn_out"]
    layers = [layer_weights(*[full[n][l] for n in order]) for l in range(DEPTH)]

    loss, dx, layer_grads = local_step(x[0], loss_target[0], layers)
    grads = {n: jnp.stack([g[n] for g in layer_grads]) for n in WEIGHTS}

    packed = pack_grads(grads)
    core = lax.axis_index("c").astype(jnp.int32).reshape(1)
    sums = pair_sum(core, packed, pair_exchange(packed))
    reduced = unpack_reduced(sibling_share(chip_sum(chip_exchange(sums))))

    small, loss_sum = unpack_small(small_sum(small_allgather(pack_small(grads, loss))))
    reduced.update(small)

    deltas, new_m, new_v = {}, {}, {}
    for n in WEIGHTS:
        w = weights[n]
        flat = (-1, w.shape[-1])
        d, nm, nv = adamw(w.reshape(flat), reduced[n].reshape(flat), given["m_" + n].reshape(flat),
                          given["v_" + n].reshape(flat), "adamw_" + n)
        deltas[n], new_m[n], new_v[n] = d.reshape(w.shape), nm.reshape(w.shape), nv.reshape(w.shape)
    return (loss_sum, dx[None], *[reduced[n] for n in WEIGHTS], *[deltas[n] for n in WEIGHTS],
            *[new_m[n] for n in WEIGHTS], *[new_v[n] for n in WEIGHTS])
```

```python
import functools

import numpy as np
import jax
import jax.numpy as jnp
from jax import lax
from jax.experimental import pallas as pl
from jax.experimental.pallas import tpu as pltpu

F32 = jnp.float32
BF16 = jnp.bfloat16

D_MODEL = 1024
DEPTH = 4
HEAD_DIM = 64
N_HEADS = 8
D_ATTN = 512
D_CONV = 256
D_POOL = 256
D_FF = 2816
D_IN = 5640
CONV_K = 3
POOL_WINDOWS = (2, 4, 8, 16)
N_GROUPS = len(POOL_WINDOWS)
EPS = 1e-6
ADAM_LR, ADAM_B1, ADAM_B2, ADAM_EPS, ADAM_WD, ADAM_STEP = 0.001, 0.9, 0.999, 1e-08, 0.01, 10

D_QKV = 3 * D_ATTN
D_F = 128
D_B = 3 * D_CONV + D_POOL + 3 * D_MODEL
D_LOCAL = 3 * D_CONV + D_POOL

LANES = 128
D_HEADS = N_HEADS * LANES
HALO = 16
VMEM_LIMIT = 56 * 1024 * 1024
NEG = -1e30
LOG2E = 1.4426950408889634
LN2 = 0.6931471805599453

TM = 256
TM_MIX = 128
TQ = 512
TK = 1024
KB = 512
QT = 1024

LANE_C = 64
LANE_ONE = 67
LANE_LSE = 70
N_PIECES = 3


def _dot(a, b):
    return jnp.dot(a, b, preferred_element_type=F32)


def _dot_nt(a, b):
    return lax.dot_general(a, b, (((1,), (1,)), ((), ())), preferred_element_type=F32)


def _dot_tn(a, b):
    return lax.dot_general(a, b, (((0,), (0,)), ((), ())), preferred_element_type=F32)


def _params(*sem):
    return pltpu.CompilerParams(dimension_semantics=sem, vmem_limit_bytes=VMEM_LIMIT)


def _rows(tm, n):
    return pl.BlockSpec((tm, n), lambda i, *_: (i, 0))


def _whole(a):
    nd = a.ndim
    return pl.BlockSpec(a.shape, lambda *_: (0,) * nd)


def _layer(shape):
    nd = len(shape)
    return pl.BlockSpec((None,) + tuple(shape), lambda *a: (a[-1][0],) + (0,) * nd)


def _layer_index(l):
    return jnp.full((1,), l, jnp.int32)


def _split_bf16(x):
    hi = x.astype(BF16)
    lo = (x - hi.astype(F32)).astype(BF16)
    return hi, lo


def _pieces(x):
    p1 = x.astype(BF16)
    r1 = x - p1.astype(F32)
    p2 = r1.astype(BF16)
    p3 = (r1 - p2.astype(F32)).astype(BF16)
    return p1, p2, p3


def _sigmoid(x):
    return 1.0 / (1.0 + jnp.exp(-x))


def w_in_prep(win, l):
    tr = 256
    n = D_IN // 4
    v_rest = D_QKV - n
    b0 = v_rest + N_HEADS

    def body(s0, s1, s2, s3, wa_ref, wf_ref, wb_ref):
        b = s1[...]
        wa_ref[...] = jnp.concatenate([s0[...], b[:, 0:v_rest]], axis=1)
        wf_ref[...] = jnp.concatenate([b[:, v_rest:b0], jnp.zeros((tr, D_F - N_HEADS), b.dtype)], axis=1)
        wb_ref[...] = jnp.concatenate([b[:, b0:n], s2[...], s3[...]], axis=1)

    shard = lambda j: pl.BlockSpec((None, None, tr, n), lambda i: (l, j, i, 0))
    return pl.pallas_call(
        body,
        grid=(D_MODEL // tr,),
        in_specs=[shard(0), shard(1), shard(2), shard(3)],
        out_specs=[_rows(tr, D_QKV), _rows(tr, D_F), _rows(tr, D_B)],
        out_shape=[jax.ShapeDtypeStruct((D_MODEL, D_QKV), win.dtype), jax.ShapeDtypeStruct((D_MODEL, D_F), win.dtype),
                   jax.ShapeDtypeStruct((D_MODEL, D_B), win.dtype)],
        compiler_params=_params("parallel"),
        name="w_in_prep",
    )(win, win, win, win)


def w_in_unprep(dwa, dwf, dwb):
    tr = 256
    n = D_IN // 4
    v_rest = D_QKV - n
    b1 = n - v_rest - N_HEADS

    def body(a_ref, f_ref, b_ref, o_ref):
        a = a_ref[...]
        b = b_ref[...]
        o_ref[0] = a[:, 0:n]
        o_ref[1] = jnp.concatenate([a[:, n:D_QKV], f_ref[:, 0:N_HEADS], b[:, 0:b1]], axis=1)
        o_ref[2] = b[:, b1:b1 + n]
        o_ref[3] = b[:, b1 + n:D_B]

    return pl.pallas_call(
        body,
        grid=(D_MODEL // tr,),
        in_specs=[_rows(tr, D_QKV), _rows(tr, D_F), _rows(tr, D_B)],
        out_specs=pl.BlockSpec((4, tr, n), lambda i: (0, i, 0)),
        out_shape=jax.ShapeDtypeStruct((4, D_MODEL, n), F32),
        compiler_params=_params("parallel"),
        name="w_in_unprep",
    )(dwa, dwf, dwb)


def branch_w_prep(wao, wco, pw, cw, l):
    gd = D_POOL // N_GROUPS
    od = D_MODEL // N_GROUPS

    def body(wao_ref, wco_ref, pw_ref, cw_ref, ao_ref, co_ref, po_ref, co8_ref):
        a = jnp.concatenate([wao_ref[j] for j in range(4)], axis=1)
        gap = jnp.zeros((LANES - HEAD_DIM, D_MODEL), a.dtype)
        ao_ref[...] = jnp.concatenate(
            [blk for h in range(N_HEADS) for blk in (a[h * HEAD_DIM:(h + 1) * HEAD_DIM], gap)], axis=0)
        co_ref[...] = jnp.concatenate([wco_ref[j] for j in range(4)], axis=1)
        zero = jnp.zeros((gd, od), a.dtype)
        po_ref[...] = jnp.concatenate(
            [jnp.concatenate([jnp.concatenate([pw_ref[j, g] for j in range(4)], axis=1) if g2 == g else zero
                              for g2 in range(N_GROUPS)], axis=1) for g in range(N_GROUPS)], axis=0)
        co8_ref[...] = jnp.zeros_like(co8_ref)
        co8_ref[0:CONV_K, :] = jnp.concatenate([cw_ref[j] for j in range(4)], axis=1)

    sel = lambda *shape: pl.BlockSpec((None,) + shape, lambda i: (l,) + (0,) * len(shape))
    return pl.pallas_call(
        body,
        grid=(1,),
        in_specs=[sel(4, D_ATTN, D_MODEL // 4), sel(4, D_CONV, D_MODEL // 4), sel(4, N_GROUPS, gd, od // 4),
                  sel(4, CONV_K, D_CONV // 4)],
        out_specs=[pl.BlockSpec((D_HEADS, D_MODEL), lambda i: (0, 0)), pl.BlockSpec((D_CONV, D_MODEL), lambda i: (0, 0)),
                   pl.BlockSpec((D_POOL, D_MODEL), lambda i: (0, 0)), pl.BlockSpec((8, D_CONV), lambda i: (0, 0))],
        out_shape=[jax.ShapeDtypeStruct((D_HEADS, D_MODEL), BF16), jax.ShapeDtypeStruct((D_CONV, D_MODEL), BF16),
                   jax.ShapeDtypeStruct((D_POOL, D_MODEL), BF16), jax.ShapeDtypeStruct((8, D_CONV), F32)],
        name="branch_w_prep",
    )(wao, wco, pw, cw)


def branch_g_place(dwao, dwco, dwpool):
    gd = D_POOL // N_GROUPS
    od = D_MODEL // N_GROUPS
    q = D_MODEL // 4

    def body(a_ref, c_ref, p_ref, ao_ref, co_ref, po_ref):
        a = jnp.concatenate([a_ref[h * LANES:h * LANES + HEAD_DIM, :] for h in range(N_HEADS)], axis=0)
        c = c_ref[...]
        p = p_ref[...]
        for j in range(4):
            ao_ref[j] = a[:, j * q:(j + 1) * q]
            co_ref[j] = c[:, j * q:(j + 1) * q]
            for g in range(N_GROUPS):
                c0 = g * od + j * (od // 4)
                po_ref[j, g] = p[g * gd:(g + 1) * gd, c0:c0 + od // 4]

    vm = pl.BlockSpec(memory_space=pltpu.VMEM)
    return pl.pallas_call(
        body,
        in_specs=[vm, vm, vm],
        out_specs=[vm, vm, vm],
        out_shape=[jax.ShapeDtypeStruct((4, D_ATTN, q), F32), jax.ShapeDtypeStruct((4, D_CONV, q), F32),
                   jax.ShapeDtypeStruct((4, N_GROUPS, gd, od // 4), F32)],
        name="branch_g_place",
    )(dwao, dwco, dwpool)


def in_proj_fwd(x, g, wa, wf, wb):
    S = x.shape[0]

    def body(x_ref, g_ref, wa_ref, wf_ref, wb_ref, h_ref, pa_ref, pf_ref, pb_ref):
        xf = x_ref[...]
        r = lax.rsqrt(jnp.mean(xf * xf, axis=-1, keepdims=True) + EPS)
        h = (xf * r * g_ref[...]).astype(BF16)
        h_ref[...] = h
        pa_ref[...] = _dot(h, wa_ref[...]).astype(BF16)
        pf_ref[...] = _dot(h, wf_ref[...])
        pb_ref[...] = _dot(h, wb_ref[...]).astype(BF16)

    return pl.pallas_call(
        body,
        grid=(S // TM,),
        in_specs=[_rows(TM, D_MODEL), _whole(g), _whole(wa), _whole(wf), _whole(wb)],
        out_specs=[_rows(TM, D_MODEL), _rows(TM, D_QKV), _rows(TM, D_F), _rows(TM, D_B)],
        out_shape=[
            jax.ShapeDtypeStruct((S, D_MODEL), BF16),
            jax.ShapeDtypeStruct((S, D_QKV), BF16),
            jax.ShapeDtypeStruct((S, D_F), F32),
            jax.ShapeDtypeStruct((S, D_B), BF16),
        ],
        compiler_params=_params("parallel"),
        name="in_proj_fwd",
    )(x, g, wa, wf, wb)


def in_proj_bwd(x, g, dxm, dpa, dpf, dpb, wa, wf, wb):
    S = x.shape[0]

    def body(x_ref, g_ref, dxm_ref, dpa_ref, dpf_ref, dpb_ref, wa_ref, wf_ref, wb_ref, dx_ref, dg_ref):
        @pl.when(pl.program_id(0) == 0)
        def _():
            dg_ref[...] = jnp.zeros_like(dg_ref)

        dh = _dot_nt(dpa_ref[...], wa_ref[...]) + _dot_nt(dpf_ref[...], wf_ref[...]) + _dot_nt(dpb_ref[...], wb_ref[...])
        xf = x_ref[...]
        r = lax.rsqrt(jnp.mean(xf * xf, axis=-1, keepdims=True) + EPS)
        xhat = xf * r
        dg_ref[...] += jnp.sum(dh * xhat, axis=0, keepdims=True)
        gdh = dh * g_ref[...]
        dx_ref[...] = dxm_ref[...] + r * (gdh - xhat * jnp.mean(xhat * gdh, axis=-1, keepdims=True))

    return pl.pallas_call(
        body,
        grid=(S // TM,),
        in_specs=[_rows(TM, D_MODEL), _whole(g), _rows(TM, D_MODEL), _rows(TM, D_QKV), _rows(TM, D_F), _rows(TM, D_B),
                  _whole(wa), _whole(wf), _whole(wb)],
        out_specs=[_rows(TM, D_MODEL), pl.BlockSpec((1, D_MODEL), lambda i: (0, 0))],
        out_shape=[jax.ShapeDtypeStruct((S, D_MODEL), F32), jax.ShapeDtypeStruct((1, D_MODEL), F32)],
        compiler_params=_params("arbitrary"),
        name="in_proj_bwd",
    )(x, g, dxm, dpa, dpf, dpb, wa, wf, wb)


def wgrad(xa, dy, name, n_split=1, ts=512):
    S, K = xa.shape
    N = dy.shape[1]
    tn = N // n_split if n_split > 1 else min(512, N)
    n_steps = S // ts

    def body(x_ref, dy_ref, o_ref):
        @pl.when(pl.program_id(1) == 0)
        def _():
            o_ref[...] = jnp.zeros_like(o_ref)

        o_ref[...] += _dot_tn(x_ref[...], dy_ref[...])

    if n_split > 1:
        out_spec = pl.BlockSpec((None, K, tn), lambda j, k: (j, 0, 0))
        out_shape = jax.ShapeDtypeStruct((n_split, K, tn), F32)
    else:
        out_spec = pl.BlockSpec((K, tn), lambda j, k: (0, j))
        out_shape = jax.ShapeDtypeStruct((K, N), F32)
    return pl.pallas_call(
        body,
        grid=(N // tn, n_steps),
        in_specs=[pl.BlockSpec((ts, K), lambda j, k: (k, 0)), pl.BlockSpec((ts, tn), lambda j, k: (k, j))],
        out_specs=out_spec,
        out_shape=out_shape,
        compiler_params=_params("parallel", "arbitrary"),
        name=name,
    )(xa, dy)


def _head_mean_matrix():
    h = np.arange(D_ATTN) // HEAD_DIM
    return jnp.asarray((h[:, None] == h[None, :]).astype(np.float32) / HEAD_DIM, BF16)


def _place_matrix(lane0):
    m = np.zeros((N_PIECES * LANES, D_HEADS), np.float32)
    for i in range(N_PIECES):
        for h in range(N_HEADS):
            m[i * LANES + h, h * LANES + lane0 + i] = 1.0
    return jnp.asarray(m, BF16)


def _tri(n, upper):
    r = np.arange(n)
    m = (r[None, :] >= r[:, None]) if upper else (r[None, :] <= r[:, None])
    return jnp.asarray(m.astype(np.float32), BF16)


def _lanes_in(lane, lo, n):
    return (lane >= lo) & (lane < lo + n)


def qk_prep(pa, pf, gq, gk, fb):
    S = pa.shape[0]
    bd = _head_mean_matrix()
    tri = _tri(TM, upper=False)
    place_q = _place_matrix(LANE_C)
    place_k = _place_matrix(LANE_ONE)

    def body(q_ref, k_ref, v_ref, pf_ref, gq_ref, gk_ref, fb_ref, bd_ref, tri_ref, pq_ref, pk_ref,
             qx_ref, kx_ref, vx_ref, carry):
        @pl.when(pl.program_id(0) == 0)
        def _():
            carry[...] = jnp.zeros_like(carry)

        def head_norm(x_ref, g_ref, scale):
            xf = x_ref[...].astype(F32)
            ms = _dot((xf * xf).astype(BF16), bd_ref[...])
            return xf * lax.rsqrt(ms + EPS) * g_ref[...] * scale

        qh = head_norm(q_ref, gq_ref, HEAD_DIM ** -0.5 * LOG2E)
        kh = head_norm(k_ref, gk_ref, 1.0)
        vf = v_ref[...].astype(F32)

        z = pf_ref[...] + fb_ref[...]
        logf = jnp.minimum(z, 0.0) - jnp.log(1.0 + jnp.exp(-jnp.abs(z)))
        hi, lo = _split_bf16(logf)
        c = _dot(tri_ref[...], hi) + _dot(tri_ref[...], lo) + carry[...]
        carry[...] += jnp.sum(hi.astype(F32) + lo.astype(F32), axis=0, keepdims=True)
        pieces = jnp.concatenate(_pieces(c * LOG2E), axis=1)
        cq = _dot(pieces, pq_ref[...])
        ck = _dot(pieces, pk_ref[...])

        lane = lax.broadcasted_iota(jnp.int32, (TM, LANES), 1)
        low = lane < HEAD_DIM
        ones_q = _lanes_in(lane, LANE_ONE, N_PIECES).astype(F32)
        ones_k = (_lanes_in(lane, LANE_C, N_PIECES) | _lanes_in(lane, LANE_LSE, N_PIECES)).astype(F32)
        ones_v = _lanes_in(lane, LANE_C, N_PIECES + 1).astype(F32)
        for h in range(N_HEADS):
            blk = slice((h // 2) * LANES, (h // 2 + 1) * LANES)
            head = (lambda a: pltpu.roll(a[:, blk], HEAD_DIM, 1)) if h % 2 else (lambda a: a[:, blk])
            mine = slice(h * LANES, (h + 1) * LANES)
            qx_ref[h] = jnp.where(low, head(qh), cq[:, mine] + ones_q).astype(BF16)
            kx_ref[h] = jnp.where(low, head(kh), ones_k - ck[:, mine]).astype(BF16)
            vx_ref[h] = jnp.where(low, head(vf), ones_v).astype(BF16)

    heads = pl.BlockSpec((N_HEADS, TM, LANES), lambda i: (0, i, 0))
    out = jax.ShapeDtypeStruct((N_HEADS, S, LANES), BF16)
    return pl.pallas_call(
        body,
        grid=(S // TM,),
        in_specs=[pl.BlockSpec((TM, D_ATTN), lambda i: (i, 0)), pl.BlockSpec((TM, D_ATTN), lambda i: (i, 1)),
                  pl.BlockSpec((TM, D_ATTN), lambda i: (i, 2)),
                  _rows(TM, D_F), _whole(gq), _whole(gk), _whole(fb), _whole(bd), _whole(tri), _whole(place_q), _whole(place_k)],
        out_specs=[heads, heads, heads],
        out_shape=[out, out, out],
        scratch_shapes=[pltpu.VMEM((1, D_F), F32)],
        compiler_params=_params("arbitrary"),
        name="qk_prep",
    )(pa, pa, pa, pf, gq, gk, fb, bd, tri, place_q, place_k)


def attn_fwd(qx, kx, vx):
    S = qx.shape[1]
    nq = S // TQ

    def body(q_ref, k_ref, v_ref, o_ref, qb_ref):
        i = pl.program_id(1)
        lane = lax.broadcasted_iota(jnp.int32, (TQ, LANES), 1)
        row = lax.broadcasted_iota(jnp.int32, (TQ, TK), 0)
        col = lax.broadcasted_iota(jnp.int32, (TQ, TK), 1)
        q = [q_ref[0], q_ref[1]]
        n_full = (i * TQ) // TK

        def step(kt, carry, masked):
            ks = pl.multiple_of(kt * TK, TK)
            new = []
            for j in range(2):
                m, acc = carry[j]
                z = _dot_nt(q[j], k_ref[j, pl.ds(ks, TK), :])
                if masked:
                    z = jnp.where(row + i * TQ >= col + ks, z, NEG)
                m_new = jnp.maximum(m, jnp.max(z, axis=1, keepdims=True))
                pr = jnp.exp2(z - m_new)
                acc = jnp.exp2(m - m_new) * acc + _dot(pr.astype(BF16), v_ref[j, pl.ds(ks, TK), :])
                new.append((m_new, acc))
            return tuple(new)

        init = tuple((jnp.full((TQ, 1), NEG, F32), jnp.zeros((TQ, LANES), F32)) for _ in range(2))
        carry = lax.fori_loop(0, n_full, functools.partial(step, masked=False), init)
        carry = step(n_full, carry, True)
        for j in range(2):
            m, acc = carry[j]
            l = jnp.sum(jnp.where(lane == LANE_C, acc, 0.0), axis=1, keepdims=True)
            o_ref[j] = acc / l
            n1, n2, n3 = _pieces(-(m + jnp.log(l) * LOG2E))
            qb_ref[j] = jnp.where(lane == LANE_LSE, n1,
                                  jnp.where(lane == LANE_LSE + 1, n2, jnp.where(lane == LANE_LSE + 2, n3, q[j])))

    pair_tile = pl.BlockSpec((2, TQ, LANES), lambda p, i: (p, i, 0))
    pair_all = pl.BlockSpec((2, S, LANES), lambda p, i: (p, 0, 0))
    return pl.pallas_call(
        body,
        grid=(N_HEADS // 2, nq),
        in_specs=[pair_tile, pair_all, pair_all],
        out_specs=[pair_tile, pair_tile],
        out_shape=[jax.ShapeDtypeStruct((N_HEADS, S, LANES), F32), jax.ShapeDtypeStruct((N_HEADS, S, LANES), BF16)],
        compiler_params=_params("parallel", "parallel"),
        name="attn_fwd",
    )(qx, kx, vx)


def attn_bwd(qxb, kx, vx, dox):
    S = qxb.shape[1]
    nq = S // QT

    def body(q_ref, k_ref, v_ref, do_ref, dq_ref, dk_ref, dv_ref):
        kt = pl.program_id(1)

        @pl.when(kt == 0)
        def _():
            dq_ref[...] = jnp.zeros_like(dq_ref)

        row = lax.broadcasted_iota(jnp.int32, (QT, KB), 0)
        col = lax.broadcasted_iota(jnp.int32, (QT, KB), 1)
        k = [k_ref[0], k_ref[1]]
        v = [v_ref[0], v_ref[1]]
        q_first = (kt * KB) // QT

        def step(qi, carry, masked):
            qs = pl.multiple_of(qi * QT, QT)
            new = []
            for j in range(2):
                dk, dv = carry[j]
                q = q_ref[j, pl.ds(qs, QT), :]
                dout = do_ref[j, pl.ds(qs, QT), :]
                z = _dot_nt(q, k[j])
                if masked:
                    z = jnp.where(row + qs >= col + kt * KB, z, NEG)
                pr = jnp.exp2(z)
                dv = dv + _dot_tn(pr.astype(BF16), dout)
                dsb = (pr * _dot_nt(dout, v[j])).astype(BF16)
                dk = dk + _dot_tn(dsb, q)
                dq_ref[j, pl.ds(qs, QT), :] += _dot(dsb, k[j])
                new.append((dk, dv))
            return tuple(new)

        zero = jnp.zeros((KB, LANES), F32)
        carry = step(q_first, ((zero, zero), (zero, zero)), True)
        carry = lax.fori_loop(q_first + 1, nq, functools.partial(step, masked=False), carry)
        for j in range(2):
            dk_ref[j] = carry[j][0]
            dv_ref[j] = carry[j][1]

    pair_tile = pl.BlockSpec((2, KB, LANES), lambda p, kt: (p, kt, 0))
    pair_all = pl.BlockSpec((2, S, LANES), lambda p, kt: (p, 0, 0))
    out = jax.ShapeDtypeStruct((N_HEADS, S, LANES), F32)
    return pl.pallas_call(
        body,
        grid=(N_HEADS // 2, S // KB),
        in_specs=[pair_all, pair_tile, pair_tile, pair_all],
        out_specs=[pair_all, pair_tile, pair_tile],
        out_shape=[out, out, out],
        compiler_params=_params("arbitrary", "arbitrary"),
        name="attn_bwd",
    )(qxb, kx, vx, dox)


def attn_bwd_post(pa, pf, dqx, dkx, dvx, gq, gk, fb):
    S = pa.shape[0]
    nt = S // TM
    bd = _head_mean_matrix()
    triu = _tri(TM, upper=True)
    rev = lambda i: nt - 1 - i

    def body(q_ref, k_ref, pf_ref, dqx_ref, dkx_ref, dvx_ref, gq_ref, gk_ref, fb_ref, bd_ref, triu_ref,
             dpa_ref, dpf_ref, dgq_ref, dgk_ref, dfb_ref, carry):
        @pl.when(pl.program_id(0) == 0)
        def _():
            carry[...] = jnp.zeros_like(carry)
            dgq_ref[...] = jnp.zeros_like(dgq_ref)
            dgk_ref[...] = jnp.zeros_like(dgk_ref)
            dfb_ref[...] = jnp.zeros_like(dfb_ref)

        lane = lax.broadcasted_iota(jnp.int32, (TM, LANES), 1)

        def columns(ref):
            return jnp.concatenate([jnp.where(lane < HEAD_DIM, ref[2 * p], pltpu.roll(ref[2 * p + 1], HEAD_DIM, 1))
                                    for p in range(N_HEADS // 2)], axis=1)

        def head_norm_bwd(x_ref, dy, g_ref, dg_ref):
            xf = x_ref[...].astype(F32)
            r = lax.rsqrt(_dot((xf * xf).astype(BF16), bd_ref[...]) + EPS)
            xhat = xf * r
            dg_ref[...] += jnp.sum(dy * xhat, axis=0, keepdims=True)
            gdy = dy * g_ref[...]
            return (r * (gdy - xhat * _dot((xhat * gdy).astype(BF16), bd_ref[...]))).astype(BF16)

        dpa_ref[:, 0:D_ATTN] = head_norm_bwd(q_ref, columns(dqx_ref) * HEAD_DIM ** -0.5, gq_ref, dgq_ref)
        dpa_ref[:, D_ATTN:2 * D_ATTN] = head_norm_bwd(k_ref, columns(dkx_ref) * LN2, gk_ref, dgk_ref)
        dpa_ref[:, 2 * D_ATTN:3 * D_ATTN] = columns(dvx_ref).astype(BF16)

        dc = jnp.zeros((TM, LANES), F32)
        for h in range(N_HEADS):
            both = jnp.where(lane == LANE_C, dqx_ref[h], 0.0) - jnp.where(lane == LANE_ONE, dkx_ref[h], 0.0)
            dc = jnp.where(lane == h, jnp.sum(both, axis=1, keepdims=True), dc)
        hi, lo = _split_bf16(dc)
        dlogf = _dot(triu_ref[...], hi) + _dot(triu_ref[...], lo) + carry[...]
        first = lax.broadcasted_iota(jnp.int32, (TM, D_F), 0) == 0
        carry[...] = jnp.sum(jnp.where(first, dlogf, 0.0), axis=0, keepdims=True)
        df = dlogf * _sigmoid(-(pf_ref[...] + fb_ref[...]))
        dfb_ref[...] += jnp.sum(df, axis=0, keepdims=True)
        dpf_ref[...] = df.astype(BF16)

    heads = pl.BlockSpec((N_HEADS, TM, LANES), lambda i: (0, rev(i), 0))
    return pl.pallas_call(
        body,
        grid=(nt,),
        in_specs=[pl.BlockSpec((TM, D_ATTN), lambda i: (rev(i), 0)), pl.BlockSpec((TM, D_ATTN), lambda i: (rev(i), 1)),
                  pl.BlockSpec((TM, D_F), lambda i: (rev(i), 0)), heads, heads, heads,
                  _whole(gq), _whole(gk), _whole(fb), _whole(bd), _whole(triu)],
        out_specs=[pl.BlockSpec((TM, D_QKV), lambda i: (rev(i), 0)), pl.BlockSpec((TM, D_F), lambda i: (rev(i), 0)),
                   pl.BlockSpec((1, D_ATTN), lambda i: (0, 0)), pl.BlockSpec((1, D_ATTN), lambda i: (0, 0)),
                   pl.BlockSpec((1, D_F), lambda i: (0, 0))],
        out_shape=[jax.ShapeDtypeStruct((S, D_QKV), BF16), jax.ShapeDtypeStruct((S, D_F), BF16),
                   jax.ShapeDtypeStruct((1, D_ATTN), F32), jax.ShapeDtypeStruct((1, D_ATTN), F32),
                   jax.ShapeDtypeStruct((1, D_F), F32)],
        scratch_shapes=[pltpu.VMEM((1, D_F), F32)],
        compiler_params=_params("arbitrary"),
        name="attn_bwd_post",
    )(pa, pa, pf, dqx, dkx, dvx, gq, gk, fb, bd, triu)


def _shift_down(ext, k):
    return pltpu.roll(ext, k, 0)[HALO:]


def _shift_up(ext, k, n):
    return pltpu.roll(ext, n + HALO - k, 0)[:n]


def _pool_lane_select(a2, a4, a8, a16, lane):
    return jnp.where(lane < 64, a2, jnp.where(lane < 128, a4, jnp.where(lane < 192, a8, a16)))


def _local_branches(o, pb, halo, have_prev, row0, wao, convw, wco, wpool, pscale):
    n = pb.shape[0]
    cx = pb[:, 0:D_CONV].astype(F32)
    cb = pb[:, D_CONV:2 * D_CONV].astype(F32)
    cc = pb[:, 2 * D_CONV:3 * D_CONV].astype(F32)
    px = pb[:, 3 * D_CONV:D_LOCAL].astype(F32)
    keep = have_prev.astype(F32)
    z = cc * cx
    z_ext = jnp.concatenate([halo[:, 2 * D_CONV:3 * D_CONV].astype(F32) * halo[:, 0:D_CONV].astype(F32) * keep, z], axis=0)
    z1 = _shift_down(z_ext, 1)
    z2 = _shift_down(z_ext, 2)
    conv = convw[0:1, :] * z2 + convw[1:2, :] * z1 + convw[2:3, :] * z
    cm = cb * conv

    u_ext = jnp.concatenate([halo[:, 3 * D_CONV:D_LOCAL].astype(F32) * keep, px], axis=0)
    s2 = u_ext + pltpu.roll(u_ext, 1, 0)
    s4 = s2 + pltpu.roll(s2, 2, 0)
    s8 = s4 + pltpu.roll(s4, 4, 0)
    s16 = s8 + pltpu.roll(s8, 8, 0)
    lane = lax.broadcasted_iota(jnp.int32, (n, D_POOL), 1)
    win = _pool_lane_select(2.0, 4.0, 8.0, 16.0, lane)
    t = (row0 + lax.broadcasted_iota(jnp.int32, (n, D_POOL), 0)).astype(F32)
    cnt = jnp.minimum(t + 1.0, win)
    feat = _pool_lane_select(s2[HALO:], s4[HALO:], s8[HALO:], s16[HALO:], lane) / cnt - px

    ya = _dot(o, wao)
    yc = _dot(cm.astype(BF16), wco)
    yp_pre = _dot(feat.astype(BF16), wpool)
    yp = yp_pre * pscale
    return dict(cx=cx, cb=cb, cc=cc, z=z, z1=z1, z2=z2, conv=conv, cm=cm, feat=feat, cnt=cnt, lane=lane,
                ya=ya, yc=yc, yp_pre=yp_pre, yp=yp)


def _halo_spec(tm, tile_of):
    per = tm // HALO
    return pl.BlockSpec((HALO, D_LOCAL), lambda i, *_: (jnp.maximum(tile_of(i) * per - 1, 0), 0))


def _heads_as_columns(ref):
    return jnp.concatenate([ref[h] for h in range(N_HEADS)], axis=1)


def mix_out_fwd(x, ox, pb, wao, convw, wco, wpool, pscale, wo_all, l):
    S = x.shape[0]
    tm = TM_MIX

    def body(l_ref, x_ref, o_ref, pb_ref, halo_ref, wao_ref, cw_ref, wco_ref, wp_ref, ps_ref, wo_ref, y_ref):
        i = pl.program_id(0)
        pb = pb_ref[...]
        o = _heads_as_columns(o_ref).astype(BF16)
        b = _local_branches(o, pb, halo_ref[...], i > 0, i * tm, wao_ref[...], cw_ref[...], wco_ref[...],
                            wp_ref[...], ps_ref[...])
        g0 = _sigmoid(pb[:, D_LOCAL:D_LOCAL + D_MODEL].astype(F32))
        g1 = _sigmoid(pb[:, D_LOCAL + D_MODEL:D_LOCAL + 2 * D_MODEL].astype(F32))
        g2 = _sigmoid(pb[:, D_LOCAL + 2 * D_MODEL:D_B].astype(F32))
        merged = g0 * b["ya"] + g1 * b["yc"] + g2 * b["yp"]
        y_ref[...] = x_ref[...] + _dot(merged.astype(BF16), wo_ref[...])

    return pl.pallas_call(
        body,
        grid_spec=pltpu.PrefetchScalarGridSpec(
            num_scalar_prefetch=1,
            grid=(S // tm,),
            in_specs=[_rows(tm, D_MODEL), pl.BlockSpec((N_HEADS, tm, LANES), lambda i, l: (0, i, 0)), _rows(tm, D_B),
                      _halo_spec(tm, lambda i: i), _whole(wao), _whole(convw), _whole(wco), _whole(wpool), _whole(pscale),
                      _layer((D_MODEL, D_MODEL))],
            out_specs=_rows(tm, D_MODEL),
        ),
        out_shape=jax.ShapeDtypeStruct((S, D_MODEL), F32),
        compiler_params=_params("parallel"),
        name="mix_out_fwd",
    )(_layer_index(l), x, ox, pb, pb, wao, convw, wco, wpool, pscale, wo_all)


def mix_out_bwd(dxm, ox, pb, wao, convw, wco, wpool, pscale, wo_all, l):
    S = dxm.shape[0]
    tm = TM_MIX
    nt = S // tm
    rev = lambda i: nt - 1 - i
    rows = lambda n: pl.BlockSpec((tm, n), lambda i, l: (rev(i), 0))
    heads = pl.BlockSpec((N_HEADS, tm, LANES), lambda i, l: (0, rev(i), 0))
    acc = lambda r, c: pl.BlockSpec((r, c), lambda i, l: (0, 0))

    def body(l_ref, dxm_ref, o_ref, pb_ref, halo_ref, wao_ref, cw_ref, wco_ref, wp_ref, ps_ref, wo_ref,
             dpb_ref, do_ref, dwo_ref, dwao_ref, dwco_ref, dwp_ref, dcw_ref, dps_ref, next_dconv, next_e):
        i = pl.program_id(0)
        r = rev(i)

        @pl.when(i == 0)
        def _():
            for ref in (dwo_ref, dwao_ref, dwco_ref, dwp_ref, dcw_ref, dps_ref, next_dconv, next_e):
                ref[...] = jnp.zeros_like(ref)

        pb = pb_ref[...]
        o = _heads_as_columns(o_ref).astype(BF16)
        cw = cw_ref[...]
        b = _local_branches(o, pb, halo_ref[...], r > 0, r * tm, wao_ref[...], cw, wco_ref[...], wp_ref[...], ps_ref[...])
        g0 = _sigmoid(pb[:, D_LOCAL:D_LOCAL + D_MODEL].astype(F32))
        g1 = _sigmoid(pb[:, D_LOCAL + D_MODEL:D_LOCAL + 2 * D_MODEL].astype(F32))
        g2 = _sigmoid(pb[:, D_LOCAL + 2 * D_MODEL:D_B].astype(F32))
        dxb = dxm_ref[...].astype(BF16)
        merged = g0 * b["ya"] + g1 * b["yc"] + g2 * b["yp"]
        dwo_ref[...] += _dot_tn(merged.astype(BF16), dxb)
        dmer = _dot_nt(dxb, wo_ref[...])
        dpb_ref[:, D_LOCAL:D_LOCAL + D_MODEL] = (dmer * b["ya"] * (g0 * (1.0 - g0))).astype(BF16)
        dpb_ref[:, D_LOCAL + D_MODEL:D_LOCAL + 2 * D_MODEL] = (dmer * b["yc"] * (g1 * (1.0 - g1))).astype(BF16)
        dpb_ref[:, D_LOCAL + 2 * D_MODEL:D_B] = (dmer * b["yp"] * (g2 * (1.0 - g2))).astype(BF16)

        dya = (dmer * g0).astype(BF16)
        dwao_ref[...] += _dot_tn(o, dya)
        da = _dot_nt(dya, wao_ref[...]).astype(BF16)
        lane = lax.broadcasted_iota(jnp.int32, (tm, LANES), 1)
        for h in range(N_HEADS):
            dah = da[:, h * LANES:(h + 1) * LANES]
            d1, d2, d3 = _pieces(-jnp.sum(dah.astype(F32) * o_ref[h], axis=1, keepdims=True))
            do_ref[h] = jnp.where(lane == LANE_C + 1, d1, jnp.where(lane == LANE_C + 2, d2,
                                                                  jnp.where(lane == LANE_C + 3, d3, dah)))

        dyc = (dmer * g1).astype(BF16)
        dwco_ref[...] += _dot_tn(b["cm"].astype(BF16), dyc)
        dcm = _dot_nt(dyc, wco_ref[...])
        dconv = dcm * b["cb"]
        dcw_ref[0:1, :] += jnp.sum(dconv * b["z2"], axis=0, keepdims=True)
        dcw_ref[1:2, :] += jnp.sum(dconv * b["z1"], axis=0, keepdims=True)
        dcw_ref[2:3, :] += jnp.sum(dconv * b["z"], axis=0, keepdims=True)
        d_ext = jnp.concatenate([dconv, next_dconv[...]], axis=0)
        dz = cw[2:3, :] * dconv + cw[1:2, :] * _shift_up(d_ext, 1, tm) + cw[0:1, :] * _shift_up(d_ext, 2, tm)
        next_dconv[...] = dconv[0:HALO]
        dpb_ref[:, 0:D_CONV] = (dz * b["cc"]).astype(BF16)
        dpb_ref[:, D_CONV:2 * D_CONV] = (dcm * b["conv"]).astype(BF16)
        dpb_ref[:, 2 * D_CONV:3 * D_CONV] = (dz * b["cx"]).astype(BF16)

        dyp = dmer * g2
        dps_ref[...] += jnp.sum(dyp * b["yp_pre"], axis=0, keepdims=True)
        dyps = (dyp * ps_ref[...]).astype(BF16)
        dwp_ref[...] += _dot_tn(b["feat"].astype(BF16), dyps)
        dfeat = _dot_nt(dyps, wp_ref[...])
        e = dfeat / b["cnt"]
        e_ext = jnp.concatenate([e, next_e[...]], axis=0)
        up = lambda a, k: pltpu.roll(a, tm + HALO - k, 0)
        f2 = e_ext + up(e_ext, 1)
        f4 = f2 + up(f2, 2)
        f8 = f4 + up(f4, 4)
        f16 = f8 + up(f8, 8)
        next_e[...] = e[0:HALO]
        dpb_ref[:, 3 * D_CONV:D_LOCAL] = (_pool_lane_select(f2[:tm], f4[:tm], f8[:tm], f16[:tm], b["lane"]) - dfeat).astype(BF16)

    return pl.pallas_call(
        body,
        grid_spec=pltpu.PrefetchScalarGridSpec(
            num_scalar_prefetch=1,
            grid=(nt,),
            in_specs=[rows(D_MODEL), heads, rows(D_B), _halo_spec(tm, rev),
                      _whole(wao), _whole(convw), _whole(wco), _whole(wpool), _whole(pscale), _layer((D_MODEL, D_MODEL))],
            out_specs=[rows(D_B), heads, acc(D_MODEL, D_MODEL), acc(D_HEADS, D_MODEL), acc(D_CONV, D_MODEL),
                       acc(D_POOL, D_MODEL), acc(8, D_CONV), acc(1, D_MODEL)],
            scratch_shapes=[pltpu.VMEM((HALO, D_CONV), F32), pltpu.VMEM((HALO, D_POOL), F32)],
        ),
        out_shape=[jax.ShapeDtypeStruct((S, D_B), BF16), jax.ShapeDtypeStruct((N_HEADS, S, LANES), BF16),
                   jax.ShapeDtypeStruct((D_MODEL, D_MODEL), F32), jax.ShapeDtypeStruct((D_HEADS, D_MODEL), F32),
                   jax.ShapeDtypeStruct((D_CONV, D_MODEL), F32), jax.ShapeDtypeStruct((D_POOL, D_MODEL), F32),
                   jax.ShapeDtypeStruct((8, D_CONV), F32), jax.ShapeDtypeStruct((1, D_MODEL), F32)],
        compiler_params=_params("arbitrary"),
        name="mix_out_bwd",
    )(_layer_index(l), dxm, ox, pb, pb, wao, convw, wco, wpool, pscale, wo_all)


FF_SHARD = 2 * D_FF // 4


def ffn_fwd(x, g, w1_all, w2_all, l):
    S = x.shape[0]

    def body(l_ref, x_ref, g_ref, w1_ref, w2_ref, y_ref, u_ref):
        xf = x_ref[...]
        r = lax.rsqrt(jnp.mean(xf * xf, axis=-1, keepdims=True) + EPS)
        h = (xf * r * g_ref[...]).astype(BF16)
        u = jnp.concatenate([_dot(h, w1_ref[j]) for j in range(4)], axis=1)
        u_ref[...] = u.astype(BF16)
        gt = u[:, 0:D_FF]
        act = gt * _sigmoid(gt) * u[:, D_FF:2 * D_FF]
        y_ref[...] = xf + _dot(act.astype(BF16), w2_ref[...])

    return pl.pallas_call(
        body,
        grid_spec=pltpu.PrefetchScalarGridSpec(
            num_scalar_prefetch=1,
            grid=(S // TM,),
            in_specs=[_rows(TM, D_MODEL), _whole(g), _layer((4, D_MODEL, FF_SHARD)), _layer((D_FF, D_MODEL))],
            out_specs=[_rows(TM, D_MODEL), _rows(TM, 2 * D_FF)],
        ),
        out_shape=[jax.ShapeDtypeStruct((S, D_MODEL), F32), jax.ShapeDtypeStruct((S, 2 * D_FF), BF16)],
        compiler_params=_params("parallel"),
        name="ffn_fwd",
    )(_layer_index(l), x, g, w1_all, w2_all)


def ffn_bwd(x, g, dy, u, w1_all, w2_all, l):
    S = x.shape[0]

    def body(l_ref, x_ref, g_ref, dy_ref, u_ref, w1_ref, w2_ref, dx_ref, du_ref, act_ref, h_ref, dg_ref):
        @pl.when(pl.program_id(0) == 0)
        def _():
            dg_ref[...] = jnp.zeros_like(dg_ref)

        dyf = dy_ref[...]
        dact = _dot_nt(dyf.astype(BF16), w2_ref[...])
        gt = u_ref[:, 0:D_FF].astype(F32)
        up = u_ref[:, D_FF:2 * D_FF].astype(F32)
        sg = _sigmoid(gt)
        silu = gt * sg
        act_ref[...] = (silu * up).astype(BF16)
        du_ref[:, 0:D_FF] = (dact * up * (sg * (1.0 + gt * (1.0 - sg)))).astype(BF16)
        du_ref[:, D_FF:2 * D_FF] = (dact * silu).astype(BF16)
        dh = _dot_nt(du_ref[:, 0:FF_SHARD], w1_ref[0])
        for j in range(1, 4):
            dh = dh + _dot_nt(du_ref[:, j * FF_SHARD:(j + 1) * FF_SHARD], w1_ref[j])
        xf = x_ref[...]
        r = lax.rsqrt(jnp.mean(xf * xf, axis=-1, keepdims=True) + EPS)
        xhat = xf * r
        h_ref[...] = (xhat * g_ref[...]).astype(BF16)
        dg_ref[...] += jnp.sum(dh * xhat, axis=0, keepdims=True)
        gdh = dh * g_ref[...]
        dx_ref[...] = dyf + r * (gdh - xhat * jnp.mean(xhat * gdh, axis=-1, keepdims=True))

    return pl.pallas_call(
        body,
        grid_spec=pltpu.PrefetchScalarGridSpec(
            num_scalar_prefetch=1,
            grid=(S // TM,),
            in_specs=[_rows(TM, D_MODEL), _whole(g), _rows(TM, D_MODEL), _rows(TM, 2 * D_FF),
                      _layer((4, D_MODEL, FF_SHARD)), _layer((D_FF, D_MODEL))],
            out_specs=[_rows(TM, D_MODEL), _rows(TM, 2 * D_FF), _rows(TM, D_FF), _rows(TM, D_MODEL),
                       pl.BlockSpec((1, D_MODEL), lambda i, l: (0, 0))],
        ),
        out_shape=[jax.ShapeDtypeStruct((S, D_MODEL), F32), jax.ShapeDtypeStruct((S, 2 * D_FF), BF16),
                   jax.ShapeDtypeStruct((S, D_FF), BF16), jax.ShapeDtypeStruct((S, D_MODEL), BF16),
                   jax.ShapeDtypeStruct((1, D_MODEL), F32)],
        compiler_params=_params("arbitrary"),
        name="ffn_bwd",
    )(_layer_index(l), x, g, dy, u, w1_all, w2_all)


def loss_head(y, target):
    S = y.shape[0]

    def body(y_ref, t_ref, loss_ref, dy_ref):
        @pl.when(pl.program_id(0) == 0)
        def _():
            loss_ref[0, 0] = 0.0

        err = y_ref[...] - t_ref[...]
        dy_ref[...] = err * (1.0 / D_MODEL)
        loss_ref[0, 0] += 0.5 * jnp.sum(jnp.mean(err * err, axis=-1))

    return pl.pallas_call(
        body,
        grid=(S // TM,),
        in_specs=[_rows(TM, D_MODEL), _rows(TM, D_MODEL)],
        out_specs=[pl.BlockSpec((1, 1), lambda i: (0, 0), memory_space=pltpu.SMEM), _rows(TM, D_MODEL)],
        out_shape=[jax.ShapeDtypeStruct((1, 1), F32), jax.ShapeDtypeStruct((S, D_MODEL), F32)],
        compiler_params=_params("arbitrary"),
        name="loss_head",
    )(y, target)


SPLIT = {
    "w_in": ((D_MODEL, D_IN), 1),
    "w_attn_out": ((D_ATTN, D_MODEL), 1),
    "w_conv_out": ((D_CONV, D_MODEL), 1),
    "pool_w": ((N_GROUPS, D_POOL // N_GROUPS, D_MODEL // N_GROUPS), 2),
    "w_o": ((D_MODEL, D_MODEL), 0),
    "w_ffn_in": ((D_MODEL, 2 * D_FF), 1),
    "w_ffn_out": ((D_FF, D_MODEL), 0),
}
SMALL = {"norm_mix_g": D_MODEL, "forget_b": N_HEADS, "q_norm_g": HEAD_DIM, "k_norm_g": HEAD_DIM,
         "pool_scale": D_MODEL, "norm_ffn_g": D_MODEL, "conv_w": CONV_K * D_CONV}
WEIGHTS = ["norm_mix_g", "w_in", "forget_b", "q_norm_g", "k_norm_g", "w_attn_out", "conv_w", "w_conv_out", "pool_w",
           "pool_scale", "w_o", "norm_ffn_g", "w_ffn_in", "w_ffn_out"]


def local_step(x, target, gathered, small):
    n_layers = small["norm_mix_g"].shape[0]
    w_o = gathered["w_o"].reshape(n_layers, D_MODEL, D_MODEL)
    w2 = gathered["w_ffn_out"].reshape(n_layers, D_FF, D_MODEL)
    w1 = gathered["w_ffn_in"]
    row = lambda a: a.astype(F32).reshape(1, -1)
    layers, saved = [], []
    for l in range(n_layers):
        wa, wf, wb = w_in_prep(gathered["w_in"], l)
        wao, wco, wpool, convw = branch_w_prep(gathered["w_attn_out"], gathered["w_conv_out"], gathered["pool_w"],
                                               gathered["conv_w"], l)
        w = dict(wa=wa, wf=wf, wb=wb, wao=wao, wco=wco, wpool=wpool, convw=convw,
                 g1=row(small["norm_mix_g"][l]), g2=row(small["norm_ffn_g"][l]), pscale=row(small["pool_scale"][l]),
                 gq=row(jnp.tile(small["q_norm_g"][l], N_HEADS)), gk=row(jnp.tile(small["k_norm_g"][l], N_HEADS)),
                 fb=row(jnp.pad(small["forget_b"][l], (0, D_F - N_HEADS))))
        layers.append(w)
        h, pa, pf, pb = in_proj_fwd(x, w["g1"], wa, wf, wb)
        qx, kx, vx = qk_prep(pa, pf, w["gq"], w["gk"], w["fb"])
        ox, qxb = attn_fwd(qx, kx, vx)
        xm = mix_out_fwd(x, ox, pb, wao, convw, wco, wpool, w["pscale"], w_o, l)
        y, u = ffn_fwd(xm, w["g2"], w1, w2, l)
        saved.append(dict(x=x, h=h, pa=pa, pf=pf, pb=pb, kx=kx, vx=vx, ox=ox, qxb=qxb, xm=xm, u=u))
        x = y
    loss, dx = loss_head(x, target)
    big = {n: [None] * n_layers for n in SPLIT}
    sm = {n: [None] * n_layers for n in SMALL}
    for l in reversed(range(n_layers)):
        w, s = layers[l], saved[l]
        dxm, du, act, h2, dg2 = ffn_bwd(s["xm"], w["g2"], dx, s["u"], w1, w2, l)
        big["w_ffn_out"][l] = wgrad(act, dx.astype(BF16), "wgrad_ffn_out").reshape(4, D_FF // 4, D_MODEL)
        big["w_ffn_in"][l] = wgrad(h2, du, "wgrad_ffn_in", n_split=4)
        dpb, dox, dwo, dwao, dwco, dwpool, dconvw, dpscale = mix_out_bwd(
            dxm, s["ox"], s["pb"], w["wao"], w["convw"], w["wco"], w["wpool"], w["pscale"], w_o, l)
        dqx, dkx, dvx = attn_bwd(s["qxb"], s["kx"], s["vx"], dox)
        dpa, dpf, dgq, dgk, dfb = attn_bwd_post(s["pa"], s["pf"], dqx, dkx, dvx, w["gq"], w["gk"], w["fb"])
        dx, dg1 = in_proj_bwd(s["x"], w["g1"], dxm, dpa, dpf, dpb, w["wa"], w["wf"], w["wb"])
        big["w_in"][l] = w_in_unprep(wgrad(s["h"], dpa, "wgrad_in_qkv"), wgrad(s["h"], dpf, "wgrad_in_f"),
                                     wgrad(s["h"], dpb, "wgrad_in_b"))
        big["w_attn_out"][l], big["w_conv_out"][l], big["pool_w"][l] = branch_g_place(dwao, dwco, dwpool)
        big["w_o"][l] = dwo.reshape(4, D_MODEL // 4, D_MODEL)
        sm["norm_mix_g"][l] = dg1[0]
        sm["forget_b"][l] = dfb[0, 0:N_HEADS]
        sm["q_norm_g"][l] = dgq.reshape(N_HEADS, HEAD_DIM).sum(0)
        sm["k_norm_g"][l] = dgk.reshape(N_HEADS, HEAD_DIM).sum(0)
        sm["pool_scale"][l] = dpscale[0]
        sm["norm_ffn_g"][l] = dg2[0]
        sm["conv_w"][l] = dconvw[0:CONV_K].reshape(-1)
    return loss, dx, big, {n: jnp.stack(v) for n, v in sm.items()}


def adamw(w, g, m, v, name):
    R, C = w.shape
    tm = 256 if R % 256 == 0 else R

    def body(w_ref, g_ref, m_ref, v_ref, d_ref, nm_ref, nv_ref):
        gr = g_ref[...]
        m_new = ADAM_B1 * m_ref[...] + (1.0 - ADAM_B1) * gr
        v_new = ADAM_B2 * v_ref[...] + (1.0 - ADAM_B2) * jnp.square(gr)
        nm_ref[...] = m_new
        nv_ref[...] = v_new
        m_hat = m_new / (1.0 - ADAM_B1 ** ADAM_STEP)
        v_hat = v_new / (1.0 - ADAM_B2 ** ADAM_STEP)
        d_ref[...] = -ADAM_LR * (m_hat / (jnp.sqrt(v_hat) + ADAM_EPS) + ADAM_WD * w_ref[...])

    spec = _rows(tm, C)
    out = jax.ShapeDtypeStruct((R, C), F32)
    return pl.pallas_call(
        body,
        grid=(R // tm,),
        in_specs=[spec] * 4,
        out_specs=[spec] * 3,
        out_shape=[out] * 3,
        compiler_params=_params("parallel"),
        name=name,
    )(w, g, m, v)


MESH = pl.DeviceIdType.MESH
HBM_REF = pl.BlockSpec(memory_space=pl.ANY)
N_CHIPS = 4
N_DEV = 8
HALF = DEPTH // 2
SMALL_SHAPE = (128, LANES)


def _mesh_pos():
    return lax.axis_index("x"), lax.axis_index("y"), lax.axis_index("c")


def _other_chips(x, y):
    return [(1 - x, y), (x, 1 - y), (1 - x, 1 - y)]


def _remote(src, dst, send_sem, recv_sem, to):
    return pltpu.make_async_remote_copy(src_ref=src, dst_ref=dst, send_sem=send_sem, recv_sem=recv_sem,
                                        device_id=to, device_id_type=MESH)


def gather_shards(shards):
    n = len(shards)

    def body(*refs):
        ins, outs = refs[:n], refs[n:2 * n]
        send_sems, recv_sems, local_sems = refs[2 * n:]
        x, y, c = _mesh_pos()
        me = 2 * x + y
        sibling = (x, y, 1 - c)
        chips = _other_chips(x, y)
        mine, theirs = pl.ds(HALF * c, HALF), pl.ds(HALF * (1 - c), HALF)
        local = [pltpu.make_async_copy(ins[t], outs[t].at[pl.ds(0, DEPTH), me], local_sems.at[t]) for t in range(n)]
        for cp in local:
            cp.start()
        sent = []
        for t in range(n):
            for j, chip in enumerate(chips):
                cp = _remote(ins[t].at[mine], outs[t].at[mine, me], send_sems.at[t, j], recv_sems.at[t, j], (*chip, c))
                cp.start()
                sent.append(cp)
        for j, (px, py) in enumerate(chips):
            k = 2 * px + py
            for t in range(n):
                _remote(ins[t].at[mine], outs[t].at[mine, k], send_sems.at[t, j], recv_sems.at[t, j], (px, py, c)).wait_recv()
                cp = _remote(outs[t].at[mine, k], outs[t].at[mine, k], send_sems.at[t, 3 + j], recv_sems.at[t, 3 + j], sibling)
                cp.start()
                sent.append(cp)
        for j, (px, py) in enumerate(chips):
            k = 2 * px + py
            for t in range(n):
                _remote(outs[t].at[theirs, k], outs[t].at[theirs, k], send_sems.at[t, 3 + j], recv_sems.at[t, 3 + j],
                        sibling).wait_recv()
        for cp in sent:
            cp.wait_send()
        for cp in local:
            cp.wait()

    return pl.pallas_call(
        body,
        in_specs=[HBM_REF] * n,
        out_specs=[HBM_REF] * n,
        out_shape=[jax.ShapeDtypeStruct((DEPTH, N_CHIPS) + s.shape[1:], s.dtype) for s in shards],
        scratch_shapes=[pltpu.SemaphoreType.DMA((n, 6)), pltpu.SemaphoreType.DMA((n, 6)), pltpu.SemaphoreType.DMA((n,))],
        name="gather_shards",
    )(*shards)


def pair_exchange(grads):
    n = len(grads)

    def body(*refs):
        ins, outs = refs[:n], refs[n:2 * n]
        send_sems, recv_sems = refs[2 * n:]
        x, y, c = _mesh_pos()
        cps = [_remote(ins[t].at[pl.ds(HALF * (1 - c), HALF)], outs[t], send_sems.at[t], recv_sems.at[t], (x, y, 1 - c))
               for t in range(n)]
        for cp in cps:
            cp.start()
        for cp in cps:
            cp.wait()

    return pl.pallas_call(
        body,
        in_specs=[HBM_REF] * n,
        out_specs=[HBM_REF] * n,
        out_shape=[jax.ShapeDtypeStruct((HALF,) + g.shape[1:], g.dtype) for g in grads],
        scratch_shapes=[pltpu.SemaphoreType.DMA((n,)), pltpu.SemaphoreType.DMA((n,))],
        name="pair_exchange",
    )(*grads)


def pair_sum(core, g, t, name, tm=256):
    _, R, C = t.shape

    def body(c_ref, g_ref, t_ref, o_ref):
        o_ref[...] = (g_ref[...] + t_ref[...]).astype(BF16)

    tile = pl.BlockSpec((None, tm, C), lambda a, i, c: (a, i, 0))
    return pl.pallas_call(
        body,
        grid_spec=pltpu.PrefetchScalarGridSpec(
            num_scalar_prefetch=1,
            grid=(HALF, R // tm),
            in_specs=[pl.BlockSpec((None, tm, C), lambda a, i, c: (HALF * c[0] + a, i, 0)), tile],
            out_specs=tile,
        ),
        out_shape=jax.ShapeDtypeStruct(t.shape, BF16),
        compiler_params=_params("parallel", "parallel"),
        name=name,
    )(core, g, t)


def chip_exchange(sums):
    n = len(sums)

    def body(*refs):
        ins, outs = refs[:n], refs[n:2 * n]
        send_sems, recv_sems, local_sems = refs[2 * n:]
        x, y, c = _mesh_pos()
        me = 2 * x + y
        chips = _other_chips(x, y)
        layers = pl.ds(0, HALF)
        local = [pltpu.make_async_copy(ins[t].at[layers, me], outs[t].at[me], local_sems.at[t]) for t in range(n)]
        for cp in local:
            cp.start()
        sent = []
        for t in range(n):
            for j, (px, py) in enumerate(chips):
                cp = _remote(ins[t].at[layers, 2 * px + py], outs[t].at[me], send_sems.at[t, j], recv_sems.at[t, j], (px, py, c))
                cp.start()
                sent.append(cp)
        for t in range(n):
            for j, (px, py) in enumerate(chips):
                k = 2 * px + py
                _remote(ins[t].at[layers, k], outs[t].at[k], send_sems.at[t, j], recv_sems.at[t, j], (px, py, c)).wait_recv()
        for cp in sent:
            cp.wait_send()
        for cp in local:
            cp.wait()

    return pl.pallas_call(
        body,
        in_specs=[HBM_REF] * n,
        out_specs=[HBM_REF] * n,
        out_shape=[jax.ShapeDtypeStruct((N_CHIPS, HALF) + s.shape[2:], s.dtype) for s in sums],
        scratch_shapes=[pltpu.SemaphoreType.DMA((n, 3)), pltpu.SemaphoreType.DMA((n, 3)), pltpu.SemaphoreType.DMA((n,))],
        name="chip_exchange",
    )(*sums)


def chip_sum(recv, name, tm=128):
    _, R, C = recv.shape

    def body(r_ref, o_ref):
        acc = r_ref[0].astype(F32)
        for s in range(1, N_CHIPS):
            acc = acc + r_ref[s].astype(F32)
        o_ref[...] = acc

    return pl.pallas_call(
        body,
        grid=(R // tm,),
        in_specs=[pl.BlockSpec((N_CHIPS, tm, C), lambda i: (0, i, 0))],
        out_specs=_rows(tm, C),
        out_shape=jax.ShapeDtypeStruct((R, C), F32),
        compiler_params=_params("parallel"),
        name=name,
    )(recv)


def sibling_share(totals):
    n = len(totals)

    def body(*refs):
        ins, outs = refs[:n], refs[n:2 * n]
        send_sems, recv_sems, local_sems = refs[2 * n:]
        x, y, c = _mesh_pos()
        mine, theirs = pl.ds(HALF * c, HALF), pl.ds(HALF * (1 - c), HALF)
        local = [pltpu.make_async_copy(ins[t], outs[t].at[mine], local_sems.at[t]) for t in range(n)]
        sent = [_remote(ins[t], outs[t].at[mine], send_sems.at[t], recv_sems.at[t], (x, y, 1 - c)) for t in range(n)]
        for cp in local + sent:
            cp.start()
        for t in range(n):
            _remote(ins[t], outs[t].at[theirs], send_sems.at[t], recv_sems.at[t], (x, y, 1 - c)).wait_recv()
        for cp in sent:
            cp.wait_send()
        for cp in local:
            cp.wait()

    return pl.pallas_call(
        body,
        in_specs=[HBM_REF] * n,
        out_specs=[HBM_REF] * n,
        out_shape=[jax.ShapeDtypeStruct((DEPTH,) + t.shape[1:], t.dtype) for t in totals],
        scratch_shapes=[pltpu.SemaphoreType.DMA((n,)), pltpu.SemaphoreType.DMA((n,)), pltpu.SemaphoreType.DMA((n,))],
        name="sibling_share",
    )(*totals)


def small_allgather(small):
    def body(s_ref, a_ref, send_sems, recv_sems, local_sem):
        x, y, c = _mesh_pos()
        me = 4 * x + 2 * y + c
        own = pltpu.make_async_copy(s_ref, a_ref.at[me], local_sem)
        own.start()
        sent = []
        for k in range(1, N_DEV):
            peer = (x ^ (k >> 2), y ^ ((k >> 1) & 1), c ^ (k & 1))
            cp = _remote(s_ref, a_ref.at[me], send_sems.at[k - 1], recv_sems.at[k - 1], peer)
            cp.start()
            sent.append(cp)
        for k in range(1, N_DEV):
            px, py, pc = x ^ (k >> 2), y ^ ((k >> 1) & 1), c ^ (k & 1)
            _remote(s_ref, a_ref.at[4 * px + 2 * py + pc], send_sems.at[k - 1], recv_sems.at[k - 1], (px, py, pc)).wait_recv()
        for cp in sent:
            cp.wait_send()
        own.wait()

    return pl.pallas_call(
        body,
        in_specs=[HBM_REF],
        out_specs=HBM_REF,
        out_shape=jax.ShapeDtypeStruct((N_DEV,) + SMALL_SHAPE, small.dtype),
        scratch_shapes=[pltpu.SemaphoreType.DMA((N_DEV - 1,)), pltpu.SemaphoreType.DMA((N_DEV - 1,)), pltpu.SemaphoreType.DMA],
        name="small_allgather",
    )(small)


def small_sum(blocks):
    def body(a_ref, o_ref):
        acc = a_ref[0]
        for d in range(1, N_DEV):
            acc = acc + a_ref[d]
        o_ref[...] = acc

    return pl.pallas_call(
        body,
        in_specs=[pl.BlockSpec(memory_space=pltpu.VMEM)],
        out_specs=pl.BlockSpec(memory_space=pltpu.VMEM),
        out_shape=jax.ShapeDtypeStruct(SMALL_SHAPE, F32),
        name="small_sum",
    )(blocks)


def pack_small(grads, loss):
    flat = jnp.concatenate([grads[n].reshape(-1) for n in SMALL] + [loss.reshape(-1)])
    size = SMALL_SHAPE[0] * SMALL_SHAPE[1]
    return jnp.pad(flat, (0, size - flat.shape[0])).reshape(SMALL_SHAPE)


def unpack_small(packed):
    flat = packed.reshape(-1)
    out, off = {}, 0
    for n, size in SMALL.items():
        out[n] = flat[off:off + DEPTH * size].reshape(DEPTH, size)
        off += DEPTH * size
    return out, flat[off]


def kernel(x, norm_mix_g, w_in, forget_b, q_norm_g, k_norm_g, w_attn_out, conv_w, w_conv_out, pool_w, pool_scale, w_o, norm_ffn_g, w_ffn_in, w_ffn_out, loss_target, m_norm_mix_g, m_w_in, m_forget_b, m_q_norm_g, m_k_norm_g, m_w_attn_out, m_conv_w, m_w_conv_out, m_pool_w, m_pool_scale, m_w_o, m_norm_ffn_g, m_w_ffn_in, m_w_ffn_out, v_norm_mix_g, v_w_in, v_forget_b, v_q_norm_g, v_k_norm_g, v_w_attn_out, v_conv_w, v_w_conv_out, v_pool_w, v_pool_scale, v_w_o, v_norm_ffn_g, v_w_ffn_in, v_w_ffn_out):
    given = dict(locals())
    weights = {n: given[n] for n in WEIGHTS}

    names = list(SPLIT) + ["conv_w"]
    gathered = dict(zip(names, gather_shards([weights[n].astype(BF16) for n in SPLIT] + [conv_w])))
    small = {n: weights[n] for n in SMALL if n != "conv_w"}

    loss, dx, big, small_grads = local_step(x[0], loss_target[0], gathered, small)

    core = lax.axis_index("c").astype(jnp.int32).reshape(1)
    grads = [jnp.stack(big[n]) for n in SPLIT]
    theirs = pair_exchange(grads)
    flat3 = lambda a: a.reshape(a.shape[0], -1, a.shape[-1])
    sums = [pair_sum(core, flat3(g), flat3(t), "pair_sum_" + n).reshape(t.shape) for n, g, t in zip(SPLIT, grads, theirs)]
    recv = chip_exchange(sums)
    totals = [chip_sum(r.reshape(N_CHIPS, -1, r.shape[-1]), "chip_sum_" + n).reshape(r.shape[1:]) for n, r in zip(SPLIT, recv)]
    reduced = dict(zip(SPLIT, sibling_share(totals)))

    small_total, loss_sum = unpack_small(small_sum(small_allgather(pack_small(small_grads, loss))))
    chip = 2 * lax.axis_index("x") + lax.axis_index("y")
    cols = D_CONV // N_CHIPS
    small_total["conv_w"] = lax.dynamic_slice_in_dim(small_total["conv_w"].reshape(DEPTH, CONV_K, D_CONV), chip * cols, cols, axis=2)
    for n in SMALL:
        reduced[n] = small_total[n].reshape(weights[n].shape)

    deltas, new_m, new_v = {}, {}, {}
    for n in WEIGHTS:
        w = weights[n]
        flat = (-1, w.shape[-1])
        d, nm, nv = adamw(w.reshape(flat), reduced[n].reshape(flat), given["m_" + n].reshape(flat),
                          given["v_" + n].reshape(flat), "adamw_" + n)
        deltas[n], new_m[n], new_v[n] = d.reshape(w.shape), nm.reshape(w.shape), nv.reshape(w.shape)
    return (loss_sum, dx[None], *[reduced[n] for n in WEIGHTS], *[deltas[n] for n in WEIGHTS],
            *[new_m[n] for n in WEIGHTS], *[new_v[n] for n in WEIGHTS])
```

```python
import functools

import numpy as np
import jax
import jax.numpy as jnp
from jax import lax
from jax.experimental import pallas as pl
from jax.experimental.pallas import tpu as pltpu

F32 = jnp.float32
BF16 = jnp.bfloat16

D_MODEL = 1024
DEPTH = 4
HEAD_DIM = 64
N_HEADS = 8
D_ATTN = 512
D_CONV = 256
D_POOL = 256
D_FF = 2816
D_IN = 5640
CONV_K = 3
POOL_WINDOWS = (2, 4, 8, 16)
N_GROUPS = len(POOL_WINDOWS)
EPS = 1e-6
ADAM_LR, ADAM_B1, ADAM_B2, ADAM_EPS, ADAM_WD, ADAM_STEP = 0.001, 0.9, 0.999, 1e-08, 0.01, 10

D_QKV = 3 * D_ATTN
D_F = 128
D_B = 3 * D_CONV + D_POOL + 3 * D_MODEL
D_LOCAL = 3 * D_CONV + D_POOL

LANES = 128
D_HEADS = N_HEADS * LANES
HALO = 16
VMEM_LIMIT = 56 * 1024 * 1024
NEG = -1e30
LOG2E = 1.4426950408889634
LN2 = 0.6931471805599453

TM = 256
TM_MIX = 128
TQ = 512
TK = 1024
KB = 512
QT = 1024

LANE_C = 64
LANE_ONE = 67
LANE_LSE = 70
N_PIECES = 3


def _dot(a, b):
    return jnp.dot(a, b, preferred_element_type=F32)


def _dot_nt(a, b):
    return lax.dot_general(a, b, (((1,), (1,)), ((), ())), preferred_element_type=F32)


def _dot_tn(a, b):
    return lax.dot_general(a, b, (((0,), (0,)), ((), ())), preferred_element_type=F32)


def _params(*sem):
    return pltpu.CompilerParams(dimension_semantics=sem, vmem_limit_bytes=VMEM_LIMIT)


def _rows(tm, n):
    return pl.BlockSpec((tm, n), lambda i, *_: (i, 0))


def _whole(a):
    nd = a.ndim
    return pl.BlockSpec(a.shape, lambda *_: (0,) * nd)


def _layer(shape):
    nd = len(shape)
    return pl.BlockSpec((None,) + tuple(shape), lambda *a: (a[-1][0],) + (0,) * nd)


def _layer_index(l):
    return jnp.full((1,), l, jnp.int32)


def _split_bf16(x):
    hi = x.astype(BF16)
    lo = (x - hi.astype(F32)).astype(BF16)
    return hi, lo


def _pieces(x):
    p1 = x.astype(BF16)
    r1 = x - p1.astype(F32)
    p2 = r1.astype(BF16)
    p3 = (r1 - p2.astype(F32)).astype(BF16)
    return p1, p2, p3


def _sigmoid(x):
    return 1.0 / (1.0 + jnp.exp(-x))


def w_in_prep(win, l):
    tr = 256
    n = D_IN // 4
    v_rest = D_QKV - n
    b0 = v_rest + N_HEADS

    def body(s0, s1, s2, s3, wa_ref, wf_ref, wb_ref):
        b = s1[...]
        wa_ref[...] = jnp.concatenate([s0[...], b[:, 0:v_rest]], axis=1)
        wf_ref[...] = jnp.concatenate([b[:, v_rest:b0], jnp.zeros((tr, D_F - N_HEADS), b.dtype)], axis=1)
        wb_ref[...] = jnp.concatenate([b[:, b0:n], s2[...], s3[...]], axis=1)

    shard = lambda j: pl.BlockSpec((None, None, tr, n), lambda i: (l, j, i, 0))
    return pl.pallas_call(
        body,
        grid=(D_MODEL // tr,),
        in_specs=[shard(0), shard(1), shard(2), shard(3)],
        out_specs=[_rows(tr, D_QKV), _rows(tr, D_F), _rows(tr, D_B)],
        out_shape=[jax.ShapeDtypeStruct((D_MODEL, D_QKV), win.dtype), jax.ShapeDtypeStruct((D_MODEL, D_F), win.dtype),
                   jax.ShapeDtypeStruct((D_MODEL, D_B), win.dtype)],
        compiler_params=_params("parallel"),
        name="w_in_prep",
    )(win, win, win, win)


def _into_layer(buf):
    return dict(in_spec=HBM_REF, out_shape=jax.ShapeDtypeStruct(buf.shape, buf.dtype), aliases={1: 0})


def w_in_unprep(dwa, dwf, dwb, buf, l):
    tr = 256
    n = D_IN // 4
    v_rest = D_QKV - n
    b1 = n - v_rest - N_HEADS
    place = _into_layer(buf)

    def body(l_ref, buf_ref, a_ref, f_ref, b_ref, o_ref):
        a = a_ref[...]
        b = b_ref[...]
        o_ref[0] = a[:, 0:n]
        o_ref[1] = jnp.concatenate([a[:, n:D_QKV], f_ref[:, 0:N_HEADS], b[:, 0:b1]], axis=1)
        o_ref[2] = b[:, b1:b1 + n]
        o_ref[3] = b[:, b1 + n:D_B]

    return pl.pallas_call(
        body,
        grid_spec=pltpu.PrefetchScalarGridSpec(
            num_scalar_prefetch=1,
            grid=(D_MODEL // tr,),
            in_specs=[place["in_spec"], _rows(tr, D_QKV), _rows(tr, D_F), _rows(tr, D_B)],
            out_specs=pl.BlockSpec((None, 4, tr, n), lambda i, l: (l[0], 0, i, 0)),
        ),
        out_shape=place["out_shape"],
        input_output_aliases=place["aliases"],
        compiler_params=_params("parallel"),
        name="w_in_unprep",
    )(_layer_index(l), buf, dwa, dwf, dwb)


def branch_w_prep(wao, wco, pw, cw, l):
    gd = D_POOL // N_GROUPS
    od = D_MODEL // N_GROUPS

    def body(wao_ref, wco_ref, pw_ref, cw_ref, ao_ref, co_ref, po_ref, co8_ref):
        a = jnp.concatenate([wao_ref[j] for j in range(4)], axis=1)
        gap = jnp.zeros((LANES - HEAD_DIM, D_MODEL), a.dtype)
        ao_ref[...] = jnp.concatenate(
            [blk for h in range(N_HEADS) for blk in (a[h * HEAD_DIM:(h + 1) * HEAD_DIM], gap)], axis=0)
        co_ref[...] = jnp.concatenate([wco_ref[j] for j in range(4)], axis=1)
        zero = jnp.zeros((gd, od), a.dtype)
        po_ref[...] = jnp.concatenate(
            [jnp.concatenate([jnp.concatenate([pw_ref[j, g] for j in range(4)], axis=1) if g2 == g else zero
                              for g2 in range(N_GROUPS)], axis=1) for g in range(N_GROUPS)], axis=0)
        co8_ref[...] = jnp.zeros_like(co8_ref)
        co8_ref[0:CONV_K, :] = jnp.concatenate([cw_ref[j] for j in range(4)], axis=1)

    sel = lambda *shape: pl.BlockSpec((None,) + shape, lambda i: (l,) + (0,) * len(shape))
    return pl.pallas_call(
        body,
        grid=(1,),
        in_specs=[sel(4, D_ATTN, D_MODEL // 4), sel(4, D_CONV, D_MODEL // 4), sel(4, N_GROUPS, gd, od // 4),
                  sel(4, CONV_K, D_CONV // 4)],
        out_specs=[pl.BlockSpec((D_HEADS, D_MODEL), lambda i: (0, 0)), pl.BlockSpec((D_CONV, D_MODEL), lambda i: (0, 0)),
                   pl.BlockSpec((D_POOL, D_MODEL), lambda i: (0, 0)), pl.BlockSpec((8, D_CONV), lambda i: (0, 0))],
        out_shape=[jax.ShapeDtypeStruct((D_HEADS, D_MODEL), BF16), jax.ShapeDtypeStruct((D_CONV, D_MODEL), BF16),
                   jax.ShapeDtypeStruct((D_POOL, D_MODEL), BF16), jax.ShapeDtypeStruct((8, D_CONV), F32)],
        name="branch_w_prep",
    )(wao, wco, pw, cw)


def branch_g_place(dwao, dwco, dwpool, dwo, bufs, l):
    gd = D_POOL // N_GROUPS
    od = D_MODEL // N_GROUPS
    q = D_MODEL // 4

    def body(l_ref, b0, b1, b2, b3, a_ref, c_ref, p_ref, w_ref, ao_ref, co_ref, po_ref, wo_ref):
        a = jnp.concatenate([a_ref[h * LANES:h * LANES + HEAD_DIM, :] for h in range(N_HEADS)], axis=0)
        c = c_ref[...]
        p = p_ref[...]
        for j in range(4):
            ao_ref[j] = a[:, j * q:(j + 1) * q]
            co_ref[j] = c[:, j * q:(j + 1) * q]
            wo_ref[j] = w_ref[j * q:(j + 1) * q, :]
            for g in range(N_GROUPS):
                c0 = g * od + j * (od // 4)
                po_ref[j, g] = p[g * gd:(g + 1) * gd, c0:c0 + od // 4]

    whole = lambda a: pl.BlockSpec(a.shape, lambda i, l: (0,) * a.ndim)
    layer = lambda b: pl.BlockSpec((None,) + b.shape[1:], lambda i, l: (l[0],) + (0,) * (b.ndim - 1))
    return pl.pallas_call(
        body,
        grid_spec=pltpu.PrefetchScalarGridSpec(
            num_scalar_prefetch=1,
            grid=(1,),
            in_specs=[HBM_REF] * 4 + [whole(dwao), whole(dwco), whole(dwpool), whole(dwo)],
            out_specs=[layer(b) for b in bufs],
        ),
        out_shape=[jax.ShapeDtypeStruct(b.shape, b.dtype) for b in bufs],
        input_output_aliases={1: 0, 2: 1, 3: 2, 4: 3},
        compiler_params=_params("arbitrary"),
        name="branch_g_place",
    )(_layer_index(l), *bufs, dwao, dwco, dwpool, dwo)


def in_proj_fwd(x, g, wa, wf, wb):
    S = x.shape[0]

    def body(x_ref, g_ref, wa_ref, wf_ref, wb_ref, h_ref, pa_ref, pf_ref, pb_ref):
        xf = x_ref[...]
        r = lax.rsqrt(jnp.mean(xf * xf, axis=-1, keepdims=True) + EPS)
        h = (xf * r * g_ref[...]).astype(BF16)
        h_ref[...] = h
        pa_ref[...] = _dot(h, wa_ref[...]).astype(BF16)
        pf_ref[...] = _dot(h, wf_ref[...])
        pb_ref[...] = _dot(h, wb_ref[...]).astype(BF16)

    return pl.pallas_call(
        body,
        grid=(S // TM,),
        in_specs=[_rows(TM, D_MODEL), _whole(g), _whole(wa), _whole(wf), _whole(wb)],
        out_specs=[_rows(TM, D_MODEL), _rows(TM, D_QKV), _rows(TM, D_F), _rows(TM, D_B)],
        out_shape=[
            jax.ShapeDtypeStruct((S, D_MODEL), BF16),
            jax.ShapeDtypeStruct((S, D_QKV), BF16),
            jax.ShapeDtypeStruct((S, D_F), F32),
            jax.ShapeDtypeStruct((S, D_B), BF16),
        ],
        compiler_params=_params("parallel"),
        name="in_proj_fwd",
    )(x, g, wa, wf, wb)


def in_proj_bwd(x, g, dxm, dpa, dpf, dpb, wa, wf, wb):
    S = x.shape[0]

    def body(x_ref, g_ref, dxm_ref, dpa_ref, dpf_ref, dpb_ref, wa_ref, wf_ref, wb_ref, dx_ref, dg_ref):
        @pl.when(pl.program_id(0) == 0)
        def _():
            dg_ref[...] = jnp.zeros_like(dg_ref)

        dh = _dot_nt(dpa_ref[...], wa_ref[...]) + _dot_nt(dpf_ref[...], wf_ref[...]) + _dot_nt(dpb_ref[...], wb_ref[...])
        xf = x_ref[...]
        r = lax.rsqrt(jnp.mean(xf * xf, axis=-1, keepdims=True) + EPS)
        xhat = xf * r
        dg_ref[...] += jnp.sum(dh * xhat, axis=0, keepdims=True)
        gdh = dh * g_ref[...]
        dx_ref[...] = dxm_ref[...] + r * (gdh - xhat * jnp.mean(xhat * gdh, axis=-1, keepdims=True))

    return pl.pallas_call(
        body,
        grid=(S // TM,),
        in_specs=[_rows(TM, D_MODEL), _whole(g), _rows(TM, D_MODEL), _rows(TM, D_QKV), _rows(TM, D_F), _rows(TM, D_B),
                  _whole(wa), _whole(wf), _whole(wb)],
        out_specs=[_rows(TM, D_MODEL), pl.BlockSpec((1, D_MODEL), lambda i: (0, 0))],
        out_shape=[jax.ShapeDtypeStruct((S, D_MODEL), F32), jax.ShapeDtypeStruct((1, D_MODEL), F32)],
        compiler_params=_params("arbitrary"),
        name="in_proj_bwd",
    )(x, g, dxm, dpa, dpf, dpb, wa, wf, wb)


def wgrad(xa, dy, name, n_split=1, ts=512):
    S, K = xa.shape
    N = dy.shape[1]
    tn = N // n_split if n_split > 1 else min(512, N)
    n_steps = S // ts

    def body(x_ref, dy_ref, o_ref):
        @pl.when(pl.program_id(1) == 0)
        def _():
            o_ref[...] = jnp.zeros_like(o_ref)

        o_ref[...] += _dot_tn(x_ref[...], dy_ref[...])

    if n_split > 1:
        out_spec = pl.BlockSpec((None, K, tn), lambda j, k: (j, 0, 0))
        out_shape = jax.ShapeDtypeStruct((n_split, K, tn), F32)
    else:
        out_spec = pl.BlockSpec((K, tn), lambda j, k: (0, j))
        out_shape = jax.ShapeDtypeStruct((K, N), F32)
    return pl.pallas_call(
        body,
        grid=(N // tn, n_steps),
        in_specs=[pl.BlockSpec((ts, K), lambda j, k: (k, 0)), pl.BlockSpec((ts, tn), lambda j, k: (k, j))],
        out_specs=out_spec,
        out_shape=out_shape,
        compiler_params=_params("parallel", "arbitrary"),
        name=name,
    )(xa, dy)


def wgrad_into(xa, dy, name, buf, l, ts=512):
    S, K = xa.shape
    N = dy.shape[1]
    split = buf.ndim == 4
    tn = buf.shape[-1] if split else min(512, N)
    place = _into_layer(buf)

    def body(l_ref, buf_ref, x_ref, dy_ref, o_ref):
        @pl.when(pl.program_id(1) == 0)
        def _():
            o_ref[...] = jnp.zeros_like(o_ref)

        o_ref[...] += _dot_tn(x_ref[...], dy_ref[...])

    if split:
        out_spec = pl.BlockSpec((None, None, K, tn), lambda j, k, l: (l[0], j, 0, 0))
    else:
        out_spec = pl.BlockSpec((None, K, tn), lambda j, k, l: (l[0], 0, j))
    return pl.pallas_call(
        body,
        grid_spec=pltpu.PrefetchScalarGridSpec(
            num_scalar_prefetch=1,
            grid=(N // tn, S // ts),
            in_specs=[place["in_spec"], pl.BlockSpec((ts, K), lambda j, k, l: (k, 0)),
                      pl.BlockSpec((ts, tn), lambda j, k, l: (k, j))],
            out_specs=out_spec,
        ),
        out_shape=place["out_shape"],
        input_output_aliases=place["aliases"],
        compiler_params=_params("parallel", "arbitrary"),
        name=name,
    )(_layer_index(l), buf, xa, dy)


def _head_mean_matrix():
    h = np.arange(D_ATTN) // HEAD_DIM
    return jnp.asarray((h[:, None] == h[None, :]).astype(np.float32) / HEAD_DIM, BF16)


def _place_matrix(lane0):
    m = np.zeros((N_PIECES * LANES, D_HEADS), np.float32)
    for i in range(N_PIECES):
        for h in range(N_HEADS):
            m[i * LANES + h, h * LANES + lane0 + i] = 1.0
    return jnp.asarray(m, BF16)


def _tri(n, upper):
    r = np.arange(n)
    m = (r[None, :] >= r[:, None]) if upper else (r[None, :] <= r[:, None])
    return jnp.asarray(m.astype(np.float32), BF16)


def _lanes_in(lane, lo, n):
    return (lane >= lo) & (lane < lo + n)


def qk_prep(pa, pf, gq, gk, fb):
    S = pa.shape[0]
    bd = _head_mean_matrix()
    tri = _tri(TM, upper=False)
    place_q = _place_matrix(LANE_C)
    place_k = _place_matrix(LANE_ONE)

    def body(q_ref, k_ref, v_ref, pf_ref, gq_ref, gk_ref, fb_ref, bd_ref, tri_ref, pq_ref, pk_ref,
             qx_ref, kx_ref, vx_ref, carry):
        @pl.when(pl.program_id(0) == 0)
        def _():
            carry[...] = jnp.zeros_like(carry)

        def head_norm(x_ref, g_ref, scale):
            xf = x_ref[...].astype(F32)
            ms = _dot((xf * xf).astype(BF16), bd_ref[...])
            return xf * lax.rsqrt(ms + EPS) * g_ref[...] * scale

        qh = head_norm(q_ref, gq_ref, HEAD_DIM ** -0.5 * LOG2E)
        kh = head_norm(k_ref, gk_ref, 1.0)
        vf = v_ref[...].astype(F32)

        z = pf_ref[...] + fb_ref[...]
        logf = jnp.minimum(z, 0.0) - jnp.log(1.0 + jnp.exp(-jnp.abs(z)))
        hi, lo = _split_bf16(logf)
        c = _dot(tri_ref[...], hi) + _dot(tri_ref[...], lo) + carry[...]
        carry[...] += jnp.sum(hi.astype(F32) + lo.astype(F32), axis=0, keepdims=True)
        pieces = jnp.concatenate(_pieces(c * LOG2E), axis=1)
        cq = _dot(pieces, pq_ref[...])
        ck = _dot(pieces, pk_ref[...])

        lane = lax.broadcasted_iota(jnp.int32, (TM, LANES), 1)
        low = lane < HEAD_DIM
        ones_q = _lanes_in(lane, LANE_ONE, N_PIECES).astype(F32)
        ones_k = (_lanes_in(lane, LANE_C, N_PIECES) | _lanes_in(lane, LANE_LSE, N_PIECES)).astype(F32)
        ones_v = _lanes_in(lane, LANE_C, N_PIECES + 1).astype(F32)
        for h in range(N_HEADS):
            blk = slice((h // 2) * LANES, (h // 2 + 1) * LANES)
            head = (lambda a: pltpu.roll(a[:, blk], HEAD_DIM, 1)) if h % 2 else (lambda a: a[:, blk])
            mine = slice(h * LANES, (h + 1) * LANES)
            qx_ref[h] = jnp.where(low, head(qh), cq[:, mine] + ones_q).astype(BF16)
            kx_ref[h] = jnp.where(low, head(kh), ones_k - ck[:, mine]).astype(BF16)
            vx_ref[h] = jnp.where(low, head(vf), ones_v).astype(BF16)

    heads = pl.BlockSpec((N_HEADS, TM, LANES), lambda i: (0, i, 0))
    out = jax.ShapeDtypeStruct((N_HEADS, S, LANES), BF16)
    return pl.pallas_call(
        body,
        grid=(S // TM,),
        in_specs=[pl.BlockSpec((TM, D_ATTN), lambda i: (i, 0)), pl.BlockSpec((TM, D_ATTN), lambda i: (i, 1)),
                  pl.BlockSpec((TM, D_ATTN), lambda i: (i, 2)),
                  _rows(TM, D_F), _whole(gq), _whole(gk), _whole(fb), _whole(bd), _whole(tri), _whole(place_q), _whole(place_k)],
        out_specs=[heads, heads, heads],
        out_shape=[out, out, out],
        scratch_shapes=[pltpu.VMEM((1, D_F), F32)],
        compiler_params=_params("arbitrary"),
        name="qk_prep",
    )(pa, pa, pa, pf, gq, gk, fb, bd, tri, place_q, place_k)


def attn_fwd(qx, kx, vx):
    S = qx.shape[1]
    nq = S // TQ

    def body(q_ref, k_ref, v_ref, o_ref, qb_ref):
        i = pl.program_id(1)
        lane = lax.broadcasted_iota(jnp.int32, (TQ, LANES), 1)
        row = lax.broadcasted_iota(jnp.int32, (TQ, TK), 0)
        col = lax.broadcasted_iota(jnp.int32, (TQ, TK), 1)
        q = [q_ref[0], q_ref[1]]
        n_full = (i * TQ) // TK

        def step(kt, carry, masked):
            ks = pl.multiple_of(kt * TK, TK)
            new = []
            for j in range(2):
                m, acc = carry[j]
                z = _dot_nt(q[j], k_ref[j, pl.ds(ks, TK), :])
                if masked:
                    z = jnp.where(row + i * TQ >= col + ks, z, NEG)
                m_new = jnp.maximum(m, jnp.max(z, axis=1, keepdims=True))
                pr = jnp.exp2(z - m_new)
                acc = jnp.exp2(m - m_new) * acc + _dot(pr.astype(BF16), v_ref[j, pl.ds(ks, TK), :])
                new.append((m_new, acc))
            return tuple(new)

        init = tuple((jnp.full((TQ, 1), NEG, F32), jnp.zeros((TQ, LANES), F32)) for _ in range(2))
        carry = lax.fori_loop(0, n_full, functools.partial(step, masked=False), init)
        carry = step(n_full, carry, True)
        for j in range(2):
            m, acc = carry[j]
            l = jnp.sum(jnp.where(lane == LANE_C, acc, 0.0), axis=1, keepdims=True)
            o_ref[j] = acc / l
            n1, n2, n3 = _pieces(-(m + jnp.log(l) * LOG2E))
            qb_ref[j] = jnp.where(lane == LANE_LSE, n1,
                                  jnp.where(lane == LANE_LSE + 1, n2, jnp.where(lane == LANE_LSE + 2, n3, q[j])))

    pair_tile = pl.BlockSpec((2, TQ, LANES), lambda p, i: (p, i, 0))
    pair_all = pl.BlockSpec((2, S, LANES), lambda p, i: (p, 0, 0))
    return pl.pallas_call(
        body,
        grid=(N_HEADS // 2, nq),
        in_specs=[pair_tile, pair_all, pair_all],
        out_specs=[pair_tile, pair_tile],
        out_shape=[jax.ShapeDtypeStruct((N_HEADS, S, LANES), F32), jax.ShapeDtypeStruct((N_HEADS, S, LANES), BF16)],
        compiler_params=_params("parallel", "parallel"),
        name="attn_fwd",
    )(qx, kx, vx)


def attn_bwd(qxb, kx, vx, dox):
    S = qxb.shape[1]
    nq = S // QT

    def body(q_ref, k_ref, v_ref, do_ref, dq_ref, dk_ref, dv_ref):
        kt = pl.program_id(1)

        @pl.when(kt == 0)
        def _():
            dq_ref[...] = jnp.zeros_like(dq_ref)

        row = lax.broadcasted_iota(jnp.int32, (QT, KB), 0)
        col = lax.broadcasted_iota(jnp.int32, (QT, KB), 1)
        k = [k_ref[0], k_ref[1]]
        v = [v_ref[0], v_ref[1]]
        q_first = (kt * KB) // QT

        def step(qi, carry, masked):
            qs = pl.multiple_of(qi * QT, QT)
            new = []
            for j in range(2):
                dk, dv = carry[j]
                q = q_ref[j, pl.ds(qs, QT), :]
                dout = do_ref[j, pl.ds(qs, QT), :]
                z = _dot_nt(q, k[j])
                if masked:
                    z = jnp.where(row + qs >= col + kt * KB, z, NEG)
                pr = jnp.exp2(z)
                dv = dv + _dot_tn(pr.astype(BF16), dout)
                dsb = (pr * _dot_nt(dout, v[j])).astype(BF16)
                dk = dk + _dot_tn(dsb, q)
                dq_ref[j, pl.ds(qs, QT), :] += _dot(dsb, k[j])
                new.append((dk, dv))
            return tuple(new)

        zero = jnp.zeros((KB, LANES), F32)
        carry = step(q_first, ((zero, zero), (zero, zero)), True)
        carry = lax.fori_loop(q_first + 1, nq, functools.partial(step, masked=False), carry)
        for j in range(2):
            dk_ref[j] = carry[j][0]
            dv_ref[j] = carry[j][1]

    pair_tile = pl.BlockSpec((2, KB, LANES), lambda p, kt: (p, kt, 0))
    pair_all = pl.BlockSpec((2, S, LANES), lambda p, kt: (p, 0, 0))
    out = jax.ShapeDtypeStruct((N_HEADS, S, LANES), F32)
    return pl.pallas_call(
        body,
        grid=(N_HEADS // 2, S // KB),
        in_specs=[pair_all, pair_tile, pair_tile, pair_all],
        out_specs=[pair_all, pair_tile, pair_tile],
        out_shape=[out, out, out],
        compiler_params=_params("arbitrary", "arbitrary"),
        name="attn_bwd",
    )(qxb, kx, vx, dox)


def attn_bwd_post(pa, pf, dqx, dkx, dvx, gq, gk, fb):
    S = pa.shape[0]
    nt = S // TM
    bd = _head_mean_matrix()
    triu = _tri(TM, upper=True)
    rev = lambda i: nt - 1 - i

    def body(q_ref, k_ref, pf_ref, dqx_ref, dkx_ref, dvx_ref, gq_ref, gk_ref, fb_ref, bd_ref, triu_ref,
             dpa_ref, dpf_ref, dgq_ref, dgk_ref, dfb_ref, carry):
        @pl.when(pl.program_id(0) == 0)
        def _():
            carry[...] = jnp.zeros_like(carry)
            dgq_ref[...] = jnp.zeros_like(dgq_ref)
            dgk_ref[...] = jnp.zeros_like(dgk_ref)
            dfb_ref[...] = jnp.zeros_like(dfb_ref)

        lane = lax.broadcasted_iota(jnp.int32, (TM, LANES), 1)

        def columns(ref):
            return jnp.concatenate([jnp.where(lane < HEAD_DIM, ref[2 * p], pltpu.roll(ref[2 * p + 1], HEAD_DIM, 1))
                                    for p in range(N_HEADS // 2)], axis=1)

        def head_norm_bwd(x_ref, dy, g_ref, dg_ref):
            xf = x_ref[...].astype(F32)
            r = lax.rsqrt(_dot((xf * xf).astype(BF16), bd_ref[...]) + EPS)
            xhat = xf * r
            dg_ref[...] += jnp.sum(dy * xhat, axis=0, keepdims=True)
            gdy = dy * g_ref[...]
            return (r * (gdy - xhat * _dot((xhat * gdy).astype(BF16), bd_ref[...]))).astype(BF16)

        dpa_ref[:, 0:D_ATTN] = head_norm_bwd(q_ref, columns(dqx_ref) * HEAD_DIM ** -0.5, gq_ref, dgq_ref)
        dpa_ref[:, D_ATTN:2 * D_ATTN] = head_norm_bwd(k_ref, columns(dkx_ref) * LN2, gk_ref, dgk_ref)
        dpa_ref[:, 2 * D_ATTN:3 * D_ATTN] = columns(dvx_ref).astype(BF16)

        dc = jnp.zeros((TM, LANES), F32)
        for h in range(N_HEADS):
            both = jnp.where(lane == LANE_C, dqx_ref[h], 0.0) - jnp.where(lane == LANE_ONE, dkx_ref[h], 0.0)
            dc = jnp.where(lane == h, jnp.sum(both, axis=1, keepdims=True), dc)
        hi, lo = _split_bf16(dc)
        dlogf = _dot(triu_ref[...], hi) + _dot(triu_ref[...], lo) + carry[...]
        first = lax.broadcasted_iota(jnp.int32, (TM, D_F), 0) == 0
        carry[...] = jnp.sum(jnp.where(first, dlogf, 0.0), axis=0, keepdims=True)
        df = dlogf * _sigmoid(-(pf_ref[...] + fb_ref[...]))
        dfb_ref[...] += jnp.sum(df, axis=0, keepdims=True)
        dpf_ref[...] = df.astype(BF16)

    heads = pl.BlockSpec((N_HEADS, TM, LANES), lambda i: (0, rev(i), 0))
    return pl.pallas_call(
        body,
        grid=(nt,),
        in_specs=[pl.BlockSpec((TM, D_ATTN), lambda i: (rev(i), 0)), pl.BlockSpec((TM, D_ATTN), lambda i: (rev(i), 1)),
                  pl.BlockSpec((TM, D_F), lambda i: (rev(i), 0)), heads, heads, heads,
                  _whole(gq), _whole(gk), _whole(fb), _whole(bd), _whole(triu)],
        out_specs=[pl.BlockSpec((TM, D_QKV), lambda i: (rev(i), 0)), pl.BlockSpec((TM, D_F), lambda i: (rev(i), 0)),
                   pl.BlockSpec((1, D_ATTN), lambda i: (0, 0)), pl.BlockSpec((1, D_ATTN), lambda i: (0, 0)),
                   pl.BlockSpec((1, D_F), lambda i: (0, 0))],
        out_shape=[jax.ShapeDtypeStruct((S, D_QKV), BF16), jax.ShapeDtypeStruct((S, D_F), BF16),
                   jax.ShapeDtypeStruct((1, D_ATTN), F32), jax.ShapeDtypeStruct((1, D_ATTN), F32),
                   jax.ShapeDtypeStruct((1, D_F), F32)],
        scratch_shapes=[pltpu.VMEM((1, D_F), F32)],
        compiler_params=_params("arbitrary"),
        name="attn_bwd_post",
    )(pa, pa, pf, dqx, dkx, dvx, gq, gk, fb, bd, triu)


def _shift_down(ext, k):
    return pltpu.roll(ext, k, 0)[HALO:]


def _shift_up(ext, k, n):
    return pltpu.roll(ext, n + HALO - k, 0)[:n]


def _pool_lane_select(a2, a4, a8, a16, lane):
    return jnp.where(lane < 64, a2, jnp.where(lane < 128, a4, jnp.where(lane < 192, a8, a16)))


def _local_branches(o, pb, halo, have_prev, row0, wao, convw, wco, wpool, pscale):
    n = pb.shape[0]
    cx = pb[:, 0:D_CONV].astype(F32)
    cb = pb[:, D_CONV:2 * D_CONV].astype(F32)
    cc = pb[:, 2 * D_CONV:3 * D_CONV].astype(F32)
    px = pb[:, 3 * D_CONV:D_LOCAL].astype(F32)
    keep = have_prev.astype(F32)
    z = cc * cx
    z_ext = jnp.concatenate([halo[:, 2 * D_CONV:3 * D_CONV].astype(F32) * halo[:, 0:D_CONV].astype(F32) * keep, z], axis=0)
    z1 = _shift_down(z_ext, 1)
    z2 = _shift_down(z_ext, 2)
    conv = convw[0:1, :] * z2 + convw[1:2, :] * z1 + convw[2:3, :] * z
    cm = cb * conv

    u_ext = jnp.concatenate([halo[:, 3 * D_CONV:D_LOCAL].astype(F32) * keep, px], axis=0)
    s2 = u_ext + pltpu.roll(u_ext, 1, 0)
    s4 = s2 + pltpu.roll(s2, 2, 0)
    s8 = s4 + pltpu.roll(s4, 4, 0)
    s16 = s8 + pltpu.roll(s8, 8, 0)
    lane = lax.broadcasted_iota(jnp.int32, (n, D_POOL), 1)
    win = _pool_lane_select(2.0, 4.0, 8.0, 16.0, lane)
    t = (row0 + lax.broadcasted_iota(jnp.int32, (n, D_POOL), 0)).astype(F32)
    cnt = jnp.minimum(t + 1.0, win)
    feat = _pool_lane_select(s2[HALO:], s4[HALO:], s8[HALO:], s16[HALO:], lane) / cnt - px

    ya = _dot(o, wao)
    yc = _dot(cm.astype(BF16), wco)
    yp_pre = _dot(feat.astype(BF16), wpool)
    yp = yp_pre * pscale
    return dict(cx=cx, cb=cb, cc=cc, z=z, z1=z1, z2=z2, conv=conv, cm=cm, feat=feat, cnt=cnt, lane=lane,
                ya=ya, yc=yc, yp_pre=yp_pre, yp=yp)


def _halo_spec(tm, tile_of):
    per = tm // HALO
    return pl.BlockSpec((HALO, D_LOCAL), lambda i, *_: (jnp.maximum(tile_of(i) * per - 1, 0), 0))


def _heads_as_columns(ref):
    return jnp.concatenate([ref[h] for h in range(N_HEADS)], axis=1)


def mix_out_fwd(x, ox, pb, wao, convw, wco, wpool, pscale, wo_all, l):
    S = x.shape[0]
    tm = TM_MIX

    def body(l_ref, x_ref, o_ref, pb_ref, halo_ref, wao_ref, cw_ref, wco_ref, wp_ref, ps_ref, wo_ref, y_ref):
        i = pl.program_id(0)
        pb = pb_ref[...]
        o = _heads_as_columns(o_ref).astype(BF16)
        b = _local_branches(o, pb, halo_ref[...], i > 0, i * tm, wao_ref[...], cw_ref[...], wco_ref[...],
                            wp_ref[...], ps_ref[...])
        g0 = _sigmoid(pb[:, D_LOCAL:D_LOCAL + D_MODEL].astype(F32))
        g1 = _sigmoid(pb[:, D_LOCAL + D_MODEL:D_LOCAL + 2 * D_MODEL].astype(F32))
        g2 = _sigmoid(pb[:, D_LOCAL + 2 * D_MODEL:D_B].astype(F32))
        merged = g0 * b["ya"] + g1 * b["yc"] + g2 * b["yp"]
        y_ref[...] = x_ref[...] + _dot(merged.astype(BF16), wo_ref[...])

    return pl.pallas_call(
        body,
        grid_spec=pltpu.PrefetchScalarGridSpec(
            num_scalar_prefetch=1,
            grid=(S // tm,),
            in_specs=[_rows(tm, D_MODEL), pl.BlockSpec((N_HEADS, tm, LANES), lambda i, l: (0, i, 0)), _rows(tm, D_B),
                      _halo_spec(tm, lambda i: i), _whole(wao), _whole(convw), _whole(wco), _whole(wpool), _whole(pscale),
                      _layer((D_MODEL, D_MODEL))],
            out_specs=_rows(tm, D_MODEL),
        ),
        out_shape=jax.ShapeDtypeStruct((S, D_MODEL), F32),
        compiler_params=_params("parallel"),
        name="mix_out_fwd",
    )(_layer_index(l), x, ox, pb, pb, wao, convw, wco, wpool, pscale, wo_all)


def mix_out_bwd(dxm, ox, pb, wao, convw, wco, wpool, pscale, wo_all, l):
    S = dxm.shape[0]
    tm = TM_MIX
    nt = S // tm
    rev = lambda i: nt - 1 - i
    rows = lambda n: pl.BlockSpec((tm, n), lambda i, l: (rev(i), 0))
    heads = pl.BlockSpec((N_HEADS, tm, LANES), lambda i, l: (0, rev(i), 0))
    acc = lambda r, c: pl.BlockSpec((r, c), lambda i, l: (0, 0))

    def body(l_ref, dxm_ref, o_ref, pb_ref, halo_ref, wao_ref, cw_ref, wco_ref, wp_ref, ps_ref, wo_ref,
             dpb_ref, do_ref, dwo_ref, dwao_ref, dwco_ref, dwp_ref, dcw_ref, dps_ref, next_dconv, next_e):
        i = pl.program_id(0)
        r = rev(i)

        @pl.when(i == 0)
        def _():
            for ref in (dwo_ref, dwao_ref, dwco_ref, dwp_ref, dcw_ref, dps_ref, next_dconv, next_e):
                ref[...] = jnp.zeros_like(ref)

        pb = pb_ref[...]
        o = _heads_as_columns(o_ref).astype(BF16)
        cw = cw_ref[...]
        b = _local_branches(o, pb, halo_ref[...], r > 0, r * tm, wao_ref[...], cw, wco_ref[...], wp_ref[...], ps_ref[...])
        g0 = _sigmoid(pb[:, D_LOCAL:D_LOCAL + D_MODEL].astype(F32))
        g1 = _sigmoid(pb[:, D_LOCAL + D_MODEL:D_LOCAL + 2 * D_MODEL].astype(F32))
        g2 = _sigmoid(pb[:, D_LOCAL + 2 * D_MODEL:D_B].astype(F32))
        dxb = dxm_ref[...].astype(BF16)
        merged = g0 * b["ya"] + g1 * b["yc"] + g2 * b["yp"]
        dwo_ref[...] += _dot_tn(merged.astype(BF16), dxb)
        dmer = _dot_nt(dxb, wo_ref[...])
        dpb_ref[:, D_LOCAL:D_LOCAL + D_MODEL] = (dmer * b["ya"] * (g0 * (1.0 - g0))).astype(BF16)
        dpb_ref[:, D_LOCAL + D_MODEL:D_LOCAL + 2 * D_MODEL] = (dmer * b["yc"] * (g1 * (1.0 - g1))).astype(BF16)
        dpb_ref[:, D_LOCAL + 2 * D_MODEL:D_B] = (dmer * b["yp"] * (g2 * (1.0 - g2))).astype(BF16)

        dya = (dmer * g0).astype(BF16)
        dwao_ref[...] += _dot_tn(o, dya)
        da = _dot_nt(dya, wao_ref[...]).astype(BF16)
        lane = lax.broadcasted_iota(jnp.int32, (tm, LANES), 1)
        for h in range(N_HEADS):
            dah = da[:, h * LANES:(h + 1) * LANES]
            d1, d2, d3 = _pieces(-jnp.sum(dah.astype(F32) * o_ref[h], axis=1, keepdims=True))
            do_ref[h] = jnp.where(lane == LANE_C + 1, d1, jnp.where(lane == LANE_C + 2, d2,
                                                                  jnp.where(lane == LANE_C + 3, d3, dah)))

        dyc = (dmer * g1).astype(BF16)
        dwco_ref[...] += _dot_tn(b["cm"].astype(BF16), dyc)
        dcm = _dot_nt(dyc, wco_ref[...])
        dconv = dcm * b["cb"]
        dcw_ref[0:1, :] += jnp.sum(dconv * b["z2"], axis=0, keepdims=True)
        dcw_ref[1:2, :] += jnp.sum(dconv * b["z1"], axis=0, keepdims=True)
        dcw_ref[2:3, :] += jnp.sum(dconv * b["z"], axis=0, keepdims=True)
        d_ext = jnp.concatenate([dconv, next_dconv[...]], axis=0)
        dz = cw[2:3, :] * dconv + cw[1:2, :] * _shift_up(d_ext, 1, tm) + cw[0:1, :] * _shift_up(d_ext, 2, tm)
        next_dconv[...] = dconv[0:HALO]
        dpb_ref[:, 0:D_CONV] = (dz * b["cc"]).astype(BF16)
        dpb_ref[:, D_CONV:2 * D_CONV] = (dcm * b["conv"]).astype(BF16)
        dpb_ref[:, 2 * D_CONV:3 * D_CONV] = (dz * b["cx"]).astype(BF16)

        dyp = dmer * g2
        dps_ref[...] += jnp.sum(dyp * b["yp_pre"], axis=0, keepdims=True)
        dyps = (dyp * ps_ref[...]).astype(BF16)
        dwp_ref[...] += _dot_tn(b["feat"].astype(BF16), dyps)
        dfeat = _dot_nt(dyps, wp_ref[...])
        e = dfeat / b["cnt"]
        e_ext = jnp.concatenate([e, next_e[...]], axis=0)
        up = lambda a, k: pltpu.roll(a, tm + HALO - k, 0)
        f2 = e_ext + up(e_ext, 1)
        f4 = f2 + up(f2, 2)
        f8 = f4 + up(f4, 4)
        f16 = f8 + up(f8, 8)
        next_e[...] = e[0:HALO]
        dpb_ref[:, 3 * D_CONV:D_LOCAL] = (_pool_lane_select(f2[:tm], f4[:tm], f8[:tm], f16[:tm], b["lane"]) - dfeat).astype(BF16)

    return pl.pallas_call(
        body,
        grid_spec=pltpu.PrefetchScalarGridSpec(
            num_scalar_prefetch=1,
            grid=(nt,),
            in_specs=[rows(D_MODEL), heads, rows(D_B), _halo_spec(tm, rev),
                      _whole(wao), _whole(convw), _whole(wco), _whole(wpool), _whole(pscale), _layer((D_MODEL, D_MODEL))],
            out_specs=[rows(D_B), heads, acc(D_MODEL, D_MODEL), acc(D_HEADS, D_MODEL), acc(D_CONV, D_MODEL),
                       acc(D_POOL, D_MODEL), acc(8, D_CONV), acc(1, D_MODEL)],
            scratch_shapes=[pltpu.VMEM((HALO, D_CONV), F32), pltpu.VMEM((HALO, D_POOL), F32)],
        ),
        out_shape=[jax.ShapeDtypeStruct((S, D_B), BF16), jax.ShapeDtypeStruct((N_HEADS, S, LANES), BF16),
                   jax.ShapeDtypeStruct((D_MODEL, D_MODEL), F32), jax.ShapeDtypeStruct((D_HEADS, D_MODEL), F32),
                   jax.ShapeDtypeStruct((D_CONV, D_MODEL), F32), jax.ShapeDtypeStruct((D_POOL, D_MODEL), F32),
                   jax.ShapeDtypeStruct((8, D_CONV), F32), jax.ShapeDtypeStruct((1, D_MODEL), F32)],
        compiler_params=_params("arbitrary"),
        name="mix_out_bwd",
    )(_layer_index(l), dxm, ox, pb, pb, wao, convw, wco, wpool, pscale, wo_all)


FF_SHARD = 2 * D_FF // 4


def ffn_fwd(x, g, w1_all, w2_all, l):
    S = x.shape[0]

    def body(l_ref, x_ref, g_ref, w1_ref, w2_ref, y_ref, u_ref):
        xf = x_ref[...]
        r = lax.rsqrt(jnp.mean(xf * xf, axis=-1, keepdims=True) + EPS)
        h = (xf * r * g_ref[...]).astype(BF16)
        u = jnp.concatenate([_dot(h, w1_ref[j]) for j in range(4)], axis=1)
        u_ref[...] = u.astype(BF16)
        gt = u[:, 0:D_FF]
        act = gt * _sigmoid(gt) * u[:, D_FF:2 * D_FF]
        y_ref[...] = xf + _dot(act.astype(BF16), w2_ref[...])

    return pl.pallas_call(
        body,
        grid_spec=pltpu.PrefetchScalarGridSpec(
            num_scalar_prefetch=1,
            grid=(S // TM,),
            in_specs=[_rows(TM, D_MODEL), _whole(g), _layer((4, D_MODEL, FF_SHARD)), _layer((D_FF, D_MODEL))],
            out_specs=[_rows(TM, D_MODEL), _rows(TM, 2 * D_FF)],
        ),
        out_shape=[jax.ShapeDtypeStruct((S, D_MODEL), F32), jax.ShapeDtypeStruct((S, 2 * D_FF), BF16)],
        compiler_params=_params("parallel"),
        name="ffn_fwd",
    )(_layer_index(l), x, g, w1_all, w2_all)


def ffn_bwd(x, g, dy, u, w1_all, w2_all, l):
    S = x.shape[0]

    def body(l_ref, x_ref, g_ref, dy_ref, u_ref, w1_ref, w2_ref, dx_ref, du_ref, act_ref, h_ref, dg_ref):
        @pl.when(pl.program_id(0) == 0)
        def _():
            dg_ref[...] = jnp.zeros_like(dg_ref)

        dyf = dy_ref[...]
        dact = _dot_nt(dyf.astype(BF16), w2_ref[...])
        gt = u_ref[:, 0:D_FF].astype(F32)
        up = u_ref[:, D_FF:2 * D_FF].astype(F32)
        sg = _sigmoid(gt)
        silu = gt * sg
        act_ref[...] = (silu * up).astype(BF16)
        du_ref[:, 0:D_FF] = (dact * up * (sg * (1.0 + gt * (1.0 - sg)))).astype(BF16)
        du_ref[:, D_FF:2 * D_FF] = (dact * silu).astype(BF16)
        dh = _dot_nt(du_ref[:, 0:FF_SHARD], w1_ref[0])
        for j in range(1, 4):
            dh = dh + _dot_nt(du_ref[:, j * FF_SHARD:(j + 1) * FF_SHARD], w1_ref[j])
        xf = x_ref[...]
        r = lax.rsqrt(jnp.mean(xf * xf, axis=-1, keepdims=True) + EPS)
        xhat = xf * r
        h_ref[...] = (xhat * g_ref[...]).astype(BF16)
        dg_ref[...] += jnp.sum(dh * xhat, axis=0, keepdims=True)
        gdh = dh * g_ref[...]
        dx_ref[...] = dyf + r * (gdh - xhat * jnp.mean(xhat * gdh, axis=-1, keepdims=True))

    return pl.pallas_call(
        body,
        grid_spec=pltpu.PrefetchScalarGridSpec(
            num_scalar_prefetch=1,
            grid=(S // TM,),
            in_specs=[_rows(TM, D_MODEL), _whole(g), _rows(TM, D_MODEL), _rows(TM, 2 * D_FF),
                      _layer((4, D_MODEL, FF_SHARD)), _layer((D_FF, D_MODEL))],
            out_specs=[_rows(TM, D_MODEL), _rows(TM, 2 * D_FF), _rows(TM, D_FF), _rows(TM, D_MODEL),
                       pl.BlockSpec((1, D_MODEL), lambda i, l: (0, 0))],
        ),
        out_shape=[jax.ShapeDtypeStruct((S, D_MODEL), F32), jax.ShapeDtypeStruct((S, 2 * D_FF), BF16),
                   jax.ShapeDtypeStruct((S, D_FF), BF16), jax.ShapeDtypeStruct((S, D_MODEL), BF16),
                   jax.ShapeDtypeStruct((1, D_MODEL), F32)],
        compiler_params=_params("arbitrary"),
        name="ffn_bwd",
    )(_layer_index(l), x, g, dy, u, w1_all, w2_all)


def loss_head(y, target):
    S = y.shape[0]

    def body(y_ref, t_ref, loss_ref, dy_ref):
        @pl.when(pl.program_id(0) == 0)
        def _():
            loss_ref[0, 0] = 0.0

        err = y_ref[...] - t_ref[...]
        dy_ref[...] = err * (1.0 / D_MODEL)
        loss_ref[0, 0] += 0.5 * jnp.sum(jnp.mean(err * err, axis=-1))

    return pl.pallas_call(
        body,
        grid=(S // TM,),
        in_specs=[_rows(TM, D_MODEL), _rows(TM, D_MODEL)],
        out_specs=[pl.BlockSpec((1, 1), lambda i: (0, 0), memory_space=pltpu.SMEM), _rows(TM, D_MODEL)],
        out_shape=[jax.ShapeDtypeStruct((1, 1), F32), jax.ShapeDtypeStruct((S, D_MODEL), F32)],
        compiler_params=_params("arbitrary"),
        name="loss_head",
    )(y, target)


SPLIT = {
    "w_in": ((D_MODEL, D_IN), 1),
    "w_attn_out": ((D_ATTN, D_MODEL), 1),
    "w_conv_out": ((D_CONV, D_MODEL), 1),
    "pool_w": ((N_GROUPS, D_POOL // N_GROUPS, D_MODEL // N_GROUPS), 2),
    "w_o": ((D_MODEL, D_MODEL), 0),
    "w_ffn_in": ((D_MODEL, 2 * D_FF), 1),
    "w_ffn_out": ((D_FF, D_MODEL), 0),
}
SMALL = {"norm_mix_g": D_MODEL, "forget_b": N_HEADS, "q_norm_g": HEAD_DIM, "k_norm_g": HEAD_DIM,
         "pool_scale": D_MODEL, "norm_ffn_g": D_MODEL, "conv_w": CONV_K * D_CONV}
WEIGHTS = ["norm_mix_g", "w_in", "forget_b", "q_norm_g", "k_norm_g", "w_attn_out", "conv_w", "w_conv_out", "pool_w",
           "pool_scale", "w_o", "norm_ffn_g", "w_ffn_in", "w_ffn_out"]


def local_step(x, target, gathered, small):
    n_layers = small["norm_mix_g"].shape[0]
    w_o = gathered["w_o"].reshape(n_layers, D_MODEL, D_MODEL)
    w2 = gathered["w_ffn_out"].reshape(n_layers, D_FF, D_MODEL)
    w1 = gathered["w_ffn_in"]
    row = lambda a: a.astype(F32).reshape(1, -1)
    layers, saved = [], []
    for l in range(n_layers):
        wa, wf, wb = w_in_prep(gathered["w_in"], l)
        wao, wco, wpool, convw = branch_w_prep(gathered["w_attn_out"], gathered["w_conv_out"], gathered["pool_w"],
                                               gathered["conv_w"], l)
        w = dict(wa=wa, wf=wf, wb=wb, wao=wao, wco=wco, wpool=wpool, convw=convw,
                 g1=row(small["norm_mix_g"][l]), g2=row(small["norm_ffn_g"][l]), pscale=row(small["pool_scale"][l]),
                 gq=row(jnp.tile(small["q_norm_g"][l], N_HEADS)), gk=row(jnp.tile(small["k_norm_g"][l], N_HEADS)),
                 fb=row(jnp.pad(small["forget_b"][l], (0, D_F - N_HEADS))))
        layers.append(w)
        h, pa, pf, pb = in_proj_fwd(x, w["g1"], wa, wf, wb)
        qx, kx, vx = qk_prep(pa, pf, w["gq"], w["gk"], w["fb"])
        ox, qxb = attn_fwd(qx, kx, vx)
        xm = mix_out_fwd(x, ox, pb, wao, convw, wco, wpool, w["pscale"], w_o, l)
        y, u = ffn_fwd(xm, w["g2"], w1, w2, l)
        saved.append(dict(x=x, h=h, pa=pa, pf=pf, pb=pb, kx=kx, vx=vx, ox=ox, qxb=qxb, xm=xm, u=u))
        x = y
    loss, dx = loss_head(x, target)
    big = {n: lax.empty((n_layers, N_CHIPS) + gathered[n].shape[2:], F32) for n in SPLIT}
    sm = {n: [None] * n_layers for n in SMALL}
    for l in reversed(range(n_layers)):
        w, s = layers[l], saved[l]
        dxm, du, act, h2, dg2 = ffn_bwd(s["xm"], w["g2"], dx, s["u"], w1, w2, l)
        big["w_ffn_out"] = wgrad_into(act, dx.astype(BF16), "wgrad_ffn_out", big["w_ffn_out"].reshape(n_layers, D_FF, D_MODEL),
                                      l).reshape(big["w_ffn_out"].shape)
        big["w_ffn_in"] = wgrad_into(h2, du, "wgrad_ffn_in", big["w_ffn_in"], l)
        dpb, dox, dwo, dwao, dwco, dwpool, dconvw, dpscale = mix_out_bwd(
            dxm, s["ox"], s["pb"], w["wao"], w["convw"], w["wco"], w["wpool"], w["pscale"], w_o, l)
        dqx, dkx, dvx = attn_bwd(s["qxb"], s["kx"], s["vx"], dox)
        dpa, dpf, dgq, dgk, dfb = attn_bwd_post(s["pa"], s["pf"], dqx, dkx, dvx, w["gq"], w["gk"], w["fb"])
        dx, dg1 = in_proj_bwd(s["x"], w["g1"], dxm, dpa, dpf, dpb, w["wa"], w["wf"], w["wb"])
        big["w_in"] = w_in_unprep(wgrad(s["h"], dpa, "wgrad_in_qkv"), wgrad(s["h"], dpf, "wgrad_in_f"),
                                  wgrad(s["h"], dpb, "wgrad_in_b"), big["w_in"], l)
        big["w_attn_out"], big["w_conv_out"], big["pool_w"], big["w_o"] = branch_g_place(
            dwao, dwco, dwpool, dwo, (big["w_attn_out"], big["w_conv_out"], big["pool_w"], big["w_o"]), l)
        sm["norm_mix_g"][l] = dg1[0]
        sm["forget_b"][l] = dfb[0, 0:N_HEADS]
        sm["q_norm_g"][l] = dgq.reshape(N_HEADS, HEAD_DIM).sum(0)
        sm["k_norm_g"][l] = dgk.reshape(N_HEADS, HEAD_DIM).sum(0)
        sm["pool_scale"][l] = dpscale[0]
        sm["norm_ffn_g"][l] = dg2[0]
        sm["conv_w"][l] = dconvw[0:CONV_K].reshape(-1)
    return loss, dx, big, {n: jnp.stack(v) for n, v in sm.items()}


def adamw(w, g, m, v, name):
    R, C = w.shape
    tm = 256 if R % 256 == 0 else R

    def body(w_ref, g_ref, m_ref, v_ref, d_ref, nm_ref, nv_ref):
        gr = g_ref[...]
        m_new = ADAM_B1 * m_ref[...] + (1.0 - ADAM_B1) * gr
        v_new = ADAM_B2 * v_ref[...] + (1.0 - ADAM_B2) * jnp.square(gr)
        nm_ref[...] = m_new
        nv_ref[...] = v_new
        m_hat = m_new / (1.0 - ADAM_B1 ** ADAM_STEP)
        v_hat = v_new / (1.0 - ADAM_B2 ** ADAM_STEP)
        d_ref[...] = -ADAM_LR * (m_hat / (jnp.sqrt(v_hat) + ADAM_EPS) + ADAM_WD * w_ref[...])

    spec = _rows(tm, C)
    out = jax.ShapeDtypeStruct((R, C), F32)
    return pl.pallas_call(
        body,
        grid=(R // tm,),
        in_specs=[spec] * 4,
        out_specs=[spec] * 3,
        out_shape=[out] * 3,
        compiler_params=_params("parallel"),
        name=name,
    )(w, g, m, v)


MESH = pl.DeviceIdType.MESH
HBM_REF = pl.BlockSpec(memory_space=pl.ANY)
N_CHIPS = 4
N_DEV = 8
HALF = DEPTH // 2
SMALL_SHAPE = (128, LANES)


def _mesh_pos():
    return lax.axis_index("x"), lax.axis_index("y"), lax.axis_index("c")


def _other_chips(x, y):
    return [(1 - x, y), (x, 1 - y), (1 - x, 1 - y)]


def _remote(src, dst, send_sem, recv_sem, to):
    return pltpu.make_async_remote_copy(src_ref=src, dst_ref=dst, send_sem=send_sem, recv_sem=recv_sem,
                                        device_id=to, device_id_type=MESH)


def _row_tile(rows):
    for tm in (256, 176):
        if rows % tm == 0:
            return tm
    return rows


def place_shard(chip, w, dtype, name):
    shape = w.shape
    w3 = w.reshape(shape[0], -1, shape[-1])
    _, R, C = w3.shape
    tm = _row_tile(R)

    def body(chip_ref, w_ref, o_ref):
        o_ref[...] = w_ref[...].astype(dtype)

    out = pl.pallas_call(
        body,
        grid_spec=pltpu.PrefetchScalarGridSpec(
            num_scalar_prefetch=1,
            grid=(DEPTH, R // tm),
            in_specs=[pl.BlockSpec((None, tm, C), lambda l, i, chip: (l, i, 0))],
            out_specs=pl.BlockSpec((None, None, tm, C), lambda l, i, chip: (l, chip[0], i, 0)),
        ),
        out_shape=jax.ShapeDtypeStruct((DEPTH, N_CHIPS, R, C), dtype),
        compiler_params=_params("parallel", "parallel"),
        name=name,
    )(chip, w3)
    return out.reshape((DEPTH, N_CHIPS) + shape[1:])


def gather_shards(bufs):
    n = len(bufs)

    def body(*refs):
        outs = refs[n:2 * n]
        send_sems, recv_sems = refs[2 * n:]
        x, y, c = _mesh_pos()
        me = 2 * x + y
        sibling = (x, y, 1 - c)
        chips = _other_chips(x, y)
        mine, theirs = pl.ds(HALF * c, HALF), pl.ds(HALF * (1 - c), HALF)
        sent = []
        for t in range(n):
            for j, chip in enumerate(chips):
                cp = _remote(outs[t].at[mine, me], outs[t].at[mine, me], send_sems.at[t, j], recv_sems.at[t, j], (*chip, c))
                cp.start()
                sent.append(cp)
        for j, (px, py) in enumerate(chips):
            k = 2 * px + py
            for t in range(n):
                _remote(outs[t].at[mine, k], outs[t].at[mine, k], send_sems.at[t, j], recv_sems.at[t, j], (px, py, c)).wait_recv()
                cp = _remote(outs[t].at[mine, k], outs[t].at[mine, k], send_sems.at[t, 3 + j], recv_sems.at[t, 3 + j], sibling)
                cp.start()
                sent.append(cp)
        for j, (px, py) in enumerate(chips):
            k = 2 * px + py
            for t in range(n):
                _remote(outs[t].at[theirs, k], outs[t].at[theirs, k], send_sems.at[t, 3 + j], recv_sems.at[t, 3 + j],
                        sibling).wait_recv()
        for cp in sent:
            cp.wait_send()

    return pl.pallas_call(
        body,
        in_specs=[HBM_REF] * n,
        out_specs=[HBM_REF] * n,
        out_shape=[jax.ShapeDtypeStruct(b.shape, b.dtype) for b in bufs],
        input_output_aliases={t: t for t in range(n)},
        scratch_shapes=[pltpu.SemaphoreType.DMA((n, 6)), pltpu.SemaphoreType.DMA((n, 6))],
        name="gather_shards",
    )(*bufs)


def pair_exchange(grads):
    n = len(grads)

    def body(*refs):
        ins, outs = refs[:n], refs[n:2 * n]
        send_sems, recv_sems = refs[2 * n:]
        x, y, c = _mesh_pos()
        cps = [_remote(ins[t].at[pl.ds(HALF * (1 - c), HALF)], outs[t], send_sems.at[t], recv_sems.at[t], (x, y, 1 - c))
               for t in range(n)]
        for cp in cps:
            cp.start()
        for cp in cps:
            cp.wait()

    return pl.pallas_call(
        body,
        in_specs=[HBM_REF] * n,
        out_specs=[HBM_REF] * n,
        out_shape=[jax.ShapeDtypeStruct((HALF,) + g.shape[1:], g.dtype) for g in grads],
        scratch_shapes=[pltpu.SemaphoreType.DMA((n,)), pltpu.SemaphoreType.DMA((n,))],
        name="pair_exchange",
    )(*grads)


def pair_sum(core, g, t, name, tm=256):
    _, R, C = t.shape

    def body(c_ref, g_ref, t_ref, o_ref):
        o_ref[...] = (g_ref[...] + t_ref[...]).astype(BF16)

    tile = pl.BlockSpec((None, tm, C), lambda a, i, c: (a, i, 0))
    return pl.pallas_call(
        body,
        grid_spec=pltpu.PrefetchScalarGridSpec(
            num_scalar_prefetch=1,
            grid=(HALF, R // tm),
            in_specs=[pl.BlockSpec((None, tm, C), lambda a, i, c: (HALF * c[0] + a, i, 0)), tile],
            out_specs=tile,
        ),
        out_shape=jax.ShapeDtypeStruct(t.shape, BF16),
        compiler_params=_params("parallel", "parallel"),
        name=name,
    )(core, g, t)


def chip_exchange(sums):
    n = len(sums)

    def body(*refs):
        ins, outs = refs[:n], refs[n:2 * n]
        send_sems, recv_sems = refs[2 * n:]
        x, y, c = _mesh_pos()
        chips = _other_chips(x, y)
        layers = pl.ds(0, HALF)
        sent = []
        for t in range(n):
            for j, (px, py) in enumerate(chips):
                cp = _remote(ins[t].at[layers, 2 * px + py], outs[t].at[j], send_sems.at[t, j], recv_sems.at[t, j], (px, py, c))
                cp.start()
                sent.append(cp)
        for cp in sent:
            cp.wait()

    return pl.pallas_call(
        body,
        in_specs=[HBM_REF] * n,
        out_specs=[HBM_REF] * n,
        out_shape=[jax.ShapeDtypeStruct((N_CHIPS - 1, HALF) + s.shape[2:], s.dtype) for s in sums],
        scratch_shapes=[pltpu.SemaphoreType.DMA((n, 3)), pltpu.SemaphoreType.DMA((n, 3))],
        name="chip_exchange",
    )(*sums)


def chip_sum(core, chip, recv, sums, name):
    _, _, R, C = recv.shape
    tm = _row_tile(R)

    def body(c_ref, k_ref, r_ref, s_ref, o_ref):
        acc = s_ref[...].astype(F32)
        for j in range(N_CHIPS - 1):
            acc = acc + r_ref[j].astype(F32)
        o_ref[...] = acc

    return pl.pallas_call(
        body,
        grid_spec=pltpu.PrefetchScalarGridSpec(
            num_scalar_prefetch=2,
            grid=(HALF, R // tm),
            in_specs=[pl.BlockSpec((N_CHIPS - 1, None, tm, C), lambda a, i, c, k: (0, a, i, 0)),
                      pl.BlockSpec((None, None, tm, C), lambda a, i, c, k: (a, k[0], i, 0))],
            out_specs=pl.BlockSpec((None, tm, C), lambda a, i, c, k: (HALF * c[0] + a, i, 0)),
        ),
        out_shape=jax.ShapeDtypeStruct((DEPTH, R, C), F32),
        compiler_params=_params("parallel", "parallel"),
        name=name,
    )(core, chip, recv, sums)


def sibling_share(totals):
    n = len(totals)

    def body(*refs):
        outs = refs[n:2 * n]
        send_sems, recv_sems = refs[2 * n:]
        x, y, c = _mesh_pos()
        mine, theirs = pl.ds(HALF * c, HALF), pl.ds(HALF * (1 - c), HALF)
        sent = [_remote(outs[t].at[mine], outs[t].at[mine], send_sems.at[t], recv_sems.at[t], (x, y, 1 - c)) for t in range(n)]
        for cp in sent:
            cp.start()
        for t in range(n):
            _remote(outs[t].at[theirs], outs[t].at[theirs], send_sems.at[t], recv_sems.at[t], (x, y, 1 - c)).wait_recv()
        for cp in sent:
            cp.wait_send()

    return pl.pallas_call(
        body,
        in_specs=[HBM_REF] * n,
        out_specs=[HBM_REF] * n,
        out_shape=[jax.ShapeDtypeStruct(t.shape, t.dtype) for t in totals],
        input_output_aliases={t: t for t in range(n)},
        scratch_shapes=[pltpu.SemaphoreType.DMA((n,)), pltpu.SemaphoreType.DMA((n,))],
        name="sibling_share",
    )(*totals)


def small_allgather(small):
    def body(s_ref, a_ref, send_sems, recv_sems, local_sem):
        x, y, c = _mesh_pos()
        me = 4 * x + 2 * y + c
        own = pltpu.make_async_copy(s_ref, a_ref.at[me], local_sem)
        own.start()
        sent = []
        for k in range(1, N_DEV):
            peer = (x ^ (k >> 2), y ^ ((k >> 1) & 1), c ^ (k & 1))
            cp = _remote(s_ref, a_ref.at[me], send_sems.at[k - 1], recv_sems.at[k - 1], peer)
            cp.start()
            sent.append(cp)
        for k in range(1, N_DEV):
            px, py, pc = x ^ (k >> 2), y ^ ((k >> 1) & 1), c ^ (k & 1)
            _remote(s_ref, a_ref.at[4 * px + 2 * py + pc], send_sems.at[k - 1], recv_sems.at[k - 1], (px, py, pc)).wait_recv()
        for cp in sent:
            cp.wait_send()
        own.wait()

    return pl.pallas_call(
        body,
        in_specs=[HBM_REF],
        out_specs=HBM_REF,
        out_shape=jax.ShapeDtypeStruct((N_DEV,) + SMALL_SHAPE, small.dtype),
        scratch_shapes=[pltpu.SemaphoreType.DMA((N_DEV - 1,)), pltpu.SemaphoreType.DMA((N_DEV - 1,)), pltpu.SemaphoreType.DMA],
        name="small_allgather",
    )(small)


def small_sum(blocks):
    def body(a_ref, o_ref):
        acc = a_ref[0]
        for d in range(1, N_DEV):
            acc = acc + a_ref[d]
        o_ref[...] = acc

    return pl.pallas_call(
        body,
        in_specs=[pl.BlockSpec(memory_space=pltpu.VMEM)],
        out_specs=pl.BlockSpec(memory_space=pltpu.VMEM),
        out_shape=jax.ShapeDtypeStruct(SMALL_SHAPE, F32),
        name="small_sum",
    )(blocks)


def pack_small(grads, loss):
    flat = jnp.concatenate([grads[n].reshape(-1) for n in SMALL] + [loss.reshape(-1)])
    size = SMALL_SHAPE[0] * SMALL_SHAPE[1]
    return jnp.pad(flat, (0, size - flat.shape[0])).reshape(SMALL_SHAPE)


def unpack_small(packed):
    flat = packed.reshape(-1)
    out, off = {}, 0
    for n, size in SMALL.items():
        out[n] = flat[off:off + DEPTH * size].reshape(DEPTH, size)
        off += DEPTH * size
    return out, flat[off]


def kernel(x, norm_mix_g, w_in, forget_b, q_norm_g, k_norm_g, w_attn_out, conv_w, w_conv_out, pool_w, pool_scale, w_o, norm_ffn_g, w_ffn_in, w_ffn_out, loss_target, m_norm_mix_g, m_w_in, m_forget_b, m_q_norm_g, m_k_norm_g, m_w_attn_out, m_conv_w, m_w_conv_out, m_pool_w, m_pool_scale, m_w_o, m_norm_ffn_g, m_w_ffn_in, m_w_ffn_out, v_norm_mix_g, v_w_in, v_forget_b, v_q_norm_g, v_k_norm_g, v_w_attn_out, v_conv_w, v_w_conv_out, v_pool_w, v_pool_scale, v_w_o, v_norm_ffn_g, v_w_ffn_in, v_w_ffn_out):
    given = dict(locals())
    weights = {n: given[n] for n in WEIGHTS}

    core = lax.axis_index("c").astype(jnp.int32)
    chip = (2 * lax.axis_index("x") + lax.axis_index("y")).astype(jnp.int32)

    names = list(SPLIT) + ["conv_w"]
    placed = [place_shard(chip.reshape(1), weights[n], F32 if n == "conv_w" else BF16, "place_" + n) for n in names]
    gathered = dict(zip(names, gather_shards(placed)))
    small = {n: weights[n] for n in SMALL if n != "conv_w"}

    loss, dx, big, small_grads = local_step(x[0], loss_target[0], gathered, small)

    grads = [big[n] for n in SPLIT]
    theirs = pair_exchange(grads)
    rows3 = lambda a, lead: a.reshape(a.shape[:lead] + (-1, a.shape[-1]))
    sums = [pair_sum(core.reshape(1), rows3(g, 1), rows3(t, 1), "pair_sum_" + n).reshape(t.shape)
            for n, g, t in zip(SPLIT, grads, theirs)]
    recv = chip_exchange(sums)
    totals = [chip_sum(core.reshape(1), chip.reshape(1), rows3(r, 2), rows3(s, 2), "chip_sum_" + n).reshape((DEPTH,) + r.shape[2:])
              for n, r, s in zip(SPLIT, recv, sums)]
    reduced = dict(zip(SPLIT, sibling_share(totals)))

    small_total, loss_sum = unpack_small(small_sum(small_allgather(pack_small(small_grads, loss))))
    cols = D_CONV // N_CHIPS
    small_total["conv_w"] = lax.dynamic_slice_in_dim(small_total["conv_w"].reshape(DEPTH, CONV_K, D_CONV), chip * cols, cols, axis=2)
    for n in SMALL:
        reduced[n] = small_total[n].reshape(weights[n].shape)

    deltas, new_m, new_v = {}, {}, {}
    for n in WEIGHTS:
        w = weights[n]
        flat = (-1, w.shape[-1])
        d, nm, nv = adamw(w.reshape(flat), reduced[n].reshape(flat), given["m_" + n].reshape(flat),
                          given["v_" + n].reshape(flat), "adamw_" + n)
        deltas[n], new_m[n], new_v[n] = d.reshape(w.shape), nm.reshape(w.shape), nv.reshape(w.shape)
    return (loss_sum, dx[None], *[reduced[n] for n in WEIGHTS], *[deltas[n] for n in WEIGHTS],
            *[new_m[n] for n in WEIGHTS], *[new_v[n] for n in WEIGHTS])
```

```python
import functools

import numpy as np
import jax
import jax.numpy as jnp
from jax import lax
from jax.experimental import pallas as pl
from jax.experimental.pallas import tpu as pltpu

F32 = jnp.float32
BF16 = jnp.bfloat16

D_MODEL = 1024
DEPTH = 4
HEAD_DIM = 64
N_HEADS = 8
D_ATTN = 512
D_CONV = 256
D_POOL = 256
D_FF = 2816
D_IN = 5640
CONV_K = 3
POOL_WINDOWS = (2, 4, 8, 16)
N_GROUPS = len(POOL_WINDOWS)
EPS = 1e-6
ADAM_LR, ADAM_B1, ADAM_B2, ADAM_EPS, ADAM_WD, ADAM_STEP = 0.001, 0.9, 0.999, 1e-08, 0.01, 10

D_QKV = 3 * D_ATTN
D_F = 128
D_B = 3 * D_CONV + D_POOL + 3 * D_MODEL
D_LOCAL = 3 * D_CONV + D_POOL

LANES = 128
D_HEADS = N_HEADS * LANES
HALO = 16
VMEM_LIMIT = 56 * 1024 * 1024
NEG = -1e30
LOG2E = 1.4426950408889634
LN2 = 0.6931471805599453

TM = 256
TM_MIX = 128
TQ = 512
TK = 1024
KB = 512
QT = 1024

LANE_C = 64
LANE_ONE = 67
LANE_LSE = 70
N_PIECES = 3


def _dot(a, b):
    return jnp.dot(a, b, preferred_element_type=F32)


def _dot_nt(a, b):
    return lax.dot_general(a, b, (((1,), (1,)), ((), ())), preferred_element_type=F32)


def _dot_tn(a, b):
    return lax.dot_general(a, b, (((0,), (0,)), ((), ())), preferred_element_type=F32)


def _params(*sem):
    return pltpu.CompilerParams(dimension_semantics=sem, vmem_limit_bytes=VMEM_LIMIT)


def _rows(tm, n):
    return pl.BlockSpec((tm, n), lambda i, *_: (i, 0))


def _whole(a):
    nd = a.ndim
    return pl.BlockSpec(a.shape, lambda *_: (0,) * nd)


def _layer(shape):
    nd = len(shape)
    return pl.BlockSpec((None,) + tuple(shape), lambda *a: (a[-1][0],) + (0,) * nd)


def _layer_index(l):
    return jnp.full((1,), l, jnp.int32)


def _split_bf16(x):
    hi = x.astype(BF16)
    lo = (x - hi.astype(F32)).astype(BF16)
    return hi, lo


def _pieces(x):
    p1 = x.astype(BF16)
    r1 = x - p1.astype(F32)
    p2 = r1.astype(BF16)
    p3 = (r1 - p2.astype(F32)).astype(BF16)
    return p1, p2, p3


def _sigmoid(x):
    return 1.0 / (1.0 + jnp.exp(-x))


def w_in_prep(win, l):
    tr = 256
    n = D_IN // 4
    v_rest = D_QKV - n
    b0 = v_rest + N_HEADS

    def body(s0, s1, s2, s3, wa_ref, wf_ref, wb_ref):
        b = s1[...]
        wa_ref[...] = jnp.concatenate([s0[...], b[:, 0:v_rest]], axis=1)
        wf_ref[...] = jnp.concatenate([b[:, v_rest:b0], jnp.zeros((tr, D_F - N_HEADS), b.dtype)], axis=1)
        wb_ref[...] = jnp.concatenate([b[:, b0:n], s2[...], s3[...]], axis=1)

    shard = lambda j: pl.BlockSpec((None, None, tr, n), lambda i: (l, j, i, 0))
    return pl.pallas_call(
        body,
        grid=(D_MODEL // tr,),
        in_specs=[shard(0), shard(1), shard(2), shard(3)],
        out_specs=[_rows(tr, D_QKV), _rows(tr, D_F), _rows(tr, D_B)],
        out_shape=[jax.ShapeDtypeStruct((D_MODEL, D_QKV), win.dtype), jax.ShapeDtypeStruct((D_MODEL, D_F), win.dtype),
                   jax.ShapeDtypeStruct((D_MODEL, D_B), win.dtype)],
        compiler_params=_params("parallel"),
        name="w_in_prep",
    )(win, win, win, win)


def _into_layer(buf):
    return dict(in_spec=HBM_REF, out_shape=jax.ShapeDtypeStruct(buf.shape, buf.dtype), aliases={1: 0})


def w_in_unprep(dwa, dwf, dwb, buf, l):
    tr = 256
    n = D_IN // 4
    v_rest = D_QKV - n
    b1 = n - v_rest - N_HEADS
    place = _into_layer(buf)

    def body(l_ref, buf_ref, a_ref, f_ref, b_ref, o_ref):
        a = a_ref[...]
        b = b_ref[...]
        o_ref[0] = a[:, 0:n]
        o_ref[1] = jnp.concatenate([a[:, n:D_QKV], f_ref[:, 0:N_HEADS], b[:, 0:b1]], axis=1)
        o_ref[2] = b[:, b1:b1 + n]
        o_ref[3] = b[:, b1 + n:D_B]

    return pl.pallas_call(
        body,
        grid_spec=pltpu.PrefetchScalarGridSpec(
            num_scalar_prefetch=1,
            grid=(D_MODEL // tr,),
            in_specs=[place["in_spec"], _rows(tr, D_QKV), _rows(tr, D_F), _rows(tr, D_B)],
            out_specs=pl.BlockSpec((None, 4, tr, n), lambda i, l: (l[0], 0, i, 0)),
        ),
        out_shape=place["out_shape"],
        input_output_aliases=place["aliases"],
        compiler_params=_params("parallel"),
        name="w_in_unprep",
    )(_layer_index(l), buf, dwa, dwf, dwb)


def branch_w_prep(wao, wco, pw, cw, l):
    gd = D_POOL // N_GROUPS
    od = D_MODEL // N_GROUPS

    def body(wao_ref, wco_ref, pw_ref, cw_ref, ao_ref, co_ref, po_ref, co8_ref):
        a = jnp.concatenate([wao_ref[j] for j in range(4)], axis=1)
        gap = jnp.zeros((LANES - HEAD_DIM, D_MODEL), a.dtype)
        ao_ref[...] = jnp.concatenate(
            [blk for h in range(N_HEADS) for blk in (a[h * HEAD_DIM:(h + 1) * HEAD_DIM], gap)], axis=0)
        co_ref[...] = jnp.concatenate([wco_ref[j] for j in range(4)], axis=1)
        zero = jnp.zeros((gd, od), a.dtype)
        po_ref[...] = jnp.concatenate(
            [jnp.concatenate([jnp.concatenate([pw_ref[j, g] for j in range(4)], axis=1) if g2 == g else zero
                              for g2 in range(N_GROUPS)], axis=1) for g in range(N_GROUPS)], axis=0)
        co8_ref[...] = jnp.zeros_like(co8_ref)
        co8_ref[0:CONV_K, :] = jnp.concatenate([cw_ref[j] for j in range(4)], axis=1)

    sel = lambda *shape: pl.BlockSpec((None,) + shape, lambda i: (l,) + (0,) * len(shape))
    return pl.pallas_call(
        body,
        grid=(1,),
        in_specs=[sel(4, D_ATTN, D_MODEL // 4), sel(4, D_CONV, D_MODEL // 4), sel(4, N_GROUPS, gd, od // 4),
                  sel(4, CONV_K, D_CONV // 4)],
        out_specs=[pl.BlockSpec((D_HEADS, D_MODEL), lambda i: (0, 0)), pl.BlockSpec((D_CONV, D_MODEL), lambda i: (0, 0)),
                   pl.BlockSpec((D_POOL, D_MODEL), lambda i: (0, 0)), pl.BlockSpec((8, D_CONV), lambda i: (0, 0))],
        out_shape=[jax.ShapeDtypeStruct((D_HEADS, D_MODEL), BF16), jax.ShapeDtypeStruct((D_CONV, D_MODEL), BF16),
                   jax.ShapeDtypeStruct((D_POOL, D_MODEL), BF16), jax.ShapeDtypeStruct((8, D_CONV), F32)],
        name="branch_w_prep",
    )(wao, wco, pw, cw)


def branch_g_place(dwao, dwco, dwpool, dwo, bufs, l):
    gd = D_POOL // N_GROUPS
    od = D_MODEL // N_GROUPS
    q = D_MODEL // 4

    def body(l_ref, b0, b1, b2, b3, a_ref, c_ref, p_ref, w_ref, ao_ref, co_ref, po_ref, wo_ref):
        a = jnp.concatenate([a_ref[h * LANES:h * LANES + HEAD_DIM, :] for h in range(N_HEADS)], axis=0)
        c = c_ref[...]
        p = p_ref[...]
        for j in range(4):
            ao_ref[j] = a[:, j * q:(j + 1) * q]
            co_ref[j] = c[:, j * q:(j + 1) * q]
            wo_ref[j] = w_ref[j * q:(j + 1) * q, :]
            for g in range(N_GROUPS):
                c0 = g * od + j * (od // 4)
                po_ref[j, g] = p[g * gd:(g + 1) * gd, c0:c0 + od // 4]

    whole = lambda a: pl.BlockSpec(a.shape, lambda i, l: (0,) * a.ndim)
    layer = lambda b: pl.BlockSpec((None,) + b.shape[1:], lambda i, l: (l[0],) + (0,) * (b.ndim - 1))
    return pl.pallas_call(
        body,
        grid_spec=pltpu.PrefetchScalarGridSpec(
            num_scalar_prefetch=1,
            grid=(1,),
            in_specs=[HBM_REF] * 4 + [whole(dwao), whole(dwco), whole(dwpool), whole(dwo)],
            out_specs=[layer(b) for b in bufs],
        ),
        out_shape=[jax.ShapeDtypeStruct(b.shape, b.dtype) for b in bufs],
        input_output_aliases={1: 0, 2: 1, 3: 2, 4: 3},
        compiler_params=_params("arbitrary"),
        name="branch_g_place",
    )(_layer_index(l), *bufs, dwao, dwco, dwpool, dwo)


def in_proj_fwd(x, g, wa, wf, wb):
    S = x.shape[0]

    def body(x_ref, g_ref, wa_ref, wf_ref, wb_ref, h_ref, pa_ref, pf_ref, pb_ref):
        xf = x_ref[...]
        r = lax.rsqrt(jnp.mean(xf * xf, axis=-1, keepdims=True) + EPS)
        h = (xf * r * g_ref[...]).astype(BF16)
        h_ref[...] = h
        pa_ref[...] = _dot(h, wa_ref[...]).astype(BF16)
        pf_ref[...] = _dot(h, wf_ref[...])
        pb_ref[...] = _dot(h, wb_ref[...]).astype(BF16)

    return pl.pallas_call(
        body,
        grid=(S // TM,),
        in_specs=[_rows(TM, D_MODEL), _whole(g), _whole(wa), _whole(wf), _whole(wb)],
        out_specs=[_rows(TM, D_MODEL), _rows(TM, D_QKV), _rows(TM, D_F), _rows(TM, D_B)],
        out_shape=[
            jax.ShapeDtypeStruct((S, D_MODEL), BF16),
            jax.ShapeDtypeStruct((S, D_QKV), BF16),
            jax.ShapeDtypeStruct((S, D_F), F32),
            jax.ShapeDtypeStruct((S, D_B), BF16),
        ],
        compiler_params=_params("parallel"),
        name="in_proj_fwd",
    )(x, g, wa, wf, wb)


def in_proj_bwd(x, g, dxm, dpa, dpf, dpb, wa, wf, wb):
    S = x.shape[0]

    def body(x_ref, g_ref, dxm_ref, dpa_ref, dpf_ref, dpb_ref, wa_ref, wf_ref, wb_ref, dx_ref, dg_ref):
        @pl.when(pl.program_id(0) == 0)
        def _():
            dg_ref[...] = jnp.zeros_like(dg_ref)

        dh = _dot_nt(dpa_ref[...], wa_ref[...]) + _dot_nt(dpf_ref[...], wf_ref[...]) + _dot_nt(dpb_ref[...], wb_ref[...])
        xf = x_ref[...]
        r = lax.rsqrt(jnp.mean(xf * xf, axis=-1, keepdims=True) + EPS)
        xhat = xf * r
        dg_ref[...] += jnp.sum(dh * xhat, axis=0, keepdims=True)
        gdh = dh * g_ref[...]
        dx_ref[...] = dxm_ref[...] + r * (gdh - xhat * jnp.mean(xhat * gdh, axis=-1, keepdims=True))

    return pl.pallas_call(
        body,
        grid=(S // TM,),
        in_specs=[_rows(TM, D_MODEL), _whole(g), _rows(TM, D_MODEL), _rows(TM, D_QKV), _rows(TM, D_F), _rows(TM, D_B),
                  _whole(wa), _whole(wf), _whole(wb)],
        out_specs=[_rows(TM, D_MODEL), pl.BlockSpec((1, D_MODEL), lambda i: (0, 0))],
        out_shape=[jax.ShapeDtypeStruct((S, D_MODEL), F32), jax.ShapeDtypeStruct((1, D_MODEL), F32)],
        compiler_params=_params("arbitrary"),
        name="in_proj_bwd",
    )(x, g, dxm, dpa, dpf, dpb, wa, wf, wb)


def wgrad(xa, dy, name, n_split=1, ts=512):
    S, K = xa.shape
    N = dy.shape[1]
    tn = N // n_split if n_split > 1 else min(512, N)
    n_steps = S // ts

    def body(x_ref, dy_ref, o_ref):
        @pl.when(pl.program_id(1) == 0)
        def _():
            o_ref[...] = jnp.zeros_like(o_ref)

        o_ref[...] += _dot_tn(x_ref[...], dy_ref[...])

    if n_split > 1:
        out_spec = pl.BlockSpec((None, K, tn), lambda j, k: (j, 0, 0))
        out_shape = jax.ShapeDtypeStruct((n_split, K, tn), F32)
    else:
        out_spec = pl.BlockSpec((K, tn), lambda j, k: (0, j))
        out_shape = jax.ShapeDtypeStruct((K, N), F32)
    return pl.pallas_call(
        body,
        grid=(N // tn, n_steps),
        in_specs=[pl.BlockSpec((ts, K), lambda j, k: (k, 0)), pl.BlockSpec((ts, tn), lambda j, k: (k, j))],
        out_specs=out_spec,
        out_shape=out_shape,
        compiler_params=_params("parallel", "arbitrary"),
        name=name,
    )(xa, dy)


def wgrad_into(xa, dy, name, buf, l, ts=512):
    S, K = xa.shape
    N = dy.shape[1]
    split = buf.ndim == 4
    tn = buf.shape[-1] if split else min(512, N)
    place = _into_layer(buf)

    def body(l_ref, buf_ref, x_ref, dy_ref, o_ref):
        @pl.when(pl.program_id(1) == 0)
        def _():
            o_ref[...] = jnp.zeros_like(o_ref)

        o_ref[...] += _dot_tn(x_ref[...], dy_ref[...])

    if split:
        out_spec = pl.BlockSpec((None, None, K, tn), lambda j, k, l: (l[0], j, 0, 0))
    else:
        out_spec = pl.BlockSpec((None, K, tn), lambda j, k, l: (l[0], 0, j))
    return pl.pallas_call(
        body,
        grid_spec=pltpu.PrefetchScalarGridSpec(
            num_scalar_prefetch=1,
            grid=(N // tn, S // ts),
            in_specs=[place["in_spec"], pl.BlockSpec((ts, K), lambda j, k, l: (k, 0)),
                      pl.BlockSpec((ts, tn), lambda j, k, l: (k, j))],
            out_specs=out_spec,
        ),
        out_shape=place["out_shape"],
        input_output_aliases=place["aliases"],
        compiler_params=_params("parallel", "arbitrary"),
        name=name,
    )(_layer_index(l), buf, xa, dy)


def _head_mean_matrix():
    h = np.arange(D_ATTN) // HEAD_DIM
    return jnp.asarray((h[:, None] == h[None, :]).astype(np.float32) / HEAD_DIM, BF16)


def _place_matrix(lane0):
    m = np.zeros((N_PIECES * LANES, D_HEADS), np.float32)
    for i in range(N_PIECES):
        for h in range(N_HEADS):
            m[i * LANES + h, h * LANES + lane0 + i] = 1.0
    return jnp.asarray(m, BF16)


def _tri(n, upper):
    r = np.arange(n)
    m = (r[None, :] >= r[:, None]) if upper else (r[None, :] <= r[:, None])
    return jnp.asarray(m.astype(np.float32), BF16)


def _lanes_in(lane, lo, n):
    return (lane >= lo) & (lane < lo + n)


def qk_prep(pa, pf, gq, gk, fb):
    S = pa.shape[0]
    bd = _head_mean_matrix()
    tri = _tri(TM, upper=False)
    place_q = _place_matrix(LANE_C)
    place_k = _place_matrix(LANE_ONE)

    def body(q_ref, k_ref, v_ref, pf_ref, gq_ref, gk_ref, fb_ref, bd_ref, tri_ref, pq_ref, pk_ref,
             qx_ref, kx_ref, vx_ref, carry):
        @pl.when(pl.program_id(0) == 0)
        def _():
            carry[...] = jnp.zeros_like(carry)

        def head_norm(x_ref, g_ref, scale):
            xf = x_ref[...].astype(F32)
            ms = _dot((xf * xf).astype(BF16), bd_ref[...])
            return xf * lax.rsqrt(ms + EPS) * g_ref[...] * scale

        qh = head_norm(q_ref, gq_ref, HEAD_DIM ** -0.5 * LOG2E)
        kh = head_norm(k_ref, gk_ref, 1.0)
        vf = v_ref[...].astype(F32)

        z = pf_ref[...] + fb_ref[...]
        logf = jnp.minimum(z, 0.0) - jnp.log(1.0 + jnp.exp(-jnp.abs(z)))
        hi, lo = _split_bf16(logf)
        c = _dot(tri_ref[...], hi) + _dot(tri_ref[...], lo) + carry[...]
        carry[...] += jnp.sum(hi.astype(F32) + lo.astype(F32), axis=0, keepdims=True)
        pieces = jnp.concatenate(_pieces(c * LOG2E), axis=1)
        cq = _dot(pieces, pq_ref[...])
        ck = _dot(pieces, pk_ref[...])

        lane = lax.broadcasted_iota(jnp.int32, (TM, LANES), 1)
        low = lane < HEAD_DIM
        ones_q = _lanes_in(lane, LANE_ONE, N_PIECES).astype(F32)
        ones_k = (_lanes_in(lane, LANE_C, N_PIECES) | _lanes_in(lane, LANE_LSE, N_PIECES)).astype(F32)
        ones_v = _lanes_in(lane, LANE_C, N_PIECES + 1).astype(F32)
        for h in range(N_HEADS):
            blk = slice((h // 2) * LANES, (h // 2 + 1) * LANES)
            head = (lambda a: pltpu.roll(a[:, blk], HEAD_DIM, 1)) if h % 2 else (lambda a: a[:, blk])
            mine = slice(h * LANES, (h + 1) * LANES)
            qx_ref[h] = jnp.where(low, head(qh), cq[:, mine] + ones_q).astype(BF16)
            kx_ref[h] = jnp.where(low, head(kh), ones_k - ck[:, mine]).astype(BF16)
            vx_ref[h] = jnp.where(low, head(vf), ones_v).astype(BF16)

    heads = pl.BlockSpec((N_HEADS, TM, LANES), lambda i: (0, i, 0))
    out = jax.ShapeDtypeStruct((N_HEADS, S, LANES), BF16)
    return pl.pallas_call(
        body,
        grid=(S // TM,),
        in_specs=[pl.BlockSpec((TM, D_ATTN), lambda i: (i, 0)), pl.BlockSpec((TM, D_ATTN), lambda i: (i, 1)),
                  pl.BlockSpec((TM, D_ATTN), lambda i: (i, 2)),
                  _rows(TM, D_F), _whole(gq), _whole(gk), _whole(fb), _whole(bd), _whole(tri), _whole(place_q), _whole(place_k)],
        out_specs=[heads, heads, heads],
        out_shape=[out, out, out],
        scratch_shapes=[pltpu.VMEM((1, D_F), F32)],
        compiler_params=_params("arbitrary"),
        name="qk_prep",
    )(pa, pa, pa, pf, gq, gk, fb, bd, tri, place_q, place_k)


def attn_fwd(qx, kx, vx):
    S = qx.shape[1]
    nq = S // TQ

    def body(q_ref, k_ref, v_ref, o_ref, qb_ref):
        i = pl.program_id(1)
        lane = lax.broadcasted_iota(jnp.int32, (TQ, LANES), 1)
        row = lax.broadcasted_iota(jnp.int32, (TQ, TK), 0)
        col = lax.broadcasted_iota(jnp.int32, (TQ, TK), 1)
        q = [q_ref[0], q_ref[1]]
        n_full = (i * TQ) // TK

        def step(kt, carry, masked):
            ks = pl.multiple_of(kt * TK, TK)
            new = []
            for j in range(2):
                m, acc = carry[j]
                z = _dot_nt(q[j], k_ref[j, pl.ds(ks, TK), :])
                if masked:
                    z = jnp.where(row + i * TQ >= col + ks, z, NEG)
                m_new = jnp.maximum(m, jnp.max(z, axis=1, keepdims=True))
                pr = jnp.exp2(z - m_new)
                acc = jnp.exp2(m - m_new) * acc + _dot(pr.astype(BF16), v_ref[j, pl.ds(ks, TK), :])
                new.append((m_new, acc))
            return tuple(new)

        init = tuple((jnp.full((TQ, 1), NEG, F32), jnp.zeros((TQ, LANES), F32)) for _ in range(2))
        carry = lax.fori_loop(0, n_full, functools.partial(step, masked=False), init)
        carry = step(n_full, carry, True)
        for j in range(2):
            m, acc = carry[j]
            l = jnp.sum(jnp.where(lane == LANE_C, acc, 0.0), axis=1, keepdims=True)
            o_ref[j] = acc / l
            n1, n2, n3 = _pieces(-(m + jnp.log(l) * LOG2E))
            qb_ref[j] = jnp.where(lane == LANE_LSE, n1,
                                  jnp.where(lane == LANE_LSE + 1, n2, jnp.where(lane == LANE_LSE + 2, n3, q[j])))

    pair_tile = pl.BlockSpec((2, TQ, LANES), lambda p, i: (p, i, 0))
    pair_all = pl.BlockSpec((2, S, LANES), lambda p, i: (p, 0, 0))
    return pl.pallas_call(
        body,
        grid=(N_HEADS // 2, nq),
        in_specs=[pair_tile, pair_all, pair_all],
        out_specs=[pair_tile, pair_tile],
        out_shape=[jax.ShapeDtypeStruct((N_HEADS, S, LANES), F32), jax.ShapeDtypeStruct((N_HEADS, S, LANES), BF16)],
        compiler_params=_params("parallel", "parallel"),
        name="attn_fwd",
    )(qx, kx, vx)


def attn_bwd(qxb, kx, vx, dox):
    S = qxb.shape[1]
    nq = S // QT

    def body(q_ref, k_ref, v_ref, do_ref, dq_ref, dk_ref, dv_ref):
        kt = pl.program_id(1)

        @pl.when(kt == 0)
        def _():
            dq_ref[...] = jnp.zeros_like(dq_ref)

        row = lax.broadcasted_iota(jnp.int32, (QT, KB), 0)
        col = lax.broadcasted_iota(jnp.int32, (QT, KB), 1)
        k = [k_ref[0], k_ref[1]]
        v = [v_ref[0], v_ref[1]]
        q_first = (kt * KB) // QT

        def step(qi, carry, masked):
            qs = pl.multiple_of(qi * QT, QT)
            new = []
            for j in range(2):
                dk, dv = carry[j]
                q = q_ref[j, pl.ds(qs, QT), :]
                dout = do_ref[j, pl.ds(qs, QT), :]
                z = _dot_nt(q, k[j])
                if masked:
                    z = jnp.where(row + qs >= col + kt * KB, z, NEG)
                pr = jnp.exp2(z)
                dv = dv + _dot_tn(pr.astype(BF16), dout)
                dsb = (pr * _dot_nt(dout, v[j])).astype(BF16)
                dk = dk + _dot_tn(dsb, q)
                dq_ref[j, pl.ds(qs, QT), :] += _dot(dsb, k[j])
                new.append((dk, dv))
            return tuple(new)

        zero = jnp.zeros((KB, LANES), F32)
        carry = step(q_first, ((zero, zero), (zero, zero)), True)
        carry = lax.fori_loop(q_first + 1, nq, functools.partial(step, masked=False), carry)
        for j in range(2):
            dk_ref[j] = carry[j][0]
            dv_ref[j] = carry[j][1]

    pair_tile = pl.BlockSpec((2, KB, LANES), lambda p, kt: (p, kt, 0))
    pair_all = pl.BlockSpec((2, S, LANES), lambda p, kt: (p, 0, 0))
    out = jax.ShapeDtypeStruct((N_HEADS, S, LANES), F32)
    return pl.pallas_call(
        body,
        grid=(N_HEADS // 2, S // KB),
        in_specs=[pair_all, pair_tile, pair_tile, pair_all],
        out_specs=[pair_all, pair_tile, pair_tile],
        out_shape=[out, out, out],
        compiler_params=_params("arbitrary", "arbitrary"),
        name="attn_bwd",
    )(qxb, kx, vx, dox)


def attn_bwd_post(pa, pf, dqx, dkx, dvx, gq, gk, fb):
    S = pa.shape[0]
    nt = S // TM
    bd = _head_mean_matrix()
    triu = _tri(TM, upper=True)
    rev = lambda i: nt - 1 - i

    def body(q_ref, k_ref, pf_ref, dqx_ref, dkx_ref, dvx_ref, gq_ref, gk_ref, fb_ref, bd_ref, triu_ref,
             dpa_ref, dpf_ref, dgq_ref, dgk_ref, dfb_ref, carry):
        @pl.when(pl.program_id(0) == 0)
        def _():
            carry[...] = jnp.zeros_like(carry)
            dgq_ref[...] = jnp.zeros_like(dgq_ref)
            dgk_ref[...] = jnp.zeros_like(dgk_ref)
            dfb_ref[...] = jnp.zeros_like(dfb_ref)

        lane = lax.broadcasted_iota(jnp.int32, (TM, LANES), 1)

        def columns(ref):
            return jnp.concatenate([jnp.where(lane < HEAD_DIM, ref[2 * p], pltpu.roll(ref[2 * p + 1], HEAD_DIM, 1))
                                    for p in range(N_HEADS // 2)], axis=1)

        def head_norm_bwd(x_ref, dy, g_ref, dg_ref):
            xf = x_ref[...].astype(F32)
            r = lax.rsqrt(_dot((xf * xf).astype(BF16), bd_ref[...]) + EPS)
            xhat = xf * r
            dg_ref[...] += jnp.sum(dy * xhat, axis=0, keepdims=True)
            gdy = dy * g_ref[...]
            return (r * (gdy - xhat * _dot((xhat * gdy).astype(BF16), bd_ref[...]))).astype(BF16)

        dpa_ref[:, 0:D_ATTN] = head_norm_bwd(q_ref, columns(dqx_ref) * HEAD_DIM ** -0.5, gq_ref, dgq_ref)
        dpa_ref[:, D_ATTN:2 * D_ATTN] = head_norm_bwd(k_ref, columns(dkx_ref) * LN2, gk_ref, dgk_ref)
        dpa_ref[:, 2 * D_ATTN:3 * D_ATTN] = columns(dvx_ref).astype(BF16)

        dc = jnp.zeros((TM, LANES), F32)
        for h in range(N_HEADS):
            both = jnp.where(lane == LANE_C, dqx_ref[h], 0.0) - jnp.where(lane == LANE_ONE, dkx_ref[h], 0.0)
            dc = jnp.where(lane == h, jnp.sum(both, axis=1, keepdims=True), dc)
        hi, lo = _split_bf16(dc)
        dlogf = _dot(triu_ref[...], hi) + _dot(triu_ref[...], lo) + carry[...]
        first = lax.broadcasted_iota(jnp.int32, (TM, D_F), 0) == 0
        carry[...] = jnp.sum(jnp.where(first, dlogf, 0.0), axis=0, keepdims=True)
        df = dlogf * _sigmoid(-(pf_ref[...] + fb_ref[...]))
        dfb_ref[...] += jnp.sum(df, axis=0, keepdims=True)
        dpf_ref[...] = df.astype(BF16)

    heads = pl.BlockSpec((N_HEADS, TM, LANES), lambda i: (0, rev(i), 0))
    return pl.pallas_call(
        body,
        grid=(nt,),
        in_specs=[pl.BlockSpec((TM, D_ATTN), lambda i: (rev(i), 0)), pl.BlockSpec((TM, D_ATTN), lambda i: (rev(i), 1)),
                  pl.BlockSpec((TM, D_F), lambda i: (rev(i), 0)), heads, heads, heads,
                  _whole(gq), _whole(gk), _whole(fb), _whole(bd), _whole(triu)],
        out_specs=[pl.BlockSpec((TM, D_QKV), lambda i: (rev(i), 0)), pl.BlockSpec((TM, D_F), lambda i: (rev(i), 0)),
                   pl.BlockSpec((1, D_ATTN), lambda i: (0, 0)), pl.BlockSpec((1, D_ATTN), lambda i: (0, 0)),
                   pl.BlockSpec((1, D_F), lambda i: (0, 0))],
        out_shape=[jax.ShapeDtypeStruct((S, D_QKV), BF16), jax.ShapeDtypeStruct((S, D_F), BF16),
                   jax.ShapeDtypeStruct((1, D_ATTN), F32), jax.ShapeDtypeStruct((1, D_ATTN), F32),
                   jax.ShapeDtypeStruct((1, D_F), F32)],
        scratch_shapes=[pltpu.VMEM((1, D_F), F32)],
        compiler_params=_params("arbitrary"),
        name="attn_bwd_post",
    )(pa, pa, pf, dqx, dkx, dvx, gq, gk, fb, bd, triu)


def _shift_down(ext, k):
    return pltpu.roll(ext, k, 0)[HALO:]


def _shift_up(ext, k, n):
    return pltpu.roll(ext, n + HALO - k, 0)[:n]


def _pool_lane_select(a2, a4, a8, a16, lane):
    return jnp.where(lane < 64, a2, jnp.where(lane < 128, a4, jnp.where(lane < 192, a8, a16)))


def _local_branches(o, pb, halo, have_prev, row0, wao, convw, wco, wpool, pscale):
    n = pb.shape[0]
    cx = pb[:, 0:D_CONV].astype(F32)
    cb = pb[:, D_CONV:2 * D_CONV].astype(F32)
    cc = pb[:, 2 * D_CONV:3 * D_CONV].astype(F32)
    px = pb[:, 3 * D_CONV:D_LOCAL].astype(F32)
    keep = have_prev.astype(F32)
    z = cc * cx
    z_ext = jnp.concatenate([halo[:, 2 * D_CONV:3 * D_CONV].astype(F32) * halo[:, 0:D_CONV].astype(F32) * keep, z], axis=0)
    z1 = _shift_down(z_ext, 1)
    z2 = _shift_down(z_ext, 2)
    conv = convw[0:1, :] * z2 + convw[1:2, :] * z1 + convw[2:3, :] * z
    cm = cb * conv

    u_ext = jnp.concatenate([halo[:, 3 * D_CONV:D_LOCAL].astype(F32) * keep, px], axis=0)
    s2 = u_ext + pltpu.roll(u_ext, 1, 0)
    s4 = s2 + pltpu.roll(s2, 2, 0)
    s8 = s4 + pltpu.roll(s4, 4, 0)
    s16 = s8 + pltpu.roll(s8, 8, 0)
    lane = lax.broadcasted_iota(jnp.int32, (n, D_POOL), 1)
    win = _pool_lane_select(2.0, 4.0, 8.0, 16.0, lane)
    t = (row0 + lax.broadcasted_iota(jnp.int32, (n, D_POOL), 0)).astype(F32)
    cnt = jnp.minimum(t + 1.0, win)
    feat = _pool_lane_select(s2[HALO:], s4[HALO:], s8[HALO:], s16[HALO:], lane) / cnt - px

    ya = _dot(o, wao)
    yc = _dot(cm.astype(BF16), wco)
    yp_pre = _dot(feat.astype(BF16), wpool)
    yp = yp_pre * pscale
    return dict(cx=cx, cb=cb, cc=cc, z=z, z1=z1, z2=z2, conv=conv, cm=cm, feat=feat, cnt=cnt, lane=lane,
                ya=ya, yc=yc, yp_pre=yp_pre, yp=yp)


def _halo_spec(tm, tile_of):
    per = tm // HALO
    return pl.BlockSpec((HALO, D_LOCAL), lambda i, *_: (jnp.maximum(tile_of(i) * per - 1, 0), 0))


def _heads_as_columns(ref):
    return jnp.concatenate([ref[h] for h in range(N_HEADS)], axis=1)


def mix_out_fwd(x, ox, pb, wao, convw, wco, wpool, pscale, wo_all, l):
    S = x.shape[0]
    tm = TM_MIX

    def body(l_ref, x_ref, o_ref, pb_ref, halo_ref, wao_ref, cw_ref, wco_ref, wp_ref, ps_ref, wo_ref, y_ref):
        i = pl.program_id(0)
        pb = pb_ref[...]
        o = _heads_as_columns(o_ref).astype(BF16)
        b = _local_branches(o, pb, halo_ref[...], i > 0, i * tm, wao_ref[...], cw_ref[...], wco_ref[...],
                            wp_ref[...], ps_ref[...])
        g0 = _sigmoid(pb[:, D_LOCAL:D_LOCAL + D_MODEL].astype(F32))
        g1 = _sigmoid(pb[:, D_LOCAL + D_MODEL:D_LOCAL + 2 * D_MODEL].astype(F32))
        g2 = _sigmoid(pb[:, D_LOCAL + 2 * D_MODEL:D_B].astype(F32))
        merged = g0 * b["ya"] + g1 * b["yc"] + g2 * b["yp"]
        y_ref[...] = x_ref[...] + _dot(merged.astype(BF16), wo_ref[...])

    return pl.pallas_call(
        body,
        grid_spec=pltpu.PrefetchScalarGridSpec(
            num_scalar_prefetch=1,
            grid=(S // tm,),
            in_specs=[_rows(tm, D_MODEL), pl.BlockSpec((N_HEADS, tm, LANES), lambda i, l: (0, i, 0)), _rows(tm, D_B),
                      _halo_spec(tm, lambda i: i), _whole(wao), _whole(convw), _whole(wco), _whole(wpool), _whole(pscale),
                      _layer((D_MODEL, D_MODEL))],
            out_specs=_rows(tm, D_MODEL),
        ),
        out_shape=jax.ShapeDtypeStruct((S, D_MODEL), F32),
        compiler_params=_params("parallel"),
        name="mix_out_fwd",
    )(_layer_index(l), x, ox, pb, pb, wao, convw, wco, wpool, pscale, wo_all)


def mix_out_bwd(dxm, ox, pb, wao, convw, wco, wpool, pscale, wo_all, l):
    S = dxm.shape[0]
    tm = TM_MIX
    nt = S // tm
    rev = lambda i: nt - 1 - i
    rows = lambda n: pl.BlockSpec((tm, n), lambda i, l: (rev(i), 0))
    heads = pl.BlockSpec((N_HEADS, tm, LANES), lambda i, l: (0, rev(i), 0))
    acc = lambda r, c: pl.BlockSpec((r, c), lambda i, l: (0, 0))

    def body(l_ref, dxm_ref, o_ref, pb_ref, halo_ref, wao_ref, cw_ref, wco_ref, wp_ref, ps_ref, wo_ref,
             dpb_ref, do_ref, dwo_ref, dwao_ref, dwco_ref, dwp_ref, dcw_ref, dps_ref, next_dconv, next_e):
        i = pl.program_id(0)
        r = rev(i)

        @pl.when(i == 0)
        def _():
            for ref in (dwo_ref, dwao_ref, dwco_ref, dwp_ref, dcw_ref, dps_ref, next_dconv, next_e):
                ref[...] = jnp.zeros_like(ref)

        pb = pb_ref[...]
        o = _heads_as_columns(o_ref).astype(BF16)
        cw = cw_ref[...]
        b = _local_branches(o, pb, halo_ref[...], r > 0, r * tm, wao_ref[...], cw, wco_ref[...], wp_ref[...], ps_ref[...])
        g0 = _sigmoid(pb[:, D_LOCAL:D_LOCAL + D_MODEL].astype(F32))
        g1 = _sigmoid(pb[:, D_LOCAL + D_MODEL:D_LOCAL + 2 * D_MODEL].astype(F32))
        g2 = _sigmoid(pb[:, D_LOCAL + 2 * D_MODEL:D_B].astype(F32))
        dxb = dxm_ref[...].astype(BF16)
        merged = g0 * b["ya"] + g1 * b["yc"] + g2 * b["yp"]
        dwo_ref[...] += _dot_tn(merged.astype(BF16), dxb)
        dmer = _dot_nt(dxb, wo_ref[...])
        dpb_ref[:, D_LOCAL:D_LOCAL + D_MODEL] = (dmer * b["ya"] * (g0 * (1.0 - g0))).astype(BF16)
        dpb_ref[:, D_LOCAL + D_MODEL:D_LOCAL + 2 * D_MODEL] = (dmer * b["yc"] * (g1 * (1.0 - g1))).astype(BF16)
        dpb_ref[:, D_LOCAL + 2 * D_MODEL:D_B] = (dmer * b["yp"] * (g2 * (1.0 - g2))).astype(BF16)

        dya = (dmer * g0).astype(BF16)
        dwao_ref[...] += _dot_tn(o, dya)
        da = _dot_nt(dya, wao_ref[...]).astype(BF16)
        lane = lax.broadcasted_iota(jnp.int32, (tm, LANES), 1)
        for h in range(N_HEADS):
            dah = da[:, h * LANES:(h + 1) * LANES]
            d1, d2, d3 = _pieces(-jnp.sum(dah.astype(F32) * o_ref[h], axis=1, keepdims=True))
            do_ref[h] = jnp.where(lane == LANE_C + 1, d1, jnp.where(lane == LANE_C + 2, d2,
                                                                  jnp.where(lane == LANE_C + 3, d3, dah)))

        dyc = (dmer * g1).astype(BF16)
        dwco_ref[...] += _dot_tn(b["cm"].astype(BF16), dyc)
        dcm = _dot_nt(dyc, wco_ref[...])
        dconv = dcm * b["cb"]
        dcw_ref[0:1, :] += jnp.sum(dconv * b["z2"], axis=0, keepdims=True)
        dcw_ref[1:2, :] += jnp.sum(dconv * b["z1"], axis=0, keepdims=True)
        dcw_ref[2:3, :] += jnp.sum(dconv * b["z"], axis=0, keepdims=True)
        d_ext = jnp.concatenate([dconv, next_dconv[...]], axis=0)
        dz = cw[2:3, :] * dconv + cw[1:2, :] * _shift_up(d_ext, 1, tm) + cw[0:1, :] * _shift_up(d_ext, 2, tm)
        next_dconv[...] = dconv[0:HALO]
        dpb_ref[:, 0:D_CONV] = (dz * b["cc"]).astype(BF16)
        dpb_ref[:, D_CONV:2 * D_CONV] = (dcm * b["conv"]).astype(BF16)
        dpb_ref[:, 2 * D_CONV:3 * D_CONV] = (dz * b["cx"]).astype(BF16)

        dyp = dmer * g2
        dps_ref[...] += jnp.sum(dyp * b["yp_pre"], axis=0, keepdims=True)
        dyps = (dyp * ps_ref[...]).astype(BF16)
        dwp_ref[...] += _dot_tn(b["feat"].astype(BF16), dyps)
        dfeat = _dot_nt(dyps, wp_ref[...])
        e = dfeat / b["cnt"]
        e_ext = jnp.concatenate([e, next_e[...]], axis=0)
        up = lambda a, k: pltpu.roll(a, tm + HALO - k, 0)
        f2 = e_ext + up(e_ext, 1)
        f4 = f2 + up(f2, 2)
        f8 = f4 + up(f4, 4)
        f16 = f8 + up(f8, 8)
        next_e[...] = e[0:HALO]
        dpb_ref[:, 3 * D_CONV:D_LOCAL] = (_pool_lane_select(f2[:tm], f4[:tm], f8[:tm], f16[:tm], b["lane"]) - dfeat).astype(BF16)

    return pl.pallas_call(
        body,
        grid_spec=pltpu.PrefetchScalarGridSpec(
            num_scalar_prefetch=1,
            grid=(nt,),
            in_specs=[rows(D_MODEL), heads, rows(D_B), _halo_spec(tm, rev),
                      _whole(wao), _whole(convw), _whole(wco), _whole(wpool), _whole(pscale), _layer((D_MODEL, D_MODEL))],
            out_specs=[rows(D_B), heads, acc(D_MODEL, D_MODEL), acc(D_HEADS, D_MODEL), acc(D_CONV, D_MODEL),
                       acc(D_POOL, D_MODEL), acc(8, D_CONV), acc(1, D_MODEL)],
            scratch_shapes=[pltpu.VMEM((HALO, D_CONV), F32), pltpu.VMEM((HALO, D_POOL), F32)],
        ),
        out_shape=[jax.ShapeDtypeStruct((S, D_B), BF16), jax.ShapeDtypeStruct((N_HEADS, S, LANES), BF16),
                   jax.ShapeDtypeStruct((D_MODEL, D_MODEL), F32), jax.ShapeDtypeStruct((D_HEADS, D_MODEL), F32),
                   jax.ShapeDtypeStruct((D_CONV, D_MODEL), F32), jax.ShapeDtypeStruct((D_POOL, D_MODEL), F32),
                   jax.ShapeDtypeStruct((8, D_CONV), F32), jax.ShapeDtypeStruct((1, D_MODEL), F32)],
        compiler_params=_params("arbitrary"),
        name="mix_out_bwd",
    )(_layer_index(l), dxm, ox, pb, pb, wao, convw, wco, wpool, pscale, wo_all)


FF_SHARD = 2 * D_FF // 4


def ffn_fwd(x, g, w1_all, w2_all, l):
    S = x.shape[0]

    def body(l_ref, x_ref, g_ref, w1_ref, w2_ref, y_ref, u_ref):
        xf = x_ref[...]
        r = lax.rsqrt(jnp.mean(xf * xf, axis=-1, keepdims=True) + EPS)
        h = (xf * r * g_ref[...]).astype(BF16)
        u = jnp.concatenate([_dot(h, w1_ref[j]) for j in range(4)], axis=1)
        u_ref[...] = u.astype(BF16)
        gt = u[:, 0:D_FF]
        act = gt * _sigmoid(gt) * u[:, D_FF:2 * D_FF]
        y_ref[...] = xf + _dot(act.astype(BF16), w2_ref[...])

    return pl.pallas_call(
        body,
        grid_spec=pltpu.PrefetchScalarGridSpec(
            num_scalar_prefetch=1,
            grid=(S // TM,),
            in_specs=[_rows(TM, D_MODEL), _whole(g), _layer((4, D_MODEL, FF_SHARD)), _layer((D_FF, D_MODEL))],
            out_specs=[_rows(TM, D_MODEL), _rows(TM, 2 * D_FF)],
        ),
        out_shape=[jax.ShapeDtypeStruct((S, D_MODEL), F32), jax.ShapeDtypeStruct((S, 2 * D_FF), BF16)],
        compiler_params=_params("parallel"),
        name="ffn_fwd",
    )(_layer_index(l), x, g, w1_all, w2_all)


def ffn_bwd(x, g, dy, u, w1_all, w2_all, l):
    S = x.shape[0]

    def body(l_ref, x_ref, g_ref, dy_ref, u_ref, w1_ref, w2_ref, dx_ref, du_ref, act_ref, h_ref, dg_ref):
        @pl.when(pl.program_id(0) == 0)
        def _():
            dg_ref[...] = jnp.zeros_like(dg_ref)

        dyf = dy_ref[...]
        dact = _dot_nt(dyf.astype(BF16), w2_ref[...])
        gt = u_ref[:, 0:D_FF].astype(F32)
        up = u_ref[:, D_FF:2 * D_FF].astype(F32)
        sg = _sigmoid(gt)
        silu = gt * sg
        act_ref[...] = (silu * up).astype(BF16)
        du_ref[:, 0:D_FF] = (dact * up * (sg * (1.0 + gt * (1.0 - sg)))).astype(BF16)
        du_ref[:, D_FF:2 * D_FF] = (dact * silu).astype(BF16)
        dh = _dot_nt(du_ref[:, 0:FF_SHARD], w1_ref[0])
        for j in range(1, 4):
            dh = dh + _dot_nt(du_ref[:, j * FF_SHARD:(j + 1) * FF_SHARD], w1_ref[j])
        xf = x_ref[...]
        r = lax.rsqrt(jnp.mean(xf * xf, axis=-1, keepdims=True) + EPS)
        xhat = xf * r
        h_ref[...] = (xhat * g_ref[...]).astype(BF16)
        dg_ref[...] += jnp.sum(dh * xhat, axis=0, keepdims=True)
        gdh = dh * g_ref[...]
        dx_ref[...] = dyf + r * (gdh - xhat * jnp.mean(xhat * gdh, axis=-1, keepdims=True))

    return pl.pallas_call(
        body,
        grid_spec=pltpu.PrefetchScalarGridSpec(
            num_scalar_prefetch=1,
            grid=(S // TM,),
            in_specs=[_rows(TM, D_MODEL), _whole(g), _rows(TM, D_MODEL), _rows(TM, 2 * D_FF),
                      _layer((4, D_MODEL, FF_SHARD)), _layer((D_FF, D_MODEL))],
            out_specs=[_rows(TM, D_MODEL), _rows(TM, 2 * D_FF), _rows(TM, D_FF), _rows(TM, D_MODEL),
                       pl.BlockSpec((1, D_MODEL), lambda i, l: (0, 0))],
        ),
        out_shape=[jax.ShapeDtypeStruct((S, D_MODEL), F32), jax.ShapeDtypeStruct((S, 2 * D_FF), BF16),
                   jax.ShapeDtypeStruct((S, D_FF), BF16), jax.ShapeDtypeStruct((S, D_MODEL), BF16),
                   jax.ShapeDtypeStruct((1, D_MODEL), F32)],
        compiler_params=_params("arbitrary"),
        name="ffn_bwd",
    )(_layer_index(l), x, g, dy, u, w1_all, w2_all)


def loss_head(y, target):
    S = y.shape[0]

    def body(y_ref, t_ref, loss_ref, dy_ref):
        @pl.when(pl.program_id(0) == 0)
        def _():
            loss_ref[0, 0] = 0.0

        err = y_ref[...] - t_ref[...]
        dy_ref[...] = err * (1.0 / D_MODEL)
        loss_ref[0, 0] += 0.5 * jnp.sum(jnp.mean(err * err, axis=-1))

    return pl.pallas_call(
        body,
        grid=(S // TM,),
        in_specs=[_rows(TM, D_MODEL), _rows(TM, D_MODEL)],
        out_specs=[pl.BlockSpec((1, 1), lambda i: (0, 0), memory_space=pltpu.SMEM), _rows(TM, D_MODEL)],
        out_shape=[jax.ShapeDtypeStruct((1, 1), F32), jax.ShapeDtypeStruct((S, D_MODEL), F32)],
        compiler_params=_params("arbitrary"),
        name="loss_head",
    )(y, target)


SPLIT = {
    "w_in": ((D_MODEL, D_IN), 1),
    "w_attn_out": ((D_ATTN, D_MODEL), 1),
    "w_conv_out": ((D_CONV, D_MODEL), 1),
    "pool_w": ((N_GROUPS, D_POOL // N_GROUPS, D_MODEL // N_GROUPS), 2),
    "w_o": ((D_MODEL, D_MODEL), 0),
    "w_ffn_in": ((D_MODEL, 2 * D_FF), 1),
    "w_ffn_out": ((D_FF, D_MODEL), 0),
}
SMALL = {"norm_mix_g": D_MODEL, "forget_b": N_HEADS, "q_norm_g": HEAD_DIM, "k_norm_g": HEAD_DIM,
         "pool_scale": D_MODEL, "norm_ffn_g": D_MODEL, "conv_w": CONV_K * D_CONV}
WEIGHTS = ["norm_mix_g", "w_in", "forget_b", "q_norm_g", "k_norm_g", "w_attn_out", "conv_w", "w_conv_out", "pool_w",
           "pool_scale", "w_o", "norm_ffn_g", "w_ffn_in", "w_ffn_out"]


def layer_fwd(x, full, li, small):
    n = full["w_o"].shape[0]
    row = lambda a: a.astype(F32).reshape(1, -1)
    wa, wf, wb = w_in_prep(full["w_in"], li)
    wao, wco, wpool, convw = branch_w_prep(full["w_attn_out"], full["w_conv_out"], full["pool_w"], full["conv_w"], li)
    w = dict(wa=wa, wf=wf, wb=wb, wao=wao, wco=wco, wpool=wpool, convw=convw,
             w_o=full["w_o"].reshape(n, D_MODEL, D_MODEL), w1=full["w_ffn_in"], w2=full["w_ffn_out"].reshape(n, D_FF, D_MODEL),
             g1=row(small["norm_mix_g"]), g2=row(small["norm_ffn_g"]), pscale=row(small["pool_scale"]),
             gq=row(jnp.tile(small["q_norm_g"], N_HEADS)), gk=row(jnp.tile(small["k_norm_g"], N_HEADS)),
             fb=row(jnp.pad(small["forget_b"], (0, D_F - N_HEADS))))
    h, pa, pf, pb = in_proj_fwd(x, w["g1"], wa, wf, wb)
    qx, kx, vx = qk_prep(pa, pf, w["gq"], w["gk"], w["fb"])
    ox, qxb = attn_fwd(qx, kx, vx)
    xm = mix_out_fwd(x, ox, pb, wao, convw, wco, wpool, w["pscale"], w["w_o"], li)
    y, u = ffn_fwd(xm, w["g2"], w["w1"], w["w2"], li)
    return y, w, dict(x=x, h=h, pa=pa, pf=pf, pb=pb, kx=kx, vx=vx, ox=ox, qxb=qxb, xm=xm, u=u)


def layer_bwd(dx, w, s, li, big):
    n = big["w_o"].shape[0]
    big = dict(big)
    dxm, du, act, h2, dg2 = ffn_bwd(s["xm"], w["g2"], dx, s["u"], w["w1"], w["w2"], li)
    big["w_ffn_out"] = wgrad_into(act, dx.astype(BF16), "wgrad_ffn_out", big["w_ffn_out"].reshape(n, D_FF, D_MODEL),
                                  li).reshape(big["w_ffn_out"].shape)
    big["w_ffn_in"] = wgrad_into(h2, du, "wgrad_ffn_in", big["w_ffn_in"], li)
    dpb, dox, dwo, dwao, dwco, dwpool, dconvw, dpscale = mix_out_bwd(
        dxm, s["ox"], s["pb"], w["wao"], w["convw"], w["wco"], w["wpool"], w["pscale"], w["w_o"], li)
    dqx, dkx, dvx = attn_bwd(s["qxb"], s["kx"], s["vx"], dox)
    dpa, dpf, dgq, dgk, dfb = attn_bwd_post(s["pa"], s["pf"], dqx, dkx, dvx, w["gq"], w["gk"], w["fb"])
    dx, dg1 = in_proj_bwd(s["x"], w["g1"], dxm, dpa, dpf, dpb, w["wa"], w["wf"], w["wb"])
    big["w_in"] = w_in_unprep(wgrad(s["h"], dpa, "wgrad_in_qkv"), wgrad(s["h"], dpf, "wgrad_in_f"),
                              wgrad(s["h"], dpb, "wgrad_in_b"), big["w_in"], li)
    big["w_attn_out"], big["w_conv_out"], big["pool_w"], big["w_o"] = branch_g_place(
        dwao, dwco, dwpool, dwo, (big["w_attn_out"], big["w_conv_out"], big["pool_w"], big["w_o"]), li)
    sm = dict(norm_mix_g=dg1[0], forget_b=dfb[0, 0:N_HEADS], q_norm_g=dgq.reshape(N_HEADS, HEAD_DIM).sum(0),
              k_norm_g=dgk.reshape(N_HEADS, HEAD_DIM).sum(0), pool_scale=dpscale[0], norm_ffn_g=dg2[0],
              conv_w=dconvw[0:CONV_K].reshape(-1))
    return dx, big, sm


def grad_buffers(full):
    return {n: lax.empty(full[n].shape, F32) for n in SPLIT}


def local_step(x, target, gathered, small):
    n_layers = small["norm_mix_g"].shape[0]
    done = []
    for l in range(n_layers):
        x, w, s = layer_fwd(x, gathered, l, {n: v[l] for n, v in small.items()})
        done.append((w, s))
    loss, dx = loss_head(x, target)
    big = grad_buffers(gathered)
    sm = [None] * n_layers
    for l in reversed(range(n_layers)):
        dx, big, sm[l] = layer_bwd(dx, *done[l], l, big)
    return loss, dx, big, {n: jnp.stack([g[n] for g in sm]) for n in SMALL}


def adamw(w, g, m, v, name):
    R, C = w.shape
    tm = 256 if R % 256 == 0 else R

    def body(w_ref, g_ref, m_ref, v_ref, d_ref, nm_ref, nv_ref):
        gr = g_ref[...]
        m_new = ADAM_B1 * m_ref[...] + (1.0 - ADAM_B1) * gr
        v_new = ADAM_B2 * v_ref[...] + (1.0 - ADAM_B2) * jnp.square(gr)
        nm_ref[...] = m_new
        nv_ref[...] = v_new
        m_hat = m_new / (1.0 - ADAM_B1 ** ADAM_STEP)
        v_hat = v_new / (1.0 - ADAM_B2 ** ADAM_STEP)
        d_ref[...] = -ADAM_LR * (m_hat / (jnp.sqrt(v_hat) + ADAM_EPS) + ADAM_WD * w_ref[...])

    spec = _rows(tm, C)
    out = jax.ShapeDtypeStruct((R, C), F32)
    return pl.pallas_call(
        body,
        grid=(R // tm,),
        in_specs=[spec] * 4,
        out_specs=[spec] * 3,
        out_shape=[out] * 3,
        compiler_params=_params("parallel"),
        name=name,
    )(w, g, m, v)


MESH = pl.DeviceIdType.MESH
HBM_REF = pl.BlockSpec(memory_space=pl.ANY)
N_CHIPS = 4
N_DEV = 8
GROUPS = ((0, 2), (2, 4))
SMALL_SHAPE = (128, LANES)


def _mesh_pos():
    return lax.axis_index("x"), lax.axis_index("y"), lax.axis_index("c")


def _other_chips(x, y):
    return [(1 - x, y), (x, 1 - y), (1 - x, 1 - y)]


def _remote(src, dst, send_sem, recv_sem, to):
    return pltpu.make_async_remote_copy(src_ref=src, dst_ref=dst, send_sem=send_sem, recv_sem=recv_sem,
                                        device_id=to, device_id_type=MESH)


def _row_tile(rows):
    for tm in (256, 176, 128):
        if rows % tm == 0:
            return tm
    return rows


def _as4(a):
    return a.reshape(a.shape[0], a.shape[1], -1, a.shape[-1])


def _core_rows(buf, c):
    R = buf.shape[2]
    if R % 2:
        return pl.ds(0, R), pl.ds(0, R), False
    return pl.ds(c * (R // 2), R // 2), pl.ds((1 - c) * (R // 2), R // 2), True


def place_shard(chip, w, lo, hi, dtype, name):
    w3 = w.reshape(w.shape[0], -1, w.shape[-1])
    _, R, C = w3.shape
    tm = _row_tile(R)

    def body(chip_ref, w_ref, o_ref):
        o_ref[...] = w_ref[...].astype(dtype)

    return pl.pallas_call(
        body,
        grid_spec=pltpu.PrefetchScalarGridSpec(
            num_scalar_prefetch=1,
            grid=(hi - lo, R // tm),
            in_specs=[pl.BlockSpec((None, tm, C), lambda l, i, chip: (lo + l, i, 0))],
            out_specs=pl.BlockSpec((None, None, tm, C), lambda l, i, chip: (l, chip[0], i, 0)),
        ),
        out_shape=jax.ShapeDtypeStruct((hi - lo, N_CHIPS, R, C), dtype),
        compiler_params=_params("parallel", "parallel"),
        name=name,
    )(chip, w3)


HBM_SPACE = pl.BlockSpec(memory_space=pltpu.HBM)
SEM_SPACE = pl.BlockSpec(memory_space=pltpu.SEMAPHORE)
IN_FLIGHT = pltpu.SideEffectType.DATAFLOW_SIDE_EFFECTING


def _sem_table(send_sems, recv_sems):
    return lambda t, j: (send_sems.at[t, j], recv_sems.at[t, j])


def _sem_per_peer(send_sems, recv_sems):
    return lambda t, j: (send_sems[j], recv_sems[j])


def _gather_ici(bufs, sem, sends=True, lands=True):
    x, y, c = _mesh_pos()
    me = 2 * x + y
    out, into = [], []
    for t, buf in enumerate(bufs):
        mine, _, _ = _core_rows(buf, c)
        part = lambda k, buf=buf, mine=mine: buf.at[pl.ds(0, buf.shape[0]), k, mine]
        for j, (px, py) in enumerate(_other_chips(x, y)):
            if sends:
                out.append(_remote(part(me), part(me), *sem(t, j), (px, py, c)))
            if lands:
                into.append(_remote(part(2 * px + py), part(2 * px + py), *sem(t, j), (px, py, c)))
    return out, into


def _gather_d2d(bufs, send_sems, recv_sems, first):
    x, y, c = _mesh_pos()
    sends, lands = [], []
    for t, buf in enumerate(bufs):
        mine, theirs, split = _core_rows(buf, c)
        if not split:
            continue
        for j, (px, py) in enumerate(_other_chips(x, y)):
            part = lambda rows, buf=buf, k=2 * px + py: buf.at[pl.ds(0, buf.shape[0]), k, rows]
            sems = (send_sems.at[t, first + j], recv_sems.at[t, first + j], (x, y, 1 - c))
            sends.append(_remote(part(mine), part(mine), *sems))
            lands.append(_remote(part(theirs), part(theirs), *sems))
    return sends, lands


def gather_shards(bufs):
    n = len(bufs)

    def body(*refs):
        outs = refs[n:2 * n]
        send_sems, recv_sems = refs[2 * n:]
        ici_out, ici_in = _gather_ici(outs, _sem_table(send_sems, recv_sems))
        d2d_out, d2d_in = _gather_d2d(outs, send_sems, recv_sems, 3)
        for cp in ici_out:
            cp.start()
        for cp in ici_in:
            cp.wait_recv()
        for cp in d2d_out:
            cp.start()
        for cp in d2d_in:
            cp.wait_recv()
        for cp in ici_out + d2d_out:
            cp.wait_send()

    return pl.pallas_call(
        body,
        in_specs=[HBM_REF] * n,
        out_specs=[HBM_REF] * n,
        out_shape=[jax.ShapeDtypeStruct(b.shape, b.dtype) for b in bufs],
        input_output_aliases={t: t for t in range(n)},
        scratch_shapes=[pltpu.SemaphoreType.DMA((n, 6)), pltpu.SemaphoreType.DMA((n, 6))],
        name="gather_shards",
    )(*bufs)


def gather_start(bufs):
    n = len(bufs)

    def body(*refs):
        send_sems, recv_sems = refs[n:n + 3], refs[n + 3:n + 6]
        outs = refs[n + 6:2 * n + 6]
        token = refs[2 * n + 6]
        for cp in _gather_ici(outs, _sem_per_peer(send_sems, recv_sems), lands=False)[0]:
            cp.start()
        token[...] = jnp.zeros_like(token)

    res = pl.pallas_call(
        body,
        in_specs=[HBM_SPACE] * n,
        out_specs=[SEM_SPACE] * 6 + [HBM_SPACE] * n + [pl.BlockSpec(memory_space=pltpu.VMEM)],
        out_shape=[pltpu.SemaphoreType.DMA(())] * 6
        + [pltpu.HBM(b.shape, b.dtype) for b in bufs] + [jax.ShapeDtypeStruct((8, LANES), F32)],
        input_output_aliases={t: t + 6 for t in range(n)},
        compiler_params=pltpu.CompilerParams(has_side_effects=IN_FLIGHT),
        name="gather_start",
    )(*[pltpu.with_memory_space_constraint(b, pltpu.HBM) for b in bufs])
    return list(res[0:3]), list(res[3:6]), list(res[6:n + 6]), res[n + 6]


def gather_wait(send_sems, recv_sems, bufs, after):
    n = len(bufs)

    def body(*refs):
        ins_send, ins_recv = refs[n:n + 3], refs[n + 3:n + 6]
        outs = refs[n + 7:]
        sends, lands = _gather_ici(outs, _sem_per_peer(ins_send, ins_recv))
        for cp in lands:
            cp.wait_recv()
        for cp in sends:
            cp.wait_send()

    return list(pl.pallas_call(
        body,
        in_specs=[HBM_SPACE] * n + [SEM_SPACE] * 6 + [HBM_REF],
        out_specs=[HBM_SPACE] * n,
        out_shape=[pltpu.HBM(b.shape, b.dtype) for b in bufs],
        input_output_aliases={t: t for t in range(n)},
        compiler_params=pltpu.CompilerParams(has_side_effects=IN_FLIGHT),
        name="gather_wait",
    )(*bufs, *send_sems, *recv_sems, after))


def gather_forward(bufs):
    n = len(bufs)

    def body(*refs):
        outs = refs[n:2 * n]
        send_sems, recv_sems = refs[2 * n:]
        sends, lands = _gather_d2d(outs, send_sems, recv_sems, 0)
        for cp in sends:
            cp.start()
        for cp in lands:
            cp.wait_recv()
        for cp in sends:
            cp.wait_send()

    return list(pl.pallas_call(
        body,
        in_specs=[HBM_REF] * n,
        out_specs=[HBM_REF] * n,
        out_shape=[jax.ShapeDtypeStruct(b.shape, b.dtype) for b in bufs],
        input_output_aliases={t: t for t in range(n)},
        scratch_shapes=[pltpu.SemaphoreType.DMA((n, 3)), pltpu.SemaphoreType.DMA((n, 3))],
        name="gather_forward",
    )(*bufs))


def pair_exchange(grads):
    n = len(grads)

    def body(*refs):
        ins, outs = refs[:n], refs[n:2 * n]
        send_sems, recv_sems = refs[2 * n:]
        x, y, c = _mesh_pos()
        cps = []
        for t in range(n):
            _, theirs, _ = _core_rows(ins[t], c)
            src = ins[t].at[pl.ds(0, ins[t].shape[0]), pl.ds(0, N_CHIPS), theirs]
            cps.append(_remote(src, outs[t], send_sems.at[t], recv_sems.at[t], (x, y, 1 - c)))
        for cp in cps:
            cp.start()
        for cp in cps:
            cp.wait()

    return pl.pallas_call(
        body,
        in_specs=[HBM_REF] * n,
        out_specs=[HBM_REF] * n,
        out_shape=[jax.ShapeDtypeStruct(g.shape[:2] + (g.shape[2] // 2, g.shape[3]), g.dtype) for g in grads],
        scratch_shapes=[pltpu.SemaphoreType.DMA((n,)), pltpu.SemaphoreType.DMA((n,))],
        name="pair_exchange",
    )(*grads)


def pair_sum(core, g, t, name):
    n, _, half, C = t.shape
    tm = _row_tile(half)
    per = half // tm

    def body(c_ref, g_ref, t_ref, o_ref):
        o_ref[...] = (g_ref[...] + t_ref[...]).astype(BF16)

    tile = pl.BlockSpec((None, tm, C), lambda a, i, c: (a, i, 0))
    out = pl.pallas_call(
        body,
        grid_spec=pltpu.PrefetchScalarGridSpec(
            num_scalar_prefetch=1,
            grid=(n * N_CHIPS, per),
            in_specs=[pl.BlockSpec((None, tm, C), lambda a, i, c: (a, per * c[0] + i, 0)), tile],
            out_specs=tile,
        ),
        out_shape=jax.ShapeDtypeStruct((n * N_CHIPS, half, C), BF16),
        compiler_params=_params("parallel", "parallel"),
        name=name,
    )(core, g.reshape(n * N_CHIPS, 2 * half, C), t.reshape(n * N_CHIPS, half, C))
    return out.reshape(t.shape)


def _chip_copies(sums, recv, sem):
    x, y, c = _mesh_pos()
    cps = []
    for t in range(len(sums)):
        for j, (px, py) in enumerate(_other_chips(x, y)):
            src = sums[t].at[pl.ds(0, sums[t].shape[0]), 2 * px + py]
            cps.append(_remote(src, recv[t].at[j], *sem(t, j), (px, py, c)))
    return cps


def _recv_shape(s):
    return (N_CHIPS - 1, s.shape[0]) + s.shape[2:]


def chip_exchange(sums):
    n = len(sums)

    def body(*refs):
        ins, outs = refs[:n], refs[n:2 * n]
        cps = _chip_copies(ins, outs, _sem_table(*refs[2 * n:]))
        for cp in cps:
            cp.start()
        for cp in cps:
            cp.wait()

    return list(pl.pallas_call(
        body,
        in_specs=[HBM_REF] * n,
        out_specs=[HBM_REF] * n,
        out_shape=[jax.ShapeDtypeStruct(_recv_shape(s), s.dtype) for s in sums],
        scratch_shapes=[pltpu.SemaphoreType.DMA((n, 3)), pltpu.SemaphoreType.DMA((n, 3))],
        name="chip_exchange",
    )(*sums))


def chip_exchange_start(sums):
    n = len(sums)
    lands = [lax.empty(_recv_shape(s), s.dtype) for s in sums]

    def body(*refs):
        send_sems, recv_sems = refs[2 * n:2 * n + 3], refs[2 * n + 3:2 * n + 6]
        outs = refs[2 * n + 6:]
        for cp in _chip_copies(outs[:n], outs[n:2 * n], _sem_per_peer(send_sems, recv_sems)):
            cp.start()

    res = pl.pallas_call(
        body,
        in_specs=[HBM_SPACE] * (2 * n),
        out_specs=[SEM_SPACE] * 6 + [HBM_SPACE] * (2 * n),
        out_shape=[pltpu.SemaphoreType.DMA(())] * 6 + [pltpu.HBM(a.shape, a.dtype) for a in list(sums) + lands],
        input_output_aliases={t: t + 6 for t in range(2 * n)},
        compiler_params=pltpu.CompilerParams(has_side_effects=IN_FLIGHT),
        name="chip_exchange_start",
    )(*[pltpu.with_memory_space_constraint(a, pltpu.HBM) for a in list(sums) + lands])
    return list(res[0:3]), list(res[3:6]), list(res[6:n + 6]), list(res[n + 6:])


def chip_exchange_wait(send_sems, recv_sems, sums, lands, after):
    n = len(sums)

    def body(*refs):
        ins_send, ins_recv = refs[2 * n:2 * n + 3], refs[2 * n + 3:2 * n + 6]
        outs = refs[2 * n + 7:]
        for cp in _chip_copies(outs[:n], outs[n:], _sem_per_peer(ins_send, ins_recv)):
            cp.wait()

    res = pl.pallas_call(
        body,
        in_specs=[HBM_SPACE] * (2 * n) + [SEM_SPACE] * 6 + [HBM_REF],
        out_specs=[HBM_SPACE] * (2 * n),
        out_shape=[pltpu.HBM(a.shape, a.dtype) for a in list(sums) + list(lands)],
        input_output_aliases={t: t for t in range(2 * n)},
        compiler_params=pltpu.CompilerParams(has_side_effects=IN_FLIGHT),
        name="chip_exchange_wait",
    )(*sums, *lands, *send_sems, *recv_sems, after)
    return list(res[:n]), list(res[n:])


def chip_sum_into(core, chip, recv, sums, total, lo, name):
    _, n, half, C = recv.shape
    tm = _row_tile(half)
    per = half // tm

    def body(c_ref, k_ref, r_ref, s_ref, t_ref, o_ref):
        acc = s_ref[...].astype(F32)
        for j in range(N_CHIPS - 1):
            acc = acc + r_ref[j].astype(F32)
        o_ref[...] = acc

    return pl.pallas_call(
        body,
        grid_spec=pltpu.PrefetchScalarGridSpec(
            num_scalar_prefetch=2,
            grid=(n, per),
            in_specs=[pl.BlockSpec((N_CHIPS - 1, None, tm, C), lambda a, i, c, k: (0, a, i, 0)),
                      pl.BlockSpec((None, None, tm, C), lambda a, i, c, k: (a, k[0], i, 0)),
                      HBM_REF],
            out_specs=pl.BlockSpec((None, tm, C), lambda a, i, c, k: (lo + a, per * c[0] + i, 0)),
        ),
        out_shape=jax.ShapeDtypeStruct(total.shape, F32),
        input_output_aliases={4: 0},
        compiler_params=_params("parallel", "parallel"),
        name=name,
    )(core, chip, recv, sums, total)


def sibling_share(totals):
    n = len(totals)

    def body(*refs):
        outs = refs[n:2 * n]
        send_sems, recv_sems = refs[2 * n:]
        x, y, c = _mesh_pos()
        half = lambda t, h: outs[t].at[pl.ds(0, DEPTH), pl.ds(h * (outs[t].shape[1] // 2), outs[t].shape[1] // 2)]
        sent = [_remote(half(t, c), half(t, c), send_sems.at[t], recv_sems.at[t], (x, y, 1 - c)) for t in range(n)]
        for cp in sent:
            cp.start()
        for t in range(n):
            _remote(half(t, 1 - c), half(t, 1 - c), send_sems.at[t], recv_sems.at[t], (x, y, 1 - c)).wait_recv()
        for cp in sent:
            cp.wait_send()

    return pl.pallas_call(
        body,
        in_specs=[HBM_REF] * n,
        out_specs=[HBM_REF] * n,
        out_shape=[jax.ShapeDtypeStruct(t.shape, t.dtype) for t in totals],
        input_output_aliases={t: t for t in range(n)},
        scratch_shapes=[pltpu.SemaphoreType.DMA((n,)), pltpu.SemaphoreType.DMA((n,))],
        name="sibling_share",
    )(*totals)


def small_allgather(small):
    def body(s_ref, a_ref, send_sems, recv_sems, local_sem):
        x, y, c = _mesh_pos()
        me = 4 * x + 2 * y + c
        own = pltpu.make_async_copy(s_ref, a_ref.at[me], local_sem)
        own.start()
        sent = []
        for k in range(1, N_DEV):
            peer = (x ^ (k >> 2), y ^ ((k >> 1) & 1), c ^ (k & 1))
            cp = _remote(s_ref, a_ref.at[me], send_sems.at[k - 1], recv_sems.at[k - 1], peer)
            cp.start()
            sent.append(cp)
        for k in range(1, N_DEV):
            px, py, pc = x ^ (k >> 2), y ^ ((k >> 1) & 1), c ^ (k & 1)
            _remote(s_ref, a_ref.at[4 * px + 2 * py + pc], send_sems.at[k - 1], recv_sems.at[k - 1], (px, py, pc)).wait_recv()
        for cp in sent:
            cp.wait_send()
        own.wait()

    return pl.pallas_call(
        body,
        in_specs=[HBM_REF],
        out_specs=HBM_REF,
        out_shape=jax.ShapeDtypeStruct((N_DEV,) + SMALL_SHAPE, small.dtype),
        scratch_shapes=[pltpu.SemaphoreType.DMA((N_DEV - 1,)), pltpu.SemaphoreType.DMA((N_DEV - 1,)), pltpu.SemaphoreType.DMA],
        name="small_allgather",
    )(small)


def small_sum(blocks):
    def body(a_ref, o_ref):
        acc = a_ref[0]
        for d in range(1, N_DEV):
            acc = acc + a_ref[d]
        o_ref[...] = acc

    return pl.pallas_call(
        body,
        in_specs=[pl.BlockSpec(memory_space=pltpu.VMEM)],
        out_specs=pl.BlockSpec(memory_space=pltpu.VMEM),
        out_shape=jax.ShapeDtypeStruct(SMALL_SHAPE, F32),
        name="small_sum",
    )(blocks)


def pack_small(grads, loss):
    flat = jnp.concatenate([grads[n].reshape(-1) for n in SMALL] + [loss.reshape(-1)])
    size = SMALL_SHAPE[0] * SMALL_SHAPE[1]
    return jnp.pad(flat, (0, size - flat.shape[0])).reshape(SMALL_SHAPE)


def unpack_small(packed):
    flat = packed.reshape(-1)
    out, off = {}, 0
    for n, size in SMALL.items():
        out[n] = flat[off:off + DEPTH * size].reshape(DEPTH, size)
        off += DEPTH * size
    return out, flat[off]


def kernel(x, norm_mix_g, w_in, forget_b, q_norm_g, k_norm_g, w_attn_out, conv_w, w_conv_out, pool_w, pool_scale, w_o, norm_ffn_g, w_ffn_in, w_ffn_out, loss_target, m_norm_mix_g, m_w_in, m_forget_b, m_q_norm_g, m_k_norm_g, m_w_attn_out, m_conv_w, m_w_conv_out, m_pool_w, m_pool_scale, m_w_o, m_norm_ffn_g, m_w_ffn_in, m_w_ffn_out, v_norm_mix_g, v_w_in, v_forget_b, v_q_norm_g, v_k_norm_g, v_w_attn_out, v_conv_w, v_w_conv_out, v_pool_w, v_pool_scale, v_w_o, v_norm_ffn_g, v_w_ffn_in, v_w_ffn_out):
    given = dict(locals())
    weights = {n: given[n] for n in WEIGHTS}

    core = lax.axis_index("c").astype(jnp.int32)
    chip = (2 * lax.axis_index("x") + lax.axis_index("y")).astype(jnp.int32)

    core, chip = core.reshape(1), chip.reshape(1)
    names = list(SPLIT) + ["conv_w"]
    (lo0, hi0), (lo1, hi1) = GROUPS

    def placed(lo, hi):
        return [place_shard(chip, weights[n], lo, hi, F32 if n == "conv_w" else BF16, "place_" + n) for n in names]

    def as_weights(bufs):
        return {n: b.reshape(b.shape[:2] + weights[n].shape[1:]) for n, b in zip(names, bufs)}

    def layer_small(l):
        return {n: weights[n][l] for n in SMALL if n != "conv_w"}

    full0 = as_weights(gather_shards(placed(lo0, hi0)))
    send_sems, recv_sems, in_flight, token = gather_start(placed(lo1, hi1))
    xs = x[0] + token[0:1, 0:1]
    done = [None] * DEPTH
    for l in range(lo0, hi0):
        xs, w, s = layer_fwd(xs, full0, l - lo0, layer_small(l))
        done[l] = (w, s)
    full1 = as_weights(gather_forward(gather_wait(send_sems, recv_sems, in_flight, xs)))
    for l in range(lo1, hi1):
        xs, w, s = layer_fwd(xs, full1, l - lo1, layer_small(l))
        done[l] = (w, s)
    loss, dx = loss_head(xs, loss_target[0])

    small_grads = [None] * DEPTH

    def backward(dx, full, lo, hi):
        big = grad_buffers({n: full[n] for n in SPLIT})
        for l in reversed(range(lo, hi)):
            dx, big, small_grads[l] = layer_bwd(dx, *done[l], l - lo, big)
        grads = [_as4(big[n]) for n in SPLIT]
        sums = [pair_sum(core, g, t, "pair_sum_" + n) for n, g, t in zip(SPLIT, grads, pair_exchange(grads))]
        return dx, sums

    def add_chips(totals, recv, sums, lo):
        return [chip_sum_into(core, chip, r, s, t, lo, "chip_sum_" + n) for n, r, s, t in zip(SPLIT, recv, sums, totals)]

    dx, sums1 = backward(dx, full1, lo1, hi1)
    send_sems, recv_sems, sums1, lands1 = chip_exchange_start(sums1)
    dx, sums0 = backward(dx, full0, lo0, hi0)
    sums1, recv1 = chip_exchange_wait(send_sems, recv_sems, sums1, lands1, dx)
    totals = [lax.empty((DEPTH, 2 * s.shape[2], s.shape[3]), F32) for s in sums0]
    totals = add_chips(totals, recv1, sums1, lo1)
    totals = add_chips(totals, chip_exchange(sums0), sums0, lo0)
    reduced = {n: t.reshape(weights[n].shape) for n, t in zip(SPLIT, sibling_share(totals))}

    small_grads = {n: jnp.stack([g[n] for g in small_grads]) for n in SMALL}
    small_total, loss_sum = unpack_small(small_sum(small_allgather(pack_small(small_grads, loss))))
    chip = chip[0]
    cols = D_CONV // N_CHIPS
    small_total["conv_w"] = lax.dynamic_slice_in_dim(small_total["conv_w"].reshape(DEPTH, CONV_K, D_CONV), chip * cols, cols, axis=2)
    for n in SMALL:
        reduced[n] = small_total[n].reshape(weights[n].shape)

    deltas, new_m, new_v = {}, {}, {}
    for n in WEIGHTS:
        w = weights[n]
        flat = (-1, w.shape[-1])
        d, nm, nv = adamw(w.reshape(flat), reduced[n].reshape(flat), given["m_" + n].reshape(flat),
                          given["v_" + n].reshape(flat), "adamw_" + n)
        deltas[n], new_m[n], new_v[n] = d.reshape(w.shape), nm.reshape(w.shape), nv.reshape(w.shape)
    return (loss_sum, dx[None], *[reduced[n] for n in WEIGHTS], *[deltas[n] for n in WEIGHTS],
            *[new_m[n] for n in WEIGHTS], *[new_v[n] for n in WEIGHTS])
```

```python
import functools

import numpy as np
import jax
import jax.numpy as jnp
from jax import lax
from jax.experimental import pallas as pl
from jax.experimental.pallas import tpu as pltpu

F32 = jnp.float32
BF16 = jnp.bfloat16

D_MODEL = 1024
DEPTH = 4
HEAD_DIM = 64
N_HEADS = 8
D_ATTN = 512
D_CONV = 256
D_POOL = 256
D_FF = 2816
D_IN = 5640
CONV_K = 3
POOL_WINDOWS = (2, 4, 8, 16)
N_GROUPS = len(POOL_WINDOWS)
EPS = 1e-6
ADAM_LR, ADAM_B1, ADAM_B2, ADAM_EPS, ADAM_WD, ADAM_STEP = 0.001, 0.9, 0.999, 1e-08, 0.01, 10

D_QKV = 3 * D_ATTN
D_F = 128
D_B = 3 * D_CONV + D_POOL + 3 * D_MODEL
D_LOCAL = 3 * D_CONV + D_POOL

LANES = 128
D_HEADS = N_HEADS * LANES
HALO = 16
VMEM_LIMIT = 56 * 1024 * 1024
NEG = -1e30
LOG2E = 1.4426950408889634
LN2 = 0.6931471805599453

TM = 256
TM_MIX = 128
TQ = 512
TK = 1024
KB = 512
QT = 1024

LANE_C = 64
LANE_ONE = 67
LANE_LSE = 70
N_PIECES = 3


def _dot(a, b):
    return jnp.dot(a, b, preferred_element_type=F32)


def _dot_nt(a, b):
    return lax.dot_general(a, b, (((1,), (1,)), ((), ())), preferred_element_type=F32)


def _dot_tn(a, b):
    return lax.dot_general(a, b, (((0,), (0,)), ((), ())), preferred_element_type=F32)


def _params(*sem):
    return pltpu.CompilerParams(dimension_semantics=sem, vmem_limit_bytes=VMEM_LIMIT)


def _rows(tm, n):
    return pl.BlockSpec((tm, n), lambda i, *_: (i, 0))


def _whole(a):
    nd = a.ndim
    return pl.BlockSpec(a.shape, lambda *_: (0,) * nd)


def _layer(shape):
    nd = len(shape)
    return pl.BlockSpec((None,) + tuple(shape), lambda *a: (a[-1][0],) + (0,) * nd)


def _layer_index(l):
    return jnp.full((1,), l, jnp.int32)


def _split_bf16(x):
    hi = x.astype(BF16)
    lo = (x - hi.astype(F32)).astype(BF16)
    return hi, lo


def _pieces(x):
    p1 = x.astype(BF16)
    r1 = x - p1.astype(F32)
    p2 = r1.astype(BF16)
    p3 = (r1 - p2.astype(F32)).astype(BF16)
    return p1, p2, p3


def _sigmoid(x):
    return 1.0 / (1.0 + jnp.exp(-x))


def w_in_prep(win, l):
    tr = 256
    n = D_IN // 4
    v_rest = D_QKV - n
    b0 = v_rest + N_HEADS

    def body(s0, s1, s2, s3, wa_ref, wf_ref, wb_ref):
        b = s1[...]
        wa_ref[...] = jnp.concatenate([s0[...], b[:, 0:v_rest]], axis=1)
        wf_ref[...] = jnp.concatenate([b[:, v_rest:b0], jnp.zeros((tr, D_F - N_HEADS), b.dtype)], axis=1)
        wb_ref[...] = jnp.concatenate([b[:, b0:n], s2[...], s3[...]], axis=1)

    shard = lambda j: pl.BlockSpec((None, None, tr, n), lambda i: (l, j, i, 0))
    return pl.pallas_call(
        body,
        grid=(D_MODEL // tr,),
        in_specs=[shard(0), shard(1), shard(2), shard(3)],
        out_specs=[_rows(tr, D_QKV), _rows(tr, D_F), _rows(tr, D_B)],
        out_shape=[jax.ShapeDtypeStruct((D_MODEL, D_QKV), win.dtype), jax.ShapeDtypeStruct((D_MODEL, D_F), win.dtype),
                   jax.ShapeDtypeStruct((D_MODEL, D_B), win.dtype)],
        compiler_params=_params("parallel"),
        name="w_in_prep",
    )(win, win, win, win)


def _into_layer(buf):
    return dict(in_spec=HBM_REF, out_shape=jax.ShapeDtypeStruct(buf.shape, buf.dtype), aliases={1: 0})


def w_in_unprep(dwa, dwf, dwb, buf, l):
    tr = 256
    n = D_IN // 4
    v_rest = D_QKV - n
    b1 = n - v_rest - N_HEADS
    place = _into_layer(buf)

    def body(l_ref, buf_ref, a_ref, f_ref, b_ref, o_ref):
        a = a_ref[...]
        b = b_ref[...]
        o_ref[0] = a[:, 0:n]
        o_ref[1] = jnp.concatenate([a[:, n:D_QKV], f_ref[:, 0:N_HEADS], b[:, 0:b1]], axis=1)
        o_ref[2] = b[:, b1:b1 + n]
        o_ref[3] = b[:, b1 + n:D_B]

    return pl.pallas_call(
        body,
        grid_spec=pltpu.PrefetchScalarGridSpec(
            num_scalar_prefetch=1,
            grid=(D_MODEL // tr,),
            in_specs=[place["in_spec"], _rows(tr, D_QKV), _rows(tr, D_F), _rows(tr, D_B)],
            out_specs=pl.BlockSpec((None, 4, tr, n), lambda i, l: (l[0], 0, i, 0)),
        ),
        out_shape=place["out_shape"],
        input_output_aliases=place["aliases"],
        compiler_params=_params("parallel"),
        name="w_in_unprep",
    )(_layer_index(l), buf, dwa, dwf, dwb)


def branch_w_prep(wao, wco, pw, cw, l):
    gd = D_POOL // N_GROUPS
    od = D_MODEL // N_GROUPS

    def body(wao_ref, wco_ref, pw_ref, cw_ref, ao_ref, co_ref, po_ref, co8_ref):
        a = jnp.concatenate([wao_ref[j] for j in range(4)], axis=1)
        gap = jnp.zeros((LANES - HEAD_DIM, D_MODEL), a.dtype)
        ao_ref[...] = jnp.concatenate(
            [blk for h in range(N_HEADS) for blk in (a[h * HEAD_DIM:(h + 1) * HEAD_DIM], gap)], axis=0)
        co_ref[...] = jnp.concatenate([wco_ref[j] for j in range(4)], axis=1)
        zero = jnp.zeros((gd, od), a.dtype)
        po_ref[...] = jnp.concatenate(
            [jnp.concatenate([jnp.concatenate([pw_ref[j, g] for j in range(4)], axis=1) if g2 == g else zero
                              for g2 in range(N_GROUPS)], axis=1) for g in range(N_GROUPS)], axis=0)
        co8_ref[...] = jnp.zeros_like(co8_ref)
        co8_ref[0:CONV_K, :] = jnp.concatenate([cw_ref[j] for j in range(4)], axis=1)

    sel = lambda *shape: pl.BlockSpec((None,) + shape, lambda i: (l,) + (0,) * len(shape))
    return pl.pallas_call(
        body,
        grid=(1,),
        in_specs=[sel(4, D_ATTN, D_MODEL // 4), sel(4, D_CONV, D_MODEL // 4), sel(4, N_GROUPS, gd, od // 4),
                  sel(4, CONV_K, D_CONV // 4)],
        out_specs=[pl.BlockSpec((D_HEADS, D_MODEL), lambda i: (0, 0)), pl.BlockSpec((D_CONV, D_MODEL), lambda i: (0, 0)),
                   pl.BlockSpec((D_POOL, D_MODEL), lambda i: (0, 0)), pl.BlockSpec((8, D_CONV), lambda i: (0, 0))],
        out_shape=[jax.ShapeDtypeStruct((D_HEADS, D_MODEL), BF16), jax.ShapeDtypeStruct((D_CONV, D_MODEL), BF16),
                   jax.ShapeDtypeStruct((D_POOL, D_MODEL), BF16), jax.ShapeDtypeStruct((8, D_CONV), F32)],
        name="branch_w_prep",
    )(wao, wco, pw, cw)


def branch_g_place(dwao, dwco, dwpool, dwo, bufs, l):
    gd = D_POOL // N_GROUPS
    od = D_MODEL // N_GROUPS
    q = D_MODEL // 4

    def body(l_ref, b0, b1, b2, b3, a_ref, c_ref, p_ref, w_ref, ao_ref, co_ref, po_ref, wo_ref):
        a = jnp.concatenate([a_ref[h * LANES:h * LANES + HEAD_DIM, :] for h in range(N_HEADS)], axis=0)
        c = c_ref[...]
        p = p_ref[...]
        for j in range(4):
            ao_ref[j] = a[:, j * q:(j + 1) * q]
            co_ref[j] = c[:, j * q:(j + 1) * q]
            wo_ref[j] = w_ref[j * q:(j + 1) * q, :]
            for g in range(N_GROUPS):
                c0 = g * od + j * (od // 4)
                po_ref[j, g] = p[g * gd:(g + 1) * gd, c0:c0 + od // 4]

    whole = lambda a: pl.BlockSpec(a.shape, lambda i, l: (0,) * a.ndim)
    layer = lambda b: pl.BlockSpec((None,) + b.shape[1:], lambda i, l: (l[0],) + (0,) * (b.ndim - 1))
    return pl.pallas_call(
        body,
        grid_spec=pltpu.PrefetchScalarGridSpec(
            num_scalar_prefetch=1,
            grid=(1,),
            in_specs=[HBM_REF] * 4 + [whole(dwao), whole(dwco), whole(dwpool), whole(dwo)],
            out_specs=[layer(b) for b in bufs],
        ),
        out_shape=[jax.ShapeDtypeStruct(b.shape, b.dtype) for b in bufs],
        input_output_aliases={1: 0, 2: 1, 3: 2, 4: 3},
        compiler_params=_params("arbitrary"),
        name="branch_g_place",
    )(_layer_index(l), *bufs, dwao, dwco, dwpool, dwo)


def in_proj_fwd(x, g, wa, wf, wb):
    S = x.shape[0]

    def body(x_ref, g_ref, wa_ref, wf_ref, wb_ref, h_ref, pa_ref, pf_ref, pb_ref):
        xf = x_ref[...]
        r = lax.rsqrt(jnp.mean(xf * xf, axis=-1, keepdims=True) + EPS)
        h = (xf * r * g_ref[...]).astype(BF16)
        h_ref[...] = h
        pa_ref[...] = _dot(h, wa_ref[...]).astype(BF16)
        pf_ref[...] = _dot(h, wf_ref[...])
        pb_ref[...] = _dot(h, wb_ref[...]).astype(BF16)

    return pl.pallas_call(
        body,
        grid=(S // TM,),
        in_specs=[_rows(TM, D_MODEL), _whole(g), _whole(wa), _whole(wf), _whole(wb)],
        out_specs=[_rows(TM, D_MODEL), _rows(TM, D_QKV), _rows(TM, D_F), _rows(TM, D_B)],
        out_shape=[
            jax.ShapeDtypeStruct((S, D_MODEL), BF16),
            jax.ShapeDtypeStruct((S, D_QKV), BF16),
            jax.ShapeDtypeStruct((S, D_F), F32),
            jax.ShapeDtypeStruct((S, D_B), BF16),
        ],
        compiler_params=_params("parallel"),
        name="in_proj_fwd",
    )(x, g, wa, wf, wb)


def in_proj_bwd(x, g, dxm, dpa, dpf, dpb, wa, wf, wb):
    S = x.shape[0]

    def body(x_ref, g_ref, dxm_ref, dpa_ref, dpf_ref, dpb_ref, wa_ref, wf_ref, wb_ref, dx_ref, dg_ref):
        @pl.when(pl.program_id(0) == 0)
        def _():
            dg_ref[...] = jnp.zeros_like(dg_ref)

        dh = _dot_nt(dpa_ref[...], wa_ref[...]) + _dot_nt(dpf_ref[...], wf_ref[...]) + _dot_nt(dpb_ref[...], wb_ref[...])
        xf = x_ref[...]
        r = lax.rsqrt(jnp.mean(xf * xf, axis=-1, keepdims=True) + EPS)
        xhat = xf * r
        dg_ref[...] += jnp.sum(dh * xhat, axis=0, keepdims=True)
        gdh = dh * g_ref[...]
        dx_ref[...] = dxm_ref[...] + r * (gdh - xhat * jnp.mean(xhat * gdh, axis=-1, keepdims=True))

    return pl.pallas_call(
        body,
        grid=(S // TM,),
        in_specs=[_rows(TM, D_MODEL), _whole(g), _rows(TM, D_MODEL), _rows(TM, D_QKV), _rows(TM, D_F), _rows(TM, D_B),
                  _whole(wa), _whole(wf), _whole(wb)],
        out_specs=[_rows(TM, D_MODEL), pl.BlockSpec((1, D_MODEL), lambda i: (0, 0))],
        out_shape=[jax.ShapeDtypeStruct((S, D_MODEL), F32), jax.ShapeDtypeStruct((1, D_MODEL), F32)],
        compiler_params=_params("arbitrary"),
        name="in_proj_bwd",
    )(x, g, dxm, dpa, dpf, dpb, wa, wf, wb)


def wgrad(xa, dy, name, n_split=1, ts=512):
    S, K = xa.shape
    N = dy.shape[1]
    tn = N // n_split if n_split > 1 else min(512, N)
    n_steps = S // ts

    def body(x_ref, dy_ref, o_ref):
        @pl.when(pl.program_id(1) == 0)
        def _():
            o_ref[...] = jnp.zeros_like(o_ref)

        o_ref[...] += _dot_tn(x_ref[...], dy_ref[...])

    if n_split > 1:
        out_spec = pl.BlockSpec((None, K, tn), lambda j, k: (j, 0, 0))
        out_shape = jax.ShapeDtypeStruct((n_split, K, tn), F32)
    else:
        out_spec = pl.BlockSpec((K, tn), lambda j, k: (0, j))
        out_shape = jax.ShapeDtypeStruct((K, N), F32)
    return pl.pallas_call(
        body,
        grid=(N // tn, n_steps),
        in_specs=[pl.BlockSpec((ts, K), lambda j, k: (k, 0)), pl.BlockSpec((ts, tn), lambda j, k: (k, j))],
        out_specs=out_spec,
        out_shape=out_shape,
        compiler_params=_params("parallel", "arbitrary"),
        name=name,
    )(xa, dy)


def wgrad_into(xa, dy, name, buf, l, ts=512):
    S, K = xa.shape
    N = dy.shape[1]
    split = buf.ndim == 4
    tn = buf.shape[-1] if split else min(512, N)
    place = _into_layer(buf)

    def body(l_ref, buf_ref, x_ref, dy_ref, o_ref):
        @pl.when(pl.program_id(1) == 0)
        def _():
            o_ref[...] = jnp.zeros_like(o_ref)

        o_ref[...] += _dot_tn(x_ref[...], dy_ref[...])

    if split:
        out_spec = pl.BlockSpec((None, None, K, tn), lambda j, k, l: (l[0], j, 0, 0))
    else:
        out_spec = pl.BlockSpec((None, K, tn), lambda j, k, l: (l[0], 0, j))
    return pl.pallas_call(
        body,
        grid_spec=pltpu.PrefetchScalarGridSpec(
            num_scalar_prefetch=1,
            grid=(N // tn, S // ts),
            in_specs=[place["in_spec"], pl.BlockSpec((ts, K), lambda j, k, l: (k, 0)),
                      pl.BlockSpec((ts, tn), lambda j, k, l: (k, j))],
            out_specs=out_spec,
        ),
        out_shape=place["out_shape"],
        input_output_aliases=place["aliases"],
        compiler_params=_params("parallel", "arbitrary"),
        name=name,
    )(_layer_index(l), buf, xa, dy)


def _head_mean_matrix():
    h = np.arange(D_ATTN) // HEAD_DIM
    return jnp.asarray((h[:, None] == h[None, :]).astype(np.float32) / HEAD_DIM, BF16)


def _place_matrix(lane0):
    m = np.zeros((N_PIECES * LANES, D_HEADS), np.float32)
    for i in range(N_PIECES):
        for h in range(N_HEADS):
            m[i * LANES + h, h * LANES + lane0 + i] = 1.0
    return jnp.asarray(m, BF16)


def _tri(n, upper):
    r = np.arange(n)
    m = (r[None, :] >= r[:, None]) if upper else (r[None, :] <= r[:, None])
    return jnp.asarray(m.astype(np.float32), BF16)


def _lanes_in(lane, lo, n):
    return (lane >= lo) & (lane < lo + n)


def qk_prep(pa, pf, gq, gk, fb):
    S = pa.shape[0]
    bd = _head_mean_matrix()
    tri = _tri(TM, upper=False)
    place_q = _place_matrix(LANE_C)
    place_k = _place_matrix(LANE_ONE)

    def body(q_ref, k_ref, v_ref, pf_ref, gq_ref, gk_ref, fb_ref, bd_ref, tri_ref, pq_ref, pk_ref,
             qx_ref, kx_ref, vx_ref, carry):
        @pl.when(pl.program_id(0) == 0)
        def _():
            carry[...] = jnp.zeros_like(carry)

        def head_norm(x_ref, g_ref, scale):
            xf = x_ref[...].astype(F32)
            ms = _dot((xf * xf).astype(BF16), bd_ref[...])
            return xf * lax.rsqrt(ms + EPS) * g_ref[...] * scale

        qh = head_norm(q_ref, gq_ref, HEAD_DIM ** -0.5 * LOG2E)
        kh = head_norm(k_ref, gk_ref, 1.0)
        vf = v_ref[...].astype(F32)

        z = pf_ref[...] + fb_ref[...]
        logf = jnp.minimum(z, 0.0) - jnp.log(1.0 + jnp.exp(-jnp.abs(z)))
        hi, lo = _split_bf16(logf)
        c = _dot(tri_ref[...], hi) + _dot(tri_ref[...], lo) + carry[...]
        carry[...] += jnp.sum(hi.astype(F32) + lo.astype(F32), axis=0, keepdims=True)
        pieces = jnp.concatenate(_pieces(c * LOG2E), axis=1)
        cq = _dot(pieces, pq_ref[...])
        ck = _dot(pieces, pk_ref[...])

        lane = lax.broadcasted_iota(jnp.int32, (TM, LANES), 1)
        low = lane < HEAD_DIM
        ones_q = _lanes_in(lane, LANE_ONE, N_PIECES).astype(F32)
        ones_k = (_lanes_in(lane, LANE_C, N_PIECES) | _lanes_in(lane, LANE_LSE, N_PIECES)).astype(F32)
        ones_v = _lanes_in(lane, LANE_C, N_PIECES + 1).astype(F32)
        for h in range(N_HEADS):
            blk = slice((h // 2) * LANES, (h // 2 + 1) * LANES)
            head = (lambda a: pltpu.roll(a[:, blk], HEAD_DIM, 1)) if h % 2 else (lambda a: a[:, blk])
            mine = slice(h * LANES, (h + 1) * LANES)
            qx_ref[h] = jnp.where(low, head(qh), cq[:, mine] + ones_q).astype(BF16)
            kx_ref[h] = jnp.where(low, head(kh), ones_k - ck[:, mine]).astype(BF16)
            vx_ref[h] = jnp.where(low, head(vf), ones_v).astype(BF16)

    heads = pl.BlockSpec((N_HEADS, TM, LANES), lambda i: (0, i, 0))
    out = jax.ShapeDtypeStruct((N_HEADS, S, LANES), BF16)
    return pl.pallas_call(
        body,
        grid=(S // TM,),
        in_specs=[pl.BlockSpec((TM, D_ATTN), lambda i: (i, 0)), pl.BlockSpec((TM, D_ATTN), lambda i: (i, 1)),
                  pl.BlockSpec((TM, D_ATTN), lambda i: (i, 2)),
                  _rows(TM, D_F), _whole(gq), _whole(gk), _whole(fb), _whole(bd), _whole(tri), _whole(place_q), _whole(place_k)],
        out_specs=[heads, heads, heads],
        out_shape=[out, out, out],
        scratch_shapes=[pltpu.VMEM((1, D_F), F32)],
        compiler_params=_params("arbitrary"),
        name="qk_prep",
    )(pa, pa, pa, pf, gq, gk, fb, bd, tri, place_q, place_k)


def attn_fwd(qx, kx, vx):
    S = qx.shape[1]
    nq = S // TQ

    def body(q_ref, k_ref, v_ref, o_ref, qb_ref):
        i = pl.program_id(1)
        lane = lax.broadcasted_iota(jnp.int32, (TQ, LANES), 1)
        row = lax.broadcasted_iota(jnp.int32, (TQ, TK), 0)
        col = lax.broadcasted_iota(jnp.int32, (TQ, TK), 1)
        q = [q_ref[0], q_ref[1]]
        n_full = (i * TQ) // TK

        def step(kt, carry, masked):
            ks = pl.multiple_of(kt * TK, TK)
            new = []
            for j in range(2):
                m, acc = carry[j]
                z = _dot_nt(q[j], k_ref[j, pl.ds(ks, TK), :])
                if masked:
                    z = jnp.where(row + i * TQ >= col + ks, z, NEG)
                m_new = jnp.maximum(m, jnp.max(z, axis=1, keepdims=True))
                pr = jnp.exp2(z - m_new)
                acc = jnp.exp2(m - m_new) * acc + _dot(pr.astype(BF16), v_ref[j, pl.ds(ks, TK), :])
                new.append((m_new, acc))
            return tuple(new)

        init = tuple((jnp.full((TQ, 1), NEG, F32), jnp.zeros((TQ, LANES), F32)) for _ in range(2))
        carry = lax.fori_loop(0, n_full, functools.partial(step, masked=False), init)
        carry = step(n_full, carry, True)
        for j in range(2):
            m, acc = carry[j]
            l = jnp.sum(jnp.where(lane == LANE_C, acc, 0.0), axis=1, keepdims=True)
            o_ref[j] = acc / l
            n1, n2, n3 = _pieces(-(m + jnp.log(l) * LOG2E))
            qb_ref[j] = jnp.where(lane == LANE_LSE, n1,
                                  jnp.where(lane == LANE_LSE + 1, n2, jnp.where(lane == LANE_LSE + 2, n3, q[j])))

    pair_tile = pl.BlockSpec((2, TQ, LANES), lambda p, i: (p, i, 0))
    pair_all = pl.BlockSpec((2, S, LANES), lambda p, i: (p, 0, 0))
    return pl.pallas_call(
        body,
        grid=(N_HEADS // 2, nq),
        in_specs=[pair_tile, pair_all, pair_all],
        out_specs=[pair_tile, pair_tile],
        out_shape=[jax.ShapeDtypeStruct((N_HEADS, S, LANES), F32), jax.ShapeDtypeStruct((N_HEADS, S, LANES), BF16)],
        compiler_params=_params("parallel", "parallel"),
        name="attn_fwd",
    )(qx, kx, vx)


def attn_bwd(qxb, kx, vx, dox):
    S = qxb.shape[1]
    nq = S // QT

    def body(q_ref, k_ref, v_ref, do_ref, dq_ref, dk_ref, dv_ref):
        kt = pl.program_id(1)

        @pl.when(kt == 0)
        def _():
            dq_ref[...] = jnp.zeros_like(dq_ref)

        row = lax.broadcasted_iota(jnp.int32, (QT, KB), 0)
        col = lax.broadcasted_iota(jnp.int32, (QT, KB), 1)
        k = [k_ref[0], k_ref[1]]
        v = [v_ref[0], v_ref[1]]
        q_first = (kt * KB) // QT

        def step(qi, carry, masked):
            qs = pl.multiple_of(qi * QT, QT)
            new = []
            for j in range(2):
                dk, dv = carry[j]
                q = q_ref[j, pl.ds(qs, QT), :]
                dout = do_ref[j, pl.ds(qs, QT), :]
                z = _dot_nt(q, k[j])
                if masked:
                    z = jnp.where(row + qs >= col + kt * KB, z, NEG)
                pr = jnp.exp2(z)
                dv = dv + _dot_tn(pr.astype(BF16), dout)
                dsb = (pr * _dot_nt(dout, v[j])).astype(BF16)
                dk = dk + _dot_tn(dsb, q)
                dq_ref[j, pl.ds(qs, QT), :] += _dot(dsb, k[j])
                new.append((dk, dv))
            return tuple(new)

        zero = jnp.zeros((KB, LANES), F32)
        carry = step(q_first, ((zero, zero), (zero, zero)), True)
        carry = lax.fori_loop(q_first + 1, nq, functools.partial(step, masked=False), carry)
        for j in range(2):
            dk_ref[j] = carry[j][0]
            dv_ref[j] = carry[j][1]

    pair_tile = pl.BlockSpec((2, KB, LANES), lambda p, kt: (p, kt, 0))
    pair_all = pl.BlockSpec((2, S, LANES), lambda p, kt: (p, 0, 0))
    out = jax.ShapeDtypeStruct((N_HEADS, S, LANES), F32)
    return pl.pallas_call(
        body,
        grid=(N_HEADS // 2, S // KB),
        in_specs=[pair_all, pair_tile, pair_tile, pair_all],
        out_specs=[pair_all, pair_tile, pair_tile],
        out_shape=[out, out, out],
        compiler_params=_params("arbitrary", "arbitrary"),
        name="attn_bwd",
    )(qxb, kx, vx, dox)


def attn_bwd_post(pa, pf, dqx, dkx, dvx, gq, gk, fb):
    S = pa.shape[0]
    nt = S // TM
    bd = _head_mean_matrix()
    triu = _tri(TM, upper=True)
    rev = lambda i: nt - 1 - i

    def body(q_ref, k_ref, pf_ref, dqx_ref, dkx_ref, dvx_ref, gq_ref, gk_ref, fb_ref, bd_ref, triu_ref,
             dpa_ref, dpf_ref, dgq_ref, dgk_ref, dfb_ref, carry):
        @pl.when(pl.program_id(0) == 0)
        def _():
            carry[...] = jnp.zeros_like(carry)
            dgq_ref[...] = jnp.zeros_like(dgq_ref)
            dgk_ref[...] = jnp.zeros_like(dgk_ref)
            dfb_ref[...] = jnp.zeros_like(dfb_ref)

        lane = lax.broadcasted_iota(jnp.int32, (TM, LANES), 1)

        def columns(ref):
            return jnp.concatenate([jnp.where(lane < HEAD_DIM, ref[2 * p], pltpu.roll(ref[2 * p + 1], HEAD_DIM, 1))
                                    for p in range(N_HEADS // 2)], axis=1)

        def head_norm_bwd(x_ref, dy, g_ref, dg_ref):
            xf = x_ref[...].astype(F32)
            r = lax.rsqrt(_dot((xf * xf).astype(BF16), bd_ref[...]) + EPS)
            xhat = xf * r
            dg_ref[...] += jnp.sum(dy * xhat, axis=0, keepdims=True)
            gdy = dy * g_ref[...]
            return (r * (gdy - xhat * _dot((xhat * gdy).astype(BF16), bd_ref[...]))).astype(BF16)

        dpa_ref[:, 0:D_ATTN] = head_norm_bwd(q_ref, columns(dqx_ref) * HEAD_DIM ** -0.5, gq_ref, dgq_ref)
        dpa_ref[:, D_ATTN:2 * D_ATTN] = head_norm_bwd(k_ref, columns(dkx_ref) * LN2, gk_ref, dgk_ref)
        dpa_ref[:, 2 * D_ATTN:3 * D_ATTN] = columns(dvx_ref).astype(BF16)

        dc = jnp.zeros((TM, LANES), F32)
        for h in range(N_HEADS):
            both = jnp.where(lane == LANE_C, dqx_ref[h], 0.0) - jnp.where(lane == LANE_ONE, dkx_ref[h], 0.0)
            dc = jnp.where(lane == h, jnp.sum(both, axis=1, keepdims=True), dc)
        hi, lo = _split_bf16(dc)
        dlogf = _dot(triu_ref[...], hi) + _dot(triu_ref[...], lo) + carry[...]
        first = lax.broadcasted_iota(jnp.int32, (TM, D_F), 0) == 0
        carry[...] = jnp.sum(jnp.where(first, dlogf, 0.0), axis=0, keepdims=True)
        df = dlogf * _sigmoid(-(pf_ref[...] + fb_ref[...]))
        dfb_ref[...] += jnp.sum(df, axis=0, keepdims=True)
        dpf_ref[...] = df.astype(BF16)

    heads = pl.BlockSpec((N_HEADS, TM, LANES), lambda i: (0, rev(i), 0))
    return pl.pallas_call(
        body,
        grid=(nt,),
        in_specs=[pl.BlockSpec((TM, D_ATTN), lambda i: (rev(i), 0)), pl.BlockSpec((TM, D_ATTN), lambda i: (rev(i), 1)),
                  pl.BlockSpec((TM, D_F), lambda i: (rev(i), 0)), heads, heads, heads,
                  _whole(gq), _whole(gk), _whole(fb), _whole(bd), _whole(triu)],
        out_specs=[pl.BlockSpec((TM, D_QKV), lambda i: (rev(i), 0)), pl.BlockSpec((TM, D_F), lambda i: (rev(i), 0)),
                   pl.BlockSpec((1, D_ATTN), lambda i: (0, 0)), pl.BlockSpec((1, D_ATTN), lambda i: (0, 0)),
                   pl.BlockSpec((1, D_F), lambda i: (0, 0))],
        out_shape=[jax.ShapeDtypeStruct((S, D_QKV), BF16), jax.ShapeDtypeStruct((S, D_F), BF16),
                   jax.ShapeDtypeStruct((1, D_ATTN), F32), jax.ShapeDtypeStruct((1, D_ATTN), F32),
                   jax.ShapeDtypeStruct((1, D_F), F32)],
        scratch_shapes=[pltpu.VMEM((1, D_F), F32)],
        compiler_params=_params("arbitrary"),
        name="attn_bwd_post",
    )(pa, pa, pf, dqx, dkx, dvx, gq, gk, fb, bd, triu)


def _shift_down(ext, k):
    return pltpu.roll(ext, k, 0)[HALO:]


def _shift_up(ext, k, n):
    return pltpu.roll(ext, n + HALO - k, 0)[:n]


def _pool_lane_select(a2, a4, a8, a16, lane):
    return jnp.where(lane < 64, a2, jnp.where(lane < 128, a4, jnp.where(lane < 192, a8, a16)))


def _local_branches(o, pb, halo, have_prev, row0, wao, convw, wco, wpool, pscale):
    n = pb.shape[0]
    cx = pb[:, 0:D_CONV].astype(F32)
    cb = pb[:, D_CONV:2 * D_CONV].astype(F32)
    cc = pb[:, 2 * D_CONV:3 * D_CONV].astype(F32)
    px = pb[:, 3 * D_CONV:D_LOCAL].astype(F32)
    keep = have_prev.astype(F32)
    z = cc * cx
    z_ext = jnp.concatenate([halo[:, 2 * D_CONV:3 * D_CONV].astype(F32) * halo[:, 0:D_CONV].astype(F32) * keep, z], axis=0)
    z1 = _shift_down(z_ext, 1)
    z2 = _shift_down(z_ext, 2)
    conv = convw[0:1, :] * z2 + convw[1:2, :] * z1 + convw[2:3, :] * z
    cm = cb * conv

    u_ext = jnp.concatenate([halo[:, 3 * D_CONV:D_LOCAL].astype(F32) * keep, px], axis=0)
    s2 = u_ext + pltpu.roll(u_ext, 1, 0)
    s4 = s2 + pltpu.roll(s2, 2, 0)
    s8 = s4 + pltpu.roll(s4, 4, 0)
    s16 = s8 + pltpu.roll(s8, 8, 0)
    lane = lax.broadcasted_iota(jnp.int32, (n, D_POOL), 1)
    win = _pool_lane_select(2.0, 4.0, 8.0, 16.0, lane)
    t = (row0 + lax.broadcasted_iota(jnp.int32, (n, D_POOL), 0)).astype(F32)
    cnt = jnp.minimum(t + 1.0, win)
    feat = _pool_lane_select(s2[HALO:], s4[HALO:], s8[HALO:], s16[HALO:], lane) / cnt - px

    ya = _dot(o, wao)
    yc = _dot(cm.astype(BF16), wco)
    yp_pre = _dot(feat.astype(BF16), wpool)
    yp = yp_pre * pscale
    return dict(cx=cx, cb=cb, cc=cc, z=z, z1=z1, z2=z2, conv=conv, cm=cm, feat=feat, cnt=cnt, lane=lane,
                ya=ya, yc=yc, yp_pre=yp_pre, yp=yp)


def _halo_spec(tm, tile_of):
    per = tm // HALO
    return pl.BlockSpec((HALO, D_LOCAL), lambda i, *_: (jnp.maximum(tile_of(i) * per - 1, 0), 0))


def _heads_as_columns(ref):
    return jnp.concatenate([ref[h] for h in range(N_HEADS)], axis=1)


def mix_out_fwd(x, ox, pb, wao, convw, wco, wpool, pscale, wo_all, l):
    S = x.shape[0]
    tm = TM_MIX

    def body(l_ref, x_ref, o_ref, pb_ref, halo_ref, wao_ref, cw_ref, wco_ref, wp_ref, ps_ref, wo_ref, y_ref):
        i = pl.program_id(0)
        pb = pb_ref[...]
        o = _heads_as_columns(o_ref).astype(BF16)
        b = _local_branches(o, pb, halo_ref[...], i > 0, i * tm, wao_ref[...], cw_ref[...], wco_ref[...],
                            wp_ref[...], ps_ref[...])
        g0 = _sigmoid(pb[:, D_LOCAL:D_LOCAL + D_MODEL].astype(F32))
        g1 = _sigmoid(pb[:, D_LOCAL + D_MODEL:D_LOCAL + 2 * D_MODEL].astype(F32))
        g2 = _sigmoid(pb[:, D_LOCAL + 2 * D_MODEL:D_B].astype(F32))
        merged = g0 * b["ya"] + g1 * b["yc"] + g2 * b["yp"]
        y_ref[...] = x_ref[...] + _dot(merged.astype(BF16), wo_ref[...])

    return pl.pallas_call(
        body,
        grid_spec=pltpu.PrefetchScalarGridSpec(
            num_scalar_prefetch=1,
            grid=(S // tm,),
            in_specs=[_rows(tm, D_MODEL), pl.BlockSpec((N_HEADS, tm, LANES), lambda i, l: (0, i, 0)), _rows(tm, D_B),
                      _halo_spec(tm, lambda i: i), _whole(wao), _whole(convw), _whole(wco), _whole(wpool), _whole(pscale),
                      _layer((D_MODEL, D_MODEL))],
            out_specs=_rows(tm, D_MODEL),
        ),
        out_shape=jax.ShapeDtypeStruct((S, D_MODEL), F32),
        compiler_params=_params("parallel"),
        name="mix_out_fwd",
    )(_layer_index(l), x, ox, pb, pb, wao, convw, wco, wpool, pscale, wo_all)


def mix_out_bwd(dxm, ox, pb, wao, convw, wco, wpool, pscale, wo_all, l):
    S = dxm.shape[0]
    tm = TM_MIX
    nt = S // tm
    rev = lambda i: nt - 1 - i
    rows = lambda n: pl.BlockSpec((tm, n), lambda i, l: (rev(i), 0))
    heads = pl.BlockSpec((N_HEADS, tm, LANES), lambda i, l: (0, rev(i), 0))
    acc = lambda r, c: pl.BlockSpec((r, c), lambda i, l: (0, 0))

    def body(l_ref, dxm_ref, o_ref, pb_ref, halo_ref, wao_ref, cw_ref, wco_ref, wp_ref, ps_ref, wo_ref,
             dpb_ref, do_ref, dwo_ref, dwao_ref, dwco_ref, dwp_ref, dcw_ref, dps_ref, next_dconv, next_e):
        i = pl.program_id(0)
        r = rev(i)

        @pl.when(i == 0)
        def _():
            for ref in (dwo_ref, dwao_ref, dwco_ref, dwp_ref, dcw_ref, dps_ref, next_dconv, next_e):
                ref[...] = jnp.zeros_like(ref)

        pb = pb_ref[...]
        o = _heads_as_columns(o_ref).astype(BF16)
        cw = cw_ref[...]
        b = _local_branches(o, pb, halo_ref[...], r > 0, r * tm, wao_ref[...], cw, wco_ref[...], wp_ref[...], ps_ref[...])
        g0 = _sigmoid(pb[:, D_LOCAL:D_LOCAL + D_MODEL].astype(F32))
        g1 = _sigmoid(pb[:, D_LOCAL + D_MODEL:D_LOCAL + 2 * D_MODEL].astype(F32))
        g2 = _sigmoid(pb[:, D_LOCAL + 2 * D_MODEL:D_B].astype(F32))
        dxb = dxm_ref[...].astype(BF16)
        merged = g0 * b["ya"] + g1 * b["yc"] + g2 * b["yp"]
        dwo_ref[...] += _dot_tn(merged.astype(BF16), dxb)
        dmer = _dot_nt(dxb, wo_ref[...])
        dpb_ref[:, D_LOCAL:D_LOCAL + D_MODEL] = (dmer * b["ya"] * (g0 * (1.0 - g0))).astype(BF16)
        dpb_ref[:, D_LOCAL + D_MODEL:D_LOCAL + 2 * D_MODEL] = (dmer * b["yc"] * (g1 * (1.0 - g1))).astype(BF16)
        dpb_ref[:, D_LOCAL + 2 * D_MODEL:D_B] = (dmer * b["yp"] * (g2 * (1.0 - g2))).astype(BF16)

        dya = (dmer * g0).astype(BF16)
        dwao_ref[...] += _dot_tn(o, dya)
        da = _dot_nt(dya, wao_ref[...]).astype(BF16)
        lane = lax.broadcasted_iota(jnp.int32, (tm, LANES), 1)
        for h in range(N_HEADS):
            dah = da[:, h * LANES:(h + 1) * LANES]
            d1, d2, d3 = _pieces(-jnp.sum(dah.astype(F32) * o_ref[h], axis=1, keepdims=True))
            do_ref[h] = jnp.where(lane == LANE_C + 1, d1, jnp.where(lane == LANE_C + 2, d2,
                                                                  jnp.where(lane == LANE_C + 3, d3, dah)))

        dyc = (dmer * g1).astype(BF16)
        dwco_ref[...] += _dot_tn(b["cm"].astype(BF16), dyc)
        dcm = _dot_nt(dyc, wco_ref[...])
        dconv = dcm * b["cb"]
        dcw_ref[0:1, :] += jnp.sum(dconv * b["z2"], axis=0, keepdims=True)
        dcw_ref[1:2, :] += jnp.sum(dconv * b["z1"], axis=0, keepdims=True)
        dcw_ref[2:3, :] += jnp.sum(dconv * b["z"], axis=0, keepdims=True)
        d_ext = jnp.concatenate([dconv, next_dconv[...]], axis=0)
        dz = cw[2:3, :] * dconv + cw[1:2, :] * _shift_up(d_ext, 1, tm) + cw[0:1, :] * _shift_up(d_ext, 2, tm)
        next_dconv[...] = dconv[0:HALO]
        dpb_ref[:, 0:D_CONV] = (dz * b["cc"]).astype(BF16)
        dpb_ref[:, D_CONV:2 * D_CONV] = (dcm * b["conv"]).astype(BF16)
        dpb_ref[:, 2 * D_CONV:3 * D_CONV] = (dz * b["cx"]).astype(BF16)

        dyp = dmer * g2
        dps_ref[...] += jnp.sum(dyp * b["yp_pre"], axis=0, keepdims=True)
        dyps = (dyp * ps_ref[...]).astype(BF16)
        dwp_ref[...] += _dot_tn(b["feat"].astype(BF16), dyps)
        dfeat = _dot_nt(dyps, wp_ref[...])
        e = dfeat / b["cnt"]
        e_ext = jnp.concatenate([e, next_e[...]], axis=0)
        up = lambda a, k: pltpu.roll(a, tm + HALO - k, 0)
        f2 = e_ext + up(e_ext, 1)
        f4 = f2 + up(f2, 2)
        f8 = f4 + up(f4, 4)
        f16 = f8 + up(f8, 8)
        next_e[...] = e[0:HALO]
        dpb_ref[:, 3 * D_CONV:D_LOCAL] = (_pool_lane_select(f2[:tm], f4[:tm], f8[:tm], f16[:tm], b["lane"]) - dfeat).astype(BF16)

    return pl.pallas_call(
        body,
        grid_spec=pltpu.PrefetchScalarGridSpec(
            num_scalar_prefetch=1,
            grid=(nt,),
            in_specs=[rows(D_MODEL), heads, rows(D_B), _halo_spec(tm, rev),
                      _whole(wao), _whole(convw), _whole(wco), _whole(wpool), _whole(pscale), _layer((D_MODEL, D_MODEL))],
            out_specs=[rows(D_B), heads, acc(D_MODEL, D_MODEL), acc(D_HEADS, D_MODEL), acc(D_CONV, D_MODEL),
                       acc(D_POOL, D_MODEL), acc(8, D_CONV), acc(1, D_MODEL)],
            scratch_shapes=[pltpu.VMEM((HALO, D_CONV), F32), pltpu.VMEM((HALO, D_POOL), F32)],
        ),
        out_shape=[jax.ShapeDtypeStruct((S, D_B), BF16), jax.ShapeDtypeStruct((N_HEADS, S, LANES), BF16),
                   jax.ShapeDtypeStruct((D_MODEL, D_MODEL), F32), jax.ShapeDtypeStruct((D_HEADS, D_MODEL), F32),
                   jax.ShapeDtypeStruct((D_CONV, D_MODEL), F32), jax.ShapeDtypeStruct((D_POOL, D_MODEL), F32),
                   jax.ShapeDtypeStruct((8, D_CONV), F32), jax.ShapeDtypeStruct((1, D_MODEL), F32)],
        compiler_params=_params("arbitrary"),
        name="mix_out_bwd",
    )(_layer_index(l), dxm, ox, pb, pb, wao, convw, wco, wpool, pscale, wo_all)


FF_SHARD = 2 * D_FF // 4


def ffn_fwd(x, g, w1_all, w2_all, l):
    S = x.shape[0]

    def body(l_ref, x_ref, g_ref, w1_ref, w2_ref, y_ref, u_ref):
        xf = x_ref[...]
        r = lax.rsqrt(jnp.mean(xf * xf, axis=-1, keepdims=True) + EPS)
        h = (xf * r * g_ref[...]).astype(BF16)
        u = jnp.concatenate([_dot(h, w1_ref[j]) for j in range(4)], axis=1)
        u_ref[...] = u.astype(BF16)
        gt = u[:, 0:D_FF]
        act = gt * _sigmoid(gt) * u[:, D_FF:2 * D_FF]
        y_ref[...] = xf + _dot(act.astype(BF16), w2_ref[...])

    return pl.pallas_call(
        body,
        grid_spec=pltpu.PrefetchScalarGridSpec(
            num_scalar_prefetch=1,
            grid=(S // TM,),
            in_specs=[_rows(TM, D_MODEL), _whole(g), _layer((4, D_MODEL, FF_SHARD)), _layer((D_FF, D_MODEL))],
            out_specs=[_rows(TM, D_MODEL), _rows(TM, 2 * D_FF)],
        ),
        out_shape=[jax.ShapeDtypeStruct((S, D_MODEL), F32), jax.ShapeDtypeStruct((S, 2 * D_FF), BF16)],
        compiler_params=_params("parallel"),
        name="ffn_fwd",
    )(_layer_index(l), x, g, w1_all, w2_all)


def ffn_bwd(x, g, dy, u, w1_all, w2_all, l):
    S = x.shape[0]

    def body(l_ref, x_ref, g_ref, dy_ref, u_ref, w1_ref, w2_ref, dx_ref, du_ref, act_ref, h_ref, dg_ref):
        @pl.when(pl.program_id(0) == 0)
        def _():
            dg_ref[...] = jnp.zeros_like(dg_ref)

        dyf = dy_ref[...]
        dact = _dot_nt(dyf.astype(BF16), w2_ref[...])
        gt = u_ref[:, 0:D_FF].astype(F32)
        up = u_ref[:, D_FF:2 * D_FF].astype(F32)
        sg = _sigmoid(gt)
        silu = gt * sg
        act_ref[...] = (silu * up).astype(BF16)
        du_ref[:, 0:D_FF] = (dact * up * (sg * (1.0 + gt * (1.0 - sg)))).astype(BF16)
        du_ref[:, D_FF:2 * D_FF] = (dact * silu).astype(BF16)
        dh = _dot_nt(du_ref[:, 0:FF_SHARD], w1_ref[0])
        for j in range(1, 4):
            dh = dh + _dot_nt(du_ref[:, j * FF_SHARD:(j + 1) * FF_SHARD], w1_ref[j])
        xf = x_ref[...]
        r = lax.rsqrt(jnp.mean(xf * xf, axis=-1, keepdims=True) + EPS)
        xhat = xf * r
        h_ref[...] = (xhat * g_ref[...]).astype(BF16)
        dg_ref[...] += jnp.sum(dh * xhat, axis=0, keepdims=True)
        gdh = dh * g_ref[...]
        dx_ref[...] = dyf + r * (gdh - xhat * jnp.mean(xhat * gdh, axis=-1, keepdims=True))

    return pl.pallas_call(
        body,
        grid_spec=pltpu.PrefetchScalarGridSpec(
            num_scalar_prefetch=1,
            grid=(S // TM,),
            in_specs=[_rows(TM, D_MODEL), _whole(g), _rows(TM, D_MODEL), _rows(TM, 2 * D_FF),
                      _layer((4, D_MODEL, FF_SHARD)), _layer((D_FF, D_MODEL))],
            out_specs=[_rows(TM, D_MODEL), _rows(TM, 2 * D_FF), _rows(TM, D_FF), _rows(TM, D_MODEL),
                       pl.BlockSpec((1, D_MODEL), lambda i, l: (0, 0))],
        ),
        out_shape=[jax.ShapeDtypeStruct((S, D_MODEL), F32), jax.ShapeDtypeStruct((S, 2 * D_FF), BF16),
                   jax.ShapeDtypeStruct((S, D_FF), BF16), jax.ShapeDtypeStruct((S, D_MODEL), BF16),
                   jax.ShapeDtypeStruct((1, D_MODEL), F32)],
        compiler_params=_params("arbitrary"),
        name="ffn_bwd",
    )(_layer_index(l), x, g, dy, u, w1_all, w2_all)


def loss_head(y, target):
    S = y.shape[0]

    def body(y_ref, t_ref, loss_ref, dy_ref):
        @pl.when(pl.program_id(0) == 0)
        def _():
            loss_ref[0, 0] = 0.0

        err = y_ref[...] - t_ref[...]
        dy_ref[...] = err * (1.0 / D_MODEL)
        loss_ref[0, 0] += 0.5 * jnp.sum(jnp.mean(err * err, axis=-1))

    return pl.pallas_call(
        body,
        grid=(S // TM,),
        in_specs=[_rows(TM, D_MODEL), _rows(TM, D_MODEL)],
        out_specs=[pl.BlockSpec((1, 1), lambda i: (0, 0), memory_space=pltpu.SMEM), _rows(TM, D_MODEL)],
        out_shape=[jax.ShapeDtypeStruct((1, 1), F32), jax.ShapeDtypeStruct((S, D_MODEL), F32)],
        compiler_params=_params("arbitrary"),
        name="loss_head",
    )(y, target)


SPLIT = {
    "w_in": ((D_MODEL, D_IN), 1),
    "w_attn_out": ((D_ATTN, D_MODEL), 1),
    "w_conv_out": ((D_CONV, D_MODEL), 1),
    "pool_w": ((N_GROUPS, D_POOL // N_GROUPS, D_MODEL // N_GROUPS), 2),
    "w_o": ((D_MODEL, D_MODEL), 0),
    "w_ffn_in": ((D_MODEL, 2 * D_FF), 1),
    "w_ffn_out": ((D_FF, D_MODEL), 0),
}
SMALL = {"norm_mix_g": D_MODEL, "forget_b": N_HEADS, "q_norm_g": HEAD_DIM, "k_norm_g": HEAD_DIM,
         "pool_scale": D_MODEL, "norm_ffn_g": D_MODEL, "conv_w": CONV_K * D_CONV}
WEIGHTS = ["norm_mix_g", "w_in", "forget_b", "q_norm_g", "k_norm_g", "w_attn_out", "conv_w", "w_conv_out", "pool_w",
           "pool_scale", "w_o", "norm_ffn_g", "w_ffn_in", "w_ffn_out"]


def layer_fwd(x, full, li, small):
    n = full["w_o"].shape[0]
    row = lambda a: a.astype(F32).reshape(1, -1)
    wa, wf, wb = w_in_prep(full["w_in"], li)
    wao, wco, wpool, convw = branch_w_prep(full["w_attn_out"], full["w_conv_out"], full["pool_w"], full["conv_w"], li)
    w = dict(wa=wa, wf=wf, wb=wb, wao=wao, wco=wco, wpool=wpool, convw=convw,
             w_o=full["w_o"].reshape(n, D_MODEL, D_MODEL), w1=full["w_ffn_in"], w2=full["w_ffn_out"].reshape(n, D_FF, D_MODEL),
             g1=row(small["norm_mix_g"]), g2=row(small["norm_ffn_g"]), pscale=row(small["pool_scale"]),
             gq=row(jnp.tile(small["q_norm_g"], N_HEADS)), gk=row(jnp.tile(small["k_norm_g"], N_HEADS)),
             fb=row(jnp.pad(small["forget_b"], (0, D_F - N_HEADS))))
    h, pa, pf, pb = in_proj_fwd(x, w["g1"], wa, wf, wb)
    qx, kx, vx = qk_prep(pa, pf, w["gq"], w["gk"], w["fb"])
    ox, qxb = attn_fwd(qx, kx, vx)
    xm = mix_out_fwd(x, ox, pb, wao, convw, wco, wpool, w["pscale"], w["w_o"], li)
    y, u = ffn_fwd(xm, w["g2"], w["w1"], w["w2"], li)
    return y, w, dict(x=x, h=h, pa=pa, pf=pf, pb=pb, kx=kx, vx=vx, ox=ox, qxb=qxb, xm=xm, u=u)


def layer_bwd(dx, w, s, wi, gi, big):
    n = big["w_o"].shape[0]
    big = dict(big)
    dxm, du, act, h2, dg2 = ffn_bwd(s["xm"], w["g2"], dx, s["u"], w["w1"], w["w2"], wi)
    big["w_ffn_out"] = wgrad_into(act, dx.astype(BF16), "wgrad_ffn_out", big["w_ffn_out"].reshape(n, D_FF, D_MODEL),
                                  gi).reshape(big["w_ffn_out"].shape)
    big["w_ffn_in"] = wgrad_into(h2, du, "wgrad_ffn_in", big["w_ffn_in"], gi)
    dpb, dox, dwo, dwao, dwco, dwpool, dconvw, dpscale = mix_out_bwd(
        dxm, s["ox"], s["pb"], w["wao"], w["convw"], w["wco"], w["wpool"], w["pscale"], w["w_o"], wi)
    dqx, dkx, dvx = attn_bwd(s["qxb"], s["kx"], s["vx"], dox)
    dpa, dpf, dgq, dgk, dfb = attn_bwd_post(s["pa"], s["pf"], dqx, dkx, dvx, w["gq"], w["gk"], w["fb"])
    dx, dg1 = in_proj_bwd(s["x"], w["g1"], dxm, dpa, dpf, dpb, w["wa"], w["wf"], w["wb"])
    big["w_in"] = w_in_unprep(wgrad(s["h"], dpa, "wgrad_in_qkv"), wgrad(s["h"], dpf, "wgrad_in_f"),
                              wgrad(s["h"], dpb, "wgrad_in_b"), big["w_in"], gi)
    big["w_attn_out"], big["w_conv_out"], big["pool_w"], big["w_o"] = branch_g_place(
        dwao, dwco, dwpool, dwo, (big["w_attn_out"], big["w_conv_out"], big["pool_w"], big["w_o"]), gi)
    sm = dict(norm_mix_g=dg1[0], forget_b=dfb[0, 0:N_HEADS], q_norm_g=dgq.reshape(N_HEADS, HEAD_DIM).sum(0),
              k_norm_g=dgk.reshape(N_HEADS, HEAD_DIM).sum(0), pool_scale=dpscale[0], norm_ffn_g=dg2[0],
              conv_w=dconvw[0:CONV_K].reshape(-1))
    return dx, big, sm


def grad_buffers(full):
    return {n: lax.empty(full[n].shape, F32) for n in SPLIT}


def local_step(x, target, gathered, small):
    n_layers = small["norm_mix_g"].shape[0]
    done = []
    for l in range(n_layers):
        x, w, s = layer_fwd(x, gathered, l, {n: v[l] for n, v in small.items()})
        done.append((w, s))
    loss, dx = loss_head(x, target)
    big = grad_buffers(gathered)
    sm = [None] * n_layers
    for l in reversed(range(n_layers)):
        dx, big, sm[l] = layer_bwd(dx, *done[l], l, l, big)
    return loss, dx, big, {n: jnp.stack([g[n] for g in sm]) for n in SMALL}


def adamw(w, g, m, v, name):
    R, C = w.shape
    tm = 256 if R % 256 == 0 else R

    def body(w_ref, g_ref, m_ref, v_ref, d_ref, nm_ref, nv_ref):
        gr = g_ref[...]
        m_new = ADAM_B1 * m_ref[...] + (1.0 - ADAM_B1) * gr
        v_new = ADAM_B2 * v_ref[...] + (1.0 - ADAM_B2) * jnp.square(gr)
        nm_ref[...] = m_new
        nv_ref[...] = v_new
        m_hat = m_new / (1.0 - ADAM_B1 ** ADAM_STEP)
        v_hat = v_new / (1.0 - ADAM_B2 ** ADAM_STEP)
        d_ref[...] = -ADAM_LR * (m_hat / (jnp.sqrt(v_hat) + ADAM_EPS) + ADAM_WD * w_ref[...])

    spec = _rows(tm, C)
    out = jax.ShapeDtypeStruct((R, C), F32)
    return pl.pallas_call(
        body,
        grid=(R // tm,),
        in_specs=[spec] * 4,
        out_specs=[spec] * 3,
        out_shape=[out] * 3,
        compiler_params=_params("parallel"),
        name=name,
    )(w, g, m, v)


MESH = pl.DeviceIdType.MESH
HBM_REF = pl.BlockSpec(memory_space=pl.ANY)
N_CHIPS = 4
N_DEV = 8
FWD_GROUPS = ((0, 1), (1, 2), (2, 4))
BWD_GROUPS = ((0, 1), (1, 4))
SMALL_SHAPE = (128, LANES)


def _mesh_pos():
    return lax.axis_index("x"), lax.axis_index("y"), lax.axis_index("c")


def _other_chips(x, y):
    return [(1 - x, y), (x, 1 - y), (1 - x, 1 - y)]


def _remote(src, dst, send_sem, recv_sem, to):
    return pltpu.make_async_remote_copy(src_ref=src, dst_ref=dst, send_sem=send_sem, recv_sem=recv_sem,
                                        device_id=to, device_id_type=MESH)


def _row_tile(rows):
    for tm in (256, 176, 128):
        if rows % tm == 0:
            return tm
    return rows


def _as4(a):
    return a.reshape(a.shape[0], a.shape[1], -1, a.shape[-1])


def _core_rows(buf, c):
    R = buf.shape[2]
    if R % 2:
        return pl.ds(0, R), pl.ds(0, R), False
    return pl.ds(c * (R // 2), R // 2), pl.ds((1 - c) * (R // 2), R // 2), True


def place_shard(chip, w, lo, hi, dtype, name):
    w3 = w.reshape(w.shape[0], -1, w.shape[-1])
    _, R, C = w3.shape
    tm = _row_tile(R)

    def body(chip_ref, w_ref, o_ref):
        o_ref[...] = w_ref[...].astype(dtype)

    return pl.pallas_call(
        body,
        grid_spec=pltpu.PrefetchScalarGridSpec(
            num_scalar_prefetch=1,
            grid=(hi - lo, R // tm),
            in_specs=[pl.BlockSpec((None, tm, C), lambda l, i, chip: (lo + l, i, 0))],
            out_specs=pl.BlockSpec((None, None, tm, C), lambda l, i, chip: (l, chip[0], i, 0)),
        ),
        out_shape=jax.ShapeDtypeStruct((hi - lo, N_CHIPS, R, C), dtype),
        compiler_params=_params("parallel", "parallel"),
        name=name,
    )(chip, w3)


HBM_SPACE = pl.BlockSpec(memory_space=pltpu.HBM)
SEM_SPACE = pl.BlockSpec(memory_space=pltpu.SEMAPHORE)
IN_FLIGHT = pltpu.SideEffectType.DATAFLOW_SIDE_EFFECTING


def _sem_table(send_sems, recv_sems):
    return lambda t, j: (send_sems.at[t, j], recv_sems.at[t, j])


def _sem_per_peer(send_sems, recv_sems):
    return lambda t, j: (send_sems[j], recv_sems[j])


def _gather_ici(bufs, sem, sends=True, lands=True):
    x, y, c = _mesh_pos()
    me = 2 * x + y
    out, into = [], []
    for t, buf in enumerate(bufs):
        mine, _, _ = _core_rows(buf, c)
        part = lambda k, buf=buf, mine=mine: buf.at[pl.ds(0, buf.shape[0]), k, mine]
        for j, (px, py) in enumerate(_other_chips(x, y)):
            if sends:
                out.append(_remote(part(me), part(me), *sem(t, j), (px, py, c)))
            if lands:
                into.append(_remote(part(2 * px + py), part(2 * px + py), *sem(t, j), (px, py, c)))
    return out, into


def _gather_d2d(bufs, send_sems, recv_sems, first):
    x, y, c = _mesh_pos()
    sends, lands = [], []
    for t, buf in enumerate(bufs):
        mine, theirs, split = _core_rows(buf, c)
        if not split:
            continue
        for j, (px, py) in enumerate(_other_chips(x, y)):
            part = lambda rows, buf=buf, k=2 * px + py: buf.at[pl.ds(0, buf.shape[0]), k, rows]
            sems = (send_sems.at[t, first + j], recv_sems.at[t, first + j], (x, y, 1 - c))
            sends.append(_remote(part(mine), part(mine), *sems))
            lands.append(_remote(part(theirs), part(theirs), *sems))
    return sends, lands


def gather_shards(bufs):
    n = len(bufs)

    def body(*refs):
        outs = refs[n:2 * n]
        send_sems, recv_sems = refs[2 * n:]
        ici_out, ici_in = _gather_ici(outs, _sem_table(send_sems, recv_sems))
        d2d_out, d2d_in = _gather_d2d(outs, send_sems, recv_sems, 3)
        for cp in ici_out:
            cp.start()
        for cp in ici_in:
            cp.wait_recv()
        for cp in d2d_out:
            cp.start()
        for cp in d2d_in:
            cp.wait_recv()
        for cp in ici_out + d2d_out:
            cp.wait_send()

    return pl.pallas_call(
        body,
        in_specs=[HBM_REF] * n,
        out_specs=[HBM_REF] * n,
        out_shape=[jax.ShapeDtypeStruct(b.shape, b.dtype) for b in bufs],
        input_output_aliases={t: t for t in range(n)},
        scratch_shapes=[pltpu.SemaphoreType.DMA((n, 6)), pltpu.SemaphoreType.DMA((n, 6))],
        name="gather_shards",
    )(*bufs)


def gather_start(bufs):
    n = len(bufs)

    def body(*refs):
        send_sems, recv_sems = refs[n:n + 3], refs[n + 3:n + 6]
        outs = refs[n + 6:2 * n + 6]
        token = refs[2 * n + 6]
        for cp in _gather_ici(outs, _sem_per_peer(send_sems, recv_sems), lands=False)[0]:
            cp.start()
        token[...] = jnp.zeros_like(token)

    res = pl.pallas_call(
        body,
        in_specs=[HBM_SPACE] * n,
        out_specs=[SEM_SPACE] * 6 + [HBM_SPACE] * n + [pl.BlockSpec(memory_space=pltpu.VMEM)],
        out_shape=[pltpu.SemaphoreType.DMA(())] * 6
        + [pltpu.HBM(b.shape, b.dtype) for b in bufs] + [jax.ShapeDtypeStruct((8, LANES), F32)],
        input_output_aliases={t: t + 6 for t in range(n)},
        compiler_params=pltpu.CompilerParams(has_side_effects=IN_FLIGHT),
        name="gather_start",
    )(*[pltpu.with_memory_space_constraint(b, pltpu.HBM) for b in bufs])
    return list(res[0:3]), list(res[3:6]), list(res[6:n + 6]), res[n + 6]


def gather_wait(send_sems, recv_sems, bufs, after):
    n = len(bufs)

    def body(*refs):
        ins_send, ins_recv = refs[n:n + 3], refs[n + 3:n + 6]
        outs = refs[n + 7:]
        sends, lands = _gather_ici(outs, _sem_per_peer(ins_send, ins_recv))
        for cp in lands:
            cp.wait_recv()
        for cp in sends:
            cp.wait_send()

    return list(pl.pallas_call(
        body,
        in_specs=[HBM_SPACE] * n + [SEM_SPACE] * 6 + [HBM_REF],
        out_specs=[HBM_SPACE] * n,
        out_shape=[pltpu.HBM(b.shape, b.dtype) for b in bufs],
        input_output_aliases={t: t for t in range(n)},
        compiler_params=pltpu.CompilerParams(has_side_effects=IN_FLIGHT),
        name="gather_wait",
    )(*bufs, *send_sems, *recv_sems, after))


def gather_forward(bufs):
    n = len(bufs)

    def body(*refs):
        outs = refs[n:2 * n]
        send_sems, recv_sems = refs[2 * n:]
        sends, lands = _gather_d2d(outs, send_sems, recv_sems, 0)
        for cp in sends:
            cp.start()
        for cp in lands:
            cp.wait_recv()
        for cp in sends:
            cp.wait_send()

    return list(pl.pallas_call(
        body,
        in_specs=[HBM_REF] * n,
        out_specs=[HBM_REF] * n,
        out_shape=[jax.ShapeDtypeStruct(b.shape, b.dtype) for b in bufs],
        input_output_aliases={t: t for t in range(n)},
        scratch_shapes=[pltpu.SemaphoreType.DMA((n, 3)), pltpu.SemaphoreType.DMA((n, 3))],
        name="gather_forward",
    )(*bufs))


def pair_exchange(grads):
    n = len(grads)

    def body(*refs):
        ins, outs = refs[:n], refs[n:2 * n]
        send_sems, recv_sems = refs[2 * n:]
        x, y, c = _mesh_pos()
        cps = []
        for t in range(n):
            _, theirs, _ = _core_rows(ins[t], c)
            src = ins[t].at[pl.ds(0, ins[t].shape[0]), pl.ds(0, N_CHIPS), theirs]
            cps.append(_remote(src, outs[t], send_sems.at[t], recv_sems.at[t], (x, y, 1 - c)))
        for cp in cps:
            cp.start()
        for cp in cps:
            cp.wait()

    return pl.pallas_call(
        body,
        in_specs=[HBM_REF] * n,
        out_specs=[HBM_REF] * n,
        out_shape=[jax.ShapeDtypeStruct(g.shape[:2] + (g.shape[2] // 2, g.shape[3]), g.dtype) for g in grads],
        scratch_shapes=[pltpu.SemaphoreType.DMA((n,)), pltpu.SemaphoreType.DMA((n,))],
        name="pair_exchange",
    )(*grads)


def pair_sum(core, g, t, name):
    n, _, half, C = t.shape
    tm = _row_tile(half)
    per = half // tm

    def body(c_ref, g_ref, t_ref, o_ref):
        o_ref[...] = (g_ref[...] + t_ref[...]).astype(BF16)

    tile = pl.BlockSpec((None, tm, C), lambda a, i, c: (a, i, 0))
    out = pl.pallas_call(
        body,
        grid_spec=pltpu.PrefetchScalarGridSpec(
            num_scalar_prefetch=1,
            grid=(n * N_CHIPS, per),
            in_specs=[pl.BlockSpec((None, tm, C), lambda a, i, c: (a, per * c[0] + i, 0)), tile],
            out_specs=tile,
        ),
        out_shape=jax.ShapeDtypeStruct((n * N_CHIPS, half, C), BF16),
        compiler_params=_params("parallel", "parallel"),
        name=name,
    )(core, g.reshape(n * N_CHIPS, 2 * half, C), t.reshape(n * N_CHIPS, half, C))
    return out.reshape(t.shape)


def _chip_copies(sums, recv, sem):
    x, y, c = _mesh_pos()
    cps = []
    for t in range(len(sums)):
        for j, (px, py) in enumerate(_other_chips(x, y)):
            src = sums[t].at[pl.ds(0, sums[t].shape[0]), 2 * px + py]
            cps.append(_remote(src, recv[t].at[j], *sem(t, j), (px, py, c)))
    return cps


def _recv_shape(s):
    return (N_CHIPS - 1, s.shape[0]) + s.shape[2:]


def chip_exchange(sums):
    n = len(sums)

    def body(*refs):
        ins, outs = refs[:n], refs[n:2 * n]
        cps = _chip_copies(ins, outs, _sem_table(*refs[2 * n:]))
        for cp in cps:
            cp.start()
        for cp in cps:
            cp.wait()

    return list(pl.pallas_call(
        body,
        in_specs=[HBM_REF] * n,
        out_specs=[HBM_REF] * n,
        out_shape=[jax.ShapeDtypeStruct(_recv_shape(s), s.dtype) for s in sums],
        scratch_shapes=[pltpu.SemaphoreType.DMA((n, 3)), pltpu.SemaphoreType.DMA((n, 3))],
        name="chip_exchange",
    )(*sums))


def chip_exchange_start(sums):
    n = len(sums)
    lands = [lax.empty(_recv_shape(s), s.dtype) for s in sums]

    def body(*refs):
        send_sems, recv_sems = refs[2 * n:2 * n + 3], refs[2 * n + 3:2 * n + 6]
        outs = refs[2 * n + 6:4 * n + 6]
        token = refs[4 * n + 6]
        for cp in _chip_copies(outs[:n], outs[n:2 * n], _sem_per_peer(send_sems, recv_sems)):
            cp.start()
        token[...] = jnp.zeros_like(token)

    res = pl.pallas_call(
        body,
        in_specs=[HBM_SPACE] * (2 * n),
        out_specs=[SEM_SPACE] * 6 + [HBM_SPACE] * (2 * n) + [pl.BlockSpec(memory_space=pltpu.VMEM)],
        out_shape=[pltpu.SemaphoreType.DMA(())] * 6 + [pltpu.HBM(a.shape, a.dtype) for a in list(sums) + lands]
        + [jax.ShapeDtypeStruct((8, LANES), F32)],
        input_output_aliases={t: t + 6 for t in range(2 * n)},
        compiler_params=pltpu.CompilerParams(has_side_effects=IN_FLIGHT),
        name="chip_exchange_start",
    )(*[pltpu.with_memory_space_constraint(a, pltpu.HBM) for a in list(sums) + lands])
    return list(res[0:3]), list(res[3:6]), list(res[6:n + 6]), list(res[n + 6:2 * n + 6]), res[2 * n + 6]


def chip_exchange_wait(send_sems, recv_sems, sums, lands, after):
    n = len(sums)

    def body(*refs):
        ins_send, ins_recv = refs[2 * n:2 * n + 3], refs[2 * n + 3:2 * n + 6]
        outs = refs[2 * n + 7:]
        for cp in _chip_copies(outs[:n], outs[n:], _sem_per_peer(ins_send, ins_recv)):
            cp.wait()

    res = pl.pallas_call(
        body,
        in_specs=[HBM_SPACE] * (2 * n) + [SEM_SPACE] * 6 + [HBM_REF],
        out_specs=[HBM_SPACE] * (2 * n),
        out_shape=[pltpu.HBM(a.shape, a.dtype) for a in list(sums) + list(lands)],
        input_output_aliases={t: t for t in range(2 * n)},
        compiler_params=pltpu.CompilerParams(has_side_effects=IN_FLIGHT),
        name="chip_exchange_wait",
    )(*sums, *lands, *send_sems, *recv_sems, after)
    return list(res[:n]), list(res[n:])


def chip_sum_into(core, chip, recv, sums, total, lo, name):
    _, n, half, C = recv.shape
    tm = _row_tile(half)
    per = half // tm

    def body(c_ref, k_ref, r_ref, s_ref, t_ref, o_ref):
        acc = s_ref[...].astype(F32)
        for j in range(N_CHIPS - 1):
            acc = acc + r_ref[j].astype(F32)
        o_ref[...] = acc

    return pl.pallas_call(
        body,
        grid_spec=pltpu.PrefetchScalarGridSpec(
            num_scalar_prefetch=2,
            grid=(n, per),
            in_specs=[pl.BlockSpec((N_CHIPS - 1, None, tm, C), lambda a, i, c, k: (0, a, i, 0)),
                      pl.BlockSpec((None, None, tm, C), lambda a, i, c, k: (a, k[0], i, 0)),
                      HBM_REF],
            out_specs=pl.BlockSpec((None, tm, C), lambda a, i, c, k: (lo + a, per * c[0] + i, 0)),
        ),
        out_shape=jax.ShapeDtypeStruct(total.shape, F32),
        input_output_aliases={4: 0},
        compiler_params=_params("parallel", "parallel"),
        name=name,
    )(core, chip, recv, sums, total)


def sibling_share(totals):
    n = len(totals)

    def body(*refs):
        outs = refs[n:2 * n]
        send_sems, recv_sems = refs[2 * n:]
        x, y, c = _mesh_pos()
        half = lambda t, h: outs[t].at[pl.ds(0, DEPTH), pl.ds(h * (outs[t].shape[1] // 2), outs[t].shape[1] // 2)]
        sent = [_remote(half(t, c), half(t, c), send_sems.at[t], recv_sems.at[t], (x, y, 1 - c)) for t in range(n)]
        for cp in sent:
            cp.start()
        for t in range(n):
            _remote(half(t, 1 - c), half(t, 1 - c), send_sems.at[t], recv_sems.at[t], (x, y, 1 - c)).wait_recv()
        for cp in sent:
            cp.wait_send()

    return pl.pallas_call(
        body,
        in_specs=[HBM_REF] * n,
        out_specs=[HBM_REF] * n,
        out_shape=[jax.ShapeDtypeStruct(t.shape, t.dtype) for t in totals],
        input_output_aliases={t: t for t in range(n)},
        scratch_shapes=[pltpu.SemaphoreType.DMA((n,)), pltpu.SemaphoreType.DMA((n,))],
        name="sibling_share",
    )(*totals)


def small_allgather(small):
    def body(s_ref, a_ref, send_sems, recv_sems, local_sem):
        x, y, c = _mesh_pos()
        me = 4 * x + 2 * y + c
        own = pltpu.make_async_copy(s_ref, a_ref.at[me], local_sem)
        own.start()
        sent = []
        for k in range(1, N_DEV):
            peer = (x ^ (k >> 2), y ^ ((k >> 1) & 1), c ^ (k & 1))
            cp = _remote(s_ref, a_ref.at[me], send_sems.at[k - 1], recv_sems.at[k - 1], peer)
            cp.start()
            sent.append(cp)
        for k in range(1, N_DEV):
            px, py, pc = x ^ (k >> 2), y ^ ((k >> 1) & 1), c ^ (k & 1)
            _remote(s_ref, a_ref.at[4 * px + 2 * py + pc], send_sems.at[k - 1], recv_sems.at[k - 1], (px, py, pc)).wait_recv()
        for cp in sent:
            cp.wait_send()
        own.wait()

    return pl.pallas_call(
        body,
        in_specs=[HBM_REF],
        out_specs=HBM_REF,
        out_shape=jax.ShapeDtypeStruct((N_DEV,) + SMALL_SHAPE, small.dtype),
        scratch_shapes=[pltpu.SemaphoreType.DMA((N_DEV - 1,)), pltpu.SemaphoreType.DMA((N_DEV - 1,)), pltpu.SemaphoreType.DMA],
        name="small_allgather",
    )(small)


def small_sum(blocks):
    def body(a_ref, o_ref):
        acc = a_ref[0]
        for d in range(1, N_DEV):
            acc = acc + a_ref[d]
        o_ref[...] = acc

    return pl.pallas_call(
        body,
        in_specs=[pl.BlockSpec(memory_space=pltpu.VMEM)],
        out_specs=pl.BlockSpec(memory_space=pltpu.VMEM),
        out_shape=jax.ShapeDtypeStruct(SMALL_SHAPE, F32),
        name="small_sum",
    )(blocks)


def pack_small(grads, loss):
    flat = jnp.concatenate([grads[n].reshape(-1) for n in SMALL] + [loss.reshape(-1)])
    size = SMALL_SHAPE[0] * SMALL_SHAPE[1]
    return jnp.pad(flat, (0, size - flat.shape[0])).reshape(SMALL_SHAPE)


def unpack_small(packed):
    flat = packed.reshape(-1)
    out, off = {}, 0
    for n, size in SMALL.items():
        out[n] = flat[off:off + DEPTH * size].reshape(DEPTH, size)
        off += DEPTH * size
    return out, flat[off]


def kernel(x, norm_mix_g, w_in, forget_b, q_norm_g, k_norm_g, w_attn_out, conv_w, w_conv_out, pool_w, pool_scale, w_o, norm_ffn_g, w_ffn_in, w_ffn_out, loss_target, m_norm_mix_g, m_w_in, m_forget_b, m_q_norm_g, m_k_norm_g, m_w_attn_out, m_conv_w, m_w_conv_out, m_pool_w, m_pool_scale, m_w_o, m_norm_ffn_g, m_w_ffn_in, m_w_ffn_out, v_norm_mix_g, v_w_in, v_forget_b, v_q_norm_g, v_k_norm_g, v_w_attn_out, v_conv_w, v_w_conv_out, v_pool_w, v_pool_scale, v_w_o, v_norm_ffn_g, v_w_ffn_in, v_w_ffn_out):
    given = dict(locals())
    weights = {n: given[n] for n in WEIGHTS}

    core = lax.axis_index("c").astype(jnp.int32)
    chip = (2 * lax.axis_index("x") + lax.axis_index("y")).astype(jnp.int32)

    core, chip = core.reshape(1), chip.reshape(1)
    names = list(SPLIT) + ["conv_w"]

    def placed(lo, hi):
        return [place_shard(chip, weights[n], lo, hi, F32 if n == "conv_w" else BF16, "place_" + n) for n in names]

    def as_weights(bufs):
        return {n: b.reshape(b.shape[:2] + weights[n].shape[1:]) for n, b in zip(names, bufs)}

    xs = x[0]
    done = [None] * DEPTH
    full = as_weights(gather_shards(placed(*FWD_GROUPS[0])))
    for g, (lo, hi) in enumerate(FWD_GROUPS):
        last = g + 1 == len(FWD_GROUPS)
        if not last:
            send_sems, recv_sems, in_flight, token = gather_start(placed(*FWD_GROUPS[g + 1]))
        for l in range(lo, hi):
            small = {n: weights[n][l] for n in SMALL if n != "conv_w"}
            if l == lo and not last:
                small["norm_mix_g"] = small["norm_mix_g"] + token[0, 0]
            xs, w, s = layer_fwd(xs, full, l - lo, small)
            done[l] = (w, s, l - lo)
        if not last:
            full = as_weights(gather_forward(gather_wait(send_sems, recv_sems, in_flight, xs)))
    loss, dx = loss_head(xs, loss_target[0])

    small_grads = [None] * DEPTH
    totals = [lax.empty((DEPTH,) + _as4(weights[n][None]).shape[2:], F32) for n in SPLIT]

    def add_chips(totals, recv, sums, lo):
        return [chip_sum_into(core, chip, r, s, t, lo, "chip_sum_" + n) for n, r, s, t in zip(SPLIT, recv, sums, totals)]

    pending = None
    for g in reversed(range(len(BWD_GROUPS))):
        lo, hi = BWD_GROUPS[g]
        big = {n: lax.empty((hi - lo, N_CHIPS) + weights[n].shape[1:], F32) for n in SPLIT}
        for l in reversed(range(lo, hi)):
            w, s, wi = done[l]
            if pending is not None and l == hi - 1:
                w = dict(w, g2=w["g2"] + pending[-1][0:1, 0:1])
            dx, big, small_grads[l] = layer_bwd(dx, w, s, wi, l - lo, big)
        if pending is not None:
            p_lo, send_sems, recv_sems, p_sums, p_lands, _ = pending
            p_sums, p_recv = chip_exchange_wait(send_sems, recv_sems, p_sums, p_lands, dx)
            totals = add_chips(totals, p_recv, p_sums, p_lo)
        grads = [_as4(big[n]) for n in SPLIT]
        sums = [pair_sum(core, a, t, "pair_sum_" + n) for n, a, t in zip(SPLIT, grads, pair_exchange(grads))]
        if g > 0:
            pending = (lo, *chip_exchange_start(sums))
        else:
            totals = add_chips(totals, chip_exchange(sums), sums, lo)
    reduced = {n: t.reshape(weights[n].shape) for n, t in zip(SPLIT, sibling_share(totals))}

    small_grads = {n: jnp.stack([g[n] for g in small_grads]) for n in SMALL}
    small_total, loss_sum = unpack_small(small_sum(small_allgather(pack_small(small_grads, loss))))
    chip = chip[0]
    cols = D_CONV // N_CHIPS
    small_total["conv_w"] = lax.dynamic_slice_in_dim(small_total["conv_w"].reshape(DEPTH, CONV_K, D_CONV), chip * cols, cols, axis=2)
    for n in SMALL:
        reduced[n] = small_total[n].reshape(weights[n].shape)

    deltas, new_m, new_v = {}, {}, {}
    for n in WEIGHTS:
        w = weights[n]
        flat = (-1, w.shape[-1])
        d, nm, nv = adamw(w.reshape(flat), reduced[n].reshape(flat), given["m_" + n].reshape(flat),
                          given["v_" + n].reshape(flat), "adamw_" + n)
        deltas[n], new_m[n], new_v[n] = d.reshape(w.shape), nm.reshape(w.shape), nv.reshape(w.shape)
    return (loss_sum, dx[None], *[reduced[n] for n in WEIGHTS], *[deltas[n] for n in WEIGHTS],
            *[new_m[n] for n in WEIGHTS], *[new_v[n] for n in WEIGHTS])
```

```python
import functools

import numpy as np
import jax
import jax.numpy as jnp
from jax import lax
from jax.experimental import pallas as pl
from jax.experimental.pallas import tpu as pltpu

F32 = jnp.float32
BF16 = jnp.bfloat16

D_MODEL = 1024
DEPTH = 4
HEAD_DIM = 64
N_HEADS = 8
D_ATTN = 512
D_CONV = 256
D_POOL = 256
D_FF = 2816
D_IN = 5640
CONV_K = 3
POOL_WINDOWS = (2, 4, 8, 16)
N_GROUPS = len(POOL_WINDOWS)
EPS = 1e-6
ADAM_LR, ADAM_B1, ADAM_B2, ADAM_EPS, ADAM_WD, ADAM_STEP = 0.001, 0.9, 0.999, 1e-08, 0.01, 10

D_QKV = 3 * D_ATTN
D_F = 128
D_B = 3 * D_CONV + D_POOL + 3 * D_MODEL
D_LOCAL = 3 * D_CONV + D_POOL

LANES = 128
D_HEADS = N_HEADS * LANES
HALO = 16
VMEM_LIMIT = 56 * 1024 * 1024
NEG = -1e30
LOG2E = 1.4426950408889634
LN2 = 0.6931471805599453

TM = 256
TM_MIX = 128
TQ = 512
TK = 1024
KB = 512
QT = 1024

LANE_C = 64
LANE_ONE = 67
LANE_LSE = 70
N_PIECES = 3


def _dot(a, b):
    return jnp.dot(a, b, preferred_element_type=F32)


def _dot_nt(a, b):
    return lax.dot_general(a, b, (((1,), (1,)), ((), ())), preferred_element_type=F32)


def _dot_tn(a, b):
    return lax.dot_general(a, b, (((0,), (0,)), ((), ())), preferred_element_type=F32)


def _params(*sem):
    return pltpu.CompilerParams(dimension_semantics=sem, vmem_limit_bytes=VMEM_LIMIT)


def _rows(tm, n):
    return pl.BlockSpec((tm, n), lambda i, *_: (i, 0))


def _whole(a):
    nd = a.ndim
    return pl.BlockSpec(a.shape, lambda *_: (0,) * nd)


def _layer(shape):
    nd = len(shape)
    return pl.BlockSpec((None,) + tuple(shape), lambda *a: (a[-1][0],) + (0,) * nd)


def _layer_index(l):
    return jnp.full((1,), l, jnp.int32)


def _split_bf16(x):
    hi = x.astype(BF16)
    lo = (x - hi.astype(F32)).astype(BF16)
    return hi, lo


def _pieces(x):
    p1 = x.astype(BF16)
    r1 = x - p1.astype(F32)
    p2 = r1.astype(BF16)
    p3 = (r1 - p2.astype(F32)).astype(BF16)
    return p1, p2, p3


def _sigmoid(x):
    return 1.0 / (1.0 + jnp.exp(-x))


def w_in_prep(win, l):
    tr = 256
    n = D_IN // 4
    v_rest = D_QKV - n
    b0 = v_rest + N_HEADS

    def body(s0, s1, s2, s3, wa_ref, wf_ref, wb_ref):
        b = s1[...]
        wa_ref[...] = jnp.concatenate([s0[...], b[:, 0:v_rest]], axis=1)
        wf_ref[...] = jnp.concatenate([b[:, v_rest:b0], jnp.zeros((tr, D_F - N_HEADS), b.dtype)], axis=1)
        wb_ref[...] = jnp.concatenate([b[:, b0:n], s2[...], s3[...]], axis=1)

    shard = lambda j: pl.BlockSpec((None, None, tr, n), lambda i: (l, j, i, 0))
    return pl.pallas_call(
        body,
        grid=(D_MODEL // tr,),
        in_specs=[shard(0), shard(1), shard(2), shard(3)],
        out_specs=[_rows(tr, D_QKV), _rows(tr, D_F), _rows(tr, D_B)],
        out_shape=[jax.ShapeDtypeStruct((D_MODEL, D_QKV), win.dtype), jax.ShapeDtypeStruct((D_MODEL, D_F), win.dtype),
                   jax.ShapeDtypeStruct((D_MODEL, D_B), win.dtype)],
        compiler_params=_params("parallel"),
        name="w_in_prep",
    )(win, win, win, win)


def _into_layer(buf):
    return dict(in_spec=HBM_REF, out_shape=jax.ShapeDtypeStruct(buf.shape, buf.dtype), aliases={1: 0})


def w_in_unprep(dwa, dwf, dwb, buf, l):
    tr = 256
    n = D_IN // 4
    v_rest = D_QKV - n
    b1 = n - v_rest - N_HEADS
    place = _into_layer(buf)

    def body(l_ref, buf_ref, a_ref, f_ref, b_ref, o_ref):
        a = a_ref[...]
        b = b_ref[...]
        o_ref[0] = a[:, 0:n]
        o_ref[1] = jnp.concatenate([a[:, n:D_QKV], f_ref[:, 0:N_HEADS], b[:, 0:b1]], axis=1)
        o_ref[2] = b[:, b1:b1 + n]
        o_ref[3] = b[:, b1 + n:D_B]

    return pl.pallas_call(
        body,
        grid_spec=pltpu.PrefetchScalarGridSpec(
            num_scalar_prefetch=1,
            grid=(D_MODEL // tr,),
            in_specs=[place["in_spec"], _rows(tr, D_QKV), _rows(tr, D_F), _rows(tr, D_B)],
            out_specs=pl.BlockSpec((None, 4, tr, n), lambda i, l: (l[0], 0, i, 0)),
        ),
        out_shape=place["out_shape"],
        input_output_aliases=place["aliases"],
        compiler_params=_params("parallel"),
        name="w_in_unprep",
    )(_layer_index(l), buf, dwa, dwf, dwb)


def branch_w_prep(wao, wco, pw, cw, l):
    gd = D_POOL // N_GROUPS
    od = D_MODEL // N_GROUPS

    def body(wao_ref, wco_ref, pw_ref, cw_ref, ao_ref, co_ref, po_ref, co8_ref):
        a = jnp.concatenate([wao_ref[j] for j in range(4)], axis=1)
        gap = jnp.zeros((LANES - HEAD_DIM, D_MODEL), a.dtype)
        ao_ref[...] = jnp.concatenate(
            [blk for h in range(N_HEADS) for blk in (a[h * HEAD_DIM:(h + 1) * HEAD_DIM], gap)], axis=0)
        co_ref[...] = jnp.concatenate([wco_ref[j] for j in range(4)], axis=1)
        zero = jnp.zeros((gd, od), a.dtype)
        po_ref[...] = jnp.concatenate(
            [jnp.concatenate([jnp.concatenate([pw_ref[j, g] for j in range(4)], axis=1) if g2 == g else zero
                              for g2 in range(N_GROUPS)], axis=1) for g in range(N_GROUPS)], axis=0)
        co8_ref[...] = jnp.zeros_like(co8_ref)
        co8_ref[0:CONV_K, :] = jnp.concatenate([cw_ref[j] for j in range(4)], axis=1)

    sel = lambda *shape: pl.BlockSpec((None,) + shape, lambda i: (l,) + (0,) * len(shape))
    return pl.pallas_call(
        body,
        grid=(1,),
        in_specs=[sel(4, D_ATTN, D_MODEL // 4), sel(4, D_CONV, D_MODEL // 4), sel(4, N_GROUPS, gd, od // 4),
                  sel(4, CONV_K, D_CONV // 4)],
        out_specs=[pl.BlockSpec((D_HEADS, D_MODEL), lambda i: (0, 0)), pl.BlockSpec((D_CONV, D_MODEL), lambda i: (0, 0)),
                   pl.BlockSpec((D_POOL, D_MODEL), lambda i: (0, 0)), pl.BlockSpec((8, D_CONV), lambda i: (0, 0))],
        out_shape=[jax.ShapeDtypeStruct((D_HEADS, D_MODEL), BF16), jax.ShapeDtypeStruct((D_CONV, D_MODEL), BF16),
                   jax.ShapeDtypeStruct((D_POOL, D_MODEL), BF16), jax.ShapeDtypeStruct((8, D_CONV), F32)],
        name="branch_w_prep",
    )(wao, wco, pw, cw)


def branch_g_place(dwao, dwco, dwpool, dwo, bufs, l):
    gd = D_POOL // N_GROUPS
    od = D_MODEL // N_GROUPS
    q = D_MODEL // 4

    def body(l_ref, b0, b1, b2, b3, a_ref, c_ref, p_ref, w_ref, ao_ref, co_ref, po_ref, wo_ref):
        a = jnp.concatenate([a_ref[h * LANES:h * LANES + HEAD_DIM, :] for h in range(N_HEADS)], axis=0)
        c = c_ref[...]
        p = p_ref[...]
        for j in range(4):
            ao_ref[j] = a[:, j * q:(j + 1) * q]
            co_ref[j] = c[:, j * q:(j + 1) * q]
            wo_ref[j] = w_ref[j * q:(j + 1) * q, :]
            for g in range(N_GROUPS):
                c0 = g * od + j * (od // 4)
                po_ref[j, g] = p[g * gd:(g + 1) * gd, c0:c0 + od // 4]

    whole = lambda a: pl.BlockSpec(a.shape, lambda i, l: (0,) * a.ndim)
    layer = lambda b: pl.BlockSpec((None,) + b.shape[1:], lambda i, l: (l[0],) + (0,) * (b.ndim - 1))
    return pl.pallas_call(
        body,
        grid_spec=pltpu.PrefetchScalarGridSpec(
            num_scalar_prefetch=1,
            grid=(1,),
            in_specs=[HBM_REF] * 4 + [whole(dwao), whole(dwco), whole(dwpool), whole(dwo)],
            out_specs=[layer(b) for b in bufs],
        ),
        out_shape=[jax.ShapeDtypeStruct(b.shape, b.dtype) for b in bufs],
        input_output_aliases={1: 0, 2: 1, 3: 2, 4: 3},
        compiler_params=_params("arbitrary"),
        name="branch_g_place",
    )(_layer_index(l), *bufs, dwao, dwco, dwpool, dwo)


def in_proj_fwd(x, g, wa, wf, wb):
    S = x.shape[0]

    def body(x_ref, g_ref, wa_ref, wf_ref, wb_ref, h_ref, pa_ref, pf_ref, pb_ref):
        xf = x_ref[...]
        r = lax.rsqrt(jnp.mean(xf * xf, axis=-1, keepdims=True) + EPS)
        h = (xf * r * g_ref[...]).astype(BF16)
        h_ref[...] = h
        pa_ref[...] = _dot(h, wa_ref[...]).astype(BF16)
        pf_ref[...] = _dot(h, wf_ref[...])
        pb_ref[...] = _dot(h, wb_ref[...]).astype(BF16)

    return pl.pallas_call(
        body,
        grid=(S // TM,),
        in_specs=[_rows(TM, D_MODEL), _whole(g), _whole(wa), _whole(wf), _whole(wb)],
        out_specs=[_rows(TM, D_MODEL), _rows(TM, D_QKV), _rows(TM, D_F), _rows(TM, D_B)],
        out_shape=[
            jax.ShapeDtypeStruct((S, D_MODEL), BF16),
            jax.ShapeDtypeStruct((S, D_QKV), BF16),
            jax.ShapeDtypeStruct((S, D_F), F32),
            jax.ShapeDtypeStruct((S, D_B), BF16),
        ],
        compiler_params=_params("parallel"),
        name="in_proj_fwd",
    )(x, g, wa, wf, wb)


def in_proj_bwd(x, g, dxm, dpa, dpf, dpb, wa, wf, wb):
    S = x.shape[0]

    def body(x_ref, g_ref, dxm_ref, dpa_ref, dpf_ref, dpb_ref, wa_ref, wf_ref, wb_ref, dx_ref, dg_ref):
        @pl.when(pl.program_id(0) == 0)
        def _():
            dg_ref[...] = jnp.zeros_like(dg_ref)

        dh = _dot_nt(dpa_ref[...], wa_ref[...]) + _dot_nt(dpf_ref[...], wf_ref[...]) + _dot_nt(dpb_ref[...], wb_ref[...])
        xf = x_ref[...]
        r = lax.rsqrt(jnp.mean(xf * xf, axis=-1, keepdims=True) + EPS)
        xhat = xf * r
        dg_ref[...] += jnp.sum(dh * xhat, axis=0, keepdims=True)
        gdh = dh * g_ref[...]
        dx_ref[...] = dxm_ref[...] + r * (gdh - xhat * jnp.mean(xhat * gdh, axis=-1, keepdims=True))

    return pl.pallas_call(
        body,
        grid=(S // TM,),
        in_specs=[_rows(TM, D_MODEL), _whole(g), _rows(TM, D_MODEL), _rows(TM, D_QKV), _rows(TM, D_F), _rows(TM, D_B),
                  _whole(wa), _whole(wf), _whole(wb)],
        out_specs=[_rows(TM, D_MODEL), pl.BlockSpec((1, D_MODEL), lambda i: (0, 0))],
        out_shape=[jax.ShapeDtypeStruct((S, D_MODEL), F32), jax.ShapeDtypeStruct((1, D_MODEL), F32)],
        compiler_params=_params("arbitrary"),
        name="in_proj_bwd",
    )(x, g, dxm, dpa, dpf, dpb, wa, wf, wb)


def wgrad(xa, dy, name, n_split=1, ts=512):
    S, K = xa.shape
    N = dy.shape[1]
    tn = N // n_split if n_split > 1 else min(512, N)
    n_steps = S // ts

    def body(x_ref, dy_ref, o_ref):
        @pl.when(pl.program_id(1) == 0)
        def _():
            o_ref[...] = jnp.zeros_like(o_ref)

        o_ref[...] += _dot_tn(x_ref[...], dy_ref[...])

    if n_split > 1:
        out_spec = pl.BlockSpec((None, K, tn), lambda j, k: (j, 0, 0))
        out_shape = jax.ShapeDtypeStruct((n_split, K, tn), F32)
    else:
        out_spec = pl.BlockSpec((K, tn), lambda j, k: (0, j))
        out_shape = jax.ShapeDtypeStruct((K, N), F32)
    return pl.pallas_call(
        body,
        grid=(N // tn, n_steps),
        in_specs=[pl.BlockSpec((ts, K), lambda j, k: (k, 0)), pl.BlockSpec((ts, tn), lambda j, k: (k, j))],
        out_specs=out_spec,
        out_shape=out_shape,
        compiler_params=_params("parallel", "arbitrary"),
        name=name,
    )(xa, dy)


def wgrad_into(xa, dy, name, buf, l, ts=512):
    S, K = xa.shape
    N = dy.shape[1]
    split = buf.ndim == 4
    tn = buf.shape[-1] if split else min(512, N)
    place = _into_layer(buf)

    def body(l_ref, buf_ref, x_ref, dy_ref, o_ref):
        @pl.when(pl.program_id(1) == 0)
        def _():
            o_ref[...] = jnp.zeros_like(o_ref)

        o_ref[...] += _dot_tn(x_ref[...], dy_ref[...])

    if split:
        out_spec = pl.BlockSpec((None, None, K, tn), lambda j, k, l: (l[0], j, 0, 0))
    else:
        out_spec = pl.BlockSpec((None, K, tn), lambda j, k, l: (l[0], 0, j))
    return pl.pallas_call(
        body,
        grid_spec=pltpu.PrefetchScalarGridSpec(
            num_scalar_prefetch=1,
            grid=(N // tn, S // ts),
            in_specs=[place["in_spec"], pl.BlockSpec((ts, K), lambda j, k, l: (k, 0)),
                      pl.BlockSpec((ts, tn), lambda j, k, l: (k, j))],
            out_specs=out_spec,
        ),
        out_shape=place["out_shape"],
        input_output_aliases=place["aliases"],
        compiler_params=_params("parallel", "arbitrary"),
        name=name,
    )(_layer_index(l), buf, xa, dy)


def _head_mean_matrix():
    h = np.arange(D_ATTN) // HEAD_DIM
    return jnp.asarray((h[:, None] == h[None, :]).astype(np.float32) / HEAD_DIM, BF16)


def _place_matrix(lane0):
    m = np.zeros((N_PIECES * LANES, D_HEADS), np.float32)
    for i in range(N_PIECES):
        for h in range(N_HEADS):
            m[i * LANES + h, h * LANES + lane0 + i] = 1.0
    return jnp.asarray(m, BF16)


def _tri(n, upper):
    r = np.arange(n)
    m = (r[None, :] >= r[:, None]) if upper else (r[None, :] <= r[:, None])
    return jnp.asarray(m.astype(np.float32), BF16)


def _lanes_in(lane, lo, n):
    return (lane >= lo) & (lane < lo + n)


def qk_prep(pa, pf, gq, gk, fb):
    S = pa.shape[0]
    bd = _head_mean_matrix()
    tri = _tri(TM, upper=False)
    place_q = _place_matrix(LANE_C)
    place_k = _place_matrix(LANE_ONE)

    def body(q_ref, k_ref, v_ref, pf_ref, gq_ref, gk_ref, fb_ref, bd_ref, tri_ref, pq_ref, pk_ref,
             qx_ref, kx_ref, vx_ref, carry):
        @pl.when(pl.program_id(0) == 0)
        def _():
            carry[...] = jnp.zeros_like(carry)

        def head_norm(x_ref, g_ref, scale):
            xf = x_ref[...].astype(F32)
            ms = _dot((xf * xf).astype(BF16), bd_ref[...])
            return xf * lax.rsqrt(ms + EPS) * g_ref[...] * scale

        qh = head_norm(q_ref, gq_ref, HEAD_DIM ** -0.5 * LOG2E)
        kh = head_norm(k_ref, gk_ref, 1.0)
        vf = v_ref[...].astype(F32)

        z = pf_ref[...] + fb_ref[...]
        logf = jnp.minimum(z, 0.0) - jnp.log(1.0 + jnp.exp(-jnp.abs(z)))
        hi, lo = _split_bf16(logf)
        c = _dot(tri_ref[...], hi) + _dot(tri_ref[...], lo) + carry[...]
        carry[...] += jnp.sum(hi.astype(F32) + lo.astype(F32), axis=0, keepdims=True)
        pieces = jnp.concatenate(_pieces(c * LOG2E), axis=1)
        cq = _dot(pieces, pq_ref[...])
        ck = _dot(pieces, pk_ref[...])

        lane = lax.broadcasted_iota(jnp.int32, (TM, LANES), 1)
        low = lane < HEAD_DIM
        ones_q = _lanes_in(lane, LANE_ONE, N_PIECES).astype(F32)
        ones_k = (_lanes_in(lane, LANE_C, N_PIECES) | _lanes_in(lane, LANE_LSE, N_PIECES)).astype(F32)
        ones_v = _lanes_in(lane, LANE_C, N_PIECES + 1).astype(F32)
        for h in range(N_HEADS):
            blk = slice((h // 2) * LANES, (h // 2 + 1) * LANES)
            head = (lambda a: pltpu.roll(a[:, blk], HEAD_DIM, 1)) if h % 2 else (lambda a: a[:, blk])
            mine = slice(h * LANES, (h + 1) * LANES)
            qx_ref[h] = jnp.where(low, head(qh), cq[:, mine] + ones_q).astype(BF16)
            kx_ref[h] = jnp.where(low, head(kh), ones_k - ck[:, mine]).astype(BF16)
            vx_ref[h] = jnp.where(low, head(vf), ones_v).astype(BF16)

    heads = pl.BlockSpec((N_HEADS, TM, LANES), lambda i: (0, i, 0))
    out = jax.ShapeDtypeStruct((N_HEADS, S, LANES), BF16)
    return pl.pallas_call(
        body,
        grid=(S // TM,),
        in_specs=[pl.BlockSpec((TM, D_ATTN), lambda i: (i, 0)), pl.BlockSpec((TM, D_ATTN), lambda i: (i, 1)),
                  pl.BlockSpec((TM, D_ATTN), lambda i: (i, 2)),
                  _rows(TM, D_F), _whole(gq), _whole(gk), _whole(fb), _whole(bd), _whole(tri), _whole(place_q), _whole(place_k)],
        out_specs=[heads, heads, heads],
        out_shape=[out, out, out],
        scratch_shapes=[pltpu.VMEM((1, D_F), F32)],
        compiler_params=_params("arbitrary"),
        name="qk_prep",
    )(pa, pa, pa, pf, gq, gk, fb, bd, tri, place_q, place_k)


def attn_fwd(qx, kx, vx):
    S = qx.shape[1]
    nq = S // TQ

    def body(q_ref, k_ref, v_ref, o_ref, qb_ref):
        i = pl.program_id(1)
        lane = lax.broadcasted_iota(jnp.int32, (TQ, LANES), 1)
        row = lax.broadcasted_iota(jnp.int32, (TQ, TK), 0)
        col = lax.broadcasted_iota(jnp.int32, (TQ, TK), 1)
        q = [q_ref[0], q_ref[1]]
        n_full = (i * TQ) // TK

        def step(kt, carry, masked):
            ks = pl.multiple_of(kt * TK, TK)
            new = []
            for j in range(2):
                m, acc = carry[j]
                z = _dot_nt(q[j], k_ref[j, pl.ds(ks, TK), :])
                if masked:
                    z = jnp.where(row + i * TQ >= col + ks, z, NEG)
                m_new = jnp.maximum(m, jnp.max(z, axis=1, keepdims=True))
                pr = jnp.exp2(z - m_new)
                acc = jnp.exp2(m - m_new) * acc + _dot(pr.astype(BF16), v_ref[j, pl.ds(ks, TK), :])
                new.append((m_new, acc))
            return tuple(new)

        init = tuple((jnp.full((TQ, 1), NEG, F32), jnp.zeros((TQ, LANES), F32)) for _ in range(2))
        carry = lax.fori_loop(0, n_full, functools.partial(step, masked=False), init)
        carry = step(n_full, carry, True)
        for j in range(2):
            m, acc = carry[j]
            l = jnp.sum(jnp.where(lane == LANE_C, acc, 0.0), axis=1, keepdims=True)
            o_ref[j] = acc / l
            n1, n2, n3 = _pieces(-(m + jnp.log(l) * LOG2E))
            qb_ref[j] = jnp.where(lane == LANE_LSE, n1,
                                  jnp.where(lane == LANE_LSE + 1, n2, jnp.where(lane == LANE_LSE + 2, n3, q[j])))

    pair_tile = pl.BlockSpec((2, TQ, LANES), lambda p, i: (p, i, 0))
    pair_all = pl.BlockSpec((2, S, LANES), lambda p, i: (p, 0, 0))
    return pl.pallas_call(
        body,
        grid=(N_HEADS // 2, nq),
        in_specs=[pair_tile, pair_all, pair_all],
        out_specs=[pair_tile, pair_tile],
        out_shape=[jax.ShapeDtypeStruct((N_HEADS, S, LANES), F32), jax.ShapeDtypeStruct((N_HEADS, S, LANES), BF16)],
        compiler_params=_params("parallel", "parallel"),
        name="attn_fwd",
    )(qx, kx, vx)


def attn_bwd(qxb, kx, vx, dox):
    S = qxb.shape[1]
    nq = S // QT

    def body(q_ref, k_ref, v_ref, do_ref, dq_ref, dk_ref, dv_ref):
        kt = pl.program_id(1)

        @pl.when(kt == 0)
        def _():
            dq_ref[...] = jnp.zeros_like(dq_ref)

        row = lax.broadcasted_iota(jnp.int32, (QT, KB), 0)
        col = lax.broadcasted_iota(jnp.int32, (QT, KB), 1)
        k = [k_ref[0], k_ref[1]]
        v = [v_ref[0], v_ref[1]]
        q_first = (kt * KB) // QT

        def step(qi, carry, masked):
            qs = pl.multiple_of(qi * QT, QT)
            new = []
            for j in range(2):
                dk, dv = carry[j]
                q = q_ref[j, pl.ds(qs, QT), :]
                dout = do_ref[j, pl.ds(qs, QT), :]
                z = _dot_nt(q, k[j])
                if masked:
                    z = jnp.where(row + qs >= col + kt * KB, z, NEG)
                pr = jnp.exp2(z)
                dv = dv + _dot_tn(pr.astype(BF16), dout)
                dsb = (pr * _dot_nt(dout, v[j])).astype(BF16)
                dk = dk + _dot_tn(dsb, q)
                dq_ref[j, pl.ds(qs, QT), :] += _dot(dsb, k[j])
                new.append((dk, dv))
            return tuple(new)

        zero = jnp.zeros((KB, LANES), F32)
        carry = step(q_first, ((zero, zero), (zero, zero)), True)
        carry = lax.fori_loop(q_first + 1, nq, functools.partial(step, masked=False), carry)
        for j in range(2):
            dk_ref[j] = carry[j][0]
            dv_ref[j] = carry[j][1]

    pair_tile = pl.BlockSpec((2, KB, LANES), lambda p, kt: (p, kt, 0))
    pair_all = pl.BlockSpec((2, S, LANES), lambda p, kt: (p, 0, 0))
    out = jax.ShapeDtypeStruct((N_HEADS, S, LANES), F32)
    return pl.pallas_call(
        body,
        grid=(N_HEADS // 2, S // KB),
        in_specs=[pair_all, pair_tile, pair_tile, pair_all],
        out_specs=[pair_all, pair_tile, pair_tile],
        out_shape=[out, out, out],
        compiler_params=_params("arbitrary", "arbitrary"),
        name="attn_bwd",
    )(qxb, kx, vx, dox)


def attn_bwd_post(pa, pf, dqx, dkx, dvx, gq, gk, fb):
    S = pa.shape[0]
    nt = S // TM
    bd = _head_mean_matrix()
    triu = _tri(TM, upper=True)
    rev = lambda i: nt - 1 - i

    def body(q_ref, k_ref, pf_ref, dqx_ref, dkx_ref, dvx_ref, gq_ref, gk_ref, fb_ref, bd_ref, triu_ref,
             dpa_ref, dpf_ref, dgq_ref, dgk_ref, dfb_ref, carry):
        @pl.when(pl.program_id(0) == 0)
        def _():
            carry[...] = jnp.zeros_like(carry)
            dgq_ref[...] = jnp.zeros_like(dgq_ref)
            dgk_ref[...] = jnp.zeros_like(dgk_ref)
            dfb_ref[...] = jnp.zeros_like(dfb_ref)

        lane = lax.broadcasted_iota(jnp.int32, (TM, LANES), 1)

        def columns(ref):
            return jnp.concatenate([jnp.where(lane < HEAD_DIM, ref[2 * p], pltpu.roll(ref[2 * p + 1], HEAD_DIM, 1))
                                    for p in range(N_HEADS // 2)], axis=1)

        def head_norm_bwd(x_ref, dy, g_ref, dg_ref):
            xf = x_ref[...].astype(F32)
            r = lax.rsqrt(_dot((xf * xf).astype(BF16), bd_ref[...]) + EPS)
            xhat = xf * r
            dg_ref[...] += jnp.sum(dy * xhat, axis=0, keepdims=True)
            gdy = dy * g_ref[...]
            return (r * (gdy - xhat * _dot((xhat * gdy).astype(BF16), bd_ref[...]))).astype(BF16)

        dpa_ref[:, 0:D_ATTN] = head_norm_bwd(q_ref, columns(dqx_ref) * HEAD_DIM ** -0.5, gq_ref, dgq_ref)
        dpa_ref[:, D_ATTN:2 * D_ATTN] = head_norm_bwd(k_ref, columns(dkx_ref) * LN2, gk_ref, dgk_ref)
        dpa_ref[:, 2 * D_ATTN:3 * D_ATTN] = columns(dvx_ref).astype(BF16)

        dc = jnp.zeros((TM, LANES), F32)
        for h in range(N_HEADS):
            both = jnp.where(lane == LANE_C, dqx_ref[h], 0.0) - jnp.where(lane == LANE_ONE, dkx_ref[h], 0.0)
            dc = jnp.where(lane == h, jnp.sum(both, axis=1, keepdims=True), dc)
        hi, lo = _split_bf16(dc)
        dlogf = _dot(triu_ref[...], hi) + _dot(triu_ref[...], lo) + carry[...]
        first = lax.broadcasted_iota(jnp.int32, (TM, D_F), 0) == 0
        carry[...] = jnp.sum(jnp.where(first, dlogf, 0.0), axis=0, keepdims=True)
        df = dlogf * _sigmoid(-(pf_ref[...] + fb_ref[...]))
        dfb_ref[...] += jnp.sum(df, axis=0, keepdims=True)
        dpf_ref[...] = df.astype(BF16)

    heads = pl.BlockSpec((N_HEADS, TM, LANES), lambda i: (0, rev(i), 0))
    return pl.pallas_call(
        body,
        grid=(nt,),
        in_specs=[pl.BlockSpec((TM, D_ATTN), lambda i: (rev(i), 0)), pl.BlockSpec((TM, D_ATTN), lambda i: (rev(i), 1)),
                  pl.BlockSpec((TM, D_F), lambda i: (rev(i), 0)), heads, heads, heads,
                  _whole(gq), _whole(gk), _whole(fb), _whole(bd), _whole(triu)],
        out_specs=[pl.BlockSpec((TM, D_QKV), lambda i: (rev(i), 0)), pl.BlockSpec((TM, D_F), lambda i: (rev(i), 0)),
                   pl.BlockSpec((1, D_ATTN), lambda i: (0, 0)), pl.BlockSpec((1, D_ATTN), lambda i: (0, 0)),
                   pl.BlockSpec((1, D_F), lambda i: (0, 0))],
        out_shape=[jax.ShapeDtypeStruct((S, D_QKV), BF16), jax.ShapeDtypeStruct((S, D_F), BF16),
                   jax.ShapeDtypeStruct((1, D_ATTN), F32), jax.ShapeDtypeStruct((1, D_ATTN), F32),
                   jax.ShapeDtypeStruct((1, D_F), F32)],
        scratch_shapes=[pltpu.VMEM((1, D_F), F32)],
        compiler_params=_params("arbitrary"),
        name="attn_bwd_post",
    )(pa, pa, pf, dqx, dkx, dvx, gq, gk, fb, bd, triu)


def _shift_down(ext, k):
    return pltpu.roll(ext, k, 0)[HALO:]


def _shift_up(ext, k, n):
    return pltpu.roll(ext, n + HALO - k, 0)[:n]


def _pool_lane_select(a2, a4, a8, a16, lane):
    return jnp.where(lane < 64, a2, jnp.where(lane < 128, a4, jnp.where(lane < 192, a8, a16)))


def _local_branches(o, pb, halo, have_prev, row0, wao, convw, wco, wpool, pscale):
    n = pb.shape[0]
    cx = pb[:, 0:D_CONV].astype(F32)
    cb = pb[:, D_CONV:2 * D_CONV].astype(F32)
    cc = pb[:, 2 * D_CONV:3 * D_CONV].astype(F32)
    px = pb[:, 3 * D_CONV:D_LOCAL].astype(F32)
    keep = have_prev.astype(F32)
    z = cc * cx
    z_ext = jnp.concatenate([halo[:, 2 * D_CONV:3 * D_CONV].astype(F32) * halo[:, 0:D_CONV].astype(F32) * keep, z], axis=0)
    z1 = _shift_down(z_ext, 1)
    z2 = _shift_down(z_ext, 2)
    conv = convw[0:1, :] * z2 + convw[1:2, :] * z1 + convw[2:3, :] * z
    cm = cb * conv

    u_ext = jnp.concatenate([halo[:, 3 * D_CONV:D_LOCAL].astype(F32) * keep, px], axis=0)
    s2 = u_ext + pltpu.roll(u_ext, 1, 0)
    s4 = s2 + pltpu.roll(s2, 2, 0)
    s8 = s4 + pltpu.roll(s4, 4, 0)
    s16 = s8 + pltpu.roll(s8, 8, 0)
    lane = lax.broadcasted_iota(jnp.int32, (n, D_POOL), 1)
    win = _pool_lane_select(2.0, 4.0, 8.0, 16.0, lane)
    t = (row0 + lax.broadcasted_iota(jnp.int32, (n, D_POOL), 0)).astype(F32)
    cnt = jnp.minimum(t + 1.0, win)
    feat = _pool_lane_select(s2[HALO:], s4[HALO:], s8[HALO:], s16[HALO:], lane) / cnt - px

    ya = _dot(o, wao)
    yc = _dot(cm.astype(BF16), wco)
    yp_pre = _dot(feat.astype(BF16), wpool)
    yp = yp_pre * pscale
    return dict(cx=cx, cb=cb, cc=cc, z=z, z1=z1, z2=z2, conv=conv, cm=cm, feat=feat, cnt=cnt, lane=lane,
                ya=ya, yc=yc, yp_pre=yp_pre, yp=yp)


def _halo_spec(tm, tile_of):
    per = tm // HALO
    return pl.BlockSpec((HALO, D_LOCAL), lambda i, *_: (jnp.maximum(tile_of(i) * per - 1, 0), 0))


def _heads_as_columns(ref):
    return jnp.concatenate([ref[h] for h in range(N_HEADS)], axis=1)


def mix_out_fwd(x, ox, pb, wao, convw, wco, wpool, pscale, wo_all, l):
    S = x.shape[0]
    tm = TM_MIX

    def body(l_ref, x_ref, o_ref, pb_ref, halo_ref, wao_ref, cw_ref, wco_ref, wp_ref, ps_ref, wo_ref, y_ref):
        i = pl.program_id(0)
        pb = pb_ref[...]
        o = _heads_as_columns(o_ref).astype(BF16)
        b = _local_branches(o, pb, halo_ref[...], i > 0, i * tm, wao_ref[...], cw_ref[...], wco_ref[...],
                            wp_ref[...], ps_ref[...])
        g0 = _sigmoid(pb[:, D_LOCAL:D_LOCAL + D_MODEL].astype(F32))
        g1 = _sigmoid(pb[:, D_LOCAL + D_MODEL:D_LOCAL + 2 * D_MODEL].astype(F32))
        g2 = _sigmoid(pb[:, D_LOCAL + 2 * D_MODEL:D_B].astype(F32))
        merged = g0 * b["ya"] + g1 * b["yc"] + g2 * b["yp"]
        y_ref[...] = x_ref[...] + _dot(merged.astype(BF16), wo_ref[...])

    return pl.pallas_call(
        body,
        grid_spec=pltpu.PrefetchScalarGridSpec(
            num_scalar_prefetch=1,
            grid=(S // tm,),
            in_specs=[_rows(tm, D_MODEL), pl.BlockSpec((N_HEADS, tm, LANES), lambda i, l: (0, i, 0)), _rows(tm, D_B),
                      _halo_spec(tm, lambda i: i), _whole(wao), _whole(convw), _whole(wco), _whole(wpool), _whole(pscale),
                      _layer((D_MODEL, D_MODEL))],
            out_specs=_rows(tm, D_MODEL),
        ),
        out_shape=jax.ShapeDtypeStruct((S, D_MODEL), F32),
        compiler_params=_params("parallel"),
        name="mix_out_fwd",
    )(_layer_index(l), x, ox, pb, pb, wao, convw, wco, wpool, pscale, wo_all)


def mix_out_bwd(dxm, ox, pb, wao, convw, wco, wpool, pscale, wo_all, l):
    S = dxm.shape[0]
    tm = TM_MIX
    nt = S // tm
    rev = lambda i: nt - 1 - i
    rows = lambda n: pl.BlockSpec((tm, n), lambda i, l: (rev(i), 0))
    heads = pl.BlockSpec((N_HEADS, tm, LANES), lambda i, l: (0, rev(i), 0))
    acc = lambda r, c: pl.BlockSpec((r, c), lambda i, l: (0, 0))

    def body(l_ref, dxm_ref, o_ref, pb_ref, halo_ref, wao_ref, cw_ref, wco_ref, wp_ref, ps_ref, wo_ref,
             dpb_ref, do_ref, dwo_ref, dwao_ref, dwco_ref, dwp_ref, dcw_ref, dps_ref, next_dconv, next_e):
        i = pl.program_id(0)
        r = rev(i)

        @pl.when(i == 0)
        def _():
            for ref in (dwo_ref, dwao_ref, dwco_ref, dwp_ref, dcw_ref, dps_ref, next_dconv, next_e):
                ref[...] = jnp.zeros_like(ref)

        pb = pb_ref[...]
        o = _heads_as_columns(o_ref).astype(BF16)
        cw = cw_ref[...]
        b = _local_branches(o, pb, halo_ref[...], r > 0, r * tm, wao_ref[...], cw, wco_ref[...], wp_ref[...], ps_ref[...])
        g0 = _sigmoid(pb[:, D_LOCAL:D_LOCAL + D_MODEL].astype(F32))
        g1 = _sigmoid(pb[:, D_LOCAL + D_MODEL:D_LOCAL + 2 * D_MODEL].astype(F32))
        g2 = _sigmoid(pb[:, D_LOCAL + 2 * D_MODEL:D_B].astype(F32))
        dxb = dxm_ref[...].astype(BF16)
        merged = g0 * b["ya"] + g1 * b["yc"] + g2 * b["yp"]
        dwo_ref[...] += _dot_tn(merged.astype(BF16), dxb)
        dmer = _dot_nt(dxb, wo_ref[...])
        dpb_ref[:, D_LOCAL:D_LOCAL + D_MODEL] = (dmer * b["ya"] * (g0 * (1.0 - g0))).astype(BF16)
        dpb_ref[:, D_LOCAL + D_MODEL:D_LOCAL + 2 * D_MODEL] = (dmer * b["yc"] * (g1 * (1.0 - g1))).astype(BF16)
        dpb_ref[:, D_LOCAL + 2 * D_MODEL:D_B] = (dmer * b["yp"] * (g2 * (1.0 - g2))).astype(BF16)

        dya = (dmer * g0).astype(BF16)
        dwao_ref[...] += _dot_tn(o, dya)
        da = _dot_nt(dya, wao_ref[...]).astype(BF16)
        lane = lax.broadcasted_iota(jnp.int32, (tm, LANES), 1)
        for h in range(N_HEADS):
            dah = da[:, h * LANES:(h + 1) * LANES]
            d1, d2, d3 = _pieces(-jnp.sum(dah.astype(F32) * o_ref[h], axis=1, keepdims=True))
            do_ref[h] = jnp.where(lane == LANE_C + 1, d1, jnp.where(lane == LANE_C + 2, d2,
                                                                  jnp.where(lane == LANE_C + 3, d3, dah)))

        dyc = (dmer * g1).astype(BF16)
        dwco_ref[...] += _dot_tn(b["cm"].astype(BF16), dyc)
        dcm = _dot_nt(dyc, wco_ref[...])
        dconv = dcm * b["cb"]
        dcw_ref[0:1, :] += jnp.sum(dconv * b["z2"], axis=0, keepdims=True)
        dcw_ref[1:2, :] += jnp.sum(dconv * b["z1"], axis=0, keepdims=True)
        dcw_ref[2:3, :] += jnp.sum(dconv * b["z"], axis=0, keepdims=True)
        d_ext = jnp.concatenate([dconv, next_dconv[...]], axis=0)
        dz = cw[2:3, :] * dconv + cw[1:2, :] * _shift_up(d_ext, 1, tm) + cw[0:1, :] * _shift_up(d_ext, 2, tm)
        next_dconv[...] = dconv[0:HALO]
        dpb_ref[:, 0:D_CONV] = (dz * b["cc"]).astype(BF16)
        dpb_ref[:, D_CONV:2 * D_CONV] = (dcm * b["conv"]).astype(BF16)
        dpb_ref[:, 2 * D_CONV:3 * D_CONV] = (dz * b["cx"]).astype(BF16)

        dyp = dmer * g2
        dps_ref[...] += jnp.sum(dyp * b["yp_pre"], axis=0, keepdims=True)
        dyps = (dyp * ps_ref[...]).astype(BF16)
        dwp_ref[...] += _dot_tn(b["feat"].astype(BF16), dyps)
        dfeat = _dot_nt(dyps, wp_ref[...])
        e = dfeat / b["cnt"]
        e_ext = jnp.concatenate([e, next_e[...]], axis=0)
        up = lambda a, k: pltpu.roll(a, tm + HALO - k, 0)
        f2 = e_ext + up(e_ext, 1)
        f4 = f2 + up(f2, 2)
        f8 = f4 + up(f4, 4)
        f16 = f8 + up(f8, 8)
        next_e[...] = e[0:HALO]
        dpb_ref[:, 3 * D_CONV:D_LOCAL] = (_pool_lane_select(f2[:tm], f4[:tm], f8[:tm], f16[:tm], b["lane"]) - dfeat).astype(BF16)

    return pl.pallas_call(
        body,
        grid_spec=pltpu.PrefetchScalarGridSpec(
            num_scalar_prefetch=1,
            grid=(nt,),
            in_specs=[rows(D_MODEL), heads, rows(D_B), _halo_spec(tm, rev),
                      _whole(wao), _whole(convw), _whole(wco), _whole(wpool), _whole(pscale), _layer((D_MODEL, D_MODEL))],
            out_specs=[rows(D_B), heads, acc(D_MODEL, D_MODEL), acc(D_HEADS, D_MODEL), acc(D_CONV, D_MODEL),
                       acc(D_POOL, D_MODEL), acc(8, D_CONV), acc(1, D_MODEL)],
            scratch_shapes=[pltpu.VMEM((HALO, D_CONV), F32), pltpu.VMEM((HALO, D_POOL), F32)],
        ),
        out_shape=[jax.ShapeDtypeStruct((S, D_B), BF16), jax.ShapeDtypeStruct((N_HEADS, S, LANES), BF16),
                   jax.ShapeDtypeStruct((D_MODEL, D_MODEL), F32), jax.ShapeDtypeStruct((D_HEADS, D_MODEL), F32),
                   jax.ShapeDtypeStruct((D_CONV, D_MODEL), F32), jax.ShapeDtypeStruct((D_POOL, D_MODEL), F32),
                   jax.ShapeDtypeStruct((8, D_CONV), F32), jax.ShapeDtypeStruct((1, D_MODEL), F32)],
        compiler_params=_params("arbitrary"),
        name="mix_out_bwd",
    )(_layer_index(l), dxm, ox, pb, pb, wao, convw, wco, wpool, pscale, wo_all)


FF_SHARD = 2 * D_FF // 4


def ffn_fwd(x, g, w1_all, w2_all, l):
    S = x.shape[0]

    def body(l_ref, x_ref, g_ref, w1_ref, w2_ref, y_ref, u_ref):
        xf = x_ref[...]
        r = lax.rsqrt(jnp.mean(xf * xf, axis=-1, keepdims=True) + EPS)
        h = (xf * r * g_ref[...]).astype(BF16)
        u = jnp.concatenate([_dot(h, w1_ref[j]) for j in range(4)], axis=1)
        u_ref[...] = u.astype(BF16)
        gt = u[:, 0:D_FF]
        act = gt * _sigmoid(gt) * u[:, D_FF:2 * D_FF]
        y_ref[...] = xf + _dot(act.astype(BF16), w2_ref[...])

    return pl.pallas_call(
        body,
        grid_spec=pltpu.PrefetchScalarGridSpec(
            num_scalar_prefetch=1,
            grid=(S // TM,),
            in_specs=[_rows(TM, D_MODEL), _whole(g), _layer((4, D_MODEL, FF_SHARD)), _layer((D_FF, D_MODEL))],
            out_specs=[_rows(TM, D_MODEL), _rows(TM, 2 * D_FF)],
        ),
        out_shape=[jax.ShapeDtypeStruct((S, D_MODEL), F32), jax.ShapeDtypeStruct((S, 2 * D_FF), BF16)],
        compiler_params=_params("parallel"),
        name="ffn_fwd",
    )(_layer_index(l), x, g, w1_all, w2_all)


def ffn_bwd(x, g, dy, u, w1_all, w2_all, l):
    S = x.shape[0]

    def body(l_ref, x_ref, g_ref, dy_ref, u_ref, w1_ref, w2_ref, dx_ref, du_ref, act_ref, h_ref, dg_ref):
        @pl.when(pl.program_id(0) == 0)
        def _():
            dg_ref[...] = jnp.zeros_like(dg_ref)

        dyf = dy_ref[...]
        dact = _dot_nt(dyf.astype(BF16), w2_ref[...])
        gt = u_ref[:, 0:D_FF].astype(F32)
        up = u_ref[:, D_FF:2 * D_FF].astype(F32)
        sg = _sigmoid(gt)
        silu = gt * sg
        act_ref[...] = (silu * up).astype(BF16)
        du_ref[:, 0:D_FF] = (dact * up * (sg * (1.0 + gt * (1.0 - sg)))).astype(BF16)
        du_ref[:, D_FF:2 * D_FF] = (dact * silu).astype(BF16)
        dh = _dot_nt(du_ref[:, 0:FF_SHARD], w1_ref[0])
        for j in range(1, 4):
            dh = dh + _dot_nt(du_ref[:, j * FF_SHARD:(j + 1) * FF_SHARD], w1_ref[j])
        xf = x_ref[...]
        r = lax.rsqrt(jnp.mean(xf * xf, axis=-1, keepdims=True) + EPS)
        xhat = xf * r
        h_ref[...] = (xhat * g_ref[...]).astype(BF16)
        dg_ref[...] += jnp.sum(dh * xhat, axis=0, keepdims=True)
        gdh = dh * g_ref[...]
        dx_ref[...] = dyf + r * (gdh - xhat * jnp.mean(xhat * gdh, axis=-1, keepdims=True))

    return pl.pallas_call(
        body,
        grid_spec=pltpu.PrefetchScalarGridSpec(
            num_scalar_prefetch=1,
            grid=(S // TM,),
            in_specs=[_rows(TM, D_MODEL), _whole(g), _rows(TM, D_MODEL), _rows(TM, 2 * D_FF),
                      _layer((4, D_MODEL, FF_SHARD)), _layer((D_FF, D_MODEL))],
            out_specs=[_rows(TM, D_MODEL), _rows(TM, 2 * D_FF), _rows(TM, D_FF), _rows(TM, D_MODEL),
                       pl.BlockSpec((1, D_MODEL), lambda i, l: (0, 0))],
        ),
        out_shape=[jax.ShapeDtypeStruct((S, D_MODEL), F32), jax.ShapeDtypeStruct((S, 2 * D_FF), BF16),
                   jax.ShapeDtypeStruct((S, D_FF), BF16), jax.ShapeDtypeStruct((S, D_MODEL), BF16),
                   jax.ShapeDtypeStruct((1, D_MODEL), F32)],
        compiler_params=_params("arbitrary"),
        name="ffn_bwd",
    )(_layer_index(l), x, g, dy, u, w1_all, w2_all)


def loss_head(y, target):
    S = y.shape[0]

    def body(y_ref, t_ref, loss_ref, dy_ref):
        @pl.when(pl.program_id(0) == 0)
        def _():
            loss_ref[0, 0] = 0.0

        err = y_ref[...] - t_ref[...]
        dy_ref[...] = err * (1.0 / D_MODEL)
        loss_ref[0, 0] += 0.5 * jnp.sum(jnp.mean(err * err, axis=-1))

    return pl.pallas_call(
        body,
        grid=(S // TM,),
        in_specs=[_rows(TM, D_MODEL), _rows(TM, D_MODEL)],
        out_specs=[pl.BlockSpec((1, 1), lambda i: (0, 0), memory_space=pltpu.SMEM), _rows(TM, D_MODEL)],
        out_shape=[jax.ShapeDtypeStruct((1, 1), F32), jax.ShapeDtypeStruct((S, D_MODEL), F32)],
        compiler_params=_params("arbitrary"),
        name="loss_head",
    )(y, target)


SPLIT = {
    "w_in": ((D_MODEL, D_IN), 1),
    "w_attn_out": ((D_ATTN, D_MODEL), 1),
    "w_conv_out": ((D_CONV, D_MODEL), 1),
    "pool_w": ((N_GROUPS, D_POOL // N_GROUPS, D_MODEL // N_GROUPS), 2),
    "w_o": ((D_MODEL, D_MODEL), 0),
    "w_ffn_in": ((D_MODEL, 2 * D_FF), 1),
    "w_ffn_out": ((D_FF, D_MODEL), 0),
}
SMALL = {"norm_mix_g": D_MODEL, "forget_b": N_HEADS, "q_norm_g": HEAD_DIM, "k_norm_g": HEAD_DIM,
         "pool_scale": D_MODEL, "norm_ffn_g": D_MODEL, "conv_w": CONV_K * D_CONV}
WEIGHTS = ["norm_mix_g", "w_in", "forget_b", "q_norm_g", "k_norm_g", "w_attn_out", "conv_w", "w_conv_out", "pool_w",
           "pool_scale", "w_o", "norm_ffn_g", "w_ffn_in", "w_ffn_out"]


def layer_fwd(x, full, li, small):
    n = full["w_o"].shape[0]
    row = lambda a: a.astype(F32).reshape(1, -1)
    wa, wf, wb = w_in_prep(full["w_in"], li)
    wao, wco, wpool, convw = branch_w_prep(full["w_attn_out"], full["w_conv_out"], full["pool_w"], full["conv_w"], li)
    w = dict(wa=wa, wf=wf, wb=wb, wao=wao, wco=wco, wpool=wpool, convw=convw,
             w_o=full["w_o"].reshape(n, D_MODEL, D_MODEL), w1=full["w_ffn_in"], w2=full["w_ffn_out"].reshape(n, D_FF, D_MODEL),
             g1=row(small["norm_mix_g"]), g2=row(small["norm_ffn_g"]), pscale=row(small["pool_scale"]),
             gq=row(jnp.tile(small["q_norm_g"], N_HEADS)), gk=row(jnp.tile(small["k_norm_g"], N_HEADS)),
             fb=row(jnp.pad(small["forget_b"], (0, D_F - N_HEADS))))
    h, pa, pf, pb = in_proj_fwd(x, w["g1"], wa, wf, wb)
    qx, kx, vx = qk_prep(pa, pf, w["gq"], w["gk"], w["fb"])
    ox, qxb = attn_fwd(qx, kx, vx)
    xm = mix_out_fwd(x, ox, pb, wao, convw, wco, wpool, w["pscale"], w["w_o"], li)
    y, u = ffn_fwd(xm, w["g2"], w["w1"], w["w2"], li)
    return y, w, dict(x=x, h=h, pa=pa, pf=pf, pb=pb, kx=kx, vx=vx, ox=ox, qxb=qxb, xm=xm, u=u)


def layer_bwd(dx, w, s, wi, gi, big):
    n = big["w_o"].shape[0]
    big = dict(big)
    dxm, du, act, h2, dg2 = ffn_bwd(s["xm"], w["g2"], dx, s["u"], w["w1"], w["w2"], wi)
    big["w_ffn_out"] = wgrad_into(act, dx.astype(BF16), "wgrad_ffn_out", big["w_ffn_out"].reshape(n, D_FF, D_MODEL),
                                  gi).reshape(big["w_ffn_out"].shape)
    big["w_ffn_in"] = wgrad_into(h2, du, "wgrad_ffn_in", big["w_ffn_in"], gi)
    dpb, dox, dwo, dwao, dwco, dwpool, dconvw, dpscale = mix_out_bwd(
        dxm, s["ox"], s["pb"], w["wao"], w["convw"], w["wco"], w["wpool"], w["pscale"], w["w_o"], wi)
    dqx, dkx, dvx = attn_bwd(s["qxb"], s["kx"], s["vx"], dox)
    dpa, dpf, dgq, dgk, dfb = attn_bwd_post(s["pa"], s["pf"], dqx, dkx, dvx, w["gq"], w["gk"], w["fb"])
    dx, dg1 = in_proj_bwd(s["x"], w["g1"], dxm, dpa, dpf, dpb, w["wa"], w["wf"], w["wb"])
    big["w_in"] = w_in_unprep(wgrad(s["h"], dpa, "wgrad_in_qkv"), wgrad(s["h"], dpf, "wgrad_in_f"),
                              wgrad(s["h"], dpb, "wgrad_in_b"), big["w_in"], gi)
    big["w_attn_out"], big["w_conv_out"], big["pool_w"], big["w_o"] = branch_g_place(
        dwao, dwco, dwpool, dwo, (big["w_attn_out"], big["w_conv_out"], big["pool_w"], big["w_o"]), gi)
    sm = dict(norm_mix_g=dg1[0], forget_b=dfb[0, 0:N_HEADS], q_norm_g=dgq.reshape(N_HEADS, HEAD_DIM).sum(0),
              k_norm_g=dgk.reshape(N_HEADS, HEAD_DIM).sum(0), pool_scale=dpscale[0], norm_ffn_g=dg2[0],
              conv_w=dconvw[0:CONV_K].reshape(-1))
    return dx, big, sm


def grad_buffers(full):
    return {n: lax.empty(full[n].shape, F32) for n in SPLIT}


def local_step(x, target, gathered, small):
    n_layers = small["norm_mix_g"].shape[0]
    done = []
    for l in range(n_layers):
        x, w, s = layer_fwd(x, gathered, l, {n: v[l] for n, v in small.items()})
        done.append((w, s))
    loss, dx = loss_head(x, target)
    big = grad_buffers(gathered)
    sm = [None] * n_layers
    for l in reversed(range(n_layers)):
        dx, big, sm[l] = layer_bwd(dx, *done[l], l, l, big)
    return loss, dx, big, {n: jnp.stack([g[n] for g in sm]) for n in SMALL}


def adamw(w, g, m, v, name):
    R, C = w.shape
    tm = 256 if R % 256 == 0 else R

    def body(w_ref, g_ref, m_ref, v_ref, d_ref, nm_ref, nv_ref):
        gr = g_ref[...]
        m_new = ADAM_B1 * m_ref[...] + (1.0 - ADAM_B1) * gr
        v_new = ADAM_B2 * v_ref[...] + (1.0 - ADAM_B2) * jnp.square(gr)
        nm_ref[...] = m_new
        nv_ref[...] = v_new
        m_hat = m_new / (1.0 - ADAM_B1 ** ADAM_STEP)
        v_hat = v_new / (1.0 - ADAM_B2 ** ADAM_STEP)
        d_ref[...] = -ADAM_LR * (m_hat / (jnp.sqrt(v_hat) + ADAM_EPS) + ADAM_WD * w_ref[...])

    spec = _rows(tm, C)
    out = jax.ShapeDtypeStruct((R, C), F32)
    return pl.pallas_call(
        body,
        grid=(R // tm,),
        in_specs=[spec] * 4,
        out_specs=[spec] * 3,
        out_shape=[out] * 3,
        compiler_params=_params("parallel"),
        name=name,
    )(w, g, m, v)


MESH = pl.DeviceIdType.MESH
HBM_REF = pl.BlockSpec(memory_space=pl.ANY)
N_CHIPS = 4
N_DEV = 8
FWD_GROUPS = ((0, 1), (1, 2), (2, 4))
BWD_GROUPS = ((0, 1), (1, 4))
SMALL_SHAPE = (128, LANES)


def _mesh_pos():
    return lax.axis_index("x"), lax.axis_index("y"), lax.axis_index("c")


def _other_chips(x, y):
    return [(1 - x, y), (x, 1 - y), (1 - x, 1 - y)]


def _remote(src, dst, send_sem, recv_sem, to):
    return pltpu.make_async_remote_copy(src_ref=src, dst_ref=dst, send_sem=send_sem, recv_sem=recv_sem,
                                        device_id=to, device_id_type=MESH)


def _row_tile(rows):
    for tm in (256, 176, 128):
        if rows % tm == 0:
            return tm
    return rows


def _as4(a):
    return a.reshape(a.shape[0], a.shape[1], -1, a.shape[-1])


def _core_rows(buf, c):
    R = buf.shape[2]
    if R % 2:
        return pl.ds(0, R), pl.ds(0, R), False
    return pl.ds(c * (R // 2), R // 2), pl.ds((1 - c) * (R // 2), R // 2), True


def place_shard(chip, w, lo, hi, dtype, name, after=None):
    w3 = w.reshape(w.shape[0], -1, w.shape[-1])
    _, R, C = w3.shape
    tm = _row_tile(R)
    idle = [] if after is None else [after]

    def body(chip_ref, w_ref, *rest):
        rest[-1][...] = w_ref[...].astype(dtype)

    return pl.pallas_call(
        body,
        grid_spec=pltpu.PrefetchScalarGridSpec(
            num_scalar_prefetch=1,
            grid=(hi - lo, R // tm),
            in_specs=[pl.BlockSpec((None, tm, C), lambda l, i, chip: (lo + l, i, 0))] + [_whole(a) for a in idle],
            out_specs=pl.BlockSpec((None, None, tm, C), lambda l, i, chip: (l, chip[0], i, 0)),
        ),
        out_shape=jax.ShapeDtypeStruct((hi - lo, N_CHIPS, R, C), dtype),
        compiler_params=_params("parallel", "parallel"),
        name=name,
    )(chip, w3, *idle)


HBM_SPACE = pl.BlockSpec(memory_space=pltpu.HBM)
SEM_SPACE = pl.BlockSpec(memory_space=pltpu.SEMAPHORE)
IN_FLIGHT = pltpu.SideEffectType.DATAFLOW_SIDE_EFFECTING


def _sem_table(send_sems, recv_sems):
    return lambda t, j: (send_sems.at[t, j], recv_sems.at[t, j])


def _sem_per_peer(send_sems, recv_sems):
    return lambda t, j: (send_sems[j], recv_sems[j])


def _gather_ici(bufs, sem, sends=True, lands=True):
    x, y, c = _mesh_pos()
    me = 2 * x + y
    out, into = [], []
    for t, buf in enumerate(bufs):
        mine, _, _ = _core_rows(buf, c)
        part = lambda k, buf=buf, mine=mine: buf.at[pl.ds(0, buf.shape[0]), k, mine]
        for j, (px, py) in enumerate(_other_chips(x, y)):
            if sends:
                out.append(_remote(part(me), part(me), *sem(t, j), (px, py, c)))
            if lands:
                into.append(_remote(part(2 * px + py), part(2 * px + py), *sem(t, j), (px, py, c)))
    return out, into


def _gather_d2d(bufs, send_sems, recv_sems, first):
    x, y, c = _mesh_pos()
    sends, lands = [], []
    for t, buf in enumerate(bufs):
        mine, theirs, split = _core_rows(buf, c)
        if not split:
            continue
        for j, (px, py) in enumerate(_other_chips(x, y)):
            part = lambda rows, buf=buf, k=2 * px + py: buf.at[pl.ds(0, buf.shape[0]), k, rows]
            sems = (send_sems.at[t, first + j], recv_sems.at[t, first + j], (x, y, 1 - c))
            sends.append(_remote(part(mine), part(mine), *sems))
            lands.append(_remote(part(theirs), part(theirs), *sems))
    return sends, lands


def gather_shards(bufs):
    n = len(bufs)

    def body(*refs):
        outs = refs[n:2 * n]
        send_sems, recv_sems = refs[2 * n:]
        ici_out, ici_in = _gather_ici(outs, _sem_table(send_sems, recv_sems))
        d2d_out, d2d_in = _gather_d2d(outs, send_sems, recv_sems, 3)
        for cp in ici_out:
            cp.start()
        for cp in ici_in:
            cp.wait_recv()
        for cp in d2d_out:
            cp.start()
        for cp in d2d_in:
            cp.wait_recv()
        for cp in ici_out + d2d_out:
            cp.wait_send()

    return pl.pallas_call(
        body,
        in_specs=[HBM_REF] * n,
        out_specs=[HBM_REF] * n,
        out_shape=[jax.ShapeDtypeStruct(b.shape, b.dtype) for b in bufs],
        input_output_aliases={t: t for t in range(n)},
        scratch_shapes=[pltpu.SemaphoreType.DMA((n, 6)), pltpu.SemaphoreType.DMA((n, 6))],
        name="gather_shards",
    )(*bufs)


def gather_start(bufs):
    n = len(bufs)

    def body(*refs):
        send_sems, recv_sems = refs[n:n + 3], refs[n + 3:n + 6]
        outs = refs[n + 6:2 * n + 6]
        token = refs[2 * n + 6]
        for cp in _gather_ici(outs, _sem_per_peer(send_sems, recv_sems), lands=False)[0]:
            cp.start()
        token[...] = jnp.zeros_like(token)

    res = pl.pallas_call(
        body,
        in_specs=[HBM_SPACE] * n,
        out_specs=[SEM_SPACE] * 6 + [HBM_SPACE] * n + [pl.BlockSpec(memory_space=pltpu.VMEM)],
        out_shape=[pltpu.SemaphoreType.DMA(())] * 6
        + [pltpu.HBM(b.shape, b.dtype) for b in bufs] + [jax.ShapeDtypeStruct((8, LANES), F32)],
        input_output_aliases={t: t + 6 for t in range(n)},
        compiler_params=pltpu.CompilerParams(has_side_effects=IN_FLIGHT),
        name="gather_start",
    )(*[pltpu.with_memory_space_constraint(b, pltpu.HBM) for b in bufs])
    return list(res[0:3]), list(res[3:6]), list(res[6:n + 6]), res[n + 6]


def gather_wait(send_sems, recv_sems, bufs, after):
    n = len(bufs)

    def body(*refs):
        ins_send, ins_recv = refs[n:n + 3], refs[n + 3:n + 6]
        outs = refs[n + 7:]
        sends, lands = _gather_ici(outs, _sem_per_peer(ins_send, ins_recv))
        for cp in lands:
            cp.wait_recv()
        for cp in sends:
            cp.wait_send()

    return list(pl.pallas_call(
        body,
        in_specs=[HBM_SPACE] * n + [SEM_SPACE] * 6 + [HBM_REF],
        out_specs=[HBM_SPACE] * n,
        out_shape=[pltpu.HBM(b.shape, b.dtype) for b in bufs],
        input_output_aliases={t: t for t in range(n)},
        compiler_params=pltpu.CompilerParams(has_side_effects=IN_FLIGHT),
        name="gather_wait",
    )(*bufs, *send_sems, *recv_sems, after))


def gather_forward(bufs):
    n = len(bufs)

    def body(*refs):
        outs = refs[n:2 * n]
        send_sems, recv_sems = refs[2 * n:]
        sends, lands = _gather_d2d(outs, send_sems, recv_sems, 0)
        for cp in sends:
            cp.start()
        for cp in lands:
            cp.wait_recv()
        for cp in sends:
            cp.wait_send()

    return list(pl.pallas_call(
        body,
        in_specs=[HBM_REF] * n,
        out_specs=[HBM_REF] * n,
        out_shape=[jax.ShapeDtypeStruct(b.shape, b.dtype) for b in bufs],
        input_output_aliases={t: t for t in range(n)},
        scratch_shapes=[pltpu.SemaphoreType.DMA((n, 3)), pltpu.SemaphoreType.DMA((n, 3))],
        name="gather_forward",
    )(*bufs))


def pair_exchange(grads):
    n = len(grads)

    def body(*refs):
        ins, outs = refs[:n], refs[n:2 * n]
        send_sems, recv_sems = refs[2 * n:]
        x, y, c = _mesh_pos()
        cps = []
        for t in range(n):
            _, theirs, _ = _core_rows(ins[t], c)
            src = ins[t].at[pl.ds(0, ins[t].shape[0]), pl.ds(0, N_CHIPS), theirs]
            cps.append(_remote(src, outs[t], send_sems.at[t], recv_sems.at[t], (x, y, 1 - c)))
        for cp in cps:
            cp.start()
        for cp in cps:
            cp.wait()

    return pl.pallas_call(
        body,
        in_specs=[HBM_REF] * n,
        out_specs=[HBM_REF] * n,
        out_shape=[jax.ShapeDtypeStruct(g.shape[:2] + (g.shape[2] // 2, g.shape[3]), g.dtype) for g in grads],
        scratch_shapes=[pltpu.SemaphoreType.DMA((n,)), pltpu.SemaphoreType.DMA((n,))],
        name="pair_exchange",
    )(*grads)


def pair_sum(core, g, t, name):
    n, _, half, C = t.shape
    tm = _row_tile(half)
    per = half // tm

    def body(c_ref, g_ref, t_ref, o_ref):
        o_ref[...] = (g_ref[...] + t_ref[...]).astype(BF16)

    tile = pl.BlockSpec((None, tm, C), lambda a, i, c: (a, i, 0))
    out = pl.pallas_call(
        body,
        grid_spec=pltpu.PrefetchScalarGridSpec(
            num_scalar_prefetch=1,
            grid=(n * N_CHIPS, per),
            in_specs=[pl.BlockSpec((None, tm, C), lambda a, i, c: (a, per * c[0] + i, 0)), tile],
            out_specs=tile,
        ),
        out_shape=jax.ShapeDtypeStruct((n * N_CHIPS, half, C), BF16),
        compiler_params=_params("parallel", "parallel"),
        name=name,
    )(core, g.reshape(n * N_CHIPS, 2 * half, C), t.reshape(n * N_CHIPS, half, C))
    return out.reshape(t.shape)


def _chip_copies(sums, recv, sem):
    x, y, c = _mesh_pos()
    cps = []
    for t in range(len(sums)):
        for j, (px, py) in enumerate(_other_chips(x, y)):
            src = sums[t].at[pl.ds(0, sums[t].shape[0]), 2 * px + py]
            cps.append(_remote(src, recv[t].at[j], *sem(t, j), (px, py, c)))
    return cps


def _recv_shape(s):
    return (N_CHIPS - 1, s.shape[0]) + s.shape[2:]


def chip_exchange(sums):
    n = len(sums)

    def body(*refs):
        ins, outs = refs[:n], refs[n:2 * n]
        cps = _chip_copies(ins, outs, _sem_table(*refs[2 * n:]))
        for cp in cps:
            cp.start()
        for cp in cps:
            cp.wait()

    return list(pl.pallas_call(
        body,
        in_specs=[HBM_REF] * n,
        out_specs=[HBM_REF] * n,
        out_shape=[jax.ShapeDtypeStruct(_recv_shape(s), s.dtype) for s in sums],
        scratch_shapes=[pltpu.SemaphoreType.DMA((n, 3)), pltpu.SemaphoreType.DMA((n, 3))],
        name="chip_exchange",
    )(*sums))


def chip_exchange_start(sums):
    n = len(sums)
    lands = [lax.empty(_recv_shape(s), s.dtype) for s in sums]

    def body(*refs):
        send_sems, recv_sems = refs[2 * n:2 * n + 3], refs[2 * n + 3:2 * n + 6]
        outs = refs[2 * n + 6:4 * n + 6]
        token = refs[4 * n + 6]
        for cp in _chip_copies(outs[:n], outs[n:2 * n], _sem_per_peer(send_sems, recv_sems)):
            cp.start()
        token[...] = jnp.zeros_like(token)

    res = pl.pallas_call(
        body,
        in_specs=[HBM_SPACE] * (2 * n),
        out_specs=[SEM_SPACE] * 6 + [HBM_SPACE] * (2 * n) + [pl.BlockSpec(memory_space=pltpu.VMEM)],
        out_shape=[pltpu.SemaphoreType.DMA(())] * 6 + [pltpu.HBM(a.shape, a.dtype) for a in list(sums) + lands]
        + [jax.ShapeDtypeStruct((8, LANES), F32)],
        input_output_aliases={t: t + 6 for t in range(2 * n)},
        compiler_params=pltpu.CompilerParams(has_side_effects=IN_FLIGHT),
        name="chip_exchange_start",
    )(*[pltpu.with_memory_space_constraint(a, pltpu.HBM) for a in list(sums) + lands])
    return list(res[0:3]), list(res[3:6]), list(res[6:n + 6]), list(res[n + 6:2 * n + 6]), res[2 * n + 6]


def chip_exchange_wait(send_sems, recv_sems, sums, lands, after):
    n = len(sums)

    def body(*refs):
        ins_send, ins_recv = refs[2 * n:2 * n + 3], refs[2 * n + 3:2 * n + 6]
        outs = refs[2 * n + 7:]
        for cp in _chip_copies(outs[:n], outs[n:], _sem_per_peer(ins_send, ins_recv)):
            cp.wait()

    res = pl.pallas_call(
        body,
        in_specs=[HBM_SPACE] * (2 * n) + [SEM_SPACE] * 6 + [HBM_REF],
        out_specs=[HBM_SPACE] * (2 * n),
        out_shape=[pltpu.HBM(a.shape, a.dtype) for a in list(sums) + list(lands)],
        input_output_aliases={t: t for t in range(2 * n)},
        compiler_params=pltpu.CompilerParams(has_side_effects=IN_FLIGHT),
        name="chip_exchange_wait",
    )(*sums, *lands, *send_sems, *recv_sems, after)
    return list(res[:n]), list(res[n:])


def chip_sum_into(core, chip, recv, sums, total, lo, name):
    _, n, half, C = recv.shape
    tm = _row_tile(half)
    per = half // tm

    def body(c_ref, k_ref, r_ref, s_ref, t_ref, o_ref):
        acc = s_ref[...].astype(F32)
        for j in range(N_CHIPS - 1):
            acc = acc + r_ref[j].astype(F32)
        o_ref[...] = acc

    return pl.pallas_call(
        body,
        grid_spec=pltpu.PrefetchScalarGridSpec(
            num_scalar_prefetch=2,
            grid=(n, per),
            in_specs=[pl.BlockSpec((N_CHIPS - 1, None, tm, C), lambda a, i, c, k: (0, a, i, 0)),
                      pl.BlockSpec((None, None, tm, C), lambda a, i, c, k: (a, k[0], i, 0)),
                      HBM_REF],
            out_specs=pl.BlockSpec((None, tm, C), lambda a, i, c, k: (lo + a, per * c[0] + i, 0)),
        ),
        out_shape=jax.ShapeDtypeStruct(total.shape, F32),
        input_output_aliases={4: 0},
        compiler_params=_params("parallel", "parallel"),
        name=name,
    )(core, chip, recv, sums, total)


def sibling_share(totals):
    n = len(totals)

    def body(*refs):
        outs = refs[n:2 * n]
        send_sems, recv_sems = refs[2 * n:]
        x, y, c = _mesh_pos()
        half = lambda t, h: outs[t].at[pl.ds(0, DEPTH), pl.ds(h * (outs[t].shape[1] // 2), outs[t].shape[1] // 2)]
        sent = [_remote(half(t, c), half(t, c), send_sems.at[t], recv_sems.at[t], (x, y, 1 - c)) for t in range(n)]
        for cp in sent:
            cp.start()
        for t in range(n):
            _remote(half(t, 1 - c), half(t, 1 - c), send_sems.at[t], recv_sems.at[t], (x, y, 1 - c)).wait_recv()
        for cp in sent:
            cp.wait_send()

    return pl.pallas_call(
        body,
        in_specs=[HBM_REF] * n,
        out_specs=[HBM_REF] * n,
        out_shape=[jax.ShapeDtypeStruct(t.shape, t.dtype) for t in totals],
        input_output_aliases={t: t for t in range(n)},
        scratch_shapes=[pltpu.SemaphoreType.DMA((n,)), pltpu.SemaphoreType.DMA((n,))],
        name="sibling_share",
    )(*totals)


def small_allgather(small):
    def body(s_ref, a_ref, send_sems, recv_sems, local_sem):
        x, y, c = _mesh_pos()
        me = 4 * x + 2 * y + c
        own = pltpu.make_async_copy(s_ref, a_ref.at[me], local_sem)
        own.start()
        sent = []
        for k in range(1, N_DEV):
            peer = (x ^ (k >> 2), y ^ ((k >> 1) & 1), c ^ (k & 1))
            cp = _remote(s_ref, a_ref.at[me], send_sems.at[k - 1], recv_sems.at[k - 1], peer)
            cp.start()
            sent.append(cp)
        for k in range(1, N_DEV):
            px, py, pc = x ^ (k >> 2), y ^ ((k >> 1) & 1), c ^ (k & 1)
            _remote(s_ref, a_ref.at[4 * px + 2 * py + pc], send_sems.at[k - 1], recv_sems.at[k - 1], (px, py, pc)).wait_recv()
        for cp in sent:
            cp.wait_send()
        own.wait()

    return pl.pallas_call(
        body,
        in_specs=[HBM_REF],
        out_specs=HBM_REF,
        out_shape=jax.ShapeDtypeStruct((N_DEV,) + SMALL_SHAPE, small.dtype),
        scratch_shapes=[pltpu.SemaphoreType.DMA((N_DEV - 1,)), pltpu.SemaphoreType.DMA((N_DEV - 1,)), pltpu.SemaphoreType.DMA],
        name="small_allgather",
    )(small)


def small_sum(blocks):
    def body(a_ref, o_ref):
        acc = a_ref[0]
        for d in range(1, N_DEV):
            acc = acc + a_ref[d]
        o_ref[...] = acc

    return pl.pallas_call(
        body,
        in_specs=[pl.BlockSpec(memory_space=pltpu.VMEM)],
        out_specs=pl.BlockSpec(memory_space=pltpu.VMEM),
        out_shape=jax.ShapeDtypeStruct(SMALL_SHAPE, F32),
        name="small_sum",
    )(blocks)


def pack_small(grads, loss):
    flat = jnp.concatenate([grads[n].reshape(-1) for n in SMALL] + [loss.reshape(-1)])
    size = SMALL_SHAPE[0] * SMALL_SHAPE[1]
    return jnp.pad(flat, (0, size - flat.shape[0])).reshape(SMALL_SHAPE)


def unpack_small(packed):
    flat = packed.reshape(-1)
    out, off = {}, 0
    for n, size in SMALL.items():
        out[n] = flat[off:off + DEPTH * size].reshape(DEPTH, size)
        off += DEPTH * size
    return out, flat[off]


def kernel(x, norm_mix_g, w_in, forget_b, q_norm_g, k_norm_g, w_attn_out, conv_w, w_conv_out, pool_w, pool_scale, w_o, norm_ffn_g, w_ffn_in, w_ffn_out, loss_target, m_norm_mix_g, m_w_in, m_forget_b, m_q_norm_g, m_k_norm_g, m_w_attn_out, m_conv_w, m_w_conv_out, m_pool_w, m_pool_scale, m_w_o, m_norm_ffn_g, m_w_ffn_in, m_w_ffn_out, v_norm_mix_g, v_w_in, v_forget_b, v_q_norm_g, v_k_norm_g, v_w_attn_out, v_conv_w, v_w_conv_out, v_pool_w, v_pool_scale, v_w_o, v_norm_ffn_g, v_w_ffn_in, v_w_ffn_out):
    given = dict(locals())
    weights = {n: given[n] for n in WEIGHTS}

    core = lax.axis_index("c").astype(jnp.int32)
    chip = (2 * lax.axis_index("x") + lax.axis_index("y")).astype(jnp.int32)

    core, chip = core.reshape(1), chip.reshape(1)
    names = list(SPLIT) + ["conv_w"]

    def placed(lo, hi, after=None):
        return [place_shard(chip, weights[n], lo, hi, F32 if n == "conv_w" else BF16, "place_" + n,
                            after["conv_w"][0, 0] if n == "conv_w" and after is not None else None) for n in names]

    def as_weights(bufs):
        return {n: b.reshape(b.shape[:2] + weights[n].shape[1:]) for n, b in zip(names, bufs)}

    xs = x[0]
    done = [None] * DEPTH
    full = as_weights(gather_shards(placed(*FWD_GROUPS[0])))
    for g, (lo, hi) in enumerate(FWD_GROUPS):
        last = g + 1 == len(FWD_GROUPS)
        if not last:
            send_sems, recv_sems, in_flight, token = gather_start(placed(*FWD_GROUPS[g + 1], after=full))
        for l in range(lo, hi):
            small = {n: weights[n][l] for n in SMALL if n != "conv_w"}
            if l == lo and not last:
                small["norm_mix_g"] = small["norm_mix_g"] + token[0, 0]
            xs, w, s = layer_fwd(xs, full, l - lo, small)
            done[l] = (w, s, l - lo)
        if not last:
            full = as_weights(gather_forward(gather_wait(send_sems, recv_sems, in_flight, xs)))
    loss, dx = loss_head(xs, loss_target[0])

    small_grads = [None] * DEPTH
    totals = [lax.empty((DEPTH,) + _as4(weights[n][None]).shape[2:], F32) for n in SPLIT]

    def add_chips(totals, recv, sums, lo):
        return [chip_sum_into(core, chip, r, s, t, lo, "chip_sum_" + n) for n, r, s, t in zip(SPLIT, recv, sums, totals)]

    pending = None
    for g in reversed(range(len(BWD_GROUPS))):
        lo, hi = BWD_GROUPS[g]
        big = {n: lax.empty((hi - lo, N_CHIPS) + weights[n].shape[1:], F32) for n in SPLIT}
        for l in reversed(range(lo, hi)):
            w, s, wi = done[l]
            if pending is not None and l == hi - 1:
                w = dict(w, g2=w["g2"] + pending[-1][0:1, 0:1])
            dx, big, small_grads[l] = layer_bwd(dx, w, s, wi, l - lo, big)
        if pending is not None:
            p_lo, send_sems, recv_sems, p_sums, p_lands, _ = pending
            p_sums, p_recv = chip_exchange_wait(send_sems, recv_sems, p_sums, p_lands, dx)
            totals = add_chips(totals, p_recv, p_sums, p_lo)
        grads = [_as4(big[n]) for n in SPLIT]
        sums = [pair_sum(core, a, t, "pair_sum_" + n) for n, a, t in zip(SPLIT, grads, pair_exchange(grads))]
        if g > 0:
            pending = (lo, *chip_exchange_start(sums))
        else:
            totals = add_chips(totals, chip_exchange(sums), sums, lo)
    reduced = {n: t.reshape(weights[n].shape) for n, t in zip(SPLIT, sibling_share(totals))}

    small_grads = {n: jnp.stack([g[n] for g in small_grads]) for n in SMALL}
    small_total, loss_sum = unpack_small(small_sum(small_allgather(pack_small(small_grads, loss))))
    chip = chip[0]
    cols = D_CONV // N_CHIPS
    small_total["conv_w"] = lax.dynamic_slice_in_dim(small_total["conv_w"].reshape(DEPTH, CONV_K, D_CONV), chip * cols, cols, axis=2)
    for n in SMALL:
        reduced[n] = small_total[n].reshape(weights[n].shape)

    deltas, new_m, new_v = {}, {}, {}
    for n in WEIGHTS:
        w = weights[n]
        flat = (-1, w.shape[-1])
        d, nm, nv = adamw(w.reshape(flat), reduced[n].reshape(flat), given["m_" + n].reshape(flat),
                          given["v_" + n].reshape(flat), "adamw_" + n)
        deltas[n], new_m[n], new_v[n] = d.reshape(w.shape), nm.reshape(w.shape), nv.reshape(w.shape)
    return (loss_sum, dx[None], *[reduced[n] for n in WEIGHTS], *[deltas[n] for n in WEIGHTS],
            *[new_m[n] for n in WEIGHTS], *[new_v[n] for n in WEIGHTS])
```

```python
import functools

import numpy as np
import jax
import jax.numpy as jnp
from jax import lax
from jax.experimental import pallas as pl
from jax.experimental.pallas import tpu as pltpu

F32 = jnp.float32
BF16 = jnp.bfloat16

D_MODEL = 1024
DEPTH = 4
HEAD_DIM = 64
N_HEADS = 8
D_ATTN = 512
D_CONV = 256
D_POOL = 256
D_FF = 2816
D_IN = 5640
CONV_K = 3
POOL_WINDOWS = (2, 4, 8, 16)
N_GROUPS = len(POOL_WINDOWS)
EPS = 1e-6
ADAM_LR, ADAM_B1, ADAM_B2, ADAM_EPS, ADAM_WD, ADAM_STEP = 0.001, 0.9, 0.999, 1e-08, 0.01, 10

D_QKV = 3 * D_ATTN
D_F = 128
D_B = 3 * D_CONV + D_POOL + 3 * D_MODEL
D_LOCAL = 3 * D_CONV + D_POOL

LANES = 128
D_HEADS = N_HEADS * LANES
HALO = 16
VMEM_LIMIT = 56 * 1024 * 1024
NEG = -1e30
LOG2E = 1.4426950408889634
LN2 = 0.6931471805599453

TM = 256
TM_MIX = 256
TQ = 512
TK = 1024
KB = 512
QT = 1024

LANE_C = 64
LANE_ONE = 67
LANE_LSE = 70
N_PIECES = 3


def _dot(a, b):
    return jnp.dot(a, b, preferred_element_type=F32)


def _dot_nt(a, b):
    return lax.dot_general(a, b, (((1,), (1,)), ((), ())), preferred_element_type=F32)


def _dot_tn(a, b):
    return lax.dot_general(a, b, (((0,), (0,)), ((), ())), preferred_element_type=F32)


def _params(*sem):
    return pltpu.CompilerParams(dimension_semantics=sem, vmem_limit_bytes=VMEM_LIMIT)


def _rows(tm, n):
    return pl.BlockSpec((tm, n), lambda i, *_: (i, 0))


def _whole(a):
    nd = a.ndim
    return pl.BlockSpec(a.shape, lambda *_: (0,) * nd)


def _layer(shape):
    nd = len(shape)
    return pl.BlockSpec((None,) + tuple(shape), lambda *a: (a[-1][0],) + (0,) * nd)


def _layer_index(l):
    return jnp.full((1,), l, jnp.int32)


def _split_bf16(x):
    hi = x.astype(BF16)
    lo = (x - hi.astype(F32)).astype(BF16)
    return hi, lo


def _pieces(x):
    p1 = x.astype(BF16)
    r1 = x - p1.astype(F32)
    p2 = r1.astype(BF16)
    p3 = (r1 - p2.astype(F32)).astype(BF16)
    return p1, p2, p3


def _sigmoid(x):
    return 1.0 / (1.0 + jnp.exp(-x))


def w_in_prep(win, l):
    tr = 256
    n = D_IN // 4
    v_rest = D_QKV - n
    b0 = v_rest + N_HEADS

    def body(s0, s1, s2, s3, wa_ref, wf_ref, wb_ref):
        b = s1[...]
        wa_ref[...] = jnp.concatenate([s0[...], b[:, 0:v_rest]], axis=1)
        wf_ref[...] = jnp.concatenate([b[:, v_rest:b0], jnp.zeros((tr, D_F - N_HEADS), b.dtype)], axis=1)
        wb_ref[...] = jnp.concatenate([b[:, b0:n], s2[...], s3[...]], axis=1)

    shard = lambda j: pl.BlockSpec((None, None, tr, n), lambda i: (l, j, i, 0))
    return pl.pallas_call(
        body,
        grid=(D_MODEL // tr,),
        in_specs=[shard(0), shard(1), shard(2), shard(3)],
        out_specs=[_rows(tr, D_QKV), _rows(tr, D_F), _rows(tr, D_B)],
        out_shape=[jax.ShapeDtypeStruct((D_MODEL, D_QKV), win.dtype), jax.ShapeDtypeStruct((D_MODEL, D_F), win.dtype),
                   jax.ShapeDtypeStruct((D_MODEL, D_B), win.dtype)],
        compiler_params=_params("parallel"),
        name="w_in_prep",
    )(win, win, win, win)


def _into_layer(buf):
    return dict(in_spec=HBM_REF, out_shape=jax.ShapeDtypeStruct(buf.shape, buf.dtype), aliases={1: 0})


def w_in_unprep(dwa, dwf, dwb, buf, l):
    tr = 256
    n = D_IN // 4
    v_rest = D_QKV - n
    b1 = n - v_rest - N_HEADS
    place = _into_layer(buf)

    def body(l_ref, buf_ref, a_ref, f_ref, b_ref, o_ref):
        a = a_ref[...]
        b = b_ref[...]
        o_ref[0] = a[:, 0:n]
        o_ref[1] = jnp.concatenate([a[:, n:D_QKV], f_ref[:, 0:N_HEADS], b[:, 0:b1]], axis=1)
        o_ref[2] = b[:, b1:b1 + n]
        o_ref[3] = b[:, b1 + n:D_B]

    return pl.pallas_call(
        body,
        grid_spec=pltpu.PrefetchScalarGridSpec(
            num_scalar_prefetch=1,
            grid=(D_MODEL // tr,),
            in_specs=[place["in_spec"], _rows(tr, D_QKV), _rows(tr, D_F), _rows(tr, D_B)],
            out_specs=pl.BlockSpec((None, 4, tr, n), lambda i, l: (l[0], 0, i, 0)),
        ),
        out_shape=place["out_shape"],
        input_output_aliases=place["aliases"],
        compiler_params=_params("parallel"),
        name="w_in_unprep",
    )(_layer_index(l), buf, dwa, dwf, dwb)


def branch_w_prep(wao, wco, pw, cw, l):
    gd = D_POOL // N_GROUPS
    od = D_MODEL // N_GROUPS

    def body(wao_ref, wco_ref, pw_ref, cw_ref, ao_ref, co_ref, po_ref, co8_ref):
        a = jnp.concatenate([wao_ref[j] for j in range(4)], axis=1)
        gap = jnp.zeros((LANES - HEAD_DIM, D_MODEL), a.dtype)
        ao_ref[...] = jnp.concatenate(
            [blk for h in range(N_HEADS) for blk in (a[h * HEAD_DIM:(h + 1) * HEAD_DIM], gap)], axis=0)
        co_ref[...] = jnp.concatenate([wco_ref[j] for j in range(4)], axis=1)
        zero = jnp.zeros((gd, od), a.dtype)
        po_ref[...] = jnp.concatenate(
            [jnp.concatenate([jnp.concatenate([pw_ref[j, g] for j in range(4)], axis=1) if g2 == g else zero
                              for g2 in range(N_GROUPS)], axis=1) for g in range(N_GROUPS)], axis=0)
        co8_ref[...] = jnp.zeros_like(co8_ref)
        co8_ref[0:CONV_K, :] = jnp.concatenate([cw_ref[j] for j in range(4)], axis=1)

    sel = lambda *shape: pl.BlockSpec((None,) + shape, lambda i: (l,) + (0,) * len(shape))
    return pl.pallas_call(
        body,
        grid=(1,),
        in_specs=[sel(4, D_ATTN, D_MODEL // 4), sel(4, D_CONV, D_MODEL // 4), sel(4, N_GROUPS, gd, od // 4),
                  sel(4, CONV_K, D_CONV // 4)],
        out_specs=[pl.BlockSpec((D_HEADS, D_MODEL), lambda i: (0, 0)), pl.BlockSpec((D_CONV, D_MODEL), lambda i: (0, 0)),
                   pl.BlockSpec((D_POOL, D_MODEL), lambda i: (0, 0)), pl.BlockSpec((8, D_CONV), lambda i: (0, 0))],
        out_shape=[jax.ShapeDtypeStruct((D_HEADS, D_MODEL), BF16), jax.ShapeDtypeStruct((D_CONV, D_MODEL), BF16),
                   jax.ShapeDtypeStruct((D_POOL, D_MODEL), BF16), jax.ShapeDtypeStruct((8, D_CONV), F32)],
        name="branch_w_prep",
    )(wao, wco, pw, cw)


def branch_g_place(dwao, dwco, dwpool, dwo, bufs, l):
    gd = D_POOL // N_GROUPS
    od = D_MODEL // N_GROUPS
    q = D_MODEL // 4

    def body(l_ref, b0, b1, b2, b3, a_ref, c_ref, p_ref, w_ref, ao_ref, co_ref, po_ref, wo_ref):
        a = jnp.concatenate([a_ref[h * LANES:h * LANES + HEAD_DIM, :] for h in range(N_HEADS)], axis=0)
        c = c_ref[...]
        p = p_ref[...]
        for j in range(4):
            ao_ref[j] = a[:, j * q:(j + 1) * q]
            co_ref[j] = c[:, j * q:(j + 1) * q]
            wo_ref[j] = w_ref[j * q:(j + 1) * q, :]
            for g in range(N_GROUPS):
                c0 = g * od + j * (od // 4)
                po_ref[j, g] = p[g * gd:(g + 1) * gd, c0:c0 + od // 4]

    whole = lambda a: pl.BlockSpec(a.shape, lambda i, l: (0,) * a.ndim)
    layer = lambda b: pl.BlockSpec((None,) + b.shape[1:], lambda i, l: (l[0],) + (0,) * (b.ndim - 1))
    return pl.pallas_call(
        body,
        grid_spec=pltpu.PrefetchScalarGridSpec(
            num_scalar_prefetch=1,
            grid=(1,),
            in_specs=[HBM_REF] * 4 + [whole(dwao), whole(dwco), whole(dwpool), whole(dwo)],
            out_specs=[layer(b) for b in bufs],
        ),
        out_shape=[jax.ShapeDtypeStruct(b.shape, b.dtype) for b in bufs],
        input_output_aliases={1: 0, 2: 1, 3: 2, 4: 3},
        compiler_params=_params("arbitrary"),
        name="branch_g_place",
    )(_layer_index(l), *bufs, dwao, dwco, dwpool, dwo)


def in_proj_fwd(x, g, wa, wf, wb):
    S = x.shape[0]

    def body(x_ref, g_ref, wa_ref, wf_ref, wb_ref, h_ref, pa_ref, pf_ref, pb_ref):
        xf = x_ref[...]
        r = lax.rsqrt(jnp.mean(xf * xf, axis=-1, keepdims=True) + EPS)
        h = (xf * r * g_ref[...]).astype(BF16)
        h_ref[...] = h
        pa_ref[...] = _dot(h, wa_ref[...]).astype(BF16)
        pf_ref[...] = _dot(h, wf_ref[...])
        pb_ref[...] = _dot(h, wb_ref[...]).astype(BF16)

    return pl.pallas_call(
        body,
        grid=(S // TM,),
        in_specs=[_rows(TM, D_MODEL), _whole(g), _whole(wa), _whole(wf), _whole(wb)],
        out_specs=[_rows(TM, D_MODEL), _rows(TM, D_QKV), _rows(TM, D_F), _rows(TM, D_B)],
        out_shape=[
            jax.ShapeDtypeStruct((S, D_MODEL), BF16),
            jax.ShapeDtypeStruct((S, D_QKV), BF16),
            jax.ShapeDtypeStruct((S, D_F), F32),
            jax.ShapeDtypeStruct((S, D_B), BF16),
        ],
        compiler_params=_params("parallel"),
        name="in_proj_fwd",
    )(x, g, wa, wf, wb)


def in_proj_bwd(x, g, dxm, dpa, dpf, dpb, wa, wf, wb):
    S = x.shape[0]

    def body(x_ref, g_ref, dxm_ref, dpa_ref, dpf_ref, dpb_ref, wa_ref, wf_ref, wb_ref, dx_ref, dg_ref):
        @pl.when(pl.program_id(0) == 0)
        def _():
            dg_ref[...] = jnp.zeros_like(dg_ref)

        dh = _dot_nt(dpa_ref[...], wa_ref[...]) + _dot_nt(dpf_ref[...], wf_ref[...]) + _dot_nt(dpb_ref[...], wb_ref[...])
        xf = x_ref[...]
        r = lax.rsqrt(jnp.mean(xf * xf, axis=-1, keepdims=True) + EPS)
        xhat = xf * r
        dg_ref[...] += jnp.sum(dh * xhat, axis=0, keepdims=True)
        gdh = dh * g_ref[...]
        dx_ref[...] = dxm_ref[...] + r * (gdh - xhat * jnp.mean(xhat * gdh, axis=-1, keepdims=True))

    return pl.pallas_call(
        body,
        grid=(S // TM,),
        in_specs=[_rows(TM, D_MODEL), _whole(g), _rows(TM, D_MODEL), _rows(TM, D_QKV), _rows(TM, D_F), _rows(TM, D_B),
                  _whole(wa), _whole(wf), _whole(wb)],
        out_specs=[_rows(TM, D_MODEL), pl.BlockSpec((1, D_MODEL), lambda i: (0, 0))],
        out_shape=[jax.ShapeDtypeStruct((S, D_MODEL), F32), jax.ShapeDtypeStruct((1, D_MODEL), F32)],
        compiler_params=_params("arbitrary"),
        name="in_proj_bwd",
    )(x, g, dxm, dpa, dpf, dpb, wa, wf, wb)


TS_WGRAD = 1024


def _wgrad_columns(n):
    return next(tn for tn in (1024, 768, 512, 256, 128) if n % tn == 0)


def wgrad(xa, dy, name):
    S, K = xa.shape
    N = dy.shape[1]
    ts = min(TS_WGRAD, S)
    tn = _wgrad_columns(N)

    def body(x_ref, dy_ref, o_ref):
        @pl.when(pl.program_id(1) == 0)
        def _():
            o_ref[...] = jnp.zeros_like(o_ref)

        o_ref[...] += _dot_tn(x_ref[...], dy_ref[...])

    return pl.pallas_call(
        body,
        grid=(N // tn, S // ts),
        in_specs=[pl.BlockSpec((ts, K), lambda j, k: (k, 0)), pl.BlockSpec((ts, tn), lambda j, k: (k, j))],
        out_specs=pl.BlockSpec((K, tn), lambda j, k: (0, j)),
        out_shape=jax.ShapeDtypeStruct((K, N), F32),
        compiler_params=_params("parallel", "arbitrary"),
        name=name,
    )(xa, dy)


def wgrad_into(xa, dy, name, buf, l):
    S, K = xa.shape
    N = dy.shape[1]
    ts = min(TS_WGRAD, S)
    split = buf.ndim == 4
    tn = buf.shape[-1] if split else _wgrad_columns(N)
    place = _into_layer(buf)

    def body(l_ref, buf_ref, x_ref, dy_ref, o_ref):
        @pl.when(pl.program_id(1) == 0)
        def _():
            o_ref[...] = jnp.zeros_like(o_ref)

        o_ref[...] += _dot_tn(x_ref[...], dy_ref[...])

    if split:
        out_spec = pl.BlockSpec((None, None, K, tn), lambda j, k, l: (l[0], j, 0, 0))
    else:
        out_spec = pl.BlockSpec((None, K, tn), lambda j, k, l: (l[0], 0, j))
    return pl.pallas_call(
        body,
        grid_spec=pltpu.PrefetchScalarGridSpec(
            num_scalar_prefetch=1,
            grid=(N // tn, S // ts),
            in_specs=[place["in_spec"], pl.BlockSpec((ts, K), lambda j, k, l: (k, 0)),
                      pl.BlockSpec((ts, tn), lambda j, k, l: (k, j))],
            out_specs=out_spec,
        ),
        out_shape=place["out_shape"],
        input_output_aliases=place["aliases"],
        compiler_params=_params("parallel", "arbitrary"),
        name=name,
    )(_layer_index(l), buf, xa, dy)


def _head_mean_matrix():
    h = np.arange(D_ATTN) // HEAD_DIM
    return jnp.asarray((h[:, None] == h[None, :]).astype(np.float32) / HEAD_DIM, BF16)


def _place_matrix(lane0):
    m = np.zeros((N_PIECES * LANES, D_HEADS), np.float32)
    for i in range(N_PIECES):
        for h in range(N_HEADS):
            m[i * LANES + h, h * LANES + lane0 + i] = 1.0
    return jnp.asarray(m, BF16)


def _tri(n, upper):
    r = np.arange(n)
    m = (r[None, :] >= r[:, None]) if upper else (r[None, :] <= r[:, None])
    return jnp.asarray(m.astype(np.float32), BF16)


def _lanes_in(lane, lo, n):
    return (lane >= lo) & (lane < lo + n)


def qk_prep(pa, pf, gq, gk, fb):
    S = pa.shape[0]
    bd = _head_mean_matrix()
    tri = _tri(TM, upper=False)
    place_q = _place_matrix(LANE_C)
    place_k = _place_matrix(LANE_ONE)

    def body(q_ref, k_ref, v_ref, pf_ref, gq_ref, gk_ref, fb_ref, bd_ref, tri_ref, pq_ref, pk_ref,
             qx_ref, kx_ref, vx_ref, carry):
        @pl.when(pl.program_id(0) == 0)
        def _():
            carry[...] = jnp.zeros_like(carry)

        def head_norm(x_ref, g_ref, scale):
            xf = x_ref[...].astype(F32)
            ms = _dot((xf * xf).astype(BF16), bd_ref[...])
            return xf * lax.rsqrt(ms + EPS) * g_ref[...] * scale

        qh = head_norm(q_ref, gq_ref, HEAD_DIM ** -0.5 * LOG2E)
        kh = head_norm(k_ref, gk_ref, 1.0)
        vf = v_ref[...].astype(F32)

        z = pf_ref[...] + fb_ref[...]
        logf = jnp.minimum(z, 0.0) - jnp.log(1.0 + jnp.exp(-jnp.abs(z)))
        hi, lo = _split_bf16(logf)
        c = _dot(tri_ref[...], hi) + _dot(tri_ref[...], lo) + carry[...]
        carry[...] += jnp.sum(hi.astype(F32) + lo.astype(F32), axis=0, keepdims=True)
        pieces = jnp.concatenate(_pieces(c * LOG2E), axis=1)
        cq = _dot(pieces, pq_ref[...])
        ck = _dot(pieces, pk_ref[...])

        lane = lax.broadcasted_iota(jnp.int32, (TM, LANES), 1)
        low = lane < HEAD_DIM
        ones_q = _lanes_in(lane, LANE_ONE, N_PIECES).astype(F32)
        ones_k = (_lanes_in(lane, LANE_C, N_PIECES) | _lanes_in(lane, LANE_LSE, N_PIECES)).astype(F32)
        ones_v = _lanes_in(lane, LANE_C, N_PIECES + 1).astype(F32)
        for h in range(N_HEADS):
            blk = slice((h // 2) * LANES, (h // 2 + 1) * LANES)
            head = (lambda a: pltpu.roll(a[:, blk], HEAD_DIM, 1)) if h % 2 else (lambda a: a[:, blk])
            mine = slice(h * LANES, (h + 1) * LANES)
            qx_ref[h] = jnp.where(low, head(qh), cq[:, mine] + ones_q).astype(BF16)
            kx_ref[h] = jnp.where(low, head(kh), ones_k - ck[:, mine]).astype(BF16)
            vx_ref[h] = jnp.where(low, head(vf), ones_v).astype(BF16)

    heads = pl.BlockSpec((N_HEADS, TM, LANES), lambda i: (0, i, 0))
    out = jax.ShapeDtypeStruct((N_HEADS, S, LANES), BF16)
    return pl.pallas_call(
        body,
        grid=(S // TM,),
        in_specs=[pl.BlockSpec((TM, D_ATTN), lambda i: (i, 0)), pl.BlockSpec((TM, D_ATTN), lambda i: (i, 1)),
                  pl.BlockSpec((TM, D_ATTN), lambda i: (i, 2)),
                  _rows(TM, D_F), _whole(gq), _whole(gk), _whole(fb), _whole(bd), _whole(tri), _whole(place_q), _whole(place_k)],
        out_specs=[heads, heads, heads],
        out_shape=[out, out, out],
        scratch_shapes=[pltpu.VMEM((1, D_F), F32)],
        compiler_params=_params("arbitrary"),
        name="qk_prep",
    )(pa, pa, pa, pf, gq, gk, fb, bd, tri, place_q, place_k)


def attn_fwd(qx, kx, vx):
    S = qx.shape[1]
    nq = S // TQ

    def body(q_ref, k_ref, v_ref, o_ref, qb_ref):
        i = pl.program_id(1)
        lane = lax.broadcasted_iota(jnp.int32, (TQ, LANES), 1)
        row = lax.broadcasted_iota(jnp.int32, (TQ, TK), 0)
        col = lax.broadcasted_iota(jnp.int32, (TQ, TK), 1)
        q = [q_ref[0], q_ref[1]]
        n_full = (i * TQ) // TK

        def step(kt, carry, masked):
            ks = pl.multiple_of(kt * TK, TK)
            new = []
            for j in range(2):
                m, acc = carry[j]
                z = _dot_nt(q[j], k_ref[j, pl.ds(ks, TK), :])
                if masked:
                    z = jnp.where(row + i * TQ >= col + ks, z, NEG)
                m_new = jnp.maximum(m, jnp.max(z, axis=1, keepdims=True))
                pr = jnp.exp2(z - m_new)
                acc = jnp.exp2(m - m_new) * acc + _dot(pr.astype(BF16), v_ref[j, pl.ds(ks, TK), :])
                new.append((m_new, acc))
            return tuple(new)

        init = tuple((jnp.full((TQ, 1), NEG, F32), jnp.zeros((TQ, LANES), F32)) for _ in range(2))
        carry = lax.fori_loop(0, n_full, functools.partial(step, masked=False), init)
        carry = step(n_full, carry, True)
        for j in range(2):
            m, acc = carry[j]
            l = jnp.sum(jnp.where(lane == LANE_C, acc, 0.0), axis=1, keepdims=True)
            o_ref[j] = acc / l
            n1, n2, n3 = _pieces(-(m + jnp.log(l) * LOG2E))
            qb_ref[j] = jnp.where(lane == LANE_LSE, n1,
                                  jnp.where(lane == LANE_LSE + 1, n2, jnp.where(lane == LANE_LSE + 2, n3, q[j])))

    pair_tile = pl.BlockSpec((2, TQ, LANES), lambda p, i: (p, i, 0))
    pair_all = pl.BlockSpec((2, S, LANES), lambda p, i: (p, 0, 0))
    return pl.pallas_call(
        body,
        grid=(N_HEADS // 2, nq),
        in_specs=[pair_tile, pair_all, pair_all],
        out_specs=[pair_tile, pair_tile],
        out_shape=[jax.ShapeDtypeStruct((N_HEADS, S, LANES), F32), jax.ShapeDtypeStruct((N_HEADS, S, LANES), BF16)],
        compiler_params=_params("parallel", "parallel"),
        name="attn_fwd",
    )(qx, kx, vx)


def attn_bwd(qxb, kx, vx, dox):
    S = qxb.shape[1]
    nq = S // QT

    def body(q_ref, k_ref, v_ref, do_ref, dq_ref, dk_ref, dv_ref):
        kt = pl.program_id(1)

        @pl.when(kt == 0)
        def _():
            dq_ref[...] = jnp.zeros_like(dq_ref)

        row = lax.broadcasted_iota(jnp.int32, (QT, KB), 0)
        col = lax.broadcasted_iota(jnp.int32, (QT, KB), 1)
        k = [k_ref[0], k_ref[1]]
        v = [v_ref[0], v_ref[1]]
        q_first = (kt * KB) // QT

        def step(qi, carry, masked):
            qs = pl.multiple_of(qi * QT, QT)
            new = []
            for j in range(2):
                dk, dv = carry[j]
                q = q_ref[j, pl.ds(qs, QT), :]
                dout = do_ref[j, pl.ds(qs, QT), :]
                z = _dot_nt(q, k[j])
                if masked:
                    z = jnp.where(row + qs >= col + kt * KB, z, NEG)
                pr = jnp.exp2(z)
                dv = dv + _dot_tn(pr.astype(BF16), dout)
                dsb = (pr * _dot_nt(dout, v[j])).astype(BF16)
                dk = dk + _dot_tn(dsb, q)
                dq_ref[j, pl.ds(qs, QT), :] += _dot(dsb, k[j])
                new.append((dk, dv))
            return tuple(new)

        zero = jnp.zeros((KB, LANES), F32)
        carry = step(q_first, ((zero, zero), (zero, zero)), True)
        carry = lax.fori_loop(q_first + 1, nq, functools.partial(step, masked=False), carry)
        for j in range(2):
            dk_ref[j] = carry[j][0]
            dv_ref[j] = carry[j][1]

    pair_tile = pl.BlockSpec((2, KB, LANES), lambda p, kt: (p, kt, 0))
    pair_all = pl.BlockSpec((2, S, LANES), lambda p, kt: (p, 0, 0))
    out = jax.ShapeDtypeStruct((N_HEADS, S, LANES), F32)
    return pl.pallas_call(
        body,
        grid=(N_HEADS // 2, S // KB),
        in_specs=[pair_all, pair_tile, pair_tile, pair_all],
        out_specs=[pair_all, pair_tile, pair_tile],
        out_shape=[out, out, out],
        compiler_params=_params("arbitrary", "arbitrary"),
        name="attn_bwd",
    )(qxb, kx, vx, dox)


def attn_bwd_post(pa, pf, dqx, dkx, dvx, gq, gk, fb):
    S = pa.shape[0]
    nt = S // TM
    bd = _head_mean_matrix()
    triu = _tri(TM, upper=True)
    rev = lambda i: nt - 1 - i

    def body(q_ref, k_ref, pf_ref, dqx_ref, dkx_ref, dvx_ref, gq_ref, gk_ref, fb_ref, bd_ref, triu_ref,
             dpa_ref, dpf_ref, dgq_ref, dgk_ref, dfb_ref, carry):
        @pl.when(pl.program_id(0) == 0)
        def _():
            carry[...] = jnp.zeros_like(carry)
            dgq_ref[...] = jnp.zeros_like(dgq_ref)
            dgk_ref[...] = jnp.zeros_like(dgk_ref)
            dfb_ref[...] = jnp.zeros_like(dfb_ref)

        lane = lax.broadcasted_iota(jnp.int32, (TM, LANES), 1)

        def columns(ref):
            return jnp.concatenate([jnp.where(lane < HEAD_DIM, ref[2 * p], pltpu.roll(ref[2 * p + 1], HEAD_DIM, 1))
                                    for p in range(N_HEADS // 2)], axis=1)

        def head_norm_bwd(x_ref, dy, g_ref, dg_ref):
            xf = x_ref[...].astype(F32)
            r = lax.rsqrt(_dot((xf * xf).astype(BF16), bd_ref[...]) + EPS)
            xhat = xf * r
            dg_ref[...] += jnp.sum(dy * xhat, axis=0, keepdims=True)
            gdy = dy * g_ref[...]
            return (r * (gdy - xhat * _dot((xhat * gdy).astype(BF16), bd_ref[...]))).astype(BF16)

        dpa_ref[:, 0:D_ATTN] = head_norm_bwd(q_ref, columns(dqx_ref) * HEAD_DIM ** -0.5, gq_ref, dgq_ref)
        dpa_ref[:, D_ATTN:2 * D_ATTN] = head_norm_bwd(k_ref, columns(dkx_ref) * LN2, gk_ref, dgk_ref)
        dpa_ref[:, 2 * D_ATTN:3 * D_ATTN] = columns(dvx_ref).astype(BF16)

        dc = jnp.zeros((TM, LANES), F32)
        for h in range(N_HEADS):
            both = jnp.where(lane == LANE_C, dqx_ref[h], 0.0) - jnp.where(lane == LANE_ONE, dkx_ref[h], 0.0)
            dc = jnp.where(lane == h, jnp.sum(both, axis=1, keepdims=True), dc)
        hi, lo = _split_bf16(dc)
        dlogf = _dot(triu_ref[...], hi) + _dot(triu_ref[...], lo) + carry[...]
        first = lax.broadcasted_iota(jnp.int32, (TM, D_F), 0) == 0
        carry[...] = jnp.sum(jnp.where(first, dlogf, 0.0), axis=0, keepdims=True)
        df = dlogf * _sigmoid(-(pf_ref[...] + fb_ref[...]))
        dfb_ref[...] += jnp.sum(df, axis=0, keepdims=True)
        dpf_ref[...] = df.astype(BF16)

    heads = pl.BlockSpec((N_HEADS, TM, LANES), lambda i: (0, rev(i), 0))
    return pl.pallas_call(
        body,
        grid=(nt,),
        in_specs=[pl.BlockSpec((TM, D_ATTN), lambda i: (rev(i), 0)), pl.BlockSpec((TM, D_ATTN), lambda i: (rev(i), 1)),
                  pl.BlockSpec((TM, D_F), lambda i: (rev(i), 0)), heads, heads, heads,
                  _whole(gq), _whole(gk), _whole(fb), _whole(bd), _whole(triu)],
        out_specs=[pl.BlockSpec((TM, D_QKV), lambda i: (rev(i), 0)), pl.BlockSpec((TM, D_F), lambda i: (rev(i), 0)),
                   pl.BlockSpec((1, D_ATTN), lambda i: (0, 0)), pl.BlockSpec((1, D_ATTN), lambda i: (0, 0)),
                   pl.BlockSpec((1, D_F), lambda i: (0, 0))],
        out_shape=[jax.ShapeDtypeStruct((S, D_QKV), BF16), jax.ShapeDtypeStruct((S, D_F), BF16),
                   jax.ShapeDtypeStruct((1, D_ATTN), F32), jax.ShapeDtypeStruct((1, D_ATTN), F32),
                   jax.ShapeDtypeStruct((1, D_F), F32)],
        scratch_shapes=[pltpu.VMEM((1, D_F), F32)],
        compiler_params=_params("arbitrary"),
        name="attn_bwd_post",
    )(pa, pa, pf, dqx, dkx, dvx, gq, gk, fb, bd, triu)


def _shift_down(ext, k):
    return pltpu.roll(ext, k, 0)[HALO:]


def _shift_up(ext, k, n):
    return pltpu.roll(ext, n + HALO - k, 0)[:n]


def _pool_lane_select(a2, a4, a8, a16, lane):
    return jnp.where(lane < 64, a2, jnp.where(lane < 128, a4, jnp.where(lane < 192, a8, a16)))


def _local_branches(o, pb, halo, have_prev, row0, wao, convw, wco, wpool, pscale):
    n = pb.shape[0]
    cx = pb[:, 0:D_CONV].astype(F32)
    cb = pb[:, D_CONV:2 * D_CONV].astype(F32)
    cc = pb[:, 2 * D_CONV:3 * D_CONV].astype(F32)
    px = pb[:, 3 * D_CONV:D_LOCAL].astype(F32)
    keep = have_prev.astype(F32)
    z = cc * cx
    z_ext = jnp.concatenate([halo[:, 2 * D_CONV:3 * D_CONV].astype(F32) * halo[:, 0:D_CONV].astype(F32) * keep, z], axis=0)
    z1 = _shift_down(z_ext, 1)
    z2 = _shift_down(z_ext, 2)
    conv = convw[0:1, :] * z2 + convw[1:2, :] * z1 + convw[2:3, :] * z
    cm = cb * conv

    u_ext = jnp.concatenate([halo[:, 3 * D_CONV:D_LOCAL].astype(F32) * keep, px], axis=0)
    s2 = u_ext + pltpu.roll(u_ext, 1, 0)
    s4 = s2 + pltpu.roll(s2, 2, 0)
    s8 = s4 + pltpu.roll(s4, 4, 0)
    s16 = s8 + pltpu.roll(s8, 8, 0)
    lane = lax.broadcasted_iota(jnp.int32, (n, D_POOL), 1)
    win = _pool_lane_select(2.0, 4.0, 8.0, 16.0, lane)
    t = (row0 + lax.broadcasted_iota(jnp.int32, (n, D_POOL), 0)).astype(F32)
    cnt = jnp.minimum(t + 1.0, win)
    feat = _pool_lane_select(s2[HALO:], s4[HALO:], s8[HALO:], s16[HALO:], lane) / cnt - px

    ya = _dot(o, wao)
    yc = _dot(cm.astype(BF16), wco)
    yp_pre = _dot(feat.astype(BF16), wpool)
    yp = yp_pre * pscale
    return dict(cx=cx, cb=cb, cc=cc, z=z, z1=z1, z2=z2, conv=conv, cm=cm, feat=feat, cnt=cnt, lane=lane,
                ya=ya, yc=yc, yp_pre=yp_pre, yp=yp)


def _halo_spec(tm, tile_of):
    per = tm // HALO
    return pl.BlockSpec((HALO, D_LOCAL), lambda i, *_: (jnp.maximum(tile_of(i) * per - 1, 0), 0))


def _heads_as_columns(ref):
    return jnp.concatenate([ref[h] for h in range(N_HEADS)], axis=1)


def mix_out_fwd(x, ox, pb, wao, convw, wco, wpool, pscale, wo_all, l):
    S = x.shape[0]
    tm = TM_MIX

    def body(l_ref, x_ref, o_ref, pb_ref, halo_ref, wao_ref, cw_ref, wco_ref, wp_ref, ps_ref, wo_ref, y_ref):
        i = pl.program_id(0)
        pb = pb_ref[...]
        o = _heads_as_columns(o_ref).astype(BF16)
        b = _local_branches(o, pb, halo_ref[...], i > 0, i * tm, wao_ref[...], cw_ref[...], wco_ref[...],
                            wp_ref[...], ps_ref[...])
        g0 = _sigmoid(pb[:, D_LOCAL:D_LOCAL + D_MODEL].astype(F32))
        g1 = _sigmoid(pb[:, D_LOCAL + D_MODEL:D_LOCAL + 2 * D_MODEL].astype(F32))
        g2 = _sigmoid(pb[:, D_LOCAL + 2 * D_MODEL:D_B].astype(F32))
        merged = g0 * b["ya"] + g1 * b["yc"] + g2 * b["yp"]
        y_ref[...] = x_ref[...] + _dot(merged.astype(BF16), wo_ref[...])

    return pl.pallas_call(
        body,
        grid_spec=pltpu.PrefetchScalarGridSpec(
            num_scalar_prefetch=1,
            grid=(S // tm,),
            in_specs=[_rows(tm, D_MODEL), pl.BlockSpec((N_HEADS, tm, LANES), lambda i, l: (0, i, 0)), _rows(tm, D_B),
                      _halo_spec(tm, lambda i: i), _whole(wao), _whole(convw), _whole(wco), _whole(wpool), _whole(pscale),
                      _layer((D_MODEL, D_MODEL))],
            out_specs=_rows(tm, D_MODEL),
        ),
        out_shape=jax.ShapeDtypeStruct((S, D_MODEL), F32),
        compiler_params=_params("parallel"),
        name="mix_out_fwd",
    )(_layer_index(l), x, ox, pb, pb, wao, convw, wco, wpool, pscale, wo_all)


def mix_out_bwd(dxm, ox, pb, wao, convw, wco, wpool, pscale, wo_all, l):
    S = dxm.shape[0]
    tm = TM_MIX
    nt = S // tm
    rev = lambda i: nt - 1 - i
    rows = lambda n: pl.BlockSpec((tm, n), lambda i, l: (rev(i), 0))
    heads = pl.BlockSpec((N_HEADS, tm, LANES), lambda i, l: (0, rev(i), 0))
    acc = lambda r, c: pl.BlockSpec((r, c), lambda i, l: (0, 0))

    def body(l_ref, dxm_ref, o_ref, pb_ref, halo_ref, wao_ref, cw_ref, wco_ref, wp_ref, ps_ref, wo_ref,
             dpb_ref, do_ref, dwo_ref, dwao_ref, dwco_ref, dwp_ref, dcw_ref, dps_ref, next_dconv, next_e):
        i = pl.program_id(0)
        r = rev(i)

        @pl.when(i == 0)
        def _():
            for ref in (dwo_ref, dwao_ref, dwco_ref, dwp_ref, dcw_ref, dps_ref, next_dconv, next_e):
                ref[...] = jnp.zeros_like(ref)

        pb = pb_ref[...]
        o = _heads_as_columns(o_ref).astype(BF16)
        cw = cw_ref[...]
        b = _local_branches(o, pb, halo_ref[...], r > 0, r * tm, wao_ref[...], cw, wco_ref[...], wp_ref[...], ps_ref[...])
        g0 = _sigmoid(pb[:, D_LOCAL:D_LOCAL + D_MODEL].astype(F32))
        g1 = _sigmoid(pb[:, D_LOCAL + D_MODEL:D_LOCAL + 2 * D_MODEL].astype(F32))
        g2 = _sigmoid(pb[:, D_LOCAL + 2 * D_MODEL:D_B].astype(F32))
        dxb = dxm_ref[...].astype(BF16)
        merged = g0 * b["ya"] + g1 * b["yc"] + g2 * b["yp"]
        dwo_ref[...] += _dot_tn(merged.astype(BF16), dxb)
        dmer = _dot_nt(dxb, wo_ref[...])
        dpb_ref[:, D_LOCAL:D_LOCAL + D_MODEL] = (dmer * b["ya"] * (g0 * (1.0 - g0))).astype(BF16)
        dpb_ref[:, D_LOCAL + D_MODEL:D_LOCAL + 2 * D_MODEL] = (dmer * b["yc"] * (g1 * (1.0 - g1))).astype(BF16)
        dpb_ref[:, D_LOCAL + 2 * D_MODEL:D_B] = (dmer * b["yp"] * (g2 * (1.0 - g2))).astype(BF16)

        dya = (dmer * g0).astype(BF16)
        dwao_ref[...] += _dot_tn(o, dya)
        da = _dot_nt(dya, wao_ref[...]).astype(BF16)
        lane = lax.broadcasted_iota(jnp.int32, (tm, LANES), 1)
        for h in range(N_HEADS):
            dah = da[:, h * LANES:(h + 1) * LANES]
            d1, d2, d3 = _pieces(-jnp.sum(dah.astype(F32) * o_ref[h], axis=1, keepdims=True))
            do_ref[h] = jnp.where(lane == LANE_C + 1, d1, jnp.where(lane == LANE_C + 2, d2,
                                                                  jnp.where(lane == LANE_C + 3, d3, dah)))

        dyc = (dmer * g1).astype(BF16)
        dwco_ref[...] += _dot_tn(b["cm"].astype(BF16), dyc)
        dcm = _dot_nt(dyc, wco_ref[...])
        dconv = dcm * b["cb"]
        dcw_ref[0:1, :] += jnp.sum(dconv * b["z2"], axis=0, keepdims=True)
        dcw_ref[1:2, :] += jnp.sum(dconv * b["z1"], axis=0, keepdims=True)
        dcw_ref[2:3, :] += jnp.sum(dconv * b["z"], axis=0, keepdims=True)
        d_ext = jnp.concatenate([dconv, next_dconv[...]], axis=0)
        dz = cw[2:3, :] * dconv + cw[1:2, :] * _shift_up(d_ext, 1, tm) + cw[0:1, :] * _shift_up(d_ext, 2, tm)
        next_dconv[...] = dconv[0:HALO]
        dpb_ref[:, 0:D_CONV] = (dz * b["cc"]).astype(BF16)
        dpb_ref[:, D_CONV:2 * D_CONV] = (dcm * b["conv"]).astype(BF16)
        dpb_ref[:, 2 * D_CONV:3 * D_CONV] = (dz * b["cx"]).astype(BF16)

        dyp = dmer * g2
        dps_ref[...] += jnp.sum(dyp * b["yp_pre"], axis=0, keepdims=True)
        dyps = (dyp * ps_ref[...]).astype(BF16)
        dwp_ref[...] += _dot_tn(b["feat"].astype(BF16), dyps)
        dfeat = _dot_nt(dyps, wp_ref[...])
        e = dfeat / b["cnt"]
        e_ext = jnp.concatenate([e, next_e[...]], axis=0)
        up = lambda a, k: pltpu.roll(a, tm + HALO - k, 0)
        f2 = e_ext + up(e_ext, 1)
        f4 = f2 + up(f2, 2)
        f8 = f4 + up(f4, 4)
        f16 = f8 + up(f8, 8)
        next_e[...] = e[0:HALO]
        dpb_ref[:, 3 * D_CONV:D_LOCAL] = (_pool_lane_select(f2[:tm], f4[:tm], f8[:tm], f16[:tm], b["lane"]) - dfeat).astype(BF16)

    return pl.pallas_call(
        body,
        grid_spec=pltpu.PrefetchScalarGridSpec(
            num_scalar_prefetch=1,
            grid=(nt,),
            in_specs=[rows(D_MODEL), heads, rows(D_B), _halo_spec(tm, rev),
                      _whole(wao), _whole(convw), _whole(wco), _whole(wpool), _whole(pscale), _layer((D_MODEL, D_MODEL))],
            out_specs=[rows(D_B), heads, acc(D_MODEL, D_MODEL), acc(D_HEADS, D_MODEL), acc(D_CONV, D_MODEL),
                       acc(D_POOL, D_MODEL), acc(8, D_CONV), acc(1, D_MODEL)],
            scratch_shapes=[pltpu.VMEM((HALO, D_CONV), F32), pltpu.VMEM((HALO, D_POOL), F32)],
        ),
        out_shape=[jax.ShapeDtypeStruct((S, D_B), BF16), jax.ShapeDtypeStruct((N_HEADS, S, LANES), BF16),
                   jax.ShapeDtypeStruct((D_MODEL, D_MODEL), F32), jax.ShapeDtypeStruct((D_HEADS, D_MODEL), F32),
                   jax.ShapeDtypeStruct((D_CONV, D_MODEL), F32), jax.ShapeDtypeStruct((D_POOL, D_MODEL), F32),
                   jax.ShapeDtypeStruct((8, D_CONV), F32), jax.ShapeDtypeStruct((1, D_MODEL), F32)],
        compiler_params=_params("arbitrary"),
        name="mix_out_bwd",
    )(_layer_index(l), dxm, ox, pb, pb, wao, convw, wco, wpool, pscale, wo_all)


FF_SHARD = 2 * D_FF // 4


def ffn_fwd(x, g, w1_all, w2_all, l):
    S = x.shape[0]

    def body(l_ref, x_ref, g_ref, w1_ref, w2_ref, y_ref, u_ref):
        xf = x_ref[...]
        r = lax.rsqrt(jnp.mean(xf * xf, axis=-1, keepdims=True) + EPS)
        h = (xf * r * g_ref[...]).astype(BF16)
        u = jnp.concatenate([_dot(h, w1_ref[j]) for j in range(4)], axis=1)
        u_ref[...] = u.astype(BF16)
        gt = u[:, 0:D_FF]
        act = gt * _sigmoid(gt) * u[:, D_FF:2 * D_FF]
        y_ref[...] = xf + _dot(act.astype(BF16), w2_ref[...])

    return pl.pallas_call(
        body,
        grid_spec=pltpu.PrefetchScalarGridSpec(
            num_scalar_prefetch=1,
            grid=(S // TM,),
            in_specs=[_rows(TM, D_MODEL), _whole(g), _layer((4, D_MODEL, FF_SHARD)), _layer((D_FF, D_MODEL))],
            out_specs=[_rows(TM, D_MODEL), _rows(TM, 2 * D_FF)],
        ),
        out_shape=[jax.ShapeDtypeStruct((S, D_MODEL), F32), jax.ShapeDtypeStruct((S, 2 * D_FF), BF16)],
        compiler_params=_params("parallel"),
        name="ffn_fwd",
    )(_layer_index(l), x, g, w1_all, w2_all)


def ffn_bwd(x, g, dy, u, w1_all, w2_all, l):
    S = x.shape[0]

    def body(l_ref, x_ref, g_ref, dy_ref, u_ref, w1_ref, w2_ref, dx_ref, du_ref, act_ref, h_ref, dg_ref):
        @pl.when(pl.program_id(0) == 0)
        def _():
            dg_ref[...] = jnp.zeros_like(dg_ref)

        dyf = dy_ref[...]
        dact = _dot_nt(dyf.astype(BF16), w2_ref[...])
        gt = u_ref[:, 0:D_FF].astype(F32)
        up = u_ref[:, D_FF:2 * D_FF].astype(F32)
        sg = _sigmoid(gt)
        silu = gt * sg
        act_ref[...] = (silu * up).astype(BF16)
        du_ref[:, 0:D_FF] = (dact * up * (sg * (1.0 + gt * (1.0 - sg)))).astype(BF16)
        du_ref[:, D_FF:2 * D_FF] = (dact * silu).astype(BF16)
        dh = _dot_nt(du_ref[:, 0:FF_SHARD], w1_ref[0])
        for j in range(1, 4):
            dh = dh + _dot_nt(du_ref[:, j * FF_SHARD:(j + 1) * FF_SHARD], w1_ref[j])
        xf = x_ref[...]
        r = lax.rsqrt(jnp.mean(xf * xf, axis=-1, keepdims=True) + EPS)
        xhat = xf * r
        h_ref[...] = (xhat * g_ref[...]).astype(BF16)
        dg_ref[...] += jnp.sum(dh * xhat, axis=0, keepdims=True)
        gdh = dh * g_ref[...]
        dx_ref[...] = dyf + r * (gdh - xhat * jnp.mean(xhat * gdh, axis=-1, keepdims=True))

    return pl.pallas_call(
        body,
        grid_spec=pltpu.PrefetchScalarGridSpec(
            num_scalar_prefetch=1,
            grid=(S // TM,),
            in_specs=[_rows(TM, D_MODEL), _whole(g), _rows(TM, D_MODEL), _rows(TM, 2 * D_FF),
                      _layer((4, D_MODEL, FF_SHARD)), _layer((D_FF, D_MODEL))],
            out_specs=[_rows(TM, D_MODEL), _rows(TM, 2 * D_FF), _rows(TM, D_FF), _rows(TM, D_MODEL),
                       pl.BlockSpec((1, D_MODEL), lambda i, l: (0, 0))],
        ),
        out_shape=[jax.ShapeDtypeStruct((S, D_MODEL), F32), jax.ShapeDtypeStruct((S, 2 * D_FF), BF16),
                   jax.ShapeDtypeStruct((S, D_FF), BF16), jax.ShapeDtypeStruct((S, D_MODEL), BF16),
                   jax.ShapeDtypeStruct((1, D_MODEL), F32)],
        compiler_params=_params("arbitrary"),
        name="ffn_bwd",
    )(_layer_index(l), x, g, dy, u, w1_all, w2_all)


def loss_head(y, target):
    S = y.shape[0]

    def body(y_ref, t_ref, loss_ref, dy_ref):
        @pl.when(pl.program_id(0) == 0)
        def _():
            loss_ref[0, 0] = 0.0

        err = y_ref[...] - t_ref[...]
        dy_ref[...] = err * (1.0 / D_MODEL)
        loss_ref[0, 0] += 0.5 * jnp.sum(jnp.mean(err * err, axis=-1))

    return pl.pallas_call(
        body,
        grid=(S // TM,),
        in_specs=[_rows(TM, D_MODEL), _rows(TM, D_MODEL)],
        out_specs=[pl.BlockSpec((1, 1), lambda i: (0, 0), memory_space=pltpu.SMEM), _rows(TM, D_MODEL)],
        out_shape=[jax.ShapeDtypeStruct((1, 1), F32), jax.ShapeDtypeStruct((S, D_MODEL), F32)],
        compiler_params=_params("arbitrary"),
        name="loss_head",
    )(y, target)


SPLIT = {
    "w_in": ((D_MODEL, D_IN), 1),
    "w_attn_out": ((D_ATTN, D_MODEL), 1),
    "w_conv_out": ((D_CONV, D_MODEL), 1),
    "pool_w": ((N_GROUPS, D_POOL // N_GROUPS, D_MODEL // N_GROUPS), 2),
    "w_o": ((D_MODEL, D_MODEL), 0),
    "w_ffn_in": ((D_MODEL, 2 * D_FF), 1),
    "w_ffn_out": ((D_FF, D_MODEL), 0),
}
SMALL = {"norm_mix_g": D_MODEL, "forget_b": N_HEADS, "q_norm_g": HEAD_DIM, "k_norm_g": HEAD_DIM,
         "pool_scale": D_MODEL, "norm_ffn_g": D_MODEL, "conv_w": CONV_K * D_CONV}
WEIGHTS = ["norm_mix_g", "w_in", "forget_b", "q_norm_g", "k_norm_g", "w_attn_out", "conv_w", "w_conv_out", "pool_w",
           "pool_scale", "w_o", "norm_ffn_g", "w_ffn_in", "w_ffn_out"]


def layer_fwd(x, full, li, small):
    n = full["w_o"].shape[0]
    row = lambda a: a.astype(F32).reshape(1, -1)
    wa, wf, wb = w_in_prep(full["w_in"], li)
    wao, wco, wpool, convw = branch_w_prep(full["w_attn_out"], full["w_conv_out"], full["pool_w"], full["conv_w"], li)
    w = dict(wa=wa, wf=wf, wb=wb, wao=wao, wco=wco, wpool=wpool, convw=convw,
             w_o=full["w_o"].reshape(n, D_MODEL, D_MODEL), w1=full["w_ffn_in"], w2=full["w_ffn_out"].reshape(n, D_FF, D_MODEL),
             g1=row(small["norm_mix_g"]), g2=row(small["norm_ffn_g"]), pscale=row(small["pool_scale"]),
             gq=row(jnp.tile(small["q_norm_g"], N_HEADS)), gk=row(jnp.tile(small["k_norm_g"], N_HEADS)),
             fb=row(jnp.pad(small["forget_b"], (0, D_F - N_HEADS))))
    h, pa, pf, pb = in_proj_fwd(x, w["g1"], wa, wf, wb)
    qx, kx, vx = qk_prep(pa, pf, w["gq"], w["gk"], w["fb"])
    ox, qxb = attn_fwd(qx, kx, vx)
    xm = mix_out_fwd(x, ox, pb, wao, convw, wco, wpool, w["pscale"], w["w_o"], li)
    y, u = ffn_fwd(xm, w["g2"], w["w1"], w["w2"], li)
    return y, w, dict(x=x, h=h, pa=pa, pf=pf, pb=pb, kx=kx, vx=vx, ox=ox, qxb=qxb, xm=xm, u=u)


def layer_bwd(dx, w, s, wi, gi, big):
    n = big["w_o"].shape[0]
    big = dict(big)
    dxm, du, act, h2, dg2 = ffn_bwd(s["xm"], w["g2"], dx, s["u"], w["w1"], w["w2"], wi)
    big["w_ffn_out"] = wgrad_into(act, dx.astype(BF16), "wgrad_ffn_out", big["w_ffn_out"].reshape(n, D_FF, D_MODEL),
                                  gi).reshape(big["w_ffn_out"].shape)
    big["w_ffn_in"] = wgrad_into(h2, du, "wgrad_ffn_in", big["w_ffn_in"], gi)
    dpb, dox, dwo, dwao, dwco, dwpool, dconvw, dpscale = mix_out_bwd(
        dxm, s["ox"], s["pb"], w["wao"], w["convw"], w["wco"], w["wpool"], w["pscale"], w["w_o"], wi)
    dqx, dkx, dvx = attn_bwd(s["qxb"], s["kx"], s["vx"], dox)
    dpa, dpf, dgq, dgk, dfb = attn_bwd_post(s["pa"], s["pf"], dqx, dkx, dvx, w["gq"], w["gk"], w["fb"])
    dx, dg1 = in_proj_bwd(s["x"], w["g1"], dxm, dpa, dpf, dpb, w["wa"], w["wf"], w["wb"])
    big["w_in"] = w_in_unprep(wgrad(s["h"], dpa, "wgrad_in_qkv"), wgrad(s["h"], dpf, "wgrad_in_f"),
                              wgrad(s["h"], dpb, "wgrad_in_b"), big["w_in"], gi)
    big["w_attn_out"], big["w_conv_out"], big["pool_w"], big["w_o"] = branch_g_place(
        dwao, dwco, dwpool, dwo, (big["w_attn_out"], big["w_conv_out"], big["pool_w"], big["w_o"]), gi)
    sm = dict(norm_mix_g=dg1[0], forget_b=dfb[0, 0:N_HEADS], q_norm_g=dgq.reshape(N_HEADS, HEAD_DIM).sum(0),
              k_norm_g=dgk.reshape(N_HEADS, HEAD_DIM).sum(0), pool_scale=dpscale[0], norm_ffn_g=dg2[0],
              conv_w=dconvw[0:CONV_K].reshape(-1))
    return dx, big, sm


def grad_buffers(full):
    return {n: lax.empty(full[n].shape, F32) for n in SPLIT}


def local_step(x, target, gathered, small):
    n_layers = small["norm_mix_g"].shape[0]
    done = []
    for l in range(n_layers):
        x, w, s = layer_fwd(x, gathered, l, {n: v[l] for n, v in small.items()})
        done.append((w, s))
    loss, dx = loss_head(x, target)
    big = grad_buffers(gathered)
    sm = [None] * n_layers
    for l in reversed(range(n_layers)):
        dx, big, sm[l] = layer_bwd(dx, *done[l], l, l, big)
    return loss, dx, big, {n: jnp.stack([g[n] for g in sm]) for n in SMALL}


def adamw(w, g, m, v, name):
    R, C = w.shape
    tm = 256 if R % 256 == 0 else R

    def body(w_ref, g_ref, m_ref, v_ref, d_ref, nm_ref, nv_ref):
        gr = g_ref[...]
        m_new = ADAM_B1 * m_ref[...] + (1.0 - ADAM_B1) * gr
        v_new = ADAM_B2 * v_ref[...] + (1.0 - ADAM_B2) * jnp.square(gr)
        nm_ref[...] = m_new
        nv_ref[...] = v_new
        m_hat = m_new / (1.0 - ADAM_B1 ** ADAM_STEP)
        v_hat = v_new / (1.0 - ADAM_B2 ** ADAM_STEP)
        d_ref[...] = -ADAM_LR * (m_hat / (jnp.sqrt(v_hat) + ADAM_EPS) + ADAM_WD * w_ref[...])

    spec = _rows(tm, C)
    out = jax.ShapeDtypeStruct((R, C), F32)
    return pl.pallas_call(
        body,
        grid=(R // tm,),
        in_specs=[spec] * 4,
        out_specs=[spec] * 3,
        out_shape=[out] * 3,
        compiler_params=_params("parallel"),
        name=name,
    )(w, g, m, v)


MESH = pl.DeviceIdType.MESH
HBM_REF = pl.BlockSpec(memory_space=pl.ANY)
N_CHIPS = 4
N_DEV = 8
FWD_GROUPS = ((0, 1), (1, 2), (2, 4))
BWD_GROUPS = ((0, 1), (1, 4))
SMALL_SHAPE = (128, LANES)


def _mesh_pos():
    return lax.axis_index("x"), lax.axis_index("y"), lax.axis_index("c")


def _other_chips(x, y):
    return [(1 - x, y), (x, 1 - y), (1 - x, 1 - y)]


def _remote(src, dst, send_sem, recv_sem, to):
    return pltpu.make_async_remote_copy(src_ref=src, dst_ref=dst, send_sem=send_sem, recv_sem=recv_sem,
                                        device_id=to, device_id_type=MESH)


def _row_tile(rows):
    for tm in (256, 176, 128):
        if rows % tm == 0:
            return tm
    return rows


def _as4(a):
    return a.reshape(a.shape[0], a.shape[1], -1, a.shape[-1])


def _core_rows(buf, c):
    R = buf.shape[2]
    if R % 2:
        return pl.ds(0, R), pl.ds(0, R), False
    return pl.ds(c * (R // 2), R // 2), pl.ds((1 - c) * (R // 2), R // 2), True


def place_shard(chip, w, lo, hi, dtype, name, after=None):
    w3 = w.reshape(w.shape[0], -1, w.shape[-1])
    _, R, C = w3.shape
    tm = _row_tile(R)
    idle = [] if after is None else [after]

    def body(chip_ref, w_ref, *rest):
        rest[-1][...] = w_ref[...].astype(dtype)

    return pl.pallas_call(
        body,
        grid_spec=pltpu.PrefetchScalarGridSpec(
            num_scalar_prefetch=1,
            grid=(hi - lo, R // tm),
            in_specs=[pl.BlockSpec((None, tm, C), lambda l, i, chip: (lo + l, i, 0))] + [_whole(a) for a in idle],
            out_specs=pl.BlockSpec((None, None, tm, C), lambda l, i, chip: (l, chip[0], i, 0)),
        ),
        out_shape=jax.ShapeDtypeStruct((hi - lo, N_CHIPS, R, C), dtype),
        compiler_params=_params("parallel", "parallel"),
        name=name,
    )(chip, w3, *idle)


HBM_SPACE = pl.BlockSpec(memory_space=pltpu.HBM)
SEM_SPACE = pl.BlockSpec(memory_space=pltpu.SEMAPHORE)
IN_FLIGHT = pltpu.SideEffectType.DATAFLOW_SIDE_EFFECTING


def _sem_table(send_sems, recv_sems):
    return lambda t, j: (send_sems.at[t, j], recv_sems.at[t, j])


def _sem_per_peer(send_sems, recv_sems):
    return lambda t, j: (send_sems[j], recv_sems[j])


def _gather_ici(bufs, sem, sends=True, lands=True):
    x, y, c = _mesh_pos()
    me = 2 * x + y
    out, into = [], []
    for t, buf in enumerate(bufs):
        mine, _, _ = _core_rows(buf, c)
        part = lambda k, buf=buf, mine=mine: buf.at[pl.ds(0, buf.shape[0]), k, mine]
        for j, (px, py) in enumerate(_other_chips(x, y)):
            if sends:
                out.append(_remote(part(me), part(me), *sem(t, j), (px, py, c)))
            if lands:
                into.append(_remote(part(2 * px + py), part(2 * px + py), *sem(t, j), (px, py, c)))
    return out, into


def _gather_d2d(bufs, send_sems, recv_sems, first):
    x, y, c = _mesh_pos()
    sends, lands = [], []
    for t, buf in enumerate(bufs):
        mine, theirs, split = _core_rows(buf, c)
        if not split:
            continue
        for j, (px, py) in enumerate(_other_chips(x, y)):
            part = lambda rows, buf=buf, k=2 * px + py: buf.at[pl.ds(0, buf.shape[0]), k, rows]
            sems = (send_sems.at[t, first + j], recv_sems.at[t, first + j], (x, y, 1 - c))
            sends.append(_remote(part(mine), part(mine), *sems))
            lands.append(_remote(part(theirs), part(theirs), *sems))
    return sends, lands


def gather_shards(bufs):
    n = len(bufs)

    def body(*refs):
        outs = refs[n:2 * n]
        send_sems, recv_sems = refs[2 * n:]
        ici_out, ici_in = _gather_ici(outs, _sem_table(send_sems, recv_sems))
        d2d_out, d2d_in = _gather_d2d(outs, send_sems, recv_sems, 3)
        for cp in ici_out:
            cp.start()
        for cp in ici_in:
            cp.wait_recv()
        for cp in d2d_out:
            cp.start()
        for cp in d2d_in:
            cp.wait_recv()
        for cp in ici_out + d2d_out:
            cp.wait_send()

    return pl.pallas_call(
        body,
        in_specs=[HBM_REF] * n,
        out_specs=[HBM_REF] * n,
        out_shape=[jax.ShapeDtypeStruct(b.shape, b.dtype) for b in bufs],
        input_output_aliases={t: t for t in range(n)},
        scratch_shapes=[pltpu.SemaphoreType.DMA((n, 6)), pltpu.SemaphoreType.DMA((n, 6))],
        name="gather_shards",
    )(*bufs)


def gather_start(bufs):
    n = len(bufs)

    def body(*refs):
        send_sems, recv_sems = refs[n:n + 3], refs[n + 3:n + 6]
        outs = refs[n + 6:2 * n + 6]
        token = refs[2 * n + 6]
        for cp in _gather_ici(outs, _sem_per_peer(send_sems, recv_sems), lands=False)[0]:
            cp.start()
        token[...] = jnp.zeros_like(token)

    res = pl.pallas_call(
        body,
        in_specs=[HBM_SPACE] * n,
        out_specs=[SEM_SPACE] * 6 + [HBM_SPACE] * n + [pl.BlockSpec(memory_space=pltpu.VMEM)],
        out_shape=[pltpu.SemaphoreType.DMA(())] * 6
        + [pltpu.HBM(b.shape, b.dtype) for b in bufs] + [jax.ShapeDtypeStruct((8, LANES), F32)],
        input_output_aliases={t: t + 6 for t in range(n)},
        compiler_params=pltpu.CompilerParams(has_side_effects=IN_FLIGHT),
        name="gather_start",
    )(*[pltpu.with_memory_space_constraint(b, pltpu.HBM) for b in bufs])
    return list(res[0:3]), list(res[3:6]), list(res[6:n + 6]), res[n + 6]


def gather_wait(send_sems, recv_sems, bufs, after):
    n = len(bufs)

    def body(*refs):
        ins_send, ins_recv = refs[n:n + 3], refs[n + 3:n + 6]
        outs = refs[n + 7:]
        sends, lands = _gather_ici(outs, _sem_per_peer(ins_send, ins_recv))
        for cp in lands:
            cp.wait_recv()
        for cp in sends:
            cp.wait_send()

    return list(pl.pallas_call(
        body,
        in_specs=[HBM_SPACE] * n + [SEM_SPACE] * 6 + [HBM_REF],
        out_specs=[HBM_SPACE] * n,
        out_shape=[pltpu.HBM(b.shape, b.dtype) for b in bufs],
        input_output_aliases={t: t for t in range(n)},
        compiler_params=pltpu.CompilerParams(has_side_effects=IN_FLIGHT),
        name="gather_wait",
    )(*bufs, *send_sems, *recv_sems, after))


def gather_forward(bufs):
    n = len(bufs)

    def body(*refs):
        outs = refs[n:2 * n]
        send_sems, recv_sems = refs[2 * n:]
        sends, lands = _gather_d2d(outs, send_sems, recv_sems, 0)
        for cp in sends:
            cp.start()
        for cp in lands:
            cp.wait_recv()
        for cp in sends:
            cp.wait_send()

    return list(pl.pallas_call(
        body,
        in_specs=[HBM_REF] * n,
        out_specs=[HBM_REF] * n,
        out_shape=[jax.ShapeDtypeStruct(b.shape, b.dtype) for b in bufs],
        input_output_aliases={t: t for t in range(n)},
        scratch_shapes=[pltpu.SemaphoreType.DMA((n, 3)), pltpu.SemaphoreType.DMA((n, 3))],
        name="gather_forward",
    )(*bufs))


def pair_exchange(grads):
    n = len(grads)

    def body(*refs):
        ins, outs = refs[:n], refs[n:2 * n]
        send_sems, recv_sems = refs[2 * n:]
        x, y, c = _mesh_pos()
        cps = []
        for t in range(n):
            _, theirs, _ = _core_rows(ins[t], c)
            src = ins[t].at[pl.ds(0, ins[t].shape[0]), pl.ds(0, N_CHIPS), theirs]
            cps.append(_remote(src, outs[t], send_sems.at[t], recv_sems.at[t], (x, y, 1 - c)))
        for cp in cps:
            cp.start()
        for cp in cps:
            cp.wait()

    return pl.pallas_call(
        body,
        in_specs=[HBM_REF] * n,
        out_specs=[HBM_REF] * n,
        out_shape=[jax.ShapeDtypeStruct(g.shape[:2] + (g.shape[2] // 2, g.shape[3]), g.dtype) for g in grads],
        scratch_shapes=[pltpu.SemaphoreType.DMA((n,)), pltpu.SemaphoreType.DMA((n,))],
        name="pair_exchange",
    )(*grads)


def pair_sum(core, g, t, name):
    n, _, half, C = t.shape
    tm = _row_tile(half)
    per = half // tm

    def body(c_ref, g_ref, t_ref, o_ref):
        o_ref[...] = (g_ref[...] + t_ref[...]).astype(BF16)

    tile = pl.BlockSpec((None, tm, C), lambda a, i, c: (a, i, 0))
    out = pl.pallas_call(
        body,
        grid_spec=pltpu.PrefetchScalarGridSpec(
            num_scalar_prefetch=1,
            grid=(n * N_CHIPS, per),
            in_specs=[pl.BlockSpec((None, tm, C), lambda a, i, c: (a, per * c[0] + i, 0)), tile],
            out_specs=tile,
        ),
        out_shape=jax.ShapeDtypeStruct((n * N_CHIPS, half, C), BF16),
        compiler_params=_params("parallel", "parallel"),
        name=name,
    )(core, g.reshape(n * N_CHIPS, 2 * half, C), t.reshape(n * N_CHIPS, half, C))
    return out.reshape(t.shape)


def _chip_copies(sums, recv, sem):
    x, y, c = _mesh_pos()
    cps = []
    for t in range(len(sums)):
        for j, (px, py) in enumerate(_other_chips(x, y)):
            src = sums[t].at[pl.ds(0, sums[t].shape[0]), 2 * px + py]
            cps.append(_remote(src, recv[t].at[j], *sem(t, j), (px, py, c)))
    return cps


def _recv_shape(s):
    return (N_CHIPS - 1, s.shape[0]) + s.shape[2:]


def chip_exchange(sums):
    n = len(sums)

    def body(*refs):
        ins, outs = refs[:n], refs[n:2 * n]
        cps = _chip_copies(ins, outs, _sem_table(*refs[2 * n:]))
        for cp in cps:
            cp.start()
        for cp in cps:
            cp.wait()

    return list(pl.pallas_call(
        body,
        in_specs=[HBM_REF] * n,
        out_specs=[HBM_REF] * n,
        out_shape=[jax.ShapeDtypeStruct(_recv_shape(s), s.dtype) for s in sums],
        scratch_shapes=[pltpu.SemaphoreType.DMA((n, 3)), pltpu.SemaphoreType.DMA((n, 3))],
        name="chip_exchange",
    )(*sums))


def chip_exchange_start(sums):
    n = len(sums)
    lands = [lax.empty(_recv_shape(s), s.dtype) for s in sums]

    def body(*refs):
        send_sems, recv_sems = refs[2 * n:2 * n + 3], refs[2 * n + 3:2 * n + 6]
        outs = refs[2 * n + 6:4 * n + 6]
        token = refs[4 * n + 6]
        for cp in _chip_copies(outs[:n], outs[n:2 * n], _sem_per_peer(send_sems, recv_sems)):
            cp.start()
        token[...] = jnp.zeros_like(token)

    res = pl.pallas_call(
        body,
        in_specs=[HBM_SPACE] * (2 * n),
        out_specs=[SEM_SPACE] * 6 + [HBM_SPACE] * (2 * n) + [pl.BlockSpec(memory_space=pltpu.VMEM)],
        out_shape=[pltpu.SemaphoreType.DMA(())] * 6 + [pltpu.HBM(a.shape, a.dtype) for a in list(sums) + lands]
        + [jax.ShapeDtypeStruct((8, LANES), F32)],
        input_output_aliases={t: t + 6 for t in range(2 * n)},
        compiler_params=pltpu.CompilerParams(has_side_effects=IN_FLIGHT),
        name="chip_exchange_start",
    )(*[pltpu.with_memory_space_constraint(a, pltpu.HBM) for a in list(sums) + lands])
    return list(res[0:3]), list(res[3:6]), list(res[6:n + 6]), list(res[n + 6:2 * n + 6]), res[2 * n + 6]


def chip_exchange_wait(send_sems, recv_sems, sums, lands, after):
    n = len(sums)

    def body(*refs):
        ins_send, ins_recv = refs[2 * n:2 * n + 3], refs[2 * n + 3:2 * n + 6]
        outs = refs[2 * n + 7:]
        for cp in _chip_copies(outs[:n], outs[n:], _sem_per_peer(ins_send, ins_recv)):
            cp.wait()

    res = pl.pallas_call(
        body,
        in_specs=[HBM_SPACE] * (2 * n) + [SEM_SPACE] * 6 + [HBM_REF],
        out_specs=[HBM_SPACE] * (2 * n),
        out_shape=[pltpu.HBM(a.shape, a.dtype) for a in list(sums) + list(lands)],
        input_output_aliases={t: t for t in range(2 * n)},
        compiler_params=pltpu.CompilerParams(has_side_effects=IN_FLIGHT),
        name="chip_exchange_wait",
    )(*sums, *lands, *send_sems, *recv_sems, after)
    return list(res[:n]), list(res[n:])


def chip_sum_into(core, chip, recv, sums, total, lo, name):
    _, n, half, C = recv.shape
    tm = _row_tile(half)
    per = half // tm

    def body(c_ref, k_ref, r_ref, s_ref, t_ref, o_ref):
        acc = s_ref[...].astype(F32)
        for j in range(N_CHIPS - 1):
            acc = acc + r_ref[j].astype(F32)
        o_ref[...] = acc

    return pl.pallas_call(
        body,
        grid_spec=pltpu.PrefetchScalarGridSpec(
            num_scalar_prefetch=2,
            grid=(n, per),
            in_specs=[pl.BlockSpec((N_CHIPS - 1, None, tm, C), lambda a, i, c, k: (0, a, i, 0)),
                      pl.BlockSpec((None, None, tm, C), lambda a, i, c, k: (a, k[0], i, 0)),
                      HBM_REF],
            out_specs=pl.BlockSpec((None, tm, C), lambda a, i, c, k: (lo + a, per * c[0] + i, 0)),
        ),
        out_shape=jax.ShapeDtypeStruct(total.shape, F32),
        input_output_aliases={4: 0},
        compiler_params=_params("parallel", "parallel"),
        name=name,
    )(core, chip, recv, sums, total)


def sibling_share(totals):
    n = len(totals)

    def body(*refs):
        outs = refs[n:2 * n]
        send_sems, recv_sems = refs[2 * n:]
        x, y, c = _mesh_pos()
        half = lambda t, h: outs[t].at[pl.ds(0, DEPTH), pl.ds(h * (outs[t].shape[1] // 2), outs[t].shape[1] // 2)]
        sent = [_remote(half(t, c), half(t, c), send_sems.at[t], recv_sems.at[t], (x, y, 1 - c)) for t in range(n)]
        for cp in sent:
            cp.start()
        for t in range(n):
            _remote(half(t, 1 - c), half(t, 1 - c), send_sems.at[t], recv_sems.at[t], (x, y, 1 - c)).wait_recv()
        for cp in sent:
            cp.wait_send()

    return pl.pallas_call(
        body,
        in_specs=[HBM_REF] * n,
        out_specs=[HBM_REF] * n,
        out_shape=[jax.ShapeDtypeStruct(t.shape, t.dtype) for t in totals],
        input_output_aliases={t: t for t in range(n)},
        scratch_shapes=[pltpu.SemaphoreType.DMA((n,)), pltpu.SemaphoreType.DMA((n,))],
        name="sibling_share",
    )(*totals)


def small_allgather(small):
    def body(s_ref, a_ref, send_sems, recv_sems, local_sem):
        x, y, c = _mesh_pos()
        me = 4 * x + 2 * y + c
        own = pltpu.make_async_copy(s_ref, a_ref.at[me], local_sem)
        own.start()
        sent = []
        for k in range(1, N_DEV):
            peer = (x ^ (k >> 2), y ^ ((k >> 1) & 1), c ^ (k & 1))
            cp = _remote(s_ref, a_ref.at[me], send_sems.at[k - 1], recv_sems.at[k - 1], peer)
            cp.start()
            sent.append(cp)
        for k in range(1, N_DEV):
            px, py, pc = x ^ (k >> 2), y ^ ((k >> 1) & 1), c ^ (k & 1)
            _remote(s_ref, a_ref.at[4 * px + 2 * py + pc], send_sems.at[k - 1], recv_sems.at[k - 1], (px, py, pc)).wait_recv()
        for cp in sent:
            cp.wait_send()
        own.wait()

    return pl.pallas_call(
        body,
        in_specs=[HBM_REF],
        out_specs=HBM_REF,
        out_shape=jax.ShapeDtypeStruct((N_DEV,) + SMALL_SHAPE, small.dtype),
        scratch_shapes=[pltpu.SemaphoreType.DMA((N_DEV - 1,)), pltpu.SemaphoreType.DMA((N_DEV - 1,)), pltpu.SemaphoreType.DMA],
        name="small_allgather",
    )(small)


def small_sum(blocks):
    def body(a_ref, o_ref):
        acc = a_ref[0]
        for d in range(1, N_DEV):
            acc = acc + a_ref[d]
        o_ref[...] = acc

    return pl.pallas_call(
        body,
        in_specs=[pl.BlockSpec(memory_space=pltpu.VMEM)],
        out_specs=pl.BlockSpec(memory_space=pltpu.VMEM),
        out_shape=jax.ShapeDtypeStruct(SMALL_SHAPE, F32),
        name="small_sum",
    )(blocks)


def pack_small(grads, loss):
    flat = jnp.concatenate([grads[n].reshape(-1) for n in SMALL] + [loss.reshape(-1)])
    size = SMALL_SHAPE[0] * SMALL_SHAPE[1]
    return jnp.pad(flat, (0, size - flat.shape[0])).reshape(SMALL_SHAPE)


def unpack_small(packed):
    flat = packed.reshape(-1)
    out, off = {}, 0
    for n, size in SMALL.items():
        out[n] = flat[off:off + DEPTH * size].reshape(DEPTH, size)
        off += DEPTH * size
    return out, flat[off]


def kernel(x, norm_mix_g, w_in, forget_b, q_norm_g, k_norm_g, w_attn_out, conv_w, w_conv_out, pool_w, pool_scale, w_o, norm_ffn_g, w_ffn_in, w_ffn_out, loss_target, m_norm_mix_g, m_w_in, m_forget_b, m_q_norm_g, m_k_norm_g, m_w_attn_out, m_conv_w, m_w_conv_out, m_pool_w, m_pool_scale, m_w_o, m_norm_ffn_g, m_w_ffn_in, m_w_ffn_out, v_norm_mix_g, v_w_in, v_forget_b, v_q_norm_g, v_k_norm_g, v_w_attn_out, v_conv_w, v_w_conv_out, v_pool_w, v_pool_scale, v_w_o, v_norm_ffn_g, v_w_ffn_in, v_w_ffn_out):
    given = dict(locals())
    weights = {n: given[n] for n in WEIGHTS}

    core = lax.axis_index("c").astype(jnp.int32)
    chip = (2 * lax.axis_index("x") + lax.axis_index("y")).astype(jnp.int32)

    core, chip = core.reshape(1), chip.reshape(1)
    names = list(SPLIT) + ["conv_w"]

    def placed(lo, hi, after=None):
        return [place_shard(chip, weights[n], lo, hi, F32 if n == "conv_w" else BF16, "place_" + n,
                            after["conv_w"][0, 0] if n == "conv_w" and after is not None else None) for n in names]

    def as_weights(bufs):
        return {n: b.reshape(b.shape[:2] + weights[n].shape[1:]) for n, b in zip(names, bufs)}

    xs = x[0]
    done = [None] * DEPTH
    full = as_weights(gather_shards(placed(*FWD_GROUPS[0])))
    for g, (lo, hi) in enumerate(FWD_GROUPS):
        last = g + 1 == len(FWD_GROUPS)
        if not last:
            send_sems, recv_sems, in_flight, token = gather_start(placed(*FWD_GROUPS[g + 1], after=full))
        for l in range(lo, hi):
            small = {n: weights[n][l] for n in SMALL if n != "conv_w"}
            if l == lo and not last:
                small["norm_mix_g"] = small["norm_mix_g"] + token[0, 0]
            xs, w, s = layer_fwd(xs, full, l - lo, small)
            done[l] = (w, s, l - lo)
        if not last:
            full = as_weights(gather_forward(gather_wait(send_sems, recv_sems, in_flight, xs)))
    loss, dx = loss_head(xs, loss_target[0])

    small_grads = [None] * DEPTH
    totals = [lax.empty((DEPTH,) + _as4(weights[n][None]).shape[2:], F32) for n in SPLIT]

    def add_chips(totals, recv, sums, lo):
        return [chip_sum_into(core, chip, r, s, t, lo, "chip_sum_" + n) for n, r, s, t in zip(SPLIT, recv, sums, totals)]

    pending = None
    for g in reversed(range(len(BWD_GROUPS))):
        lo, hi = BWD_GROUPS[g]
        big = {n: lax.empty((hi - lo, N_CHIPS) + weights[n].shape[1:], F32) for n in SPLIT}
        for l in reversed(range(lo, hi)):
            w, s, wi = done[l]
            if pending is not None and l == hi - 1:
                w = dict(w, g2=w["g2"] + pending[-1][0:1, 0:1])
            dx, big, small_grads[l] = layer_bwd(dx, w, s, wi, l - lo, big)
        if pending is not None:
            p_lo, send_sems, recv_sems, p_sums, p_lands, _ = pending
            p_sums, p_recv = chip_exchange_wait(send_sems, recv_sems, p_sums, p_lands, dx)
            totals = add_chips(totals, p_recv, p_sums, p_lo)
        grads = [_as4(big[n]) for n in SPLIT]
        sums = [pair_sum(core, a, t, "pair_sum_" + n) for n, a, t in zip(SPLIT, grads, pair_exchange(grads))]
        if g > 0:
            pending = (lo, *chip_exchange_start(sums))
        else:
            totals = add_chips(totals, chip_exchange(sums), sums, lo)
    reduced = {n: t.reshape(weights[n].shape) for n, t in zip(SPLIT, sibling_share(totals))}

    small_grads = {n: jnp.stack([g[n] for g in small_grads]) for n in SMALL}
    small_total, loss_sum = unpack_small(small_sum(small_allgather(pack_small(small_grads, loss))))
    chip = chip[0]
    cols = D_CONV // N_CHIPS
    small_total["conv_w"] = lax.dynamic_slice_in_dim(small_total["conv_w"].reshape(DEPTH, CONV_K, D_CONV), chip * cols, cols, axis=2)
    for n in SMALL:
        reduced[n] = small_total[n].reshape(weights[n].shape)

    deltas, new_m, new_v = {}, {}, {}
    for n in WEIGHTS:
        w = weights[n]
        flat = (-1, w.shape[-1])
        d, nm, nv = adamw(w.reshape(flat), reduced[n].reshape(flat), given["m_" + n].reshape(flat),
                          given["v_" + n].reshape(flat), "adamw_" + n)
        deltas[n], new_m[n], new_v[n] = d.reshape(w.shape), nm.reshape(w.shape), nv.reshape(w.shape)
    return (loss_sum, dx[None], *[reduced[n] for n in WEIGHTS], *[deltas[n] for n in WEIGHTS],
            *[new_m[n] for n in WEIGHTS], *[new_v[n] for n in WEIGHTS])
```

```python
import functools

import numpy as np
import jax
import jax.numpy as jnp
from jax import lax
from jax.experimental import pallas as pl
from jax.experimental.pallas import tpu as pltpu

F32 = jnp.float32
BF16 = jnp.bfloat16

D_MODEL = 1024
DEPTH = 4
HEAD_DIM = 64
N_HEADS = 8
D_ATTN = 512
D_CONV = 256
D_POOL = 256
D_FF = 2816
D_IN = 5640
CONV_K = 3
POOL_WINDOWS = (2, 4, 8, 16)
N_GROUPS = len(POOL_WINDOWS)
EPS = 1e-6
ADAM_LR, ADAM_B1, ADAM_B2, ADAM_EPS, ADAM_WD, ADAM_STEP = 0.001, 0.9, 0.999, 1e-08, 0.01, 10

D_QKV = 3 * D_ATTN
D_F = 128
D_B = 3 * D_CONV + D_POOL + 3 * D_MODEL
D_LOCAL = 3 * D_CONV + D_POOL

LANES = 128
D_HEADS = N_HEADS * LANES
HALO = 16
VMEM_LIMIT = 56 * 1024 * 1024
NEG = -1e30
LOG2E = 1.4426950408889634
LN2 = 0.6931471805599453

TM = 256
TM_MIX = 256
TQ = 512
TK = 1024
KB = 512
QT = 1024

LANE_C = 64
LANE_ONE = 67
LANE_LSE = 70
N_PIECES = 3


def _dot(a, b):
    return jnp.dot(a, b, preferred_element_type=F32)


def _dot_nt(a, b):
    return lax.dot_general(a, b, (((1,), (1,)), ((), ())), preferred_element_type=F32)


def _dot_tn(a, b):
    return lax.dot_general(a, b, (((0,), (0,)), ((), ())), preferred_element_type=F32)


def _params(*sem):
    return pltpu.CompilerParams(dimension_semantics=sem, vmem_limit_bytes=VMEM_LIMIT)


def _rows(tm, n):
    return pl.BlockSpec((tm, n), lambda i, *_: (i, 0))


def _whole(a):
    nd = a.ndim
    return pl.BlockSpec(a.shape, lambda *_: (0,) * nd)


def _layer(shape):
    nd = len(shape)
    return pl.BlockSpec((None,) + tuple(shape), lambda *a: (a[-1][0],) + (0,) * nd)


def _layer_index(l):
    return jnp.full((1,), l, jnp.int32)


def _split_bf16(x):
    hi = x.astype(BF16)
    lo = (x - hi.astype(F32)).astype(BF16)
    return hi, lo


def _pieces(x):
    p1 = x.astype(BF16)
    r1 = x - p1.astype(F32)
    p2 = r1.astype(BF16)
    p3 = (r1 - p2.astype(F32)).astype(BF16)
    return p1, p2, p3


def _sigmoid(x):
    return 1.0 / (1.0 + jnp.exp(-x))


def w_in_prep(win, l):
    tr = 256
    n = D_IN // 4
    v_rest = D_QKV - n
    b0 = v_rest + N_HEADS

    def body(s0, s1, s2, s3, wa_ref, wf_ref, wb_ref):
        b = s1[...]
        wa_ref[...] = jnp.concatenate([s0[...], b[:, 0:v_rest]], axis=1)
        wf_ref[...] = jnp.concatenate([b[:, v_rest:b0], jnp.zeros((tr, D_F - N_HEADS), b.dtype)], axis=1)
        wb_ref[...] = jnp.concatenate([b[:, b0:n], s2[...], s3[...]], axis=1)

    shard = lambda j: pl.BlockSpec((None, None, tr, n), lambda i: (l, j, i, 0))
    return pl.pallas_call(
        body,
        grid=(D_MODEL // tr,),
        in_specs=[shard(0), shard(1), shard(2), shard(3)],
        out_specs=[_rows(tr, D_QKV), _rows(tr, D_F), _rows(tr, D_B)],
        out_shape=[jax.ShapeDtypeStruct((D_MODEL, D_QKV), win.dtype), jax.ShapeDtypeStruct((D_MODEL, D_F), win.dtype),
                   jax.ShapeDtypeStruct((D_MODEL, D_B), win.dtype)],
        compiler_params=_params("parallel"),
        name="w_in_prep",
    )(win, win, win, win)


def _into_layer(buf):
    return dict(in_spec=HBM_REF, out_shape=jax.ShapeDtypeStruct(buf.shape, buf.dtype), aliases={1: 0})


def w_in_unprep(dwa, dwf, dwb, buf, l):
    tr = 256
    n = D_IN // 4
    v_rest = D_QKV - n
    b1 = n - v_rest - N_HEADS
    place = _into_layer(buf)

    def body(l_ref, buf_ref, a_ref, f_ref, b_ref, o_ref):
        a = a_ref[...]
        b = b_ref[...]
        o_ref[0] = a[:, 0:n]
        o_ref[1] = jnp.concatenate([a[:, n:D_QKV], f_ref[:, 0:N_HEADS], b[:, 0:b1]], axis=1)
        o_ref[2] = b[:, b1:b1 + n]
        o_ref[3] = b[:, b1 + n:D_B]

    return pl.pallas_call(
        body,
        grid_spec=pltpu.PrefetchScalarGridSpec(
            num_scalar_prefetch=1,
            grid=(D_MODEL // tr,),
            in_specs=[place["in_spec"], _rows(tr, D_QKV), _rows(tr, D_F), _rows(tr, D_B)],
            out_specs=pl.BlockSpec((None, 4, tr, n), lambda i, l: (l[0], 0, i, 0)),
        ),
        out_shape=place["out_shape"],
        input_output_aliases=place["aliases"],
        compiler_params=_params("parallel"),
        name="w_in_unprep",
    )(_layer_index(l), buf, dwa, dwf, dwb)


def branch_w_prep(wao, wco, pw, cw, l):
    gd = D_POOL // N_GROUPS
    od = D_MODEL // N_GROUPS

    def body(wao_ref, wco_ref, pw_ref, cw_ref, ao_ref, co_ref, po_ref, co8_ref):
        a = jnp.concatenate([wao_ref[j] for j in range(4)], axis=1)
        gap = jnp.zeros((LANES - HEAD_DIM, D_MODEL), a.dtype)
        ao_ref[...] = jnp.concatenate(
            [blk for h in range(N_HEADS) for blk in (a[h * HEAD_DIM:(h + 1) * HEAD_DIM], gap)], axis=0)
        co_ref[...] = jnp.concatenate([wco_ref[j] for j in range(4)], axis=1)
        zero = jnp.zeros((gd, od), a.dtype)
        po_ref[...] = jnp.concatenate(
            [jnp.concatenate([jnp.concatenate([pw_ref[j, g] for j in range(4)], axis=1) if g2 == g else zero
                              for g2 in range(N_GROUPS)], axis=1) for g in range(N_GROUPS)], axis=0)
        co8_ref[...] = jnp.zeros_like(co8_ref)
        co8_ref[0:CONV_K, :] = jnp.concatenate([cw_ref[j] for j in range(4)], axis=1)

    sel = lambda *shape: pl.BlockSpec((None,) + shape, lambda i: (l,) + (0,) * len(shape))
    return pl.pallas_call(
        body,
        grid=(1,),
        in_specs=[sel(4, D_ATTN, D_MODEL // 4), sel(4, D_CONV, D_MODEL // 4), sel(4, N_GROUPS, gd, od // 4),
                  sel(4, CONV_K, D_CONV // 4)],
        out_specs=[pl.BlockSpec((D_HEADS, D_MODEL), lambda i: (0, 0)), pl.BlockSpec((D_CONV, D_MODEL), lambda i: (0, 0)),
                   pl.BlockSpec((D_POOL, D_MODEL), lambda i: (0, 0)), pl.BlockSpec((8, D_CONV), lambda i: (0, 0))],
        out_shape=[jax.ShapeDtypeStruct((D_HEADS, D_MODEL), BF16), jax.ShapeDtypeStruct((D_CONV, D_MODEL), BF16),
                   jax.ShapeDtypeStruct((D_POOL, D_MODEL), BF16), jax.ShapeDtypeStruct((8, D_CONV), F32)],
        name="branch_w_prep",
    )(wao, wco, pw, cw)


def branch_g_place(dwao, dwco, dwpool, dwo, bufs, l):
    gd = D_POOL // N_GROUPS
    od = D_MODEL // N_GROUPS
    q = D_MODEL // 4

    def body(l_ref, b0, b1, b2, b3, a_ref, c_ref, p_ref, w_ref, ao_ref, co_ref, po_ref, wo_ref):
        a = jnp.concatenate([a_ref[h * LANES:h * LANES + HEAD_DIM, :] for h in range(N_HEADS)], axis=0)
        c = c_ref[...]
        p = p_ref[...]
        for j in range(4):
            ao_ref[j] = a[:, j * q:(j + 1) * q]
            co_ref[j] = c[:, j * q:(j + 1) * q]
            wo_ref[j] = w_ref[j * q:(j + 1) * q, :]
            for g in range(N_GROUPS):
                c0 = g * od + j * (od // 4)
                po_ref[j, g] = p[g * gd:(g + 1) * gd, c0:c0 + od // 4]

    whole = lambda a: pl.BlockSpec(a.shape, lambda i, l: (0,) * a.ndim)
    layer = lambda b: pl.BlockSpec((None,) + b.shape[1:], lambda i, l: (l[0],) + (0,) * (b.ndim - 1))
    return pl.pallas_call(
        body,
        grid_spec=pltpu.PrefetchScalarGridSpec(
            num_scalar_prefetch=1,
            grid=(1,),
            in_specs=[HBM_REF] * 4 + [whole(dwao), whole(dwco), whole(dwpool), whole(dwo)],
            out_specs=[layer(b) for b in bufs],
        ),
        out_shape=[jax.ShapeDtypeStruct(b.shape, b.dtype) for b in bufs],
        input_output_aliases={1: 0, 2: 1, 3: 2, 4: 3},
        compiler_params=_params("arbitrary"),
        name="branch_g_place",
    )(_layer_index(l), *bufs, dwao, dwco, dwpool, dwo)


def in_proj_fwd(x, g, wa, wf, wb):
    S = x.shape[0]

    def body(x_ref, g_ref, wa_ref, wf_ref, wb_ref, h_ref, pa_ref, pf_ref, pb_ref):
        xf = x_ref[...]
        r = lax.rsqrt(jnp.mean(xf * xf, axis=-1, keepdims=True) + EPS)
        h = (xf * r * g_ref[...]).astype(BF16)
        h_ref[...] = h
        pa_ref[...] = _dot(h, wa_ref[...]).astype(BF16)
        pf_ref[...] = _dot(h, wf_ref[...])
        pb_ref[...] = _dot(h, wb_ref[...]).astype(BF16)

    return pl.pallas_call(
        body,
        grid=(S // TM,),
        in_specs=[_rows(TM, D_MODEL), _whole(g), _whole(wa), _whole(wf), _whole(wb)],
        out_specs=[_rows(TM, D_MODEL), _rows(TM, D_QKV), _rows(TM, D_F), _rows(TM, D_B)],
        out_shape=[
            jax.ShapeDtypeStruct((S, D_MODEL), BF16),
            jax.ShapeDtypeStruct((S, D_QKV), BF16),
            jax.ShapeDtypeStruct((S, D_F), F32),
            jax.ShapeDtypeStruct((S, D_B), BF16),
        ],
        compiler_params=_params("parallel"),
        name="in_proj_fwd",
    )(x, g, wa, wf, wb)


def in_proj_bwd(x, g, dxm, dpa, dpf, dpb, wa, wf, wb):
    S = x.shape[0]

    def body(x_ref, g_ref, dxm_ref, dpa_ref, dpf_ref, dpb_ref, wa_ref, wf_ref, wb_ref, dx_ref, dg_ref):
        @pl.when(pl.program_id(0) == 0)
        def _():
            dg_ref[...] = jnp.zeros_like(dg_ref)

        dh = _dot_nt(dpa_ref[...], wa_ref[...]) + _dot_nt(dpf_ref[...], wf_ref[...]) + _dot_nt(dpb_ref[...], wb_ref[...])
        xf = x_ref[...]
        r = lax.rsqrt(jnp.mean(xf * xf, axis=-1, keepdims=True) + EPS)
        xhat = xf * r
        dg_ref[...] += jnp.sum(dh * xhat, axis=0, keepdims=True)
        gdh = dh * g_ref[...]
        dx_ref[...] = dxm_ref[...] + r * (gdh - xhat * jnp.mean(xhat * gdh, axis=-1, keepdims=True))

    return pl.pallas_call(
        body,
        grid=(S // TM,),
        in_specs=[_rows(TM, D_MODEL), _whole(g), _rows(TM, D_MODEL), _rows(TM, D_QKV), _rows(TM, D_F), _rows(TM, D_B),
                  _whole(wa), _whole(wf), _whole(wb)],
        out_specs=[_rows(TM, D_MODEL), pl.BlockSpec((1, D_MODEL), lambda i: (0, 0))],
        out_shape=[jax.ShapeDtypeStruct((S, D_MODEL), F32), jax.ShapeDtypeStruct((1, D_MODEL), F32)],
        compiler_params=_params("arbitrary"),
        name="in_proj_bwd",
    )(x, g, dxm, dpa, dpf, dpb, wa, wf, wb)


TS_WGRAD = 1024


def _wgrad_columns(n):
    return next(tn for tn in (1024, 768, 512, 256, 128) if n % tn == 0)


def wgrad(xa, dy, name):
    S, K = xa.shape
    N = dy.shape[1]
    ts = min(TS_WGRAD, S)
    tn = _wgrad_columns(N)

    def body(x_ref, dy_ref, o_ref):
        @pl.when(pl.program_id(1) == 0)
        def _():
            o_ref[...] = jnp.zeros_like(o_ref)

        o_ref[...] += _dot_tn(x_ref[...], dy_ref[...])

    return pl.pallas_call(
        body,
        grid=(N // tn, S // ts),
        in_specs=[pl.BlockSpec((ts, K), lambda j, k: (k, 0)), pl.BlockSpec((ts, tn), lambda j, k: (k, j))],
        out_specs=pl.BlockSpec((K, tn), lambda j, k: (0, j)),
        out_shape=jax.ShapeDtypeStruct((K, N), F32),
        compiler_params=_params("parallel", "arbitrary"),
        name=name,
    )(xa, dy)


def wgrad_into(xa, dy, name, buf, l):
    S, K = xa.shape
    N = dy.shape[1]
    ts = min(TS_WGRAD, S)
    split = buf.ndim == 4
    tn = buf.shape[-1] if split else _wgrad_columns(N)
    place = _into_layer(buf)

    def body(l_ref, buf_ref, x_ref, dy_ref, o_ref):
        @pl.when(pl.program_id(1) == 0)
        def _():
            o_ref[...] = jnp.zeros_like(o_ref)

        o_ref[...] += _dot_tn(x_ref[...], dy_ref[...])

    if split:
        out_spec = pl.BlockSpec((None, None, K, tn), lambda j, k, l: (l[0], j, 0, 0))
    else:
        out_spec = pl.BlockSpec((None, K, tn), lambda j, k, l: (l[0], 0, j))
    return pl.pallas_call(
        body,
        grid_spec=pltpu.PrefetchScalarGridSpec(
            num_scalar_prefetch=1,
            grid=(N // tn, S // ts),
            in_specs=[place["in_spec"], pl.BlockSpec((ts, K), lambda j, k, l: (k, 0)),
                      pl.BlockSpec((ts, tn), lambda j, k, l: (k, j))],
            out_specs=out_spec,
        ),
        out_shape=place["out_shape"],
        input_output_aliases=place["aliases"],
        compiler_params=_params("parallel", "arbitrary"),
        name=name,
    )(_layer_index(l), buf, xa, dy)


def _head_mean_matrix():
    h = np.arange(D_ATTN) // HEAD_DIM
    return jnp.asarray((h[:, None] == h[None, :]).astype(np.float32) / HEAD_DIM, BF16)


def _place_matrix(lane0):
    m = np.zeros((N_PIECES * LANES, D_HEADS), np.float32)
    for i in range(N_PIECES):
        for h in range(N_HEADS):
            m[i * LANES + h, h * LANES + lane0 + i] = 1.0
    return jnp.asarray(m, BF16)


def _tri(n, upper):
    r = np.arange(n)
    m = (r[None, :] >= r[:, None]) if upper else (r[None, :] <= r[:, None])
    return jnp.asarray(m.astype(np.float32), BF16)


def _lanes_in(lane, lo, n):
    return (lane >= lo) & (lane < lo + n)


def qk_prep(pa, pf, gq, gk, fb):
    S = pa.shape[0]
    bd = _head_mean_matrix()
    tri = _tri(TM, upper=False)
    place_q = _place_matrix(LANE_C)
    place_k = _place_matrix(LANE_ONE)

    def body(q_ref, k_ref, v_ref, pf_ref, gq_ref, gk_ref, fb_ref, bd_ref, tri_ref, pq_ref, pk_ref,
             qx_ref, kx_ref, vx_ref, carry):
        @pl.when(pl.program_id(0) == 0)
        def _():
            carry[...] = jnp.zeros_like(carry)

        def head_norm(x_ref, g_ref, scale):
            xf = x_ref[...].astype(F32)
            ms = _dot((xf * xf).astype(BF16), bd_ref[...])
            return xf * lax.rsqrt(ms + EPS) * g_ref[...] * scale

        qh = head_norm(q_ref, gq_ref, HEAD_DIM ** -0.5 * LOG2E)
        kh = head_norm(k_ref, gk_ref, 1.0)
        vf = v_ref[...].astype(F32)

        z = pf_ref[...] + fb_ref[...]
        logf = jnp.minimum(z, 0.0) - jnp.log(1.0 + jnp.exp(-jnp.abs(z)))
        hi, lo = _split_bf16(logf)
        c = _dot(tri_ref[...], hi) + _dot(tri_ref[...], lo) + carry[...]
        carry[...] += jnp.sum(hi.astype(F32) + lo.astype(F32), axis=0, keepdims=True)
        pieces = jnp.concatenate(_pieces(c * LOG2E), axis=1)
        cq = _dot(pieces, pq_ref[...])
        ck = _dot(pieces, pk_ref[...])

        lane = lax.broadcasted_iota(jnp.int32, (TM, LANES), 1)
        low = lane < HEAD_DIM
        ones_q = _lanes_in(lane, LANE_ONE, N_PIECES).astype(F32)
        ones_k = (_lanes_in(lane, LANE_C, N_PIECES) | _lanes_in(lane, LANE_LSE, N_PIECES)).astype(F32)
        ones_v = _lanes_in(lane, LANE_C, N_PIECES + 1).astype(F32)
        for h in range(N_HEADS):
            blk = slice((h // 2) * LANES, (h // 2 + 1) * LANES)
            head = (lambda a: pltpu.roll(a[:, blk], HEAD_DIM, 1)) if h % 2 else (lambda a: a[:, blk])
            mine = slice(h * LANES, (h + 1) * LANES)
            qx_ref[h] = jnp.where(low, head(qh), cq[:, mine] + ones_q).astype(BF16)
            kx_ref[h] = jnp.where(low, head(kh), ones_k - ck[:, mine]).astype(BF16)
            vx_ref[h] = jnp.where(low, head(vf), ones_v).astype(BF16)

    heads = pl.BlockSpec((N_HEADS, TM, LANES), lambda i: (0, i, 0))
    out = jax.ShapeDtypeStruct((N_HEADS, S, LANES), BF16)
    return pl.pallas_call(
        body,
        grid=(S // TM,),
        in_specs=[pl.BlockSpec((TM, D_ATTN), lambda i: (i, 0)), pl.BlockSpec((TM, D_ATTN), lambda i: (i, 1)),
                  pl.BlockSpec((TM, D_ATTN), lambda i: (i, 2)),
                  _rows(TM, D_F), _whole(gq), _whole(gk), _whole(fb), _whole(bd), _whole(tri), _whole(place_q), _whole(place_k)],
        out_specs=[heads, heads, heads],
        out_shape=[out, out, out],
        scratch_shapes=[pltpu.VMEM((1, D_F), F32)],
        compiler_params=_params("arbitrary"),
        name="qk_prep",
    )(pa, pa, pa, pf, gq, gk, fb, bd, tri, place_q, place_k)


def attn_fwd(qx, kx, vx):
    S = qx.shape[1]
    nq = S // TQ

    def body(q_ref, k_ref, v_ref, o_ref, qb_ref):
        i = pl.program_id(1)
        lane = lax.broadcasted_iota(jnp.int32, (TQ, LANES), 1)
        row = lax.broadcasted_iota(jnp.int32, (TQ, TK), 0)
        col = lax.broadcasted_iota(jnp.int32, (TQ, TK), 1)
        q = [q_ref[0], q_ref[1]]
        n_full = (i * TQ) // TK

        def step(kt, carry, masked):
            ks = pl.multiple_of(kt * TK, TK)
            new = []
            for j in range(2):
                m, acc = carry[j]
                z = _dot_nt(q[j], k_ref[j, pl.ds(ks, TK), :])
                if masked:
                    z = jnp.where(row + i * TQ >= col + ks, z, NEG)
                m_new = jnp.maximum(m, jnp.max(z, axis=1, keepdims=True))
                pr = jnp.exp2(z - m_new)
                acc = jnp.exp2(m - m_new) * acc + _dot(pr.astype(BF16), v_ref[j, pl.ds(ks, TK), :])
                new.append((m_new, acc))
            return tuple(new)

        init = tuple((jnp.full((TQ, 1), NEG, F32), jnp.zeros((TQ, LANES), F32)) for _ in range(2))
        carry = lax.fori_loop(0, n_full, functools.partial(step, masked=False), init)
        carry = step(n_full, carry, True)
        for j in range(2):
            m, acc = carry[j]
            l = jnp.sum(jnp.where(lane == LANE_C, acc, 0.0), axis=1, keepdims=True)
            o_ref[j] = acc / l
            n1, n2, n3 = _pieces(-(m + jnp.log(l) * LOG2E))
            qb_ref[j] = jnp.where(lane == LANE_LSE, n1,
                                  jnp.where(lane == LANE_LSE + 1, n2, jnp.where(lane == LANE_LSE + 2, n3, q[j])))

    pair_tile = pl.BlockSpec((2, TQ, LANES), lambda p, i: (p, i, 0))
    pair_all = pl.BlockSpec((2, S, LANES), lambda p, i: (p, 0, 0))
    return pl.pallas_call(
        body,
        grid=(N_HEADS // 2, nq),
        in_specs=[pair_tile, pair_all, pair_all],
        out_specs=[pair_tile, pair_tile],
        out_shape=[jax.ShapeDtypeStruct((N_HEADS, S, LANES), F32), jax.ShapeDtypeStruct((N_HEADS, S, LANES), BF16)],
        compiler_params=_params("parallel", "parallel"),
        name="attn_fwd",
    )(qx, kx, vx)


def attn_bwd(qxb, kx, vx, dox):
    S = qxb.shape[1]
    nq = S // QT

    def body(q_ref, k_ref, v_ref, do_ref, dq_ref, dk_ref, dv_ref):
        kt = pl.program_id(1)

        @pl.when(kt == 0)
        def _():
            dq_ref[...] = jnp.zeros_like(dq_ref)

        row = lax.broadcasted_iota(jnp.int32, (QT, KB), 0)
        col = lax.broadcasted_iota(jnp.int32, (QT, KB), 1)
        k = [k_ref[0], k_ref[1]]
        v = [v_ref[0], v_ref[1]]
        q_first = (kt * KB) // QT

        def step(qi, carry, masked):
            qs = pl.multiple_of(qi * QT, QT)
            new = []
            for j in range(2):
                dk, dv = carry[j]
                q = q_ref[j, pl.ds(qs, QT), :]
                dout = do_ref[j, pl.ds(qs, QT), :]
                z = _dot_nt(q, k[j])
                if masked:
                    z = jnp.where(row + qs >= col + kt * KB, z, NEG)
                pr = jnp.exp2(z)
                dv = dv + _dot_tn(pr.astype(BF16), dout)
                dsb = (pr * _dot_nt(dout, v[j])).astype(BF16)
                dk = dk + _dot_tn(dsb, q)
                dq_ref[j, pl.ds(qs, QT), :] += _dot(dsb, k[j])
                new.append((dk, dv))
            return tuple(new)

        zero = jnp.zeros((KB, LANES), F32)
        carry = step(q_first, ((zero, zero), (zero, zero)), True)
        carry = lax.fori_loop(q_first + 1, nq, functools.partial(step, masked=False), carry)
        for j in range(2):
            dk_ref[j] = carry[j][0]
            dv_ref[j] = carry[j][1]

    pair_tile = pl.BlockSpec((2, KB, LANES), lambda p, kt: (p, kt, 0))
    pair_all = pl.BlockSpec((2, S, LANES), lambda p, kt: (p, 0, 0))
    out = jax.ShapeDtypeStruct((N_HEADS, S, LANES), F32)
    return pl.pallas_call(
        body,
        grid=(N_HEADS // 2, S // KB),
        in_specs=[pair_all, pair_tile, pair_tile, pair_all],
        out_specs=[pair_all, pair_tile, pair_tile],
        out_shape=[out, out, out],
        compiler_params=_params("arbitrary", "arbitrary"),
        name="attn_bwd",
    )(qxb, kx, vx, dox)


def attn_bwd_post(pa, pf, dqx, dkx, dvx, gq, gk, fb):
    S = pa.shape[0]
    nt = S // TM
    bd = _head_mean_matrix()
    triu = _tri(TM, upper=True)
    rev = lambda i: nt - 1 - i

    def body(q_ref, k_ref, pf_ref, dqx_ref, dkx_ref, dvx_ref, gq_ref, gk_ref, fb_ref, bd_ref, triu_ref,
             dpa_ref, dpf_ref, dgq_ref, dgk_ref, dfb_ref, carry):
        @pl.when(pl.program_id(0) == 0)
        def _():
            carry[...] = jnp.zeros_like(carry)
            dgq_ref[...] = jnp.zeros_like(dgq_ref)
            dgk_ref[...] = jnp.zeros_like(dgk_ref)
            dfb_ref[...] = jnp.zeros_like(dfb_ref)

        lane = lax.broadcasted_iota(jnp.int32, (TM, LANES), 1)

        def columns(ref):
            return jnp.concatenate([jnp.where(lane < HEAD_DIM, ref[2 * p], pltpu.roll(ref[2 * p + 1], HEAD_DIM, 1))
                                    for p in range(N_HEADS // 2)], axis=1)

        def head_norm_bwd(x_ref, dy, g_ref, dg_ref):
            xf = x_ref[...].astype(F32)
            r = lax.rsqrt(_dot((xf * xf).astype(BF16), bd_ref[...]) + EPS)
            xhat = xf * r
            dg_ref[...] += jnp.sum(dy * xhat, axis=0, keepdims=True)
            gdy = dy * g_ref[...]
            return (r * (gdy - xhat * _dot((xhat * gdy).astype(BF16), bd_ref[...]))).astype(BF16)

        dpa_ref[:, 0:D_ATTN] = head_norm_bwd(q_ref, columns(dqx_ref) * HEAD_DIM ** -0.5, gq_ref, dgq_ref)
        dpa_ref[:, D_ATTN:2 * D_ATTN] = head_norm_bwd(k_ref, columns(dkx_ref) * LN2, gk_ref, dgk_ref)
        dpa_ref[:, 2 * D_ATTN:3 * D_ATTN] = columns(dvx_ref).astype(BF16)

        dc = jnp.zeros((TM, LANES), F32)
        for h in range(N_HEADS):
            both = jnp.where(lane == LANE_C, dqx_ref[h], 0.0) - jnp.where(lane == LANE_ONE, dkx_ref[h], 0.0)
            dc = jnp.where(lane == h, jnp.sum(both, axis=1, keepdims=True), dc)
        hi, lo = _split_bf16(dc)
        dlogf = _dot(triu_ref[...], hi) + _dot(triu_ref[...], lo) + carry[...]
        first = lax.broadcasted_iota(jnp.int32, (TM, D_F), 0) == 0
        carry[...] = jnp.sum(jnp.where(first, dlogf, 0.0), axis=0, keepdims=True)
        df = dlogf * _sigmoid(-(pf_ref[...] + fb_ref[...]))
        dfb_ref[...] += jnp.sum(df, axis=0, keepdims=True)
        dpf_ref[...] = df.astype(BF16)

    heads = pl.BlockSpec((N_HEADS, TM, LANES), lambda i: (0, rev(i), 0))
    return pl.pallas_call(
        body,
        grid=(nt,),
        in_specs=[pl.BlockSpec((TM, D_ATTN), lambda i: (rev(i), 0)), pl.BlockSpec((TM, D_ATTN), lambda i: (rev(i), 1)),
                  pl.BlockSpec((TM, D_F), lambda i: (rev(i), 0)), heads, heads, heads,
                  _whole(gq), _whole(gk), _whole(fb), _whole(bd), _whole(triu)],
        out_specs=[pl.BlockSpec((TM, D_QKV), lambda i: (rev(i), 0)), pl.BlockSpec((TM, D_F), lambda i: (rev(i), 0)),
                   pl.BlockSpec((1, D_ATTN), lambda i: (0, 0)), pl.BlockSpec((1, D_ATTN), lambda i: (0, 0)),
                   pl.BlockSpec((1, D_F), lambda i: (0, 0))],
        out_shape=[jax.ShapeDtypeStruct((S, D_QKV), BF16), jax.ShapeDtypeStruct((S, D_F), BF16),
                   jax.ShapeDtypeStruct((1, D_ATTN), F32), jax.ShapeDtypeStruct((1, D_ATTN), F32),
                   jax.ShapeDtypeStruct((1, D_F), F32)],
        scratch_shapes=[pltpu.VMEM((1, D_F), F32)],
        compiler_params=_params("arbitrary"),
        name="attn_bwd_post",
    )(pa, pa, pf, dqx, dkx, dvx, gq, gk, fb, bd, triu)


def _shift_down(ext, k):
    return pltpu.roll(ext, k, 0)[HALO:]


def _shift_up(ext, k, n):
    return pltpu.roll(ext, n + HALO - k, 0)[:n]


def _pool_lane_select(a2, a4, a8, a16, lane):
    return jnp.where(lane < 64, a2, jnp.where(lane < 128, a4, jnp.where(lane < 192, a8, a16)))


def _local_branches(o, pb, halo, have_prev, row0, wao, convw, wco, wpool, pscale):
    n = pb.shape[0]
    cx = pb[:, 0:D_CONV].astype(F32)
    cb = pb[:, D_CONV:2 * D_CONV].astype(F32)
    cc = pb[:, 2 * D_CONV:3 * D_CONV].astype(F32)
    px = pb[:, 3 * D_CONV:D_LOCAL].astype(F32)
    keep = have_prev.astype(F32)
    z = cc * cx
    z_ext = jnp.concatenate([halo[:, 2 * D_CONV:3 * D_CONV].astype(F32) * halo[:, 0:D_CONV].astype(F32) * keep, z], axis=0)
    z1 = _shift_down(z_ext, 1)
    z2 = _shift_down(z_ext, 2)
    conv = convw[0:1, :] * z2 + convw[1:2, :] * z1 + convw[2:3, :] * z
    cm = cb * conv

    u_ext = jnp.concatenate([halo[:, 3 * D_CONV:D_LOCAL].astype(F32) * keep, px], axis=0)
    s2 = u_ext + pltpu.roll(u_ext, 1, 0)
    s4 = s2 + pltpu.roll(s2, 2, 0)
    s8 = s4 + pltpu.roll(s4, 4, 0)
    s16 = s8 + pltpu.roll(s8, 8, 0)
    lane = lax.broadcasted_iota(jnp.int32, (n, D_POOL), 1)
    win = _pool_lane_select(2.0, 4.0, 8.0, 16.0, lane)
    t = (row0 + lax.broadcasted_iota(jnp.int32, (n, D_POOL), 0)).astype(F32)
    cnt = jnp.minimum(t + 1.0, win)
    feat = _pool_lane_select(s2[HALO:], s4[HALO:], s8[HALO:], s16[HALO:], lane) / cnt - px

    ya = _dot(o, wao)
    yc = _dot(cm.astype(BF16), wco)
    yp_pre = _dot(feat.astype(BF16), wpool)
    yp = yp_pre * pscale
    return dict(cx=cx, cb=cb, cc=cc, z=z, z1=z1, z2=z2, conv=conv, cm=cm, feat=feat, cnt=cnt, lane=lane,
                ya=ya, yc=yc, yp_pre=yp_pre, yp=yp)


def _halo_spec(tm, tile_of):
    per = tm // HALO
    return pl.BlockSpec((HALO, D_LOCAL), lambda i, *_: (jnp.maximum(tile_of(i) * per - 1, 0), 0))


def _heads_as_columns(ref):
    return jnp.concatenate([ref[h] for h in range(N_HEADS)], axis=1)


def mix_out_fwd(x, ox, pb, wao, convw, wco, wpool, pscale, wo_all, l):
    S = x.shape[0]
    tm = TM_MIX

    def body(l_ref, x_ref, o_ref, pb_ref, halo_ref, wao_ref, cw_ref, wco_ref, wp_ref, ps_ref, wo_ref, y_ref):
        i = pl.program_id(0)
        pb = pb_ref[...]
        o = _heads_as_columns(o_ref).astype(BF16)
        b = _local_branches(o, pb, halo_ref[...], i > 0, i * tm, wao_ref[...], cw_ref[...], wco_ref[...],
                            wp_ref[...], ps_ref[...])
        g0 = _sigmoid(pb[:, D_LOCAL:D_LOCAL + D_MODEL].astype(F32))
        g1 = _sigmoid(pb[:, D_LOCAL + D_MODEL:D_LOCAL + 2 * D_MODEL].astype(F32))
        g2 = _sigmoid(pb[:, D_LOCAL + 2 * D_MODEL:D_B].astype(F32))
        merged = g0 * b["ya"] + g1 * b["yc"] + g2 * b["yp"]
        y_ref[...] = x_ref[...] + _dot(merged.astype(BF16), wo_ref[...])

    return pl.pallas_call(
        body,
        grid_spec=pltpu.PrefetchScalarGridSpec(
            num_scalar_prefetch=1,
            grid=(S // tm,),
            in_specs=[_rows(tm, D_MODEL), pl.BlockSpec((N_HEADS, tm, LANES), lambda i, l: (0, i, 0)), _rows(tm, D_B),
                      _halo_spec(tm, lambda i: i), _whole(wao), _whole(convw), _whole(wco), _whole(wpool), _whole(pscale),
                      _layer((D_MODEL, D_MODEL))],
            out_specs=_rows(tm, D_MODEL),
        ),
        out_shape=jax.ShapeDtypeStruct((S, D_MODEL), F32),
        compiler_params=_params("parallel"),
        name="mix_out_fwd",
    )(_layer_index(l), x, ox, pb, pb, wao, convw, wco, wpool, pscale, wo_all)


def mix_out_bwd(dxm, ox, pb, wao, convw, wco, wpool, pscale, wo_all, l):
    S = dxm.shape[0]
    tm = TM_MIX
    nt = S // tm
    rev = lambda i: nt - 1 - i
    rows = lambda n: pl.BlockSpec((tm, n), lambda i, l: (rev(i), 0))
    heads = pl.BlockSpec((N_HEADS, tm, LANES), lambda i, l: (0, rev(i), 0))
    acc = lambda r, c: pl.BlockSpec((r, c), lambda i, l: (0, 0))

    def body(l_ref, dxm_ref, o_ref, pb_ref, halo_ref, wao_ref, cw_ref, wco_ref, wp_ref, ps_ref, wo_ref,
             dpb_ref, do_ref, dwo_ref, dwao_ref, dwco_ref, dwp_ref, dcw_ref, dps_ref, next_dconv, next_e):
        i = pl.program_id(0)
        r = rev(i)

        @pl.when(i == 0)
        def _():
            for ref in (dwo_ref, dwao_ref, dwco_ref, dwp_ref, dcw_ref, dps_ref, next_dconv, next_e):
                ref[...] = jnp.zeros_like(ref)

        pb = pb_ref[...]
        o = _heads_as_columns(o_ref).astype(BF16)
        cw = cw_ref[...]
        b = _local_branches(o, pb, halo_ref[...], r > 0, r * tm, wao_ref[...], cw, wco_ref[...], wp_ref[...], ps_ref[...])
        g0 = _sigmoid(pb[:, D_LOCAL:D_LOCAL + D_MODEL].astype(F32))
        g1 = _sigmoid(pb[:, D_LOCAL + D_MODEL:D_LOCAL + 2 * D_MODEL].astype(F32))
        g2 = _sigmoid(pb[:, D_LOCAL + 2 * D_MODEL:D_B].astype(F32))
        dxb = dxm_ref[...].astype(BF16)
        merged = g0 * b["ya"] + g1 * b["yc"] + g2 * b["yp"]
        dwo_ref[...] += _dot_tn(merged.astype(BF16), dxb)
        dmer = _dot_nt(dxb, wo_ref[...])
        dpb_ref[:, D_LOCAL:D_LOCAL + D_MODEL] = (dmer * b["ya"] * (g0 * (1.0 - g0))).astype(BF16)
        dpb_ref[:, D_LOCAL + D_MODEL:D_LOCAL + 2 * D_MODEL] = (dmer * b["yc"] * (g1 * (1.0 - g1))).astype(BF16)
        dpb_ref[:, D_LOCAL + 2 * D_MODEL:D_B] = (dmer * b["yp"] * (g2 * (1.0 - g2))).astype(BF16)

        dya = (dmer * g0).astype(BF16)
        dwao_ref[...] += _dot_tn(o, dya)
        da = _dot_nt(dya, wao_ref[...]).astype(BF16)
        lane = lax.broadcasted_iota(jnp.int32, (tm, LANES), 1)
        for h in range(N_HEADS):
            dah = da[:, h * LANES:(h + 1) * LANES]
            d1, d2, d3 = _pieces(-jnp.sum(dah.astype(F32) * o_ref[h], axis=1, keepdims=True))
            do_ref[h] = jnp.where(lane == LANE_C + 1, d1, jnp.where(lane == LANE_C + 2, d2,
                                                                  jnp.where(lane == LANE_C + 3, d3, dah)))

        dyc = (dmer * g1).astype(BF16)
        dwco_ref[...] += _dot_tn(b["cm"].astype(BF16), dyc)
        dcm = _dot_nt(dyc, wco_ref[...])
        dconv = dcm * b["cb"]
        dcw_ref[0:1, :] += jnp.sum(dconv * b["z2"], axis=0, keepdims=True)
        dcw_ref[1:2, :] += jnp.sum(dconv * b["z1"], axis=0, keepdims=True)
        dcw_ref[2:3, :] += jnp.sum(dconv * b["z"], axis=0, keepdims=True)
        d_ext = jnp.concatenate([dconv, next_dconv[...]], axis=0)
        dz = cw[2:3, :] * dconv + cw[1:2, :] * _shift_up(d_ext, 1, tm) + cw[0:1, :] * _shift_up(d_ext, 2, tm)
        next_dconv[...] = dconv[0:HALO]
        dpb_ref[:, 0:D_CONV] = (dz * b["cc"]).astype(BF16)
        dpb_ref[:, D_CONV:2 * D_CONV] = (dcm * b["conv"]).astype(BF16)
        dpb_ref[:, 2 * D_CONV:3 * D_CONV] = (dz * b["cx"]).astype(BF16)

        dyp = dmer * g2
        dps_ref[...] += jnp.sum(dyp * b["yp_pre"], axis=0, keepdims=True)
        dyps = (dyp * ps_ref[...]).astype(BF16)
        dwp_ref[...] += _dot_tn(b["feat"].astype(BF16), dyps)
        dfeat = _dot_nt(dyps, wp_ref[...])
        e = dfeat / b["cnt"]
        e_ext = jnp.concatenate([e, next_e[...]], axis=0)
        up = lambda a, k: pltpu.roll(a, tm + HALO - k, 0)
        f2 = e_ext + up(e_ext, 1)
        f4 = f2 + up(f2, 2)
        f8 = f4 + up(f4, 4)
        f16 = f8 + up(f8, 8)
        next_e[...] = e[0:HALO]
        dpb_ref[:, 3 * D_CONV:D_LOCAL] = (_pool_lane_select(f2[:tm], f4[:tm], f8[:tm], f16[:tm], b["lane"]) - dfeat).astype(BF16)

    return pl.pallas_call(
        body,
        grid_spec=pltpu.PrefetchScalarGridSpec(
            num_scalar_prefetch=1,
            grid=(nt,),
            in_specs=[rows(D_MODEL), heads, rows(D_B), _halo_spec(tm, rev),
                      _whole(wao), _whole(convw), _whole(wco), _whole(wpool), _whole(pscale), _layer((D_MODEL, D_MODEL))],
            out_specs=[rows(D_B), heads, acc(D_MODEL, D_MODEL), acc(D_HEADS, D_MODEL), acc(D_CONV, D_MODEL),
                       acc(D_POOL, D_MODEL), acc(8, D_CONV), acc(1, D_MODEL)],
            scratch_shapes=[pltpu.VMEM((HALO, D_CONV), F32), pltpu.VMEM((HALO, D_POOL), F32)],
        ),
        out_shape=[jax.ShapeDtypeStruct((S, D_B), BF16), jax.ShapeDtypeStruct((N_HEADS, S, LANES), BF16),
                   jax.ShapeDtypeStruct((D_MODEL, D_MODEL), F32), jax.ShapeDtypeStruct((D_HEADS, D_MODEL), F32),
                   jax.ShapeDtypeStruct((D_CONV, D_MODEL), F32), jax.ShapeDtypeStruct((D_POOL, D_MODEL), F32),
                   jax.ShapeDtypeStruct((8, D_CONV), F32), jax.ShapeDtypeStruct((1, D_MODEL), F32)],
        compiler_params=_params("arbitrary"),
        name="mix_out_bwd",
    )(_layer_index(l), dxm, ox, pb, pb, wao, convw, wco, wpool, pscale, wo_all)


FF_SHARD = 2 * D_FF // 4


def ffn_fwd(x, g, w1_all, w2_all, l):
    S = x.shape[0]

    def body(l_ref, x_ref, g_ref, w1_ref, w2_ref, y_ref, u_ref):
        xf = x_ref[...]
        r = lax.rsqrt(jnp.mean(xf * xf, axis=-1, keepdims=True) + EPS)
        h = (xf * r * g_ref[...]).astype(BF16)
        u = jnp.concatenate([_dot(h, w1_ref[j]) for j in range(4)], axis=1)
        u_ref[...] = u.astype(BF16)
        gt = u[:, 0:D_FF]
        act = gt * _sigmoid(gt) * u[:, D_FF:2 * D_FF]
        y_ref[...] = xf + _dot(act.astype(BF16), w2_ref[...])

    return pl.pallas_call(
        body,
        grid_spec=pltpu.PrefetchScalarGridSpec(
            num_scalar_prefetch=1,
            grid=(S // TM,),
            in_specs=[_rows(TM, D_MODEL), _whole(g), _layer((4, D_MODEL, FF_SHARD)), _layer((D_FF, D_MODEL))],
            out_specs=[_rows(TM, D_MODEL), _rows(TM, 2 * D_FF)],
        ),
        out_shape=[jax.ShapeDtypeStruct((S, D_MODEL), F32), jax.ShapeDtypeStruct((S, 2 * D_FF), BF16)],
        compiler_params=_params("parallel"),
        name="ffn_fwd",
    )(_layer_index(l), x, g, w1_all, w2_all)


def ffn_bwd(x, g, dy, u, w1_all, w2_all, l):
    S = x.shape[0]

    def body(l_ref, x_ref, g_ref, dy_ref, u_ref, w1_ref, w2_ref, dx_ref, du_ref, act_ref, h_ref, dyb_ref, dg_ref):
        @pl.when(pl.program_id(0) == 0)
        def _():
            dg_ref[...] = jnp.zeros_like(dg_ref)

        dyf = dy_ref[...]
        dyb_ref[...] = dyf.astype(BF16)
        dact = _dot_nt(dyb_ref[...], w2_ref[...])
        gt = u_ref[:, 0:D_FF].astype(F32)
        up = u_ref[:, D_FF:2 * D_FF].astype(F32)
        sg = _sigmoid(gt)
        silu = gt * sg
        act_ref[...] = (silu * up).astype(BF16)
        du_ref[:, 0:D_FF] = (dact * up * (sg * (1.0 + gt * (1.0 - sg)))).astype(BF16)
        du_ref[:, D_FF:2 * D_FF] = (dact * silu).astype(BF16)
        dh = _dot_nt(du_ref[:, 0:FF_SHARD], w1_ref[0])
        for j in range(1, 4):
            dh = dh + _dot_nt(du_ref[:, j * FF_SHARD:(j + 1) * FF_SHARD], w1_ref[j])
        xf = x_ref[...]
        r = lax.rsqrt(jnp.mean(xf * xf, axis=-1, keepdims=True) + EPS)
        xhat = xf * r
        h_ref[...] = (xhat * g_ref[...]).astype(BF16)
        dg_ref[...] += jnp.sum(dh * xhat, axis=0, keepdims=True)
        gdh = dh * g_ref[...]
        dx_ref[...] = dyf + r * (gdh - xhat * jnp.mean(xhat * gdh, axis=-1, keepdims=True))

    return pl.pallas_call(
        body,
        grid_spec=pltpu.PrefetchScalarGridSpec(
            num_scalar_prefetch=1,
            grid=(S // TM,),
            in_specs=[_rows(TM, D_MODEL), _whole(g), _rows(TM, D_MODEL), _rows(TM, 2 * D_FF),
                      _layer((4, D_MODEL, FF_SHARD)), _layer((D_FF, D_MODEL))],
            out_specs=[_rows(TM, D_MODEL), _rows(TM, 2 * D_FF), _rows(TM, D_FF), _rows(TM, D_MODEL), _rows(TM, D_MODEL),
                       pl.BlockSpec((1, D_MODEL), lambda i, l: (0, 0))],
        ),
        out_shape=[jax.ShapeDtypeStruct((S, D_MODEL), F32), jax.ShapeDtypeStruct((S, 2 * D_FF), BF16),
                   jax.ShapeDtypeStruct((S, D_FF), BF16), jax.ShapeDtypeStruct((S, D_MODEL), BF16),
                   jax.ShapeDtypeStruct((S, D_MODEL), BF16), jax.ShapeDtypeStruct((1, D_MODEL), F32)],
        compiler_params=_params("arbitrary"),
        name="ffn_bwd",
    )(_layer_index(l), x, g, dy, u, w1_all, w2_all)


def loss_head(y, target):
    S = y.shape[0]

    def body(y_ref, t_ref, loss_ref, dy_ref):
        @pl.when(pl.program_id(0) == 0)
        def _():
            loss_ref[0, 0] = 0.0

        err = y_ref[...] - t_ref[...]
        dy_ref[...] = err * (1.0 / D_MODEL)
        loss_ref[0, 0] += 0.5 * jnp.sum(jnp.mean(err * err, axis=-1))

    return pl.pallas_call(
        body,
        grid=(S // TM,),
        in_specs=[_rows(TM, D_MODEL), _rows(TM, D_MODEL)],
        out_specs=[pl.BlockSpec((1, 1), lambda i: (0, 0), memory_space=pltpu.SMEM), _rows(TM, D_MODEL)],
        out_shape=[jax.ShapeDtypeStruct((1, 1), F32), jax.ShapeDtypeStruct((S, D_MODEL), F32)],
        compiler_params=_params("arbitrary"),
        name="loss_head",
    )(y, target)


SPLIT = {
    "w_in": ((D_MODEL, D_IN), 1),
    "w_attn_out": ((D_ATTN, D_MODEL), 1),
    "w_conv_out": ((D_CONV, D_MODEL), 1),
    "pool_w": ((N_GROUPS, D_POOL // N_GROUPS, D_MODEL // N_GROUPS), 2),
    "w_o": ((D_MODEL, D_MODEL), 0),
    "w_ffn_in": ((D_MODEL, 2 * D_FF), 1),
    "w_ffn_out": ((D_FF, D_MODEL), 0),
}
SMALL = {"norm_mix_g": D_MODEL, "forget_b": N_HEADS, "q_norm_g": HEAD_DIM, "k_norm_g": HEAD_DIM,
         "pool_scale": D_MODEL, "norm_ffn_g": D_MODEL, "conv_w": CONV_K * D_CONV}
WEIGHTS = ["norm_mix_g", "w_in", "forget_b", "q_norm_g", "k_norm_g", "w_attn_out", "conv_w", "w_conv_out", "pool_w",
           "pool_scale", "w_o", "norm_ffn_g", "w_ffn_in", "w_ffn_out"]


MIX_W = ["w_in", "w_attn_out", "w_conv_out", "pool_w", "w_o"]
FFN_W = ["w_ffn_in", "w_ffn_out"]


def _row(a):
    return a.astype(F32).reshape(1, -1)


def mix_fwd(x, full, li, small):
    n = full["w_o"].shape[0]
    wa, wf, wb = w_in_prep(full["w_in"], li)
    wao, wco, wpool, convw = branch_w_prep(full["w_attn_out"], full["w_conv_out"], full["pool_w"], full["conv_w"], li)
    w = dict(wa=wa, wf=wf, wb=wb, wao=wao, wco=wco, wpool=wpool, convw=convw,
             w_o=full["w_o"].reshape(n, D_MODEL, D_MODEL), at=li,
             g1=_row(small["norm_mix_g"]), pscale=_row(small["pool_scale"]),
             gq=_row(jnp.tile(small["q_norm_g"], N_HEADS)), gk=_row(jnp.tile(small["k_norm_g"], N_HEADS)),
             fb=_row(jnp.pad(small["forget_b"], (0, D_F - N_HEADS))))
    h, pa, pf, pb = in_proj_fwd(x, w["g1"], wa, wf, wb)
    qx, kx, vx = qk_prep(pa, pf, w["gq"], w["gk"], w["fb"])
    ox, qxb = attn_fwd(qx, kx, vx)
    xm = mix_out_fwd(x, ox, pb, wao, convw, wco, wpool, w["pscale"], w["w_o"], li)
    return xm, w, dict(x=x, h=h, pa=pa, pf=pf, pb=pb, kx=kx, vx=vx, ox=ox, qxb=qxb)


def ffn_half_fwd(xm, full, li, g2):
    n = full["w_ffn_out"].shape[0]
    w = dict(w1=full["w_ffn_in"], w2=full["w_ffn_out"].reshape(n, D_FF, D_MODEL), at=li, g2=_row(g2))
    y, u = ffn_fwd(xm, w["g2"], w["w1"], w["w2"], li)
    return y, w, dict(xm=xm, u=u)


def ffn_half_bwd(dx, w, s, gi, big):
    n = big["w_ffn_out"].shape[0]
    big = dict(big)
    dxm, du, act, h2, dyb, dg2 = ffn_bwd(s["xm"], w["g2"], dx, s["u"], w["w1"], w["w2"], w["at"])
    big["w_ffn_out"] = wgrad_into(act, dyb, "wgrad_ffn_out", big["w_ffn_out"].reshape(n, D_FF, D_MODEL),
                                  gi).reshape(big["w_ffn_out"].shape)
    big["w_ffn_in"] = wgrad_into(h2, du, "wgrad_ffn_in", big["w_ffn_in"], gi)
    return dxm, big, dict(norm_ffn_g=dg2[0])


def mix_bwd(dxm, w, s, gi, big):
    big = dict(big)
    dpb, dox, dwo, dwao, dwco, dwpool, dconvw, dpscale = mix_out_bwd(
        dxm, s["ox"], s["pb"], w["wao"], w["convw"], w["wco"], w["wpool"], w["pscale"], w["w_o"], w["at"])
    dqx, dkx, dvx = attn_bwd(s["qxb"], s["kx"], s["vx"], dox)
    dpa, dpf, dgq, dgk, dfb = attn_bwd_post(s["pa"], s["pf"], dqx, dkx, dvx, w["gq"], w["gk"], w["fb"])
    dx, dg1 = in_proj_bwd(s["x"], w["g1"], dxm, dpa, dpf, dpb, w["wa"], w["wf"], w["wb"])
    big["w_in"] = w_in_unprep(wgrad(s["h"], dpa, "wgrad_in_qkv"), wgrad(s["h"], dpf, "wgrad_in_f"),
                              wgrad(s["h"], dpb, "wgrad_in_b"), big["w_in"], gi)
    big["w_attn_out"], big["w_conv_out"], big["pool_w"], big["w_o"] = branch_g_place(
        dwao, dwco, dwpool, dwo, (big["w_attn_out"], big["w_conv_out"], big["pool_w"], big["w_o"]), gi)
    sm = dict(norm_mix_g=dg1[0], forget_b=dfb[0, 0:N_HEADS], q_norm_g=dgq.reshape(N_HEADS, HEAD_DIM).sum(0),
              k_norm_g=dgk.reshape(N_HEADS, HEAD_DIM).sum(0), pool_scale=dpscale[0],
              conv_w=dconvw[0:CONV_K].reshape(-1))
    return dx, big, sm


def grad_buffers(full):
    return {n: lax.empty(full[n].shape, F32) for n in SPLIT}


def local_step(x, target, gathered, small):
    n_layers = small["norm_mix_g"].shape[0]
    done = []
    for l in range(n_layers):
        xm, wm, sm = mix_fwd(x, gathered, l, {n: v[l] for n, v in small.items()})
        x, wf, sf = ffn_half_fwd(xm, gathered, l, small["norm_ffn_g"][l])
        done.append((wm, sm, wf, sf))
    loss, dx = loss_head(x, target)
    big = grad_buffers(gathered)
    small_grads = [None] * n_layers
    for l in reversed(range(n_layers)):
        wm, sm, wf, sf = done[l]
        dxm, ffn_big, g_ffn = ffn_half_bwd(dx, wf, sf, l, {n: big[n] for n in FFN_W})
        dx, mix_big, g_mix = mix_bwd(dxm, wm, sm, l, {n: big[n] for n in MIX_W})
        big = {**ffn_big, **mix_big}
        small_grads[l] = {**g_ffn, **g_mix}
    return loss, dx, big, {n: jnp.stack([g[n] for g in small_grads]) for n in SMALL}


def adamw(w, g, m, v, name):
    R, C = w.shape
    tm = 256 if R % 256 == 0 else R

    def body(w_ref, g_ref, m_ref, v_ref, d_ref, nm_ref, nv_ref):
        gr = g_ref[...]
        m_new = ADAM_B1 * m_ref[...] + (1.0 - ADAM_B1) * gr
        v_new = ADAM_B2 * v_ref[...] + (1.0 - ADAM_B2) * jnp.square(gr)
        nm_ref[...] = m_new
        nv_ref[...] = v_new
        m_hat = m_new / (1.0 - ADAM_B1 ** ADAM_STEP)
        v_hat = v_new / (1.0 - ADAM_B2 ** ADAM_STEP)
        d_ref[...] = -ADAM_LR * (m_hat / (jnp.sqrt(v_hat) + ADAM_EPS) + ADAM_WD * w_ref[...])

    spec = _rows(tm, C)
    out = jax.ShapeDtypeStruct((R, C), F32)
    return pl.pallas_call(
        body,
        grid=(R // tm,),
        in_specs=[spec] * 4,
        out_specs=[spec] * 3,
        out_shape=[out] * 3,
        compiler_params=_params("parallel"),
        name=name,
    )(w, g, m, v)


MESH = pl.DeviceIdType.MESH
HBM_REF = pl.BlockSpec(memory_space=pl.ANY)
N_CHIPS = 4
N_DEV = 8
SMALL_SHAPE = (128, LANES)


def _mesh_pos():
    return lax.axis_index("x"), lax.axis_index("y"), lax.axis_index("c")


def _other_chips(x, y):
    return [(1 - x, y), (x, 1 - y), (1 - x, 1 - y)]


def _remote(src, dst, send_sem, recv_sem, to):
    return pltpu.make_async_remote_copy(src_ref=src, dst_ref=dst, send_sem=send_sem, recv_sem=recv_sem,
                                        device_id=to, device_id_type=MESH)


def _row_tile(rows):
    for tm in (256, 176, 128):
        if rows % tm == 0:
            return tm
    return rows


def _as4(a):
    return a.reshape(a.shape[0], a.shape[1], -1, a.shape[-1])


def _core_rows(buf, c):
    R = buf.shape[2]
    if R % 2:
        return pl.ds(0, R), pl.ds(0, R), False
    return pl.ds(c * (R // 2), R // 2), pl.ds((1 - c) * (R // 2), R // 2), True


def place_shard(chip, w, lo, hi, dtype, name, after=None):
    w3 = w.reshape(w.shape[0], -1, w.shape[-1])
    _, R, C = w3.shape
    tm = _row_tile(R)
    idle = [] if after is None else [after]

    def body(chip_ref, w_ref, *rest):
        rest[-1][...] = w_ref[...].astype(dtype)

    return pl.pallas_call(
        body,
        grid_spec=pltpu.PrefetchScalarGridSpec(
            num_scalar_prefetch=1,
            grid=(hi - lo, R // tm),
            in_specs=[pl.BlockSpec((None, tm, C), lambda l, i, chip: (lo + l, i, 0))] + [_whole(a) for a in idle],
            out_specs=pl.BlockSpec((None, None, tm, C), lambda l, i, chip: (l, chip[0], i, 0)),
        ),
        out_shape=jax.ShapeDtypeStruct((hi - lo, N_CHIPS, R, C), dtype),
        compiler_params=_params("parallel", "parallel"),
        name=name,
    )(chip, w3, *idle)


HBM_SPACE = pl.BlockSpec(memory_space=pltpu.HBM)
SEM_SPACE = pl.BlockSpec(memory_space=pltpu.SEMAPHORE)
IN_FLIGHT = pltpu.SideEffectType.DATAFLOW_SIDE_EFFECTING


def _sem_table(send_sems, recv_sems):
    return lambda t, j: (send_sems.at[t, j], recv_sems.at[t, j])


def _sem_per_peer(send_sems, recv_sems):
    return lambda t, j: (send_sems[j], recv_sems[j])


def _gather_ici(bufs, sem, sends=True, lands=True):
    x, y, c = _mesh_pos()
    me = 2 * x + y
    out, into = [], []
    for t, buf in enumerate(bufs):
        mine, _, _ = _core_rows(buf, c)
        part = lambda k, buf=buf, mine=mine: buf.at[pl.ds(0, buf.shape[0]), k, mine]
        for j, (px, py) in enumerate(_other_chips(x, y)):
            if sends:
                out.append(_remote(part(me), part(me), *sem(t, j), (px, py, c)))
            if lands:
                into.append(_remote(part(2 * px + py), part(2 * px + py), *sem(t, j), (px, py, c)))
    return out, into


def _gather_d2d(bufs, send_sems, recv_sems, first):
    x, y, c = _mesh_pos()
    sends, lands = [], []
    for t, buf in enumerate(bufs):
        mine, theirs, split = _core_rows(buf, c)
        if not split:
            continue
        for j, (px, py) in enumerate(_other_chips(x, y)):
            part = lambda rows, buf=buf, k=2 * px + py: buf.at[pl.ds(0, buf.shape[0]), k, rows]
            sems = (send_sems.at[t, first + j], recv_sems.at[t, first + j], (x, y, 1 - c))
            sends.append(_remote(part(mine), part(mine), *sems))
            lands.append(_remote(part(theirs), part(theirs), *sems))
    return sends, lands


def gather_shards(bufs):
    n = len(bufs)

    def body(*refs):
        outs = refs[n:2 * n]
        send_sems, recv_sems = refs[2 * n:]
        ici_out, ici_in = _gather_ici(outs, _sem_table(send_sems, recv_sems))
        d2d_out, d2d_in = _gather_d2d(outs, send_sems, recv_sems, 3)
        for cp in ici_out:
            cp.start()
        for cp in ici_in:
            cp.wait_recv()
        for cp in d2d_out:
            cp.start()
        for cp in d2d_in:
            cp.wait_recv()
        for cp in ici_out + d2d_out:
            cp.wait_send()

    return pl.pallas_call(
        body,
        in_specs=[HBM_REF] * n,
        out_specs=[HBM_REF] * n,
        out_shape=[jax.ShapeDtypeStruct(b.shape, b.dtype) for b in bufs],
        input_output_aliases={t: t for t in range(n)},
        scratch_shapes=[pltpu.SemaphoreType.DMA((n, 6)), pltpu.SemaphoreType.DMA((n, 6))],
        name="gather_shards",
    )(*bufs)


def gather_start(bufs):
    n = len(bufs)

    def body(*refs):
        send_sems, recv_sems = refs[n:n + 3], refs[n + 3:n + 6]
        outs = refs[n + 6:2 * n + 6]
        token = refs[2 * n + 6]
        for cp in _gather_ici(outs, _sem_per_peer(send_sems, recv_sems), lands=False)[0]:
            cp.start()
        token[...] = jnp.zeros_like(token)

    res = pl.pallas_call(
        body,
        in_specs=[HBM_SPACE] * n,
        out_specs=[SEM_SPACE] * 6 + [HBM_SPACE] * n + [pl.BlockSpec(memory_space=pltpu.VMEM)],
        out_shape=[pltpu.SemaphoreType.DMA(())] * 6
        + [pltpu.HBM(b.shape, b.dtype) for b in bufs] + [jax.ShapeDtypeStruct((8, LANES), F32)],
        input_output_aliases={t: t + 6 for t in range(n)},
        compiler_params=pltpu.CompilerParams(has_side_effects=IN_FLIGHT),
        name="gather_start",
    )(*[pltpu.with_memory_space_constraint(b, pltpu.HBM) for b in bufs])
    return list(res[0:3]), list(res[3:6]), list(res[6:n + 6]), res[n + 6]


def gather_wait(send_sems, recv_sems, bufs, after):
    n = len(bufs)

    def body(*refs):
        ins_send, ins_recv = refs[n:n + 3], refs[n + 3:n + 6]
        outs = refs[n + 7:]
        sends, lands = _gather_ici(outs, _sem_per_peer(ins_send, ins_recv))
        for cp in lands:
            cp.wait_recv()
        for cp in sends:
            cp.wait_send()

    return list(pl.pallas_call(
        body,
        in_specs=[HBM_SPACE] * n + [SEM_SPACE] * 6 + [HBM_REF],
        out_specs=[HBM_SPACE] * n,
        out_shape=[pltpu.HBM(b.shape, b.dtype) for b in bufs],
        input_output_aliases={t: t for t in range(n)},
        compiler_params=pltpu.CompilerParams(has_side_effects=IN_FLIGHT),
        name="gather_wait",
    )(*bufs, *send_sems, *recv_sems, after))


def gather_forward(bufs):
    n = len(bufs)

    def body(*refs):
        outs = refs[n:2 * n]
        send_sems, recv_sems = refs[2 * n:]
        sends, lands = _gather_d2d(outs, send_sems, recv_sems, 0)
        for cp in sends:
            cp.start()
        for cp in lands:
            cp.wait_recv()
        for cp in sends:
            cp.wait_send()

    return list(pl.pallas_call(
        body,
        in_specs=[HBM_REF] * n,
        out_specs=[HBM_REF] * n,
        out_shape=[jax.ShapeDtypeStruct(b.shape, b.dtype) for b in bufs],
        input_output_aliases={t: t for t in range(n)},
        scratch_shapes=[pltpu.SemaphoreType.DMA((n, 3)), pltpu.SemaphoreType.DMA((n, 3))],
        name="gather_forward",
    )(*bufs))


def pair_exchange(grads):
    n = len(grads)

    def body(*refs):
        ins, outs = refs[:n], refs[n:2 * n]
        send_sems, recv_sems = refs[2 * n:]
        cps = _pair_copies(ins, outs, lambda t: (send_sems.at[t], recv_sems.at[t]))
        for cp in cps:
            cp.start()
        for cp in cps:
            cp.wait()

    return list(pl.pallas_call(
        body,
        in_specs=[HBM_REF] * n,
        out_specs=[HBM_REF] * n,
        out_shape=[jax.ShapeDtypeStruct(_half_shape(g), g.dtype) for g in grads],
        scratch_shapes=[pltpu.SemaphoreType.DMA((n,)), pltpu.SemaphoreType.DMA((n,))],
        name="pair_exchange",
    )(*grads))


def _half_shape(g):
    return g.shape[:2] + (g.shape[2] // 2, g.shape[3])


def _pair_copies(grads, lands, sem):
    x, y, c = _mesh_pos()
    cps = []
    for t in range(len(grads)):
        _, theirs, _ = _core_rows(grads[t], c)
        src = grads[t].at[pl.ds(0, grads[t].shape[0]), pl.ds(0, N_CHIPS), theirs]
        cps.append(_remote(src, lands[t], *sem(t), (x, y, 1 - c)))
    return cps


def pair_exchange_start(grads):
    n = len(grads)
    lands = [lax.empty(_half_shape(g), g.dtype) for g in grads]

    def body(*refs):
        send_sem, recv_sem = refs[2 * n:2 * n + 2]
        outs = refs[2 * n + 2:4 * n + 2]
        token = refs[4 * n + 2]
        for cp in _pair_copies(outs[:n], outs[n:], lambda t: (send_sem, recv_sem)):
            cp.start()
        token[...] = jnp.zeros_like(token)

    res = pl.pallas_call(
        body,
        in_specs=[HBM_SPACE] * (2 * n),
        out_specs=[SEM_SPACE] * 2 + [HBM_SPACE] * (2 * n) + [pl.BlockSpec(memory_space=pltpu.VMEM)],
        out_shape=[pltpu.SemaphoreType.DMA(())] * 2 + [pltpu.HBM(a.shape, a.dtype) for a in list(grads) + lands]
        + [jax.ShapeDtypeStruct((8, LANES), F32)],
        input_output_aliases={t: t + 2 for t in range(2 * n)},
        compiler_params=pltpu.CompilerParams(has_side_effects=IN_FLIGHT),
        name="pair_exchange_start",
    )(*[pltpu.with_memory_space_constraint(a, pltpu.HBM) for a in list(grads) + lands])
    return res[0], res[1], list(res[2:n + 2]), list(res[n + 2:2 * n + 2]), res[2 * n + 2]


def pair_exchange_wait(send_sem, recv_sem, grads, lands, after):
    n = len(grads)

    def body(*refs):
        in_send, in_recv = refs[2 * n:2 * n + 2]
        outs = refs[2 * n + 3:]
        for cp in _pair_copies(outs[:n], outs[n:], lambda t: (in_send, in_recv)):
            cp.wait()

    res = pl.pallas_call(
        body,
        in_specs=[HBM_SPACE] * (2 * n) + [SEM_SPACE] * 2 + [HBM_REF],
        out_specs=[HBM_SPACE] * (2 * n),
        out_shape=[pltpu.HBM(a.shape, a.dtype) for a in list(grads) + list(lands)],
        input_output_aliases={t: t for t in range(2 * n)},
        compiler_params=pltpu.CompilerParams(has_side_effects=IN_FLIGHT),
        name="pair_exchange_wait",
    )(*grads, *lands, send_sem, recv_sem, after)
    return list(res[:n]), list(res[n:])


def pair_sum(core, g, t, name):
    n, _, half, C = t.shape
    tm = _row_tile(half)
    per = half // tm

    def body(c_ref, g_ref, t_ref, o_ref):
        o_ref[...] = (g_ref[...] + t_ref[...]).astype(BF16)

    tile = pl.BlockSpec((None, tm, C), lambda a, i, c: (a, i, 0))
    out = pl.pallas_call(
        body,
        grid_spec=pltpu.PrefetchScalarGridSpec(
            num_scalar_prefetch=1,
            grid=(n * N_CHIPS, per),
            in_specs=[pl.BlockSpec((None, tm, C), lambda a, i, c: (a, per * c[0] + i, 0)), tile],
            out_specs=tile,
        ),
        out_shape=jax.ShapeDtypeStruct((n * N_CHIPS, half, C), BF16),
        compiler_params=_params("parallel", "parallel"),
        name=name,
    )(core, g.reshape(n * N_CHIPS, 2 * half, C), t.reshape(n * N_CHIPS, half, C))
    return out.reshape(t.shape)


def _chip_copies(sums, recv, sem):
    x, y, c = _mesh_pos()
    cps = []
    for t in range(len(sums)):
        for j, (px, py) in enumerate(_other_chips(x, y)):
            src = sums[t].at[pl.ds(0, sums[t].shape[0]), 2 * px + py]
            cps.append(_remote(src, recv[t].at[j], *sem(t, j), (px, py, c)))
    return cps


def _recv_shape(s):
    return (N_CHIPS - 1, s.shape[0]) + s.shape[2:]


def chip_exchange(sums):
    n = len(sums)

    def body(*refs):
        ins, outs = refs[:n], refs[n:2 * n]
        cps = _chip_copies(ins, outs, _sem_table(*refs[2 * n:]))
        for cp in cps:
            cp.start()
        for cp in cps:
            cp.wait()

    return list(pl.pallas_call(
        body,
        in_specs=[HBM_REF] * n,
        out_specs=[HBM_REF] * n,
        out_shape=[jax.ShapeDtypeStruct(_recv_shape(s), s.dtype) for s in sums],
        scratch_shapes=[pltpu.SemaphoreType.DMA((n, 3)), pltpu.SemaphoreType.DMA((n, 3))],
        name="chip_exchange",
    )(*sums))


def chip_exchange_start(sums):
    n = len(sums)
    lands = [lax.empty(_recv_shape(s), s.dtype) for s in sums]

    def body(*refs):
        send_sems, recv_sems = refs[2 * n:2 * n + 3], refs[2 * n + 3:2 * n + 6]
        outs = refs[2 * n + 6:4 * n + 6]
        token = refs[4 * n + 6]
        for cp in _chip_copies(outs[:n], outs[n:2 * n], _sem_per_peer(send_sems, recv_sems)):
            cp.start()
        token[...] = jnp.zeros_like(token)

    res = pl.pallas_call(
        body,
        in_specs=[HBM_SPACE] * (2 * n),
        out_specs=[SEM_SPACE] * 6 + [HBM_SPACE] * (2 * n) + [pl.BlockSpec(memory_space=pltpu.VMEM)],
        out_shape=[pltpu.SemaphoreType.DMA(())] * 6 + [pltpu.HBM(a.shape, a.dtype) for a in list(sums) + lands]
        + [jax.ShapeDtypeStruct((8, LANES), F32)],
        input_output_aliases={t: t + 6 for t in range(2 * n)},
        compiler_params=pltpu.CompilerParams(has_side_effects=IN_FLIGHT),
        name="chip_exchange_start",
    )(*[pltpu.with_memory_space_constraint(a, pltpu.HBM) for a in list(sums) + lands])
    return list(res[0:3]), list(res[3:6]), list(res[6:n + 6]), list(res[n + 6:2 * n + 6]), res[2 * n + 6]


def chip_exchange_wait(send_sems, recv_sems, sums, lands, after):
    n = len(sums)

    def body(*refs):
        ins_send, ins_recv = refs[2 * n:2 * n + 3], refs[2 * n + 3:2 * n + 6]
        outs = refs[2 * n + 7:]
        for cp in _chip_copies(outs[:n], outs[n:], _sem_per_peer(ins_send, ins_recv)):
            cp.wait()

    res = pl.pallas_call(
        body,
        in_specs=[HBM_SPACE] * (2 * n) + [SEM_SPACE] * 6 + [HBM_REF],
        out_specs=[HBM_SPACE] * (2 * n),
        out_shape=[pltpu.HBM(a.shape, a.dtype) for a in list(sums) + list(lands)],
        input_output_aliases={t: t for t in range(2 * n)},
        compiler_params=pltpu.CompilerParams(has_side_effects=IN_FLIGHT),
        name="chip_exchange_wait",
    )(*sums, *lands, *send_sems, *recv_sems, after)
    return list(res[:n]), list(res[n:])


def chip_sum_into(core, chip, recv, sums, total, lo, name):
    _, n, half, C = recv.shape
    tm = _row_tile(half)
    per = half // tm

    def body(c_ref, k_ref, r_ref, s_ref, t_ref, o_ref):
        acc = s_ref[...].astype(F32)
        for j in range(N_CHIPS - 1):
            acc = acc + r_ref[j].astype(F32)
        o_ref[...] = acc

    return pl.pallas_call(
        body,
        grid_spec=pltpu.PrefetchScalarGridSpec(
            num_scalar_prefetch=2,
            grid=(n, per),
            in_specs=[pl.BlockSpec((N_CHIPS - 1, None, tm, C), lambda a, i, c, k: (0, a, i, 0)),
                      pl.BlockSpec((None, None, tm, C), lambda a, i, c, k: (a, k[0], i, 0)),
                      HBM_REF],
            out_specs=pl.BlockSpec((None, tm, C), lambda a, i, c, k: (lo + a, per * c[0] + i, 0)),
        ),
        out_shape=jax.ShapeDtypeStruct(total.shape, F32),
        input_output_aliases={4: 0},
        compiler_params=_params("parallel", "parallel"),
        name=name,
    )(core, chip, recv, sums, total)


def sibling_share(totals):
    n = len(totals)

    def body(*refs):
        outs = refs[n:2 * n]
        send_sems, recv_sems = refs[2 * n:]
        x, y, c = _mesh_pos()
        half = lambda t, h: outs[t].at[pl.ds(0, DEPTH), pl.ds(h * (outs[t].shape[1] // 2), outs[t].shape[1] // 2)]
        sent = [_remote(half(t, c), half(t, c), send_sems.at[t], recv_sems.at[t], (x, y, 1 - c)) for t in range(n)]
        for cp in sent:
            cp.start()
        for t in range(n):
            _remote(half(t, 1 - c), half(t, 1 - c), send_sems.at[t], recv_sems.at[t], (x, y, 1 - c)).wait_recv()
        for cp in sent:
            cp.wait_send()

    return pl.pallas_call(
        body,
        in_specs=[HBM_REF] * n,
        out_specs=[HBM_REF] * n,
        out_shape=[jax.ShapeDtypeStruct(t.shape, t.dtype) for t in totals],
        input_output_aliases={t: t for t in range(n)},
        scratch_shapes=[pltpu.SemaphoreType.DMA((n,)), pltpu.SemaphoreType.DMA((n,))],
        name="sibling_share",
    )(*totals)


def small_allgather(small):
    def body(s_ref, a_ref, send_sems, recv_sems, local_sem):
        x, y, c = _mesh_pos()
        me = 4 * x + 2 * y + c
        own = pltpu.make_async_copy(s_ref, a_ref.at[me], local_sem)
        own.start()
        sent = []
        for k in range(1, N_DEV):
            peer = (x ^ (k >> 2), y ^ ((k >> 1) & 1), c ^ (k & 1))
            cp = _remote(s_ref, a_ref.at[me], send_sems.at[k - 1], recv_sems.at[k - 1], peer)
            cp.start()
            sent.append(cp)
        for k in range(1, N_DEV):
            px, py, pc = x ^ (k >> 2), y ^ ((k >> 1) & 1), c ^ (k & 1)
            _remote(s_ref, a_ref.at[4 * px + 2 * py + pc], send_sems.at[k - 1], recv_sems.at[k - 1], (px, py, pc)).wait_recv()
        for cp in sent:
            cp.wait_send()
        own.wait()

    return pl.pallas_call(
        body,
        in_specs=[HBM_REF],
        out_specs=HBM_REF,
        out_shape=jax.ShapeDtypeStruct((N_DEV,) + SMALL_SHAPE, small.dtype),
        scratch_shapes=[pltpu.SemaphoreType.DMA((N_DEV - 1,)), pltpu.SemaphoreType.DMA((N_DEV - 1,)), pltpu.SemaphoreType.DMA],
        name="small_allgather",
    )(small)


def small_sum(blocks):
    def body(a_ref, o_ref):
        acc = a_ref[0]
        for d in range(1, N_DEV):
            acc = acc + a_ref[d]
        o_ref[...] = acc

    return pl.pallas_call(
        body,
        in_specs=[pl.BlockSpec(memory_space=pltpu.VMEM)],
        out_specs=pl.BlockSpec(memory_space=pltpu.VMEM),
        out_shape=jax.ShapeDtypeStruct(SMALL_SHAPE, F32),
        name="small_sum",
    )(blocks)


def pack_small(grads, loss):
    flat = jnp.concatenate([grads[n].reshape(-1) for n in SMALL] + [loss.reshape(-1)])
    size = SMALL_SHAPE[0] * SMALL_SHAPE[1]
    return jnp.pad(flat, (0, size - flat.shape[0])).reshape(SMALL_SHAPE)


def unpack_small(packed):
    flat = packed.reshape(-1)
    out, off = {}, 0
    for n, size in SMALL.items():
        out[n] = flat[off:off + DEPTH * size].reshape(DEPTH, size)
        off += DEPTH * size
    return out, flat[off]


def kernel(x, norm_mix_g, w_in, forget_b, q_norm_g, k_norm_g, w_attn_out, conv_w, w_conv_out, pool_w, pool_scale, w_o, norm_ffn_g, w_ffn_in, w_ffn_out, loss_target, m_norm_mix_g, m_w_in, m_forget_b, m_q_norm_g, m_k_norm_g, m_w_attn_out, m_conv_w, m_w_conv_out, m_pool_w, m_pool_scale, m_w_o, m_norm_ffn_g, m_w_ffn_in, m_w_ffn_out, v_norm_mix_g, v_w_in, v_forget_b, v_q_norm_g, v_k_norm_g, v_w_attn_out, v_conv_w, v_w_conv_out, v_pool_w, v_pool_scale, v_w_o, v_norm_ffn_g, v_w_ffn_in, v_w_ffn_out):
    given = dict(locals())
    weights = {n: given[n] for n in WEIGHTS}

    core = lax.axis_index("c").astype(jnp.int32)
    chip = (2 * lax.axis_index("x") + lax.axis_index("y")).astype(jnp.int32)

    core, chip = core.reshape(1), chip.reshape(1)
    mix_names, all_names = MIX_W + ["conv_w"], MIX_W + FFN_W + ["conv_w"]

    def placed(lo, hi, names, after=None):
        piece = None if after is None else _as4(next(iter(after.values())))[0, 0, 0:HALO, 0:LANES]
        return [place_shard(chip, weights[n], lo, hi, F32 if n == "conv_w" else BF16, "place_" + n,
                            piece if i == 0 else None) for i, n in enumerate(names)]

    def as_weights(bufs, names):
        return {n: b.reshape(b.shape[:2] + weights[n].shape[1:]) for n, b in zip(names, bufs)}

    def landed(start, names, after):
        return as_weights(gather_forward(gather_wait(*start[:3], after)), names)

    def layer_small(l, *tokens):
        small = {n: weights[n][l] for n in SMALL if n != "conv_w"}
        for t in tokens:
            small["norm_mix_g"] = small["norm_mix_g"] + t[0, 0]
        return small

    done = [None] * DEPTH
    first = as_weights(gather_shards(placed(0, 1, mix_names)), mix_names)
    ffn0 = gather_start(placed(0, 1, FFN_W, after=first))
    layer1 = gather_start(placed(1, 2, all_names, after=first))
    xm, wm, sm = mix_fwd(x[0], first, 0, layer_small(0, ffn0[3], layer1[3]))
    xs, wf, sf = ffn_half_fwd(xm, landed(ffn0, FFN_W, xm), 0, weights["norm_ffn_g"][0])
    done[0] = (wm, sm, wf, sf)
    full = landed(layer1, all_names, xs)
    rest = gather_start(placed(2, DEPTH, all_names, after=full))
    xm, wm, sm = mix_fwd(xs, full, 0, layer_small(1, rest[3]))
    xs, wf, sf = ffn_half_fwd(xm, full, 0, weights["norm_ffn_g"][1])
    done[1] = (wm, sm, wf, sf)
    full = landed(rest, all_names, xs)
    for l in range(2, DEPTH):
        xm, wm, sm = mix_fwd(xs, full, l - 2, layer_small(l))
        xs, wf, sf = ffn_half_fwd(xm, full, l - 2, weights["norm_ffn_g"][l])
        done[l] = (wm, sm, wf, sf)
    loss, dx = loss_head(xs, loss_target[0])

    small_grads = [None] * DEPTH
    totals = {n: lax.empty((DEPTH,) + _as4(weights[n][None]).shape[2:], F32) for n in SPLIT}

    def buffers(names, n_layers):
        return {n: lax.empty((n_layers, N_CHIPS) + weights[n].shape[1:], F32) for n in names}

    def pair_sums(names, grads, theirs):
        return [pair_sum(core, a, t, "pair_sum_" + n) for n, a, t in zip(names, grads, theirs)]

    def add_chips(names, recv, sums, lo):
        for n, r, s in zip(names, recv, sums):
            totals[n] = chip_sum_into(core, chip, r, s, totals[n], lo, "chip_sum_" + n)

    big = buffers(SPLIT, DEPTH - 1)
    for l in reversed(range(1, DEPTH)):
        wm, sm, wf, sf = done[l]
        dxm, ffn_big, g_ffn = ffn_half_bwd(dx, wf, sf, l - 1, {n: big[n] for n in FFN_W})
        dx, mix_big, g_mix = mix_bwd(dxm, wm, sm, l - 1, {n: big[n] for n in MIX_W})
        big = {**ffn_big, **mix_big}
        small_grads[l] = {**g_ffn, **g_mix}
    pair = pair_exchange_start([_as4(big[n]) for n in SPLIT])

    wm, sm, wf, sf = done[0]
    wf = dict(wf, g2=wf["g2"] + pair[4][0:1, 0:1])
    dxm, ffn_big, g_ffn = ffn_half_bwd(dx, wf, sf, 0, buffers(FFN_W, 1))
    chips = chip_exchange_start(pair_sums(SPLIT, *pair_exchange_wait(*pair[:4], dxm)))
    ffn_grads = [_as4(ffn_big[n]) for n in FFN_W]
    ffn_chips = chip_exchange_start(pair_sums(FFN_W, ffn_grads, pair_exchange(ffn_grads)))
    wm = dict(wm, pscale=wm["pscale"] + chips[4][0:1, 0:1] + ffn_chips[4][0:1, 0:1])
    dx, mix_big, g_mix = mix_bwd(dxm, wm, sm, 0, buffers(MIX_W, 1))
    small_grads[0] = {**g_ffn, **g_mix}

    sums, recv = chip_exchange_wait(*chips[:4], dx)
    add_chips(SPLIT, recv, sums, 1)
    sums, recv = chip_exchange_wait(*ffn_chips[:4], dx)
    add_chips(FFN_W, recv, sums, 0)
    mix_grads = [_as4(mix_big[n]) for n in MIX_W]
    sums = pair_sums(MIX_W, mix_grads, pair_exchange(mix_grads))
    add_chips(MIX_W, chip_exchange(sums), sums, 0)
    shared = sibling_share([totals[n] for n in SPLIT])
    reduced = {n: t.reshape(weights[n].shape) for n, t in zip(SPLIT, shared)}

    small_grads = {n: jnp.stack([g[n] for g in small_grads]) for n in SMALL}
    small_total, loss_sum = unpack_small(small_sum(small_allgather(pack_small(small_grads, loss))))
    chip = chip[0]
    cols = D_CONV // N_CHIPS
    small_total["conv_w"] = lax.dynamic_slice_in_dim(small_total["conv_w"].reshape(DEPTH, CONV_K, D_CONV), chip * cols, cols, axis=2)
    for n in SMALL:
        reduced[n] = small_total[n].reshape(weights[n].shape)

    deltas, new_m, new_v = {}, {}, {}
    for n in WEIGHTS:
        w = weights[n]
        flat = (-1, w.shape[-1])
        d, nm, nv = adamw(w.reshape(flat), reduced[n].reshape(flat), given["m_" + n].reshape(flat),
                          given["v_" + n].reshape(flat), "adamw_" + n)
        deltas[n], new_m[n], new_v[n] = d.reshape(w.shape), nm.reshape(w.shape), nv.reshape(w.shape)
    return (loss_sum, dx[None], *[reduced[n] for n in WEIGHTS], *[deltas[n] for n in WEIGHTS],
            *[new_m[n] for n in WEIGHTS], *[new_v[n] for n in WEIGHTS])
```

```python
import functools

import numpy as np
import jax
import jax.numpy as jnp
from jax import lax
from jax.experimental import pallas as pl
from jax.experimental.pallas import tpu as pltpu

F32 = jnp.float32
BF16 = jnp.bfloat16

D_MODEL = 1024
DEPTH = 4
HEAD_DIM = 64
N_HEADS = 8
D_ATTN = 512
D_CONV = 256
D_POOL = 256
D_FF = 2816
D_IN = 5640
CONV_K = 3
POOL_WINDOWS = (2, 4, 8, 16)
N_GROUPS = len(POOL_WINDOWS)
EPS = 1e-6
ADAM_LR, ADAM_B1, ADAM_B2, ADAM_EPS, ADAM_WD, ADAM_STEP = 0.001, 0.9, 0.999, 1e-08, 0.01, 10

D_QKV = 3 * D_ATTN
D_F = 128
D_B = 3 * D_CONV + D_POOL + 3 * D_MODEL
D_LOCAL = 3 * D_CONV + D_POOL

LANES = 128
D_HEADS = N_HEADS * LANES
HALO = 16
VMEM_LIMIT = 56 * 1024 * 1024
NEG = -1e30
LOG2E = 1.4426950408889634
LN2 = 0.6931471805599453

TM = 256
TM_MIX = 256
TQ = 1024
TK = 1024
KB = 1024
QT = 1024

LANE_C = 64
LANE_ONE = 67
LANE_LSE = 70
N_PIECES = 3


def _dot(a, b):
    return jnp.dot(a, b, preferred_element_type=F32)


def _dot_nt(a, b):
    return lax.dot_general(a, b, (((1,), (1,)), ((), ())), preferred_element_type=F32)


def _dot_tn(a, b):
    return lax.dot_general(a, b, (((0,), (0,)), ((), ())), preferred_element_type=F32)


def _params(*sem):
    return pltpu.CompilerParams(dimension_semantics=sem, vmem_limit_bytes=VMEM_LIMIT)


def _rows(tm, n):
    return pl.BlockSpec((tm, n), lambda i, *_: (i, 0))


def _whole(a):
    nd = a.ndim
    return pl.BlockSpec(a.shape, lambda *_: (0,) * nd)


def _layer(shape):
    nd = len(shape)
    return pl.BlockSpec((None,) + tuple(shape), lambda *a: (a[-1][0],) + (0,) * nd)


def _layer_index(l):
    return jnp.full((1,), l, jnp.int32)


def _split_bf16(x):
    hi = x.astype(BF16)
    lo = (x - hi.astype(F32)).astype(BF16)
    return hi, lo


def _pieces(x):
    p1 = x.astype(BF16)
    r1 = x - p1.astype(F32)
    p2 = r1.astype(BF16)
    p3 = (r1 - p2.astype(F32)).astype(BF16)
    return p1, p2, p3


def _sigmoid(x):
    return 1.0 / (1.0 + jnp.exp(-x))


def w_in_prep(win, l):
    tr = 256
    n = D_IN // 4
    v_rest = D_QKV - n
    b0 = v_rest + N_HEADS

    def body(s0, s1, s2, s3, wa_ref, wf_ref, wb_ref):
        b = s1[...]
        wa_ref[...] = jnp.concatenate([s0[...], b[:, 0:v_rest]], axis=1)
        wf_ref[...] = jnp.concatenate([b[:, v_rest:b0], jnp.zeros((tr, D_F - N_HEADS), b.dtype)], axis=1)
        wb_ref[...] = jnp.concatenate([b[:, b0:n], s2[...], s3[...]], axis=1)

    shard = lambda j: pl.BlockSpec((None, None, tr, n), lambda i: (l, j, i, 0))
    return pl.pallas_call(
        body,
        grid=(D_MODEL // tr,),
        in_specs=[shard(0), shard(1), shard(2), shard(3)],
        out_specs=[_rows(tr, D_QKV), _rows(tr, D_F), _rows(tr, D_B)],
        out_shape=[jax.ShapeDtypeStruct((D_MODEL, D_QKV), win.dtype), jax.ShapeDtypeStruct((D_MODEL, D_F), win.dtype),
                   jax.ShapeDtypeStruct((D_MODEL, D_B), win.dtype)],
        compiler_params=_params("parallel"),
        name="w_in_prep",
    )(win, win, win, win)


def _into_layer(buf):
    return dict(in_spec=HBM_REF, out_shape=jax.ShapeDtypeStruct(buf.shape, buf.dtype), aliases={1: 0})


def w_in_unprep(dwa, dwf, dwb, buf, l):
    tr = 256
    n = D_IN // 4
    v_rest = D_QKV - n
    b1 = n - v_rest - N_HEADS
    place = _into_layer(buf)

    def body(l_ref, buf_ref, a_ref, f_ref, b_ref, o_ref):
        a = a_ref[...]
        b = b_ref[...]
        o_ref[0] = a[:, 0:n]
        o_ref[1] = jnp.concatenate([a[:, n:D_QKV], f_ref[:, 0:N_HEADS], b[:, 0:b1]], axis=1)
        o_ref[2] = b[:, b1:b1 + n]
        o_ref[3] = b[:, b1 + n:D_B]

    return pl.pallas_call(
        body,
        grid_spec=pltpu.PrefetchScalarGridSpec(
            num_scalar_prefetch=1,
            grid=(D_MODEL // tr,),
            in_specs=[place["in_spec"], _rows(tr, D_QKV), _rows(tr, D_F), _rows(tr, D_B)],
            out_specs=pl.BlockSpec((None, 4, tr, n), lambda i, l: (l[0], 0, i, 0)),
        ),
        out_shape=place["out_shape"],
        input_output_aliases=place["aliases"],
        compiler_params=_params("parallel"),
        name="w_in_unprep",
    )(_layer_index(l), buf, dwa, dwf, dwb)


def branch_w_prep(wao, wco, pw, cw, l):
    gd = D_POOL // N_GROUPS
    od = D_MODEL // N_GROUPS

    def body(wao_ref, wco_ref, pw_ref, cw_ref, ao_ref, co_ref, po_ref, co8_ref):
        a = jnp.concatenate([wao_ref[j] for j in range(4)], axis=1)
        gap = jnp.zeros((LANES - HEAD_DIM, D_MODEL), a.dtype)
        ao_ref[...] = jnp.concatenate(
            [blk for h in range(N_HEADS) for blk in (a[h * HEAD_DIM:(h + 1) * HEAD_DIM], gap)], axis=0)
        co_ref[...] = jnp.concatenate([wco_ref[j] for j in range(4)], axis=1)
        zero = jnp.zeros((gd, od), a.dtype)
        po_ref[...] = jnp.concatenate(
            [jnp.concatenate([jnp.concatenate([pw_ref[j, g] for j in range(4)], axis=1) if g2 == g else zero
                              for g2 in range(N_GROUPS)], axis=1) for g in range(N_GROUPS)], axis=0)
        co8_ref[...] = jnp.zeros_like(co8_ref)
        co8_ref[0:CONV_K, :] = jnp.concatenate([cw_ref[j] for j in range(4)], axis=1)

    sel = lambda *shape: pl.BlockSpec((None,) + shape, lambda i: (l,) + (0,) * len(shape))
    return pl.pallas_call(
        body,
        grid=(1,),
        in_specs=[sel(4, D_ATTN, D_MODEL // 4), sel(4, D_CONV, D_MODEL // 4), sel(4, N_GROUPS, gd, od // 4),
                  sel(4, CONV_K, D_CONV // 4)],
        out_specs=[pl.BlockSpec((D_HEADS, D_MODEL), lambda i: (0, 0)), pl.BlockSpec((D_CONV, D_MODEL), lambda i: (0, 0)),
                   pl.BlockSpec((D_POOL, D_MODEL), lambda i: (0, 0)), pl.BlockSpec((8, D_CONV), lambda i: (0, 0))],
        out_shape=[jax.ShapeDtypeStruct((D_HEADS, D_MODEL), BF16), jax.ShapeDtypeStruct((D_CONV, D_MODEL), BF16),
                   jax.ShapeDtypeStruct((D_POOL, D_MODEL), BF16), jax.ShapeDtypeStruct((8, D_CONV), F32)],
        name="branch_w_prep",
    )(wao, wco, pw, cw)


def branch_g_place(dwao, dwco, dwpool, dwo, bufs, l):
    gd = D_POOL // N_GROUPS
    od = D_MODEL // N_GROUPS
    q = D_MODEL // 4

    def body(l_ref, b0, b1, b2, b3, a_ref, c_ref, p_ref, w_ref, ao_ref, co_ref, po_ref, wo_ref):
        a = jnp.concatenate([a_ref[h * LANES:h * LANES + HEAD_DIM, :] for h in range(N_HEADS)], axis=0)
        c = c_ref[...]
        p = p_ref[...]
        for j in range(4):
            ao_ref[j] = a[:, j * q:(j + 1) * q]
            co_ref[j] = c[:, j * q:(j + 1) * q]
            wo_ref[j] = w_ref[j * q:(j + 1) * q, :]
            for g in range(N_GROUPS):
                c0 = g * od + j * (od // 4)
                po_ref[j, g] = p[g * gd:(g + 1) * gd, c0:c0 + od // 4]

    whole = lambda a: pl.BlockSpec(a.shape, lambda i, l: (0,) * a.ndim)
    layer = lambda b: pl.BlockSpec((None,) + b.shape[1:], lambda i, l: (l[0],) + (0,) * (b.ndim - 1))
    return pl.pallas_call(
        body,
        grid_spec=pltpu.PrefetchScalarGridSpec(
            num_scalar_prefetch=1,
            grid=(1,),
            in_specs=[HBM_REF] * 4 + [whole(dwao), whole(dwco), whole(dwpool), whole(dwo)],
            out_specs=[layer(b) for b in bufs],
        ),
        out_shape=[jax.ShapeDtypeStruct(b.shape, b.dtype) for b in bufs],
        input_output_aliases={1: 0, 2: 1, 3: 2, 4: 3},
        compiler_params=_params("arbitrary"),
        name="branch_g_place",
    )(_layer_index(l), *bufs, dwao, dwco, dwpool, dwo)


def in_proj_fwd(x, g, wa, wf, wb):
    S = x.shape[0]

    def body(x_ref, g_ref, wa_ref, wf_ref, wb_ref, h_ref, pa_ref, pf_ref, pb_ref):
        xf = x_ref[...]
        r = lax.rsqrt(jnp.mean(xf * xf, axis=-1, keepdims=True) + EPS)
        h = (xf * r * g_ref[...]).astype(BF16)
        h_ref[...] = h
        pa_ref[...] = _dot(h, wa_ref[...]).astype(BF16)
        pf_ref[...] = _dot(h, wf_ref[...])
        pb_ref[...] = _dot(h, wb_ref[...]).astype(BF16)

    return pl.pallas_call(
        body,
        grid=(S // TM,),
        in_specs=[_rows(TM, D_MODEL), _whole(g), _whole(wa), _whole(wf), _whole(wb)],
        out_specs=[_rows(TM, D_MODEL), _rows(TM, D_QKV), _rows(TM, D_F), _rows(TM, D_B)],
        out_shape=[
            jax.ShapeDtypeStruct((S, D_MODEL), BF16),
            jax.ShapeDtypeStruct((S, D_QKV), BF16),
            jax.ShapeDtypeStruct((S, D_F), F32),
            jax.ShapeDtypeStruct((S, D_B), BF16),
        ],
        compiler_params=_params("parallel"),
        name="in_proj_fwd",
    )(x, g, wa, wf, wb)


def in_proj_bwd(x, g, dxm, dpa, dpf, dpb, wa, wf, wb):
    S = x.shape[0]

    def body(x_ref, g_ref, dxm_ref, dpa_ref, dpf_ref, dpb_ref, wa_ref, wf_ref, wb_ref, dx_ref, dg_ref):
        @pl.when(pl.program_id(0) == 0)
        def _():
            dg_ref[...] = jnp.zeros_like(dg_ref)

        dh = _dot_nt(dpa_ref[...], wa_ref[...]) + _dot_nt(dpf_ref[...], wf_ref[...]) + _dot_nt(dpb_ref[...], wb_ref[...])
        xf = x_ref[...]
        r = lax.rsqrt(jnp.mean(xf * xf, axis=-1, keepdims=True) + EPS)
        xhat = xf * r
        dg_ref[...] += jnp.sum(dh * xhat, axis=0, keepdims=True)
        gdh = dh * g_ref[...]
        dx_ref[...] = dxm_ref[...] + r * (gdh - xhat * jnp.mean(xhat * gdh, axis=-1, keepdims=True))

    return pl.pallas_call(
        body,
        grid=(S // TM,),
        in_specs=[_rows(TM, D_MODEL), _whole(g), _rows(TM, D_MODEL), _rows(TM, D_QKV), _rows(TM, D_F), _rows(TM, D_B),
                  _whole(wa), _whole(wf), _whole(wb)],
        out_specs=[_rows(TM, D_MODEL), pl.BlockSpec((1, D_MODEL), lambda i: (0, 0))],
        out_shape=[jax.ShapeDtypeStruct((S, D_MODEL), F32), jax.ShapeDtypeStruct((1, D_MODEL), F32)],
        compiler_params=_params("arbitrary"),
        name="in_proj_bwd",
    )(x, g, dxm, dpa, dpf, dpb, wa, wf, wb)


TS_WGRAD = 1024


def _wgrad_columns(n):
    return next(tn for tn in (1024, 768, 512, 256, 128) if n % tn == 0)


def wgrad(xa, dy, name):
    S, K = xa.shape
    N = dy.shape[1]
    ts = min(TS_WGRAD, S)
    tn = _wgrad_columns(N)

    def body(x_ref, dy_ref, o_ref):
        @pl.when(pl.program_id(1) == 0)
        def _():
            o_ref[...] = jnp.zeros_like(o_ref)

        o_ref[...] += _dot_tn(x_ref[...], dy_ref[...])

    return pl.pallas_call(
        body,
        grid=(N // tn, S // ts),
        in_specs=[pl.BlockSpec((ts, K), lambda j, k: (k, 0)), pl.BlockSpec((ts, tn), lambda j, k: (k, j))],
        out_specs=pl.BlockSpec((K, tn), lambda j, k: (0, j)),
        out_shape=jax.ShapeDtypeStruct((K, N), F32),
        compiler_params=_params("parallel", "arbitrary"),
        name=name,
    )(xa, dy)


def wgrad_into(xa, dy, name, buf, l):
    S, K = xa.shape
    N = dy.shape[1]
    ts = min(TS_WGRAD, S)
    split = buf.ndim == 4
    tn = buf.shape[-1] if split else _wgrad_columns(N)
    place = _into_layer(buf)

    def body(l_ref, buf_ref, x_ref, dy_ref, o_ref):
        @pl.when(pl.program_id(1) == 0)
        def _():
            o_ref[...] = jnp.zeros_like(o_ref)

        o_ref[...] += _dot_tn(x_ref[...], dy_ref[...])

    if split:
        out_spec = pl.BlockSpec((None, None, K, tn), lambda j, k, l: (l[0], j, 0, 0))
    else:
        out_spec = pl.BlockSpec((None, K, tn), lambda j, k, l: (l[0], 0, j))
    return pl.pallas_call(
        body,
        grid_spec=pltpu.PrefetchScalarGridSpec(
            num_scalar_prefetch=1,
            grid=(N // tn, S // ts),
            in_specs=[place["in_spec"], pl.BlockSpec((ts, K), lambda j, k, l: (k, 0)),
                      pl.BlockSpec((ts, tn), lambda j, k, l: (k, j))],
            out_specs=out_spec,
        ),
        out_shape=place["out_shape"],
        input_output_aliases=place["aliases"],
        compiler_params=_params("parallel", "arbitrary"),
        name=name,
    )(_layer_index(l), buf, xa, dy)


def _head_mean_matrix():
    h = np.arange(D_ATTN) // HEAD_DIM
    return jnp.asarray((h[:, None] == h[None, :]).astype(np.float32) / HEAD_DIM, BF16)


def _place_matrix(lane0):
    m = np.zeros((N_PIECES * LANES, D_HEADS), np.float32)
    for i in range(N_PIECES):
        for h in range(N_HEADS):
            m[i * LANES + h, h * LANES + lane0 + i] = 1.0
    return jnp.asarray(m, BF16)


def _tri(n, upper):
    r = np.arange(n)
    m = (r[None, :] >= r[:, None]) if upper else (r[None, :] <= r[:, None])
    return jnp.asarray(m.astype(np.float32), BF16)


def _lanes_in(lane, lo, n):
    return (lane >= lo) & (lane < lo + n)


def qk_prep(pa, pf, gq, gk, fb):
    S = pa.shape[0]
    bd = _head_mean_matrix()
    tri = _tri(TM, upper=False)
    place_q = _place_matrix(LANE_C)
    place_k = _place_matrix(LANE_ONE)

    def body(q_ref, k_ref, v_ref, pf_ref, gq_ref, gk_ref, fb_ref, bd_ref, tri_ref, pq_ref, pk_ref,
             qx_ref, kx_ref, vx_ref, carry):
        @pl.when(pl.program_id(0) == 0)
        def _():
            carry[...] = jnp.zeros_like(carry)

        def head_norm(x_ref, g_ref, scale):
            xf = x_ref[...].astype(F32)
            ms = _dot((xf * xf).astype(BF16), bd_ref[...])
            return xf * lax.rsqrt(ms + EPS) * g_ref[...] * scale

        qh = head_norm(q_ref, gq_ref, HEAD_DIM ** -0.5 * LOG2E)
        kh = head_norm(k_ref, gk_ref, 1.0)
        vf = v_ref[...].astype(F32)

        z = pf_ref[...] + fb_ref[...]
        logf = jnp.minimum(z, 0.0) - jnp.log(1.0 + jnp.exp(-jnp.abs(z)))
        hi, lo = _split_bf16(logf)
        c = _dot(tri_ref[...], hi) + _dot(tri_ref[...], lo) + carry[...]
        carry[...] += jnp.sum(hi.astype(F32) + lo.astype(F32), axis=0, keepdims=True)
        pieces = jnp.concatenate(_pieces(c * LOG2E), axis=1)
        cq = _dot(pieces, pq_ref[...])
        ck = _dot(pieces, pk_ref[...])

        lane = lax.broadcasted_iota(jnp.int32, (TM, LANES), 1)
        low = lane < HEAD_DIM
        ones_q = _lanes_in(lane, LANE_ONE, N_PIECES).astype(F32)
        ones_k = (_lanes_in(lane, LANE_C, N_PIECES) | _lanes_in(lane, LANE_LSE, N_PIECES)).astype(F32)
        ones_v = _lanes_in(lane, LANE_C, N_PIECES + 1).astype(F32)
        for h in range(N_HEADS):
            blk = slice((h // 2) * LANES, (h // 2 + 1) * LANES)
            head = (lambda a: pltpu.roll(a[:, blk], HEAD_DIM, 1)) if h % 2 else (lambda a: a[:, blk])
            mine = slice(h * LANES, (h + 1) * LANES)
            qx_ref[h] = jnp.where(low, head(qh), cq[:, mine] + ones_q).astype(BF16)
            kx_ref[h] = jnp.where(low, head(kh), ones_k - ck[:, mine]).astype(BF16)
            vx_ref[h] = jnp.where(low, head(vf), ones_v).astype(BF16)

    heads = pl.BlockSpec((N_HEADS, TM, LANES), lambda i: (0, i, 0))
    out = jax.ShapeDtypeStruct((N_HEADS, S, LANES), BF16)
    return pl.pallas_call(
        body,
        grid=(S // TM,),
        in_specs=[pl.BlockSpec((TM, D_ATTN), lambda i: (i, 0)), pl.BlockSpec((TM, D_ATTN), lambda i: (i, 1)),
                  pl.BlockSpec((TM, D_ATTN), lambda i: (i, 2)),
                  _rows(TM, D_F), _whole(gq), _whole(gk), _whole(fb), _whole(bd), _whole(tri), _whole(place_q), _whole(place_k)],
        out_specs=[heads, heads, heads],
        out_shape=[out, out, out],
        scratch_shapes=[pltpu.VMEM((1, D_F), F32)],
        compiler_params=_params("arbitrary"),
        name="qk_prep",
    )(pa, pa, pa, pf, gq, gk, fb, bd, tri, place_q, place_k)


def attn_fwd(qx, kx, vx):
    S = qx.shape[1]
    nq = S // TQ

    def body(q_ref, k_ref, v_ref, o_ref, qb_ref):
        i = pl.program_id(1)
        lane = lax.broadcasted_iota(jnp.int32, (TQ, LANES), 1)
        row = lax.broadcasted_iota(jnp.int32, (TQ, TK), 0)
        col = lax.broadcasted_iota(jnp.int32, (TQ, TK), 1)
        q = [q_ref[0], q_ref[1]]
        n_full = (i * TQ) // TK

        def step(kt, carry, masked):
            ks = pl.multiple_of(kt * TK, TK)
            new = []
            for j in range(2):
                m, acc = carry[j]
                z = _dot_nt(q[j], k_ref[j, pl.ds(ks, TK), :])
                if masked:
                    z = jnp.where(row + i * TQ >= col + ks, z, NEG)
                m_new = jnp.maximum(m, jnp.max(z, axis=1, keepdims=True))
                pr = jnp.exp2(z - m_new)
                acc = jnp.exp2(m - m_new) * acc + _dot(pr.astype(BF16), v_ref[j, pl.ds(ks, TK), :])
                new.append((m_new, acc))
            return tuple(new)

        init = tuple((jnp.full((TQ, 1), NEG, F32), jnp.zeros((TQ, LANES), F32)) for _ in range(2))
        carry = lax.fori_loop(0, n_full, functools.partial(step, masked=False), init)
        carry = step(n_full, carry, True)
        for j in range(2):
            m, acc = carry[j]
            l = jnp.sum(jnp.where(lane == LANE_C, acc, 0.0), axis=1, keepdims=True)
            o_ref[j] = acc / l
            n1, n2, n3 = _pieces(-(m + jnp.log(l) * LOG2E))
            qb_ref[j] = jnp.where(lane == LANE_LSE, n1,
                                  jnp.where(lane == LANE_LSE + 1, n2, jnp.where(lane == LANE_LSE + 2, n3, q[j])))

    pair_tile = pl.BlockSpec((2, TQ, LANES), lambda p, i: (p, i, 0))
    pair_all = pl.BlockSpec((2, S, LANES), lambda p, i: (p, 0, 0))
    return pl.pallas_call(
        body,
        grid=(N_HEADS // 2, nq),
        in_specs=[pair_tile, pair_all, pair_all],
        out_specs=[pair_tile, pair_tile],
        out_shape=[jax.ShapeDtypeStruct((N_HEADS, S, LANES), F32), jax.ShapeDtypeStruct((N_HEADS, S, LANES), BF16)],
        compiler_params=_params("parallel", "parallel"),
        name="attn_fwd",
    )(qx, kx, vx)


def attn_bwd(qxb, kx, vx, dox):
    S = qxb.shape[1]
    nq = S // QT

    def body(q_ref, k_ref, v_ref, do_ref, dq_ref, dk_ref, dv_ref):
        kt = pl.program_id(1)

        @pl.when(kt == 0)
        def _():
            dq_ref[...] = jnp.zeros_like(dq_ref)

        row = lax.broadcasted_iota(jnp.int32, (QT, KB), 0)
        col = lax.broadcasted_iota(jnp.int32, (QT, KB), 1)
        k = [k_ref[0], k_ref[1]]
        v = [v_ref[0], v_ref[1]]
        q_first = (kt * KB) // QT

        def step(qi, carry, masked):
            qs = pl.multiple_of(qi * QT, QT)
            new = []
            for j in range(2):
                dk, dv = carry[j]
                q = q_ref[j, pl.ds(qs, QT), :]
                dout = do_ref[j, pl.ds(qs, QT), :]
                z = _dot_nt(q, k[j])
                if masked:
                    z = jnp.where(row + qs >= col + kt * KB, z, NEG)
                pr = jnp.exp2(z)
                dv = dv + _dot_tn(pr.astype(BF16), dout)
                dsb = (pr * _dot_nt(dout, v[j])).astype(BF16)
                dk = dk + _dot_tn(dsb, q)
                dq_ref[j, pl.ds(qs, QT), :] += _dot(dsb, k[j])
                new.append((dk, dv))
            return tuple(new)

        zero = jnp.zeros((KB, LANES), F32)
        carry = step(q_first, ((zero, zero), (zero, zero)), True)
        carry = lax.fori_loop(q_first + 1, nq, functools.partial(step, masked=False), carry)
        for j in range(2):
            dk_ref[j] = carry[j][0]
            dv_ref[j] = carry[j][1]

    pair_tile = pl.BlockSpec((2, KB, LANES), lambda p, kt: (p, kt, 0))
    pair_all = pl.BlockSpec((2, S, LANES), lambda p, kt: (p, 0, 0))
    out = jax.ShapeDtypeStruct((N_HEADS, S, LANES), F32)
    return pl.pallas_call(
        body,
        grid=(N_HEADS // 2, S // KB),
        in_specs=[pair_all, pair_tile, pair_tile, pair_all],
        out_specs=[pair_all, pair_tile, pair_tile],
        out_shape=[out, out, out],
        compiler_params=_params("arbitrary", "arbitrary"),
        name="attn_bwd",
    )(qxb, kx, vx, dox)


def attn_bwd_post(pa, pf, dqx, dkx, dvx, gq, gk, fb):
    S = pa.shape[0]
    nt = S // TM
    bd = _head_mean_matrix()
    triu = _tri(TM, upper=True)
    rev = lambda i: nt - 1 - i

    def body(q_ref, k_ref, pf_ref, dqx_ref, dkx_ref, dvx_ref, gq_ref, gk_ref, fb_ref, bd_ref, triu_ref,
             dpa_ref, dpf_ref, dgq_ref, dgk_ref, dfb_ref, carry):
        @pl.when(pl.program_id(0) == 0)
        def _():
            carry[...] = jnp.zeros_like(carry)
            dgq_ref[...] = jnp.zeros_like(dgq_ref)
            dgk_ref[...] = jnp.zeros_like(dgk_ref)
            dfb_ref[...] = jnp.zeros_like(dfb_ref)

        lane = lax.broadcasted_iota(jnp.int32, (TM, LANES), 1)

        def columns(ref):
            return jnp.concatenate([jnp.where(lane < HEAD_DIM, ref[2 * p], pltpu.roll(ref[2 * p + 1], HEAD_DIM, 1))
                                    for p in range(N_HEADS // 2)], axis=1)

        def head_norm_bwd(x_ref, dy, g_ref, dg_ref):
            xf = x_ref[...].astype(F32)
            r = lax.rsqrt(_dot((xf * xf).astype(BF16), bd_ref[...]) + EPS)
            xhat = xf * r
            dg_ref[...] += jnp.sum(dy * xhat, axis=0, keepdims=True)
            gdy = dy * g_ref[...]
            return (r * (gdy - xhat * _dot((xhat * gdy).astype(BF16), bd_ref[...]))).astype(BF16)

        dpa_ref[:, 0:D_ATTN] = head_norm_bwd(q_ref, columns(dqx_ref) * HEAD_DIM ** -0.5, gq_ref, dgq_ref)
        dpa_ref[:, D_ATTN:2 * D_ATTN] = head_norm_bwd(k_ref, columns(dkx_ref) * LN2, gk_ref, dgk_ref)
        dpa_ref[:, 2 * D_ATTN:3 * D_ATTN] = columns(dvx_ref).astype(BF16)

        dc = jnp.zeros((TM, LANES), F32)
        for h in range(N_HEADS):
            both = jnp.where(lane == LANE_C, dqx_ref[h], 0.0) - jnp.where(lane == LANE_ONE, dkx_ref[h], 0.0)
            dc = jnp.where(lane == h, jnp.sum(both, axis=1, keepdims=True), dc)
        hi, lo = _split_bf16(dc)
        dlogf = _dot(triu_ref[...], hi) + _dot(triu_ref[...], lo) + carry[...]
        first = lax.broadcasted_iota(jnp.int32, (TM, D_F), 0) == 0
        carry[...] = jnp.sum(jnp.where(first, dlogf, 0.0), axis=0, keepdims=True)
        df = dlogf * _sigmoid(-(pf_ref[...] + fb_ref[...]))
        dfb_ref[...] += jnp.sum(df, axis=0, keepdims=True)
        dpf_ref[...] = df.astype(BF16)

    heads = pl.BlockSpec((N_HEADS, TM, LANES), lambda i: (0, rev(i), 0))
    return pl.pallas_call(
        body,
        grid=(nt,),
        in_specs=[pl.BlockSpec((TM, D_ATTN), lambda i: (rev(i), 0)), pl.BlockSpec((TM, D_ATTN), lambda i: (rev(i), 1)),
                  pl.BlockSpec((TM, D_F), lambda i: (rev(i), 0)), heads, heads, heads,
                  _whole(gq), _whole(gk), _whole(fb), _whole(bd), _whole(triu)],
        out_specs=[pl.BlockSpec((TM, D_QKV), lambda i: (rev(i), 0)), pl.BlockSpec((TM, D_F), lambda i: (rev(i), 0)),
                   pl.BlockSpec((1, D_ATTN), lambda i: (0, 0)), pl.BlockSpec((1, D_ATTN), lambda i: (0, 0)),
                   pl.BlockSpec((1, D_F), lambda i: (0, 0))],
        out_shape=[jax.ShapeDtypeStruct((S, D_QKV), BF16), jax.ShapeDtypeStruct((S, D_F), BF16),
                   jax.ShapeDtypeStruct((1, D_ATTN), F32), jax.ShapeDtypeStruct((1, D_ATTN), F32),
                   jax.ShapeDtypeStruct((1, D_F), F32)],
        scratch_shapes=[pltpu.VMEM((1, D_F), F32)],
        compiler_params=_params("arbitrary"),
        name="attn_bwd_post",
    )(pa, pa, pf, dqx, dkx, dvx, gq, gk, fb, bd, triu)


def _shift_down(ext, k):
    return pltpu.roll(ext, k, 0)[HALO:]


def _shift_up(ext, k, n):
    return pltpu.roll(ext, n + HALO - k, 0)[:n]


def _pool_lane_select(a2, a4, a8, a16, lane):
    return jnp.where(lane < 64, a2, jnp.where(lane < 128, a4, jnp.where(lane < 192, a8, a16)))


def _local_branches(o, pb, halo, have_prev, row0, wao, convw, wco, wpool, pscale):
    n = pb.shape[0]
    cx = pb[:, 0:D_CONV].astype(F32)
    cb = pb[:, D_CONV:2 * D_CONV].astype(F32)
    cc = pb[:, 2 * D_CONV:3 * D_CONV].astype(F32)
    px = pb[:, 3 * D_CONV:D_LOCAL].astype(F32)
    keep = have_prev.astype(F32)
    z = cc * cx
    z_ext = jnp.concatenate([halo[:, 2 * D_CONV:3 * D_CONV].astype(F32) * halo[:, 0:D_CONV].astype(F32) * keep, z], axis=0)
    z1 = _shift_down(z_ext, 1)
    z2 = _shift_down(z_ext, 2)
    conv = convw[0:1, :] * z2 + convw[1:2, :] * z1 + convw[2:3, :] * z
    cm = cb * conv

    u_ext = jnp.concatenate([halo[:, 3 * D_CONV:D_LOCAL].astype(F32) * keep, px], axis=0)
    s2 = u_ext + pltpu.roll(u_ext, 1, 0)
    s4 = s2 + pltpu.roll(s2, 2, 0)
    s8 = s4 + pltpu.roll(s4, 4, 0)
    s16 = s8 + pltpu.roll(s8, 8, 0)
    lane = lax.broadcasted_iota(jnp.int32, (n, D_POOL), 1)
    win = _pool_lane_select(2.0, 4.0, 8.0, 16.0, lane)
    t = (row0 + lax.broadcasted_iota(jnp.int32, (n, D_POOL), 0)).astype(F32)
    cnt = jnp.minimum(t + 1.0, win)
    feat = _pool_lane_select(s2[HALO:], s4[HALO:], s8[HALO:], s16[HALO:], lane) / cnt - px

    ya = _dot(o, wao)
    yc = _dot(cm.astype(BF16), wco)
    yp_pre = _dot(feat.astype(BF16), wpool)
    yp = yp_pre * pscale
    return dict(cx=cx, cb=cb, cc=cc, z=z, z1=z1, z2=z2, conv=conv, cm=cm, feat=feat, cnt=cnt, lane=lane,
                ya=ya, yc=yc, yp_pre=yp_pre, yp=yp)


def _halo_spec(tm, tile_of):
    per = tm // HALO
    return pl.BlockSpec((HALO, D_LOCAL), lambda i, *_: (jnp.maximum(tile_of(i) * per - 1, 0), 0))


def _heads_as_columns(ref):
    return jnp.concatenate([ref[h] for h in range(N_HEADS)], axis=1)


def mix_out_fwd(x, ox, pb, wao, convw, wco, wpool, pscale, wo_all, l):
    S = x.shape[0]
    tm = TM_MIX

    def body(l_ref, x_ref, o_ref, pb_ref, halo_ref, wao_ref, cw_ref, wco_ref, wp_ref, ps_ref, wo_ref, y_ref):
        i = pl.program_id(0)
        pb = pb_ref[...]
        o = _heads_as_columns(o_ref).astype(BF16)
        b = _local_branches(o, pb, halo_ref[...], i > 0, i * tm, wao_ref[...], cw_ref[...], wco_ref[...],
                            wp_ref[...], ps_ref[...])
        g0 = _sigmoid(pb[:, D_LOCAL:D_LOCAL + D_MODEL].astype(F32))
        g1 = _sigmoid(pb[:, D_LOCAL + D_MODEL:D_LOCAL + 2 * D_MODEL].astype(F32))
        g2 = _sigmoid(pb[:, D_LOCAL + 2 * D_MODEL:D_B].astype(F32))
        merged = g0 * b["ya"] + g1 * b["yc"] + g2 * b["yp"]
        y_ref[...] = x_ref[...] + _dot(merged.astype(BF16), wo_ref[...])

    return pl.pallas_call(
        body,
        grid_spec=pltpu.PrefetchScalarGridSpec(
            num_scalar_prefetch=1,
            grid=(S // tm,),
            in_specs=[_rows(tm, D_MODEL), pl.BlockSpec((N_HEADS, tm, LANES), lambda i, l: (0, i, 0)), _rows(tm, D_B),
                      _halo_spec(tm, lambda i: i), _whole(wao), _whole(convw), _whole(wco), _whole(wpool), _whole(pscale),
                      _layer((D_MODEL, D_MODEL))],
            out_specs=_rows(tm, D_MODEL),
        ),
        out_shape=jax.ShapeDtypeStruct((S, D_MODEL), F32),
        compiler_params=_params("parallel"),
        name="mix_out_fwd",
    )(_layer_index(l), x, ox, pb, pb, wao, convw, wco, wpool, pscale, wo_all)


def mix_out_bwd(dxm, ox, pb, wao, convw, wco, wpool, pscale, wo_all, l):
    S = dxm.shape[0]
    tm = TM_MIX
    nt = S // tm
    rev = lambda i: nt - 1 - i
    rows = lambda n: pl.BlockSpec((tm, n), lambda i, l: (rev(i), 0))
    heads = pl.BlockSpec((N_HEADS, tm, LANES), lambda i, l: (0, rev(i), 0))
    acc = lambda r, c: pl.BlockSpec((r, c), lambda i, l: (0, 0))

    def body(l_ref, dxm_ref, o_ref, pb_ref, halo_ref, wao_ref, cw_ref, wco_ref, wp_ref, ps_ref, wo_ref,
             dpb_ref, do_ref, dwo_ref, dwao_ref, dwco_ref, dwp_ref, dcw_ref, dps_ref, next_dconv, next_e):
        i = pl.program_id(0)
        r = rev(i)

        @pl.when(i == 0)
        def _():
            for ref in (dwo_ref, dwao_ref, dwco_ref, dwp_ref, dcw_ref, dps_ref, next_dconv, next_e):
                ref[...] = jnp.zeros_like(ref)

        pb = pb_ref[...]
        o = _heads_as_columns(o_ref).astype(BF16)
        cw = cw_ref[...]
        b = _local_branches(o, pb, halo_ref[...], r > 0, r * tm, wao_ref[...], cw, wco_ref[...], wp_ref[...], ps_ref[...])
        g0 = _sigmoid(pb[:, D_LOCAL:D_LOCAL + D_MODEL].astype(F32))
        g1 = _sigmoid(pb[:, D_LOCAL + D_MODEL:D_LOCAL + 2 * D_MODEL].astype(F32))
        g2 = _sigmoid(pb[:, D_LOCAL + 2 * D_MODEL:D_B].astype(F32))
        dxb = dxm_ref[...].astype(BF16)
        merged = g0 * b["ya"] + g1 * b["yc"] + g2 * b["yp"]
        dwo_ref[...] += _dot_tn(merged.astype(BF16), dxb)
        dmer = _dot_nt(dxb, wo_ref[...])
        dpb_ref[:, D_LOCAL:D_LOCAL + D_MODEL] = (dmer * b["ya"] * (g0 * (1.0 - g0))).astype(BF16)
        dpb_ref[:, D_LOCAL + D_MODEL:D_LOCAL + 2 * D_MODEL] = (dmer * b["yc"] * (g1 * (1.0 - g1))).astype(BF16)
        dpb_ref[:, D_LOCAL + 2 * D_MODEL:D_B] = (dmer * b["yp"] * (g2 * (1.0 - g2))).astype(BF16)

        dya = (dmer * g0).astype(BF16)
        dwao_ref[...] += _dot_tn(o, dya)
        da = _dot_nt(dya, wao_ref[...]).astype(BF16)
        lane = lax.broadcasted_iota(jnp.int32, (tm, LANES), 1)
        for h in range(N_HEADS):
            dah = da[:, h * LANES:(h + 1) * LANES]
            d1, d2, d3 = _pieces(-jnp.sum(dah.astype(F32) * o_ref[h], axis=1, keepdims=True))
            do_ref[h] = jnp.where(lane == LANE_C + 1, d1, jnp.where(lane == LANE_C + 2, d2,
                                                                  jnp.where(lane == LANE_C + 3, d3, dah)))

        dyc = (dmer * g1).astype(BF16)
        dwco_ref[...] += _dot_tn(b["cm"].astype(BF16), dyc)
        dcm = _dot_nt(dyc, wco_ref[...])
        dconv = dcm * b["cb"]
        dcw_ref[0:1, :] += jnp.sum(dconv * b["z2"], axis=0, keepdims=True)
        dcw_ref[1:2, :] += jnp.sum(dconv * b["z1"], axis=0, keepdims=True)
        dcw_ref[2:3, :] += jnp.sum(dconv * b["z"], axis=0, keepdims=True)
        d_ext = jnp.concatenate([dconv, next_dconv[...]], axis=0)
        dz = cw[2:3, :] * dconv + cw[1:2, :] * _shift_up(d_ext, 1, tm) + cw[0:1, :] * _shift_up(d_ext, 2, tm)
        next_dconv[...] = dconv[0:HALO]
        dpb_ref[:, 0:D_CONV] = (dz * b["cc"]).astype(BF16)
        dpb_ref[:, D_CONV:2 * D_CONV] = (dcm * b["conv"]).astype(BF16)
        dpb_ref[:, 2 * D_CONV:3 * D_CONV] = (dz * b["cx"]).astype(BF16)

        dyp = dmer * g2
        dps_ref[...] += jnp.sum(dyp * b["yp_pre"], axis=0, keepdims=True)
        dyps = (dyp * ps_ref[...]).astype(BF16)
        dwp_ref[...] += _dot_tn(b["feat"].astype(BF16), dyps)
        dfeat = _dot_nt(dyps, wp_ref[...])
        e = dfeat / b["cnt"]
        e_ext = jnp.concatenate([e, next_e[...]], axis=0)
        up = lambda a, k: pltpu.roll(a, tm + HALO - k, 0)
        f2 = e_ext + up(e_ext, 1)
        f4 = f2 + up(f2, 2)
        f8 = f4 + up(f4, 4)
        f16 = f8 + up(f8, 8)
        next_e[...] = e[0:HALO]
        dpb_ref[:, 3 * D_CONV:D_LOCAL] = (_pool_lane_select(f2[:tm], f4[:tm], f8[:tm], f16[:tm], b["lane"]) - dfeat).astype(BF16)

    return pl.pallas_call(
        body,
        grid_spec=pltpu.PrefetchScalarGridSpec(
            num_scalar_prefetch=1,
            grid=(nt,),
            in_specs=[rows(D_MODEL), heads, rows(D_B), _halo_spec(tm, rev),
                      _whole(wao), _whole(convw), _whole(wco), _whole(wpool), _whole(pscale), _layer((D_MODEL, D_MODEL))],
            out_specs=[rows(D_B), heads, acc(D_MODEL, D_MODEL), acc(D_HEADS, D_MODEL), acc(D_CONV, D_MODEL),
                       acc(D_POOL, D_MODEL), acc(8, D_CONV), acc(1, D_MODEL)],
            scratch_shapes=[pltpu.VMEM((HALO, D_CONV), F32), pltpu.VMEM((HALO, D_POOL), F32)],
        ),
        out_shape=[jax.ShapeDtypeStruct((S, D_B), BF16), jax.ShapeDtypeStruct((N_HEADS, S, LANES), BF16),
                   jax.ShapeDtypeStruct((D_MODEL, D_MODEL), F32), jax.ShapeDtypeStruct((D_HEADS, D_MODEL), F32),
                   jax.ShapeDtypeStruct((D_CONV, D_MODEL), F32), jax.ShapeDtypeStruct((D_POOL, D_MODEL), F32),
                   jax.ShapeDtypeStruct((8, D_CONV), F32), jax.ShapeDtypeStruct((1, D_MODEL), F32)],
        compiler_params=_params("arbitrary"),
        name="mix_out_bwd",
    )(_layer_index(l), dxm, ox, pb, pb, wao, convw, wco, wpool, pscale, wo_all)


FF_SHARD = 2 * D_FF // 4


def ffn_fwd(x, g, w1_all, w2_all, l):
    S = x.shape[0]

    def body(l_ref, x_ref, g_ref, w1_ref, w2_ref, y_ref, u_ref):
        xf = x_ref[...]
        r = lax.rsqrt(jnp.mean(xf * xf, axis=-1, keepdims=True) + EPS)
        h = (xf * r * g_ref[...]).astype(BF16)
        u = jnp.concatenate([_dot(h, w1_ref[j]) for j in range(4)], axis=1)
        u_ref[...] = u.astype(BF16)
        gt = u[:, 0:D_FF]
        act = gt * _sigmoid(gt) * u[:, D_FF:2 * D_FF]
        y_ref[...] = xf + _dot(act.astype(BF16), w2_ref[...])

    return pl.pallas_call(
        body,
        grid_spec=pltpu.PrefetchScalarGridSpec(
            num_scalar_prefetch=1,
            grid=(S // TM,),
            in_specs=[_rows(TM, D_MODEL), _whole(g), _layer((4, D_MODEL, FF_SHARD)), _layer((D_FF, D_MODEL))],
            out_specs=[_rows(TM, D_MODEL), _rows(TM, 2 * D_FF)],
        ),
        out_shape=[jax.ShapeDtypeStruct((S, D_MODEL), F32), jax.ShapeDtypeStruct((S, 2 * D_FF), BF16)],
        compiler_params=_params("parallel"),
        name="ffn_fwd",
    )(_layer_index(l), x, g, w1_all, w2_all)


def ffn_bwd(x, g, dy, u, w1_all, w2_all, l):
    S = x.shape[0]

    def body(l_ref, x_ref, g_ref, dy_ref, u_ref, w1_ref, w2_ref, dx_ref, du_ref, act_ref, h_ref, dyb_ref, dg_ref):
        @pl.when(pl.program_id(0) == 0)
        def _():
            dg_ref[...] = jnp.zeros_like(dg_ref)

        dyf = dy_ref[...]
        dyb_ref[...] = dyf.astype(BF16)
        dact = _dot_nt(dyb_ref[...], w2_ref[...])
        gt = u_ref[:, 0:D_FF].astype(F32)
        up = u_ref[:, D_FF:2 * D_FF].astype(F32)
        sg = _sigmoid(gt)
        silu = gt * sg
        act_ref[...] = (silu * up).astype(BF16)
        du_ref[:, 0:D_FF] = (dact * up * (sg * (1.0 + gt * (1.0 - sg)))).astype(BF16)
        du_ref[:, D_FF:2 * D_FF] = (dact * silu).astype(BF16)
        dh = _dot_nt(du_ref[:, 0:FF_SHARD], w1_ref[0])
        for j in range(1, 4):
            dh = dh + _dot_nt(du_ref[:, j * FF_SHARD:(j + 1) * FF_SHARD], w1_ref[j])
        xf = x_ref[...]
        r = lax.rsqrt(jnp.mean(xf * xf, axis=-1, keepdims=True) + EPS)
        xhat = xf * r
        h_ref[...] = (xhat * g_ref[...]).astype(BF16)
        dg_ref[...] += jnp.sum(dh * xhat, axis=0, keepdims=True)
        gdh = dh * g_ref[...]
        dx_ref[...] = dyf + r * (gdh - xhat * jnp.mean(xhat * gdh, axis=-1, keepdims=True))

    return pl.pallas_call(
        body,
        grid_spec=pltpu.PrefetchScalarGridSpec(
            num_scalar_prefetch=1,
            grid=(S // TM,),
            in_specs=[_rows(TM, D_MODEL), _whole(g), _rows(TM, D_MODEL), _rows(TM, 2 * D_FF),
                      _layer((4, D_MODEL, FF_SHARD)), _layer((D_FF, D_MODEL))],
            out_specs=[_rows(TM, D_MODEL), _rows(TM, 2 * D_FF), _rows(TM, D_FF), _rows(TM, D_MODEL), _rows(TM, D_MODEL),
                       pl.BlockSpec((1, D_MODEL), lambda i, l: (0, 0))],
        ),
        out_shape=[jax.ShapeDtypeStruct((S, D_MODEL), F32), jax.ShapeDtypeStruct((S, 2 * D_FF), BF16),
                   jax.ShapeDtypeStruct((S, D_FF), BF16), jax.ShapeDtypeStruct((S, D_MODEL), BF16),
                   jax.ShapeDtypeStruct((S, D_MODEL), BF16), jax.ShapeDtypeStruct((1, D_MODEL), F32)],
        compiler_params=_params("arbitrary"),
        name="ffn_bwd",
    )(_layer_index(l), x, g, dy, u, w1_all, w2_all)


def loss_head(y, target):
    S = y.shape[0]

    def body(y_ref, t_ref, loss_ref, dy_ref):
        @pl.when(pl.program_id(0) == 0)
        def _():
            loss_ref[0, 0] = 0.0

        err = y_ref[...] - t_ref[...]
        dy_ref[...] = err * (1.0 / D_MODEL)
        loss_ref[0, 0] += 0.5 * jnp.sum(jnp.mean(err * err, axis=-1))

    return pl.pallas_call(
        body,
        grid=(S // TM,),
        in_specs=[_rows(TM, D_MODEL), _rows(TM, D_MODEL)],
        out_specs=[pl.BlockSpec((1, 1), lambda i: (0, 0), memory_space=pltpu.SMEM), _rows(TM, D_MODEL)],
        out_shape=[jax.ShapeDtypeStruct((1, 1), F32), jax.ShapeDtypeStruct((S, D_MODEL), F32)],
        compiler_params=_params("arbitrary"),
        name="loss_head",
    )(y, target)


SPLIT = {
    "w_in": ((D_MODEL, D_IN), 1),
    "w_attn_out": ((D_ATTN, D_MODEL), 1),
    "w_conv_out": ((D_CONV, D_MODEL), 1),
    "pool_w": ((N_GROUPS, D_POOL // N_GROUPS, D_MODEL // N_GROUPS), 2),
    "w_o": ((D_MODEL, D_MODEL), 0),
    "w_ffn_in": ((D_MODEL, 2 * D_FF), 1),
    "w_ffn_out": ((D_FF, D_MODEL), 0),
}
SMALL = {"norm_mix_g": D_MODEL, "forget_b": N_HEADS, "q_norm_g": HEAD_DIM, "k_norm_g": HEAD_DIM,
         "pool_scale": D_MODEL, "norm_ffn_g": D_MODEL, "conv_w": CONV_K * D_CONV}
WEIGHTS = ["norm_mix_g", "w_in", "forget_b", "q_norm_g", "k_norm_g", "w_attn_out", "conv_w", "w_conv_out", "pool_w",
           "pool_scale", "w_o", "norm_ffn_g", "w_ffn_in", "w_ffn_out"]


MIX_W = ["w_in", "w_attn_out", "w_conv_out", "pool_w", "w_o"]
FFN_W = ["w_ffn_in", "w_ffn_out"]


def _row(a):
    return a.astype(F32).reshape(1, -1)


def mix_fwd(x, full, li, small):
    n = full["w_o"].shape[0]
    wa, wf, wb = w_in_prep(full["w_in"], li)
    wao, wco, wpool, convw = branch_w_prep(full["w_attn_out"], full["w_conv_out"], full["pool_w"], full["conv_w"], li)
    w = dict(wa=wa, wf=wf, wb=wb, wao=wao, wco=wco, wpool=wpool, convw=convw,
             w_o=full["w_o"].reshape(n, D_MODEL, D_MODEL), at=li,
             g1=_row(small["norm_mix_g"]), pscale=_row(small["pool_scale"]),
             gq=_row(jnp.tile(small["q_norm_g"], N_HEADS)), gk=_row(jnp.tile(small["k_norm_g"], N_HEADS)),
             fb=_row(jnp.pad(small["forget_b"], (0, D_F - N_HEADS))))
    h, pa, pf, pb = in_proj_fwd(x, w["g1"], wa, wf, wb)
    qx, kx, vx = qk_prep(pa, pf, w["gq"], w["gk"], w["fb"])
    ox, qxb = attn_fwd(qx, kx, vx)
    xm = mix_out_fwd(x, ox, pb, wao, convw, wco, wpool, w["pscale"], w["w_o"], li)
    return xm, w, dict(x=x, h=h, pa=pa, pf=pf, pb=pb, kx=kx, vx=vx, ox=ox, qxb=qxb)


def ffn_half_fwd(xm, full, li, g2):
    n = full["w_ffn_out"].shape[0]
    w = dict(w1=full["w_ffn_in"], w2=full["w_ffn_out"].reshape(n, D_FF, D_MODEL), at=li, g2=_row(g2))
    y, u = ffn_fwd(xm, w["g2"], w["w1"], w["w2"], li)
    return y, w, dict(xm=xm, u=u)


def ffn_half_bwd(dx, w, s, gi, big):
    n = big["w_ffn_out"].shape[0]
    big = dict(big)
    dxm, du, act, h2, dyb, dg2 = ffn_bwd(s["xm"], w["g2"], dx, s["u"], w["w1"], w["w2"], w["at"])
    big["w_ffn_out"] = wgrad_into(act, dyb, "wgrad_ffn_out", big["w_ffn_out"].reshape(n, D_FF, D_MODEL),
                                  gi).reshape(big["w_ffn_out"].shape)
    big["w_ffn_in"] = wgrad_into(h2, du, "wgrad_ffn_in", big["w_ffn_in"], gi)
    return dxm, big, dict(norm_ffn_g=dg2[0])


def mix_bwd(dxm, w, s, gi, big):
    big = dict(big)
    dpb, dox, dwo, dwao, dwco, dwpool, dconvw, dpscale = mix_out_bwd(
        dxm, s["ox"], s["pb"], w["wao"], w["convw"], w["wco"], w["wpool"], w["pscale"], w["w_o"], w["at"])
    dqx, dkx, dvx = attn_bwd(s["qxb"], s["kx"], s["vx"], dox)
    dpa, dpf, dgq, dgk, dfb = attn_bwd_post(s["pa"], s["pf"], dqx, dkx, dvx, w["gq"], w["gk"], w["fb"])
    dx, dg1 = in_proj_bwd(s["x"], w["g1"], dxm, dpa, dpf, dpb, w["wa"], w["wf"], w["wb"])
    big["w_in"] = w_in_unprep(wgrad(s["h"], dpa, "wgrad_in_qkv"), wgrad(s["h"], dpf, "wgrad_in_f"),
                              wgrad(s["h"], dpb, "wgrad_in_b"), big["w_in"], gi)
    big["w_attn_out"], big["w_conv_out"], big["pool_w"], big["w_o"] = branch_g_place(
        dwao, dwco, dwpool, dwo, (big["w_attn_out"], big["w_conv_out"], big["pool_w"], big["w_o"]), gi)
    sm = dict(norm_mix_g=dg1[0], forget_b=dfb[0, 0:N_HEADS], q_norm_g=dgq.reshape(N_HEADS, HEAD_DIM).sum(0),
              k_norm_g=dgk.reshape(N_HEADS, HEAD_DIM).sum(0), pool_scale=dpscale[0],
              conv_w=dconvw[0:CONV_K].reshape(-1))
    return dx, big, sm


def grad_buffers(full):
    return {n: lax.empty(full[n].shape, F32) for n in SPLIT}


def local_step(x, target, gathered, small):
    n_layers = small["norm_mix_g"].shape[0]
    done = []
    for l in range(n_layers):
        xm, wm, sm = mix_fwd(x, gathered, l, {n: v[l] for n, v in small.items()})
        x, wf, sf = ffn_half_fwd(xm, gathered, l, small["norm_ffn_g"][l])
        done.append((wm, sm, wf, sf))
    loss, dx = loss_head(x, target)
    big = grad_buffers(gathered)
    small_grads = [None] * n_layers
    for l in reversed(range(n_layers)):
        wm, sm, wf, sf = done[l]
        dxm, ffn_big, g_ffn = ffn_half_bwd(dx, wf, sf, l, {n: big[n] for n in FFN_W})
        dx, mix_big, g_mix = mix_bwd(dxm, wm, sm, l, {n: big[n] for n in MIX_W})
        big = {**ffn_big, **mix_big}
        small_grads[l] = {**g_ffn, **g_mix}
    return loss, dx, big, {n: jnp.stack([g[n] for g in small_grads]) for n in SMALL}


def adamw(w, g, m, v, name):
    R, C = w.shape
    tm = 256 if R % 256 == 0 else R

    def body(w_ref, g_ref, m_ref, v_ref, d_ref, nm_ref, nv_ref):
        gr = g_ref[...]
        m_new = ADAM_B1 * m_ref[...] + (1.0 - ADAM_B1) * gr
        v_new = ADAM_B2 * v_ref[...] + (1.0 - ADAM_B2) * jnp.square(gr)
        nm_ref[...] = m_new
        nv_ref[...] = v_new
        m_hat = m_new / (1.0 - ADAM_B1 ** ADAM_STEP)
        v_hat = v_new / (1.0 - ADAM_B2 ** ADAM_STEP)
        d_ref[...] = -ADAM_LR * (m_hat / (jnp.sqrt(v_hat) + ADAM_EPS) + ADAM_WD * w_ref[...])

    spec = _rows(tm, C)
    out = jax.ShapeDtypeStruct((R, C), F32)
    return pl.pallas_call(
        body,
        grid=(R // tm,),
        in_specs=[spec] * 4,
        out_specs=[spec] * 3,
        out_shape=[out] * 3,
        compiler_params=_params("parallel"),
        name=name,
    )(w, g, m, v)


MESH = pl.DeviceIdType.MESH
HBM_REF = pl.BlockSpec(memory_space=pl.ANY)
N_CHIPS = 4
N_DEV = 8
SMALL_SHAPE = (128, LANES)


def _mesh_pos():
    return lax.axis_index("x"), lax.axis_index("y"), lax.axis_index("c")


def _other_chips(x, y):
    return [(1 - x, y), (x, 1 - y), (1 - x, 1 - y)]


def _remote(src, dst, send_sem, recv_sem, to):
    return pltpu.make_async_remote_copy(src_ref=src, dst_ref=dst, send_sem=send_sem, recv_sem=recv_sem,
                                        device_id=to, device_id_type=MESH)


def _row_tile(rows):
    for tm in (256, 176, 128):
        if rows % tm == 0:
            return tm
    return rows


def _as4(a):
    return a.reshape(a.shape[0], a.shape[1], -1, a.shape[-1])


def _core_rows(buf, c):
    R = buf.shape[2]
    if R % 2:
        return pl.ds(0, R), pl.ds(0, R), False
    return pl.ds(c * (R // 2), R // 2), pl.ds((1 - c) * (R // 2), R // 2), True


def place_shard(chip, w, lo, hi, dtype, name, after=None):
    w3 = w.reshape(w.shape[0], -1, w.shape[-1])
    _, R, C = w3.shape
    tm = _row_tile(R)
    idle = [] if after is None else [after]

    def body(chip_ref, w_ref, *rest):
        rest[-1][...] = w_ref[...].astype(dtype)

    return pl.pallas_call(
        body,
        grid_spec=pltpu.PrefetchScalarGridSpec(
            num_scalar_prefetch=1,
            grid=(hi - lo, R // tm),
            in_specs=[pl.BlockSpec((None, tm, C), lambda l, i, chip: (lo + l, i, 0))] + [_whole(a) for a in idle],
            out_specs=pl.BlockSpec((None, None, tm, C), lambda l, i, chip: (l, chip[0], i, 0)),
        ),
        out_shape=jax.ShapeDtypeStruct((hi - lo, N_CHIPS, R, C), dtype),
        compiler_params=_params("parallel", "parallel"),
        name=name,
    )(chip, w3, *idle)


HBM_SPACE = pl.BlockSpec(memory_space=pltpu.HBM)
SEM_SPACE = pl.BlockSpec(memory_space=pltpu.SEMAPHORE)
IN_FLIGHT = pltpu.SideEffectType.DATAFLOW_SIDE_EFFECTING


def _sem_table(send_sems, recv_sems):
    return lambda t, j: (send_sems.at[t, j], recv_sems.at[t, j])


def _sem_per_peer(send_sems, recv_sems):
    return lambda t, j: (send_sems[j], recv_sems[j])


def _gather_ici(bufs, sem, sends=True, lands=True):
    x, y, c = _mesh_pos()
    me = 2 * x + y
    out, into = [], []
    for t, buf in enumerate(bufs):
        mine, _, _ = _core_rows(buf, c)
        part = lambda k, buf=buf, mine=mine: buf.at[pl.ds(0, buf.shape[0]), k, mine]
        for j, (px, py) in enumerate(_other_chips(x, y)):
            if sends:
                out.append(_remote(part(me), part(me), *sem(t, j), (px, py, c)))
            if lands:
                into.append(_remote(part(2 * px + py), part(2 * px + py), *sem(t, j), (px, py, c)))
    return out, into


def _gather_d2d(bufs, send_sems, recv_sems, first):
    x, y, c = _mesh_pos()
    sends, lands = [], []
    for t, buf in enumerate(bufs):
        mine, theirs, split = _core_rows(buf, c)
        if not split:
            continue
        for j, (px, py) in enumerate(_other_chips(x, y)):
            part = lambda rows, buf=buf, k=2 * px + py: buf.at[pl.ds(0, buf.shape[0]), k, rows]
            sems = (send_sems.at[t, first + j], recv_sems.at[t, first + j], (x, y, 1 - c))
            sends.append(_remote(part(mine), part(mine), *sems))
            lands.append(_remote(part(theirs), part(theirs), *sems))
    return sends, lands


def gather_shards(bufs):
    n = len(bufs)

    def body(*refs):
        outs = refs[n:2 * n]
        send_sems, recv_sems = refs[2 * n:]
        ici_out, ici_in = _gather_ici(outs, _sem_table(send_sems, recv_sems))
        d2d_out, d2d_in = _gather_d2d(outs, send_sems, recv_sems, 3)
        for cp in ici_out:
            cp.start()
        for cp in ici_in:
            cp.wait_recv()
        for cp in d2d_out:
            cp.start()
        for cp in d2d_in:
            cp.wait_recv()
        for cp in ici_out + d2d_out:
            cp.wait_send()

    return pl.pallas_call(
        body,
        in_specs=[HBM_REF] * n,
        out_specs=[HBM_REF] * n,
        out_shape=[jax.ShapeDtypeStruct(b.shape, b.dtype) for b in bufs],
        input_output_aliases={t: t for t in range(n)},
        scratch_shapes=[pltpu.SemaphoreType.DMA((n, 6)), pltpu.SemaphoreType.DMA((n, 6))],
        name="gather_shards",
    )(*bufs)


def gather_start(bufs):
    n = len(bufs)

    def body(*refs):
        send_sems, recv_sems = refs[n:n + 3], refs[n + 3:n + 6]
        outs = refs[n + 6:2 * n + 6]
        token = refs[2 * n + 6]
        for cp in _gather_ici(outs, _sem_per_peer(send_sems, recv_sems), lands=False)[0]:
            cp.start()
        token[...] = jnp.zeros_like(token)

    res = pl.pallas_call(
        body,
        in_specs=[HBM_SPACE] * n,
        out_specs=[SEM_SPACE] * 6 + [HBM_SPACE] * n + [pl.BlockSpec(memory_space=pltpu.VMEM)],
        out_shape=[pltpu.SemaphoreType.DMA(())] * 6
        + [pltpu.HBM(b.shape, b.dtype) for b in bufs] + [jax.ShapeDtypeStruct((8, LANES), F32)],
        input_output_aliases={t: t + 6 for t in range(n)},
        compiler_params=pltpu.CompilerParams(has_side_effects=IN_FLIGHT),
        name="gather_start",
    )(*[pltpu.with_memory_space_constraint(b, pltpu.HBM) for b in bufs])
    return list(res[0:3]), list(res[3:6]), list(res[6:n + 6]), res[n + 6]


def gather_wait(send_sems, recv_sems, bufs, after):
    n = len(bufs)

    def body(*refs):
        ins_send, ins_recv = refs[n:n + 3], refs[n + 3:n + 6]
        outs = refs[n + 7:]
        sends, lands = _gather_ici(outs, _sem_per_peer(ins_send, ins_recv))
        for cp in lands:
            cp.wait_recv()
        for cp in sends:
            cp.wait_send()

    return list(pl.pallas_call(
        body,
        in_specs=[HBM_SPACE] * n + [SEM_SPACE] * 6 + [HBM_REF],
        out_specs=[HBM_SPACE] * n,
        out_shape=[pltpu.HBM(b.shape, b.dtype) for b in bufs],
        input_output_aliases={t: t for t in range(n)},
        compiler_params=pltpu.CompilerParams(has_side_effects=IN_FLIGHT),
        name="gather_wait",
    )(*bufs, *send_sems, *recv_sems, after))


def gather_forward(bufs):
    n = len(bufs)

    def body(*refs):
        outs = refs[n:2 * n]
        send_sems, recv_sems = refs[2 * n:]
        sends, lands = _gather_d2d(outs, send_sems, recv_sems, 0)
        for cp in sends:
            cp.start()
        for cp in lands:
            cp.wait_recv()
        for cp in sends:
            cp.wait_send()

    return list(pl.pallas_call(
        body,
        in_specs=[HBM_REF] * n,
        out_specs=[HBM_REF] * n,
        out_shape=[jax.ShapeDtypeStruct(b.shape, b.dtype) for b in bufs],
        input_output_aliases={t: t for t in range(n)},
        scratch_shapes=[pltpu.SemaphoreType.DMA((n, 3)), pltpu.SemaphoreType.DMA((n, 3))],
        name="gather_forward",
    )(*bufs))


def pair_exchange(grads):
    n = len(grads)

    def body(*refs):
        ins, outs = refs[:n], refs[n:2 * n]
        send_sems, recv_sems = refs[2 * n:]
        cps = _pair_copies(ins, outs, lambda t: (send_sems.at[t], recv_sems.at[t]))
        for cp in cps:
            cp.start()
        for cp in cps:
            cp.wait()

    return list(pl.pallas_call(
        body,
        in_specs=[HBM_REF] * n,
        out_specs=[HBM_REF] * n,
        out_shape=[jax.ShapeDtypeStruct(_half_shape(g), g.dtype) for g in grads],
        scratch_shapes=[pltpu.SemaphoreType.DMA((n,)), pltpu.SemaphoreType.DMA((n,))],
        name="pair_exchange",
    )(*grads))


def _half_shape(g):
    return g.shape[:2] + (g.shape[2] // 2, g.shape[3])


def _pair_copies(grads, lands, sem):
    x, y, c = _mesh_pos()
    cps = []
    for t in range(len(grads)):
        _, theirs, _ = _core_rows(grads[t], c)
        src = grads[t].at[pl.ds(0, grads[t].shape[0]), pl.ds(0, N_CHIPS), theirs]
        cps.append(_remote(src, lands[t], *sem(t), (x, y, 1 - c)))
    return cps


def pair_exchange_start(grads):
    n = len(grads)
    lands = [lax.empty(_half_shape(g), g.dtype) for g in grads]

    def body(*refs):
        send_sem, recv_sem = refs[2 * n:2 * n + 2]
        outs = refs[2 * n + 2:4 * n + 2]
        token = refs[4 * n + 2]
        for cp in _pair_copies(outs[:n], outs[n:], lambda t: (send_sem, recv_sem)):
            cp.start()
        token[...] = jnp.zeros_like(token)

    res = pl.pallas_call(
        body,
        in_specs=[HBM_SPACE] * (2 * n),
        out_specs=[SEM_SPACE] * 2 + [HBM_SPACE] * (2 * n) + [pl.BlockSpec(memory_space=pltpu.VMEM)],
        out_shape=[pltpu.SemaphoreType.DMA(())] * 2 + [pltpu.HBM(a.shape, a.dtype) for a in list(grads) + lands]
        + [jax.ShapeDtypeStruct((8, LANES), F32)],
        input_output_aliases={t: t + 2 for t in range(2 * n)},
        compiler_params=pltpu.CompilerParams(has_side_effects=IN_FLIGHT),
        name="pair_exchange_start",
    )(*[pltpu.with_memory_space_constraint(a, pltpu.HBM) for a in list(grads) + lands])
    return res[0], res[1], list(res[2:n + 2]), list(res[n + 2:2 * n + 2]), res[2 * n + 2]


def pair_exchange_wait(send_sem, recv_sem, grads, lands, after):
    n = len(grads)

    def body(*refs):
        in_send, in_recv = refs[2 * n:2 * n + 2]
        outs = refs[2 * n + 3:]
        for cp in _pair_copies(outs[:n], outs[n:], lambda t: (in_send, in_recv)):
            cp.wait()

    res = pl.pallas_call(
        body,
        in_specs=[HBM_SPACE] * (2 * n) + [SEM_SPACE] * 2 + [HBM_REF],
        out_specs=[HBM_SPACE] * (2 * n),
        out_shape=[pltpu.HBM(a.shape, a.dtype) for a in list(grads) + list(lands)],
        input_output_aliases={t: t for t in range(2 * n)},
        compiler_params=pltpu.CompilerParams(has_side_effects=IN_FLIGHT),
        name="pair_exchange_wait",
    )(*grads, *lands, send_sem, recv_sem, after)
    return list(res[:n]), list(res[n:])


def pair_sum(core, g, t, name):
    n, _, half, C = t.shape
    tm = _row_tile(half)
    per = half // tm

    def body(c_ref, g_ref, t_ref, o_ref):
        o_ref[...] = (g_ref[...] + t_ref[...]).astype(BF16)

    tile = pl.BlockSpec((None, tm, C), lambda a, i, c: (a, i, 0))
    out = pl.pallas_call(
        body,
        grid_spec=pltpu.PrefetchScalarGridSpec(
            num_scalar_prefetch=1,
            grid=(n * N_CHIPS, per),
            in_specs=[pl.BlockSpec((None, tm, C), lambda a, i, c: (a, per * c[0] + i, 0)), tile],
            out_specs=tile,
        ),
        out_shape=jax.ShapeDtypeStruct((n * N_CHIPS, half, C), BF16),
        compiler_params=_params("parallel", "parallel"),
        name=name,
    )(core, g.reshape(n * N_CHIPS, 2 * half, C), t.reshape(n * N_CHIPS, half, C))
    return out.reshape(t.shape)


def _chip_copies(sums, recv, sem):
    x, y, c = _mesh_pos()
    cps = []
    for t in range(len(sums)):
        for j, (px, py) in enumerate(_other_chips(x, y)):
            src = sums[t].at[pl.ds(0, sums[t].shape[0]), 2 * px + py]
            cps.append(_remote(src, recv[t].at[j], *sem(t, j), (px, py, c)))
    return cps


def _recv_shape(s):
    return (N_CHIPS - 1, s.shape[0]) + s.shape[2:]


def chip_exchange(sums):
    n = len(sums)

    def body(*refs):
        ins, outs = refs[:n], refs[n:2 * n]
        cps = _chip_copies(ins, outs, _sem_table(*refs[2 * n:]))
        for cp in cps:
            cp.start()
        for cp in cps:
            cp.wait()

    return list(pl.pallas_call(
        body,
        in_specs=[HBM_REF] * n,
        out_specs=[HBM_REF] * n,
        out_shape=[jax.ShapeDtypeStruct(_recv_shape(s), s.dtype) for s in sums],
        scratch_shapes=[pltpu.SemaphoreType.DMA((n, 3)), pltpu.SemaphoreType.DMA((n, 3))],
        name="chip_exchange",
    )(*sums))


def chip_exchange_start(sums):
    n = len(sums)
    lands = [lax.empty(_recv_shape(s), s.dtype) for s in sums]

    def body(*refs):
        send_sems, recv_sems = refs[2 * n:2 * n + 3], refs[2 * n + 3:2 * n + 6]
        outs = refs[2 * n + 6:4 * n + 6]
        token = refs[4 * n + 6]
        for cp in _chip_copies(outs[:n], outs[n:2 * n], _sem_per_peer(send_sems, recv_sems)):
            cp.start()
        token[...] = jnp.zeros_like(token)

    res = pl.pallas_call(
        body,
        in_specs=[HBM_SPACE] * (2 * n),
        out_specs=[SEM_SPACE] * 6 + [HBM_SPACE] * (2 * n) + [pl.BlockSpec(memory_space=pltpu.VMEM)],
        out_shape=[pltpu.SemaphoreType.DMA(())] * 6 + [pltpu.HBM(a.shape, a.dtype) for a in list(sums) + lands]
        + [jax.ShapeDtypeStruct((8, LANES), F32)],
        input_output_aliases={t: t + 6 for t in range(2 * n)},
        compiler_params=pltpu.CompilerParams(has_side_effects=IN_FLIGHT),
        name="chip_exchange_start",
    )(*[pltpu.with_memory_space_constraint(a, pltpu.HBM) for a in list(sums) + lands])
    return list(res[0:3]), list(res[3:6]), list(res[6:n + 6]), list(res[n + 6:2 * n + 6]), res[2 * n + 6]


def chip_exchange_wait(send_sems, recv_sems, sums, lands, after):
    n = len(sums)

    def body(*refs):
        ins_send, ins_recv = refs[2 * n:2 * n + 3], refs[2 * n + 3:2 * n + 6]
        outs = refs[2 * n + 7:]
        for cp in _chip_copies(outs[:n], outs[n:], _sem_per_peer(ins_send, ins_recv)):
            cp.wait()

    res = pl.pallas_call(
        body,
        in_specs=[HBM_SPACE] * (2 * n) + [SEM_SPACE] * 6 + [HBM_REF],
        out_specs=[HBM_SPACE] * (2 * n),
        out_shape=[pltpu.HBM(a.shape, a.dtype) for a in list(sums) + list(lands)],
        input_output_aliases={t: t for t in range(2 * n)},
        compiler_params=pltpu.CompilerParams(has_side_effects=IN_FLIGHT),
        name="chip_exchange_wait",
    )(*sums, *lands, *send_sems, *recv_sems, after)
    return list(res[:n]), list(res[n:])


def chip_sum_into(core, chip, recv, sums, total, lo, name):
    _, n, half, C = recv.shape
    tm = _row_tile(half)
    per = half // tm

    def body(c_ref, k_ref, r_ref, s_ref, t_ref, o_ref):
        acc = s_ref[...].astype(F32)
        for j in range(N_CHIPS - 1):
            acc = acc + r_ref[j].astype(F32)
        o_ref[...] = acc

    return pl.pallas_call(
        body,
        grid_spec=pltpu.PrefetchScalarGridSpec(
            num_scalar_prefetch=2,
            grid=(n, per),
            in_specs=[pl.BlockSpec((N_CHIPS - 1, None, tm, C), lambda a, i, c, k: (0, a, i, 0)),
                      pl.BlockSpec((None, None, tm, C), lambda a, i, c, k: (a, k[0], i, 0)),
                      HBM_REF],
            out_specs=pl.BlockSpec((None, tm, C), lambda a, i, c, k: (lo + a, per * c[0] + i, 0)),
        ),
        out_shape=jax.ShapeDtypeStruct(total.shape, F32),
        input_output_aliases={4: 0},
        compiler_params=_params("parallel", "parallel"),
        name=name,
    )(core, chip, recv, sums, total)


def sibling_share(totals):
    n = len(totals)

    def body(*refs):
        outs = refs[n:2 * n]
        send_sems, recv_sems = refs[2 * n:]
        x, y, c = _mesh_pos()
        half = lambda t, h: outs[t].at[pl.ds(0, DEPTH), pl.ds(h * (outs[t].shape[1] // 2), outs[t].shape[1] // 2)]
        sent = [_remote(half(t, c), half(t, c), send_sems.at[t], recv_sems.at[t], (x, y, 1 - c)) for t in range(n)]
        for cp in sent:
            cp.start()
        for t in range(n):
            _remote(half(t, 1 - c), half(t, 1 - c), send_sems.at[t], recv_sems.at[t], (x, y, 1 - c)).wait_recv()
        for cp in sent:
            cp.wait_send()

    return pl.pallas_call(
        body,
        in_specs=[HBM_REF] * n,
        out_specs=[HBM_REF] * n,
        out_shape=[jax.ShapeDtypeStruct(t.shape, t.dtype) for t in totals],
        input_output_aliases={t: t for t in range(n)},
        scratch_shapes=[pltpu.SemaphoreType.DMA((n,)), pltpu.SemaphoreType.DMA((n,))],
        name="sibling_share",
    )(*totals)


def small_allgather(small):
    def body(s_ref, a_ref, send_sems, recv_sems, local_sem):
        x, y, c = _mesh_pos()
        me = 4 * x + 2 * y + c
        own = pltpu.make_async_copy(s_ref, a_ref.at[me], local_sem)
        own.start()
        sent = []
        for k in range(1, N_DEV):
            peer = (x ^ (k >> 2), y ^ ((k >> 1) & 1), c ^ (k & 1))
            cp = _remote(s_ref, a_ref.at[me], send_sems.at[k - 1], recv_sems.at[k - 1], peer)
            cp.start()
            sent.append(cp)
        for k in range(1, N_DEV):
            px, py, pc = x ^ (k >> 2), y ^ ((k >> 1) & 1), c ^ (k & 1)
            _remote(s_ref, a_ref.at[4 * px + 2 * py + pc], send_sems.at[k - 1], recv_sems.at[k - 1], (px, py, pc)).wait_recv()
        for cp in sent:
            cp.wait_send()
        own.wait()

    return pl.pallas_call(
        body,
        in_specs=[HBM_REF],
        out_specs=HBM_REF,
        out_shape=jax.ShapeDtypeStruct((N_DEV,) + SMALL_SHAPE, small.dtype),
        scratch_shapes=[pltpu.SemaphoreType.DMA((N_DEV - 1,)), pltpu.SemaphoreType.DMA((N_DEV - 1,)), pltpu.SemaphoreType.DMA],
        name="small_allgather",
    )(small)


def small_sum(blocks):
    def body(a_ref, o_ref):
        acc = a_ref[0]
        for d in range(1, N_DEV):
            acc = acc + a_ref[d]
        o_ref[...] = acc

    return pl.pallas_call(
        body,
        in_specs=[pl.BlockSpec(memory_space=pltpu.VMEM)],
        out_specs=pl.BlockSpec(memory_space=pltpu.VMEM),
        out_shape=jax.ShapeDtypeStruct(SMALL_SHAPE, F32),
        name="small_sum",
    )(blocks)


def pack_small(grads, loss):
    flat = jnp.concatenate([grads[n].reshape(-1) for n in SMALL] + [loss.reshape(-1)])
    size = SMALL_SHAPE[0] * SMALL_SHAPE[1]
    return jnp.pad(flat, (0, size - flat.shape[0])).reshape(SMALL_SHAPE)


def unpack_small(packed):
    flat = packed.reshape(-1)
    out, off = {}, 0
    for n, size in SMALL.items():
        out[n] = flat[off:off + DEPTH * size].reshape(DEPTH, size)
        off += DEPTH * size
    return out, flat[off]


def kernel(x, norm_mix_g, w_in, forget_b, q_norm_g, k_norm_g, w_attn_out, conv_w, w_conv_out, pool_w, pool_scale, w_o, norm_ffn_g, w_ffn_in, w_ffn_out, loss_target, m_norm_mix_g, m_w_in, m_forget_b, m_q_norm_g, m_k_norm_g, m_w_attn_out, m_conv_w, m_w_conv_out, m_pool_w, m_pool_scale, m_w_o, m_norm_ffn_g, m_w_ffn_in, m_w_ffn_out, v_norm_mix_g, v_w_in, v_forget_b, v_q_norm_g, v_k_norm_g, v_w_attn_out, v_conv_w, v_w_conv_out, v_pool_w, v_pool_scale, v_w_o, v_norm_ffn_g, v_w_ffn_in, v_w_ffn_out):
    given = dict(locals())
    weights = {n: given[n] for n in WEIGHTS}

    core = lax.axis_index("c").astype(jnp.int32)
    chip = (2 * lax.axis_index("x") + lax.axis_index("y")).astype(jnp.int32)

    core, chip = core.reshape(1), chip.reshape(1)
    mix_names, all_names = MIX_W + ["conv_w"], MIX_W + FFN_W + ["conv_w"]

    def placed(lo, hi, names, after=None):
        piece = None if after is None else _as4(next(iter(after.values())))[0, 0, 0:HALO, 0:LANES]
        return [place_shard(chip, weights[n], lo, hi, F32 if n == "conv_w" else BF16, "place_" + n,
                            piece if i == 0 else None) for i, n in enumerate(names)]

    def as_weights(bufs, names):
        return {n: b.reshape(b.shape[:2] + weights[n].shape[1:]) for n, b in zip(names, bufs)}

    def landed(start, names, after):
        return as_weights(gather_forward(gather_wait(*start[:3], after)), names)

    def layer_small(l, *tokens):
        small = {n: weights[n][l] for n in SMALL if n != "conv_w"}
        for t in tokens:
            small["norm_mix_g"] = small["norm_mix_g"] + t[0, 0]
        return small

    done = [None] * DEPTH
    first = as_weights(gather_shards(placed(0, 1, mix_names)), mix_names)
    ffn0 = gather_start(placed(0, 1, FFN_W, after=first))
    layer1 = gather_start(placed(1, 2, all_names, after=first))
    xm, wm, sm = mix_fwd(x[0], first, 0, layer_small(0, ffn0[3], layer1[3]))
    xs, wf, sf = ffn_half_fwd(xm, landed(ffn0, FFN_W, xm), 0, weights["norm_ffn_g"][0])
    done[0] = (wm, sm, wf, sf)
    full = landed(layer1, all_names, xs)
    rest = gather_start(placed(2, DEPTH, all_names, after=full))
    xm, wm, sm = mix_fwd(xs, full, 0, layer_small(1, rest[3]))
    xs, wf, sf = ffn_half_fwd(xm, full, 0, weights["norm_ffn_g"][1])
    done[1] = (wm, sm, wf, sf)
    full = landed(rest, all_names, xs)
    for l in range(2, DEPTH):
        xm, wm, sm = mix_fwd(xs, full, l - 2, layer_small(l))
        xs, wf, sf = ffn_half_fwd(xm, full, l - 2, weights["norm_ffn_g"][l])
        done[l] = (wm, sm, wf, sf)
    loss, dx = loss_head(xs, loss_target[0])

    small_grads = [None] * DEPTH
    totals = {n: lax.empty((DEPTH,) + _as4(weights[n][None]).shape[2:], F32) for n in SPLIT}

    def buffers(names, n_layers):
        return {n: lax.empty((n_layers, N_CHIPS) + weights[n].shape[1:], F32) for n in names}

    def pair_sums(names, grads, theirs):
        return [pair_sum(core, a, t, "pair_sum_" + n) for n, a, t in zip(names, grads, theirs)]

    def add_chips(names, recv, sums, lo):
        for n, r, s in zip(names, recv, sums):
            totals[n] = chip_sum_into(core, chip, r, s, totals[n], lo, "chip_sum_" + n)

    big = buffers(SPLIT, DEPTH - 1)
    for l in reversed(range(1, DEPTH)):
        wm, sm, wf, sf = done[l]
        dxm, ffn_big, g_ffn = ffn_half_bwd(dx, wf, sf, l - 1, {n: big[n] for n in FFN_W})
        dx, mix_big, g_mix = mix_bwd(dxm, wm, sm, l - 1, {n: big[n] for n in MIX_W})
        big = {**ffn_big, **mix_big}
        small_grads[l] = {**g_ffn, **g_mix}
    pair = pair_exchange_start([_as4(big[n]) for n in SPLIT])

    wm, sm, wf, sf = done[0]
    wf = dict(wf, g2=wf["g2"] + pair[4][0:1, 0:1])
    dxm, ffn_big, g_ffn = ffn_half_bwd(dx, wf, sf, 0, buffers(FFN_W, 1))
    chips = chip_exchange_start(pair_sums(SPLIT, *pair_exchange_wait(*pair[:4], dxm)))
    ffn_grads = [_as4(ffn_big[n]) for n in FFN_W]
    ffn_chips = chip_exchange_start(pair_sums(FFN_W, ffn_grads, pair_exchange(ffn_grads)))
    wm = dict(wm, pscale=wm["pscale"] + chips[4][0:1, 0:1] + ffn_chips[4][0:1, 0:1])
    dx, mix_big, g_mix = mix_bwd(dxm, wm, sm, 0, buffers(MIX_W, 1))
    small_grads[0] = {**g_ffn, **g_mix}

    sums, recv = chip_exchange_wait(*chips[:4], dx)
    add_chips(SPLIT, recv, sums, 1)
    sums, recv = chip_exchange_wait(*ffn_chips[:4], dx)
    add_chips(FFN_W, recv, sums, 0)
    mix_grads = [_as4(mix_big[n]) for n in MIX_W]
    sums = pair_sums(MIX_W, mix_grads, pair_exchange(mix_grads))
    add_chips(MIX_W, chip_exchange(sums), sums, 0)
    shared = sibling_share([totals[n] for n in SPLIT])
    reduced = {n: t.reshape(weights[n].shape) for n, t in zip(SPLIT, shared)}

    small_grads = {n: jnp.stack([g[n] for g in small_grads]) for n in SMALL}
    small_total, loss_sum = unpack_small(small_sum(small_allgather(pack_small(small_grads, loss))))
    chip = chip[0]
    cols = D_CONV // N_CHIPS
    small_total["conv_w"] = lax.dynamic_slice_in_dim(small_total["conv_w"].reshape(DEPTH, CONV_K, D_CONV), chip * cols, cols, axis=2)
    for n in SMALL:
        reduced[n] = small_total[n].reshape(weights[n].shape)

    deltas, new_m, new_v = {}, {}, {}
    for n in WEIGHTS:
        w = weights[n]
        flat = (-1, w.shape[-1])
        d, nm, nv = adamw(w.reshape(flat), reduced[n].reshape(flat), given["m_" + n].reshape(flat),
                          given["v_" + n].reshape(flat), "adamw_" + n)
        deltas[n], new_m[n], new_v[n] = d.reshape(w.shape), nm.reshape(w.shape), nv.reshape(w.shape)
    return (loss_sum, dx[None], *[reduced[n] for n in WEIGHTS], *[deltas[n] for n in WEIGHTS],
            *[new_m[n] for n in WEIGHTS], *[new_v[n] for n in WEIGHTS])
```

```python
import functools

import numpy as np
import jax
import jax.numpy as jnp
from jax import lax
from jax.experimental import pallas as pl
from jax.experimental.pallas import tpu as pltpu

F32 = jnp.float32
BF16 = jnp.bfloat16

D_MODEL = 1024
DEPTH = 4
HEAD_DIM = 64
N_HEADS = 8
D_ATTN = 512
D_CONV = 256
D_POOL = 256
D_FF = 2816
D_IN = 5640
CONV_K = 3
POOL_WINDOWS = (2, 4, 8, 16)
N_GROUPS = len(POOL_WINDOWS)
EPS = 1e-6
ADAM_LR, ADAM_B1, ADAM_B2, ADAM_EPS, ADAM_WD, ADAM_STEP = 0.001, 0.9, 0.999, 1e-08, 0.01, 10

D_QKV = 3 * D_ATTN
D_F = 128
D_B = 3 * D_CONV + D_POOL + 3 * D_MODEL
D_LOCAL = 3 * D_CONV + D_POOL

LANES = 128
D_HEADS = N_HEADS * LANES
HALO = 16
VMEM_LIMIT = 56 * 1024 * 1024
NEG = -1e30
LOG2E = 1.4426950408889634
LN2 = 0.6931471805599453

TM = 256
TM_MIX = 256
TQ = 1024
TK = 1024
KB = 1024
QT = 1024

LANE_C = 64
LANE_ONE = 67
LANE_LSE = 70
N_PIECES = 3


def _dot(a, b):
    return jnp.dot(a, b, preferred_element_type=F32)


def _dot_nt(a, b):
    return lax.dot_general(a, b, (((1,), (1,)), ((), ())), preferred_element_type=F32)


def _dot_tn(a, b):
    return lax.dot_general(a, b, (((0,), (0,)), ((), ())), preferred_element_type=F32)


def _params(*sem):
    return pltpu.CompilerParams(dimension_semantics=sem, vmem_limit_bytes=VMEM_LIMIT)


def _rows(tm, n):
    return pl.BlockSpec((tm, n), lambda i, *_: (i, 0))


def _whole(a):
    nd = a.ndim
    return pl.BlockSpec(a.shape, lambda *_: (0,) * nd)


def _layer(shape):
    nd = len(shape)
    return pl.BlockSpec((None,) + tuple(shape), lambda *a: (a[-1][0],) + (0,) * nd)


def _layer_index(l):
    return jnp.full((1,), l, jnp.int32)


def _split_bf16(x):
    hi = x.astype(BF16)
    lo = (x - hi.astype(F32)).astype(BF16)
    return hi, lo


def _pieces(x):
    p1 = x.astype(BF16)
    r1 = x - p1.astype(F32)
    p2 = r1.astype(BF16)
    p3 = (r1 - p2.astype(F32)).astype(BF16)
    return p1, p2, p3


def _sigmoid(x):
    return 1.0 / (1.0 + jnp.exp(-x))


def w_in_prep(win, l):
    tr = 256
    n = D_IN // 4
    v_rest = D_QKV - n
    b0 = v_rest + N_HEADS

    def body(s0, s1, s2, s3, wa_ref, wf_ref, wb_ref):
        b = s1[...]
        wa_ref[...] = jnp.concatenate([s0[...], b[:, 0:v_rest]], axis=1)
        wf_ref[...] = jnp.concatenate([b[:, v_rest:b0], jnp.zeros((tr, D_F - N_HEADS), b.dtype)], axis=1)
        wb_ref[...] = jnp.concatenate([b[:, b0:n], s2[...], s3[...]], axis=1)

    shard = lambda j: pl.BlockSpec((None, None, tr, n), lambda i: (l, j, i, 0))
    return pl.pallas_call(
        body,
        grid=(D_MODEL // tr,),
        in_specs=[shard(0), shard(1), shard(2), shard(3)],
        out_specs=[_rows(tr, D_QKV), _rows(tr, D_F), _rows(tr, D_B)],
        out_shape=[jax.ShapeDtypeStruct((D_MODEL, D_QKV), win.dtype), jax.ShapeDtypeStruct((D_MODEL, D_F), win.dtype),
                   jax.ShapeDtypeStruct((D_MODEL, D_B), win.dtype)],
        compiler_params=_params("parallel"),
        name="w_in_prep",
    )(win, win, win, win)


def _into_layer(buf):
    return dict(in_spec=HBM_REF, out_shape=jax.ShapeDtypeStruct(buf.shape, buf.dtype), aliases={1: 0})


def w_in_unprep(dwa, dwf, dwb, buf, l):
    tr = 256
    n = D_IN // 4
    v_rest = D_QKV - n
    b1 = n - v_rest - N_HEADS
    place = _into_layer(buf)

    def body(l_ref, buf_ref, a_ref, f_ref, b_ref, o_ref):
        a = a_ref[...]
        b = b_ref[...]
        o_ref[0] = a[:, 0:n]
        o_ref[1] = jnp.concatenate([a[:, n:D_QKV], f_ref[:, 0:N_HEADS], b[:, 0:b1]], axis=1)
        o_ref[2] = b[:, b1:b1 + n]
        o_ref[3] = b[:, b1 + n:D_B]

    return pl.pallas_call(
        body,
        grid_spec=pltpu.PrefetchScalarGridSpec(
            num_scalar_prefetch=1,
            grid=(D_MODEL // tr,),
            in_specs=[place["in_spec"], _rows(tr, D_QKV), _rows(tr, D_F), _rows(tr, D_B)],
            out_specs=pl.BlockSpec((None, 4, tr, n), lambda i, l: (l[0], 0, i, 0)),
        ),
        out_shape=place["out_shape"],
        input_output_aliases=place["aliases"],
        compiler_params=_params("parallel"),
        name="w_in_unprep",
    )(_layer_index(l), buf, dwa, dwf, dwb)


def branch_w_prep(wao, wco, pw, cw, l):
    gd = D_POOL // N_GROUPS
    od = D_MODEL // N_GROUPS

    def body(wao_ref, wco_ref, pw_ref, cw_ref, ao_ref, co_ref, po_ref, co8_ref):
        a = jnp.concatenate([wao_ref[j] for j in range(4)], axis=1)
        gap = jnp.zeros((LANES - HEAD_DIM, D_MODEL), a.dtype)
        ao_ref[...] = jnp.concatenate(
            [blk for h in range(N_HEADS) for blk in (a[h * HEAD_DIM:(h + 1) * HEAD_DIM], gap)], axis=0)
        co_ref[...] = jnp.concatenate([wco_ref[j] for j in range(4)], axis=1)
        zero = jnp.zeros((gd, od), a.dtype)
        po_ref[...] = jnp.concatenate(
            [jnp.concatenate([jnp.concatenate([pw_ref[j, g] for j in range(4)], axis=1) if g2 == g else zero
                              for g2 in range(N_GROUPS)], axis=1) for g in range(N_GROUPS)], axis=0)
        co8_ref[...] = jnp.zeros_like(co8_ref)
        co8_ref[0:CONV_K, :] = jnp.concatenate([cw_ref[j] for j in range(4)], axis=1)

    sel = lambda *shape: pl.BlockSpec((None,) + shape, lambda i: (l,) + (0,) * len(shape))
    return pl.pallas_call(
        body,
        grid=(1,),
        in_specs=[sel(4, D_ATTN, D_MODEL // 4), sel(4, D_CONV, D_MODEL // 4), sel(4, N_GROUPS, gd, od // 4),
                  sel(4, CONV_K, D_CONV // 4)],
        out_specs=[pl.BlockSpec((D_HEADS, D_MODEL), lambda i: (0, 0)), pl.BlockSpec((D_CONV, D_MODEL), lambda i: (0, 0)),
                   pl.BlockSpec((D_POOL, D_MODEL), lambda i: (0, 0)), pl.BlockSpec((8, D_CONV), lambda i: (0, 0))],
        out_shape=[jax.ShapeDtypeStruct((D_HEADS, D_MODEL), BF16), jax.ShapeDtypeStruct((D_CONV, D_MODEL), BF16),
                   jax.ShapeDtypeStruct((D_POOL, D_MODEL), BF16), jax.ShapeDtypeStruct((8, D_CONV), F32)],
        name="branch_w_prep",
    )(wao, wco, pw, cw)


def branch_g_place(dwao, dwco, dwpool, dwo, bufs, l):
    gd = D_POOL // N_GROUPS
    od = D_MODEL // N_GROUPS
    q = D_MODEL // 4

    def body(l_ref, b0, b1, b2, b3, a_ref, c_ref, p_ref, w_ref, ao_ref, co_ref, po_ref, wo_ref):
        a = jnp.concatenate([a_ref[h * LANES:h * LANES + HEAD_DIM, :] for h in range(N_HEADS)], axis=0)
        c = c_ref[...]
        p = p_ref[...]
        for j in range(4):
            ao_ref[j] = a[:, j * q:(j + 1) * q]
            co_ref[j] = c[:, j * q:(j + 1) * q]
            wo_ref[j] = w_ref[j * q:(j + 1) * q, :]
            for g in range(N_GROUPS):
                c0 = g * od + j * (od // 4)
                po_ref[j, g] = p[g * gd:(g + 1) * gd, c0:c0 + od // 4]

    whole = lambda a: pl.BlockSpec(a.shape, lambda i, l: (0,) * a.ndim)
    layer = lambda b: pl.BlockSpec((None,) + b.shape[1:], lambda i, l: (l[0],) + (0,) * (b.ndim - 1))
    return pl.pallas_call(
        body,
        grid_spec=pltpu.PrefetchScalarGridSpec(
            num_scalar_prefetch=1,
            grid=(1,),
            in_specs=[HBM_REF] * 4 + [whole(dwao), whole(dwco), whole(dwpool), whole(dwo)],
            out_specs=[layer(b) for b in bufs],
        ),
        out_shape=[jax.ShapeDtypeStruct(b.shape, b.dtype) for b in bufs],
        input_output_aliases={1: 0, 2: 1, 3: 2, 4: 3},
        compiler_params=_params("arbitrary"),
        name="branch_g_place",
    )(_layer_index(l), *bufs, dwao, dwco, dwpool, dwo)


def in_proj_fwd(x, g, wa, wf, wb):
    S = x.shape[0]

    def body(x_ref, g_ref, wa_ref, wf_ref, wb_ref, h_ref, pa_ref, pf_ref, pb_ref):
        xf = x_ref[...]
        r = lax.rsqrt(jnp.mean(xf * xf, axis=-1, keepdims=True) + EPS)
        h = (xf * r * g_ref[...]).astype(BF16)
        h_ref[...] = h
        pa_ref[...] = _dot(h, wa_ref[...]).astype(BF16)
        pf_ref[...] = _dot(h, wf_ref[...])
        pb_ref[...] = _dot(h, wb_ref[...]).astype(BF16)

    return pl.pallas_call(
        body,
        grid=(S // TM,),
        in_specs=[_rows(TM, D_MODEL), _whole(g), _whole(wa), _whole(wf), _whole(wb)],
        out_specs=[_rows(TM, D_MODEL), _rows(TM, D_QKV), _rows(TM, D_F), _rows(TM, D_B)],
        out_shape=[
            jax.ShapeDtypeStruct((S, D_MODEL), BF16),
            jax.ShapeDtypeStruct((S, D_QKV), BF16),
            jax.ShapeDtypeStruct((S, D_F), F32),
            jax.ShapeDtypeStruct((S, D_B), BF16),
        ],
        compiler_params=_params("parallel"),
        name="in_proj_fwd",
    )(x, g, wa, wf, wb)


def in_proj_bwd(x, g, dxm, dpa, dpf, dpb, wa, wf, wb):
    S = x.shape[0]

    def body(x_ref, g_ref, dxm_ref, dpa_ref, dpf_ref, dpb_ref, wa_ref, wf_ref, wb_ref, dx_ref, dg_ref):
        @pl.when(pl.program_id(0) == 0)
        def _():
            dg_ref[...] = jnp.zeros_like(dg_ref)

        dh = _dot_nt(dpa_ref[...], wa_ref[...]) + _dot_nt(dpf_ref[...], wf_ref[...]) + _dot_nt(dpb_ref[...], wb_ref[...])
        xf = x_ref[...]
        r = lax.rsqrt(jnp.mean(xf * xf, axis=-1, keepdims=True) + EPS)
        xhat = xf * r
        dg_ref[...] += jnp.sum(dh * xhat, axis=0, keepdims=True)
        gdh = dh * g_ref[...]
        dx_ref[...] = dxm_ref[...] + r * (gdh - xhat * jnp.mean(xhat * gdh, axis=-1, keepdims=True))

    return pl.pallas_call(
        body,
        grid=(S // TM,),
        in_specs=[_rows(TM, D_MODEL), _whole(g), _rows(TM, D_MODEL), _rows(TM, D_QKV), _rows(TM, D_F), _rows(TM, D_B),
                  _whole(wa), _whole(wf), _whole(wb)],
        out_specs=[_rows(TM, D_MODEL), pl.BlockSpec((1, D_MODEL), lambda i: (0, 0))],
        out_shape=[jax.ShapeDtypeStruct((S, D_MODEL), F32), jax.ShapeDtypeStruct((1, D_MODEL), F32)],
        compiler_params=_params("arbitrary"),
        name="in_proj_bwd",
    )(x, g, dxm, dpa, dpf, dpb, wa, wf, wb)


TS_WGRAD = 1024


def _wgrad_columns(n):
    return next(tn for tn in (1024, 768, 512, 256, 128) if n % tn == 0)


def wgrad(xa, dy, name):
    S, K = xa.shape
    N = dy.shape[1]
    ts = min(TS_WGRAD, S)
    tn = _wgrad_columns(N)

    def body(x_ref, dy_ref, o_ref):
        @pl.when(pl.program_id(1) == 0)
        def _():
            o_ref[...] = jnp.zeros_like(o_ref)

        o_ref[...] += _dot_tn(x_ref[...], dy_ref[...])

    return pl.pallas_call(
        body,
        grid=(N // tn, S // ts),
        in_specs=[pl.BlockSpec((ts, K), lambda j, k: (k, 0)), pl.BlockSpec((ts, tn), lambda j, k: (k, j))],
        out_specs=pl.BlockSpec((K, tn), lambda j, k: (0, j)),
        out_shape=jax.ShapeDtypeStruct((K, N), F32),
        compiler_params=_params("parallel", "arbitrary"),
        name=name,
    )(xa, dy)


def wgrad_into(xa, dy, name, buf, l):
    S, K = xa.shape
    N = dy.shape[1]
    ts = min(TS_WGRAD, S)
    split = buf.ndim == 4
    tn = buf.shape[-1] if split else _wgrad_columns(N)
    place = _into_layer(buf)

    def body(l_ref, buf_ref, x_ref, dy_ref, o_ref):
        @pl.when(pl.program_id(1) == 0)
        def _():
            o_ref[...] = jnp.zeros_like(o_ref)

        o_ref[...] += _dot_tn(x_ref[...], dy_ref[...])

    if split:
        out_spec = pl.BlockSpec((None, None, K, tn), lambda j, k, l: (l[0], j, 0, 0))
    else:
        out_spec = pl.BlockSpec((None, K, tn), lambda j, k, l: (l[0], 0, j))
    return pl.pallas_call(
        body,
        grid_spec=pltpu.PrefetchScalarGridSpec(
            num_scalar_prefetch=1,
            grid=(N // tn, S // ts),
            in_specs=[place["in_spec"], pl.BlockSpec((ts, K), lambda j, k, l: (k, 0)),
                      pl.BlockSpec((ts, tn), lambda j, k, l: (k, j))],
            out_specs=out_spec,
        ),
        out_shape=place["out_shape"],
        input_output_aliases=place["aliases"],
        compiler_params=_params("parallel", "arbitrary"),
        name=name,
    )(_layer_index(l), buf, xa, dy)


def _head_mean_matrix():
    h = np.arange(D_ATTN) // HEAD_DIM
    return jnp.asarray((h[:, None] == h[None, :]).astype(np.float32) / HEAD_DIM, BF16)


def _place_matrix(lane0):
    m = np.zeros((N_PIECES * LANES, D_HEADS), np.float32)
    for i in range(N_PIECES):
        for h in range(N_HEADS):
            m[i * LANES + h, h * LANES + lane0 + i] = 1.0
    return jnp.asarray(m, BF16)


def _tri(n, upper):
    r = np.arange(n)
    m = (r[None, :] >= r[:, None]) if upper else (r[None, :] <= r[:, None])
    return jnp.asarray(m.astype(np.float32), BF16)


def _lanes_in(lane, lo, n):
    return (lane >= lo) & (lane < lo + n)


def qk_prep(pa, pf, gq, gk, fb):
    S = pa.shape[0]
    bd = _head_mean_matrix()
    tri = _tri(TM, upper=False)
    place_q = _place_matrix(LANE_C)
    place_k = _place_matrix(LANE_ONE)

    def body(q_ref, k_ref, v_ref, pf_ref, gq_ref, gk_ref, fb_ref, bd_ref, tri_ref, pq_ref, pk_ref,
             qx_ref, kx_ref, vx_ref, carry):
        @pl.when(pl.program_id(0) == 0)
        def _():
            carry[...] = jnp.zeros_like(carry)

        def head_norm(x_ref, g_ref, scale):
            xf = x_ref[...].astype(F32)
            ms = _dot((xf * xf).astype(BF16), bd_ref[...])
            return xf * lax.rsqrt(ms + EPS) * g_ref[...] * scale

        qh = head_norm(q_ref, gq_ref, HEAD_DIM ** -0.5 * LOG2E)
        kh = head_norm(k_ref, gk_ref, 1.0)
        vf = v_ref[...].astype(F32)

        z = pf_ref[...] + fb_ref[...]
        logf = jnp.minimum(z, 0.0) - jnp.log(1.0 + jnp.exp(-jnp.abs(z)))
        hi, lo = _split_bf16(logf)
        c = _dot(tri_ref[...], hi) + _dot(tri_ref[...], lo) + carry[...]
        carry[...] += jnp.sum(hi.astype(F32) + lo.astype(F32), axis=0, keepdims=True)
        pieces = jnp.concatenate(_pieces(c * LOG2E), axis=1)
        cq = _dot(pieces, pq_ref[...])
        ck = _dot(pieces, pk_ref[...])

        lane = lax.broadcasted_iota(jnp.int32, (TM, LANES), 1)
        low = lane < HEAD_DIM
        ones_q = _lanes_in(lane, LANE_ONE, N_PIECES).astype(F32)
        ones_k = (_lanes_in(lane, LANE_C, N_PIECES) | _lanes_in(lane, LANE_LSE, N_PIECES)).astype(F32)
        ones_v = _lanes_in(lane, LANE_C, N_PIECES + 1).astype(F32)
        for h in range(N_HEADS):
            blk = slice((h // 2) * LANES, (h // 2 + 1) * LANES)
            head = (lambda a: pltpu.roll(a[:, blk], HEAD_DIM, 1)) if h % 2 else (lambda a: a[:, blk])
            mine = slice(h * LANES, (h + 1) * LANES)
            qx_ref[h] = jnp.where(low, head(qh), cq[:, mine] + ones_q).astype(BF16)
            kx_ref[h] = jnp.where(low, head(kh), ones_k - ck[:, mine]).astype(BF16)
            vx_ref[h] = jnp.where(low, head(vf), ones_v).astype(BF16)

    heads = pl.BlockSpec((N_HEADS, TM, LANES), lambda i: (0, i, 0))
    out = jax.ShapeDtypeStruct((N_HEADS, S, LANES), BF16)
    return pl.pallas_call(
        body,
        grid=(S // TM,),
        in_specs=[pl.BlockSpec((TM, D_ATTN), lambda i: (i, 0)), pl.BlockSpec((TM, D_ATTN), lambda i: (i, 1)),
                  pl.BlockSpec((TM, D_ATTN), lambda i: (i, 2)),
                  _rows(TM, D_F), _whole(gq), _whole(gk), _whole(fb), _whole(bd), _whole(tri), _whole(place_q), _whole(place_k)],
        out_specs=[heads, heads, heads],
        out_shape=[out, out, out],
        scratch_shapes=[pltpu.VMEM((1, D_F), F32)],
        compiler_params=_params("arbitrary"),
        name="qk_prep",
    )(pa, pa, pa, pf, gq, gk, fb, bd, tri, place_q, place_k)


def attn_fwd(qx, kx, vx):
    S = qx.shape[1]
    nq = S // TQ

    def body(q_ref, k_ref, v_ref, o_ref, qb_ref):
        i = pl.program_id(1)
        lane = lax.broadcasted_iota(jnp.int32, (TQ, LANES), 1)
        row = lax.broadcasted_iota(jnp.int32, (TQ, TK), 0)
        col = lax.broadcasted_iota(jnp.int32, (TQ, TK), 1)
        q = [q_ref[0], q_ref[1]]
        n_full = (i * TQ) // TK

        def step(kt, carry, masked):
            ks = pl.multiple_of(kt * TK, TK)
            new = []
            for j in range(2):
                m, acc = carry[j]
                z = _dot_nt(q[j], k_ref[j, pl.ds(ks, TK), :])
                if masked:
                    z = jnp.where(row + i * TQ >= col + ks, z, NEG)
                m_new = jnp.maximum(m, jnp.max(z, axis=1, keepdims=True))
                pr = jnp.exp2(z - m_new)
                acc = jnp.exp2(m - m_new) * acc + _dot(pr.astype(BF16), v_ref[j, pl.ds(ks, TK), :])
                new.append((m_new, acc))
            return tuple(new)

        init = tuple((jnp.full((TQ, 1), NEG, F32), jnp.zeros((TQ, LANES), F32)) for _ in range(2))
        carry = lax.fori_loop(0, n_full, functools.partial(step, masked=False), init)
        carry = step(n_full, carry, True)
        for j in range(2):
            m, acc = carry[j]
            l = jnp.sum(jnp.where(lane == LANE_C, acc, 0.0), axis=1, keepdims=True)
            o_ref[j] = acc / l
            n1, n2, n3 = _pieces(-(m + jnp.log(l) * LOG2E))
            qb_ref[j] = jnp.where(lane == LANE_LSE, n1,
                                  jnp.where(lane == LANE_LSE + 1, n2, jnp.where(lane == LANE_LSE + 2, n3, q[j])))

    pair_tile = pl.BlockSpec((2, TQ, LANES), lambda p, i: (p, i, 0))
    pair_all = pl.BlockSpec((2, S, LANES), lambda p, i: (p, 0, 0))
    return pl.pallas_call(
        body,
        grid=(N_HEADS // 2, nq),
        in_specs=[pair_tile, pair_all, pair_all],
        out_specs=[pair_tile, pair_tile],
        out_shape=[jax.ShapeDtypeStruct((N_HEADS, S, LANES), F32), jax.ShapeDtypeStruct((N_HEADS, S, LANES), BF16)],
        compiler_params=_params("parallel", "parallel"),
        name="attn_fwd",
    )(qx, kx, vx)


def attn_bwd(qxb, kx, vx, dox):
    S = qxb.shape[1]
    nq = S // QT

    def body(q_ref, k_ref, v_ref, do_ref, dq_ref, dk_ref, dv_ref):
        kt = pl.program_id(1)

        @pl.when(kt == 0)
        def _():
            dq_ref[...] = jnp.zeros_like(dq_ref)

        row = lax.broadcasted_iota(jnp.int32, (QT, KB), 0)
        col = lax.broadcasted_iota(jnp.int32, (QT, KB), 1)
        k = [k_ref[0], k_ref[1]]
        v = [v_ref[0], v_ref[1]]
        q_first = (kt * KB) // QT

        def step(qi, carry, masked):
            qs = pl.multiple_of(qi * QT, QT)
            new = []
            for j in range(2):
                dk, dv = carry[j]
                q = q_ref[j, pl.ds(qs, QT), :]
                dout = do_ref[j, pl.ds(qs, QT), :]
                z = _dot_nt(q, k[j])
                if masked:
                    z = jnp.where(row + qs >= col + kt * KB, z, NEG)
                pr = jnp.exp2(z)
                dv = dv + _dot_tn(pr.astype(BF16), dout)
                dsb = (pr * _dot_nt(dout, v[j])).astype(BF16)
                dk = dk + _dot_tn(dsb, q)
                dq_ref[j, pl.ds(qs, QT), :] += _dot(dsb, k[j])
                new.append((dk, dv))
            return tuple(new)

        zero = jnp.zeros((KB, LANES), F32)
        carry = step(q_first, ((zero, zero), (zero, zero)), True)
        carry = lax.fori_loop(q_first + 1, nq, functools.partial(step, masked=False), carry)
        for j in range(2):
            dk_ref[j] = carry[j][0]
            dv_ref[j] = carry[j][1]

    pair_tile = pl.BlockSpec((2, KB, LANES), lambda p, kt: (p, kt, 0))
    pair_all = pl.BlockSpec((2, S, LANES), lambda p, kt: (p, 0, 0))
    out = jax.ShapeDtypeStruct((N_HEADS, S, LANES), F32)
    return pl.pallas_call(
        body,
        grid=(N_HEADS // 2, S // KB),
        in_specs=[pair_all, pair_tile, pair_tile, pair_all],
        out_specs=[pair_all, pair_tile, pair_tile],
        out_shape=[out, out, out],
        compiler_params=_params("arbitrary", "arbitrary"),
        name="attn_bwd",
    )(qxb, kx, vx, dox)


def attn_bwd_post(pa, pf, dqx, dkx, dvx, gq, gk, fb):
    S = pa.shape[0]
    nt = S // TM
    bd = _head_mean_matrix()
    triu = _tri(TM, upper=True)
    rev = lambda i: nt - 1 - i

    def body(q_ref, k_ref, pf_ref, dqx_ref, dkx_ref, dvx_ref, gq_ref, gk_ref, fb_ref, bd_ref, triu_ref,
             dpa_ref, dpf_ref, dgq_ref, dgk_ref, dfb_ref, carry):
        @pl.when(pl.program_id(0) == 0)
        def _():
            carry[...] = jnp.zeros_like(carry)
            dgq_ref[...] = jnp.zeros_like(dgq_ref)
            dgk_ref[...] = jnp.zeros_like(dgk_ref)
            dfb_ref[...] = jnp.zeros_like(dfb_ref)

        lane = lax.broadcasted_iota(jnp.int32, (TM, LANES), 1)

        def columns(ref):
            return jnp.concatenate([jnp.where(lane < HEAD_DIM, ref[2 * p], pltpu.roll(ref[2 * p + 1], HEAD_DIM, 1))
                                    for p in range(N_HEADS // 2)], axis=1)

        def head_norm_bwd(x_ref, dy, g_ref, dg_ref):
            xf = x_ref[...].astype(F32)
            r = lax.rsqrt(_dot((xf * xf).astype(BF16), bd_ref[...]) + EPS)
            xhat = xf * r
            dg_ref[...] += jnp.sum(dy * xhat, axis=0, keepdims=True)
            gdy = dy * g_ref[...]
            return (r * (gdy - xhat * _dot((xhat * gdy).astype(BF16), bd_ref[...]))).astype(BF16)

        dpa_ref[:, 0:D_ATTN] = head_norm_bwd(q_ref, columns(dqx_ref) * HEAD_DIM ** -0.5, gq_ref, dgq_ref)
        dpa_ref[:, D_ATTN:2 * D_ATTN] = head_norm_bwd(k_ref, columns(dkx_ref) * LN2, gk_ref, dgk_ref)
        dpa_ref[:, 2 * D_ATTN:3 * D_ATTN] = columns(dvx_ref).astype(BF16)

        dc = jnp.zeros((TM, LANES), F32)
        for h in range(N_HEADS):
            both = jnp.where(lane == LANE_C, dqx_ref[h], 0.0) - jnp.where(lane == LANE_ONE, dkx_ref[h], 0.0)
            dc = jnp.where(lane == h, jnp.sum(both, axis=1, keepdims=True), dc)
        hi, lo = _split_bf16(dc)
        dlogf = _dot(triu_ref[...], hi) + _dot(triu_ref[...], lo) + carry[...]
        first = lax.broadcasted_iota(jnp.int32, (TM, D_F), 0) == 0
        carry[...] = jnp.sum(jnp.where(first, dlogf, 0.0), axis=0, keepdims=True)
        df = dlogf * _sigmoid(-(pf_ref[...] + fb_ref[...]))
        dfb_ref[...] += jnp.sum(df, axis=0, keepdims=True)
        dpf_ref[...] = df.astype(BF16)

    heads = pl.BlockSpec((N_HEADS, TM, LANES), lambda i: (0, rev(i), 0))
    return pl.pallas_call(
        body,
        grid=(nt,),
        in_specs=[pl.BlockSpec((TM, D_ATTN), lambda i: (rev(i), 0)), pl.BlockSpec((TM, D_ATTN), lambda i: (rev(i), 1)),
                  pl.BlockSpec((TM, D_F), lambda i: (rev(i), 0)), heads, heads, heads,
                  _whole(gq), _whole(gk), _whole(fb), _whole(bd), _whole(triu)],
        out_specs=[pl.BlockSpec((TM, D_QKV), lambda i: (rev(i), 0)), pl.BlockSpec((TM, D_F), lambda i: (rev(i), 0)),
                   pl.BlockSpec((1, D_ATTN), lambda i: (0, 0)), pl.BlockSpec((1, D_ATTN), lambda i: (0, 0)),
                   pl.BlockSpec((1, D_F), lambda i: (0, 0))],
        out_shape=[jax.ShapeDtypeStruct((S, D_QKV), BF16), jax.ShapeDtypeStruct((S, D_F), BF16),
                   jax.ShapeDtypeStruct((1, D_ATTN), F32), jax.ShapeDtypeStruct((1, D_ATTN), F32),
                   jax.ShapeDtypeStruct((1, D_F), F32)],
        scratch_shapes=[pltpu.VMEM((1, D_F), F32)],
        compiler_params=_params("arbitrary"),
        name="attn_bwd_post",
    )(pa, pa, pf, dqx, dkx, dvx, gq, gk, fb, bd, triu)


def _shift_down(ext, k):
    return pltpu.roll(ext, k, 0)[HALO:]


def _shift_up(ext, k, n):
    return pltpu.roll(ext, n + HALO - k, 0)[:n]


def _pool_lane_select(a2, a4, a8, a16, lane):
    return jnp.where(lane < 64, a2, jnp.where(lane < 128, a4, jnp.where(lane < 192, a8, a16)))


def _local_branches(o, pb, halo, have_prev, row0, wao, convw, wco, wpool, pscale):
    n = pb.shape[0]
    cx = pb[:, 0:D_CONV].astype(F32)
    cb = pb[:, D_CONV:2 * D_CONV].astype(F32)
    cc = pb[:, 2 * D_CONV:3 * D_CONV].astype(F32)
    px = pb[:, 3 * D_CONV:D_LOCAL].astype(F32)
    keep = have_prev.astype(F32)
    z = cc * cx
    z_ext = jnp.concatenate([halo[:, 2 * D_CONV:3 * D_CONV].astype(F32) * halo[:, 0:D_CONV].astype(F32) * keep, z], axis=0)
    z1 = _shift_down(z_ext, 1)
    z2 = _shift_down(z_ext, 2)
    conv = convw[0:1, :] * z2 + convw[1:2, :] * z1 + convw[2:3, :] * z
    cm = cb * conv

    u_ext = jnp.concatenate([halo[:, 3 * D_CONV:D_LOCAL].astype(F32) * keep, px], axis=0)
    s2 = u_ext + pltpu.roll(u_ext, 1, 0)
    s4 = s2 + pltpu.roll(s2, 2, 0)
    s8 = s4 + pltpu.roll(s4, 4, 0)
    s16 = s8 + pltpu.roll(s8, 8, 0)
    lane = lax.broadcasted_iota(jnp.int32, (n, D_POOL), 1)
    win = _pool_lane_select(2.0, 4.0, 8.0, 16.0, lane)
    t = (row0 + lax.broadcasted_iota(jnp.int32, (n, D_POOL), 0)).astype(F32)
    cnt = jnp.minimum(t + 1.0, win)
    feat = _pool_lane_select(s2[HALO:], s4[HALO:], s8[HALO:], s16[HALO:], lane) / cnt - px

    ya = _dot(o, wao)
    yc = _dot(cm.astype(BF16), wco)
    yp_pre = _dot(feat.astype(BF16), wpool)
    yp = yp_pre * pscale
    return dict(cx=cx, cb=cb, cc=cc, z=z, z1=z1, z2=z2, conv=conv, cm=cm, feat=feat, cnt=cnt, lane=lane,
                ya=ya, yc=yc, yp_pre=yp_pre, yp=yp)


def _halo_spec(tm, tile_of):
    per = tm // HALO
    return pl.BlockSpec((HALO, D_LOCAL), lambda i, *_: (jnp.maximum(tile_of(i) * per - 1, 0), 0))


def _heads_as_columns(ref):
    return jnp.concatenate([ref[h] for h in range(N_HEADS)], axis=1)


def mix_out_fwd(x, ox, pb, wao, convw, wco, wpool, pscale, wo_all, l):
    S = x.shape[0]
    tm = TM_MIX

    def body(l_ref, x_ref, o_ref, pb_ref, halo_ref, wao_ref, cw_ref, wco_ref, wp_ref, ps_ref, wo_ref, y_ref):
        i = pl.program_id(0)
        pb = pb_ref[...]
        o = _heads_as_columns(o_ref).astype(BF16)
        b = _local_branches(o, pb, halo_ref[...], i > 0, i * tm, wao_ref[...], cw_ref[...], wco_ref[...],
                            wp_ref[...], ps_ref[...])
        g0 = _sigmoid(pb[:, D_LOCAL:D_LOCAL + D_MODEL].astype(F32))
        g1 = _sigmoid(pb[:, D_LOCAL + D_MODEL:D_LOCAL + 2 * D_MODEL].astype(F32))
        g2 = _sigmoid(pb[:, D_LOCAL + 2 * D_MODEL:D_B].astype(F32))
        merged = g0 * b["ya"] + g1 * b["yc"] + g2 * b["yp"]
        y_ref[...] = x_ref[...] + _dot(merged.astype(BF16), wo_ref[...])

    return pl.pallas_call(
        body,
        grid_spec=pltpu.PrefetchScalarGridSpec(
            num_scalar_prefetch=1,
            grid=(S // tm,),
            in_specs=[_rows(tm, D_MODEL), pl.BlockSpec((N_HEADS, tm, LANES), lambda i, l: (0, i, 0)), _rows(tm, D_B),
                      _halo_spec(tm, lambda i: i), _whole(wao), _whole(convw), _whole(wco), _whole(wpool), _whole(pscale),
                      _layer((D_MODEL, D_MODEL))],
            out_specs=_rows(tm, D_MODEL),
        ),
        out_shape=jax.ShapeDtypeStruct((S, D_MODEL), F32),
        compiler_params=_params("parallel"),
        name="mix_out_fwd",
    )(_layer_index(l), x, ox, pb, pb, wao, convw, wco, wpool, pscale, wo_all)


def mix_out_bwd(dxm, ox, pb, wao, convw, wco, wpool, pscale, wo_all, l):
    S = dxm.shape[0]
    tm = TM_MIX
    nt = S // tm
    rev = lambda i: nt - 1 - i
    rows = lambda n: pl.BlockSpec((tm, n), lambda i, l: (rev(i), 0))
    heads = pl.BlockSpec((N_HEADS, tm, LANES), lambda i, l: (0, rev(i), 0))
    acc = lambda r, c: pl.BlockSpec((r, c), lambda i, l: (0, 0))

    def body(l_ref, dxm_ref, o_ref, pb_ref, halo_ref, wao_ref, cw_ref, wco_ref, wp_ref, ps_ref, wo_ref,
             dpb_ref, do_ref, dwo_ref, dwao_ref, dwco_ref, dwp_ref, dcw_ref, dps_ref, next_dconv, next_e):
        i = pl.program_id(0)
        r = rev(i)

        @pl.when(i == 0)
        def _():
            for ref in (dwo_ref, dwao_ref, dwco_ref, dwp_ref, dcw_ref, dps_ref, next_dconv, next_e):
                ref[...] = jnp.zeros_like(ref)

        pb = pb_ref[...]
        o = _heads_as_columns(o_ref).astype(BF16)
        cw = cw_ref[...]
        b = _local_branches(o, pb, halo_ref[...], r > 0, r * tm, wao_ref[...], cw, wco_ref[...], wp_ref[...], ps_ref[...])
        g0 = _sigmoid(pb[:, D_LOCAL:D_LOCAL + D_MODEL].astype(F32))
        g1 = _sigmoid(pb[:, D_LOCAL + D_MODEL:D_LOCAL + 2 * D_MODEL].astype(F32))
        g2 = _sigmoid(pb[:, D_LOCAL + 2 * D_MODEL:D_B].astype(F32))
        dxb = dxm_ref[...].astype(BF16)
        merged = g0 * b["ya"] + g1 * b["yc"] + g2 * b["yp"]
        dwo_ref[...] += _dot_tn(merged.astype(BF16), dxb)
        dmer = _dot_nt(dxb, wo_ref[...])
        dpb_ref[:, D_LOCAL:D_LOCAL + D_MODEL] = (dmer * b["ya"] * (g0 * (1.0 - g0))).astype(BF16)
        dpb_ref[:, D_LOCAL + D_MODEL:D_LOCAL + 2 * D_MODEL] = (dmer * b["yc"] * (g1 * (1.0 - g1))).astype(BF16)
        dpb_ref[:, D_LOCAL + 2 * D_MODEL:D_B] = (dmer * b["yp"] * (g2 * (1.0 - g2))).astype(BF16)

        dya = (dmer * g0).astype(BF16)
        dwao_ref[...] += _dot_tn(o, dya)
        da = _dot_nt(dya, wao_ref[...]).astype(BF16)
        lane = lax.broadcasted_iota(jnp.int32, (tm, LANES), 1)
        for h in range(N_HEADS):
            dah = da[:, h * LANES:(h + 1) * LANES]
            d1, d2, d3 = _pieces(-jnp.sum(dah.astype(F32) * o_ref[h], axis=1, keepdims=True))
            do_ref[h] = jnp.where(lane == LANE_C + 1, d1, jnp.where(lane == LANE_C + 2, d2,
                                                                  jnp.where(lane == LANE_C + 3, d3, dah)))

        dyc = (dmer * g1).astype(BF16)
        dwco_ref[...] += _dot_tn(b["cm"].astype(BF16), dyc)
        dcm = _dot_nt(dyc, wco_ref[...])
        dconv = dcm * b["cb"]
        dcw_ref[0:1, :] += jnp.sum(dconv * b["z2"], axis=0, keepdims=True)
        dcw_ref[1:2, :] += jnp.sum(dconv * b["z1"], axis=0, keepdims=True)
        dcw_ref[2:3, :] += jnp.sum(dconv * b["z"], axis=0, keepdims=True)
        d_ext = jnp.concatenate([dconv, next_dconv[...]], axis=0)
        dz = cw[2:3, :] * dconv + cw[1:2, :] * _shift_up(d_ext, 1, tm) + cw[0:1, :] * _shift_up(d_ext, 2, tm)
        next_dconv[...] = dconv[0:HALO]
        dpb_ref[:, 0:D_CONV] = (dz * b["cc"]).astype(BF16)
        dpb_ref[:, D_CONV:2 * D_CONV] = (dcm * b["conv"]).astype(BF16)
        dpb_ref[:, 2 * D_CONV:3 * D_CONV] = (dz * b["cx"]).astype(BF16)

        dyp = dmer * g2
        dps_ref[...] += jnp.sum(dyp * b["yp_pre"], axis=0, keepdims=True)
        dyps = (dyp * ps_ref[...]).astype(BF16)
        dwp_ref[...] += _dot_tn(b["feat"].astype(BF16), dyps)
        dfeat = _dot_nt(dyps, wp_ref[...])
        e = dfeat / b["cnt"]
        e_ext = jnp.concatenate([e, next_e[...]], axis=0)
        up = lambda a, k: pltpu.roll(a, tm + HALO - k, 0)
        f2 = e_ext + up(e_ext, 1)
        f4 = f2 + up(f2, 2)
        f8 = f4 + up(f4, 4)
        f16 = f8 + up(f8, 8)
        next_e[...] = e[0:HALO]
        dpb_ref[:, 3 * D_CONV:D_LOCAL] = (_pool_lane_select(f2[:tm], f4[:tm], f8[:tm], f16[:tm], b["lane"]) - dfeat).astype(BF16)

    return pl.pallas_call(
        body,
        grid_spec=pltpu.PrefetchScalarGridSpec(
            num_scalar_prefetch=1,
            grid=(nt,),
            in_specs=[rows(D_MODEL), heads, rows(D_B), _halo_spec(tm, rev),
                      _whole(wao), _whole(convw), _whole(wco), _whole(wpool), _whole(pscale), _layer((D_MODEL, D_MODEL))],
            out_specs=[rows(D_B), heads, acc(D_MODEL, D_MODEL), acc(D_HEADS, D_MODEL), acc(D_CONV, D_MODEL),
                       acc(D_POOL, D_MODEL), acc(8, D_CONV), acc(1, D_MODEL)],
            scratch_shapes=[pltpu.VMEM((HALO, D_CONV), F32), pltpu.VMEM((HALO, D_POOL), F32)],
        ),
        out_shape=[jax.ShapeDtypeStruct((S, D_B), BF16), jax.ShapeDtypeStruct((N_HEADS, S, LANES), BF16),
                   jax.ShapeDtypeStruct((D_MODEL, D_MODEL), F32), jax.ShapeDtypeStruct((D_HEADS, D_MODEL), F32),
                   jax.ShapeDtypeStruct((D_CONV, D_MODEL), F32), jax.ShapeDtypeStruct((D_POOL, D_MODEL), F32),
                   jax.ShapeDtypeStruct((8, D_CONV), F32), jax.ShapeDtypeStruct((1, D_MODEL), F32)],
        compiler_params=_params("arbitrary"),
        name="mix_out_bwd",
    )(_layer_index(l), dxm, ox, pb, pb, wao, convw, wco, wpool, pscale, wo_all)


FF_SHARD = 2 * D_FF // 4


def ffn_fwd(x, g, w1_all, w2_all, l):
    S = x.shape[0]

    def body(l_ref, x_ref, g_ref, w1_ref, w2_ref, y_ref, u_ref):
        xf = x_ref[...]
        r = lax.rsqrt(jnp.mean(xf * xf, axis=-1, keepdims=True) + EPS)
        h = (xf * r * g_ref[...]).astype(BF16)
        u = jnp.concatenate([_dot(h, w1_ref[j]) for j in range(4)], axis=1)
        u_ref[...] = u.astype(BF16)
        gt = u[:, 0:D_FF]
        act = gt * _sigmoid(gt) * u[:, D_FF:2 * D_FF]
        y_ref[...] = xf + _dot(act.astype(BF16), w2_ref[...])

    return pl.pallas_call(
        body,
        grid_spec=pltpu.PrefetchScalarGridSpec(
            num_scalar_prefetch=1,
            grid=(S // TM,),
            in_specs=[_rows(TM, D_MODEL), _whole(g), _layer((4, D_MODEL, FF_SHARD)), _layer((D_FF, D_MODEL))],
            out_specs=[_rows(TM, D_MODEL), _rows(TM, 2 * D_FF)],
        ),
        out_shape=[jax.ShapeDtypeStruct((S, D_MODEL), F32), jax.ShapeDtypeStruct((S, 2 * D_FF), BF16)],
        compiler_params=_params("parallel"),
        name="ffn_fwd",
    )(_layer_index(l), x, g, w1_all, w2_all)


def ffn_bwd(x, g, dy, u, w1_all, w2_all, l):
    S = x.shape[0]

    def body(l_ref, x_ref, g_ref, dy_ref, u_ref, w1_ref, w2_ref, dx_ref, du_ref, act_ref, h_ref, dyb_ref, dg_ref):
        @pl.when(pl.program_id(0) == 0)
        def _():
            dg_ref[...] = jnp.zeros_like(dg_ref)

        dyf = dy_ref[...]
        dyb_ref[...] = dyf.astype(BF16)
        dact = _dot_nt(dyb_ref[...], w2_ref[...])
        gt = u_ref[:, 0:D_FF].astype(F32)
        up = u_ref[:, D_FF:2 * D_FF].astype(F32)
        sg = _sigmoid(gt)
        silu = gt * sg
        act_ref[...] = (silu * up).astype(BF16)
        du_ref[:, 0:D_FF] = (dact * up * (sg * (1.0 + gt * (1.0 - sg)))).astype(BF16)
        du_ref[:, D_FF:2 * D_FF] = (dact * silu).astype(BF16)
        dh = _dot_nt(du_ref[:, 0:FF_SHARD], w1_ref[0])
        for j in range(1, 4):
            dh = dh + _dot_nt(du_ref[:, j * FF_SHARD:(j + 1) * FF_SHARD], w1_ref[j])
        xf = x_ref[...]
        r = lax.rsqrt(jnp.mean(xf * xf, axis=-1, keepdims=True) + EPS)
        xhat = xf * r
        h_ref[...] = (xhat * g_ref[...]).astype(BF16)
        dg_ref[...] += jnp.sum(dh * xhat, axis=0, keepdims=True)
        gdh = dh * g_ref[...]
        dx_ref[...] = dyf + r * (gdh - xhat * jnp.mean(xhat * gdh, axis=-1, keepdims=True))

    return pl.pallas_call(
        body,
        grid_spec=pltpu.PrefetchScalarGridSpec(
            num_scalar_prefetch=1,
            grid=(S // TM,),
            in_specs=[_rows(TM, D_MODEL), _whole(g), _rows(TM, D_MODEL), _rows(TM, 2 * D_FF),
                      _layer((4, D_MODEL, FF_SHARD)), _layer((D_FF, D_MODEL))],
            out_specs=[_rows(TM, D_MODEL), _rows(TM, 2 * D_FF), _rows(TM, D_FF), _rows(TM, D_MODEL), _rows(TM, D_MODEL),
                       pl.BlockSpec((1, D_MODEL), lambda i, l: (0, 0))],
        ),
        out_shape=[jax.ShapeDtypeStruct((S, D_MODEL), F32), jax.ShapeDtypeStruct((S, 2 * D_FF), BF16),
                   jax.ShapeDtypeStruct((S, D_FF), BF16), jax.ShapeDtypeStruct((S, D_MODEL), BF16),
                   jax.ShapeDtypeStruct((S, D_MODEL), BF16), jax.ShapeDtypeStruct((1, D_MODEL), F32)],
        compiler_params=_params("arbitrary"),
        name="ffn_bwd",
    )(_layer_index(l), x, g, dy, u, w1_all, w2_all)


def loss_head(y, target):
    S = y.shape[0]

    def body(y_ref, t_ref, loss_ref, dy_ref):
        @pl.when(pl.program_id(0) == 0)
        def _():
            loss_ref[0, 0] = 0.0

        err = y_ref[...] - t_ref[...]
        dy_ref[...] = err * (1.0 / D_MODEL)
        loss_ref[0, 0] += 0.5 * jnp.sum(jnp.mean(err * err, axis=-1))

    return pl.pallas_call(
        body,
        grid=(S // TM,),
        in_specs=[_rows(TM, D_MODEL), _rows(TM, D_MODEL)],
        out_specs=[pl.BlockSpec((1, 1), lambda i: (0, 0), memory_space=pltpu.SMEM), _rows(TM, D_MODEL)],
        out_shape=[jax.ShapeDtypeStruct((1, 1), F32), jax.ShapeDtypeStruct((S, D_MODEL), F32)],
        compiler_params=_params("arbitrary"),
        name="loss_head",
    )(y, target)


SPLIT = {
    "w_in": ((D_MODEL, D_IN), 1),
    "w_attn_out": ((D_ATTN, D_MODEL), 1),
    "w_conv_out": ((D_CONV, D_MODEL), 1),
    "pool_w": ((N_GROUPS, D_POOL // N_GROUPS, D_MODEL // N_GROUPS), 2),
    "w_o": ((D_MODEL, D_MODEL), 0),
    "w_ffn_in": ((D_MODEL, 2 * D_FF), 1),
    "w_ffn_out": ((D_FF, D_MODEL), 0),
}
SMALL = {"norm_mix_g": D_MODEL, "forget_b": N_HEADS, "q_norm_g": HEAD_DIM, "k_norm_g": HEAD_DIM,
         "pool_scale": D_MODEL, "norm_ffn_g": D_MODEL, "conv_w": CONV_K * D_CONV}
WEIGHTS = ["norm_mix_g", "w_in", "forget_b", "q_norm_g", "k_norm_g", "w_attn_out", "conv_w", "w_conv_out", "pool_w",
           "pool_scale", "w_o", "norm_ffn_g", "w_ffn_in", "w_ffn_out"]


MIX_W = ["w_in", "w_attn_out", "w_conv_out", "pool_w", "w_o"]
FFN_W = ["w_ffn_in", "w_ffn_out"]


def _row(a):
    return a.astype(F32).reshape(1, -1)


def mix_fwd(x, full, li, small):
    n = full["w_o"].shape[0]
    wa, wf, wb = w_in_prep(full["w_in"], li)
    wao, wco, wpool, convw = branch_w_prep(full["w_attn_out"], full["w_conv_out"], full["pool_w"], full["conv_w"], li)
    w = dict(wa=wa, wf=wf, wb=wb, wao=wao, wco=wco, wpool=wpool, convw=convw,
             w_o=full["w_o"].reshape(n, D_MODEL, D_MODEL), at=li,
             g1=_row(small["norm_mix_g"]), pscale=_row(small["pool_scale"]),
             gq=_row(jnp.tile(small["q_norm_g"], N_HEADS)), gk=_row(jnp.tile(small["k_norm_g"], N_HEADS)),
             fb=_row(jnp.pad(small["forget_b"], (0, D_F - N_HEADS))))
    h, pa, pf, pb = in_proj_fwd(x, w["g1"], wa, wf, wb)
    qx, kx, vx = qk_prep(pa, pf, w["gq"], w["gk"], w["fb"])
    ox, qxb = attn_fwd(qx, kx, vx)
    xm = mix_out_fwd(x, ox, pb, wao, convw, wco, wpool, w["pscale"], w["w_o"], li)
    return xm, w, dict(x=x, h=h, pa=pa, pf=pf, pb=pb, kx=kx, vx=vx, ox=ox, qxb=qxb)


def ffn_half_fwd(xm, full, li, g2):
    n = full["w_ffn_out"].shape[0]
    w = dict(w1=full["w_ffn_in"], w2=full["w_ffn_out"].reshape(n, D_FF, D_MODEL), at=li, g2=_row(g2))
    y, u = ffn_fwd(xm, w["g2"], w["w1"], w["w2"], li)
    return y, w, dict(xm=xm, u=u)


def ffn_half_bwd(dx, w, s, gi, big):
    n = big["w_ffn_out"].shape[0]
    big = dict(big)
    dxm, du, act, h2, dyb, dg2 = ffn_bwd(s["xm"], w["g2"], dx, s["u"], w["w1"], w["w2"], w["at"])
    big["w_ffn_out"] = wgrad_into(act, dyb, "wgrad_ffn_out", big["w_ffn_out"].reshape(n, D_FF, D_MODEL),
                                  gi).reshape(big["w_ffn_out"].shape)
    big["w_ffn_in"] = wgrad_into(h2, du, "wgrad_ffn_in", big["w_ffn_in"], gi)
    return dxm, big, dict(norm_ffn_g=dg2[0])


def mix_bwd_weights(dxm, w, s, gi, big):
    big = dict(big)
    dpb, dox, dwo, dwao, dwco, dwpool, dconvw, dpscale = mix_out_bwd(
        dxm, s["ox"], s["pb"], w["wao"], w["convw"], w["wco"], w["wpool"], w["pscale"], w["w_o"], w["at"])
    dqx, dkx, dvx = attn_bwd(s["qxb"], s["kx"], s["vx"], dox)
    dpa, dpf, dgq, dgk, dfb = attn_bwd_post(s["pa"], s["pf"], dqx, dkx, dvx, w["gq"], w["gk"], w["fb"])
    big["w_in"] = w_in_unprep(wgrad(s["h"], dpa, "wgrad_in_qkv"), wgrad(s["h"], dpf, "wgrad_in_f"),
                              wgrad(s["h"], dpb, "wgrad_in_b"), big["w_in"], gi)
    big["w_attn_out"], big["w_conv_out"], big["pool_w"], big["w_o"] = branch_g_place(
        dwao, dwco, dwpool, dwo, (big["w_attn_out"], big["w_conv_out"], big["pool_w"], big["w_o"]), gi)
    sm = dict(forget_b=dfb[0, 0:N_HEADS], q_norm_g=dgq.reshape(N_HEADS, HEAD_DIM).sum(0),
              k_norm_g=dgk.reshape(N_HEADS, HEAD_DIM).sum(0), pool_scale=dpscale[0],
              conv_w=dconvw[0:CONV_K].reshape(-1))
    return (dpa, dpf, dpb), big, sm


def mix_bwd_input(dproj, dxm, w, s):
    dx, dg1 = in_proj_bwd(s["x"], w["g1"], dxm, *dproj, w["wa"], w["wf"], w["wb"])
    return dx, dict(norm_mix_g=dg1[0])


def mix_bwd(dxm, w, s, gi, big):
    dproj, big, sm = mix_bwd_weights(dxm, w, s, gi, big)
    dx, g1 = mix_bwd_input(dproj, dxm, w, s)
    return dx, big, {**sm, **g1}


def grad_buffers(full):
    return {n: lax.empty(full[n].shape, F32) for n in SPLIT}


def local_step(x, target, gathered, small):
    n_layers = small["norm_mix_g"].shape[0]
    done = []
    for l in range(n_layers):
        xm, wm, sm = mix_fwd(x, gathered, l, {n: v[l] for n, v in small.items()})
        x, wf, sf = ffn_half_fwd(xm, gathered, l, small["norm_ffn_g"][l])
        done.append((wm, sm, wf, sf))
    loss, dx = loss_head(x, target)
    big = grad_buffers(gathered)
    small_grads = [None] * n_layers
    for l in reversed(range(n_layers)):
        wm, sm, wf, sf = done[l]
        dxm, ffn_big, g_ffn = ffn_half_bwd(dx, wf, sf, l, {n: big[n] for n in FFN_W})
        dx, mix_big, g_mix = mix_bwd(dxm, wm, sm, l, {n: big[n] for n in MIX_W})
        big = {**ffn_big, **mix_big}
        small_grads[l] = {**g_ffn, **g_mix}
    return loss, dx, big, {n: jnp.stack([g[n] for g in small_grads]) for n in SMALL}


def adamw(w, g, m, v, name):
    R, C = w.shape
    tm = 256 if R % 256 == 0 else R

    def body(w_ref, g_ref, m_ref, v_ref, d_ref, nm_ref, nv_ref):
        gr = g_ref[...]
        m_new = ADAM_B1 * m_ref[...] + (1.0 - ADAM_B1) * gr
        v_new = ADAM_B2 * v_ref[...] + (1.0 - ADAM_B2) * jnp.square(gr)
        nm_ref[...] = m_new
        nv_ref[...] = v_new
        m_hat = m_new / (1.0 - ADAM_B1 ** ADAM_STEP)
        v_hat = v_new / (1.0 - ADAM_B2 ** ADAM_STEP)
        d_ref[...] = -ADAM_LR * (m_hat / (jnp.sqrt(v_hat) + ADAM_EPS) + ADAM_WD * w_ref[...])

    spec = _rows(tm, C)
    out = jax.ShapeDtypeStruct((R, C), F32)
    return pl.pallas_call(
        body,
        grid=(R // tm,),
        in_specs=[spec] * 4,
        out_specs=[spec] * 3,
        out_shape=[out] * 3,
        compiler_params=_params("parallel"),
        name=name,
    )(w, g, m, v)


MESH = pl.DeviceIdType.MESH
HBM_REF = pl.BlockSpec(memory_space=pl.ANY)
N_CHIPS = 4
N_DEV = 8
SMALL_SHAPE = (128, LANES)


def _mesh_pos():
    return lax.axis_index("x"), lax.axis_index("y"), lax.axis_index("c")


def _other_chips(x, y):
    return [(1 - x, y), (x, 1 - y), (1 - x, 1 - y)]


def _remote(src, dst, send_sem, recv_sem, to):
    return pltpu.make_async_remote_copy(src_ref=src, dst_ref=dst, send_sem=send_sem, recv_sem=recv_sem,
                                        device_id=to, device_id_type=MESH)


def _row_tile(rows):
    for tm in (256, 176, 128):
        if rows % tm == 0:
            return tm
    return rows


def _as4(a):
    return a.reshape(a.shape[0], a.shape[1], -1, a.shape[-1])


def _core_rows(buf, c):
    R = buf.shape[2]
    if R % 2:
        return pl.ds(0, R), pl.ds(0, R), False
    return pl.ds(c * (R // 2), R // 2), pl.ds((1 - c) * (R // 2), R // 2), True


def place_shard(chip, w, lo, hi, dtype, name, after=None):
    w3 = w.reshape(w.shape[0], -1, w.shape[-1])
    _, R, C = w3.shape
    tm = _row_tile(R)
    idle = [] if after is None else [after]

    def body(chip_ref, w_ref, *rest):
        rest[-1][...] = w_ref[...].astype(dtype)

    return pl.pallas_call(
        body,
        grid_spec=pltpu.PrefetchScalarGridSpec(
            num_scalar_prefetch=1,
            grid=(hi - lo, R // tm),
            in_specs=[pl.BlockSpec((None, tm, C), lambda l, i, chip: (lo + l, i, 0))] + [_whole(a) for a in idle],
            out_specs=pl.BlockSpec((None, None, tm, C), lambda l, i, chip: (l, chip[0], i, 0)),
        ),
        out_shape=jax.ShapeDtypeStruct((hi - lo, N_CHIPS, R, C), dtype),
        compiler_params=_params("parallel", "parallel"),
        name=name,
    )(chip, w3, *idle)


HBM_SPACE = pl.BlockSpec(memory_space=pltpu.HBM)
SEM_SPACE = pl.BlockSpec(memory_space=pltpu.SEMAPHORE)
IN_FLIGHT = pltpu.SideEffectType.DATAFLOW_SIDE_EFFECTING


def _sem_table(send_sems, recv_sems):
    return lambda t, j: (send_sems.at[t, j], recv_sems.at[t, j])


def _sem_per_peer(send_sems, recv_sems):
    return lambda t, j: (send_sems[j], recv_sems[j])


def _gather_ici(bufs, sem, sends=True, lands=True):
    x, y, c = _mesh_pos()
    me = 2 * x + y
    out, into = [], []
    for t, buf in enumerate(bufs):
        mine, _, _ = _core_rows(buf, c)
        part = lambda k, buf=buf, mine=mine: buf.at[pl.ds(0, buf.shape[0]), k, mine]
        for j, (px, py) in enumerate(_other_chips(x, y)):
            if sends:
                out.append(_remote(part(me), part(me), *sem(t, j), (px, py, c)))
            if lands:
                into.append(_remote(part(2 * px + py), part(2 * px + py), *sem(t, j), (px, py, c)))
    return out, into


def _gather_d2d(bufs, send_sems, recv_sems, first):
    x, y, c = _mesh_pos()
    sends, lands = [], []
    for t, buf in enumerate(bufs):
        mine, theirs, split = _core_rows(buf, c)
        if not split:
            continue
        for j, (px, py) in enumerate(_other_chips(x, y)):
            part = lambda rows, buf=buf, k=2 * px + py: buf.at[pl.ds(0, buf.shape[0]), k, rows]
            sems = (send_sems.at[t, first + j], recv_sems.at[t, first + j], (x, y, 1 - c))
            sends.append(_remote(part(mine), part(mine), *sems))
            lands.append(_remote(part(theirs), part(theirs), *sems))
    return sends, lands


def gather_shards(bufs):
    n = len(bufs)

    def body(*refs):
        outs = refs[n:2 * n]
        send_sems, recv_sems = refs[2 * n:]
        ici_out, ici_in = _gather_ici(outs, _sem_table(send_sems, recv_sems))
        d2d_out, d2d_in = _gather_d2d(outs, send_sems, recv_sems, 3)
        for cp in ici_out:
            cp.start()
        for cp in ici_in:
            cp.wait_recv()
        for cp in d2d_out:
            cp.start()
        for cp in d2d_in:
            cp.wait_recv()
        for cp in ici_out + d2d_out:
            cp.wait_send()

    return pl.pallas_call(
        body,
        in_specs=[HBM_REF] * n,
        out_specs=[HBM_REF] * n,
        out_shape=[jax.ShapeDtypeStruct(b.shape, b.dtype) for b in bufs],
        input_output_aliases={t: t for t in range(n)},
        scratch_shapes=[pltpu.SemaphoreType.DMA((n, 6)), pltpu.SemaphoreType.DMA((n, 6))],
        name="gather_shards",
    )(*bufs)


def gather_start(bufs):
    n = len(bufs)

    def body(*refs):
        send_sems, recv_sems = refs[n:n + 3], refs[n + 3:n + 6]
        outs = refs[n + 6:2 * n + 6]
        token = refs[2 * n + 6]
        for cp in _gather_ici(outs, _sem_per_peer(send_sems, recv_sems), lands=False)[0]:
            cp.start()
        token[...] = jnp.zeros_like(token)

    res = pl.pallas_call(
        body,
        in_specs=[HBM_SPACE] * n,
        out_specs=[SEM_SPACE] * 6 + [HBM_SPACE] * n + [pl.BlockSpec(memory_space=pltpu.VMEM)],
        out_shape=[pltpu.SemaphoreType.DMA(())] * 6
        + [pltpu.HBM(b.shape, b.dtype) for b in bufs] + [jax.ShapeDtypeStruct((8, LANES), F32)],
        input_output_aliases={t: t + 6 for t in range(n)},
        compiler_params=pltpu.CompilerParams(has_side_effects=IN_FLIGHT),
        name="gather_start",
    )(*[pltpu.with_memory_space_constraint(b, pltpu.HBM) for b in bufs])
    return list(res[0:3]), list(res[3:6]), list(res[6:n + 6]), res[n + 6]


def gather_wait(send_sems, recv_sems, bufs, after):
    n = len(bufs)

    def body(*refs):
        ins_send, ins_recv = refs[n:n + 3], refs[n + 3:n + 6]
        outs = refs[n + 7:]
        sends, lands = _gather_ici(outs, _sem_per_peer(ins_send, ins_recv))
        for cp in lands:
            cp.wait_recv()
        for cp in sends:
            cp.wait_send()

    return list(pl.pallas_call(
        body,
        in_specs=[HBM_SPACE] * n + [SEM_SPACE] * 6 + [HBM_REF],
        out_specs=[HBM_SPACE] * n,
        out_shape=[pltpu.HBM(b.shape, b.dtype) for b in bufs],
        input_output_aliases={t: t for t in range(n)},
        compiler_params=pltpu.CompilerParams(has_side_effects=IN_FLIGHT),
        name="gather_wait",
    )(*bufs, *send_sems, *recv_sems, after))


def gather_forward(bufs):
    n = len(bufs)

    def body(*refs):
        outs = refs[n:2 * n]
        send_sems, recv_sems = refs[2 * n:]
        sends, lands = _gather_d2d(outs, send_sems, recv_sems, 0)
        for cp in sends:
            cp.start()
        for cp in lands:
            cp.wait_recv()
        for cp in sends:
            cp.wait_send()

    return list(pl.pallas_call(
        body,
        in_specs=[HBM_REF] * n,
        out_specs=[HBM_REF] * n,
        out_shape=[jax.ShapeDtypeStruct(b.shape, b.dtype) for b in bufs],
        input_output_aliases={t: t for t in range(n)},
        scratch_shapes=[pltpu.SemaphoreType.DMA((n, 3)), pltpu.SemaphoreType.DMA((n, 3))],
        name="gather_forward",
    )(*bufs))


def pair_exchange(grads):
    n = len(grads)

    def body(*refs):
        ins, outs = refs[:n], refs[n:2 * n]
        send_sems, recv_sems = refs[2 * n:]
        cps = _pair_copies(ins, outs, lambda t: (send_sems.at[t], recv_sems.at[t]))
        for cp in cps:
            cp.start()
        for cp in cps:
            cp.wait()

    return list(pl.pallas_call(
        body,
        in_specs=[HBM_REF] * n,
        out_specs=[HBM_REF] * n,
        out_shape=[jax.ShapeDtypeStruct(_half_shape(g), g.dtype) for g in grads],
        scratch_shapes=[pltpu.SemaphoreType.DMA((n,)), pltpu.SemaphoreType.DMA((n,))],
        name="pair_exchange",
    )(*grads))


def _half_shape(g):
    return g.shape[:2] + (g.shape[2] // 2, g.shape[3])


def _pair_copies(grads, lands, sem):
    x, y, c = _mesh_pos()
    cps = []
    for t in range(len(grads)):
        _, theirs, _ = _core_rows(grads[t], c)
        src = grads[t].at[pl.ds(0, grads[t].shape[0]), pl.ds(0, N_CHIPS), theirs]
        cps.append(_remote(src, lands[t], *sem(t), (x, y, 1 - c)))
    return cps


def pair_exchange_start(grads):
    n = len(grads)
    lands = [lax.empty(_half_shape(g), g.dtype) for g in grads]

    def body(*refs):
        send_sem, recv_sem = refs[2 * n:2 * n + 2]
        outs = refs[2 * n + 2:4 * n + 2]
        token = refs[4 * n + 2]
        for cp in _pair_copies(outs[:n], outs[n:], lambda t: (send_sem, recv_sem)):
            cp.start()
        token[...] = jnp.zeros_like(token)

    res = pl.pallas_call(
        body,
        in_specs=[HBM_SPACE] * (2 * n),
        out_specs=[SEM_SPACE] * 2 + [HBM_SPACE] * (2 * n) + [pl.BlockSpec(memory_space=pltpu.VMEM)],
        out_shape=[pltpu.SemaphoreType.DMA(())] * 2 + [pltpu.HBM(a.shape, a.dtype) for a in list(grads) + lands]
        + [jax.ShapeDtypeStruct((8, LANES), F32)],
        input_output_aliases={t: t + 2 for t in range(2 * n)},
        compiler_params=pltpu.CompilerParams(has_side_effects=IN_FLIGHT),
        name="pair_exchange_start",
    )(*[pltpu.with_memory_space_constraint(a, pltpu.HBM) for a in list(grads) + lands])
    return res[0], res[1], list(res[2:n + 2]), list(res[n + 2:2 * n + 2]), res[2 * n + 2]


def pair_exchange_wait(send_sem, recv_sem, grads, lands, after):
    n = len(grads)

    def body(*refs):
        in_send, in_recv = refs[2 * n:2 * n + 2]
        outs = refs[2 * n + 3:]
        for cp in _pair_copies(outs[:n], outs[n:], lambda t: (in_send, in_recv)):
            cp.wait()

    res = pl.pallas_call(
        body,
        in_specs=[HBM_SPACE] * (2 * n) + [SEM_SPACE] * 2 + [HBM_REF],
        out_specs=[HBM_SPACE] * (2 * n),
        out_shape=[pltpu.HBM(a.shape, a.dtype) for a in list(grads) + list(lands)],
        input_output_aliases={t: t for t in range(2 * n)},
        compiler_params=pltpu.CompilerParams(has_side_effects=IN_FLIGHT),
        name="pair_exchange_wait",
    )(*grads, *lands, send_sem, recv_sem, after)
    return list(res[:n]), list(res[n:])


def pair_sum(core, g, t, name):
    n, _, half, C = t.shape
    tm = _row_tile(half)
    per = half // tm

    def body(c_ref, g_ref, t_ref, o_ref):
        o_ref[...] = (g_ref[...] + t_ref[...]).astype(BF16)

    tile = pl.BlockSpec((None, tm, C), lambda a, i, c: (a, i, 0))
    out = pl.pallas_call(
        body,
        grid_spec=pltpu.PrefetchScalarGridSpec(
            num_scalar_prefetch=1,
            grid=(n * N_CHIPS, per),
            in_specs=[pl.BlockSpec((None, tm, C), lambda a, i, c: (a, per * c[0] + i, 0)), tile],
            out_specs=tile,
        ),
        out_shape=jax.ShapeDtypeStruct((n * N_CHIPS, half, C), BF16),
        compiler_params=_params("parallel", "parallel"),
        name=name,
    )(core, g.reshape(n * N_CHIPS, 2 * half, C), t.reshape(n * N_CHIPS, half, C))
    return out.reshape(t.shape)


def _chip_copies(sums, recv, sem):
    x, y, c = _mesh_pos()
    cps = []
    for t in range(len(sums)):
        for j, (px, py) in enumerate(_other_chips(x, y)):
            src = sums[t].at[pl.ds(0, sums[t].shape[0]), 2 * px + py]
            cps.append(_remote(src, recv[t].at[j], *sem(t, j), (px, py, c)))
    return cps


def _recv_shape(s):
    return (N_CHIPS - 1, s.shape[0]) + s.shape[2:]


def chip_exchange(sums):
    n = len(sums)

    def body(*refs):
        ins, outs = refs[:n], refs[n:2 * n]
        cps = _chip_copies(ins, outs, _sem_table(*refs[2 * n:]))
        for cp in cps:
            cp.start()
        for cp in cps:
            cp.wait()

    return list(pl.pallas_call(
        body,
        in_specs=[HBM_REF] * n,
        out_specs=[HBM_REF] * n,
        out_shape=[jax.ShapeDtypeStruct(_recv_shape(s), s.dtype) for s in sums],
        scratch_shapes=[pltpu.SemaphoreType.DMA((n, 3)), pltpu.SemaphoreType.DMA((n, 3))],
        name="chip_exchange",
    )(*sums))


def chip_exchange_start(sums):
    n = len(sums)
    lands = [lax.empty(_recv_shape(s), s.dtype) for s in sums]

    def body(*refs):
        send_sems, recv_sems = refs[2 * n:2 * n + 3], refs[2 * n + 3:2 * n + 6]
        outs = refs[2 * n + 6:4 * n + 6]
        token = refs[4 * n + 6]
        for cp in _chip_copies(outs[:n], outs[n:2 * n], _sem_per_peer(send_sems, recv_sems)):
            cp.start()
        token[...] = jnp.zeros_like(token)

    res = pl.pallas_call(
        body,
        in_specs=[HBM_SPACE] * (2 * n),
        out_specs=[SEM_SPACE] * 6 + [HBM_SPACE] * (2 * n) + [pl.BlockSpec(memory_space=pltpu.VMEM)],
        out_shape=[pltpu.SemaphoreType.DMA(())] * 6 + [pltpu.HBM(a.shape, a.dtype) for a in list(sums) + lands]
        + [jax.ShapeDtypeStruct((8, LANES), F32)],
        input_output_aliases={t: t + 6 for t in range(2 * n)},
        compiler_params=pltpu.CompilerParams(has_side_effects=IN_FLIGHT),
        name="chip_exchange_start",
    )(*[pltpu.with_memory_space_constraint(a, pltpu.HBM) for a in list(sums) + lands])
    return list(res[0:3]), list(res[3:6]), list(res[6:n + 6]), list(res[n + 6:2 * n + 6]), res[2 * n + 6]


def chip_exchange_wait(send_sems, recv_sems, sums, lands, after):
    n = len(sums)

    def body(*refs):
        ins_send, ins_recv = refs[2 * n:2 * n + 3], refs[2 * n + 3:2 * n + 6]
        outs = refs[2 * n + 7:]
        for cp in _chip_copies(outs[:n], outs[n:], _sem_per_peer(ins_send, ins_recv)):
            cp.wait()

    res = pl.pallas_call(
        body,
        in_specs=[HBM_SPACE] * (2 * n) + [SEM_SPACE] * 6 + [HBM_REF],
        out_specs=[HBM_SPACE] * (2 * n),
        out_shape=[pltpu.HBM(a.shape, a.dtype) for a in list(sums) + list(lands)],
        input_output_aliases={t: t for t in range(2 * n)},
        compiler_params=pltpu.CompilerParams(has_side_effects=IN_FLIGHT),
        name="chip_exchange_wait",
    )(*sums, *lands, *send_sems, *recv_sems, after)
    return list(res[:n]), list(res[n:])


def chip_sum_into(core, chip, recv, sums, total, lo, name):
    _, n, half, C = recv.shape
    tm = _row_tile(half)
    per = half // tm

    def body(c_ref, k_ref, r_ref, s_ref, t_ref, o_ref):
        acc = s_ref[...].astype(F32)
        for j in range(N_CHIPS - 1):
            acc = acc + r_ref[j].astype(F32)
        o_ref[...] = acc

    return pl.pallas_call(
        body,
        grid_spec=pltpu.PrefetchScalarGridSpec(
            num_scalar_prefetch=2,
            grid=(n, per),
            in_specs=[pl.BlockSpec((N_CHIPS - 1, None, tm, C), lambda a, i, c, k: (0, a, i, 0)),
                      pl.BlockSpec((None, None, tm, C), lambda a, i, c, k: (a, k[0], i, 0)),
                      HBM_REF],
            out_specs=pl.BlockSpec((None, tm, C), lambda a, i, c, k: (lo + a, per * c[0] + i, 0)),
        ),
        out_shape=jax.ShapeDtypeStruct(total.shape, F32),
        input_output_aliases={4: 0},
        compiler_params=_params("parallel", "parallel"),
        name=name,
    )(core, chip, recv, sums, total)


def sibling_share(totals):
    n = len(totals)

    def body(*refs):
        outs = refs[n:2 * n]
        send_sems, recv_sems = refs[2 * n:]
        x, y, c = _mesh_pos()
        half = lambda t, h: outs[t].at[pl.ds(0, DEPTH), pl.ds(h * (outs[t].shape[1] // 2), outs[t].shape[1] // 2)]
        sent = [_remote(half(t, c), half(t, c), send_sems.at[t], recv_sems.at[t], (x, y, 1 - c)) for t in range(n)]
        for cp in sent:
            cp.start()
        for t in range(n):
            _remote(half(t, 1 - c), half(t, 1 - c), send_sems.at[t], recv_sems.at[t], (x, y, 1 - c)).wait_recv()
        for cp in sent:
            cp.wait_send()

    return pl.pallas_call(
        body,
        in_specs=[HBM_REF] * n,
        out_specs=[HBM_REF] * n,
        out_shape=[jax.ShapeDtypeStruct(t.shape, t.dtype) for t in totals],
        input_output_aliases={t: t for t in range(n)},
        scratch_shapes=[pltpu.SemaphoreType.DMA((n,)), pltpu.SemaphoreType.DMA((n,))],
        name="sibling_share",
    )(*totals)


def small_allgather(small):
    def body(s_ref, a_ref, send_sems, recv_sems, local_sem):
        x, y, c = _mesh_pos()
        me = 4 * x + 2 * y + c
        own = pltpu.make_async_copy(s_ref, a_ref.at[me], local_sem)
        own.start()
        sent = []
        for k in range(1, N_DEV):
            peer = (x ^ (k >> 2), y ^ ((k >> 1) & 1), c ^ (k & 1))
            cp = _remote(s_ref, a_ref.at[me], send_sems.at[k - 1], recv_sems.at[k - 1], peer)
            cp.start()
            sent.append(cp)
        for k in range(1, N_DEV):
            px, py, pc = x ^ (k >> 2), y ^ ((k >> 1) & 1), c ^ (k & 1)
            _remote(s_ref, a_ref.at[4 * px + 2 * py + pc], send_sems.at[k - 1], recv_sems.at[k - 1], (px, py, pc)).wait_recv()
        for cp in sent:
            cp.wait_send()
        own.wait()

    return pl.pallas_call(
        body,
        in_specs=[HBM_REF],
        out_specs=HBM_REF,
        out_shape=jax.ShapeDtypeStruct((N_DEV,) + SMALL_SHAPE, small.dtype),
        scratch_shapes=[pltpu.SemaphoreType.DMA((N_DEV - 1,)), pltpu.SemaphoreType.DMA((N_DEV - 1,)), pltpu.SemaphoreType.DMA],
        name="small_allgather",
    )(small)


def small_sum(blocks):
    def body(a_ref, o_ref):
        acc = a_ref[0]
        for d in range(1, N_DEV):
            acc = acc + a_ref[d]
        o_ref[...] = acc

    return pl.pallas_call(
        body,
        in_specs=[pl.BlockSpec(memory_space=pltpu.VMEM)],
        out_specs=pl.BlockSpec(memory_space=pltpu.VMEM),
        out_shape=jax.ShapeDtypeStruct(SMALL_SHAPE, F32),
        name="small_sum",
    )(blocks)


def pack_small(grads, loss):
    flat = jnp.concatenate([grads[n].reshape(-1) for n in SMALL] + [loss.reshape(-1)])
    size = SMALL_SHAPE[0] * SMALL_SHAPE[1]
    return jnp.pad(flat, (0, size - flat.shape[0])).reshape(SMALL_SHAPE)


def unpack_small(packed):
    flat = packed.reshape(-1)
    out, off = {}, 0
    for n, size in SMALL.items():
        out[n] = flat[off:off + DEPTH * size].reshape(DEPTH, size)
        off += DEPTH * size
    return out, flat[off]


def kernel(x, norm_mix_g, w_in, forget_b, q_norm_g, k_norm_g, w_attn_out, conv_w, w_conv_out, pool_w, pool_scale, w_o, norm_ffn_g, w_ffn_in, w_ffn_out, loss_target, m_norm_mix_g, m_w_in, m_forget_b, m_q_norm_g, m_k_norm_g, m_w_attn_out, m_conv_w, m_w_conv_out, m_pool_w, m_pool_scale, m_w_o, m_norm_ffn_g, m_w_ffn_in, m_w_ffn_out, v_norm_mix_g, v_w_in, v_forget_b, v_q_norm_g, v_k_norm_g, v_w_attn_out, v_conv_w, v_w_conv_out, v_pool_w, v_pool_scale, v_w_o, v_norm_ffn_g, v_w_ffn_in, v_w_ffn_out):
    given = dict(locals())
    weights = {n: given[n] for n in WEIGHTS}

    core = lax.axis_index("c").astype(jnp.int32)
    chip = (2 * lax.axis_index("x") + lax.axis_index("y")).astype(jnp.int32)

    core, chip = core.reshape(1), chip.reshape(1)
    mix_names, all_names = MIX_W + ["conv_w"], MIX_W + FFN_W + ["conv_w"]

    def placed(lo, hi, names, after=None):
        piece = None if after is None else _as4(next(iter(after.values())))[0, 0, 0:HALO, 0:LANES]
        return [place_shard(chip, weights[n], lo, hi, F32 if n == "conv_w" else BF16, "place_" + n,
                            piece if i == 0 else None) for i, n in enumerate(names)]

    def as_weights(bufs, names):
        return {n: b.reshape(b.shape[:2] + weights[n].shape[1:]) for n, b in zip(names, bufs)}

    def landed(start, names, after):
        return as_weights(gather_forward(gather_wait(*start[:3], after)), names)

    def layer_small(l, *tokens):
        small = {n: weights[n][l] for n in SMALL if n != "conv_w"}
        for t in tokens:
            small["norm_mix_g"] = small["norm_mix_g"] + t[0, 0]
        return small

    done = [None] * DEPTH
    first = as_weights(gather_shards(placed(0, 1, mix_names)), mix_names)
    ffn0 = gather_start(placed(0, 1, FFN_W, after=first))
    layer1 = gather_start(placed(1, 2, all_names, after=first))
    xm, wm, sm = mix_fwd(x[0], first, 0, layer_small(0, ffn0[3], layer1[3]))
    xs, wf, sf = ffn_half_fwd(xm, landed(ffn0, FFN_W, xm), 0, weights["norm_ffn_g"][0])
    done[0] = (wm, sm, wf, sf)
    full = landed(layer1, all_names, xs)
    rest = gather_start(placed(2, DEPTH, all_names, after=full))
    xm, wm, sm = mix_fwd(xs, full, 0, layer_small(1, rest[3]))
    xs, wf, sf = ffn_half_fwd(xm, full, 0, weights["norm_ffn_g"][1])
    done[1] = (wm, sm, wf, sf)
    full = landed(rest, all_names, xs)
    for l in range(2, DEPTH):
        xm, wm, sm = mix_fwd(xs, full, l - 2, layer_small(l))
        xs, wf, sf = ffn_half_fwd(xm, full, l - 2, weights["norm_ffn_g"][l])
        done[l] = (wm, sm, wf, sf)
    loss, dx = loss_head(xs, loss_target[0])

    small_grads = [None] * DEPTH
    totals = {n: lax.empty((DEPTH,) + _as4(weights[n][None]).shape[2:], F32) for n in SPLIT}

    def buffers(names, n_layers):
        return {n: lax.empty((n_layers, N_CHIPS) + weights[n].shape[1:], F32) for n in names}

    def pair_sums(names, grads, theirs):
        return [pair_sum(core, a, t, "pair_sum_" + n) for n, a, t in zip(names, grads, theirs)]

    def add_chips(names, recv, sums, lo):
        for n, r, s in zip(names, recv, sums):
            totals[n] = chip_sum_into(core, chip, r, s, totals[n], lo, "chip_sum_" + n)

    big = buffers(SPLIT, DEPTH - 1)
    for l in reversed(range(1, DEPTH)):
        wm, sm, wf, sf = done[l]
        dxm, ffn_big, g_ffn = ffn_half_bwd(dx, wf, sf, l - 1, {n: big[n] for n in FFN_W})
        dx, mix_big, g_mix = mix_bwd(dxm, wm, sm, l - 1, {n: big[n] for n in MIX_W})
        big = {**ffn_big, **mix_big}
        small_grads[l] = {**g_ffn, **g_mix}
    pair = pair_exchange_start([_as4(big[n]) for n in SPLIT])

    wm, sm, wf, sf = done[0]
    wf = dict(wf, g2=wf["g2"] + pair[4][0:1, 0:1])
    dxm, ffn_big, g_ffn = ffn_half_bwd(dx, wf, sf, 0, buffers(FFN_W, 1))
    chips = chip_exchange_start(pair_sums(SPLIT, *pair_exchange_wait(*pair[:4], dxm)))
    ffn_grads = [_as4(ffn_big[n]) for n in FFN_W]
    ffn_chips = chip_exchange_start(pair_sums(FFN_W, ffn_grads, pair_exchange(ffn_grads)))
    wm = dict(wm, pscale=wm["pscale"] + chips[4][0:1, 0:1] + ffn_chips[4][0:1, 0:1])
    dproj, mix_big, g_mix = mix_bwd_weights(dxm, wm, sm, 0, buffers(MIX_W, 1))
    mix_grads = [_as4(mix_big[n]) for n in MIX_W]
    mix_chips = chip_exchange_start(pair_sums(MIX_W, mix_grads, pair_exchange(mix_grads)))
    dx, g_in = mix_bwd_input(dproj, dxm, dict(wm, g1=wm["g1"] + mix_chips[4][0:1, 0:1]), sm)
    small_grads[0] = {**g_ffn, **g_mix, **g_in}

    reduced, deltas, new_m, new_v = {}, {}, {}, {}

    def update(names):
        for n in names:
            w = weights[n]
            flat = (-1, w.shape[-1])
            d, nm, nv = adamw(w.reshape(flat), reduced[n].reshape(flat), given["m_" + n].reshape(flat),
                              given["v_" + n].reshape(flat), "adamw_" + n)
            deltas[n], new_m[n], new_v[n] = d.reshape(w.shape), nm.reshape(w.shape), nv.reshape(w.shape)

    def share(names):
        for n, t in zip(names, sibling_share([totals[n] for n in names])):
            reduced[n] = t.reshape(weights[n].shape)

    sums, recv = chip_exchange_wait(*chips[:4], dx)
    add_chips(SPLIT, recv, sums, 1)
    sums, recv = chip_exchange_wait(*ffn_chips[:4], dx)
    add_chips(FFN_W, recv, sums, 0)
    share(FFN_W)
    update(FFN_W)
    sums, recv = chip_exchange_wait(*mix_chips[:4], deltas["w_ffn_in"])
    add_chips(MIX_W, recv, sums, 0)
    share(MIX_W)

    small_grads = {n: jnp.stack([g[n] for g in small_grads]) for n in SMALL}
    small_total, loss_sum = unpack_small(small_sum(small_allgather(pack_small(small_grads, loss))))
    chip = chip[0]
    cols = D_CONV // N_CHIPS
    small_total["conv_w"] = lax.dynamic_slice_in_dim(small_total["conv_w"].reshape(DEPTH, CONV_K, D_CONV), chip * cols, cols, axis=2)
    for n in SMALL:
        reduced[n] = small_total[n].reshape(weights[n].shape)
    update(MIX_W + list(SMALL))

    return (loss_sum, dx[None], *[reduced[n] for n in WEIGHTS], *[deltas[n] for n in WEIGHTS],
            *[new_m[n] for n in WEIGHTS], *[new_v[n] for n in WEIGHTS])
```

```python
import functools

import numpy as np
import jax
import jax.numpy as jnp
from jax import lax
from jax.experimental import pallas as pl
from jax.experimental.pallas import tpu as pltpu

F32 = jnp.float32
BF16 = jnp.bfloat16

D_MODEL = 1024
DEPTH = 4
HEAD_DIM = 64
N_HEADS = 8
D_ATTN = 512
D_CONV = 256
D_POOL = 256
D_FF = 2816
D_IN = 5640
CONV_K = 3
POOL_WINDOWS = (2, 4, 8, 16)
N_GROUPS = len(POOL_WINDOWS)
EPS = 1e-6
ADAM_LR, ADAM_B1, ADAM_B2, ADAM_EPS, ADAM_WD, ADAM_STEP = 0.001, 0.9, 0.999, 1e-08, 0.01, 10

D_QKV = 3 * D_ATTN
D_F = 128
D_B = 3 * D_CONV + D_POOL + 3 * D_MODEL
D_LOCAL = 3 * D_CONV + D_POOL

LANES = 128
D_HEADS = N_HEADS * LANES
HALO = 16
VMEM_LIMIT = 56 * 1024 * 1024
NEG = -1e30
LOG2E = 1.4426950408889634
LN2 = 0.6931471805599453

TM = 256
TM_MIX = 256
TQ = 1024

LANE_C = 64
LANE_ONE = 67
LANE_LSE = 70
N_PIECES = 3


def _dot(a, b):
    return jnp.dot(a, b, preferred_element_type=F32)


def _dot_nt(a, b):
    return lax.dot_general(a, b, (((1,), (1,)), ((), ())), preferred_element_type=F32)


def _dot_tn(a, b):
    return lax.dot_general(a, b, (((0,), (0,)), ((), ())), preferred_element_type=F32)


def _params(*sem):
    return pltpu.CompilerParams(dimension_semantics=sem, vmem_limit_bytes=VMEM_LIMIT)


def _rows(tm, n):
    return pl.BlockSpec((tm, n), lambda i, *_: (i, 0))


def _whole(a):
    nd = a.ndim
    return pl.BlockSpec(a.shape, lambda *_: (0,) * nd)


def _layer(shape):
    nd = len(shape)
    return pl.BlockSpec((None,) + tuple(shape), lambda *a: (a[-1][0],) + (0,) * nd)


def _layer_index(l):
    return jnp.full((1,), l, jnp.int32)


def _split_bf16(x):
    hi = x.astype(BF16)
    lo = (x - hi.astype(F32)).astype(BF16)
    return hi, lo


def _pieces(x):
    p1 = x.astype(BF16)
    r1 = x - p1.astype(F32)
    p2 = r1.astype(BF16)
    p3 = (r1 - p2.astype(F32)).astype(BF16)
    return p1, p2, p3


def _sigmoid(x):
    return 1.0 / (1.0 + jnp.exp(-x))


def w_in_prep(win, l):
    tr = 256
    n = D_IN // 4
    v_rest = D_QKV - n
    b0 = v_rest + N_HEADS

    def body(s0, s1, s2, s3, wa_ref, wf_ref, wb_ref):
        b = s1[...]
        wa_ref[...] = jnp.concatenate([s0[...], b[:, 0:v_rest]], axis=1)
        wf_ref[...] = jnp.concatenate([b[:, v_rest:b0], jnp.zeros((tr, D_F - N_HEADS), b.dtype)], axis=1)
        wb_ref[...] = jnp.concatenate([b[:, b0:n], s2[...], s3[...]], axis=1)

    shard = lambda j: pl.BlockSpec((None, None, tr, n), lambda i: (l, j, i, 0))
    return pl.pallas_call(
        body,
        grid=(D_MODEL // tr,),
        in_specs=[shard(0), shard(1), shard(2), shard(3)],
        out_specs=[_rows(tr, D_QKV), _rows(tr, D_F), _rows(tr, D_B)],
        out_shape=[jax.ShapeDtypeStruct((D_MODEL, D_QKV), win.dtype), jax.ShapeDtypeStruct((D_MODEL, D_F), win.dtype),
                   jax.ShapeDtypeStruct((D_MODEL, D_B), win.dtype)],
        compiler_params=_params("parallel"),
        name="w_in_prep",
    )(win, win, win, win)


def _into_layer(buf):
    return dict(in_spec=HBM_REF, out_shape=jax.ShapeDtypeStruct(buf.shape, buf.dtype), aliases={1: 0})


def w_in_unprep(dwa, dwf, dwb, buf, l):
    tr = 256
    n = D_IN // 4
    v_rest = D_QKV - n
    b1 = n - v_rest - N_HEADS
    place = _into_layer(buf)

    def body(l_ref, buf_ref, a_ref, f_ref, b_ref, o_ref):
        a = a_ref[...]
        b = b_ref[...]
        o_ref[0] = a[:, 0:n]
        o_ref[1] = jnp.concatenate([a[:, n:D_QKV], f_ref[:, 0:N_HEADS], b[:, 0:b1]], axis=1)
        o_ref[2] = b[:, b1:b1 + n]
        o_ref[3] = b[:, b1 + n:D_B]

    return pl.pallas_call(
        body,
        grid_spec=pltpu.PrefetchScalarGridSpec(
            num_scalar_prefetch=1,
            grid=(D_MODEL // tr,),
            in_specs=[place["in_spec"], _rows(tr, D_QKV), _rows(tr, D_F), _rows(tr, D_B)],
            out_specs=pl.BlockSpec((None, 4, tr, n), lambda i, l: (l[0], 0, i, 0)),
        ),
        out_shape=place["out_shape"],
        input_output_aliases=place["aliases"],
        compiler_params=_params("parallel"),
        name="w_in_unprep",
    )(_layer_index(l), buf, dwa, dwf, dwb)


def branch_w_prep(wao, wco, pw, cw, l):
    gd = D_POOL // N_GROUPS
    od = D_MODEL // N_GROUPS

    def body(wao_ref, wco_ref, pw_ref, cw_ref, ao_ref, co_ref, po_ref, co8_ref):
        a = jnp.concatenate([wao_ref[j] for j in range(4)], axis=1)
        gap = jnp.zeros((LANES - HEAD_DIM, D_MODEL), a.dtype)
        ao_ref[...] = jnp.concatenate(
            [blk for h in range(N_HEADS) for blk in (a[h * HEAD_DIM:(h + 1) * HEAD_DIM], gap)], axis=0)
        co_ref[...] = jnp.concatenate([wco_ref[j] for j in range(4)], axis=1)
        zero = jnp.zeros((gd, od), a.dtype)
        po_ref[...] = jnp.concatenate(
            [jnp.concatenate([jnp.concatenate([pw_ref[j, g] for j in range(4)], axis=1) if g2 == g else zero
                              for g2 in range(N_GROUPS)], axis=1) for g in range(N_GROUPS)], axis=0)
        co8_ref[...] = jnp.zeros_like(co8_ref)
        co8_ref[0:CONV_K, :] = jnp.concatenate([cw_ref[j] for j in range(4)], axis=1)

    sel = lambda *shape: pl.BlockSpec((None,) + shape, lambda i: (l,) + (0,) * len(shape))
    return pl.pallas_call(
        body,
        grid=(1,),
        in_specs=[sel(4, D_ATTN, D_MODEL // 4), sel(4, D_CONV, D_MODEL // 4), sel(4, N_GROUPS, gd, od // 4),
                  sel(4, CONV_K, D_CONV // 4)],
        out_specs=[pl.BlockSpec((D_HEADS, D_MODEL), lambda i: (0, 0)), pl.BlockSpec((D_CONV, D_MODEL), lambda i: (0, 0)),
                   pl.BlockSpec((D_POOL, D_MODEL), lambda i: (0, 0)), pl.BlockSpec((8, D_CONV), lambda i: (0, 0))],
        out_shape=[jax.ShapeDtypeStruct((D_HEADS, D_MODEL), BF16), jax.ShapeDtypeStruct((D_CONV, D_MODEL), BF16),
                   jax.ShapeDtypeStruct((D_POOL, D_MODEL), BF16), jax.ShapeDtypeStruct((8, D_CONV), F32)],
        name="branch_w_prep",
    )(wao, wco, pw, cw)


def branch_g_place(dwao, dwco, dwpool, dwo, bufs, l):
    gd = D_POOL // N_GROUPS
    od = D_MODEL // N_GROUPS
    q = D_MODEL // 4

    def body(l_ref, b0, b1, b2, b3, a_ref, c_ref, p_ref, w_ref, ao_ref, co_ref, po_ref, wo_ref):
        a = jnp.concatenate([a_ref[h * LANES:h * LANES + HEAD_DIM, :] for h in range(N_HEADS)], axis=0)
        c = c_ref[...]
        p = p_ref[...]
        for j in range(4):
            ao_ref[j] = a[:, j * q:(j + 1) * q]
            co_ref[j] = c[:, j * q:(j + 1) * q]
            wo_ref[j] = w_ref[j * q:(j + 1) * q, :]
            for g in range(N_GROUPS):
                c0 = g * od + j * (od // 4)
                po_ref[j, g] = p[g * gd:(g + 1) * gd, c0:c0 + od // 4]

    whole = lambda a: pl.BlockSpec(a.shape, lambda i, l: (0,) * a.ndim)
    layer = lambda b: pl.BlockSpec((None,) + b.shape[1:], lambda i, l: (l[0],) + (0,) * (b.ndim - 1))
    return pl.pallas_call(
        body,
        grid_spec=pltpu.PrefetchScalarGridSpec(
            num_scalar_prefetch=1,
            grid=(1,),
            in_specs=[HBM_REF] * 4 + [whole(dwao), whole(dwco), whole(dwpool), whole(dwo)],
            out_specs=[layer(b) for b in bufs],
        ),
        out_shape=[jax.ShapeDtypeStruct(b.shape, b.dtype) for b in bufs],
        input_output_aliases={1: 0, 2: 1, 3: 2, 4: 3},
        compiler_params=_params("arbitrary"),
        name="branch_g_place",
    )(_layer_index(l), *bufs, dwao, dwco, dwpool, dwo)


def in_proj_fwd(x, g, wa, wf, wb):
    S = x.shape[0]

    def body(x_ref, g_ref, wa_ref, wf_ref, wb_ref, h_ref, pa_ref, pf_ref, pb_ref):
        xf = x_ref[...]
        r = lax.rsqrt(jnp.mean(xf * xf, axis=-1, keepdims=True) + EPS)
        h = (xf * r * g_ref[...]).astype(BF16)
        h_ref[...] = h
        pa_ref[...] = _dot(h, wa_ref[...]).astype(BF16)
        pf_ref[...] = _dot(h, wf_ref[...])
        pb_ref[...] = _dot(h, wb_ref[...]).astype(BF16)

    return pl.pallas_call(
        body,
        grid=(S // TM,),
        in_specs=[_rows(TM, D_MODEL), _whole(g), _whole(wa), _whole(wf), _whole(wb)],
        out_specs=[_rows(TM, D_MODEL), _rows(TM, D_QKV), _rows(TM, D_F), _rows(TM, D_B)],
        out_shape=[
            jax.ShapeDtypeStruct((S, D_MODEL), BF16),
            jax.ShapeDtypeStruct((S, D_QKV), BF16),
            jax.ShapeDtypeStruct((S, D_F), F32),
            jax.ShapeDtypeStruct((S, D_B), BF16),
        ],
        compiler_params=_params("parallel"),
        name="in_proj_fwd",
    )(x, g, wa, wf, wb)


def in_proj_bwd(x, g, dxm, dpa, dpf, dpb, wa, wf, wb):
    S = x.shape[0]

    def body(x_ref, g_ref, dxm_ref, dpa_ref, dpf_ref, dpb_ref, wa_ref, wf_ref, wb_ref, dx_ref, dg_ref):
        @pl.when(pl.program_id(0) == 0)
        def _():
            dg_ref[...] = jnp.zeros_like(dg_ref)

        dh = _dot_nt(dpa_ref[...], wa_ref[...]) + _dot_nt(dpf_ref[...], wf_ref[...]) + _dot_nt(dpb_ref[...], wb_ref[...])
        xf = x_ref[...]
        r = lax.rsqrt(jnp.mean(xf * xf, axis=-1, keepdims=True) + EPS)
        xhat = xf * r
        dg_ref[...] += jnp.sum(dh * xhat, axis=0, keepdims=True)
        gdh = dh * g_ref[...]
        dx_ref[...] = dxm_ref[...] + r * (gdh - xhat * jnp.mean(xhat * gdh, axis=-1, keepdims=True))

    return pl.pallas_call(
        body,
        grid=(S // TM,),
        in_specs=[_rows(TM, D_MODEL), _whole(g), _rows(TM, D_MODEL), _rows(TM, D_QKV), _rows(TM, D_F), _rows(TM, D_B),
                  _whole(wa), _whole(wf), _whole(wb)],
        out_specs=[_rows(TM, D_MODEL), pl.BlockSpec((1, D_MODEL), lambda i: (0, 0))],
        out_shape=[jax.ShapeDtypeStruct((S, D_MODEL), F32), jax.ShapeDtypeStruct((1, D_MODEL), F32)],
        compiler_params=_params("arbitrary"),
        name="in_proj_bwd",
    )(x, g, dxm, dpa, dpf, dpb, wa, wf, wb)


TS_WGRAD = 1024


def _wgrad_columns(n):
    return next(tn for tn in (1024, 768, 512, 256, 128) if n % tn == 0)


def wgrad(xa, dy, name):
    S, K = xa.shape
    N = dy.shape[1]
    ts = min(TS_WGRAD, S)
    tn = _wgrad_columns(N)
    tk = _wgrad_columns(K)

    def body(x_ref, dy_ref, o_ref):
        @pl.when(pl.program_id(2) == 0)
        def _():
            o_ref[...] = jnp.zeros_like(o_ref)

        o_ref[...] += _dot_tn(x_ref[...], dy_ref[...])

    return pl.pallas_call(
        body,
        grid=(K // tk, N // tn, S // ts),
        in_specs=[pl.BlockSpec((ts, tk), lambda i, j, k: (k, i)), pl.BlockSpec((ts, tn), lambda i, j, k: (k, j))],
        out_specs=pl.BlockSpec((tk, tn), lambda i, j, k: (i, j)),
        out_shape=jax.ShapeDtypeStruct((K, N), F32),
        compiler_params=_params("parallel", "parallel", "arbitrary"),
        name=name,
    )(xa, dy)


def wgrad_into(xa, dy, name, buf, l):
    S, K = xa.shape
    N = dy.shape[1]
    ts = min(TS_WGRAD, S)
    split = buf.ndim == 4
    tn = buf.shape[-1] if split else _wgrad_columns(N)
    place = _into_layer(buf)

    def body(l_ref, buf_ref, x_ref, dy_ref, o_ref):
        @pl.when(pl.program_id(1) == 0)
        def _():
            o_ref[...] = jnp.zeros_like(o_ref)

        o_ref[...] += _dot_tn(x_ref[...], dy_ref[...])

    if split:
        out_spec = pl.BlockSpec((None, None, K, tn), lambda j, k, l: (l[0], j, 0, 0))
    else:
        out_spec = pl.BlockSpec((None, K, tn), lambda j, k, l: (l[0], 0, j))
    return pl.pallas_call(
        body,
        grid_spec=pltpu.PrefetchScalarGridSpec(
            num_scalar_prefetch=1,
            grid=(N // tn, S // ts),
            in_specs=[place["in_spec"], pl.BlockSpec((ts, K), lambda j, k, l: (k, 0)),
                      pl.BlockSpec((ts, tn), lambda j, k, l: (k, j))],
            out_specs=out_spec,
        ),
        out_shape=place["out_shape"],
        input_output_aliases=place["aliases"],
        compiler_params=_params("parallel", "arbitrary"),
        name=name,
    )(_layer_index(l), buf, xa, dy)


def _head_mean_matrix():
    h = np.arange(D_ATTN) // HEAD_DIM
    return jnp.asarray((h[:, None] == h[None, :]).astype(np.float32) / HEAD_DIM, BF16)


def _place_matrix(lane0):
    m = np.zeros((N_PIECES * LANES, D_HEADS), np.float32)
    for i in range(N_PIECES):
        for h in range(N_HEADS):
            m[i * LANES + h, h * LANES + lane0 + i] = 1.0
    return jnp.asarray(m, BF16)


def _tri(n, upper):
    r = np.arange(n)
    m = (r[None, :] >= r[:, None]) if upper else (r[None, :] <= r[:, None])
    return jnp.asarray(m.astype(np.float32), BF16)


def _lanes_in(lane, lo, n):
    return (lane >= lo) & (lane < lo + n)


def qk_prep(pa, pf, gq, gk, fb):
    S = pa.shape[0]
    bd = _head_mean_matrix()
    tri = _tri(TM, upper=False)
    place_q = _place_matrix(LANE_C)
    place_k = _place_matrix(LANE_ONE)

    def body(q_ref, k_ref, v_ref, pf_ref, gq_ref, gk_ref, fb_ref, bd_ref, tri_ref, pq_ref, pk_ref,
             qx_ref, kx_ref, vx_ref, carry):
        @pl.when(pl.program_id(0) == 0)
        def _():
            carry[...] = jnp.zeros_like(carry)

        def head_norm(x_ref, g_ref, scale):
            xf = x_ref[...].astype(F32)
            ms = _dot((xf * xf).astype(BF16), bd_ref[...])
            return xf * lax.rsqrt(ms + EPS) * g_ref[...] * scale

        qh = head_norm(q_ref, gq_ref, HEAD_DIM ** -0.5 * LOG2E)
        kh = head_norm(k_ref, gk_ref, 1.0)
        vf = v_ref[...].astype(F32)

        z = pf_ref[...] + fb_ref[...]
        logf = jnp.minimum(z, 0.0) - jnp.log(1.0 + jnp.exp(-jnp.abs(z)))
        hi, lo = _split_bf16(logf)
        c = _dot(tri_ref[...], hi) + _dot(tri_ref[...], lo) + carry[...]
        carry[...] += jnp.sum(hi.astype(F32) + lo.astype(F32), axis=0, keepdims=True)
        pieces = jnp.concatenate(_pieces(c * LOG2E), axis=1)
        cq = _dot(pieces, pq_ref[...])
        ck = _dot(pieces, pk_ref[...])

        lane = lax.broadcasted_iota(jnp.int32, (TM, LANES), 1)
        low = lane < HEAD_DIM
        ones_q = _lanes_in(lane, LANE_ONE, N_PIECES).astype(F32)
        ones_k = (_lanes_in(lane, LANE_C, N_PIECES) | _lanes_in(lane, LANE_LSE, N_PIECES)).astype(F32)
        ones_v = _lanes_in(lane, LANE_C, N_PIECES + 1).astype(F32)
        for h in range(N_HEADS):
            blk = slice((h // 2) * LANES, (h // 2 + 1) * LANES)
            head = (lambda a: pltpu.roll(a[:, blk], HEAD_DIM, 1)) if h % 2 else (lambda a: a[:, blk])
            mine = slice(h * LANES, (h + 1) * LANES)
            qx_ref[h] = jnp.where(low, head(qh), cq[:, mine] + ones_q).astype(BF16)
            kx_ref[h] = jnp.where(low, head(kh), ones_k - ck[:, mine]).astype(BF16)
            vx_ref[h] = jnp.where(low, head(vf), ones_v).astype(BF16)

    heads = pl.BlockSpec((N_HEADS, TM, LANES), lambda i: (0, i, 0))
    out = jax.ShapeDtypeStruct((N_HEADS, S, LANES), BF16)
    return pl.pallas_call(
        body,
        grid=(S // TM,),
        in_specs=[pl.BlockSpec((TM, D_ATTN), lambda i: (i, 0)), pl.BlockSpec((TM, D_ATTN), lambda i: (i, 1)),
                  pl.BlockSpec((TM, D_ATTN), lambda i: (i, 2)),
                  _rows(TM, D_F), _whole(gq), _whole(gk), _whole(fb), _whole(bd), _whole(tri), _whole(place_q), _whole(place_k)],
        out_specs=[heads, heads, heads],
        out_shape=[out, out, out],
        scratch_shapes=[pltpu.VMEM((1, D_F), F32)],
        compiler_params=_params("arbitrary"),
        name="qk_prep",
    )(pa, pa, pa, pf, gq, gk, fb, bd, tri, place_q, place_k)


def attn_fwd(qx, kx, vx):
    S = qx.shape[1]
    nq = S // TQ

    def body(q_ref, k_ref, v_ref, o_ref, qb_ref):
        i = pl.program_id(1)
        lane = lax.broadcasted_iota(jnp.int32, (TQ, LANES), 1)
        q = [q_ref[0], q_ref[1]]

        def update(qh, m, acc, k, v, visible):
            z = _dot_nt(qh, k)
            if visible is not None:
                z = jnp.where(visible, z, NEG)
            m_new = jnp.maximum(m, jnp.max(z, axis=1, keepdims=True))
            pr = jnp.exp2(z - m_new)
            return m_new, jnp.exp2(m - m_new) * acc + _dot(pr.astype(BF16), v)

        def step(kt, carry, diagonal=False):
            ks = pl.multiple_of(kt * TQ, TQ)
            visible = None
            if diagonal:
                visible = lax.broadcasted_iota(jnp.int32, (TQ, TQ), 0) >= lax.broadcasted_iota(jnp.int32, (TQ, TQ), 1)
            return tuple(update(q[j], *carry[j], k_ref[j, pl.ds(ks, TQ), :], v_ref[j, pl.ds(ks, TQ), :], visible)
                         for j in range(2))

        init = tuple((jnp.full((TQ, 1), NEG, F32), jnp.zeros((TQ, LANES), F32)) for _ in range(2))
        carry = step(i, lax.fori_loop(0, i, step, init), diagonal=True)
        for j in range(2):
            m, acc = carry[j]
            l = jnp.sum(jnp.where(lane == LANE_C, acc, 0.0), axis=1, keepdims=True)
            o_ref[j] = acc / l
            n1, n2, n3 = _pieces(-(m + jnp.log(l) * LOG2E))
            qb_ref[j] = jnp.where(lane == LANE_LSE, n1,
                                  jnp.where(lane == LANE_LSE + 1, n2, jnp.where(lane == LANE_LSE + 2, n3, q[j])))

    pair_tile = pl.BlockSpec((2, TQ, LANES), lambda p, i: (p, i, 0))
    pair_all = pl.BlockSpec((2, S, LANES), lambda p, i: (p, 0, 0))
    return pl.pallas_call(
        body,
        grid=(N_HEADS // 2, nq),
        in_specs=[pair_tile, pair_all, pair_all],
        out_specs=[pair_tile, pair_tile],
        out_shape=[jax.ShapeDtypeStruct((N_HEADS, S, LANES), F32), jax.ShapeDtypeStruct((N_HEADS, S, LANES), BF16)],
        compiler_params=_params("parallel", "parallel"),
        name="attn_fwd",
    )(qx, kx, vx)


def attn_bwd(qxb, kx, vx, dox):
    S = qxb.shape[1]
    nq = S // TQ

    def body(q_ref, k_ref, v_ref, do_ref, dq_ref, dk_ref, dv_ref):
        kt = pl.program_id(1)

        @pl.when(kt == 0)
        def _():
            dq_ref[...] = jnp.zeros_like(dq_ref)

        k = [k_ref[0], k_ref[1]]
        v = [v_ref[0], v_ref[1]]

        def step(qi, carry, masked):
            qs = pl.multiple_of(qi * TQ, TQ)
            new = []
            for j in range(2):
                dk, dv = carry[j]
                q = q_ref[j, pl.ds(qs, TQ), :]
                dout = do_ref[j, pl.ds(qs, TQ), :]
                z = _dot_nt(q, k[j])
                if masked:
                    visible = lax.broadcasted_iota(jnp.int32, (TQ, TQ), 0) >= lax.broadcasted_iota(jnp.int32, (TQ, TQ), 1)
                    z = jnp.where(visible, z, NEG)
                pr = jnp.exp2(z)
                dv = dv + _dot_tn(pr.astype(BF16), dout)
                dsb = (pr * _dot_nt(dout, v[j])).astype(BF16)
                dk = dk + _dot_tn(dsb, q)
                dq_ref[j, pl.ds(qs, TQ), :] += _dot(dsb, k[j])
                new.append((dk, dv))
            return tuple(new)

        zero = jnp.zeros((TQ, LANES), F32)
        carry = step(kt, ((zero, zero), (zero, zero)), True)
        carry = lax.fori_loop(kt + 1, nq, functools.partial(step, masked=False), carry)
        for j in range(2):
            dk_ref[j] = carry[j][0]
            dv_ref[j] = carry[j][1]

    pair_tile = pl.BlockSpec((2, TQ, LANES), lambda p, kt: (p, kt, 0))
    pair_all = pl.BlockSpec((2, S, LANES), lambda p, kt: (p, 0, 0))
    out = jax.ShapeDtypeStruct((N_HEADS, S, LANES), F32)
    return pl.pallas_call(
        body,
        grid=(N_HEADS // 2, nq),
        in_specs=[pair_all, pair_tile, pair_tile, pair_all],
        out_specs=[pair_all, pair_tile, pair_tile],
        out_shape=[out, out, out],
        compiler_params=_params("arbitrary", "arbitrary"),
        name="attn_bwd",
    )(qxb, kx, vx, dox)


def attn_bwd_post(pa, pf, dqx, dkx, dvx, gq, gk, fb):
    S = pa.shape[0]
    nt = S // TM
    bd = _head_mean_matrix()
    triu = _tri(TM, upper=True)
    rev = lambda i: nt - 1 - i

    def body(q_ref, k_ref, pf_ref, dqx_ref, dkx_ref, dvx_ref, gq_ref, gk_ref, fb_ref, bd_ref, triu_ref,
             dpa_ref, dpf_ref, dgq_ref, dgk_ref, dfb_ref, carry):
        @pl.when(pl.program_id(0) == 0)
        def _():
            carry[...] = jnp.zeros_like(carry)
            dgq_ref[...] = jnp.zeros_like(dgq_ref)
            dgk_ref[...] = jnp.zeros_like(dgk_ref)
            dfb_ref[...] = jnp.zeros_like(dfb_ref)

        lane = lax.broadcasted_iota(jnp.int32, (TM, LANES), 1)

        def columns(ref):
            return jnp.concatenate([jnp.where(lane < HEAD_DIM, ref[2 * p], pltpu.roll(ref[2 * p + 1], HEAD_DIM, 1))
                                    for p in range(N_HEADS // 2)], axis=1)

        def head_norm_bwd(x_ref, dy, g_ref, dg_ref):
            xf = x_ref[...].astype(F32)
            r = lax.rsqrt(_dot((xf * xf).astype(BF16), bd_ref[...]) + EPS)
            xhat = xf * r
            dg_ref[...] += jnp.sum(dy * xhat, axis=0, keepdims=True)
            gdy = dy * g_ref[...]
            return (r * (gdy - xhat * _dot((xhat * gdy).astype(BF16), bd_ref[...]))).astype(BF16)

        dpa_ref[:, 0:D_ATTN] = head_norm_bwd(q_ref, columns(dqx_ref) * HEAD_DIM ** -0.5, gq_ref, dgq_ref)
        dpa_ref[:, D_ATTN:2 * D_ATTN] = head_norm_bwd(k_ref, columns(dkx_ref) * LN2, gk_ref, dgk_ref)
        dpa_ref[:, 2 * D_ATTN:3 * D_ATTN] = columns(dvx_ref).astype(BF16)

        dc = jnp.zeros((TM, LANES), F32)
        for h in range(N_HEADS):
            both = jnp.where(lane == LANE_C, dqx_ref[h], 0.0) - jnp.where(lane == LANE_ONE, dkx_ref[h], 0.0)
            dc = jnp.where(lane == h, jnp.sum(both, axis=1, keepdims=True), dc)
        hi, lo = _split_bf16(dc)
        dlogf = _dot(triu_ref[...], hi) + _dot(triu_ref[...], lo) + carry[...]
        first = lax.broadcasted_iota(jnp.int32, (TM, D_F), 0) == 0
        carry[...] = jnp.sum(jnp.where(first, dlogf, 0.0), axis=0, keepdims=True)
        df = dlogf * _sigmoid(-(pf_ref[...] + fb_ref[...]))
        dfb_ref[...] += jnp.sum(df, axis=0, keepdims=True)
        dpf_ref[...] = df.astype(BF16)

    heads = pl.BlockSpec((N_HEADS, TM, LANES), lambda i: (0, rev(i), 0))
    return pl.pallas_call(
        body,
        grid=(nt,),
        in_specs=[pl.BlockSpec((TM, D_ATTN), lambda i: (rev(i), 0)), pl.BlockSpec((TM, D_ATTN), lambda i: (rev(i), 1)),
                  pl.BlockSpec((TM, D_F), lambda i: (rev(i), 0)), heads, heads, heads,
                  _whole(gq), _whole(gk), _whole(fb), _whole(bd), _whole(triu)],
        out_specs=[pl.BlockSpec((TM, D_QKV), lambda i: (rev(i), 0)), pl.BlockSpec((TM, D_F), lambda i: (rev(i), 0)),
                   pl.BlockSpec((1, D_ATTN), lambda i: (0, 0)), pl.BlockSpec((1, D_ATTN), lambda i: (0, 0)),
                   pl.BlockSpec((1, D_F), lambda i: (0, 0))],
        out_shape=[jax.ShapeDtypeStruct((S, D_QKV), BF16), jax.ShapeDtypeStruct((S, D_F), BF16),
                   jax.ShapeDtypeStruct((1, D_ATTN), F32), jax.ShapeDtypeStruct((1, D_ATTN), F32),
                   jax.ShapeDtypeStruct((1, D_F), F32)],
        scratch_shapes=[pltpu.VMEM((1, D_F), F32)],
        compiler_params=_params("arbitrary"),
        name="attn_bwd_post",
    )(pa, pa, pf, dqx, dkx, dvx, gq, gk, fb, bd, triu)


def _shift_down(ext, k):
    return pltpu.roll(ext, k, 0)[HALO:]


def _shift_up(ext, k, n):
    return pltpu.roll(ext, n + HALO - k, 0)[:n]


def _pool_lane_select(a2, a4, a8, a16, lane):
    return jnp.where(lane < 64, a2, jnp.where(lane < 128, a4, jnp.where(lane < 192, a8, a16)))


def _local_branches(o, pb, halo, have_prev, row0, wao, convw, wco, wpool, pscale):
    n = pb.shape[0]
    cx = pb[:, 0:D_CONV].astype(F32)
    cb = pb[:, D_CONV:2 * D_CONV].astype(F32)
    cc = pb[:, 2 * D_CONV:3 * D_CONV].astype(F32)
    px = pb[:, 3 * D_CONV:D_LOCAL].astype(F32)
    keep = have_prev.astype(F32)
    z = cc * cx
    z_ext = jnp.concatenate([halo[:, 2 * D_CONV:3 * D_CONV].astype(F32) * halo[:, 0:D_CONV].astype(F32) * keep, z], axis=0)
    z1 = _shift_down(z_ext, 1)
    z2 = _shift_down(z_ext, 2)
    conv = convw[0:1, :] * z2 + convw[1:2, :] * z1 + convw[2:3, :] * z
    cm = cb * conv

    u_ext = jnp.concatenate([halo[:, 3 * D_CONV:D_LOCAL].astype(F32) * keep, px], axis=0)
    s2 = u_ext + pltpu.roll(u_ext, 1, 0)
    s4 = s2 + pltpu.roll(s2, 2, 0)
    s8 = s4 + pltpu.roll(s4, 4, 0)
    s16 = s8 + pltpu.roll(s8, 8, 0)
    lane = lax.broadcasted_iota(jnp.int32, (n, D_POOL), 1)
    win = _pool_lane_select(2.0, 4.0, 8.0, 16.0, lane)
    t = (row0 + lax.broadcasted_iota(jnp.int32, (n, D_POOL), 0)).astype(F32)
    cnt = jnp.minimum(t + 1.0, win)
    feat = _pool_lane_select(s2[HALO:], s4[HALO:], s8[HALO:], s16[HALO:], lane) / cnt - px

    ya = _dot(o, wao)
    yc = _dot(cm.astype(BF16), wco)
    yp_pre = _dot(feat.astype(BF16), wpool)
    yp = yp_pre * pscale
    return dict(cx=cx, cb=cb, cc=cc, z=z, z1=z1, z2=z2, conv=conv, cm=cm, feat=feat, cnt=cnt, lane=lane,
                ya=ya, yc=yc, yp_pre=yp_pre, yp=yp)


def _halo_spec(tm, tile_of):
    per = tm // HALO
    return pl.BlockSpec((HALO, D_LOCAL), lambda i, *_: (jnp.maximum(tile_of(i) * per - 1, 0), 0))


def _heads_as_columns(ref):
    return jnp.concatenate([ref[h] for h in range(N_HEADS)], axis=1)


def mix_out_fwd(x, ox, pb, wao, convw, wco, wpool, pscale, wo_all, l):
    S = x.shape[0]
    tm = TM_MIX

    def body(l_ref, x_ref, o_ref, pb_ref, halo_ref, wao_ref, cw_ref, wco_ref, wp_ref, ps_ref, wo_ref, y_ref):
        i = pl.program_id(0)
        pb = pb_ref[...]
        o = _heads_as_columns(o_ref).astype(BF16)
        b = _local_branches(o, pb, halo_ref[...], i > 0, i * tm, wao_ref[...], cw_ref[...], wco_ref[...],
                            wp_ref[...], ps_ref[...])
        g0 = _sigmoid(pb[:, D_LOCAL:D_LOCAL + D_MODEL].astype(F32))
        g1 = _sigmoid(pb[:, D_LOCAL + D_MODEL:D_LOCAL + 2 * D_MODEL].astype(F32))
        g2 = _sigmoid(pb[:, D_LOCAL + 2 * D_MODEL:D_B].astype(F32))
        merged = g0 * b["ya"] + g1 * b["yc"] + g2 * b["yp"]
        y_ref[...] = x_ref[...] + _dot(merged.astype(BF16), wo_ref[...])

    return pl.pallas_call(
        body,
        grid_spec=pltpu.PrefetchScalarGridSpec(
            num_scalar_prefetch=1,
            grid=(S // tm,),
            in_specs=[_rows(tm, D_MODEL), pl.BlockSpec((N_HEADS, tm, LANES), lambda i, l: (0, i, 0)), _rows(tm, D_B),
                      _halo_spec(tm, lambda i: i), _whole(wao), _whole(convw), _whole(wco), _whole(wpool), _whole(pscale),
                      _layer((D_MODEL, D_MODEL))],
            out_specs=_rows(tm, D_MODEL),
        ),
        out_shape=jax.ShapeDtypeStruct((S, D_MODEL), F32),
        compiler_params=_params("parallel"),
        name="mix_out_fwd",
    )(_layer_index(l), x, ox, pb, pb, wao, convw, wco, wpool, pscale, wo_all)


def mix_out_bwd(dxm, ox, pb, wao, convw, wco, wpool, pscale, wo_all, l):
    S = dxm.shape[0]
    tm = TM_MIX
    nt = S // tm
    rev = lambda i: nt - 1 - i
    rows = lambda n: pl.BlockSpec((tm, n), lambda i, l: (rev(i), 0))
    heads = pl.BlockSpec((N_HEADS, tm, LANES), lambda i, l: (0, rev(i), 0))
    acc = lambda r, c: pl.BlockSpec((r, c), lambda i, l: (0, 0))

    def body(l_ref, dxm_ref, o_ref, pb_ref, halo_ref, wao_ref, cw_ref, wco_ref, wp_ref, ps_ref, wo_ref,
             dpb_ref, do_ref, dwo_ref, dwao_ref, dwco_ref, dwp_ref, dcw_ref, dps_ref, next_dconv, next_e):
        i = pl.program_id(0)
        r = rev(i)

        @pl.when(i == 0)
        def _():
            for ref in (dwo_ref, dwao_ref, dwco_ref, dwp_ref, dcw_ref, dps_ref, next_dconv, next_e):
                ref[...] = jnp.zeros_like(ref)

        pb = pb_ref[...]
        o = _heads_as_columns(o_ref).astype(BF16)
        cw = cw_ref[...]
        b = _local_branches(o, pb, halo_ref[...], r > 0, r * tm, wao_ref[...], cw, wco_ref[...], wp_ref[...], ps_ref[...])
        g0 = _sigmoid(pb[:, D_LOCAL:D_LOCAL + D_MODEL].astype(F32))
        g1 = _sigmoid(pb[:, D_LOCAL + D_MODEL:D_LOCAL + 2 * D_MODEL].astype(F32))
        g2 = _sigmoid(pb[:, D_LOCAL + 2 * D_MODEL:D_B].astype(F32))
        dxb = dxm_ref[...].astype(BF16)
        merged = g0 * b["ya"] + g1 * b["yc"] + g2 * b["yp"]
        dwo_ref[...] += _dot_tn(merged.astype(BF16), dxb)
        dmer = _dot_nt(dxb, wo_ref[...])
        dpb_ref[:, D_LOCAL:D_LOCAL + D_MODEL] = (dmer * b["ya"] * (g0 * (1.0 - g0))).astype(BF16)
        dpb_ref[:, D_LOCAL + D_MODEL:D_LOCAL + 2 * D_MODEL] = (dmer * b["yc"] * (g1 * (1.0 - g1))).astype(BF16)
        dpb_ref[:, D_LOCAL + 2 * D_MODEL:D_B] = (dmer * b["yp"] * (g2 * (1.0 - g2))).astype(BF16)

        dya = (dmer * g0).astype(BF16)
        dwao_ref[...] += _dot_tn(o, dya)
        da = _dot_nt(dya, wao_ref[...]).astype(BF16)
        lane = lax.broadcasted_iota(jnp.int32, (tm, LANES), 1)
        for h in range(N_HEADS):
            dah = da[:, h * LANES:(h + 1) * LANES]
            d1, d2, d3 = _pieces(-jnp.sum(dah.astype(F32) * o_ref[h], axis=1, keepdims=True))
            do_ref[h] = jnp.where(lane == LANE_C + 1, d1, jnp.where(lane == LANE_C + 2, d2,
                                                                  jnp.where(lane == LANE_C + 3, d3, dah)))

        dyc = (dmer * g1).astype(BF16)
        dwco_ref[...] += _dot_tn(b["cm"].astype(BF16), dyc)
        dcm = _dot_nt(dyc, wco_ref[...])
        dconv = dcm * b["cb"]
        dcw_ref[0:1, :] += jnp.sum(dconv * b["z2"], axis=0, keepdims=True)
        dcw_ref[1:2, :] += jnp.sum(dconv * b["z1"], axis=0, keepdims=True)
        dcw_ref[2:3, :] += jnp.sum(dconv * b["z"], axis=0, keepdims=True)
        d_ext = jnp.concatenate([dconv, next_dconv[...]], axis=0)
        dz = cw[2:3, :] * dconv + cw[1:2, :] * _shift_up(d_ext, 1, tm) + cw[0:1, :] * _shift_up(d_ext, 2, tm)
        next_dconv[...] = dconv[0:HALO]
        dpb_ref[:, 0:D_CONV] = (dz * b["cc"]).astype(BF16)
        dpb_ref[:, D_CONV:2 * D_CONV] = (dcm * b["conv"]).astype(BF16)
        dpb_ref[:, 2 * D_CONV:3 * D_CONV] = (dz * b["cx"]).astype(BF16)

        dyp = dmer * g2
        dps_ref[...] += jnp.sum(dyp * b["yp_pre"], axis=0, keepdims=True)
        dyps = (dyp * ps_ref[...]).astype(BF16)
        dwp_ref[...] += _dot_tn(b["feat"].astype(BF16), dyps)
        dfeat = _dot_nt(dyps, wp_ref[...])
        e = dfeat / b["cnt"]
        e_ext = jnp.concatenate([e, next_e[...]], axis=0)
        up = lambda a, k: pltpu.roll(a, tm + HALO - k, 0)
        f2 = e_ext + up(e_ext, 1)
        f4 = f2 + up(f2, 2)
        f8 = f4 + up(f4, 4)
        f16 = f8 + up(f8, 8)
        next_e[...] = e[0:HALO]
        dpb_ref[:, 3 * D_CONV:D_LOCAL] = (_pool_lane_select(f2[:tm], f4[:tm], f8[:tm], f16[:tm], b["lane"]) - dfeat).astype(BF16)

    return pl.pallas_call(
        body,
        grid_spec=pltpu.PrefetchScalarGridSpec(
            num_scalar_prefetch=1,
            grid=(nt,),
            in_specs=[rows(D_MODEL), heads, rows(D_B), _halo_spec(tm, rev),
                      _whole(wao), _whole(convw), _whole(wco), _whole(wpool), _whole(pscale), _layer((D_MODEL, D_MODEL))],
            out_specs=[rows(D_B), heads, acc(D_MODEL, D_MODEL), acc(D_HEADS, D_MODEL), acc(D_CONV, D_MODEL),
                       acc(D_POOL, D_MODEL), acc(8, D_CONV), acc(1, D_MODEL)],
            scratch_shapes=[pltpu.VMEM((HALO, D_CONV), F32), pltpu.VMEM((HALO, D_POOL), F32)],
        ),
        out_shape=[jax.ShapeDtypeStruct((S, D_B), BF16), jax.ShapeDtypeStruct((N_HEADS, S, LANES), BF16),
                   jax.ShapeDtypeStruct((D_MODEL, D_MODEL), F32), jax.ShapeDtypeStruct((D_HEADS, D_MODEL), F32),
                   jax.ShapeDtypeStruct((D_CONV, D_MODEL), F32), jax.ShapeDtypeStruct((D_POOL, D_MODEL), F32),
                   jax.ShapeDtypeStruct((8, D_CONV), F32), jax.ShapeDtypeStruct((1, D_MODEL), F32)],
        compiler_params=_params("arbitrary"),
        name="mix_out_bwd",
    )(_layer_index(l), dxm, ox, pb, pb, wao, convw, wco, wpool, pscale, wo_all)


FF_SHARD = 2 * D_FF // 4


def ffn_fwd(x, g, w1_all, w2_all, l):
    S = x.shape[0]

    def body(l_ref, x_ref, g_ref, w1_ref, w2_ref, y_ref, u_ref):
        xf = x_ref[...]
        r = lax.rsqrt(jnp.mean(xf * xf, axis=-1, keepdims=True) + EPS)
        h = (xf * r * g_ref[...]).astype(BF16)
        u = jnp.concatenate([_dot(h, w1_ref[j]) for j in range(4)], axis=1)
        u_ref[...] = u.astype(BF16)
        gt = u[:, 0:D_FF]
        act = gt * _sigmoid(gt) * u[:, D_FF:2 * D_FF]
        y_ref[...] = xf + _dot(act.astype(BF16), w2_ref[...])

    return pl.pallas_call(
        body,
        grid_spec=pltpu.PrefetchScalarGridSpec(
            num_scalar_prefetch=1,
            grid=(S // TM,),
            in_specs=[_rows(TM, D_MODEL), _whole(g), _layer((4, D_MODEL, FF_SHARD)), _layer((D_FF, D_MODEL))],
            out_specs=[_rows(TM, D_MODEL), _rows(TM, 2 * D_FF)],
        ),
        out_shape=[jax.ShapeDtypeStruct((S, D_MODEL), F32), jax.ShapeDtypeStruct((S, 2 * D_FF), BF16)],
        compiler_params=_params("parallel"),
        name="ffn_fwd",
    )(_layer_index(l), x, g, w1_all, w2_all)


def ffn_bwd(x, g, dy, u, w1_all, w2_all, l):
    S = x.shape[0]

    def body(l_ref, x_ref, g_ref, dy_ref, u_ref, w1_ref, w2_ref, dx_ref, du_ref, act_ref, h_ref, dyb_ref, dg_ref):
        @pl.when(pl.program_id(0) == 0)
        def _():
            dg_ref[...] = jnp.zeros_like(dg_ref)

        dyf = dy_ref[...]
        dyb_ref[...] = dyf.astype(BF16)
        dact = _dot_nt(dyb_ref[...], w2_ref[...])
        gt = u_ref[:, 0:D_FF].astype(F32)
        up = u_ref[:, D_FF:2 * D_FF].astype(F32)
        sg = _sigmoid(gt)
        silu = gt * sg
        act_ref[...] = (silu * up).astype(BF16)
        du_ref[:, 0:D_FF] = (dact * up * (sg * (1.0 + gt * (1.0 - sg)))).astype(BF16)
        du_ref[:, D_FF:2 * D_FF] = (dact * silu).astype(BF16)
        dh = _dot_nt(du_ref[:, 0:FF_SHARD], w1_ref[0])
        for j in range(1, 4):
            dh = dh + _dot_nt(du_ref[:, j * FF_SHARD:(j + 1) * FF_SHARD], w1_ref[j])
        xf = x_ref[...]
        r = lax.rsqrt(jnp.mean(xf * xf, axis=-1, keepdims=True) + EPS)
        xhat = xf * r
        h_ref[...] = (xhat * g_ref[...]).astype(BF16)
        dg_ref[...] += jnp.sum(dh * xhat, axis=0, keepdims=True)
        gdh = dh * g_ref[...]
        dx_ref[...] = dyf + r * (gdh - xhat * jnp.mean(xhat * gdh, axis=-1, keepdims=True))

    return pl.pallas_call(
        body,
        grid_spec=pltpu.PrefetchScalarGridSpec(
            num_scalar_prefetch=1,
            grid=(S // TM,),
            in_specs=[_rows(TM, D_MODEL), _whole(g), _rows(TM, D_MODEL), _rows(TM, 2 * D_FF),
                      _layer((4, D_MODEL, FF_SHARD)), _layer((D_FF, D_MODEL))],
            out_specs=[_rows(TM, D_MODEL), _rows(TM, 2 * D_FF), _rows(TM, D_FF), _rows(TM, D_MODEL), _rows(TM, D_MODEL),
                       pl.BlockSpec((1, D_MODEL), lambda i, l: (0, 0))],
        ),
        out_shape=[jax.ShapeDtypeStruct((S, D_MODEL), F32), jax.ShapeDtypeStruct((S, 2 * D_FF), BF16),
                   jax.ShapeDtypeStruct((S, D_FF), BF16), jax.ShapeDtypeStruct((S, D_MODEL), BF16),
                   jax.ShapeDtypeStruct((S, D_MODEL), BF16), jax.ShapeDtypeStruct((1, D_MODEL), F32)],
        compiler_params=_params("arbitrary"),
        name="ffn_bwd",
    )(_layer_index(l), x, g, dy, u, w1_all, w2_all)


def loss_head(y, target):
    S = y.shape[0]

    def body(y_ref, t_ref, loss_ref, dy_ref):
        @pl.when(pl.program_id(0) == 0)
        def _():
            loss_ref[0, 0] = 0.0

        err = y_ref[...] - t_ref[...]
        dy_ref[...] = err * (1.0 / D_MODEL)
        loss_ref[0, 0] += 0.5 * jnp.sum(jnp.mean(err * err, axis=-1))

    return pl.pallas_call(
        body,
        grid=(S // TM,),
        in_specs=[_rows(TM, D_MODEL), _rows(TM, D_MODEL)],
        out_specs=[pl.BlockSpec((1, 1), lambda i: (0, 0), memory_space=pltpu.SMEM), _rows(TM, D_MODEL)],
        out_shape=[jax.ShapeDtypeStruct((1, 1), F32), jax.ShapeDtypeStruct((S, D_MODEL), F32)],
        compiler_params=_params("arbitrary"),
        name="loss_head",
    )(y, target)


SPLIT = {
    "w_in": ((D_MODEL, D_IN), 1),
    "w_attn_out": ((D_ATTN, D_MODEL), 1),
    "w_conv_out": ((D_CONV, D_MODEL), 1),
    "pool_w": ((N_GROUPS, D_POOL // N_GROUPS, D_MODEL // N_GROUPS), 2),
    "w_o": ((D_MODEL, D_MODEL), 0),
    "w_ffn_in": ((D_MODEL, 2 * D_FF), 1),
    "w_ffn_out": ((D_FF, D_MODEL), 0),
}
SMALL = {"norm_mix_g": D_MODEL, "forget_b": N_HEADS, "q_norm_g": HEAD_DIM, "k_norm_g": HEAD_DIM,
         "pool_scale": D_MODEL, "norm_ffn_g": D_MODEL, "conv_w": CONV_K * D_CONV}
WEIGHTS = ["norm_mix_g", "w_in", "forget_b", "q_norm_g", "k_norm_g", "w_attn_out", "conv_w", "w_conv_out", "pool_w",
           "pool_scale", "w_o", "norm_ffn_g", "w_ffn_in", "w_ffn_out"]


MIX_W = ["w_in", "w_attn_out", "w_conv_out", "pool_w", "w_o"]
FFN_W = ["w_ffn_in", "w_ffn_out"]


def _row(a):
    return a.astype(F32).reshape(1, -1)


def mix_fwd(x, full, li, small):
    n = full["w_o"].shape[0]
    wa, wf, wb = w_in_prep(full["w_in"], li)
    wao, wco, wpool, convw = branch_w_prep(full["w_attn_out"], full["w_conv_out"], full["pool_w"], full["conv_w"], li)
    w = dict(wa=wa, wf=wf, wb=wb, wao=wao, wco=wco, wpool=wpool, convw=convw,
             w_o=full["w_o"].reshape(n, D_MODEL, D_MODEL), at=li,
             g1=_row(small["norm_mix_g"]), pscale=_row(small["pool_scale"]),
             gq=_row(jnp.tile(small["q_norm_g"], N_HEADS)), gk=_row(jnp.tile(small["k_norm_g"], N_HEADS)),
             fb=_row(jnp.pad(small["forget_b"], (0, D_F - N_HEADS))))
    h, pa, pf, pb = in_proj_fwd(x, w["g1"], wa, wf, wb)
    qx, kx, vx = qk_prep(pa, pf, w["gq"], w["gk"], w["fb"])
    ox, qxb = attn_fwd(qx, kx, vx)
    xm = mix_out_fwd(x, ox, pb, wao, convw, wco, wpool, w["pscale"], w["w_o"], li)
    return xm, w, dict(x=x, h=h, pa=pa, pf=pf, pb=pb, kx=kx, vx=vx, ox=ox, qxb=qxb)


def ffn_half_fwd(xm, full, li, g2):
    n = full["w_ffn_out"].shape[0]
    w = dict(w1=full["w_ffn_in"], w2=full["w_ffn_out"].reshape(n, D_FF, D_MODEL), at=li, g2=_row(g2))
    y, u = ffn_fwd(xm, w["g2"], w["w1"], w["w2"], li)
    return y, w, dict(xm=xm, u=u)


def ffn_half_bwd(dx, w, s, gi, big):
    n = big["w_ffn_out"].shape[0]
    big = dict(big)
    dxm, du, act, h2, dyb, dg2 = ffn_bwd(s["xm"], w["g2"], dx, s["u"], w["w1"], w["w2"], w["at"])
    big["w_ffn_out"] = wgrad_into(act, dyb, "wgrad_ffn_out", big["w_ffn_out"].reshape(n, D_FF, D_MODEL),
                                  gi).reshape(big["w_ffn_out"].shape)
    big["w_ffn_in"] = wgrad_into(h2, du, "wgrad_ffn_in", big["w_ffn_in"], gi)
    return dxm, big, dict(norm_ffn_g=dg2[0])


def mix_bwd_weights(dxm, w, s, gi, big):
    big = dict(big)
    dpb, dox, dwo, dwao, dwco, dwpool, dconvw, dpscale = mix_out_bwd(
        dxm, s["ox"], s["pb"], w["wao"], w["convw"], w["wco"], w["wpool"], w["pscale"], w["w_o"], w["at"])
    dqx, dkx, dvx = attn_bwd(s["qxb"], s["kx"], s["vx"], dox)
    dpa, dpf, dgq, dgk, dfb = attn_bwd_post(s["pa"], s["pf"], dqx, dkx, dvx, w["gq"], w["gk"], w["fb"])
    big["w_in"] = w_in_unprep(wgrad(s["h"], dpa, "wgrad_in_qkv"), wgrad(s["h"], dpf, "wgrad_in_f"),
                              wgrad(s["h"], dpb, "wgrad_in_b"), big["w_in"], gi)
    big["w_attn_out"], big["w_conv_out"], big["pool_w"], big["w_o"] = branch_g_place(
        dwao, dwco, dwpool, dwo, (big["w_attn_out"], big["w_conv_out"], big["pool_w"], big["w_o"]), gi)
    sm = dict(forget_b=dfb[0, 0:N_HEADS], q_norm_g=dgq.reshape(N_HEADS, HEAD_DIM).sum(0),
              k_norm_g=dgk.reshape(N_HEADS, HEAD_DIM).sum(0), pool_scale=dpscale[0],
              conv_w=dconvw[0:CONV_K].reshape(-1))
    return (dpa, dpf, dpb), big, sm


def mix_bwd_input(dproj, dxm, w, s):
    dx, dg1 = in_proj_bwd(s["x"], w["g1"], dxm, *dproj, w["wa"], w["wf"], w["wb"])
    return dx, dict(norm_mix_g=dg1[0])


def mix_bwd(dxm, w, s, gi, big):
    dproj, big, sm = mix_bwd_weights(dxm, w, s, gi, big)
    dx, g1 = mix_bwd_input(dproj, dxm, w, s)
    return dx, big, {**sm, **g1}


def grad_buffers(full):
    return {n: lax.empty(full[n].shape, F32) for n in SPLIT}


def local_step(x, target, gathered, small):
    n_layers = small["norm_mix_g"].shape[0]
    done = []
    for l in range(n_layers):
        xm, wm, sm = mix_fwd(x, gathered, l, {n: v[l] for n, v in small.items()})
        x, wf, sf = ffn_half_fwd(xm, gathered, l, small["norm_ffn_g"][l])
        done.append((wm, sm, wf, sf))
    loss, dx = loss_head(x, target)
    big = grad_buffers(gathered)
    small_grads = [None] * n_layers
    for l in reversed(range(n_layers)):
        wm, sm, wf, sf = done[l]
        dxm, ffn_big, g_ffn = ffn_half_bwd(dx, wf, sf, l, {n: big[n] for n in FFN_W})
        dx, mix_big, g_mix = mix_bwd(dxm, wm, sm, l, {n: big[n] for n in MIX_W})
        big = {**ffn_big, **mix_big}
        small_grads[l] = {**g_ffn, **g_mix}
    return loss, dx, big, {n: jnp.stack([g[n] for g in small_grads]) for n in SMALL}


def adamw(w, g, m, v, name):
    R, C = w.shape
    tm = 256 if R % 256 == 0 else R

    def body(w_ref, g_ref, m_ref, v_ref, d_ref, nm_ref, nv_ref):
        gr = g_ref[...]
        m_new = ADAM_B1 * m_ref[...] + (1.0 - ADAM_B1) * gr
        v_new = ADAM_B2 * v_ref[...] + (1.0 - ADAM_B2) * jnp.square(gr)
        nm_ref[...] = m_new
        nv_ref[...] = v_new
        m_hat = m_new / (1.0 - ADAM_B1 ** ADAM_STEP)
        v_hat = v_new / (1.0 - ADAM_B2 ** ADAM_STEP)
        d_ref[...] = -ADAM_LR * (m_hat / (jnp.sqrt(v_hat) + ADAM_EPS) + ADAM_WD * w_ref[...])

    spec = _rows(tm, C)
    out = jax.ShapeDtypeStruct((R, C), F32)
    return pl.pallas_call(
        body,
        grid=(R // tm,),
        in_specs=[spec] * 4,
        out_specs=[spec] * 3,
        out_shape=[out] * 3,
        compiler_params=_params("parallel"),
        name=name,
    )(w, g, m, v)


MESH = pl.DeviceIdType.MESH
HBM_REF = pl.BlockSpec(memory_space=pl.ANY)
N_CHIPS = 4
N_DEV = 8
SMALL_SHAPE = (128, LANES)


def _mesh_pos():
    return lax.axis_index("x"), lax.axis_index("y"), lax.axis_index("c")


def _other_chips(x, y):
    return [(1 - x, y), (x, 1 - y), (1 - x, 1 - y)]


def _remote(src, dst, send_sem, recv_sem, to):
    return pltpu.make_async_remote_copy(src_ref=src, dst_ref=dst, send_sem=send_sem, recv_sem=recv_sem,
                                        device_id=to, device_id_type=MESH)


def _row_tile(rows):
    for tm in (256, 176, 128):
        if rows % tm == 0:
            return tm
    return rows


def _as4(a):
    return a.reshape(a.shape[0], a.shape[1], -1, a.shape[-1])


def _by_columns(shape):
    return (shape[-2] // 2) % 8 != 0


def _core_rows(buf, c):
    R, C = buf.shape[-2:]
    if R % 2:
        whole = (pl.ds(0, R), pl.ds(0, C))
        return whole, whole, False
    if _by_columns(buf.shape):
        return (pl.ds(0, R), pl.ds(c * (C // 2), C // 2)), (pl.ds(0, R), pl.ds((1 - c) * (C // 2), C // 2)), True
    return (pl.ds(c * (R // 2), R // 2), pl.ds(0, C)), (pl.ds((1 - c) * (R // 2), R // 2), pl.ds(0, C)), True


def _half_shape(g):
    R, C = g.shape[-2:]
    return g.shape[:-2] + ((R, C // 2) if _by_columns(g.shape) else (R // 2, C))


def place_shard(chip, w, lo, hi, dtype, name, after=None):
    w3 = w.reshape(w.shape[0], -1, w.shape[-1])
    _, R, C = w3.shape
    tm = _row_tile(R)
    idle = [] if after is None else [after]

    def body(chip_ref, w_ref, *rest):
        rest[-1][...] = w_ref[...].astype(dtype)

    return pl.pallas_call(
        body,
        grid_spec=pltpu.PrefetchScalarGridSpec(
            num_scalar_prefetch=1,
            grid=(hi - lo, R // tm),
            in_specs=[pl.BlockSpec((None, tm, C), lambda l, i, chip: (lo + l, i, 0))] + [_whole(a) for a in idle],
            out_specs=pl.BlockSpec((None, None, tm, C), lambda l, i, chip: (l, chip[0], i, 0)),
        ),
        out_shape=jax.ShapeDtypeStruct((hi - lo, N_CHIPS, R, C), dtype),
        compiler_params=_params("parallel", "parallel"),
        name=name,
    )(chip, w3, *idle)


HBM_SPACE = pl.BlockSpec(memory_space=pltpu.HBM)
SEM_SPACE = pl.BlockSpec(memory_space=pltpu.SEMAPHORE)
IN_FLIGHT = pltpu.SideEffectType.DATAFLOW_SIDE_EFFECTING


def _sem_table(send_sems, recv_sems):
    return lambda t, j: (send_sems.at[t, j], recv_sems.at[t, j])


def _sem_per_peer(send_sems, recv_sems):
    return lambda t, j: (send_sems[j], recv_sems[j])


def _gather_ici(bufs, sem, sends=True, lands=True):
    x, y, c = _mesh_pos()
    me = 2 * x + y
    out, into = [], []
    for t, buf in enumerate(bufs):
        mine, _, _ = _core_rows(buf, c)
        part = lambda k, buf=buf, mine=mine: buf.at[pl.ds(0, buf.shape[0]), k, *mine]
        for j, (px, py) in enumerate(_other_chips(x, y)):
            if sends:
                out.append(_remote(part(me), part(me), *sem(t, j), (px, py, c)))
            if lands:
                into.append(_remote(part(2 * px + py), part(2 * px + py), *sem(t, j), (px, py, c)))
    return out, into


def _gather_d2d(bufs, send_sems, recv_sems, first):
    x, y, c = _mesh_pos()
    sends, lands = [], []
    for t, buf in enumerate(bufs):
        mine, theirs, split = _core_rows(buf, c)
        if not split:
            continue
        for j, (px, py) in enumerate(_other_chips(x, y)):
            part = lambda half, buf=buf, k=2 * px + py: buf.at[pl.ds(0, buf.shape[0]), k, *half]
            sems = (send_sems.at[t, first + j], recv_sems.at[t, first + j], (x, y, 1 - c))
            sends.append(_remote(part(mine), part(mine), *sems))
            lands.append(_remote(part(theirs), part(theirs), *sems))
    return sends, lands


def gather_shards(bufs):
    n = len(bufs)

    def body(*refs):
        outs = refs[n:2 * n]
        send_sems, recv_sems = refs[2 * n:]
        ici_out, ici_in = _gather_ici(outs, _sem_table(send_sems, recv_sems))
        d2d_out, d2d_in = _gather_d2d(outs, send_sems, recv_sems, 3)
        for cp in ici_out:
            cp.start()
        for cp in ici_in:
            cp.wait_recv()
        for cp in d2d_out:
            cp.start()
        for cp in d2d_in:
            cp.wait_recv()
        for cp in ici_out + d2d_out:
            cp.wait_send()

    return pl.pallas_call(
        body,
        in_specs=[HBM_REF] * n,
        out_specs=[HBM_REF] * n,
        out_shape=[jax.ShapeDtypeStruct(b.shape, b.dtype) for b in bufs],
        input_output_aliases={t: t for t in range(n)},
        scratch_shapes=[pltpu.SemaphoreType.DMA((n, 6)), pltpu.SemaphoreType.DMA((n, 6))],
        name="gather_shards",
    )(*bufs)


def gather_start(bufs):
    n = len(bufs)

    def body(*refs):
        send_sems, recv_sems = refs[n:n + 3], refs[n + 3:n + 6]
        outs = refs[n + 6:2 * n + 6]
        token = refs[2 * n + 6]
        for cp in _gather_ici(outs, _sem_per_peer(send_sems, recv_sems), lands=False)[0]:
            cp.start()
        token[...] = jnp.zeros_like(token)

    res = pl.pallas_call(
        body,
        in_specs=[HBM_SPACE] * n,
        out_specs=[SEM_SPACE] * 6 + [HBM_SPACE] * n + [pl.BlockSpec(memory_space=pltpu.VMEM)],
        out_shape=[pltpu.SemaphoreType.DMA(())] * 6
        + [pltpu.HBM(b.shape, b.dtype) for b in bufs] + [jax.ShapeDtypeStruct((8, LANES), F32)],
        input_output_aliases={t: t + 6 for t in range(n)},
        compiler_params=pltpu.CompilerParams(has_side_effects=IN_FLIGHT),
        name="gather_start",
    )(*[pltpu.with_memory_space_constraint(b, pltpu.HBM) for b in bufs])
    return list(res[0:3]), list(res[3:6]), list(res[6:n + 6]), res[n + 6]


def gather_wait(send_sems, recv_sems, bufs, after):
    n = len(bufs)

    def body(*refs):
        ins_send, ins_recv = refs[n:n + 3], refs[n + 3:n + 6]
        outs = refs[n + 7:]
        sends, lands = _gather_ici(outs, _sem_per_peer(ins_send, ins_recv))
        for cp in lands:
            cp.wait_recv()
        for cp in sends:
            cp.wait_send()

    return list(pl.pallas_call(
        body,
        in_specs=[HBM_SPACE] * n + [SEM_SPACE] * 6 + [HBM_REF],
        out_specs=[HBM_SPACE] * n,
        out_shape=[pltpu.HBM(b.shape, b.dtype) for b in bufs],
        input_output_aliases={t: t for t in range(n)},
        compiler_params=pltpu.CompilerParams(has_side_effects=IN_FLIGHT),
        name="gather_wait",
    )(*bufs, *send_sems, *recv_sems, after))


def gather_forward(bufs):
    n = len(bufs)

    def body(*refs):
        outs = refs[n:2 * n]
        send_sems, recv_sems = refs[2 * n:]
        sends, lands = _gather_d2d(outs, send_sems, recv_sems, 0)
        for cp in sends:
            cp.start()
        for cp in lands:
            cp.wait_recv()
        for cp in sends:
            cp.wait_send()

    return list(pl.pallas_call(
        body,
        in_specs=[HBM_REF] * n,
        out_specs=[HBM_REF] * n,
        out_shape=[jax.ShapeDtypeStruct(b.shape, b.dtype) for b in bufs],
        input_output_aliases={t: t for t in range(n)},
        scratch_shapes=[pltpu.SemaphoreType.DMA((n, 3)), pltpu.SemaphoreType.DMA((n, 3))],
        name="gather_forward",
    )(*bufs))


def pair_exchange(grads):
    n = len(grads)

    def body(*refs):
        ins, outs = refs[:n], refs[n:2 * n]
        send_sems, recv_sems = refs[2 * n:]
        cps = _pair_copies(ins, outs, lambda t: (send_sems.at[t], recv_sems.at[t]))
        for cp in cps:
            cp.start()
        for cp in cps:
            cp.wait()

    return list(pl.pallas_call(
        body,
        in_specs=[HBM_REF] * n,
        out_specs=[HBM_REF] * n,
        out_shape=[jax.ShapeDtypeStruct(_half_shape(g), g.dtype) for g in grads],
        scratch_shapes=[pltpu.SemaphoreType.DMA((n,)), pltpu.SemaphoreType.DMA((n,))],
        name="pair_exchange",
    )(*grads))


def _pair_copies(grads, lands, sem):
    x, y, c = _mesh_pos()
    cps = []
    for t in range(len(grads)):
        _, theirs, _ = _core_rows(grads[t], c)
        src = grads[t].at[pl.ds(0, grads[t].shape[0]), pl.ds(0, N_CHIPS), *theirs]
        cps.append(_remote(src, lands[t], *sem(t), (x, y, 1 - c)))
    return cps


def pair_exchange_start(grads):
    n = len(grads)
    lands = [lax.empty(_half_shape(g), g.dtype) for g in grads]

    def body(*refs):
        send_sem, recv_sem = refs[2 * n:2 * n + 2]
        outs = refs[2 * n + 2:4 * n + 2]
        token = refs[4 * n + 2]
        for cp in _pair_copies(outs[:n], outs[n:], lambda t: (send_sem, recv_sem)):
            cp.start()
        token[...] = jnp.zeros_like(token)

    res = pl.pallas_call(
        body,
        in_specs=[HBM_SPACE] * (2 * n),
        out_specs=[SEM_SPACE] * 2 + [HBM_SPACE] * (2 * n) + [pl.BlockSpec(memory_space=pltpu.VMEM)],
        out_shape=[pltpu.SemaphoreType.DMA(())] * 2 + [pltpu.HBM(a.shape, a.dtype) for a in list(grads) + lands]
        + [jax.ShapeDtypeStruct((8, LANES), F32)],
        input_output_aliases={t: t + 2 for t in range(2 * n)},
        compiler_params=pltpu.CompilerParams(has_side_effects=IN_FLIGHT),
        name="pair_exchange_start",
    )(*[pltpu.with_memory_space_constraint(a, pltpu.HBM) for a in list(grads) + lands])
    return res[0], res[1], list(res[2:n + 2]), list(res[n + 2:2 * n + 2]), res[2 * n + 2]


def pair_exchange_wait(send_sem, recv_sem, grads, lands, after):
    n = len(grads)

    def body(*refs):
        in_send, in_recv = refs[2 * n:2 * n + 2]
        outs = refs[2 * n + 3:]
        for cp in _pair_copies(outs[:n], outs[n:], lambda t: (in_send, in_recv)):
            cp.wait()

    res = pl.pallas_call(
        body,
        in_specs=[HBM_SPACE] * (2 * n) + [SEM_SPACE] * 2 + [HBM_REF],
        out_specs=[HBM_SPACE] * (2 * n),
        out_shape=[pltpu.HBM(a.shape, a.dtype) for a in list(grads) + list(lands)],
        input_output_aliases={t: t for t in range(2 * n)},
        compiler_params=pltpu.CompilerParams(has_side_effects=IN_FLIGHT),
        name="pair_exchange_wait",
    )(*grads, *lands, send_sem, recv_sem, after)
    return list(res[:n]), list(res[n:])


def pair_sum(core, g, t, name):
    n, _, hr, hc = t.shape
    R, C = g.shape[2:]
    tm = _row_tile(hr)
    per = hr // tm
    mine = (lambda a, i, c: (a, i, c[0])) if _by_columns(g.shape) else (lambda a, i, c: (a, per * c[0] + i, 0))

    def body(c_ref, g_ref, t_ref, o_ref):
        o_ref[...] = (g_ref[...] + t_ref[...]).astype(BF16)

    tile = pl.BlockSpec((None, tm, hc), lambda a, i, c: (a, i, 0))
    out = pl.pallas_call(
        body,
        grid_spec=pltpu.PrefetchScalarGridSpec(
            num_scalar_prefetch=1,
            grid=(n * N_CHIPS, per),
            in_specs=[pl.BlockSpec((None, tm, hc), mine), tile],
            out_specs=tile,
        ),
        out_shape=jax.ShapeDtypeStruct((n * N_CHIPS, hr, hc), BF16),
        compiler_params=_params("parallel", "parallel"),
        name=name,
    )(core, g.reshape(n * N_CHIPS, R, C), t.reshape(n * N_CHIPS, hr, hc))
    return out.reshape(t.shape)


def _chip_copies(sums, recv, sem):
    x, y, c = _mesh_pos()
    cps = []
    for t in range(len(sums)):
        for j, (px, py) in enumerate(_other_chips(x, y)):
            src = sums[t].at[pl.ds(0, sums[t].shape[0]), 2 * px + py]
            cps.append(_remote(src, recv[t].at[j], *sem(t, j), (px, py, c)))
    return cps


def _recv_shape(s):
    return (N_CHIPS - 1, s.shape[0]) + s.shape[2:]


def chip_exchange(sums):
    n = len(sums)

    def body(*refs):
        ins, outs = refs[:n], refs[n:2 * n]
        cps = _chip_copies(ins, outs, _sem_table(*refs[2 * n:]))
        for cp in cps:
            cp.start()
        for cp in cps:
            cp.wait()

    return list(pl.pallas_call(
        body,
        in_specs=[HBM_REF] * n,
        out_specs=[HBM_REF] * n,
        out_shape=[jax.ShapeDtypeStruct(_recv_shape(s), s.dtype) for s in sums],
        scratch_shapes=[pltpu.SemaphoreType.DMA((n, 3)), pltpu.SemaphoreType.DMA((n, 3))],
        name="chip_exchange",
    )(*sums))


def chip_exchange_start(sums):
    n = len(sums)
    lands = [lax.empty(_recv_shape(s), s.dtype) for s in sums]

    def body(*refs):
        send_sems, recv_sems = refs[2 * n:2 * n + 3], refs[2 * n + 3:2 * n + 6]
        outs = refs[2 * n + 6:4 * n + 6]
        token = refs[4 * n + 6]
        for cp in _chip_copies(outs[:n], outs[n:2 * n], _sem_per_peer(send_sems, recv_sems)):
            cp.start()
        token[...] = jnp.zeros_like(token)

    res = pl.pallas_call(
        body,
        in_specs=[HBM_SPACE] * (2 * n),
        out_specs=[SEM_SPACE] * 6 + [HBM_SPACE] * (2 * n) + [pl.BlockSpec(memory_space=pltpu.VMEM)],
        out_shape=[pltpu.SemaphoreType.DMA(())] * 6 + [pltpu.HBM(a.shape, a.dtype) for a in list(sums) + lands]
        + [jax.ShapeDtypeStruct((8, LANES), F32)],
        input_output_aliases={t: t + 6 for t in range(2 * n)},
        compiler_params=pltpu.CompilerParams(has_side_effects=IN_FLIGHT),
        name="chip_exchange_start",
    )(*[pltpu.with_memory_space_constraint(a, pltpu.HBM) for a in list(sums) + lands])
    return list(res[0:3]), list(res[3:6]), list(res[6:n + 6]), list(res[n + 6:2 * n + 6]), res[2 * n + 6]


def chip_exchange_wait(send_sems, recv_sems, sums, lands, after):
    n = len(sums)

    def body(*refs):
        ins_send, ins_recv = refs[2 * n:2 * n + 3], refs[2 * n + 3:2 * n + 6]
        outs = refs[2 * n + 7:]
        for cp in _chip_copies(outs[:n], outs[n:], _sem_per_peer(ins_send, ins_recv)):
            cp.wait()

    res = pl.pallas_call(
        body,
        in_specs=[HBM_SPACE] * (2 * n) + [SEM_SPACE] * 6 + [HBM_REF],
        out_specs=[HBM_SPACE] * (2 * n),
        out_shape=[pltpu.HBM(a.shape, a.dtype) for a in list(sums) + list(lands)],
        input_output_aliases={t: t for t in range(2 * n)},
        compiler_params=pltpu.CompilerParams(has_side_effects=IN_FLIGHT),
        name="chip_exchange_wait",
    )(*sums, *lands, *send_sems, *recv_sems, after)
    return list(res[:n]), list(res[n:])


def chip_sum_into(core, chip, recv, sums, total, lo, name):
    _, n, hr, hc = recv.shape
    tm = _row_tile(hr)
    per = hr // tm
    if _by_columns(total.shape):
        mine = lambda a, i, c, k: (lo + a, i, c[0])
    else:
        mine = lambda a, i, c, k: (lo + a, per * c[0] + i, 0)

    def body(c_ref, k_ref, r_ref, s_ref, t_ref, o_ref):
        acc = s_ref[...].astype(F32)
        for j in range(N_CHIPS - 1):
            acc = acc + r_ref[j].astype(F32)
        o_ref[...] = acc

    return pl.pallas_call(
        body,
        grid_spec=pltpu.PrefetchScalarGridSpec(
            num_scalar_prefetch=2,
            grid=(n, per),
            in_specs=[pl.BlockSpec((N_CHIPS - 1, None, tm, hc), lambda a, i, c, k: (0, a, i, 0)),
                      pl.BlockSpec((None, None, tm, hc), lambda a, i, c, k: (a, k[0], i, 0)),
                      HBM_REF],
            out_specs=pl.BlockSpec((None, tm, hc), mine),
        ),
        out_shape=jax.ShapeDtypeStruct(total.shape, F32),
        input_output_aliases={4: 0},
        compiler_params=_params("parallel", "parallel"),
        name=name,
    )(core, chip, recv, sums, total)


def sibling_share(totals):
    n = len(totals)

    def body(*refs):
        outs = refs[n:2 * n]
        send_sems, recv_sems = refs[2 * n:]
        x, y, c = _mesh_pos()
        half = lambda t, which: outs[t].at[pl.ds(0, DEPTH), *_core_rows(outs[t], c)[which]]
        sent = [_remote(half(t, 0), half(t, 0), send_sems.at[t], recv_sems.at[t], (x, y, 1 - c)) for t in range(n)]
        for cp in sent:
            cp.start()
        for t in range(n):
            _remote(half(t, 1), half(t, 1), send_sems.at[t], recv_sems.at[t], (x, y, 1 - c)).wait_recv()
        for cp in sent:
            cp.wait_send()

    return pl.pallas_call(
        body,
        in_specs=[HBM_REF] * n,
        out_specs=[HBM_REF] * n,
        out_shape=[jax.ShapeDtypeStruct(t.shape, t.dtype) for t in totals],
        input_output_aliases={t: t for t in range(n)},
        scratch_shapes=[pltpu.SemaphoreType.DMA((n,)), pltpu.SemaphoreType.DMA((n,))],
        name="sibling_share",
    )(*totals)


def small_allgather(small):
    def body(s_ref, a_ref, send_sems, recv_sems, local_sem):
        x, y, c = _mesh_pos()
        me = 4 * x + 2 * y + c
        own = pltpu.make_async_copy(s_ref, a_ref.at[me], local_sem)
        own.start()
        sent = []
        for k in range(1, N_DEV):
            peer = (x ^ (k >> 2), y ^ ((k >> 1) & 1), c ^ (k & 1))
            cp = _remote(s_ref, a_ref.at[me], send_sems.at[k - 1], recv_sems.at[k - 1], peer)
            cp.start()
            sent.append(cp)
        for k in range(1, N_DEV):
            px, py, pc = x ^ (k >> 2), y ^ ((k >> 1) & 1), c ^ (k & 1)
            _remote(s_ref, a_ref.at[4 * px + 2 * py + pc], send_sems.at[k - 1], recv_sems.at[k - 1], (px, py, pc)).wait_recv()
        for cp in sent:
            cp.wait_send()
        own.wait()

    return pl.pallas_call(
        body,
        in_specs=[HBM_REF],
        out_specs=HBM_REF,
        out_shape=jax.ShapeDtypeStruct((N_DEV,) + SMALL_SHAPE, small.dtype),
        scratch_shapes=[pltpu.SemaphoreType.DMA((N_DEV - 1,)), pltpu.SemaphoreType.DMA((N_DEV - 1,)), pltpu.SemaphoreType.DMA],
        name="small_allgather",
    )(small)


def small_sum(blocks):
    def body(a_ref, o_ref):
        acc = a_ref[0]
        for d in range(1, N_DEV):
            acc = acc + a_ref[d]
        o_ref[...] = acc

    return pl.pallas_call(
        body,
        in_specs=[pl.BlockSpec(memory_space=pltpu.VMEM)],
        out_specs=pl.BlockSpec(memory_space=pltpu.VMEM),
        out_shape=jax.ShapeDtypeStruct(SMALL_SHAPE, F32),
        name="small_sum",
    )(blocks)


def pack_small(grads, loss):
    flat = jnp.concatenate([grads[n].reshape(-1) for n in SMALL] + [loss.reshape(-1)])
    size = SMALL_SHAPE[0] * SMALL_SHAPE[1]
    return jnp.pad(flat, (0, size - flat.shape[0])).reshape(SMALL_SHAPE)


def unpack_small(packed):
    flat = packed.reshape(-1)
    out, off = {}, 0
    for n, size in SMALL.items():
        out[n] = flat[off:off + DEPTH * size].reshape(DEPTH, size)
        off += DEPTH * size
    return out, flat[off]


def kernel(x, norm_mix_g, w_in, forget_b, q_norm_g, k_norm_g, w_attn_out, conv_w, w_conv_out, pool_w, pool_scale, w_o, norm_ffn_g, w_ffn_in, w_ffn_out, loss_target, m_norm_mix_g, m_w_in, m_forget_b, m_q_norm_g, m_k_norm_g, m_w_attn_out, m_conv_w, m_w_conv_out, m_pool_w, m_pool_scale, m_w_o, m_norm_ffn_g, m_w_ffn_in, m_w_ffn_out, v_norm_mix_g, v_w_in, v_forget_b, v_q_norm_g, v_k_norm_g, v_w_attn_out, v_conv_w, v_w_conv_out, v_pool_w, v_pool_scale, v_w_o, v_norm_ffn_g, v_w_ffn_in, v_w_ffn_out):
    given = dict(locals())
    weights = {n: given[n] for n in WEIGHTS}

    core = lax.axis_index("c").astype(jnp.int32)
    chip = (2 * lax.axis_index("x") + lax.axis_index("y")).astype(jnp.int32)

    core, chip = core.reshape(1), chip.reshape(1)
    mix_names, all_names = MIX_W + ["conv_w"], MIX_W + FFN_W + ["conv_w"]

    def placed(lo, hi, names, after=None):
        piece = None if after is None else _as4(next(iter(after.values())))[0, 0, 0:HALO, 0:LANES]
        return [place_shard(chip, weights[n], lo, hi, F32 if n == "conv_w" else BF16, "place_" + n,
                            piece if i == 0 else None) for i, n in enumerate(names)]

    def as_weights(bufs, names):
        return {n: b.reshape(b.shape[:2] + weights[n].shape[1:]) for n, b in zip(names, bufs)}

    def landed(start, names, after):
        return as_weights(gather_forward(gather_wait(*start[:3], after)), names)

    def layer_small(l, *tokens):
        small = {n: weights[n][l] for n in SMALL if n != "conv_w"}
        for t in tokens:
            small["norm_mix_g"] = small["norm_mix_g"] + t[0, 0]
        return small

    done = [None] * DEPTH
    first = as_weights(gather_shards(placed(0, 1, mix_names)), mix_names)
    ffn0 = gather_start(placed(0, 1, FFN_W, after=first))
    layer1 = gather_start(placed(1, 2, all_names, after=first))
    xm, wm, sm = mix_fwd(x[0], first, 0, layer_small(0, ffn0[3], layer1[3]))
    xs, wf, sf = ffn_half_fwd(xm, landed(ffn0, FFN_W, xm), 0, weights["norm_ffn_g"][0])
    done[0] = (wm, sm, wf, sf)
    full = landed(layer1, all_names, xs)
    rest = gather_start(placed(2, DEPTH, all_names, after=full))
    xm, wm, sm = mix_fwd(xs, full, 0, layer_small(1, rest[3]))
    xs, wf, sf = ffn_half_fwd(xm, full, 0, weights["norm_ffn_g"][1])
    done[1] = (wm, sm, wf, sf)
    full = landed(rest, all_names, xs)
    for l in range(2, DEPTH):
        xm, wm, sm = mix_fwd(xs, full, l - 2, layer_small(l))
        xs, wf, sf = ffn_half_fwd(xm, full, l - 2, weights["norm_ffn_g"][l])
        done[l] = (wm, sm, wf, sf)
    loss, dx = loss_head(xs, loss_target[0])

    small_grads = [None] * DEPTH
    totals = {n: lax.empty((DEPTH,) + _as4(weights[n][None]).shape[2:], F32) for n in SPLIT}

    def buffers(names, n_layers):
        return {n: lax.empty((n_layers, N_CHIPS) + weights[n].shape[1:], F32) for n in names}

    def pair_sums(names, grads, theirs):
        return [pair_sum(core, a, t, "pair_sum_" + n) for n, a, t in zip(names, grads, theirs)]

    def add_chips(names, recv, sums, lo):
        for n, r, s in zip(names, recv, sums):
            totals[n] = chip_sum_into(core, chip, r, s, totals[n], lo, "chip_sum_" + n)

    big = buffers(SPLIT, DEPTH - 1)
    for l in reversed(range(1, DEPTH)):
        wm, sm, wf, sf = done[l]
        dxm, ffn_big, g_ffn = ffn_half_bwd(dx, wf, sf, l - 1, {n: big[n] for n in FFN_W})
        dx, mix_big, g_mix = mix_bwd(dxm, wm, sm, l - 1, {n: big[n] for n in MIX_W})
        big = {**ffn_big, **mix_big}
        small_grads[l] = {**g_ffn, **g_mix}
    pair = pair_exchange_start([_as4(big[n]) for n in SPLIT])

    wm, sm, wf, sf = done[0]
    wf = dict(wf, g2=wf["g2"] + pair[4][0:1, 0:1])
    dxm, ffn_big, g_ffn = ffn_half_bwd(dx, wf, sf, 0, buffers(FFN_W, 1))
    chips = chip_exchange_start(pair_sums(SPLIT, *pair_exchange_wait(*pair[:4], dxm)))
    ffn_grads = [_as4(ffn_big[n]) for n in FFN_W]
    ffn_chips = chip_exchange_start(pair_sums(FFN_W, ffn_grads, pair_exchange(ffn_grads)))
    wm = dict(wm, pscale=wm["pscale"] + chips[4][0:1, 0:1] + ffn_chips[4][0:1, 0:1])
    dproj, mix_big, g_mix = mix_bwd_weights(dxm, wm, sm, 0, buffers(MIX_W, 1))
    mix_grads = [_as4(mix_big[n]) for n in MIX_W]
    mix_chips = chip_exchange_start(pair_sums(MIX_W, mix_grads, pair_exchange(mix_grads)))
    dx, g_in = mix_bwd_input(dproj, dxm, dict(wm, g1=wm["g1"] + mix_chips[4][0:1, 0:1]), sm)
    small_grads[0] = {**g_ffn, **g_mix, **g_in}

    reduced, deltas, new_m, new_v = {}, {}, {}, {}

    def update(names):
        for n in names:
            w = weights[n]
            flat = (-1, w.shape[-1])
            d, nm, nv = adamw(w.reshape(flat), reduced[n].reshape(flat), given["m_" + n].reshape(flat),
                              given["v_" + n].reshape(flat), "adamw_" + n)
            deltas[n], new_m[n], new_v[n] = d.reshape(w.shape), nm.reshape(w.shape), nv.reshape(w.shape)

    def share(names):
        for n, t in zip(names, sibling_share([totals[n] for n in names])):
            reduced[n] = t.reshape(weights[n].shape)

    sums, recv = chip_exchange_wait(*chips[:4], dx)
    add_chips(SPLIT, recv, sums, 1)
    sums, recv = chip_exchange_wait(*ffn_chips[:4], dx)
    add_chips(FFN_W, recv, sums, 0)
    share(FFN_W)
    update(FFN_W)
    sums, recv = chip_exchange_wait(*mix_chips[:4], deltas["w_ffn_in"])
    add_chips(MIX_W, recv, sums, 0)
    share(MIX_W)

    small_grads = {n: jnp.stack([g[n] for g in small_grads]) for n in SMALL}
    small_total, loss_sum = unpack_small(small_sum(small_allgather(pack_small(small_grads, loss))))
    chip = chip[0]
    cols = D_CONV // N_CHIPS
    small_total["conv_w"] = lax.dynamic_slice_in_dim(small_total["conv_w"].reshape(DEPTH, CONV_K, D_CONV), chip * cols, cols, axis=2)
    for n in SMALL:
        reduced[n] = small_total[n].reshape(weights[n].shape)
    update(MIX_W + list(SMALL))

    return (loss_sum, dx[None], *[reduced[n] for n in WEIGHTS], *[deltas[n] for n in WEIGHTS],
            *[new_m[n] for n in WEIGHTS], *[new_v[n] for n in WEIGHTS])
```

```python
import functools

import numpy as np
import jax
import jax.numpy as jnp
from jax import lax
from jax.experimental import pallas as pl
from jax.experimental.pallas import tpu as pltpu

F32 = jnp.float32
BF16 = jnp.bfloat16

D_MODEL = 1024
DEPTH = 4
HEAD_DIM = 64
N_HEADS = 8
D_ATTN = 512
D_CONV = 256
D_POOL = 256
D_FF = 2816
D_IN = 5640
CONV_K = 3
POOL_WINDOWS = (2, 4, 8, 16)
N_GROUPS = len(POOL_WINDOWS)
EPS = 1e-6
ADAM_LR, ADAM_B1, ADAM_B2, ADAM_EPS, ADAM_WD, ADAM_STEP = 0.001, 0.9, 0.999, 1e-08, 0.01, 10

D_QKV = 3 * D_ATTN
D_F = 128
D_B = 3 * D_CONV + D_POOL + 3 * D_MODEL
D_LOCAL = 3 * D_CONV + D_POOL

LANES = 128
D_HEADS = N_HEADS * LANES
HALO = 16
VMEM_LIMIT = 56 * 1024 * 1024
NEG = -1e30
LOG2E = 1.4426950408889634
LN2 = 0.6931471805599453

TM = 256
TM_MIX = 256
TQ = 1024

LANE_C = 64
LANE_ONE = 67
LANE_LSE = 70
N_PIECES = 3


def _dot(a, b):
    return jnp.dot(a, b, preferred_element_type=F32)


def _dot_nt(a, b):
    return lax.dot_general(a, b, (((1,), (1,)), ((), ())), preferred_element_type=F32)


def _dot_tn(a, b):
    return lax.dot_general(a, b, (((0,), (0,)), ((), ())), preferred_element_type=F32)


def _params(*sem):
    return pltpu.CompilerParams(dimension_semantics=sem, vmem_limit_bytes=VMEM_LIMIT)


def _rows(tm, n):
    return pl.BlockSpec((tm, n), lambda i, *_: (i, 0))


def _whole(a):
    nd = a.ndim
    return pl.BlockSpec(a.shape, lambda *_: (0,) * nd)


def _layer(shape):
    nd = len(shape)
    return pl.BlockSpec((None,) + tuple(shape), lambda *a: (a[-1][0],) + (0,) * nd)


def _layer_index(l):
    return jnp.full((1,), l, jnp.int32)


def _split_bf16(x):
    hi = x.astype(BF16)
    lo = (x - hi.astype(F32)).astype(BF16)
    return hi, lo


def _pieces(x):
    p1 = x.astype(BF16)
    r1 = x - p1.astype(F32)
    p2 = r1.astype(BF16)
    p3 = (r1 - p2.astype(F32)).astype(BF16)
    return p1, p2, p3


def _sigmoid(x):
    return 1.0 / (1.0 + jnp.exp(-x))


def w_in_prep(win, l):
    tr = 256
    n = D_IN // 4
    v_rest = D_QKV - n
    b0 = v_rest + N_HEADS

    def body(s0, s1, s2, s3, wa_ref, wf_ref, wb_ref):
        b = s1[...]
        wa_ref[...] = jnp.concatenate([s0[...], b[:, 0:v_rest]], axis=1)
        wf_ref[...] = jnp.concatenate([b[:, v_rest:b0], jnp.zeros((tr, D_F - N_HEADS), b.dtype)], axis=1)
        wb_ref[...] = jnp.concatenate([b[:, b0:n], s2[...], s3[...]], axis=1)

    shard = lambda j: pl.BlockSpec((None, None, tr, n), lambda i: (l, j, i, 0))
    return pl.pallas_call(
        body,
        grid=(D_MODEL // tr,),
        in_specs=[shard(0), shard(1), shard(2), shard(3)],
        out_specs=[_rows(tr, D_QKV), _rows(tr, D_F), _rows(tr, D_B)],
        out_shape=[jax.ShapeDtypeStruct((D_MODEL, D_QKV), win.dtype), jax.ShapeDtypeStruct((D_MODEL, D_F), win.dtype),
                   jax.ShapeDtypeStruct((D_MODEL, D_B), win.dtype)],
        compiler_params=_params("parallel"),
        name="w_in_prep",
    )(win, win, win, win)


def _into_layer(buf):
    return dict(in_spec=HBM_REF, out_shape=jax.ShapeDtypeStruct(buf.shape, buf.dtype), aliases={1: 0})


def w_in_unprep(dwa, dwf, dwb, buf, l):
    tr = 256
    n = D_IN // 4
    v_rest = D_QKV - n
    b1 = n - v_rest - N_HEADS
    place = _into_layer(buf)

    def body(l_ref, buf_ref, a_ref, f_ref, b_ref, o_ref):
        a = a_ref[...]
        b = b_ref[...]
        o_ref[0] = a[:, 0:n]
        o_ref[1] = jnp.concatenate([a[:, n:D_QKV], f_ref[:, 0:N_HEADS], b[:, 0:b1]], axis=1)
        o_ref[2] = b[:, b1:b1 + n]
        o_ref[3] = b[:, b1 + n:D_B]

    return pl.pallas_call(
        body,
        grid_spec=pltpu.PrefetchScalarGridSpec(
            num_scalar_prefetch=1,
            grid=(D_MODEL // tr,),
            in_specs=[place["in_spec"], _rows(tr, D_QKV), _rows(tr, D_F), _rows(tr, D_B)],
            out_specs=pl.BlockSpec((None, 4, tr, n), lambda i, l: (l[0], 0, i, 0)),
        ),
        out_shape=place["out_shape"],
        input_output_aliases=place["aliases"],
        compiler_params=_params("parallel"),
        name="w_in_unprep",
    )(_layer_index(l), buf, dwa, dwf, dwb)


def branch_w_prep(wao, wco, pw, cw, l):
    gd = D_POOL // N_GROUPS
    od = D_MODEL // N_GROUPS

    def body(wao_ref, wco_ref, pw_ref, cw_ref, ao_ref, co_ref, po_ref, co8_ref):
        ao_ref[...] = jnp.concatenate([wao_ref[j] for j in range(4)], axis=1)
        co_ref[...] = jnp.concatenate([wco_ref[j] for j in range(4)], axis=1)
        zero = jnp.zeros((gd, od), co_ref.dtype)
        po_ref[...] = jnp.concatenate(
            [jnp.concatenate([jnp.concatenate([pw_ref[j, g] for j in range(4)], axis=1) if g2 == g else zero
                              for g2 in range(N_GROUPS)], axis=1) for g in range(N_GROUPS)], axis=0)
        co8_ref[...] = jnp.zeros_like(co8_ref)
        co8_ref[0:CONV_K, :] = jnp.concatenate([cw_ref[j] for j in range(4)], axis=1)

    sel = lambda *shape: pl.BlockSpec((None,) + shape, lambda i: (l,) + (0,) * len(shape))
    return pl.pallas_call(
        body,
        grid=(1,),
        in_specs=[sel(4, D_ATTN, D_MODEL // 4), sel(4, D_CONV, D_MODEL // 4), sel(4, N_GROUPS, gd, od // 4),
                  sel(4, CONV_K, D_CONV // 4)],
        out_specs=[pl.BlockSpec((D_ATTN, D_MODEL), lambda i: (0, 0)), pl.BlockSpec((D_CONV, D_MODEL), lambda i: (0, 0)),
                   pl.BlockSpec((D_POOL, D_MODEL), lambda i: (0, 0)), pl.BlockSpec((8, D_CONV), lambda i: (0, 0))],
        out_shape=[jax.ShapeDtypeStruct((D_ATTN, D_MODEL), BF16), jax.ShapeDtypeStruct((D_CONV, D_MODEL), BF16),
                   jax.ShapeDtypeStruct((D_POOL, D_MODEL), BF16), jax.ShapeDtypeStruct((8, D_CONV), F32)],
        name="branch_w_prep",
    )(wao, wco, pw, cw)


def branch_g_place(dwao, dwco, dwpool, dwo, bufs, l):
    gd = D_POOL // N_GROUPS
    od = D_MODEL // N_GROUPS
    q = D_MODEL // 4

    def body(l_ref, b0, b1, b2, b3, a_ref, c_ref, p_ref, w_ref, ao_ref, co_ref, po_ref, wo_ref):
        a = a_ref[...]
        c = c_ref[...]
        p = p_ref[...]
        for j in range(4):
            ao_ref[j] = a[:, j * q:(j + 1) * q]
            co_ref[j] = c[:, j * q:(j + 1) * q]
            wo_ref[j] = w_ref[j * q:(j + 1) * q, :]
            for g in range(N_GROUPS):
                c0 = g * od + j * (od // 4)
                po_ref[j, g] = p[g * gd:(g + 1) * gd, c0:c0 + od // 4]

    whole = lambda a: pl.BlockSpec(a.shape, lambda i, l: (0,) * a.ndim)
    layer = lambda b: pl.BlockSpec((None,) + b.shape[1:], lambda i, l: (l[0],) + (0,) * (b.ndim - 1))
    return pl.pallas_call(
        body,
        grid_spec=pltpu.PrefetchScalarGridSpec(
            num_scalar_prefetch=1,
            grid=(1,),
            in_specs=[HBM_REF] * 4 + [whole(dwao), whole(dwco), whole(dwpool), whole(dwo)],
            out_specs=[layer(b) for b in bufs],
        ),
        out_shape=[jax.ShapeDtypeStruct(b.shape, b.dtype) for b in bufs],
        input_output_aliases={1: 0, 2: 1, 3: 2, 4: 3},
        compiler_params=_params("arbitrary"),
        name="branch_g_place",
    )(_layer_index(l), *bufs, dwao, dwco, dwpool, dwo)


def in_proj_fwd(x, g, wa, wf, wb):
    S = x.shape[0]

    def body(x_ref, g_ref, wa_ref, wf_ref, wb_ref, h_ref, pa_ref, pf_ref, pb_ref):
        xf = x_ref[...]
        r = lax.rsqrt(jnp.mean(xf * xf, axis=-1, keepdims=True) + EPS)
        h = (xf * r * g_ref[...]).astype(BF16)
        h_ref[...] = h
        pa_ref[...] = _dot(h, wa_ref[...]).astype(BF16)
        pf_ref[...] = _dot(h, wf_ref[...])
        pb_ref[...] = _dot(h, wb_ref[...]).astype(BF16)

    return pl.pallas_call(
        body,
        grid=(S // TM,),
        in_specs=[_rows(TM, D_MODEL), _whole(g), _whole(wa), _whole(wf), _whole(wb)],
        out_specs=[_rows(TM, D_MODEL), _rows(TM, D_QKV), _rows(TM, D_F), _rows(TM, D_B)],
        out_shape=[
            jax.ShapeDtypeStruct((S, D_MODEL), BF16),
            jax.ShapeDtypeStruct((S, D_QKV), BF16),
            jax.ShapeDtypeStruct((S, D_F), F32),
            jax.ShapeDtypeStruct((S, D_B), BF16),
        ],
        compiler_params=_params("parallel"),
        name="in_proj_fwd",
    )(x, g, wa, wf, wb)


def in_proj_bwd(x, g, dxm, dpa, dpf, dpb, wa, wf, wb):
    S = x.shape[0]

    def body(x_ref, g_ref, dxm_ref, dpa_ref, dpf_ref, dpb_ref, wa_ref, wf_ref, wb_ref, dx_ref, dg_ref):
        @pl.when(pl.program_id(0) == 0)
        def _():
            dg_ref[...] = jnp.zeros_like(dg_ref)

        dh = _dot_nt(dpa_ref[...], wa_ref[...]) + _dot_nt(dpf_ref[...], wf_ref[...]) + _dot_nt(dpb_ref[...], wb_ref[...])
        xf = x_ref[...]
        r = lax.rsqrt(jnp.mean(xf * xf, axis=-1, keepdims=True) + EPS)
        xhat = xf * r
        dg_ref[...] += jnp.sum(dh * xhat, axis=0, keepdims=True)
        gdh = dh * g_ref[...]
        dx_ref[...] = dxm_ref[...] + r * (gdh - xhat * jnp.mean(xhat * gdh, axis=-1, keepdims=True))

    return pl.pallas_call(
        body,
        grid=(S // TM,),
        in_specs=[_rows(TM, D_MODEL), _whole(g), _rows(TM, D_MODEL), _rows(TM, D_QKV), _rows(TM, D_F), _rows(TM, D_B),
                  _whole(wa), _whole(wf), _whole(wb)],
        out_specs=[_rows(TM, D_MODEL), pl.BlockSpec((1, D_MODEL), lambda i: (0, 0))],
        out_shape=[jax.ShapeDtypeStruct((S, D_MODEL), F32), jax.ShapeDtypeStruct((1, D_MODEL), F32)],
        compiler_params=_params("arbitrary"),
        name="in_proj_bwd",
    )(x, g, dxm, dpa, dpf, dpb, wa, wf, wb)


TS_WGRAD = 1024


def _wgrad_columns(n):
    return next(tn for tn in (1024, 768, 512, 256, 128) if n % tn == 0)


def wgrad(xa, dy, name):
    S, K = xa.shape
    N = dy.shape[1]
    ts = min(TS_WGRAD, S)
    tn = _wgrad_columns(N)
    tk = _wgrad_columns(K)

    def body(x_ref, dy_ref, o_ref):
        @pl.when(pl.program_id(2) == 0)
        def _():
            o_ref[...] = jnp.zeros_like(o_ref)

        o_ref[...] += _dot_tn(x_ref[...], dy_ref[...])

    return pl.pallas_call(
        body,
        grid=(K // tk, N // tn, S // ts),
        in_specs=[pl.BlockSpec((ts, tk), lambda i, j, k: (k, i)), pl.BlockSpec((ts, tn), lambda i, j, k: (k, j))],
        out_specs=pl.BlockSpec((tk, tn), lambda i, j, k: (i, j)),
        out_shape=jax.ShapeDtypeStruct((K, N), F32),
        compiler_params=_params("parallel", "parallel", "arbitrary"),
        name=name,
    )(xa, dy)


def wgrad_into(xa, dy, name, buf, l):
    S, K = xa.shape
    N = dy.shape[1]
    ts = min(TS_WGRAD, S)
    split = buf.ndim == 4
    tn = buf.shape[-1] if split else _wgrad_columns(N)
    place = _into_layer(buf)

    def body(l_ref, buf_ref, x_ref, dy_ref, o_ref):
        @pl.when(pl.program_id(1) == 0)
        def _():
            o_ref[...] = jnp.zeros_like(o_ref)

        o_ref[...] += _dot_tn(x_ref[...], dy_ref[...])

    if split:
        out_spec = pl.BlockSpec((None, None, K, tn), lambda j, k, l: (l[0], j, 0, 0))
    else:
        out_spec = pl.BlockSpec((None, K, tn), lambda j, k, l: (l[0], 0, j))
    return pl.pallas_call(
        body,
        grid_spec=pltpu.PrefetchScalarGridSpec(
            num_scalar_prefetch=1,
            grid=(N // tn, S // ts),
            in_specs=[place["in_spec"], pl.BlockSpec((ts, K), lambda j, k, l: (k, 0)),
                      pl.BlockSpec((ts, tn), lambda j, k, l: (k, j))],
            out_specs=out_spec,
        ),
        out_shape=place["out_shape"],
        input_output_aliases=place["aliases"],
        compiler_params=_params("parallel", "arbitrary"),
        name=name,
    )(_layer_index(l), buf, xa, dy)


def _head_mean_matrix():
    h = np.arange(D_ATTN) // HEAD_DIM
    return jnp.asarray((h[:, None] == h[None, :]).astype(np.float32) / HEAD_DIM, BF16)


def _place_matrix(lane0):
    m = np.zeros((N_PIECES * LANES, D_HEADS), np.float32)
    for i in range(N_PIECES):
        for h in range(N_HEADS):
            m[i * LANES + h, h * LANES + lane0 + i] = 1.0
    return jnp.asarray(m, BF16)


def _tri(n, upper):
    r = np.arange(n)
    m = (r[None, :] >= r[:, None]) if upper else (r[None, :] <= r[:, None])
    return jnp.asarray(m.astype(np.float32), BF16)


def _lanes_in(lane, lo, n):
    return (lane >= lo) & (lane < lo + n)


def qk_prep(pa, pf, gq, gk, fb):
    S = pa.shape[0]
    bd = _head_mean_matrix()
    tri = _tri(TM, upper=False)
    place_q = _place_matrix(LANE_C)
    place_k = _place_matrix(LANE_ONE)

    def body(q_ref, k_ref, v_ref, pf_ref, gq_ref, gk_ref, fb_ref, bd_ref, tri_ref, pq_ref, pk_ref,
             qx_ref, kx_ref, vx_ref, carry):
        @pl.when(pl.program_id(0) == 0)
        def _():
            carry[...] = jnp.zeros_like(carry)

        def head_norm(x_ref, g_ref, scale):
            xf = x_ref[...].astype(F32)
            ms = _dot((xf * xf).astype(BF16), bd_ref[...])
            return xf * lax.rsqrt(ms + EPS) * g_ref[...] * scale

        qh = head_norm(q_ref, gq_ref, HEAD_DIM ** -0.5 * LOG2E)
        kh = head_norm(k_ref, gk_ref, 1.0)
        vf = v_ref[...].astype(F32)

        z = pf_ref[...] + fb_ref[...]
        logf = jnp.minimum(z, 0.0) - jnp.log(1.0 + jnp.exp(-jnp.abs(z)))
        hi, lo = _split_bf16(logf)
        c = _dot(tri_ref[...], hi) + _dot(tri_ref[...], lo) + carry[...]
        carry[...] += jnp.sum(hi.astype(F32) + lo.astype(F32), axis=0, keepdims=True)
        pieces = jnp.concatenate(_pieces(c * LOG2E), axis=1)
        cq = _dot(pieces, pq_ref[...])
        ck = _dot(pieces, pk_ref[...])

        lane = lax.broadcasted_iota(jnp.int32, (TM, LANES), 1)
        low = lane < HEAD_DIM
        ones_q = _lanes_in(lane, LANE_ONE, N_PIECES).astype(F32)
        ones_k = (_lanes_in(lane, LANE_C, N_PIECES) | _lanes_in(lane, LANE_LSE, N_PIECES)).astype(F32)
        ones_v = _lanes_in(lane, LANE_C, N_PIECES + 1).astype(F32)
        for h in range(N_HEADS):
            blk = slice((h // 2) * LANES, (h // 2 + 1) * LANES)
            head = (lambda a: pltpu.roll(a[:, blk], HEAD_DIM, 1)) if h % 2 else (lambda a: a[:, blk])
            mine = slice(h * LANES, (h + 1) * LANES)
            qx_ref[h] = jnp.where(low, head(qh), cq[:, mine] + ones_q).astype(BF16)
            kx_ref[h] = jnp.where(low, head(kh), ones_k - ck[:, mine]).astype(BF16)
            vx_ref[h] = jnp.where(low, head(vf), ones_v).astype(BF16)

    heads = pl.BlockSpec((N_HEADS, TM, LANES), lambda i: (0, i, 0))
    out = jax.ShapeDtypeStruct((N_HEADS, S, LANES), BF16)
    return pl.pallas_call(
        body,
        grid=(S // TM,),
        in_specs=[pl.BlockSpec((TM, D_ATTN), lambda i: (i, 0)), pl.BlockSpec((TM, D_ATTN), lambda i: (i, 1)),
                  pl.BlockSpec((TM, D_ATTN), lambda i: (i, 2)),
                  _rows(TM, D_F), _whole(gq), _whole(gk), _whole(fb), _whole(bd), _whole(tri), _whole(place_q), _whole(place_k)],
        out_specs=[heads, heads, heads],
        out_shape=[out, out, out],
        scratch_shapes=[pltpu.VMEM((1, D_F), F32)],
        compiler_params=_params("arbitrary"),
        name="qk_prep",
    )(pa, pa, pa, pf, gq, gk, fb, bd, tri, place_q, place_k)


def attn_fwd(qx, kx, vx):
    S = qx.shape[1]
    nq = S // TQ

    def body(q_ref, k_ref, v_ref, o_ref, qb_ref):
        i = pl.program_id(1)
        lane = lax.broadcasted_iota(jnp.int32, (TQ, LANES), 1)
        q = [q_ref[0], q_ref[1]]

        def update(qh, m, acc, k, v, visible):
            z = _dot_nt(qh, k)
            if visible is not None:
                z = jnp.where(visible, z, NEG)
            m_new = jnp.maximum(m, jnp.max(z, axis=1, keepdims=True))
            pr = jnp.exp2(z - m_new)
            return m_new, jnp.exp2(m - m_new) * acc + _dot(pr.astype(BF16), v)

        def step(kt, carry, diagonal=False):
            ks = pl.multiple_of(kt * TQ, TQ)
            visible = None
            if diagonal:
                visible = lax.broadcasted_iota(jnp.int32, (TQ, TQ), 0) >= lax.broadcasted_iota(jnp.int32, (TQ, TQ), 1)
            return tuple(update(q[j], *carry[j], k_ref[j, pl.ds(ks, TQ), :], v_ref[j, pl.ds(ks, TQ), :], visible)
                         for j in range(2))

        init = tuple((jnp.full((TQ, 1), NEG, F32), jnp.zeros((TQ, LANES), F32)) for _ in range(2))
        carry = step(i, lax.fori_loop(0, i, step, init), diagonal=True)
        for j in range(2):
            m, acc = carry[j]
            l = jnp.sum(jnp.where(lane == LANE_C, acc, 0.0), axis=1, keepdims=True)
            o_ref[j] = acc / l
            n1, n2, n3 = _pieces(-(m + jnp.log(l) * LOG2E))
            qb_ref[j] = jnp.where(lane == LANE_LSE, n1,
                                  jnp.where(lane == LANE_LSE + 1, n2, jnp.where(lane == LANE_LSE + 2, n3, q[j])))

    pair_tile = pl.BlockSpec((2, TQ, LANES), lambda p, i: (p, i, 0))
    pair_all = pl.BlockSpec((2, S, LANES), lambda p, i: (p, 0, 0))
    return pl.pallas_call(
        body,
        grid=(N_HEADS // 2, nq),
        in_specs=[pair_tile, pair_all, pair_all],
        out_specs=[pair_tile, pair_tile],
        out_shape=[jax.ShapeDtypeStruct((N_HEADS, S, LANES), F32), jax.ShapeDtypeStruct((N_HEADS, S, LANES), BF16)],
        compiler_params=_params("parallel", "parallel"),
        name="attn_fwd",
    )(qx, kx, vx)


def attn_bwd(qxb, kx, vx, dox):
    S = qxb.shape[1]
    nq = S // TQ

    def body(q_ref, k_ref, v_ref, do_ref, dq_ref, dk_ref, dv_ref):
        kt = pl.program_id(1)

        @pl.when(kt == 0)
        def _():
            dq_ref[...] = jnp.zeros_like(dq_ref)

        k = [k_ref[0], k_ref[1]]
        v = [v_ref[0], v_ref[1]]

        def step(qi, carry, masked):
            qs = pl.multiple_of(qi * TQ, TQ)
            new = []
            for j in range(2):
                dk, dv = carry[j]
                q = q_ref[j, pl.ds(qs, TQ), :]
                dout = do_ref[j, pl.ds(qs, TQ), :]
                z = _dot_nt(q, k[j])
                if masked:
                    visible = lax.broadcasted_iota(jnp.int32, (TQ, TQ), 0) >= lax.broadcasted_iota(jnp.int32, (TQ, TQ), 1)
                    z = jnp.where(visible, z, NEG)
                pr = jnp.exp2(z)
                dv = dv + _dot_tn(pr.astype(BF16), dout)
                dsb = (pr * _dot_nt(dout, v[j])).astype(BF16)
                dk = dk + _dot_tn(dsb, q)
                dq_ref[j, pl.ds(qs, TQ), :] += _dot(dsb, k[j])
                new.append((dk, dv))
            return tuple(new)

        zero = jnp.zeros((TQ, LANES), F32)
        carry = step(kt, ((zero, zero), (zero, zero)), True)
        carry = lax.fori_loop(kt + 1, nq, functools.partial(step, masked=False), carry)
        for j in range(2):
            dk_ref[j] = carry[j][0]
            dv_ref[j] = carry[j][1]

    pair_tile = pl.BlockSpec((2, TQ, LANES), lambda p, kt: (p, kt, 0))
    pair_all = pl.BlockSpec((2, S, LANES), lambda p, kt: (p, 0, 0))
    out = jax.ShapeDtypeStruct((N_HEADS, S, LANES), F32)
    return pl.pallas_call(
        body,
        grid=(N_HEADS // 2, nq),
        in_specs=[pair_all, pair_tile, pair_tile, pair_all],
        out_specs=[pair_all, pair_tile, pair_tile],
        out_shape=[out, out, out],
        compiler_params=_params("arbitrary", "arbitrary"),
        name="attn_bwd",
    )(qxb, kx, vx, dox)


def attn_bwd_post(pa, pf, dqx, dkx, dvx, gq, gk, fb):
    S = pa.shape[0]
    nt = S // TM
    bd = _head_mean_matrix()
    triu = _tri(TM, upper=True)
    rev = lambda i: nt - 1 - i

    def body(q_ref, k_ref, pf_ref, dqx_ref, dkx_ref, dvx_ref, gq_ref, gk_ref, fb_ref, bd_ref, triu_ref,
             dpa_ref, dpf_ref, dgq_ref, dgk_ref, dfb_ref, carry):
        @pl.when(pl.program_id(0) == 0)
        def _():
            carry[...] = jnp.zeros_like(carry)
            dgq_ref[...] = jnp.zeros_like(dgq_ref)
            dgk_ref[...] = jnp.zeros_like(dgk_ref)
            dfb_ref[...] = jnp.zeros_like(dfb_ref)

        lane = lax.broadcasted_iota(jnp.int32, (TM, LANES), 1)

        def columns(ref):
            return jnp.concatenate([jnp.where(lane < HEAD_DIM, ref[2 * p], pltpu.roll(ref[2 * p + 1], HEAD_DIM, 1))
                                    for p in range(N_HEADS // 2)], axis=1)

        def head_norm_bwd(x_ref, dy, g_ref, dg_ref):
            xf = x_ref[...].astype(F32)
            r = lax.rsqrt(_dot((xf * xf).astype(BF16), bd_ref[...]) + EPS)
            xhat = xf * r
            dg_ref[...] += jnp.sum(dy * xhat, axis=0, keepdims=True)
            gdy = dy * g_ref[...]
            return (r * (gdy - xhat * _dot((xhat * gdy).astype(BF16), bd_ref[...]))).astype(BF16)

        dpa_ref[:, 0:D_ATTN] = head_norm_bwd(q_ref, columns(dqx_ref) * HEAD_DIM ** -0.5, gq_ref, dgq_ref)
        dpa_ref[:, D_ATTN:2 * D_ATTN] = head_norm_bwd(k_ref, columns(dkx_ref) * LN2, gk_ref, dgk_ref)
        dpa_ref[:, 2 * D_ATTN:3 * D_ATTN] = columns(dvx_ref).astype(BF16)

        dc = jnp.zeros((TM, LANES), F32)
        for h in range(N_HEADS):
            both = jnp.where(lane == LANE_C, dqx_ref[h], 0.0) - jnp.where(lane == LANE_ONE, dkx_ref[h], 0.0)
            dc = jnp.where(lane == h, jnp.sum(both, axis=1, keepdims=True), dc)
        hi, lo = _split_bf16(dc)
        dlogf = _dot(triu_ref[...], hi) + _dot(triu_ref[...], lo) + carry[...]
        first = lax.broadcasted_iota(jnp.int32, (TM, D_F), 0) == 0
        carry[...] = jnp.sum(jnp.where(first, dlogf, 0.0), axis=0, keepdims=True)
        df = dlogf * _sigmoid(-(pf_ref[...] + fb_ref[...]))
        dfb_ref[...] += jnp.sum(df, axis=0, keepdims=True)
        dpf_ref[...] = df.astype(BF16)

    heads = pl.BlockSpec((N_HEADS, TM, LANES), lambda i: (0, rev(i), 0))
    return pl.pallas_call(
        body,
        grid=(nt,),
        in_specs=[pl.BlockSpec((TM, D_ATTN), lambda i: (rev(i), 0)), pl.BlockSpec((TM, D_ATTN), lambda i: (rev(i), 1)),
                  pl.BlockSpec((TM, D_F), lambda i: (rev(i), 0)), heads, heads, heads,
                  _whole(gq), _whole(gk), _whole(fb), _whole(bd), _whole(triu)],
        out_specs=[pl.BlockSpec((TM, D_QKV), lambda i: (rev(i), 0)), pl.BlockSpec((TM, D_F), lambda i: (rev(i), 0)),
                   pl.BlockSpec((1, D_ATTN), lambda i: (0, 0)), pl.BlockSpec((1, D_ATTN), lambda i: (0, 0)),
                   pl.BlockSpec((1, D_F), lambda i: (0, 0))],
        out_shape=[jax.ShapeDtypeStruct((S, D_QKV), BF16), jax.ShapeDtypeStruct((S, D_F), BF16),
                   jax.ShapeDtypeStruct((1, D_ATTN), F32), jax.ShapeDtypeStruct((1, D_ATTN), F32),
                   jax.ShapeDtypeStruct((1, D_F), F32)],
        scratch_shapes=[pltpu.VMEM((1, D_F), F32)],
        compiler_params=_params("arbitrary"),
        name="attn_bwd_post",
    )(pa, pa, pf, dqx, dkx, dvx, gq, gk, fb, bd, triu)


def _shift_down(ext, k):
    return pltpu.roll(ext, k, 0)[HALO:]


def _shift_up(ext, k, n):
    return pltpu.roll(ext, n + HALO - k, 0)[:n]


def _pool_lane_select(a2, a4, a8, a16, lane):
    return jnp.where(lane < 64, a2, jnp.where(lane < 128, a4, jnp.where(lane < 192, a8, a16)))


def _local_branches(o, pb, halo, have_prev, row0, wao, convw, wco, wpool, pscale):
    n = pb.shape[0]
    cx = pb[:, 0:D_CONV].astype(F32)
    cb = pb[:, D_CONV:2 * D_CONV].astype(F32)
    cc = pb[:, 2 * D_CONV:3 * D_CONV].astype(F32)
    px = pb[:, 3 * D_CONV:D_LOCAL].astype(F32)
    keep = have_prev.astype(F32)
    z = cc * cx
    z_ext = jnp.concatenate([halo[:, 2 * D_CONV:3 * D_CONV].astype(F32) * halo[:, 0:D_CONV].astype(F32) * keep, z], axis=0)
    z1 = _shift_down(z_ext, 1)
    z2 = _shift_down(z_ext, 2)
    conv = convw[0:1, :] * z2 + convw[1:2, :] * z1 + convw[2:3, :] * z
    cm = cb * conv

    u_ext = jnp.concatenate([halo[:, 3 * D_CONV:D_LOCAL].astype(F32) * keep, px], axis=0)
    s2 = u_ext + pltpu.roll(u_ext, 1, 0)
    s4 = s2 + pltpu.roll(s2, 2, 0)
    s8 = s4 + pltpu.roll(s4, 4, 0)
    s16 = s8 + pltpu.roll(s8, 8, 0)
    lane = lax.broadcasted_iota(jnp.int32, (n, D_POOL), 1)
    win = _pool_lane_select(2.0, 4.0, 8.0, 16.0, lane)
    t = (row0 + lax.broadcasted_iota(jnp.int32, (n, D_POOL), 0)).astype(F32)
    cnt = jnp.minimum(t + 1.0, win)
    feat = _pool_lane_select(s2[HALO:], s4[HALO:], s8[HALO:], s16[HALO:], lane) / cnt - px

    ya = _dot(o, wao)
    yc = _dot(cm.astype(BF16), wco)
    yp_pre = _dot(feat.astype(BF16), wpool)
    yp = yp_pre * pscale
    return dict(cx=cx, cb=cb, cc=cc, z=z, z1=z1, z2=z2, conv=conv, cm=cm, feat=feat, cnt=cnt, lane=lane,
                ya=ya, yc=yc, yp_pre=yp_pre, yp=yp)


def _halo_spec(tm, tile_of):
    per = tm // HALO
    return pl.BlockSpec((HALO, D_LOCAL), lambda i, *_: (jnp.maximum(tile_of(i) * per - 1, 0), 0))


def _heads_as_columns(ref):
    lane = lax.broadcasted_iota(jnp.int32, ref.shape[1:], 1)
    return jnp.concatenate([jnp.where(lane < HEAD_DIM, ref[2 * p], pltpu.roll(ref[2 * p + 1], HEAD_DIM, 1))
                            for p in range(N_HEADS // 2)], axis=1)


def mix_out_fwd(x, ox, pb, wao, convw, wco, wpool, pscale, wo_all, l):
    S = x.shape[0]
    tm = TM_MIX

    def body(l_ref, x_ref, o_ref, pb_ref, halo_ref, wao_ref, cw_ref, wco_ref, wp_ref, ps_ref, wo_ref, y_ref):
        i = pl.program_id(0)
        pb = pb_ref[...]
        o = _heads_as_columns(o_ref).astype(BF16)
        b = _local_branches(o, pb, halo_ref[...], i > 0, i * tm, wao_ref[...], cw_ref[...], wco_ref[...],
                            wp_ref[...], ps_ref[...])
        g0 = _sigmoid(pb[:, D_LOCAL:D_LOCAL + D_MODEL].astype(F32))
        g1 = _sigmoid(pb[:, D_LOCAL + D_MODEL:D_LOCAL + 2 * D_MODEL].astype(F32))
        g2 = _sigmoid(pb[:, D_LOCAL + 2 * D_MODEL:D_B].astype(F32))
        merged = g0 * b["ya"] + g1 * b["yc"] + g2 * b["yp"]
        y_ref[...] = x_ref[...] + _dot(merged.astype(BF16), wo_ref[...])

    return pl.pallas_call(
        body,
        grid_spec=pltpu.PrefetchScalarGridSpec(
            num_scalar_prefetch=1,
            grid=(S // tm,),
            in_specs=[_rows(tm, D_MODEL), pl.BlockSpec((N_HEADS, tm, LANES), lambda i, l: (0, i, 0)), _rows(tm, D_B),
                      _halo_spec(tm, lambda i: i), _whole(wao), _whole(convw), _whole(wco), _whole(wpool), _whole(pscale),
                      _layer((D_MODEL, D_MODEL))],
            out_specs=_rows(tm, D_MODEL),
        ),
        out_shape=jax.ShapeDtypeStruct((S, D_MODEL), F32),
        compiler_params=_params("parallel"),
        name="mix_out_fwd",
    )(_layer_index(l), x, ox, pb, pb, wao, convw, wco, wpool, pscale, wo_all)


def mix_out_bwd(dxm, ox, pb, wao, convw, wco, wpool, pscale, wo_all, l):
    S = dxm.shape[0]
    tm = TM_MIX
    nt = S // tm
    rev = lambda i: nt - 1 - i
    rows = lambda n: pl.BlockSpec((tm, n), lambda i, l: (rev(i), 0))
    heads = pl.BlockSpec((N_HEADS, tm, LANES), lambda i, l: (0, rev(i), 0))
    acc = lambda r, c: pl.BlockSpec((r, c), lambda i, l: (0, 0))

    def body(l_ref, dxm_ref, o_ref, pb_ref, halo_ref, wao_ref, cw_ref, wco_ref, wp_ref, ps_ref, wo_ref,
             dpb_ref, do_ref, dwo_ref, dwao_ref, dwco_ref, dwp_ref, dcw_ref, dps_ref, next_dconv, next_e):
        i = pl.program_id(0)
        r = rev(i)

        @pl.when(i == 0)
        def _():
            for ref in (dwo_ref, dwao_ref, dwco_ref, dwp_ref, dcw_ref, dps_ref, next_dconv, next_e):
                ref[...] = jnp.zeros_like(ref)

        pb = pb_ref[...]
        o = _heads_as_columns(o_ref).astype(BF16)
        cw = cw_ref[...]
        b = _local_branches(o, pb, halo_ref[...], r > 0, r * tm, wao_ref[...], cw, wco_ref[...], wp_ref[...], ps_ref[...])
        g0 = _sigmoid(pb[:, D_LOCAL:D_LOCAL + D_MODEL].astype(F32))
        g1 = _sigmoid(pb[:, D_LOCAL + D_MODEL:D_LOCAL + 2 * D_MODEL].astype(F32))
        g2 = _sigmoid(pb[:, D_LOCAL + 2 * D_MODEL:D_B].astype(F32))
        dxb = dxm_ref[...].astype(BF16)
        merged = g0 * b["ya"] + g1 * b["yc"] + g2 * b["yp"]
        dwo_ref[...] += _dot_tn(merged.astype(BF16), dxb)
        dmer = _dot_nt(dxb, wo_ref[...])
        dpb_ref[:, D_LOCAL:D_LOCAL + D_MODEL] = (dmer * b["ya"] * (g0 * (1.0 - g0))).astype(BF16)
        dpb_ref[:, D_LOCAL + D_MODEL:D_LOCAL + 2 * D_MODEL] = (dmer * b["yc"] * (g1 * (1.0 - g1))).astype(BF16)
        dpb_ref[:, D_LOCAL + 2 * D_MODEL:D_B] = (dmer * b["yp"] * (g2 * (1.0 - g2))).astype(BF16)

        dya = (dmer * g0).astype(BF16)
        dwao_ref[...] += _dot_tn(o, dya)
        da = _dot_nt(dya, wao_ref[...])
        lane = lax.broadcasted_iota(jnp.int32, (tm, LANES), 1)
        for h in range(N_HEADS):
            two = da[:, (h // 2) * LANES:(h // 2 + 1) * LANES]
            dah = jnp.where(lane < HEAD_DIM, pltpu.roll(two, HEAD_DIM, 1) if h % 2 else two, 0.0).astype(BF16)
            d1, d2, d3 = _pieces(-jnp.sum(dah.astype(F32) * o_ref[h], axis=1, keepdims=True))
            do_ref[h] = jnp.where(lane == LANE_C + 1, d1, jnp.where(lane == LANE_C + 2, d2,
                                                                  jnp.where(lane == LANE_C + 3, d3, dah)))

        dyc = (dmer * g1).astype(BF16)
        dwco_ref[...] += _dot_tn(b["cm"].astype(BF16), dyc)
        dcm = _dot_nt(dyc, wco_ref[...])
        dconv = dcm * b["cb"]
        dcw_ref[0:1, :] += jnp.sum(dconv * b["z2"], axis=0, keepdims=True)
        dcw_ref[1:2, :] += jnp.sum(dconv * b["z1"], axis=0, keepdims=True)
        dcw_ref[2:3, :] += jnp.sum(dconv * b["z"], axis=0, keepdims=True)
        d_ext = jnp.concatenate([dconv, next_dconv[...]], axis=0)
        dz = cw[2:3, :] * dconv + cw[1:2, :] * _shift_up(d_ext, 1, tm) + cw[0:1, :] * _shift_up(d_ext, 2, tm)
        next_dconv[...] = dconv[0:HALO]
        dpb_ref[:, 0:D_CONV] = (dz * b["cc"]).astype(BF16)
        dpb_ref[:, D_CONV:2 * D_CONV] = (dcm * b["conv"]).astype(BF16)
        dpb_ref[:, 2 * D_CONV:3 * D_CONV] = (dz * b["cx"]).astype(BF16)

        dyp = dmer * g2
        dps_ref[...] += jnp.sum(dyp * b["yp_pre"], axis=0, keepdims=True)
        dyps = (dyp * ps_ref[...]).astype(BF16)
        dwp_ref[...] += _dot_tn(b["feat"].astype(BF16), dyps)
        dfeat = _dot_nt(dyps, wp_ref[...])
        e = dfeat / b["cnt"]
        e_ext = jnp.concatenate([e, next_e[...]], axis=0)
        up = lambda a, k: pltpu.roll(a, tm + HALO - k, 0)
        f2 = e_ext + up(e_ext, 1)
        f4 = f2 + up(f2, 2)
        f8 = f4 + up(f4, 4)
        f16 = f8 + up(f8, 8)
        next_e[...] = e[0:HALO]
        dpb_ref[:, 3 * D_CONV:D_LOCAL] = (_pool_lane_select(f2[:tm], f4[:tm], f8[:tm], f16[:tm], b["lane"]) - dfeat).astype(BF16)

    return pl.pallas_call(
        body,
        grid_spec=pltpu.PrefetchScalarGridSpec(
            num_scalar_prefetch=1,
            grid=(nt,),
            in_specs=[rows(D_MODEL), heads, rows(D_B), _halo_spec(tm, rev),
                      _whole(wao), _whole(convw), _whole(wco), _whole(wpool), _whole(pscale), _layer((D_MODEL, D_MODEL))],
            out_specs=[rows(D_B), heads, acc(D_MODEL, D_MODEL), acc(D_ATTN, D_MODEL), acc(D_CONV, D_MODEL),
                       acc(D_POOL, D_MODEL), acc(8, D_CONV), acc(1, D_MODEL)],
            scratch_shapes=[pltpu.VMEM((HALO, D_CONV), F32), pltpu.VMEM((HALO, D_POOL), F32)],
        ),
        out_shape=[jax.ShapeDtypeStruct((S, D_B), BF16), jax.ShapeDtypeStruct((N_HEADS, S, LANES), BF16),
                   jax.ShapeDtypeStruct((D_MODEL, D_MODEL), F32), jax.ShapeDtypeStruct((D_ATTN, D_MODEL), F32),
                   jax.ShapeDtypeStruct((D_CONV, D_MODEL), F32), jax.ShapeDtypeStruct((D_POOL, D_MODEL), F32),
                   jax.ShapeDtypeStruct((8, D_CONV), F32), jax.ShapeDtypeStruct((1, D_MODEL), F32)],
        compiler_params=_params("arbitrary"),
        name="mix_out_bwd",
    )(_layer_index(l), dxm, ox, pb, pb, wao, convw, wco, wpool, pscale, wo_all)


FF_SHARD = 2 * D_FF // 4


def ffn_fwd(x, g, w1_all, w2_all, l):
    S = x.shape[0]

    def body(l_ref, x_ref, g_ref, w1_ref, w2_ref, y_ref, u_ref):
        xf = x_ref[...]
        r = lax.rsqrt(jnp.mean(xf * xf, axis=-1, keepdims=True) + EPS)
        h = (xf * r * g_ref[...]).astype(BF16)
        u = jnp.concatenate([_dot(h, w1_ref[j]) for j in range(4)], axis=1)
        u_ref[...] = u.astype(BF16)
        gt = u[:, 0:D_FF]
        act = gt * _sigmoid(gt) * u[:, D_FF:2 * D_FF]
        y_ref[...] = xf + _dot(act.astype(BF16), w2_ref[...])

    return pl.pallas_call(
        body,
        grid_spec=pltpu.PrefetchScalarGridSpec(
            num_scalar_prefetch=1,
            grid=(S // TM,),
            in_specs=[_rows(TM, D_MODEL), _whole(g), _layer((4, D_MODEL, FF_SHARD)), _layer((D_FF, D_MODEL))],
            out_specs=[_rows(TM, D_MODEL), _rows(TM, 2 * D_FF)],
        ),
        out_shape=[jax.ShapeDtypeStruct((S, D_MODEL), F32), jax.ShapeDtypeStruct((S, 2 * D_FF), BF16)],
        compiler_params=_params("parallel"),
        name="ffn_fwd",
    )(_layer_index(l), x, g, w1_all, w2_all)


def ffn_bwd(x, g, dy, u, w1_all, w2_all, l):
    S = x.shape[0]

    def body(l_ref, x_ref, g_ref, dy_ref, u_ref, w1_ref, w2_ref, dx_ref, du_ref, act_ref, h_ref, dyb_ref, dg_ref):
        @pl.when(pl.program_id(0) == 0)
        def _():
            dg_ref[...] = jnp.zeros_like(dg_ref)

        dyf = dy_ref[...]
        dyb_ref[...] = dyf.astype(BF16)
        dact = _dot_nt(dyb_ref[...], w2_ref[...])
        gt = u_ref[:, 0:D_FF].astype(F32)
        up = u_ref[:, D_FF:2 * D_FF].astype(F32)
        sg = _sigmoid(gt)
        silu = gt * sg
        act_ref[...] = (silu * up).astype(BF16)
        du_ref[:, 0:D_FF] = (dact * up * (sg * (1.0 + gt * (1.0 - sg)))).astype(BF16)
        du_ref[:, D_FF:2 * D_FF] = (dact * silu).astype(BF16)
        dh = _dot_nt(du_ref[:, 0:FF_SHARD], w1_ref[0])
        for j in range(1, 4):
            dh = dh + _dot_nt(du_ref[:, j * FF_SHARD:(j + 1) * FF_SHARD], w1_ref[j])
        xf = x_ref[...]
        r = lax.rsqrt(jnp.mean(xf * xf, axis=-1, keepdims=True) + EPS)
        xhat = xf * r
        h_ref[...] = (xhat * g_ref[...]).astype(BF16)
        dg_ref[...] += jnp.sum(dh * xhat, axis=0, keepdims=True)
        gdh = dh * g_ref[...]
        dx_ref[...] = dyf + r * (gdh - xhat * jnp.mean(xhat * gdh, axis=-1, keepdims=True))

    return pl.pallas_call(
        body,
        grid_spec=pltpu.PrefetchScalarGridSpec(
            num_scalar_prefetch=1,
            grid=(S // TM,),
            in_specs=[_rows(TM, D_MODEL), _whole(g), _rows(TM, D_MODEL), _rows(TM, 2 * D_FF),
                      _layer((4, D_MODEL, FF_SHARD)), _layer((D_FF, D_MODEL))],
            out_specs=[_rows(TM, D_MODEL), _rows(TM, 2 * D_FF), _rows(TM, D_FF), _rows(TM, D_MODEL), _rows(TM, D_MODEL),
                       pl.BlockSpec((1, D_MODEL), lambda i, l: (0, 0))],
        ),
        out_shape=[jax.ShapeDtypeStruct((S, D_MODEL), F32), jax.ShapeDtypeStruct((S, 2 * D_FF), BF16),
                   jax.ShapeDtypeStruct((S, D_FF), BF16), jax.ShapeDtypeStruct((S, D_MODEL), BF16),
                   jax.ShapeDtypeStruct((S, D_MODEL), BF16), jax.ShapeDtypeStruct((1, D_MODEL), F32)],
        compiler_params=_params("arbitrary"),
        name="ffn_bwd",
    )(_layer_index(l), x, g, dy, u, w1_all, w2_all)


def loss_head(y, target):
    S = y.shape[0]

    def body(y_ref, t_ref, loss_ref, dy_ref):
        @pl.when(pl.program_id(0) == 0)
        def _():
            loss_ref[0, 0] = 0.0

        err = y_ref[...] - t_ref[...]
        dy_ref[...] = err * (1.0 / D_MODEL)
        loss_ref[0, 0] += 0.5 * jnp.sum(jnp.mean(err * err, axis=-1))

    return pl.pallas_call(
        body,
        grid=(S // TM,),
        in_specs=[_rows(TM, D_MODEL), _rows(TM, D_MODEL)],
        out_specs=[pl.BlockSpec((1, 1), lambda i: (0, 0), memory_space=pltpu.SMEM), _rows(TM, D_MODEL)],
        out_shape=[jax.ShapeDtypeStruct((1, 1), F32), jax.ShapeDtypeStruct((S, D_MODEL), F32)],
        compiler_params=_params("arbitrary"),
        name="loss_head",
    )(y, target)


SPLIT = {
    "w_in": ((D_MODEL, D_IN), 1),
    "w_attn_out": ((D_ATTN, D_MODEL), 1),
    "w_conv_out": ((D_CONV, D_MODEL), 1),
    "pool_w": ((N_GROUPS, D_POOL // N_GROUPS, D_MODEL // N_GROUPS), 2),
    "w_o": ((D_MODEL, D_MODEL), 0),
    "w_ffn_in": ((D_MODEL, 2 * D_FF), 1),
    "w_ffn_out": ((D_FF, D_MODEL), 0),
}
SMALL = {"norm_mix_g": D_MODEL, "forget_b": N_HEADS, "q_norm_g": HEAD_DIM, "k_norm_g": HEAD_DIM,
         "pool_scale": D_MODEL, "norm_ffn_g": D_MODEL, "conv_w": CONV_K * D_CONV}
WEIGHTS = ["norm_mix_g", "w_in", "forget_b", "q_norm_g", "k_norm_g", "w_attn_out", "conv_w", "w_conv_out", "pool_w",
           "pool_scale", "w_o", "norm_ffn_g", "w_ffn_in", "w_ffn_out"]


MIX_W = ["w_in", "w_attn_out", "w_conv_out", "pool_w", "w_o"]
FFN_W = ["w_ffn_in", "w_ffn_out"]


def _row(a):
    return a.astype(F32).reshape(1, -1)


def mix_fwd(x, full, li, small):
    n = full["w_o"].shape[0]
    wa, wf, wb = w_in_prep(full["w_in"], li)
    wao, wco, wpool, convw = branch_w_prep(full["w_attn_out"], full["w_conv_out"], full["pool_w"], full["conv_w"], li)
    w = dict(wa=wa, wf=wf, wb=wb, wao=wao, wco=wco, wpool=wpool, convw=convw,
             w_o=full["w_o"].reshape(n, D_MODEL, D_MODEL), at=li,
             g1=_row(small["norm_mix_g"]), pscale=_row(small["pool_scale"]),
             gq=_row(jnp.tile(small["q_norm_g"], N_HEADS)), gk=_row(jnp.tile(small["k_norm_g"], N_HEADS)),
             fb=_row(jnp.pad(small["forget_b"], (0, D_F - N_HEADS))))
    h, pa, pf, pb = in_proj_fwd(x, w["g1"], wa, wf, wb)
    qx, kx, vx = qk_prep(pa, pf, w["gq"], w["gk"], w["fb"])
    ox, qxb = attn_fwd(qx, kx, vx)
    xm = mix_out_fwd(x, ox, pb, wao, convw, wco, wpool, w["pscale"], w["w_o"], li)
    return xm, w, dict(x=x, h=h, pa=pa, pf=pf, pb=pb, kx=kx, vx=vx, ox=ox, qxb=qxb)


def ffn_half_fwd(xm, full, li, g2):
    n = full["w_ffn_out"].shape[0]
    w = dict(w1=full["w_ffn_in"], w2=full["w_ffn_out"].reshape(n, D_FF, D_MODEL), at=li, g2=_row(g2))
    y, u = ffn_fwd(xm, w["g2"], w["w1"], w["w2"], li)
    return y, w, dict(xm=xm, u=u)


def ffn_half_bwd(dx, w, s, gi, big):
    n = big["w_ffn_out"].shape[0]
    big = dict(big)
    dxm, du, act, h2, dyb, dg2 = ffn_bwd(s["xm"], w["g2"], dx, s["u"], w["w1"], w["w2"], w["at"])
    big["w_ffn_out"] = wgrad_into(act, dyb, "wgrad_ffn_out", big["w_ffn_out"].reshape(n, D_FF, D_MODEL),
                                  gi).reshape(big["w_ffn_out"].shape)
    big["w_ffn_in"] = wgrad_into(h2, du, "wgrad_ffn_in", big["w_ffn_in"], gi)
    return dxm, big, dict(norm_ffn_g=dg2[0])


def mix_bwd_weights(dxm, w, s, gi, big):
    big = dict(big)
    dpb, dox, dwo, dwao, dwco, dwpool, dconvw, dpscale = mix_out_bwd(
        dxm, s["ox"], s["pb"], w["wao"], w["convw"], w["wco"], w["wpool"], w["pscale"], w["w_o"], w["at"])
    dqx, dkx, dvx = attn_bwd(s["qxb"], s["kx"], s["vx"], dox)
    dpa, dpf, dgq, dgk, dfb = attn_bwd_post(s["pa"], s["pf"], dqx, dkx, dvx, w["gq"], w["gk"], w["fb"])
    big["w_in"] = w_in_unprep(wgrad(s["h"], dpa, "wgrad_in_qkv"), wgrad(s["h"], dpf, "wgrad_in_f"),
                              wgrad(s["h"], dpb, "wgrad_in_b"), big["w_in"], gi)
    big["w_attn_out"], big["w_conv_out"], big["pool_w"], big["w_o"] = branch_g_place(
        dwao, dwco, dwpool, dwo, (big["w_attn_out"], big["w_conv_out"], big["pool_w"], big["w_o"]), gi)
    sm = dict(forget_b=dfb[0, 0:N_HEADS], q_norm_g=dgq.reshape(N_HEADS, HEAD_DIM).sum(0),
              k_norm_g=dgk.reshape(N_HEADS, HEAD_DIM).sum(0), pool_scale=dpscale[0],
              conv_w=dconvw[0:CONV_K].reshape(-1))
    return (dpa, dpf, dpb), big, sm


def mix_bwd_input(dproj, dxm, w, s):
    dx, dg1 = in_proj_bwd(s["x"], w["g1"], dxm, *dproj, w["wa"], w["wf"], w["wb"])
    return dx, dict(norm_mix_g=dg1[0])


def mix_bwd(dxm, w, s, gi, big):
    dproj, big, sm = mix_bwd_weights(dxm, w, s, gi, big)
    dx, g1 = mix_bwd_input(dproj, dxm, w, s)
    return dx, big, {**sm, **g1}


def grad_buffers(full):
    return {n: lax.empty(full[n].shape, F32) for n in SPLIT}


def local_step(x, target, gathered, small):
    n_layers = small["norm_mix_g"].shape[0]
    done = []
    for l in range(n_layers):
        xm, wm, sm = mix_fwd(x, gathered, l, {n: v[l] for n, v in small.items()})
        x, wf, sf = ffn_half_fwd(xm, gathered, l, small["norm_ffn_g"][l])
        done.append((wm, sm, wf, sf))
    loss, dx = loss_head(x, target)
    big = grad_buffers(gathered)
    small_grads = [None] * n_layers
    for l in reversed(range(n_layers)):
        wm, sm, wf, sf = done[l]
        dxm, ffn_big, g_ffn = ffn_half_bwd(dx, wf, sf, l, {n: big[n] for n in FFN_W})
        dx, mix_big, g_mix = mix_bwd(dxm, wm, sm, l, {n: big[n] for n in MIX_W})
        big = {**ffn_big, **mix_big}
        small_grads[l] = {**g_ffn, **g_mix}
    return loss, dx, big, {n: jnp.stack([g[n] for g in small_grads]) for n in SMALL}


def adamw(w, g, m, v, name):
    R, C = w.shape
    tm = 256 if R % 256 == 0 else R

    def body(w_ref, g_ref, m_ref, v_ref, d_ref, nm_ref, nv_ref):
        gr = g_ref[...]
        m_new = ADAM_B1 * m_ref[...] + (1.0 - ADAM_B1) * gr
        v_new = ADAM_B2 * v_ref[...] + (1.0 - ADAM_B2) * jnp.square(gr)
        nm_ref[...] = m_new
        nv_ref[...] = v_new
        m_hat = m_new / (1.0 - ADAM_B1 ** ADAM_STEP)
        v_hat = v_new / (1.0 - ADAM_B2 ** ADAM_STEP)
        d_ref[...] = -ADAM_LR * (m_hat / (jnp.sqrt(v_hat) + ADAM_EPS) + ADAM_WD * w_ref[...])

    spec = _rows(tm, C)
    out = jax.ShapeDtypeStruct((R, C), F32)
    return pl.pallas_call(
        body,
        grid=(R // tm,),
        in_specs=[spec] * 4,
        out_specs=[spec] * 3,
        out_shape=[out] * 3,
        compiler_params=_params("parallel"),
        name=name,
    )(w, g, m, v)


MESH = pl.DeviceIdType.MESH
HBM_REF = pl.BlockSpec(memory_space=pl.ANY)
N_CHIPS = 4
N_DEV = 8
SMALL_SHAPE = (128, LANES)


def _mesh_pos():
    return lax.axis_index("x"), lax.axis_index("y"), lax.axis_index("c")


def _other_chips(x, y):
    return [(1 - x, y), (x, 1 - y), (1 - x, 1 - y)]


def _remote(src, dst, send_sem, recv_sem, to):
    return pltpu.make_async_remote_copy(src_ref=src, dst_ref=dst, send_sem=send_sem, recv_sem=recv_sem,
                                        device_id=to, device_id_type=MESH)


def _row_tile(rows):
    for tm in (256, 176, 128):
        if rows % tm == 0:
            return tm
    return rows


def _as4(a):
    return a.reshape(a.shape[0], a.shape[1], -1, a.shape[-1])


def _by_columns(shape):
    return (shape[-2] // 2) % 8 != 0


def _core_rows(buf, c):
    R, C = buf.shape[-2:]
    if R % 2:
        whole = (pl.ds(0, R), pl.ds(0, C))
        return whole, whole, False
    if _by_columns(buf.shape):
        return (pl.ds(0, R), pl.ds(c * (C // 2), C // 2)), (pl.ds(0, R), pl.ds((1 - c) * (C // 2), C // 2)), True
    return (pl.ds(c * (R // 2), R // 2), pl.ds(0, C)), (pl.ds((1 - c) * (R // 2), R // 2), pl.ds(0, C)), True


def _half_shape(g):
    R, C = g.shape[-2:]
    return g.shape[:-2] + ((R, C // 2) if _by_columns(g.shape) else (R // 2, C))


def place_shard(chip, w, lo, hi, dtype, name, after=None):
    w3 = w.reshape(w.shape[0], -1, w.shape[-1])
    _, R, C = w3.shape
    tm = _row_tile(R)
    idle = [] if after is None else [after]

    def body(chip_ref, w_ref, *rest):
        rest[-1][...] = w_ref[...].astype(dtype)

    return pl.pallas_call(
        body,
        grid_spec=pltpu.PrefetchScalarGridSpec(
            num_scalar_prefetch=1,
            grid=(hi - lo, R // tm),
            in_specs=[pl.BlockSpec((None, tm, C), lambda l, i, chip: (lo + l, i, 0))] + [_whole(a) for a in idle],
            out_specs=pl.BlockSpec((None, None, tm, C), lambda l, i, chip: (l, chip[0], i, 0)),
        ),
        out_shape=jax.ShapeDtypeStruct((hi - lo, N_CHIPS, R, C), dtype),
        compiler_params=_params("parallel", "parallel"),
        name=name,
    )(chip, w3, *idle)


HBM_SPACE = pl.BlockSpec(memory_space=pltpu.HBM)
SEM_SPACE = pl.BlockSpec(memory_space=pltpu.SEMAPHORE)
IN_FLIGHT = pltpu.SideEffectType.DATAFLOW_SIDE_EFFECTING


def _sem_table(send_sems, recv_sems):
    return lambda t, j: (send_sems.at[t, j], recv_sems.at[t, j])


def _sem_per_peer(send_sems, recv_sems):
    return lambda t, j: (send_sems[j], recv_sems[j])


def _gather_ici(bufs, sem, sends=True, lands=True):
    x, y, c = _mesh_pos()
    me = 2 * x + y
    out, into = [], []
    for t, buf in enumerate(bufs):
        mine, _, _ = _core_rows(buf, c)
        part = lambda k, buf=buf, mine=mine: buf.at[pl.ds(0, buf.shape[0]), k, *mine]
        for j, (px, py) in enumerate(_other_chips(x, y)):
            if sends:
                out.append(_remote(part(me), part(me), *sem(t, j), (px, py, c)))
            if lands:
                into.append(_remote(part(2 * px + py), part(2 * px + py), *sem(t, j), (px, py, c)))
    return out, into


def _gather_d2d(bufs, send_sems, recv_sems, first):
    x, y, c = _mesh_pos()
    sends, lands = [], []
    for t, buf in enumerate(bufs):
        mine, theirs, split = _core_rows(buf, c)
        if not split:
            continue
        for j, (px, py) in enumerate(_other_chips(x, y)):
            part = lambda half, buf=buf, k=2 * px + py: buf.at[pl.ds(0, buf.shape[0]), k, *half]
            sems = (send_sems.at[t, first + j], recv_sems.at[t, first + j], (x, y, 1 - c))
            sends.append(_remote(part(mine), part(mine), *sems))
            lands.append(_remote(part(theirs), part(theirs), *sems))
    return sends, lands


def gather_shards(bufs):
    n = len(bufs)

    def body(*refs):
        outs = refs[n:2 * n]
        send_sems, recv_sems = refs[2 * n:]
        ici_out, ici_in = _gather_ici(outs, _sem_table(send_sems, recv_sems))
        d2d_out, d2d_in = _gather_d2d(outs, send_sems, recv_sems, 3)
        for cp in ici_out:
            cp.start()
        for cp in ici_in:
            cp.wait_recv()
        for cp in d2d_out:
            cp.start()
        for cp in d2d_in:
            cp.wait_recv()
        for cp in ici_out + d2d_out:
            cp.wait_send()

    return pl.pallas_call(
        body,
        in_specs=[HBM_REF] * n,
        out_specs=[HBM_REF] * n,
        out_shape=[jax.ShapeDtypeStruct(b.shape, b.dtype) for b in bufs],
        input_output_aliases={t: t for t in range(n)},
        scratch_shapes=[pltpu.SemaphoreType.DMA((n, 6)), pltpu.SemaphoreType.DMA((n, 6))],
        name="gather_shards",
    )(*bufs)


def gather_start(bufs):
    n = len(bufs)

    def body(*refs):
        send_sems, recv_sems = refs[n:n + 3], refs[n + 3:n + 6]
        outs = refs[n + 6:2 * n + 6]
        token = refs[2 * n + 6]
        for cp in _gather_ici(outs, _sem_per_peer(send_sems, recv_sems), lands=False)[0]:
            cp.start()
        token[...] = jnp.zeros_like(token)

    res = pl.pallas_call(
        body,
        in_specs=[HBM_SPACE] * n,
        out_specs=[SEM_SPACE] * 6 + [HBM_SPACE] * n + [pl.BlockSpec(memory_space=pltpu.VMEM)],
        out_shape=[pltpu.SemaphoreType.DMA(())] * 6
        + [pltpu.HBM(b.shape, b.dtype) for b in bufs] + [jax.ShapeDtypeStruct((8, LANES), F32)],
        input_output_aliases={t: t + 6 for t in range(n)},
        compiler_params=pltpu.CompilerParams(has_side_effects=IN_FLIGHT),
        name="gather_start",
    )(*[pltpu.with_memory_space_constraint(b, pltpu.HBM) for b in bufs])
    return list(res[0:3]), list(res[3:6]), list(res[6:n + 6]), res[n + 6]


def gather_wait(send_sems, recv_sems, bufs, after):
    n = len(bufs)

    def body(*refs):
        ins_send, ins_recv = refs[n:n + 3], refs[n + 3:n + 6]
        outs = refs[n + 7:]
        sends, lands = _gather_ici(outs, _sem_per_peer(ins_send, ins_recv))
        for cp in lands:
            cp.wait_recv()
        for cp in sends:
            cp.wait_send()

    return list(pl.pallas_call(
        body,
        in_specs=[HBM_SPACE] * n + [SEM_SPACE] * 6 + [HBM_REF],
        out_specs=[HBM_SPACE] * n,
        out_shape=[pltpu.HBM(b.shape, b.dtype) for b in bufs],
        input_output_aliases={t: t for t in range(n)},
        compiler_params=pltpu.CompilerParams(has_side_effects=IN_FLIGHT),
        name="gather_wait",
    )(*bufs, *send_sems, *recv_sems, after))


def gather_forward(bufs):
    n = len(bufs)

    def body(*refs):
        outs = refs[n:2 * n]
        send_sems, recv_sems = refs[2 * n:]
        sends, lands = _gather_d2d(outs, send_sems, recv_sems, 0)
        for cp in sends:
            cp.start()
        for cp in lands:
            cp.wait_recv()
        for cp in sends:
            cp.wait_send()

    return list(pl.pallas_call(
        body,
        in_specs=[HBM_REF] * n,
        out_specs=[HBM_REF] * n,
        out_shape=[jax.ShapeDtypeStruct(b.shape, b.dtype) for b in bufs],
        input_output_aliases={t: t for t in range(n)},
        scratch_shapes=[pltpu.SemaphoreType.DMA((n, 3)), pltpu.SemaphoreType.DMA((n, 3))],
        name="gather_forward",
    )(*bufs))


def pair_exchange(grads):
    n = len(grads)

    def body(*refs):
        ins, outs = refs[:n], refs[n:2 * n]
        send_sems, recv_sems = refs[2 * n:]
        cps = _pair_copies(ins, outs, lambda t: (send_sems.at[t], recv_sems.at[t]))
        for cp in cps:
            cp.start()
        for cp in cps:
            cp.wait()

    return list(pl.pallas_call(
        body,
        in_specs=[HBM_REF] * n,
        out_specs=[HBM_REF] * n,
        out_shape=[jax.ShapeDtypeStruct(_half_shape(g), g.dtype) for g in grads],
        scratch_shapes=[pltpu.SemaphoreType.DMA((n,)), pltpu.SemaphoreType.DMA((n,))],
        name="pair_exchange",
    )(*grads))


def _pair_copies(grads, lands, sem):
    x, y, c = _mesh_pos()
    cps = []
    for t in range(len(grads)):
        _, theirs, _ = _core_rows(grads[t], c)
        src = grads[t].at[pl.ds(0, grads[t].shape[0]), pl.ds(0, N_CHIPS), *theirs]
        cps.append(_remote(src, lands[t], *sem(t), (x, y, 1 - c)))
    return cps


def pair_exchange_start(grads):
    n = len(grads)
    lands = [lax.empty(_half_shape(g), g.dtype) for g in grads]

    def body(*refs):
        send_sem, recv_sem = refs[2 * n:2 * n + 2]
        outs = refs[2 * n + 2:4 * n + 2]
        token = refs[4 * n + 2]
        for cp in _pair_copies(outs[:n], outs[n:], lambda t: (send_sem, recv_sem)):
            cp.start()
        token[...] = jnp.zeros_like(token)

    res = pl.pallas_call(
        body,
        in_specs=[HBM_SPACE] * (2 * n),
        out_specs=[SEM_SPACE] * 2 + [HBM_SPACE] * (2 * n) + [pl.BlockSpec(memory_space=pltpu.VMEM)],
        out_shape=[pltpu.SemaphoreType.DMA(())] * 2 + [pltpu.HBM(a.shape, a.dtype) for a in list(grads) + lands]
        + [jax.ShapeDtypeStruct((8, LANES), F32)],
        input_output_aliases={t: t + 2 for t in range(2 * n)},
        compiler_params=pltpu.CompilerParams(has_side_effects=IN_FLIGHT),
        name="pair_exchange_start",
    )(*[pltpu.with_memory_space_constraint(a, pltpu.HBM) for a in list(grads) + lands])
    return res[0], res[1], list(res[2:n + 2]), list(res[n + 2:2 * n + 2]), res[2 * n + 2]


def pair_exchange_wait(send_sem, recv_sem, grads, lands, after):
    n = len(grads)

    def body(*refs):
        in_send, in_recv = refs[2 * n:2 * n + 2]
        outs = refs[2 * n + 3:]
        for cp in _pair_copies(outs[:n], outs[n:], lambda t: (in_send, in_recv)):
            cp.wait()

    res = pl.pallas_call(
        body,
        in_specs=[HBM_SPACE] * (2 * n) + [SEM_SPACE] * 2 + [HBM_REF],
        out_specs=[HBM_SPACE] * (2 * n),
        out_shape=[pltpu.HBM(a.shape, a.dtype) for a in list(grads) + list(lands)],
        input_output_aliases={t: t for t in range(2 * n)},
        compiler_params=pltpu.CompilerParams(has_side_effects=IN_FLIGHT),
        name="pair_exchange_wait",
    )(*grads, *lands, send_sem, recv_sem, after)
    return list(res[:n]), list(res[n:])


def pair_sum(core, g, t, name):
    n, _, hr, hc = t.shape
    R, C = g.shape[2:]
    tm = _row_tile(hr)
    per = hr // tm
    mine = (lambda a, i, c: (a, i, c[0])) if _by_columns(g.shape) else (lambda a, i, c: (a, per * c[0] + i, 0))

    def body(c_ref, g_ref, t_ref, o_ref):
        o_ref[...] = (g_ref[...] + t_ref[...]).astype(BF16)

    tile = pl.BlockSpec((None, tm, hc), lambda a, i, c: (a, i, 0))
    out = pl.pallas_call(
        body,
        grid_spec=pltpu.PrefetchScalarGridSpec(
            num_scalar_prefetch=1,
            grid=(n * N_CHIPS, per),
            in_specs=[pl.BlockSpec((None, tm, hc), mine), tile],
            out_specs=tile,
        ),
        out_shape=jax.ShapeDtypeStruct((n * N_CHIPS, hr, hc), BF16),
        compiler_params=_params("parallel", "parallel"),
        name=name,
    )(core, g.reshape(n * N_CHIPS, R, C), t.reshape(n * N_CHIPS, hr, hc))
    return out.reshape(t.shape)


def _chip_copies(sums, recv, sem):
    x, y, c = _mesh_pos()
    cps = []
    for t in range(len(sums)):
        for j, (px, py) in enumerate(_other_chips(x, y)):
            src = sums[t].at[pl.ds(0, sums[t].shape[0]), 2 * px + py]
            cps.append(_remote(src, recv[t].at[j], *sem(t, j), (px, py, c)))
    return cps


def _recv_shape(s):
    return (N_CHIPS - 1, s.shape[0]) + s.shape[2:]


def chip_exchange(sums):
    n = len(sums)

    def body(*refs):
        ins, outs = refs[:n], refs[n:2 * n]
        cps = _chip_copies(ins, outs, _sem_table(*refs[2 * n:]))
        for cp in cps:
            cp.start()
        for cp in cps:
            cp.wait()

    return list(pl.pallas_call(
        body,
        in_specs=[HBM_REF] * n,
        out_specs=[HBM_REF] * n,
        out_shape=[jax.ShapeDtypeStruct(_recv_shape(s), s.dtype) for s in sums],
        scratch_shapes=[pltpu.SemaphoreType.DMA((n, 3)), pltpu.SemaphoreType.DMA((n, 3))],
        name="chip_exchange",
    )(*sums))


def chip_exchange_start(sums):
    n = len(sums)
    lands = [lax.empty(_recv_shape(s), s.dtype) for s in sums]

    def body(*refs):
        send_sems, recv_sems = refs[2 * n:2 * n + 3], refs[2 * n + 3:2 * n + 6]
        outs = refs[2 * n + 6:4 * n + 6]
        token = refs[4 * n + 6]
        for cp in _chip_copies(outs[:n], outs[n:2 * n], _sem_per_peer(send_sems, recv_sems)):
            cp.start()
        token[...] = jnp.zeros_like(token)

    res = pl.pallas_call(
        body,
        in_specs=[HBM_SPACE] * (2 * n),
        out_specs=[SEM_SPACE] * 6 + [HBM_SPACE] * (2 * n) + [pl.BlockSpec(memory_space=pltpu.VMEM)],
        out_shape=[pltpu.SemaphoreType.DMA(())] * 6 + [pltpu.HBM(a.shape, a.dtype) for a in list(sums) + lands]
        + [jax.ShapeDtypeStruct((8, LANES), F32)],
        input_output_aliases={t: t + 6 for t in range(2 * n)},
        compiler_params=pltpu.CompilerParams(has_side_effects=IN_FLIGHT),
        name="chip_exchange_start",
    )(*[pltpu.with_memory_space_constraint(a, pltpu.HBM) for a in list(sums) + lands])
    return list(res[0:3]), list(res[3:6]), list(res[6:n + 6]), list(res[n + 6:2 * n + 6]), res[2 * n + 6]


def chip_exchange_wait(send_sems, recv_sems, sums, lands, after):
    n = len(sums)

    def body(*refs):
        ins_send, ins_recv = refs[2 * n:2 * n + 3], refs[2 * n + 3:2 * n + 6]
        outs = refs[2 * n + 7:]
        for cp in _chip_copies(outs[:n], outs[n:], _sem_per_peer(ins_send, ins_recv)):
            cp.wait()

    res = pl.pallas_call(
        body,
        in_specs=[HBM_SPACE] * (2 * n) + [SEM_SPACE] * 6 + [HBM_REF],
        out_specs=[HBM_SPACE] * (2 * n),
        out_shape=[pltpu.HBM(a.shape, a.dtype) for a in list(sums) + list(lands)],
        input_output_aliases={t: t for t in range(2 * n)},
        compiler_params=pltpu.CompilerParams(has_side_effects=IN_FLIGHT),
        name="chip_exchange_wait",
    )(*sums, *lands, *send_sems, *recv_sems, after)
    return list(res[:n]), list(res[n:])


def chip_sum_into(core, chip, recv, sums, total, lo, name):
    _, n, hr, hc = recv.shape
    tm = _row_tile(hr)
    per = hr // tm
    if _by_columns(total.shape):
        mine = lambda a, i, c, k: (lo + a, i, c[0])
    else:
        mine = lambda a, i, c, k: (lo + a, per * c[0] + i, 0)

    def body(c_ref, k_ref, r_ref, s_ref, t_ref, o_ref):
        acc = s_ref[...].astype(F32)
        for j in range(N_CHIPS - 1):
            acc = acc + r_ref[j].astype(F32)
        o_ref[...] = acc

    return pl.pallas_call(
        body,
        grid_spec=pltpu.PrefetchScalarGridSpec(
            num_scalar_prefetch=2,
            grid=(n, per),
            in_specs=[pl.BlockSpec((N_CHIPS - 1, None, tm, hc), lambda a, i, c, k: (0, a, i, 0)),
                      pl.BlockSpec((None, None, tm, hc), lambda a, i, c, k: (a, k[0], i, 0)),
                      HBM_REF],
            out_specs=pl.BlockSpec((None, tm, hc), mine),
        ),
        out_shape=jax.ShapeDtypeStruct(total.shape, F32),
        input_output_aliases={4: 0},
        compiler_params=_params("parallel", "parallel"),
        name=name,
    )(core, chip, recv, sums, total)


def sibling_share(totals):
    n = len(totals)

    def body(*refs):
        outs = refs[n:2 * n]
        send_sems, recv_sems = refs[2 * n:]
        x, y, c = _mesh_pos()
        half = lambda t, which: outs[t].at[pl.ds(0, DEPTH), *_core_rows(outs[t], c)[which]]
        sent = [_remote(half(t, 0), half(t, 0), send_sems.at[t], recv_sems.at[t], (x, y, 1 - c)) for t in range(n)]
        for cp in sent:
            cp.start()
        for t in range(n):
            _remote(half(t, 1), half(t, 1), send_sems.at[t], recv_sems.at[t], (x, y, 1 - c)).wait_recv()
        for cp in sent:
            cp.wait_send()

    return pl.pallas_call(
        body,
        in_specs=[HBM_REF] * n,
        out_specs=[HBM_REF] * n,
        out_shape=[jax.ShapeDtypeStruct(t.shape, t.dtype) for t in totals],
        input_output_aliases={t: t for t in range(n)},
        scratch_shapes=[pltpu.SemaphoreType.DMA((n,)), pltpu.SemaphoreType.DMA((n,))],
        name="sibling_share",
    )(*totals)


def small_allgather(small):
    def body(s_ref, a_ref, send_sems, recv_sems, local_sem):
        x, y, c = _mesh_pos()
        me = 4 * x + 2 * y + c
        own = pltpu.make_async_copy(s_ref, a_ref.at[me], local_sem)
        own.start()
        sent = []
        for k in range(1, N_DEV):
            peer = (x ^ (k >> 2), y ^ ((k >> 1) & 1), c ^ (k & 1))
            cp = _remote(s_ref, a_ref.at[me], send_sems.at[k - 1], recv_sems.at[k - 1], peer)
            cp.start()
            sent.append(cp)
        for k in range(1, N_DEV):
            px, py, pc = x ^ (k >> 2), y ^ ((k >> 1) & 1), c ^ (k & 1)
            _remote(s_ref, a_ref.at[4 * px + 2 * py + pc], send_sems.at[k - 1], recv_sems.at[k - 1], (px, py, pc)).wait_recv()
        for cp in sent:
            cp.wait_send()
        own.wait()

    return pl.pallas_call(
        body,
        in_specs=[HBM_REF],
        out_specs=HBM_REF,
        out_shape=jax.ShapeDtypeStruct((N_DEV,) + SMALL_SHAPE, small.dtype),
        scratch_shapes=[pltpu.SemaphoreType.DMA((N_DEV - 1,)), pltpu.SemaphoreType.DMA((N_DEV - 1,)), pltpu.SemaphoreType.DMA],
        name="small_allgather",
    )(small)


def small_sum(blocks):
    def body(a_ref, o_ref):
        acc = a_ref[0]
        for d in range(1, N_DEV):
            acc = acc + a_ref[d]
        o_ref[...] = acc

    return pl.pallas_call(
        body,
        in_specs=[pl.BlockSpec(memory_space=pltpu.VMEM)],
        out_specs=pl.BlockSpec(memory_space=pltpu.VMEM),
        out_shape=jax.ShapeDtypeStruct(SMALL_SHAPE, F32),
        name="small_sum",
    )(blocks)


def pack_small(grads, loss):
    flat = jnp.concatenate([grads[n].reshape(-1) for n in SMALL] + [loss.reshape(-1)])
    size = SMALL_SHAPE[0] * SMALL_SHAPE[1]
    return jnp.pad(flat, (0, size - flat.shape[0])).reshape(SMALL_SHAPE)


def unpack_small(packed):
    flat = packed.reshape(-1)
    out, off = {}, 0
    for n, size in SMALL.items():
        out[n] = flat[off:off + DEPTH * size].reshape(DEPTH, size)
        off += DEPTH * size
    return out, flat[off]


def kernel(x, norm_mix_g, w_in, forget_b, q_norm_g, k_norm_g, w_attn_out, conv_w, w_conv_out, pool_w, pool_scale, w_o, norm_ffn_g, w_ffn_in, w_ffn_out, loss_target, m_norm_mix_g, m_w_in, m_forget_b, m_q_norm_g, m_k_norm_g, m_w_attn_out, m_conv_w, m_w_conv_out, m_pool_w, m_pool_scale, m_w_o, m_norm_ffn_g, m_w_ffn_in, m_w_ffn_out, v_norm_mix_g, v_w_in, v_forget_b, v_q_norm_g, v_k_norm_g, v_w_attn_out, v_conv_w, v_w_conv_out, v_pool_w, v_pool_scale, v_w_o, v_norm_ffn_g, v_w_ffn_in, v_w_ffn_out):
    given = dict(locals())
    weights = {n: given[n] for n in WEIGHTS}

    core = lax.axis_index("c").astype(jnp.int32)
    chip = (2 * lax.axis_index("x") + lax.axis_index("y")).astype(jnp.int32)

    core, chip = core.reshape(1), chip.reshape(1)
    mix_names, all_names = MIX_W + ["conv_w"], MIX_W + FFN_W + ["conv_w"]

    def placed(lo, hi, names, after=None):
        piece = None if after is None else _as4(next(iter(after.values())))[0, 0, 0:HALO, 0:LANES]
        return [place_shard(chip, weights[n], lo, hi, F32 if n == "conv_w" else BF16, "place_" + n,
                            piece if i == 0 else None) for i, n in enumerate(names)]

    def as_weights(bufs, names):
        return {n: b.reshape(b.shape[:2] + weights[n].shape[1:]) for n, b in zip(names, bufs)}

    def landed(start, names, after):
        return as_weights(gather_forward(gather_wait(*start[:3], after)), names)

    def layer_small(l, *tokens):
        small = {n: weights[n][l] for n in SMALL if n != "conv_w"}
        for t in tokens:
            small["norm_mix_g"] = small["norm_mix_g"] + t[0, 0]
        return small

    done = [None] * DEPTH
    first = as_weights(gather_shards(placed(0, 1, mix_names)), mix_names)
    ffn0 = gather_start(placed(0, 1, FFN_W, after=first))
    layer1 = gather_start(placed(1, 2, all_names, after=first))
    xm, wm, sm = mix_fwd(x[0], first, 0, layer_small(0, ffn0[3], layer1[3]))
    xs, wf, sf = ffn_half_fwd(xm, landed(ffn0, FFN_W, xm), 0, weights["norm_ffn_g"][0])
    done[0] = (wm, sm, wf, sf)
    full = landed(layer1, all_names, xs)
    rest = gather_start(placed(2, DEPTH, all_names, after=full))
    xm, wm, sm = mix_fwd(xs, full, 0, layer_small(1, rest[3]))
    xs, wf, sf = ffn_half_fwd(xm, full, 0, weights["norm_ffn_g"][1])
    done[1] = (wm, sm, wf, sf)
    full = landed(rest, all_names, xs)
    for l in range(2, DEPTH):
        xm, wm, sm = mix_fwd(xs, full, l - 2, layer_small(l))
        xs, wf, sf = ffn_half_fwd(xm, full, l - 2, weights["norm_ffn_g"][l])
        done[l] = (wm, sm, wf, sf)
    loss, dx = loss_head(xs, loss_target[0])

    small_grads = [None] * DEPTH
    totals = {n: lax.empty((DEPTH,) + _as4(weights[n][None]).shape[2:], F32) for n in SPLIT}

    def buffers(names, n_layers):
        return {n: lax.empty((n_layers, N_CHIPS) + weights[n].shape[1:], F32) for n in names}

    def pair_sums(names, grads, theirs):
        return [pair_sum(core, a, t, "pair_sum_" + n) for n, a, t in zip(names, grads, theirs)]

    def add_chips(names, recv, sums, lo):
        for n, r, s in zip(names, recv, sums):
            totals[n] = chip_sum_into(core, chip, r, s, totals[n], lo, "chip_sum_" + n)

    big = buffers(SPLIT, DEPTH - 1)
    for l in reversed(range(1, DEPTH)):
        wm, sm, wf, sf = done[l]
        dxm, ffn_big, g_ffn = ffn_half_bwd(dx, wf, sf, l - 1, {n: big[n] for n in FFN_W})
        dx, mix_big, g_mix = mix_bwd(dxm, wm, sm, l - 1, {n: big[n] for n in MIX_W})
        big = {**ffn_big, **mix_big}
        small_grads[l] = {**g_ffn, **g_mix}
    pair = pair_exchange_start([_as4(big[n]) for n in SPLIT])

    wm, sm, wf, sf = done[0]
    wf = dict(wf, g2=wf["g2"] + pair[4][0:1, 0:1])
    dxm, ffn_big, g_ffn = ffn_half_bwd(dx, wf, sf, 0, buffers(FFN_W, 1))
    chips = chip_exchange_start(pair_sums(SPLIT, *pair_exchange_wait(*pair[:4], dxm)))
    ffn_grads = [_as4(ffn_big[n]) for n in FFN_W]
    ffn_chips = chip_exchange_start(pair_sums(FFN_W, ffn_grads, pair_exchange(ffn_grads)))
    wm = dict(wm, pscale=wm["pscale"] + chips[4][0:1, 0:1] + ffn_chips[4][0:1, 0:1])
    dproj, mix_big, g_mix = mix_bwd_weights(dxm, wm, sm, 0, buffers(MIX_W, 1))
    mix_grads = [_as4(mix_big[n]) for n in MIX_W]
    mix_chips = chip_exchange_start(pair_sums(MIX_W, mix_grads, pair_exchange(mix_grads)))
    dx, g_in = mix_bwd_input(dproj, dxm, dict(wm, g1=wm["g1"] + mix_chips[4][0:1, 0:1]), sm)
    small_grads[0] = {**g_ffn, **g_mix, **g_in}

    reduced, deltas, new_m, new_v = {}, {}, {}, {}

    def update(names):
        for n in names:
            w = weights[n]
            flat = (-1, w.shape[-1])
            d, nm, nv = adamw(w.reshape(flat), reduced[n].reshape(flat), given["m_" + n].reshape(flat),
                              given["v_" + n].reshape(flat), "adamw_" + n)
            deltas[n], new_m[n], new_v[n] = d.reshape(w.shape), nm.reshape(w.shape), nv.reshape(w.shape)

    def share(names):
        for n, t in zip(names, sibling_share([totals[n] for n in names])):
            reduced[n] = t.reshape(weights[n].shape)

    sums, recv = chip_exchange_wait(*chips[:4], dx)
    add_chips(SPLIT, recv, sums, 1)
    sums, recv = chip_exchange_wait(*ffn_chips[:4], dx)
    add_chips(FFN_W, recv, sums, 0)
    share(FFN_W)
    update(FFN_W)
    sums, recv = chip_exchange_wait(*mix_chips[:4], deltas["w_ffn_in"])
    add_chips(MIX_W, recv, sums, 0)
    share(MIX_W)

    small_grads = {n: jnp.stack([g[n] for g in small_grads]) for n in SMALL}
    small_total, loss_sum = unpack_small(small_sum(small_allgather(pack_small(small_grads, loss))))
    chip = chip[0]
    cols = D_CONV // N_CHIPS
    small_total["conv_w"] = lax.dynamic_slice_in_dim(small_total["conv_w"].reshape(DEPTH, CONV_K, D_CONV), chip * cols, cols, axis=2)
    for n in SMALL:
        reduced[n] = small_total[n].reshape(weights[n].shape)
    update(MIX_W + list(SMALL))

    return (loss_sum, dx[None], *[reduced[n] for n in WEIGHTS], *[deltas[n] for n in WEIGHTS],
            *[new_m[n] for n in WEIGHTS], *[new_v[n] for n in WEIGHTS])
```

```python
import functools

import numpy as np
import jax
import jax.numpy as jnp
from jax import lax
from jax.experimental import pallas as pl
from jax.experimental.pallas import tpu as pltpu

F32 = jnp.float32
BF16 = jnp.bfloat16

D_MODEL = 1024
DEPTH = 4
HEAD_DIM = 64
N_HEADS = 8
D_ATTN = 512
D_CONV = 256
D_POOL = 256
D_FF = 2816
D_IN = 5640
CONV_K = 3
POOL_WINDOWS = (2, 4, 8, 16)
N_GROUPS = len(POOL_WINDOWS)
EPS = 1e-6
ADAM_LR, ADAM_B1, ADAM_B2, ADAM_EPS, ADAM_WD, ADAM_STEP = 0.001, 0.9, 0.999, 1e-08, 0.01, 10

D_QKV = 3 * D_ATTN
D_F = 128
D_B = 3 * D_CONV + D_POOL + 3 * D_MODEL
D_LOCAL = 3 * D_CONV + D_POOL

LANES = 128
D_HEADS = N_HEADS * LANES
HALO = 16
VMEM_LIMIT = 56 * 1024 * 1024
NEG = -1e30
LOG2E = 1.4426950408889634
LN2 = 0.6931471805599453

TM = 256
TM_MIX = 256
TQ = 1024

LANE_C = 64
LANE_ONE = 67
LANE_LSE = 70
N_PIECES = 3


def _dot(a, b):
    return jnp.dot(a, b, preferred_element_type=F32)


def _dot_nt(a, b):
    return lax.dot_general(a, b, (((1,), (1,)), ((), ())), preferred_element_type=F32)


def _dot_tn(a, b):
    return lax.dot_general(a, b, (((0,), (0,)), ((), ())), preferred_element_type=F32)


def _params(*sem):
    return pltpu.CompilerParams(dimension_semantics=sem, vmem_limit_bytes=VMEM_LIMIT)


def _rows(tm, n):
    return pl.BlockSpec((tm, n), lambda i, *_: (i, 0))


def _whole(a):
    nd = a.ndim
    return pl.BlockSpec(a.shape, lambda *_: (0,) * nd)


def _layer(shape):
    nd = len(shape)
    return pl.BlockSpec((None,) + tuple(shape), lambda *a: (a[-1][0],) + (0,) * nd)


def _layer_index(l):
    return jnp.full((1,), l, jnp.int32)


def _split_bf16(x):
    hi = x.astype(BF16)
    lo = (x - hi.astype(F32)).astype(BF16)
    return hi, lo


def _pieces(x):
    p1 = x.astype(BF16)
    r1 = x - p1.astype(F32)
    p2 = r1.astype(BF16)
    p3 = (r1 - p2.astype(F32)).astype(BF16)
    return p1, p2, p3


def _sigmoid(x):
    return 1.0 / (1.0 + jnp.exp(-x))


def w_in_prep(win, l):
    tr = 256
    n = D_IN // 4
    v_rest = D_QKV - n
    b0 = v_rest + N_HEADS

    def body(s0, s1, s2, s3, wa_ref, wf_ref, wb_ref):
        b = s1[...]
        wa_ref[...] = jnp.concatenate([s0[...], b[:, 0:v_rest]], axis=1)
        wf_ref[...] = jnp.concatenate([b[:, v_rest:b0], jnp.zeros((tr, D_F - N_HEADS), b.dtype)], axis=1)
        wb_ref[...] = jnp.concatenate([b[:, b0:n], s2[...], s3[...]], axis=1)

    shard = lambda j: pl.BlockSpec((None, None, tr, n), lambda i: (l, j, i, 0))
    return pl.pallas_call(
        body,
        grid=(D_MODEL // tr,),
        in_specs=[shard(0), shard(1), shard(2), shard(3)],
        out_specs=[_rows(tr, D_QKV), _rows(tr, D_F), _rows(tr, D_B)],
        out_shape=[jax.ShapeDtypeStruct((D_MODEL, D_QKV), win.dtype), jax.ShapeDtypeStruct((D_MODEL, D_F), win.dtype),
                   jax.ShapeDtypeStruct((D_MODEL, D_B), win.dtype)],
        compiler_params=_params("parallel"),
        name="w_in_prep",
    )(win, win, win, win)


def _into_layer(buf):
    return dict(in_spec=HBM_REF, out_shape=jax.ShapeDtypeStruct(buf.shape, buf.dtype), aliases={1: 0})


def w_in_unprep(dwa, dwf, dwb, buf, l):
    tr = 256
    n = D_IN // 4
    v_rest = D_QKV - n
    b1 = n - v_rest - N_HEADS
    place = _into_layer(buf)

    def body(l_ref, buf_ref, a_ref, f_ref, b_ref, o_ref):
        a = a_ref[...]
        b = b_ref[...]
        o_ref[0] = a[:, 0:n]
        o_ref[1] = jnp.concatenate([a[:, n:D_QKV], f_ref[:, 0:N_HEADS], b[:, 0:b1]], axis=1)
        o_ref[2] = b[:, b1:b1 + n]
        o_ref[3] = b[:, b1 + n:D_B]

    return pl.pallas_call(
        body,
        grid_spec=pltpu.PrefetchScalarGridSpec(
            num_scalar_prefetch=1,
            grid=(D_MODEL // tr,),
            in_specs=[place["in_spec"], _rows(tr, D_QKV), _rows(tr, D_F), _rows(tr, D_B)],
            out_specs=pl.BlockSpec((None, 4, tr, n), lambda i, l: (l[0], 0, i, 0)),
        ),
        out_shape=place["out_shape"],
        input_output_aliases=place["aliases"],
        compiler_params=_params("parallel"),
        name="w_in_unprep",
    )(_layer_index(l), buf, dwa, dwf, dwb)


def branch_w_prep(wao, wco, pw, cw, l):
    gd = D_POOL // N_GROUPS
    od = D_MODEL // N_GROUPS

    def body(wao_ref, wco_ref, pw_ref, cw_ref, ao_ref, co_ref, po_ref, co8_ref):
        ao_ref[...] = jnp.concatenate([wao_ref[j] for j in range(4)], axis=1)
        co_ref[...] = jnp.concatenate([wco_ref[j] for j in range(4)], axis=1)
        zero = jnp.zeros((gd, od), co_ref.dtype)
        po_ref[...] = jnp.concatenate(
            [jnp.concatenate([jnp.concatenate([pw_ref[j, g] for j in range(4)], axis=1) if g2 == g else zero
                              for g2 in range(N_GROUPS)], axis=1) for g in range(N_GROUPS)], axis=0)
        co8_ref[...] = jnp.zeros_like(co8_ref)
        co8_ref[0:CONV_K, :] = jnp.concatenate([cw_ref[j] for j in range(4)], axis=1)

    sel = lambda *shape: pl.BlockSpec((None,) + shape, lambda i: (l,) + (0,) * len(shape))
    return pl.pallas_call(
        body,
        grid=(1,),
        in_specs=[sel(4, D_ATTN, D_MODEL // 4), sel(4, D_CONV, D_MODEL // 4), sel(4, N_GROUPS, gd, od // 4),
                  sel(4, CONV_K, D_CONV // 4)],
        out_specs=[pl.BlockSpec((D_ATTN, D_MODEL), lambda i: (0, 0)), pl.BlockSpec((D_CONV, D_MODEL), lambda i: (0, 0)),
                   pl.BlockSpec((D_POOL, D_MODEL), lambda i: (0, 0)), pl.BlockSpec((8, D_CONV), lambda i: (0, 0))],
        out_shape=[jax.ShapeDtypeStruct((D_ATTN, D_MODEL), BF16), jax.ShapeDtypeStruct((D_CONV, D_MODEL), BF16),
                   jax.ShapeDtypeStruct((D_POOL, D_MODEL), BF16), jax.ShapeDtypeStruct((8, D_CONV), F32)],
        name="branch_w_prep",
    )(wao, wco, pw, cw)


def branch_g_place(dwao, dwco, dwpool, dwo, bufs, l):
    gd = D_POOL // N_GROUPS
    od = D_MODEL // N_GROUPS
    q = D_MODEL // 4

    def body(l_ref, b0, b1, b2, b3, a_ref, c_ref, p_ref, w_ref, ao_ref, co_ref, po_ref, wo_ref):
        a = a_ref[...]
        c = c_ref[...]
        p = p_ref[...]
        for j in range(4):
            ao_ref[j] = a[:, j * q:(j + 1) * q]
            co_ref[j] = c[:, j * q:(j + 1) * q]
            wo_ref[j] = w_ref[j * q:(j + 1) * q, :]
            for g in range(N_GROUPS):
                c0 = g * od + j * (od // 4)
                po_ref[j, g] = p[g * gd:(g + 1) * gd, c0:c0 + od // 4]

    whole = lambda a: pl.BlockSpec(a.shape, lambda i, l: (0,) * a.ndim)
    layer = lambda b: pl.BlockSpec((None,) + b.shape[1:], lambda i, l: (l[0],) + (0,) * (b.ndim - 1))
    return pl.pallas_call(
        body,
        grid_spec=pltpu.PrefetchScalarGridSpec(
            num_scalar_prefetch=1,
            grid=(1,),
            in_specs=[HBM_REF] * 4 + [whole(dwao), whole(dwco), whole(dwpool), whole(dwo)],
            out_specs=[layer(b) for b in bufs],
        ),
        out_shape=[jax.ShapeDtypeStruct(b.shape, b.dtype) for b in bufs],
        input_output_aliases={1: 0, 2: 1, 3: 2, 4: 3},
        compiler_params=_params("arbitrary"),
        name="branch_g_place",
    )(_layer_index(l), *bufs, dwao, dwco, dwpool, dwo)


def in_proj_fwd(x, g, wa, wf, wb):
    S = x.shape[0]

    def body(x_ref, g_ref, wa_ref, wf_ref, wb_ref, h_ref, pa_ref, pf_ref, pb_ref):
        xf = x_ref[...]
        r = lax.rsqrt(jnp.mean(xf * xf, axis=-1, keepdims=True) + EPS)
        h = (xf * r * g_ref[...]).astype(BF16)
        h_ref[...] = h
        pa_ref[...] = _dot(h, wa_ref[...]).astype(BF16)
        pf_ref[...] = _dot(h, wf_ref[...])
        pb_ref[...] = _dot(h, wb_ref[...]).astype(BF16)

    return pl.pallas_call(
        body,
        grid=(S // TM,),
        in_specs=[_rows(TM, D_MODEL), _whole(g), _whole(wa), _whole(wf), _whole(wb)],
        out_specs=[_rows(TM, D_MODEL), _rows(TM, D_QKV), _rows(TM, D_F), _rows(TM, D_B)],
        out_shape=[
            jax.ShapeDtypeStruct((S, D_MODEL), BF16),
            jax.ShapeDtypeStruct((S, D_QKV), BF16),
            jax.ShapeDtypeStruct((S, D_F), F32),
            jax.ShapeDtypeStruct((S, D_B), BF16),
        ],
        compiler_params=_params("parallel"),
        name="in_proj_fwd",
    )(x, g, wa, wf, wb)


def in_proj_bwd(x, g, dxm, dpa, dpf, dpb, wa, wf, wb):
    S = x.shape[0]

    def body(x_ref, g_ref, dxm_ref, dpa_ref, dpf_ref, dpb_ref, wa_ref, wf_ref, wb_ref, dx_ref, dg_ref):
        @pl.when(pl.program_id(0) == 0)
        def _():
            dg_ref[...] = jnp.zeros_like(dg_ref)

        dh = _dot_nt(dpa_ref[...], wa_ref[...]) + _dot_nt(dpf_ref[...], wf_ref[...]) + _dot_nt(dpb_ref[...], wb_ref[...])
        xf = x_ref[...]
        r = lax.rsqrt(jnp.mean(xf * xf, axis=-1, keepdims=True) + EPS)
        xhat = xf * r
        dg_ref[...] += jnp.sum(dh * xhat, axis=0, keepdims=True)
        gdh = dh * g_ref[...]
        dx_ref[...] = dxm_ref[...] + r * (gdh - xhat * jnp.mean(xhat * gdh, axis=-1, keepdims=True))

    return pl.pallas_call(
        body,
        grid=(S // TM,),
        in_specs=[_rows(TM, D_MODEL), _whole(g), _rows(TM, D_MODEL), _rows(TM, D_QKV), _rows(TM, D_F), _rows(TM, D_B),
                  _whole(wa), _whole(wf), _whole(wb)],
        out_specs=[_rows(TM, D_MODEL), pl.BlockSpec((1, D_MODEL), lambda i: (0, 0))],
        out_shape=[jax.ShapeDtypeStruct((S, D_MODEL), F32), jax.ShapeDtypeStruct((1, D_MODEL), F32)],
        compiler_params=_params("arbitrary"),
        name="in_proj_bwd",
    )(x, g, dxm, dpa, dpf, dpb, wa, wf, wb)


TS_WGRAD = 1024


def _wgrad_columns(n):
    return next(tn for tn in (1024, 768, 512, 256, 128) if n % tn == 0)


def wgrad(xa, dy, name):
    S, K = xa.shape
    N = dy.shape[1]
    ts = min(TS_WGRAD, S)
    tn = _wgrad_columns(N)
    tk = _wgrad_columns(K)

    def body(x_ref, dy_ref, o_ref):
        @pl.when(pl.program_id(2) == 0)
        def _():
            o_ref[...] = jnp.zeros_like(o_ref)

        o_ref[...] += _dot_tn(x_ref[...], dy_ref[...])

    return pl.pallas_call(
        body,
        grid=(K // tk, N // tn, S // ts),
        in_specs=[pl.BlockSpec((ts, tk), lambda i, j, k: (k, i)), pl.BlockSpec((ts, tn), lambda i, j, k: (k, j))],
        out_specs=pl.BlockSpec((tk, tn), lambda i, j, k: (i, j)),
        out_shape=jax.ShapeDtypeStruct((K, N), F32),
        compiler_params=_params("parallel", "parallel", "arbitrary"),
        name=name,
    )(xa, dy)


def wgrad_into(xa, dy, name, buf, l):
    S, K = xa.shape
    N = dy.shape[1]
    ts = min(TS_WGRAD, S)
    split = buf.ndim == 4
    tn = buf.shape[-1] if split else _wgrad_columns(N)
    place = _into_layer(buf)

    def body(l_ref, buf_ref, x_ref, dy_ref, o_ref):
        @pl.when(pl.program_id(1) == 0)
        def _():
            o_ref[...] = jnp.zeros_like(o_ref)

        o_ref[...] += _dot_tn(x_ref[...], dy_ref[...])

    if split:
        out_spec = pl.BlockSpec((None, None, K, tn), lambda j, k, l: (l[0], j, 0, 0))
    else:
        out_spec = pl.BlockSpec((None, K, tn), lambda j, k, l: (l[0], 0, j))
    return pl.pallas_call(
        body,
        grid_spec=pltpu.PrefetchScalarGridSpec(
            num_scalar_prefetch=1,
            grid=(N // tn, S // ts),
            in_specs=[place["in_spec"], pl.BlockSpec((ts, K), lambda j, k, l: (k, 0)),
                      pl.BlockSpec((ts, tn), lambda j, k, l: (k, j))],
            out_specs=out_spec,
        ),
        out_shape=place["out_shape"],
        input_output_aliases=place["aliases"],
        compiler_params=_params("parallel", "arbitrary"),
        name=name,
    )(_layer_index(l), buf, xa, dy)


def _head_mean_matrix():
    h = np.arange(D_ATTN) // HEAD_DIM
    return jnp.asarray((h[:, None] == h[None, :]).astype(np.float32) / HEAD_DIM, BF16)


def _place_matrix(lane0):
    m = np.zeros((N_PIECES * LANES, D_HEADS), np.float32)
    for i in range(N_PIECES):
        for h in range(N_HEADS):
            m[i * LANES + h, h * LANES + lane0 + i] = 1.0
    return jnp.asarray(m, BF16)


def _tri(n, upper):
    r = np.arange(n)
    m = (r[None, :] >= r[:, None]) if upper else (r[None, :] <= r[:, None])
    return jnp.asarray(m.astype(np.float32), BF16)


def _lanes_in(lane, lo, n):
    return (lane >= lo) & (lane < lo + n)


def qk_prep(pa, pf, gq, gk, fb):
    S = pa.shape[0]
    bd = _head_mean_matrix()
    tri = _tri(TM, upper=False)
    place_q = _place_matrix(LANE_C)
    place_k = _place_matrix(LANE_ONE)

    def body(q_ref, k_ref, v_ref, pf_ref, gq_ref, gk_ref, fb_ref, bd_ref, tri_ref, pq_ref, pk_ref,
             qx_ref, kx_ref, vx_ref, carry):
        @pl.when(pl.program_id(0) == 0)
        def _():
            carry[...] = jnp.zeros_like(carry)

        def head_norm(x_ref, g_ref, scale):
            xf = x_ref[...].astype(F32)
            ms = _dot((xf * xf).astype(BF16), bd_ref[...])
            return xf * lax.rsqrt(ms + EPS) * g_ref[...] * scale

        qh = head_norm(q_ref, gq_ref, HEAD_DIM ** -0.5 * LOG2E)
        kh = head_norm(k_ref, gk_ref, 1.0)
        vf = v_ref[...].astype(F32)

        z = pf_ref[...] + fb_ref[...]
        logf = jnp.minimum(z, 0.0) - jnp.log(1.0 + jnp.exp(-jnp.abs(z)))
        hi, lo = _split_bf16(logf)
        c = _dot(tri_ref[...], hi) + _dot(tri_ref[...], lo) + carry[...]
        carry[...] += jnp.sum(hi.astype(F32) + lo.astype(F32), axis=0, keepdims=True)
        pieces = jnp.concatenate(_pieces(c * LOG2E), axis=1)
        cq = _dot(pieces, pq_ref[...])
        ck = _dot(pieces, pk_ref[...])

        lane = lax.broadcasted_iota(jnp.int32, (TM, LANES), 1)
        low = lane < HEAD_DIM
        ones_q = _lanes_in(lane, LANE_ONE, N_PIECES).astype(F32)
        ones_k = (_lanes_in(lane, LANE_C, N_PIECES) | _lanes_in(lane, LANE_LSE, N_PIECES)).astype(F32)
        ones_v = _lanes_in(lane, LANE_C, N_PIECES + 1).astype(F32)
        for h in range(N_HEADS):
            blk = slice((h // 2) * LANES, (h // 2 + 1) * LANES)
            head = (lambda a: pltpu.roll(a[:, blk], HEAD_DIM, 1)) if h % 2 else (lambda a: a[:, blk])
            mine = slice(h * LANES, (h + 1) * LANES)
            qx_ref[h] = jnp.where(low, head(qh), cq[:, mine] + ones_q).astype(BF16)
            kx_ref[h] = jnp.where(low, head(kh), ones_k - ck[:, mine]).astype(BF16)
            vx_ref[h] = jnp.where(low, head(vf), ones_v).astype(BF16)

    heads = pl.BlockSpec((N_HEADS, TM, LANES), lambda i: (0, i, 0))
    out = jax.ShapeDtypeStruct((N_HEADS, S, LANES), BF16)
    return pl.pallas_call(
        body,
        grid=(S // TM,),
        in_specs=[pl.BlockSpec((TM, D_ATTN), lambda i: (i, 0)), pl.BlockSpec((TM, D_ATTN), lambda i: (i, 1)),
                  pl.BlockSpec((TM, D_ATTN), lambda i: (i, 2)),
                  _rows(TM, D_F), _whole(gq), _whole(gk), _whole(fb), _whole(bd), _whole(tri), _whole(place_q), _whole(place_k)],
        out_specs=[heads, heads, heads],
        out_shape=[out, out, out],
        scratch_shapes=[pltpu.VMEM((1, D_F), F32)],
        compiler_params=_params("arbitrary"),
        name="qk_prep",
    )(pa, pa, pa, pf, gq, gk, fb, bd, tri, place_q, place_k)


def attn_fwd(qx, kx, vx):
    S = qx.shape[1]
    nq = S // TQ

    def body(q_ref, k_ref, v_ref, o_ref, qb_ref):
        i = pl.program_id(1)
        lane = lax.broadcasted_iota(jnp.int32, (TQ, LANES), 1)
        q = [q_ref[0], q_ref[1]]

        def update(qh, m, acc, k, v, visible):
            z = _dot_nt(qh, k)
            if visible is not None:
                z = jnp.where(visible, z, NEG)
            m_new = jnp.maximum(m, jnp.max(z, axis=1, keepdims=True))
            pr = jnp.exp2(z - m_new)
            return m_new, jnp.exp2(m - m_new) * acc + _dot(pr.astype(BF16), v)

        def step(kt, carry, diagonal=False):
            ks = pl.multiple_of(kt * TQ, TQ)
            visible = None
            if diagonal:
                visible = lax.broadcasted_iota(jnp.int32, (TQ, TQ), 0) >= lax.broadcasted_iota(jnp.int32, (TQ, TQ), 1)
            return tuple(update(q[j], *carry[j], k_ref[j, pl.ds(ks, TQ), :], v_ref[j, pl.ds(ks, TQ), :], visible)
                         for j in range(2))

        init = tuple((jnp.full((TQ, 1), NEG, F32), jnp.zeros((TQ, LANES), F32)) for _ in range(2))
        carry = step(i, lax.fori_loop(0, i, step, init), diagonal=True)
        for j in range(2):
            m, acc = carry[j]
            l = jnp.sum(jnp.where(lane == LANE_C, acc, 0.0), axis=1, keepdims=True)
            o_ref[j] = acc / l
            n1, n2, n3 = _pieces(-(m + jnp.log(l) * LOG2E))
            qb_ref[j] = jnp.where(lane == LANE_LSE, n1,
                                  jnp.where(lane == LANE_LSE + 1, n2, jnp.where(lane == LANE_LSE + 2, n3, q[j])))

    pair_tile = pl.BlockSpec((2, TQ, LANES), lambda p, i: (p, i, 0))
    pair_all = pl.BlockSpec((2, S, LANES), lambda p, i: (p, 0, 0))
    return pl.pallas_call(
        body,
        grid=(N_HEADS // 2, nq),
        in_specs=[pair_tile, pair_all, pair_all],
        out_specs=[pair_tile, pair_tile],
        out_shape=[jax.ShapeDtypeStruct((N_HEADS, S, LANES), F32), jax.ShapeDtypeStruct((N_HEADS, S, LANES), BF16)],
        compiler_params=_params("parallel", "parallel"),
        name="attn_fwd",
    )(qx, kx, vx)


def attn_bwd(qxb, kx, vx, dox):
    S = qxb.shape[1]
    nq = S // TQ

    def body(q_ref, k_ref, v_ref, do_ref, dq_ref, dk_ref, dv_ref):
        kt = pl.program_id(1)

        @pl.when(kt == 0)
        def _():
            dq_ref[...] = jnp.zeros_like(dq_ref)

        k = [k_ref[0], k_ref[1]]
        v = [v_ref[0], v_ref[1]]

        def block(j, q0, nq_rows, k0, nk, diagonal):
            q = q_ref[j, pl.ds(q0, nq_rows), :]
            dout = do_ref[j, pl.ds(q0, nq_rows), :]
            kj, vj = k[j][k0:k0 + nk], v[j][k0:k0 + nk]
            z = _dot_nt(q, kj)
            if diagonal:
                visible = (lax.broadcasted_iota(jnp.int32, (nq_rows, nk), 0)
                           >= lax.broadcasted_iota(jnp.int32, (nq_rows, nk), 1))
                z = jnp.where(visible, z, NEG)
            pr = jnp.exp2(z)
            dv = _dot_tn(pr.astype(BF16), dout)
            dsb = (pr * _dot_nt(dout, vj)).astype(BF16)
            dq_ref[j, pl.ds(q0, nq_rows), :] += _dot(dsb, kj)
            return _dot_tn(dsb, q), dv

        def step(qi, carry):
            qs = pl.multiple_of(qi * TQ, TQ)
            new = []
            for j in range(2):
                dk, dv = block(j, qs, TQ, 0, TQ, False)
                new.append((carry[j][0] + dk, carry[j][1] + dv))
            return tuple(new)

        half = TQ // 2
        qs = pl.multiple_of(kt * TQ, TQ)
        carry = []
        for j in range(2):
            first = block(j, qs, TQ, 0, half, True)
            second = block(j, pl.multiple_of(qs + half, half), half, half, half, True)
            carry.append(tuple(jnp.concatenate(a, axis=0) for a in zip(first, second)))
        carry = lax.fori_loop(kt + 1, nq, step, tuple(carry))
        for j in range(2):
            dk_ref[j] = carry[j][0]
            dv_ref[j] = carry[j][1]

    pair_tile = pl.BlockSpec((2, TQ, LANES), lambda p, kt: (p, kt, 0))
    pair_all = pl.BlockSpec((2, S, LANES), lambda p, kt: (p, 0, 0))
    out = jax.ShapeDtypeStruct((N_HEADS, S, LANES), F32)
    return pl.pallas_call(
        body,
        grid=(N_HEADS // 2, nq),
        in_specs=[pair_all, pair_tile, pair_tile, pair_all],
        out_specs=[pair_all, pair_tile, pair_tile],
        out_shape=[out, out, out],
        compiler_params=_params("arbitrary", "arbitrary"),
        name="attn_bwd",
    )(qxb, kx, vx, dox)


def attn_bwd_post(pa, pf, dqx, dkx, dvx, gq, gk, fb):
    S = pa.shape[0]
    nt = S // TM
    bd = _head_mean_matrix()
    triu = _tri(TM, upper=True)
    rev = lambda i: nt - 1 - i

    def body(q_ref, k_ref, pf_ref, dqx_ref, dkx_ref, dvx_ref, gq_ref, gk_ref, fb_ref, bd_ref, triu_ref,
             dpa_ref, dpf_ref, dgq_ref, dgk_ref, dfb_ref, carry):
        @pl.when(pl.program_id(0) == 0)
        def _():
            carry[...] = jnp.zeros_like(carry)
            dgq_ref[...] = jnp.zeros_like(dgq_ref)
            dgk_ref[...] = jnp.zeros_like(dgk_ref)
            dfb_ref[...] = jnp.zeros_like(dfb_ref)

        lane = lax.broadcasted_iota(jnp.int32, (TM, LANES), 1)

        def columns(ref):
            return jnp.concatenate([jnp.where(lane < HEAD_DIM, ref[2 * p], pltpu.roll(ref[2 * p + 1], HEAD_DIM, 1))
                                    for p in range(N_HEADS // 2)], axis=1)

        def head_norm_bwd(x_ref, dy, g_ref, dg_ref):
            xf = x_ref[...].astype(F32)
            r = lax.rsqrt(_dot((xf * xf).astype(BF16), bd_ref[...]) + EPS)
            xhat = xf * r
            dg_ref[...] += jnp.sum(dy * xhat, axis=0, keepdims=True)
            gdy = dy * g_ref[...]
            return (r * (gdy - xhat * _dot((xhat * gdy).astype(BF16), bd_ref[...]))).astype(BF16)

        dpa_ref[:, 0:D_ATTN] = head_norm_bwd(q_ref, columns(dqx_ref) * HEAD_DIM ** -0.5, gq_ref, dgq_ref)
        dpa_ref[:, D_ATTN:2 * D_ATTN] = head_norm_bwd(k_ref, columns(dkx_ref) * LN2, gk_ref, dgk_ref)
        dpa_ref[:, 2 * D_ATTN:3 * D_ATTN] = columns(dvx_ref).astype(BF16)

        dc = jnp.zeros((TM, LANES), F32)
        for h in range(N_HEADS):
            both = jnp.where(lane == LANE_C, dqx_ref[h], 0.0) - jnp.where(lane == LANE_ONE, dkx_ref[h], 0.0)
            dc = jnp.where(lane == h, jnp.sum(both, axis=1, keepdims=True), dc)
        hi, lo = _split_bf16(dc)
        dlogf = _dot(triu_ref[...], hi) + _dot(triu_ref[...], lo) + carry[...]
        first = lax.broadcasted_iota(jnp.int32, (TM, D_F), 0) == 0
        carry[...] = jnp.sum(jnp.where(first, dlogf, 0.0), axis=0, keepdims=True)
        df = dlogf * _sigmoid(-(pf_ref[...] + fb_ref[...]))
        dfb_ref[...] += jnp.sum(df, axis=0, keepdims=True)
        dpf_ref[...] = df.astype(BF16)

    heads = pl.BlockSpec((N_HEADS, TM, LANES), lambda i: (0, rev(i), 0))
    return pl.pallas_call(
        body,
        grid=(nt,),
        in_specs=[pl.BlockSpec((TM, D_ATTN), lambda i: (rev(i), 0)), pl.BlockSpec((TM, D_ATTN), lambda i: (rev(i), 1)),
                  pl.BlockSpec((TM, D_F), lambda i: (rev(i), 0)), heads, heads, heads,
                  _whole(gq), _whole(gk), _whole(fb), _whole(bd), _whole(triu)],
        out_specs=[pl.BlockSpec((TM, D_QKV), lambda i: (rev(i), 0)), pl.BlockSpec((TM, D_F), lambda i: (rev(i), 0)),
                   pl.BlockSpec((1, D_ATTN), lambda i: (0, 0)), pl.BlockSpec((1, D_ATTN), lambda i: (0, 0)),
                   pl.BlockSpec((1, D_F), lambda i: (0, 0))],
        out_shape=[jax.ShapeDtypeStruct((S, D_QKV), BF16), jax.ShapeDtypeStruct((S, D_F), BF16),
                   jax.ShapeDtypeStruct((1, D_ATTN), F32), jax.ShapeDtypeStruct((1, D_ATTN), F32),
                   jax.ShapeDtypeStruct((1, D_F), F32)],
        scratch_shapes=[pltpu.VMEM((1, D_F), F32)],
        compiler_params=_params("arbitrary"),
        name="attn_bwd_post",
    )(pa, pa, pf, dqx, dkx, dvx, gq, gk, fb, bd, triu)


def _shift_down(ext, k):
    return pltpu.roll(ext, k, 0)[HALO:]


def _shift_up(ext, k, n):
    return pltpu.roll(ext, n + HALO - k, 0)[:n]


def _pool_lane_select(a2, a4, a8, a16, lane):
    return jnp.where(lane < 64, a2, jnp.where(lane < 128, a4, jnp.where(lane < 192, a8, a16)))


def _local_branches(o, pb, halo, have_prev, row0, wao, convw, wco, wpool, pscale):
    n = pb.shape[0]
    cx = pb[:, 0:D_CONV].astype(F32)
    cb = pb[:, D_CONV:2 * D_CONV].astype(F32)
    cc = pb[:, 2 * D_CONV:3 * D_CONV].astype(F32)
    px = pb[:, 3 * D_CONV:D_LOCAL].astype(F32)
    keep = have_prev.astype(F32)
    z = cc * cx
    z_ext = jnp.concatenate([halo[:, 2 * D_CONV:3 * D_CONV].astype(F32) * halo[:, 0:D_CONV].astype(F32) * keep, z], axis=0)
    z1 = _shift_down(z_ext, 1)
    z2 = _shift_down(z_ext, 2)
    conv = convw[0:1, :] * z2 + convw[1:2, :] * z1 + convw[2:3, :] * z
    cm = cb * conv

    u_ext = jnp.concatenate([halo[:, 3 * D_CONV:D_LOCAL].astype(F32) * keep, px], axis=0)
    s2 = u_ext + pltpu.roll(u_ext, 1, 0)
    s4 = s2 + pltpu.roll(s2, 2, 0)
    s8 = s4 + pltpu.roll(s4, 4, 0)
    s16 = s8 + pltpu.roll(s8, 8, 0)
    lane = lax.broadcasted_iota(jnp.int32, (n, D_POOL), 1)
    win = _pool_lane_select(2.0, 4.0, 8.0, 16.0, lane)
    t = (row0 + lax.broadcasted_iota(jnp.int32, (n, D_POOL), 0)).astype(F32)
    cnt = jnp.minimum(t + 1.0, win)
    feat = _pool_lane_select(s2[HALO:], s4[HALO:], s8[HALO:], s16[HALO:], lane) / cnt - px

    ya = _dot(o, wao)
    yc = _dot(cm.astype(BF16), wco)
    yp_pre = _dot(feat.astype(BF16), wpool)
    yp = yp_pre * pscale
    return dict(cx=cx, cb=cb, cc=cc, z=z, z1=z1, z2=z2, conv=conv, cm=cm, feat=feat, cnt=cnt, lane=lane,
                ya=ya, yc=yc, yp_pre=yp_pre, yp=yp)


def _halo_spec(tm, tile_of):
    per = tm // HALO
    return pl.BlockSpec((HALO, D_LOCAL), lambda i, *_: (jnp.maximum(tile_of(i) * per - 1, 0), 0))


def _heads_as_columns(ref):
    lane = lax.broadcasted_iota(jnp.int32, ref.shape[1:], 1)
    return jnp.concatenate([jnp.where(lane < HEAD_DIM, ref[2 * p], pltpu.roll(ref[2 * p + 1], HEAD_DIM, 1))
                            for p in range(N_HEADS // 2)], axis=1)


def mix_out_fwd(x, ox, pb, wao, convw, wco, wpool, pscale, wo_all, l):
    S = x.shape[0]
    tm = TM_MIX

    def body(l_ref, x_ref, o_ref, pb_ref, halo_ref, wao_ref, cw_ref, wco_ref, wp_ref, ps_ref, wo_ref, y_ref):
        i = pl.program_id(0)
        pb = pb_ref[...]
        o = _heads_as_columns(o_ref).astype(BF16)
        b = _local_branches(o, pb, halo_ref[...], i > 0, i * tm, wao_ref[...], cw_ref[...], wco_ref[...],
                            wp_ref[...], ps_ref[...])
        g0 = _sigmoid(pb[:, D_LOCAL:D_LOCAL + D_MODEL].astype(F32))
        g1 = _sigmoid(pb[:, D_LOCAL + D_MODEL:D_LOCAL + 2 * D_MODEL].astype(F32))
        g2 = _sigmoid(pb[:, D_LOCAL + 2 * D_MODEL:D_B].astype(F32))
        merged = g0 * b["ya"] + g1 * b["yc"] + g2 * b["yp"]
        y_ref[...] = x_ref[...] + _dot(merged.astype(BF16), wo_ref[...])

    return pl.pallas_call(
        body,
        grid_spec=pltpu.PrefetchScalarGridSpec(
            num_scalar_prefetch=1,
            grid=(S // tm,),
            in_specs=[_rows(tm, D_MODEL), pl.BlockSpec((N_HEADS, tm, LANES), lambda i, l: (0, i, 0)), _rows(tm, D_B),
                      _halo_spec(tm, lambda i: i), _whole(wao), _whole(convw), _whole(wco), _whole(wpool), _whole(pscale),
                      _layer((D_MODEL, D_MODEL))],
            out_specs=_rows(tm, D_MODEL),
        ),
        out_shape=jax.ShapeDtypeStruct((S, D_MODEL), F32),
        compiler_params=_params("parallel"),
        name="mix_out_fwd",
    )(_layer_index(l), x, ox, pb, pb, wao, convw, wco, wpool, pscale, wo_all)


def mix_out_bwd(dxm, ox, pb, wao, convw, wco, wpool, pscale, wo_all, l):
    S = dxm.shape[0]
    tm = TM_MIX
    nt = S // tm
    rev = lambda i: nt - 1 - i
    rows = lambda n: pl.BlockSpec((tm, n), lambda i, l: (rev(i), 0))
    heads = pl.BlockSpec((N_HEADS, tm, LANES), lambda i, l: (0, rev(i), 0))
    acc = lambda r, c: pl.BlockSpec((r, c), lambda i, l: (0, 0))

    def body(l_ref, dxm_ref, o_ref, pb_ref, halo_ref, wao_ref, cw_ref, wco_ref, wp_ref, ps_ref, wo_ref,
             dpb_ref, do_ref, dwo_ref, dwao_ref, dwco_ref, dwp_ref, dcw_ref, dps_ref, next_dconv, next_e):
        i = pl.program_id(0)
        r = rev(i)

        @pl.when(i == 0)
        def _():
            for ref in (dwo_ref, dwao_ref, dwco_ref, dwp_ref, dcw_ref, dps_ref, next_dconv, next_e):
                ref[...] = jnp.zeros_like(ref)

        pb = pb_ref[...]
        o = _heads_as_columns(o_ref).astype(BF16)
        cw = cw_ref[...]
        b = _local_branches(o, pb, halo_ref[...], r > 0, r * tm, wao_ref[...], cw, wco_ref[...], wp_ref[...], ps_ref[...])
        g0 = _sigmoid(pb[:, D_LOCAL:D_LOCAL + D_MODEL].astype(F32))
        g1 = _sigmoid(pb[:, D_LOCAL + D_MODEL:D_LOCAL + 2 * D_MODEL].astype(F32))
        g2 = _sigmoid(pb[:, D_LOCAL + 2 * D_MODEL:D_B].astype(F32))
        dxb = dxm_ref[...].astype(BF16)
        merged = g0 * b["ya"] + g1 * b["yc"] + g2 * b["yp"]
        dwo_ref[...] += _dot_tn(merged.astype(BF16), dxb)
        dmer = _dot_nt(dxb, wo_ref[...])
        dpb_ref[:, D_LOCAL:D_LOCAL + D_MODEL] = (dmer * b["ya"] * (g0 * (1.0 - g0))).astype(BF16)
        dpb_ref[:, D_LOCAL + D_MODEL:D_LOCAL + 2 * D_MODEL] = (dmer * b["yc"] * (g1 * (1.0 - g1))).astype(BF16)
        dpb_ref[:, D_LOCAL + 2 * D_MODEL:D_B] = (dmer * b["yp"] * (g2 * (1.0 - g2))).astype(BF16)

        dya = (dmer * g0).astype(BF16)
        dwao_ref[...] += _dot_tn(o, dya)
        da = _dot_nt(dya, wao_ref[...])
        lane = lax.broadcasted_iota(jnp.int32, (tm, LANES), 1)
        for h in range(N_HEADS):
            two = da[:, (h // 2) * LANES:(h // 2 + 1) * LANES]
            dah = jnp.where(lane < HEAD_DIM, pltpu.roll(two, HEAD_DIM, 1) if h % 2 else two, 0.0).astype(BF16)
            d1, d2, d3 = _pieces(-jnp.sum(dah.astype(F32) * o_ref[h], axis=1, keepdims=True))
            do_ref[h] = jnp.where(lane == LANE_C + 1, d1, jnp.where(lane == LANE_C + 2, d2,
                                                                  jnp.where(lane == LANE_C + 3, d3, dah)))

        dyc = (dmer * g1).astype(BF16)
        dwco_ref[...] += _dot_tn(b["cm"].astype(BF16), dyc)
        dcm = _dot_nt(dyc, wco_ref[...])
        dconv = dcm * b["cb"]
        dcw_ref[0:1, :] += jnp.sum(dconv * b["z2"], axis=0, keepdims=True)
        dcw_ref[1:2, :] += jnp.sum(dconv * b["z1"], axis=0, keepdims=True)
        dcw_ref[2:3, :] += jnp.sum(dconv * b["z"], axis=0, keepdims=True)
        d_ext = jnp.concatenate([dconv, next_dconv[...]], axis=0)
        dz = cw[2:3, :] * dconv + cw[1:2, :] * _shift_up(d_ext, 1, tm) + cw[0:1, :] * _shift_up(d_ext, 2, tm)
        next_dconv[...] = dconv[0:HALO]
        dpb_ref[:, 0:D_CONV] = (dz * b["cc"]).astype(BF16)
        dpb_ref[:, D_CONV:2 * D_CONV] = (dcm * b["conv"]).astype(BF16)
        dpb_ref[:, 2 * D_CONV:3 * D_CONV] = (dz * b["cx"]).astype(BF16)

        dyp = dmer * g2
        dps_ref[...] += jnp.sum(dyp * b["yp_pre"], axis=0, keepdims=True)
        dyps = (dyp * ps_ref[...]).astype(BF16)
        dwp_ref[...] += _dot_tn(b["feat"].astype(BF16), dyps)
        dfeat = _dot_nt(dyps, wp_ref[...])
        e = dfeat / b["cnt"]
        e_ext = jnp.concatenate([e, next_e[...]], axis=0)
        up = lambda a, k: pltpu.roll(a, tm + HALO - k, 0)
        f2 = e_ext + up(e_ext, 1)
        f4 = f2 + up(f2, 2)
        f8 = f4 + up(f4, 4)
        f16 = f8 + up(f8, 8)
        next_e[...] = e[0:HALO]
        dpb_ref[:, 3 * D_CONV:D_LOCAL] = (_pool_lane_select(f2[:tm], f4[:tm], f8[:tm], f16[:tm], b["lane"]) - dfeat).astype(BF16)

    return pl.pallas_call(
        body,
        grid_spec=pltpu.PrefetchScalarGridSpec(
            num_scalar_prefetch=1,
            grid=(nt,),
            in_specs=[rows(D_MODEL), heads, rows(D_B), _halo_spec(tm, rev),
                      _whole(wao), _whole(convw), _whole(wco), _whole(wpool), _whole(pscale), _layer((D_MODEL, D_MODEL))],
            out_specs=[rows(D_B), heads, acc(D_MODEL, D_MODEL), acc(D_ATTN, D_MODEL), acc(D_CONV, D_MODEL),
                       acc(D_POOL, D_MODEL), acc(8, D_CONV), acc(1, D_MODEL)],
            scratch_shapes=[pltpu.VMEM((HALO, D_CONV), F32), pltpu.VMEM((HALO, D_POOL), F32)],
        ),
        out_shape=[jax.ShapeDtypeStruct((S, D_B), BF16), jax.ShapeDtypeStruct((N_HEADS, S, LANES), BF16),
                   jax.ShapeDtypeStruct((D_MODEL, D_MODEL), F32), jax.ShapeDtypeStruct((D_ATTN, D_MODEL), F32),
                   jax.ShapeDtypeStruct((D_CONV, D_MODEL), F32), jax.ShapeDtypeStruct((D_POOL, D_MODEL), F32),
                   jax.ShapeDtypeStruct((8, D_CONV), F32), jax.ShapeDtypeStruct((1, D_MODEL), F32)],
        compiler_params=_params("arbitrary"),
        name="mix_out_bwd",
    )(_layer_index(l), dxm, ox, pb, pb, wao, convw, wco, wpool, pscale, wo_all)


FF_SHARD = 2 * D_FF // 4


def ffn_fwd(x, g, w1_all, w2_all, l):
    S = x.shape[0]

    def body(l_ref, x_ref, g_ref, w1_ref, w2_ref, y_ref, u_ref):
        xf = x_ref[...]
        r = lax.rsqrt(jnp.mean(xf * xf, axis=-1, keepdims=True) + EPS)
        h = (xf * r * g_ref[...]).astype(BF16)
        u = jnp.concatenate([_dot(h, w1_ref[j]) for j in range(4)], axis=1)
        u_ref[...] = u.astype(BF16)
        gt = u[:, 0:D_FF]
        act = gt * _sigmoid(gt) * u[:, D_FF:2 * D_FF]
        y_ref[...] = xf + _dot(act.astype(BF16), w2_ref[...])

    return pl.pallas_call(
        body,
        grid_spec=pltpu.PrefetchScalarGridSpec(
            num_scalar_prefetch=1,
            grid=(S // TM,),
            in_specs=[_rows(TM, D_MODEL), _whole(g), _layer((4, D_MODEL, FF_SHARD)), _layer((D_FF, D_MODEL))],
            out_specs=[_rows(TM, D_MODEL), _rows(TM, 2 * D_FF)],
        ),
        out_shape=[jax.ShapeDtypeStruct((S, D_MODEL), F32), jax.ShapeDtypeStruct((S, 2 * D_FF), BF16)],
        compiler_params=_params("parallel"),
        name="ffn_fwd",
    )(_layer_index(l), x, g, w1_all, w2_all)


def ffn_bwd(x, g, dy, u, w1_all, w2_all, l):
    S = x.shape[0]

    def body(l_ref, x_ref, g_ref, dy_ref, u_ref, w1_ref, w2_ref, dx_ref, du_ref, act_ref, h_ref, dyb_ref, dg_ref):
        @pl.when(pl.program_id(0) == 0)
        def _():
            dg_ref[...] = jnp.zeros_like(dg_ref)

        dyf = dy_ref[...]
        dyb_ref[...] = dyf.astype(BF16)
        dact = _dot_nt(dyb_ref[...], w2_ref[...])
        gt = u_ref[:, 0:D_FF].astype(F32)
        up = u_ref[:, D_FF:2 * D_FF].astype(F32)
        sg = _sigmoid(gt)
        silu = gt * sg
        act_ref[...] = (silu * up).astype(BF16)
        du_ref[:, 0:D_FF] = (dact * up * (sg * (1.0 + gt * (1.0 - sg)))).astype(BF16)
        du_ref[:, D_FF:2 * D_FF] = (dact * silu).astype(BF16)
        dh = _dot_nt(du_ref[:, 0:FF_SHARD], w1_ref[0])
        for j in range(1, 4):
            dh = dh + _dot_nt(du_ref[:, j * FF_SHARD:(j + 1) * FF_SHARD], w1_ref[j])
        xf = x_ref[...]
        r = lax.rsqrt(jnp.mean(xf * xf, axis=-1, keepdims=True) + EPS)
        xhat = xf * r
        h_ref[...] = (xhat * g_ref[...]).astype(BF16)
        dg_ref[...] += jnp.sum(dh * xhat, axis=0, keepdims=True)
        gdh = dh * g_ref[...]
        dx_ref[...] = dyf + r * (gdh - xhat * jnp.mean(xhat * gdh, axis=-1, keepdims=True))

    return pl.pallas_call(
        body,
        grid_spec=pltpu.PrefetchScalarGridSpec(
            num_scalar_prefetch=1,
            grid=(S // TM,),
            in_specs=[_rows(TM, D_MODEL), _whole(g), _rows(TM, D_MODEL), _rows(TM, 2 * D_FF),
                      _layer((4, D_MODEL, FF_SHARD)), _layer((D_FF, D_MODEL))],
            out_specs=[_rows(TM, D_MODEL), _rows(TM, 2 * D_FF), _rows(TM, D_FF), _rows(TM, D_MODEL), _rows(TM, D_MODEL),
                       pl.BlockSpec((1, D_MODEL), lambda i, l: (0, 0))],
        ),
        out_shape=[jax.ShapeDtypeStruct((S, D_MODEL), F32), jax.ShapeDtypeStruct((S, 2 * D_FF), BF16),
                   jax.ShapeDtypeStruct((S, D_FF), BF16), jax.ShapeDtypeStruct((S, D_MODEL), BF16),
                   jax.ShapeDtypeStruct((S, D_MODEL), BF16), jax.ShapeDtypeStruct((1, D_MODEL), F32)],
        compiler_params=_params("arbitrary"),
        name="ffn_bwd",
    )(_layer_index(l), x, g, dy, u, w1_all, w2_all)


def loss_head(y, target):
    S = y.shape[0]

    def body(y_ref, t_ref, loss_ref, dy_ref):
        @pl.when(pl.program_id(0) == 0)
        def _():
            loss_ref[0, 0] = 0.0

        err = y_ref[...] - t_ref[...]
        dy_ref[...] = err * (1.0 / D_MODEL)
        loss_ref[0, 0] += 0.5 * jnp.sum(jnp.mean(err * err, axis=-1))

    return pl.pallas_call(
        body,
        grid=(S // TM,),
        in_specs=[_rows(TM, D_MODEL), _rows(TM, D_MODEL)],
        out_specs=[pl.BlockSpec((1, 1), lambda i: (0, 0), memory_space=pltpu.SMEM), _rows(TM, D_MODEL)],
        out_shape=[jax.ShapeDtypeStruct((1, 1), F32), jax.ShapeDtypeStruct((S, D_MODEL), F32)],
        compiler_params=_params("arbitrary"),
        name="loss_head",
    )(y, target)


SPLIT = {
    "w_in": ((D_MODEL, D_IN), 1),
    "w_attn_out": ((D_ATTN, D_MODEL), 1),
    "w_conv_out": ((D_CONV, D_MODEL), 1),
    "pool_w": ((N_GROUPS, D_POOL // N_GROUPS, D_MODEL // N_GROUPS), 2),
    "w_o": ((D_MODEL, D_MODEL), 0),
    "w_ffn_in": ((D_MODEL, 2 * D_FF), 1),
    "w_ffn_out": ((D_FF, D_MODEL), 0),
}
SMALL = {"norm_mix_g": D_MODEL, "forget_b": N_HEADS, "q_norm_g": HEAD_DIM, "k_norm_g": HEAD_DIM,
         "pool_scale": D_MODEL, "norm_ffn_g": D_MODEL, "conv_w": CONV_K * D_CONV}
WEIGHTS = ["norm_mix_g", "w_in", "forget_b", "q_norm_g", "k_norm_g", "w_attn_out", "conv_w", "w_conv_out", "pool_w",
           "pool_scale", "w_o", "norm_ffn_g", "w_ffn_in", "w_ffn_out"]


MIX_W = ["w_in", "w_attn_out", "w_conv_out", "pool_w", "w_o"]
FFN_W = ["w_ffn_in", "w_ffn_out"]


def _row(a):
    return a.astype(F32).reshape(1, -1)


def mix_fwd(x, full, li, small):
    n = full["w_o"].shape[0]
    wa, wf, wb = w_in_prep(full["w_in"], li)
    wao, wco, wpool, convw = branch_w_prep(full["w_attn_out"], full["w_conv_out"], full["pool_w"], full["conv_w"], li)
    w = dict(wa=wa, wf=wf, wb=wb, wao=wao, wco=wco, wpool=wpool, convw=convw,
             w_o=full["w_o"].reshape(n, D_MODEL, D_MODEL), at=li,
             g1=_row(small["norm_mix_g"]), pscale=_row(small["pool_scale"]),
             gq=_row(jnp.tile(small["q_norm_g"], N_HEADS)), gk=_row(jnp.tile(small["k_norm_g"], N_HEADS)),
             fb=_row(jnp.pad(small["forget_b"], (0, D_F - N_HEADS))))
    h, pa, pf, pb = in_proj_fwd(x, w["g1"], wa, wf, wb)
    qx, kx, vx = qk_prep(pa, pf, w["gq"], w["gk"], w["fb"])
    ox, qxb = attn_fwd(qx, kx, vx)
    xm = mix_out_fwd(x, ox, pb, wao, convw, wco, wpool, w["pscale"], w["w_o"], li)
    return xm, w, dict(x=x, h=h, pa=pa, pf=pf, pb=pb, kx=kx, vx=vx, ox=ox, qxb=qxb)


def ffn_half_fwd(xm, full, li, g2):
    n = full["w_ffn_out"].shape[0]
    w = dict(w1=full["w_ffn_in"], w2=full["w_ffn_out"].reshape(n, D_FF, D_MODEL), at=li, g2=_row(g2))
    y, u = ffn_fwd(xm, w["g2"], w["w1"], w["w2"], li)
    return y, w, dict(xm=xm, u=u)


def ffn_half_bwd(dx, w, s, gi, big):
    n = big["w_ffn_out"].shape[0]
    big = dict(big)
    dxm, du, act, h2, dyb, dg2 = ffn_bwd(s["xm"], w["g2"], dx, s["u"], w["w1"], w["w2"], w["at"])
    big["w_ffn_out"] = wgrad_into(act, dyb, "wgrad_ffn_out", big["w_ffn_out"].reshape(n, D_FF, D_MODEL),
                                  gi).reshape(big["w_ffn_out"].shape)
    big["w_ffn_in"] = wgrad_into(h2, du, "wgrad_ffn_in", big["w_ffn_in"], gi)
    return dxm, big, dict(norm_ffn_g=dg2[0])


def mix_bwd_weights(dxm, w, s, gi, big):
    big = dict(big)
    dpb, dox, dwo, dwao, dwco, dwpool, dconvw, dpscale = mix_out_bwd(
        dxm, s["ox"], s["pb"], w["wao"], w["convw"], w["wco"], w["wpool"], w["pscale"], w["w_o"], w["at"])
    dqx, dkx, dvx = attn_bwd(s["qxb"], s["kx"], s["vx"], dox)
    dpa, dpf, dgq, dgk, dfb = attn_bwd_post(s["pa"], s["pf"], dqx, dkx, dvx, w["gq"], w["gk"], w["fb"])
    big["w_in"] = w_in_unprep(wgrad(s["h"], dpa, "wgrad_in_qkv"), wgrad(s["h"], dpf, "wgrad_in_f"),
                              wgrad(s["h"], dpb, "wgrad_in_b"), big["w_in"], gi)
    big["w_attn_out"], big["w_conv_out"], big["pool_w"], big["w_o"] = branch_g_place(
        dwao, dwco, dwpool, dwo, (big["w_attn_out"], big["w_conv_out"], big["pool_w"], big["w_o"]), gi)
    sm = dict(forget_b=dfb[0, 0:N_HEADS], q_norm_g=dgq.reshape(N_HEADS, HEAD_DIM).sum(0),
              k_norm_g=dgk.reshape(N_HEADS, HEAD_DIM).sum(0), pool_scale=dpscale[0],
              conv_w=dconvw[0:CONV_K].reshape(-1))
    return (dpa, dpf, dpb), big, sm


def mix_bwd_input(dproj, dxm, w, s):
    dx, dg1 = in_proj_bwd(s["x"], w["g1"], dxm, *dproj, w["wa"], w["wf"], w["wb"])
    return dx, dict(norm_mix_g=dg1[0])


def mix_bwd(dxm, w, s, gi, big):
    dproj, big, sm = mix_bwd_weights(dxm, w, s, gi, big)
    dx, g1 = mix_bwd_input(dproj, dxm, w, s)
    return dx, big, {**sm, **g1}


def grad_buffers(full):
    return {n: lax.empty(full[n].shape, F32) for n in SPLIT}


def local_step(x, target, gathered, small):
    n_layers = small["norm_mix_g"].shape[0]
    done = []
    for l in range(n_layers):
        xm, wm, sm = mix_fwd(x, gathered, l, {n: v[l] for n, v in small.items()})
        x, wf, sf = ffn_half_fwd(xm, gathered, l, small["norm_ffn_g"][l])
        done.append((wm, sm, wf, sf))
    loss, dx = loss_head(x, target)
    big = grad_buffers(gathered)
    small_grads = [None] * n_layers
    for l in reversed(range(n_layers)):
        wm, sm, wf, sf = done[l]
        dxm, ffn_big, g_ffn = ffn_half_bwd(dx, wf, sf, l, {n: big[n] for n in FFN_W})
        dx, mix_big, g_mix = mix_bwd(dxm, wm, sm, l, {n: big[n] for n in MIX_W})
        big = {**ffn_big, **mix_big}
        small_grads[l] = {**g_ffn, **g_mix}
    return loss, dx, big, {n: jnp.stack([g[n] for g in small_grads]) for n in SMALL}


def adamw(w, g, m, v, name):
    R, C = w.shape
    tm = 256 if R % 256 == 0 else R

    def body(w_ref, g_ref, m_ref, v_ref, d_ref, nm_ref, nv_ref):
        gr = g_ref[...]
        m_new = ADAM_B1 * m_ref[...] + (1.0 - ADAM_B1) * gr
        v_new = ADAM_B2 * v_ref[...] + (1.0 - ADAM_B2) * jnp.square(gr)
        nm_ref[...] = m_new
        nv_ref[...] = v_new
        m_hat = m_new / (1.0 - ADAM_B1 ** ADAM_STEP)
        v_hat = v_new / (1.0 - ADAM_B2 ** ADAM_STEP)
        d_ref[...] = -ADAM_LR * (m_hat / (jnp.sqrt(v_hat) + ADAM_EPS) + ADAM_WD * w_ref[...])

    spec = _rows(tm, C)
    out = jax.ShapeDtypeStruct((R, C), F32)
    return pl.pallas_call(
        body,
        grid=(R // tm,),
        in_specs=[spec] * 4,
        out_specs=[spec] * 3,
        out_shape=[out] * 3,
        compiler_params=_params("parallel"),
        name=name,
    )(w, g, m, v)


MESH = pl.DeviceIdType.MESH
HBM_REF = pl.BlockSpec(memory_space=pl.ANY)
N_CHIPS = 4
N_DEV = 8
SMALL_SHAPE = (128, LANES)


def _mesh_pos():
    return lax.axis_index("x"), lax.axis_index("y"), lax.axis_index("c")


def _other_chips(x, y):
    return [(1 - x, y), (x, 1 - y), (1 - x, 1 - y)]


def _remote(src, dst, send_sem, recv_sem, to):
    return pltpu.make_async_remote_copy(src_ref=src, dst_ref=dst, send_sem=send_sem, recv_sem=recv_sem,
                                        device_id=to, device_id_type=MESH)


def _row_tile(rows):
    for tm in (256, 176, 128):
        if rows % tm == 0:
            return tm
    return rows


def _as4(a):
    return a.reshape(a.shape[0], a.shape[1], -1, a.shape[-1])


def _by_columns(shape):
    return (shape[-2] // 2) % 8 != 0


def _core_rows(buf, c):
    R, C = buf.shape[-2:]
    if R % 2:
        whole = (pl.ds(0, R), pl.ds(0, C))
        return whole, whole, False
    if _by_columns(buf.shape):
        return (pl.ds(0, R), pl.ds(c * (C // 2), C // 2)), (pl.ds(0, R), pl.ds((1 - c) * (C // 2), C // 2)), True
    return (pl.ds(c * (R // 2), R // 2), pl.ds(0, C)), (pl.ds((1 - c) * (R // 2), R // 2), pl.ds(0, C)), True


def _half_shape(g):
    R, C = g.shape[-2:]
    return g.shape[:-2] + ((R, C // 2) if _by_columns(g.shape) else (R // 2, C))


def place_shard(chip, w, lo, hi, dtype, name, after=None):
    w3 = w.reshape(w.shape[0], -1, w.shape[-1])
    _, R, C = w3.shape
    tm = _row_tile(R)
    idle = [] if after is None else [after]

    def body(chip_ref, w_ref, *rest):
        rest[-1][...] = w_ref[...].astype(dtype)

    return pl.pallas_call(
        body,
        grid_spec=pltpu.PrefetchScalarGridSpec(
            num_scalar_prefetch=1,
            grid=(hi - lo, R // tm),
            in_specs=[pl.BlockSpec((None, tm, C), lambda l, i, chip: (lo + l, i, 0))] + [_whole(a) for a in idle],
            out_specs=pl.BlockSpec((None, None, tm, C), lambda l, i, chip: (l, chip[0], i, 0)),
        ),
        out_shape=jax.ShapeDtypeStruct((hi - lo, N_CHIPS, R, C), dtype),
        compiler_params=_params("parallel", "parallel"),
        name=name,
    )(chip, w3, *idle)


HBM_SPACE = pl.BlockSpec(memory_space=pltpu.HBM)
SEM_SPACE = pl.BlockSpec(memory_space=pltpu.SEMAPHORE)
IN_FLIGHT = pltpu.SideEffectType.DATAFLOW_SIDE_EFFECTING


def _sem_table(send_sems, recv_sems):
    return lambda t, j: (send_sems.at[t, j], recv_sems.at[t, j])


def _sem_per_peer(send_sems, recv_sems):
    return lambda t, j: (send_sems[j], recv_sems[j])


def _gather_ici(bufs, sem, sends=True, lands=True):
    x, y, c = _mesh_pos()
    me = 2 * x + y
    out, into = [], []
    for t, buf in enumerate(bufs):
        mine, _, _ = _core_rows(buf, c)
        part = lambda k, buf=buf, mine=mine: buf.at[pl.ds(0, buf.shape[0]), k, *mine]
        for j, (px, py) in enumerate(_other_chips(x, y)):
            if sends:
                out.append(_remote(part(me), part(me), *sem(t, j), (px, py, c)))
            if lands:
                into.append(_remote(part(2 * px + py), part(2 * px + py), *sem(t, j), (px, py, c)))
    return out, into


def _gather_d2d(bufs, send_sems, recv_sems, first):
    x, y, c = _mesh_pos()
    sends, lands = [], []
    for t, buf in enumerate(bufs):
        mine, theirs, split = _core_rows(buf, c)
        if not split:
            continue
        for j, (px, py) in enumerate(_other_chips(x, y)):
            part = lambda half, buf=buf, k=2 * px + py: buf.at[pl.ds(0, buf.shape[0]), k, *half]
            sems = (send_sems.at[t, first + j], recv_sems.at[t, first + j], (x, y, 1 - c))
            sends.append(_remote(part(mine), part(mine), *sems))
            lands.append(_remote(part(theirs), part(theirs), *sems))
    return sends, lands


def gather_shards(bufs):
    n = len(bufs)

    def body(*refs):
        outs = refs[n:2 * n]
        send_sems, recv_sems = refs[2 * n:]
        ici_out, ici_in = _gather_ici(outs, _sem_table(send_sems, recv_sems))
        d2d_out, d2d_in = _gather_d2d(outs, send_sems, recv_sems, 3)
        for cp in ici_out:
            cp.start()
        for cp in ici_in:
            cp.wait_recv()
        for cp in d2d_out:
            cp.start()
        for cp in d2d_in:
            cp.wait_recv()
        for cp in ici_out + d2d_out:
            cp.wait_send()

    return pl.pallas_call(
        body,
        in_specs=[HBM_REF] * n,
        out_specs=[HBM_REF] * n,
        out_shape=[jax.ShapeDtypeStruct(b.shape, b.dtype) for b in bufs],
        input_output_aliases={t: t for t in range(n)},
        scratch_shapes=[pltpu.SemaphoreType.DMA((n, 6)), pltpu.SemaphoreType.DMA((n, 6))],
        name="gather_shards",
    )(*bufs)


def gather_start(bufs):
    n = len(bufs)

    def body(*refs):
        send_sems, recv_sems = refs[n:n + 3], refs[n + 3:n + 6]
        outs = refs[n + 6:2 * n + 6]
        token = refs[2 * n + 6]
        for cp in _gather_ici(outs, _sem_per_peer(send_sems, recv_sems), lands=False)[0]:
            cp.start()
        token[...] = jnp.zeros_like(token)

    res = pl.pallas_call(
        body,
        in_specs=[HBM_SPACE] * n,
        out_specs=[SEM_SPACE] * 6 + [HBM_SPACE] * n + [pl.BlockSpec(memory_space=pltpu.VMEM)],
        out_shape=[pltpu.SemaphoreType.DMA(())] * 6
        + [pltpu.HBM(b.shape, b.dtype) for b in bufs] + [jax.ShapeDtypeStruct((8, LANES), F32)],
        input_output_aliases={t: t + 6 for t in range(n)},
        compiler_params=pltpu.CompilerParams(has_side_effects=IN_FLIGHT),
        name="gather_start",
    )(*[pltpu.with_memory_space_constraint(b, pltpu.HBM) for b in bufs])
    return list(res[0:3]), list(res[3:6]), list(res[6:n + 6]), res[n + 6]


def gather_wait(send_sems, recv_sems, bufs, after):
    n = len(bufs)

    def body(*refs):
        ins_send, ins_recv = refs[n:n + 3], refs[n + 3:n + 6]
        outs = refs[n + 7:]
        sends, lands = _gather_ici(outs, _sem_per_peer(ins_send, ins_recv))
        for cp in lands:
            cp.wait_recv()
        for cp in sends:
            cp.wait_send()

    return list(pl.pallas_call(
        body,
        in_specs=[HBM_SPACE] * n + [SEM_SPACE] * 6 + [HBM_REF],
        out_specs=[HBM_SPACE] * n,
        out_shape=[pltpu.HBM(b.shape, b.dtype) for b in bufs],
        input_output_aliases={t: t for t in range(n)},
        compiler_params=pltpu.CompilerParams(has_side_effects=IN_FLIGHT),
        name="gather_wait",
    )(*bufs, *send_sems, *recv_sems, after))


def gather_forward(bufs):
    n = len(bufs)

    def body(*refs):
        outs = refs[n:2 * n]
        send_sems, recv_sems = refs[2 * n:]
        sends, lands = _gather_d2d(outs, send_sems, recv_sems, 0)
        for cp in sends:
            cp.start()
        for cp in lands:
            cp.wait_recv()
        for cp in sends:
            cp.wait_send()

    return list(pl.pallas_call(
        body,
        in_specs=[HBM_REF] * n,
        out_specs=[HBM_REF] * n,
        out_shape=[jax.ShapeDtypeStruct(b.shape, b.dtype) for b in bufs],
        input_output_aliases={t: t for t in range(n)},
        scratch_shapes=[pltpu.SemaphoreType.DMA((n, 3)), pltpu.SemaphoreType.DMA((n, 3))],
        name="gather_forward",
    )(*bufs))


def pair_exchange(grads):
    n = len(grads)

    def body(*refs):
        ins, outs = refs[:n], refs[n:2 * n]
        send_sems, recv_sems = refs[2 * n:]
        cps = _pair_copies(ins, outs, lambda t: (send_sems.at[t], recv_sems.at[t]))
        for cp in cps:
            cp.start()
        for cp in cps:
            cp.wait()

    return list(pl.pallas_call(
        body,
        in_specs=[HBM_REF] * n,
        out_specs=[HBM_REF] * n,
        out_shape=[jax.ShapeDtypeStruct(_half_shape(g), g.dtype) for g in grads],
        scratch_shapes=[pltpu.SemaphoreType.DMA((n,)), pltpu.SemaphoreType.DMA((n,))],
        name="pair_exchange",
    )(*grads))


def _pair_copies(grads, lands, sem):
    x, y, c = _mesh_pos()
    cps = []
    for t in range(len(grads)):
        _, theirs, _ = _core_rows(grads[t], c)
        src = grads[t].at[pl.ds(0, grads[t].shape[0]), pl.ds(0, N_CHIPS), *theirs]
        cps.append(_remote(src, lands[t], *sem(t), (x, y, 1 - c)))
    return cps


def pair_exchange_start(grads):
    n = len(grads)
    lands = [lax.empty(_half_shape(g), g.dtype) for g in grads]

    def body(*refs):
        send_sem, recv_sem = refs[2 * n:2 * n + 2]
        outs = refs[2 * n + 2:4 * n + 2]
        token = refs[4 * n + 2]
        for cp in _pair_copies(outs[:n], outs[n:], lambda t: (send_sem, recv_sem)):
            cp.start()
        token[...] = jnp.zeros_like(token)

    res = pl.pallas_call(
        body,
        in_specs=[HBM_SPACE] * (2 * n),
        out_specs=[SEM_SPACE] * 2 + [HBM_SPACE] * (2 * n) + [pl.BlockSpec(memory_space=pltpu.VMEM)],
        out_shape=[pltpu.SemaphoreType.DMA(())] * 2 + [pltpu.HBM(a.shape, a.dtype) for a in list(grads) + lands]
        + [jax.ShapeDtypeStruct((8, LANES), F32)],
        input_output_aliases={t: t + 2 for t in range(2 * n)},
        compiler_params=pltpu.CompilerParams(has_side_effects=IN_FLIGHT),
        name="pair_exchange_start",
    )(*[pltpu.with_memory_space_constraint(a, pltpu.HBM) for a in list(grads) + lands])
    return res[0], res[1], list(res[2:n + 2]), list(res[n + 2:2 * n + 2]), res[2 * n + 2]


def pair_exchange_wait(send_sem, recv_sem, grads, lands, after):
    n = len(grads)

    def body(*refs):
        in_send, in_recv = refs[2 * n:2 * n + 2]
        outs = refs[2 * n + 3:]
        for cp in _pair_copies(outs[:n], outs[n:], lambda t: (in_send, in_recv)):
            cp.wait()

    res = pl.pallas_call(
        body,
        in_specs=[HBM_SPACE] * (2 * n) + [SEM_SPACE] * 2 + [HBM_REF],
        out_specs=[HBM_SPACE] * (2 * n),
        out_shape=[pltpu.HBM(a.shape, a.dtype) for a in list(grads) + list(lands)],
        input_output_aliases={t: t for t in range(2 * n)},
        compiler_params=pltpu.CompilerParams(has_side_effects=IN_FLIGHT),
        name="pair_exchange_wait",
    )(*grads, *lands, send_sem, recv_sem, after)
    return list(res[:n]), list(res[n:])


def pair_sum(core, g, t, name):
    n, _, hr, hc = t.shape
    R, C = g.shape[2:]
    tm = _row_tile(hr)
    per = hr // tm
    mine = (lambda a, i, c: (a, i, c[0])) if _by_columns(g.shape) else (lambda a, i, c: (a, per * c[0] + i, 0))

    def body(c_ref, g_ref, t_ref, o_ref):
        o_ref[...] = (g_ref[...] + t_ref[...]).astype(BF16)

    tile = pl.BlockSpec((None, tm, hc), lambda a, i, c: (a, i, 0))
    out = pl.pallas_call(
        body,
        grid_spec=pltpu.PrefetchScalarGridSpec(
            num_scalar_prefetch=1,
            grid=(n * N_CHIPS, per),
            in_specs=[pl.BlockSpec((None, tm, hc), mine), tile],
            out_specs=tile,
        ),
        out_shape=jax.ShapeDtypeStruct((n * N_CHIPS, hr, hc), BF16),
        compiler_params=_params("parallel", "parallel"),
        name=name,
    )(core, g.reshape(n * N_CHIPS, R, C), t.reshape(n * N_CHIPS, hr, hc))
    return out.reshape(t.shape)


def _chip_copies(sums, recv, sem):
    x, y, c = _mesh_pos()
    cps = []
    for t in range(len(sums)):
        for j, (px, py) in enumerate(_other_chips(x, y)):
            src = sums[t].at[pl.ds(0, sums[t].shape[0]), 2 * px + py]
            cps.append(_remote(src, recv[t].at[j], *sem(t, j), (px, py, c)))
    return cps


def _recv_shape(s):
    return (N_CHIPS - 1, s.shape[0]) + s.shape[2:]


def chip_exchange(sums):
    n = len(sums)

    def body(*refs):
        ins, outs = refs[:n], refs[n:2 * n]
        cps = _chip_copies(ins, outs, _sem_table(*refs[2 * n:]))
        for cp in cps:
            cp.start()
        for cp in cps:
            cp.wait()

    return list(pl.pallas_call(
        body,
        in_specs=[HBM_REF] * n,
        out_specs=[HBM_REF] * n,
        out_shape=[jax.ShapeDtypeStruct(_recv_shape(s), s.dtype) for s in sums],
        scratch_shapes=[pltpu.SemaphoreType.DMA((n, 3)), pltpu.SemaphoreType.DMA((n, 3))],
        name="chip_exchange",
    )(*sums))


def chip_exchange_start(sums):
    n = len(sums)
    lands = [lax.empty(_recv_shape(s), s.dtype) for s in sums]

    def body(*refs):
        send_sems, recv_sems = refs[2 * n:2 * n + 3], refs[2 * n + 3:2 * n + 6]
        outs = refs[2 * n + 6:4 * n + 6]
        token = refs[4 * n + 6]
        for cp in _chip_copies(outs[:n], outs[n:2 * n], _sem_per_peer(send_sems, recv_sems)):
            cp.start()
        token[...] = jnp.zeros_like(token)

    res = pl.pallas_call(
        body,
        in_specs=[HBM_SPACE] * (2 * n),
        out_specs=[SEM_SPACE] * 6 + [HBM_SPACE] * (2 * n) + [pl.BlockSpec(memory_space=pltpu.VMEM)],
        out_shape=[pltpu.SemaphoreType.DMA(())] * 6 + [pltpu.HBM(a.shape, a.dtype) for a in list(sums) + lands]
        + [jax.ShapeDtypeStruct((8, LANES), F32)],
        input_output_aliases={t: t + 6 for t in range(2 * n)},
        compiler_params=pltpu.CompilerParams(has_side_effects=IN_FLIGHT),
        name="chip_exchange_start",
    )(*[pltpu.with_memory_space_constraint(a, pltpu.HBM) for a in list(sums) + lands])
    return list(res[0:3]), list(res[3:6]), list(res[6:n + 6]), list(res[n + 6:2 * n + 6]), res[2 * n + 6]


def chip_exchange_wait(send_sems, recv_sems, sums, lands, after):
    n = len(sums)

    def body(*refs):
        ins_send, ins_recv = refs[2 * n:2 * n + 3], refs[2 * n + 3:2 * n + 6]
        outs = refs[2 * n + 7:]
        for cp in _chip_copies(outs[:n], outs[n:], _sem_per_peer(ins_send, ins_recv)):
            cp.wait()

    res = pl.pallas_call(
        body,
        in_specs=[HBM_SPACE] * (2 * n) + [SEM_SPACE] * 6 + [HBM_REF],
        out_specs=[HBM_SPACE] * (2 * n),
        out_shape=[pltpu.HBM(a.shape, a.dtype) for a in list(sums) + list(lands)],
        input_output_aliases={t: t for t in range(2 * n)},
        compiler_params=pltpu.CompilerParams(has_side_effects=IN_FLIGHT),
        name="chip_exchange_wait",
    )(*sums, *lands, *send_sems, *recv_sems, after)
    return list(res[:n]), list(res[n:])


def chip_sum_into(core, chip, recv, sums, total, lo, name):
    _, n, hr, hc = recv.shape
    tm = _row_tile(hr)
    per = hr // tm
    if _by_columns(total.shape):
        mine = lambda a, i, c, k: (lo + a, i, c[0])
    else:
        mine = lambda a, i, c, k: (lo + a, per * c[0] + i, 0)

    def body(c_ref, k_ref, r_ref, s_ref, t_ref, o_ref):
        acc = s_ref[...].astype(F32)
        for j in range(N_CHIPS - 1):
            acc = acc + r_ref[j].astype(F32)
        o_ref[...] = acc

    return pl.pallas_call(
        body,
        grid_spec=pltpu.PrefetchScalarGridSpec(
            num_scalar_prefetch=2,
            grid=(n, per),
            in_specs=[pl.BlockSpec((N_CHIPS - 1, None, tm, hc), lambda a, i, c, k: (0, a, i, 0)),
                      pl.BlockSpec((None, None, tm, hc), lambda a, i, c, k: (a, k[0], i, 0)),
                      HBM_REF],
            out_specs=pl.BlockSpec((None, tm, hc), mine),
        ),
        out_shape=jax.ShapeDtypeStruct(total.shape, F32),
        input_output_aliases={4: 0},
        compiler_params=_params("parallel", "parallel"),
        name=name,
    )(core, chip, recv, sums, total)


def sibling_share(totals):
    n = len(totals)

    def body(*refs):
        outs = refs[n:2 * n]
        send_sems, recv_sems = refs[2 * n:]
        x, y, c = _mesh_pos()
        half = lambda t, which: outs[t].at[pl.ds(0, DEPTH), *_core_rows(outs[t], c)[which]]
        sent = [_remote(half(t, 0), half(t, 0), send_sems.at[t], recv_sems.at[t], (x, y, 1 - c)) for t in range(n)]
        for cp in sent:
            cp.start()
        for t in range(n):
            _remote(half(t, 1), half(t, 1), send_sems.at[t], recv_sems.at[t], (x, y, 1 - c)).wait_recv()
        for cp in sent:
            cp.wait_send()

    return pl.pallas_call(
        body,
        in_specs=[HBM_REF] * n,
        out_specs=[HBM_REF] * n,
        out_shape=[jax.ShapeDtypeStruct(t.shape, t.dtype) for t in totals],
        input_output_aliases={t: t for t in range(n)},
        scratch_shapes=[pltpu.SemaphoreType.DMA((n,)), pltpu.SemaphoreType.DMA((n,))],
        name="sibling_share",
    )(*totals)


def small_allgather(small):
    def body(s_ref, a_ref, send_sems, recv_sems, local_sem):
        x, y, c = _mesh_pos()
        me = 4 * x + 2 * y + c
        own = pltpu.make_async_copy(s_ref, a_ref.at[me], local_sem)
        own.start()
        sent = []
        for k in range(1, N_DEV):
            peer = (x ^ (k >> 2), y ^ ((k >> 1) & 1), c ^ (k & 1))
            cp = _remote(s_ref, a_ref.at[me], send_sems.at[k - 1], recv_sems.at[k - 1], peer)
            cp.start()
            sent.append(cp)
        for k in range(1, N_DEV):
            px, py, pc = x ^ (k >> 2), y ^ ((k >> 1) & 1), c ^ (k & 1)
            _remote(s_ref, a_ref.at[4 * px + 2 * py + pc], send_sems.at[k - 1], recv_sems.at[k - 1], (px, py, pc)).wait_recv()
        for cp in sent:
            cp.wait_send()
        own.wait()

    return pl.pallas_call(
        body,
        in_specs=[HBM_REF],
        out_specs=HBM_REF,
        out_shape=jax.ShapeDtypeStruct((N_DEV,) + SMALL_SHAPE, small.dtype),
        scratch_shapes=[pltpu.SemaphoreType.DMA((N_DEV - 1,)), pltpu.SemaphoreType.DMA((N_DEV - 1,)), pltpu.SemaphoreType.DMA],
        name="small_allgather",
    )(small)


def small_sum(blocks):
    def body(a_ref, o_ref):
        acc = a_ref[0]
        for d in range(1, N_DEV):
            acc = acc + a_ref[d]
        o_ref[...] = acc

    return pl.pallas_call(
        body,
        in_specs=[pl.BlockSpec(memory_space=pltpu.VMEM)],
        out_specs=pl.BlockSpec(memory_space=pltpu.VMEM),
        out_shape=jax.ShapeDtypeStruct(SMALL_SHAPE, F32),
        name="small_sum",
    )(blocks)


def pack_small(grads, loss):
    flat = jnp.concatenate([grads[n].reshape(-1) for n in SMALL] + [loss.reshape(-1)])
    size = SMALL_SHAPE[0] * SMALL_SHAPE[1]
    return jnp.pad(flat, (0, size - flat.shape[0])).reshape(SMALL_SHAPE)


def unpack_small(packed):
    flat = packed.reshape(-1)
    out, off = {}, 0
    for n, size in SMALL.items():
        out[n] = flat[off:off + DEPTH * size].reshape(DEPTH, size)
        off += DEPTH * size
    return out, flat[off]


def kernel(x, norm_mix_g, w_in, forget_b, q_norm_g, k_norm_g, w_attn_out, conv_w, w_conv_out, pool_w, pool_scale, w_o, norm_ffn_g, w_ffn_in, w_ffn_out, loss_target, m_norm_mix_g, m_w_in, m_forget_b, m_q_norm_g, m_k_norm_g, m_w_attn_out, m_conv_w, m_w_conv_out, m_pool_w, m_pool_scale, m_w_o, m_norm_ffn_g, m_w_ffn_in, m_w_ffn_out, v_norm_mix_g, v_w_in, v_forget_b, v_q_norm_g, v_k_norm_g, v_w_attn_out, v_conv_w, v_w_conv_out, v_pool_w, v_pool_scale, v_w_o, v_norm_ffn_g, v_w_ffn_in, v_w_ffn_out):
    given = dict(locals())
    weights = {n: given[n] for n in WEIGHTS}

    core = lax.axis_index("c").astype(jnp.int32)
    chip = (2 * lax.axis_index("x") + lax.axis_index("y")).astype(jnp.int32)

    core, chip = core.reshape(1), chip.reshape(1)
    mix_names, all_names = MIX_W + ["conv_w"], MIX_W + FFN_W + ["conv_w"]

    def placed(lo, hi, names, after=None):
        piece = None if after is None else _as4(next(iter(after.values())))[0, 0, 0:HALO, 0:LANES]
        return [place_shard(chip, weights[n], lo, hi, F32 if n == "conv_w" else BF16, "place_" + n,
                            piece if i == 0 else None) for i, n in enumerate(names)]

    def as_weights(bufs, names):
        return {n: b.reshape(b.shape[:2] + weights[n].shape[1:]) for n, b in zip(names, bufs)}

    def landed(start, names, after):
        return as_weights(gather_forward(gather_wait(*start[:3], after)), names)

    def layer_small(l, *tokens):
        small = {n: weights[n][l] for n in SMALL if n != "conv_w"}
        for t in tokens:
            small["norm_mix_g"] = small["norm_mix_g"] + t[0, 0]
        return small

    done = [None] * DEPTH
    first = as_weights(gather_shards(placed(0, 1, mix_names)), mix_names)
    ffn0 = gather_start(placed(0, 1, FFN_W, after=first))
    layer1 = gather_start(placed(1, 2, all_names, after=first))
    xm, wm, sm = mix_fwd(x[0], first, 0, layer_small(0, ffn0[3], layer1[3]))
    xs, wf, sf = ffn_half_fwd(xm, landed(ffn0, FFN_W, xm), 0, weights["norm_ffn_g"][0])
    done[0] = (wm, sm, wf, sf)
    full = landed(layer1, all_names, xs)
    rest = gather_start(placed(2, DEPTH, all_names, after=full))
    xm, wm, sm = mix_fwd(xs, full, 0, layer_small(1, rest[3]))
    xs, wf, sf = ffn_half_fwd(xm, full, 0, weights["norm_ffn_g"][1])
    done[1] = (wm, sm, wf, sf)
    full = landed(rest, all_names, xs)
    for l in range(2, DEPTH):
        xm, wm, sm = mix_fwd(xs, full, l - 2, layer_small(l))
        xs, wf, sf = ffn_half_fwd(xm, full, l - 2, weights["norm_ffn_g"][l])
        done[l] = (wm, sm, wf, sf)
    loss, dx = loss_head(xs, loss_target[0])

    small_grads = [None] * DEPTH
    totals = {n: lax.empty((DEPTH,) + _as4(weights[n][None]).shape[2:], F32) for n in SPLIT}

    def buffers(names, n_layers):
        return {n: lax.empty((n_layers, N_CHIPS) + weights[n].shape[1:], F32) for n in names}

    def pair_sums(names, grads, theirs):
        return [pair_sum(core, a, t, "pair_sum_" + n) for n, a, t in zip(names, grads, theirs)]

    def add_chips(names, recv, sums, lo):
        for n, r, s in zip(names, recv, sums):
            totals[n] = chip_sum_into(core, chip, r, s, totals[n], lo, "chip_sum_" + n)

    big = buffers(SPLIT, DEPTH - 1)
    for l in reversed(range(1, DEPTH)):
        wm, sm, wf, sf = done[l]
        dxm, ffn_big, g_ffn = ffn_half_bwd(dx, wf, sf, l - 1, {n: big[n] for n in FFN_W})
        dx, mix_big, g_mix = mix_bwd(dxm, wm, sm, l - 1, {n: big[n] for n in MIX_W})
        big = {**ffn_big, **mix_big}
        small_grads[l] = {**g_ffn, **g_mix}
    pair = pair_exchange_start([_as4(big[n]) for n in SPLIT])

    wm, sm, wf, sf = done[0]
    wf = dict(wf, g2=wf["g2"] + pair[4][0:1, 0:1])
    dxm, ffn_big, g_ffn = ffn_half_bwd(dx, wf, sf, 0, buffers(FFN_W, 1))
    chips = chip_exchange_start(pair_sums(SPLIT, *pair_exchange_wait(*pair[:4], ffn_big["w_ffn_in"])))
    ffn_grads = [_as4(ffn_big[n]) for n in FFN_W]
    ffn_chips = chip_exchange_start(pair_sums(FFN_W, ffn_grads, pair_exchange(ffn_grads)))
    wm = dict(wm, pscale=wm["pscale"] + chips[4][0:1, 0:1] + ffn_chips[4][0:1, 0:1])
    dproj, mix_big, g_mix = mix_bwd_weights(dxm, wm, sm, 0, buffers(MIX_W, 1))
    mix_grads = [_as4(mix_big[n]) for n in MIX_W]
    mix_chips = chip_exchange_start(pair_sums(MIX_W, mix_grads, pair_exchange(mix_grads)))
    dx, g_in = mix_bwd_input(dproj, dxm, dict(wm, g1=wm["g1"] + mix_chips[4][0:1, 0:1]), sm)
    small_grads[0] = {**g_ffn, **g_mix, **g_in}

    reduced, deltas, new_m, new_v = {}, {}, {}, {}

    def update(names):
        for n in names:
            w = weights[n]
            flat = (-1, w.shape[-1])
            d, nm, nv = adamw(w.reshape(flat), reduced[n].reshape(flat), given["m_" + n].reshape(flat),
                              given["v_" + n].reshape(flat), "adamw_" + n)
            deltas[n], new_m[n], new_v[n] = d.reshape(w.shape), nm.reshape(w.shape), nv.reshape(w.shape)

    def share(names):
        for n, t in zip(names, sibling_share([totals[n] for n in names])):
            reduced[n] = t.reshape(weights[n].shape)

    sums, recv = chip_exchange_wait(*chips[:4], dx)
    add_chips(SPLIT, recv, sums, 1)
    sums, recv = chip_exchange_wait(*ffn_chips[:4], dx)
    add_chips(FFN_W, recv, sums, 0)
    share(FFN_W)
    update(FFN_W)
    sums, recv = chip_exchange_wait(*mix_chips[:4], deltas["w_ffn_in"])
    add_chips(MIX_W, recv, sums, 0)
    share(MIX_W)

    small_grads = {n: jnp.stack([g[n] for g in small_grads]) for n in SMALL}
    small_total, loss_sum = unpack_small(small_sum(small_allgather(pack_small(small_grads, loss))))
    chip = chip[0]
    cols = D_CONV // N_CHIPS
    small_total["conv_w"] = lax.dynamic_slice_in_dim(small_total["conv_w"].reshape(DEPTH, CONV_K, D_CONV), chip * cols, cols, axis=2)
    for n in SMALL:
        reduced[n] = small_total[n].reshape(weights[n].shape)
    update(MIX_W + list(SMALL))

    return (loss_sum, dx[None], *[reduced[n] for n in WEIGHTS], *[deltas[n] for n in WEIGHTS],
            *[new_m[n] for n in WEIGHTS], *[new_v[n] for n in WEIGHTS])
```

```python
import functools

import numpy as np
import jax
import jax.numpy as jnp
from jax import lax
from jax.experimental import pallas as pl
from jax.experimental.pallas import tpu as pltpu

F32 = jnp.float32
BF16 = jnp.bfloat16

D_MODEL = 1024
DEPTH = 4
HEAD_DIM = 64
N_HEADS = 8
D_ATTN = 512
D_CONV = 256
D_POOL = 256
D_FF = 2816
D_IN = 5640
CONV_K = 3
POOL_WINDOWS = (2, 4, 8, 16)
N_GROUPS = len(POOL_WINDOWS)
EPS = 1e-6
ADAM_LR, ADAM_B1, ADAM_B2, ADAM_EPS, ADAM_WD, ADAM_STEP = 0.001, 0.9, 0.999, 1e-08, 0.01, 10

D_QKV = 3 * D_ATTN
D_F = 128
D_B = 3 * D_CONV + D_POOL + 3 * D_MODEL
D_LOCAL = 3 * D_CONV + D_POOL

LANES = 128
D_HEADS = N_HEADS * LANES
HALO = 16
VMEM_LIMIT = 56 * 1024 * 1024
NEG = -1e30
LOG2E = 1.4426950408889634
LN2 = 0.6931471805599453

TM = 256
TM_MIX = 256
TQ = 1024

LANE_C = 64
LANE_ONE = 67
LANE_LSE = 70
N_PIECES = 3


def _dot(a, b):
    return jnp.dot(a, b, preferred_element_type=F32)


def _dot_nt(a, b):
    return lax.dot_general(a, b, (((1,), (1,)), ((), ())), preferred_element_type=F32)


def _dot_tn(a, b):
    return lax.dot_general(a, b, (((0,), (0,)), ((), ())), preferred_element_type=F32)


def _params(*sem):
    return pltpu.CompilerParams(dimension_semantics=sem, vmem_limit_bytes=VMEM_LIMIT)


def _rows(tm, n):
    return pl.BlockSpec((tm, n), lambda i, *_: (i, 0))


def _whole(a):
    nd = a.ndim
    return pl.BlockSpec(a.shape, lambda *_: (0,) * nd)


def _layer(shape):
    nd = len(shape)
    return pl.BlockSpec((None,) + tuple(shape), lambda *a: (a[-1][0],) + (0,) * nd)


def _layer_index(l):
    return jnp.full((1,), l, jnp.int32)


def _split_bf16(x):
    hi = x.astype(BF16)
    lo = (x - hi.astype(F32)).astype(BF16)
    return hi, lo


def _pieces(x):
    p1 = x.astype(BF16)
    r1 = x - p1.astype(F32)
    p2 = r1.astype(BF16)
    p3 = (r1 - p2.astype(F32)).astype(BF16)
    return p1, p2, p3


def _sigmoid(x):
    return 1.0 / (1.0 + jnp.exp(-x))


def w_in_prep(win, l):
    tr = 256
    n = D_IN // 4
    v_rest = D_QKV - n
    b0 = v_rest + N_HEADS

    def body(s0, s1, s2, s3, wa_ref, wf_ref, wb_ref):
        b = s1[...]
        wa_ref[...] = jnp.concatenate([s0[...], b[:, 0:v_rest]], axis=1)
        wf_ref[...] = jnp.concatenate([b[:, v_rest:b0], jnp.zeros((tr, D_F - N_HEADS), b.dtype)], axis=1)
        wb_ref[...] = jnp.concatenate([b[:, b0:n], s2[...], s3[...]], axis=1)

    shard = lambda j: pl.BlockSpec((None, None, tr, n), lambda i: (l, j, i, 0))
    return pl.pallas_call(
        body,
        grid=(D_MODEL // tr,),
        in_specs=[shard(0), shard(1), shard(2), shard(3)],
        out_specs=[_rows(tr, D_QKV), _rows(tr, D_F), _rows(tr, D_B)],
        out_shape=[jax.ShapeDtypeStruct((D_MODEL, D_QKV), win.dtype), jax.ShapeDtypeStruct((D_MODEL, D_F), win.dtype),
                   jax.ShapeDtypeStruct((D_MODEL, D_B), win.dtype)],
        compiler_params=_params("parallel"),
        name="w_in_prep",
    )(win, win, win, win)


def _into_layer(buf):
    return dict(in_spec=HBM_REF, out_shape=jax.ShapeDtypeStruct(buf.shape, buf.dtype), aliases={1: 0})


def w_in_unprep(dwa, dwf, dwb, buf, l):
    tr = 256
    n = D_IN // 4
    v_rest = D_QKV - n
    b1 = n - v_rest - N_HEADS
    place = _into_layer(buf)

    def body(l_ref, buf_ref, a_ref, f_ref, b_ref, o_ref):
        a = a_ref[...]
        b = b_ref[...]
        o_ref[0] = a[:, 0:n]
        o_ref[1] = jnp.concatenate([a[:, n:D_QKV], f_ref[:, 0:N_HEADS], b[:, 0:b1]], axis=1)
        o_ref[2] = b[:, b1:b1 + n]
        o_ref[3] = b[:, b1 + n:D_B]

    return pl.pallas_call(
        body,
        grid_spec=pltpu.PrefetchScalarGridSpec(
            num_scalar_prefetch=1,
            grid=(D_MODEL // tr,),
            in_specs=[place["in_spec"], _rows(tr, D_QKV), _rows(tr, D_F), _rows(tr, D_B)],
            out_specs=pl.BlockSpec((None, 4, tr, n), lambda i, l: (l[0], 0, i, 0)),
        ),
        out_shape=place["out_shape"],
        input_output_aliases=place["aliases"],
        compiler_params=_params("parallel"),
        name="w_in_unprep",
    )(_layer_index(l), buf, dwa, dwf, dwb)


def branch_w_prep(wao, wco, pw, cw, l):
    gd = D_POOL // N_GROUPS
    od = D_MODEL // N_GROUPS

    def body(wao_ref, wco_ref, pw_ref, cw_ref, ao_ref, co_ref, po_ref, co8_ref):
        ao_ref[...] = jnp.concatenate([wao_ref[j] for j in range(4)], axis=1)
        co_ref[...] = jnp.concatenate([wco_ref[j] for j in range(4)], axis=1)
        zero = jnp.zeros((gd, od), co_ref.dtype)
        po_ref[...] = jnp.concatenate(
            [jnp.concatenate([jnp.concatenate([pw_ref[j, g] for j in range(4)], axis=1) if g2 == g else zero
                              for g2 in range(N_GROUPS)], axis=1) for g in range(N_GROUPS)], axis=0)
        co8_ref[...] = jnp.zeros_like(co8_ref)
        co8_ref[0:CONV_K, :] = jnp.concatenate([cw_ref[j] for j in range(4)], axis=1)

    sel = lambda *shape: pl.BlockSpec((None,) + shape, lambda i: (l,) + (0,) * len(shape))
    return pl.pallas_call(
        body,
        grid=(1,),
        in_specs=[sel(4, D_ATTN, D_MODEL // 4), sel(4, D_CONV, D_MODEL // 4), sel(4, N_GROUPS, gd, od // 4),
                  sel(4, CONV_K, D_CONV // 4)],
        out_specs=[pl.BlockSpec((D_ATTN, D_MODEL), lambda i: (0, 0)), pl.BlockSpec((D_CONV, D_MODEL), lambda i: (0, 0)),
                   pl.BlockSpec((D_POOL, D_MODEL), lambda i: (0, 0)), pl.BlockSpec((8, D_CONV), lambda i: (0, 0))],
        out_shape=[jax.ShapeDtypeStruct((D_ATTN, D_MODEL), BF16), jax.ShapeDtypeStruct((D_CONV, D_MODEL), BF16),
                   jax.ShapeDtypeStruct((D_POOL, D_MODEL), BF16), jax.ShapeDtypeStruct((8, D_CONV), F32)],
        name="branch_w_prep",
    )(wao, wco, pw, cw)


def branch_g_place(dwao, dwco, dwpool, dwo, bufs, l):
    gd = D_POOL // N_GROUPS
    od = D_MODEL // N_GROUPS
    q = D_MODEL // 4

    def body(l_ref, b0, b1, b2, b3, a_ref, c_ref, p_ref, w_ref, ao_ref, co_ref, po_ref, wo_ref):
        a = a_ref[...]
        c = c_ref[...]
        p = p_ref[...]
        for j in range(4):
            ao_ref[j] = a[:, j * q:(j + 1) * q]
            co_ref[j] = c[:, j * q:(j + 1) * q]
            wo_ref[j] = w_ref[j * q:(j + 1) * q, :]
            for g in range(N_GROUPS):
                c0 = g * od + j * (od // 4)
                po_ref[j, g] = p[g * gd:(g + 1) * gd, c0:c0 + od // 4]

    whole = lambda a: pl.BlockSpec(a.shape, lambda i, l: (0,) * a.ndim)
    layer = lambda b: pl.BlockSpec((None,) + b.shape[1:], lambda i, l: (l[0],) + (0,) * (b.ndim - 1))
    return pl.pallas_call(
        body,
        grid_spec=pltpu.PrefetchScalarGridSpec(
            num_scalar_prefetch=1,
            grid=(1,),
            in_specs=[HBM_REF] * 4 + [whole(dwao), whole(dwco), whole(dwpool), whole(dwo)],
            out_specs=[layer(b) for b in bufs],
        ),
        out_shape=[jax.ShapeDtypeStruct(b.shape, b.dtype) for b in bufs],
        input_output_aliases={1: 0, 2: 1, 3: 2, 4: 3},
        compiler_params=_params("arbitrary"),
        name="branch_g_place",
    )(_layer_index(l), *bufs, dwao, dwco, dwpool, dwo)


def in_proj_fwd(x, g, wa, wf, wb):
    S = x.shape[0]

    def body(x_ref, g_ref, wa_ref, wf_ref, wb_ref, h_ref, pa_ref, pf_ref, pb_ref):
        xf = x_ref[...]
        r = lax.rsqrt(jnp.mean(xf * xf, axis=-1, keepdims=True) + EPS)
        h = (xf * r * g_ref[...]).astype(BF16)
        h_ref[...] = h
        pa_ref[...] = _dot(h, wa_ref[...]).astype(BF16)
        pf_ref[...] = _dot(h, wf_ref[...])
        pb_ref[...] = _dot(h, wb_ref[...]).astype(BF16)

    return pl.pallas_call(
        body,
        grid=(S // TM,),
        in_specs=[_rows(TM, D_MODEL), _whole(g), _whole(wa), _whole(wf), _whole(wb)],
        out_specs=[_rows(TM, D_MODEL), _rows(TM, D_QKV), _rows(TM, D_F), _rows(TM, D_B)],
        out_shape=[
            jax.ShapeDtypeStruct((S, D_MODEL), BF16),
            jax.ShapeDtypeStruct((S, D_QKV), BF16),
            jax.ShapeDtypeStruct((S, D_F), F32),
            jax.ShapeDtypeStruct((S, D_B), BF16),
        ],
        compiler_params=_params("parallel"),
        name="in_proj_fwd",
    )(x, g, wa, wf, wb)


def in_proj_bwd(x, g, dxm, dpa, dpf, dpb, wa, wf, wb):
    S = x.shape[0]

    def body(x_ref, g_ref, dxm_ref, dpa_ref, dpf_ref, dpb_ref, wa_ref, wf_ref, wb_ref, dx_ref, dg_ref):
        @pl.when(pl.program_id(0) == 0)
        def _():
            dg_ref[...] = jnp.zeros_like(dg_ref)

        dh = _dot_nt(dpa_ref[...], wa_ref[...]) + _dot_nt(dpf_ref[...], wf_ref[...]) + _dot_nt(dpb_ref[...], wb_ref[...])
        xf = x_ref[...]
        r = lax.rsqrt(jnp.mean(xf * xf, axis=-1, keepdims=True) + EPS)
        xhat = xf * r
        dg_ref[...] += jnp.sum(dh * xhat, axis=0, keepdims=True)
        gdh = dh * g_ref[...]
        dx_ref[...] = dxm_ref[...] + r * (gdh - xhat * jnp.mean(xhat * gdh, axis=-1, keepdims=True))

    return pl.pallas_call(
        body,
        grid=(S // TM,),
        in_specs=[_rows(TM, D_MODEL), _whole(g), _rows(TM, D_MODEL), _rows(TM, D_QKV), _rows(TM, D_F), _rows(TM, D_B),
                  _whole(wa), _whole(wf), _whole(wb)],
        out_specs=[_rows(TM, D_MODEL), pl.BlockSpec((1, D_MODEL), lambda i: (0, 0))],
        out_shape=[jax.ShapeDtypeStruct((S, D_MODEL), F32), jax.ShapeDtypeStruct((1, D_MODEL), F32)],
        compiler_params=_params("arbitrary"),
        name="in_proj_bwd",
    )(x, g, dxm, dpa, dpf, dpb, wa, wf, wb)


def _wgrad_tokens(S, K):
    return min(S, 2048 if K <= 1024 else 1024)


def _wgrad_columns(n):
    return next(tn for tn in (1024, 768, 512, 256, 128) if n % tn == 0)


def wgrad(xa, dy, name):
    S, K = xa.shape
    N = dy.shape[1]
    ts = _wgrad_tokens(S, K)
    tn = _wgrad_columns(N)
    tk = _wgrad_columns(K)

    def body(x_ref, dy_ref, o_ref):
        @pl.when(pl.program_id(2) == 0)
        def _():
            o_ref[...] = jnp.zeros_like(o_ref)

        o_ref[...] += _dot_tn(x_ref[...], dy_ref[...])

    return pl.pallas_call(
        body,
        grid=(K // tk, N // tn, S // ts),
        in_specs=[pl.BlockSpec((ts, tk), lambda i, j, k: (k, i)), pl.BlockSpec((ts, tn), lambda i, j, k: (k, j))],
        out_specs=pl.BlockSpec((tk, tn), lambda i, j, k: (i, j)),
        out_shape=jax.ShapeDtypeStruct((K, N), F32),
        compiler_params=_params("parallel", "parallel", "arbitrary"),
        name=name,
    )(xa, dy)


def wgrad_into(xa, dy, name, buf, l):
    S, K = xa.shape
    N = dy.shape[1]
    ts = _wgrad_tokens(S, K)
    split = buf.ndim == 4
    tn = buf.shape[-1] if split else _wgrad_columns(N)
    place = _into_layer(buf)

    def body(l_ref, buf_ref, x_ref, dy_ref, o_ref):
        @pl.when(pl.program_id(1) == 0)
        def _():
            o_ref[...] = jnp.zeros_like(o_ref)

        o_ref[...] += _dot_tn(x_ref[...], dy_ref[...])

    if split:
        out_spec = pl.BlockSpec((None, None, K, tn), lambda j, k, l: (l[0], j, 0, 0))
    else:
        out_spec = pl.BlockSpec((None, K, tn), lambda j, k, l: (l[0], 0, j))
    return pl.pallas_call(
        body,
        grid_spec=pltpu.PrefetchScalarGridSpec(
            num_scalar_prefetch=1,
            grid=(N // tn, S // ts),
            in_specs=[place["in_spec"], pl.BlockSpec((ts, K), lambda j, k, l: (k, 0)),
                      pl.BlockSpec((ts, tn), lambda j, k, l: (k, j))],
            out_specs=out_spec,
        ),
        out_shape=place["out_shape"],
        input_output_aliases=place["aliases"],
        compiler_params=_params("parallel", "arbitrary"),
        name=name,
    )(_layer_index(l), buf, xa, dy)


def _head_mean_matrix():
    h = np.arange(D_ATTN) // HEAD_DIM
    return jnp.asarray((h[:, None] == h[None, :]).astype(np.float32) / HEAD_DIM, BF16)


def _place_matrix(lane0):
    m = np.zeros((N_PIECES * LANES, D_HEADS), np.float32)
    for i in range(N_PIECES):
        for h in range(N_HEADS):
            m[i * LANES + h, h * LANES + lane0 + i] = 1.0
    return jnp.asarray(m, BF16)


def _tri(n, upper):
    r = np.arange(n)
    m = (r[None, :] >= r[:, None]) if upper else (r[None, :] <= r[:, None])
    return jnp.asarray(m.astype(np.float32), BF16)


def _lanes_in(lane, lo, n):
    return (lane >= lo) & (lane < lo + n)


def qk_prep(pa, pf, gq, gk, fb):
    S = pa.shape[0]
    bd = _head_mean_matrix()
    tri = _tri(TM, upper=False)
    place_q = _place_matrix(LANE_C)
    place_k = _place_matrix(LANE_ONE)

    def body(q_ref, k_ref, v_ref, pf_ref, gq_ref, gk_ref, fb_ref, bd_ref, tri_ref, pq_ref, pk_ref,
             qx_ref, kx_ref, vx_ref, carry):
        @pl.when(pl.program_id(0) == 0)
        def _():
            carry[...] = jnp.zeros_like(carry)

        def head_norm(x_ref, g_ref, scale):
            xf = x_ref[...].astype(F32)
            ms = _dot((xf * xf).astype(BF16), bd_ref[...])
            return xf * lax.rsqrt(ms + EPS) * g_ref[...] * scale

        qh = head_norm(q_ref, gq_ref, HEAD_DIM ** -0.5 * LOG2E)
        kh = head_norm(k_ref, gk_ref, 1.0)
        vf = v_ref[...].astype(F32)

        z = pf_ref[...] + fb_ref[...]
        logf = jnp.minimum(z, 0.0) - jnp.log(1.0 + jnp.exp(-jnp.abs(z)))
        hi, lo = _split_bf16(logf)
        c = _dot(tri_ref[...], hi) + _dot(tri_ref[...], lo) + carry[...]
        carry[...] += jnp.sum(hi.astype(F32) + lo.astype(F32), axis=0, keepdims=True)
        pieces = jnp.concatenate(_pieces(c * LOG2E), axis=1)
        cq = _dot(pieces, pq_ref[...])
        ck = _dot(pieces, pk_ref[...])

        lane = lax.broadcasted_iota(jnp.int32, (TM, LANES), 1)
        low = lane < HEAD_DIM
        ones_q = _lanes_in(lane, LANE_ONE, N_PIECES).astype(F32)
        ones_k = (_lanes_in(lane, LANE_C, N_PIECES) | _lanes_in(lane, LANE_LSE, N_PIECES)).astype(F32)
        ones_v = _lanes_in(lane, LANE_C, N_PIECES + 1).astype(F32)
        for h in range(N_HEADS):
            blk = slice((h // 2) * LANES, (h // 2 + 1) * LANES)
            head = (lambda a: pltpu.roll(a[:, blk], HEAD_DIM, 1)) if h % 2 else (lambda a: a[:, blk])
            mine = slice(h * LANES, (h + 1) * LANES)
            qx_ref[h] = jnp.where(low, head(qh), cq[:, mine] + ones_q).astype(BF16)
            kx_ref[h] = jnp.where(low, head(kh), ones_k - ck[:, mine]).astype(BF16)
            vx_ref[h] = jnp.where(low, head(vf), ones_v).astype(BF16)

    heads = pl.BlockSpec((N_HEADS, TM, LANES), lambda i: (0, i, 0))
    out = jax.ShapeDtypeStruct((N_HEADS, S, LANES), BF16)
    return pl.pallas_call(
        body,
        grid=(S // TM,),
        in_specs=[pl.BlockSpec((TM, D_ATTN), lambda i: (i, 0)), pl.BlockSpec((TM, D_ATTN), lambda i: (i, 1)),
                  pl.BlockSpec((TM, D_ATTN), lambda i: (i, 2)),
                  _rows(TM, D_F), _whole(gq), _whole(gk), _whole(fb), _whole(bd), _whole(tri), _whole(place_q), _whole(place_k)],
        out_specs=[heads, heads, heads],
        out_shape=[out, out, out],
        scratch_shapes=[pltpu.VMEM((1, D_F), F32)],
        compiler_params=_params("arbitrary"),
        name="qk_prep",
    )(pa, pa, pa, pf, gq, gk, fb, bd, tri, place_q, place_k)


def attn_fwd(qx, kx, vx):
    S = qx.shape[1]
    nq = S // TQ

    def body(q_ref, k_ref, v_ref, o_ref, qb_ref):
        i = pl.program_id(1)
        lane = lax.broadcasted_iota(jnp.int32, (TQ, LANES), 1)
        q = [q_ref[0], q_ref[1]]

        def update(qh, m, acc, k, v, visible):
            z = _dot_nt(qh, k)
            if visible is not None:
                z = jnp.where(visible, z, NEG)
            m_new = jnp.maximum(m, jnp.max(z, axis=1, keepdims=True))
            pr = jnp.exp2(z - m_new)
            return m_new, jnp.exp2(m - m_new) * acc + _dot(pr.astype(BF16), v)

        def step(kt, carry, diagonal=False):
            ks = pl.multiple_of(kt * TQ, TQ)
            visible = None
            if diagonal:
                visible = lax.broadcasted_iota(jnp.int32, (TQ, TQ), 0) >= lax.broadcasted_iota(jnp.int32, (TQ, TQ), 1)
            return tuple(update(q[j], *carry[j], k_ref[j, pl.ds(ks, TQ), :], v_ref[j, pl.ds(ks, TQ), :], visible)
                         for j in range(2))

        init = tuple((jnp.full((TQ, 1), NEG, F32), jnp.zeros((TQ, LANES), F32)) for _ in range(2))
        carry = step(i, lax.fori_loop(0, i, step, init), diagonal=True)
        for j in range(2):
            m, acc = carry[j]
            l = jnp.sum(jnp.where(lane == LANE_C, acc, 0.0), axis=1, keepdims=True)
            o_ref[j] = acc / l
            n1, n2, n3 = _pieces(-(m + jnp.log(l) * LOG2E))
            qb_ref[j] = jnp.where(lane == LANE_LSE, n1,
                                  jnp.where(lane == LANE_LSE + 1, n2, jnp.where(lane == LANE_LSE + 2, n3, q[j])))

    pair_tile = pl.BlockSpec((2, TQ, LANES), lambda p, i: (p, i, 0))
    pair_all = pl.BlockSpec((2, S, LANES), lambda p, i: (p, 0, 0))
    return pl.pallas_call(
        body,
        grid=(N_HEADS // 2, nq),
        in_specs=[pair_tile, pair_all, pair_all],
        out_specs=[pair_tile, pair_tile],
        out_shape=[jax.ShapeDtypeStruct((N_HEADS, S, LANES), F32), jax.ShapeDtypeStruct((N_HEADS, S, LANES), BF16)],
        compiler_params=_params("parallel", "parallel"),
        name="attn_fwd",
    )(qx, kx, vx)


def attn_bwd(qxb, kx, vx, dox):
    S = qxb.shape[1]
    nq = S // TQ

    def body(q_ref, k_ref, v_ref, do_ref, dq_ref, dk_ref, dv_ref):
        kt = pl.program_id(1)

        @pl.when(kt == 0)
        def _():
            dq_ref[...] = jnp.zeros_like(dq_ref)

        k = [k_ref[0], k_ref[1]]
        v = [v_ref[0], v_ref[1]]

        def block(j, q0, nq_rows, k0, nk, diagonal):
            q = q_ref[j, pl.ds(q0, nq_rows), :]
            dout = do_ref[j, pl.ds(q0, nq_rows), :]
            kj, vj = k[j][k0:k0 + nk], v[j][k0:k0 + nk]
            z = _dot_nt(q, kj)
            if diagonal:
                visible = (lax.broadcasted_iota(jnp.int32, (nq_rows, nk), 0)
                           >= lax.broadcasted_iota(jnp.int32, (nq_rows, nk), 1))
                z = jnp.where(visible, z, NEG)
            pr = jnp.exp2(z)
            dv = _dot_tn(pr.astype(BF16), dout)
            dsb = (pr * _dot_nt(dout, vj)).astype(BF16)
            dq_ref[j, pl.ds(q0, nq_rows), :] += _dot(dsb, kj)
            return _dot_tn(dsb, q), dv

        def step(qi, carry):
            qs = pl.multiple_of(qi * TQ, TQ)
            new = []
            for j in range(2):
                dk, dv = block(j, qs, TQ, 0, TQ, False)
                new.append((carry[j][0] + dk, carry[j][1] + dv))
            return tuple(new)

        half = TQ // 2
        qs = pl.multiple_of(kt * TQ, TQ)
        carry = []
        for j in range(2):
            first = block(j, qs, TQ, 0, half, True)
            second = block(j, pl.multiple_of(qs + half, half), half, half, half, True)
            carry.append(tuple(jnp.concatenate(a, axis=0) for a in zip(first, second)))
        carry = lax.fori_loop(kt + 1, nq, step, tuple(carry))
        for j in range(2):
            dk_ref[j] = carry[j][0]
            dv_ref[j] = carry[j][1]

    pair_tile = pl.BlockSpec((2, TQ, LANES), lambda p, kt: (p, kt, 0))
    pair_all = pl.BlockSpec((2, S, LANES), lambda p, kt: (p, 0, 0))
    out = jax.ShapeDtypeStruct((N_HEADS, S, LANES), F32)
    return pl.pallas_call(
        body,
        grid=(N_HEADS // 2, nq),
        in_specs=[pair_all, pair_tile, pair_tile, pair_all],
        out_specs=[pair_all, pair_tile, pair_tile],
        out_shape=[out, out, out],
        compiler_params=_params("arbitrary", "arbitrary"),
        name="attn_bwd",
    )(qxb, kx, vx, dox)


def attn_bwd_post(pa, pf, dqx, dkx, dvx, gq, gk, fb):
    S = pa.shape[0]
    nt = S // TM
    bd = _head_mean_matrix()
    triu = _tri(TM, upper=True)
    rev = lambda i: nt - 1 - i

    def body(q_ref, k_ref, pf_ref, dqx_ref, dkx_ref, dvx_ref, gq_ref, gk_ref, fb_ref, bd_ref, triu_ref,
             dpa_ref, dpf_ref, dgq_ref, dgk_ref, dfb_ref, carry):
        @pl.when(pl.program_id(0) == 0)
        def _():
            carry[...] = jnp.zeros_like(carry)
            dgq_ref[...] = jnp.zeros_like(dgq_ref)
            dgk_ref[...] = jnp.zeros_like(dgk_ref)
            dfb_ref[...] = jnp.zeros_like(dfb_ref)

        lane = lax.broadcasted_iota(jnp.int32, (TM, LANES), 1)

        def columns(ref):
            return jnp.concatenate([jnp.where(lane < HEAD_DIM, ref[2 * p], pltpu.roll(ref[2 * p + 1], HEAD_DIM, 1))
                                    for p in range(N_HEADS // 2)], axis=1)

        def head_norm_bwd(x_ref, dy, g_ref, dg_ref):
            xf = x_ref[...].astype(F32)
            r = lax.rsqrt(_dot((xf * xf).astype(BF16), bd_ref[...]) + EPS)
            xhat = xf * r
            dg_ref[...] += jnp.sum(dy * xhat, axis=0, keepdims=True)
            gdy = dy * g_ref[...]
            return (r * (gdy - xhat * _dot((xhat * gdy).astype(BF16), bd_ref[...]))).astype(BF16)

        dpa_ref[:, 0:D_ATTN] = head_norm_bwd(q_ref, columns(dqx_ref) * HEAD_DIM ** -0.5, gq_ref, dgq_ref)
        dpa_ref[:, D_ATTN:2 * D_ATTN] = head_norm_bwd(k_ref, columns(dkx_ref) * LN2, gk_ref, dgk_ref)
        dpa_ref[:, 2 * D_ATTN:3 * D_ATTN] = columns(dvx_ref).astype(BF16)

        dc = jnp.zeros((TM, LANES), F32)
        for h in range(N_HEADS):
            both = jnp.where(lane == LANE_C, dqx_ref[h], 0.0) - jnp.where(lane == LANE_ONE, dkx_ref[h], 0.0)
            dc = jnp.where(lane == h, jnp.sum(both, axis=1, keepdims=True), dc)
        hi, lo = _split_bf16(dc)
        dlogf = _dot(triu_ref[...], hi) + _dot(triu_ref[...], lo) + carry[...]
        first = lax.broadcasted_iota(jnp.int32, (TM, D_F), 0) == 0
        carry[...] = jnp.sum(jnp.where(first, dlogf, 0.0), axis=0, keepdims=True)
        df = dlogf * _sigmoid(-(pf_ref[...] + fb_ref[...]))
        dfb_ref[...] += jnp.sum(df, axis=0, keepdims=True)
        dpf_ref[...] = df.astype(BF16)

    heads = pl.BlockSpec((N_HEADS, TM, LANES), lambda i: (0, rev(i), 0))
    return pl.pallas_call(
        body,
        grid=(nt,),
        in_specs=[pl.BlockSpec((TM, D_ATTN), lambda i: (rev(i), 0)), pl.BlockSpec((TM, D_ATTN), lambda i: (rev(i), 1)),
                  pl.BlockSpec((TM, D_F), lambda i: (rev(i), 0)), heads, heads, heads,
                  _whole(gq), _whole(gk), _whole(fb), _whole(bd), _whole(triu)],
        out_specs=[pl.BlockSpec((TM, D_QKV), lambda i: (rev(i), 0)), pl.BlockSpec((TM, D_F), lambda i: (rev(i), 0)),
                   pl.BlockSpec((1, D_ATTN), lambda i: (0, 0)), pl.BlockSpec((1, D_ATTN), lambda i: (0, 0)),
                   pl.BlockSpec((1, D_F), lambda i: (0, 0))],
        out_shape=[jax.ShapeDtypeStruct((S, D_QKV), BF16), jax.ShapeDtypeStruct((S, D_F), BF16),
                   jax.ShapeDtypeStruct((1, D_ATTN), F32), jax.ShapeDtypeStruct((1, D_ATTN), F32),
                   jax.ShapeDtypeStruct((1, D_F), F32)],
        scratch_shapes=[pltpu.VMEM((1, D_F), F32)],
        compiler_params=_params("arbitrary"),
        name="attn_bwd_post",
    )(pa, pa, pf, dqx, dkx, dvx, gq, gk, fb, bd, triu)


def _shift_down(ext, k):
    return pltpu.roll(ext, k, 0)[HALO:]


def _shift_up(ext, k, n):
    return pltpu.roll(ext, n + HALO - k, 0)[:n]


def _pool_lane_select(a2, a4, a8, a16, lane):
    return jnp.where(lane < 64, a2, jnp.where(lane < 128, a4, jnp.where(lane < 192, a8, a16)))


def _local_branches(o, pb, halo, have_prev, row0, wao, convw, wco, wpool, pscale):
    n = pb.shape[0]
    cx = pb[:, 0:D_CONV].astype(F32)
    cb = pb[:, D_CONV:2 * D_CONV].astype(F32)
    cc = pb[:, 2 * D_CONV:3 * D_CONV].astype(F32)
    px = pb[:, 3 * D_CONV:D_LOCAL].astype(F32)
    keep = have_prev.astype(F32)
    z = cc * cx
    z_ext = jnp.concatenate([halo[:, 2 * D_CONV:3 * D_CONV].astype(F32) * halo[:, 0:D_CONV].astype(F32) * keep, z], axis=0)
    z1 = _shift_down(z_ext, 1)
    z2 = _shift_down(z_ext, 2)
    conv = convw[0:1, :] * z2 + convw[1:2, :] * z1 + convw[2:3, :] * z
    cm = cb * conv

    u_ext = jnp.concatenate([halo[:, 3 * D_CONV:D_LOCAL].astype(F32) * keep, px], axis=0)
    s2 = u_ext + pltpu.roll(u_ext, 1, 0)
    s4 = s2 + pltpu.roll(s2, 2, 0)
    s8 = s4 + pltpu.roll(s4, 4, 0)
    s16 = s8 + pltpu.roll(s8, 8, 0)
    lane = lax.broadcasted_iota(jnp.int32, (n, D_POOL), 1)
    win = _pool_lane_select(2.0, 4.0, 8.0, 16.0, lane)
    t = (row0 + lax.broadcasted_iota(jnp.int32, (n, D_POOL), 0)).astype(F32)
    cnt = jnp.minimum(t + 1.0, win)
    feat = _pool_lane_select(s2[HALO:], s4[HALO:], s8[HALO:], s16[HALO:], lane) / cnt - px

    ya = _dot(o, wao)
    yc = _dot(cm.astype(BF16), wco)
    yp_pre = _dot(feat.astype(BF16), wpool)
    yp = yp_pre * pscale
    return dict(cx=cx, cb=cb, cc=cc, z=z, z1=z1, z2=z2, conv=conv, cm=cm, feat=feat, cnt=cnt, lane=lane,
                ya=ya, yc=yc, yp_pre=yp_pre, yp=yp)


def _halo_spec(tm, tile_of):
    per = tm // HALO
    return pl.BlockSpec((HALO, D_LOCAL), lambda i, *_: (jnp.maximum(tile_of(i) * per - 1, 0), 0))


def _heads_as_columns(ref):
    lane = lax.broadcasted_iota(jnp.int32, ref.shape[1:], 1)
    return jnp.concatenate([jnp.where(lane < HEAD_DIM, ref[2 * p], pltpu.roll(ref[2 * p + 1], HEAD_DIM, 1))
                            for p in range(N_HEADS // 2)], axis=1)


def mix_out_fwd(x, ox, pb, wao, convw, wco, wpool, pscale, wo_all, l):
    S = x.shape[0]
    tm = TM_MIX

    def body(l_ref, x_ref, o_ref, pb_ref, halo_ref, wao_ref, cw_ref, wco_ref, wp_ref, ps_ref, wo_ref, y_ref):
        i = pl.program_id(0)
        pb = pb_ref[...]
        o = _heads_as_columns(o_ref).astype(BF16)
        b = _local_branches(o, pb, halo_ref[...], i > 0, i * tm, wao_ref[...], cw_ref[...], wco_ref[...],
                            wp_ref[...], ps_ref[...])
        g0 = _sigmoid(pb[:, D_LOCAL:D_LOCAL + D_MODEL].astype(F32))
        g1 = _sigmoid(pb[:, D_LOCAL + D_MODEL:D_LOCAL + 2 * D_MODEL].astype(F32))
        g2 = _sigmoid(pb[:, D_LOCAL + 2 * D_MODEL:D_B].astype(F32))
        merged = g0 * b["ya"] + g1 * b["yc"] + g2 * b["yp"]
        y_ref[...] = x_ref[...] + _dot(merged.astype(BF16), wo_ref[...])

    return pl.pallas_call(
        body,
        grid_spec=pltpu.PrefetchScalarGridSpec(
            num_scalar_prefetch=1,
            grid=(S // tm,),
            in_specs=[_rows(tm, D_MODEL), pl.BlockSpec((N_HEADS, tm, LANES), lambda i, l: (0, i, 0)), _rows(tm, D_B),
                      _halo_spec(tm, lambda i: i), _whole(wao), _whole(convw), _whole(wco), _whole(wpool), _whole(pscale),
                      _layer((D_MODEL, D_MODEL))],
            out_specs=_rows(tm, D_MODEL),
        ),
        out_shape=jax.ShapeDtypeStruct((S, D_MODEL), F32),
        compiler_params=_params("parallel"),
        name="mix_out_fwd",
    )(_layer_index(l), x, ox, pb, pb, wao, convw, wco, wpool, pscale, wo_all)


def mix_out_bwd(dxm, ox, pb, wao, convw, wco, wpool, pscale, wo_all, l):
    S = dxm.shape[0]
    tm = TM_MIX
    nt = S // tm
    rev = lambda i: nt - 1 - i
    rows = lambda n: pl.BlockSpec((tm, n), lambda i, l: (rev(i), 0))
    heads = pl.BlockSpec((N_HEADS, tm, LANES), lambda i, l: (0, rev(i), 0))
    acc = lambda r, c: pl.BlockSpec((r, c), lambda i, l: (0, 0))

    def body(l_ref, dxm_ref, o_ref, pb_ref, halo_ref, wao_ref, cw_ref, wco_ref, wp_ref, ps_ref, wo_ref,
             dpb_ref, do_ref, dwo_ref, dwao_ref, dwco_ref, dwp_ref, dcw_ref, dps_ref, next_dconv, next_e):
        i = pl.program_id(0)
        r = rev(i)

        @pl.when(i == 0)
        def _():
            for ref in (dwo_ref, dwao_ref, dwco_ref, dwp_ref, dcw_ref, dps_ref, next_dconv, next_e):
                ref[...] = jnp.zeros_like(ref)

        pb = pb_ref[...]
        o = _heads_as_columns(o_ref).astype(BF16)
        cw = cw_ref[...]
        b = _local_branches(o, pb, halo_ref[...], r > 0, r * tm, wao_ref[...], cw, wco_ref[...], wp_ref[...], ps_ref[...])
        g0 = _sigmoid(pb[:, D_LOCAL:D_LOCAL + D_MODEL].astype(F32))
        g1 = _sigmoid(pb[:, D_LOCAL + D_MODEL:D_LOCAL + 2 * D_MODEL].astype(F32))
        g2 = _sigmoid(pb[:, D_LOCAL + 2 * D_MODEL:D_B].astype(F32))
        dxb = dxm_ref[...].astype(BF16)
        merged = g0 * b["ya"] + g1 * b["yc"] + g2 * b["yp"]
        dwo_ref[...] += _dot_tn(merged.astype(BF16), dxb)
        dmer = _dot_nt(dxb, wo_ref[...])
        dpb_ref[:, D_LOCAL:D_LOCAL + D_MODEL] = (dmer * b["ya"] * (g0 * (1.0 - g0))).astype(BF16)
        dpb_ref[:, D_LOCAL + D_MODEL:D_LOCAL + 2 * D_MODEL] = (dmer * b["yc"] * (g1 * (1.0 - g1))).astype(BF16)
        dpb_ref[:, D_LOCAL + 2 * D_MODEL:D_B] = (dmer * b["yp"] * (g2 * (1.0 - g2))).astype(BF16)

        dya = (dmer * g0).astype(BF16)
        dwao_ref[...] += _dot_tn(o, dya)
        da = _dot_nt(dya, wao_ref[...])
        lane = lax.broadcasted_iota(jnp.int32, (tm, LANES), 1)
        for h in range(N_HEADS):
            two = da[:, (h // 2) * LANES:(h // 2 + 1) * LANES]
            dah = jnp.where(lane < HEAD_DIM, pltpu.roll(two, HEAD_DIM, 1) if h % 2 else two, 0.0).astype(BF16)
            d1, d2, d3 = _pieces(-jnp.sum(dah.astype(F32) * o_ref[h], axis=1, keepdims=True))
            do_ref[h] = jnp.where(lane == LANE_C + 1, d1, jnp.where(lane == LANE_C + 2, d2,
                                                                  jnp.where(lane == LANE_C + 3, d3, dah)))

        dyc = (dmer * g1).astype(BF16)
        dwco_ref[...] += _dot_tn(b["cm"].astype(BF16), dyc)
        dcm = _dot_nt(dyc, wco_ref[...])
        dconv = dcm * b["cb"]
        dcw_ref[0:1, :] += jnp.sum(dconv * b["z2"], axis=0, keepdims=True)
        dcw_ref[1:2, :] += jnp.sum(dconv * b["z1"], axis=0, keepdims=True)
        dcw_ref[2:3, :] += jnp.sum(dconv * b["z"], axis=0, keepdims=True)
        d_ext = jnp.concatenate([dconv, next_dconv[...]], axis=0)
        dz = cw[2:3, :] * dconv + cw[1:2, :] * _shift_up(d_ext, 1, tm) + cw[0:1, :] * _shift_up(d_ext, 2, tm)
        next_dconv[...] = dconv[0:HALO]
        dpb_ref[:, 0:D_CONV] = (dz * b["cc"]).astype(BF16)
        dpb_ref[:, D_CONV:2 * D_CONV] = (dcm * b["conv"]).astype(BF16)
        dpb_ref[:, 2 * D_CONV:3 * D_CONV] = (dz * b["cx"]).astype(BF16)

        dyp = dmer * g2
        dps_ref[...] += jnp.sum(dyp * b["yp_pre"], axis=0, keepdims=True)
        dyps = (dyp * ps_ref[...]).astype(BF16)
        dwp_ref[...] += _dot_tn(b["feat"].astype(BF16), dyps)
        dfeat = _dot_nt(dyps, wp_ref[...])
        e = dfeat / b["cnt"]
        e_ext = jnp.concatenate([e, next_e[...]], axis=0)
        up = lambda a, k: pltpu.roll(a, tm + HALO - k, 0)
        f2 = e_ext + up(e_ext, 1)
        f4 = f2 + up(f2, 2)
        f8 = f4 + up(f4, 4)
        f16 = f8 + up(f8, 8)
        next_e[...] = e[0:HALO]
        dpb_ref[:, 3 * D_CONV:D_LOCAL] = (_pool_lane_select(f2[:tm], f4[:tm], f8[:tm], f16[:tm], b["lane"]) - dfeat).astype(BF16)

    return pl.pallas_call(
        body,
        grid_spec=pltpu.PrefetchScalarGridSpec(
            num_scalar_prefetch=1,
            grid=(nt,),
            in_specs=[rows(D_MODEL), heads, rows(D_B), _halo_spec(tm, rev),
                      _whole(wao), _whole(convw), _whole(wco), _whole(wpool), _whole(pscale), _layer((D_MODEL, D_MODEL))],
            out_specs=[rows(D_B), heads, acc(D_MODEL, D_MODEL), acc(D_ATTN, D_MODEL), acc(D_CONV, D_MODEL),
                       acc(D_POOL, D_MODEL), acc(8, D_CONV), acc(1, D_MODEL)],
            scratch_shapes=[pltpu.VMEM((HALO, D_CONV), F32), pltpu.VMEM((HALO, D_POOL), F32)],
        ),
        out_shape=[jax.ShapeDtypeStruct((S, D_B), BF16), jax.ShapeDtypeStruct((N_HEADS, S, LANES), BF16),
                   jax.ShapeDtypeStruct((D_MODEL, D_MODEL), F32), jax.ShapeDtypeStruct((D_ATTN, D_MODEL), F32),
                   jax.ShapeDtypeStruct((D_CONV, D_MODEL), F32), jax.ShapeDtypeStruct((D_POOL, D_MODEL), F32),
                   jax.ShapeDtypeStruct((8, D_CONV), F32), jax.ShapeDtypeStruct((1, D_MODEL), F32)],
        compiler_params=_params("arbitrary"),
        name="mix_out_bwd",
    )(_layer_index(l), dxm, ox, pb, pb, wao, convw, wco, wpool, pscale, wo_all)


FF_SHARD = 2 * D_FF // 4


def ffn_fwd(x, g, w1_all, w2_all, l):
    S = x.shape[0]

    def body(l_ref, x_ref, g_ref, w1_ref, w2_ref, y_ref, u_ref):
        xf = x_ref[...]
        r = lax.rsqrt(jnp.mean(xf * xf, axis=-1, keepdims=True) + EPS)
        h = (xf * r * g_ref[...]).astype(BF16)
        u = jnp.concatenate([_dot(h, w1_ref[j]) for j in range(4)], axis=1)
        u_ref[...] = u.astype(BF16)
        gt = u[:, 0:D_FF]
        act = gt * _sigmoid(gt) * u[:, D_FF:2 * D_FF]
        y_ref[...] = xf + _dot(act.astype(BF16), w2_ref[...])

    return pl.pallas_call(
        body,
        grid_spec=pltpu.PrefetchScalarGridSpec(
            num_scalar_prefetch=1,
            grid=(S // TM,),
            in_specs=[_rows(TM, D_MODEL), _whole(g), _layer((4, D_MODEL, FF_SHARD)), _layer((D_FF, D_MODEL))],
            out_specs=[_rows(TM, D_MODEL), _rows(TM, 2 * D_FF)],
        ),
        out_shape=[jax.ShapeDtypeStruct((S, D_MODEL), F32), jax.ShapeDtypeStruct((S, 2 * D_FF), BF16)],
        compiler_params=_params("parallel"),
        name="ffn_fwd",
    )(_layer_index(l), x, g, w1_all, w2_all)


def ffn_bwd(x, g, dy, u, w1_all, w2_all, l):
    S = x.shape[0]

    def body(l_ref, x_ref, g_ref, dy_ref, u_ref, w1_ref, w2_ref, dx_ref, du_ref, act_ref, h_ref, dyb_ref, dg_ref):
        @pl.when(pl.program_id(0) == 0)
        def _():
            dg_ref[...] = jnp.zeros_like(dg_ref)

        dyf = dy_ref[...]
        dyb_ref[...] = dyf.astype(BF16)
        dact = _dot_nt(dyb_ref[...], w2_ref[...])
        gt = u_ref[:, 0:D_FF].astype(F32)
        up = u_ref[:, D_FF:2 * D_FF].astype(F32)
        sg = _sigmoid(gt)
        silu = gt * sg
        act_ref[...] = (silu * up).astype(BF16)
        du_ref[:, 0:D_FF] = (dact * up * (sg * (1.0 + gt * (1.0 - sg)))).astype(BF16)
        du_ref[:, D_FF:2 * D_FF] = (dact * silu).astype(BF16)
        dh = _dot_nt(du_ref[:, 0:FF_SHARD], w1_ref[0])
        for j in range(1, 4):
            dh = dh + _dot_nt(du_ref[:, j * FF_SHARD:(j + 1) * FF_SHARD], w1_ref[j])
        xf = x_ref[...]
        r = lax.rsqrt(jnp.mean(xf * xf, axis=-1, keepdims=True) + EPS)
        xhat = xf * r
        h_ref[...] = (xhat * g_ref[...]).astype(BF16)
        dg_ref[...] += jnp.sum(dh * xhat, axis=0, keepdims=True)
        gdh = dh * g_ref[...]
        dx_ref[...] = dyf + r * (gdh - xhat * jnp.mean(xhat * gdh, axis=-1, keepdims=True))

    return pl.pallas_call(
        body,
        grid_spec=pltpu.PrefetchScalarGridSpec(
            num_scalar_prefetch=1,
            grid=(S // TM,),
            in_specs=[_rows(TM, D_MODEL), _whole(g), _rows(TM, D_MODEL), _rows(TM, 2 * D_FF),
                      _layer((4, D_MODEL, FF_SHARD)), _layer((D_FF, D_MODEL))],
            out_specs=[_rows(TM, D_MODEL), _rows(TM, 2 * D_FF), _rows(TM, D_FF), _rows(TM, D_MODEL), _rows(TM, D_MODEL),
                       pl.BlockSpec((1, D_MODEL), lambda i, l: (0, 0))],
        ),
        out_shape=[jax.ShapeDtypeStruct((S, D_MODEL), F32), jax.ShapeDtypeStruct((S, 2 * D_FF), BF16),
                   jax.ShapeDtypeStruct((S, D_FF), BF16), jax.ShapeDtypeStruct((S, D_MODEL), BF16),
                   jax.ShapeDtypeStruct((S, D_MODEL), BF16), jax.ShapeDtypeStruct((1, D_MODEL), F32)],
        compiler_params=_params("arbitrary"),
        name="ffn_bwd",
    )(_layer_index(l), x, g, dy, u, w1_all, w2_all)


def loss_head(y, target):
    S = y.shape[0]

    def body(y_ref, t_ref, loss_ref, dy_ref):
        @pl.when(pl.program_id(0) == 0)
        def _():
            loss_ref[0, 0] = 0.0

        err = y_ref[...] - t_ref[...]
        dy_ref[...] = err * (1.0 / D_MODEL)
        loss_ref[0, 0] += 0.5 * jnp.sum(jnp.mean(err * err, axis=-1))

    return pl.pallas_call(
        body,
        grid=(S // TM,),
        in_specs=[_rows(TM, D_MODEL), _rows(TM, D_MODEL)],
        out_specs=[pl.BlockSpec((1, 1), lambda i: (0, 0), memory_space=pltpu.SMEM), _rows(TM, D_MODEL)],
        out_shape=[jax.ShapeDtypeStruct((1, 1), F32), jax.ShapeDtypeStruct((S, D_MODEL), F32)],
        compiler_params=_params("arbitrary"),
        name="loss_head",
    )(y, target)


SPLIT = {
    "w_in": ((D_MODEL, D_IN), 1),
    "w_attn_out": ((D_ATTN, D_MODEL), 1),
    "w_conv_out": ((D_CONV, D_MODEL), 1),
    "pool_w": ((N_GROUPS, D_POOL // N_GROUPS, D_MODEL // N_GROUPS), 2),
    "w_o": ((D_MODEL, D_MODEL), 0),
    "w_ffn_in": ((D_MODEL, 2 * D_FF), 1),
    "w_ffn_out": ((D_FF, D_MODEL), 0),
}
SMALL = {"norm_mix_g": D_MODEL, "forget_b": N_HEADS, "q_norm_g": HEAD_DIM, "k_norm_g": HEAD_DIM,
         "pool_scale": D_MODEL, "norm_ffn_g": D_MODEL, "conv_w": CONV_K * D_CONV}
WEIGHTS = ["norm_mix_g", "w_in", "forget_b", "q_norm_g", "k_norm_g", "w_attn_out", "conv_w", "w_conv_out", "pool_w",
           "pool_scale", "w_o", "norm_ffn_g", "w_ffn_in", "w_ffn_out"]


MIX_W = ["w_in", "w_attn_out", "w_conv_out", "pool_w", "w_o"]
FFN_W = ["w_ffn_in", "w_ffn_out"]


def _row(a):
    return a.astype(F32).reshape(1, -1)


def mix_fwd(x, full, li, small):
    n = full["w_o"].shape[0]
    wa, wf, wb = w_in_prep(full["w_in"], li)
    wao, wco, wpool, convw = branch_w_prep(full["w_attn_out"], full["w_conv_out"], full["pool_w"], full["conv_w"], li)
    w = dict(wa=wa, wf=wf, wb=wb, wao=wao, wco=wco, wpool=wpool, convw=convw,
             w_o=full["w_o"].reshape(n, D_MODEL, D_MODEL), at=li,
             g1=_row(small["norm_mix_g"]), pscale=_row(small["pool_scale"]),
             gq=_row(jnp.tile(small["q_norm_g"], N_HEADS)), gk=_row(jnp.tile(small["k_norm_g"], N_HEADS)),
             fb=_row(jnp.pad(small["forget_b"], (0, D_F - N_HEADS))))
    h, pa, pf, pb = in_proj_fwd(x, w["g1"], wa, wf, wb)
    qx, kx, vx = qk_prep(pa, pf, w["gq"], w["gk"], w["fb"])
    ox, qxb = attn_fwd(qx, kx, vx)
    xm = mix_out_fwd(x, ox, pb, wao, convw, wco, wpool, w["pscale"], w["w_o"], li)
    return xm, w, dict(x=x, h=h, pa=pa, pf=pf, pb=pb, kx=kx, vx=vx, ox=ox, qxb=qxb)


def ffn_half_fwd(xm, full, li, g2):
    n = full["w_ffn_out"].shape[0]
    w = dict(w1=full["w_ffn_in"], w2=full["w_ffn_out"].reshape(n, D_FF, D_MODEL), at=li, g2=_row(g2))
    y, u = ffn_fwd(xm, w["g2"], w["w1"], w["w2"], li)
    return y, w, dict(xm=xm, u=u)


def ffn_half_bwd(dx, w, s, gi, big):
    n = big["w_ffn_out"].shape[0]
    big = dict(big)
    dxm, du, act, h2, dyb, dg2 = ffn_bwd(s["xm"], w["g2"], dx, s["u"], w["w1"], w["w2"], w["at"])
    big["w_ffn_out"] = wgrad_into(act, dyb, "wgrad_ffn_out", big["w_ffn_out"].reshape(n, D_FF, D_MODEL),
                                  gi).reshape(big["w_ffn_out"].shape)
    big["w_ffn_in"] = wgrad_into(h2, du, "wgrad_ffn_in", big["w_ffn_in"], gi)
    return dxm, big, dict(norm_ffn_g=dg2[0])


def mix_bwd_weights(dxm, w, s, gi, big):
    big = dict(big)
    dpb, dox, dwo, dwao, dwco, dwpool, dconvw, dpscale = mix_out_bwd(
        dxm, s["ox"], s["pb"], w["wao"], w["convw"], w["wco"], w["wpool"], w["pscale"], w["w_o"], w["at"])
    dqx, dkx, dvx = attn_bwd(s["qxb"], s["kx"], s["vx"], dox)
    dpa, dpf, dgq, dgk, dfb = attn_bwd_post(s["pa"], s["pf"], dqx, dkx, dvx, w["gq"], w["gk"], w["fb"])
    big["w_in"] = w_in_unprep(wgrad(s["h"], dpa, "wgrad_in_qkv"), wgrad(s["h"], dpf, "wgrad_in_f"),
                              wgrad(s["h"], dpb, "wgrad_in_b"), big["w_in"], gi)
    big["w_attn_out"], big["w_conv_out"], big["pool_w"], big["w_o"] = branch_g_place(
        dwao, dwco, dwpool, dwo, (big["w_attn_out"], big["w_conv_out"], big["pool_w"], big["w_o"]), gi)
    sm = dict(forget_b=dfb[0, 0:N_HEADS], q_norm_g=dgq.reshape(N_HEADS, HEAD_DIM).sum(0),
              k_norm_g=dgk.reshape(N_HEADS, HEAD_DIM).sum(0), pool_scale=dpscale[0],
              conv_w=dconvw[0:CONV_K].reshape(-1))
    return (dpa, dpf, dpb), big, sm


def mix_bwd_input(dproj, dxm, w, s):
    dx, dg1 = in_proj_bwd(s["x"], w["g1"], dxm, *dproj, w["wa"], w["wf"], w["wb"])
    return dx, dict(norm_mix_g=dg1[0])


def mix_bwd(dxm, w, s, gi, big):
    dproj, big, sm = mix_bwd_weights(dxm, w, s, gi, big)
    dx, g1 = mix_bwd_input(dproj, dxm, w, s)
    return dx, big, {**sm, **g1}


def grad_buffers(full):
    return {n: lax.empty(full[n].shape, F32) for n in SPLIT}


def local_step(x, target, gathered, small):
    n_layers = small["norm_mix_g"].shape[0]
    done = []
    for l in range(n_layers):
        xm, wm, sm = mix_fwd(x, gathered, l, {n: v[l] for n, v in small.items()})
        x, wf, sf = ffn_half_fwd(xm, gathered, l, small["norm_ffn_g"][l])
        done.append((wm, sm, wf, sf))
    loss, dx = loss_head(x, target)
    big = grad_buffers(gathered)
    small_grads = [None] * n_layers
    for l in reversed(range(n_layers)):
        wm, sm, wf, sf = done[l]
        dxm, ffn_big, g_ffn = ffn_half_bwd(dx, wf, sf, l, {n: big[n] for n in FFN_W})
        dx, mix_big, g_mix = mix_bwd(dxm, wm, sm, l, {n: big[n] for n in MIX_W})
        big = {**ffn_big, **mix_big}
        small_grads[l] = {**g_ffn, **g_mix}
    return loss, dx, big, {n: jnp.stack([g[n] for g in small_grads]) for n in SMALL}


def adamw(w, g, m, v, name):
    R, C = w.shape
    tm = 256 if R % 256 == 0 else R

    def body(w_ref, g_ref, m_ref, v_ref, d_ref, nm_ref, nv_ref):
        gr = g_ref[...]
        m_new = ADAM_B1 * m_ref[...] + (1.0 - ADAM_B1) * gr
        v_new = ADAM_B2 * v_ref[...] + (1.0 - ADAM_B2) * jnp.square(gr)
        nm_ref[...] = m_new
        nv_ref[...] = v_new
        m_hat = m_new / (1.0 - ADAM_B1 ** ADAM_STEP)
        v_hat = v_new / (1.0 - ADAM_B2 ** ADAM_STEP)
        d_ref[...] = -ADAM_LR * (m_hat / (jnp.sqrt(v_hat) + ADAM_EPS) + ADAM_WD * w_ref[...])

    spec = _rows(tm, C)
    out = jax.ShapeDtypeStruct((R, C), F32)
    return pl.pallas_call(
        body,
        grid=(R // tm,),
        in_specs=[spec] * 4,
        out_specs=[spec] * 3,
        out_shape=[out] * 3,
        compiler_params=_params("parallel"),
        name=name,
    )(w, g, m, v)


MESH = pl.DeviceIdType.MESH
HBM_REF = pl.BlockSpec(memory_space=pl.ANY)
N_CHIPS = 4
N_DEV = 8
SMALL_SHAPE = (128, LANES)


def _mesh_pos():
    return lax.axis_index("x"), lax.axis_index("y"), lax.axis_index("c")


def _other_chips(x, y):
    return [(1 - x, y), (x, 1 - y), (1 - x, 1 - y)]


def _remote(src, dst, send_sem, recv_sem, to):
    return pltpu.make_async_remote_copy(src_ref=src, dst_ref=dst, send_sem=send_sem, recv_sem=recv_sem,
                                        device_id=to, device_id_type=MESH)


def _row_tile(rows):
    for tm in (256, 176, 128):
        if rows % tm == 0:
            return tm
    return rows


def _as4(a):
    return a.reshape(a.shape[0], a.shape[1], -1, a.shape[-1])


def _by_columns(shape):
    return (shape[-2] // 2) % 8 != 0


def _core_rows(buf, c):
    R, C = buf.shape[-2:]
    if R % 2:
        whole = (pl.ds(0, R), pl.ds(0, C))
        return whole, whole, False
    if _by_columns(buf.shape):
        return (pl.ds(0, R), pl.ds(c * (C // 2), C // 2)), (pl.ds(0, R), pl.ds((1 - c) * (C // 2), C // 2)), True
    return (pl.ds(c * (R // 2), R // 2), pl.ds(0, C)), (pl.ds((1 - c) * (R // 2), R // 2), pl.ds(0, C)), True


def _half_shape(g):
    R, C = g.shape[-2:]
    return g.shape[:-2] + ((R, C // 2) if _by_columns(g.shape) else (R // 2, C))


def place_shard(chip, w, lo, hi, dtype, name, after=None):
    w3 = w.reshape(w.shape[0], -1, w.shape[-1])
    _, R, C = w3.shape
    tm = _row_tile(R)
    idle = [] if after is None else [after]

    def body(chip_ref, w_ref, *rest):
        rest[-1][...] = w_ref[...].astype(dtype)

    return pl.pallas_call(
        body,
        grid_spec=pltpu.PrefetchScalarGridSpec(
            num_scalar_prefetch=1,
            grid=(hi - lo, R // tm),
            in_specs=[pl.BlockSpec((None, tm, C), lambda l, i, chip: (lo + l, i, 0))] + [_whole(a) for a in idle],
            out_specs=pl.BlockSpec((None, None, tm, C), lambda l, i, chip: (l, chip[0], i, 0)),
        ),
        out_shape=jax.ShapeDtypeStruct((hi - lo, N_CHIPS, R, C), dtype),
        compiler_params=_params("parallel", "parallel"),
        name=name,
    )(chip, w3, *idle)


HBM_SPACE = pl.BlockSpec(memory_space=pltpu.HBM)
SEM_SPACE = pl.BlockSpec(memory_space=pltpu.SEMAPHORE)
IN_FLIGHT = pltpu.SideEffectType.DATAFLOW_SIDE_EFFECTING


def _sem_table(send_sems, recv_sems):
    return lambda t, j: (send_sems.at[t, j], recv_sems.at[t, j])


def _sem_per_peer(send_sems, recv_sems):
    return lambda t, j: (send_sems[j], recv_sems[j])


def _gather_ici(bufs, sem, sends=True, lands=True):
    x, y, c = _mesh_pos()
    me = 2 * x + y
    out, into = [], []
    for t, buf in enumerate(bufs):
        mine, _, _ = _core_rows(buf, c)
        part = lambda k, buf=buf, mine=mine: buf.at[pl.ds(0, buf.shape[0]), k, *mine]
        for j, (px, py) in enumerate(_other_chips(x, y)):
            if sends:
                out.append(_remote(part(me), part(me), *sem(t, j), (px, py, c)))
            if lands:
                into.append(_remote(part(2 * px + py), part(2 * px + py), *sem(t, j), (px, py, c)))
    return out, into


def _gather_d2d(bufs, send_sems, recv_sems, first):
    x, y, c = _mesh_pos()
    sends, lands = [], []
    for t, buf in enumerate(bufs):
        mine, theirs, split = _core_rows(buf, c)
        if not split:
            continue
        for j, (px, py) in enumerate(_other_chips(x, y)):
            part = lambda half, buf=buf, k=2 * px + py: buf.at[pl.ds(0, buf.shape[0]), k, *half]
            sems = (send_sems.at[t, first + j], recv_sems.at[t, first + j], (x, y, 1 - c))
            sends.append(_remote(part(mine), part(mine), *sems))
            lands.append(_remote(part(theirs), part(theirs), *sems))
    return sends, lands


def gather_shards(bufs):
    n = len(bufs)

    def body(*refs):
        outs = refs[n:2 * n]
        send_sems, recv_sems = refs[2 * n:]
        ici_out, ici_in = _gather_ici(outs, _sem_table(send_sems, recv_sems))
        d2d_out, d2d_in = _gather_d2d(outs, send_sems, recv_sems, 3)
        for cp in ici_out:
            cp.start()
        for cp in ici_in:
            cp.wait_recv()
        for cp in d2d_out:
            cp.start()
        for cp in d2d_in:
            cp.wait_recv()
        for cp in ici_out + d2d_out:
            cp.wait_send()

    return pl.pallas_call(
        body,
        in_specs=[HBM_REF] * n,
        out_specs=[HBM_REF] * n,
        out_shape=[jax.ShapeDtypeStruct(b.shape, b.dtype) for b in bufs],
        input_output_aliases={t: t for t in range(n)},
        scratch_shapes=[pltpu.SemaphoreType.DMA((n, 6)), pltpu.SemaphoreType.DMA((n, 6))],
        name="gather_shards",
    )(*bufs)


def gather_start(bufs):
    n = len(bufs)

    def body(*refs):
        send_sems, recv_sems = refs[n:n + 3], refs[n + 3:n + 6]
        outs = refs[n + 6:2 * n + 6]
        token = refs[2 * n + 6]
        for cp in _gather_ici(outs, _sem_per_peer(send_sems, recv_sems), lands=False)[0]:
            cp.start()
        token[...] = jnp.zeros_like(token)

    res = pl.pallas_call(
        body,
        in_specs=[HBM_SPACE] * n,
        out_specs=[SEM_SPACE] * 6 + [HBM_SPACE] * n + [pl.BlockSpec(memory_space=pltpu.VMEM)],
        out_shape=[pltpu.SemaphoreType.DMA(())] * 6
        + [pltpu.HBM(b.shape, b.dtype) for b in bufs] + [jax.ShapeDtypeStruct((8, LANES), F32)],
        input_output_aliases={t: t + 6 for t in range(n)},
        compiler_params=pltpu.CompilerParams(has_side_effects=IN_FLIGHT),
        name="gather_start",
    )(*[pltpu.with_memory_space_constraint(b, pltpu.HBM) for b in bufs])
    return list(res[0:3]), list(res[3:6]), list(res[6:n + 6]), res[n + 6]


def gather_wait(send_sems, recv_sems, bufs, after):
    n = len(bufs)

    def body(*refs):
        ins_send, ins_recv = refs[n:n + 3], refs[n + 3:n + 6]
        outs = refs[n + 7:]
        sends, lands = _gather_ici(outs, _sem_per_peer(ins_send, ins_recv))
        for cp in lands:
            cp.wait_recv()
        for cp in sends:
            cp.wait_send()

    return list(pl.pallas_call(
        body,
        in_specs=[HBM_SPACE] * n + [SEM_SPACE] * 6 + [HBM_REF],
        out_specs=[HBM_SPACE] * n,
        out_shape=[pltpu.HBM(b.shape, b.dtype) for b in bufs],
        input_output_aliases={t: t for t in range(n)},
        compiler_params=pltpu.CompilerParams(has_side_effects=IN_FLIGHT),
        name="gather_wait",
    )(*bufs, *send_sems, *recv_sems, after))


def gather_forward(bufs):
    n = len(bufs)

    def body(*refs):
        outs = refs[n:2 * n]
        send_sems, recv_sems = refs[2 * n:]
        sends, lands = _gather_d2d(outs, send_sems, recv_sems, 0)
        for cp in sends:
            cp.start()
        for cp in lands:
            cp.wait_recv()
        for cp in sends:
            cp.wait_send()

    return list(pl.pallas_call(
        body,
        in_specs=[HBM_REF] * n,
        out_specs=[HBM_REF] * n,
        out_shape=[jax.ShapeDtypeStruct(b.shape, b.dtype) for b in bufs],
        input_output_aliases={t: t for t in range(n)},
        scratch_shapes=[pltpu.SemaphoreType.DMA((n, 3)), pltpu.SemaphoreType.DMA((n, 3))],
        name="gather_forward",
    )(*bufs))


def pair_exchange(grads):
    n = len(grads)

    def body(*refs):
        ins, outs = refs[:n], refs[n:2 * n]
        send_sems, recv_sems = refs[2 * n:]
        cps = _pair_copies(ins, outs, lambda t: (send_sems.at[t], recv_sems.at[t]))
        for cp in cps:
            cp.start()
        for cp in cps:
            cp.wait()

    return list(pl.pallas_call(
        body,
        in_specs=[HBM_REF] * n,
        out_specs=[HBM_REF] * n,
        out_shape=[jax.ShapeDtypeStruct(_half_shape(g), g.dtype) for g in grads],
        scratch_shapes=[pltpu.SemaphoreType.DMA((n,)), pltpu.SemaphoreType.DMA((n,))],
        name="pair_exchange",
    )(*grads))


def _pair_copies(grads, lands, sem):
    x, y, c = _mesh_pos()
    cps = []
    for t in range(len(grads)):
        _, theirs, _ = _core_rows(grads[t], c)
        src = grads[t].at[pl.ds(0, grads[t].shape[0]), pl.ds(0, N_CHIPS), *theirs]
        cps.append(_remote(src, lands[t], *sem(t), (x, y, 1 - c)))
    return cps


def pair_exchange_start(grads):
    n = len(grads)
    lands = [lax.empty(_half_shape(g), g.dtype) for g in grads]

    def body(*refs):
        send_sem, recv_sem = refs[2 * n:2 * n + 2]
        outs = refs[2 * n + 2:4 * n + 2]
        token = refs[4 * n + 2]
        for cp in _pair_copies(outs[:n], outs[n:], lambda t: (send_sem, recv_sem)):
            cp.start()
        token[...] = jnp.zeros_like(token)

    res = pl.pallas_call(
        body,
        in_specs=[HBM_SPACE] * (2 * n),
        out_specs=[SEM_SPACE] * 2 + [HBM_SPACE] * (2 * n) + [pl.BlockSpec(memory_space=pltpu.VMEM)],
        out_shape=[pltpu.SemaphoreType.DMA(())] * 2 + [pltpu.HBM(a.shape, a.dtype) for a in list(grads) + lands]
        + [jax.ShapeDtypeStruct((8, LANES), F32)],
        input_output_aliases={t: t + 2 for t in range(2 * n)},
        compiler_params=pltpu.CompilerParams(has_side_effects=IN_FLIGHT),
        name="pair_exchange_start",
    )(*[pltpu.with_memory_space_constraint(a, pltpu.HBM) for a in list(grads) + lands])
    return res[0], res[1], list(res[2:n + 2]), list(res[n + 2:2 * n + 2]), res[2 * n + 2]


def pair_exchange_wait(send_sem, recv_sem, grads, lands, after):
    n = len(grads)

    def body(*refs):
        in_send, in_recv = refs[2 * n:2 * n + 2]
        outs = refs[2 * n + 3:]
        for cp in _pair_copies(outs[:n], outs[n:], lambda t: (in_send, in_recv)):
            cp.wait()

    res = pl.pallas_call(
        body,
        in_specs=[HBM_SPACE] * (2 * n) + [SEM_SPACE] * 2 + [HBM_REF],
        out_specs=[HBM_SPACE] * (2 * n),
        out_shape=[pltpu.HBM(a.shape, a.dtype) for a in list(grads) + list(lands)],
        input_output_aliases={t: t for t in range(2 * n)},
        compiler_params=pltpu.CompilerParams(has_side_effects=IN_FLIGHT),
        name="pair_exchange_wait",
    )(*grads, *lands, send_sem, recv_sem, after)
    return list(res[:n]), list(res[n:])


def pair_sum(core, g, t, name):
    n, _, hr, hc = t.shape
    R, C = g.shape[2:]
    tm = _row_tile(hr)
    per = hr // tm
    mine = (lambda a, i, c: (a, i, c[0])) if _by_columns(g.shape) else (lambda a, i, c: (a, per * c[0] + i, 0))

    def body(c_ref, g_ref, t_ref, o_ref):
        o_ref[...] = (g_ref[...] + t_ref[...]).astype(BF16)

    tile = pl.BlockSpec((None, tm, hc), lambda a, i, c: (a, i, 0))
    out = pl.pallas_call(
        body,
        grid_spec=pltpu.PrefetchScalarGridSpec(
            num_scalar_prefetch=1,
            grid=(n * N_CHIPS, per),
            in_specs=[pl.BlockSpec((None, tm, hc), mine), tile],
            out_specs=tile,
        ),
        out_shape=jax.ShapeDtypeStruct((n * N_CHIPS, hr, hc), BF16),
        compiler_params=_params("parallel", "parallel"),
        name=name,
    )(core, g.reshape(n * N_CHIPS, R, C), t.reshape(n * N_CHIPS, hr, hc))
    return out.reshape(t.shape)


def _chip_copies(sums, recv, sem):
    x, y, c = _mesh_pos()
    cps = []
    for t in range(len(sums)):
        for j, (px, py) in enumerate(_other_chips(x, y)):
            src = sums[t].at[pl.ds(0, sums[t].shape[0]), 2 * px + py]
            cps.append(_remote(src, recv[t].at[j], *sem(t, j), (px, py, c)))
    return cps


def _recv_shape(s):
    return (N_CHIPS - 1, s.shape[0]) + s.shape[2:]


def chip_exchange(sums):
    n = len(sums)

    def body(*refs):
        ins, outs = refs[:n], refs[n:2 * n]
        cps = _chip_copies(ins, outs, _sem_table(*refs[2 * n:]))
        for cp in cps:
            cp.start()
        for cp in cps:
            cp.wait()

    return list(pl.pallas_call(
        body,
        in_specs=[HBM_REF] * n,
        out_specs=[HBM_REF] * n,
        out_shape=[jax.ShapeDtypeStruct(_recv_shape(s), s.dtype) for s in sums],
        scratch_shapes=[pltpu.SemaphoreType.DMA((n, 3)), pltpu.SemaphoreType.DMA((n, 3))],
        name="chip_exchange",
    )(*sums))


def chip_exchange_start(sums):
    n = len(sums)
    lands = [lax.empty(_recv_shape(s), s.dtype) for s in sums]

    def body(*refs):
        send_sems, recv_sems = refs[2 * n:2 * n + 3], refs[2 * n + 3:2 * n + 6]
        outs = refs[2 * n + 6:4 * n + 6]
        token = refs[4 * n + 6]
        for cp in _chip_copies(outs[:n], outs[n:2 * n], _sem_per_peer(send_sems, recv_sems)):
            cp.start()
        token[...] = jnp.zeros_like(token)

    res = pl.pallas_call(
        body,
        in_specs=[HBM_SPACE] * (2 * n),
        out_specs=[SEM_SPACE] * 6 + [HBM_SPACE] * (2 * n) + [pl.BlockSpec(memory_space=pltpu.VMEM)],
        out_shape=[pltpu.SemaphoreType.DMA(())] * 6 + [pltpu.HBM(a.shape, a.dtype) for a in list(sums) + lands]
        + [jax.ShapeDtypeStruct((8, LANES), F32)],
        input_output_aliases={t: t + 6 for t in range(2 * n)},
        compiler_params=pltpu.CompilerParams(has_side_effects=IN_FLIGHT),
        name="chip_exchange_start",
    )(*[pltpu.with_memory_space_constraint(a, pltpu.HBM) for a in list(sums) + lands])
    return list(res[0:3]), list(res[3:6]), list(res[6:n + 6]), list(res[n + 6:2 * n + 6]), res[2 * n + 6]


def chip_exchange_wait(send_sems, recv_sems, sums, lands, after):
    n = len(sums)

    def body(*refs):
        ins_send, ins_recv = refs[2 * n:2 * n + 3], refs[2 * n + 3:2 * n + 6]
        outs = refs[2 * n + 7:]
        for cp in _chip_copies(outs[:n], outs[n:], _sem_per_peer(ins_send, ins_recv)):
            cp.wait()

    res = pl.pallas_call(
        body,
        in_specs=[HBM_SPACE] * (2 * n) + [SEM_SPACE] * 6 + [HBM_REF],
        out_specs=[HBM_SPACE] * (2 * n),
        out_shape=[pltpu.HBM(a.shape, a.dtype) for a in list(sums) + list(lands)],
        input_output_aliases={t: t for t in range(2 * n)},
        compiler_params=pltpu.CompilerParams(has_side_effects=IN_FLIGHT),
        name="chip_exchange_wait",
    )(*sums, *lands, *send_sems, *recv_sems, after)
    return list(res[:n]), list(res[n:])


def chip_sum_into(core, chip, recv, sums, total, lo, name):
    _, n, hr, hc = recv.shape
    tm = _row_tile(hr)
    per = hr // tm
    if _by_columns(total.shape):
        mine = lambda a, i, c, k: (lo + a, i, c[0])
    else:
        mine = lambda a, i, c, k: (lo + a, per * c[0] + i, 0)

    def body(c_ref, k_ref, r_ref, s_ref, t_ref, o_ref):
        acc = s_ref[...].astype(F32)
        for j in range(N_CHIPS - 1):
            acc = acc + r_ref[j].astype(F32)
        o_ref[...] = acc

    return pl.pallas_call(
        body,
        grid_spec=pltpu.PrefetchScalarGridSpec(
            num_scalar_prefetch=2,
            grid=(n, per),
            in_specs=[pl.BlockSpec((N_CHIPS - 1, None, tm, hc), lambda a, i, c, k: (0, a, i, 0)),
                      pl.BlockSpec((None, None, tm, hc), lambda a, i, c, k: (a, k[0], i, 0)),
                      HBM_REF],
            out_specs=pl.BlockSpec((None, tm, hc), mine),
        ),
        out_shape=jax.ShapeDtypeStruct(total.shape, F32),
        input_output_aliases={4: 0},
        compiler_params=_params("parallel", "parallel"),
        name=name,
    )(core, chip, recv, sums, total)


def sibling_share(totals):
    n = len(totals)

    def body(*refs):
        outs = refs[n:2 * n]
        send_sems, recv_sems = refs[2 * n:]
        x, y, c = _mesh_pos()
        half = lambda t, which: outs[t].at[pl.ds(0, DEPTH), *_core_rows(outs[t], c)[which]]
        sent = [_remote(half(t, 0), half(t, 0), send_sems.at[t], recv_sems.at[t], (x, y, 1 - c)) for t in range(n)]
        for cp in sent:
            cp.start()
        for t in range(n):
            _remote(half(t, 1), half(t, 1), send_sems.at[t], recv_sems.at[t], (x, y, 1 - c)).wait_recv()
        for cp in sent:
            cp.wait_send()

    return pl.pallas_call(
        body,
        in_specs=[HBM_REF] * n,
        out_specs=[HBM_REF] * n,
        out_shape=[jax.ShapeDtypeStruct(t.shape, t.dtype) for t in totals],
        input_output_aliases={t: t for t in range(n)},
        scratch_shapes=[pltpu.SemaphoreType.DMA((n,)), pltpu.SemaphoreType.DMA((n,))],
        name="sibling_share",
    )(*totals)


def small_allgather(small):
    def body(s_ref, a_ref, send_sems, recv_sems, local_sem):
        x, y, c = _mesh_pos()
        me = 4 * x + 2 * y + c
        own = pltpu.make_async_copy(s_ref, a_ref.at[me], local_sem)
        own.start()
        sent = []
        for k in range(1, N_DEV):
            peer = (x ^ (k >> 2), y ^ ((k >> 1) & 1), c ^ (k & 1))
            cp = _remote(s_ref, a_ref.at[me], send_sems.at[k - 1], recv_sems.at[k - 1], peer)
            cp.start()
            sent.append(cp)
        for k in range(1, N_DEV):
            px, py, pc = x ^ (k >> 2), y ^ ((k >> 1) & 1), c ^ (k & 1)
            _remote(s_ref, a_ref.at[4 * px + 2 * py + pc], send_sems.at[k - 1], recv_sems.at[k - 1], (px, py, pc)).wait_recv()
        for cp in sent:
            cp.wait_send()
        own.wait()

    return pl.pallas_call(
        body,
        in_specs=[HBM_REF],
        out_specs=HBM_REF,
        out_shape=jax.ShapeDtypeStruct((N_DEV,) + SMALL_SHAPE, small.dtype),
        scratch_shapes=[pltpu.SemaphoreType.DMA((N_DEV - 1,)), pltpu.SemaphoreType.DMA((N_DEV - 1,)), pltpu.SemaphoreType.DMA],
        name="small_allgather",
    )(small)


def small_sum(blocks):
    def body(a_ref, o_ref):
        acc = a_ref[0]
        for d in range(1, N_DEV):
            acc = acc + a_ref[d]
        o_ref[...] = acc

    return pl.pallas_call(
        body,
        in_specs=[pl.BlockSpec(memory_space=pltpu.VMEM)],
        out_specs=pl.BlockSpec(memory_space=pltpu.VMEM),
        out_shape=jax.ShapeDtypeStruct(SMALL_SHAPE, F32),
        name="small_sum",
    )(blocks)


def pack_small(grads, loss):
    flat = jnp.concatenate([grads[n].reshape(-1) for n in SMALL] + [loss.reshape(-1)])
    size = SMALL_SHAPE[0] * SMALL_SHAPE[1]
    return jnp.pad(flat, (0, size - flat.shape[0])).reshape(SMALL_SHAPE)


def unpack_small(packed):
    flat = packed.reshape(-1)
    out, off = {}, 0
    for n, size in SMALL.items():
        out[n] = flat[off:off + DEPTH * size].reshape(DEPTH, size)
        off += DEPTH * size
    return out, flat[off]


def kernel(x, norm_mix_g, w_in, forget_b, q_norm_g, k_norm_g, w_attn_out, conv_w, w_conv_out, pool_w, pool_scale, w_o, norm_ffn_g, w_ffn_in, w_ffn_out, loss_target, m_norm_mix_g, m_w_in, m_forget_b, m_q_norm_g, m_k_norm_g, m_w_attn_out, m_conv_w, m_w_conv_out, m_pool_w, m_pool_scale, m_w_o, m_norm_ffn_g, m_w_ffn_in, m_w_ffn_out, v_norm_mix_g, v_w_in, v_forget_b, v_q_norm_g, v_k_norm_g, v_w_attn_out, v_conv_w, v_w_conv_out, v_pool_w, v_pool_scale, v_w_o, v_norm_ffn_g, v_w_ffn_in, v_w_ffn_out):
    given = dict(locals())
    weights = {n: given[n] for n in WEIGHTS}

    core = lax.axis_index("c").astype(jnp.int32)
    chip = (2 * lax.axis_index("x") + lax.axis_index("y")).astype(jnp.int32)

    core, chip = core.reshape(1), chip.reshape(1)
    mix_names, all_names = MIX_W + ["conv_w"], MIX_W + FFN_W + ["conv_w"]

    def placed(lo, hi, names, after=None):
        piece = None if after is None else _as4(next(iter(after.values())))[0, 0, 0:HALO, 0:LANES]
        return [place_shard(chip, weights[n], lo, hi, F32 if n == "conv_w" else BF16, "place_" + n,
                            piece if i == 0 else None) for i, n in enumerate(names)]

    def as_weights(bufs, names):
        return {n: b.reshape(b.shape[:2] + weights[n].shape[1:]) for n, b in zip(names, bufs)}

    def landed(start, names, after):
        return as_weights(gather_forward(gather_wait(*start[:3], after)), names)

    def layer_small(l, *tokens):
        small = {n: weights[n][l] for n in SMALL if n != "conv_w"}
        for t in tokens:
            small["norm_mix_g"] = small["norm_mix_g"] + t[0, 0]
        return small

    done = [None] * DEPTH
    first = as_weights(gather_shards(placed(0, 1, mix_names)), mix_names)
    ffn0 = gather_start(placed(0, 1, FFN_W, after=first))
    layer1 = gather_start(placed(1, 2, all_names, after=first))
    xm, wm, sm = mix_fwd(x[0], first, 0, layer_small(0, ffn0[3], layer1[3]))
    xs, wf, sf = ffn_half_fwd(xm, landed(ffn0, FFN_W, xm), 0, weights["norm_ffn_g"][0])
    done[0] = (wm, sm, wf, sf)
    full = landed(layer1, all_names, xs)
    rest = gather_start(placed(2, DEPTH, all_names, after=full))
    xm, wm, sm = mix_fwd(xs, full, 0, layer_small(1, rest[3]))
    xs, wf, sf = ffn_half_fwd(xm, full, 0, weights["norm_ffn_g"][1])
    done[1] = (wm, sm, wf, sf)
    full = landed(rest, all_names, xs)
    for l in range(2, DEPTH):
        xm, wm, sm = mix_fwd(xs, full, l - 2, layer_small(l))
        xs, wf, sf = ffn_half_fwd(xm, full, l - 2, weights["norm_ffn_g"][l])
        done[l] = (wm, sm, wf, sf)
    loss, dx = loss_head(xs, loss_target[0])

    small_grads = [None] * DEPTH
    totals = {n: lax.empty((DEPTH,) + _as4(weights[n][None]).shape[2:], F32) for n in SPLIT}

    def buffers(names, n_layers):
        return {n: lax.empty((n_layers, N_CHIPS) + weights[n].shape[1:], F32) for n in names}

    def pair_sums(names, grads, theirs):
        return [pair_sum(core, a, t, "pair_sum_" + n) for n, a, t in zip(names, grads, theirs)]

    def add_chips(names, recv, sums, lo):
        for n, r, s in zip(names, recv, sums):
            totals[n] = chip_sum_into(core, chip, r, s, totals[n], lo, "chip_sum_" + n)

    big = buffers(SPLIT, DEPTH - 1)
    for l in reversed(range(1, DEPTH)):
        wm, sm, wf, sf = done[l]
        dxm, ffn_big, g_ffn = ffn_half_bwd(dx, wf, sf, l - 1, {n: big[n] for n in FFN_W})
        dx, mix_big, g_mix = mix_bwd(dxm, wm, sm, l - 1, {n: big[n] for n in MIX_W})
        big = {**ffn_big, **mix_big}
        small_grads[l] = {**g_ffn, **g_mix}
    pair = pair_exchange_start([_as4(big[n]) for n in SPLIT])

    wm, sm, wf, sf = done[0]
    wf = dict(wf, g2=wf["g2"] + pair[4][0:1, 0:1])
    dxm, ffn_big, g_ffn = ffn_half_bwd(dx, wf, sf, 0, buffers(FFN_W, 1))
    chips = chip_exchange_start(pair_sums(SPLIT, *pair_exchange_wait(*pair[:4], ffn_big["w_ffn_in"])))
    ffn_grads = [_as4(ffn_big[n]) for n in FFN_W]
    ffn_chips = chip_exchange_start(pair_sums(FFN_W, ffn_grads, pair_exchange(ffn_grads)))
    wm = dict(wm, pscale=wm["pscale"] + chips[4][0:1, 0:1] + ffn_chips[4][0:1, 0:1])
    dproj, mix_big, g_mix = mix_bwd_weights(dxm, wm, sm, 0, buffers(MIX_W, 1))
    mix_grads = [_as4(mix_big[n]) for n in MIX_W]
    mix_chips = chip_exchange_start(pair_sums(MIX_W, mix_grads, pair_exchange(mix_grads)))
    dx, g_in = mix_bwd_input(dproj, dxm, dict(wm, g1=wm["g1"] + mix_chips[4][0:1, 0:1]), sm)
    small_grads[0] = {**g_ffn, **g_mix, **g_in}

    reduced, deltas, new_m, new_v = {}, {}, {}, {}

    def update(names):
        for n in names:
            w = weights[n]
            flat = (-1, w.shape[-1])
            d, nm, nv = adamw(w.reshape(flat), reduced[n].reshape(flat), given["m_" + n].reshape(flat),
                              given["v_" + n].reshape(flat), "adamw_" + n)
            deltas[n], new_m[n], new_v[n] = d.reshape(w.shape), nm.reshape(w.shape), nv.reshape(w.shape)

    def share(names):
        for n, t in zip(names, sibling_share([totals[n] for n in names])):
            reduced[n] = t.reshape(weights[n].shape)

    sums, recv = chip_exchange_wait(*chips[:4], dx)
    add_chips(SPLIT, recv, sums, 1)
    sums, recv = chip_exchange_wait(*ffn_chips[:4], dx)
    add_chips(FFN_W, recv, sums, 0)
    share(FFN_W)
    update(FFN_W)
    sums, recv = chip_exchange_wait(*mix_chips[:4], deltas["w_ffn_in"])
    add_chips(MIX_W, recv, sums, 0)
    share(MIX_W)

    small_grads = {n: jnp.stack([g[n] for g in small_grads]) for n in SMALL}
    small_total, loss_sum = unpack_small(small_sum(small_allgather(pack_small(small_grads, loss))))
    chip = chip[0]
    cols = D_CONV // N_CHIPS
    small_total["conv_w"] = lax.dynamic_slice_in_dim(small_total["conv_w"].reshape(DEPTH, CONV_K, D_CONV), chip * cols, cols, axis=2)
    for n in SMALL:
        reduced[n] = small_total[n].reshape(weights[n].shape)
    update(MIX_W + list(SMALL))

    return (loss_sum, dx[None], *[reduced[n] for n in WEIGHTS], *[deltas[n] for n in WEIGHTS],
            *[new_m[n] for n in WEIGHTS], *[new_v[n] for n in WEIGHTS])
```

```python
import functools

import numpy as np
import jax
import jax.numpy as jnp
from jax import lax
from jax.experimental import pallas as pl
from jax.experimental.pallas import tpu as pltpu

F32 = jnp.float32
BF16 = jnp.bfloat16

D_MODEL = 1024
DEPTH = 4
HEAD_DIM = 64
N_HEADS = 8
D_ATTN = 512
D_CONV = 256
D_POOL = 256
D_FF = 2816
D_IN = 5640
CONV_K = 3
POOL_WINDOWS = (2, 4, 8, 16)
N_GROUPS = len(POOL_WINDOWS)
EPS = 1e-6
ADAM_LR, ADAM_B1, ADAM_B2, ADAM_EPS, ADAM_WD, ADAM_STEP = 0.001, 0.9, 0.999, 1e-08, 0.01, 10

D_QKV = 3 * D_ATTN
D_F = 128
D_B = 3 * D_CONV + D_POOL + 3 * D_MODEL
D_LOCAL = 3 * D_CONV + D_POOL

LANES = 128
D_HEADS = N_HEADS * LANES
HALO = 16
VMEM_LIMIT = 56 * 1024 * 1024
NEG = -1e30
LOG2E = 1.4426950408889634
LN2 = 0.6931471805599453

TM = 256
TM_MIX = 256
TQ = 1024

LANE_C = 64
LANE_ONE = 67
LANE_LSE = 70
N_PIECES = 3


def _dot(a, b):
    return jnp.dot(a, b, preferred_element_type=F32)


def _dot_nt(a, b):
    return lax.dot_general(a, b, (((1,), (1,)), ((), ())), preferred_element_type=F32)


def _dot_tn(a, b):
    return lax.dot_general(a, b, (((0,), (0,)), ((), ())), preferred_element_type=F32)


def _params(*sem):
    return pltpu.CompilerParams(dimension_semantics=sem, vmem_limit_bytes=VMEM_LIMIT)


def _rows(tm, n):
    return pl.BlockSpec((tm, n), lambda i, *_: (i, 0))


def _whole(a):
    nd = a.ndim
    return pl.BlockSpec(a.shape, lambda *_: (0,) * nd)


def _layer(shape):
    nd = len(shape)
    return pl.BlockSpec((None,) + tuple(shape), lambda *a: (a[-1][0],) + (0,) * nd)


def _layer_index(l):
    return jnp.full((1,), l, jnp.int32)


def _split_bf16(x):
    hi = x.astype(BF16)
    lo = (x - hi.astype(F32)).astype(BF16)
    return hi, lo


def _pieces(x):
    p1 = x.astype(BF16)
    r1 = x - p1.astype(F32)
    p2 = r1.astype(BF16)
    p3 = (r1 - p2.astype(F32)).astype(BF16)
    return p1, p2, p3


def _sigmoid(x):
    return 1.0 / (1.0 + jnp.exp(-x))


def w_in_prep(win, l):
    tr = 256
    n = D_IN // 4
    v_rest = D_QKV - n
    b0 = v_rest + N_HEADS

    def body(s0, s1, s2, s3, wa_ref, wf_ref, wb_ref):
        b = s1[...]
        wa_ref[...] = jnp.concatenate([s0[...], b[:, 0:v_rest]], axis=1)
        wf_ref[...] = jnp.concatenate([b[:, v_rest:b0], jnp.zeros((tr, D_F - N_HEADS), b.dtype)], axis=1)
        wb_ref[...] = jnp.concatenate([b[:, b0:n], s2[...], s3[...]], axis=1)

    shard = lambda j: pl.BlockSpec((None, None, tr, n), lambda i: (l, j, i, 0))
    return pl.pallas_call(
        body,
        grid=(D_MODEL // tr,),
        in_specs=[shard(0), shard(1), shard(2), shard(3)],
        out_specs=[_rows(tr, D_QKV), _rows(tr, D_F), _rows(tr, D_B)],
        out_shape=[jax.ShapeDtypeStruct((D_MODEL, D_QKV), win.dtype), jax.ShapeDtypeStruct((D_MODEL, D_F), win.dtype),
                   jax.ShapeDtypeStruct((D_MODEL, D_B), win.dtype)],
        compiler_params=_params("parallel"),
        name="w_in_prep",
    )(win, win, win, win)


def _into_layer(buf):
    return dict(in_spec=HBM_REF, out_shape=jax.ShapeDtypeStruct(buf.shape, buf.dtype), aliases={1: 0})


def w_in_unprep(dwa, dwf, dwb, buf, l):
    tr = 256
    n = D_IN // 4
    v_rest = D_QKV - n
    b1 = n - v_rest - N_HEADS
    place = _into_layer(buf)

    def body(l_ref, buf_ref, a_ref, f_ref, b_ref, o_ref):
        a = a_ref[...]
        b = b_ref[...]
        o_ref[0] = a[:, 0:n]
        o_ref[1] = jnp.concatenate([a[:, n:D_QKV], f_ref[:, 0:N_HEADS], b[:, 0:b1]], axis=1)
        o_ref[2] = b[:, b1:b1 + n]
        o_ref[3] = b[:, b1 + n:D_B]

    return pl.pallas_call(
        body,
        grid_spec=pltpu.PrefetchScalarGridSpec(
            num_scalar_prefetch=1,
            grid=(D_MODEL // tr,),
            in_specs=[place["in_spec"], _rows(tr, D_QKV), _rows(tr, D_F), _rows(tr, D_B)],
            out_specs=pl.BlockSpec((None, 4, tr, n), lambda i, l: (l[0], 0, i, 0)),
        ),
        out_shape=place["out_shape"],
        input_output_aliases=place["aliases"],
        compiler_params=_params("parallel"),
        name="w_in_unprep",
    )(_layer_index(l), buf, dwa, dwf, dwb)


def branch_w_prep(wao, wco, pw, cw, l):
    gd = D_POOL // N_GROUPS
    od = D_MODEL // N_GROUPS

    def body(wao_ref, wco_ref, pw_ref, cw_ref, ao_ref, co_ref, po_ref, co8_ref):
        ao_ref[...] = jnp.concatenate([wao_ref[j] for j in range(4)], axis=1)
        co_ref[...] = jnp.concatenate([wco_ref[j] for j in range(4)], axis=1)
        zero = jnp.zeros((gd, od), co_ref.dtype)
        po_ref[...] = jnp.concatenate(
            [jnp.concatenate([jnp.concatenate([pw_ref[j, g] for j in range(4)], axis=1) if g2 == g else zero
                              for g2 in range(N_GROUPS)], axis=1) for g in range(N_GROUPS)], axis=0)
        co8_ref[...] = jnp.zeros_like(co8_ref)
        co8_ref[0:CONV_K, :] = jnp.concatenate([cw_ref[j] for j in range(4)], axis=1)

    sel = lambda *shape: pl.BlockSpec((None,) + shape, lambda i: (l,) + (0,) * len(shape))
    return pl.pallas_call(
        body,
        grid=(1,),
        in_specs=[sel(4, D_ATTN, D_MODEL // 4), sel(4, D_CONV, D_MODEL // 4), sel(4, N_GROUPS, gd, od // 4),
                  sel(4, CONV_K, D_CONV // 4)],
        out_specs=[pl.BlockSpec((D_ATTN, D_MODEL), lambda i: (0, 0)), pl.BlockSpec((D_CONV, D_MODEL), lambda i: (0, 0)),
                   pl.BlockSpec((D_POOL, D_MODEL), lambda i: (0, 0)), pl.BlockSpec((8, D_CONV), lambda i: (0, 0))],
        out_shape=[jax.ShapeDtypeStruct((D_ATTN, D_MODEL), BF16), jax.ShapeDtypeStruct((D_CONV, D_MODEL), BF16),
                   jax.ShapeDtypeStruct((D_POOL, D_MODEL), BF16), jax.ShapeDtypeStruct((8, D_CONV), F32)],
        name="branch_w_prep",
    )(wao, wco, pw, cw)


def branch_g_place(dwao, dwco, dwpool, dwo, bufs, l):
    gd = D_POOL // N_GROUPS
    od = D_MODEL // N_GROUPS
    q = D_MODEL // 4

    def body(l_ref, b0, b1, b2, b3, a_ref, c_ref, p_ref, w_ref, ao_ref, co_ref, po_ref, wo_ref):
        a = a_ref[...]
        c = c_ref[...]
        p = p_ref[...]
        for j in range(4):
            ao_ref[j] = a[:, j * q:(j + 1) * q]
            co_ref[j] = c[:, j * q:(j + 1) * q]
            wo_ref[j] = w_ref[j * q:(j + 1) * q, :]
            for g in range(N_GROUPS):
                c0 = g * od + j * (od // 4)
                po_ref[j, g] = p[g * gd:(g + 1) * gd, c0:c0 + od // 4]

    whole = lambda a: pl.BlockSpec(a.shape, lambda i, l: (0,) * a.ndim)
    layer = lambda b: pl.BlockSpec((None,) + b.shape[1:], lambda i, l: (l[0],) + (0,) * (b.ndim - 1))
    return pl.pallas_call(
        body,
        grid_spec=pltpu.PrefetchScalarGridSpec(
            num_scalar_prefetch=1,
            grid=(1,),
            in_specs=[HBM_REF] * 4 + [whole(dwao), whole(dwco), whole(dwpool), whole(dwo)],
            out_specs=[layer(b) for b in bufs],
        ),
        out_shape=[jax.ShapeDtypeStruct(b.shape, b.dtype) for b in bufs],
        input_output_aliases={1: 0, 2: 1, 3: 2, 4: 3},
        compiler_params=_params("arbitrary"),
        name="branch_g_place",
    )(_layer_index(l), *bufs, dwao, dwco, dwpool, dwo)


def in_proj_fwd(x, g, wa, wf, wb):
    S = x.shape[0]

    def body(x_ref, g_ref, wa_ref, wf_ref, wb_ref, h_ref, pa_ref, pf_ref, pb_ref):
        xf = x_ref[...]
        r = lax.rsqrt(jnp.mean(xf * xf, axis=-1, keepdims=True) + EPS)
        h = (xf * r * g_ref[...]).astype(BF16)
        h_ref[...] = h
        pa_ref[...] = _dot(h, wa_ref[...]).astype(BF16)
        pf_ref[...] = _dot(h, wf_ref[...])
        pb_ref[...] = _dot(h, wb_ref[...]).astype(BF16)

    return pl.pallas_call(
        body,
        grid=(S // TM,),
        in_specs=[_rows(TM, D_MODEL), _whole(g), _whole(wa), _whole(wf), _whole(wb)],
        out_specs=[_rows(TM, D_MODEL), _rows(TM, D_QKV), _rows(TM, D_F), _rows(TM, D_B)],
        out_shape=[
            jax.ShapeDtypeStruct((S, D_MODEL), BF16),
            jax.ShapeDtypeStruct((S, D_QKV), BF16),
            jax.ShapeDtypeStruct((S, D_F), F32),
            jax.ShapeDtypeStruct((S, D_B), BF16),
        ],
        compiler_params=_params("parallel"),
        name="in_proj_fwd",
    )(x, g, wa, wf, wb)


def in_proj_bwd(x, g, dxm, dpa, dpf, dpb, wa, wf, wb):
    S = x.shape[0]

    def body(x_ref, g_ref, dxm_ref, dpa_ref, dpf_ref, dpb_ref, wa_ref, wf_ref, wb_ref, dx_ref, dg_ref):
        @pl.when(pl.program_id(0) == 0)
        def _():
            dg_ref[...] = jnp.zeros_like(dg_ref)

        dh = _dot_nt(dpa_ref[...], wa_ref[...]) + _dot_nt(dpf_ref[...], wf_ref[...]) + _dot_nt(dpb_ref[...], wb_ref[...])
        xf = x_ref[...]
        r = lax.rsqrt(jnp.mean(xf * xf, axis=-1, keepdims=True) + EPS)
        xhat = xf * r
        dg_ref[...] += jnp.sum(dh * xhat, axis=0, keepdims=True)
        gdh = dh * g_ref[...]
        dx_ref[...] = dxm_ref[...] + r * (gdh - xhat * jnp.mean(xhat * gdh, axis=-1, keepdims=True))

    return pl.pallas_call(
        body,
        grid=(S // TM,),
        in_specs=[_rows(TM, D_MODEL), _whole(g), _rows(TM, D_MODEL), _rows(TM, D_QKV), _rows(TM, D_F), _rows(TM, D_B),
                  _whole(wa), _whole(wf), _whole(wb)],
        out_specs=[_rows(TM, D_MODEL), pl.BlockSpec((1, D_MODEL), lambda i: (0, 0))],
        out_shape=[jax.ShapeDtypeStruct((S, D_MODEL), F32), jax.ShapeDtypeStruct((1, D_MODEL), F32)],
        compiler_params=_params("arbitrary"),
        name="in_proj_bwd",
    )(x, g, dxm, dpa, dpf, dpb, wa, wf, wb)


def _wgrad_tokens(S, K):
    return min(S, 2048 if K <= 1024 else 1024)


def _wgrad_columns(n):
    return next(tn for tn in (1024, 768, 512, 256, 128) if n % tn == 0)


def wgrad(xa, dy, name):
    S, K = xa.shape
    N = dy.shape[1]
    ts = _wgrad_tokens(S, K)
    tn = _wgrad_columns(N)
    tk = _wgrad_columns(K)

    def body(x_ref, dy_ref, o_ref):
        @pl.when(pl.program_id(2) == 0)
        def _():
            o_ref[...] = jnp.zeros_like(o_ref)

        o_ref[...] += _dot_tn(x_ref[...], dy_ref[...])

    return pl.pallas_call(
        body,
        grid=(K // tk, N // tn, S // ts),
        in_specs=[pl.BlockSpec((ts, tk), lambda i, j, k: (k, i)), pl.BlockSpec((ts, tn), lambda i, j, k: (k, j))],
        out_specs=pl.BlockSpec((tk, tn), lambda i, j, k: (i, j)),
        out_shape=jax.ShapeDtypeStruct((K, N), F32),
        compiler_params=_params("parallel", "parallel", "arbitrary"),
        name=name,
    )(xa, dy)


def wgrad_into(xa, dy, name, buf, l):
    S, K = xa.shape
    N = dy.shape[1]
    ts = _wgrad_tokens(S, K)
    split = buf.ndim == 4
    tn = buf.shape[-1] if split else _wgrad_columns(N)
    place = _into_layer(buf)

    def body(l_ref, buf_ref, x_ref, dy_ref, o_ref):
        @pl.when(pl.program_id(1) == 0)
        def _():
            o_ref[...] = jnp.zeros_like(o_ref)

        o_ref[...] += _dot_tn(x_ref[...], dy_ref[...])

    if split:
        out_spec = pl.BlockSpec((None, None, K, tn), lambda j, k, l: (l[0], j, 0, 0))
    else:
        out_spec = pl.BlockSpec((None, K, tn), lambda j, k, l: (l[0], 0, j))
    return pl.pallas_call(
        body,
        grid_spec=pltpu.PrefetchScalarGridSpec(
            num_scalar_prefetch=1,
            grid=(N // tn, S // ts),
            in_specs=[place["in_spec"], pl.BlockSpec((ts, K), lambda j, k, l: (k, 0)),
                      pl.BlockSpec((ts, tn), lambda j, k, l: (k, j))],
            out_specs=out_spec,
        ),
        out_shape=place["out_shape"],
        input_output_aliases=place["aliases"],
        compiler_params=_params("parallel", "arbitrary"),
        name=name,
    )(_layer_index(l), buf, xa, dy)


def _head_mean_matrix():
    h = np.arange(D_ATTN) // HEAD_DIM
    return jnp.asarray((h[:, None] == h[None, :]).astype(np.float32) / HEAD_DIM, BF16)


def _place_matrix(lane0):
    m = np.zeros((N_PIECES * LANES, D_HEADS), np.float32)
    for i in range(N_PIECES):
        for h in range(N_HEADS):
            m[i * LANES + h, h * LANES + lane0 + i] = 1.0
    return jnp.asarray(m, BF16)


def _tri(n, upper):
    r = np.arange(n)
    m = (r[None, :] >= r[:, None]) if upper else (r[None, :] <= r[:, None])
    return jnp.asarray(m.astype(np.float32), BF16)


def _lanes_in(lane, lo, n):
    return (lane >= lo) & (lane < lo + n)


def qk_prep(pa, pf, gq, gk, fb):
    S = pa.shape[0]
    bd = _head_mean_matrix()
    tri = _tri(TM, upper=False)
    place_q = _place_matrix(LANE_C)
    place_k = _place_matrix(LANE_ONE)

    def body(q_ref, k_ref, v_ref, pf_ref, gq_ref, gk_ref, fb_ref, bd_ref, tri_ref, pq_ref, pk_ref,
             qx_ref, kx_ref, vx_ref, carry):
        @pl.when(pl.program_id(0) == 0)
        def _():
            carry[...] = jnp.zeros_like(carry)

        def head_norm(x_ref, g_ref, scale):
            xf = x_ref[...].astype(F32)
            ms = _dot((xf * xf).astype(BF16), bd_ref[...])
            return xf * lax.rsqrt(ms + EPS) * g_ref[...] * scale

        qh = head_norm(q_ref, gq_ref, HEAD_DIM ** -0.5 * LOG2E)
        kh = head_norm(k_ref, gk_ref, 1.0)
        vf = v_ref[...].astype(F32)

        z = pf_ref[...] + fb_ref[...]
        logf = jnp.minimum(z, 0.0) - jnp.log(1.0 + jnp.exp(-jnp.abs(z)))
        hi, lo = _split_bf16(logf)
        c = _dot(tri_ref[...], hi) + _dot(tri_ref[...], lo) + carry[...]
        carry[...] += jnp.sum(hi.astype(F32) + lo.astype(F32), axis=0, keepdims=True)
        pieces = jnp.concatenate(_pieces(c * LOG2E), axis=1)
        cq = _dot(pieces, pq_ref[...])
        ck = _dot(pieces, pk_ref[...])

        lane = lax.broadcasted_iota(jnp.int32, (TM, LANES), 1)
        low = lane < HEAD_DIM
        ones_q = _lanes_in(lane, LANE_ONE, N_PIECES).astype(F32)
        ones_k = (_lanes_in(lane, LANE_C, N_PIECES) | _lanes_in(lane, LANE_LSE, N_PIECES)).astype(F32)
        ones_v = _lanes_in(lane, LANE_C, N_PIECES + 1).astype(F32)
        for h in range(N_HEADS):
            blk = slice((h // 2) * LANES, (h // 2 + 1) * LANES)
            head = (lambda a: pltpu.roll(a[:, blk], HEAD_DIM, 1)) if h % 2 else (lambda a: a[:, blk])
            mine = slice(h * LANES, (h + 1) * LANES)
            qx_ref[h] = jnp.where(low, head(qh), cq[:, mine] + ones_q).astype(BF16)
            kx_ref[h] = jnp.where(low, head(kh), ones_k - ck[:, mine]).astype(BF16)
            vx_ref[h] = jnp.where(low, head(vf), ones_v).astype(BF16)

    heads = pl.BlockSpec((N_HEADS, TM, LANES), lambda i: (0, i, 0))
    out = jax.ShapeDtypeStruct((N_HEADS, S, LANES), BF16)
    return pl.pallas_call(
        body,
        grid=(S // TM,),
        in_specs=[pl.BlockSpec((TM, D_ATTN), lambda i: (i, 0)), pl.BlockSpec((TM, D_ATTN), lambda i: (i, 1)),
                  pl.BlockSpec((TM, D_ATTN), lambda i: (i, 2)),
                  _rows(TM, D_F), _whole(gq), _whole(gk), _whole(fb), _whole(bd), _whole(tri), _whole(place_q), _whole(place_k)],
        out_specs=[heads, heads, heads],
        out_shape=[out, out, out],
        scratch_shapes=[pltpu.VMEM((1, D_F), F32)],
        compiler_params=_params("arbitrary"),
        name="qk_prep",
    )(pa, pa, pa, pf, gq, gk, fb, bd, tri, place_q, place_k)


def attn_fwd(qx, kx, vx):
    S = qx.shape[1]
    nq = S // TQ

    def body(q_ref, k_ref, v_ref, o_ref, qb_ref):
        i = pl.program_id(1)
        lane = lax.broadcasted_iota(jnp.int32, (TQ, LANES), 1)
        q = [q_ref[0], q_ref[1]]

        def update(qh, m, acc, k, v, visible):
            z = _dot_nt(qh, k)
            if visible is not None:
                z = jnp.where(visible, z, NEG)
            m_new = jnp.maximum(m, jnp.max(z, axis=1, keepdims=True))
            pr = jnp.exp2(z - m_new)
            return m_new, jnp.exp2(m - m_new) * acc + _dot(pr.astype(BF16), v)

        def step(kt, carry, diagonal=False):
            ks = pl.multiple_of(kt * TQ, TQ)
            visible = None
            if diagonal:
                visible = lax.broadcasted_iota(jnp.int32, (TQ, TQ), 0) >= lax.broadcasted_iota(jnp.int32, (TQ, TQ), 1)
            return tuple(update(q[j], *carry[j], k_ref[j, pl.ds(ks, TQ), :], v_ref[j, pl.ds(ks, TQ), :], visible)
                         for j in range(2))

        init = tuple((jnp.full((TQ, 1), NEG, F32), jnp.zeros((TQ, LANES), F32)) for _ in range(2))
        carry = step(i, lax.fori_loop(0, i, step, init), diagonal=True)
        for j in range(2):
            m, acc = carry[j]
            l = jnp.sum(jnp.where(lane == LANE_C, acc, 0.0), axis=1, keepdims=True)
            o_ref[j] = acc / l
            n1, n2, n3 = _pieces(-(m + jnp.log(l) * LOG2E))
            qb_ref[j] = jnp.where(lane == LANE_LSE, n1,
                                  jnp.where(lane == LANE_LSE + 1, n2, jnp.where(lane == LANE_LSE + 2, n3, q[j])))

    pair_tile = pl.BlockSpec((2, TQ, LANES), lambda p, i: (p, i, 0))
    pair_all = pl.BlockSpec((2, S, LANES), lambda p, i: (p, 0, 0))
    return pl.pallas_call(
        body,
        grid=(N_HEADS // 2, nq),
        in_specs=[pair_tile, pair_all, pair_all],
        out_specs=[pair_tile, pair_tile],
        out_shape=[jax.ShapeDtypeStruct((N_HEADS, S, LANES), F32), jax.ShapeDtypeStruct((N_HEADS, S, LANES), BF16)],
        compiler_params=_params("parallel", "parallel"),
        name="attn_fwd",
    )(qx, kx, vx)


def attn_bwd(qxb, kx, vx, dox):
    S = qxb.shape[1]
    nq = S // TQ

    def body(q_ref, k_ref, v_ref, do_ref, dq_ref, dk_ref, dv_ref):
        kt = pl.program_id(1)

        @pl.when(kt == 0)
        def _():
            dq_ref[...] = jnp.zeros_like(dq_ref)

        k = [k_ref[0], k_ref[1]]
        v = [v_ref[0], v_ref[1]]

        def block(j, q0, nq_rows, k0, nk, diagonal):
            q = q_ref[j, pl.ds(q0, nq_rows), :]
            dout = do_ref[j, pl.ds(q0, nq_rows), :]
            kj, vj = k[j][k0:k0 + nk], v[j][k0:k0 + nk]
            z = _dot_nt(q, kj)
            if diagonal:
                visible = (lax.broadcasted_iota(jnp.int32, (nq_rows, nk), 0)
                           >= lax.broadcasted_iota(jnp.int32, (nq_rows, nk), 1))
                z = jnp.where(visible, z, NEG)
            pr = jnp.exp2(z)
            dv = _dot_tn(pr.astype(BF16), dout)
            dsb = (pr * _dot_nt(dout, vj)).astype(BF16)
            dq_ref[j, pl.ds(q0, nq_rows), :] += _dot(dsb, kj)
            return _dot_tn(dsb, q), dv

        def step(qi, carry):
            qs = pl.multiple_of(qi * TQ, TQ)
            new = []
            for j in range(2):
                dk, dv = block(j, qs, TQ, 0, TQ, False)
                new.append((carry[j][0] + dk, carry[j][1] + dv))
            return tuple(new)

        half = TQ // 2
        qs = pl.multiple_of(kt * TQ, TQ)
        carry = []
        for j in range(2):
            first = block(j, qs, TQ, 0, half, True)
            second = block(j, pl.multiple_of(qs + half, half), half, half, half, True)
            carry.append(tuple(jnp.concatenate(a, axis=0) for a in zip(first, second)))
        carry = lax.fori_loop(kt + 1, nq, step, tuple(carry))
        for j in range(2):
            dk_ref[j] = carry[j][0]
            dv_ref[j] = carry[j][1]

    pair_tile = pl.BlockSpec((2, TQ, LANES), lambda p, kt: (p, kt, 0))
    pair_all = pl.BlockSpec((2, S, LANES), lambda p, kt: (p, 0, 0))
    out = jax.ShapeDtypeStruct((N_HEADS, S, LANES), F32)
    return pl.pallas_call(
        body,
        grid=(N_HEADS // 2, nq),
        in_specs=[pair_all, pair_tile, pair_tile, pair_all],
        out_specs=[pair_all, pair_tile, pair_tile],
        out_shape=[out, out, out],
        compiler_params=_params("arbitrary", "arbitrary"),
        name="attn_bwd",
    )(qxb, kx, vx, dox)


def attn_bwd_post(pa, pf, dqx, dkx, dvx, gq, gk, fb):
    S = pa.shape[0]
    nt = S // TM
    bd = _head_mean_matrix()
    triu = _tri(TM, upper=True)
    rev = lambda i: nt - 1 - i

    def body(q_ref, k_ref, pf_ref, dqx_ref, dkx_ref, dvx_ref, gq_ref, gk_ref, fb_ref, bd_ref, triu_ref,
             dpa_ref, dpf_ref, dgq_ref, dgk_ref, dfb_ref, carry):
        @pl.when(pl.program_id(0) == 0)
        def _():
            carry[...] = jnp.zeros_like(carry)
            dgq_ref[...] = jnp.zeros_like(dgq_ref)
            dgk_ref[...] = jnp.zeros_like(dgk_ref)
            dfb_ref[...] = jnp.zeros_like(dfb_ref)

        lane = lax.broadcasted_iota(jnp.int32, (TM, LANES), 1)
        columns = _heads_as_columns

        def head_norm_bwd(x_ref, dy, g_ref, dg_ref):
            xf = x_ref[...].astype(F32)
            r = lax.rsqrt(_dot((xf * xf).astype(BF16), bd_ref[...]) + EPS)
            xhat = xf * r
            dg_ref[...] += jnp.sum(dy * xhat, axis=0, keepdims=True)
            gdy = dy * g_ref[...]
            return (r * (gdy - xhat * _dot((xhat * gdy).astype(BF16), bd_ref[...]))).astype(BF16)

        dpa_ref[:, 0:D_ATTN] = head_norm_bwd(q_ref, columns(dqx_ref) * HEAD_DIM ** -0.5, gq_ref, dgq_ref)
        dpa_ref[:, D_ATTN:2 * D_ATTN] = head_norm_bwd(k_ref, columns(dkx_ref) * LN2, gk_ref, dgk_ref)
        dpa_ref[:, 2 * D_ATTN:3 * D_ATTN] = columns(dvx_ref).astype(BF16)

        dc = jnp.zeros((TM, LANES), F32)
        for h in range(N_HEADS):
            both = jnp.where(lane == LANE_C, dqx_ref[h], 0.0) - jnp.where(lane == LANE_ONE, dkx_ref[h], 0.0)
            dc = jnp.where(lane == h, jnp.sum(both, axis=1, keepdims=True), dc)
        hi, lo = _split_bf16(dc)
        dlogf = _dot(triu_ref[...], hi) + _dot(triu_ref[...], lo) + carry[...]
        first = lax.broadcasted_iota(jnp.int32, (TM, D_F), 0) == 0
        carry[...] = jnp.sum(jnp.where(first, dlogf, 0.0), axis=0, keepdims=True)
        df = dlogf * _sigmoid(-(pf_ref[...] + fb_ref[...]))
        dfb_ref[...] += jnp.sum(df, axis=0, keepdims=True)
        dpf_ref[...] = df.astype(BF16)

    heads = pl.BlockSpec((N_HEADS, TM, LANES), lambda i: (0, rev(i), 0))
    return pl.pallas_call(
        body,
        grid=(nt,),
        in_specs=[pl.BlockSpec((TM, D_ATTN), lambda i: (rev(i), 0)), pl.BlockSpec((TM, D_ATTN), lambda i: (rev(i), 1)),
                  pl.BlockSpec((TM, D_F), lambda i: (rev(i), 0)), heads, heads, heads,
                  _whole(gq), _whole(gk), _whole(fb), _whole(bd), _whole(triu)],
        out_specs=[pl.BlockSpec((TM, D_QKV), lambda i: (rev(i), 0)), pl.BlockSpec((TM, D_F), lambda i: (rev(i), 0)),
                   pl.BlockSpec((1, D_ATTN), lambda i: (0, 0)), pl.BlockSpec((1, D_ATTN), lambda i: (0, 0)),
                   pl.BlockSpec((1, D_F), lambda i: (0, 0))],
        out_shape=[jax.ShapeDtypeStruct((S, D_QKV), BF16), jax.ShapeDtypeStruct((S, D_F), BF16),
                   jax.ShapeDtypeStruct((1, D_ATTN), F32), jax.ShapeDtypeStruct((1, D_ATTN), F32),
                   jax.ShapeDtypeStruct((1, D_F), F32)],
        scratch_shapes=[pltpu.VMEM((1, D_F), F32)],
        compiler_params=_params("arbitrary"),
        name="attn_bwd_post",
    )(pa, pa, pf, dqx, dkx, dvx, gq, gk, fb, bd, triu)


def _shift_down(ext, k):
    return pltpu.roll(ext, k, 0)[HALO:]


def _shift_up(ext, k, n):
    return pltpu.roll(ext, n + HALO - k, 0)[:n]


def _pool_lane_select(a2, a4, a8, a16, lane):
    return jnp.where(lane < 64, a2, jnp.where(lane < 128, a4, jnp.where(lane < 192, a8, a16)))


def _local_branches(o, pb, halo, have_prev, row0, wao, convw, wco, wpool, pscale):
    n = pb.shape[0]
    cx = pb[:, 0:D_CONV].astype(F32)
    cb = pb[:, D_CONV:2 * D_CONV].astype(F32)
    cc = pb[:, 2 * D_CONV:3 * D_CONV].astype(F32)
    px = pb[:, 3 * D_CONV:D_LOCAL].astype(F32)
    keep = have_prev.astype(F32)
    z = cc * cx
    z_ext = jnp.concatenate([halo[:, 2 * D_CONV:3 * D_CONV].astype(F32) * halo[:, 0:D_CONV].astype(F32) * keep, z], axis=0)
    z1 = _shift_down(z_ext, 1)
    z2 = _shift_down(z_ext, 2)
    conv = convw[0:1, :] * z2 + convw[1:2, :] * z1 + convw[2:3, :] * z
    cm = cb * conv

    u_ext = jnp.concatenate([halo[:, 3 * D_CONV:D_LOCAL].astype(F32) * keep, px], axis=0)
    s2 = u_ext + pltpu.roll(u_ext, 1, 0)
    s4 = s2 + pltpu.roll(s2, 2, 0)
    s8 = s4 + pltpu.roll(s4, 4, 0)
    s16 = s8 + pltpu.roll(s8, 8, 0)
    lane = lax.broadcasted_iota(jnp.int32, (n, D_POOL), 1)
    win = _pool_lane_select(2.0, 4.0, 8.0, 16.0, lane)
    t = (row0 + lax.broadcasted_iota(jnp.int32, (n, D_POOL), 0)).astype(F32)
    cnt = jnp.minimum(t + 1.0, win)
    feat = _pool_lane_select(s2[HALO:], s4[HALO:], s8[HALO:], s16[HALO:], lane) / cnt - px

    ya = _dot(o, wao)
    yc = _dot(cm.astype(BF16), wco)
    yp_pre = _dot(feat.astype(BF16), wpool)
    yp = yp_pre * pscale
    return dict(cx=cx, cb=cb, cc=cc, z=z, z1=z1, z2=z2, conv=conv, cm=cm, feat=feat, cnt=cnt, lane=lane,
                ya=ya, yc=yc, yp_pre=yp_pre, yp=yp)


def _halo_spec(tm, tile_of):
    per = tm // HALO
    return pl.BlockSpec((HALO, D_LOCAL), lambda i, *_: (jnp.maximum(tile_of(i) * per - 1, 0), 0))


def _heads_as_columns(ref):
    lane = lax.broadcasted_iota(jnp.int32, ref.shape[1:], 1)
    return jnp.concatenate([jnp.where(lane < HEAD_DIM, ref[2 * p], pltpu.roll(ref[2 * p + 1], HEAD_DIM, 1))
                            for p in range(N_HEADS // 2)], axis=1)


def mix_out_fwd(x, ox, pb, wao, convw, wco, wpool, pscale, wo_all, l):
    S = x.shape[0]
    tm = TM_MIX

    def body(l_ref, x_ref, o_ref, pb_ref, halo_ref, wao_ref, cw_ref, wco_ref, wp_ref, ps_ref, wo_ref, y_ref):
        i = pl.program_id(0)
        pb = pb_ref[...]
        o = _heads_as_columns(o_ref).astype(BF16)
        b = _local_branches(o, pb, halo_ref[...], i > 0, i * tm, wao_ref[...], cw_ref[...], wco_ref[...],
                            wp_ref[...], ps_ref[...])
        g0 = _sigmoid(pb[:, D_LOCAL:D_LOCAL + D_MODEL].astype(F32))
        g1 = _sigmoid(pb[:, D_LOCAL + D_MODEL:D_LOCAL + 2 * D_MODEL].astype(F32))
        g2 = _sigmoid(pb[:, D_LOCAL + 2 * D_MODEL:D_B].astype(F32))
        merged = g0 * b["ya"] + g1 * b["yc"] + g2 * b["yp"]
        y_ref[...] = x_ref[...] + _dot(merged.astype(BF16), wo_ref[...])

    return pl.pallas_call(
        body,
        grid_spec=pltpu.PrefetchScalarGridSpec(
            num_scalar_prefetch=1,
            grid=(S // tm,),
            in_specs=[_rows(tm, D_MODEL), pl.BlockSpec((N_HEADS, tm, LANES), lambda i, l: (0, i, 0)), _rows(tm, D_B),
                      _halo_spec(tm, lambda i: i), _whole(wao), _whole(convw), _whole(wco), _whole(wpool), _whole(pscale),
                      _layer((D_MODEL, D_MODEL))],
            out_specs=_rows(tm, D_MODEL),
        ),
        out_shape=jax.ShapeDtypeStruct((S, D_MODEL), F32),
        compiler_params=_params("parallel"),
        name="mix_out_fwd",
    )(_layer_index(l), x, ox, pb, pb, wao, convw, wco, wpool, pscale, wo_all)


def mix_out_bwd(dxm, ox, pb, wao, convw, wco, wpool, pscale, wo_all, l):
    S = dxm.shape[0]
    tm = TM_MIX
    nt = S // tm
    rev = lambda i: nt - 1 - i
    rows = lambda n: pl.BlockSpec((tm, n), lambda i, l: (rev(i), 0))
    heads = pl.BlockSpec((N_HEADS, tm, LANES), lambda i, l: (0, rev(i), 0))
    acc = lambda r, c: pl.BlockSpec((r, c), lambda i, l: (0, 0))

    def body(l_ref, dxm_ref, o_ref, pb_ref, halo_ref, wao_ref, cw_ref, wco_ref, wp_ref, ps_ref, wo_ref,
             dpb_ref, do_ref, dwo_ref, dwao_ref, dwco_ref, dwp_ref, dcw_ref, dps_ref, next_dconv, next_e):
        i = pl.program_id(0)
        r = rev(i)

        @pl.when(i == 0)
        def _():
            for ref in (dwo_ref, dwao_ref, dwco_ref, dwp_ref, dcw_ref, dps_ref, next_dconv, next_e):
                ref[...] = jnp.zeros_like(ref)

        pb = pb_ref[...]
        o = _heads_as_columns(o_ref).astype(BF16)
        cw = cw_ref[...]
        b = _local_branches(o, pb, halo_ref[...], r > 0, r * tm, wao_ref[...], cw, wco_ref[...], wp_ref[...], ps_ref[...])
        g0 = _sigmoid(pb[:, D_LOCAL:D_LOCAL + D_MODEL].astype(F32))
        g1 = _sigmoid(pb[:, D_LOCAL + D_MODEL:D_LOCAL + 2 * D_MODEL].astype(F32))
        g2 = _sigmoid(pb[:, D_LOCAL + 2 * D_MODEL:D_B].astype(F32))
        dxb = dxm_ref[...].astype(BF16)
        merged = g0 * b["ya"] + g1 * b["yc"] + g2 * b["yp"]
        dwo_ref[...] += _dot_tn(merged.astype(BF16), dxb)
        dmer = _dot_nt(dxb, wo_ref[...])
        dpb_ref[:, D_LOCAL:D_LOCAL + D_MODEL] = (dmer * b["ya"] * (g0 * (1.0 - g0))).astype(BF16)
        dpb_ref[:, D_LOCAL + D_MODEL:D_LOCAL + 2 * D_MODEL] = (dmer * b["yc"] * (g1 * (1.0 - g1))).astype(BF16)
        dpb_ref[:, D_LOCAL + 2 * D_MODEL:D_B] = (dmer * b["yp"] * (g2 * (1.0 - g2))).astype(BF16)

        dya = (dmer * g0).astype(BF16)
        dwao_ref[...] += _dot_tn(o, dya)
        da = _dot_nt(dya, wao_ref[...])
        lane = lax.broadcasted_iota(jnp.int32, (tm, LANES), 1)
        for h in range(N_HEADS):
            two = da[:, (h // 2) * LANES:(h // 2 + 1) * LANES]
            dah = jnp.where(lane < HEAD_DIM, pltpu.roll(two, HEAD_DIM, 1) if h % 2 else two, 0.0).astype(BF16)
            d1, d2, d3 = _pieces(-jnp.sum(dah.astype(F32) * o_ref[h], axis=1, keepdims=True))
            do_ref[h] = jnp.where(lane == LANE_C + 1, d1, jnp.where(lane == LANE_C + 2, d2,
                                                                  jnp.where(lane == LANE_C + 3, d3, dah)))

        dyc = (dmer * g1).astype(BF16)
        dwco_ref[...] += _dot_tn(b["cm"].astype(BF16), dyc)
        dcm = _dot_nt(dyc, wco_ref[...])
        dconv = dcm * b["cb"]
        dcw_ref[0:1, :] += jnp.sum(dconv * b["z2"], axis=0, keepdims=True)
        dcw_ref[1:2, :] += jnp.sum(dconv * b["z1"], axis=0, keepdims=True)
        dcw_ref[2:3, :] += jnp.sum(dconv * b["z"], axis=0, keepdims=True)
        d_ext = jnp.concatenate([dconv, next_dconv[...]], axis=0)
        dz = cw[2:3, :] * dconv + cw[1:2, :] * _shift_up(d_ext, 1, tm) + cw[0:1, :] * _shift_up(d_ext, 2, tm)
        next_dconv[...] = dconv[0:HALO]
        dpb_ref[:, 0:D_CONV] = (dz * b["cc"]).astype(BF16)
        dpb_ref[:, D_CONV:2 * D_CONV] = (dcm * b["conv"]).astype(BF16)
        dpb_ref[:, 2 * D_CONV:3 * D_CONV] = (dz * b["cx"]).astype(BF16)

        dyp = dmer * g2
        dps_ref[...] += jnp.sum(dyp * b["yp_pre"], axis=0, keepdims=True)
        dyps = (dyp * ps_ref[...]).astype(BF16)
        dwp_ref[...] += _dot_tn(b["feat"].astype(BF16), dyps)
        dfeat = _dot_nt(dyps, wp_ref[...])
        e = dfeat / b["cnt"]
        e_ext = jnp.concatenate([e, next_e[...]], axis=0)
        up = lambda a, k: pltpu.roll(a, tm + HALO - k, 0)
        f2 = e_ext + up(e_ext, 1)
        f4 = f2 + up(f2, 2)
        f8 = f4 + up(f4, 4)
        f16 = f8 + up(f8, 8)
        next_e[...] = e[0:HALO]
        dpb_ref[:, 3 * D_CONV:D_LOCAL] = (_pool_lane_select(f2[:tm], f4[:tm], f8[:tm], f16[:tm], b["lane"]) - dfeat).astype(BF16)

    return pl.pallas_call(
        body,
        grid_spec=pltpu.PrefetchScalarGridSpec(
            num_scalar_prefetch=1,
            grid=(nt,),
            in_specs=[rows(D_MODEL), heads, rows(D_B), _halo_spec(tm, rev),
                      _whole(wao), _whole(convw), _whole(wco), _whole(wpool), _whole(pscale), _layer((D_MODEL, D_MODEL))],
            out_specs=[rows(D_B), heads, acc(D_MODEL, D_MODEL), acc(D_ATTN, D_MODEL), acc(D_CONV, D_MODEL),
                       acc(D_POOL, D_MODEL), acc(8, D_CONV), acc(1, D_MODEL)],
            scratch_shapes=[pltpu.VMEM((HALO, D_CONV), F32), pltpu.VMEM((HALO, D_POOL), F32)],
        ),
        out_shape=[jax.ShapeDtypeStruct((S, D_B), BF16), jax.ShapeDtypeStruct((N_HEADS, S, LANES), BF16),
                   jax.ShapeDtypeStruct((D_MODEL, D_MODEL), F32), jax.ShapeDtypeStruct((D_ATTN, D_MODEL), F32),
                   jax.ShapeDtypeStruct((D_CONV, D_MODEL), F32), jax.ShapeDtypeStruct((D_POOL, D_MODEL), F32),
                   jax.ShapeDtypeStruct((8, D_CONV), F32), jax.ShapeDtypeStruct((1, D_MODEL), F32)],
        compiler_params=_params("arbitrary"),
        name="mix_out_bwd",
    )(_layer_index(l), dxm, ox, pb, pb, wao, convw, wco, wpool, pscale, wo_all)


FF_SHARD = 2 * D_FF // 4


def ffn_fwd(x, g, w1_all, w2_all, l):
    S = x.shape[0]

    def body(l_ref, x_ref, g_ref, w1_ref, w2_ref, y_ref, u_ref):
        xf = x_ref[...]
        r = lax.rsqrt(jnp.mean(xf * xf, axis=-1, keepdims=True) + EPS)
        h = (xf * r * g_ref[...]).astype(BF16)
        u = jnp.concatenate([_dot(h, w1_ref[j]) for j in range(4)], axis=1)
        u_ref[...] = u.astype(BF16)
        gt = u[:, 0:D_FF]
        act = gt * _sigmoid(gt) * u[:, D_FF:2 * D_FF]
        y_ref[...] = xf + _dot(act.astype(BF16), w2_ref[...])

    return pl.pallas_call(
        body,
        grid_spec=pltpu.PrefetchScalarGridSpec(
            num_scalar_prefetch=1,
            grid=(S // TM,),
            in_specs=[_rows(TM, D_MODEL), _whole(g), _layer((4, D_MODEL, FF_SHARD)), _layer((D_FF, D_MODEL))],
            out_specs=[_rows(TM, D_MODEL), _rows(TM, 2 * D_FF)],
        ),
        out_shape=[jax.ShapeDtypeStruct((S, D_MODEL), F32), jax.ShapeDtypeStruct((S, 2 * D_FF), BF16)],
        compiler_params=_params("parallel"),
        name="ffn_fwd",
    )(_layer_index(l), x, g, w1_all, w2_all)


def ffn_bwd(x, g, dy, u, w1_all, w2_all, l):
    S = x.shape[0]

    def body(l_ref, x_ref, g_ref, dy_ref, u_ref, w1_ref, w2_ref, dx_ref, du_ref, act_ref, h_ref, dyb_ref, dg_ref):
        @pl.when(pl.program_id(0) == 0)
        def _():
            dg_ref[...] = jnp.zeros_like(dg_ref)

        dyf = dy_ref[...]
        dyb_ref[...] = dyf.astype(BF16)
        dact = _dot_nt(dyb_ref[...], w2_ref[...])
        gt = u_ref[:, 0:D_FF].astype(F32)
        up = u_ref[:, D_FF:2 * D_FF].astype(F32)
        sg = _sigmoid(gt)
        silu = gt * sg
        act_ref[...] = (silu * up).astype(BF16)
        du_ref[:, 0:D_FF] = (dact * up * (sg * (1.0 + gt * (1.0 - sg)))).astype(BF16)
        du_ref[:, D_FF:2 * D_FF] = (dact * silu).astype(BF16)
        dh = _dot_nt(du_ref[:, 0:FF_SHARD], w1_ref[0])
        for j in range(1, 4):
            dh = dh + _dot_nt(du_ref[:, j * FF_SHARD:(j + 1) * FF_SHARD], w1_ref[j])
        xf = x_ref[...]
        r = lax.rsqrt(jnp.mean(xf * xf, axis=-1, keepdims=True) + EPS)
        xhat = xf * r
        h_ref[...] = (xhat * g_ref[...]).astype(BF16)
        dg_ref[...] += jnp.sum(dh * xhat, axis=0, keepdims=True)
        gdh = dh * g_ref[...]
        dx_ref[...] = dyf + r * (gdh - xhat * jnp.mean(xhat * gdh, axis=-1, keepdims=True))

    return pl.pallas_call(
        body,
        grid_spec=pltpu.PrefetchScalarGridSpec(
            num_scalar_prefetch=1,
            grid=(S // TM,),
            in_specs=[_rows(TM, D_MODEL), _whole(g), _rows(TM, D_MODEL), _rows(TM, 2 * D_FF),
                      _layer((4, D_MODEL, FF_SHARD)), _layer((D_FF, D_MODEL))],
            out_specs=[_rows(TM, D_MODEL), _rows(TM, 2 * D_FF), _rows(TM, D_FF), _rows(TM, D_MODEL), _rows(TM, D_MODEL),
                       pl.BlockSpec((1, D_MODEL), lambda i, l: (0, 0))],
        ),
        out_shape=[jax.ShapeDtypeStruct((S, D_MODEL), F32), jax.ShapeDtypeStruct((S, 2 * D_FF), BF16),
                   jax.ShapeDtypeStruct((S, D_FF), BF16), jax.ShapeDtypeStruct((S, D_MODEL), BF16),
                   jax.ShapeDtypeStruct((S, D_MODEL), BF16), jax.ShapeDtypeStruct((1, D_MODEL), F32)],
        compiler_params=_params("arbitrary"),
        name="ffn_bwd",
    )(_layer_index(l), x, g, dy, u, w1_all, w2_all)


def loss_head(y, target):
    S = y.shape[0]

    def body(y_ref, t_ref, loss_ref, dy_ref):
        @pl.when(pl.program_id(0) == 0)
        def _():
            loss_ref[0, 0] = 0.0

        err = y_ref[...] - t_ref[...]
        dy_ref[...] = err * (1.0 / D_MODEL)
        loss_ref[0, 0] += 0.5 * jnp.sum(jnp.mean(err * err, axis=-1))

    return pl.pallas_call(
        body,
        grid=(S // TM,),
        in_specs=[_rows(TM, D_MODEL), _rows(TM, D_MODEL)],
        out_specs=[pl.BlockSpec((1, 1), lambda i: (0, 0), memory_space=pltpu.SMEM), _rows(TM, D_MODEL)],
        out_shape=[jax.ShapeDtypeStruct((1, 1), F32), jax.ShapeDtypeStruct((S, D_MODEL), F32)],
        compiler_params=_params("arbitrary"),
        name="loss_head",
    )(y, target)


SPLIT = {
    "w_in": ((D_MODEL, D_IN), 1),
    "w_attn_out": ((D_ATTN, D_MODEL), 1),
    "w_conv_out": ((D_CONV, D_MODEL), 1),
    "pool_w": ((N_GROUPS, D_POOL // N_GROUPS, D_MODEL // N_GROUPS), 2),
    "w_o": ((D_MODEL, D_MODEL), 0),
    "w_ffn_in": ((D_MODEL, 2 * D_FF), 1),
    "w_ffn_out": ((D_FF, D_MODEL), 0),
}
SMALL = {"norm_mix_g": D_MODEL, "forget_b": N_HEADS, "q_norm_g": HEAD_DIM, "k_norm_g": HEAD_DIM,
         "pool_scale": D_MODEL, "norm_ffn_g": D_MODEL, "conv_w": CONV_K * D_CONV}
WEIGHTS = ["norm_mix_g", "w_in", "forget_b", "q_norm_g", "k_norm_g", "w_attn_out", "conv_w", "w_conv_out", "pool_w",
           "pool_scale", "w_o", "norm_ffn_g", "w_ffn_in", "w_ffn_out"]


MIX_W = ["w_in", "w_attn_out", "w_conv_out", "pool_w", "w_o"]
FFN_W = ["w_ffn_in", "w_ffn_out"]


def _row(a):
    return a.astype(F32).reshape(1, -1)


def mix_fwd(x, full, li, small):
    n = full["w_o"].shape[0]
    wa, wf, wb = w_in_prep(full["w_in"], li)
    wao, wco, wpool, convw = branch_w_prep(full["w_attn_out"], full["w_conv_out"], full["pool_w"], full["conv_w"], li)
    w = dict(wa=wa, wf=wf, wb=wb, wao=wao, wco=wco, wpool=wpool, convw=convw,
             w_o=full["w_o"].reshape(n, D_MODEL, D_MODEL), at=li,
             g1=_row(small["norm_mix_g"]), pscale=_row(small["pool_scale"]),
             gq=_row(jnp.tile(small["q_norm_g"], N_HEADS)), gk=_row(jnp.tile(small["k_norm_g"], N_HEADS)),
             fb=_row(jnp.pad(small["forget_b"], (0, D_F - N_HEADS))))
    h, pa, pf, pb = in_proj_fwd(x, w["g1"], wa, wf, wb)
    qx, kx, vx = qk_prep(pa, pf, w["gq"], w["gk"], w["fb"])
    ox, qxb = attn_fwd(qx, kx, vx)
    xm = mix_out_fwd(x, ox, pb, wao, convw, wco, wpool, w["pscale"], w["w_o"], li)
    return xm, w, dict(x=x, h=h, pa=pa, pf=pf, pb=pb, kx=kx, vx=vx, ox=ox, qxb=qxb)


def ffn_half_fwd(xm, full, li, g2):
    n = full["w_ffn_out"].shape[0]
    w = dict(w1=full["w_ffn_in"], w2=full["w_ffn_out"].reshape(n, D_FF, D_MODEL), at=li, g2=_row(g2))
    y, u = ffn_fwd(xm, w["g2"], w["w1"], w["w2"], li)
    return y, w, dict(xm=xm, u=u)


def ffn_half_bwd(dx, w, s, gi, big):
    n = big["w_ffn_out"].shape[0]
    big = dict(big)
    dxm, du, act, h2, dyb, dg2 = ffn_bwd(s["xm"], w["g2"], dx, s["u"], w["w1"], w["w2"], w["at"])
    big["w_ffn_out"] = wgrad_into(act, dyb, "wgrad_ffn_out", big["w_ffn_out"].reshape(n, D_FF, D_MODEL),
                                  gi).reshape(big["w_ffn_out"].shape)
    big["w_ffn_in"] = wgrad_into(h2, du, "wgrad_ffn_in", big["w_ffn_in"], gi)
    return dxm, big, dict(norm_ffn_g=dg2[0])


def mix_bwd_weights(dxm, w, s, gi, big):
    big = dict(big)
    dpb, dox, dwo, dwao, dwco, dwpool, dconvw, dpscale = mix_out_bwd(
        dxm, s["ox"], s["pb"], w["wao"], w["convw"], w["wco"], w["wpool"], w["pscale"], w["w_o"], w["at"])
    dqx, dkx, dvx = attn_bwd(s["qxb"], s["kx"], s["vx"], dox)
    dpa, dpf, dgq, dgk, dfb = attn_bwd_post(s["pa"], s["pf"], dqx, dkx, dvx, w["gq"], w["gk"], w["fb"])
    big["w_in"] = w_in_unprep(wgrad(s["h"], dpa, "wgrad_in_qkv"), wgrad(s["h"], dpf, "wgrad_in_f"),
                              wgrad(s["h"], dpb, "wgrad_in_b"), big["w_in"], gi)
    big["w_attn_out"], big["w_conv_out"], big["pool_w"], big["w_o"] = branch_g_place(
        dwao, dwco, dwpool, dwo, (big["w_attn_out"], big["w_conv_out"], big["pool_w"], big["w_o"]), gi)
    sm = dict(forget_b=dfb[0, 0:N_HEADS], q_norm_g=dgq.reshape(N_HEADS, HEAD_DIM).sum(0),
              k_norm_g=dgk.reshape(N_HEADS, HEAD_DIM).sum(0), pool_scale=dpscale[0],
              conv_w=dconvw[0:CONV_K].reshape(-1))
    return (dpa, dpf, dpb), big, sm


def mix_bwd_input(dproj, dxm, w, s):
    dx, dg1 = in_proj_bwd(s["x"], w["g1"], dxm, *dproj, w["wa"], w["wf"], w["wb"])
    return dx, dict(norm_mix_g=dg1[0])


def mix_bwd(dxm, w, s, gi, big):
    dproj, big, sm = mix_bwd_weights(dxm, w, s, gi, big)
    dx, g1 = mix_bwd_input(dproj, dxm, w, s)
    return dx, big, {**sm, **g1}


def grad_buffers(full):
    return {n: lax.empty(full[n].shape, F32) for n in SPLIT}


def local_step(x, target, gathered, small):
    n_layers = small["norm_mix_g"].shape[0]
    done = []
    for l in range(n_layers):
        xm, wm, sm = mix_fwd(x, gathered, l, {n: v[l] for n, v in small.items()})
        x, wf, sf = ffn_half_fwd(xm, gathered, l, small["norm_ffn_g"][l])
        done.append((wm, sm, wf, sf))
    loss, dx = loss_head(x, target)
    big = grad_buffers(gathered)
    small_grads = [None] * n_layers
    for l in reversed(range(n_layers)):
        wm, sm, wf, sf = done[l]
        dxm, ffn_big, g_ffn = ffn_half_bwd(dx, wf, sf, l, {n: big[n] for n in FFN_W})
        dx, mix_big, g_mix = mix_bwd(dxm, wm, sm, l, {n: big[n] for n in MIX_W})
        big = {**ffn_big, **mix_big}
        small_grads[l] = {**g_ffn, **g_mix}
    return loss, dx, big, {n: jnp.stack([g[n] for g in small_grads]) for n in SMALL}


def adamw(w, g, m, v, name):
    R, C = w.shape
    tm = 256 if R % 256 == 0 else R

    def body(w_ref, g_ref, m_ref, v_ref, d_ref, nm_ref, nv_ref):
        gr = g_ref[...]
        m_new = ADAM_B1 * m_ref[...] + (1.0 - ADAM_B1) * gr
        v_new = ADAM_B2 * v_ref[...] + (1.0 - ADAM_B2) * jnp.square(gr)
        nm_ref[...] = m_new
        nv_ref[...] = v_new
        m_hat = m_new / (1.0 - ADAM_B1 ** ADAM_STEP)
        v_hat = v_new / (1.0 - ADAM_B2 ** ADAM_STEP)
        d_ref[...] = -ADAM_LR * (m_hat / (jnp.sqrt(v_hat) + ADAM_EPS) + ADAM_WD * w_ref[...])

    spec = _rows(tm, C)
    out = jax.ShapeDtypeStruct((R, C), F32)
    return pl.pallas_call(
        body,
        grid=(R // tm,),
        in_specs=[spec] * 4,
        out_specs=[spec] * 3,
        out_shape=[out] * 3,
        compiler_params=_params("parallel"),
        name=name,
    )(w, g, m, v)


MESH = pl.DeviceIdType.MESH
HBM_REF = pl.BlockSpec(memory_space=pl.ANY)
N_CHIPS = 4
N_DEV = 8
SMALL_SHAPE = (128, LANES)


def _mesh_pos():
    return lax.axis_index("x"), lax.axis_index("y"), lax.axis_index("c")


def _other_chips(x, y):
    return [(1 - x, y), (x, 1 - y), (1 - x, 1 - y)]


def _remote(src, dst, send_sem, recv_sem, to):
    return pltpu.make_async_remote_copy(src_ref=src, dst_ref=dst, send_sem=send_sem, recv_sem=recv_sem,
                                        device_id=to, device_id_type=MESH)


def _row_tile(rows):
    for tm in (256, 176, 128):
        if rows % tm == 0:
            return tm
    return rows


def _as4(a):
    return a.reshape(a.shape[0], a.shape[1], -1, a.shape[-1])


def _core_rows(buf, c):
    R, C = buf.shape[-2:]
    if R % 2:
        whole = (pl.ds(0, R), pl.ds(0, C))
        return whole, whole, False
    return (pl.ds(c * (R // 2), R // 2), pl.ds(0, C)), (pl.ds((1 - c) * (R // 2), R // 2), pl.ds(0, C)), True


def _half_shape(g):
    return g.shape[:-2] + (g.shape[-2] // 2, g.shape[-1])


def place_shard(chip, w, lo, hi, dtype, name, after=None):
    w3 = w.reshape(w.shape[0], -1, w.shape[-1])
    _, R, C = w3.shape
    tm = _row_tile(R)
    idle = [] if after is None else [after]

    def body(chip_ref, w_ref, *rest):
        rest[-1][...] = w_ref[...].astype(dtype)

    return pl.pallas_call(
        body,
        grid_spec=pltpu.PrefetchScalarGridSpec(
            num_scalar_prefetch=1,
            grid=(hi - lo, R // tm),
            in_specs=[pl.BlockSpec((None, tm, C), lambda l, i, chip: (lo + l, i, 0))] + [_whole(a) for a in idle],
            out_specs=pl.BlockSpec((None, None, tm, C), lambda l, i, chip: (l, chip[0], i, 0)),
        ),
        out_shape=jax.ShapeDtypeStruct((hi - lo, N_CHIPS, R, C), dtype),
        compiler_params=_params("parallel", "parallel"),
        name=name,
    )(chip, w3, *idle)


HBM_SPACE = pl.BlockSpec(memory_space=pltpu.HBM)
SEM_SPACE = pl.BlockSpec(memory_space=pltpu.SEMAPHORE)
IN_FLIGHT = pltpu.SideEffectType.DATAFLOW_SIDE_EFFECTING


def _sem_table(send_sems, recv_sems):
    return lambda t, j: (send_sems.at[t, j], recv_sems.at[t, j])


def _sem_per_peer(send_sems, recv_sems):
    return lambda t, j: (send_sems[j], recv_sems[j])


def _gather_ici(bufs, sem, sends=True, lands=True):
    x, y, c = _mesh_pos()
    me = 2 * x + y
    out, into = [], []
    for t, buf in enumerate(bufs):
        mine, _, _ = _core_rows(buf, c)
        part = lambda k, buf=buf, mine=mine: buf.at[pl.ds(0, buf.shape[0]), k, *mine]
        for j, (px, py) in enumerate(_other_chips(x, y)):
            if sends:
                out.append(_remote(part(me), part(me), *sem(t, j), (px, py, c)))
            if lands:
                into.append(_remote(part(2 * px + py), part(2 * px + py), *sem(t, j), (px, py, c)))
    return out, into


def _gather_d2d(bufs, send_sems, recv_sems, first):
    x, y, c = _mesh_pos()
    sends, lands = [], []
    for t, buf in enumerate(bufs):
        mine, theirs, split = _core_rows(buf, c)
        if not split:
            continue
        for j, (px, py) in enumerate(_other_chips(x, y)):
            part = lambda half, buf=buf, k=2 * px + py: buf.at[pl.ds(0, buf.shape[0]), k, *half]
            sems = (send_sems.at[t, first + j], recv_sems.at[t, first + j], (x, y, 1 - c))
            sends.append(_remote(part(mine), part(mine), *sems))
            lands.append(_remote(part(theirs), part(theirs), *sems))
    return sends, lands


def gather_shards(bufs):
    n = len(bufs)

    def body(*refs):
        outs = refs[n:2 * n]
        send_sems, recv_sems = refs[2 * n:]
        ici_out, ici_in = _gather_ici(outs, _sem_table(send_sems, recv_sems))
        d2d_out, d2d_in = _gather_d2d(outs, send_sems, recv_sems, 3)
        for cp in ici_out:
            cp.start()
        for cp in ici_in:
            cp.wait_recv()
        for cp in d2d_out:
            cp.start()
        for cp in d2d_in:
            cp.wait_recv()
        for cp in ici_out + d2d_out:
            cp.wait_send()

    return pl.pallas_call(
        body,
        in_specs=[HBM_REF] * n,
        out_specs=[HBM_REF] * n,
        out_shape=[jax.ShapeDtypeStruct(b.shape, b.dtype) for b in bufs],
        input_output_aliases={t: t for t in range(n)},
        scratch_shapes=[pltpu.SemaphoreType.DMA((n, 6)), pltpu.SemaphoreType.DMA((n, 6))],
        name="gather_shards",
    )(*bufs)


def gather_start(bufs):
    n = len(bufs)

    def body(*refs):
        send_sems, recv_sems = refs[n:n + 3], refs[n + 3:n + 6]
        outs = refs[n + 6:2 * n + 6]
        token = refs[2 * n + 6]
        for cp in _gather_ici(outs, _sem_per_peer(send_sems, recv_sems), lands=False)[0]:
            cp.start()
        token[...] = jnp.zeros_like(token)

    res = pl.pallas_call(
        body,
        in_specs=[HBM_SPACE] * n,
        out_specs=[SEM_SPACE] * 6 + [HBM_SPACE] * n + [pl.BlockSpec(memory_space=pltpu.VMEM)],
        out_shape=[pltpu.SemaphoreType.DMA(())] * 6
        + [pltpu.HBM(b.shape, b.dtype) for b in bufs] + [jax.ShapeDtypeStruct((8, LANES), F32)],
        input_output_aliases={t: t + 6 for t in range(n)},
        compiler_params=pltpu.CompilerParams(has_side_effects=IN_FLIGHT),
        name="gather_start",
    )(*[pltpu.with_memory_space_constraint(b, pltpu.HBM) for b in bufs])
    return list(res[0:3]), list(res[3:6]), list(res[6:n + 6]), res[n + 6]


def gather_wait(send_sems, recv_sems, bufs, after):
    n = len(bufs)

    def body(*refs):
        ins_send, ins_recv = refs[n:n + 3], refs[n + 3:n + 6]
        outs = refs[n + 7:]
        sends, lands = _gather_ici(outs, _sem_per_peer(ins_send, ins_recv))
        for cp in lands:
            cp.wait_recv()
        for cp in sends:
            cp.wait_send()

    return list(pl.pallas_call(
        body,
        in_specs=[HBM_SPACE] * n + [SEM_SPACE] * 6 + [HBM_REF],
        out_specs=[HBM_SPACE] * n,
        out_shape=[pltpu.HBM(b.shape, b.dtype) for b in bufs],
        input_output_aliases={t: t for t in range(n)},
        compiler_params=pltpu.CompilerParams(has_side_effects=IN_FLIGHT),
        name="gather_wait",
    )(*bufs, *send_sems, *recv_sems, after))


def gather_forward(bufs):
    n = len(bufs)

    def body(*refs):
        outs = refs[n:2 * n]
        send_sems, recv_sems = refs[2 * n:]
        sends, lands = _gather_d2d(outs, send_sems, recv_sems, 0)
        for cp in sends:
            cp.start()
        for cp in lands:
            cp.wait_recv()
        for cp in sends:
            cp.wait_send()

    return list(pl.pallas_call(
        body,
        in_specs=[HBM_REF] * n,
        out_specs=[HBM_REF] * n,
        out_shape=[jax.ShapeDtypeStruct(b.shape, b.dtype) for b in bufs],
        input_output_aliases={t: t for t in range(n)},
        scratch_shapes=[pltpu.SemaphoreType.DMA((n, 3)), pltpu.SemaphoreType.DMA((n, 3))],
        name="gather_forward",
    )(*bufs))


def pair_exchange(grads):
    n = len(grads)

    def body(*refs):
        ins, outs = refs[:n], refs[n:2 * n]
        send_sems, recv_sems = refs[2 * n:]
        cps = _pair_copies(ins, outs, lambda t: (send_sems.at[t], recv_sems.at[t]))
        for cp in cps:
            cp.start()
        for cp in cps:
            cp.wait()

    return list(pl.pallas_call(
        body,
        in_specs=[HBM_REF] * n,
        out_specs=[HBM_REF] * n,
        out_shape=[jax.ShapeDtypeStruct(_half_shape(g), g.dtype) for g in grads],
        scratch_shapes=[pltpu.SemaphoreType.DMA((n,)), pltpu.SemaphoreType.DMA((n,))],
        name="pair_exchange",
    )(*grads))


def _pair_copies(grads, lands, sem):
    x, y, c = _mesh_pos()
    cps = []
    for t in range(len(grads)):
        _, theirs, _ = _core_rows(grads[t], c)
        src = grads[t].at[pl.ds(0, grads[t].shape[0]), pl.ds(0, N_CHIPS), *theirs]
        cps.append(_remote(src, lands[t], *sem(t), (x, y, 1 - c)))
    return cps


def pair_exchange_start(grads):
    n = len(grads)
    lands = [lax.empty(_half_shape(g), g.dtype) for g in grads]

    def body(*refs):
        send_sem, recv_sem = refs[2 * n:2 * n + 2]
        outs = refs[2 * n + 2:4 * n + 2]
        token = refs[4 * n + 2]
        for cp in _pair_copies(outs[:n], outs[n:], lambda t: (send_sem, recv_sem)):
            cp.start()
        token[...] = jnp.zeros_like(token)

    res = pl.pallas_call(
        body,
        in_specs=[HBM_SPACE] * (2 * n),
        out_specs=[SEM_SPACE] * 2 + [HBM_SPACE] * (2 * n) + [pl.BlockSpec(memory_space=pltpu.VMEM)],
        out_shape=[pltpu.SemaphoreType.DMA(())] * 2 + [pltpu.HBM(a.shape, a.dtype) for a in list(grads) + lands]
        + [jax.ShapeDtypeStruct((8, LANES), F32)],
        input_output_aliases={t: t + 2 for t in range(2 * n)},
        compiler_params=pltpu.CompilerParams(has_side_effects=IN_FLIGHT),
        name="pair_exchange_start",
    )(*[pltpu.with_memory_space_constraint(a, pltpu.HBM) for a in list(grads) + lands])
    return res[0], res[1], list(res[2:n + 2]), list(res[n + 2:2 * n + 2]), res[2 * n + 2]


def pair_exchange_wait(send_sem, recv_sem, grads, lands, after):
    n = len(grads)

    def body(*refs):
        in_send, in_recv = refs[2 * n:2 * n + 2]
        outs = refs[2 * n + 3:]
        for cp in _pair_copies(outs[:n], outs[n:], lambda t: (in_send, in_recv)):
            cp.wait()

    res = pl.pallas_call(
        body,
        in_specs=[HBM_SPACE] * (2 * n) + [SEM_SPACE] * 2 + [HBM_REF],
        out_specs=[HBM_SPACE] * (2 * n),
        out_shape=[pltpu.HBM(a.shape, a.dtype) for a in list(grads) + list(lands)],
        input_output_aliases={t: t for t in range(2 * n)},
        compiler_params=pltpu.CompilerParams(has_side_effects=IN_FLIGHT),
        name="pair_exchange_wait",
    )(*grads, *lands, send_sem, recv_sem, after)
    return list(res[:n]), list(res[n:])


def pair_sum(core, g, t, name):
    n, _, hr, hc = t.shape
    R, C = g.shape[2:]
    tm = _row_tile(hr)
    per = hr // tm

    def body(c_ref, g_ref, t_ref, o_ref):
        o_ref[...] = (g_ref[...] + t_ref[...]).astype(BF16)

    tile = pl.BlockSpec((None, tm, hc), lambda a, i, c: (a, i, 0))
    out = pl.pallas_call(
        body,
        grid_spec=pltpu.PrefetchScalarGridSpec(
            num_scalar_prefetch=1,
            grid=(n * N_CHIPS, per),
            in_specs=[pl.BlockSpec((None, tm, hc), lambda a, i, c: (a, per * c[0] + i, 0)), tile],
            out_specs=tile,
        ),
        out_shape=jax.ShapeDtypeStruct((n * N_CHIPS, hr, hc), BF16),
        compiler_params=_params("parallel", "parallel"),
        name=name,
    )(core, g.reshape(n * N_CHIPS, R, C), t.reshape(n * N_CHIPS, hr, hc))
    return out.reshape(t.shape)


def _chip_copies(sums, recv, sem):
    x, y, c = _mesh_pos()
    cps = []
    for t in range(len(sums)):
        for j, (px, py) in enumerate(_other_chips(x, y)):
            src = sums[t].at[pl.ds(0, sums[t].shape[0]), 2 * px + py]
            cps.append(_remote(src, recv[t].at[j], *sem(t, j), (px, py, c)))
    return cps


def _recv_shape(s):
    return (N_CHIPS - 1, s.shape[0]) + s.shape[2:]


def chip_exchange(sums):
    n = len(sums)

    def body(*refs):
        ins, outs = refs[:n], refs[n:2 * n]
        cps = _chip_copies(ins, outs, _sem_table(*refs[2 * n:]))
        for cp in cps:
            cp.start()
        for cp in cps:
            cp.wait()

    return list(pl.pallas_call(
        body,
        in_specs=[HBM_REF] * n,
        out_specs=[HBM_REF] * n,
        out_shape=[jax.ShapeDtypeStruct(_recv_shape(s), s.dtype) for s in sums],
        scratch_shapes=[pltpu.SemaphoreType.DMA((n, 3)), pltpu.SemaphoreType.DMA((n, 3))],
        name="chip_exchange",
    )(*sums))


def chip_exchange_start(sums):
    n = len(sums)
    lands = [lax.empty(_recv_shape(s), s.dtype) for s in sums]

    def body(*refs):
        send_sems, recv_sems = refs[2 * n:2 * n + 3], refs[2 * n + 3:2 * n + 6]
        outs = refs[2 * n + 6:4 * n + 6]
        token = refs[4 * n + 6]
        for cp in _chip_copies(outs[:n], outs[n:2 * n], _sem_per_peer(send_sems, recv_sems)):
            cp.start()
        token[...] = jnp.zeros_like(token)

    res = pl.pallas_call(
        body,
        in_specs=[HBM_SPACE] * (2 * n),
        out_specs=[SEM_SPACE] * 6 + [HBM_SPACE] * (2 * n) + [pl.BlockSpec(memory_space=pltpu.VMEM)],
        out_shape=[pltpu.SemaphoreType.DMA(())] * 6 + [pltpu.HBM(a.shape, a.dtype) for a in list(sums) + lands]
        + [jax.ShapeDtypeStruct((8, LANES), F32)],
        input_output_aliases={t: t + 6 for t in range(2 * n)},
        compiler_params=pltpu.CompilerParams(has_side_effects=IN_FLIGHT),
        name="chip_exchange_start",
    )(*[pltpu.with_memory_space_constraint(a, pltpu.HBM) for a in list(sums) + lands])
    return list(res[0:3]), list(res[3:6]), list(res[6:n + 6]), list(res[n + 6:2 * n + 6]), res[2 * n + 6]


def chip_exchange_wait(send_sems, recv_sems, sums, lands, after):
    n = len(sums)

    def body(*refs):
        ins_send, ins_recv = refs[2 * n:2 * n + 3], refs[2 * n + 3:2 * n + 6]
        outs = refs[2 * n + 7:]
        for cp in _chip_copies(outs[:n], outs[n:], _sem_per_peer(ins_send, ins_recv)):
            cp.wait()

    res = pl.pallas_call(
        body,
        in_specs=[HBM_SPACE] * (2 * n) + [SEM_SPACE] * 6 + [HBM_REF],
        out_specs=[HBM_SPACE] * (2 * n),
        out_shape=[pltpu.HBM(a.shape, a.dtype) for a in list(sums) + list(lands)],
        input_output_aliases={t: t for t in range(2 * n)},
        compiler_params=pltpu.CompilerParams(has_side_effects=IN_FLIGHT),
        name="chip_exchange_wait",
    )(*sums, *lands, *send_sems, *recv_sems, after)
    return list(res[:n]), list(res[n:])


def chip_sum_into(core, chip, recv, sums, total, lo, name):
    _, n, hr, hc = recv.shape
    tm = _row_tile(hr)
    per = hr // tm

    def body(c_ref, k_ref, r_ref, s_ref, t_ref, o_ref):
        acc = s_ref[...].astype(F32)
        for j in range(N_CHIPS - 1):
            acc = acc + r_ref[j].astype(F32)
        o_ref[...] = acc

    return pl.pallas_call(
        body,
        grid_spec=pltpu.PrefetchScalarGridSpec(
            num_scalar_prefetch=2,
            grid=(n, per),
            in_specs=[pl.BlockSpec((N_CHIPS - 1, None, tm, hc), lambda a, i, c, k: (0, a, i, 0)),
                      pl.BlockSpec((None, None, tm, hc), lambda a, i, c, k: (a, k[0], i, 0)),
                      HBM_REF],
            out_specs=pl.BlockSpec((None, tm, hc), lambda a, i, c, k: (lo + a, per * c[0] + i, 0)),
        ),
        out_shape=jax.ShapeDtypeStruct(total.shape, F32),
        input_output_aliases={4: 0},
        compiler_params=_params("parallel", "parallel"),
        name=name,
    )(core, chip, recv, sums, total)


def sibling_share(totals):
    n = len(totals)

    def body(*refs):
        outs = refs[n:2 * n]
        send_sems, recv_sems = refs[2 * n:]
        x, y, c = _mesh_pos()
        half = lambda t, which: outs[t].at[pl.ds(0, DEPTH), *_core_rows(outs[t], c)[which]]
        sent = [_remote(half(t, 0), half(t, 0), send_sems.at[t], recv_sems.at[t], (x, y, 1 - c)) for t in range(n)]
        for cp in sent:
            cp.start()
        for t in range(n):
            _remote(half(t, 1), half(t, 1), send_sems.at[t], recv_sems.at[t], (x, y, 1 - c)).wait_recv()
        for cp in sent:
            cp.wait_send()

    return pl.pallas_call(
        body,
        in_specs=[HBM_REF] * n,
        out_specs=[HBM_REF] * n,
        out_shape=[jax.ShapeDtypeStruct(t.shape, t.dtype) for t in totals],
        input_output_aliases={t: t for t in range(n)},
        scratch_shapes=[pltpu.SemaphoreType.DMA((n,)), pltpu.SemaphoreType.DMA((n,))],
        name="sibling_share",
    )(*totals)


def small_allgather(small):
    def body(s_ref, a_ref, send_sems, recv_sems, local_sem):
        x, y, c = _mesh_pos()
        me = 4 * x + 2 * y + c
        own = pltpu.make_async_copy(s_ref, a_ref.at[me], local_sem)
        own.start()
        sent = []
        for k in range(1, N_DEV):
            peer = (x ^ (k >> 2), y ^ ((k >> 1) & 1), c ^ (k & 1))
            cp = _remote(s_ref, a_ref.at[me], send_sems.at[k - 1], recv_sems.at[k - 1], peer)
            cp.start()
            sent.append(cp)
        for k in range(1, N_DEV):
            px, py, pc = x ^ (k >> 2), y ^ ((k >> 1) & 1), c ^ (k & 1)
            _remote(s_ref, a_ref.at[4 * px + 2 * py + pc], send_sems.at[k - 1], recv_sems.at[k - 1], (px, py, pc)).wait_recv()
        for cp in sent:
            cp.wait_send()
        own.wait()

    return pl.pallas_call(
        body,
        in_specs=[HBM_REF],
        out_specs=HBM_REF,
        out_shape=jax.ShapeDtypeStruct((N_DEV,) + SMALL_SHAPE, small.dtype),
        scratch_shapes=[pltpu.SemaphoreType.DMA((N_DEV - 1,)), pltpu.SemaphoreType.DMA((N_DEV - 1,)), pltpu.SemaphoreType.DMA],
        name="small_allgather",
    )(small)


def small_sum(blocks):
    def body(a_ref, o_ref):
        acc = a_ref[0]
        for d in range(1, N_DEV):
            acc = acc + a_ref[d]
        o_ref[...] = acc

    return pl.pallas_call(
        body,
        in_specs=[pl.BlockSpec(memory_space=pltpu.VMEM)],
        out_specs=pl.BlockSpec(memory_space=pltpu.VMEM),
        out_shape=jax.ShapeDtypeStruct(SMALL_SHAPE, F32),
        name="small_sum",
    )(blocks)


def pack_small(grads, loss):
    flat = jnp.concatenate([grads[n].reshape(-1) for n in SMALL] + [loss.reshape(-1)])
    size = SMALL_SHAPE[0] * SMALL_SHAPE[1]
    return jnp.pad(flat, (0, size - flat.shape[0])).reshape(SMALL_SHAPE)


def unpack_small(packed):
    flat = packed.reshape(-1)
    out, off = {}, 0
    for n, size in SMALL.items():
        out[n] = flat[off:off + DEPTH * size].reshape(DEPTH, size)
        off += DEPTH * size
    return out, flat[off]


def kernel(x, norm_mix_g, w_in, forget_b, q_norm_g, k_norm_g, w_attn_out, conv_w, w_conv_out, pool_w, pool_scale, w_o, norm_ffn_g, w_ffn_in, w_ffn_out, loss_target, m_norm_mix_g, m_w_in, m_forget_b, m_q_norm_g, m_k_norm_g, m_w_attn_out, m_conv_w, m_w_conv_out, m_pool_w, m_pool_scale, m_w_o, m_norm_ffn_g, m_w_ffn_in, m_w_ffn_out, v_norm_mix_g, v_w_in, v_forget_b, v_q_norm_g, v_k_norm_g, v_w_attn_out, v_conv_w, v_w_conv_out, v_pool_w, v_pool_scale, v_w_o, v_norm_ffn_g, v_w_ffn_in, v_w_ffn_out):
    given = dict(locals())
    weights = {n: given[n] for n in WEIGHTS}

    core = lax.axis_index("c").astype(jnp.int32)
    chip = (2 * lax.axis_index("x") + lax.axis_index("y")).astype(jnp.int32)

    core, chip = core.reshape(1), chip.reshape(1)
    mix_names, all_names = MIX_W + ["conv_w"], MIX_W + FFN_W + ["conv_w"]

    def placed(lo, hi, names, after=None):
        piece = None if after is None else _as4(next(iter(after.values())))[0, 0, 0:HALO, 0:LANES]
        return [place_shard(chip, weights[n], lo, hi, F32 if n == "conv_w" else BF16, "place_" + n,
                            piece if i == 0 else None) for i, n in enumerate(names)]

    def as_weights(bufs, names):
        return {n: b.reshape(b.shape[:2] + weights[n].shape[1:]) for n, b in zip(names, bufs)}

    def landed(start, names, after):
        return as_weights(gather_forward(gather_wait(*start[:3], after)), names)

    def layer_small(l, *tokens):
        small = {n: weights[n][l] for n in SMALL if n != "conv_w"}
        for t in tokens:
            small["norm_mix_g"] = small["norm_mix_g"] + t[0, 0]
        return small

    done = [None] * DEPTH
    first = as_weights(gather_shards(placed(0, 1, mix_names)), mix_names)
    ffn0 = gather_start(placed(0, 1, FFN_W, after=first))
    layer1 = gather_start(placed(1, 2, all_names, after=first))
    xm, wm, sm = mix_fwd(x[0], first, 0, layer_small(0, ffn0[3], layer1[3]))
    xs, wf, sf = ffn_half_fwd(xm, landed(ffn0, FFN_W, xm), 0, weights["norm_ffn_g"][0])
    done[0] = (wm, sm, wf, sf)
    full = landed(layer1, all_names, xs)
    rest = gather_start(placed(2, DEPTH, all_names, after=full))
    xm, wm, sm = mix_fwd(xs, full, 0, layer_small(1, rest[3]))
    xs, wf, sf = ffn_half_fwd(xm, full, 0, weights["norm_ffn_g"][1])
    done[1] = (wm, sm, wf, sf)
    full = landed(rest, all_names, xs)
    for l in range(2, DEPTH):
        xm, wm, sm = mix_fwd(xs, full, l - 2, layer_small(l))
        xs, wf, sf = ffn_half_fwd(xm, full, l - 2, weights["norm_ffn_g"][l])
        done[l] = (wm, sm, wf, sf)
    loss, dx = loss_head(xs, loss_target[0])

    small_grads = [None] * DEPTH
    totals = {n: lax.empty((DEPTH,) + _as4(weights[n][None]).shape[2:], F32) for n in SPLIT}

    def buffers(names, n_layers):
        return {n: lax.empty((n_layers, N_CHIPS) + weights[n].shape[1:], F32) for n in names}

    def pair_sums(names, grads, theirs):
        return [pair_sum(core, a, t, "pair_sum_" + n) for n, a, t in zip(names, grads, theirs)]

    def add_chips(names, recv, sums, lo):
        for n, r, s in zip(names, recv, sums):
            totals[n] = chip_sum_into(core, chip, r, s, totals[n], lo, "chip_sum_" + n)

    big = buffers(SPLIT, DEPTH - 1)
    for l in reversed(range(1, DEPTH)):
        wm, sm, wf, sf = done[l]
        dxm, ffn_big, g_ffn = ffn_half_bwd(dx, wf, sf, l - 1, {n: big[n] for n in FFN_W})
        dx, mix_big, g_mix = mix_bwd(dxm, wm, sm, l - 1, {n: big[n] for n in MIX_W})
        big = {**ffn_big, **mix_big}
        small_grads[l] = {**g_ffn, **g_mix}
    pair = pair_exchange_start([_as4(big[n]) for n in SPLIT])

    wm, sm, wf, sf = done[0]
    wf = dict(wf, g2=wf["g2"] + pair[4][0:1, 0:1])
    dxm, ffn_big, g_ffn = ffn_half_bwd(dx, wf, sf, 0, buffers(FFN_W, 1))
    chips = chip_exchange_start(pair_sums(SPLIT, *pair_exchange_wait(*pair[:4], ffn_big["w_ffn_in"])))
    ffn_grads = [_as4(ffn_big[n]) for n in FFN_W]
    ffn_chips = chip_exchange_start(pair_sums(FFN_W, ffn_grads, pair_exchange(ffn_grads)))
    wm = dict(wm, pscale=wm["pscale"] + chips[4][0:1, 0:1] + ffn_chips[4][0:1, 0:1])
    dproj, mix_big, g_mix = mix_bwd_weights(dxm, wm, sm, 0, buffers(MIX_W, 1))
    mix_grads = [_as4(mix_big[n]) for n in MIX_W]
    mix_chips = chip_exchange_start(pair_sums(MIX_W, mix_grads, pair_exchange(mix_grads)))
    dx, g_in = mix_bwd_input(dproj, dxm, dict(wm, g1=wm["g1"] + mix_chips[4][0:1, 0:1]), sm)
    small_grads[0] = {**g_ffn, **g_mix, **g_in}

    reduced, deltas, new_m, new_v = {}, {}, {}, {}

    def update(names):
        for n in names:
            w = weights[n]
            flat = (-1, w.shape[-1])
            d, nm, nv = adamw(w.reshape(flat), reduced[n].reshape(flat), given["m_" + n].reshape(flat),
                              given["v_" + n].reshape(flat), "adamw_" + n)
            deltas[n], new_m[n], new_v[n] = d.reshape(w.shape), nm.reshape(w.shape), nv.reshape(w.shape)

    def share(names):
        for n, t in zip(names, sibling_share([totals[n] for n in names])):
            reduced[n] = t.reshape(weights[n].shape)

    sums, recv = chip_exchange_wait(*chips[:4], dx)
    add_chips(SPLIT, recv, sums, 1)
    sums, recv = chip_exchange_wait(*ffn_chips[:4], dx)
    add_chips(FFN_W, recv, sums, 0)
    share(FFN_W)
    update(FFN_W)
    sums, recv = chip_exchange_wait(*mix_chips[:4], deltas["w_ffn_in"])
    add_chips(MIX_W, recv, sums, 0)
    share(MIX_W)

    small_grads = {n: jnp.stack([g[n] for g in small_grads]) for n in SMALL}
    small_total, loss_sum = unpack_small(small_sum(small_allgather(pack_small(small_grads, loss))))
    chip = chip[0]
    cols = D_CONV // N_CHIPS
    small_total["conv_w"] = lax.dynamic_slice_in_dim(small_total["conv_w"].reshape(DEPTH, CONV_K, D_CONV), chip * cols, cols, axis=2)
    for n in SMALL:
        reduced[n] = small_total[n].reshape(weights[n].shape)
    update(MIX_W + list(SMALL))

    return (loss_sum, dx[None], *[reduced[n] for n in WEIGHTS], *[deltas[n] for n in WEIGHTS],
            *[new_m[n] for n in WEIGHTS], *[new_v[n] for n in WEIGHTS])
```

```python
import functools

import numpy as np
import jax
import jax.numpy as jnp
from jax import lax
from jax.experimental import pallas as pl
from jax.experimental.pallas import tpu as pltpu

F32 = jnp.float32
BF16 = jnp.bfloat16

D_MODEL = 1024
DEPTH = 4
HEAD_DIM = 64
N_HEADS = 8
D_ATTN = 512
D_CONV = 256
D_POOL = 256
D_FF = 2816
D_IN = 5640
CONV_K = 3
POOL_WINDOWS = (2, 4, 8, 16)
N_GROUPS = len(POOL_WINDOWS)
EPS = 1e-6
ADAM_LR, ADAM_B1, ADAM_B2, ADAM_EPS, ADAM_WD, ADAM_STEP = 0.001, 0.9, 0.999, 1e-08, 0.01, 10

D_QKV = 3 * D_ATTN
D_F = 128
D_B = 3 * D_CONV + D_POOL + 3 * D_MODEL
D_LOCAL = 3 * D_CONV + D_POOL

LANES = 128
D_HEADS = N_HEADS * LANES
HALO = 16
VMEM_LIMIT = 56 * 1024 * 1024
NEG = -1e30
LOG2E = 1.4426950408889634
LN2 = 0.6931471805599453

TM = 256
TM_MIX = 256
TQ = 1024

LANE_C = 64
LANE_ONE = 67
LANE_LSE = 70
N_PIECES = 3


def _dot(a, b):
    return jnp.dot(a, b, preferred_element_type=F32)


def _dot_nt(a, b):
    return lax.dot_general(a, b, (((1,), (1,)), ((), ())), preferred_element_type=F32)


def _dot_tn(a, b):
    return lax.dot_general(a, b, (((0,), (0,)), ((), ())), preferred_element_type=F32)


def _params(*sem):
    return pltpu.CompilerParams(dimension_semantics=sem, vmem_limit_bytes=VMEM_LIMIT)


def _rows(tm, n):
    return pl.BlockSpec((tm, n), lambda i, *_: (i, 0))


def _whole(a):
    nd = a.ndim
    return pl.BlockSpec(a.shape, lambda *_: (0,) * nd)


def _layer(shape):
    nd = len(shape)
    return pl.BlockSpec((None,) + tuple(shape), lambda *a: (a[-1][0],) + (0,) * nd)


def _layer_index(l):
    return jnp.full((1,), l, jnp.int32)


def _split_bf16(x):
    hi = x.astype(BF16)
    lo = (x - hi.astype(F32)).astype(BF16)
    return hi, lo


def _pieces(x):
    p1 = x.astype(BF16)
    r1 = x - p1.astype(F32)
    p2 = r1.astype(BF16)
    p3 = (r1 - p2.astype(F32)).astype(BF16)
    return p1, p2, p3


def _sigmoid(x):
    return 1.0 / (1.0 + jnp.exp(-x))


def w_in_prep(win, l):
    tr = 256
    n = D_IN // 4
    v_rest = D_QKV - n
    b0 = v_rest + N_HEADS

    def body(s0, s1, s2, s3, wa_ref, wf_ref, wb_ref):
        b = s1[...]
        wa_ref[...] = jnp.concatenate([s0[...], b[:, 0:v_rest]], axis=1)
        wf_ref[...] = jnp.concatenate([b[:, v_rest:b0], jnp.zeros((tr, D_F - N_HEADS), b.dtype)], axis=1)
        wb_ref[...] = jnp.concatenate([b[:, b0:n], s2[...], s3[...]], axis=1)

    shard = lambda j: pl.BlockSpec((None, None, tr, n), lambda i: (l, j, i, 0))
    return pl.pallas_call(
        body,
        grid=(D_MODEL // tr,),
        in_specs=[shard(0), shard(1), shard(2), shard(3)],
        out_specs=[_rows(tr, D_QKV), _rows(tr, D_F), _rows(tr, D_B)],
        out_shape=[jax.ShapeDtypeStruct((D_MODEL, D_QKV), win.dtype), jax.ShapeDtypeStruct((D_MODEL, D_F), win.dtype),
                   jax.ShapeDtypeStruct((D_MODEL, D_B), win.dtype)],
        compiler_params=_params("parallel"),
        name="w_in_prep",
    )(win, win, win, win)


def _into_layer(buf):
    return dict(in_spec=HBM_REF, out_shape=jax.ShapeDtypeStruct(buf.shape, buf.dtype), aliases={1: 0})


def w_in_unprep(dwa, dwf, dwb, buf, l):
    tr = 256
    n = D_IN // 4
    v_rest = D_QKV - n
    b1 = n - v_rest - N_HEADS
    place = _into_layer(buf)

    def body(l_ref, buf_ref, a_ref, f_ref, b_ref, o_ref):
        a = a_ref[...]
        b = b_ref[...]
        o_ref[0] = a[:, 0:n]
        o_ref[1] = jnp.concatenate([a[:, n:D_QKV], f_ref[:, 0:N_HEADS], b[:, 0:b1]], axis=1)
        o_ref[2] = b[:, b1:b1 + n]
        o_ref[3] = b[:, b1 + n:D_B]

    return pl.pallas_call(
        body,
        grid_spec=pltpu.PrefetchScalarGridSpec(
            num_scalar_prefetch=1,
            grid=(D_MODEL // tr,),
            in_specs=[place["in_spec"], _rows(tr, D_QKV), _rows(tr, D_F), _rows(tr, D_B)],
            out_specs=pl.BlockSpec((None, 4, tr, n), lambda i, l: (l[0], 0, i, 0)),
        ),
        out_shape=place["out_shape"],
        input_output_aliases=place["aliases"],
        compiler_params=_params("parallel"),
        name="w_in_unprep",
    )(_layer_index(l), buf, dwa, dwf, dwb)


def branch_w_prep(wao, wco, pw, cw, l):
    gd = D_POOL // N_GROUPS
    od = D_MODEL // N_GROUPS

    def body(wao_ref, wco_ref, pw_ref, cw_ref, ao_ref, co_ref, po_ref, co8_ref):
        ao_ref[...] = jnp.concatenate([wao_ref[j] for j in range(4)], axis=1)
        co_ref[...] = jnp.concatenate([wco_ref[j] for j in range(4)], axis=1)
        zero = jnp.zeros((gd, od), co_ref.dtype)
        po_ref[...] = jnp.concatenate(
            [jnp.concatenate([jnp.concatenate([pw_ref[j, g] for j in range(4)], axis=1) if g2 == g else zero
                              for g2 in range(N_GROUPS)], axis=1) for g in range(N_GROUPS)], axis=0)
        co8_ref[...] = jnp.zeros_like(co8_ref)
        co8_ref[0:CONV_K, :] = jnp.concatenate([cw_ref[j] for j in range(4)], axis=1)

    sel = lambda *shape: pl.BlockSpec((None,) + shape, lambda i: (l,) + (0,) * len(shape))
    return pl.pallas_call(
        body,
        grid=(1,),
        in_specs=[sel(4, D_ATTN, D_MODEL // 4), sel(4, D_CONV, D_MODEL // 4), sel(4, N_GROUPS, gd, od // 4),
                  sel(4, CONV_K, D_CONV // 4)],
        out_specs=[pl.BlockSpec((D_ATTN, D_MODEL), lambda i: (0, 0)), pl.BlockSpec((D_CONV, D_MODEL), lambda i: (0, 0)),
                   pl.BlockSpec((D_POOL, D_MODEL), lambda i: (0, 0)), pl.BlockSpec((8, D_CONV), lambda i: (0, 0))],
        out_shape=[jax.ShapeDtypeStruct((D_ATTN, D_MODEL), BF16), jax.ShapeDtypeStruct((D_CONV, D_MODEL), BF16),
                   jax.ShapeDtypeStruct((D_POOL, D_MODEL), BF16), jax.ShapeDtypeStruct((8, D_CONV), F32)],
        name="branch_w_prep",
    )(wao, wco, pw, cw)


def branch_g_place(dwao, dwco, dwpool, dwo, bufs, l):
    gd = D_POOL // N_GROUPS
    od = D_MODEL // N_GROUPS
    q = D_MODEL // 4

    def body(l_ref, b0, b1, b2, b3, a_ref, c_ref, p_ref, w_ref, ao_ref, co_ref, po_ref, wo_ref):
        a = a_ref[...]
        c = c_ref[...]
        p = p_ref[...]
        for j in range(4):
            ao_ref[j] = a[:, j * q:(j + 1) * q]
            co_ref[j] = c[:, j * q:(j + 1) * q]
            wo_ref[j] = w_ref[j * q:(j + 1) * q, :]
            for g in range(N_GROUPS):
                c0 = g * od + j * (od // 4)
                po_ref[j, g] = p[g * gd:(g + 1) * gd, c0:c0 + od // 4]

    whole = lambda a: pl.BlockSpec(a.shape, lambda i, l: (0,) * a.ndim)
    layer = lambda b: pl.BlockSpec((None,) + b.shape[1:], lambda i, l: (l[0],) + (0,) * (b.ndim - 1))
    return pl.pallas_call(
        body,
        grid_spec=pltpu.PrefetchScalarGridSpec(
            num_scalar_prefetch=1,
            grid=(1,),
            in_specs=[HBM_REF] * 4 + [whole(dwao), whole(dwco), whole(dwpool), whole(dwo)],
            out_specs=[layer(b) for b in bufs],
        ),
        out_shape=[jax.ShapeDtypeStruct(b.shape, b.dtype) for b in bufs],
        input_output_aliases={1: 0, 2: 1, 3: 2, 4: 3},
        compiler_params=_params("arbitrary"),
        name="branch_g_place",
    )(_layer_index(l), *bufs, dwao, dwco, dwpool, dwo)


def in_proj_fwd(x, g, wa, wf, wb):
    S = x.shape[0]

    def body(x_ref, g_ref, wa_ref, wf_ref, wb_ref, h_ref, pa_ref, pf_ref, pb_ref):
        xf = x_ref[...]
        r = lax.rsqrt(jnp.mean(xf * xf, axis=-1, keepdims=True) + EPS)
        h = (xf * r * g_ref[...]).astype(BF16)
        h_ref[...] = h
        pa_ref[...] = _dot(h, wa_ref[...]).astype(BF16)
        pf_ref[...] = _dot(h, wf_ref[...])
        pb_ref[...] = _dot(h, wb_ref[...]).astype(BF16)

    return pl.pallas_call(
        body,
        grid=(S // TM,),
        in_specs=[_rows(TM, D_MODEL), _whole(g), _whole(wa), _whole(wf), _whole(wb)],
        out_specs=[_rows(TM, D_MODEL), _rows(TM, D_QKV), _rows(TM, D_F), _rows(TM, D_B)],
        out_shape=[
            jax.ShapeDtypeStruct((S, D_MODEL), BF16),
            jax.ShapeDtypeStruct((S, D_QKV), BF16),
            jax.ShapeDtypeStruct((S, D_F), F32),
            jax.ShapeDtypeStruct((S, D_B), BF16),
        ],
        compiler_params=_params("parallel"),
        name="in_proj_fwd",
    )(x, g, wa, wf, wb)


def in_proj_bwd(x, g, dxm, dpa, dpf, dpb, wa, wf, wb):
    S = x.shape[0]

    def body(x_ref, g_ref, dxm_ref, dpa_ref, dpf_ref, dpb_ref, wa_ref, wf_ref, wb_ref, dx_ref, dg_ref):
        @pl.when(pl.program_id(0) == 0)
        def _():
            dg_ref[...] = jnp.zeros_like(dg_ref)

        dh = _dot_nt(dpa_ref[...], wa_ref[...]) + _dot_nt(dpf_ref[...], wf_ref[...]) + _dot_nt(dpb_ref[...], wb_ref[...])
        xf = x_ref[...]
        r = lax.rsqrt(jnp.mean(xf * xf, axis=-1, keepdims=True) + EPS)
        xhat = xf * r
        dg_ref[...] += jnp.sum(dh * xhat, axis=0, keepdims=True)
        gdh = dh * g_ref[...]
        dx_ref[...] = dxm_ref[...] + r * (gdh - xhat * jnp.mean(xhat * gdh, axis=-1, keepdims=True))

    return pl.pallas_call(
        body,
        grid=(S // TM,),
        in_specs=[_rows(TM, D_MODEL), _whole(g), _rows(TM, D_MODEL), _rows(TM, D_QKV), _rows(TM, D_F), _rows(TM, D_B),
                  _whole(wa), _whole(wf), _whole(wb)],
        out_specs=[_rows(TM, D_MODEL), pl.BlockSpec((1, D_MODEL), lambda i: (0, 0))],
        out_shape=[jax.ShapeDtypeStruct((S, D_MODEL), F32), jax.ShapeDtypeStruct((1, D_MODEL), F32)],
        compiler_params=_params("arbitrary"),
        name="in_proj_bwd",
    )(x, g, dxm, dpa, dpf, dpb, wa, wf, wb)


def _wgrad_tokens(S, K):
    return min(S, 2048 if K <= 1024 else 1024)


def _wgrad_columns(n):
    return next(tn for tn in (1024, 768, 512, 256, 128) if n % tn == 0)


def wgrad(xa, dy, name):
    S, K = xa.shape
    N = dy.shape[1]
    ts = _wgrad_tokens(S, K)
    tn = _wgrad_columns(N)
    tk = _wgrad_columns(K)

    def body(x_ref, dy_ref, o_ref):
        @pl.when(pl.program_id(2) == 0)
        def _():
            o_ref[...] = jnp.zeros_like(o_ref)

        o_ref[...] += _dot_tn(x_ref[...], dy_ref[...])

    return pl.pallas_call(
        body,
        grid=(K // tk, N // tn, S // ts),
        in_specs=[pl.BlockSpec((ts, tk), lambda i, j, k: (k, i)), pl.BlockSpec((ts, tn), lambda i, j, k: (k, j))],
        out_specs=pl.BlockSpec((tk, tn), lambda i, j, k: (i, j)),
        out_shape=jax.ShapeDtypeStruct((K, N), F32),
        compiler_params=_params("parallel", "parallel", "arbitrary"),
        name=name,
    )(xa, dy)


def wgrad_into(xa, dy, name, buf, l):
    S, K = xa.shape
    N = dy.shape[1]
    ts = _wgrad_tokens(S, K)
    split = buf.ndim == 4
    tn = buf.shape[-1] if split else _wgrad_columns(N)
    place = _into_layer(buf)

    def body(l_ref, buf_ref, x_ref, dy_ref, o_ref):
        @pl.when(pl.program_id(1) == 0)
        def _():
            o_ref[...] = jnp.zeros_like(o_ref)

        o_ref[...] += _dot_tn(x_ref[...], dy_ref[...])

    if split:
        out_spec = pl.BlockSpec((None, None, K, tn), lambda j, k, l: (l[0], j, 0, 0))
    else:
        out_spec = pl.BlockSpec((None, K, tn), lambda j, k, l: (l[0], 0, j))
    return pl.pallas_call(
        body,
        grid_spec=pltpu.PrefetchScalarGridSpec(
            num_scalar_prefetch=1,
            grid=(N // tn, S // ts),
            in_specs=[place["in_spec"], pl.BlockSpec((ts, K), lambda j, k, l: (k, 0)),
                      pl.BlockSpec((ts, tn), lambda j, k, l: (k, j))],
            out_specs=out_spec,
        ),
        out_shape=place["out_shape"],
        input_output_aliases=place["aliases"],
        compiler_params=_params("parallel", "arbitrary"),
        name=name,
    )(_layer_index(l), buf, xa, dy)


def _head_mean_matrix():
    h = np.arange(D_ATTN) // HEAD_DIM
    return jnp.asarray((h[:, None] == h[None, :]).astype(np.float32) / HEAD_DIM, BF16)


def _place_matrix(lane0):
    m = np.zeros((N_PIECES * LANES, D_HEADS), np.float32)
    for i in range(N_PIECES):
        for h in range(N_HEADS):
            m[i * LANES + h, h * LANES + lane0 + i] = 1.0
    return jnp.asarray(m, BF16)


def _tri(n, upper):
    r = np.arange(n)
    m = (r[None, :] >= r[:, None]) if upper else (r[None, :] <= r[:, None])
    return jnp.asarray(m.astype(np.float32), BF16)


def _lanes_in(lane, lo, n):
    return (lane >= lo) & (lane < lo + n)


def qk_prep(pa, pf, gq, gk, fb):
    S = pa.shape[0]
    bd = _head_mean_matrix()
    tri = _tri(TM, upper=False)
    place_q = _place_matrix(LANE_C)
    place_k = _place_matrix(LANE_ONE)

    def body(q_ref, k_ref, v_ref, pf_ref, gq_ref, gk_ref, fb_ref, bd_ref, tri_ref, pq_ref, pk_ref,
             qx_ref, kx_ref, vx_ref, carry):
        @pl.when(pl.program_id(0) == 0)
        def _():
            carry[...] = jnp.zeros_like(carry)

        def head_norm(x_ref, g_ref, scale):
            xf = x_ref[...].astype(F32)
            ms = _dot((xf * xf).astype(BF16), bd_ref[...])
            return xf * lax.rsqrt(ms + EPS) * g_ref[...] * scale

        qh = head_norm(q_ref, gq_ref, HEAD_DIM ** -0.5 * LOG2E)
        kh = head_norm(k_ref, gk_ref, 1.0)
        vf = v_ref[...].astype(F32)

        z = pf_ref[...] + fb_ref[...]
        logf = jnp.minimum(z, 0.0) - jnp.log(1.0 + jnp.exp(-jnp.abs(z)))
        hi, lo = _split_bf16(logf)
        c = _dot(tri_ref[...], hi) + _dot(tri_ref[...], lo) + carry[...]
        carry[...] += jnp.sum(hi.astype(F32) + lo.astype(F32), axis=0, keepdims=True)
        pieces = jnp.concatenate(_pieces(c * LOG2E), axis=1)
        cq = _dot(pieces, pq_ref[...])
        ck = _dot(pieces, pk_ref[...])

        lane = lax.broadcasted_iota(jnp.int32, (TM, LANES), 1)
        low = lane < HEAD_DIM
        ones_q = _lanes_in(lane, LANE_ONE, N_PIECES).astype(F32)
        ones_k = (_lanes_in(lane, LANE_C, N_PIECES) | _lanes_in(lane, LANE_LSE, N_PIECES)).astype(F32)
        ones_v = _lanes_in(lane, LANE_C, N_PIECES + 1).astype(F32)
        for h in range(N_HEADS):
            blk = slice((h // 2) * LANES, (h // 2 + 1) * LANES)
            head = (lambda a: pltpu.roll(a[:, blk], HEAD_DIM, 1)) if h % 2 else (lambda a: a[:, blk])
            mine = slice(h * LANES, (h + 1) * LANES)
            qx_ref[h] = jnp.where(low, head(qh), cq[:, mine] + ones_q).astype(BF16)
            kx_ref[h] = jnp.where(low, head(kh), ones_k - ck[:, mine]).astype(BF16)
            vx_ref[h] = jnp.where(low, head(vf), ones_v).astype(BF16)

    heads = pl.BlockSpec((N_HEADS, TM, LANES), lambda i: (0, i, 0))
    out = jax.ShapeDtypeStruct((N_HEADS, S, LANES), BF16)
    return pl.pallas_call(
        body,
        grid=(S // TM,),
        in_specs=[pl.BlockSpec((TM, D_ATTN), lambda i: (i, 0)), pl.BlockSpec((TM, D_ATTN), lambda i: (i, 1)),
                  pl.BlockSpec((TM, D_ATTN), lambda i: (i, 2)),
                  _rows(TM, D_F), _whole(gq), _whole(gk), _whole(fb), _whole(bd), _whole(tri), _whole(place_q), _whole(place_k)],
        out_specs=[heads, heads, heads],
        out_shape=[out, out, out],
        scratch_shapes=[pltpu.VMEM((1, D_F), F32)],
        compiler_params=_params("arbitrary"),
        name="qk_prep",
    )(pa, pa, pa, pf, gq, gk, fb, bd, tri, place_q, place_k)


def attn_fwd(qx, kx, vx):
    S = qx.shape[1]
    nq = S // TQ

    def body(q_ref, k_ref, v_ref, o_ref, qb_ref):
        i = pl.program_id(1)
        lane = lax.broadcasted_iota(jnp.int32, (TQ, LANES), 1)
        q = [q_ref[0], q_ref[1]]

        def update(qh, m, acc, k, v, visible):
            z = _dot_nt(qh, k)
            if visible is not None:
                z = jnp.where(visible, z, NEG)
            m_new = jnp.maximum(m, jnp.max(z, axis=1, keepdims=True))
            pr = jnp.exp2(z - m_new)
            return m_new, jnp.exp2(m - m_new) * acc + _dot(pr.astype(BF16), v)

        def step(kt, carry, diagonal=False):
            ks = pl.multiple_of(kt * TQ, TQ)
            visible = None
            if diagonal:
                visible = lax.broadcasted_iota(jnp.int32, (TQ, TQ), 0) >= lax.broadcasted_iota(jnp.int32, (TQ, TQ), 1)
            return tuple(update(q[j], *carry[j], k_ref[j, pl.ds(ks, TQ), :], v_ref[j, pl.ds(ks, TQ), :], visible)
                         for j in range(2))

        init = tuple((jnp.full((TQ, 1), NEG, F32), jnp.zeros((TQ, LANES), F32)) for _ in range(2))
        carry = step(i, lax.fori_loop(0, i, step, init), diagonal=True)
        for j in range(2):
            m, acc = carry[j]
            l = jnp.sum(jnp.where(lane == LANE_C, acc, 0.0), axis=1, keepdims=True)
            o_ref[j] = acc / l
            n1, n2, n3 = _pieces(-(m + jnp.log(l) * LOG2E))
            qb_ref[j] = jnp.where(lane == LANE_LSE, n1,
                                  jnp.where(lane == LANE_LSE + 1, n2, jnp.where(lane == LANE_LSE + 2, n3, q[j])))

    pair_tile = pl.BlockSpec((2, TQ, LANES), lambda p, i: (p, i, 0))
    pair_all = pl.BlockSpec((2, S, LANES), lambda p, i: (p, 0, 0))
    return pl.pallas_call(
        body,
        grid=(N_HEADS // 2, nq),
        in_specs=[pair_tile, pair_all, pair_all],
        out_specs=[pair_tile, pair_tile],
        out_shape=[jax.ShapeDtypeStruct((N_HEADS, S, LANES), F32), jax.ShapeDtypeStruct((N_HEADS, S, LANES), BF16)],
        compiler_params=_params("parallel", "parallel"),
        name="attn_fwd",
    )(qx, kx, vx)


def attn_bwd(qxb, kx, vx, dox):
    S = qxb.shape[1]
    nq = S // TQ

    def body(q_ref, k_ref, v_ref, do_ref, dq_ref, dk_ref, dv_ref):
        kt = pl.program_id(1)

        @pl.when(kt == 0)
        def _():
            dq_ref[...] = jnp.zeros_like(dq_ref)

        k = [k_ref[0], k_ref[1]]
        v = [v_ref[0], v_ref[1]]

        def block(j, q0, nq_rows, k0, nk, diagonal):
            q = q_ref[j, pl.ds(q0, nq_rows), :]
            dout = do_ref[j, pl.ds(q0, nq_rows), :]
            kj, vj = k[j][k0:k0 + nk], v[j][k0:k0 + nk]
            z = _dot_nt(q, kj)
            if diagonal:
                visible = (lax.broadcasted_iota(jnp.int32, (nq_rows, nk), 0)
                           >= lax.broadcasted_iota(jnp.int32, (nq_rows, nk), 1))
                z = jnp.where(visible, z, NEG)
            pr = jnp.exp2(z)
            dv = _dot_tn(pr.astype(BF16), dout)
            dsb = (pr * _dot_nt(dout, vj)).astype(BF16)
            dq_ref[j, pl.ds(q0, nq_rows), :] += _dot(dsb, kj)
            return _dot_tn(dsb, q), dv

        def step(qi, carry):
            qs = pl.multiple_of(qi * TQ, TQ)
            new = []
            for j in range(2):
                dk, dv = block(j, qs, TQ, 0, TQ, False)
                new.append((carry[j][0] + dk, carry[j][1] + dv))
            return tuple(new)

        half = TQ // 2
        qs = pl.multiple_of(kt * TQ, TQ)
        carry = []
        for j in range(2):
            first = block(j, qs, TQ, 0, half, True)
            second = block(j, pl.multiple_of(qs + half, half), half, half, half, True)
            carry.append(tuple(jnp.concatenate(a, axis=0) for a in zip(first, second)))
        carry = lax.fori_loop(kt + 1, nq, step, tuple(carry))
        for j in range(2):
            dk_ref[j] = carry[j][0]
            dv_ref[j] = carry[j][1]

    pair_tile = pl.BlockSpec((2, TQ, LANES), lambda p, kt: (p, kt, 0))
    pair_all = pl.BlockSpec((2, S, LANES), lambda p, kt: (p, 0, 0))
    out = jax.ShapeDtypeStruct((N_HEADS, S, LANES), F32)
    return pl.pallas_call(
        body,
        grid=(N_HEADS // 2, nq),
        in_specs=[pair_all, pair_tile, pair_tile, pair_all],
        out_specs=[pair_all, pair_tile, pair_tile],
        out_shape=[out, out, out],
        compiler_params=_params("arbitrary", "arbitrary"),
        name="attn_bwd",
    )(qxb, kx, vx, dox)


def attn_bwd_post(pa, pf, dqx, dkx, dvx, gq, gk, fb):
    S = pa.shape[0]
    nt = S // TM
    bd = _head_mean_matrix()
    triu = _tri(TM, upper=True)
    rev = lambda i: nt - 1 - i

    def body(q_ref, k_ref, pf_ref, dqx_ref, dkx_ref, dvx_ref, gq_ref, gk_ref, fb_ref, bd_ref, triu_ref,
             dpa_ref, dpf_ref, dgq_ref, dgk_ref, dfb_ref, carry):
        @pl.when(pl.program_id(0) == 0)
        def _():
            carry[...] = jnp.zeros_like(carry)
            dgq_ref[...] = jnp.zeros_like(dgq_ref)
            dgk_ref[...] = jnp.zeros_like(dgk_ref)
            dfb_ref[...] = jnp.zeros_like(dfb_ref)

        lane = lax.broadcasted_iota(jnp.int32, (TM, LANES), 1)
        columns = _heads_as_columns

        def head_norm_bwd(x_ref, dy, g_ref, dg_ref):
            xf = x_ref[...].astype(F32)
            r = lax.rsqrt(_dot((xf * xf).astype(BF16), bd_ref[...]) + EPS)
            xhat = xf * r
            dg_ref[...] += jnp.sum(dy * xhat, axis=0, keepdims=True)
            gdy = dy * g_ref[...]
            return (r * (gdy - xhat * _dot((xhat * gdy).astype(BF16), bd_ref[...]))).astype(BF16)

        dpa_ref[:, 0:D_ATTN] = head_norm_bwd(q_ref, columns(dqx_ref) * HEAD_DIM ** -0.5, gq_ref, dgq_ref)
        dpa_ref[:, D_ATTN:2 * D_ATTN] = head_norm_bwd(k_ref, columns(dkx_ref) * LN2, gk_ref, dgk_ref)
        dpa_ref[:, 2 * D_ATTN:3 * D_ATTN] = columns(dvx_ref).astype(BF16)

        dc = jnp.zeros((TM, LANES), F32)
        for h in range(N_HEADS):
            both = jnp.where(lane == LANE_C, dqx_ref[h], 0.0) - jnp.where(lane == LANE_ONE, dkx_ref[h], 0.0)
            dc = jnp.where(lane == h, jnp.sum(both, axis=1, keepdims=True), dc)
        hi, lo = _split_bf16(dc)
        dlogf = _dot(triu_ref[...], hi) + _dot(triu_ref[...], lo) + carry[...]
        first = lax.broadcasted_iota(jnp.int32, (TM, D_F), 0) == 0
        carry[...] = jnp.sum(jnp.where(first, dlogf, 0.0), axis=0, keepdims=True)
        df = dlogf * _sigmoid(-(pf_ref[...] + fb_ref[...]))
        dfb_ref[...] += jnp.sum(df, axis=0, keepdims=True)
        dpf_ref[...] = df.astype(BF16)

    heads = pl.BlockSpec((N_HEADS, TM, LANES), lambda i: (0, rev(i), 0))
    return pl.pallas_call(
        body,
        grid=(nt,),
        in_specs=[pl.BlockSpec((TM, D_ATTN), lambda i: (rev(i), 0)), pl.BlockSpec((TM, D_ATTN), lambda i: (rev(i), 1)),
                  pl.BlockSpec((TM, D_F), lambda i: (rev(i), 0)), heads, heads, heads,
                  _whole(gq), _whole(gk), _whole(fb), _whole(bd), _whole(triu)],
        out_specs=[pl.BlockSpec((TM, D_QKV), lambda i: (rev(i), 0)), pl.BlockSpec((TM, D_F), lambda i: (rev(i), 0)),
                   pl.BlockSpec((1, D_ATTN), lambda i: (0, 0)), pl.BlockSpec((1, D_ATTN), lambda i: (0, 0)),
                   pl.BlockSpec((1, D_F), lambda i: (0, 0))],
        out_shape=[jax.ShapeDtypeStruct((S, D_QKV), BF16), jax.ShapeDtypeStruct((S, D_F), BF16),
                   jax.ShapeDtypeStruct((1, D_ATTN), F32), jax.ShapeDtypeStruct((1, D_ATTN), F32),
                   jax.ShapeDtypeStruct((1, D_F), F32)],
        scratch_shapes=[pltpu.VMEM((1, D_F), F32)],
        compiler_params=_params("arbitrary"),
        name="attn_bwd_post",
    )(pa, pa, pf, dqx, dkx, dvx, gq, gk, fb, bd, triu)


def _shift_down(ext, k):
    return pltpu.roll(ext, k, 0)[HALO:]


def _shift_up(ext, k, n):
    return pltpu.roll(ext, n + HALO - k, 0)[:n]


def _pool_lane_select(a2, a4, a8, a16, lane):
    return jnp.where(lane < 64, a2, jnp.where(lane < 128, a4, jnp.where(lane < 192, a8, a16)))


def _local_branches(o, pb, halo, have_prev, row0, wao, convw, wco, wpool, pscale):
    n = pb.shape[0]
    cx = pb[:, 0:D_CONV].astype(F32)
    cb = pb[:, D_CONV:2 * D_CONV].astype(F32)
    cc = pb[:, 2 * D_CONV:3 * D_CONV].astype(F32)
    px = pb[:, 3 * D_CONV:D_LOCAL].astype(F32)
    keep = have_prev.astype(F32)
    z = cc * cx
    z_ext = jnp.concatenate([halo[:, 2 * D_CONV:3 * D_CONV].astype(F32) * halo[:, 0:D_CONV].astype(F32) * keep, z], axis=0)
    z1 = _shift_down(z_ext, 1)
    z2 = _shift_down(z_ext, 2)
    conv = convw[0:1, :] * z2 + convw[1:2, :] * z1 + convw[2:3, :] * z
    cm = cb * conv

    u_ext = jnp.concatenate([halo[:, 3 * D_CONV:D_LOCAL].astype(F32) * keep, px], axis=0)
    s2 = u_ext + pltpu.roll(u_ext, 1, 0)
    s4 = s2 + pltpu.roll(s2, 2, 0)
    s8 = s4 + pltpu.roll(s4, 4, 0)
    s16 = s8 + pltpu.roll(s8, 8, 0)
    lane = lax.broadcasted_iota(jnp.int32, (n, D_POOL), 1)
    win = _pool_lane_select(2.0, 4.0, 8.0, 16.0, lane)
    t = (row0 + lax.broadcasted_iota(jnp.int32, (n, D_POOL), 0)).astype(F32)
    cnt = jnp.minimum(t + 1.0, win)
    feat = _pool_lane_select(s2[HALO:], s4[HALO:], s8[HALO:], s16[HALO:], lane) / cnt - px

    ya = _dot(o, wao)
    yc = _dot(cm.astype(BF16), wco)
    yp_pre = _dot(feat.astype(BF16), wpool)
    yp = yp_pre * pscale
    return dict(cx=cx, cb=cb, cc=cc, z=z, z1=z1, z2=z2, conv=conv, cm=cm, feat=feat, cnt=cnt, lane=lane,
                ya=ya, yc=yc, yp_pre=yp_pre, yp=yp)


def _halo_spec(tm, tile_of):
    per = tm // HALO
    return pl.BlockSpec((HALO, D_LOCAL), lambda i, *_: (jnp.maximum(tile_of(i) * per - 1, 0), 0))


def _heads_as_columns(ref):
    lane = lax.broadcasted_iota(jnp.int32, ref.shape[1:], 1)
    return jnp.concatenate([jnp.where(lane < HEAD_DIM, ref[2 * p], pltpu.roll(ref[2 * p + 1], HEAD_DIM, 1))
                            for p in range(N_HEADS // 2)], axis=1)


def mix_out_fwd(x, ox, pb, wao, convw, wco, wpool, pscale, wo_all, l):
    S = x.shape[0]
    tm = TM_MIX

    def body(l_ref, x_ref, o_ref, pb_ref, halo_ref, wao_ref, cw_ref, wco_ref, wp_ref, ps_ref, wo_ref, y_ref):
        i = pl.program_id(0)
        pb = pb_ref[...]
        o = _heads_as_columns(o_ref).astype(BF16)
        b = _local_branches(o, pb, halo_ref[...], i > 0, i * tm, wao_ref[...], cw_ref[...], wco_ref[...],
                            wp_ref[...], ps_ref[...])
        g0 = _sigmoid(pb[:, D_LOCAL:D_LOCAL + D_MODEL].astype(F32))
        g1 = _sigmoid(pb[:, D_LOCAL + D_MODEL:D_LOCAL + 2 * D_MODEL].astype(F32))
        g2 = _sigmoid(pb[:, D_LOCAL + 2 * D_MODEL:D_B].astype(F32))
        merged = g0 * b["ya"] + g1 * b["yc"] + g2 * b["yp"]
        y_ref[...] = x_ref[...] + _dot(merged.astype(BF16), wo_ref[...])

    return pl.pallas_call(
        body,
        grid_spec=pltpu.PrefetchScalarGridSpec(
            num_scalar_prefetch=1,
            grid=(S // tm,),
            in_specs=[_rows(tm, D_MODEL), pl.BlockSpec((N_HEADS, tm, LANES), lambda i, l: (0, i, 0)), _rows(tm, D_B),
                      _halo_spec(tm, lambda i: i), _whole(wao), _whole(convw), _whole(wco), _whole(wpool), _whole(pscale),
                      _layer((D_MODEL, D_MODEL))],
            out_specs=_rows(tm, D_MODEL),
        ),
        out_shape=jax.ShapeDtypeStruct((S, D_MODEL), F32),
        compiler_params=_params("parallel"),
        name="mix_out_fwd",
    )(_layer_index(l), x, ox, pb, pb, wao, convw, wco, wpool, pscale, wo_all)


def mix_out_bwd(dxm, ox, pb, wao, convw, wco, wpool, pscale, wo_all, l):
    S = dxm.shape[0]
    tm = TM_MIX
    nt = S // tm
    rev = lambda i: nt - 1 - i
    rows = lambda n: pl.BlockSpec((tm, n), lambda i, l: (rev(i), 0))
    heads = pl.BlockSpec((N_HEADS, tm, LANES), lambda i, l: (0, rev(i), 0))
    acc = lambda r, c: pl.BlockSpec((r, c), lambda i, l: (0, 0))

    def body(l_ref, dxm_ref, o_ref, pb_ref, halo_ref, wao_ref, cw_ref, wco_ref, wp_ref, ps_ref, wo_ref,
             dpb_ref, do_ref, dwo_ref, dwao_ref, dwco_ref, dwp_ref, dcw_ref, dps_ref, next_dconv, next_e):
        i = pl.program_id(0)
        r = rev(i)

        @pl.when(i == 0)
        def _():
            for ref in (dwo_ref, dwao_ref, dwco_ref, dwp_ref, dcw_ref, dps_ref, next_dconv, next_e):
                ref[...] = jnp.zeros_like(ref)

        pb = pb_ref[...]
        o = _heads_as_columns(o_ref).astype(BF16)
        cw = cw_ref[...]
        b = _local_branches(o, pb, halo_ref[...], r > 0, r * tm, wao_ref[...], cw, wco_ref[...], wp_ref[...], ps_ref[...])
        g0 = _sigmoid(pb[:, D_LOCAL:D_LOCAL + D_MODEL].astype(F32))
        g1 = _sigmoid(pb[:, D_LOCAL + D_MODEL:D_LOCAL + 2 * D_MODEL].astype(F32))
        g2 = _sigmoid(pb[:, D_LOCAL + 2 * D_MODEL:D_B].astype(F32))
        dxb = dxm_ref[...].astype(BF16)
        merged = g0 * b["ya"] + g1 * b["yc"] + g2 * b["yp"]
        dwo_ref[...] += _dot_tn(merged.astype(BF16), dxb)
        dmer = _dot_nt(dxb, wo_ref[...])
        dpb_ref[:, D_LOCAL:D_LOCAL + D_MODEL] = (dmer * b["ya"] * (g0 * (1.0 - g0))).astype(BF16)
        dpb_ref[:, D_LOCAL + D_MODEL:D_LOCAL + 2 * D_MODEL] = (dmer * b["yc"] * (g1 * (1.0 - g1))).astype(BF16)
        dpb_ref[:, D_LOCAL + 2 * D_MODEL:D_B] = (dmer * b["yp"] * (g2 * (1.0 - g2))).astype(BF16)

        dya = (dmer * g0).astype(BF16)
        dwao_ref[...] += _dot_tn(o, dya)
        da = _dot_nt(dya, wao_ref[...])
        lane = lax.broadcasted_iota(jnp.int32, (tm, LANES), 1)
        for h in range(N_HEADS):
            two = da[:, (h // 2) * LANES:(h // 2 + 1) * LANES]
            dah = jnp.where(lane < HEAD_DIM, pltpu.roll(two, HEAD_DIM, 1) if h % 2 else two, 0.0).astype(BF16)
            d1, d2, d3 = _pieces(-jnp.sum(dah.astype(F32) * o_ref[h], axis=1, keepdims=True))
            do_ref[h] = jnp.where(lane == LANE_C + 1, d1, jnp.where(lane == LANE_C + 2, d2,
                                                                  jnp.where(lane == LANE_C + 3, d3, dah)))

        dyc = (dmer * g1).astype(BF16)
        dwco_ref[...] += _dot_tn(b["cm"].astype(BF16), dyc)
        dcm = _dot_nt(dyc, wco_ref[...])
        dconv = dcm * b["cb"]
        dcw_ref[0:1, :] += jnp.sum(dconv * b["z2"], axis=0, keepdims=True)
        dcw_ref[1:2, :] += jnp.sum(dconv * b["z1"], axis=0, keepdims=True)
        dcw_ref[2:3, :] += jnp.sum(dconv * b["z"], axis=0, keepdims=True)
        d_ext = jnp.concatenate([dconv, next_dconv[...]], axis=0)
        dz = cw[2:3, :] * dconv + cw[1:2, :] * _shift_up(d_ext, 1, tm) + cw[0:1, :] * _shift_up(d_ext, 2, tm)
        next_dconv[...] = dconv[0:HALO]
        dpb_ref[:, 0:D_CONV] = (dz * b["cc"]).astype(BF16)
        dpb_ref[:, D_CONV:2 * D_CONV] = (dcm * b["conv"]).astype(BF16)
        dpb_ref[:, 2 * D_CONV:3 * D_CONV] = (dz * b["cx"]).astype(BF16)

        dyp = dmer * g2
        dps_ref[...] += jnp.sum(dyp * b["yp_pre"], axis=0, keepdims=True)
        dyps = (dyp * ps_ref[...]).astype(BF16)
        dwp_ref[...] += _dot_tn(b["feat"].astype(BF16), dyps)
        dfeat = _dot_nt(dyps, wp_ref[...])
        e = dfeat / b["cnt"]
        e_ext = jnp.concatenate([e, next_e[...]], axis=0)
        up = lambda a, k: pltpu.roll(a, tm + HALO - k, 0)
        f2 = e_ext + up(e_ext, 1)
        f4 = f2 + up(f2, 2)
        f8 = f4 + up(f4, 4)
        f16 = f8 + up(f8, 8)
        next_e[...] = e[0:HALO]
        dpb_ref[:, 3 * D_CONV:D_LOCAL] = (_pool_lane_select(f2[:tm], f4[:tm], f8[:tm], f16[:tm], b["lane"]) - dfeat).astype(BF16)

    return pl.pallas_call(
        body,
        grid_spec=pltpu.PrefetchScalarGridSpec(
            num_scalar_prefetch=1,
            grid=(nt,),
            in_specs=[rows(D_MODEL), heads, rows(D_B), _halo_spec(tm, rev),
                      _whole(wao), _whole(convw), _whole(wco), _whole(wpool), _whole(pscale), _layer((D_MODEL, D_MODEL))],
            out_specs=[rows(D_B), heads, acc(D_MODEL, D_MODEL), acc(D_ATTN, D_MODEL), acc(D_CONV, D_MODEL),
                       acc(D_POOL, D_MODEL), acc(8, D_CONV), acc(1, D_MODEL)],
            scratch_shapes=[pltpu.VMEM((HALO, D_CONV), F32), pltpu.VMEM((HALO, D_POOL), F32)],
        ),
        out_shape=[jax.ShapeDtypeStruct((S, D_B), BF16), jax.ShapeDtypeStruct((N_HEADS, S, LANES), BF16),
                   jax.ShapeDtypeStruct((D_MODEL, D_MODEL), F32), jax.ShapeDtypeStruct((D_ATTN, D_MODEL), F32),
                   jax.ShapeDtypeStruct((D_CONV, D_MODEL), F32), jax.ShapeDtypeStruct((D_POOL, D_MODEL), F32),
                   jax.ShapeDtypeStruct((8, D_CONV), F32), jax.ShapeDtypeStruct((1, D_MODEL), F32)],
        compiler_params=_params("arbitrary"),
        name="mix_out_bwd",
    )(_layer_index(l), dxm, ox, pb, pb, wao, convw, wco, wpool, pscale, wo_all)


FF_SHARD = 2 * D_FF // 4


def ffn_fwd(x, g, w1_all, w2_all, l):
    S = x.shape[0]

    def body(l_ref, x_ref, g_ref, w1_ref, w2_ref, y_ref, u_ref):
        xf = x_ref[...]
        r = lax.rsqrt(jnp.mean(xf * xf, axis=-1, keepdims=True) + EPS)
        h = (xf * r * g_ref[...]).astype(BF16)
        u = jnp.concatenate([_dot(h, w1_ref[j]) for j in range(4)], axis=1)
        u_ref[...] = u.astype(BF16)
        gt = u[:, 0:D_FF]
        act = gt * _sigmoid(gt) * u[:, D_FF:2 * D_FF]
        y_ref[...] = xf + _dot(act.astype(BF16), w2_ref[...])

    return pl.pallas_call(
        body,
        grid_spec=pltpu.PrefetchScalarGridSpec(
            num_scalar_prefetch=1,
            grid=(S // TM,),
            in_specs=[_rows(TM, D_MODEL), _whole(g), _layer((4, D_MODEL, FF_SHARD)), _layer((D_FF, D_MODEL))],
            out_specs=[_rows(TM, D_MODEL), _rows(TM, 2 * D_FF)],
        ),
        out_shape=[jax.ShapeDtypeStruct((S, D_MODEL), F32), jax.ShapeDtypeStruct((S, 2 * D_FF), BF16)],
        compiler_params=_params("parallel"),
        name="ffn_fwd",
    )(_layer_index(l), x, g, w1_all, w2_all)


def ffn_bwd(x, g, dy, u, w1_all, w2_all, l):
    S = x.shape[0]

    def body(l_ref, x_ref, g_ref, dy_ref, u_ref, w1_ref, w2_ref, dx_ref, du_ref, act_ref, h_ref, dyb_ref, dg_ref):
        @pl.when(pl.program_id(0) == 0)
        def _():
            dg_ref[...] = jnp.zeros_like(dg_ref)

        dyf = dy_ref[...]
        dyb_ref[...] = dyf.astype(BF16)
        dact = _dot_nt(dyb_ref[...], w2_ref[...])
        gt = u_ref[:, 0:D_FF].astype(F32)
        up = u_ref[:, D_FF:2 * D_FF].astype(F32)
        sg = _sigmoid(gt)
        silu = gt * sg
        act_ref[...] = (silu * up).astype(BF16)
        du_ref[:, 0:D_FF] = (dact * up * (sg * (1.0 + gt * (1.0 - sg)))).astype(BF16)
        du_ref[:, D_FF:2 * D_FF] = (dact * silu).astype(BF16)
        dh = _dot_nt(du_ref[:, 0:FF_SHARD], w1_ref[0])
        for j in range(1, 4):
            dh = dh + _dot_nt(du_ref[:, j * FF_SHARD:(j + 1) * FF_SHARD], w1_ref[j])
        xf = x_ref[...]
        r = lax.rsqrt(jnp.mean(xf * xf, axis=-1, keepdims=True) + EPS)
        xhat = xf * r
        h_ref[...] = (xhat * g_ref[...]).astype(BF16)
        dg_ref[...] += jnp.sum(dh * xhat, axis=0, keepdims=True)
        gdh = dh * g_ref[...]
        dx_ref[...] = dyf + r * (gdh - xhat * jnp.mean(xhat * gdh, axis=-1, keepdims=True))

    return pl.pallas_call(
        body,
        grid_spec=pltpu.PrefetchScalarGridSpec(
            num_scalar_prefetch=1,
            grid=(S // TM,),
            in_specs=[_rows(TM, D_MODEL), _whole(g), _rows(TM, D_MODEL), _rows(TM, 2 * D_FF),
                      _layer((4, D_MODEL, FF_SHARD)), _layer((D_FF, D_MODEL))],
            out_specs=[_rows(TM, D_MODEL), _rows(TM, 2 * D_FF), _rows(TM, D_FF), _rows(TM, D_MODEL), _rows(TM, D_MODEL),
                       pl.BlockSpec((1, D_MODEL), lambda i, l: (0, 0))],
        ),
        out_shape=[jax.ShapeDtypeStruct((S, D_MODEL), F32), jax.ShapeDtypeStruct((S, 2 * D_FF), BF16),
                   jax.ShapeDtypeStruct((S, D_FF), BF16), jax.ShapeDtypeStruct((S, D_MODEL), BF16),
                   jax.ShapeDtypeStruct((S, D_MODEL), BF16), jax.ShapeDtypeStruct((1, D_MODEL), F32)],
        compiler_params=_params("arbitrary"),
        name="ffn_bwd",
    )(_layer_index(l), x, g, dy, u, w1_all, w2_all)


def loss_head(y, target):
    S = y.shape[0]

    def body(y_ref, t_ref, loss_ref, dy_ref):
        @pl.when(pl.program_id(0) == 0)
        def _():
            loss_ref[0, 0] = 0.0

        err = y_ref[...] - t_ref[...]
        dy_ref[...] = err * (1.0 / D_MODEL)
        loss_ref[0, 0] += 0.5 * jnp.sum(jnp.mean(err * err, axis=-1))

    return pl.pallas_call(
        body,
        grid=(S // TM,),
        in_specs=[_rows(TM, D_MODEL), _rows(TM, D_MODEL)],
        out_specs=[pl.BlockSpec((1, 1), lambda i: (0, 0), memory_space=pltpu.SMEM), _rows(TM, D_MODEL)],
        out_shape=[jax.ShapeDtypeStruct((1, 1), F32), jax.ShapeDtypeStruct((S, D_MODEL), F32)],
        compiler_params=_params("arbitrary"),
        name="loss_head",
    )(y, target)


SPLIT = {
    "w_in": ((D_MODEL, D_IN), 1),
    "w_attn_out": ((D_ATTN, D_MODEL), 1),
    "w_conv_out": ((D_CONV, D_MODEL), 1),
    "pool_w": ((N_GROUPS, D_POOL // N_GROUPS, D_MODEL // N_GROUPS), 2),
    "w_o": ((D_MODEL, D_MODEL), 0),
    "w_ffn_in": ((D_MODEL, 2 * D_FF), 1),
    "w_ffn_out": ((D_FF, D_MODEL), 0),
}
SMALL = {"norm_mix_g": D_MODEL, "forget_b": N_HEADS, "q_norm_g": HEAD_DIM, "k_norm_g": HEAD_DIM,
         "pool_scale": D_MODEL, "norm_ffn_g": D_MODEL, "conv_w": CONV_K * D_CONV}
WEIGHTS = ["norm_mix_g", "w_in", "forget_b", "q_norm_g", "k_norm_g", "w_attn_out", "conv_w", "w_conv_out", "pool_w",
           "pool_scale", "w_o", "norm_ffn_g", "w_ffn_in", "w_ffn_out"]


MIX_W = ["w_in", "w_attn_out", "w_conv_out", "pool_w", "w_o"]
FFN_W = ["w_ffn_in", "w_ffn_out"]


def _row(a):
    return a.astype(F32).reshape(1, -1)


def mix_fwd(x, full, li, small):
    n = full["w_o"].shape[0]
    wa, wf, wb = w_in_prep(full["w_in"], li)
    wao, wco, wpool, convw = branch_w_prep(full["w_attn_out"], full["w_conv_out"], full["pool_w"], full["conv_w"], li)
    w = dict(wa=wa, wf=wf, wb=wb, wao=wao, wco=wco, wpool=wpool, convw=convw,
             w_o=full["w_o"].reshape(n, D_MODEL, D_MODEL), at=li,
             g1=_row(small["norm_mix_g"]), pscale=_row(small["pool_scale"]),
             gq=_row(jnp.tile(small["q_norm_g"], N_HEADS)), gk=_row(jnp.tile(small["k_norm_g"], N_HEADS)),
             fb=_row(jnp.pad(small["forget_b"], (0, D_F - N_HEADS))))
    h, pa, pf, pb = in_proj_fwd(x, w["g1"], wa, wf, wb)
    qx, kx, vx = qk_prep(pa, pf, w["gq"], w["gk"], w["fb"])
    ox, qxb = attn_fwd(qx, kx, vx)
    xm = mix_out_fwd(x, ox, pb, wao, convw, wco, wpool, w["pscale"], w["w_o"], li)
    return xm, w, dict(x=x, h=h, pa=pa, pf=pf, pb=pb, kx=kx, vx=vx, ox=ox, qxb=qxb)


def ffn_half_fwd(xm, full, li, g2):
    n = full["w_ffn_out"].shape[0]
    w = dict(w1=full["w_ffn_in"], w2=full["w_ffn_out"].reshape(n, D_FF, D_MODEL), at=li, g2=_row(g2))
    y, u = ffn_fwd(xm, w["g2"], w["w1"], w["w2"], li)
    return y, w, dict(xm=xm, u=u)


def ffn_half_bwd(dx, w, s, gi, big):
    n = big["w_ffn_out"].shape[0]
    big = dict(big)
    dxm, du, act, h2, dyb, dg2 = ffn_bwd(s["xm"], w["g2"], dx, s["u"], w["w1"], w["w2"], w["at"])
    big["w_ffn_out"] = wgrad_into(act, dyb, "wgrad_ffn_out", big["w_ffn_out"].reshape(n, D_FF, D_MODEL),
                                  gi).reshape(big["w_ffn_out"].shape)
    big["w_ffn_in"] = wgrad_into(h2, du, "wgrad_ffn_in", big["w_ffn_in"], gi)
    return dxm, big, dict(norm_ffn_g=dg2[0])


def mix_bwd_weights(dxm, w, s, gi, big):
    big = dict(big)
    dpb, dox, dwo, dwao, dwco, dwpool, dconvw, dpscale = mix_out_bwd(
        dxm, s["ox"], s["pb"], w["wao"], w["convw"], w["wco"], w["wpool"], w["pscale"], w["w_o"], w["at"])
    dqx, dkx, dvx = attn_bwd(s["qxb"], s["kx"], s["vx"], dox)
    dpa, dpf, dgq, dgk, dfb = attn_bwd_post(s["pa"], s["pf"], dqx, dkx, dvx, w["gq"], w["gk"], w["fb"])
    big["w_in"] = w_in_unprep(wgrad(s["h"], dpa, "wgrad_in_qkv"), wgrad(s["h"], dpf, "wgrad_in_f"),
                              wgrad(s["h"], dpb, "wgrad_in_b"), big["w_in"], gi)
    big["w_attn_out"], big["w_conv_out"], big["pool_w"], big["w_o"] = branch_g_place(
        dwao, dwco, dwpool, dwo, (big["w_attn_out"], big["w_conv_out"], big["pool_w"], big["w_o"]), gi)
    sm = dict(forget_b=dfb[0, 0:N_HEADS], q_norm_g=dgq.reshape(N_HEADS, HEAD_DIM).sum(0),
              k_norm_g=dgk.reshape(N_HEADS, HEAD_DIM).sum(0), pool_scale=dpscale[0],
              conv_w=dconvw[0:CONV_K].reshape(-1))
    return (dpa, dpf, dpb), big, sm


def mix_bwd_input(dproj, dxm, w, s):
    dx, dg1 = in_proj_bwd(s["x"], w["g1"], dxm, *dproj, w["wa"], w["wf"], w["wb"])
    return dx, dict(norm_mix_g=dg1[0])


def mix_bwd(dxm, w, s, gi, big):
    dproj, big, sm = mix_bwd_weights(dxm, w, s, gi, big)
    dx, g1 = mix_bwd_input(dproj, dxm, w, s)
    return dx, big, {**sm, **g1}


def grad_buffers(full):
    return {n: lax.empty(full[n].shape, F32) for n in SPLIT}


def local_step(x, target, gathered, small):
    n_layers = small["norm_mix_g"].shape[0]
    done = []
    for l in range(n_layers):
        xm, wm, sm = mix_fwd(x, gathered, l, {n: v[l] for n, v in small.items()})
        x, wf, sf = ffn_half_fwd(xm, gathered, l, small["norm_ffn_g"][l])
        done.append((wm, sm, wf, sf))
    loss, dx = loss_head(x, target)
    big = grad_buffers(gathered)
    small_grads = [None] * n_layers
    for l in reversed(range(n_layers)):
        wm, sm, wf, sf = done[l]
        dxm, ffn_big, g_ffn = ffn_half_bwd(dx, wf, sf, l, {n: big[n] for n in FFN_W})
        dx, mix_big, g_mix = mix_bwd(dxm, wm, sm, l, {n: big[n] for n in MIX_W})
        big = {**ffn_big, **mix_big}
        small_grads[l] = {**g_ffn, **g_mix}
    return loss, dx, big, {n: jnp.stack([g[n] for g in small_grads]) for n in SMALL}


def adamw(w, g, m, v, name):
    R, C = w.shape
    tm = 256 if R % 256 == 0 else R

    def body(w_ref, g_ref, m_ref, v_ref, d_ref, nm_ref, nv_ref):
        gr = g_ref[...]
        m_new = ADAM_B1 * m_ref[...] + (1.0 - ADAM_B1) * gr
        v_new = ADAM_B2 * v_ref[...] + (1.0 - ADAM_B2) * jnp.square(gr)
        nm_ref[...] = m_new
        nv_ref[...] = v_new
        m_hat = m_new / (1.0 - ADAM_B1 ** ADAM_STEP)
        v_hat = v_new / (1.0 - ADAM_B2 ** ADAM_STEP)
        d_ref[...] = -ADAM_LR * (m_hat / (jnp.sqrt(v_hat) + ADAM_EPS) + ADAM_WD * w_ref[...])

    spec = _rows(tm, C)
    out = jax.ShapeDtypeStruct((R, C), F32)
    return pl.pallas_call(
        body,
        grid=(R // tm,),
        in_specs=[spec] * 4,
        out_specs=[spec] * 3,
        out_shape=[out] * 3,
        compiler_params=_params("parallel"),
        name=name,
    )(w, g, m, v)


MESH = pl.DeviceIdType.MESH
HBM_REF = pl.BlockSpec(memory_space=pl.ANY)
N_CHIPS = 4
N_DEV = 8
SMALL_SHAPE = (128, LANES)


def _mesh_pos():
    return lax.axis_index("x"), lax.axis_index("y"), lax.axis_index("c")


def _other_chips(x, y):
    return [(1 - x, y), (x, 1 - y), (1 - x, 1 - y)]


def _remote(src, dst, send_sem, recv_sem, to):
    return pltpu.make_async_remote_copy(src_ref=src, dst_ref=dst, send_sem=send_sem, recv_sem=recv_sem,
                                        device_id=to, device_id_type=MESH)


def _row_tile(rows):
    for tm in (256, 176, 128):
        if rows % tm == 0:
            return tm
    return rows


def _as4(a):
    return a.reshape(a.shape[0], a.shape[1], -1, a.shape[-1])


def _core_rows(buf, c):
    R, C = buf.shape[-2:]
    if R % 2:
        whole = (pl.ds(0, R), pl.ds(0, C))
        return whole, whole, False
    return (pl.ds(c * (R // 2), R // 2), pl.ds(0, C)), (pl.ds((1 - c) * (R // 2), R // 2), pl.ds(0, C)), True


def _half_shape(g):
    return g.shape[:-2] + (g.shape[-2] // 2, g.shape[-1])


def place_shard(chip, w, lo, hi, dtype, name, after=None):
    w3 = w.reshape(w.shape[0], -1, w.shape[-1])
    _, R, C = w3.shape
    tm = _row_tile(R)
    idle = [] if after is None else [after]

    def body(chip_ref, w_ref, *rest):
        rest[-1][...] = w_ref[...].astype(dtype)

    return pl.pallas_call(
        body,
        grid_spec=pltpu.PrefetchScalarGridSpec(
            num_scalar_prefetch=1,
            grid=(hi - lo, R // tm),
            in_specs=[pl.BlockSpec((None, tm, C), lambda l, i, chip: (lo + l, i, 0))] + [_whole(a) for a in idle],
            out_specs=pl.BlockSpec((None, None, tm, C), lambda l, i, chip: (l, chip[0], i, 0)),
        ),
        out_shape=jax.ShapeDtypeStruct((hi - lo, N_CHIPS, R, C), dtype),
        compiler_params=_params("parallel", "parallel"),
        name=name,
    )(chip, w3, *idle)


HBM_SPACE = pl.BlockSpec(memory_space=pltpu.HBM)
SEM_SPACE = pl.BlockSpec(memory_space=pltpu.SEMAPHORE)
IN_FLIGHT = pltpu.SideEffectType.DATAFLOW_SIDE_EFFECTING


def _sem_table(send_sems, recv_sems):
    return lambda t, j: (send_sems.at[t, j], recv_sems.at[t, j])


def _sem_per_peer(send_sems, recv_sems):
    return lambda t, j: (send_sems[j], recv_sems[j])


def _gather_ici(bufs, sem, sends=True, lands=True):
    x, y, c = _mesh_pos()
    me = 2 * x + y
    out, into = [], []
    for t, buf in enumerate(bufs):
        mine, _, _ = _core_rows(buf, c)
        part = lambda k, buf=buf, mine=mine: buf.at[pl.ds(0, buf.shape[0]), k, *mine]
        for j, (px, py) in enumerate(_other_chips(x, y)):
            if sends:
                out.append(_remote(part(me), part(me), *sem(t, j), (px, py, c)))
            if lands:
                into.append(_remote(part(2 * px + py), part(2 * px + py), *sem(t, j), (px, py, c)))
    return out, into


def _gather_d2d(bufs, send_sems, recv_sems, first):
    x, y, c = _mesh_pos()
    sends, lands = [], []
    for t, buf in enumerate(bufs):
        mine, theirs, split = _core_rows(buf, c)
        if not split:
            continue
        for j, (px, py) in enumerate(_other_chips(x, y)):
            part = lambda half, buf=buf, k=2 * px + py: buf.at[pl.ds(0, buf.shape[0]), k, *half]
            sems = (send_sems.at[t, first + j], recv_sems.at[t, first + j], (x, y, 1 - c))
            sends.append(_remote(part(mine), part(mine), *sems))
            lands.append(_remote(part(theirs), part(theirs), *sems))
    return sends, lands


def gather_shards(bufs):
    n = len(bufs)

    def body(*refs):
        outs = refs[n:2 * n]
        send_sems, recv_sems = refs[2 * n:]
        ici_out, ici_in = _gather_ici(outs, _sem_table(send_sems, recv_sems))
        d2d_out, d2d_in = _gather_d2d(outs, send_sems, recv_sems, 3)
        for cp in ici_out:
            cp.start()
        for cp in ici_in:
            cp.wait_recv()
        for cp in d2d_out:
            cp.start()
        for cp in d2d_in:
            cp.wait_recv()
        for cp in ici_out + d2d_out:
            cp.wait_send()

    return pl.pallas_call(
        body,
        in_specs=[HBM_REF] * n,
        out_specs=[HBM_REF] * n,
        out_shape=[jax.ShapeDtypeStruct(b.shape, b.dtype) for b in bufs],
        input_output_aliases={t: t for t in range(n)},
        scratch_shapes=[pltpu.SemaphoreType.DMA((n, 6)), pltpu.SemaphoreType.DMA((n, 6))],
        name="gather_shards",
    )(*bufs)


def gather_start(bufs):
    n = len(bufs)

    def body(*refs):
        send_sems, recv_sems = refs[n:n + 3], refs[n + 3:n + 6]
        outs = refs[n + 6:2 * n + 6]
        token = refs[2 * n + 6]
        for cp in _gather_ici(outs, _sem_per_peer(send_sems, recv_sems), lands=False)[0]:
            cp.start()
        token[...] = jnp.zeros_like(token)

    res = pl.pallas_call(
        body,
        in_specs=[HBM_SPACE] * n,
        out_specs=[SEM_SPACE] * 6 + [HBM_SPACE] * n + [pl.BlockSpec(memory_space=pltpu.VMEM)],
        out_shape=[pltpu.SemaphoreType.DMA(())] * 6
        + [pltpu.HBM(b.shape, b.dtype) for b in bufs] + [jax.ShapeDtypeStruct((8, LANES), F32)],
        input_output_aliases={t: t + 6 for t in range(n)},
        compiler_params=pltpu.CompilerParams(has_side_effects=IN_FLIGHT),
        name="gather_start",
    )(*[pltpu.with_memory_space_constraint(b, pltpu.HBM) for b in bufs])
    return list(res[0:3]), list(res[3:6]), list(res[6:n + 6]), res[n + 6]


def gather_wait(send_sems, recv_sems, bufs, after):
    n = len(bufs)

    def body(*refs):
        ins_send, ins_recv = refs[n:n + 3], refs[n + 3:n + 6]
        outs = refs[n + 7:]
        sends, lands = _gather_ici(outs, _sem_per_peer(ins_send, ins_recv))
        for cp in lands:
            cp.wait_recv()
        for cp in sends:
            cp.wait_send()

    return list(pl.pallas_call(
        body,
        in_specs=[HBM_SPACE] * n + [SEM_SPACE] * 6 + [HBM_REF],
        out_specs=[HBM_SPACE] * n,
        out_shape=[pltpu.HBM(b.shape, b.dtype) for b in bufs],
        input_output_aliases={t: t for t in range(n)},
        compiler_params=pltpu.CompilerParams(has_side_effects=IN_FLIGHT),
        name="gather_wait",
    )(*bufs, *send_sems, *recv_sems, after))


def gather_forward(bufs):
    n = len(bufs)

    def body(*refs):
        outs = refs[n:2 * n]
        send_sems, recv_sems = refs[2 * n:]
        sends, lands = _gather_d2d(outs, send_sems, recv_sems, 0)
        for cp in sends:
            cp.start()
        for cp in lands:
            cp.wait_recv()
        for cp in sends:
            cp.wait_send()

    return list(pl.pallas_call(
        body,
        in_specs=[HBM_REF] * n,
        out_specs=[HBM_REF] * n,
        out_shape=[jax.ShapeDtypeStruct(b.shape, b.dtype) for b in bufs],
        input_output_aliases={t: t for t in range(n)},
        scratch_shapes=[pltpu.SemaphoreType.DMA((n, 3)), pltpu.SemaphoreType.DMA((n, 3))],
        name="gather_forward",
    )(*bufs))


def pair_exchange(grads):
    n = len(grads)

    def body(*refs):
        ins, outs = refs[:n], refs[n:2 * n]
        send_sems, recv_sems = refs[2 * n:]
        cps = _pair_copies(ins, outs, lambda t: (send_sems.at[t], recv_sems.at[t]))
        for cp in cps:
            cp.start()
        for cp in cps:
            cp.wait()

    return list(pl.pallas_call(
        body,
        in_specs=[HBM_REF] * n,
        out_specs=[HBM_REF] * n,
        out_shape=[jax.ShapeDtypeStruct(_half_shape(g), g.dtype) for g in grads],
        scratch_shapes=[pltpu.SemaphoreType.DMA((n,)), pltpu.SemaphoreType.DMA((n,))],
        name="pair_exchange",
    )(*grads))


def _pair_copies(grads, lands, sem):
    x, y, c = _mesh_pos()
    cps = []
    for t in range(len(grads)):
        _, theirs, _ = _core_rows(grads[t], c)
        src = grads[t].at[pl.ds(0, grads[t].shape[0]), pl.ds(0, N_CHIPS), *theirs]
        cps.append(_remote(src, lands[t], *sem(t), (x, y, 1 - c)))
    return cps


def pair_exchange_start(grads):
    n = len(grads)
    lands = [lax.empty(_half_shape(g), g.dtype) for g in grads]

    def body(*refs):
        send_sem, recv_sem = refs[2 * n:2 * n + 2]
        outs = refs[2 * n + 2:4 * n + 2]
        token = refs[4 * n + 2]
        for cp in _pair_copies(outs[:n], outs[n:], lambda t: (send_sem, recv_sem)):
            cp.start()
        token[...] = jnp.zeros_like(token)

    res = pl.pallas_call(
        body,
        in_specs=[HBM_SPACE] * (2 * n),
        out_specs=[SEM_SPACE] * 2 + [HBM_SPACE] * (2 * n) + [pl.BlockSpec(memory_space=pltpu.VMEM)],
        out_shape=[pltpu.SemaphoreType.DMA(())] * 2 + [pltpu.HBM(a.shape, a.dtype) for a in list(grads) + lands]
        + [jax.ShapeDtypeStruct((8, LANES), F32)],
        input_output_aliases={t: t + 2 for t in range(2 * n)},
        compiler_params=pltpu.CompilerParams(has_side_effects=IN_FLIGHT),
        name="pair_exchange_start",
    )(*[pltpu.with_memory_space_constraint(a, pltpu.HBM) for a in list(grads) + lands])
    return res[0], res[1], list(res[2:n + 2]), list(res[n + 2:2 * n + 2]), res[2 * n + 2]


def pair_exchange_wait(send_sem, recv_sem, grads, lands, after):
    n = len(grads)

    def body(*refs):
        in_send, in_recv = refs[2 * n:2 * n + 2]
        outs = refs[2 * n + 3:]
        for cp in _pair_copies(outs[:n], outs[n:], lambda t: (in_send, in_recv)):
            cp.wait()

    res = pl.pallas_call(
        body,
        in_specs=[HBM_SPACE] * (2 * n) + [SEM_SPACE] * 2 + [HBM_REF],
        out_specs=[HBM_SPACE] * (2 * n),
        out_shape=[pltpu.HBM(a.shape, a.dtype) for a in list(grads) + list(lands)],
        input_output_aliases={t: t for t in range(2 * n)},
        compiler_params=pltpu.CompilerParams(has_side_effects=IN_FLIGHT),
        name="pair_exchange_wait",
    )(*grads, *lands, send_sem, recv_sem, after)
    return list(res[:n]), list(res[n:])


def pair_sum(core, g, t, name):
    n, _, hr, hc = t.shape
    R, C = g.shape[2:]
    tm = _row_tile(hr)
    per = hr // tm

    def body(c_ref, g_ref, t_ref, o_ref):
        o_ref[...] = (g_ref[...] + t_ref[...]).astype(BF16)

    tile = pl.BlockSpec((None, tm, hc), lambda a, i, c: (a, i, 0))
    out = pl.pallas_call(
        body,
        grid_spec=pltpu.PrefetchScalarGridSpec(
            num_scalar_prefetch=1,
            grid=(n * N_CHIPS, per),
            in_specs=[pl.BlockSpec((None, tm, hc), lambda a, i, c: (a, per * c[0] + i, 0)), tile],
            out_specs=tile,
        ),
        out_shape=jax.ShapeDtypeStruct((n * N_CHIPS, hr, hc), BF16),
        compiler_params=_params("parallel", "parallel"),
        name=name,
    )(core, g.reshape(n * N_CHIPS, R, C), t.reshape(n * N_CHIPS, hr, hc))
    return out.reshape(t.shape)


def _chip_copies(sums, recv, sem):
    x, y, c = _mesh_pos()
    cps = []
    for t in range(len(sums)):
        for j, (px, py) in enumerate(_other_chips(x, y)):
            src = sums[t].at[pl.ds(0, sums[t].shape[0]), 2 * px + py]
            cps.append(_remote(src, recv[t].at[j], *sem(t, j), (px, py, c)))
    return cps


def _recv_shape(s):
    return (N_CHIPS - 1, s.shape[0]) + s.shape[2:]


def chip_exchange(sums):
    n = len(sums)

    def body(*refs):
        ins, outs = refs[:n], refs[n:2 * n]
        cps = _chip_copies(ins, outs, _sem_table(*refs[2 * n:]))
        for cp in cps:
            cp.start()
        for cp in cps:
            cp.wait()

    return list(pl.pallas_call(
        body,
        in_specs=[HBM_REF] * n,
        out_specs=[HBM_REF] * n,
        out_shape=[jax.ShapeDtypeStruct(_recv_shape(s), s.dtype) for s in sums],
        scratch_shapes=[pltpu.SemaphoreType.DMA((n, 3)), pltpu.SemaphoreType.DMA((n, 3))],
        name="chip_exchange",
    )(*sums))


def chip_exchange_start(sums):
    n = len(sums)
    lands = [lax.empty(_recv_shape(s), s.dtype) for s in sums]

    def body(*refs):
        send_sems, recv_sems = refs[2 * n:2 * n + 3], refs[2 * n + 3:2 * n + 6]
        outs = refs[2 * n + 6:4 * n + 6]
        token = refs[4 * n + 6]
        for cp in _chip_copies(outs[:n], outs[n:2 * n], _sem_per_peer(send_sems, recv_sems)):
            cp.start()
        token[...] = jnp.zeros_like(token)

    res = pl.pallas_call(
        body,
        in_specs=[HBM_SPACE] * (2 * n),
        out_specs=[SEM_SPACE] * 6 + [HBM_SPACE] * (2 * n) + [pl.BlockSpec(memory_space=pltpu.VMEM)],
        out_shape=[pltpu.SemaphoreType.DMA(())] * 6 + [pltpu.HBM(a.shape, a.dtype) for a in list(sums) + lands]
        + [jax.ShapeDtypeStruct((8, LANES), F32)],
        input_output_aliases={t: t + 6 for t in range(2 * n)},
        compiler_params=pltpu.CompilerParams(has_side_effects=IN_FLIGHT),
        name="chip_exchange_start",
    )(*[pltpu.with_memory_space_constraint(a, pltpu.HBM) for a in list(sums) + lands])
    return list(res[0:3]), list(res[3:6]), list(res[6:n + 6]), list(res[n + 6:2 * n + 6]), res[2 * n + 6]


def chip_exchange_wait(send_sems, recv_sems, sums, lands, after):
    n = len(sums)

    def body(*refs):
        ins_send, ins_recv = refs[2 * n:2 * n + 3], refs[2 * n + 3:2 * n + 6]
        outs = refs[2 * n + 7:]
        for cp in _chip_copies(outs[:n], outs[n:], _sem_per_peer(ins_send, ins_recv)):
            cp.wait()

    res = pl.pallas_call(
        body,
        in_specs=[HBM_SPACE] * (2 * n) + [SEM_SPACE] * 6 + [HBM_REF],
        out_specs=[HBM_SPACE] * (2 * n),
        out_shape=[pltpu.HBM(a.shape, a.dtype) for a in list(sums) + list(lands)],
        input_output_aliases={t: t for t in range(2 * n)},
        compiler_params=pltpu.CompilerParams(has_side_effects=IN_FLIGHT),
        name="chip_exchange_wait",
    )(*sums, *lands, *send_sems, *recv_sems, after)
    return list(res[:n]), list(res[n:])


def chip_sum_into(core, chip, recv, sums, total, lo, name):
    _, n, hr, hc = recv.shape
    tm = _row_tile(hr)
    per = hr // tm

    def body(c_ref, k_ref, r_ref, s_ref, t_ref, o_ref):
        acc = s_ref[...].astype(F32)
        for j in range(N_CHIPS - 1):
            acc = acc + r_ref[j].astype(F32)
        o_ref[...] = acc

    return pl.pallas_call(
        body,
        grid_spec=pltpu.PrefetchScalarGridSpec(
            num_scalar_prefetch=2,
            grid=(n, per),
            in_specs=[pl.BlockSpec((N_CHIPS - 1, None, tm, hc), lambda a, i, c, k: (0, a, i, 0)),
                      pl.BlockSpec((None, None, tm, hc), lambda a, i, c, k: (a, k[0], i, 0)),
                      HBM_REF],
            out_specs=pl.BlockSpec((None, tm, hc), lambda a, i, c, k: (lo + a, per * c[0] + i, 0)),
        ),
        out_shape=jax.ShapeDtypeStruct(total.shape, F32),
        input_output_aliases={4: 0},
        compiler_params=_params("parallel", "parallel"),
        name=name,
    )(core, chip, recv, sums, total)


def sibling_share(totals):
    n = len(totals)

    def body(*refs):
        outs = refs[n:2 * n]
        send_sems, recv_sems = refs[2 * n:]
        x, y, c = _mesh_pos()
        half = lambda t, which: outs[t].at[pl.ds(0, DEPTH), *_core_rows(outs[t], c)[which]]
        sent = [_remote(half(t, 0), half(t, 0), send_sems.at[t], recv_sems.at[t], (x, y, 1 - c)) for t in range(n)]
        for cp in sent:
            cp.start()
        for t in range(n):
            _remote(half(t, 1), half(t, 1), send_sems.at[t], recv_sems.at[t], (x, y, 1 - c)).wait_recv()
        for cp in sent:
            cp.wait_send()

    return pl.pallas_call(
        body,
        in_specs=[HBM_REF] * n,
        out_specs=[HBM_REF] * n,
        out_shape=[jax.ShapeDtypeStruct(t.shape, t.dtype) for t in totals],
        input_output_aliases={t: t for t in range(n)},
        scratch_shapes=[pltpu.SemaphoreType.DMA((n,)), pltpu.SemaphoreType.DMA((n,))],
        name="sibling_share",
    )(*totals)


def small_allgather(small):
    def body(s_ref, a_ref, send_sems, recv_sems, local_sem):
        x, y, c = _mesh_pos()
        me = 4 * x + 2 * y + c
        own = pltpu.make_async_copy(s_ref, a_ref.at[me], local_sem)
        own.start()
        sent = []
        for k in range(1, N_DEV):
            peer = (x ^ (k >> 2), y ^ ((k >> 1) & 1), c ^ (k & 1))
            cp = _remote(s_ref, a_ref.at[me], send_sems.at[k - 1], recv_sems.at[k - 1], peer)
            cp.start()
            sent.append(cp)
        for k in range(1, N_DEV):
            px, py, pc = x ^ (k >> 2), y ^ ((k >> 1) & 1), c ^ (k & 1)
            _remote(s_ref, a_ref.at[4 * px + 2 * py + pc], send_sems.at[k - 1], recv_sems.at[k - 1], (px, py, pc)).wait_recv()
        for cp in sent:
            cp.wait_send()
        own.wait()

    return pl.pallas_call(
        body,
        in_specs=[HBM_REF],
        out_specs=HBM_REF,
        out_shape=jax.ShapeDtypeStruct((N_DEV,) + SMALL_SHAPE, small.dtype),
        scratch_shapes=[pltpu.SemaphoreType.DMA((N_DEV - 1,)), pltpu.SemaphoreType.DMA((N_DEV - 1,)), pltpu.SemaphoreType.DMA],
        name="small_allgather",
    )(small)


def small_sum(blocks):
    def body(a_ref, o_ref):
        acc = a_ref[0]
        for d in range(1, N_DEV):
            acc = acc + a_ref[d]
        o_ref[...] = acc

    return pl.pallas_call(
        body,
        in_specs=[pl.BlockSpec(memory_space=pltpu.VMEM)],
        out_specs=pl.BlockSpec(memory_space=pltpu.VMEM),
        out_shape=jax.ShapeDtypeStruct(SMALL_SHAPE, F32),
        name="small_sum",
    )(blocks)


def pack_small(grads, loss):
    flat = jnp.concatenate([grads[n].reshape(-1) for n in SMALL] + [loss.reshape(-1)])
    size = SMALL_SHAPE[0] * SMALL_SHAPE[1]
    return jnp.pad(flat, (0, size - flat.shape[0])).reshape(SMALL_SHAPE)


def unpack_small(packed):
    flat = packed.reshape(-1)
    out, off = {}, 0
    for n, size in SMALL.items():
        out[n] = flat[off:off + DEPTH * size].reshape(DEPTH, size)
        off += DEPTH * size
    return out, flat[off]


def kernel(x, norm_mix_g, w_in, forget_b, q_norm_g, k_norm_g, w_attn_out, conv_w, w_conv_out, pool_w, pool_scale, w_o, norm_ffn_g, w_ffn_in, w_ffn_out, loss_target, m_norm_mix_g, m_w_in, m_forget_b, m_q_norm_g, m_k_norm_g, m_w_attn_out, m_conv_w, m_w_conv_out, m_pool_w, m_pool_scale, m_w_o, m_norm_ffn_g, m_w_ffn_in, m_w_ffn_out, v_norm_mix_g, v_w_in, v_forget_b, v_q_norm_g, v_k_norm_g, v_w_attn_out, v_conv_w, v_w_conv_out, v_pool_w, v_pool_scale, v_w_o, v_norm_ffn_g, v_w_ffn_in, v_w_ffn_out):
    given = dict(locals())
    weights = {n: given[n] for n in WEIGHTS}

    core = lax.axis_index("c").astype(jnp.int32)
    chip = (2 * lax.axis_index("x") + lax.axis_index("y")).astype(jnp.int32)

    core, chip = core.reshape(1), chip.reshape(1)
    mix_names, all_names = MIX_W + ["conv_w"], MIX_W + FFN_W + ["conv_w"]

    def placed(lo, hi, names, after=None):
        piece = None if after is None else _as4(next(iter(after.values())))[0, 0, 0:HALO, 0:LANES]
        return [place_shard(chip, weights[n], lo, hi, F32 if n == "conv_w" else BF16, "place_" + n,
                            piece if i == 0 else None) for i, n in enumerate(names)]

    def as_weights(bufs, names):
        return {n: b.reshape(b.shape[:2] + weights[n].shape[1:]) for n, b in zip(names, bufs)}

    def landed(start, names, after):
        return as_weights(gather_forward(gather_wait(*start[:3], after)), names)

    def layer_small(l, *tokens):
        small = {n: weights[n][l] for n in SMALL if n != "conv_w"}
        for t in tokens:
            small["norm_mix_g"] = small["norm_mix_g"] + t[0, 0]
        return small

    done = [None] * DEPTH
    first = gather_start(placed(0, 1, mix_names))
    next_bufs = (placed(0, 1, FFN_W), placed(1, 2, all_names))
    first = landed(first, mix_names, next_bufs[1][0])
    (ffn0, layer1), _ = lax.optimization_barrier((next_bufs, first["conv_w"]))
    ffn0, layer1 = gather_start(ffn0), gather_start(layer1)
    xm, wm, sm = mix_fwd(x[0], first, 0, layer_small(0, ffn0[3], layer1[3]))
    xs, wf, sf = ffn_half_fwd(xm, landed(ffn0, FFN_W, xm), 0, weights["norm_ffn_g"][0])
    done[0] = (wm, sm, wf, sf)
    full = landed(layer1, all_names, xs)
    rest = gather_start(placed(2, DEPTH, all_names, after=full))
    xm, wm, sm = mix_fwd(xs, full, 0, layer_small(1, rest[3]))
    xs, wf, sf = ffn_half_fwd(xm, full, 0, weights["norm_ffn_g"][1])
    done[1] = (wm, sm, wf, sf)
    full = landed(rest, all_names, xs)
    for l in range(2, DEPTH):
        xm, wm, sm = mix_fwd(xs, full, l - 2, layer_small(l))
        xs, wf, sf = ffn_half_fwd(xm, full, l - 2, weights["norm_ffn_g"][l])
        done[l] = (wm, sm, wf, sf)
    loss, dx = loss_head(xs, loss_target[0])

    small_grads = [None] * DEPTH
    totals = {n: lax.empty((DEPTH,) + _as4(weights[n][None]).shape[2:], F32) for n in SPLIT}

    def buffers(names, n_layers):
        return {n: lax.empty((n_layers, N_CHIPS) + weights[n].shape[1:], F32) for n in names}

    def pair_sums(names, grads, theirs):
        return [pair_sum(core, a, t, "pair_sum_" + n) for n, a, t in zip(names, grads, theirs)]

    def add_chips(names, recv, sums, lo):
        for n, r, s in zip(names, recv, sums):
            totals[n] = chip_sum_into(core, chip, r, s, totals[n], lo, "chip_sum_" + n)

    big = buffers(SPLIT, DEPTH - 1)
    for l in reversed(range(1, DEPTH)):
        wm, sm, wf, sf = done[l]
        dxm, ffn_big, g_ffn = ffn_half_bwd(dx, wf, sf, l - 1, {n: big[n] for n in FFN_W})
        dx, mix_big, g_mix = mix_bwd(dxm, wm, sm, l - 1, {n: big[n] for n in MIX_W})
        big = {**ffn_big, **mix_big}
        small_grads[l] = {**g_ffn, **g_mix}
    pair = pair_exchange_start([_as4(big[n]) for n in SPLIT])

    wm, sm, wf, sf = done[0]
    wf = dict(wf, g2=wf["g2"] + pair[4][0:1, 0:1])
    dxm, ffn_big, g_ffn = ffn_half_bwd(dx, wf, sf, 0, buffers(FFN_W, 1))
    chips = chip_exchange_start(pair_sums(SPLIT, *pair_exchange_wait(*pair[:4], ffn_big["w_ffn_in"])))
    ffn_grads = [_as4(ffn_big[n]) for n in FFN_W]
    ffn_chips = chip_exchange_start(pair_sums(FFN_W, ffn_grads, pair_exchange(ffn_grads)))
    wm = dict(wm, pscale=wm["pscale"] + chips[4][0:1, 0:1] + ffn_chips[4][0:1, 0:1])
    dproj, mix_big, g_mix = mix_bwd_weights(dxm, wm, sm, 0, buffers(MIX_W, 1))
    mix_grads = [_as4(mix_big[n]) for n in MIX_W]
    mix_chips = chip_exchange_start(pair_sums(MIX_W, mix_grads, pair_exchange(mix_grads)))
    dx, g_in = mix_bwd_input(dproj, dxm, dict(wm, g1=wm["g1"] + mix_chips[4][0:1, 0:1]), sm)
    small_grads[0] = {**g_ffn, **g_mix, **g_in}

    reduced, deltas, new_m, new_v = {}, {}, {}, {}

    def update(names):
        for n in names:
            w = weights[n]
            flat = (-1, w.shape[-1])
            d, nm, nv = adamw(w.reshape(flat), reduced[n].reshape(flat), given["m_" + n].reshape(flat),
                              given["v_" + n].reshape(flat), "adamw_" + n)
            deltas[n], new_m[n], new_v[n] = d.reshape(w.shape), nm.reshape(w.shape), nv.reshape(w.shape)

    def share(names):
        for n, t in zip(names, sibling_share([totals[n] for n in names])):
            reduced[n] = t.reshape(weights[n].shape)

    sums, recv = chip_exchange_wait(*chips[:4], dx)
    add_chips(SPLIT, recv, sums, 1)
    sums, recv = chip_exchange_wait(*ffn_chips[:4], dx)
    add_chips(FFN_W, recv, sums, 0)
    share(FFN_W)
    update(FFN_W)
    sums, recv = chip_exchange_wait(*mix_chips[:4], deltas["w_ffn_in"])
    add_chips(MIX_W, recv, sums, 0)
    share(MIX_W)

    small_grads = {n: jnp.stack([g[n] for g in small_grads]) for n in SMALL}
    small_total, loss_sum = unpack_small(small_sum(small_allgather(pack_small(small_grads, loss))))
    chip = chip[0]
    cols = D_CONV // N_CHIPS
    small_total["conv_w"] = lax.dynamic_slice_in_dim(small_total["conv_w"].reshape(DEPTH, CONV_K, D_CONV), chip * cols, cols, axis=2)
    for n in SMALL:
        reduced[n] = small_total[n].reshape(weights[n].shape)
    update(MIX_W + list(SMALL))

    return (loss_sum, dx[None], *[reduced[n] for n in WEIGHTS], *[deltas[n] for n in WEIGHTS],
            *[new_m[n] for n in WEIGHTS], *[new_v[n] for n in WEIGHTS])
```

```python
import functools

import numpy as np
import jax
import jax.numpy as jnp
from jax import lax
from jax.experimental import pallas as pl
from jax.experimental.pallas import tpu as pltpu

F32 = jnp.float32
BF16 = jnp.bfloat16

D_MODEL = 1024
DEPTH = 4
HEAD_DIM = 64
N_HEADS = 8
D_ATTN = 512
D_CONV = 256
D_POOL = 256
D_FF = 2816
D_IN = 5640
CONV_K = 3
POOL_WINDOWS = (2, 4, 8, 16)
N_GROUPS = len(POOL_WINDOWS)
EPS = 1e-6
ADAM_LR, ADAM_B1, ADAM_B2, ADAM_EPS, ADAM_WD, ADAM_STEP = 0.001, 0.9, 0.999, 1e-08, 0.01, 10

D_QKV = 3 * D_ATTN
D_F = 128
D_B = 3 * D_CONV + D_POOL + 3 * D_MODEL
D_LOCAL = 3 * D_CONV + D_POOL

LANES = 128
D_HEADS = N_HEADS * LANES
HALO = 16
VMEM_LIMIT = 56 * 1024 * 1024
NEG = -1e30
LOG2E = 1.4426950408889634
LN2 = 0.6931471805599453

TM = 256
TM_MIX = 256
TQ = 1024

LANE_C = 64
LANE_ONE = 67
LANE_LSE = 70
N_PIECES = 3


def _dot(a, b):
    return jnp.dot(a, b, preferred_element_type=F32)


def _dot_nt(a, b):
    return lax.dot_general(a, b, (((1,), (1,)), ((), ())), preferred_element_type=F32)


def _dot_tn(a, b):
    return lax.dot_general(a, b, (((0,), (0,)), ((), ())), preferred_element_type=F32)


def _params(*sem):
    return pltpu.CompilerParams(dimension_semantics=sem, vmem_limit_bytes=VMEM_LIMIT)


def _rows(tm, n):
    return pl.BlockSpec((tm, n), lambda i, *_: (i, 0))


def _whole(a):
    nd = a.ndim
    return pl.BlockSpec(a.shape, lambda *_: (0,) * nd)


def _layer(shape):
    nd = len(shape)
    return pl.BlockSpec((None,) + tuple(shape), lambda *a: (a[-1][0],) + (0,) * nd)


def _layer_index(l):
    return jnp.full((1,), l, jnp.int32)


def _split_bf16(x):
    hi = x.astype(BF16)
    lo = (x - hi.astype(F32)).astype(BF16)
    return hi, lo


def _pieces(x):
    p1 = x.astype(BF16)
    r1 = x - p1.astype(F32)
    p2 = r1.astype(BF16)
    p3 = (r1 - p2.astype(F32)).astype(BF16)
    return p1, p2, p3


def _sigmoid(x):
    return 1.0 / (1.0 + jnp.exp(-x))


def w_in_prep(win, l):
    tr = 256
    n = D_IN // 4
    v_rest = D_QKV - n
    b0 = v_rest + N_HEADS

    def body(s0, s1, s2, s3, wa_ref, wf_ref, wb_ref):
        b = s1[...]
        wa_ref[...] = jnp.concatenate([s0[...], b[:, 0:v_rest]], axis=1)
        wf_ref[...] = jnp.concatenate([b[:, v_rest:b0], jnp.zeros((tr, D_F - N_HEADS), b.dtype)], axis=1)
        wb_ref[...] = jnp.concatenate([b[:, b0:n], s2[...], s3[...]], axis=1)

    shard = lambda j: pl.BlockSpec((None, None, tr, n), lambda i: (l, j, i, 0))
    return pl.pallas_call(
        body,
        grid=(D_MODEL // tr,),
        in_specs=[shard(0), shard(1), shard(2), shard(3)],
        out_specs=[_rows(tr, D_QKV), _rows(tr, D_F), _rows(tr, D_B)],
        out_shape=[jax.ShapeDtypeStruct((D_MODEL, D_QKV), win.dtype), jax.ShapeDtypeStruct((D_MODEL, D_F), win.dtype),
                   jax.ShapeDtypeStruct((D_MODEL, D_B), win.dtype)],
        compiler_params=_params("parallel"),
        name="w_in_prep",
    )(win, win, win, win)


def _into_layer(buf):
    return dict(in_spec=HBM_REF, out_shape=jax.ShapeDtypeStruct(buf.shape, buf.dtype), aliases={1: 0})


def w_in_unprep(dwa, dwf, dwb, buf, l):
    tr = 256
    n = D_IN // 4
    v_rest = D_QKV - n
    b1 = n - v_rest - N_HEADS
    place = _into_layer(buf)

    def body(l_ref, buf_ref, a_ref, f_ref, b_ref, o_ref):
        a = a_ref[...]
        b = b_ref[...]
        o_ref[0] = a[:, 0:n]
        o_ref[1] = jnp.concatenate([a[:, n:D_QKV], f_ref[:, 0:N_HEADS], b[:, 0:b1]], axis=1)
        o_ref[2] = b[:, b1:b1 + n]
        o_ref[3] = b[:, b1 + n:D_B]

    return pl.pallas_call(
        body,
        grid_spec=pltpu.PrefetchScalarGridSpec(
            num_scalar_prefetch=1,
            grid=(D_MODEL // tr,),
            in_specs=[place["in_spec"], _rows(tr, D_QKV), _rows(tr, D_F), _rows(tr, D_B)],
            out_specs=pl.BlockSpec((None, 4, tr, n), lambda i, l: (l[0], 0, i, 0)),
        ),
        out_shape=place["out_shape"],
        input_output_aliases=place["aliases"],
        compiler_params=_params("parallel"),
        name="w_in_unprep",
    )(_layer_index(l), buf, dwa, dwf, dwb)


def branch_w_prep(wao, wco, pw, cw, l):
    gd = D_POOL // N_GROUPS
    od = D_MODEL // N_GROUPS

    def body(wao_ref, wco_ref, pw_ref, cw_ref, ao_ref, co_ref, po_ref, co8_ref):
        ao_ref[...] = jnp.concatenate([wao_ref[j] for j in range(4)], axis=1)
        co_ref[...] = jnp.concatenate([wco_ref[j] for j in range(4)], axis=1)
        zero = jnp.zeros((gd, od), co_ref.dtype)
        po_ref[...] = jnp.concatenate(
            [jnp.concatenate([jnp.concatenate([pw_ref[j, g] for j in range(4)], axis=1) if g2 == g else zero
                              for g2 in range(N_GROUPS)], axis=1) for g in range(N_GROUPS)], axis=0)
        co8_ref[...] = jnp.zeros_like(co8_ref)
        co8_ref[0:CONV_K, :] = jnp.concatenate([cw_ref[j] for j in range(4)], axis=1)

    sel = lambda *shape: pl.BlockSpec((None,) + shape, lambda i: (l,) + (0,) * len(shape))
    return pl.pallas_call(
        body,
        grid=(1,),
        in_specs=[sel(4, D_ATTN, D_MODEL // 4), sel(4, D_CONV, D_MODEL // 4), sel(4, N_GROUPS, gd, od // 4),
                  sel(4, CONV_K, D_CONV // 4)],
        out_specs=[pl.BlockSpec((D_ATTN, D_MODEL), lambda i: (0, 0)), pl.BlockSpec((D_CONV, D_MODEL), lambda i: (0, 0)),
                   pl.BlockSpec((D_POOL, D_MODEL), lambda i: (0, 0)), pl.BlockSpec((8, D_CONV), lambda i: (0, 0))],
        out_shape=[jax.ShapeDtypeStruct((D_ATTN, D_MODEL), BF16), jax.ShapeDtypeStruct((D_CONV, D_MODEL), BF16),
                   jax.ShapeDtypeStruct((D_POOL, D_MODEL), BF16), jax.ShapeDtypeStruct((8, D_CONV), F32)],
        name="branch_w_prep",
    )(wao, wco, pw, cw)


def branch_g_place(dwao, dwco, dwpool, dwo, bufs, l):
    gd = D_POOL // N_GROUPS
    od = D_MODEL // N_GROUPS
    q = D_MODEL // 4

    def body(l_ref, b0, b1, b2, b3, a_ref, c_ref, p_ref, w_ref, ao_ref, co_ref, po_ref, wo_ref):
        a = a_ref[...]
        c = c_ref[...]
        p = p_ref[...]
        for j in range(4):
            ao_ref[j] = a[:, j * q:(j + 1) * q]
            co_ref[j] = c[:, j * q:(j + 1) * q]
            wo_ref[j] = w_ref[j * q:(j + 1) * q, :]
            for g in range(N_GROUPS):
                c0 = g * od + j * (od // 4)
                po_ref[j, g] = p[g * gd:(g + 1) * gd, c0:c0 + od // 4]

    whole = lambda a: pl.BlockSpec(a.shape, lambda i, l: (0,) * a.ndim)
    layer = lambda b: pl.BlockSpec((None,) + b.shape[1:], lambda i, l: (l[0],) + (0,) * (b.ndim - 1))
    return pl.pallas_call(
        body,
        grid_spec=pltpu.PrefetchScalarGridSpec(
            num_scalar_prefetch=1,
            grid=(1,),
            in_specs=[HBM_REF] * 4 + [whole(dwao), whole(dwco), whole(dwpool), whole(dwo)],
            out_specs=[layer(b) for b in bufs],
        ),
        out_shape=[jax.ShapeDtypeStruct(b.shape, b.dtype) for b in bufs],
        input_output_aliases={1: 0, 2: 1, 3: 2, 4: 3},
        compiler_params=_params("arbitrary"),
        name="branch_g_place",
    )(_layer_index(l), *bufs, dwao, dwco, dwpool, dwo)


def in_proj_fwd(x, g, wa, wf, wb):
    S = x.shape[0]

    def body(x_ref, g_ref, wa_ref, wf_ref, wb_ref, h_ref, pa_ref, pf_ref, pb_ref):
        xf = x_ref[...]
        r = lax.rsqrt(jnp.mean(xf * xf, axis=-1, keepdims=True) + EPS)
        h = (xf * r * g_ref[...]).astype(BF16)
        h_ref[...] = h
        pa_ref[...] = _dot(h, wa_ref[...]).astype(BF16)
        pf_ref[...] = _dot(h, wf_ref[...])
        pb_ref[...] = _dot(h, wb_ref[...]).astype(BF16)

    return pl.pallas_call(
        body,
        grid=(S // TM,),
        in_specs=[_rows(TM, D_MODEL), _whole(g), _whole(wa), _whole(wf), _whole(wb)],
        out_specs=[_rows(TM, D_MODEL), _rows(TM, D_QKV), _rows(TM, D_F), _rows(TM, D_B)],
        out_shape=[
            jax.ShapeDtypeStruct((S, D_MODEL), BF16),
            jax.ShapeDtypeStruct((S, D_QKV), BF16),
            jax.ShapeDtypeStruct((S, D_F), F32),
            jax.ShapeDtypeStruct((S, D_B), BF16),
        ],
        compiler_params=_params("parallel"),
        name="in_proj_fwd",
    )(x, g, wa, wf, wb)


def in_proj_bwd(x, g, dxm, dpa, dpf, dpb, wa, wf, wb):
    S = x.shape[0]

    def body(x_ref, g_ref, dxm_ref, dpa_ref, dpf_ref, dpb_ref, wa_ref, wf_ref, wb_ref, dx_ref, dg_ref):
        @pl.when(pl.program_id(0) == 0)
        def _():
            dg_ref[...] = jnp.zeros_like(dg_ref)

        dh = _dot_nt(dpa_ref[...], wa_ref[...]) + _dot_nt(dpf_ref[...], wf_ref[...]) + _dot_nt(dpb_ref[...], wb_ref[...])
        xf = x_ref[...]
        r = lax.rsqrt(jnp.mean(xf * xf, axis=-1, keepdims=True) + EPS)
        xhat = xf * r
        dg_ref[...] += jnp.sum(dh * xhat, axis=0, keepdims=True)
        gdh = dh * g_ref[...]
        dx_ref[...] = dxm_ref[...] + r * (gdh - xhat * jnp.mean(xhat * gdh, axis=-1, keepdims=True))

    return pl.pallas_call(
        body,
        grid=(S // TM,),
        in_specs=[_rows(TM, D_MODEL), _whole(g), _rows(TM, D_MODEL), _rows(TM, D_QKV), _rows(TM, D_F), _rows(TM, D_B),
                  _whole(wa), _whole(wf), _whole(wb)],
        out_specs=[_rows(TM, D_MODEL), pl.BlockSpec((1, D_MODEL), lambda i: (0, 0))],
        out_shape=[jax.ShapeDtypeStruct((S, D_MODEL), F32), jax.ShapeDtypeStruct((1, D_MODEL), F32)],
        compiler_params=_params("arbitrary"),
        name="in_proj_bwd",
    )(x, g, dxm, dpa, dpf, dpb, wa, wf, wb)


def _wgrad_tokens(S, K):
    return min(S, 2048 if K <= 1024 else 1024)


def _wgrad_columns(n):
    return next(tn for tn in (1024, 768, 512, 256, 128) if n % tn == 0)


def wgrad(xa, dy, name):
    S, K = xa.shape
    N = dy.shape[1]
    ts = _wgrad_tokens(S, K)
    tn = _wgrad_columns(N)
    tk = _wgrad_columns(K)

    def body(x_ref, dy_ref, o_ref):
        @pl.when(pl.program_id(2) == 0)
        def _():
            o_ref[...] = jnp.zeros_like(o_ref)

        o_ref[...] += _dot_tn(x_ref[...], dy_ref[...])

    return pl.pallas_call(
        body,
        grid=(K // tk, N // tn, S // ts),
        in_specs=[pl.BlockSpec((ts, tk), lambda i, j, k: (k, i)), pl.BlockSpec((ts, tn), lambda i, j, k: (k, j))],
        out_specs=pl.BlockSpec((tk, tn), lambda i, j, k: (i, j)),
        out_shape=jax.ShapeDtypeStruct((K, N), F32),
        compiler_params=_params("parallel", "parallel", "arbitrary"),
        name=name,
    )(xa, dy)


def wgrad_into(xa, dy, name, buf, l):
    S, K = xa.shape
    N = dy.shape[1]
    ts = _wgrad_tokens(S, K)
    split = buf.ndim == 4
    tn = buf.shape[-1] if split else _wgrad_columns(N)
    place = _into_layer(buf)

    def body(l_ref, buf_ref, x_ref, dy_ref, o_ref):
        @pl.when(pl.program_id(1) == 0)
        def _():
            o_ref[...] = jnp.zeros_like(o_ref)

        o_ref[...] += _dot_tn(x_ref[...], dy_ref[...])

    if split:
        out_spec = pl.BlockSpec((None, None, K, tn), lambda j, k, l: (l[0], j, 0, 0))
    else:
        out_spec = pl.BlockSpec((None, K, tn), lambda j, k, l: (l[0], 0, j))
    return pl.pallas_call(
        body,
        grid_spec=pltpu.PrefetchScalarGridSpec(
            num_scalar_prefetch=1,
            grid=(N // tn, S // ts),
            in_specs=[place["in_spec"], pl.BlockSpec((ts, K), lambda j, k, l: (k, 0)),
                      pl.BlockSpec((ts, tn), lambda j, k, l: (k, j))],
            out_specs=out_spec,
        ),
        out_shape=place["out_shape"],
        input_output_aliases=place["aliases"],
        compiler_params=_params("parallel", "arbitrary"),
        name=name,
    )(_layer_index(l), buf, xa, dy)


def _head_mean_matrix():
    h = np.arange(D_ATTN) // HEAD_DIM
    return jnp.asarray((h[:, None] == h[None, :]).astype(np.float32) / HEAD_DIM, BF16)


def _place_matrix(lane0):
    m = np.zeros((N_PIECES * LANES, D_HEADS), np.float32)
    for i in range(N_PIECES):
        for h in range(N_HEADS):
            m[i * LANES + h, h * LANES + lane0 + i] = 1.0
    return jnp.asarray(m, BF16)


def _tri(n, upper):
    r = np.arange(n)
    m = (r[None, :] >= r[:, None]) if upper else (r[None, :] <= r[:, None])
    return jnp.asarray(m.astype(np.float32), BF16)


def _lanes_in(lane, lo, n):
    return (lane >= lo) & (lane < lo + n)


def qk_prep(pa, pf, gq, gk, fb):
    S = pa.shape[0]
    bd = _head_mean_matrix()
    tri = _tri(TM, upper=False)
    place_q = _place_matrix(LANE_C)
    place_k = _place_matrix(LANE_ONE)

    def body(q_ref, k_ref, v_ref, pf_ref, gq_ref, gk_ref, fb_ref, bd_ref, tri_ref, pq_ref, pk_ref,
             qx_ref, kx_ref, vx_ref, carry):
        @pl.when(pl.program_id(0) == 0)
        def _():
            carry[...] = jnp.zeros_like(carry)

        def head_norm(x_ref, g_ref, scale):
            xf = x_ref[...].astype(F32)
            ms = _dot((xf * xf).astype(BF16), bd_ref[...])
            return xf * lax.rsqrt(ms + EPS) * g_ref[...] * scale

        qh = head_norm(q_ref, gq_ref, HEAD_DIM ** -0.5 * LOG2E)
        kh = head_norm(k_ref, gk_ref, 1.0)
        vf = v_ref[...].astype(F32)

        z = pf_ref[...] + fb_ref[...]
        logf = jnp.minimum(z, 0.0) - jnp.log(1.0 + jnp.exp(-jnp.abs(z)))
        hi, lo = _split_bf16(logf)
        c = _dot(tri_ref[...], hi) + _dot(tri_ref[...], lo) + carry[...]
        carry[...] += jnp.sum(hi.astype(F32) + lo.astype(F32), axis=0, keepdims=True)
        pieces = jnp.concatenate(_pieces(c * LOG2E), axis=1)
        cq = _dot(pieces, pq_ref[...])
        ck = _dot(pieces, pk_ref[...])

        lane = lax.broadcasted_iota(jnp.int32, (TM, LANES), 1)
        low = lane < HEAD_DIM
        ones_q = _lanes_in(lane, LANE_ONE, N_PIECES).astype(F32)
        ones_k = (_lanes_in(lane, LANE_C, N_PIECES) | _lanes_in(lane, LANE_LSE, N_PIECES)).astype(F32)
        ones_v = _lanes_in(lane, LANE_C, N_PIECES + 1).astype(F32)
        for h in range(N_HEADS):
            blk = slice((h // 2) * LANES, (h // 2 + 1) * LANES)
            head = (lambda a: pltpu.roll(a[:, blk], HEAD_DIM, 1)) if h % 2 else (lambda a: a[:, blk])
            mine = slice(h * LANES, (h + 1) * LANES)
            qx_ref[h] = jnp.where(low, head(qh), cq[:, mine] + ones_q).astype(BF16)
            kx_ref[h] = jnp.where(low, head(kh), ones_k - ck[:, mine]).astype(BF16)
            vx_ref[h] = jnp.where(low, head(vf), ones_v).astype(BF16)

    heads = pl.BlockSpec((N_HEADS, TM, LANES), lambda i: (0, i, 0))
    out = jax.ShapeDtypeStruct((N_HEADS, S, LANES), BF16)
    return pl.pallas_call(
        body,
        grid=(S // TM,),
        in_specs=[pl.BlockSpec((TM, D_ATTN), lambda i: (i, 0)), pl.BlockSpec((TM, D_ATTN), lambda i: (i, 1)),
                  pl.BlockSpec((TM, D_ATTN), lambda i: (i, 2)),
                  _rows(TM, D_F), _whole(gq), _whole(gk), _whole(fb), _whole(bd), _whole(tri), _whole(place_q), _whole(place_k)],
        out_specs=[heads, heads, heads],
        out_shape=[out, out, out],
        scratch_shapes=[pltpu.VMEM((1, D_F), F32)],
        compiler_params=_params("arbitrary"),
        name="qk_prep",
    )(pa, pa, pa, pf, gq, gk, fb, bd, tri, place_q, place_k)


def attn_fwd(qx, kx, vx):
    S = qx.shape[1]
    nq = S // TQ

    def body(q_ref, k_ref, v_ref, o_ref, qb_ref):
        i = pl.program_id(1)
        lane = lax.broadcasted_iota(jnp.int32, (TQ, LANES), 1)
        q = [q_ref[0], q_ref[1]]

        def update(qh, m, acc, k, v, visible):
            z = _dot_nt(qh, k)
            if visible is not None:
                z = jnp.where(visible, z, NEG)
            m_new = jnp.maximum(m, jnp.max(z, axis=1, keepdims=True))
            pr = jnp.exp2(z - m_new)
            return m_new, jnp.exp2(m - m_new) * acc + _dot(pr.astype(BF16), v)

        def step(kt, carry, diagonal=False):
            ks = pl.multiple_of(kt * TQ, TQ)
            visible = None
            if diagonal:
                visible = lax.broadcasted_iota(jnp.int32, (TQ, TQ), 0) >= lax.broadcasted_iota(jnp.int32, (TQ, TQ), 1)
            return tuple(update(q[j], *carry[j], k_ref[j, pl.ds(ks, TQ), :], v_ref[j, pl.ds(ks, TQ), :], visible)
                         for j in range(2))

        init = tuple((jnp.full((TQ, 1), NEG, F32), jnp.zeros((TQ, LANES), F32)) for _ in range(2))
        carry = step(i, lax.fori_loop(0, i, step, init), diagonal=True)
        for j in range(2):
            m, acc = carry[j]
            l = jnp.sum(jnp.where(lane == LANE_C, acc, 0.0), axis=1, keepdims=True)
            o_ref[j] = acc / l
            n1, n2, n3 = _pieces(-(m + jnp.log(l) * LOG2E))
            qb_ref[j] = jnp.where(lane == LANE_LSE, n1,
                                  jnp.where(lane == LANE_LSE + 1, n2, jnp.where(lane == LANE_LSE + 2, n3, q[j])))

    pair_tile = pl.BlockSpec((2, TQ, LANES), lambda p, i: (p, i, 0))
    pair_all = pl.BlockSpec((2, S, LANES), lambda p, i: (p, 0, 0))
    return pl.pallas_call(
        body,
        grid=(N_HEADS // 2, nq),
        in_specs=[pair_tile, pair_all, pair_all],
        out_specs=[pair_tile, pair_tile],
        out_shape=[jax.ShapeDtypeStruct((N_HEADS, S, LANES), F32), jax.ShapeDtypeStruct((N_HEADS, S, LANES), BF16)],
        compiler_params=_params("parallel", "parallel"),
        name="attn_fwd",
    )(qx, kx, vx)


def attn_bwd(qxb, kx, vx, dox):
    S = qxb.shape[1]
    nq = S // TQ

    def body(q_ref, k_ref, v_ref, do_ref, dq_ref, dk_ref, dv_ref):
        kt = pl.program_id(1)

        @pl.when(kt == 0)
        def _():
            dq_ref[...] = jnp.zeros_like(dq_ref)

        k = [k_ref[0], k_ref[1]]
        v = [v_ref[0], v_ref[1]]

        def block(j, q0, nq_rows, k0, nk, diagonal):
            q = q_ref[j, pl.ds(q0, nq_rows), :]
            dout = do_ref[j, pl.ds(q0, nq_rows), :]
            kj, vj = k[j][k0:k0 + nk], v[j][k0:k0 + nk]
            z = _dot_nt(q, kj)
            if diagonal:
                visible = (lax.broadcasted_iota(jnp.int32, (nq_rows, nk), 0)
                           >= lax.broadcasted_iota(jnp.int32, (nq_rows, nk), 1))
                z = jnp.where(visible, z, NEG)
            pr = jnp.exp2(z)
            dv = _dot_tn(pr.astype(BF16), dout)
            dsb = (pr * _dot_nt(dout, vj)).astype(BF16)
            dq_ref[j, pl.ds(q0, nq_rows), :] += _dot(dsb, kj)
            return _dot_tn(dsb, q), dv

        def step(qi, carry):
            qs = pl.multiple_of(qi * TQ, TQ)
            new = []
            for j in range(2):
                dk, dv = block(j, qs, TQ, 0, TQ, False)
                new.append((carry[j][0] + dk, carry[j][1] + dv))
            return tuple(new)

        half = TQ // 2
        qs = pl.multiple_of(kt * TQ, TQ)
        carry = []
        for j in range(2):
            first = block(j, qs, TQ, 0, half, True)
            second = block(j, pl.multiple_of(qs + half, half), half, half, half, True)
            carry.append(tuple(jnp.concatenate(a, axis=0) for a in zip(first, second)))
        carry = lax.fori_loop(kt + 1, nq, step, tuple(carry))
        for j in range(2):
            dk_ref[j] = carry[j][0]
            dv_ref[j] = carry[j][1]

    pair_tile = pl.BlockSpec((2, TQ, LANES), lambda p, kt: (p, kt, 0))
    pair_all = pl.BlockSpec((2, S, LANES), lambda p, kt: (p, 0, 0))
    out = jax.ShapeDtypeStruct((N_HEADS, S, LANES), F32)
    return pl.pallas_call(
        body,
        grid=(N_HEADS // 2, nq),
        in_specs=[pair_all, pair_tile, pair_tile, pair_all],
        out_specs=[pair_all, pair_tile, pair_tile],
        out_shape=[out, out, out],
        compiler_params=_params("arbitrary", "arbitrary"),
        name="attn_bwd",
    )(qxb, kx, vx, dox)


def attn_bwd_post(pa, pf, dqx, dkx, dvx, gq, gk, fb):
    S = pa.shape[0]
    nt = S // TM
    bd = _head_mean_matrix()
    triu = _tri(TM, upper=True)
    rev = lambda i: nt - 1 - i

    def body(q_ref, k_ref, pf_ref, dqx_ref, dkx_ref, dvx_ref, gq_ref, gk_ref, fb_ref, bd_ref, triu_ref,
             dpa_ref, dpf_ref, dgq_ref, dgk_ref, dfb_ref, carry):
        @pl.when(pl.program_id(0) == 0)
        def _():
            carry[...] = jnp.zeros_like(carry)
            dgq_ref[...] = jnp.zeros_like(dgq_ref)
            dgk_ref[...] = jnp.zeros_like(dgk_ref)
            dfb_ref[...] = jnp.zeros_like(dfb_ref)

        lane = lax.broadcasted_iota(jnp.int32, (TM, LANES), 1)
        columns = _heads_as_columns

        def head_norm_bwd(x_ref, dy, g_ref, dg_ref):
            xf = x_ref[...].astype(F32)
            r = lax.rsqrt(_dot((xf * xf).astype(BF16), bd_ref[...]) + EPS)
            xhat = xf * r
            dg_ref[...] += jnp.sum(dy * xhat, axis=0, keepdims=True)
            gdy = dy * g_ref[...]
            return (r * (gdy - xhat * _dot((xhat * gdy).astype(BF16), bd_ref[...]))).astype(BF16)

        dpa_ref[:, 0:D_ATTN] = head_norm_bwd(q_ref, columns(dqx_ref) * HEAD_DIM ** -0.5, gq_ref, dgq_ref)
        dpa_ref[:, D_ATTN:2 * D_ATTN] = head_norm_bwd(k_ref, columns(dkx_ref) * LN2, gk_ref, dgk_ref)
        dpa_ref[:, 2 * D_ATTN:3 * D_ATTN] = columns(dvx_ref).astype(BF16)

        dc = jnp.zeros((TM, LANES), F32)
        for h in range(N_HEADS):
            both = jnp.where(lane == LANE_C, dqx_ref[h], 0.0) - jnp.where(lane == LANE_ONE, dkx_ref[h], 0.0)
            dc = jnp.where(lane == h, jnp.sum(both, axis=1, keepdims=True), dc)
        hi, lo = _split_bf16(dc)
        dlogf = _dot(triu_ref[...], hi) + _dot(triu_ref[...], lo) + carry[...]
        first = lax.broadcasted_iota(jnp.int32, (TM, D_F), 0) == 0
        carry[...] = jnp.sum(jnp.where(first, dlogf, 0.0), axis=0, keepdims=True)
        df = dlogf * _sigmoid(-(pf_ref[...] + fb_ref[...]))
        dfb_ref[...] += jnp.sum(df, axis=0, keepdims=True)
        dpf_ref[...] = df.astype(BF16)

    heads = pl.BlockSpec((N_HEADS, TM, LANES), lambda i: (0, rev(i), 0))
    return pl.pallas_call(
        body,
        grid=(nt,),
        in_specs=[pl.BlockSpec((TM, D_ATTN), lambda i: (rev(i), 0)), pl.BlockSpec((TM, D_ATTN), lambda i: (rev(i), 1)),
                  pl.BlockSpec((TM, D_F), lambda i: (rev(i), 0)), heads, heads, heads,
                  _whole(gq), _whole(gk), _whole(fb), _whole(bd), _whole(triu)],
        out_specs=[pl.BlockSpec((TM, D_QKV), lambda i: (rev(i), 0)), pl.BlockSpec((TM, D_F), lambda i: (rev(i), 0)),
                   pl.BlockSpec((1, D_ATTN), lambda i: (0, 0)), pl.BlockSpec((1, D_ATTN), lambda i: (0, 0)),
                   pl.BlockSpec((1, D_F), lambda i: (0, 0))],
        out_shape=[jax.ShapeDtypeStruct((S, D_QKV), BF16), jax.ShapeDtypeStruct((S, D_F), BF16),
                   jax.ShapeDtypeStruct((1, D_ATTN), F32), jax.ShapeDtypeStruct((1, D_ATTN), F32),
                   jax.ShapeDtypeStruct((1, D_F), F32)],
        scratch_shapes=[pltpu.VMEM((1, D_F), F32)],
        compiler_params=_params("arbitrary"),
        name="attn_bwd_post",
    )(pa, pa, pf, dqx, dkx, dvx, gq, gk, fb, bd, triu)


def _shift_down(ext, k):
    return pltpu.roll(ext, k, 0)[HALO:]


def _shift_up(ext, k, n):
    return pltpu.roll(ext, n + HALO - k, 0)[:n]


def _pool_lane_select(a2, a4, a8, a16, lane):
    return jnp.where(lane < 64, a2, jnp.where(lane < 128, a4, jnp.where(lane < 192, a8, a16)))


def _local_branches(o, pb, halo, have_prev, row0, wao, convw, wco, wpool, pscale):
    n = pb.shape[0]
    cx = pb[:, 0:D_CONV].astype(F32)
    cb = pb[:, D_CONV:2 * D_CONV].astype(F32)
    cc = pb[:, 2 * D_CONV:3 * D_CONV].astype(F32)
    px = pb[:, 3 * D_CONV:D_LOCAL].astype(F32)
    keep = have_prev.astype(F32)
    z = cc * cx
    z_ext = jnp.concatenate([halo[:, 2 * D_CONV:3 * D_CONV].astype(F32) * halo[:, 0:D_CONV].astype(F32) * keep, z], axis=0)
    z1 = _shift_down(z_ext, 1)
    z2 = _shift_down(z_ext, 2)
    conv = convw[0:1, :] * z2 + convw[1:2, :] * z1 + convw[2:3, :] * z
    cm = cb * conv

    u_ext = jnp.concatenate([halo[:, 3 * D_CONV:D_LOCAL].astype(F32) * keep, px], axis=0)
    s2 = u_ext + pltpu.roll(u_ext, 1, 0)
    s4 = s2 + pltpu.roll(s2, 2, 0)
    s8 = s4 + pltpu.roll(s4, 4, 0)
    s16 = s8 + pltpu.roll(s8, 8, 0)
    lane = lax.broadcasted_iota(jnp.int32, (n, D_POOL), 1)
    win = _pool_lane_select(2.0, 4.0, 8.0, 16.0, lane)
    t = (row0 + lax.broadcasted_iota(jnp.int32, (n, D_POOL), 0)).astype(F32)
    cnt = jnp.minimum(t + 1.0, win)
    feat = _pool_lane_select(s2[HALO:], s4[HALO:], s8[HALO:], s16[HALO:], lane) / cnt - px

    ya = _dot(o, wao)
    yc = _dot(cm.astype(BF16), wco)
    yp_pre = _dot(feat.astype(BF16), wpool)
    yp = yp_pre * pscale
    return dict(cx=cx, cb=cb, cc=cc, z=z, z1=z1, z2=z2, conv=conv, cm=cm, feat=feat, cnt=cnt, lane=lane,
                ya=ya, yc=yc, yp_pre=yp_pre, yp=yp)


def _halo_spec(tm, tile_of):
    per = tm // HALO
    return pl.BlockSpec((HALO, D_LOCAL), lambda i, *_: (jnp.maximum(tile_of(i) * per - 1, 0), 0))


def _heads_as_columns(ref):
    lane = lax.broadcasted_iota(jnp.int32, ref.shape[1:], 1)
    return jnp.concatenate([jnp.where(lane < HEAD_DIM, ref[2 * p], pltpu.roll(ref[2 * p + 1], HEAD_DIM, 1))
                            for p in range(N_HEADS // 2)], axis=1)


def mix_out_fwd(x, ox, pb, wao, convw, wco, wpool, pscale, wo_all, l):
    S = x.shape[0]
    tm = TM_MIX

    def body(l_ref, x_ref, o_ref, pb_ref, halo_ref, wao_ref, cw_ref, wco_ref, wp_ref, ps_ref, wo_ref, y_ref):
        i = pl.program_id(0)
        pb = pb_ref[...]
        o = _heads_as_columns(o_ref).astype(BF16)
        b = _local_branches(o, pb, halo_ref[...], i > 0, i * tm, wao_ref[...], cw_ref[...], wco_ref[...],
                            wp_ref[...], ps_ref[...])
        g0 = _sigmoid(pb[:, D_LOCAL:D_LOCAL + D_MODEL].astype(F32))
        g1 = _sigmoid(pb[:, D_LOCAL + D_MODEL:D_LOCAL + 2 * D_MODEL].astype(F32))
        g2 = _sigmoid(pb[:, D_LOCAL + 2 * D_MODEL:D_B].astype(F32))
        merged = g0 * b["ya"] + g1 * b["yc"] + g2 * b["yp"]
        y_ref[...] = x_ref[...] + _dot(merged.astype(BF16), wo_ref[...])

    return pl.pallas_call(
        body,
        grid_spec=pltpu.PrefetchScalarGridSpec(
            num_scalar_prefetch=1,
            grid=(S // tm,),
            in_specs=[_rows(tm, D_MODEL), pl.BlockSpec((N_HEADS, tm, LANES), lambda i, l: (0, i, 0)), _rows(tm, D_B),
                      _halo_spec(tm, lambda i: i), _whole(wao), _whole(convw), _whole(wco), _whole(wpool), _whole(pscale),
                      _layer((D_MODEL, D_MODEL))],
            out_specs=_rows(tm, D_MODEL),
        ),
        out_shape=jax.ShapeDtypeStruct((S, D_MODEL), F32),
        compiler_params=_params("parallel"),
        name="mix_out_fwd",
    )(_layer_index(l), x, ox, pb, pb, wao, convw, wco, wpool, pscale, wo_all)


def mix_out_bwd(dxm, ox, pb, wao, convw, wco, wpool, pscale, wo_all, l):
    S = dxm.shape[0]
    tm = TM_MIX
    nt = S // tm
    rev = lambda i: nt - 1 - i
    rows = lambda n: pl.BlockSpec((tm, n), lambda i, l: (rev(i), 0))
    heads = pl.BlockSpec((N_HEADS, tm, LANES), lambda i, l: (0, rev(i), 0))
    acc = lambda r, c: pl.BlockSpec((r, c), lambda i, l: (0, 0))

    def body(l_ref, dxm_ref, o_ref, pb_ref, halo_ref, wao_ref, cw_ref, wco_ref, wp_ref, ps_ref, wo_ref,
             dpb_ref, do_ref, dwo_ref, dwao_ref, dwco_ref, dwp_ref, dcw_ref, dps_ref, next_dconv, next_e):
        i = pl.program_id(0)
        r = rev(i)

        @pl.when(i == 0)
        def _():
            for ref in (dwo_ref, dwao_ref, dwco_ref, dwp_ref, dcw_ref, dps_ref, next_dconv, next_e):
                ref[...] = jnp.zeros_like(ref)

        pb = pb_ref[...]
        o = _heads_as_columns(o_ref).astype(BF16)
        cw = cw_ref[...]
        b = _local_branches(o, pb, halo_ref[...], r > 0, r * tm, wao_ref[...], cw, wco_ref[...], wp_ref[...], ps_ref[...])
        g0 = _sigmoid(pb[:, D_LOCAL:D_LOCAL + D_MODEL].astype(F32))
        g1 = _sigmoid(pb[:, D_LOCAL + D_MODEL:D_LOCAL + 2 * D_MODEL].astype(F32))
        g2 = _sigmoid(pb[:, D_LOCAL + 2 * D_MODEL:D_B].astype(F32))
        dxb = dxm_ref[...].astype(BF16)
        merged = g0 * b["ya"] + g1 * b["yc"] + g2 * b["yp"]
        dwo_ref[...] += _dot_tn(merged.astype(BF16), dxb)
        dmer = _dot_nt(dxb, wo_ref[...])
        dpb_ref[:, D_LOCAL:D_LOCAL + D_MODEL] = (dmer * b["ya"] * (g0 * (1.0 - g0))).astype(BF16)
        dpb_ref[:, D_LOCAL + D_MODEL:D_LOCAL + 2 * D_MODEL] = (dmer * b["yc"] * (g1 * (1.0 - g1))).astype(BF16)
        dpb_ref[:, D_LOCAL + 2 * D_MODEL:D_B] = (dmer * b["yp"] * (g2 * (1.0 - g2))).astype(BF16)

        dya = (dmer * g0).astype(BF16)
        dwao_ref[...] += _dot_tn(o, dya)
        da = _dot_nt(dya, wao_ref[...])
        lane = lax.broadcasted_iota(jnp.int32, (tm, LANES), 1)
        for h in range(N_HEADS):
            two = da[:, (h // 2) * LANES:(h // 2 + 1) * LANES]
            dah = jnp.where(lane < HEAD_DIM, pltpu.roll(two, HEAD_DIM, 1) if h % 2 else two, 0.0).astype(BF16)
            d1, d2, d3 = _pieces(-jnp.sum(dah.astype(F32) * o_ref[h], axis=1, keepdims=True))
            do_ref[h] = jnp.where(lane == LANE_C + 1, d1, jnp.where(lane == LANE_C + 2, d2,
                                                                  jnp.where(lane == LANE_C + 3, d3, dah)))

        dyc = (dmer * g1).astype(BF16)
        dwco_ref[...] += _dot_tn(b["cm"].astype(BF16), dyc)
        dcm = _dot_nt(dyc, wco_ref[...])
        dconv = dcm * b["cb"]
        dcw_ref[0:1, :] += jnp.sum(dconv * b["z2"], axis=0, keepdims=True)
        dcw_ref[1:2, :] += jnp.sum(dconv * b["z1"], axis=0, keepdims=True)
        dcw_ref[2:3, :] += jnp.sum(dconv * b["z"], axis=0, keepdims=True)
        d_ext = jnp.concatenate([dconv, next_dconv[...]], axis=0)
        dz = cw[2:3, :] * dconv + cw[1:2, :] * _shift_up(d_ext, 1, tm) + cw[0:1, :] * _shift_up(d_ext, 2, tm)
        next_dconv[...] = dconv[0:HALO]
        dpb_ref[:, 0:D_CONV] = (dz * b["cc"]).astype(BF16)
        dpb_ref[:, D_CONV:2 * D_CONV] = (dcm * b["conv"]).astype(BF16)
        dpb_ref[:, 2 * D_CONV:3 * D_CONV] = (dz * b["cx"]).astype(BF16)

        dyp = dmer * g2
        dps_ref[...] += jnp.sum(dyp * b["yp_pre"], axis=0, keepdims=True)
        dyps = (dyp * ps_ref[...]).astype(BF16)
        dwp_ref[...] += _dot_tn(b["feat"].astype(BF16), dyps)
        dfeat = _dot_nt(dyps, wp_ref[...])
        e = dfeat / b["cnt"]
        e_ext = jnp.concatenate([e, next_e[...]], axis=0)
        up = lambda a, k: pltpu.roll(a, tm + HALO - k, 0)
        f2 = e_ext + up(e_ext, 1)
        f4 = f2 + up(f2, 2)
        f8 = f4 + up(f4, 4)
        f16 = f8 + up(f8, 8)
        next_e[...] = e[0:HALO]
        dpb_ref[:, 3 * D_CONV:D_LOCAL] = (_pool_lane_select(f2[:tm], f4[:tm], f8[:tm], f16[:tm], b["lane"]) - dfeat).astype(BF16)

    return pl.pallas_call(
        body,
        grid_spec=pltpu.PrefetchScalarGridSpec(
            num_scalar_prefetch=1,
            grid=(nt,),
            in_specs=[rows(D_MODEL), heads, rows(D_B), _halo_spec(tm, rev),
                      _whole(wao), _whole(convw), _whole(wco), _whole(wpool), _whole(pscale), _layer((D_MODEL, D_MODEL))],
            out_specs=[rows(D_B), heads, acc(D_MODEL, D_MODEL), acc(D_ATTN, D_MODEL), acc(D_CONV, D_MODEL),
                       acc(D_POOL, D_MODEL), acc(8, D_CONV), acc(1, D_MODEL)],
            scratch_shapes=[pltpu.VMEM((HALO, D_CONV), F32), pltpu.VMEM((HALO, D_POOL), F32)],
        ),
        out_shape=[jax.ShapeDtypeStruct((S, D_B), BF16), jax.ShapeDtypeStruct((N_HEADS, S, LANES), BF16),
                   jax.ShapeDtypeStruct((D_MODEL, D_MODEL), F32), jax.ShapeDtypeStruct((D_ATTN, D_MODEL), F32),
                   jax.ShapeDtypeStruct((D_CONV, D_MODEL), F32), jax.ShapeDtypeStruct((D_POOL, D_MODEL), F32),
                   jax.ShapeDtypeStruct((8, D_CONV), F32), jax.ShapeDtypeStruct((1, D_MODEL), F32)],
        compiler_params=_params("arbitrary"),
        name="mix_out_bwd",
    )(_layer_index(l), dxm, ox, pb, pb, wao, convw, wco, wpool, pscale, wo_all)


FF_SHARD = 2 * D_FF // 4


def ffn_fwd(x, g, w1_all, w2_all, l):
    S = x.shape[0]

    def body(l_ref, x_ref, g_ref, w1_ref, w2_ref, y_ref, u_ref):
        xf = x_ref[...]
        r = lax.rsqrt(jnp.mean(xf * xf, axis=-1, keepdims=True) + EPS)
        h = (xf * r * g_ref[...]).astype(BF16)
        u = jnp.concatenate([_dot(h, w1_ref[j]) for j in range(4)], axis=1)
        u_ref[...] = u.astype(BF16)
        gt = u[:, 0:D_FF]
        act = gt * _sigmoid(gt) * u[:, D_FF:2 * D_FF]
        y_ref[...] = xf + _dot(act.astype(BF16), w2_ref[...])

    return pl.pallas_call(
        body,
        grid_spec=pltpu.PrefetchScalarGridSpec(
            num_scalar_prefetch=1,
            grid=(S // TM,),
            in_specs=[_rows(TM, D_MODEL), _whole(g), _layer((4, D_MODEL, FF_SHARD)), _layer((D_FF, D_MODEL))],
            out_specs=[_rows(TM, D_MODEL), _rows(TM, 2 * D_FF)],
        ),
        out_shape=[jax.ShapeDtypeStruct((S, D_MODEL), F32), jax.ShapeDtypeStruct((S, 2 * D_FF), BF16)],
        compiler_params=_params("parallel"),
        name="ffn_fwd",
    )(_layer_index(l), x, g, w1_all, w2_all)


def ffn_bwd(x, g, dy, u, w1_all, w2_all, l):
    S = x.shape[0]

    def body(l_ref, x_ref, g_ref, dy_ref, u_ref, w1_ref, w2_ref, dx_ref, du_ref, act_ref, h_ref, dyb_ref, dg_ref):
        @pl.when(pl.program_id(0) == 0)
        def _():
            dg_ref[...] = jnp.zeros_like(dg_ref)

        dyf = dy_ref[...]
        dyb_ref[...] = dyf.astype(BF16)
        dact = _dot_nt(dyb_ref[...], w2_ref[...])
        gt = u_ref[:, 0:D_FF].astype(F32)
        up = u_ref[:, D_FF:2 * D_FF].astype(F32)
        sg = _sigmoid(gt)
        silu = gt * sg
        act_ref[...] = (silu * up).astype(BF16)
        du_ref[:, 0:D_FF] = (dact * up * (sg * (1.0 + gt * (1.0 - sg)))).astype(BF16)
        du_ref[:, D_FF:2 * D_FF] = (dact * silu).astype(BF16)
        dh = _dot_nt(du_ref[:, 0:FF_SHARD], w1_ref[0])
        for j in range(1, 4):
            dh = dh + _dot_nt(du_ref[:, j * FF_SHARD:(j + 1) * FF_SHARD], w1_ref[j])
        xf = x_ref[...]
        r = lax.rsqrt(jnp.mean(xf * xf, axis=-1, keepdims=True) + EPS)
        xhat = xf * r
        h_ref[...] = (xhat * g_ref[...]).astype(BF16)
        dg_ref[...] += jnp.sum(dh * xhat, axis=0, keepdims=True)
        gdh = dh * g_ref[...]
        dx_ref[...] = dyf + r * (gdh - xhat * jnp.mean(xhat * gdh, axis=-1, keepdims=True))

    return pl.pallas_call(
        body,
        grid_spec=pltpu.PrefetchScalarGridSpec(
            num_scalar_prefetch=1,
            grid=(S // TM,),
            in_specs=[_rows(TM, D_MODEL), _whole(g), _rows(TM, D_MODEL), _rows(TM, 2 * D_FF),
                      _layer((4, D_MODEL, FF_SHARD)), _layer((D_FF, D_MODEL))],
            out_specs=[_rows(TM, D_MODEL), _rows(TM, 2 * D_FF), _rows(TM, D_FF), _rows(TM, D_MODEL), _rows(TM, D_MODEL),
                       pl.BlockSpec((1, D_MODEL), lambda i, l: (0, 0))],
        ),
        out_shape=[jax.ShapeDtypeStruct((S, D_MODEL), F32), jax.ShapeDtypeStruct((S, 2 * D_FF), BF16),
                   jax.ShapeDtypeStruct((S, D_FF), BF16), jax.ShapeDtypeStruct((S, D_MODEL), BF16),
                   jax.ShapeDtypeStruct((S, D_MODEL), BF16), jax.ShapeDtypeStruct((1, D_MODEL), F32)],
        compiler_params=_params("arbitrary"),
        name="ffn_bwd",
    )(_layer_index(l), x, g, dy, u, w1_all, w2_all)


def loss_head(y, target):
    S = y.shape[0]

    def body(y_ref, t_ref, loss_ref, dy_ref):
        @pl.when(pl.program_id(0) == 0)
        def _():
            loss_ref[0, 0] = 0.0

        err = y_ref[...] - t_ref[...]
        dy_ref[...] = err * (1.0 / D_MODEL)
        loss_ref[0, 0] += 0.5 * jnp.sum(jnp.mean(err * err, axis=-1))

    return pl.pallas_call(
        body,
        grid=(S // TM,),
        in_specs=[_rows(TM, D_MODEL), _rows(TM, D_MODEL)],
        out_specs=[pl.BlockSpec((1, 1), lambda i: (0, 0), memory_space=pltpu.SMEM), _rows(TM, D_MODEL)],
        out_shape=[jax.ShapeDtypeStruct((1, 1), F32), jax.ShapeDtypeStruct((S, D_MODEL), F32)],
        compiler_params=_params("arbitrary"),
        name="loss_head",
    )(y, target)


SPLIT = {
    "w_in": ((D_MODEL, D_IN), 1),
    "w_attn_out": ((D_ATTN, D_MODEL), 1),
    "w_conv_out": ((D_CONV, D_MODEL), 1),
    "pool_w": ((N_GROUPS, D_POOL // N_GROUPS, D_MODEL // N_GROUPS), 2),
    "w_o": ((D_MODEL, D_MODEL), 0),
    "w_ffn_in": ((D_MODEL, 2 * D_FF), 1),
    "w_ffn_out": ((D_FF, D_MODEL), 0),
}
SMALL = {"norm_mix_g": D_MODEL, "forget_b": N_HEADS, "q_norm_g": HEAD_DIM, "k_norm_g": HEAD_DIM,
         "pool_scale": D_MODEL, "norm_ffn_g": D_MODEL, "conv_w": CONV_K * D_CONV}
WEIGHTS = ["norm_mix_g", "w_in", "forget_b", "q_norm_g", "k_norm_g", "w_attn_out", "conv_w", "w_conv_out", "pool_w",
           "pool_scale", "w_o", "norm_ffn_g", "w_ffn_in", "w_ffn_out"]


MIX_W = ["w_in", "w_attn_out", "w_conv_out", "pool_w", "w_o"]
FFN_W = ["w_ffn_in", "w_ffn_out"]


def _row(a):
    return a.astype(F32).reshape(1, -1)


def mix_fwd(x, full, li, small):
    n = full["w_o"].shape[0]
    wa, wf, wb = w_in_prep(full["w_in"], li)
    wao, wco, wpool, convw = branch_w_prep(full["w_attn_out"], full["w_conv_out"], full["pool_w"], full["conv_w"], li)
    w = dict(wa=wa, wf=wf, wb=wb, wao=wao, wco=wco, wpool=wpool, convw=convw,
             w_o=full["w_o"].reshape(n, D_MODEL, D_MODEL), at=li,
             g1=_row(small["norm_mix_g"]), pscale=_row(small["pool_scale"]),
             gq=_row(jnp.tile(small["q_norm_g"], N_HEADS)), gk=_row(jnp.tile(small["k_norm_g"], N_HEADS)),
             fb=_row(jnp.pad(small["forget_b"], (0, D_F - N_HEADS))))
    h, pa, pf, pb = in_proj_fwd(x, w["g1"], wa, wf, wb)
    qx, kx, vx = qk_prep(pa, pf, w["gq"], w["gk"], w["fb"])
    ox, qxb = attn_fwd(qx, kx, vx)
    xm = mix_out_fwd(x, ox, pb, wao, convw, wco, wpool, w["pscale"], w["w_o"], li)
    return xm, w, dict(x=x, h=h, pa=pa, pf=pf, pb=pb, kx=kx, vx=vx, ox=ox, qxb=qxb)


def ffn_half_fwd(xm, full, li, g2):
    n = full["w_ffn_out"].shape[0]
    w = dict(w1=full["w_ffn_in"], w2=full["w_ffn_out"].reshape(n, D_FF, D_MODEL), at=li, g2=_row(g2))
    y, u = ffn_fwd(xm, w["g2"], w["w1"], w["w2"], li)
    return y, w, dict(xm=xm, u=u)


def ffn_half_bwd(dx, w, s, gi, big):
    n = big["w_ffn_out"].shape[0]
    big = dict(big)
    dxm, du, act, h2, dyb, dg2 = ffn_bwd(s["xm"], w["g2"], dx, s["u"], w["w1"], w["w2"], w["at"])
    big["w_ffn_out"] = wgrad_into(act, dyb, "wgrad_ffn_out", big["w_ffn_out"].reshape(n, D_FF, D_MODEL),
                                  gi).reshape(big["w_ffn_out"].shape)
    big["w_ffn_in"] = wgrad_into(h2, du, "wgrad_ffn_in", big["w_ffn_in"], gi)
    return dxm, big, dict(norm_ffn_g=dg2[0])


def mix_bwd_weights(dxm, w, s, gi, big):
    big = dict(big)
    dpb, dox, dwo, dwao, dwco, dwpool, dconvw, dpscale = mix_out_bwd(
        dxm, s["ox"], s["pb"], w["wao"], w["convw"], w["wco"], w["wpool"], w["pscale"], w["w_o"], w["at"])
    dqx, dkx, dvx = attn_bwd(s["qxb"], s["kx"], s["vx"], dox)
    dpa, dpf, dgq, dgk, dfb = attn_bwd_post(s["pa"], s["pf"], dqx, dkx, dvx, w["gq"], w["gk"], w["fb"])
    big["w_in"] = w_in_unprep(wgrad(s["h"], dpa, "wgrad_in_qkv"), wgrad(s["h"], dpf, "wgrad_in_f"),
                              wgrad(s["h"], dpb, "wgrad_in_b"), big["w_in"], gi)
    big["w_attn_out"], big["w_conv_out"], big["pool_w"], big["w_o"] = branch_g_place(
        dwao, dwco, dwpool, dwo, (big["w_attn_out"], big["w_conv_out"], big["pool_w"], big["w_o"]), gi)
    sm = dict(forget_b=dfb[0, 0:N_HEADS], q_norm_g=dgq.reshape(N_HEADS, HEAD_DIM).sum(0),
              k_norm_g=dgk.reshape(N_HEADS, HEAD_DIM).sum(0), pool_scale=dpscale[0],
              conv_w=dconvw[0:CONV_K].reshape(-1))
    return (dpa, dpf, dpb), big, sm


def mix_bwd_input(dproj, dxm, w, s):
    dx, dg1 = in_proj_bwd(s["x"], w["g1"], dxm, *dproj, w["wa"], w["wf"], w["wb"])
    return dx, dict(norm_mix_g=dg1[0])


def mix_bwd(dxm, w, s, gi, big):
    dproj, big, sm = mix_bwd_weights(dxm, w, s, gi, big)
    dx, g1 = mix_bwd_input(dproj, dxm, w, s)
    return dx, big, {**sm, **g1}


def grad_buffers(full):
    return {n: lax.empty(full[n].shape, F32) for n in SPLIT}


def local_step(x, target, gathered, small):
    n_layers = small["norm_mix_g"].shape[0]
    done = []
    for l in range(n_layers):
        xm, wm, sm = mix_fwd(x, gathered, l, {n: v[l] for n, v in small.items()})
        x, wf, sf = ffn_half_fwd(xm, gathered, l, small["norm_ffn_g"][l])
        done.append((wm, sm, wf, sf))
    loss, dx = loss_head(x, target)
    big = grad_buffers(gathered)
    small_grads = [None] * n_layers
    for l in reversed(range(n_layers)):
        wm, sm, wf, sf = done[l]
        dxm, ffn_big, g_ffn = ffn_half_bwd(dx, wf, sf, l, {n: big[n] for n in FFN_W})
        dx, mix_big, g_mix = mix_bwd(dxm, wm, sm, l, {n: big[n] for n in MIX_W})
        big = {**ffn_big, **mix_big}
        small_grads[l] = {**g_ffn, **g_mix}
    return loss, dx, big, {n: jnp.stack([g[n] for g in small_grads]) for n in SMALL}


def adamw(w, g, m, v, name):
    R, C = w.shape
    tm = 256 if R % 256 == 0 else R

    def body(w_ref, g_ref, m_ref, v_ref, d_ref, nm_ref, nv_ref):
        gr = g_ref[...]
        m_new = ADAM_B1 * m_ref[...] + (1.0 - ADAM_B1) * gr
        v_new = ADAM_B2 * v_ref[...] + (1.0 - ADAM_B2) * jnp.square(gr)
        nm_ref[...] = m_new
        nv_ref[...] = v_new
        m_hat = m_new / (1.0 - ADAM_B1 ** ADAM_STEP)
        v_hat = v_new / (1.0 - ADAM_B2 ** ADAM_STEP)
        d_ref[...] = -ADAM_LR * (m_hat / (jnp.sqrt(v_hat) + ADAM_EPS) + ADAM_WD * w_ref[...])

    spec = _rows(tm, C)
    out = jax.ShapeDtypeStruct((R, C), F32)
    return pl.pallas_call(
        body,
        grid=(R // tm,),
        in_specs=[spec] * 4,
        out_specs=[spec] * 3,
        out_shape=[out] * 3,
        compiler_params=_params("parallel"),
        name=name,
    )(w, g, m, v)


MESH = pl.DeviceIdType.MESH
HBM_REF = pl.BlockSpec(memory_space=pl.ANY)
N_CHIPS = 4
N_DEV = 8
SMALL_SHAPE = (128, LANES)


def _mesh_pos():
    return lax.axis_index("x"), lax.axis_index("y"), lax.axis_index("c")


def _other_chips(x, y):
    return [(1 - x, y), (x, 1 - y), (1 - x, 1 - y)]


def _remote(src, dst, send_sem, recv_sem, to):
    return pltpu.make_async_remote_copy(src_ref=src, dst_ref=dst, send_sem=send_sem, recv_sem=recv_sem,
                                        device_id=to, device_id_type=MESH)


def _row_tile(rows):
    for tm in (256, 176, 128):
        if rows % tm == 0:
            return tm
    return rows


def _as4(a):
    return a.reshape(a.shape[0], a.shape[1], -1, a.shape[-1])


def _core_rows(buf, c):
    R, C = buf.shape[-2:]
    if R % 2:
        whole = (pl.ds(0, R), pl.ds(0, C))
        return whole, whole, False
    return (pl.ds(c * (R // 2), R // 2), pl.ds(0, C)), (pl.ds((1 - c) * (R // 2), R // 2), pl.ds(0, C)), True


def _half_shape(g):
    return g.shape[:-2] + (g.shape[-2] // 2, g.shape[-1])


def place_shard(chip, w, lo, hi, dtype, name, after=None):
    w3 = w.reshape(w.shape[0], -1, w.shape[-1])
    _, R, C = w3.shape
    tm = _row_tile(R)
    idle = [] if after is None else [after]

    def body(chip_ref, w_ref, *rest):
        rest[-1][...] = w_ref[...].astype(dtype)

    return pl.pallas_call(
        body,
        grid_spec=pltpu.PrefetchScalarGridSpec(
            num_scalar_prefetch=1,
            grid=(hi - lo, R // tm),
            in_specs=[pl.BlockSpec((None, tm, C), lambda l, i, chip: (lo + l, i, 0))] + [_whole(a) for a in idle],
            out_specs=pl.BlockSpec((None, None, tm, C), lambda l, i, chip: (l, chip[0], i, 0)),
        ),
        out_shape=jax.ShapeDtypeStruct((hi - lo, N_CHIPS, R, C), dtype),
        compiler_params=_params("parallel", "parallel"),
        name=name,
    )(chip, w3, *idle)


HBM_SPACE = pl.BlockSpec(memory_space=pltpu.HBM)
SEM_SPACE = pl.BlockSpec(memory_space=pltpu.SEMAPHORE)
IN_FLIGHT = pltpu.SideEffectType.DATAFLOW_SIDE_EFFECTING


def _sem_table(send_sems, recv_sems):
    return lambda t, j: (send_sems.at[t, j], recv_sems.at[t, j])


def _sem_per_peer(send_sems, recv_sems):
    return lambda t, j: (send_sems[j], recv_sems[j])


def _gather_ici(bufs, sem, sends=True, lands=True):
    x, y, c = _mesh_pos()
    me = 2 * x + y
    out, into = [], []
    for t, buf in enumerate(bufs):
        mine, _, _ = _core_rows(buf, c)
        part = lambda k, buf=buf, mine=mine: buf.at[pl.ds(0, buf.shape[0]), k, *mine]
        for j, (px, py) in enumerate(_other_chips(x, y)):
            if sends:
                out.append(_remote(part(me), part(me), *sem(t, j), (px, py, c)))
            if lands:
                into.append(_remote(part(2 * px + py), part(2 * px + py), *sem(t, j), (px, py, c)))
    return out, into


def _gather_d2d(bufs, send_sems, recv_sems, first):
    x, y, c = _mesh_pos()
    sends, lands = [], []
    for t, buf in enumerate(bufs):
        mine, theirs, split = _core_rows(buf, c)
        if not split:
            continue
        for j, (px, py) in enumerate(_other_chips(x, y)):
            part = lambda half, buf=buf, k=2 * px + py: buf.at[pl.ds(0, buf.shape[0]), k, *half]
            sems = (send_sems.at[t, first + j], recv_sems.at[t, first + j], (x, y, 1 - c))
            sends.append(_remote(part(mine), part(mine), *sems))
            lands.append(_remote(part(theirs), part(theirs), *sems))
    return sends, lands


def gather_shards(bufs):
    n = len(bufs)

    def body(*refs):
        outs = refs[n:2 * n]
        send_sems, recv_sems = refs[2 * n:]
        ici_out, ici_in = _gather_ici(outs, _sem_table(send_sems, recv_sems))
        d2d_out, d2d_in = _gather_d2d(outs, send_sems, recv_sems, 3)
        for cp in ici_out:
            cp.start()
        for cp in ici_in:
            cp.wait_recv()
        for cp in d2d_out:
            cp.start()
        for cp in d2d_in:
            cp.wait_recv()
        for cp in ici_out + d2d_out:
            cp.wait_send()

    return pl.pallas_call(
        body,
        in_specs=[HBM_REF] * n,
        out_specs=[HBM_REF] * n,
        out_shape=[jax.ShapeDtypeStruct(b.shape, b.dtype) for b in bufs],
        input_output_aliases={t: t for t in range(n)},
        scratch_shapes=[pltpu.SemaphoreType.DMA((n, 6)), pltpu.SemaphoreType.DMA((n, 6))],
        name="gather_shards",
    )(*bufs)


def gather_start(bufs):
    n = len(bufs)

    def body(*refs):
        send_sems, recv_sems = refs[n:n + 3], refs[n + 3:n + 6]
        outs = refs[n + 6:2 * n + 6]
        token = refs[2 * n + 6]
        for cp in _gather_ici(outs, _sem_per_peer(send_sems, recv_sems), lands=False)[0]:
            cp.start()
        token[...] = jnp.zeros_like(token)

    res = pl.pallas_call(
        body,
        in_specs=[HBM_SPACE] * n,
        out_specs=[SEM_SPACE] * 6 + [HBM_SPACE] * n + [pl.BlockSpec(memory_space=pltpu.VMEM)],
        out_shape=[pltpu.SemaphoreType.DMA(())] * 6
        + [pltpu.HBM(b.shape, b.dtype) for b in bufs] + [jax.ShapeDtypeStruct((8, LANES), F32)],
        input_output_aliases={t: t + 6 for t in range(n)},
        compiler_params=pltpu.CompilerParams(has_side_effects=IN_FLIGHT),
        name="gather_start",
    )(*[pltpu.with_memory_space_constraint(b, pltpu.HBM) for b in bufs])
    return list(res[0:3]), list(res[3:6]), list(res[6:n + 6]), res[n + 6]


def gather_wait(send_sems, recv_sems, bufs, after):
    n = len(bufs)

    def body(*refs):
        ins_send, ins_recv = refs[n:n + 3], refs[n + 3:n + 6]
        outs = refs[n + 7:]
        sends, lands = _gather_ici(outs, _sem_per_peer(ins_send, ins_recv))
        for cp in lands:
            cp.wait_recv()
        for cp in sends:
            cp.wait_send()

    return list(pl.pallas_call(
        body,
        in_specs=[HBM_SPACE] * n + [SEM_SPACE] * 6 + [HBM_REF],
        out_specs=[HBM_SPACE] * n,
        out_shape=[pltpu.HBM(b.shape, b.dtype) for b in bufs],
        input_output_aliases={t: t for t in range(n)},
        compiler_params=pltpu.CompilerParams(has_side_effects=IN_FLIGHT),
        name="gather_wait",
    )(*bufs, *send_sems, *recv_sems, after))


def gather_forward(bufs):
    n = len(bufs)

    def body(*refs):
        outs = refs[n:2 * n]
        send_sems, recv_sems = refs[2 * n:]
        sends, lands = _gather_d2d(outs, send_sems, recv_sems, 0)
        for cp in sends:
            cp.start()
        for cp in lands:
            cp.wait_recv()
        for cp in sends:
            cp.wait_send()

    return list(pl.pallas_call(
        body,
        in_specs=[HBM_REF] * n,
        out_specs=[HBM_REF] * n,
        out_shape=[jax.ShapeDtypeStruct(b.shape, b.dtype) for b in bufs],
        input_output_aliases={t: t for t in range(n)},
        scratch_shapes=[pltpu.SemaphoreType.DMA((n, 3)), pltpu.SemaphoreType.DMA((n, 3))],
        name="gather_forward",
    )(*bufs))


def pair_exchange(grads):
    n = len(grads)

    def body(*refs):
        ins, outs = refs[:n], refs[n:2 * n]
        send_sems, recv_sems = refs[2 * n:]
        cps = _pair_copies(ins, outs, lambda t: (send_sems.at[t], recv_sems.at[t]))
        for cp in cps:
            cp.start()
        for cp in cps:
            cp.wait()

    return list(pl.pallas_call(
        body,
        in_specs=[HBM_REF] * n,
        out_specs=[HBM_REF] * n,
        out_shape=[jax.ShapeDtypeStruct(_half_shape(g), g.dtype) for g in grads],
        scratch_shapes=[pltpu.SemaphoreType.DMA((n,)), pltpu.SemaphoreType.DMA((n,))],
        name="pair_exchange",
    )(*grads))


def _pair_copies(grads, lands, sem):
    x, y, c = _mesh_pos()
    cps = []
    for t in range(len(grads)):
        _, theirs, _ = _core_rows(grads[t], c)
        src = grads[t].at[pl.ds(0, grads[t].shape[0]), pl.ds(0, N_CHIPS), *theirs]
        cps.append(_remote(src, lands[t], *sem(t), (x, y, 1 - c)))
    return cps


def pair_exchange_start(grads):
    n = len(grads)
    lands = [lax.empty(_half_shape(g), g.dtype) for g in grads]

    def body(*refs):
        send_sem, recv_sem = refs[2 * n:2 * n + 2]
        outs = refs[2 * n + 2:4 * n + 2]
        token = refs[4 * n + 2]
        for cp in _pair_copies(outs[:n], outs[n:], lambda t: (send_sem, recv_sem)):
            cp.start()
        token[...] = jnp.zeros_like(token)

    res = pl.pallas_call(
        body,
        in_specs=[HBM_SPACE] * (2 * n),
        out_specs=[SEM_SPACE] * 2 + [HBM_SPACE] * (2 * n) + [pl.BlockSpec(memory_space=pltpu.VMEM)],
        out_shape=[pltpu.SemaphoreType.DMA(())] * 2 + [pltpu.HBM(a.shape, a.dtype) for a in list(grads) + lands]
        + [jax.ShapeDtypeStruct((8, LANES), F32)],
        input_output_aliases={t: t + 2 for t in range(2 * n)},
        compiler_params=pltpu.CompilerParams(has_side_effects=IN_FLIGHT),
        name="pair_exchange_start",
    )(*[pltpu.with_memory_space_constraint(a, pltpu.HBM) for a in list(grads) + lands])
    return res[0], res[1], list(res[2:n + 2]), list(res[n + 2:2 * n + 2]), res[2 * n + 2]


def pair_exchange_wait(send_sem, recv_sem, grads, lands, after):
    n = len(grads)

    def body(*refs):
        in_send, in_recv = refs[2 * n:2 * n + 2]
        outs = refs[2 * n + 3:]
        for cp in _pair_copies(outs[:n], outs[n:], lambda t: (in_send, in_recv)):
            cp.wait()

    res = pl.pallas_call(
        body,
        in_specs=[HBM_SPACE] * (2 * n) + [SEM_SPACE] * 2 + [HBM_REF],
        out_specs=[HBM_SPACE] * (2 * n),
        out_shape=[pltpu.HBM(a.shape, a.dtype) for a in list(grads) + list(lands)],
        input_output_aliases={t: t for t in range(2 * n)},
        compiler_params=pltpu.CompilerParams(has_side_effects=IN_FLIGHT),
        name="pair_exchange_wait",
    )(*grads, *lands, send_sem, recv_sem, after)
    return list(res[:n]), list(res[n:])


def pair_sum(core, g, t, name):
    n, _, hr, hc = t.shape
    R, C = g.shape[2:]
    tm = _row_tile(hr)
    per = hr // tm

    def body(c_ref, g_ref, t_ref, o_ref):
        o_ref[...] = (g_ref[...] + t_ref[...]).astype(BF16)

    tile = pl.BlockSpec((None, tm, hc), lambda a, i, c: (a, i, 0))
    out = pl.pallas_call(
        body,
        grid_spec=pltpu.PrefetchScalarGridSpec(
            num_scalar_prefetch=1,
            grid=(n * N_CHIPS, per),
            in_specs=[pl.BlockSpec((None, tm, hc), lambda a, i, c: (a, per * c[0] + i, 0)), tile],
            out_specs=tile,
        ),
        out_shape=jax.ShapeDtypeStruct((n * N_CHIPS, hr, hc), BF16),
        compiler_params=_params("parallel", "parallel"),
        name=name,
    )(core, g.reshape(n * N_CHIPS, R, C), t.reshape(n * N_CHIPS, hr, hc))
    return out.reshape(t.shape)


def _chip_copies(sums, recv, sem):
    x, y, c = _mesh_pos()
    cps = []
    for t in range(len(sums)):
        for j, (px, py) in enumerate(_other_chips(x, y)):
            src = sums[t].at[pl.ds(0, sums[t].shape[0]), 2 * px + py]
            cps.append(_remote(src, recv[t].at[j], *sem(t, j), (px, py, c)))
    return cps


def _recv_shape(s):
    return (N_CHIPS - 1, s.shape[0]) + s.shape[2:]


def chip_exchange(sums):
    n = len(sums)

    def body(*refs):
        ins, outs = refs[:n], refs[n:2 * n]
        cps = _chip_copies(ins, outs, _sem_table(*refs[2 * n:]))
        for cp in cps:
            cp.start()
        for cp in cps:
            cp.wait()

    return list(pl.pallas_call(
        body,
        in_specs=[HBM_REF] * n,
        out_specs=[HBM_REF] * n,
        out_shape=[jax.ShapeDtypeStruct(_recv_shape(s), s.dtype) for s in sums],
        scratch_shapes=[pltpu.SemaphoreType.DMA((n, 3)), pltpu.SemaphoreType.DMA((n, 3))],
        name="chip_exchange",
    )(*sums))


def chip_exchange_start(sums):
    n = len(sums)
    lands = [lax.empty(_recv_shape(s), s.dtype) for s in sums]

    def body(*refs):
        send_sems, recv_sems = refs[2 * n:2 * n + 3], refs[2 * n + 3:2 * n + 6]
        outs = refs[2 * n + 6:4 * n + 6]
        token = refs[4 * n + 6]
        for cp in _chip_copies(outs[:n], outs[n:2 * n], _sem_per_peer(send_sems, recv_sems)):
            cp.start()
        token[...] = jnp.zeros_like(token)

    res = pl.pallas_call(
        body,
        in_specs=[HBM_SPACE] * (2 * n),
        out_specs=[SEM_SPACE] * 6 + [HBM_SPACE] * (2 * n) + [pl.BlockSpec(memory_space=pltpu.VMEM)],
        out_shape=[pltpu.SemaphoreType.DMA(())] * 6 + [pltpu.HBM(a.shape, a.dtype) for a in list(sums) + lands]
        + [jax.ShapeDtypeStruct((8, LANES), F32)],
        input_output_aliases={t: t + 6 for t in range(2 * n)},
        compiler_params=pltpu.CompilerParams(has_side_effects=IN_FLIGHT),
        name="chip_exchange_start",
    )(*[pltpu.with_memory_space_constraint(a, pltpu.HBM) for a in list(sums) + lands])
    return list(res[0:3]), list(res[3:6]), list(res[6:n + 6]), list(res[n + 6:2 * n + 6]), res[2 * n + 6]


def chip_exchange_wait(send_sems, recv_sems, sums, lands, after):
    n = len(sums)

    def body(*refs):
        ins_send, ins_recv = refs[2 * n:2 * n + 3], refs[2 * n + 3:2 * n + 6]
        outs = refs[2 * n + 7:]
        for cp in _chip_copies(outs[:n], outs[n:], _sem_per_peer(ins_send, ins_recv)):
            cp.wait()

    res = pl.pallas_call(
        body,
        in_specs=[HBM_SPACE] * (2 * n) + [SEM_SPACE] * 6 + [HBM_REF],
        out_specs=[HBM_SPACE] * (2 * n),
        out_shape=[pltpu.HBM(a.shape, a.dtype) for a in list(sums) + list(lands)],
        input_output_aliases={t: t for t in range(2 * n)},
        compiler_params=pltpu.CompilerParams(has_side_effects=IN_FLIGHT),
        name="chip_exchange_wait",
    )(*sums, *lands, *send_sems, *recv_sems, after)
    return list(res[:n]), list(res[n:])


def chip_sum_into(core, chip, recv, sums, total, lo, name):
    _, n, hr, hc = recv.shape
    tm = _row_tile(hr)
    per = hr // tm

    def body(c_ref, k_ref, r_ref, s_ref, t_ref, o_ref):
        acc = s_ref[...].astype(F32)
        for j in range(N_CHIPS - 1):
            acc = acc + r_ref[j].astype(F32)
        o_ref[...] = acc

    return pl.pallas_call(
        body,
        grid_spec=pltpu.PrefetchScalarGridSpec(
            num_scalar_prefetch=2,
            grid=(n, per),
            in_specs=[pl.BlockSpec((N_CHIPS - 1, None, tm, hc), lambda a, i, c, k: (0, a, i, 0)),
                      pl.BlockSpec((None, None, tm, hc), lambda a, i, c, k: (a, k[0], i, 0)),
                      HBM_REF],
            out_specs=pl.BlockSpec((None, tm, hc), lambda a, i, c, k: (lo + a, per * c[0] + i, 0)),
        ),
        out_shape=jax.ShapeDtypeStruct(total.shape, F32),
        input_output_aliases={4: 0},
        compiler_params=_params("parallel", "parallel"),
        name=name,
    )(core, chip, recv, sums, total)


def sibling_share(totals):
    n = len(totals)

    def body(*refs):
        outs = refs[n:2 * n]
        send_sems, recv_sems = refs[2 * n:]
        x, y, c = _mesh_pos()
        half = lambda t, which: outs[t].at[pl.ds(0, DEPTH), *_core_rows(outs[t], c)[which]]
        sent = [_remote(half(t, 0), half(t, 0), send_sems.at[t], recv_sems.at[t], (x, y, 1 - c)) for t in range(n)]
        for cp in sent:
            cp.start()
        for t in range(n):
            _remote(half(t, 1), half(t, 1), send_sems.at[t], recv_sems.at[t], (x, y, 1 - c)).wait_recv()
        for cp in sent:
            cp.wait_send()

    return pl.pallas_call(
        body,
        in_specs=[HBM_REF] * n,
        out_specs=[HBM_REF] * n,
        out_shape=[jax.ShapeDtypeStruct(t.shape, t.dtype) for t in totals],
        input_output_aliases={t: t for t in range(n)},
        scratch_shapes=[pltpu.SemaphoreType.DMA((n,)), pltpu.SemaphoreType.DMA((n,))],
        name="sibling_share",
    )(*totals)


def small_allgather(small):
    def body(s_ref, a_ref, send_sems, recv_sems, local_sem):
        x, y, c = _mesh_pos()
        me = 4 * x + 2 * y + c
        own = pltpu.make_async_copy(s_ref, a_ref.at[me], local_sem)
        own.start()
        sent = []
        for k in range(1, N_DEV):
            peer = (x ^ (k >> 2), y ^ ((k >> 1) & 1), c ^ (k & 1))
            cp = _remote(s_ref, a_ref.at[me], send_sems.at[k - 1], recv_sems.at[k - 1], peer)
            cp.start()
            sent.append(cp)
        for k in range(1, N_DEV):
            px, py, pc = x ^ (k >> 2), y ^ ((k >> 1) & 1), c ^ (k & 1)
            _remote(s_ref, a_ref.at[4 * px + 2 * py + pc], send_sems.at[k - 1], recv_sems.at[k - 1], (px, py, pc)).wait_recv()
        for cp in sent:
            cp.wait_send()
        own.wait()

    return pl.pallas_call(
        body,
        in_specs=[HBM_REF],
        out_specs=HBM_REF,
        out_shape=jax.ShapeDtypeStruct((N_DEV,) + SMALL_SHAPE, small.dtype),
        scratch_shapes=[pltpu.SemaphoreType.DMA((N_DEV - 1,)), pltpu.SemaphoreType.DMA((N_DEV - 1,)), pltpu.SemaphoreType.DMA],
        name="small_allgather",
    )(small)


def small_sum(blocks):
    def body(a_ref, o_ref):
        acc = a_ref[0]
        for d in range(1, N_DEV):
            acc = acc + a_ref[d]
        o_ref[...] = acc

    return pl.pallas_call(
        body,
        in_specs=[pl.BlockSpec(memory_space=pltpu.VMEM)],
        out_specs=pl.BlockSpec(memory_space=pltpu.VMEM),
        out_shape=jax.ShapeDtypeStruct(SMALL_SHAPE, F32),
        name="small_sum",
    )(blocks)


def pack_small(grads, loss):
    flat = jnp.concatenate([grads[n].reshape(-1) for n in SMALL] + [loss.reshape(-1)])
    size = SMALL_SHAPE[0] * SMALL_SHAPE[1]
    return jnp.pad(flat, (0, size - flat.shape[0])).reshape(SMALL_SHAPE)


def unpack_small(packed):
    flat = packed.reshape(-1)
    out, off = {}, 0
    for n, size in SMALL.items():
        out[n] = flat[off:off + DEPTH * size].reshape(DEPTH, size)
        off += DEPTH * size
    return out, flat[off]


def kernel(x, norm_mix_g, w_in, forget_b, q_norm_g, k_norm_g, w_attn_out, conv_w, w_conv_out, pool_w, pool_scale, w_o, norm_ffn_g, w_ffn_in, w_ffn_out, loss_target, m_norm_mix_g, m_w_in, m_forget_b, m_q_norm_g, m_k_norm_g, m_w_attn_out, m_conv_w, m_w_conv_out, m_pool_w, m_pool_scale, m_w_o, m_norm_ffn_g, m_w_ffn_in, m_w_ffn_out, v_norm_mix_g, v_w_in, v_forget_b, v_q_norm_g, v_k_norm_g, v_w_attn_out, v_conv_w, v_w_conv_out, v_pool_w, v_pool_scale, v_w_o, v_norm_ffn_g, v_w_ffn_in, v_w_ffn_out):
    given = dict(locals())
    weights = {n: given[n] for n in WEIGHTS}

    core = lax.axis_index("c").astype(jnp.int32)
    chip = (2 * lax.axis_index("x") + lax.axis_index("y")).astype(jnp.int32)

    core, chip = core.reshape(1), chip.reshape(1)
    mix_names, all_names = MIX_W + ["conv_w"], MIX_W + FFN_W + ["conv_w"]

    def placed(lo, hi, names, after=None, first=False):
        piece = None
        if after is not None:
            piece = _as4(next(iter(after.values())))[0, 0, 0:HALO, 0:LANES] if isinstance(after, dict) else after
        return [place_shard(chip, weights[n][lo:hi] if first else weights[n], 0 if first else lo, hi - lo if first else hi,
                            F32 if n == "conv_w" else BF16, "place_" + n, piece) for n in names]

    def as_weights(bufs, names):
        return {n: b.reshape(b.shape[:2] + weights[n].shape[1:]) for n, b in zip(names, bufs)}

    def landed(start, names, after):
        return as_weights(gather_forward(gather_wait(*start[:3], after)), names)

    def layer_small(l, *tokens):
        small = {n: weights[n][l] for n in SMALL if n != "conv_w"}
        for t in tokens:
            small["norm_mix_g"] = small["norm_mix_g"] + t[0, 0]
        return small

    done = [None] * DEPTH
    first = gather_start(placed(0, 1, mix_names, first=True))
    next_bufs = (placed(0, 1, FFN_W, after=first[3]), placed(1, 2, all_names, after=first[3]))
    first = landed(first, mix_names, next_bufs[1][0])
    (ffn0, layer1), _ = lax.optimization_barrier((next_bufs, first["conv_w"]))
    ffn0, layer1 = gather_start(ffn0), gather_start(layer1)
    xm, wm, sm = mix_fwd(x[0], first, 0, layer_small(0, ffn0[3], layer1[3]))
    xs, wf, sf = ffn_half_fwd(xm, landed(ffn0, FFN_W, xm), 0, weights["norm_ffn_g"][0])
    done[0] = (wm, sm, wf, sf)
    full = landed(layer1, all_names, xs)
    rest = gather_start(placed(2, DEPTH, all_names, after=full))
    xm, wm, sm = mix_fwd(xs, full, 0, layer_small(1, rest[3]))
    xs, wf, sf = ffn_half_fwd(xm, full, 0, weights["norm_ffn_g"][1])
    done[1] = (wm, sm, wf, sf)
    full = landed(rest, all_names, xs)
    for l in range(2, DEPTH):
        xm, wm, sm = mix_fwd(xs, full, l - 2, layer_small(l))
        xs, wf, sf = ffn_half_fwd(xm, full, l - 2, weights["norm_ffn_g"][l])
        done[l] = (wm, sm, wf, sf)
    loss, dx = loss_head(xs, loss_target[0])

    small_grads = [None] * DEPTH
    totals = {n: lax.empty((DEPTH,) + _as4(weights[n][None]).shape[2:], F32) for n in SPLIT}

    def buffers(names, n_layers):
        return {n: lax.empty((n_layers, N_CHIPS) + weights[n].shape[1:], F32) for n in names}

    def pair_sums(names, grads, theirs):
        return [pair_sum(core, a, t, "pair_sum_" + n) for n, a, t in zip(names, grads, theirs)]

    def add_chips(names, recv, sums, lo):
        for n, r, s in zip(names, recv, sums):
            totals[n] = chip_sum_into(core, chip, r, s, totals[n], lo, "chip_sum_" + n)

    big = buffers(SPLIT, DEPTH - 1)
    for l in reversed(range(1, DEPTH)):
        wm, sm, wf, sf = done[l]
        dxm, ffn_big, g_ffn = ffn_half_bwd(dx, wf, sf, l - 1, {n: big[n] for n in FFN_W})
        dx, mix_big, g_mix = mix_bwd(dxm, wm, sm, l - 1, {n: big[n] for n in MIX_W})
        big = {**ffn_big, **mix_big}
        small_grads[l] = {**g_ffn, **g_mix}
    pair = pair_exchange_start([_as4(big[n]) for n in SPLIT])

    wm, sm, wf, sf = done[0]
    wf = dict(wf, g2=wf["g2"] + pair[4][0:1, 0:1])
    dxm, ffn_big, g_ffn = ffn_half_bwd(dx, wf, sf, 0, buffers(FFN_W, 1))
    chips = chip_exchange_start(pair_sums(SPLIT, *pair_exchange_wait(*pair[:4], ffn_big["w_ffn_in"])))
    ffn_grads = [_as4(ffn_big[n]) for n in FFN_W]
    ffn_chips = chip_exchange_start(pair_sums(FFN_W, ffn_grads, pair_exchange(ffn_grads)))
    wm = dict(wm, pscale=wm["pscale"] + chips[4][0:1, 0:1] + ffn_chips[4][0:1, 0:1])
    dproj, mix_big, g_mix = mix_bwd_weights(dxm, wm, sm, 0, buffers(MIX_W, 1))
    mix_grads = [_as4(mix_big[n]) for n in MIX_W]
    mix_chips = chip_exchange_start(pair_sums(MIX_W, mix_grads, pair_exchange(mix_grads)))
    dx, g_in = mix_bwd_input(dproj, dxm, dict(wm, g1=wm["g1"] + mix_chips[4][0:1, 0:1]), sm)
    small_grads[0] = {**g_ffn, **g_mix, **g_in}

    reduced, deltas, new_m, new_v = {}, {}, {}, {}

    def update(names):
        for n in names:
            w = weights[n]
            flat = (-1, w.shape[-1])
            d, nm, nv = adamw(w.reshape(flat), reduced[n].reshape(flat), given["m_" + n].reshape(flat),
                              given["v_" + n].reshape(flat), "adamw_" + n)
            deltas[n], new_m[n], new_v[n] = d.reshape(w.shape), nm.reshape(w.shape), nv.reshape(w.shape)

    def share(names):
        for n, t in zip(names, sibling_share([totals[n] for n in names])):
            reduced[n] = t.reshape(weights[n].shape)

    sums, recv = chip_exchange_wait(*chips[:4], dx)
    add_chips(SPLIT, recv, sums, 1)
    sums, recv = chip_exchange_wait(*ffn_chips[:4], dx)
    add_chips(FFN_W, recv, sums, 0)
    share(FFN_W)
    update(FFN_W)
    sums, recv = chip_exchange_wait(*mix_chips[:4], deltas["w_ffn_in"])
    add_chips(MIX_W, recv, sums, 0)
    share(MIX_W)

    small_grads = {n: jnp.stack([g[n] for g in small_grads]) for n in SMALL}
    small_total, loss_sum = unpack_small(small_sum(small_allgather(pack_small(small_grads, loss))))
    chip = chip[0]
    cols = D_CONV // N_CHIPS
    small_total["conv_w"] = lax.dynamic_slice_in_dim(small_total["conv_w"].reshape(DEPTH, CONV_K, D_CONV), chip * cols, cols, axis=2)
    for n in SMALL:
        reduced[n] = small_total[n].reshape(weights[n].shape)
    update(MIX_W + list(SMALL))

    return (loss_sum, dx[None], *[reduced[n] for n in WEIGHTS], *[deltas[n] for n in WEIGHTS],
            *[new_m[n] for n in WEIGHTS], *[new_v[n] for n in WEIGHTS])
```

```python
import functools

import numpy as np
import jax
import jax.numpy as jnp
from jax import lax
from jax.experimental import pallas as pl
from jax.experimental.pallas import tpu as pltpu

F32 = jnp.float32
BF16 = jnp.bfloat16

D_MODEL = 1024
DEPTH = 4
HEAD_DIM = 64
N_HEADS = 8
D_ATTN = 512
D_CONV = 256
D_POOL = 256
D_FF = 2816
D_IN = 5640
CONV_K = 3
POOL_WINDOWS = (2, 4, 8, 16)
N_GROUPS = len(POOL_WINDOWS)
EPS = 1e-6
ADAM_LR, ADAM_B1, ADAM_B2, ADAM_EPS, ADAM_WD, ADAM_STEP = 0.001, 0.9, 0.999, 1e-08, 0.01, 10

D_QKV = 3 * D_ATTN
D_F = 128
D_B = 3 * D_CONV + D_POOL + 3 * D_MODEL
D_LOCAL = 3 * D_CONV + D_POOL

LANES = 128
D_HEADS = N_HEADS * LANES
HALO = 16
VMEM_LIMIT = 56 * 1024 * 1024
NEG = -1e30
LOG2E = 1.4426950408889634
LN2 = 0.6931471805599453

TM = 256
TM_MIX = 256
TQ = 1024

LANE_C = 64
LANE_ONE = 67
LANE_LSE = 70
N_PIECES = 3


def _dot(a, b):
    return jnp.dot(a, b, preferred_element_type=F32)


def _dot_nt(a, b):
    return lax.dot_general(a, b, (((1,), (1,)), ((), ())), preferred_element_type=F32)


def _dot_tn(a, b):
    return lax.dot_general(a, b, (((0,), (0,)), ((), ())), preferred_element_type=F32)


def _params(*sem):
    return pltpu.CompilerParams(dimension_semantics=sem, vmem_limit_bytes=VMEM_LIMIT)


def _rows(tm, n):
    return pl.BlockSpec((tm, n), lambda i, *_: (i, 0))


def _whole(a):
    nd = a.ndim
    return pl.BlockSpec(a.shape, lambda *_: (0,) * nd)


def _layer(shape):
    nd = len(shape)
    return pl.BlockSpec((None,) + tuple(shape), lambda *a: (a[-1][0],) + (0,) * nd)


def _layer_index(l):
    return jnp.full((1,), l, jnp.int32)


def _split_bf16(x):
    hi = x.astype(BF16)
    lo = (x - hi.astype(F32)).astype(BF16)
    return hi, lo


def _pieces(x):
    p1 = x.astype(BF16)
    r1 = x - p1.astype(F32)
    p2 = r1.astype(BF16)
    p3 = (r1 - p2.astype(F32)).astype(BF16)
    return p1, p2, p3


def _sigmoid(x):
    return 1.0 / (1.0 + jnp.exp(-x))


def w_in_prep(win, l):
    tr = 256
    n = D_IN // 4
    v_rest = D_QKV - n
    b0 = v_rest + N_HEADS

    def body(s0, s1, s2, s3, wa_ref, wf_ref, wb_ref):
        b = s1[...]
        wa_ref[...] = jnp.concatenate([s0[...], b[:, 0:v_rest]], axis=1)
        wf_ref[...] = jnp.concatenate([b[:, v_rest:b0], jnp.zeros((tr, D_F - N_HEADS), b.dtype)], axis=1)
        wb_ref[...] = jnp.concatenate([b[:, b0:n], s2[...], s3[...]], axis=1)

    shard = lambda j: pl.BlockSpec((None, None, tr, n), lambda i: (l, j, i, 0))
    return pl.pallas_call(
        body,
        grid=(D_MODEL // tr,),
        in_specs=[shard(0), shard(1), shard(2), shard(3)],
        out_specs=[_rows(tr, D_QKV), _rows(tr, D_F), _rows(tr, D_B)],
        out_shape=[jax.ShapeDtypeStruct((D_MODEL, D_QKV), win.dtype), jax.ShapeDtypeStruct((D_MODEL, D_F), win.dtype),
                   jax.ShapeDtypeStruct((D_MODEL, D_B), win.dtype)],
        compiler_params=_params("parallel"),
        name="w_in_prep",
    )(win, win, win, win)


def _into_layer(buf):
    return dict(in_spec=HBM_REF, out_shape=jax.ShapeDtypeStruct(buf.shape, buf.dtype), aliases={1: 0})


def w_in_unprep(dwa, dwf, dwb, buf, l):
    tr = 256
    n = D_IN // 4
    v_rest = D_QKV - n
    b1 = n - v_rest - N_HEADS
    place = _into_layer(buf)

    def body(l_ref, buf_ref, a_ref, f_ref, b_ref, o_ref):
        a = a_ref[...]
        b = b_ref[...]
        o_ref[0] = a[:, 0:n]
        o_ref[1] = jnp.concatenate([a[:, n:D_QKV], f_ref[:, 0:N_HEADS], b[:, 0:b1]], axis=1)
        o_ref[2] = b[:, b1:b1 + n]
        o_ref[3] = b[:, b1 + n:D_B]

    return pl.pallas_call(
        body,
        grid_spec=pltpu.PrefetchScalarGridSpec(
            num_scalar_prefetch=1,
            grid=(D_MODEL // tr,),
            in_specs=[place["in_spec"], _rows(tr, D_QKV), _rows(tr, D_F), _rows(tr, D_B)],
            out_specs=pl.BlockSpec((None, 4, tr, n), lambda i, l: (l[0], 0, i, 0)),
        ),
        out_shape=place["out_shape"],
        input_output_aliases=place["aliases"],
        compiler_params=_params("parallel"),
        name="w_in_unprep",
    )(_layer_index(l), buf, dwa, dwf, dwb)


def branch_w_prep(wao, wco, pw, cw, l):
    gd = D_POOL // N_GROUPS
    od = D_MODEL // N_GROUPS

    def body(wao_ref, wco_ref, pw_ref, cw_ref, ao_ref, co_ref, po_ref, co8_ref):
        ao_ref[...] = jnp.concatenate([wao_ref[j] for j in range(4)], axis=1)
        co_ref[...] = jnp.concatenate([wco_ref[j] for j in range(4)], axis=1)
        zero = jnp.zeros((gd, od), co_ref.dtype)
        po_ref[...] = jnp.concatenate(
            [jnp.concatenate([jnp.concatenate([pw_ref[j, g] for j in range(4)], axis=1) if g2 == g else zero
                              for g2 in range(N_GROUPS)], axis=1) for g in range(N_GROUPS)], axis=0)
        co8_ref[...] = jnp.zeros_like(co8_ref)
        co8_ref[0:CONV_K, :] = jnp.concatenate([cw_ref[j] for j in range(4)], axis=1)

    sel = lambda *shape: pl.BlockSpec((None,) + shape, lambda i: (l,) + (0,) * len(shape))
    return pl.pallas_call(
        body,
        grid=(1,),
        in_specs=[sel(4, D_ATTN, D_MODEL // 4), sel(4, D_CONV, D_MODEL // 4), sel(4, N_GROUPS, gd, od // 4),
                  sel(4, CONV_K, D_CONV // 4)],
        out_specs=[pl.BlockSpec((D_ATTN, D_MODEL), lambda i: (0, 0)), pl.BlockSpec((D_CONV, D_MODEL), lambda i: (0, 0)),
                   pl.BlockSpec((D_POOL, D_MODEL), lambda i: (0, 0)), pl.BlockSpec((8, D_CONV), lambda i: (0, 0))],
        out_shape=[jax.ShapeDtypeStruct((D_ATTN, D_MODEL), BF16), jax.ShapeDtypeStruct((D_CONV, D_MODEL), BF16),
                   jax.ShapeDtypeStruct((D_POOL, D_MODEL), BF16), jax.ShapeDtypeStruct((8, D_CONV), F32)],
        name="branch_w_prep",
    )(wao, wco, pw, cw)


def branch_g_place(dwao, dwco, dwpool, dwo, bufs, l):
    gd = D_POOL // N_GROUPS
    od = D_MODEL // N_GROUPS
    q = D_MODEL // 4

    def body(l_ref, b0, b1, b2, b3, a_ref, c_ref, p_ref, w_ref, ao_ref, co_ref, po_ref, wo_ref):
        a = a_ref[...]
        c = c_ref[...]
        p = p_ref[...]
        for j in range(4):
            ao_ref[j] = a[:, j * q:(j + 1) * q]
            co_ref[j] = c[:, j * q:(j + 1) * q]
            wo_ref[j] = w_ref[j * q:(j + 1) * q, :]
            for g in range(N_GROUPS):
                c0 = g * od + j * (od // 4)
                po_ref[j, g] = p[g * gd:(g + 1) * gd, c0:c0 + od // 4]

    whole = lambda a: pl.BlockSpec(a.shape, lambda i, l: (0,) * a.ndim)
    layer = lambda b: pl.BlockSpec((None,) + b.shape[1:], lambda i, l: (l[0],) + (0,) * (b.ndim - 1))
    return pl.pallas_call(
        body,
        grid_spec=pltpu.PrefetchScalarGridSpec(
            num_scalar_prefetch=1,
            grid=(1,),
            in_specs=[HBM_REF] * 4 + [whole(dwao), whole(dwco), whole(dwpool), whole(dwo)],
            out_specs=[layer(b) for b in bufs],
        ),
        out_shape=[jax.ShapeDtypeStruct(b.shape, b.dtype) for b in bufs],
        input_output_aliases={1: 0, 2: 1, 3: 2, 4: 3},
        compiler_params=_params("arbitrary"),
        name="branch_g_place",
    )(_layer_index(l), *bufs, dwao, dwco, dwpool, dwo)


def in_proj_fwd(x, g, wa, wf, wb):
    S = x.shape[0]

    def body(x_ref, g_ref, wa_ref, wf_ref, wb_ref, h_ref, pa_ref, pf_ref, pb_ref):
        xf = x_ref[...]
        r = lax.rsqrt(jnp.mean(xf * xf, axis=-1, keepdims=True) + EPS)
        h = (xf * r * g_ref[...]).astype(BF16)
        h_ref[...] = h
        pa_ref[...] = _dot(h, wa_ref[...]).astype(BF16)
        pf_ref[...] = _dot(h, wf_ref[...])
        pb_ref[...] = _dot(h, wb_ref[...]).astype(BF16)

    return pl.pallas_call(
        body,
        grid=(S // TM,),
        in_specs=[_rows(TM, D_MODEL), _whole(g), _whole(wa), _whole(wf), _whole(wb)],
        out_specs=[_rows(TM, D_MODEL), _rows(TM, D_QKV), _rows(TM, D_F), _rows(TM, D_B)],
        out_shape=[
            jax.ShapeDtypeStruct((S, D_MODEL), BF16),
            jax.ShapeDtypeStruct((S, D_QKV), BF16),
            jax.ShapeDtypeStruct((S, D_F), F32),
            jax.ShapeDtypeStruct((S, D_B), BF16),
        ],
        compiler_params=_params("parallel"),
        name="in_proj_fwd",
    )(x, g, wa, wf, wb)


def in_proj_bwd(x, g, dxm, dpa, dpf, dpb, wa, wf, wb):
    S = x.shape[0]

    def body(x_ref, g_ref, dxm_ref, dpa_ref, dpf_ref, dpb_ref, wa_ref, wf_ref, wb_ref, dx_ref, dg_ref):
        @pl.when(pl.program_id(0) == 0)
        def _():
            dg_ref[...] = jnp.zeros_like(dg_ref)

        dh = _dot_nt(dpa_ref[...], wa_ref[...]) + _dot_nt(dpf_ref[...], wf_ref[...]) + _dot_nt(dpb_ref[...], wb_ref[...])
        xf = x_ref[...]
        r = lax.rsqrt(jnp.mean(xf * xf, axis=-1, keepdims=True) + EPS)
        xhat = xf * r
        dg_ref[...] += jnp.sum(dh * xhat, axis=0, keepdims=True)
        gdh = dh * g_ref[...]
        dx_ref[...] = dxm_ref[...] + r * (gdh - xhat * jnp.mean(xhat * gdh, axis=-1, keepdims=True))

    return pl.pallas_call(
        body,
        grid=(S // TM,),
        in_specs=[_rows(TM, D_MODEL), _whole(g), _rows(TM, D_MODEL), _rows(TM, D_QKV), _rows(TM, D_F), _rows(TM, D_B),
                  _whole(wa), _whole(wf), _whole(wb)],
        out_specs=[_rows(TM, D_MODEL), pl.BlockSpec((1, D_MODEL), lambda i: (0, 0))],
        out_shape=[jax.ShapeDtypeStruct((S, D_MODEL), F32), jax.ShapeDtypeStruct((1, D_MODEL), F32)],
        compiler_params=_params("arbitrary"),
        name="in_proj_bwd",
    )(x, g, dxm, dpa, dpf, dpb, wa, wf, wb)


def _wgrad_tokens(S, K):
    return min(S, 2048 if K <= 1024 else 1024)


def _wgrad_columns(n):
    return next(tn for tn in (1024, 768, 512, 256, 128) if n % tn == 0)


def wgrad(xa, dy, name):
    S, K = xa.shape
    N = dy.shape[1]
    ts = _wgrad_tokens(S, K)
    tn = _wgrad_columns(N)
    tk = _wgrad_columns(K)

    def body(x_ref, dy_ref, o_ref):
        @pl.when(pl.program_id(2) == 0)
        def _():
            o_ref[...] = jnp.zeros_like(o_ref)

        o_ref[...] += _dot_tn(x_ref[...], dy_ref[...])

    return pl.pallas_call(
        body,
        grid=(K // tk, N // tn, S // ts),
        in_specs=[pl.BlockSpec((ts, tk), lambda i, j, k: (k, i)), pl.BlockSpec((ts, tn), lambda i, j, k: (k, j))],
        out_specs=pl.BlockSpec((tk, tn), lambda i, j, k: (i, j)),
        out_shape=jax.ShapeDtypeStruct((K, N), F32),
        compiler_params=_params("parallel", "parallel", "arbitrary"),
        name=name,
    )(xa, dy)


def wgrad_into(xa, dy, name, buf, l):
    S, K = xa.shape
    N = dy.shape[1]
    ts = _wgrad_tokens(S, K)
    split = buf.ndim == 4
    tn = buf.shape[-1] if split else _wgrad_columns(N)
    place = _into_layer(buf)

    def body(l_ref, buf_ref, x_ref, dy_ref, o_ref):
        @pl.when(pl.program_id(1) == 0)
        def _():
            o_ref[...] = jnp.zeros_like(o_ref)

        o_ref[...] += _dot_tn(x_ref[...], dy_ref[...])

    if split:
        out_spec = pl.BlockSpec((None, None, K, tn), lambda j, k, l: (l[0], j, 0, 0))
    else:
        out_spec = pl.BlockSpec((None, K, tn), lambda j, k, l: (l[0], 0, j))
    return pl.pallas_call(
        body,
        grid_spec=pltpu.PrefetchScalarGridSpec(
            num_scalar_prefetch=1,
            grid=(N // tn, S // ts),
            in_specs=[place["in_spec"], pl.BlockSpec((ts, K), lambda j, k, l: (k, 0)),
                      pl.BlockSpec((ts, tn), lambda j, k, l: (k, j))],
            out_specs=out_spec,
        ),
        out_shape=place["out_shape"],
        input_output_aliases=place["aliases"],
        compiler_params=_params("parallel", "arbitrary"),
        name=name,
    )(_layer_index(l), buf, xa, dy)


def _head_mean_matrix():
    h = np.arange(D_ATTN) // HEAD_DIM
    return jnp.asarray((h[:, None] == h[None, :]).astype(np.float32) / HEAD_DIM, BF16)


def _place_matrix(lane0):
    m = np.zeros((N_PIECES * LANES, D_HEADS), np.float32)
    for i in range(N_PIECES):
        for h in range(N_HEADS):
            m[i * LANES + h, h * LANES + lane0 + i] = 1.0
    return jnp.asarray(m, BF16)


def _tri(n, upper):
    r = np.arange(n)
    m = (r[None, :] >= r[:, None]) if upper else (r[None, :] <= r[:, None])
    return jnp.asarray(m.astype(np.float32), BF16)


def _lanes_in(lane, lo, n):
    return (lane >= lo) & (lane < lo + n)


def qk_prep(pa, pf, gq, gk, fb):
    S = pa.shape[0]
    bd = _head_mean_matrix()
    tri = _tri(TM, upper=False)
    place_q = _place_matrix(LANE_C)
    place_k = _place_matrix(LANE_ONE)

    def body(q_ref, k_ref, v_ref, pf_ref, gq_ref, gk_ref, fb_ref, bd_ref, tri_ref, pq_ref, pk_ref,
             qx_ref, kx_ref, vx_ref, carry):
        @pl.when(pl.program_id(0) == 0)
        def _():
            carry[...] = jnp.zeros_like(carry)

        def head_norm(x_ref, g_ref, scale):
            xf = x_ref[...].astype(F32)
            ms = _dot((xf * xf).astype(BF16), bd_ref[...])
            return xf * lax.rsqrt(ms + EPS) * g_ref[...] * scale

        qh = head_norm(q_ref, gq_ref, HEAD_DIM ** -0.5 * LOG2E)
        kh = head_norm(k_ref, gk_ref, 1.0)
        vf = v_ref[...].astype(F32)

        z = pf_ref[...] + fb_ref[...]
        logf = jnp.minimum(z, 0.0) - jnp.log(1.0 + jnp.exp(-jnp.abs(z)))
        hi, lo = _split_bf16(logf)
        c = _dot(tri_ref[...], hi) + _dot(tri_ref[...], lo) + carry[...]
        carry[...] += jnp.sum(hi.astype(F32) + lo.astype(F32), axis=0, keepdims=True)
        pieces = jnp.concatenate(_pieces(c * LOG2E), axis=1)
        cq = _dot(pieces, pq_ref[...])
        ck = _dot(pieces, pk_ref[...])

        lane = lax.broadcasted_iota(jnp.int32, (TM, LANES), 1)
        low = lane < HEAD_DIM
        ones_q = _lanes_in(lane, LANE_ONE, N_PIECES).astype(F32)
        ones_k = (_lanes_in(lane, LANE_C, N_PIECES) | _lanes_in(lane, LANE_LSE, N_PIECES)).astype(F32)
        ones_v = _lanes_in(lane, LANE_C, N_PIECES + 1).astype(F32)
        for h in range(N_HEADS):
            blk = slice((h // 2) * LANES, (h // 2 + 1) * LANES)
            head = (lambda a: pltpu.roll(a[:, blk], HEAD_DIM, 1)) if h % 2 else (lambda a: a[:, blk])
            mine = slice(h * LANES, (h + 1) * LANES)
            qx_ref[h] = jnp.where(low, head(qh), cq[:, mine] + ones_q).astype(BF16)
            kx_ref[h] = jnp.where(low, head(kh), ones_k - ck[:, mine]).astype(BF16)
            vx_ref[h] = jnp.where(low, head(vf), ones_v).astype(BF16)

    heads = pl.BlockSpec((N_HEADS, TM, LANES), lambda i: (0, i, 0))
    out = jax.ShapeDtypeStruct((N_HEADS, S, LANES), BF16)
    return pl.pallas_call(
        body,
        grid=(S // TM,),
        in_specs=[pl.BlockSpec((TM, D_ATTN), lambda i: (i, 0)), pl.BlockSpec((TM, D_ATTN), lambda i: (i, 1)),
                  pl.BlockSpec((TM, D_ATTN), lambda i: (i, 2)),
                  _rows(TM, D_F), _whole(gq), _whole(gk), _whole(fb), _whole(bd), _whole(tri), _whole(place_q), _whole(place_k)],
        out_specs=[heads, heads, heads],
        out_shape=[out, out, out],
        scratch_shapes=[pltpu.VMEM((1, D_F), F32)],
        compiler_params=_params("arbitrary"),
        name="qk_prep",
    )(pa, pa, pa, pf, gq, gk, fb, bd, tri, place_q, place_k)


def attn_fwd(qx, kx, vx):
    S = qx.shape[1]
    nq = S // TQ

    def body(q_ref, k_ref, v_ref, o_ref, qb_ref):
        i = pl.program_id(1)
        lane = lax.broadcasted_iota(jnp.int32, (TQ, LANES), 1)
        q = [q_ref[0], q_ref[1]]

        def update(qh, m, acc, k, v, visible):
            z = _dot_nt(qh, k)
            if visible is not None:
                z = jnp.where(visible, z, NEG)
            m_new = jnp.maximum(m, jnp.max(z, axis=1, keepdims=True))
            pr = jnp.exp2(z - m_new)
            return m_new, jnp.exp2(m - m_new) * acc + _dot(pr.astype(BF16), v)

        def step(kt, carry, diagonal=False):
            ks = pl.multiple_of(kt * TQ, TQ)
            visible = None
            if diagonal:
                visible = lax.broadcasted_iota(jnp.int32, (TQ, TQ), 0) >= lax.broadcasted_iota(jnp.int32, (TQ, TQ), 1)
            return tuple(update(q[j], *carry[j], k_ref[j, pl.ds(ks, TQ), :], v_ref[j, pl.ds(ks, TQ), :], visible)
                         for j in range(2))

        init = tuple((jnp.full((TQ, 1), NEG, F32), jnp.zeros((TQ, LANES), F32)) for _ in range(2))
        carry = step(i, lax.fori_loop(0, i, step, init), diagonal=True)
        for j in range(2):
            m, acc = carry[j]
            l = jnp.sum(jnp.where(lane == LANE_C, acc, 0.0), axis=1, keepdims=True)
            o_ref[j] = acc / l
            n1, n2, n3 = _pieces(-(m + jnp.log(l) * LOG2E))
            qb_ref[j] = jnp.where(lane == LANE_LSE, n1,
                                  jnp.where(lane == LANE_LSE + 1, n2, jnp.where(lane == LANE_LSE + 2, n3, q[j])))

    pair_tile = pl.BlockSpec((2, TQ, LANES), lambda p, i: (p, i, 0))
    pair_all = pl.BlockSpec((2, S, LANES), lambda p, i: (p, 0, 0))
    return pl.pallas_call(
        body,
        grid=(N_HEADS // 2, nq),
        in_specs=[pair_tile, pair_all, pair_all],
        out_specs=[pair_tile, pair_tile],
        out_shape=[jax.ShapeDtypeStruct((N_HEADS, S, LANES), F32), jax.ShapeDtypeStruct((N_HEADS, S, LANES), BF16)],
        compiler_params=_params("parallel", "parallel"),
        name="attn_fwd",
    )(qx, kx, vx)


def attn_bwd(qxb, kx, vx, dox):
    S = qxb.shape[1]
    nq = S // TQ

    def body(q_ref, k_ref, v_ref, do_ref, dq_ref, dk_ref, dv_ref):
        kt = pl.program_id(1)

        @pl.when(kt == 0)
        def _():
            dq_ref[...] = jnp.zeros_like(dq_ref)

        k = [k_ref[0], k_ref[1]]
        v = [v_ref[0], v_ref[1]]

        def block(j, q0, nq_rows, k0, nk, diagonal):
            q = q_ref[j, pl.ds(q0, nq_rows), :]
            dout = do_ref[j, pl.ds(q0, nq_rows), :]
            kj, vj = k[j][k0:k0 + nk], v[j][k0:k0 + nk]
            z = _dot_nt(q, kj)
            if diagonal:
                visible = (lax.broadcasted_iota(jnp.int32, (nq_rows, nk), 0)
                           >= lax.broadcasted_iota(jnp.int32, (nq_rows, nk), 1))
                z = jnp.where(visible, z, NEG)
            pr = jnp.exp2(z)
            dv = _dot_tn(pr.astype(BF16), dout)
            dsb = (pr * _dot_nt(dout, vj)).astype(BF16)
            dq_ref[j, pl.ds(q0, nq_rows), :] += _dot(dsb, kj)
            return _dot_tn(dsb, q), dv

        def step(qi, carry):
            qs = pl.multiple_of(qi * TQ, TQ)
            new = []
            for j in range(2):
                dk, dv = block(j, qs, TQ, 0, TQ, False)
                new.append((carry[j][0] + dk, carry[j][1] + dv))
            return tuple(new)

        half = TQ // 2
        qs = pl.multiple_of(kt * TQ, TQ)
        carry = []
        for j in range(2):
            first = block(j, qs, TQ, 0, half, True)
            second = block(j, pl.multiple_of(qs + half, half), half, half, half, True)
            carry.append(tuple(jnp.concatenate(a, axis=0) for a in zip(first, second)))
        carry = lax.fori_loop(kt + 1, nq, step, tuple(carry))
        for j in range(2):
            dk_ref[j] = carry[j][0]
            dv_ref[j] = carry[j][1]

    pair_tile = pl.BlockSpec((2, TQ, LANES), lambda p, kt: (p, kt, 0))
    pair_all = pl.BlockSpec((2, S, LANES), lambda p, kt: (p, 0, 0))
    out = jax.ShapeDtypeStruct((N_HEADS, S, LANES), F32)
    return pl.pallas_call(
        body,
        grid=(N_HEADS // 2, nq),
        in_specs=[pair_all, pair_tile, pair_tile, pair_all],
        out_specs=[pair_all, pair_tile, pair_tile],
        out_shape=[out, out, out],
        compiler_params=_params("arbitrary", "arbitrary"),
        name="attn_bwd",
    )(qxb, kx, vx, dox)


def attn_bwd_post(pa, pf, dqx, dkx, dvx, gq, gk, fb):
    S = pa.shape[0]
    nt = S // TM
    bd = _head_mean_matrix()
    triu = _tri(TM, upper=True)
    rev = lambda i: nt - 1 - i

    def body(q_ref, k_ref, pf_ref, dqx_ref, dkx_ref, dvx_ref, gq_ref, gk_ref, fb_ref, bd_ref, triu_ref,
             dpa_ref, dpf_ref, dgq_ref, dgk_ref, dfb_ref, carry):
        @pl.when(pl.program_id(0) == 0)
        def _():
            carry[...] = jnp.zeros_like(carry)
            dgq_ref[...] = jnp.zeros_like(dgq_ref)
            dgk_ref[...] = jnp.zeros_like(dgk_ref)
            dfb_ref[...] = jnp.zeros_like(dfb_ref)

        lane = lax.broadcasted_iota(jnp.int32, (TM, LANES), 1)
        columns = _heads_as_columns

        def head_norm_bwd(x_ref, dy, g_ref, dg_ref):
            xf = x_ref[...].astype(F32)
            r = lax.rsqrt(_dot((xf * xf).astype(BF16), bd_ref[...]) + EPS)
            xhat = xf * r
            dg_ref[...] += jnp.sum(dy * xhat, axis=0, keepdims=True)
            gdy = dy * g_ref[...]
            return (r * (gdy - xhat * _dot((xhat * gdy).astype(BF16), bd_ref[...]))).astype(BF16)

        dpa_ref[:, 0:D_ATTN] = head_norm_bwd(q_ref, columns(dqx_ref) * HEAD_DIM ** -0.5, gq_ref, dgq_ref)
        dpa_ref[:, D_ATTN:2 * D_ATTN] = head_norm_bwd(k_ref, columns(dkx_ref) * LN2, gk_ref, dgk_ref)
        dpa_ref[:, 2 * D_ATTN:3 * D_ATTN] = columns(dvx_ref).astype(BF16)

        dc = jnp.zeros((TM, LANES), F32)
        for h in range(N_HEADS):
            both = jnp.where(lane == LANE_C, dqx_ref[h], 0.0) - jnp.where(lane == LANE_ONE, dkx_ref[h], 0.0)
            dc = jnp.where(lane == h, jnp.sum(both, axis=1, keepdims=True), dc)
        hi, lo = _split_bf16(dc)
        dlogf = _dot(triu_ref[...], hi) + _dot(triu_ref[...], lo) + carry[...]
        first = lax.broadcasted_iota(jnp.int32, (TM, D_F), 0) == 0
        carry[...] = jnp.sum(jnp.where(first, dlogf, 0.0), axis=0, keepdims=True)
        df = dlogf * _sigmoid(-(pf_ref[...] + fb_ref[...]))
        dfb_ref[...] += jnp.sum(df, axis=0, keepdims=True)
        dpf_ref[...] = df.astype(BF16)

    heads = pl.BlockSpec((N_HEADS, TM, LANES), lambda i: (0, rev(i), 0))
    return pl.pallas_call(
        body,
        grid=(nt,),
        in_specs=[pl.BlockSpec((TM, D_ATTN), lambda i: (rev(i), 0)), pl.BlockSpec((TM, D_ATTN), lambda i: (rev(i), 1)),
                  pl.BlockSpec((TM, D_F), lambda i: (rev(i), 0)), heads, heads, heads,
                  _whole(gq), _whole(gk), _whole(fb), _whole(bd), _whole(triu)],
        out_specs=[pl.BlockSpec((TM, D_QKV), lambda i: (rev(i), 0)), pl.BlockSpec((TM, D_F), lambda i: (rev(i), 0)),
                   pl.BlockSpec((1, D_ATTN), lambda i: (0, 0)), pl.BlockSpec((1, D_ATTN), lambda i: (0, 0)),
                   pl.BlockSpec((1, D_F), lambda i: (0, 0))],
        out_shape=[jax.ShapeDtypeStruct((S, D_QKV), BF16), jax.ShapeDtypeStruct((S, D_F), BF16),
                   jax.ShapeDtypeStruct((1, D_ATTN), F32), jax.ShapeDtypeStruct((1, D_ATTN), F32),
                   jax.ShapeDtypeStruct((1, D_F), F32)],
        scratch_shapes=[pltpu.VMEM((1, D_F), F32)],
        compiler_params=_params("arbitrary"),
        name="attn_bwd_post",
    )(pa, pa, pf, dqx, dkx, dvx, gq, gk, fb, bd, triu)


def _shift_down(ext, k):
    return pltpu.roll(ext, k, 0)[HALO:]


def _shift_up(ext, k, n):
    return pltpu.roll(ext, n + HALO - k, 0)[:n]


def _pool_lane_select(a2, a4, a8, a16, lane):
    return jnp.where(lane < 64, a2, jnp.where(lane < 128, a4, jnp.where(lane < 192, a8, a16)))


def _local_branches(o, pb, halo, have_prev, row0, wao, convw, wco, wpool, pscale):
    n = pb.shape[0]
    cx = pb[:, 0:D_CONV].astype(F32)
    cb = pb[:, D_CONV:2 * D_CONV].astype(F32)
    cc = pb[:, 2 * D_CONV:3 * D_CONV].astype(F32)
    px = pb[:, 3 * D_CONV:D_LOCAL].astype(F32)
    keep = have_prev.astype(F32)
    z = cc * cx
    z_ext = jnp.concatenate([halo[:, 2 * D_CONV:3 * D_CONV].astype(F32) * halo[:, 0:D_CONV].astype(F32) * keep, z], axis=0)
    z1 = _shift_down(z_ext, 1)
    z2 = _shift_down(z_ext, 2)
    conv = convw[0:1, :] * z2 + convw[1:2, :] * z1 + convw[2:3, :] * z
    cm = cb * conv

    u_ext = jnp.concatenate([halo[:, 3 * D_CONV:D_LOCAL].astype(F32) * keep, px], axis=0)
    s2 = u_ext + pltpu.roll(u_ext, 1, 0)
    s4 = s2 + pltpu.roll(s2, 2, 0)
    s8 = s4 + pltpu.roll(s4, 4, 0)
    s16 = s8 + pltpu.roll(s8, 8, 0)
    lane = lax.broadcasted_iota(jnp.int32, (n, D_POOL), 1)
    win = _pool_lane_select(2.0, 4.0, 8.0, 16.0, lane)
    t = (row0 + lax.broadcasted_iota(jnp.int32, (n, D_POOL), 0)).astype(F32)
    cnt = jnp.minimum(t + 1.0, win)
    feat = _pool_lane_select(s2[HALO:], s4[HALO:], s8[HALO:], s16[HALO:], lane) / cnt - px

    ya = _dot(o, wao)
    yc = _dot(cm.astype(BF16), wco)
    yp_pre = _dot(feat.astype(BF16), wpool)
    yp = yp_pre * pscale
    return dict(cx=cx, cb=cb, cc=cc, z=z, z1=z1, z2=z2, conv=conv, cm=cm, feat=feat, cnt=cnt, lane=lane,
                ya=ya, yc=yc, yp_pre=yp_pre, yp=yp)


def _halo_spec(tm, tile_of):
    per = tm // HALO
    return pl.BlockSpec((HALO, D_LOCAL), lambda i, *_: (jnp.maximum(tile_of(i) * per - 1, 0), 0))


def _heads_as_columns(ref):
    lane = lax.broadcasted_iota(jnp.int32, ref.shape[1:], 1)
    return jnp.concatenate([jnp.where(lane < HEAD_DIM, ref[2 * p], pltpu.roll(ref[2 * p + 1], HEAD_DIM, 1))
                            for p in range(N_HEADS // 2)], axis=1)


def mix_out_fwd(x, ox, pb, wao, convw, wco, wpool, pscale, wo_all, l):
    S = x.shape[0]
    tm = TM_MIX

    def body(l_ref, x_ref, o_ref, pb_ref, halo_ref, wao_ref, cw_ref, wco_ref, wp_ref, ps_ref, wo_ref, y_ref):
        i = pl.program_id(0)
        pb = pb_ref[...]
        o = _heads_as_columns(o_ref).astype(BF16)
        b = _local_branches(o, pb, halo_ref[...], i > 0, i * tm, wao_ref[...], cw_ref[...], wco_ref[...],
                            wp_ref[...], ps_ref[...])
        g0 = _sigmoid(pb[:, D_LOCAL:D_LOCAL + D_MODEL].astype(F32))
        g1 = _sigmoid(pb[:, D_LOCAL + D_MODEL:D_LOCAL + 2 * D_MODEL].astype(F32))
        g2 = _sigmoid(pb[:, D_LOCAL + 2 * D_MODEL:D_B].astype(F32))
        merged = g0 * b["ya"] + g1 * b["yc"] + g2 * b["yp"]
        y_ref[...] = x_ref[...] + _dot(merged.astype(BF16), wo_ref[...])

    return pl.pallas_call(
        body,
        grid_spec=pltpu.PrefetchScalarGridSpec(
            num_scalar_prefetch=1,
            grid=(S // tm,),
            in_specs=[_rows(tm, D_MODEL), pl.BlockSpec((N_HEADS, tm, LANES), lambda i, l: (0, i, 0)), _rows(tm, D_B),
                      _halo_spec(tm, lambda i: i), _whole(wao), _whole(convw), _whole(wco), _whole(wpool), _whole(pscale),
                      _layer((D_MODEL, D_MODEL))],
            out_specs=_rows(tm, D_MODEL),
        ),
        out_shape=jax.ShapeDtypeStruct((S, D_MODEL), F32),
        compiler_params=_params("parallel"),
        name="mix_out_fwd",
    )(_layer_index(l), x, ox, pb, pb, wao, convw, wco, wpool, pscale, wo_all)


def mix_out_bwd(dxm, ox, pb, wao, convw, wco, wpool, pscale, wo_all, l):
    S = dxm.shape[0]
    tm = TM_MIX
    nt = S // tm
    rev = lambda i: nt - 1 - i
    rows = lambda n: pl.BlockSpec((tm, n), lambda i, l: (rev(i), 0))
    heads = pl.BlockSpec((N_HEADS, tm, LANES), lambda i, l: (0, rev(i), 0))
    acc = lambda r, c: pl.BlockSpec((r, c), lambda i, l: (0, 0))

    def body(l_ref, dxm_ref, o_ref, pb_ref, halo_ref, wao_ref, cw_ref, wco_ref, wp_ref, ps_ref, wo_ref,
             dpb_ref, do_ref, dwo_ref, dwao_ref, dwco_ref, dwp_ref, dcw_ref, dps_ref, next_dconv, next_e):
        i = pl.program_id(0)
        r = rev(i)

        @pl.when(i == 0)
        def _():
            for ref in (dwo_ref, dwao_ref, dwco_ref, dwp_ref, dcw_ref, dps_ref, next_dconv, next_e):
                ref[...] = jnp.zeros_like(ref)

        pb = pb_ref[...]
        o = _heads_as_columns(o_ref).astype(BF16)
        cw = cw_ref[...]
        b = _local_branches(o, pb, halo_ref[...], r > 0, r * tm, wao_ref[...], cw, wco_ref[...], wp_ref[...], ps_ref[...])
        g0 = _sigmoid(pb[:, D_LOCAL:D_LOCAL + D_MODEL].astype(F32))
        g1 = _sigmoid(pb[:, D_LOCAL + D_MODEL:D_LOCAL + 2 * D_MODEL].astype(F32))
        g2 = _sigmoid(pb[:, D_LOCAL + 2 * D_MODEL:D_B].astype(F32))
        dxb = dxm_ref[...].astype(BF16)
        merged = g0 * b["ya"] + g1 * b["yc"] + g2 * b["yp"]
        dwo_ref[...] += _dot_tn(merged.astype(BF16), dxb)
        dmer = _dot_nt(dxb, wo_ref[...])
        dpb_ref[:, D_LOCAL:D_LOCAL + D_MODEL] = (dmer * b["ya"] * (g0 * (1.0 - g0))).astype(BF16)
        dpb_ref[:, D_LOCAL + D_MODEL:D_LOCAL + 2 * D_MODEL] = (dmer * b["yc"] * (g1 * (1.0 - g1))).astype(BF16)
        dpb_ref[:, D_LOCAL + 2 * D_MODEL:D_B] = (dmer * b["yp"] * (g2 * (1.0 - g2))).astype(BF16)

        dya = (dmer * g0).astype(BF16)
        dwao_ref[...] += _dot_tn(o, dya)
        da = _dot_nt(dya, wao_ref[...])
        lane = lax.broadcasted_iota(jnp.int32, (tm, LANES), 1)
        for h in range(N_HEADS):
            two = da[:, (h // 2) * LANES:(h // 2 + 1) * LANES]
            dah = jnp.where(lane < HEAD_DIM, pltpu.roll(two, HEAD_DIM, 1) if h % 2 else two, 0.0).astype(BF16)
            d1, d2, d3 = _pieces(-jnp.sum(dah.astype(F32) * o_ref[h], axis=1, keepdims=True))
            do_ref[h] = jnp.where(lane == LANE_C + 1, d1, jnp.where(lane == LANE_C + 2, d2,
                                                                  jnp.where(lane == LANE_C + 3, d3, dah)))

        dyc = (dmer * g1).astype(BF16)
        dwco_ref[...] += _dot_tn(b["cm"].astype(BF16), dyc)
        dcm = _dot_nt(dyc, wco_ref[...])
        dconv = dcm * b["cb"]
        dcw_ref[0:1, :] += jnp.sum(dconv * b["z2"], axis=0, keepdims=True)
        dcw_ref[1:2, :] += jnp.sum(dconv * b["z1"], axis=0, keepdims=True)
        dcw_ref[2:3, :] += jnp.sum(dconv * b["z"], axis=0, keepdims=True)
        d_ext = jnp.concatenate([dconv, next_dconv[...]], axis=0)
        dz = cw[2:3, :] * dconv + cw[1:2, :] * _shift_up(d_ext, 1, tm) + cw[0:1, :] * _shift_up(d_ext, 2, tm)
        next_dconv[...] = dconv[0:HALO]
        dpb_ref[:, 0:D_CONV] = (dz * b["cc"]).astype(BF16)
        dpb_ref[:, D_CONV:2 * D_CONV] = (dcm * b["conv"]).astype(BF16)
        dpb_ref[:, 2 * D_CONV:3 * D_CONV] = (dz * b["cx"]).astype(BF16)

        dyp = dmer * g2
        dps_ref[...] += jnp.sum(dyp * b["yp_pre"], axis=0, keepdims=True)
        dyps = (dyp * ps_ref[...]).astype(BF16)
        dwp_ref[...] += _dot_tn(b["feat"].astype(BF16), dyps)
        dfeat = _dot_nt(dyps, wp_ref[...])
        e = dfeat / b["cnt"]
        e_ext = jnp.concatenate([e, next_e[...]], axis=0)
        up = lambda a, k: pltpu.roll(a, tm + HALO - k, 0)
        f2 = e_ext + up(e_ext, 1)
        f4 = f2 + up(f2, 2)
        f8 = f4 + up(f4, 4)
        f16 = f8 + up(f8, 8)
        next_e[...] = e[0:HALO]
        dpb_ref[:, 3 * D_CONV:D_LOCAL] = (_pool_lane_select(f2[:tm], f4[:tm], f8[:tm], f16[:tm], b["lane"]) - dfeat).astype(BF16)

    return pl.pallas_call(
        body,
        grid_spec=pltpu.PrefetchScalarGridSpec(
            num_scalar_prefetch=1,
            grid=(nt,),
            in_specs=[rows(D_MODEL), heads, rows(D_B), _halo_spec(tm, rev),
                      _whole(wao), _whole(convw), _whole(wco), _whole(wpool), _whole(pscale), _layer((D_MODEL, D_MODEL))],
            out_specs=[rows(D_B), heads, acc(D_MODEL, D_MODEL), acc(D_ATTN, D_MODEL), acc(D_CONV, D_MODEL),
                       acc(D_POOL, D_MODEL), acc(8, D_CONV), acc(1, D_MODEL)],
            scratch_shapes=[pltpu.VMEM((HALO, D_CONV), F32), pltpu.VMEM((HALO, D_POOL), F32)],
        ),
        out_shape=[jax.ShapeDtypeStruct((S, D_B), BF16), jax.ShapeDtypeStruct((N_HEADS, S, LANES), BF16),
                   jax.ShapeDtypeStruct((D_MODEL, D_MODEL), F32), jax.ShapeDtypeStruct((D_ATTN, D_MODEL), F32),
                   jax.ShapeDtypeStruct((D_CONV, D_MODEL), F32), jax.ShapeDtypeStruct((D_POOL, D_MODEL), F32),
                   jax.ShapeDtypeStruct((8, D_CONV), F32), jax.ShapeDtypeStruct((1, D_MODEL), F32)],
        compiler_params=_params("arbitrary"),
        name="mix_out_bwd",
    )(_layer_index(l), dxm, ox, pb, pb, wao, convw, wco, wpool, pscale, wo_all)


FF_SHARD = 2 * D_FF // 4


def ffn_fwd(x, g, w1_all, w2_all, l):
    S = x.shape[0]

    def body(l_ref, x_ref, g_ref, w1_ref, w2_ref, y_ref, u_ref):
        xf = x_ref[...]
        r = lax.rsqrt(jnp.mean(xf * xf, axis=-1, keepdims=True) + EPS)
        h = (xf * r * g_ref[...]).astype(BF16)
        u = jnp.concatenate([_dot(h, w1_ref[j]) for j in range(4)], axis=1)
        u_ref[...] = u.astype(BF16)
        gt = u[:, 0:D_FF]
        act = gt * _sigmoid(gt) * u[:, D_FF:2 * D_FF]
        y_ref[...] = xf + _dot(act.astype(BF16), w2_ref[...])

    return pl.pallas_call(
        body,
        grid_spec=pltpu.PrefetchScalarGridSpec(
            num_scalar_prefetch=1,
            grid=(S // TM,),
            in_specs=[_rows(TM, D_MODEL), _whole(g), _layer((4, D_MODEL, FF_SHARD)), _layer((D_FF, D_MODEL))],
            out_specs=[_rows(TM, D_MODEL), _rows(TM, 2 * D_FF)],
        ),
        out_shape=[jax.ShapeDtypeStruct((S, D_MODEL), F32), jax.ShapeDtypeStruct((S, 2 * D_FF), BF16)],
        compiler_params=_params("parallel"),
        name="ffn_fwd",
    )(_layer_index(l), x, g, w1_all, w2_all)


def ffn_bwd(x, g, dy, u, w1_all, w2_all, l):
    S = x.shape[0]

    def body(l_ref, x_ref, g_ref, dy_ref, u_ref, w1_ref, w2_ref, dx_ref, du_ref, act_ref, h_ref, dyb_ref, dg_ref):
        @pl.when(pl.program_id(0) == 0)
        def _():
            dg_ref[...] = jnp.zeros_like(dg_ref)

        dyf = dy_ref[...]
        dyb_ref[...] = dyf.astype(BF16)
        dact = _dot_nt(dyb_ref[...], w2_ref[...])
        gt = u_ref[:, 0:D_FF].astype(F32)
        up = u_ref[:, D_FF:2 * D_FF].astype(F32)
        sg = _sigmoid(gt)
        silu = gt * sg
        act_ref[...] = (silu * up).astype(BF16)
        du_ref[:, 0:D_FF] = (dact * up * (sg * (1.0 + gt * (1.0 - sg)))).astype(BF16)
        du_ref[:, D_FF:2 * D_FF] = (dact * silu).astype(BF16)
        dh = _dot_nt(du_ref[:, 0:FF_SHARD], w1_ref[0])
        for j in range(1, 4):
            dh = dh + _dot_nt(du_ref[:, j * FF_SHARD:(j + 1) * FF_SHARD], w1_ref[j])
        xf = x_ref[...]
        r = lax.rsqrt(jnp.mean(xf * xf, axis=-1, keepdims=True) + EPS)
        xhat = xf * r
        h_ref[...] = (xhat * g_ref[...]).astype(BF16)
        dg_ref[...] += jnp.sum(dh * xhat, axis=0, keepdims=True)
        gdh = dh * g_ref[...]
        dx_ref[...] = dyf + r * (gdh - xhat * jnp.mean(xhat * gdh, axis=-1, keepdims=True))

    return pl.pallas_call(
        body,
        grid_spec=pltpu.PrefetchScalarGridSpec(
            num_scalar_prefetch=1,
            grid=(S // TM,),
            in_specs=[_rows(TM, D_MODEL), _whole(g), _rows(TM, D_MODEL), _rows(TM, 2 * D_FF),
                      _layer((4, D_MODEL, FF_SHARD)), _layer((D_FF, D_MODEL))],
            out_specs=[_rows(TM, D_MODEL), _rows(TM, 2 * D_FF), _rows(TM, D_FF), _rows(TM, D_MODEL), _rows(TM, D_MODEL),
                       pl.BlockSpec((1, D_MODEL), lambda i, l: (0, 0))],
        ),
        out_shape=[jax.ShapeDtypeStruct((S, D_MODEL), F32), jax.ShapeDtypeStruct((S, 2 * D_FF), BF16),
                   jax.ShapeDtypeStruct((S, D_FF), BF16), jax.ShapeDtypeStruct((S, D_MODEL), BF16),
                   jax.ShapeDtypeStruct((S, D_MODEL), BF16), jax.ShapeDtypeStruct((1, D_MODEL), F32)],
        compiler_params=_params("arbitrary"),
        name="ffn_bwd",
    )(_layer_index(l), x, g, dy, u, w1_all, w2_all)


def loss_head(y, target):
    S = y.shape[0]

    def body(y_ref, t_ref, loss_ref, dy_ref):
        @pl.when(pl.program_id(0) == 0)
        def _():
            loss_ref[0, 0] = 0.0

        err = y_ref[...] - t_ref[...]
        dy_ref[...] = err * (1.0 / D_MODEL)
        loss_ref[0, 0] += 0.5 * jnp.sum(jnp.mean(err * err, axis=-1))

    return pl.pallas_call(
        body,
        grid=(S // TM,),
        in_specs=[_rows(TM, D_MODEL), _rows(TM, D_MODEL)],
        out_specs=[pl.BlockSpec((1, 1), lambda i: (0, 0), memory_space=pltpu.SMEM), _rows(TM, D_MODEL)],
        out_shape=[jax.ShapeDtypeStruct((1, 1), F32), jax.ShapeDtypeStruct((S, D_MODEL), F32)],
        compiler_params=_params("arbitrary"),
        name="loss_head",
    )(y, target)


SPLIT = {
    "w_in": ((D_MODEL, D_IN), 1),
    "w_attn_out": ((D_ATTN, D_MODEL), 1),
    "w_conv_out": ((D_CONV, D_MODEL), 1),
    "pool_w": ((N_GROUPS, D_POOL // N_GROUPS, D_MODEL // N_GROUPS), 2),
    "w_o": ((D_MODEL, D_MODEL), 0),
    "w_ffn_in": ((D_MODEL, 2 * D_FF), 1),
    "w_ffn_out": ((D_FF, D_MODEL), 0),
}
SMALL = {"norm_mix_g": D_MODEL, "forget_b": N_HEADS, "q_norm_g": HEAD_DIM, "k_norm_g": HEAD_DIM,
         "pool_scale": D_MODEL, "norm_ffn_g": D_MODEL, "conv_w": CONV_K * D_CONV}
WEIGHTS = ["norm_mix_g", "w_in", "forget_b", "q_norm_g", "k_norm_g", "w_attn_out", "conv_w", "w_conv_out", "pool_w",
           "pool_scale", "w_o", "norm_ffn_g", "w_ffn_in", "w_ffn_out"]


MIX_W = ["w_in", "w_attn_out", "w_conv_out", "pool_w", "w_o"]
FFN_W = ["w_ffn_in", "w_ffn_out"]


def _row(a):
    return a.astype(F32).reshape(1, -1)


def mix_fwd(x, full, li, small):
    n = full["w_o"].shape[0]
    wa, wf, wb = w_in_prep(full["w_in"], li)
    wao, wco, wpool, convw = branch_w_prep(full["w_attn_out"], full["w_conv_out"], full["pool_w"], full["conv_w"], li)
    w = dict(wa=wa, wf=wf, wb=wb, wao=wao, wco=wco, wpool=wpool, convw=convw,
             w_o=full["w_o"].reshape(n, D_MODEL, D_MODEL), at=li,
             g1=_row(small["norm_mix_g"]), pscale=_row(small["pool_scale"]),
             gq=_row(jnp.tile(small["q_norm_g"], N_HEADS)), gk=_row(jnp.tile(small["k_norm_g"], N_HEADS)),
             fb=_row(jnp.pad(small["forget_b"], (0, D_F - N_HEADS))))
    h, pa, pf, pb = in_proj_fwd(x, w["g1"], wa, wf, wb)
    qx, kx, vx = qk_prep(pa, pf, w["gq"], w["gk"], w["fb"])
    ox, qxb = attn_fwd(qx, kx, vx)
    xm = mix_out_fwd(x, ox, pb, wao, convw, wco, wpool, w["pscale"], w["w_o"], li)
    return xm, w, dict(x=x, h=h, pa=pa, pf=pf, pb=pb, kx=kx, vx=vx, ox=ox, qxb=qxb)


def ffn_half_fwd(xm, full, li, g2):
    n = full["w_ffn_out"].shape[0]
    w = dict(w1=full["w_ffn_in"], w2=full["w_ffn_out"].reshape(n, D_FF, D_MODEL), at=li, g2=_row(g2))
    y, u = ffn_fwd(xm, w["g2"], w["w1"], w["w2"], li)
    return y, w, dict(xm=xm, u=u)


def ffn_half_bwd(dx, w, s, gi, big):
    n = big["w_ffn_out"].shape[0]
    big = dict(big)
    dxm, du, act, h2, dyb, dg2 = ffn_bwd(s["xm"], w["g2"], dx, s["u"], w["w1"], w["w2"], w["at"])
    big["w_ffn_out"] = wgrad_into(act, dyb, "wgrad_ffn_out", big["w_ffn_out"].reshape(n, D_FF, D_MODEL),
                                  gi).reshape(big["w_ffn_out"].shape)
    big["w_ffn_in"] = wgrad_into(h2, du, "wgrad_ffn_in", big["w_ffn_in"], gi)
    return dxm, big, dict(norm_ffn_g=dg2[0])


def mix_bwd_weights(dxm, w, s, gi, big):
    big = dict(big)
    dpb, dox, dwo, dwao, dwco, dwpool, dconvw, dpscale = mix_out_bwd(
        dxm, s["ox"], s["pb"], w["wao"], w["convw"], w["wco"], w["wpool"], w["pscale"], w["w_o"], w["at"])
    dqx, dkx, dvx = attn_bwd(s["qxb"], s["kx"], s["vx"], dox)
    dpa, dpf, dgq, dgk, dfb = attn_bwd_post(s["pa"], s["pf"], dqx, dkx, dvx, w["gq"], w["gk"], w["fb"])
    big["w_in"] = w_in_unprep(wgrad(s["h"], dpa, "wgrad_in_qkv"), wgrad(s["h"], dpf, "wgrad_in_f"),
                              wgrad(s["h"], dpb, "wgrad_in_b"), big["w_in"], gi)
    big["w_attn_out"], big["w_conv_out"], big["pool_w"], big["w_o"] = branch_g_place(
        dwao, dwco, dwpool, dwo, (big["w_attn_out"], big["w_conv_out"], big["pool_w"], big["w_o"]), gi)
    sm = dict(forget_b=dfb[0, 0:N_HEADS], q_norm_g=dgq.reshape(N_HEADS, HEAD_DIM).sum(0),
              k_norm_g=dgk.reshape(N_HEADS, HEAD_DIM).sum(0), pool_scale=dpscale[0],
              conv_w=dconvw[0:CONV_K].reshape(-1))
    return (dpa, dpf, dpb), big, sm


def mix_bwd_input(dproj, dxm, w, s):
    dx, dg1 = in_proj_bwd(s["x"], w["g1"], dxm, *dproj, w["wa"], w["wf"], w["wb"])
    return dx, dict(norm_mix_g=dg1[0])


def mix_bwd(dxm, w, s, gi, big):
    dproj, big, sm = mix_bwd_weights(dxm, w, s, gi, big)
    dx, g1 = mix_bwd_input(dproj, dxm, w, s)
    return dx, big, {**sm, **g1}


def grad_buffers(full):
    return {n: lax.empty(full[n].shape, F32) for n in SPLIT}


def local_step(x, target, gathered, small):
    n_layers = small["norm_mix_g"].shape[0]
    done = []
    for l in range(n_layers):
        xm, wm, sm = mix_fwd(x, gathered, l, {n: v[l] for n, v in small.items()})
        x, wf, sf = ffn_half_fwd(xm, gathered, l, small["norm_ffn_g"][l])
        done.append((wm, sm, wf, sf))
    loss, dx = loss_head(x, target)
    big = grad_buffers(gathered)
    small_grads = [None] * n_layers
    for l in reversed(range(n_layers)):
        wm, sm, wf, sf = done[l]
        dxm, ffn_big, g_ffn = ffn_half_bwd(dx, wf, sf, l, {n: big[n] for n in FFN_W})
        dx, mix_big, g_mix = mix_bwd(dxm, wm, sm, l, {n: big[n] for n in MIX_W})
        big = {**ffn_big, **mix_big}
        small_grads[l] = {**g_ffn, **g_mix}
    return loss, dx, big, {n: jnp.stack([g[n] for g in small_grads]) for n in SMALL}


def adamw(w, g, m, v, name):
    R, C = w.shape
    tm = 256 if R % 256 == 0 else R

    def body(w_ref, g_ref, m_ref, v_ref, d_ref, nm_ref, nv_ref):
        gr = g_ref[...]
        m_new = ADAM_B1 * m_ref[...] + (1.0 - ADAM_B1) * gr
        v_new = ADAM_B2 * v_ref[...] + (1.0 - ADAM_B2) * jnp.square(gr)
        nm_ref[...] = m_new
        nv_ref[...] = v_new
        m_hat = m_new / (1.0 - ADAM_B1 ** ADAM_STEP)
        v_hat = v_new / (1.0 - ADAM_B2 ** ADAM_STEP)
        d_ref[...] = -ADAM_LR * (m_hat / (jnp.sqrt(v_hat) + ADAM_EPS) + ADAM_WD * w_ref[...])

    spec = _rows(tm, C)
    out = jax.ShapeDtypeStruct((R, C), F32)
    return pl.pallas_call(
        body,
        grid=(R // tm,),
        in_specs=[spec] * 4,
        out_specs=[spec] * 3,
        out_shape=[out] * 3,
        compiler_params=_params("parallel"),
        name=name,
    )(w, g, m, v)


MESH = pl.DeviceIdType.MESH
HBM_REF = pl.BlockSpec(memory_space=pl.ANY)
N_CHIPS = 4
N_DEV = 8
SMALL_SHAPE = (128, LANES)


def _mesh_pos():
    return lax.axis_index("x"), lax.axis_index("y"), lax.axis_index("c")


def _other_chips(x, y):
    return [(1 - x, y), (x, 1 - y), (1 - x, 1 - y)]


def _remote(src, dst, send_sem, recv_sem, to):
    return pltpu.make_async_remote_copy(src_ref=src, dst_ref=dst, send_sem=send_sem, recv_sem=recv_sem,
                                        device_id=to, device_id_type=MESH)


def _row_tile(rows):
    for tm in (256, 176, 128):
        if rows % tm == 0:
            return tm
    return rows


def _as4(a):
    return a.reshape(a.shape[0], a.shape[1], -1, a.shape[-1])


def _core_rows(buf, c):
    R, C = buf.shape[-2:]
    if R % 2:
        whole = (pl.ds(0, R), pl.ds(0, C))
        return whole, whole, False
    return (pl.ds(c * (R // 2), R // 2), pl.ds(0, C)), (pl.ds((1 - c) * (R // 2), R // 2), pl.ds(0, C)), True


def _half_shape(g):
    return g.shape[:-2] + (g.shape[-2] // 2, g.shape[-1])


def place_shard(chip, w, lo, hi, dtype, name, after=None):
    w3 = w.reshape(w.shape[0], -1, w.shape[-1])
    _, R, C = w3.shape
    tm = _row_tile(R)
    idle = [] if after is None else [after]

    def body(chip_ref, w_ref, *rest):
        rest[-1][...] = w_ref[...].astype(dtype)

    return pl.pallas_call(
        body,
        grid_spec=pltpu.PrefetchScalarGridSpec(
            num_scalar_prefetch=1,
            grid=(hi - lo, R // tm),
            in_specs=[pl.BlockSpec((None, tm, C), lambda l, i, chip: (lo + l, i, 0))] + [_whole(a) for a in idle],
            out_specs=pl.BlockSpec((None, None, tm, C), lambda l, i, chip: (l, chip[0], i, 0)),
        ),
        out_shape=jax.ShapeDtypeStruct((hi - lo, N_CHIPS, R, C), dtype),
        compiler_params=_params("parallel", "parallel"),
        name=name,
    )(chip, w3, *idle)


HBM_SPACE = pl.BlockSpec(memory_space=pltpu.HBM)
SEM_SPACE = pl.BlockSpec(memory_space=pltpu.SEMAPHORE)
IN_FLIGHT = pltpu.SideEffectType.DATAFLOW_SIDE_EFFECTING


def _sem_table(send_sems, recv_sems):
    return lambda t, j: (send_sems.at[t, j], recv_sems.at[t, j])


def _sem_per_peer(send_sems, recv_sems):
    return lambda t, j: (send_sems[j], recv_sems[j])


def _gather_ici(bufs, sem, sends=True, lands=True):
    x, y, c = _mesh_pos()
    me = 2 * x + y
    out, into = [], []
    for t, buf in enumerate(bufs):
        mine, _, _ = _core_rows(buf, c)
        part = lambda k, buf=buf, mine=mine: buf.at[pl.ds(0, buf.shape[0]), k, *mine]
        for j, (px, py) in enumerate(_other_chips(x, y)):
            if sends:
                out.append(_remote(part(me), part(me), *sem(t, j), (px, py, c)))
            if lands:
                into.append(_remote(part(2 * px + py), part(2 * px + py), *sem(t, j), (px, py, c)))
    return out, into


def _gather_d2d(bufs, send_sems, recv_sems, first):
    x, y, c = _mesh_pos()
    sends, lands = [], []
    for t, buf in enumerate(bufs):
        mine, theirs, split = _core_rows(buf, c)
        if not split:
            continue
        for j, (px, py) in enumerate(_other_chips(x, y)):
            part = lambda half, buf=buf, k=2 * px + py: buf.at[pl.ds(0, buf.shape[0]), k, *half]
            sems = (send_sems.at[t, first + j], recv_sems.at[t, first + j], (x, y, 1 - c))
            sends.append(_remote(part(mine), part(mine), *sems))
            lands.append(_remote(part(theirs), part(theirs), *sems))
    return sends, lands


def gather_start(bufs):
    n = len(bufs)

    def body(*refs):
        send_sems, recv_sems = refs[n:n + 3], refs[n + 3:n + 6]
        outs = refs[n + 6:2 * n + 6]
        token = refs[2 * n + 6]
        for cp in _gather_ici(outs, _sem_per_peer(send_sems, recv_sems), lands=False)[0]:
            cp.start()
        token[...] = jnp.zeros_like(token)

    res = pl.pallas_call(
        body,
        in_specs=[HBM_SPACE] * n,
        out_specs=[SEM_SPACE] * 6 + [HBM_SPACE] * n + [pl.BlockSpec(memory_space=pltpu.VMEM)],
        out_shape=[pltpu.SemaphoreType.DMA(())] * 6
        + [pltpu.HBM(b.shape, b.dtype) for b in bufs] + [jax.ShapeDtypeStruct((8, LANES), F32)],
        input_output_aliases={t: t + 6 for t in range(n)},
        compiler_params=pltpu.CompilerParams(has_side_effects=IN_FLIGHT),
        name="gather_start",
    )(*[pltpu.with_memory_space_constraint(b, pltpu.HBM) for b in bufs])
    return list(res[0:3]), list(res[3:6]), list(res[6:n + 6]), res[n + 6]


def gather_wait(send_sems, recv_sems, bufs, after):
    n = len(bufs)

    def body(*refs):
        ins_send, ins_recv = refs[n:n + 3], refs[n + 3:n + 6]
        outs = refs[n + 7:]
        sends, lands = _gather_ici(outs, _sem_per_peer(ins_send, ins_recv))
        for cp in lands:
            cp.wait_recv()
        for cp in sends:
            cp.wait_send()

    return list(pl.pallas_call(
        body,
        in_specs=[HBM_SPACE] * n + [SEM_SPACE] * 6 + [HBM_REF],
        out_specs=[HBM_SPACE] * n,
        out_shape=[pltpu.HBM(b.shape, b.dtype) for b in bufs],
        input_output_aliases={t: t for t in range(n)},
        compiler_params=pltpu.CompilerParams(has_side_effects=IN_FLIGHT),
        name="gather_wait",
    )(*bufs, *send_sems, *recv_sems, after))


def gather_forward(bufs):
    n = len(bufs)

    def body(*refs):
        outs = refs[n:2 * n]
        send_sems, recv_sems = refs[2 * n:]
        sends, lands = _gather_d2d(outs, send_sems, recv_sems, 0)
        for cp in sends:
            cp.start()
        for cp in lands:
            cp.wait_recv()
        for cp in sends:
            cp.wait_send()

    return list(pl.pallas_call(
        body,
        in_specs=[HBM_REF] * n,
        out_specs=[HBM_REF] * n,
        out_shape=[jax.ShapeDtypeStruct(b.shape, b.dtype) for b in bufs],
        input_output_aliases={t: t for t in range(n)},
        scratch_shapes=[pltpu.SemaphoreType.DMA((n, 3)), pltpu.SemaphoreType.DMA((n, 3))],
        name="gather_forward",
    )(*bufs))


def pair_exchange(grads):
    n = len(grads)

    def body(*refs):
        ins, outs = refs[:n], refs[n:2 * n]
        send_sems, recv_sems = refs[2 * n:]
        cps = _pair_copies(ins, outs, lambda t: (send_sems.at[t], recv_sems.at[t]))
        for cp in cps:
            cp.start()
        for cp in cps:
            cp.wait()

    return list(pl.pallas_call(
        body,
        in_specs=[HBM_REF] * n,
        out_specs=[HBM_REF] * n,
        out_shape=[jax.ShapeDtypeStruct(_half_shape(g), g.dtype) for g in grads],
        scratch_shapes=[pltpu.SemaphoreType.DMA((n,)), pltpu.SemaphoreType.DMA((n,))],
        name="pair_exchange",
    )(*grads))


def _pair_copies(grads, lands, sem):
    x, y, c = _mesh_pos()
    cps = []
    for t in range(len(grads)):
        _, theirs, _ = _core_rows(grads[t], c)
        src = grads[t].at[pl.ds(0, grads[t].shape[0]), pl.ds(0, N_CHIPS), *theirs]
        cps.append(_remote(src, lands[t], *sem(t), (x, y, 1 - c)))
    return cps


def pair_exchange_start(grads):
    n = len(grads)
    lands = [lax.empty(_half_shape(g), g.dtype) for g in grads]

    def body(*refs):
        send_sem, recv_sem = refs[2 * n:2 * n + 2]
        outs = refs[2 * n + 2:4 * n + 2]
        token = refs[4 * n + 2]
        for cp in _pair_copies(outs[:n], outs[n:], lambda t: (send_sem, recv_sem)):
            cp.start()
        token[...] = jnp.zeros_like(token)

    res = pl.pallas_call(
        body,
        in_specs=[HBM_SPACE] * (2 * n),
        out_specs=[SEM_SPACE] * 2 + [HBM_SPACE] * (2 * n) + [pl.BlockSpec(memory_space=pltpu.VMEM)],
        out_shape=[pltpu.SemaphoreType.DMA(())] * 2 + [pltpu.HBM(a.shape, a.dtype) for a in list(grads) + lands]
        + [jax.ShapeDtypeStruct((8, LANES), F32)],
        input_output_aliases={t: t + 2 for t in range(2 * n)},
        compiler_params=pltpu.CompilerParams(has_side_effects=IN_FLIGHT),
        name="pair_exchange_start",
    )(*[pltpu.with_memory_space_constraint(a, pltpu.HBM) for a in list(grads) + lands])
    return res[0], res[1], list(res[2:n + 2]), list(res[n + 2:2 * n + 2]), res[2 * n + 2]


def pair_exchange_wait(send_sem, recv_sem, grads, lands, after):
    n = len(grads)

    def body(*refs):
        in_send, in_recv = refs[2 * n:2 * n + 2]
        outs = refs[2 * n + 3:]
        for cp in _pair_copies(outs[:n], outs[n:], lambda t: (in_send, in_recv)):
            cp.wait()

    res = pl.pallas_call(
        body,
        in_specs=[HBM_SPACE] * (2 * n) + [SEM_SPACE] * 2 + [HBM_REF],
        out_specs=[HBM_SPACE] * (2 * n),
        out_shape=[pltpu.HBM(a.shape, a.dtype) for a in list(grads) + list(lands)],
        input_output_aliases={t: t for t in range(2 * n)},
        compiler_params=pltpu.CompilerParams(has_side_effects=IN_FLIGHT),
        name="pair_exchange_wait",
    )(*grads, *lands, send_sem, recv_sem, after)
    return list(res[:n]), list(res[n:])


def pair_sum(core, g, t, name):
    n, _, hr, hc = t.shape
    R, C = g.shape[2:]
    tm = _row_tile(hr)
    per = hr // tm

    def body(c_ref, g_ref, t_ref, o_ref):
        o_ref[...] = (g_ref[...] + t_ref[...]).astype(BF16)

    tile = pl.BlockSpec((None, tm, hc), lambda a, i, c: (a, i, 0))
    out = pl.pallas_call(
        body,
        grid_spec=pltpu.PrefetchScalarGridSpec(
            num_scalar_prefetch=1,
            grid=(n * N_CHIPS, per),
            in_specs=[pl.BlockSpec((None, tm, hc), lambda a, i, c: (a, per * c[0] + i, 0)), tile],
            out_specs=tile,
        ),
        out_shape=jax.ShapeDtypeStruct((n * N_CHIPS, hr, hc), BF16),
        compiler_params=_params("parallel", "parallel"),
        name=name,
    )(core, g.reshape(n * N_CHIPS, R, C), t.reshape(n * N_CHIPS, hr, hc))
    return out.reshape(t.shape)


def _chip_copies(sums, recv, sem):
    x, y, c = _mesh_pos()
    cps = []
    for t in range(len(sums)):
        for j, (px, py) in enumerate(_other_chips(x, y)):
            src = sums[t].at[pl.ds(0, sums[t].shape[0]), 2 * px + py]
            cps.append(_remote(src, recv[t].at[j], *sem(t, j), (px, py, c)))
    return cps


def _recv_shape(s):
    return (N_CHIPS - 1, s.shape[0]) + s.shape[2:]


def chip_exchange(sums):
    n = len(sums)

    def body(*refs):
        ins, outs = refs[:n], refs[n:2 * n]
        cps = _chip_copies(ins, outs, _sem_table(*refs[2 * n:]))
        for cp in cps:
            cp.start()
        for cp in cps:
            cp.wait()

    return list(pl.pallas_call(
        body,
        in_specs=[HBM_REF] * n,
        out_specs=[HBM_REF] * n,
        out_shape=[jax.ShapeDtypeStruct(_recv_shape(s), s.dtype) for s in sums],
        scratch_shapes=[pltpu.SemaphoreType.DMA((n, 3)), pltpu.SemaphoreType.DMA((n, 3))],
        name="chip_exchange",
    )(*sums))


def chip_exchange_start(sums):
    n = len(sums)
    lands = [lax.empty(_recv_shape(s), s.dtype) for s in sums]

    def body(*refs):
        send_sems, recv_sems = refs[2 * n:2 * n + 3], refs[2 * n + 3:2 * n + 6]
        outs = refs[2 * n + 6:4 * n + 6]
        token = refs[4 * n + 6]
        for cp in _chip_copies(outs[:n], outs[n:2 * n], _sem_per_peer(send_sems, recv_sems)):
            cp.start()
        token[...] = jnp.zeros_like(token)

    res = pl.pallas_call(
        body,
        in_specs=[HBM_SPACE] * (2 * n),
        out_specs=[SEM_SPACE] * 6 + [HBM_SPACE] * (2 * n) + [pl.BlockSpec(memory_space=pltpu.VMEM)],
        out_shape=[pltpu.SemaphoreType.DMA(())] * 6 + [pltpu.HBM(a.shape, a.dtype) for a in list(sums) + lands]
        + [jax.ShapeDtypeStruct((8, LANES), F32)],
        input_output_aliases={t: t + 6 for t in range(2 * n)},
        compiler_params=pltpu.CompilerParams(has_side_effects=IN_FLIGHT),
        name="chip_exchange_start",
    )(*[pltpu.with_memory_space_constraint(a, pltpu.HBM) for a in list(sums) + lands])
    return list(res[0:3]), list(res[3:6]), list(res[6:n + 6]), list(res[n + 6:2 * n + 6]), res[2 * n + 6]


def chip_exchange_wait(send_sems, recv_sems, sums, lands, after):
    n = len(sums)

    def body(*refs):
        ins_send, ins_recv = refs[2 * n:2 * n + 3], refs[2 * n + 3:2 * n + 6]
        outs = refs[2 * n + 7:]
        for cp in _chip_copies(outs[:n], outs[n:], _sem_per_peer(ins_send, ins_recv)):
            cp.wait()

    res = pl.pallas_call(
        body,
        in_specs=[HBM_SPACE] * (2 * n) + [SEM_SPACE] * 6 + [HBM_REF],
        out_specs=[HBM_SPACE] * (2 * n),
        out_shape=[pltpu.HBM(a.shape, a.dtype) for a in list(sums) + list(lands)],
        input_output_aliases={t: t for t in range(2 * n)},
        compiler_params=pltpu.CompilerParams(has_side_effects=IN_FLIGHT),
        name="chip_exchange_wait",
    )(*sums, *lands, *send_sems, *recv_sems, after)
    return list(res[:n]), list(res[n:])


def chip_sum_into(core, chip, recv, sums, total, lo, name):
    _, n, hr, hc = recv.shape
    tm = _row_tile(hr)
    per = hr // tm

    def body(c_ref, k_ref, r_ref, s_ref, t_ref, o_ref):
        acc = s_ref[...].astype(F32)
        for j in range(N_CHIPS - 1):
            acc = acc + r_ref[j].astype(F32)
        o_ref[...] = acc

    return pl.pallas_call(
        body,
        grid_spec=pltpu.PrefetchScalarGridSpec(
            num_scalar_prefetch=2,
            grid=(n, per),
            in_specs=[pl.BlockSpec((N_CHIPS - 1, None, tm, hc), lambda a, i, c, k: (0, a, i, 0)),
                      pl.BlockSpec((None, None, tm, hc), lambda a, i, c, k: (a, k[0], i, 0)),
                      HBM_REF],
            out_specs=pl.BlockSpec((None, tm, hc), lambda a, i, c, k: (lo + a, per * c[0] + i, 0)),
        ),
        out_shape=jax.ShapeDtypeStruct(total.shape, F32),
        input_output_aliases={4: 0},
        compiler_params=_params("parallel", "parallel"),
        name=name,
    )(core, chip, recv, sums, total)


def sibling_share(totals):
    n = len(totals)

    def body(*refs):
        outs = refs[n:2 * n]
        send_sems, recv_sems = refs[2 * n:]
        x, y, c = _mesh_pos()
        half = lambda t, which: outs[t].at[pl.ds(0, DEPTH), *_core_rows(outs[t], c)[which]]
        sent = [_remote(half(t, 0), half(t, 0), send_sems.at[t], recv_sems.at[t], (x, y, 1 - c)) for t in range(n)]
        for cp in sent:
            cp.start()
        for t in range(n):
            _remote(half(t, 1), half(t, 1), send_sems.at[t], recv_sems.at[t], (x, y, 1 - c)).wait_recv()
        for cp in sent:
            cp.wait_send()

    return pl.pallas_call(
        body,
        in_specs=[HBM_REF] * n,
        out_specs=[HBM_REF] * n,
        out_shape=[jax.ShapeDtypeStruct(t.shape, t.dtype) for t in totals],
        input_output_aliases={t: t for t in range(n)},
        scratch_shapes=[pltpu.SemaphoreType.DMA((n,)), pltpu.SemaphoreType.DMA((n,))],
        name="sibling_share",
    )(*totals)


def small_allgather(small):
    def body(s_ref, a_ref, send_sems, recv_sems, local_sem):
        x, y, c = _mesh_pos()
        me = 4 * x + 2 * y + c
        own = pltpu.make_async_copy(s_ref, a_ref.at[me], local_sem)
        own.start()
        sent = []
        for k in range(1, N_DEV):
            peer = (x ^ (k >> 2), y ^ ((k >> 1) & 1), c ^ (k & 1))
            cp = _remote(s_ref, a_ref.at[me], send_sems.at[k - 1], recv_sems.at[k - 1], peer)
            cp.start()
            sent.append(cp)
        for k in range(1, N_DEV):
            px, py, pc = x ^ (k >> 2), y ^ ((k >> 1) & 1), c ^ (k & 1)
            _remote(s_ref, a_ref.at[4 * px + 2 * py + pc], send_sems.at[k - 1], recv_sems.at[k - 1], (px, py, pc)).wait_recv()
        for cp in sent:
            cp.wait_send()
        own.wait()

    return pl.pallas_call(
        body,
        in_specs=[HBM_REF],
        out_specs=HBM_REF,
        out_shape=jax.ShapeDtypeStruct((N_DEV,) + SMALL_SHAPE, small.dtype),
        scratch_shapes=[pltpu.SemaphoreType.DMA((N_DEV - 1,)), pltpu.SemaphoreType.DMA((N_DEV - 1,)), pltpu.SemaphoreType.DMA],
        name="small_allgather",
    )(small)


def small_sum(blocks):
    def body(a_ref, o_ref):
        acc = a_ref[0]
        for d in range(1, N_DEV):
            acc = acc + a_ref[d]
        o_ref[...] = acc

    return pl.pallas_call(
        body,
        in_specs=[pl.BlockSpec(memory_space=pltpu.VMEM)],
        out_specs=pl.BlockSpec(memory_space=pltpu.VMEM),
        out_shape=jax.ShapeDtypeStruct(SMALL_SHAPE, F32),
        name="small_sum",
    )(blocks)


def pack_small(grads, loss):
    flat = jnp.concatenate([grads[n].reshape(-1) for n in SMALL] + [loss.reshape(-1)])
    size = SMALL_SHAPE[0] * SMALL_SHAPE[1]
    return jnp.pad(flat, (0, size - flat.shape[0])).reshape(SMALL_SHAPE)


def unpack_small(packed):
    flat = packed.reshape(-1)
    out, off = {}, 0
    for n, size in SMALL.items():
        out[n] = flat[off:off + DEPTH * size].reshape(DEPTH, size)
        off += DEPTH * size
    return out, flat[off]


def kernel(x, norm_mix_g, w_in, forget_b, q_norm_g, k_norm_g, w_attn_out, conv_w, w_conv_out, pool_w, pool_scale, w_o, norm_ffn_g, w_ffn_in, w_ffn_out, loss_target, m_norm_mix_g, m_w_in, m_forget_b, m_q_norm_g, m_k_norm_g, m_w_attn_out, m_conv_w, m_w_conv_out, m_pool_w, m_pool_scale, m_w_o, m_norm_ffn_g, m_w_ffn_in, m_w_ffn_out, v_norm_mix_g, v_w_in, v_forget_b, v_q_norm_g, v_k_norm_g, v_w_attn_out, v_conv_w, v_w_conv_out, v_pool_w, v_pool_scale, v_w_o, v_norm_ffn_g, v_w_ffn_in, v_w_ffn_out):
    given = dict(locals())
    weights = {n: given[n] for n in WEIGHTS}

    core = lax.axis_index("c").astype(jnp.int32)
    chip = (2 * lax.axis_index("x") + lax.axis_index("y")).astype(jnp.int32)

    core, chip = core.reshape(1), chip.reshape(1)
    mix_names, all_names = MIX_W + ["conv_w"], MIX_W + FFN_W + ["conv_w"]

    def placed(lo, hi, names, after=None):
        piece = None
        if after is not None:
            piece = _as4(next(iter(after.values())))[0, 0, 0:HALO, 0:LANES] if isinstance(after, dict) else after
        return [place_shard(chip, weights[n], lo, hi, F32 if n == "conv_w" else BF16, "place_" + n, piece)
                for n in names]

    def as_weights(bufs, names):
        return {n: b.reshape(b.shape[:2] + weights[n].shape[1:]) for n, b in zip(names, bufs)}

    def landed(start, names, after):
        return as_weights(gather_forward(gather_wait(*start[:3], after)), names)

    def layer_small(l, *tokens):
        small = {n: weights[n][l] for n in SMALL if n != "conv_w"}
        for t in tokens:
            small["norm_mix_g"] = small["norm_mix_g"] + t[0, 0]
        return small

    done = [None] * DEPTH
    first = gather_start(placed(0, 1, mix_names))
    next_bufs = (placed(0, 1, FFN_W, after=first[3]), placed(1, 2, all_names, after=first[3]))
    first = landed(first, mix_names, next_bufs[1][0])
    (ffn0, layer1), _ = lax.optimization_barrier((next_bufs, first["conv_w"]))
    ffn0, layer1 = gather_start(ffn0), gather_start(layer1)
    xm, wm, sm = mix_fwd(x[0], first, 0, layer_small(0, ffn0[3], layer1[3]))
    xs, wf, sf = ffn_half_fwd(xm, landed(ffn0, FFN_W, xm), 0, weights["norm_ffn_g"][0])
    done[0] = (wm, sm, wf, sf)
    full = landed(layer1, all_names, xs)
    rest = gather_start(placed(2, DEPTH, all_names, after=full))
    xm, wm, sm = mix_fwd(xs, full, 0, layer_small(1, rest[3]))
    xs, wf, sf = ffn_half_fwd(xm, full, 0, weights["norm_ffn_g"][1])
    done[1] = (wm, sm, wf, sf)
    full = landed(rest, all_names, xs)
    for l in range(2, DEPTH):
        xm, wm, sm = mix_fwd(xs, full, l - 2, layer_small(l))
        xs, wf, sf = ffn_half_fwd(xm, full, l - 2, weights["norm_ffn_g"][l])
        done[l] = (wm, sm, wf, sf)
    loss, dx = loss_head(xs, loss_target[0])

    small_grads = [None] * DEPTH
    totals = {n: lax.empty((DEPTH,) + _as4(weights[n][None]).shape[2:], F32) for n in SPLIT}

    def buffers(names, n_layers):
        return {n: lax.empty((n_layers, N_CHIPS) + weights[n].shape[1:], F32) for n in names}

    def pair_sums(names, grads, theirs):
        return [pair_sum(core, a, t, "pair_sum_" + n) for n, a, t in zip(names, grads, theirs)]

    def add_chips(names, recv, sums, lo):
        for n, r, s in zip(names, recv, sums):
            totals[n] = chip_sum_into(core, chip, r, s, totals[n], lo, "chip_sum_" + n)

    big = buffers(SPLIT, DEPTH - 1)
    for l in reversed(range(1, DEPTH)):
        wm, sm, wf, sf = done[l]
        dxm, ffn_big, g_ffn = ffn_half_bwd(dx, wf, sf, l - 1, {n: big[n] for n in FFN_W})
        dx, mix_big, g_mix = mix_bwd(dxm, wm, sm, l - 1, {n: big[n] for n in MIX_W})
        big = {**ffn_big, **mix_big}
        small_grads[l] = {**g_ffn, **g_mix}
    pair = pair_exchange_start([_as4(big[n]) for n in SPLIT])

    wm, sm, wf, sf = done[0]
    wf = dict(wf, g2=wf["g2"] + pair[4][0:1, 0:1])
    dxm, ffn_big, g_ffn = ffn_half_bwd(dx, wf, sf, 0, buffers(FFN_W, 1))
    chips = chip_exchange_start(pair_sums(SPLIT, *pair_exchange_wait(*pair[:4], ffn_big["w_ffn_in"])))
    ffn_grads = [_as4(ffn_big[n]) for n in FFN_W]
    ffn_chips = chip_exchange_start(pair_sums(FFN_W, ffn_grads, pair_exchange(ffn_grads)))
    wm = dict(wm, pscale=wm["pscale"] + chips[4][0:1, 0:1] + ffn_chips[4][0:1, 0:1])
    dproj, mix_big, g_mix = mix_bwd_weights(dxm, wm, sm, 0, buffers(MIX_W, 1))
    mix_grads = [_as4(mix_big[n]) for n in MIX_W]
    mix_chips = chip_exchange_start(pair_sums(MIX_W, mix_grads, pair_exchange(mix_grads)))
    dx, g_in = mix_bwd_input(dproj, dxm, dict(wm, g1=wm["g1"] + mix_chips[4][0:1, 0:1]), sm)
    small_grads[0] = {**g_ffn, **g_mix, **g_in}

    reduced, deltas, new_m, new_v = {}, {}, {}, {}

    def update(names):
        for n in names:
            w = weights[n]
            flat = (-1, w.shape[-1])
            d, nm, nv = adamw(w.reshape(flat), reduced[n].reshape(flat), given["m_" + n].reshape(flat),
                              given["v_" + n].reshape(flat), "adamw_" + n)
            deltas[n], new_m[n], new_v[n] = d.reshape(w.shape), nm.reshape(w.shape), nv.reshape(w.shape)

    def share(names):
        for n, t in zip(names, sibling_share([totals[n] for n in names])):
            reduced[n] = t.reshape(weights[n].shape)

    sums, recv = chip_exchange_wait(*chips[:4], dx)
    add_chips(SPLIT, recv, sums, 1)
    sums, recv = chip_exchange_wait(*ffn_chips[:4], dx)
    add_chips(FFN_W, recv, sums, 0)
    share(FFN_W)
    update(FFN_W)
    sums, recv = chip_exchange_wait(*mix_chips[:4], deltas["w_ffn_in"])
    add_chips(MIX_W, recv, sums, 0)
    share(MIX_W)

    small_grads = {n: jnp.stack([g[n] for g in small_grads]) for n in SMALL}
    small_total, loss_sum = unpack_small(small_sum(small_allgather(pack_small(small_grads, loss))))
    chip = chip[0]
    cols = D_CONV // N_CHIPS
    small_total["conv_w"] = lax.dynamic_slice_in_dim(small_total["conv_w"].reshape(DEPTH, CONV_K, D_CONV), chip * cols, cols, axis=2)
    for n in SMALL:
        reduced[n] = small_total[n].reshape(weights[n].shape)
    update(MIX_W + list(SMALL))

    return (loss_sum, dx[None], *[reduced[n] for n in WEIGHTS], *[deltas[n] for n in WEIGHTS],
            *[new_m[n] for n in WEIGHTS], *[new_v[n] for n in WEIGHTS])
```

```python
import functools

import numpy as np
import jax
import jax.numpy as jnp
from jax import lax
from jax.experimental import pallas as pl
from jax.experimental.pallas import tpu as pltpu

F32 = jnp.float32
BF16 = jnp.bfloat16

D_MODEL = 1024
DEPTH = 4
HEAD_DIM = 64
N_HEADS = 8
D_ATTN = 512
D_CONV = 256
D_POOL = 256
D_FF = 2816
D_IN = 5640
CONV_K = 3
POOL_WINDOWS = (2, 4, 8, 16)
N_GROUPS = len(POOL_WINDOWS)
EPS = 1e-6
ADAM_LR, ADAM_B1, ADAM_B2, ADAM_EPS, ADAM_WD, ADAM_STEP = 0.001, 0.9, 0.999, 1e-08, 0.01, 10

D_QKV = 3 * D_ATTN
D_F = 128
D_B = 3 * D_CONV + D_POOL + 3 * D_MODEL
D_LOCAL = 3 * D_CONV + D_POOL

LANES = 128
D_HEADS = N_HEADS * LANES
HALO = 16
VMEM_LIMIT = 56 * 1024 * 1024
NEG = -1e30
LOG2E = 1.4426950408889634
LN2 = 0.6931471805599453

TM = 256
TM_MIX = 256
TM_MIX_FWD = 512
TQ = 1024

LANE_C = 64
LANE_ONE = 67
LANE_LSE = 70
N_PIECES = 3


def _dot(a, b):
    return jnp.dot(a, b, preferred_element_type=F32)


def _dot_nt(a, b):
    return lax.dot_general(a, b, (((1,), (1,)), ((), ())), preferred_element_type=F32)


def _dot_tn(a, b):
    return lax.dot_general(a, b, (((0,), (0,)), ((), ())), preferred_element_type=F32)


def _params(*sem):
    return pltpu.CompilerParams(dimension_semantics=sem, vmem_limit_bytes=VMEM_LIMIT)


def _rows(tm, n):
    return pl.BlockSpec((tm, n), lambda i, *_: (i, 0))


def _whole(a):
    nd = a.ndim
    return pl.BlockSpec(a.shape, lambda *_: (0,) * nd)


def _layer(shape):
    nd = len(shape)
    return pl.BlockSpec((None,) + tuple(shape), lambda *a: (a[-1][0],) + (0,) * nd)


def _layer_index(l):
    return jnp.full((1,), l, jnp.int32)


def _split_bf16(x):
    hi = x.astype(BF16)
    lo = (x - hi.astype(F32)).astype(BF16)
    return hi, lo


def _pieces(x):
    p1 = x.astype(BF16)
    r1 = x - p1.astype(F32)
    p2 = r1.astype(BF16)
    p3 = (r1 - p2.astype(F32)).astype(BF16)
    return p1, p2, p3


def _sigmoid(x):
    return 1.0 / (1.0 + jnp.exp(-x))


def w_in_prep(win, l):
    tr = 256
    n = D_IN // 4
    v_rest = D_QKV - n
    b0 = v_rest + N_HEADS

    def body(s0, s1, s2, s3, wa_ref, wf_ref, wb_ref):
        b = s1[...]
        wa_ref[...] = jnp.concatenate([s0[...], b[:, 0:v_rest]], axis=1)
        wf_ref[...] = jnp.concatenate([b[:, v_rest:b0], jnp.zeros((tr, D_F - N_HEADS), b.dtype)], axis=1)
        wb_ref[...] = jnp.concatenate([b[:, b0:n], s2[...], s3[...]], axis=1)

    shard = lambda j: pl.BlockSpec((None, None, tr, n), lambda i: (l, j, i, 0))
    return pl.pallas_call(
        body,
        grid=(D_MODEL // tr,),
        in_specs=[shard(0), shard(1), shard(2), shard(3)],
        out_specs=[_rows(tr, D_QKV), _rows(tr, D_F), _rows(tr, D_B)],
        out_shape=[jax.ShapeDtypeStruct((D_MODEL, D_QKV), win.dtype), jax.ShapeDtypeStruct((D_MODEL, D_F), win.dtype),
                   jax.ShapeDtypeStruct((D_MODEL, D_B), win.dtype)],
        compiler_params=_params("parallel"),
        name="w_in_prep",
    )(win, win, win, win)


def _into_layer(buf):
    return dict(in_spec=HBM_REF, out_shape=jax.ShapeDtypeStruct(buf.shape, buf.dtype), aliases={1: 0})


def w_in_unprep(dwa, dwf, dwb, buf, l):
    tr = 256
    n = D_IN // 4
    v_rest = D_QKV - n
    b1 = n - v_rest - N_HEADS
    place = _into_layer(buf)

    def body(l_ref, buf_ref, a_ref, f_ref, b_ref, o_ref):
        a = a_ref[...]
        b = b_ref[...]
        o_ref[0] = a[:, 0:n]
        o_ref[1] = jnp.concatenate([a[:, n:D_QKV], f_ref[:, 0:N_HEADS], b[:, 0:b1]], axis=1)
        o_ref[2] = b[:, b1:b1 + n]
        o_ref[3] = b[:, b1 + n:D_B]

    return pl.pallas_call(
        body,
        grid_spec=pltpu.PrefetchScalarGridSpec(
            num_scalar_prefetch=1,
            grid=(D_MODEL // tr,),
            in_specs=[place["in_spec"], _rows(tr, D_QKV), _rows(tr, D_F), _rows(tr, D_B)],
            out_specs=pl.BlockSpec((None, 4, tr, n), lambda i, l: (l[0], 0, i, 0)),
        ),
        out_shape=place["out_shape"],
        input_output_aliases=place["aliases"],
        compiler_params=_params("parallel"),
        name="w_in_unprep",
    )(_layer_index(l), buf, dwa, dwf, dwb)


def branch_w_prep(wao, wco, pw, cw, l):
    gd = D_POOL // N_GROUPS
    od = D_MODEL // N_GROUPS

    def body(wao_ref, wco_ref, pw_ref, cw_ref, ao_ref, co_ref, po_ref, co8_ref):
        ao_ref[...] = jnp.concatenate([wao_ref[j] for j in range(4)], axis=1)
        co_ref[...] = jnp.concatenate([wco_ref[j] for j in range(4)], axis=1)
        zero = jnp.zeros((gd, od), co_ref.dtype)
        po_ref[...] = jnp.concatenate(
            [jnp.concatenate([jnp.concatenate([pw_ref[j, g] for j in range(4)], axis=1) if g2 == g else zero
                              for g2 in range(N_GROUPS)], axis=1) for g in range(N_GROUPS)], axis=0)
        co8_ref[...] = jnp.zeros_like(co8_ref)
        co8_ref[0:CONV_K, :] = jnp.concatenate([cw_ref[j] for j in range(4)], axis=1)

    sel = lambda *shape: pl.BlockSpec((None,) + shape, lambda i: (l,) + (0,) * len(shape))
    return pl.pallas_call(
        body,
        grid=(1,),
        in_specs=[sel(4, D_ATTN, D_MODEL // 4), sel(4, D_CONV, D_MODEL // 4), sel(4, N_GROUPS, gd, od // 4),
                  sel(4, CONV_K, D_CONV // 4)],
        out_specs=[pl.BlockSpec((D_ATTN, D_MODEL), lambda i: (0, 0)), pl.BlockSpec((D_CONV, D_MODEL), lambda i: (0, 0)),
                   pl.BlockSpec((D_POOL, D_MODEL), lambda i: (0, 0)), pl.BlockSpec((8, D_CONV), lambda i: (0, 0))],
        out_shape=[jax.ShapeDtypeStruct((D_ATTN, D_MODEL), BF16), jax.ShapeDtypeStruct((D_CONV, D_MODEL), BF16),
                   jax.ShapeDtypeStruct((D_POOL, D_MODEL), BF16), jax.ShapeDtypeStruct((8, D_CONV), F32)],
        name="branch_w_prep",
    )(wao, wco, pw, cw)


def branch_g_place(dwao, dwco, dwpool, dwo, bufs, l):
    gd = D_POOL // N_GROUPS
    od = D_MODEL // N_GROUPS
    q = D_MODEL // 4

    def body(l_ref, b0, b1, b2, b3, a_ref, c_ref, p_ref, w_ref, ao_ref, co_ref, po_ref, wo_ref):
        a = a_ref[...]
        c = c_ref[...]
        p = p_ref[...]
        for j in range(4):
            ao_ref[j] = a[:, j * q:(j + 1) * q]
            co_ref[j] = c[:, j * q:(j + 1) * q]
            wo_ref[j] = w_ref[j * q:(j + 1) * q, :]
            for g in range(N_GROUPS):
                c0 = g * od + j * (od // 4)
                po_ref[j, g] = p[g * gd:(g + 1) * gd, c0:c0 + od // 4]

    whole = lambda a: pl.BlockSpec(a.shape, lambda i, l: (0,) * a.ndim)
    layer = lambda b: pl.BlockSpec((None,) + b.shape[1:], lambda i, l: (l[0],) + (0,) * (b.ndim - 1))
    return pl.pallas_call(
        body,
        grid_spec=pltpu.PrefetchScalarGridSpec(
            num_scalar_prefetch=1,
            grid=(1,),
            in_specs=[HBM_REF] * 4 + [whole(dwao), whole(dwco), whole(dwpool), whole(dwo)],
            out_specs=[layer(b) for b in bufs],
        ),
        out_shape=[jax.ShapeDtypeStruct(b.shape, b.dtype) for b in bufs],
        input_output_aliases={1: 0, 2: 1, 3: 2, 4: 3},
        compiler_params=_params("arbitrary"),
        name="branch_g_place",
    )(_layer_index(l), *bufs, dwao, dwco, dwpool, dwo)


def in_proj_fwd(x, g, wa, wf, wb):
    S = x.shape[0]

    def body(x_ref, g_ref, wa_ref, wf_ref, wb_ref, h_ref, pa_ref, pf_ref, pb_ref):
        xf = x_ref[...]
        r = lax.rsqrt(jnp.mean(xf * xf, axis=-1, keepdims=True) + EPS)
        h = (xf * r * g_ref[...]).astype(BF16)
        h_ref[...] = h
        pa_ref[...] = _dot(h, wa_ref[...]).astype(BF16)
        pf_ref[...] = _dot(h, wf_ref[...])
        pb_ref[...] = _dot(h, wb_ref[...]).astype(BF16)

    return pl.pallas_call(
        body,
        grid=(S // TM,),
        in_specs=[_rows(TM, D_MODEL), _whole(g), _whole(wa), _whole(wf), _whole(wb)],
        out_specs=[_rows(TM, D_MODEL), _rows(TM, D_QKV), _rows(TM, D_F), _rows(TM, D_B)],
        out_shape=[
            jax.ShapeDtypeStruct((S, D_MODEL), BF16),
            jax.ShapeDtypeStruct((S, D_QKV), BF16),
            jax.ShapeDtypeStruct((S, D_F), F32),
            jax.ShapeDtypeStruct((S, D_B), BF16),
        ],
        compiler_params=_params("parallel"),
        name="in_proj_fwd",
    )(x, g, wa, wf, wb)


def in_proj_bwd(x, g, dxm, dpa, dpf, dpb, wa, wf, wb):
    S = x.shape[0]

    def body(x_ref, g_ref, dxm_ref, dpa_ref, dpf_ref, dpb_ref, wa_ref, wf_ref, wb_ref, dx_ref, dg_ref):
        @pl.when(pl.program_id(0) == 0)
        def _():
            dg_ref[...] = jnp.zeros_like(dg_ref)

        dh = _dot_nt(dpa_ref[...], wa_ref[...]) + _dot_nt(dpf_ref[...], wf_ref[...]) + _dot_nt(dpb_ref[...], wb_ref[...])
        xf = x_ref[...]
        r = lax.rsqrt(jnp.mean(xf * xf, axis=-1, keepdims=True) + EPS)
        xhat = xf * r
        dg_ref[...] += jnp.sum(dh * xhat, axis=0, keepdims=True)
        gdh = dh * g_ref[...]
        dx_ref[...] = dxm_ref[...] + r * (gdh - xhat * jnp.mean(xhat * gdh, axis=-1, keepdims=True))

    return pl.pallas_call(
        body,
        grid=(S // TM,),
        in_specs=[_rows(TM, D_MODEL), _whole(g), _rows(TM, D_MODEL), _rows(TM, D_QKV), _rows(TM, D_F), _rows(TM, D_B),
                  _whole(wa), _whole(wf), _whole(wb)],
        out_specs=[_rows(TM, D_MODEL), pl.BlockSpec((1, D_MODEL), lambda i: (0, 0))],
        out_shape=[jax.ShapeDtypeStruct((S, D_MODEL), F32), jax.ShapeDtypeStruct((1, D_MODEL), F32)],
        compiler_params=_params("arbitrary"),
        name="in_proj_bwd",
    )(x, g, dxm, dpa, dpf, dpb, wa, wf, wb)


def _wgrad_tokens(S, K):
    return min(S, 2048 if K <= 1024 else 1024)


def _wgrad_columns(n):
    return next(tn for tn in (1024, 768, 512, 256, 128) if n % tn == 0)


def wgrad(xa, dy, name):
    S, K = xa.shape
    N = dy.shape[1]
    ts = _wgrad_tokens(S, K)
    tn = _wgrad_columns(N)
    tk = _wgrad_columns(K)

    def body(x_ref, dy_ref, o_ref):
        @pl.when(pl.program_id(2) == 0)
        def _():
            o_ref[...] = jnp.zeros_like(o_ref)

        o_ref[...] += _dot_tn(x_ref[...], dy_ref[...])

    return pl.pallas_call(
        body,
        grid=(K // tk, N // tn, S // ts),
        in_specs=[pl.BlockSpec((ts, tk), lambda i, j, k: (k, i)), pl.BlockSpec((ts, tn), lambda i, j, k: (k, j))],
        out_specs=pl.BlockSpec((tk, tn), lambda i, j, k: (i, j)),
        out_shape=jax.ShapeDtypeStruct((K, N), F32),
        compiler_params=_params("parallel", "parallel", "arbitrary"),
        name=name,
    )(xa, dy)


def wgrad_into(xa, dy, name, buf, l):
    S, K = xa.shape
    N = dy.shape[1]
    ts = _wgrad_tokens(S, K)
    split = buf.ndim == 4
    tn = buf.shape[-1] if split else _wgrad_columns(N)
    place = _into_layer(buf)

    def body(l_ref, buf_ref, x_ref, dy_ref, o_ref):
        @pl.when(pl.program_id(1) == 0)
        def _():
            o_ref[...] = jnp.zeros_like(o_ref)

        o_ref[...] += _dot_tn(x_ref[...], dy_ref[...])

    if split:
        out_spec = pl.BlockSpec((None, None, K, tn), lambda j, k, l: (l[0], j, 0, 0))
    else:
        out_spec = pl.BlockSpec((None, K, tn), lambda j, k, l: (l[0], 0, j))
    return pl.pallas_call(
        body,
        grid_spec=pltpu.PrefetchScalarGridSpec(
            num_scalar_prefetch=1,
            grid=(N // tn, S // ts),
            in_specs=[place["in_spec"], pl.BlockSpec((ts, K), lambda j, k, l: (k, 0)),
                      pl.BlockSpec((ts, tn), lambda j, k, l: (k, j))],
            out_specs=out_spec,
        ),
        out_shape=place["out_shape"],
        input_output_aliases=place["aliases"],
        compiler_params=_params("parallel", "arbitrary"),
        name=name,
    )(_layer_index(l), buf, xa, dy)


def _head_mean_matrix():
    h = np.arange(D_ATTN) // HEAD_DIM
    return jnp.asarray((h[:, None] == h[None, :]).astype(np.float32) / HEAD_DIM, BF16)


def _place_matrix(lane0):
    m = np.zeros((N_PIECES * LANES, D_HEADS), np.float32)
    for i in range(N_PIECES):
        for h in range(N_HEADS):
            m[i * LANES + h, h * LANES + lane0 + i] = 1.0
    return jnp.asarray(m, BF16)


def _tri(n, upper):
    r = np.arange(n)
    m = (r[None, :] >= r[:, None]) if upper else (r[None, :] <= r[:, None])
    return jnp.asarray(m.astype(np.float32), BF16)


def _lanes_in(lane, lo, n):
    return (lane >= lo) & (lane < lo + n)


def qk_prep(pa, pf, gq, gk, fb):
    S = pa.shape[0]
    bd = _head_mean_matrix()
    tri = _tri(TM, upper=False)
    place_q = _place_matrix(LANE_C)

    def body(q_ref, k_ref, v_ref, pf_ref, gq_ref, gk_ref, fb_ref, bd_ref, tri_ref, pq_ref,
             qx_ref, kx_ref, vx_ref, carry):
        @pl.when(pl.program_id(0) == 0)
        def _():
            carry[...] = jnp.zeros_like(carry)

        def head_norm(x_ref, g_ref, scale):
            xf = x_ref[...].astype(F32)
            ms = _dot((xf * xf).astype(BF16), bd_ref[...])
            return xf * lax.rsqrt(ms + EPS) * g_ref[...] * scale

        qh = head_norm(q_ref, gq_ref, HEAD_DIM ** -0.5 * LOG2E)
        kh = head_norm(k_ref, gk_ref, 1.0)
        vf = v_ref[...].astype(F32)

        z = pf_ref[...] + fb_ref[...]
        logf = jnp.minimum(z, 0.0) - jnp.log(1.0 + jnp.exp(-jnp.abs(z)))
        hi, lo = _split_bf16(logf)
        c = _dot(tri_ref[...], hi) + _dot(tri_ref[...], lo) + carry[...]
        carry[...] += jnp.sum(hi.astype(F32) + lo.astype(F32), axis=0, keepdims=True)
        pieces = jnp.concatenate(_pieces(c * LOG2E), axis=1)
        cq = _dot(pieces, pq_ref[...])

        lane = lax.broadcasted_iota(jnp.int32, (TM, LANES), 1)
        low = lane < HEAD_DIM
        ones_q = _lanes_in(lane, LANE_ONE, N_PIECES).astype(F32)
        ones_k = (_lanes_in(lane, LANE_C, N_PIECES) | _lanes_in(lane, LANE_LSE, N_PIECES)).astype(F32)
        ones_v = _lanes_in(lane, LANE_C, N_PIECES + 1).astype(F32)
        for h in range(N_HEADS):
            blk = slice((h // 2) * LANES, (h // 2 + 1) * LANES)
            head = (lambda a: pltpu.roll(a[:, blk], HEAD_DIM, 1)) if h % 2 else (lambda a: a[:, blk])
            mine = slice(h * LANES, (h + 1) * LANES)
            qx_ref[h] = jnp.where(low, head(qh), cq[:, mine] + ones_q).astype(BF16)
            ck = pltpu.roll(cq[:, mine], LANE_ONE - LANE_C, 1)
            kx_ref[h] = jnp.where(low, head(kh), ones_k - ck).astype(BF16)
            vx_ref[h] = jnp.where(low, head(vf), ones_v).astype(BF16)

    heads = pl.BlockSpec((N_HEADS, TM, LANES), lambda i: (0, i, 0))
    out = jax.ShapeDtypeStruct((N_HEADS, S, LANES), BF16)
    return pl.pallas_call(
        body,
        grid=(S // TM,),
        in_specs=[pl.BlockSpec((TM, D_ATTN), lambda i: (i, 0)), pl.BlockSpec((TM, D_ATTN), lambda i: (i, 1)),
                  pl.BlockSpec((TM, D_ATTN), lambda i: (i, 2)),
                  _rows(TM, D_F), _whole(gq), _whole(gk), _whole(fb), _whole(bd), _whole(tri), _whole(place_q)],
        out_specs=[heads, heads, heads],
        out_shape=[out, out, out],
        scratch_shapes=[pltpu.VMEM((1, D_F), F32)],
        compiler_params=_params("arbitrary"),
        name="qk_prep",
    )(pa, pa, pa, pf, gq, gk, fb, bd, tri, place_q)


def attn_fwd(qx, kx, vx):
    S = qx.shape[1]
    nq = S // TQ

    def body(q_ref, k_ref, v_ref, o_ref, qb_ref):
        i = pl.program_id(1)
        lane = lax.broadcasted_iota(jnp.int32, (TQ, LANES), 1)
        q = [q_ref[0], q_ref[1]]

        def update(qh, m, acc, k, v, visible):
            z = _dot_nt(qh, k)
            if visible is not None:
                z = jnp.where(visible, z, NEG)
            m_new = jnp.maximum(m, jnp.max(z, axis=1, keepdims=True))
            pr = jnp.exp2(z - m_new)
            return m_new, jnp.exp2(m - m_new) * acc + _dot(pr.astype(BF16), v)

        def step(kt, carry, diagonal=False):
            ks = pl.multiple_of(kt * TQ, TQ)
            visible = None
            if diagonal:
                visible = lax.broadcasted_iota(jnp.int32, (TQ, TQ), 0) >= lax.broadcasted_iota(jnp.int32, (TQ, TQ), 1)
            return tuple(update(q[j], *carry[j], k_ref[j, pl.ds(ks, TQ), :], v_ref[j, pl.ds(ks, TQ), :], visible)
                         for j in range(2))

        init = tuple((jnp.full((TQ, 1), NEG, F32), jnp.zeros((TQ, LANES), F32)) for _ in range(2))
        carry = step(i, lax.fori_loop(0, i, step, init), diagonal=True)
        for j in range(2):
            m, acc = carry[j]
            l = jnp.sum(jnp.where(lane == LANE_C, acc, 0.0), axis=1, keepdims=True)
            o_ref[j] = acc / l
            n1, n2, n3 = _pieces(-(m + jnp.log(l) * LOG2E))
            qb_ref[j] = jnp.where(lane == LANE_LSE, n1,
                                  jnp.where(lane == LANE_LSE + 1, n2, jnp.where(lane == LANE_LSE + 2, n3, q[j])))

    pair_tile = pl.BlockSpec((2, TQ, LANES), lambda p, i: (p, i, 0))
    pair_all = pl.BlockSpec((2, S, LANES), lambda p, i: (p, 0, 0))
    return pl.pallas_call(
        body,
        grid=(N_HEADS // 2, nq),
        in_specs=[pair_tile, pair_all, pair_all],
        out_specs=[pair_tile, pair_tile],
        out_shape=[jax.ShapeDtypeStruct((N_HEADS, S, LANES), F32), jax.ShapeDtypeStruct((N_HEADS, S, LANES), BF16)],
        compiler_params=_params("parallel", "parallel"),
        name="attn_fwd",
    )(qx, kx, vx)


def attn_bwd(qxb, kx, vx, dox):
    S = qxb.shape[1]
    nq = S // TQ

    def body(q_ref, k_ref, v_ref, do_ref, dq_ref, dk_ref, dv_ref):
        kt = pl.program_id(1)

        @pl.when(kt == 0)
        def _():
            dq_ref[...] = jnp.zeros_like(dq_ref)

        k = [k_ref[0], k_ref[1]]
        v = [v_ref[0], v_ref[1]]

        def block(j, q0, nq_rows, k0, nk, diagonal):
            q = q_ref[j, pl.ds(q0, nq_rows), :]
            dout = do_ref[j, pl.ds(q0, nq_rows), :]
            kj, vj = k[j][k0:k0 + nk], v[j][k0:k0 + nk]
            z = _dot_nt(q, kj)
            if diagonal:
                visible = (lax.broadcasted_iota(jnp.int32, (nq_rows, nk), 0)
                           >= lax.broadcasted_iota(jnp.int32, (nq_rows, nk), 1))
                z = jnp.where(visible, z, NEG)
            pr = jnp.exp2(z)
            dv = _dot_tn(pr.astype(BF16), dout)
            dsb = (pr * _dot_nt(dout, vj)).astype(BF16)
            dq_ref[j, pl.ds(q0, nq_rows), :] += _dot(dsb, kj)
            return _dot_tn(dsb, q), dv

        def step(qi, carry):
            qs = pl.multiple_of(qi * TQ, TQ)
            new = []
            for j in range(2):
                dk, dv = block(j, qs, TQ, 0, TQ, False)
                new.append((carry[j][0] + dk, carry[j][1] + dv))
            return tuple(new)

        half = TQ // 2
        qs = pl.multiple_of(kt * TQ, TQ)
        carry = []
        for j in range(2):
            first = block(j, qs, TQ, 0, half, True)
            second = block(j, pl.multiple_of(qs + half, half), half, half, half, True)
            carry.append(tuple(jnp.concatenate(a, axis=0) for a in zip(first, second)))
        carry = lax.fori_loop(kt + 1, nq, step, tuple(carry))
        for j in range(2):
            dk_ref[j] = carry[j][0]
            dv_ref[j] = carry[j][1]

    pair_tile = pl.BlockSpec((2, TQ, LANES), lambda p, kt: (p, kt, 0))
    pair_all = pl.BlockSpec((2, S, LANES), lambda p, kt: (p, 0, 0))
    out = jax.ShapeDtypeStruct((N_HEADS, S, LANES), F32)
    return pl.pallas_call(
        body,
        grid=(N_HEADS // 2, nq),
        in_specs=[pair_all, pair_tile, pair_tile, pair_all],
        out_specs=[pair_all, pair_tile, pair_tile],
        out_shape=[out, out, out],
        compiler_params=_params("arbitrary", "arbitrary"),
        name="attn_bwd",
    )(qxb, kx, vx, dox)


def attn_bwd_post(pa, pf, dqx, dkx, dvx, gq, gk, fb):
    S = pa.shape[0]
    nt = S // TM
    bd = _head_mean_matrix()
    triu = _tri(TM, upper=True)
    rev = lambda i: nt - 1 - i

    def body(q_ref, k_ref, pf_ref, dqx_ref, dkx_ref, dvx_ref, gq_ref, gk_ref, fb_ref, bd_ref, triu_ref,
             dpa_ref, dpf_ref, dgq_ref, dgk_ref, dfb_ref, carry):
        @pl.when(pl.program_id(0) == 0)
        def _():
            carry[...] = jnp.zeros_like(carry)
            dgq_ref[...] = jnp.zeros_like(dgq_ref)
            dgk_ref[...] = jnp.zeros_like(dgk_ref)
            dfb_ref[...] = jnp.zeros_like(dfb_ref)

        lane = lax.broadcasted_iota(jnp.int32, (TM, LANES), 1)
        columns = _heads_as_columns

        def head_norm_bwd(x_ref, dy, g_ref, dg_ref):
            xf = x_ref[...].astype(F32)
            r = lax.rsqrt(_dot((xf * xf).astype(BF16), bd_ref[...]) + EPS)
            xhat = xf * r
            dg_ref[...] += jnp.sum(dy * xhat, axis=0, keepdims=True)
            gdy = dy * g_ref[...]
            return (r * (gdy - xhat * _dot((xhat * gdy).astype(BF16), bd_ref[...]))).astype(BF16)

        dpa_ref[:, 0:D_ATTN] = head_norm_bwd(q_ref, columns(dqx_ref) * HEAD_DIM ** -0.5, gq_ref, dgq_ref)
        dpa_ref[:, D_ATTN:2 * D_ATTN] = head_norm_bwd(k_ref, columns(dkx_ref) * LN2, gk_ref, dgk_ref)
        dpa_ref[:, 2 * D_ATTN:3 * D_ATTN] = columns(dvx_ref).astype(BF16)

        dc = jnp.zeros((TM, LANES), F32)
        for h in range(N_HEADS):
            both = jnp.where(lane == LANE_C, dqx_ref[h], 0.0) - jnp.where(lane == LANE_ONE, dkx_ref[h], 0.0)
            dc = jnp.where(lane == h, jnp.sum(both, axis=1, keepdims=True), dc)
        hi, lo = _split_bf16(dc)
        dlogf = _dot(triu_ref[...], hi) + _dot(triu_ref[...], lo) + carry[...]
        first = lax.broadcasted_iota(jnp.int32, (TM, D_F), 0) == 0
        carry[...] = jnp.sum(jnp.where(first, dlogf, 0.0), axis=0, keepdims=True)
        df = dlogf * _sigmoid(-(pf_ref[...] + fb_ref[...]))
        dfb_ref[...] += jnp.sum(df, axis=0, keepdims=True)
        dpf_ref[...] = df.astype(BF16)

    heads = pl.BlockSpec((N_HEADS, TM, LANES), lambda i: (0, rev(i), 0))
    return pl.pallas_call(
        body,
        grid=(nt,),
        in_specs=[pl.BlockSpec((TM, D_ATTN), lambda i: (rev(i), 0)), pl.BlockSpec((TM, D_ATTN), lambda i: (rev(i), 1)),
                  pl.BlockSpec((TM, D_F), lambda i: (rev(i), 0)), heads, heads, heads,
                  _whole(gq), _whole(gk), _whole(fb), _whole(bd), _whole(triu)],
        out_specs=[pl.BlockSpec((TM, D_QKV), lambda i: (rev(i), 0)), pl.BlockSpec((TM, D_F), lambda i: (rev(i), 0)),
                   pl.BlockSpec((1, D_ATTN), lambda i: (0, 0)), pl.BlockSpec((1, D_ATTN), lambda i: (0, 0)),
                   pl.BlockSpec((1, D_F), lambda i: (0, 0))],
        out_shape=[jax.ShapeDtypeStruct((S, D_QKV), BF16), jax.ShapeDtypeStruct((S, D_F), BF16),
                   jax.ShapeDtypeStruct((1, D_ATTN), F32), jax.ShapeDtypeStruct((1, D_ATTN), F32),
                   jax.ShapeDtypeStruct((1, D_F), F32)],
        scratch_shapes=[pltpu.VMEM((1, D_F), F32)],
        compiler_params=_params("arbitrary"),
        name="attn_bwd_post",
    )(pa, pa, pf, dqx, dkx, dvx, gq, gk, fb, bd, triu)


def _shift_down(ext, k):
    return pltpu.roll(ext, k, 0)[HALO:]


def _shift_up(ext, k, n):
    return pltpu.roll(ext, n + HALO - k, 0)[:n]


def _pool_lane_select(a2, a4, a8, a16, lane):
    return jnp.where(lane < 64, a2, jnp.where(lane < 128, a4, jnp.where(lane < 192, a8, a16)))


def _local_branches(o, pb, halo, have_prev, row0, wao, convw, wco, wpool, pscale):
    n = pb.shape[0]
    cx = pb[:, 0:D_CONV].astype(F32)
    cb = pb[:, D_CONV:2 * D_CONV].astype(F32)
    cc = pb[:, 2 * D_CONV:3 * D_CONV].astype(F32)
    px = pb[:, 3 * D_CONV:D_LOCAL].astype(F32)
    keep = have_prev.astype(F32)
    z = cc * cx
    z_ext = jnp.concatenate([halo[:, 2 * D_CONV:3 * D_CONV].astype(F32) * halo[:, 0:D_CONV].astype(F32) * keep, z], axis=0)
    z1 = _shift_down(z_ext, 1)
    z2 = _shift_down(z_ext, 2)
    conv = convw[0:1, :] * z2 + convw[1:2, :] * z1 + convw[2:3, :] * z
    cm = cb * conv

    u_ext = jnp.concatenate([halo[:, 3 * D_CONV:D_LOCAL].astype(F32) * keep, px], axis=0)
    s2 = u_ext + pltpu.roll(u_ext, 1, 0)
    s4 = s2 + pltpu.roll(s2, 2, 0)
    s8 = s4 + pltpu.roll(s4, 4, 0)
    s16 = s8 + pltpu.roll(s8, 8, 0)
    lane = lax.broadcasted_iota(jnp.int32, (n, D_POOL), 1)
    win = _pool_lane_select(2.0, 4.0, 8.0, 16.0, lane)
    t = (row0 + lax.broadcasted_iota(jnp.int32, (n, D_POOL), 0)).astype(F32)
    cnt = jnp.minimum(t + 1.0, win)
    feat = _pool_lane_select(s2[HALO:], s4[HALO:], s8[HALO:], s16[HALO:], lane) / cnt - px

    ya = _dot(o, wao)
    yc = _dot(cm.astype(BF16), wco)
    yp_pre = _dot(feat.astype(BF16), wpool)
    yp = yp_pre * pscale
    return dict(cx=cx, cb=cb, cc=cc, z=z, z1=z1, z2=z2, conv=conv, cm=cm, feat=feat, cnt=cnt, lane=lane,
                ya=ya, yc=yc, yp_pre=yp_pre, yp=yp)


def _halo_spec(tm, tile_of):
    per = tm // HALO
    return pl.BlockSpec((HALO, D_LOCAL), lambda i, *_: (jnp.maximum(tile_of(i) * per - 1, 0), 0))


def _heads_as_columns(ref):
    lane = lax.broadcasted_iota(jnp.int32, ref.shape[1:], 1)
    return jnp.concatenate([jnp.where(lane < HEAD_DIM, ref[2 * p], pltpu.roll(ref[2 * p + 1], HEAD_DIM, 1))
                            for p in range(N_HEADS // 2)], axis=1)


def mix_out_fwd(x, ox, pb, wao, convw, wco, wpool, pscale, wo_all, l):
    S = x.shape[0]
    tm = TM_MIX_FWD

    def body(l_ref, x_ref, o_ref, pb_ref, halo_ref, wao_ref, cw_ref, wco_ref, wp_ref, ps_ref, wo_ref, y_ref):
        i = pl.program_id(0)
        pb = pb_ref[...]
        o = _heads_as_columns(o_ref).astype(BF16)
        b = _local_branches(o, pb, halo_ref[...], i > 0, i * tm, wao_ref[...], cw_ref[...], wco_ref[...],
                            wp_ref[...], ps_ref[...])
        g0 = _sigmoid(pb[:, D_LOCAL:D_LOCAL + D_MODEL].astype(F32))
        g1 = _sigmoid(pb[:, D_LOCAL + D_MODEL:D_LOCAL + 2 * D_MODEL].astype(F32))
        g2 = _sigmoid(pb[:, D_LOCAL + 2 * D_MODEL:D_B].astype(F32))
        merged = g0 * b["ya"] + g1 * b["yc"] + g2 * b["yp"]
        y_ref[...] = x_ref[...] + _dot(merged.astype(BF16), wo_ref[...])

    return pl.pallas_call(
        body,
        grid_spec=pltpu.PrefetchScalarGridSpec(
            num_scalar_prefetch=1,
            grid=(S // tm,),
            in_specs=[_rows(tm, D_MODEL), pl.BlockSpec((N_HEADS, tm, LANES), lambda i, l: (0, i, 0)), _rows(tm, D_B),
                      _halo_spec(tm, lambda i: i), _whole(wao), _whole(convw), _whole(wco), _whole(wpool), _whole(pscale),
                      _layer((D_MODEL, D_MODEL))],
            out_specs=_rows(tm, D_MODEL),
        ),
        out_shape=jax.ShapeDtypeStruct((S, D_MODEL), F32),
        compiler_params=_params("parallel"),
        name="mix_out_fwd",
    )(_layer_index(l), x, ox, pb, pb, wao, convw, wco, wpool, pscale, wo_all)


def mix_out_bwd(dxm, ox, pb, wao, convw, wco, wpool, pscale, wo_all, l):
    S = dxm.shape[0]
    tm = TM_MIX
    nt = S // tm
    rev = lambda i: nt - 1 - i
    rows = lambda n: pl.BlockSpec((tm, n), lambda i, l: (rev(i), 0))
    heads = pl.BlockSpec((N_HEADS, tm, LANES), lambda i, l: (0, rev(i), 0))
    acc = lambda r, c: pl.BlockSpec((r, c), lambda i, l: (0, 0))

    def body(l_ref, dxm_ref, o_ref, pb_ref, halo_ref, wao_ref, cw_ref, wco_ref, wp_ref, ps_ref, wo_ref,
             dpb_ref, do_ref, dwo_ref, dwao_ref, dwco_ref, dwp_ref, dcw_ref, dps_ref, next_dconv, next_e):
        i = pl.program_id(0)
        r = rev(i)

        @pl.when(i == 0)
        def _():
            for ref in (dwo_ref, dwao_ref, dwco_ref, dwp_ref, dcw_ref, dps_ref, next_dconv, next_e):
                ref[...] = jnp.zeros_like(ref)

        pb = pb_ref[...]
        o = _heads_as_columns(o_ref).astype(BF16)
        cw = cw_ref[...]
        b = _local_branches(o, pb, halo_ref[...], r > 0, r * tm, wao_ref[...], cw, wco_ref[...], wp_ref[...], ps_ref[...])
        g0 = _sigmoid(pb[:, D_LOCAL:D_LOCAL + D_MODEL].astype(F32))
        g1 = _sigmoid(pb[:, D_LOCAL + D_MODEL:D_LOCAL + 2 * D_MODEL].astype(F32))
        g2 = _sigmoid(pb[:, D_LOCAL + 2 * D_MODEL:D_B].astype(F32))
        dxb = dxm_ref[...].astype(BF16)
        merged = g0 * b["ya"] + g1 * b["yc"] + g2 * b["yp"]
        dwo_ref[...] += _dot_tn(merged.astype(BF16), dxb)
        dmer = _dot_nt(dxb, wo_ref[...])
        dpb_ref[:, D_LOCAL:D_LOCAL + D_MODEL] = (dmer * b["ya"] * (g0 * (1.0 - g0))).astype(BF16)
        dpb_ref[:, D_LOCAL + D_MODEL:D_LOCAL + 2 * D_MODEL] = (dmer * b["yc"] * (g1 * (1.0 - g1))).astype(BF16)
        dpb_ref[:, D_LOCAL + 2 * D_MODEL:D_B] = (dmer * b["yp"] * (g2 * (1.0 - g2))).astype(BF16)

        dya = (dmer * g0).astype(BF16)
        dwao_ref[...] += _dot_tn(o, dya)
        da = _dot_nt(dya, wao_ref[...])
        lane = lax.broadcasted_iota(jnp.int32, (tm, LANES), 1)
        for h in range(N_HEADS):
            two = da[:, (h // 2) * LANES:(h // 2 + 1) * LANES]
            dah = jnp.where(lane < HEAD_DIM, pltpu.roll(two, HEAD_DIM, 1) if h % 2 else two, 0.0).astype(BF16)
            d1, d2, d3 = _pieces(-jnp.sum(dah.astype(F32) * o_ref[h], axis=1, keepdims=True))
            do_ref[h] = jnp.where(lane == LANE_C + 1, d1, jnp.where(lane == LANE_C + 2, d2,
                                                                  jnp.where(lane == LANE_C + 3, d3, dah)))

        dyc = (dmer * g1).astype(BF16)
        dwco_ref[...] += _dot_tn(b["cm"].astype(BF16), dyc)
        dcm = _dot_nt(dyc, wco_ref[...])
        dconv = dcm * b["cb"]
        dcw_ref[0:1, :] += jnp.sum(dconv * b["z2"], axis=0, keepdims=True)
        dcw_ref[1:2, :] += jnp.sum(dconv * b["z1"], axis=0, keepdims=True)
        dcw_ref[2:3, :] += jnp.sum(dconv * b["z"], axis=0, keepdims=True)
        d_ext = jnp.concatenate([dconv, next_dconv[...]], axis=0)
        dz = cw[2:3, :] * dconv + cw[1:2, :] * _shift_up(d_ext, 1, tm) + cw[0:1, :] * _shift_up(d_ext, 2, tm)
        next_dconv[...] = dconv[0:HALO]
        dpb_ref[:, 0:D_CONV] = (dz * b["cc"]).astype(BF16)
        dpb_ref[:, D_CONV:2 * D_CONV] = (dcm * b["conv"]).astype(BF16)
        dpb_ref[:, 2 * D_CONV:3 * D_CONV] = (dz * b["cx"]).astype(BF16)

        dyp = dmer * g2
        dps_ref[...] += jnp.sum(dyp * b["yp_pre"], axis=0, keepdims=True)
        dyps = (dyp * ps_ref[...]).astype(BF16)
        dwp_ref[...] += _dot_tn(b["feat"].astype(BF16), dyps)
        dfeat = _dot_nt(dyps, wp_ref[...])
        e = dfeat / b["cnt"]
        e_ext = jnp.concatenate([e, next_e[...]], axis=0)
        up = lambda a, k: pltpu.roll(a, tm + HALO - k, 0)
        f2 = e_ext + up(e_ext, 1)
        f4 = f2 + up(f2, 2)
        f8 = f4 + up(f4, 4)
        f16 = f8 + up(f8, 8)
        next_e[...] = e[0:HALO]
        dpb_ref[:, 3 * D_CONV:D_LOCAL] = (_pool_lane_select(f2[:tm], f4[:tm], f8[:tm], f16[:tm], b["lane"]) - dfeat).astype(BF16)

    return pl.pallas_call(
        body,
        grid_spec=pltpu.PrefetchScalarGridSpec(
            num_scalar_prefetch=1,
            grid=(nt,),
            in_specs=[rows(D_MODEL), heads, rows(D_B), _halo_spec(tm, rev),
                      _whole(wao), _whole(convw), _whole(wco), _whole(wpool), _whole(pscale), _layer((D_MODEL, D_MODEL))],
            out_specs=[rows(D_B), heads, acc(D_MODEL, D_MODEL), acc(D_ATTN, D_MODEL), acc(D_CONV, D_MODEL),
                       acc(D_POOL, D_MODEL), acc(8, D_CONV), acc(1, D_MODEL)],
            scratch_shapes=[pltpu.VMEM((HALO, D_CONV), F32), pltpu.VMEM((HALO, D_POOL), F32)],
        ),
        out_shape=[jax.ShapeDtypeStruct((S, D_B), BF16), jax.ShapeDtypeStruct((N_HEADS, S, LANES), BF16),
                   jax.ShapeDtypeStruct((D_MODEL, D_MODEL), F32), jax.ShapeDtypeStruct((D_ATTN, D_MODEL), F32),
                   jax.ShapeDtypeStruct((D_CONV, D_MODEL), F32), jax.ShapeDtypeStruct((D_POOL, D_MODEL), F32),
                   jax.ShapeDtypeStruct((8, D_CONV), F32), jax.ShapeDtypeStruct((1, D_MODEL), F32)],
        compiler_params=_params("arbitrary"),
        name="mix_out_bwd",
    )(_layer_index(l), dxm, ox, pb, pb, wao, convw, wco, wpool, pscale, wo_all)


FF_SHARD = 2 * D_FF // 4


def ffn_fwd(x, g, w1_all, w2_all, l):
    S = x.shape[0]

    def body(l_ref, x_ref, g_ref, w1_ref, w2_ref, y_ref, u_ref):
        xf = x_ref[...]
        r = lax.rsqrt(jnp.mean(xf * xf, axis=-1, keepdims=True) + EPS)
        h = (xf * r * g_ref[...]).astype(BF16)
        u = jnp.concatenate([_dot(h, w1_ref[j]) for j in range(4)], axis=1)
        u_ref[...] = u.astype(BF16)
        gt = u[:, 0:D_FF]
        act = gt * _sigmoid(gt) * u[:, D_FF:2 * D_FF]
        y_ref[...] = xf + _dot(act.astype(BF16), w2_ref[...])

    return pl.pallas_call(
        body,
        grid_spec=pltpu.PrefetchScalarGridSpec(
            num_scalar_prefetch=1,
            grid=(S // TM,),
            in_specs=[_rows(TM, D_MODEL), _whole(g), _layer((4, D_MODEL, FF_SHARD)), _layer((D_FF, D_MODEL))],
            out_specs=[_rows(TM, D_MODEL), _rows(TM, 2 * D_FF)],
        ),
        out_shape=[jax.ShapeDtypeStruct((S, D_MODEL), F32), jax.ShapeDtypeStruct((S, 2 * D_FF), BF16)],
        compiler_params=_params("parallel"),
        name="ffn_fwd",
    )(_layer_index(l), x, g, w1_all, w2_all)


def ffn_bwd(x, g, dy, u, w1_all, w2_all, l):
    S = x.shape[0]

    def body(l_ref, x_ref, g_ref, dy_ref, u_ref, w1_ref, w2_ref, dx_ref, du_ref, act_ref, h_ref, dyb_ref, dg_ref):
        @pl.when(pl.program_id(0) == 0)
        def _():
            dg_ref[...] = jnp.zeros_like(dg_ref)

        dyf = dy_ref[...]
        dyb_ref[...] = dyf.astype(BF16)
        dact = _dot_nt(dyb_ref[...], w2_ref[...])
        gt = u_ref[:, 0:D_FF].astype(F32)
        up = u_ref[:, D_FF:2 * D_FF].astype(F32)
        sg = _sigmoid(gt)
        silu = gt * sg
        act_ref[...] = (silu * up).astype(BF16)
        du_ref[:, 0:D_FF] = (dact * up * (sg * (1.0 + gt * (1.0 - sg)))).astype(BF16)
        du_ref[:, D_FF:2 * D_FF] = (dact * silu).astype(BF16)
        dh = _dot_nt(du_ref[:, 0:FF_SHARD], w1_ref[0])
        for j in range(1, 4):
            dh = dh + _dot_nt(du_ref[:, j * FF_SHARD:(j + 1) * FF_SHARD], w1_ref[j])
        xf = x_ref[...]
        r = lax.rsqrt(jnp.mean(xf * xf, axis=-1, keepdims=True) + EPS)
        xhat = xf * r
        h_ref[...] = (xhat * g_ref[...]).astype(BF16)
        dg_ref[...] += jnp.sum(dh * xhat, axis=0, keepdims=True)
        gdh = dh * g_ref[...]
        dx_ref[...] = dyf + r * (gdh - xhat * jnp.mean(xhat * gdh, axis=-1, keepdims=True))

    return pl.pallas_call(
        body,
        grid_spec=pltpu.PrefetchScalarGridSpec(
            num_scalar_prefetch=1,
            grid=(S // TM,),
            in_specs=[_rows(TM, D_MODEL), _whole(g), _rows(TM, D_MODEL), _rows(TM, 2 * D_FF),
                      _layer((4, D_MODEL, FF_SHARD)), _layer((D_FF, D_MODEL))],
            out_specs=[_rows(TM, D_MODEL), _rows(TM, 2 * D_FF), _rows(TM, D_FF), _rows(TM, D_MODEL), _rows(TM, D_MODEL),
                       pl.BlockSpec((1, D_MODEL), lambda i, l: (0, 0))],
        ),
        out_shape=[jax.ShapeDtypeStruct((S, D_MODEL), F32), jax.ShapeDtypeStruct((S, 2 * D_FF), BF16),
                   jax.ShapeDtypeStruct((S, D_FF), BF16), jax.ShapeDtypeStruct((S, D_MODEL), BF16),
                   jax.ShapeDtypeStruct((S, D_MODEL), BF16), jax.ShapeDtypeStruct((1, D_MODEL), F32)],
        compiler_params=_params("arbitrary"),
        name="ffn_bwd",
    )(_layer_index(l), x, g, dy, u, w1_all, w2_all)


def loss_head(y, target):
    S = y.shape[0]

    def body(y_ref, t_ref, loss_ref, dy_ref):
        @pl.when(pl.program_id(0) == 0)
        def _():
            loss_ref[0, 0] = 0.0

        err = y_ref[...] - t_ref[...]
        dy_ref[...] = err * (1.0 / D_MODEL)
        loss_ref[0, 0] += 0.5 * jnp.sum(jnp.mean(err * err, axis=-1))

    return pl.pallas_call(
        body,
        grid=(S // TM,),
        in_specs=[_rows(TM, D_MODEL), _rows(TM, D_MODEL)],
        out_specs=[pl.BlockSpec((1, 1), lambda i: (0, 0), memory_space=pltpu.SMEM), _rows(TM, D_MODEL)],
        out_shape=[jax.ShapeDtypeStruct((1, 1), F32), jax.ShapeDtypeStruct((S, D_MODEL), F32)],
        compiler_params=_params("arbitrary"),
        name="loss_head",
    )(y, target)


SPLIT = {
    "w_in": ((D_MODEL, D_IN), 1),
    "w_attn_out": ((D_ATTN, D_MODEL), 1),
    "w_conv_out": ((D_CONV, D_MODEL), 1),
    "pool_w": ((N_GROUPS, D_POOL // N_GROUPS, D_MODEL // N_GROUPS), 2),
    "w_o": ((D_MODEL, D_MODEL), 0),
    "w_ffn_in": ((D_MODEL, 2 * D_FF), 1),
    "w_ffn_out": ((D_FF, D_MODEL), 0),
}
SMALL = {"norm_mix_g": D_MODEL, "forget_b": N_HEADS, "q_norm_g": HEAD_DIM, "k_norm_g": HEAD_DIM,
         "pool_scale": D_MODEL, "norm_ffn_g": D_MODEL, "conv_w": CONV_K * D_CONV}
WEIGHTS = ["norm_mix_g", "w_in", "forget_b", "q_norm_g", "k_norm_g", "w_attn_out", "conv_w", "w_conv_out", "pool_w",
           "pool_scale", "w_o", "norm_ffn_g", "w_ffn_in", "w_ffn_out"]


MIX_W = ["w_in", "w_attn_out", "w_conv_out", "pool_w", "w_o"]
FFN_W = ["w_ffn_in", "w_ffn_out"]


def _row(a):
    return a.astype(F32).reshape(1, -1)


def mix_fwd(x, full, li, small):
    n = full["w_o"].shape[0]
    wa, wf, wb = w_in_prep(full["w_in"], li)
    wao, wco, wpool, convw = branch_w_prep(full["w_attn_out"], full["w_conv_out"], full["pool_w"], full["conv_w"], li)
    w = dict(wa=wa, wf=wf, wb=wb, wao=wao, wco=wco, wpool=wpool, convw=convw,
             w_o=full["w_o"].reshape(n, D_MODEL, D_MODEL), at=li,
             g1=_row(small["norm_mix_g"]), pscale=_row(small["pool_scale"]),
             gq=_row(jnp.tile(small["q_norm_g"], N_HEADS)), gk=_row(jnp.tile(small["k_norm_g"], N_HEADS)),
             fb=_row(jnp.pad(small["forget_b"], (0, D_F - N_HEADS))))
    h, pa, pf, pb = in_proj_fwd(x, w["g1"], wa, wf, wb)
    qx, kx, vx = qk_prep(pa, pf, w["gq"], w["gk"], w["fb"])
    ox, qxb = attn_fwd(qx, kx, vx)
    xm = mix_out_fwd(x, ox, pb, wao, convw, wco, wpool, w["pscale"], w["w_o"], li)
    return xm, w, dict(x=x, h=h, pa=pa, pf=pf, pb=pb, kx=kx, vx=vx, ox=ox, qxb=qxb)


def ffn_half_fwd(xm, full, li, g2):
    n = full["w_ffn_out"].shape[0]
    w = dict(w1=full["w_ffn_in"], w2=full["w_ffn_out"].reshape(n, D_FF, D_MODEL), at=li, g2=_row(g2))
    y, u = ffn_fwd(xm, w["g2"], w["w1"], w["w2"], li)
    return y, w, dict(xm=xm, u=u)


def ffn_half_bwd(dx, w, s, gi, big):
    n = big["w_ffn_out"].shape[0]
    big = dict(big)
    dxm, du, act, h2, dyb, dg2 = ffn_bwd(s["xm"], w["g2"], dx, s["u"], w["w1"], w["w2"], w["at"])
    big["w_ffn_out"] = wgrad_into(act, dyb, "wgrad_ffn_out", big["w_ffn_out"].reshape(n, D_FF, D_MODEL),
                                  gi).reshape(big["w_ffn_out"].shape)
    big["w_ffn_in"] = wgrad_into(h2, du, "wgrad_ffn_in", big["w_ffn_in"], gi)
    return dxm, big, dict(norm_ffn_g=dg2[0])


def mix_bwd_weights(dxm, w, s, gi, big):
    big = dict(big)
    dpb, dox, dwo, dwao, dwco, dwpool, dconvw, dpscale = mix_out_bwd(
        dxm, s["ox"], s["pb"], w["wao"], w["convw"], w["wco"], w["wpool"], w["pscale"], w["w_o"], w["at"])
    dqx, dkx, dvx = attn_bwd(s["qxb"], s["kx"], s["vx"], dox)
    dpa, dpf, dgq, dgk, dfb = attn_bwd_post(s["pa"], s["pf"], dqx, dkx, dvx, w["gq"], w["gk"], w["fb"])
    big["w_in"] = w_in_unprep(wgrad(s["h"], dpa, "wgrad_in_qkv"), wgrad(s["h"], dpf, "wgrad_in_f"),
                              wgrad(s["h"], dpb, "wgrad_in_b"), big["w_in"], gi)
    big["w_attn_out"], big["w_conv_out"], big["pool_w"], big["w_o"] = branch_g_place(
        dwao, dwco, dwpool, dwo, (big["w_attn_out"], big["w_conv_out"], big["pool_w"], big["w_o"]), gi)
    sm = dict(forget_b=dfb[0, 0:N_HEADS], q_norm_g=dgq.reshape(N_HEADS, HEAD_DIM).sum(0),
              k_norm_g=dgk.reshape(N_HEADS, HEAD_DIM).sum(0), pool_scale=dpscale[0],
              conv_w=dconvw[0:CONV_K].reshape(-1))
    return (dpa, dpf, dpb), big, sm


def mix_bwd_input(dproj, dxm, w, s):
    dx, dg1 = in_proj_bwd(s["x"], w["g1"], dxm, *dproj, w["wa"], w["wf"], w["wb"])
    return dx, dict(norm_mix_g=dg1[0])


def mix_bwd(dxm, w, s, gi, big):
    dproj, big, sm = mix_bwd_weights(dxm, w, s, gi, big)
    dx, g1 = mix_bwd_input(dproj, dxm, w, s)
    return dx, big, {**sm, **g1}


def grad_buffers(full):
    return {n: lax.empty(full[n].shape, F32) for n in SPLIT}


def local_step(x, target, gathered, small):
    n_layers = small["norm_mix_g"].shape[0]
    done = []
    for l in range(n_layers):
        xm, wm, sm = mix_fwd(x, gathered, l, {n: v[l] for n, v in small.items()})
        x, wf, sf = ffn_half_fwd(xm, gathered, l, small["norm_ffn_g"][l])
        done.append((wm, sm, wf, sf))
    loss, dx = loss_head(x, target)
    big = grad_buffers(gathered)
    small_grads = [None] * n_layers
    for l in reversed(range(n_layers)):
        wm, sm, wf, sf = done[l]
        dxm, ffn_big, g_ffn = ffn_half_bwd(dx, wf, sf, l, {n: big[n] for n in FFN_W})
        dx, mix_big, g_mix = mix_bwd(dxm, wm, sm, l, {n: big[n] for n in MIX_W})
        big = {**ffn_big, **mix_big}
        small_grads[l] = {**g_ffn, **g_mix}
    return loss, dx, big, {n: jnp.stack([g[n] for g in small_grads]) for n in SMALL}


def adamw(w, g, m, v, name):
    R, C = w.shape
    tm = 256 if R % 256 == 0 else R

    def body(w_ref, g_ref, m_ref, v_ref, d_ref, nm_ref, nv_ref):
        gr = g_ref[...]
        m_new = ADAM_B1 * m_ref[...] + (1.0 - ADAM_B1) * gr
        v_new = ADAM_B2 * v_ref[...] + (1.0 - ADAM_B2) * jnp.square(gr)
        nm_ref[...] = m_new
        nv_ref[...] = v_new
        m_hat = m_new / (1.0 - ADAM_B1 ** ADAM_STEP)
        v_hat = v_new / (1.0 - ADAM_B2 ** ADAM_STEP)
        d_ref[...] = -ADAM_LR * (m_hat / (jnp.sqrt(v_hat) + ADAM_EPS) + ADAM_WD * w_ref[...])

    spec = _rows(tm, C)
    out = jax.ShapeDtypeStruct((R, C), F32)
    return pl.pallas_call(
        body,
        grid=(R // tm,),
        in_specs=[spec] * 4,
        out_specs=[spec] * 3,
        out_shape=[out] * 3,
        compiler_params=_params("parallel"),
        name=name,
    )(w, g, m, v)


MESH = pl.DeviceIdType.MESH
HBM_REF = pl.BlockSpec(memory_space=pl.ANY)
N_CHIPS = 4
N_DEV = 8
SMALL_SHAPE = (128, LANES)


def _mesh_pos():
    return lax.axis_index("x"), lax.axis_index("y"), lax.axis_index("c")


def _other_chips(x, y):
    return [(1 - x, y), (x, 1 - y), (1 - x, 1 - y)]


def _remote(src, dst, send_sem, recv_sem, to):
    return pltpu.make_async_remote_copy(src_ref=src, dst_ref=dst, send_sem=send_sem, recv_sem=recv_sem,
                                        device_id=to, device_id_type=MESH)


def _row_tile(rows):
    for tm in (256, 176, 128):
        if rows % tm == 0:
            return tm
    return rows


def _as4(a):
    return a.reshape(a.shape[0], a.shape[1], -1, a.shape[-1])


def _core_rows(buf, c):
    R, C = buf.shape[-2:]
    if R % 2:
        whole = (pl.ds(0, R), pl.ds(0, C))
        return whole, whole, False
    return (pl.ds(c * (R // 2), R // 2), pl.ds(0, C)), (pl.ds((1 - c) * (R // 2), R // 2), pl.ds(0, C)), True


def _half_shape(g):
    return g.shape[:-2] + (g.shape[-2] // 2, g.shape[-1])


def place_shard(chip, w, lo, hi, dtype, name, after=None):
    w3 = w.reshape(w.shape[0], -1, w.shape[-1])
    _, R, C = w3.shape
    tm = _row_tile(R)
    idle = [] if after is None else [after]

    def body(chip_ref, w_ref, *rest):
        rest[-1][...] = w_ref[...].astype(dtype)

    return pl.pallas_call(
        body,
        grid_spec=pltpu.PrefetchScalarGridSpec(
            num_scalar_prefetch=1,
            grid=(hi - lo, R // tm),
            in_specs=[pl.BlockSpec((None, tm, C), lambda l, i, chip: (lo + l, i, 0))] + [_whole(a) for a in idle],
            out_specs=pl.BlockSpec((None, None, tm, C), lambda l, i, chip: (l, chip[0], i, 0)),
        ),
        out_shape=jax.ShapeDtypeStruct((hi - lo, N_CHIPS, R, C), dtype),
        compiler_params=_params("parallel", "parallel"),
        name=name,
    )(chip, w3, *idle)


HBM_SPACE = pl.BlockSpec(memory_space=pltpu.HBM)
SEM_SPACE = pl.BlockSpec(memory_space=pltpu.SEMAPHORE)
IN_FLIGHT = pltpu.SideEffectType.DATAFLOW_SIDE_EFFECTING


def _sem_table(send_sems, recv_sems):
    return lambda t, j: (send_sems.at[t, j], recv_sems.at[t, j])


def _sem_per_peer(send_sems, recv_sems):
    return lambda t, j: (send_sems[j], recv_sems[j])


def _gather_ici(bufs, sem, sends=True, lands=True):
    x, y, c = _mesh_pos()
    me = 2 * x + y
    out, into = [], []
    for t, buf in enumerate(bufs):
        mine, _, _ = _core_rows(buf, c)
        part = lambda k, buf=buf, mine=mine: buf.at[pl.ds(0, buf.shape[0]), k, *mine]
        for j, (px, py) in enumerate(_other_chips(x, y)):
            if sends:
                out.append(_remote(part(me), part(me), *sem(t, j), (px, py, c)))
            if lands:
                into.append(_remote(part(2 * px + py), part(2 * px + py), *sem(t, j), (px, py, c)))
    return out, into


def _gather_d2d(bufs, send_sems, recv_sems, first):
    x, y, c = _mesh_pos()
    sends, lands = [], []
    for t, buf in enumerate(bufs):
        mine, theirs, split = _core_rows(buf, c)
        if not split:
            continue
        for j, (px, py) in enumerate(_other_chips(x, y)):
            part = lambda half, buf=buf, k=2 * px + py: buf.at[pl.ds(0, buf.shape[0]), k, *half]
            sems = (send_sems.at[t, first + j], recv_sems.at[t, first + j], (x, y, 1 - c))
            sends.append(_remote(part(mine), part(mine), *sems))
            lands.append(_remote(part(theirs), part(theirs), *sems))
    return sends, lands


def gather_start(bufs):
    n = len(bufs)

    def body(*refs):
        send_sems, recv_sems = refs[n:n + 3], refs[n + 3:n + 6]
        outs = refs[n + 6:2 * n + 6]
        token = refs[2 * n + 6]
        for cp in _gather_ici(outs, _sem_per_peer(send_sems, recv_sems), lands=False)[0]:
            cp.start()
        token[...] = jnp.zeros_like(token)

    res = pl.pallas_call(
        body,
        in_specs=[HBM_SPACE] * n,
        out_specs=[SEM_SPACE] * 6 + [HBM_SPACE] * n + [pl.BlockSpec(memory_space=pltpu.VMEM)],
        out_shape=[pltpu.SemaphoreType.DMA(())] * 6
        + [pltpu.HBM(b.shape, b.dtype) for b in bufs] + [jax.ShapeDtypeStruct((8, LANES), F32)],
        input_output_aliases={t: t + 6 for t in range(n)},
        compiler_params=pltpu.CompilerParams(has_side_effects=IN_FLIGHT),
        name="gather_start",
    )(*[pltpu.with_memory_space_constraint(b, pltpu.HBM) for b in bufs])
    return list(res[0:3]), list(res[3:6]), list(res[6:n + 6]), res[n + 6]


def gather_wait(send_sems, recv_sems, bufs, after):
    n = len(bufs)

    def body(*refs):
        ins_send, ins_recv = refs[n:n + 3], refs[n + 3:n + 6]
        outs = refs[n + 7:]
        sends, lands = _gather_ici(outs, _sem_per_peer(ins_send, ins_recv))
        for cp in lands:
            cp.wait_recv()
        for cp in sends:
            cp.wait_send()

    return list(pl.pallas_call(
        body,
        in_specs=[HBM_SPACE] * n + [SEM_SPACE] * 6 + [HBM_REF],
        out_specs=[HBM_SPACE] * n,
        out_shape=[pltpu.HBM(b.shape, b.dtype) for b in bufs],
        input_output_aliases={t: t for t in range(n)},
        compiler_params=pltpu.CompilerParams(has_side_effects=IN_FLIGHT),
        name="gather_wait",
    )(*bufs, *send_sems, *recv_sems, after))


def gather_forward(bufs):
    n = len(bufs)

    def body(*refs):
        outs = refs[n:2 * n]
        send_sems, recv_sems = refs[2 * n:]
        sends, lands = _gather_d2d(outs, send_sems, recv_sems, 0)
        for cp in sends:
            cp.start()
        for cp in lands:
            cp.wait_recv()
        for cp in sends:
            cp.wait_send()

    return list(pl.pallas_call(
        body,
        in_specs=[HBM_REF] * n,
        out_specs=[HBM_REF] * n,
        out_shape=[jax.ShapeDtypeStruct(b.shape, b.dtype) for b in bufs],
        input_output_aliases={t: t for t in range(n)},
        scratch_shapes=[pltpu.SemaphoreType.DMA((n, 3)), pltpu.SemaphoreType.DMA((n, 3))],
        name="gather_forward",
    )(*bufs))


def pair_exchange(grads):
    n = len(grads)

    def body(*refs):
        ins, outs = refs[:n], refs[n:2 * n]
        send_sems, recv_sems = refs[2 * n:]
        cps = _pair_copies(ins, outs, lambda t: (send_sems.at[t], recv_sems.at[t]))
        for cp in cps:
            cp.start()
        for cp in cps:
            cp.wait()

    return list(pl.pallas_call(
        body,
        in_specs=[HBM_REF] * n,
        out_specs=[HBM_REF] * n,
        out_shape=[jax.ShapeDtypeStruct(_half_shape(g), g.dtype) for g in grads],
        scratch_shapes=[pltpu.SemaphoreType.DMA((n,)), pltpu.SemaphoreType.DMA((n,))],
        name="pair_exchange",
    )(*grads))


def _pair_copies(grads, lands, sem):
    x, y, c = _mesh_pos()
    cps = []
    for t in range(len(grads)):
        _, theirs, _ = _core_rows(grads[t], c)
        src = grads[t].at[pl.ds(0, grads[t].shape[0]), pl.ds(0, N_CHIPS), *theirs]
        cps.append(_remote(src, lands[t], *sem(t), (x, y, 1 - c)))
    return cps


def pair_exchange_start(grads):
    n = len(grads)
    lands = [lax.empty(_half_shape(g), g.dtype) for g in grads]

    def body(*refs):
        send_sem, recv_sem = refs[2 * n:2 * n + 2]
        outs = refs[2 * n + 2:4 * n + 2]
        token = refs[4 * n + 2]
        for cp in _pair_copies(outs[:n], outs[n:], lambda t: (send_sem, recv_sem)):
            cp.start()
        token[...] = jnp.zeros_like(token)

    res = pl.pallas_call(
        body,
        in_specs=[HBM_SPACE] * (2 * n),
        out_specs=[SEM_SPACE] * 2 + [HBM_SPACE] * (2 * n) + [pl.BlockSpec(memory_space=pltpu.VMEM)],
        out_shape=[pltpu.SemaphoreType.DMA(())] * 2 + [pltpu.HBM(a.shape, a.dtype) for a in list(grads) + lands]
        + [jax.ShapeDtypeStruct((8, LANES), F32)],
        input_output_aliases={t: t + 2 for t in range(2 * n)},
        compiler_params=pltpu.CompilerParams(has_side_effects=IN_FLIGHT),
        name="pair_exchange_start",
    )(*[pltpu.with_memory_space_constraint(a, pltpu.HBM) for a in list(grads) + lands])
    return res[0], res[1], list(res[2:n + 2]), list(res[n + 2:2 * n + 2]), res[2 * n + 2]


def pair_exchange_wait(send_sem, recv_sem, grads, lands, after):
    n = len(grads)

    def body(*refs):
        in_send, in_recv = refs[2 * n:2 * n + 2]
        outs = refs[2 * n + 3:]
        for cp in _pair_copies(outs[:n], outs[n:], lambda t: (in_send, in_recv)):
            cp.wait()

    res = pl.pallas_call(
        body,
        in_specs=[HBM_SPACE] * (2 * n) + [SEM_SPACE] * 2 + [HBM_REF],
        out_specs=[HBM_SPACE] * (2 * n),
        out_shape=[pltpu.HBM(a.shape, a.dtype) for a in list(grads) + list(lands)],
        input_output_aliases={t: t for t in range(2 * n)},
        compiler_params=pltpu.CompilerParams(has_side_effects=IN_FLIGHT),
        name="pair_exchange_wait",
    )(*grads, *lands, send_sem, recv_sem, after)
    return list(res[:n]), list(res[n:])


def pair_sum(core, g, t, name):
    n, _, hr, hc = t.shape
    R, C = g.shape[2:]
    tm = _row_tile(hr)
    per = hr // tm

    def body(c_ref, g_ref, t_ref, o_ref):
        o_ref[...] = (g_ref[...] + t_ref[...]).astype(BF16)

    tile = pl.BlockSpec((None, tm, hc), lambda a, i, c: (a, i, 0))
    out = pl.pallas_call(
        body,
        grid_spec=pltpu.PrefetchScalarGridSpec(
            num_scalar_prefetch=1,
            grid=(n * N_CHIPS, per),
            in_specs=[pl.BlockSpec((None, tm, hc), lambda a, i, c: (a, per * c[0] + i, 0)), tile],
            out_specs=tile,
        ),
        out_shape=jax.ShapeDtypeStruct((n * N_CHIPS, hr, hc), BF16),
        compiler_params=_params("parallel", "parallel"),
        name=name,
    )(core, g.reshape(n * N_CHIPS, R, C), t.reshape(n * N_CHIPS, hr, hc))
    return out.reshape(t.shape)


def _chip_copies(sums, recv, sem):
    x, y, c = _mesh_pos()
    cps = []
    for t in range(len(sums)):
        for j, (px, py) in enumerate(_other_chips(x, y)):
            src = sums[t].at[pl.ds(0, sums[t].shape[0]), 2 * px + py]
            cps.append(_remote(src, recv[t].at[j], *sem(t, j), (px, py, c)))
    return cps


def _recv_shape(s):
    return (N_CHIPS - 1, s.shape[0]) + s.shape[2:]


def chip_exchange(sums):
    n = len(sums)

    def body(*refs):
        ins, outs = refs[:n], refs[n:2 * n]
        cps = _chip_copies(ins, outs, _sem_table(*refs[2 * n:]))
        for cp in cps:
            cp.start()
        for cp in cps:
            cp.wait()

    return list(pl.pallas_call(
        body,
        in_specs=[HBM_REF] * n,
        out_specs=[HBM_REF] * n,
        out_shape=[jax.ShapeDtypeStruct(_recv_shape(s), s.dtype) for s in sums],
        scratch_shapes=[pltpu.SemaphoreType.DMA((n, 3)), pltpu.SemaphoreType.DMA((n, 3))],
        name="chip_exchange",
    )(*sums))


def chip_exchange_start(sums):
    n = len(sums)
    lands = [lax.empty(_recv_shape(s), s.dtype) for s in sums]

    def body(*refs):
        send_sems, recv_sems = refs[2 * n:2 * n + 3], refs[2 * n + 3:2 * n + 6]
        outs = refs[2 * n + 6:4 * n + 6]
        token = refs[4 * n + 6]
        for cp in _chip_copies(outs[:n], outs[n:2 * n], _sem_per_peer(send_sems, recv_sems)):
            cp.start()
        token[...] = jnp.zeros_like(token)

    res = pl.pallas_call(
        body,
        in_specs=[HBM_SPACE] * (2 * n),
        out_specs=[SEM_SPACE] * 6 + [HBM_SPACE] * (2 * n) + [pl.BlockSpec(memory_space=pltpu.VMEM)],
        out_shape=[pltpu.SemaphoreType.DMA(())] * 6 + [pltpu.HBM(a.shape, a.dtype) for a in list(sums) + lands]
        + [jax.ShapeDtypeStruct((8, LANES), F32)],
        input_output_aliases={t: t + 6 for t in range(2 * n)},
        compiler_params=pltpu.CompilerParams(has_side_effects=IN_FLIGHT),
        name="chip_exchange_start",
    )(*[pltpu.with_memory_space_constraint(a, pltpu.HBM) for a in list(sums) + lands])
    return list(res[0:3]), list(res[3:6]), list(res[6:n + 6]), list(res[n + 6:2 * n + 6]), res[2 * n + 6]


def chip_exchange_wait(send_sems, recv_sems, sums, lands, after):
    n = len(sums)

    def body(*refs):
        ins_send, ins_recv = refs[2 * n:2 * n + 3], refs[2 * n + 3:2 * n + 6]
        outs = refs[2 * n + 7:]
        for cp in _chip_copies(outs[:n], outs[n:], _sem_per_peer(ins_send, ins_recv)):
            cp.wait()

    res = pl.pallas_call(
        body,
        in_specs=[HBM_SPACE] * (2 * n) + [SEM_SPACE] * 6 + [HBM_REF],
        out_specs=[HBM_SPACE] * (2 * n),
        out_shape=[pltpu.HBM(a.shape, a.dtype) for a in list(sums) + list(lands)],
        input_output_aliases={t: t for t in range(2 * n)},
        compiler_params=pltpu.CompilerParams(has_side_effects=IN_FLIGHT),
        name="chip_exchange_wait",
    )(*sums, *lands, *send_sems, *recv_sems, after)
    return list(res[:n]), list(res[n:])


def chip_sum_into(core, chip, recv, sums, total, lo, name):
    _, n, hr, hc = recv.shape
    tm = _row_tile(hr)
    per = hr // tm

    def body(c_ref, k_ref, r_ref, s_ref, t_ref, o_ref):
        acc = s_ref[...].astype(F32)
        for j in range(N_CHIPS - 1):
            acc = acc + r_ref[j].astype(F32)
        o_ref[...] = acc

    return pl.pallas_call(
        body,
        grid_spec=pltpu.PrefetchScalarGridSpec(
            num_scalar_prefetch=2,
            grid=(n, per),
            in_specs=[pl.BlockSpec((N_CHIPS - 1, None, tm, hc), lambda a, i, c, k: (0, a, i, 0)),
                      pl.BlockSpec((None, None, tm, hc), lambda a, i, c, k: (a, k[0], i, 0)),
                      HBM_REF],
            out_specs=pl.BlockSpec((None, tm, hc), lambda a, i, c, k: (lo + a, per * c[0] + i, 0)),
        ),
        out_shape=jax.ShapeDtypeStruct(total.shape, F32),
        input_output_aliases={4: 0},
        compiler_params=_params("parallel", "parallel"),
        name=name,
    )(core, chip, recv, sums, total)


def sibling_share(totals):
    n = len(totals)

    def body(*refs):
        outs = refs[n:2 * n]
        send_sems, recv_sems = refs[2 * n:]
        x, y, c = _mesh_pos()
        half = lambda t, which: outs[t].at[pl.ds(0, DEPTH), *_core_rows(outs[t], c)[which]]
        sent = [_remote(half(t, 0), half(t, 0), send_sems.at[t], recv_sems.at[t], (x, y, 1 - c)) for t in range(n)]
        for cp in sent:
            cp.start()
        for t in range(n):
            _remote(half(t, 1), half(t, 1), send_sems.at[t], recv_sems.at[t], (x, y, 1 - c)).wait_recv()
        for cp in sent:
            cp.wait_send()

    return pl.pallas_call(
        body,
        in_specs=[HBM_REF] * n,
        out_specs=[HBM_REF] * n,
        out_shape=[jax.ShapeDtypeStruct(t.shape, t.dtype) for t in totals],
        input_output_aliases={t: t for t in range(n)},
        scratch_shapes=[pltpu.SemaphoreType.DMA((n,)), pltpu.SemaphoreType.DMA((n,))],
        name="sibling_share",
    )(*totals)


def small_allgather(small):
    def body(s_ref, a_ref, send_sems, recv_sems, local_sem):
        x, y, c = _mesh_pos()
        me = 4 * x + 2 * y + c
        own = pltpu.make_async_copy(s_ref, a_ref.at[me], local_sem)
        own.start()
        sent = []
        for k in range(1, N_DEV):
            peer = (x ^ (k >> 2), y ^ ((k >> 1) & 1), c ^ (k & 1))
            cp = _remote(s_ref, a_ref.at[me], send_sems.at[k - 1], recv_sems.at[k - 1], peer)
            cp.start()
            sent.append(cp)
        for k in range(1, N_DEV):
            px, py, pc = x ^ (k >> 2), y ^ ((k >> 1) & 1), c ^ (k & 1)
            _remote(s_ref, a_ref.at[4 * px + 2 * py + pc], send_sems.at[k - 1], recv_sems.at[k - 1], (px, py, pc)).wait_recv()
        for cp in sent:
            cp.wait_send()
        own.wait()

    return pl.pallas_call(
        body,
        in_specs=[HBM_REF],
        out_specs=HBM_REF,
        out_shape=jax.ShapeDtypeStruct((N_DEV,) + SMALL_SHAPE, small.dtype),
        scratch_shapes=[pltpu.SemaphoreType.DMA((N_DEV - 1,)), pltpu.SemaphoreType.DMA((N_DEV - 1,)), pltpu.SemaphoreType.DMA],
        name="small_allgather",
    )(small)


def small_sum(blocks):
    def body(a_ref, o_ref):
        acc = a_ref[0]
        for d in range(1, N_DEV):
            acc = acc + a_ref[d]
        o_ref[...] = acc

    return pl.pallas_call(
        body,
        in_specs=[pl.BlockSpec(memory_space=pltpu.VMEM)],
        out_specs=pl.BlockSpec(memory_space=pltpu.VMEM),
        out_shape=jax.ShapeDtypeStruct(SMALL_SHAPE, F32),
        name="small_sum",
    )(blocks)


def pack_small(grads, loss):
    flat = jnp.concatenate([grads[n].reshape(-1) for n in SMALL] + [loss.reshape(-1)])
    size = SMALL_SHAPE[0] * SMALL_SHAPE[1]
    return jnp.pad(flat, (0, size - flat.shape[0])).reshape(SMALL_SHAPE)


def unpack_small(packed):
    flat = packed.reshape(-1)
    out, off = {}, 0
    for n, size in SMALL.items():
        out[n] = flat[off:off + DEPTH * size].reshape(DEPTH, size)
        off += DEPTH * size
    return out, flat[off]


def kernel(x, norm_mix_g, w_in, forget_b, q_norm_g, k_norm_g, w_attn_out, conv_w, w_conv_out, pool_w, pool_scale, w_o, norm_ffn_g, w_ffn_in, w_ffn_out, loss_target, m_norm_mix_g, m_w_in, m_forget_b, m_q_norm_g, m_k_norm_g, m_w_attn_out, m_conv_w, m_w_conv_out, m_pool_w, m_pool_scale, m_w_o, m_norm_ffn_g, m_w_ffn_in, m_w_ffn_out, v_norm_mix_g, v_w_in, v_forget_b, v_q_norm_g, v_k_norm_g, v_w_attn_out, v_conv_w, v_w_conv_out, v_pool_w, v_pool_scale, v_w_o, v_norm_ffn_g, v_w_ffn_in, v_w_ffn_out):
    given = dict(locals())
    weights = {n: given[n] for n in WEIGHTS}

    core = lax.axis_index("c").astype(jnp.int32)
    chip = (2 * lax.axis_index("x") + lax.axis_index("y")).astype(jnp.int32)

    core, chip = core.reshape(1), chip.reshape(1)
    mix_names, all_names = MIX_W + ["conv_w"], MIX_W + FFN_W + ["conv_w"]

    def placed(lo, hi, names, after=None):
        piece = None
        if after is not None:
            piece = _as4(next(iter(after.values())))[0, 0, 0:HALO, 0:LANES] if isinstance(after, dict) else after
        return [place_shard(chip, weights[n], lo, hi, F32 if n == "conv_w" else BF16, "place_" + n, piece)
                for n in names]

    def as_weights(bufs, names):
        return {n: b.reshape(b.shape[:2] + weights[n].shape[1:]) for n, b in zip(names, bufs)}

    def landed(start, names, after):
        return as_weights(gather_forward(gather_wait(*start[:3], after)), names)

    def layer_small(l, *tokens):
        small = {n: weights[n][l] for n in SMALL if n != "conv_w"}
        for t in tokens:
            small["norm_mix_g"] = small["norm_mix_g"] + t[0, 0]
        return small

    done = [None] * DEPTH
    first = gather_start(placed(0, 1, mix_names))
    next_bufs = (placed(0, 1, FFN_W, after=first[3]), placed(1, 2, all_names, after=first[3]))
    first = landed(first, mix_names, next_bufs[1][0])
    (ffn0, layer1), _ = lax.optimization_barrier((next_bufs, first["conv_w"]))
    ffn0, layer1 = gather_start(ffn0), gather_start(layer1)
    xm, wm, sm = mix_fwd(x[0], first, 0, layer_small(0, ffn0[3], layer1[3]))
    xs, wf, sf = ffn_half_fwd(xm, landed(ffn0, FFN_W, xm), 0, weights["norm_ffn_g"][0])
    done[0] = (wm, sm, wf, sf)
    full = landed(layer1, all_names, xs)
    rest = gather_start(placed(2, DEPTH, all_names, after=full))
    xm, wm, sm = mix_fwd(xs, full, 0, layer_small(1, rest[3]))
    xs, wf, sf = ffn_half_fwd(xm, full, 0, weights["norm_ffn_g"][1])
    done[1] = (wm, sm, wf, sf)
    full = landed(rest, all_names, xs)
    for l in range(2, DEPTH):
        xm, wm, sm = mix_fwd(xs, full, l - 2, layer_small(l))
        xs, wf, sf = ffn_half_fwd(xm, full, l - 2, weights["norm_ffn_g"][l])
        done[l] = (wm, sm, wf, sf)
    loss, dx = loss_head(xs, loss_target[0])

    small_grads = [None] * DEPTH
    totals = {n: lax.empty((DEPTH,) + _as4(weights[n][None]).shape[2:], F32) for n in SPLIT}

    def buffers(names, n_layers):
        return {n: lax.empty((n_layers, N_CHIPS) + weights[n].shape[1:], F32) for n in names}

    def pair_sums(names, grads, theirs):
        return [pair_sum(core, a, t, "pair_sum_" + n) for n, a, t in zip(names, grads, theirs)]

    def add_chips(names, recv, sums, lo):
        for n, r, s in zip(names, recv, sums):
            totals[n] = chip_sum_into(core, chip, r, s, totals[n], lo, "chip_sum_" + n)

    big = buffers(SPLIT, DEPTH - 1)
    for l in reversed(range(1, DEPTH)):
        wm, sm, wf, sf = done[l]
        dxm, ffn_big, g_ffn = ffn_half_bwd(dx, wf, sf, l - 1, {n: big[n] for n in FFN_W})
        dx, mix_big, g_mix = mix_bwd(dxm, wm, sm, l - 1, {n: big[n] for n in MIX_W})
        big = {**ffn_big, **mix_big}
        small_grads[l] = {**g_ffn, **g_mix}
    pair = pair_exchange_start([_as4(big[n]) for n in SPLIT])

    wm, sm, wf, sf = done[0]
    wf = dict(wf, g2=wf["g2"] + pair[4][0:1, 0:1])
    dxm, ffn_big, g_ffn = ffn_half_bwd(dx, wf, sf, 0, buffers(FFN_W, 1))
    chips = chip_exchange_start(pair_sums(SPLIT, *pair_exchange_wait(*pair[:4], ffn_big["w_ffn_in"])))
    ffn_grads = [_as4(ffn_big[n]) for n in FFN_W]
    ffn_chips = chip_exchange_start(pair_sums(FFN_W, ffn_grads, pair_exchange(ffn_grads)))
    wm = dict(wm, pscale=wm["pscale"] + chips[4][0:1, 0:1] + ffn_chips[4][0:1, 0:1])
    dproj, mix_big, g_mix = mix_bwd_weights(dxm, wm, sm, 0, buffers(MIX_W, 1))
    mix_grads = [_as4(mix_big[n]) for n in MIX_W]
    mix_chips = chip_exchange_start(pair_sums(MIX_W, mix_grads, pair_exchange(mix_grads)))
    dx, g_in = mix_bwd_input(dproj, dxm, dict(wm, g1=wm["g1"] + mix_chips[4][0:1, 0:1]), sm)
    small_grads[0] = {**g_ffn, **g_mix, **g_in}

    reduced, deltas, new_m, new_v = {}, {}, {}, {}

    def update(names):
        for n in names:
            w = weights[n]
            flat = (-1, w.shape[-1])
            d, nm, nv = adamw(w.reshape(flat), reduced[n].reshape(flat), given["m_" + n].reshape(flat),
                              given["v_" + n].reshape(flat), "adamw_" + n)
            deltas[n], new_m[n], new_v[n] = d.reshape(w.shape), nm.reshape(w.shape), nv.reshape(w.shape)

    def share(names):
        for n, t in zip(names, sibling_share([totals[n] for n in names])):
            reduced[n] = t.reshape(weights[n].shape)

    sums, recv = chip_exchange_wait(*chips[:4], dx)
    add_chips(SPLIT, recv, sums, 1)
    sums, recv = chip_exchange_wait(*ffn_chips[:4], dx)
    add_chips(FFN_W, recv, sums, 0)
    share(FFN_W)
    update(FFN_W)
    sums, recv = chip_exchange_wait(*mix_chips[:4], deltas["w_ffn_in"])
    add_chips(MIX_W, recv, sums, 0)
    share(MIX_W)

    small_grads = {n: jnp.stack([g[n] for g in small_grads]) for n in SMALL}
    small_total, loss_sum = unpack_small(small_sum(small_allgather(pack_small(small_grads, loss))))
    chip = chip[0]
    cols = D_CONV // N_CHIPS
    small_total["conv_w"] = lax.dynamic_slice_in_dim(small_total["conv_w"].reshape(DEPTH, CONV_K, D_CONV), chip * cols, cols, axis=2)
    for n in SMALL:
        reduced[n] = small_total[n].reshape(weights[n].shape)
    update(MIX_W + list(SMALL))

    return (loss_sum, dx[None], *[reduced[n] for n in WEIGHTS], *[deltas[n] for n in WEIGHTS],
            *[new_m[n] for n in WEIGHTS], *[new_v[n] for n in WEIGHTS])
```

```python
import functools

import numpy as np
import jax
import jax.numpy as jnp
from jax import lax
from jax.experimental import pallas as pl
from jax.experimental.pallas import tpu as pltpu

F32 = jnp.float32
BF16 = jnp.bfloat16

D_MODEL = 1024
DEPTH = 4
HEAD_DIM = 64
N_HEADS = 8
D_ATTN = 512
D_CONV = 256
D_POOL = 256
D_FF = 2816
D_IN = 5640
CONV_K = 3
POOL_WINDOWS = (2, 4, 8, 16)
N_GROUPS = len(POOL_WINDOWS)
EPS = 1e-6
ADAM_LR, ADAM_B1, ADAM_B2, ADAM_EPS, ADAM_WD, ADAM_STEP = 0.001, 0.9, 0.999, 1e-08, 0.01, 10

D_QKV = 3 * D_ATTN
D_F = 128
D_B = 3 * D_CONV + D_POOL + 3 * D_MODEL
D_LOCAL = 3 * D_CONV + D_POOL

LANES = 128
D_HEADS = N_HEADS * LANES
HALO = 16
VMEM_LIMIT = 56 * 1024 * 1024
NEG = -1e30
LOG2E = 1.4426950408889634
LN2 = 0.6931471805599453

TM = 256
TM_MIX = 256
TM_MIX_FWD = 512
TQ = 1024

LANE_C = 64
LANE_ONE = 67
LANE_LSE = 70
N_PIECES = 3


def _dot(a, b):
    return jnp.dot(a, b, preferred_element_type=F32)


def _dot_nt(a, b):
    return lax.dot_general(a, b, (((1,), (1,)), ((), ())), preferred_element_type=F32)


def _dot_tn(a, b):
    return lax.dot_general(a, b, (((0,), (0,)), ((), ())), preferred_element_type=F32)


def _params(*sem):
    return pltpu.CompilerParams(dimension_semantics=sem, vmem_limit_bytes=VMEM_LIMIT)


def _rows(tm, n):
    return pl.BlockSpec((tm, n), lambda i, *_: (i, 0))


def _whole(a):
    nd = a.ndim
    return pl.BlockSpec(a.shape, lambda *_: (0,) * nd)


def _layer(shape):
    nd = len(shape)
    return pl.BlockSpec((None,) + tuple(shape), lambda *a: (a[-1][0],) + (0,) * nd)


def _layer_index(l):
    return jnp.full((1,), l, jnp.int32)


def _split_bf16(x):
    hi = x.astype(BF16)
    lo = (x - hi.astype(F32)).astype(BF16)
    return hi, lo


def _pieces(x):
    p1 = x.astype(BF16)
    r1 = x - p1.astype(F32)
    p2 = r1.astype(BF16)
    p3 = (r1 - p2.astype(F32)).astype(BF16)
    return p1, p2, p3


def _sigmoid(x):
    return 0.5 * jnp.tanh(0.5 * x) + 0.5


def w_in_prep(win, l):
    tr = 256
    n = D_IN // 4
    v_rest = D_QKV - n
    b0 = v_rest + N_HEADS

    def body(s0, s1, s2, s3, wa_ref, wf_ref, wb_ref):
        b = s1[...]
        wa_ref[...] = jnp.concatenate([s0[...], b[:, 0:v_rest]], axis=1)
        wf_ref[...] = jnp.concatenate([b[:, v_rest:b0], jnp.zeros((tr, D_F - N_HEADS), b.dtype)], axis=1)
        wb_ref[...] = jnp.concatenate([b[:, b0:n], s2[...], s3[...]], axis=1)

    shard = lambda j: pl.BlockSpec((None, None, tr, n), lambda i: (l, j, i, 0))
    return pl.pallas_call(
        body,
        grid=(D_MODEL // tr,),
        in_specs=[shard(0), shard(1), shard(2), shard(3)],
        out_specs=[_rows(tr, D_QKV), _rows(tr, D_F), _rows(tr, D_B)],
        out_shape=[jax.ShapeDtypeStruct((D_MODEL, D_QKV), win.dtype), jax.ShapeDtypeStruct((D_MODEL, D_F), win.dtype),
                   jax.ShapeDtypeStruct((D_MODEL, D_B), win.dtype)],
        compiler_params=_params("parallel"),
        name="w_in_prep",
    )(win, win, win, win)


def _into_layer(buf):
    return dict(in_spec=HBM_REF, out_shape=jax.ShapeDtypeStruct(buf.shape, buf.dtype), aliases={1: 0})


def w_in_unprep(dwa, dwf, dwb, buf, l):
    tr = 256
    n = D_IN // 4
    v_rest = D_QKV - n
    b1 = n - v_rest - N_HEADS
    place = _into_layer(buf)

    def body(l_ref, buf_ref, a_ref, f_ref, b_ref, o_ref):
        a = a_ref[...]
        b = b_ref[...]
        o_ref[0] = a[:, 0:n]
        o_ref[1] = jnp.concatenate([a[:, n:D_QKV], f_ref[:, 0:N_HEADS], b[:, 0:b1]], axis=1)
        o_ref[2] = b[:, b1:b1 + n]
        o_ref[3] = b[:, b1 + n:D_B]

    return pl.pallas_call(
        body,
        grid_spec=pltpu.PrefetchScalarGridSpec(
            num_scalar_prefetch=1,
            grid=(D_MODEL // tr,),
            in_specs=[place["in_spec"], _rows(tr, D_QKV), _rows(tr, D_F), _rows(tr, D_B)],
            out_specs=pl.BlockSpec((None, 4, tr, n), lambda i, l: (l[0], 0, i, 0)),
        ),
        out_shape=place["out_shape"],
        input_output_aliases=place["aliases"],
        compiler_params=_params("parallel"),
        name="w_in_unprep",
    )(_layer_index(l), buf, dwa, dwf, dwb)


def branch_w_prep(wao, wco, pw, cw, l):
    gd = D_POOL // N_GROUPS
    od = D_MODEL // N_GROUPS

    def body(wao_ref, wco_ref, pw_ref, cw_ref, ao_ref, co_ref, po_ref, co8_ref):
        ao_ref[...] = jnp.concatenate([wao_ref[j] for j in range(4)], axis=1)
        co_ref[...] = jnp.concatenate([wco_ref[j] for j in range(4)], axis=1)
        zero = jnp.zeros((gd, od), co_ref.dtype)
        po_ref[...] = jnp.concatenate(
            [jnp.concatenate([jnp.concatenate([pw_ref[j, g] for j in range(4)], axis=1) if g2 == g else zero
                              for g2 in range(N_GROUPS)], axis=1) for g in range(N_GROUPS)], axis=0)
        co8_ref[...] = jnp.zeros_like(co8_ref)
        co8_ref[0:CONV_K, :] = jnp.concatenate([cw_ref[j] for j in range(4)], axis=1)

    sel = lambda *shape: pl.BlockSpec((None,) + shape, lambda i: (l,) + (0,) * len(shape))
    return pl.pallas_call(
        body,
        grid=(1,),
        in_specs=[sel(4, D_ATTN, D_MODEL // 4), sel(4, D_CONV, D_MODEL // 4), sel(4, N_GROUPS, gd, od // 4),
                  sel(4, CONV_K, D_CONV // 4)],
        out_specs=[pl.BlockSpec((D_ATTN, D_MODEL), lambda i: (0, 0)), pl.BlockSpec((D_CONV, D_MODEL), lambda i: (0, 0)),
                   pl.BlockSpec((D_POOL, D_MODEL), lambda i: (0, 0)), pl.BlockSpec((8, D_CONV), lambda i: (0, 0))],
        out_shape=[jax.ShapeDtypeStruct((D_ATTN, D_MODEL), BF16), jax.ShapeDtypeStruct((D_CONV, D_MODEL), BF16),
                   jax.ShapeDtypeStruct((D_POOL, D_MODEL), BF16), jax.ShapeDtypeStruct((8, D_CONV), F32)],
        name="branch_w_prep",
    )(wao, wco, pw, cw)


def branch_g_place(dwao, dwco, dwpool, dwo, bufs, l):
    gd = D_POOL // N_GROUPS
    od = D_MODEL // N_GROUPS
    q = D_MODEL // 4

    def body(l_ref, b0, b1, b2, b3, a_ref, c_ref, p_ref, w_ref, ao_ref, co_ref, po_ref, wo_ref):
        a = a_ref[...]
        c = c_ref[...]
        p = p_ref[...]
        for j in range(4):
            ao_ref[j] = a[:, j * q:(j + 1) * q]
            co_ref[j] = c[:, j * q:(j + 1) * q]
            wo_ref[j] = w_ref[j * q:(j + 1) * q, :]
            for g in range(N_GROUPS):
                c0 = g * od + j * (od // 4)
                po_ref[j, g] = p[g * gd:(g + 1) * gd, c0:c0 + od // 4]

    whole = lambda a: pl.BlockSpec(a.shape, lambda i, l: (0,) * a.ndim)
    layer = lambda b: pl.BlockSpec((None,) + b.shape[1:], lambda i, l: (l[0],) + (0,) * (b.ndim - 1))
    return pl.pallas_call(
        body,
        grid_spec=pltpu.PrefetchScalarGridSpec(
            num_scalar_prefetch=1,
            grid=(1,),
            in_specs=[HBM_REF] * 4 + [whole(dwao), whole(dwco), whole(dwpool), whole(dwo)],
            out_specs=[layer(b) for b in bufs],
        ),
        out_shape=[jax.ShapeDtypeStruct(b.shape, b.dtype) for b in bufs],
        input_output_aliases={1: 0, 2: 1, 3: 2, 4: 3},
        compiler_params=_params("arbitrary"),
        name="branch_g_place",
    )(_layer_index(l), *bufs, dwao, dwco, dwpool, dwo)


def in_proj_fwd(x, g, wa, wf, wb):
    S = x.shape[0]

    def body(x_ref, g_ref, wa_ref, wf_ref, wb_ref, h_ref, pa_ref, pf_ref, pb_ref):
        xf = x_ref[...]
        r = lax.rsqrt(jnp.mean(xf * xf, axis=-1, keepdims=True) + EPS)
        h = (xf * r * g_ref[...]).astype(BF16)
        h_ref[...] = h
        pa_ref[...] = _dot(h, wa_ref[...]).astype(BF16)
        pf_ref[...] = _dot(h, wf_ref[...])
        pb_ref[...] = _dot(h, wb_ref[...]).astype(BF16)

    return pl.pallas_call(
        body,
        grid=(S // TM,),
        in_specs=[_rows(TM, D_MODEL), _whole(g), _whole(wa), _whole(wf), _whole(wb)],
        out_specs=[_rows(TM, D_MODEL), _rows(TM, D_QKV), _rows(TM, D_F), _rows(TM, D_B)],
        out_shape=[
            jax.ShapeDtypeStruct((S, D_MODEL), BF16),
            jax.ShapeDtypeStruct((S, D_QKV), BF16),
            jax.ShapeDtypeStruct((S, D_F), F32),
            jax.ShapeDtypeStruct((S, D_B), BF16),
        ],
        compiler_params=_params("parallel"),
        name="in_proj_fwd",
    )(x, g, wa, wf, wb)


def in_proj_bwd(x, g, dxm, dpa, dpf, dpb, wa, wf, wb):
    S = x.shape[0]

    def body(x_ref, g_ref, dxm_ref, dpa_ref, dpf_ref, dpb_ref, wa_ref, wf_ref, wb_ref, dx_ref, dg_ref):
        @pl.when(pl.program_id(0) == 0)
        def _():
            dg_ref[...] = jnp.zeros_like(dg_ref)

        dh = _dot_nt(dpa_ref[...], wa_ref[...]) + _dot_nt(dpf_ref[...], wf_ref[...]) + _dot_nt(dpb_ref[...], wb_ref[...])
        xf = x_ref[...]
        r = lax.rsqrt(jnp.mean(xf * xf, axis=-1, keepdims=True) + EPS)
        xhat = xf * r
        dg_ref[...] += jnp.sum(dh * xhat, axis=0, keepdims=True)
        gdh = dh * g_ref[...]
        dx_ref[...] = dxm_ref[...] + r * (gdh - xhat * jnp.mean(xhat * gdh, axis=-1, keepdims=True))

    return pl.pallas_call(
        body,
        grid=(S // TM,),
        in_specs=[_rows(TM, D_MODEL), _whole(g), _rows(TM, D_MODEL), _rows(TM, D_QKV), _rows(TM, D_F), _rows(TM, D_B),
                  _whole(wa), _whole(wf), _whole(wb)],
        out_specs=[_rows(TM, D_MODEL), pl.BlockSpec((1, D_MODEL), lambda i: (0, 0))],
        out_shape=[jax.ShapeDtypeStruct((S, D_MODEL), F32), jax.ShapeDtypeStruct((1, D_MODEL), F32)],
        compiler_params=_params("arbitrary"),
        name="in_proj_bwd",
    )(x, g, dxm, dpa, dpf, dpb, wa, wf, wb)


def _wgrad_tokens(S, K):
    return min(S, 2048 if K <= 1024 else 1024)


def _wgrad_columns(n):
    return next(tn for tn in (1024, 768, 512, 256, 128) if n % tn == 0)


def wgrad(xa, dy, name):
    S, K = xa.shape
    N = dy.shape[1]
    ts = _wgrad_tokens(S, K)
    tn = _wgrad_columns(N)
    tk = _wgrad_columns(K)

    def body(x_ref, dy_ref, o_ref):
        @pl.when(pl.program_id(2) == 0)
        def _():
            o_ref[...] = jnp.zeros_like(o_ref)

        o_ref[...] += _dot_tn(x_ref[...], dy_ref[...])

    return pl.pallas_call(
        body,
        grid=(K // tk, N // tn, S // ts),
        in_specs=[pl.BlockSpec((ts, tk), lambda i, j, k: (k, i)), pl.BlockSpec((ts, tn), lambda i, j, k: (k, j))],
        out_specs=pl.BlockSpec((tk, tn), lambda i, j, k: (i, j)),
        out_shape=jax.ShapeDtypeStruct((K, N), F32),
        compiler_params=_params("parallel", "parallel", "arbitrary"),
        name=name,
    )(xa, dy)


def wgrad_into(xa, dy, name, buf, l):
    S, K = xa.shape
    N = dy.shape[1]
    ts = _wgrad_tokens(S, K)
    split = buf.ndim == 4
    tn = buf.shape[-1] if split else _wgrad_columns(N)
    place = _into_layer(buf)

    def body(l_ref, buf_ref, x_ref, dy_ref, o_ref):
        @pl.when(pl.program_id(1) == 0)
        def _():
            o_ref[...] = jnp.zeros_like(o_ref)

        o_ref[...] += _dot_tn(x_ref[...], dy_ref[...])

    if split:
        out_spec = pl.BlockSpec((None, None, K, tn), lambda j, k, l: (l[0], j, 0, 0))
    else:
        out_spec = pl.BlockSpec((None, K, tn), lambda j, k, l: (l[0], 0, j))
    return pl.pallas_call(
        body,
        grid_spec=pltpu.PrefetchScalarGridSpec(
            num_scalar_prefetch=1,
            grid=(N // tn, S // ts),
            in_specs=[place["in_spec"], pl.BlockSpec((ts, K), lambda j, k, l: (k, 0)),
                      pl.BlockSpec((ts, tn), lambda j, k, l: (k, j))],
            out_specs=out_spec,
        ),
        out_shape=place["out_shape"],
        input_output_aliases=place["aliases"],
        compiler_params=_params("parallel", "arbitrary"),
        name=name,
    )(_layer_index(l), buf, xa, dy)


def _head_mean_matrix():
    h = np.arange(D_ATTN) // HEAD_DIM
    return jnp.asarray((h[:, None] == h[None, :]).astype(np.float32) / HEAD_DIM, BF16)


def _place_matrix(lane0):
    m = np.zeros((N_PIECES * LANES, D_HEADS), np.float32)
    for i in range(N_PIECES):
        for h in range(N_HEADS):
            m[i * LANES + h, h * LANES + lane0 + i] = 1.0
    return jnp.asarray(m, BF16)


def _tri(n, upper):
    r = np.arange(n)
    m = (r[None, :] >= r[:, None]) if upper else (r[None, :] <= r[:, None])
    return jnp.asarray(m.astype(np.float32), BF16)


def _lanes_in(lane, lo, n):
    return (lane >= lo) & (lane < lo + n)


def qk_prep(pa, pf, gq, gk, fb):
    S = pa.shape[0]
    bd = _head_mean_matrix()
    tri = _tri(TM, upper=False)
    place_q = _place_matrix(LANE_C)

    def body(q_ref, k_ref, v_ref, pf_ref, gq_ref, gk_ref, fb_ref, bd_ref, tri_ref, pq_ref,
             qx_ref, kx_ref, vx_ref, carry):
        @pl.when(pl.program_id(0) == 0)
        def _():
            carry[...] = jnp.zeros_like(carry)

        def head_norm(x_ref, g_ref, scale):
            xf = x_ref[...].astype(F32)
            ms = _dot((xf * xf).astype(BF16), bd_ref[...])
            return xf * lax.rsqrt(ms + EPS) * g_ref[...] * scale

        qh = head_norm(q_ref, gq_ref, HEAD_DIM ** -0.5 * LOG2E)
        kh = head_norm(k_ref, gk_ref, 1.0)
        vf = v_ref[...].astype(F32)

        z = pf_ref[...] + fb_ref[...]
        logf = jnp.minimum(z, 0.0) - jnp.log(1.0 + jnp.exp(-jnp.abs(z)))
        hi, lo = _split_bf16(logf)
        c = _dot(tri_ref[...], hi) + _dot(tri_ref[...], lo) + carry[...]
        carry[...] += jnp.sum(hi.astype(F32) + lo.astype(F32), axis=0, keepdims=True)
        pieces = jnp.concatenate(_pieces(c * LOG2E), axis=1)
        cq = _dot(pieces, pq_ref[...])

        lane = lax.broadcasted_iota(jnp.int32, (TM, LANES), 1)
        low = lane < HEAD_DIM
        ones_q = _lanes_in(lane, LANE_ONE, N_PIECES).astype(F32)
        ones_k = (_lanes_in(lane, LANE_C, N_PIECES) | _lanes_in(lane, LANE_LSE, N_PIECES)).astype(F32)
        ones_v = _lanes_in(lane, LANE_C, N_PIECES + 1).astype(F32)
        for h in range(N_HEADS):
            blk = slice((h // 2) * LANES, (h // 2 + 1) * LANES)
            head = (lambda a: pltpu.roll(a[:, blk], HEAD_DIM, 1)) if h % 2 else (lambda a: a[:, blk])
            mine = slice(h * LANES, (h + 1) * LANES)
            qx_ref[h] = jnp.where(low, head(qh), cq[:, mine] + ones_q).astype(BF16)
            ck = pltpu.roll(cq[:, mine], LANE_ONE - LANE_C, 1)
            kx_ref[h] = jnp.where(low, head(kh), ones_k - ck).astype(BF16)
            vx_ref[h] = jnp.where(low, head(vf), ones_v).astype(BF16)

    heads = pl.BlockSpec((N_HEADS, TM, LANES), lambda i: (0, i, 0))
    out = jax.ShapeDtypeStruct((N_HEADS, S, LANES), BF16)
    return pl.pallas_call(
        body,
        grid=(S // TM,),
        in_specs=[pl.BlockSpec((TM, D_ATTN), lambda i: (i, 0)), pl.BlockSpec((TM, D_ATTN), lambda i: (i, 1)),
                  pl.BlockSpec((TM, D_ATTN), lambda i: (i, 2)),
                  _rows(TM, D_F), _whole(gq), _whole(gk), _whole(fb), _whole(bd), _whole(tri), _whole(place_q)],
        out_specs=[heads, heads, heads],
        out_shape=[out, out, out],
        scratch_shapes=[pltpu.VMEM((1, D_F), F32)],
        compiler_params=_params("arbitrary"),
        name="qk_prep",
    )(pa, pa, pa, pf, gq, gk, fb, bd, tri, place_q)


def attn_fwd(qx, kx, vx):
    S = qx.shape[1]
    nq = S // TQ

    def body(q_ref, k_ref, v_ref, o_ref, qb_ref):
        i = pl.program_id(1)
        lane = lax.broadcasted_iota(jnp.int32, (TQ, LANES), 1)
        q = [q_ref[0], q_ref[1]]

        def update(qh, m, acc, k, v, visible):
            z = _dot_nt(qh, k)
            if visible is not None:
                z = jnp.where(visible, z, NEG)
            m_new = jnp.maximum(m, jnp.max(z, axis=1, keepdims=True))
            pr = jnp.exp2(z - m_new)
            return m_new, jnp.exp2(m - m_new) * acc + _dot(pr.astype(BF16), v)

        def step(kt, carry, diagonal=False):
            ks = pl.multiple_of(kt * TQ, TQ)
            visible = None
            if diagonal:
                visible = lax.broadcasted_iota(jnp.int32, (TQ, TQ), 0) >= lax.broadcasted_iota(jnp.int32, (TQ, TQ), 1)
            return tuple(update(q[j], *carry[j], k_ref[j, pl.ds(ks, TQ), :], v_ref[j, pl.ds(ks, TQ), :], visible)
                         for j in range(2))

        init = tuple((jnp.full((TQ, 1), NEG, F32), jnp.zeros((TQ, LANES), F32)) for _ in range(2))
        carry = step(i, lax.fori_loop(0, i, step, init), diagonal=True)
        for j in range(2):
            m, acc = carry[j]
            l = jnp.sum(jnp.where(lane == LANE_C, acc, 0.0), axis=1, keepdims=True)
            o_ref[j] = acc / l
            n1, n2, n3 = _pieces(-(m + jnp.log(l) * LOG2E))
            qb_ref[j] = jnp.where(lane == LANE_LSE, n1,
                                  jnp.where(lane == LANE_LSE + 1, n2, jnp.where(lane == LANE_LSE + 2, n3, q[j])))

    pair_tile = pl.BlockSpec((2, TQ, LANES), lambda p, i: (p, i, 0))
    pair_all = pl.BlockSpec((2, S, LANES), lambda p, i: (p, 0, 0))
    return pl.pallas_call(
        body,
        grid=(N_HEADS // 2, nq),
        in_specs=[pair_tile, pair_all, pair_all],
        out_specs=[pair_tile, pair_tile],
        out_shape=[jax.ShapeDtypeStruct((N_HEADS, S, LANES), F32), jax.ShapeDtypeStruct((N_HEADS, S, LANES), BF16)],
        compiler_params=_params("parallel", "parallel"),
        name="attn_fwd",
    )(qx, kx, vx)


def attn_bwd(qxb, kx, vx, dox):
    S = qxb.shape[1]
    nq = S // TQ

    def body(q_ref, k_ref, v_ref, do_ref, dq_ref, dk_ref, dv_ref):
        kt = pl.program_id(1)

        @pl.when(kt == 0)
        def _():
            dq_ref[...] = jnp.zeros_like(dq_ref)

        k = [k_ref[0], k_ref[1]]
        v = [v_ref[0], v_ref[1]]

        def block(j, q0, nq_rows, k0, nk, diagonal):
            q = q_ref[j, pl.ds(q0, nq_rows), :]
            dout = do_ref[j, pl.ds(q0, nq_rows), :]
            kj, vj = k[j][k0:k0 + nk], v[j][k0:k0 + nk]
            z = _dot_nt(q, kj)
            if diagonal:
                visible = (lax.broadcasted_iota(jnp.int32, (nq_rows, nk), 0)
                           >= lax.broadcasted_iota(jnp.int32, (nq_rows, nk), 1))
                z = jnp.where(visible, z, NEG)
            pr = jnp.exp2(z)
            dv = _dot_tn(pr.astype(BF16), dout)
            dsb = (pr * _dot_nt(dout, vj)).astype(BF16)
            dq_ref[j, pl.ds(q0, nq_rows), :] += _dot(dsb, kj)
            return _dot_tn(dsb, q), dv

        def step(qi, carry):
            qs = pl.multiple_of(qi * TQ, TQ)
            new = []
            for j in range(2):
                dk, dv = block(j, qs, TQ, 0, TQ, False)
                new.append((carry[j][0] + dk, carry[j][1] + dv))
            return tuple(new)

        half = TQ // 2
        qs = pl.multiple_of(kt * TQ, TQ)
        carry = []
        for j in range(2):
            first = block(j, qs, TQ, 0, half, True)
            second = block(j, pl.multiple_of(qs + half, half), half, half, half, True)
            carry.append(tuple(jnp.concatenate(a, axis=0) for a in zip(first, second)))
        carry = lax.fori_loop(kt + 1, nq, step, tuple(carry))
        for j in range(2):
            dk_ref[j] = carry[j][0]
            dv_ref[j] = carry[j][1]

    pair_tile = pl.BlockSpec((2, TQ, LANES), lambda p, kt: (p, kt, 0))
    pair_all = pl.BlockSpec((2, S, LANES), lambda p, kt: (p, 0, 0))
    out = jax.ShapeDtypeStruct((N_HEADS, S, LANES), F32)
    return pl.pallas_call(
        body,
        grid=(N_HEADS // 2, nq),
        in_specs=[pair_all, pair_tile, pair_tile, pair_all],
        out_specs=[pair_all, pair_tile, pair_tile],
        out_shape=[out, out, out],
        compiler_params=_params("arbitrary", "arbitrary"),
        name="attn_bwd",
    )(qxb, kx, vx, dox)


def attn_bwd_post(pa, pf, dqx, dkx, dvx, gq, gk, fb):
    S = pa.shape[0]
    nt = S // TM
    bd = _head_mean_matrix()
    triu = _tri(TM, upper=True)
    rev = lambda i: nt - 1 - i

    def body(q_ref, k_ref, pf_ref, dqx_ref, dkx_ref, dvx_ref, gq_ref, gk_ref, fb_ref, bd_ref, triu_ref,
             dpa_ref, dpf_ref, dgq_ref, dgk_ref, dfb_ref, carry):
        @pl.when(pl.program_id(0) == 0)
        def _():
            carry[...] = jnp.zeros_like(carry)
            dgq_ref[...] = jnp.zeros_like(dgq_ref)
            dgk_ref[...] = jnp.zeros_like(dgk_ref)
            dfb_ref[...] = jnp.zeros_like(dfb_ref)

        lane = lax.broadcasted_iota(jnp.int32, (TM, LANES), 1)
        columns = _heads_as_columns

        def head_norm_bwd(x_ref, dy, g_ref, dg_ref):
            xf = x_ref[...].astype(F32)
            r = lax.rsqrt(_dot((xf * xf).astype(BF16), bd_ref[...]) + EPS)
            xhat = xf * r
            dg_ref[...] += jnp.sum(dy * xhat, axis=0, keepdims=True)
            gdy = dy * g_ref[...]
            return (r * (gdy - xhat * _dot((xhat * gdy).astype(BF16), bd_ref[...]))).astype(BF16)

        dpa_ref[:, 0:D_ATTN] = head_norm_bwd(q_ref, columns(dqx_ref) * HEAD_DIM ** -0.5, gq_ref, dgq_ref)
        dpa_ref[:, D_ATTN:2 * D_ATTN] = head_norm_bwd(k_ref, columns(dkx_ref) * LN2, gk_ref, dgk_ref)
        dpa_ref[:, 2 * D_ATTN:3 * D_ATTN] = columns(dvx_ref).astype(BF16)

        dc = jnp.zeros((TM, LANES), F32)
        for h in range(N_HEADS):
            both = jnp.where(lane == LANE_C, dqx_ref[h], 0.0) - jnp.where(lane == LANE_ONE, dkx_ref[h], 0.0)
            dc = jnp.where(lane == h, jnp.sum(both, axis=1, keepdims=True), dc)
        hi, lo = _split_bf16(dc)
        dlogf = _dot(triu_ref[...], hi) + _dot(triu_ref[...], lo) + carry[...]
        first = lax.broadcasted_iota(jnp.int32, (TM, D_F), 0) == 0
        carry[...] = jnp.sum(jnp.where(first, dlogf, 0.0), axis=0, keepdims=True)
        df = dlogf * _sigmoid(-(pf_ref[...] + fb_ref[...]))
        dfb_ref[...] += jnp.sum(df, axis=0, keepdims=True)
        dpf_ref[...] = df.astype(BF16)

    heads = pl.BlockSpec((N_HEADS, TM, LANES), lambda i: (0, rev(i), 0))
    return pl.pallas_call(
        body,
        grid=(nt,),
        in_specs=[pl.BlockSpec((TM, D_ATTN), lambda i: (rev(i), 0)), pl.BlockSpec((TM, D_ATTN), lambda i: (rev(i), 1)),
                  pl.BlockSpec((TM, D_F), lambda i: (rev(i), 0)), heads, heads, heads,
                  _whole(gq), _whole(gk), _whole(fb), _whole(bd), _whole(triu)],
        out_specs=[pl.BlockSpec((TM, D_QKV), lambda i: (rev(i), 0)), pl.BlockSpec((TM, D_F), lambda i: (rev(i), 0)),
                   pl.BlockSpec((1, D_ATTN), lambda i: (0, 0)), pl.BlockSpec((1, D_ATTN), lambda i: (0, 0)),
                   pl.BlockSpec((1, D_F), lambda i: (0, 0))],
        out_shape=[jax.ShapeDtypeStruct((S, D_QKV), BF16), jax.ShapeDtypeStruct((S, D_F), BF16),
                   jax.ShapeDtypeStruct((1, D_ATTN), F32), jax.ShapeDtypeStruct((1, D_ATTN), F32),
                   jax.ShapeDtypeStruct((1, D_F), F32)],
        scratch_shapes=[pltpu.VMEM((1, D_F), F32)],
        compiler_params=_params("arbitrary"),
        name="attn_bwd_post",
    )(pa, pa, pf, dqx, dkx, dvx, gq, gk, fb, bd, triu)


def _shift_down(ext, k):
    return pltpu.roll(ext, k, 0)[HALO:]


def _shift_up(ext, k, n):
    return pltpu.roll(ext, n + HALO - k, 0)[:n]


def _pool_lane_select(a2, a4, a8, a16, lane):
    return jnp.where(lane < 64, a2, jnp.where(lane < 128, a4, jnp.where(lane < 192, a8, a16)))


def _local_branches(o, pb, halo, have_prev, row0, wao, convw, wco, wpool, pscale):
    n = pb.shape[0]
    cx = pb[:, 0:D_CONV].astype(F32)
    cb = pb[:, D_CONV:2 * D_CONV].astype(F32)
    cc = pb[:, 2 * D_CONV:3 * D_CONV].astype(F32)
    px = pb[:, 3 * D_CONV:D_LOCAL].astype(F32)
    keep = have_prev.astype(F32)
    z = cc * cx
    z_ext = jnp.concatenate([halo[:, 2 * D_CONV:3 * D_CONV].astype(F32) * halo[:, 0:D_CONV].astype(F32) * keep, z], axis=0)
    z1 = _shift_down(z_ext, 1)
    z2 = _shift_down(z_ext, 2)
    conv = convw[0:1, :] * z2 + convw[1:2, :] * z1 + convw[2:3, :] * z
    cm = cb * conv

    u_ext = jnp.concatenate([halo[:, 3 * D_CONV:D_LOCAL].astype(F32) * keep, px], axis=0)
    s2 = u_ext + pltpu.roll(u_ext, 1, 0)
    s4 = s2 + pltpu.roll(s2, 2, 0)
    s8 = s4 + pltpu.roll(s4, 4, 0)
    s16 = s8 + pltpu.roll(s8, 8, 0)
    lane = lax.broadcasted_iota(jnp.int32, (n, D_POOL), 1)
    win = _pool_lane_select(2.0, 4.0, 8.0, 16.0, lane)
    t = (row0 + lax.broadcasted_iota(jnp.int32, (n, D_POOL), 0)).astype(F32)
    cnt = jnp.minimum(t + 1.0, win)
    feat = _pool_lane_select(s2[HALO:], s4[HALO:], s8[HALO:], s16[HALO:], lane) / cnt - px

    ya = _dot(o, wao)
    yc = _dot(cm.astype(BF16), wco)
    yp_pre = _dot(feat.astype(BF16), wpool)
    yp = yp_pre * pscale
    return dict(cx=cx, cb=cb, cc=cc, z=z, z1=z1, z2=z2, conv=conv, cm=cm, feat=feat, cnt=cnt, lane=lane,
                ya=ya, yc=yc, yp_pre=yp_pre, yp=yp)


def _halo_spec(tm, tile_of):
    per = tm // HALO
    return pl.BlockSpec((HALO, D_LOCAL), lambda i, *_: (jnp.maximum(tile_of(i) * per - 1, 0), 0))


def _heads_as_columns(ref):
    lane = lax.broadcasted_iota(jnp.int32, ref.shape[1:], 1)
    return jnp.concatenate([jnp.where(lane < HEAD_DIM, ref[2 * p], pltpu.roll(ref[2 * p + 1], HEAD_DIM, 1))
                            for p in range(N_HEADS // 2)], axis=1)


def mix_out_fwd(x, ox, pb, wao, convw, wco, wpool, pscale, wo_all, l):
    S = x.shape[0]
    tm = TM_MIX_FWD

    def body(l_ref, x_ref, o_ref, pb_ref, halo_ref, wao_ref, cw_ref, wco_ref, wp_ref, ps_ref, wo_ref, y_ref):
        i = pl.program_id(0)
        pb = pb_ref[...]
        o = _heads_as_columns(o_ref).astype(BF16)
        b = _local_branches(o, pb, halo_ref[...], i > 0, i * tm, wao_ref[...], cw_ref[...], wco_ref[...],
                            wp_ref[...], ps_ref[...])
        g0 = _sigmoid(pb[:, D_LOCAL:D_LOCAL + D_MODEL].astype(F32))
        g1 = _sigmoid(pb[:, D_LOCAL + D_MODEL:D_LOCAL + 2 * D_MODEL].astype(F32))
        g2 = _sigmoid(pb[:, D_LOCAL + 2 * D_MODEL:D_B].astype(F32))
        merged = g0 * b["ya"] + g1 * b["yc"] + g2 * b["yp"]
        y_ref[...] = x_ref[...] + _dot(merged.astype(BF16), wo_ref[...])

    return pl.pallas_call(
        body,
        grid_spec=pltpu.PrefetchScalarGridSpec(
            num_scalar_prefetch=1,
            grid=(S // tm,),
            in_specs=[_rows(tm, D_MODEL), pl.BlockSpec((N_HEADS, tm, LANES), lambda i, l: (0, i, 0)), _rows(tm, D_B),
                      _halo_spec(tm, lambda i: i), _whole(wao), _whole(convw), _whole(wco), _whole(wpool), _whole(pscale),
                      _layer((D_MODEL, D_MODEL))],
            out_specs=_rows(tm, D_MODEL),
        ),
        out_shape=jax.ShapeDtypeStruct((S, D_MODEL), F32),
        compiler_params=_params("parallel"),
        name="mix_out_fwd",
    )(_layer_index(l), x, ox, pb, pb, wao, convw, wco, wpool, pscale, wo_all)


def mix_out_bwd(dxm, ox, pb, wao, convw, wco, wpool, pscale, wo_all, l):
    S = dxm.shape[0]
    tm = TM_MIX
    nt = S // tm
    rev = lambda i: nt - 1 - i
    rows = lambda n: pl.BlockSpec((tm, n), lambda i, l: (rev(i), 0))
    heads = pl.BlockSpec((N_HEADS, tm, LANES), lambda i, l: (0, rev(i), 0))
    acc = lambda r, c: pl.BlockSpec((r, c), lambda i, l: (0, 0))

    def body(l_ref, dxm_ref, o_ref, pb_ref, halo_ref, wao_ref, cw_ref, wco_ref, wp_ref, ps_ref, wo_ref,
             dpb_ref, do_ref, dwo_ref, dwao_ref, dwco_ref, dwp_ref, dcw_ref, dps_ref, next_dconv, next_e):
        i = pl.program_id(0)
        r = rev(i)

        @pl.when(i == 0)
        def _():
            for ref in (dwo_ref, dwao_ref, dwco_ref, dwp_ref, dcw_ref, dps_ref, next_dconv, next_e):
                ref[...] = jnp.zeros_like(ref)

        pb = pb_ref[...]
        o = _heads_as_columns(o_ref).astype(BF16)
        cw = cw_ref[...]
        b = _local_branches(o, pb, halo_ref[...], r > 0, r * tm, wao_ref[...], cw, wco_ref[...], wp_ref[...], ps_ref[...])
        g0 = _sigmoid(pb[:, D_LOCAL:D_LOCAL + D_MODEL].astype(F32))
        g1 = _sigmoid(pb[:, D_LOCAL + D_MODEL:D_LOCAL + 2 * D_MODEL].astype(F32))
        g2 = _sigmoid(pb[:, D_LOCAL + 2 * D_MODEL:D_B].astype(F32))
        dxb = dxm_ref[...].astype(BF16)
        merged = g0 * b["ya"] + g1 * b["yc"] + g2 * b["yp"]
        dwo_ref[...] += _dot_tn(merged.astype(BF16), dxb)
        dmer = _dot_nt(dxb, wo_ref[...])
        dpb_ref[:, D_LOCAL:D_LOCAL + D_MODEL] = (dmer * b["ya"] * (g0 * (1.0 - g0))).astype(BF16)
        dpb_ref[:, D_LOCAL + D_MODEL:D_LOCAL + 2 * D_MODEL] = (dmer * b["yc"] * (g1 * (1.0 - g1))).astype(BF16)
        dpb_ref[:, D_LOCAL + 2 * D_MODEL:D_B] = (dmer * b["yp"] * (g2 * (1.0 - g2))).astype(BF16)

        dya = (dmer * g0).astype(BF16)
        dwao_ref[...] += _dot_tn(o, dya)
        da = _dot_nt(dya, wao_ref[...])
        lane = lax.broadcasted_iota(jnp.int32, (tm, LANES), 1)
        for h in range(N_HEADS):
            two = da[:, (h // 2) * LANES:(h // 2 + 1) * LANES]
            dah = jnp.where(lane < HEAD_DIM, pltpu.roll(two, HEAD_DIM, 1) if h % 2 else two, 0.0).astype(BF16)
            d1, d2, d3 = _pieces(-jnp.sum(dah.astype(F32) * o_ref[h], axis=1, keepdims=True))
            do_ref[h] = jnp.where(lane == LANE_C + 1, d1, jnp.where(lane == LANE_C + 2, d2,
                                                                  jnp.where(lane == LANE_C + 3, d3, dah)))

        dyc = (dmer * g1).astype(BF16)
        dwco_ref[...] += _dot_tn(b["cm"].astype(BF16), dyc)
        dcm = _dot_nt(dyc, wco_ref[...])
        dconv = dcm * b["cb"]
        dcw_ref[0:1, :] += jnp.sum(dconv * b["z2"], axis=0, keepdims=True)
        dcw_ref[1:2, :] += jnp.sum(dconv * b["z1"], axis=0, keepdims=True)
        dcw_ref[2:3, :] += jnp.sum(dconv * b["z"], axis=0, keepdims=True)
        d_ext = jnp.concatenate([dconv, next_dconv[...]], axis=0)
        dz = cw[2:3, :] * dconv + cw[1:2, :] * _shift_up(d_ext, 1, tm) + cw[0:1, :] * _shift_up(d_ext, 2, tm)
        next_dconv[...] = dconv[0:HALO]
        dpb_ref[:, 0:D_CONV] = (dz * b["cc"]).astype(BF16)
        dpb_ref[:, D_CONV:2 * D_CONV] = (dcm * b["conv"]).astype(BF16)
        dpb_ref[:, 2 * D_CONV:3 * D_CONV] = (dz * b["cx"]).astype(BF16)

        dyp = dmer * g2
        dps_ref[...] += jnp.sum(dyp * b["yp_pre"], axis=0, keepdims=True)
        dyps = (dyp * ps_ref[...]).astype(BF16)
        dwp_ref[...] += _dot_tn(b["feat"].astype(BF16), dyps)
        dfeat = _dot_nt(dyps, wp_ref[...])
        e = dfeat / b["cnt"]
        e_ext = jnp.concatenate([e, next_e[...]], axis=0)
        up = lambda a, k: pltpu.roll(a, tm + HALO - k, 0)
        f2 = e_ext + up(e_ext, 1)
        f4 = f2 + up(f2, 2)
        f8 = f4 + up(f4, 4)
        f16 = f8 + up(f8, 8)
        next_e[...] = e[0:HALO]
        dpb_ref[:, 3 * D_CONV:D_LOCAL] = (_pool_lane_select(f2[:tm], f4[:tm], f8[:tm], f16[:tm], b["lane"]) - dfeat).astype(BF16)

    return pl.pallas_call(
        body,
        grid_spec=pltpu.PrefetchScalarGridSpec(
            num_scalar_prefetch=1,
            grid=(nt,),
            in_specs=[rows(D_MODEL), heads, rows(D_B), _halo_spec(tm, rev),
                      _whole(wao), _whole(convw), _whole(wco), _whole(wpool), _whole(pscale), _layer((D_MODEL, D_MODEL))],
            out_specs=[rows(D_B), heads, acc(D_MODEL, D_MODEL), acc(D_ATTN, D_MODEL), acc(D_CONV, D_MODEL),
                       acc(D_POOL, D_MODEL), acc(8, D_CONV), acc(1, D_MODEL)],
            scratch_shapes=[pltpu.VMEM((HALO, D_CONV), F32), pltpu.VMEM((HALO, D_POOL), F32)],
        ),
        out_shape=[jax.ShapeDtypeStruct((S, D_B), BF16), jax.ShapeDtypeStruct((N_HEADS, S, LANES), BF16),
                   jax.ShapeDtypeStruct((D_MODEL, D_MODEL), F32), jax.ShapeDtypeStruct((D_ATTN, D_MODEL), F32),
                   jax.ShapeDtypeStruct((D_CONV, D_MODEL), F32), jax.ShapeDtypeStruct((D_POOL, D_MODEL), F32),
                   jax.ShapeDtypeStruct((8, D_CONV), F32), jax.ShapeDtypeStruct((1, D_MODEL), F32)],
        compiler_params=_params("arbitrary"),
        name="mix_out_bwd",
    )(_layer_index(l), dxm, ox, pb, pb, wao, convw, wco, wpool, pscale, wo_all)


FF_SHARD = 2 * D_FF // 4


def ffn_fwd(x, g, w1_all, w2_all, l):
    S = x.shape[0]

    def body(l_ref, x_ref, g_ref, w1_ref, w2_ref, y_ref, u_ref):
        xf = x_ref[...]
        r = lax.rsqrt(jnp.mean(xf * xf, axis=-1, keepdims=True) + EPS)
        h = (xf * r * g_ref[...]).astype(BF16)
        u = jnp.concatenate([_dot(h, w1_ref[j]) for j in range(4)], axis=1)
        u_ref[...] = u.astype(BF16)
        gt = u[:, 0:D_FF]
        act = gt * _sigmoid(gt) * u[:, D_FF:2 * D_FF]
        y_ref[...] = xf + _dot(act.astype(BF16), w2_ref[...])

    return pl.pallas_call(
        body,
        grid_spec=pltpu.PrefetchScalarGridSpec(
            num_scalar_prefetch=1,
            grid=(S // TM,),
            in_specs=[_rows(TM, D_MODEL), _whole(g), _layer((4, D_MODEL, FF_SHARD)), _layer((D_FF, D_MODEL))],
            out_specs=[_rows(TM, D_MODEL), _rows(TM, 2 * D_FF)],
        ),
        out_shape=[jax.ShapeDtypeStruct((S, D_MODEL), F32), jax.ShapeDtypeStruct((S, 2 * D_FF), BF16)],
        compiler_params=_params("parallel"),
        name="ffn_fwd",
    )(_layer_index(l), x, g, w1_all, w2_all)


def ffn_bwd(x, g, dy, u, w1_all, w2_all, l):
    S = x.shape[0]

    def body(l_ref, x_ref, g_ref, dy_ref, u_ref, w1_ref, w2_ref, dx_ref, du_ref, act_ref, h_ref, dyb_ref, dg_ref):
        @pl.when(pl.program_id(0) == 0)
        def _():
            dg_ref[...] = jnp.zeros_like(dg_ref)

        dyf = dy_ref[...]
        dyb_ref[...] = dyf.astype(BF16)
        dact = _dot_nt(dyb_ref[...], w2_ref[...])
        gt = u_ref[:, 0:D_FF].astype(F32)
        up = u_ref[:, D_FF:2 * D_FF].astype(F32)
        sg = _sigmoid(gt)
        silu = gt * sg
        act_ref[...] = (silu * up).astype(BF16)
        du_ref[:, 0:D_FF] = (dact * up * (sg * (1.0 + gt * (1.0 - sg)))).astype(BF16)
        du_ref[:, D_FF:2 * D_FF] = (dact * silu).astype(BF16)
        dh = _dot_nt(du_ref[:, 0:FF_SHARD], w1_ref[0])
        for j in range(1, 4):
            dh = dh + _dot_nt(du_ref[:, j * FF_SHARD:(j + 1) * FF_SHARD], w1_ref[j])
        xf = x_ref[...]
        r = lax.rsqrt(jnp.mean(xf * xf, axis=-1, keepdims=True) + EPS)
        xhat = xf * r
        h_ref[...] = (xhat * g_ref[...]).astype(BF16)
        dg_ref[...] += jnp.sum(dh * xhat, axis=0, keepdims=True)
        gdh = dh * g_ref[...]
        dx_ref[...] = dyf + r * (gdh - xhat * jnp.mean(xhat * gdh, axis=-1, keepdims=True))

    return pl.pallas_call(
        body,
        grid_spec=pltpu.PrefetchScalarGridSpec(
            num_scalar_prefetch=1,
            grid=(S // TM,),
            in_specs=[_rows(TM, D_MODEL), _whole(g), _rows(TM, D_MODEL), _rows(TM, 2 * D_FF),
                      _layer((4, D_MODEL, FF_SHARD)), _layer((D_FF, D_MODEL))],
            out_specs=[_rows(TM, D_MODEL), _rows(TM, 2 * D_FF), _rows(TM, D_FF), _rows(TM, D_MODEL), _rows(TM, D_MODEL),
                       pl.BlockSpec((1, D_MODEL), lambda i, l: (0, 0))],
        ),
        out_shape=[jax.ShapeDtypeStruct((S, D_MODEL), F32), jax.ShapeDtypeStruct((S, 2 * D_FF), BF16),
                   jax.ShapeDtypeStruct((S, D_FF), BF16), jax.ShapeDtypeStruct((S, D_MODEL), BF16),
                   jax.ShapeDtypeStruct((S, D_MODEL), BF16), jax.ShapeDtypeStruct((1, D_MODEL), F32)],
        compiler_params=_params("arbitrary"),
        name="ffn_bwd",
    )(_layer_index(l), x, g, dy, u, w1_all, w2_all)


def loss_head(y, target):
    S = y.shape[0]

    def body(y_ref, t_ref, loss_ref, dy_ref):
        @pl.when(pl.program_id(0) == 0)
        def _():
            loss_ref[0, 0] = 0.0

        err = y_ref[...] - t_ref[...]
        dy_ref[...] = err * (1.0 / D_MODEL)
        loss_ref[0, 0] += 0.5 * jnp.sum(jnp.mean(err * err, axis=-1))

    return pl.pallas_call(
        body,
        grid=(S // TM,),
        in_specs=[_rows(TM, D_MODEL), _rows(TM, D_MODEL)],
        out_specs=[pl.BlockSpec((1, 1), lambda i: (0, 0), memory_space=pltpu.SMEM), _rows(TM, D_MODEL)],
        out_shape=[jax.ShapeDtypeStruct((1, 1), F32), jax.ShapeDtypeStruct((S, D_MODEL), F32)],
        compiler_params=_params("arbitrary"),
        name="loss_head",
    )(y, target)


SPLIT = {
    "w_in": ((D_MODEL, D_IN), 1),
    "w_attn_out": ((D_ATTN, D_MODEL), 1),
    "w_conv_out": ((D_CONV, D_MODEL), 1),
    "pool_w": ((N_GROUPS, D_POOL // N_GROUPS, D_MODEL // N_GROUPS), 2),
    "w_o": ((D_MODEL, D_MODEL), 0),
    "w_ffn_in": ((D_MODEL, 2 * D_FF), 1),
    "w_ffn_out": ((D_FF, D_MODEL), 0),
}
SMALL = {"norm_mix_g": D_MODEL, "forget_b": N_HEADS, "q_norm_g": HEAD_DIM, "k_norm_g": HEAD_DIM,
         "pool_scale": D_MODEL, "norm_ffn_g": D_MODEL, "conv_w": CONV_K * D_CONV}
WEIGHTS = ["norm_mix_g", "w_in", "forget_b", "q_norm_g", "k_norm_g", "w_attn_out", "conv_w", "w_conv_out", "pool_w",
           "pool_scale", "w_o", "norm_ffn_g", "w_ffn_in", "w_ffn_out"]


MIX_W = ["w_in", "w_attn_out", "w_conv_out", "pool_w", "w_o"]
FFN_W = ["w_ffn_in", "w_ffn_out"]


def _row(a):
    return a.astype(F32).reshape(1, -1)


def mix_fwd(x, full, li, small):
    n = full["w_o"].shape[0]
    wa, wf, wb = w_in_prep(full["w_in"], li)
    wao, wco, wpool, convw = branch_w_prep(full["w_attn_out"], full["w_conv_out"], full["pool_w"], full["conv_w"], li)
    w = dict(wa=wa, wf=wf, wb=wb, wao=wao, wco=wco, wpool=wpool, convw=convw,
             w_o=full["w_o"].reshape(n, D_MODEL, D_MODEL), at=li,
             g1=_row(small["norm_mix_g"]), pscale=_row(small["pool_scale"]),
             gq=_row(jnp.tile(small["q_norm_g"], N_HEADS)), gk=_row(jnp.tile(small["k_norm_g"], N_HEADS)),
             fb=_row(jnp.pad(small["forget_b"], (0, D_F - N_HEADS))))
    h, pa, pf, pb = in_proj_fwd(x, w["g1"], wa, wf, wb)
    qx, kx, vx = qk_prep(pa, pf, w["gq"], w["gk"], w["fb"])
    ox, qxb = attn_fwd(qx, kx, vx)
    xm = mix_out_fwd(x, ox, pb, wao, convw, wco, wpool, w["pscale"], w["w_o"], li)
    return xm, w, dict(x=x, h=h, pa=pa, pf=pf, pb=pb, kx=kx, vx=vx, ox=ox, qxb=qxb)


def ffn_half_fwd(xm, full, li, g2):
    n = full["w_ffn_out"].shape[0]
    w = dict(w1=full["w_ffn_in"], w2=full["w_ffn_out"].reshape(n, D_FF, D_MODEL), at=li, g2=_row(g2))
    y, u = ffn_fwd(xm, w["g2"], w["w1"], w["w2"], li)
    return y, w, dict(xm=xm, u=u)


def ffn_half_bwd(dx, w, s, gi, big):
    n = big["w_ffn_out"].shape[0]
    big = dict(big)
    dxm, du, act, h2, dyb, dg2 = ffn_bwd(s["xm"], w["g2"], dx, s["u"], w["w1"], w["w2"], w["at"])
    big["w_ffn_out"] = wgrad_into(act, dyb, "wgrad_ffn_out", big["w_ffn_out"].reshape(n, D_FF, D_MODEL),
                                  gi).reshape(big["w_ffn_out"].shape)
    big["w_ffn_in"] = wgrad_into(h2, du, "wgrad_ffn_in", big["w_ffn_in"], gi)
    return dxm, big, dict(norm_ffn_g=dg2[0])


def mix_bwd_weights(dxm, w, s, gi, big):
    big = dict(big)
    dpb, dox, dwo, dwao, dwco, dwpool, dconvw, dpscale = mix_out_bwd(
        dxm, s["ox"], s["pb"], w["wao"], w["convw"], w["wco"], w["wpool"], w["pscale"], w["w_o"], w["at"])
    dqx, dkx, dvx = attn_bwd(s["qxb"], s["kx"], s["vx"], dox)
    dpa, dpf, dgq, dgk, dfb = attn_bwd_post(s["pa"], s["pf"], dqx, dkx, dvx, w["gq"], w["gk"], w["fb"])
    big["w_in"] = w_in_unprep(wgrad(s["h"], dpa, "wgrad_in_qkv"), wgrad(s["h"], dpf, "wgrad_in_f"),
                              wgrad(s["h"], dpb, "wgrad_in_b"), big["w_in"], gi)
    big["w_attn_out"], big["w_conv_out"], big["pool_w"], big["w_o"] = branch_g_place(
        dwao, dwco, dwpool, dwo, (big["w_attn_out"], big["w_conv_out"], big["pool_w"], big["w_o"]), gi)
    sm = dict(forget_b=dfb[0, 0:N_HEADS], q_norm_g=dgq.reshape(N_HEADS, HEAD_DIM).sum(0),
              k_norm_g=dgk.reshape(N_HEADS, HEAD_DIM).sum(0), pool_scale=dpscale[0],
              conv_w=dconvw[0:CONV_K].reshape(-1))
    return (dpa, dpf, dpb), big, sm


def mix_bwd_input(dproj, dxm, w, s):
    dx, dg1 = in_proj_bwd(s["x"], w["g1"], dxm, *dproj, w["wa"], w["wf"], w["wb"])
    return dx, dict(norm_mix_g=dg1[0])


def mix_bwd(dxm, w, s, gi, big):
    dproj, big, sm = mix_bwd_weights(dxm, w, s, gi, big)
    dx, g1 = mix_bwd_input(dproj, dxm, w, s)
    return dx, big, {**sm, **g1}


def grad_buffers(full):
    return {n: lax.empty(full[n].shape, F32) for n in SPLIT}


def local_step(x, target, gathered, small):
    n_layers = small["norm_mix_g"].shape[0]
    done = []
    for l in range(n_layers):
        xm, wm, sm = mix_fwd(x, gathered, l, {n: v[l] for n, v in small.items()})
        x, wf, sf = ffn_half_fwd(xm, gathered, l, small["norm_ffn_g"][l])
        done.append((wm, sm, wf, sf))
    loss, dx = loss_head(x, target)
    big = grad_buffers(gathered)
    small_grads = [None] * n_layers
    for l in reversed(range(n_layers)):
        wm, sm, wf, sf = done[l]
        dxm, ffn_big, g_ffn = ffn_half_bwd(dx, wf, sf, l, {n: big[n] for n in FFN_W})
        dx, mix_big, g_mix = mix_bwd(dxm, wm, sm, l, {n: big[n] for n in MIX_W})
        big = {**ffn_big, **mix_big}
        small_grads[l] = {**g_ffn, **g_mix}
    return loss, dx, big, {n: jnp.stack([g[n] for g in small_grads]) for n in SMALL}


def adamw(w, g, m, v, name):
    R, C = w.shape
    tm = 256 if R % 256 == 0 else R

    def body(w_ref, g_ref, m_ref, v_ref, d_ref, nm_ref, nv_ref):
        gr = g_ref[...]
        m_new = ADAM_B1 * m_ref[...] + (1.0 - ADAM_B1) * gr
        v_new = ADAM_B2 * v_ref[...] + (1.0 - ADAM_B2) * jnp.square(gr)
        nm_ref[...] = m_new
        nv_ref[...] = v_new
        m_hat = m_new / (1.0 - ADAM_B1 ** ADAM_STEP)
        v_hat = v_new / (1.0 - ADAM_B2 ** ADAM_STEP)
        d_ref[...] = -ADAM_LR * (m_hat / (jnp.sqrt(v_hat) + ADAM_EPS) + ADAM_WD * w_ref[...])

    spec = _rows(tm, C)
    out = jax.ShapeDtypeStruct((R, C), F32)
    return pl.pallas_call(
        body,
        grid=(R // tm,),
        in_specs=[spec] * 4,
        out_specs=[spec] * 3,
        out_shape=[out] * 3,
        compiler_params=_params("parallel"),
        name=name,
    )(w, g, m, v)


MESH = pl.DeviceIdType.MESH
HBM_REF = pl.BlockSpec(memory_space=pl.ANY)
N_CHIPS = 4
N_DEV = 8
SMALL_SHAPE = (128, LANES)


def _mesh_pos():
    return lax.axis_index("x"), lax.axis_index("y"), lax.axis_index("c")


def _other_chips(x, y):
    return [(1 - x, y), (x, 1 - y), (1 - x, 1 - y)]


def _remote(src, dst, send_sem, recv_sem, to):
    return pltpu.make_async_remote_copy(src_ref=src, dst_ref=dst, send_sem=send_sem, recv_sem=recv_sem,
                                        device_id=to, device_id_type=MESH)


def _row_tile(rows):
    for tm in (256, 176, 128):
        if rows % tm == 0:
            return tm
    return rows


def _as4(a):
    return a.reshape(a.shape[0], a.shape[1], -1, a.shape[-1])


def _core_rows(buf, c):
    R, C = buf.shape[-2:]
    if R % 2:
        whole = (pl.ds(0, R), pl.ds(0, C))
        return whole, whole, False
    return (pl.ds(c * (R // 2), R // 2), pl.ds(0, C)), (pl.ds((1 - c) * (R // 2), R // 2), pl.ds(0, C)), True


def _half_shape(g):
    return g.shape[:-2] + (g.shape[-2] // 2, g.shape[-1])


def place_shard(chip, w, lo, hi, dtype, name, after=None):
    w3 = w.reshape(w.shape[0], -1, w.shape[-1])
    _, R, C = w3.shape
    tm = _row_tile(R)
    idle = [] if after is None else [after]

    def body(chip_ref, w_ref, *rest):
        rest[-1][...] = w_ref[...].astype(dtype)

    return pl.pallas_call(
        body,
        grid_spec=pltpu.PrefetchScalarGridSpec(
            num_scalar_prefetch=1,
            grid=(hi - lo, R // tm),
            in_specs=[pl.BlockSpec((None, tm, C), lambda l, i, chip: (lo + l, i, 0))] + [_whole(a) for a in idle],
            out_specs=pl.BlockSpec((None, None, tm, C), lambda l, i, chip: (l, chip[0], i, 0)),
        ),
        out_shape=jax.ShapeDtypeStruct((hi - lo, N_CHIPS, R, C), dtype),
        compiler_params=_params("parallel", "parallel"),
        name=name,
    )(chip, w3, *idle)


HBM_SPACE = pl.BlockSpec(memory_space=pltpu.HBM)
SEM_SPACE = pl.BlockSpec(memory_space=pltpu.SEMAPHORE)
IN_FLIGHT = pltpu.SideEffectType.DATAFLOW_SIDE_EFFECTING


def _sem_table(send_sems, recv_sems):
    return lambda t, j: (send_sems.at[t, j], recv_sems.at[t, j])


def _sem_per_peer(send_sems, recv_sems):
    return lambda t, j: (send_sems[j], recv_sems[j])


def _gather_ici(bufs, sem, sends=True, lands=True):
    x, y, c = _mesh_pos()
    me = 2 * x + y
    out, into = [], []
    for t, buf in enumerate(bufs):
        mine, _, _ = _core_rows(buf, c)
        part = lambda k, buf=buf, mine=mine: buf.at[pl.ds(0, buf.shape[0]), k, *mine]
        for j, (px, py) in enumerate(_other_chips(x, y)):
            if sends:
                out.append(_remote(part(me), part(me), *sem(t, j), (px, py, c)))
            if lands:
                into.append(_remote(part(2 * px + py), part(2 * px + py), *sem(t, j), (px, py, c)))
    return out, into


def _gather_d2d(bufs, send_sems, recv_sems, first):
    x, y, c = _mesh_pos()
    sends, lands = [], []
    for t, buf in enumerate(bufs):
        mine, theirs, split = _core_rows(buf, c)
        if not split:
            continue
        for j, (px, py) in enumerate(_other_chips(x, y)):
            part = lambda half, buf=buf, k=2 * px + py: buf.at[pl.ds(0, buf.shape[0]), k, *half]
            sems = (send_sems.at[t, first + j], recv_sems.at[t, first + j], (x, y, 1 - c))
            sends.append(_remote(part(mine), part(mine), *sems))
            lands.append(_remote(part(theirs), part(theirs), *sems))
    return sends, lands


def gather_start(bufs):
    n = len(bufs)

    def body(*refs):
        send_sems, recv_sems = refs[n:n + 3], refs[n + 3:n + 6]
        outs = refs[n + 6:2 * n + 6]
        token = refs[2 * n + 6]
        for cp in _gather_ici(outs, _sem_per_peer(send_sems, recv_sems), lands=False)[0]:
            cp.start()
        token[...] = jnp.zeros_like(token)

    res = pl.pallas_call(
        body,
        in_specs=[HBM_SPACE] * n,
        out_specs=[SEM_SPACE] * 6 + [HBM_SPACE] * n + [pl.BlockSpec(memory_space=pltpu.VMEM)],
        out_shape=[pltpu.SemaphoreType.DMA(())] * 6
        + [pltpu.HBM(b.shape, b.dtype) for b in bufs] + [jax.ShapeDtypeStruct((8, LANES), F32)],
        input_output_aliases={t: t + 6 for t in range(n)},
        compiler_params=pltpu.CompilerParams(has_side_effects=IN_FLIGHT),
        name="gather_start",
    )(*[pltpu.with_memory_space_constraint(b, pltpu.HBM) for b in bufs])
    return list(res[0:3]), list(res[3:6]), list(res[6:n + 6]), res[n + 6]


def gather_wait(send_sems, recv_sems, bufs, after):
    n = len(bufs)

    def body(*refs):
        ins_send, ins_recv = refs[n:n + 3], refs[n + 3:n + 6]
        outs = refs[n + 7:]
        sends, lands = _gather_ici(outs, _sem_per_peer(ins_send, ins_recv))
        for cp in lands:
            cp.wait_recv()
        for cp in sends:
            cp.wait_send()

    return list(pl.pallas_call(
        body,
        in_specs=[HBM_SPACE] * n + [SEM_SPACE] * 6 + [HBM_REF],
        out_specs=[HBM_SPACE] * n,
        out_shape=[pltpu.HBM(b.shape, b.dtype) for b in bufs],
        input_output_aliases={t: t for t in range(n)},
        compiler_params=pltpu.CompilerParams(has_side_effects=IN_FLIGHT),
        name="gather_wait",
    )(*bufs, *send_sems, *recv_sems, after))


def gather_forward(bufs):
    n = len(bufs)

    def body(*refs):
        outs = refs[n:2 * n]
        send_sems, recv_sems = refs[2 * n:]
        sends, lands = _gather_d2d(outs, send_sems, recv_sems, 0)
        for cp in sends:
            cp.start()
        for cp in lands:
            cp.wait_recv()
        for cp in sends:
            cp.wait_send()

    return list(pl.pallas_call(
        body,
        in_specs=[HBM_REF] * n,
        out_specs=[HBM_REF] * n,
        out_shape=[jax.ShapeDtypeStruct(b.shape, b.dtype) for b in bufs],
        input_output_aliases={t: t for t in range(n)},
        scratch_shapes=[pltpu.SemaphoreType.DMA((n, 3)), pltpu.SemaphoreType.DMA((n, 3))],
        name="gather_forward",
    )(*bufs))


def pair_exchange(grads):
    n = len(grads)

    def body(*refs):
        ins, outs = refs[:n], refs[n:2 * n]
        send_sems, recv_sems = refs[2 * n:]
        cps = _pair_copies(ins, outs, lambda t: (send_sems.at[t], recv_sems.at[t]))
        for cp in cps:
            cp.start()
        for cp in cps:
            cp.wait()

    return list(pl.pallas_call(
        body,
        in_specs=[HBM_REF] * n,
        out_specs=[HBM_REF] * n,
        out_shape=[jax.ShapeDtypeStruct(_half_shape(g), g.dtype) for g in grads],
        scratch_shapes=[pltpu.SemaphoreType.DMA((n,)), pltpu.SemaphoreType.DMA((n,))],
        name="pair_exchange",
    )(*grads))


def _pair_copies(grads, lands, sem):
    x, y, c = _mesh_pos()
    cps = []
    for t in range(len(grads)):
        _, theirs, _ = _core_rows(grads[t], c)
        src = grads[t].at[pl.ds(0, grads[t].shape[0]), pl.ds(0, N_CHIPS), *theirs]
        cps.append(_remote(src, lands[t], *sem(t), (x, y, 1 - c)))
    return cps


def pair_exchange_start(grads):
    n = len(grads)
    lands = [lax.empty(_half_shape(g), g.dtype) for g in grads]

    def body(*refs):
        send_sem, recv_sem = refs[2 * n:2 * n + 2]
        outs = refs[2 * n + 2:4 * n + 2]
        token = refs[4 * n + 2]
        for cp in _pair_copies(outs[:n], outs[n:], lambda t: (send_sem, recv_sem)):
            cp.start()
        token[...] = jnp.zeros_like(token)

    res = pl.pallas_call(
        body,
        in_specs=[HBM_SPACE] * (2 * n),
        out_specs=[SEM_SPACE] * 2 + [HBM_SPACE] * (2 * n) + [pl.BlockSpec(memory_space=pltpu.VMEM)],
        out_shape=[pltpu.SemaphoreType.DMA(())] * 2 + [pltpu.HBM(a.shape, a.dtype) for a in list(grads) + lands]
        + [jax.ShapeDtypeStruct((8, LANES), F32)],
        input_output_aliases={t: t + 2 for t in range(2 * n)},
        compiler_params=pltpu.CompilerParams(has_side_effects=IN_FLIGHT),
        name="pair_exchange_start",
    )(*[pltpu.with_memory_space_constraint(a, pltpu.HBM) for a in list(grads) + lands])
    return res[0], res[1], list(res[2:n + 2]), list(res[n + 2:2 * n + 2]), res[2 * n + 2]


def pair_exchange_wait(send_sem, recv_sem, grads, lands, after):
    n = len(grads)

    def body(*refs):
        in_send, in_recv = refs[2 * n:2 * n + 2]
        outs = refs[2 * n + 3:]
        for cp in _pair_copies(outs[:n], outs[n:], lambda t: (in_send, in_recv)):
            cp.wait()

    res = pl.pallas_call(
        body,
        in_specs=[HBM_SPACE] * (2 * n) + [SEM_SPACE] * 2 + [HBM_REF],
        out_specs=[HBM_SPACE] * (2 * n),
        out_shape=[pltpu.HBM(a.shape, a.dtype) for a in list(grads) + list(lands)],
        input_output_aliases={t: t for t in range(2 * n)},
        compiler_params=pltpu.CompilerParams(has_side_effects=IN_FLIGHT),
        name="pair_exchange_wait",
    )(*grads, *lands, send_sem, recv_sem, after)
    return list(res[:n]), list(res[n:])


def pair_sum(core, g, t, name):
    n, _, hr, hc = t.shape
    R, C = g.shape[2:]
    tm = _row_tile(hr)
    per = hr // tm

    def body(c_ref, g_ref, t_ref, o_ref):
        o_ref[...] = (g_ref[...] + t_ref[...]).astype(BF16)

    tile = pl.BlockSpec((None, tm, hc), lambda a, i, c: (a, i, 0))
    out = pl.pallas_call(
        body,
        grid_spec=pltpu.PrefetchScalarGridSpec(
            num_scalar_prefetch=1,
            grid=(n * N_CHIPS, per),
            in_specs=[pl.BlockSpec((None, tm, hc), lambda a, i, c: (a, per * c[0] + i, 0)), tile],
            out_specs=tile,
        ),
        out_shape=jax.ShapeDtypeStruct((n * N_CHIPS, hr, hc), BF16),
        compiler_params=_params("parallel", "parallel"),
        name=name,
    )(core, g.reshape(n * N_CHIPS, R, C), t.reshape(n * N_CHIPS, hr, hc))
    return out.reshape(t.shape)


def _chip_copies(sums, recv, sem):
    x, y, c = _mesh_pos()
    cps = []
    for t in range(len(sums)):
        for j, (px, py) in enumerate(_other_chips(x, y)):
            src = sums[t].at[pl.ds(0, sums[t].shape[0]), 2 * px + py]
            cps.append(_remote(src, recv[t].at[j], *sem(t, j), (px, py, c)))
    return cps


def _recv_shape(s):
    return (N_CHIPS - 1, s.shape[0]) + s.shape[2:]


def chip_exchange(sums):
    n = len(sums)

    def body(*refs):
        ins, outs = refs[:n], refs[n:2 * n]
        cps = _chip_copies(ins, outs, _sem_table(*refs[2 * n:]))
        for cp in cps:
            cp.start()
        for cp in cps:
            cp.wait()

    return list(pl.pallas_call(
        body,
        in_specs=[HBM_REF] * n,
        out_specs=[HBM_REF] * n,
        out_shape=[jax.ShapeDtypeStruct(_recv_shape(s), s.dtype) for s in sums],
        scratch_shapes=[pltpu.SemaphoreType.DMA((n, 3)), pltpu.SemaphoreType.DMA((n, 3))],
        name="chip_exchange",
    )(*sums))


def chip_exchange_start(sums):
    n = len(sums)
    lands = [lax.empty(_recv_shape(s), s.dtype) for s in sums]

    def body(*refs):
        send_sems, recv_sems = refs[2 * n:2 * n + 3], refs[2 * n + 3:2 * n + 6]
        outs = refs[2 * n + 6:4 * n + 6]
        token = refs[4 * n + 6]
        for cp in _chip_copies(outs[:n], outs[n:2 * n], _sem_per_peer(send_sems, recv_sems)):
            cp.start()
        token[...] = jnp.zeros_like(token)

    res = pl.pallas_call(
        body,
        in_specs=[HBM_SPACE] * (2 * n),
        out_specs=[SEM_SPACE] * 6 + [HBM_SPACE] * (2 * n) + [pl.BlockSpec(memory_space=pltpu.VMEM)],
        out_shape=[pltpu.SemaphoreType.DMA(())] * 6 + [pltpu.HBM(a.shape, a.dtype) for a in list(sums) + lands]
        + [jax.ShapeDtypeStruct((8, LANES), F32)],
        input_output_aliases={t: t + 6 for t in range(2 * n)},
        compiler_params=pltpu.CompilerParams(has_side_effects=IN_FLIGHT),
        name="chip_exchange_start",
    )(*[pltpu.with_memory_space_constraint(a, pltpu.HBM) for a in list(sums) + lands])
    return list(res[0:3]), list(res[3:6]), list(res[6:n + 6]), list(res[n + 6:2 * n + 6]), res[2 * n + 6]


def chip_exchange_wait(send_sems, recv_sems, sums, lands, after):
    n = len(sums)

    def body(*refs):
        ins_send, ins_recv = refs[2 * n:2 * n + 3], refs[2 * n + 3:2 * n + 6]
        outs = refs[2 * n + 7:]
        for cp in _chip_copies(outs[:n], outs[n:], _sem_per_peer(ins_send, ins_recv)):
            cp.wait()

    res = pl.pallas_call(
        body,
        in_specs=[HBM_SPACE] * (2 * n) + [SEM_SPACE] * 6 + [HBM_REF],
        out_specs=[HBM_SPACE] * (2 * n),
        out_shape=[pltpu.HBM(a.shape, a.dtype) for a in list(sums) + list(lands)],
        input_output_aliases={t: t for t in range(2 * n)},
        compiler_params=pltpu.CompilerParams(has_side_effects=IN_FLIGHT),
        name="chip_exchange_wait",
    )(*sums, *lands, *send_sems, *recv_sems, after)
    return list(res[:n]), list(res[n:])


def chip_sum_into(core, chip, recv, sums, total, lo, name):
    _, n, hr, hc = recv.shape
    tm = _row_tile(hr)
    per = hr // tm

    def body(c_ref, k_ref, r_ref, s_ref, t_ref, o_ref):
        acc = s_ref[...].astype(F32)
        for j in range(N_CHIPS - 1):
            acc = acc + r_ref[j].astype(F32)
        o_ref[...] = acc

    return pl.pallas_call(
        body,
        grid_spec=pltpu.PrefetchScalarGridSpec(
            num_scalar_prefetch=2,
            grid=(n, per),
            in_specs=[pl.BlockSpec((N_CHIPS - 1, None, tm, hc), lambda a, i, c, k: (0, a, i, 0)),
                      pl.BlockSpec((None, None, tm, hc), lambda a, i, c, k: (a, k[0], i, 0)),
                      HBM_REF],
            out_specs=pl.BlockSpec((None, tm, hc), lambda a, i, c, k: (lo + a, per * c[0] + i, 0)),
        ),
        out_shape=jax.ShapeDtypeStruct(total.shape, F32),
        input_output_aliases={4: 0},
        compiler_params=_params("parallel", "parallel"),
        name=name,
    )(core, chip, recv, sums, total)


def sibling_share(totals):
    n = len(totals)

    def body(*refs):
        outs = refs[n:2 * n]
        send_sems, recv_sems = refs[2 * n:]
        x, y, c = _mesh_pos()
        half = lambda t, which: outs[t].at[pl.ds(0, DEPTH), *_core_rows(outs[t], c)[which]]
        sent = [_remote(half(t, 0), half(t, 0), send_sems.at[t], recv_sems.at[t], (x, y, 1 - c)) for t in range(n)]
        for cp in sent:
            cp.start()
        for t in range(n):
            _remote(half(t, 1), half(t, 1), send_sems.at[t], recv_sems.at[t], (x, y, 1 - c)).wait_recv()
        for cp in sent:
            cp.wait_send()

    return pl.pallas_call(
        body,
        in_specs=[HBM_REF] * n,
        out_specs=[HBM_REF] * n,
        out_shape=[jax.ShapeDtypeStruct(t.shape, t.dtype) for t in totals],
        input_output_aliases={t: t for t in range(n)},
        scratch_shapes=[pltpu.SemaphoreType.DMA((n,)), pltpu.SemaphoreType.DMA((n,))],
        name="sibling_share",
    )(*totals)


def small_allgather(small):
    def body(s_ref, a_ref, send_sems, recv_sems, local_sem):
        x, y, c = _mesh_pos()
        me = 4 * x + 2 * y + c
        own = pltpu.make_async_copy(s_ref, a_ref.at[me], local_sem)
        own.start()
        sent = []
        for k in range(1, N_DEV):
            peer = (x ^ (k >> 2), y ^ ((k >> 1) & 1), c ^ (k & 1))
            cp = _remote(s_ref, a_ref.at[me], send_sems.at[k - 1], recv_sems.at[k - 1], peer)
            cp.start()
            sent.append(cp)
        for k in range(1, N_DEV):
            px, py, pc = x ^ (k >> 2), y ^ ((k >> 1) & 1), c ^ (k & 1)
            _remote(s_ref, a_ref.at[4 * px + 2 * py + pc], send_sems.at[k - 1], recv_sems.at[k - 1], (px, py, pc)).wait_recv()
        for cp in sent:
            cp.wait_send()
        own.wait()

    return pl.pallas_call(
        body,
        in_specs=[HBM_REF],
        out_specs=HBM_REF,
        out_shape=jax.ShapeDtypeStruct((N_DEV,) + SMALL_SHAPE, small.dtype),
        scratch_shapes=[pltpu.SemaphoreType.DMA((N_DEV - 1,)), pltpu.SemaphoreType.DMA((N_DEV - 1,)), pltpu.SemaphoreType.DMA],
        name="small_allgather",
    )(small)


def small_sum(blocks):
    def body(a_ref, o_ref):
        acc = a_ref[0]
        for d in range(1, N_DEV):
            acc = acc + a_ref[d]
        o_ref[...] = acc

    return pl.pallas_call(
        body,
        in_specs=[pl.BlockSpec(memory_space=pltpu.VMEM)],
        out_specs=pl.BlockSpec(memory_space=pltpu.VMEM),
        out_shape=jax.ShapeDtypeStruct(SMALL_SHAPE, F32),
        name="small_sum",
    )(blocks)


def pack_small(grads, loss):
    flat = jnp.concatenate([grads[n].reshape(-1) for n in SMALL] + [loss.reshape(-1)])
    size = SMALL_SHAPE[0] * SMALL_SHAPE[1]
    return jnp.pad(flat, (0, size - flat.shape[0])).reshape(SMALL_SHAPE)


def unpack_small(packed):
    flat = packed.reshape(-1)
    out, off = {}, 0
    for n, size in SMALL.items():
        out[n] = flat[off:off + DEPTH * size].reshape(DEPTH, size)
        off += DEPTH * size
    return out, flat[off]


def kernel(x, norm_mix_g, w_in, forget_b, q_norm_g, k_norm_g, w_attn_out, conv_w, w_conv_out, pool_w, pool_scale, w_o, norm_ffn_g, w_ffn_in, w_ffn_out, loss_target, m_norm_mix_g, m_w_in, m_forget_b, m_q_norm_g, m_k_norm_g, m_w_attn_out, m_conv_w, m_w_conv_out, m_pool_w, m_pool_scale, m_w_o, m_norm_ffn_g, m_w_ffn_in, m_w_ffn_out, v_norm_mix_g, v_w_in, v_forget_b, v_q_norm_g, v_k_norm_g, v_w_attn_out, v_conv_w, v_w_conv_out, v_pool_w, v_pool_scale, v_w_o, v_norm_ffn_g, v_w_ffn_in, v_w_ffn_out):
    given = dict(locals())
    weights = {n: given[n] for n in WEIGHTS}

    core = lax.axis_index("c").astype(jnp.int32)
    chip = (2 * lax.axis_index("x") + lax.axis_index("y")).astype(jnp.int32)

    core, chip = core.reshape(1), chip.reshape(1)
    mix_names, all_names = MIX_W + ["conv_w"], MIX_W + FFN_W + ["conv_w"]

    def placed(lo, hi, names, after=None):
        piece = None
        if after is not None:
            piece = _as4(next(iter(after.values())))[0, 0, 0:HALO, 0:LANES] if isinstance(after, dict) else after
        return [place_shard(chip, weights[n], lo, hi, F32 if n == "conv_w" else BF16, "place_" + n, piece)
                for n in names]

    def as_weights(bufs, names):
        return {n: b.reshape(b.shape[:2] + weights[n].shape[1:]) for n, b in zip(names, bufs)}

    def landed(start, names, after):
        return as_weights(gather_forward(gather_wait(*start[:3], after)), names)

    def layer_small(l, *tokens):
        small = {n: weights[n][l] for n in SMALL if n != "conv_w"}
        for t in tokens:
            small["norm_mix_g"] = small["norm_mix_g"] + t[0, 0]
        return small

    done = [None] * DEPTH
    first = gather_start(placed(0, 1, mix_names))
    next_bufs = (placed(0, 1, FFN_W, after=first[3]), placed(1, 2, all_names, after=first[3]))
    first = landed(first, mix_names, next_bufs[1][0])
    (ffn0, layer1), _ = lax.optimization_barrier((next_bufs, first["conv_w"]))
    ffn0, layer1 = gather_start(ffn0), gather_start(layer1)
    xm, wm, sm = mix_fwd(x[0], first, 0, layer_small(0, ffn0[3], layer1[3]))
    xs, wf, sf = ffn_half_fwd(xm, landed(ffn0, FFN_W, xm), 0, weights["norm_ffn_g"][0])
    done[0] = (wm, sm, wf, sf)
    full = landed(layer1, all_names, xs)
    rest = gather_start(placed(2, DEPTH, all_names, after=full))
    xm, wm, sm = mix_fwd(xs, full, 0, layer_small(1, rest[3]))
    xs, wf, sf = ffn_half_fwd(xm, full, 0, weights["norm_ffn_g"][1])
    done[1] = (wm, sm, wf, sf)
    full = landed(rest, all_names, xs)
    for l in range(2, DEPTH):
        xm, wm, sm = mix_fwd(xs, full, l - 2, layer_small(l))
        xs, wf, sf = ffn_half_fwd(xm, full, l - 2, weights["norm_ffn_g"][l])
        done[l] = (wm, sm, wf, sf)
    loss, dx = loss_head(xs, loss_target[0])

    small_grads = [None] * DEPTH
    totals = {n: lax.empty((DEPTH,) + _as4(weights[n][None]).shape[2:], F32) for n in SPLIT}

    def buffers(names, n_layers):
        return {n: lax.empty((n_layers, N_CHIPS) + weights[n].shape[1:], F32) for n in names}

    def pair_sums(names, grads, theirs):
        return [pair_sum(core, a, t, "pair_sum_" + n) for n, a, t in zip(names, grads, theirs)]

    def add_chips(names, recv, sums, lo):
        for n, r, s in zip(names, recv, sums):
            totals[n] = chip_sum_into(core, chip, r, s, totals[n], lo, "chip_sum_" + n)

    big = buffers(SPLIT, DEPTH - 1)
    for l in reversed(range(1, DEPTH)):
        wm, sm, wf, sf = done[l]
        dxm, ffn_big, g_ffn = ffn_half_bwd(dx, wf, sf, l - 1, {n: big[n] for n in FFN_W})
        dx, mix_big, g_mix = mix_bwd(dxm, wm, sm, l - 1, {n: big[n] for n in MIX_W})
        big = {**ffn_big, **mix_big}
        small_grads[l] = {**g_ffn, **g_mix}
    pair = pair_exchange_start([_as4(big[n]) for n in SPLIT])

    wm, sm, wf, sf = done[0]
    wf = dict(wf, g2=wf["g2"] + pair[4][0:1, 0:1])
    dxm, ffn_big, g_ffn = ffn_half_bwd(dx, wf, sf, 0, buffers(FFN_W, 1))
    chips = chip_exchange_start(pair_sums(SPLIT, *pair_exchange_wait(*pair[:4], ffn_big["w_ffn_in"])))
    ffn_grads = [_as4(ffn_big[n]) for n in FFN_W]
    ffn_chips = chip_exchange_start(pair_sums(FFN_W, ffn_grads, pair_exchange(ffn_grads)))
    wm = dict(wm, pscale=wm["pscale"] + chips[4][0:1, 0:1] + ffn_chips[4][0:1, 0:1])
    dproj, mix_big, g_mix = mix_bwd_weights(dxm, wm, sm, 0, buffers(MIX_W, 1))
    mix_grads = [_as4(mix_big[n]) for n in MIX_W]
    mix_chips = chip_exchange_start(pair_sums(MIX_W, mix_grads, pair_exchange(mix_grads)))
    dx, g_in = mix_bwd_input(dproj, dxm, dict(wm, g1=wm["g1"] + mix_chips[4][0:1, 0:1]), sm)
    small_grads[0] = {**g_ffn, **g_mix, **g_in}

    reduced, deltas, new_m, new_v = {}, {}, {}, {}

    def update(names):
        for n in names:
            w = weights[n]
            flat = (-1, w.shape[-1])
            d, nm, nv = adamw(w.reshape(flat), reduced[n].reshape(flat), given["m_" + n].reshape(flat),
                              given["v_" + n].reshape(flat), "adamw_" + n)
            deltas[n], new_m[n], new_v[n] = d.reshape(w.shape), nm.reshape(w.shape), nv.reshape(w.shape)

    def share(names):
        for n, t in zip(names, sibling_share([totals[n] for n in names])):
            reduced[n] = t.reshape(weights[n].shape)

    sums, recv = chip_exchange_wait(*chips[:4], dx)
    add_chips(SPLIT, recv, sums, 1)
    sums, recv = chip_exchange_wait(*ffn_chips[:4], dx)
    add_chips(FFN_W, recv, sums, 0)
    share(FFN_W)
    update(FFN_W)
    sums, recv = chip_exchange_wait(*mix_chips[:4], deltas["w_ffn_in"])
    add_chips(MIX_W, recv, sums, 0)
    share(MIX_W)

    small_grads = {n: jnp.stack([g[n] for g in small_grads]) for n in SMALL}
    small_total, loss_sum = unpack_small(small_sum(small_allgather(pack_small(small_grads, loss))))
    chip = chip[0]
    cols = D_CONV // N_CHIPS
    small_total["conv_w"] = lax.dynamic_slice_in_dim(small_total["conv_w"].reshape(DEPTH, CONV_K, D_CONV), chip * cols, cols, axis=2)
    for n in SMALL:
        reduced[n] = small_total[n].reshape(weights[n].shape)
    update(MIX_W + list(SMALL))

    return (loss_sum, dx[None], *[reduced[n] for n in WEIGHTS], *[deltas[n] for n in WEIGHTS],
            *[new_m[n] for n in WEIGHTS], *[new_v[n] for n in WEIGHTS])
```

```python
import functools

import numpy as np
import jax
import jax.numpy as jnp
from jax import lax
from jax.experimental import pallas as pl
from jax.experimental.pallas import tpu as pltpu

F32 = jnp.float32
BF16 = jnp.bfloat16

D_MODEL = 1024
DEPTH = 4
HEAD_DIM = 64
N_HEADS = 8
D_ATTN = 512
D_CONV = 256
D_POOL = 256
D_FF = 2816
D_IN = 5640
CONV_K = 3
POOL_WINDOWS = (2, 4, 8, 16)
N_GROUPS = len(POOL_WINDOWS)
EPS = 1e-6
ADAM_LR, ADAM_B1, ADAM_B2, ADAM_EPS, ADAM_WD, ADAM_STEP = 0.001, 0.9, 0.999, 1e-08, 0.01, 10

D_QKV = 3 * D_ATTN
D_F = 128
D_B = 3 * D_CONV + D_POOL + 3 * D_MODEL
D_LOCAL = 3 * D_CONV + D_POOL

LANES = 128
D_HEADS = N_HEADS * LANES
HALO = 16
VMEM_LIMIT = 56 * 1024 * 1024
NEG = -1e30
LOG2E = 1.4426950408889634
LN2 = 0.6931471805599453

TM = 256
TM_MIX = 256
TM_PROJ = 512
TM_MIX_FWD = 512
TQ = 1024

LANE_C = 64
LANE_ONE = 67
LANE_LSE = 70
N_PIECES = 3


def _dot(a, b):
    return jnp.dot(a, b, preferred_element_type=F32)


def _dot_nt(a, b):
    return lax.dot_general(a, b, (((1,), (1,)), ((), ())), preferred_element_type=F32)


def _dot_tn(a, b):
    return lax.dot_general(a, b, (((0,), (0,)), ((), ())), preferred_element_type=F32)


def _params(*sem):
    return pltpu.CompilerParams(dimension_semantics=sem, vmem_limit_bytes=VMEM_LIMIT)


def _rows(tm, n):
    return pl.BlockSpec((tm, n), lambda i, *_: (i, 0))


def _whole(a):
    nd = a.ndim
    return pl.BlockSpec(a.shape, lambda *_: (0,) * nd)


def _whole_once(a):
    nd = a.ndim
    return pl.BlockSpec(a.shape, lambda *_: (0,) * nd, pipeline_mode=pl.Buffered(1))


def _layer(shape):
    nd = len(shape)
    return pl.BlockSpec((None,) + tuple(shape), lambda *a: (a[-1][0],) + (0,) * nd)


def _layer_index(l):
    return jnp.full((1,), l, jnp.int32)


def _split_bf16(x):
    hi = x.astype(BF16)
    lo = (x - hi.astype(F32)).astype(BF16)
    return hi, lo


def _pieces(x):
    p1 = x.astype(BF16)
    r1 = x - p1.astype(F32)
    p2 = r1.astype(BF16)
    p3 = (r1 - p2.astype(F32)).astype(BF16)
    return p1, p2, p3


def _sigmoid(x):
    return 0.5 * jnp.tanh(0.5 * x) + 0.5


def w_in_prep(win, l):
    tr = 256
    n = D_IN // 4
    v_rest = D_QKV - n
    b0 = v_rest + N_HEADS

    def body(s0, s1, s2, s3, wa_ref, wf_ref, wb_ref):
        b = s1[...]
        wa_ref[...] = jnp.concatenate([s0[...], b[:, 0:v_rest]], axis=1)
        wf_ref[...] = jnp.concatenate([b[:, v_rest:b0], jnp.zeros((tr, D_F - N_HEADS), b.dtype)], axis=1)
        wb_ref[...] = jnp.concatenate([b[:, b0:n], s2[...], s3[...]], axis=1)

    shard = lambda j: pl.BlockSpec((None, None, tr, n), lambda i: (l, j, i, 0))
    return pl.pallas_call(
        body,
        grid=(D_MODEL // tr,),
        in_specs=[shard(0), shard(1), shard(2), shard(3)],
        out_specs=[_rows(tr, D_QKV), _rows(tr, D_F), _rows(tr, D_B)],
        out_shape=[jax.ShapeDtypeStruct((D_MODEL, D_QKV), win.dtype), jax.ShapeDtypeStruct((D_MODEL, D_F), win.dtype),
                   jax.ShapeDtypeStruct((D_MODEL, D_B), win.dtype)],
        compiler_params=_params("parallel"),
        name="w_in_prep",
    )(win, win, win, win)


def _into_layer(buf):
    return dict(in_spec=HBM_REF, out_shape=jax.ShapeDtypeStruct(buf.shape, buf.dtype), aliases={1: 0})


def w_in_unprep(dwa, dwf, dwb, buf, l):
    tr = 256
    n = D_IN // 4
    v_rest = D_QKV - n
    b1 = n - v_rest - N_HEADS
    place = _into_layer(buf)

    def body(l_ref, buf_ref, a_ref, f_ref, b_ref, o_ref):
        a = a_ref[...]
        b = b_ref[...]
        o_ref[0] = a[:, 0:n]
        o_ref[1] = jnp.concatenate([a[:, n:D_QKV], f_ref[:, 0:N_HEADS], b[:, 0:b1]], axis=1)
        o_ref[2] = b[:, b1:b1 + n]
        o_ref[3] = b[:, b1 + n:D_B]

    return pl.pallas_call(
        body,
        grid_spec=pltpu.PrefetchScalarGridSpec(
            num_scalar_prefetch=1,
            grid=(D_MODEL // tr,),
            in_specs=[place["in_spec"], _rows(tr, D_QKV), _rows(tr, D_F), _rows(tr, D_B)],
            out_specs=pl.BlockSpec((None, 4, tr, n), lambda i, l: (l[0], 0, i, 0)),
        ),
        out_shape=place["out_shape"],
        input_output_aliases=place["aliases"],
        compiler_params=_params("parallel"),
        name="w_in_unprep",
    )(_layer_index(l), buf, dwa, dwf, dwb)


def branch_w_prep(wao, wco, pw, cw, l):
    gd = D_POOL // N_GROUPS
    od = D_MODEL // N_GROUPS

    def body(wao_ref, wco_ref, pw_ref, cw_ref, ao_ref, co_ref, po_ref, co8_ref):
        ao_ref[...] = jnp.concatenate([wao_ref[j] for j in range(4)], axis=1)
        co_ref[...] = jnp.concatenate([wco_ref[j] for j in range(4)], axis=1)
        zero = jnp.zeros((gd, od), co_ref.dtype)
        po_ref[...] = jnp.concatenate(
            [jnp.concatenate([jnp.concatenate([pw_ref[j, g] for j in range(4)], axis=1) if g2 == g else zero
                              for g2 in range(N_GROUPS)], axis=1) for g in range(N_GROUPS)], axis=0)
        co8_ref[...] = jnp.zeros_like(co8_ref)
        co8_ref[0:CONV_K, :] = jnp.concatenate([cw_ref[j] for j in range(4)], axis=1)

    sel = lambda *shape: pl.BlockSpec((None,) + shape, lambda i: (l,) + (0,) * len(shape))
    return pl.pallas_call(
        body,
        grid=(1,),
        in_specs=[sel(4, D_ATTN, D_MODEL // 4), sel(4, D_CONV, D_MODEL // 4), sel(4, N_GROUPS, gd, od // 4),
                  sel(4, CONV_K, D_CONV // 4)],
        out_specs=[pl.BlockSpec((D_ATTN, D_MODEL), lambda i: (0, 0)), pl.BlockSpec((D_CONV, D_MODEL), lambda i: (0, 0)),
                   pl.BlockSpec((D_POOL, D_MODEL), lambda i: (0, 0)), pl.BlockSpec((8, D_CONV), lambda i: (0, 0))],
        out_shape=[jax.ShapeDtypeStruct((D_ATTN, D_MODEL), BF16), jax.ShapeDtypeStruct((D_CONV, D_MODEL), BF16),
                   jax.ShapeDtypeStruct((D_POOL, D_MODEL), BF16), jax.ShapeDtypeStruct((8, D_CONV), F32)],
        name="branch_w_prep",
    )(wao, wco, pw, cw)


def branch_g_place(dwao, dwco, dwpool, dwo, bufs, l):
    gd = D_POOL // N_GROUPS
    od = D_MODEL // N_GROUPS
    q = D_MODEL // 4

    def body(l_ref, b0, b1, b2, b3, a_ref, c_ref, p_ref, w_ref, ao_ref, co_ref, po_ref, wo_ref):
        a = a_ref[...]
        c = c_ref[...]
        p = p_ref[...]
        for j in range(4):
            ao_ref[j] = a[:, j * q:(j + 1) * q]
            co_ref[j] = c[:, j * q:(j + 1) * q]
            wo_ref[j] = w_ref[j * q:(j + 1) * q, :]
            for g in range(N_GROUPS):
                c0 = g * od + j * (od // 4)
                po_ref[j, g] = p[g * gd:(g + 1) * gd, c0:c0 + od // 4]

    whole = lambda a: pl.BlockSpec(a.shape, lambda i, l: (0,) * a.ndim)
    layer = lambda b: pl.BlockSpec((None,) + b.shape[1:], lambda i, l: (l[0],) + (0,) * (b.ndim - 1))
    return pl.pallas_call(
        body,
        grid_spec=pltpu.PrefetchScalarGridSpec(
            num_scalar_prefetch=1,
            grid=(1,),
            in_specs=[HBM_REF] * 4 + [whole(dwao), whole(dwco), whole(dwpool), whole(dwo)],
            out_specs=[layer(b) for b in bufs],
        ),
        out_shape=[jax.ShapeDtypeStruct(b.shape, b.dtype) for b in bufs],
        input_output_aliases={1: 0, 2: 1, 3: 2, 4: 3},
        compiler_params=_params("arbitrary"),
        name="branch_g_place",
    )(_layer_index(l), *bufs, dwao, dwco, dwpool, dwo)


def in_proj_fwd(x, g, wa, wf, wb):
    S = x.shape[0]

    def body(x_ref, g_ref, wa_ref, wf_ref, wb_ref, h_ref, pa_ref, pf_ref, pb_ref):
        xf = x_ref[...]
        r = lax.rsqrt(jnp.mean(xf * xf, axis=-1, keepdims=True) + EPS)
        h = (xf * r * g_ref[...]).astype(BF16)
        h_ref[...] = h
        pa_ref[...] = _dot(h, wa_ref[...]).astype(BF16)
        pf_ref[...] = _dot(h, wf_ref[...])
        pb_ref[...] = _dot(h, wb_ref[...]).astype(BF16)

    return pl.pallas_call(
        body,
        grid=(S // TM_PROJ,),
        in_specs=[_rows(TM_PROJ, D_MODEL), _whole(g), _whole_once(wa), _whole_once(wf), _whole_once(wb)],
        out_specs=[_rows(TM_PROJ, D_MODEL), _rows(TM_PROJ, D_QKV), _rows(TM_PROJ, D_F), _rows(TM_PROJ, D_B)],
        out_shape=[
            jax.ShapeDtypeStruct((S, D_MODEL), BF16),
            jax.ShapeDtypeStruct((S, D_QKV), BF16),
            jax.ShapeDtypeStruct((S, D_F), F32),
            jax.ShapeDtypeStruct((S, D_B), BF16),
        ],
        compiler_params=_params("parallel"),
        name="in_proj_fwd",
    )(x, g, wa, wf, wb)


def in_proj_bwd(x, g, dxm, dpa, dpf, dpb, wa, wf, wb):
    S = x.shape[0]

    def body(x_ref, g_ref, dxm_ref, dpa_ref, dpf_ref, dpb_ref, wa_ref, wf_ref, wb_ref, dx_ref, dg_ref):
        @pl.when(pl.program_id(0) == 0)
        def _():
            dg_ref[...] = jnp.zeros_like(dg_ref)

        dh = _dot_nt(dpa_ref[...], wa_ref[...]) + _dot_nt(dpf_ref[...], wf_ref[...]) + _dot_nt(dpb_ref[...], wb_ref[...])
        xf = x_ref[...]
        r = lax.rsqrt(jnp.mean(xf * xf, axis=-1, keepdims=True) + EPS)
        xhat = xf * r
        dg_ref[...] += jnp.sum(dh * xhat, axis=0, keepdims=True)
        gdh = dh * g_ref[...]
        dx_ref[...] = dxm_ref[...] + r * (gdh - xhat * jnp.mean(xhat * gdh, axis=-1, keepdims=True))

    return pl.pallas_call(
        body,
        grid=(S // TM_PROJ,),
        in_specs=[_rows(TM_PROJ, D_MODEL), _whole(g), _rows(TM_PROJ, D_MODEL), _rows(TM_PROJ, D_QKV), _rows(TM_PROJ, D_F),
                  _rows(TM_PROJ, D_B), _whole_once(wa), _whole_once(wf), _whole_once(wb)],
        out_specs=[_rows(TM_PROJ, D_MODEL), pl.BlockSpec((1, D_MODEL), lambda i: (0, 0))],
        out_shape=[jax.ShapeDtypeStruct((S, D_MODEL), F32), jax.ShapeDtypeStruct((1, D_MODEL), F32)],
        compiler_params=_params("arbitrary"),
        name="in_proj_bwd",
    )(x, g, dxm, dpa, dpf, dpb, wa, wf, wb)


def _wgrad_tokens(S, K):
    return min(S, 2048 if K <= 1024 else 1024)


def _wgrad_columns(n):
    return next(tn for tn in (1024, 768, 512, 256, 128) if n % tn == 0)


def wgrad(xa, dy, name):
    S, K = xa.shape
    N = dy.shape[1]
    ts = _wgrad_tokens(S, K)
    tn = _wgrad_columns(N)
    tk = _wgrad_columns(K)

    def body(x_ref, dy_ref, o_ref):
        @pl.when(pl.program_id(2) == 0)
        def _():
            o_ref[...] = jnp.zeros_like(o_ref)

        o_ref[...] += _dot_tn(x_ref[...], dy_ref[...])

    return pl.pallas_call(
        body,
        grid=(K // tk, N // tn, S // ts),
        in_specs=[pl.BlockSpec((ts, tk), lambda i, j, k: (k, i)), pl.BlockSpec((ts, tn), lambda i, j, k: (k, j))],
        out_specs=pl.BlockSpec((tk, tn), lambda i, j, k: (i, j)),
        out_shape=jax.ShapeDtypeStruct((K, N), F32),
        compiler_params=_params("parallel", "parallel", "arbitrary"),
        name=name,
    )(xa, dy)


def wgrad_into(xa, dy, name, buf, l):
    S, K = xa.shape
    N = dy.shape[1]
    ts = _wgrad_tokens(S, K)
    split = buf.ndim == 4
    tn = buf.shape[-1] if split else _wgrad_columns(N)
    place = _into_layer(buf)

    def body(l_ref, buf_ref, x_ref, dy_ref, o_ref):
        @pl.when(pl.program_id(1) == 0)
        def _():
            o_ref[...] = jnp.zeros_like(o_ref)

        o_ref[...] += _dot_tn(x_ref[...], dy_ref[...])

    if split:
        out_spec = pl.BlockSpec((None, None, K, tn), lambda j, k, l: (l[0], j, 0, 0))
    else:
        out_spec = pl.BlockSpec((None, K, tn), lambda j, k, l: (l[0], 0, j))
    return pl.pallas_call(
        body,
        grid_spec=pltpu.PrefetchScalarGridSpec(
            num_scalar_prefetch=1,
            grid=(N // tn, S // ts),
            in_specs=[place["in_spec"], pl.BlockSpec((ts, K), lambda j, k, l: (k, 0)),
                      pl.BlockSpec((ts, tn), lambda j, k, l: (k, j))],
            out_specs=out_spec,
        ),
        out_shape=place["out_shape"],
        input_output_aliases=place["aliases"],
        compiler_params=_params("parallel", "arbitrary"),
        name=name,
    )(_layer_index(l), buf, xa, dy)


def _head_mean_matrix():
    h = np.arange(D_ATTN) // HEAD_DIM
    return jnp.asarray((h[:, None] == h[None, :]).astype(np.float32) / HEAD_DIM, BF16)


def _place_matrix(lane0):
    m = np.zeros((N_PIECES * LANES, D_HEADS), np.float32)
    for i in range(N_PIECES):
        for h in range(N_HEADS):
            m[i * LANES + h, h * LANES + lane0 + i] = 1.0
    return jnp.asarray(m, BF16)


def _tri(n, upper):
    r = np.arange(n)
    m = (r[None, :] >= r[:, None]) if upper else (r[None, :] <= r[:, None])
    return jnp.asarray(m.astype(np.float32), BF16)


def _lanes_in(lane, lo, n):
    return (lane >= lo) & (lane < lo + n)


def qk_prep(pa, pf, gq, gk, fb):
    S = pa.shape[0]
    bd = _head_mean_matrix()
    tri = _tri(TM, upper=False)
    place_q = _place_matrix(LANE_C)

    def body(q_ref, k_ref, v_ref, pf_ref, gq_ref, gk_ref, fb_ref, bd_ref, tri_ref, pq_ref,
             qx_ref, kx_ref, vx_ref, carry):
        @pl.when(pl.program_id(0) == 0)
        def _():
            carry[...] = jnp.zeros_like(carry)

        def head_norm(x_ref, g_ref, scale):
            xf = x_ref[...].astype(F32)
            ms = _dot((xf * xf).astype(BF16), bd_ref[...])
            return xf * lax.rsqrt(ms + EPS) * g_ref[...] * scale

        qh = head_norm(q_ref, gq_ref, HEAD_DIM ** -0.5 * LOG2E)
        kh = head_norm(k_ref, gk_ref, 1.0)
        vf = v_ref[...].astype(F32)

        z = pf_ref[...] + fb_ref[...]
        logf = jnp.minimum(z, 0.0) - jnp.log(1.0 + jnp.exp(-jnp.abs(z)))
        hi, lo = _split_bf16(logf)
        c = _dot(tri_ref[...], hi) + _dot(tri_ref[...], lo) + carry[...]
        carry[...] += jnp.sum(hi.astype(F32) + lo.astype(F32), axis=0, keepdims=True)
        pieces = jnp.concatenate(_pieces(c * LOG2E), axis=1)
        cq = _dot(pieces, pq_ref[...])

        lane = lax.broadcasted_iota(jnp.int32, (TM, LANES), 1)
        low = lane < HEAD_DIM
        ones_q = _lanes_in(lane, LANE_ONE, N_PIECES).astype(F32)
        ones_k = (_lanes_in(lane, LANE_C, N_PIECES) | _lanes_in(lane, LANE_LSE, N_PIECES)).astype(F32)
        ones_v = _lanes_in(lane, LANE_C, N_PIECES + 1).astype(F32)
        for h in range(N_HEADS):
            blk = slice((h // 2) * LANES, (h // 2 + 1) * LANES)
            head = (lambda a: pltpu.roll(a[:, blk], HEAD_DIM, 1)) if h % 2 else (lambda a: a[:, blk])
            mine = slice(h * LANES, (h + 1) * LANES)
            qx_ref[h] = jnp.where(low, head(qh), cq[:, mine] + ones_q).astype(BF16)
            ck = pltpu.roll(cq[:, mine], LANE_ONE - LANE_C, 1)
            kx_ref[h] = jnp.where(low, head(kh), ones_k - ck).astype(BF16)
            vx_ref[h] = jnp.where(low, head(vf), ones_v).astype(BF16)

    heads = pl.BlockSpec((N_HEADS, TM, LANES), lambda i: (0, i, 0))
    out = jax.ShapeDtypeStruct((N_HEADS, S, LANES), BF16)
    return pl.pallas_call(
        body,
        grid=(S // TM,),
        in_specs=[pl.BlockSpec((TM, D_ATTN), lambda i: (i, 0)), pl.BlockSpec((TM, D_ATTN), lambda i: (i, 1)),
                  pl.BlockSpec((TM, D_ATTN), lambda i: (i, 2)),
                  _rows(TM, D_F), _whole(gq), _whole(gk), _whole(fb), _whole(bd), _whole(tri), _whole(place_q)],
        out_specs=[heads, heads, heads],
        out_shape=[out, out, out],
        scratch_shapes=[pltpu.VMEM((1, D_F), F32)],
        compiler_params=_params("arbitrary"),
        name="qk_prep",
    )(pa, pa, pa, pf, gq, gk, fb, bd, tri, place_q)


def attn_fwd(qx, kx, vx):
    S = qx.shape[1]
    nq = S // TQ

    def body(q_ref, k_ref, v_ref, o_ref, qb_ref):
        i = pl.program_id(1)
        lane = lax.broadcasted_iota(jnp.int32, (TQ, LANES), 1)
        q = [q_ref[0], q_ref[1]]

        def update(qh, m, acc, k, v, visible):
            z = _dot_nt(qh, k)
            if visible is not None:
                z = jnp.where(visible, z, NEG)
            m_new = jnp.maximum(m, jnp.max(z, axis=1, keepdims=True))
            pr = jnp.exp2(z - m_new)
            return m_new, jnp.exp2(m - m_new) * acc + _dot(pr.astype(BF16), v)

        def step(kt, carry, diagonal=False):
            ks = pl.multiple_of(kt * TQ, TQ)
            visible = None
            if diagonal:
                visible = lax.broadcasted_iota(jnp.int32, (TQ, TQ), 0) >= lax.broadcasted_iota(jnp.int32, (TQ, TQ), 1)
            return tuple(update(q[j], *carry[j], k_ref[j, pl.ds(ks, TQ), :], v_ref[j, pl.ds(ks, TQ), :], visible)
                         for j in range(2))

        init = tuple((jnp.full((TQ, 1), NEG, F32), jnp.zeros((TQ, LANES), F32)) for _ in range(2))
        carry = step(i, lax.fori_loop(0, i, step, init), diagonal=True)
        for j in range(2):
            m, acc = carry[j]
            l = jnp.sum(jnp.where(lane == LANE_C, acc, 0.0), axis=1, keepdims=True)
            o_ref[j] = acc / l
            n1, n2, n3 = _pieces(-(m + jnp.log(l) * LOG2E))
            qb_ref[j] = jnp.where(lane == LANE_LSE, n1,
                                  jnp.where(lane == LANE_LSE + 1, n2, jnp.where(lane == LANE_LSE + 2, n3, q[j])))

    pair_tile = pl.BlockSpec((2, TQ, LANES), lambda p, i: (p, i, 0))
    pair_all = pl.BlockSpec((2, S, LANES), lambda p, i: (p, 0, 0))
    return pl.pallas_call(
        body,
        grid=(N_HEADS // 2, nq),
        in_specs=[pair_tile, pair_all, pair_all],
        out_specs=[pair_tile, pair_tile],
        out_shape=[jax.ShapeDtypeStruct((N_HEADS, S, LANES), F32), jax.ShapeDtypeStruct((N_HEADS, S, LANES), BF16)],
        compiler_params=_params("parallel", "parallel"),
        name="attn_fwd",
    )(qx, kx, vx)


def attn_bwd(qxb, kx, vx, dox):
    S = qxb.shape[1]
    nq = S // TQ

    def body(q_ref, k_ref, v_ref, do_ref, dq_ref, dk_ref, dv_ref):
        kt = pl.program_id(1)

        @pl.when(kt == 0)
        def _():
            dq_ref[...] = jnp.zeros_like(dq_ref)

        k = [k_ref[0], k_ref[1]]
        v = [v_ref[0], v_ref[1]]

        def block(j, q0, nq_rows, k0, nk, diagonal):
            q = q_ref[j, pl.ds(q0, nq_rows), :]
            dout = do_ref[j, pl.ds(q0, nq_rows), :]
            kj, vj = k[j][k0:k0 + nk], v[j][k0:k0 + nk]
            z = _dot_nt(q, kj)
            if diagonal:
                visible = (lax.broadcasted_iota(jnp.int32, (nq_rows, nk), 0)
                           >= lax.broadcasted_iota(jnp.int32, (nq_rows, nk), 1))
                z = jnp.where(visible, z, NEG)
            pr = jnp.exp2(z)
            dv = _dot_tn(pr.astype(BF16), dout)
            dsb = (pr * _dot_nt(dout, vj)).astype(BF16)
            dq_ref[j, pl.ds(q0, nq_rows), :] += _dot(dsb, kj)
            return _dot_tn(dsb, q), dv

        def step(qi, carry):
            qs = pl.multiple_of(qi * TQ, TQ)
            new = []
            for j in range(2):
                dk, dv = block(j, qs, TQ, 0, TQ, False)
                new.append((carry[j][0] + dk, carry[j][1] + dv))
            return tuple(new)

        half = TQ // 2
        qs = pl.multiple_of(kt * TQ, TQ)
        carry = []
        for j in range(2):
            first = block(j, qs, TQ, 0, half, True)
            second = block(j, pl.multiple_of(qs + half, half), half, half, half, True)
            carry.append(tuple(jnp.concatenate(a, axis=0) for a in zip(first, second)))
        carry = lax.fori_loop(kt + 1, nq, step, tuple(carry))
        for j in range(2):
            dk_ref[j] = carry[j][0]
            dv_ref[j] = carry[j][1]

    pair_tile = pl.BlockSpec((2, TQ, LANES), lambda p, kt: (p, kt, 0))
    pair_all = pl.BlockSpec((2, S, LANES), lambda p, kt: (p, 0, 0))
    out = jax.ShapeDtypeStruct((N_HEADS, S, LANES), F32)
    return pl.pallas_call(
        body,
        grid=(N_HEADS // 2, nq),
        in_specs=[pair_all, pair_tile, pair_tile, pair_all],
        out_specs=[pair_all, pair_tile, pair_tile],
        out_shape=[out, out, out],
        compiler_params=_params("arbitrary", "arbitrary"),
        name="attn_bwd",
    )(qxb, kx, vx, dox)


def attn_bwd_post(pa, pf, dqx, dkx, dvx, gq, gk, fb):
    S = pa.shape[0]
    nt = S // TM
    bd = _head_mean_matrix()
    triu = _tri(TM, upper=True)
    rev = lambda i: nt - 1 - i

    def body(q_ref, k_ref, pf_ref, dqx_ref, dkx_ref, dvx_ref, gq_ref, gk_ref, fb_ref, bd_ref, triu_ref,
             dpa_ref, dpf_ref, dgq_ref, dgk_ref, dfb_ref, carry):
        @pl.when(pl.program_id(0) == 0)
        def _():
            carry[...] = jnp.zeros_like(carry)
            dgq_ref[...] = jnp.zeros_like(dgq_ref)
            dgk_ref[...] = jnp.zeros_like(dgk_ref)
            dfb_ref[...] = jnp.zeros_like(dfb_ref)

        lane = lax.broadcasted_iota(jnp.int32, (TM, LANES), 1)
        columns = _heads_as_columns

        def head_norm_bwd(x_ref, dy, g_ref, dg_ref):
            xf = x_ref[...].astype(F32)
            r = lax.rsqrt(_dot((xf * xf).astype(BF16), bd_ref[...]) + EPS)
            xhat = xf * r
            dg_ref[...] += jnp.sum(dy * xhat, axis=0, keepdims=True)
            gdy = dy * g_ref[...]
            return (r * (gdy - xhat * _dot((xhat * gdy).astype(BF16), bd_ref[...]))).astype(BF16)

        dpa_ref[:, 0:D_ATTN] = head_norm_bwd(q_ref, columns(dqx_ref) * HEAD_DIM ** -0.5, gq_ref, dgq_ref)
        dpa_ref[:, D_ATTN:2 * D_ATTN] = head_norm_bwd(k_ref, columns(dkx_ref) * LN2, gk_ref, dgk_ref)
        dpa_ref[:, 2 * D_ATTN:3 * D_ATTN] = columns(dvx_ref).astype(BF16)

        dc = jnp.zeros((TM, LANES), F32)
        for h in range(N_HEADS):
            both = jnp.where(lane == LANE_C, dqx_ref[h], 0.0) - jnp.where(lane == LANE_ONE, dkx_ref[h], 0.0)
            dc = jnp.where(lane == h, jnp.sum(both, axis=1, keepdims=True), dc)
        hi, lo = _split_bf16(dc)
        dlogf = _dot(triu_ref[...], hi) + _dot(triu_ref[...], lo) + carry[...]
        first = lax.broadcasted_iota(jnp.int32, (TM, D_F), 0) == 0
        carry[...] = jnp.sum(jnp.where(first, dlogf, 0.0), axis=0, keepdims=True)
        df = dlogf * _sigmoid(-(pf_ref[...] + fb_ref[...]))
        dfb_ref[...] += jnp.sum(df, axis=0, keepdims=True)
        dpf_ref[...] = df.astype(BF16)

    heads = pl.BlockSpec((N_HEADS, TM, LANES), lambda i: (0, rev(i), 0))
    return pl.pallas_call(
        body,
        grid=(nt,),
        in_specs=[pl.BlockSpec((TM, D_ATTN), lambda i: (rev(i), 0)), pl.BlockSpec((TM, D_ATTN), lambda i: (rev(i), 1)),
                  pl.BlockSpec((TM, D_F), lambda i: (rev(i), 0)), heads, heads, heads,
                  _whole(gq), _whole(gk), _whole(fb), _whole(bd), _whole(triu)],
        out_specs=[pl.BlockSpec((TM, D_QKV), lambda i: (rev(i), 0)), pl.BlockSpec((TM, D_F), lambda i: (rev(i), 0)),
                   pl.BlockSpec((1, D_ATTN), lambda i: (0, 0)), pl.BlockSpec((1, D_ATTN), lambda i: (0, 0)),
                   pl.BlockSpec((1, D_F), lambda i: (0, 0))],
        out_shape=[jax.ShapeDtypeStruct((S, D_QKV), BF16), jax.ShapeDtypeStruct((S, D_F), BF16),
                   jax.ShapeDtypeStruct((1, D_ATTN), F32), jax.ShapeDtypeStruct((1, D_ATTN), F32),
                   jax.ShapeDtypeStruct((1, D_F), F32)],
        scratch_shapes=[pltpu.VMEM((1, D_F), F32)],
        compiler_params=_params("arbitrary"),
        name="attn_bwd_post",
    )(pa, pa, pf, dqx, dkx, dvx, gq, gk, fb, bd, triu)


def _shift_down(ext, k):
    return pltpu.roll(ext, k, 0)[HALO:]


def _shift_up(ext, k, n):
    return pltpu.roll(ext, n + HALO - k, 0)[:n]


def _pool_lane_select(a2, a4, a8, a16, lane):
    return jnp.where(lane < 64, a2, jnp.where(lane < 128, a4, jnp.where(lane < 192, a8, a16)))


def _local_branches(o, pb, halo, have_prev, row0, wao, convw, wco, wpool, pscale):
    n = pb.shape[0]
    cx = pb[:, 0:D_CONV].astype(F32)
    cb = pb[:, D_CONV:2 * D_CONV].astype(F32)
    cc = pb[:, 2 * D_CONV:3 * D_CONV].astype(F32)
    px = pb[:, 3 * D_CONV:D_LOCAL].astype(F32)
    keep = have_prev.astype(F32)
    z = cc * cx
    z_ext = jnp.concatenate([halo[:, 2 * D_CONV:3 * D_CONV].astype(F32) * halo[:, 0:D_CONV].astype(F32) * keep, z], axis=0)
    z1 = _shift_down(z_ext, 1)
    z2 = _shift_down(z_ext, 2)
    conv = convw[0:1, :] * z2 + convw[1:2, :] * z1 + convw[2:3, :] * z
    cm = cb * conv

    u_ext = jnp.concatenate([halo[:, 3 * D_CONV:D_LOCAL].astype(F32) * keep, px], axis=0)
    s2 = u_ext + pltpu.roll(u_ext, 1, 0)
    s4 = s2 + pltpu.roll(s2, 2, 0)
    s8 = s4 + pltpu.roll(s4, 4, 0)
    s16 = s8 + pltpu.roll(s8, 8, 0)
    lane = lax.broadcasted_iota(jnp.int32, (n, D_POOL), 1)
    win = _pool_lane_select(2.0, 4.0, 8.0, 16.0, lane)
    t = (row0 + lax.broadcasted_iota(jnp.int32, (n, D_POOL), 0)).astype(F32)
    cnt = jnp.minimum(t + 1.0, win)
    feat = _pool_lane_select(s2[HALO:], s4[HALO:], s8[HALO:], s16[HALO:], lane) / cnt - px

    ya = _dot(o, wao)
    yc = _dot(cm.astype(BF16), wco)
    yp_pre = _dot(feat.astype(BF16), wpool)
    yp = yp_pre * pscale
    return dict(cx=cx, cb=cb, cc=cc, z=z, z1=z1, z2=z2, conv=conv, cm=cm, feat=feat, cnt=cnt, lane=lane,
                ya=ya, yc=yc, yp_pre=yp_pre, yp=yp)


def _halo_spec(tm, tile_of):
    per = tm // HALO
    return pl.BlockSpec((HALO, D_LOCAL), lambda i, *_: (jnp.maximum(tile_of(i) * per - 1, 0), 0))


def _heads_as_columns(ref):
    lane = lax.broadcasted_iota(jnp.int32, ref.shape[1:], 1)
    return jnp.concatenate([jnp.where(lane < HEAD_DIM, ref[2 * p], pltpu.roll(ref[2 * p + 1], HEAD_DIM, 1))
                            for p in range(N_HEADS // 2)], axis=1)


def mix_out_fwd(x, ox, pb, wao, convw, wco, wpool, pscale, wo_all, l):
    S = x.shape[0]
    tm = TM_MIX_FWD

    def body(l_ref, x_ref, o_ref, pb_ref, halo_ref, wao_ref, cw_ref, wco_ref, wp_ref, ps_ref, wo_ref, y_ref):
        i = pl.program_id(0)
        pb = pb_ref[...]
        o = _heads_as_columns(o_ref).astype(BF16)
        b = _local_branches(o, pb, halo_ref[...], i > 0, i * tm, wao_ref[...], cw_ref[...], wco_ref[...],
                            wp_ref[...], ps_ref[...])
        g0 = _sigmoid(pb[:, D_LOCAL:D_LOCAL + D_MODEL].astype(F32))
        g1 = _sigmoid(pb[:, D_LOCAL + D_MODEL:D_LOCAL + 2 * D_MODEL].astype(F32))
        g2 = _sigmoid(pb[:, D_LOCAL + 2 * D_MODEL:D_B].astype(F32))
        merged = g0 * b["ya"] + g1 * b["yc"] + g2 * b["yp"]
        y_ref[...] = x_ref[...] + _dot(merged.astype(BF16), wo_ref[...])

    return pl.pallas_call(
        body,
        grid_spec=pltpu.PrefetchScalarGridSpec(
            num_scalar_prefetch=1,
            grid=(S // tm,),
            in_specs=[_rows(tm, D_MODEL), pl.BlockSpec((N_HEADS, tm, LANES), lambda i, l: (0, i, 0)), _rows(tm, D_B),
                      _halo_spec(tm, lambda i: i), _whole(wao), _whole(convw), _whole(wco), _whole(wpool), _whole(pscale),
                      _layer((D_MODEL, D_MODEL))],
            out_specs=_rows(tm, D_MODEL),
        ),
        out_shape=jax.ShapeDtypeStruct((S, D_MODEL), F32),
        compiler_params=_params("parallel"),
        name="mix_out_fwd",
    )(_layer_index(l), x, ox, pb, pb, wao, convw, wco, wpool, pscale, wo_all)


def mix_out_bwd(dxm, ox, pb, wao, convw, wco, wpool, pscale, wo_all, l):
    S = dxm.shape[0]
    tm = TM_MIX
    nt = S // tm
    rev = lambda i: nt - 1 - i
    rows = lambda n: pl.BlockSpec((tm, n), lambda i, l: (rev(i), 0))
    heads = pl.BlockSpec((N_HEADS, tm, LANES), lambda i, l: (0, rev(i), 0))
    acc = lambda r, c: pl.BlockSpec((r, c), lambda i, l: (0, 0))

    def body(l_ref, dxm_ref, o_ref, pb_ref, halo_ref, wao_ref, cw_ref, wco_ref, wp_ref, ps_ref, wo_ref,
             dpb_ref, do_ref, dwo_ref, dwao_ref, dwco_ref, dwp_ref, dcw_ref, dps_ref, next_dconv, next_e):
        i = pl.program_id(0)
        r = rev(i)

        @pl.when(i == 0)
        def _():
            for ref in (dwo_ref, dwao_ref, dwco_ref, dwp_ref, dcw_ref, dps_ref, next_dconv, next_e):
                ref[...] = jnp.zeros_like(ref)

        pb = pb_ref[...]
        o = _heads_as_columns(o_ref).astype(BF16)
        cw = cw_ref[...]
        b = _local_branches(o, pb, halo_ref[...], r > 0, r * tm, wao_ref[...], cw, wco_ref[...], wp_ref[...], ps_ref[...])
        g0 = _sigmoid(pb[:, D_LOCAL:D_LOCAL + D_MODEL].astype(F32))
        g1 = _sigmoid(pb[:, D_LOCAL + D_MODEL:D_LOCAL + 2 * D_MODEL].astype(F32))
        g2 = _sigmoid(pb[:, D_LOCAL + 2 * D_MODEL:D_B].astype(F32))
        dxb = dxm_ref[...].astype(BF16)
        merged = g0 * b["ya"] + g1 * b["yc"] + g2 * b["yp"]
        dwo_ref[...] += _dot_tn(merged.astype(BF16), dxb)
        dmer = _dot_nt(dxb, wo_ref[...])
        dpb_ref[:, D_LOCAL:D_LOCAL + D_MODEL] = (dmer * b["ya"] * (g0 * (1.0 - g0))).astype(BF16)
        dpb_ref[:, D_LOCAL + D_MODEL:D_LOCAL + 2 * D_MODEL] = (dmer * b["yc"] * (g1 * (1.0 - g1))).astype(BF16)
        dpb_ref[:, D_LOCAL + 2 * D_MODEL:D_B] = (dmer * b["yp"] * (g2 * (1.0 - g2))).astype(BF16)

        dya = (dmer * g0).astype(BF16)
        dwao_ref[...] += _dot_tn(o, dya)
        da = _dot_nt(dya, wao_ref[...])
        lane = lax.broadcasted_iota(jnp.int32, (tm, LANES), 1)
        for h in range(N_HEADS):
            two = da[:, (h // 2) * LANES:(h // 2 + 1) * LANES]
            dah = jnp.where(lane < HEAD_DIM, pltpu.roll(two, HEAD_DIM, 1) if h % 2 else two, 0.0).astype(BF16)
            d1, d2, d3 = _pieces(-jnp.sum(dah.astype(F32) * o_ref[h], axis=1, keepdims=True))
            do_ref[h] = jnp.where(lane == LANE_C + 1, d1, jnp.where(lane == LANE_C + 2, d2,
                                                                  jnp.where(lane == LANE_C + 3, d3, dah)))

        dyc = (dmer * g1).astype(BF16)
        dwco_ref[...] += _dot_tn(b["cm"].astype(BF16), dyc)
        dcm = _dot_nt(dyc, wco_ref[...])
        dconv = dcm * b["cb"]
        dcw_ref[0:1, :] += jnp.sum(dconv * b["z2"], axis=0, keepdims=True)
        dcw_ref[1:2, :] += jnp.sum(dconv * b["z1"], axis=0, keepdims=True)
        dcw_ref[2:3, :] += jnp.sum(dconv * b["z"], axis=0, keepdims=True)
        d_ext = jnp.concatenate([dconv, next_dconv[...]], axis=0)
        dz = cw[2:3, :] * dconv + cw[1:2, :] * _shift_up(d_ext, 1, tm) + cw[0:1, :] * _shift_up(d_ext, 2, tm)
        next_dconv[...] = dconv[0:HALO]
        dpb_ref[:, 0:D_CONV] = (dz * b["cc"]).astype(BF16)
        dpb_ref[:, D_CONV:2 * D_CONV] = (dcm * b["conv"]).astype(BF16)
        dpb_ref[:, 2 * D_CONV:3 * D_CONV] = (dz * b["cx"]).astype(BF16)

        dyp = dmer * g2
        dps_ref[...] += jnp.sum(dyp * b["yp_pre"], axis=0, keepdims=True)
        dyps = (dyp * ps_ref[...]).astype(BF16)
        dwp_ref[...] += _dot_tn(b["feat"].astype(BF16), dyps)
        dfeat = _dot_nt(dyps, wp_ref[...])
        e = dfeat / b["cnt"]
        e_ext = jnp.concatenate([e, next_e[...]], axis=0)
        up = lambda a, k: pltpu.roll(a, tm + HALO - k, 0)
        f2 = e_ext + up(e_ext, 1)
        f4 = f2 + up(f2, 2)
        f8 = f4 + up(f4, 4)
        f16 = f8 + up(f8, 8)
        next_e[...] = e[0:HALO]
        dpb_ref[:, 3 * D_CONV:D_LOCAL] = (_pool_lane_select(f2[:tm], f4[:tm], f8[:tm], f16[:tm], b["lane"]) - dfeat).astype(BF16)

    return pl.pallas_call(
        body,
        grid_spec=pltpu.PrefetchScalarGridSpec(
            num_scalar_prefetch=1,
            grid=(nt,),
            in_specs=[rows(D_MODEL), heads, rows(D_B), _halo_spec(tm, rev),
                      _whole(wao), _whole(convw), _whole(wco), _whole(wpool), _whole(pscale), _layer((D_MODEL, D_MODEL))],
            out_specs=[rows(D_B), heads, acc(D_MODEL, D_MODEL), acc(D_ATTN, D_MODEL), acc(D_CONV, D_MODEL),
                       acc(D_POOL, D_MODEL), acc(8, D_CONV), acc(1, D_MODEL)],
            scratch_shapes=[pltpu.VMEM((HALO, D_CONV), F32), pltpu.VMEM((HALO, D_POOL), F32)],
        ),
        out_shape=[jax.ShapeDtypeStruct((S, D_B), BF16), jax.ShapeDtypeStruct((N_HEADS, S, LANES), BF16),
                   jax.ShapeDtypeStruct((D_MODEL, D_MODEL), F32), jax.ShapeDtypeStruct((D_ATTN, D_MODEL), F32),
                   jax.ShapeDtypeStruct((D_CONV, D_MODEL), F32), jax.ShapeDtypeStruct((D_POOL, D_MODEL), F32),
                   jax.ShapeDtypeStruct((8, D_CONV), F32), jax.ShapeDtypeStruct((1, D_MODEL), F32)],
        compiler_params=_params("arbitrary"),
        name="mix_out_bwd",
    )(_layer_index(l), dxm, ox, pb, pb, wao, convw, wco, wpool, pscale, wo_all)


FF_SHARD = 2 * D_FF // 4


def ffn_fwd(x, g, w1_all, w2_all, l):
    S = x.shape[0]

    def body(l_ref, x_ref, g_ref, w1_ref, w2_ref, y_ref, u_ref):
        xf = x_ref[...]
        r = lax.rsqrt(jnp.mean(xf * xf, axis=-1, keepdims=True) + EPS)
        h = (xf * r * g_ref[...]).astype(BF16)
        u = jnp.concatenate([_dot(h, w1_ref[j]) for j in range(4)], axis=1)
        u_ref[...] = u.astype(BF16)
        gt = u[:, 0:D_FF]
        act = gt * _sigmoid(gt) * u[:, D_FF:2 * D_FF]
        y_ref[...] = xf + _dot(act.astype(BF16), w2_ref[...])

    return pl.pallas_call(
        body,
        grid_spec=pltpu.PrefetchScalarGridSpec(
            num_scalar_prefetch=1,
            grid=(S // TM,),
            in_specs=[_rows(TM, D_MODEL), _whole(g), _layer((4, D_MODEL, FF_SHARD)), _layer((D_FF, D_MODEL))],
            out_specs=[_rows(TM, D_MODEL), _rows(TM, 2 * D_FF)],
        ),
        out_shape=[jax.ShapeDtypeStruct((S, D_MODEL), F32), jax.ShapeDtypeStruct((S, 2 * D_FF), BF16)],
        compiler_params=_params("parallel"),
        name="ffn_fwd",
    )(_layer_index(l), x, g, w1_all, w2_all)


def ffn_bwd(x, g, dy, u, w1_all, w2_all, l):
    S = x.shape[0]

    def body(l_ref, x_ref, g_ref, dy_ref, u_ref, w1_ref, w2_ref, dx_ref, du_ref, act_ref, h_ref, dyb_ref, dg_ref):
        @pl.when(pl.program_id(0) == 0)
        def _():
            dg_ref[...] = jnp.zeros_like(dg_ref)

        dyf = dy_ref[...]
        dyb_ref[...] = dyf.astype(BF16)
        dact = _dot_nt(dyb_ref[...], w2_ref[...])
        gt = u_ref[:, 0:D_FF].astype(F32)
        up = u_ref[:, D_FF:2 * D_FF].astype(F32)
        sg = _sigmoid(gt)
        silu = gt * sg
        act_ref[...] = (silu * up).astype(BF16)
        du_ref[:, 0:D_FF] = (dact * up * (sg * (1.0 + gt * (1.0 - sg)))).astype(BF16)
        du_ref[:, D_FF:2 * D_FF] = (dact * silu).astype(BF16)
        dh = _dot_nt(du_ref[:, 0:FF_SHARD], w1_ref[0])
        for j in range(1, 4):
            dh = dh + _dot_nt(du_ref[:, j * FF_SHARD:(j + 1) * FF_SHARD], w1_ref[j])
        xf = x_ref[...]
        r = lax.rsqrt(jnp.mean(xf * xf, axis=-1, keepdims=True) + EPS)
        xhat = xf * r
        h_ref[...] = (xhat * g_ref[...]).astype(BF16)
        dg_ref[...] += jnp.sum(dh * xhat, axis=0, keepdims=True)
        gdh = dh * g_ref[...]
        dx_ref[...] = dyf + r * (gdh - xhat * jnp.mean(xhat * gdh, axis=-1, keepdims=True))

    return pl.pallas_call(
        body,
        grid_spec=pltpu.PrefetchScalarGridSpec(
            num_scalar_prefetch=1,
            grid=(S // TM,),
            in_specs=[_rows(TM, D_MODEL), _whole(g), _rows(TM, D_MODEL), _rows(TM, 2 * D_FF),
                      _layer((4, D_MODEL, FF_SHARD)), _layer((D_FF, D_MODEL))],
            out_specs=[_rows(TM, D_MODEL), _rows(TM, 2 * D_FF), _rows(TM, D_FF), _rows(TM, D_MODEL), _rows(TM, D_MODEL),
                       pl.BlockSpec((1, D_MODEL), lambda i, l: (0, 0))],
        ),
        out_shape=[jax.ShapeDtypeStruct((S, D_MODEL), F32), jax.ShapeDtypeStruct((S, 2 * D_FF), BF16),
                   jax.ShapeDtypeStruct((S, D_FF), BF16), jax.ShapeDtypeStruct((S, D_MODEL), BF16),
                   jax.ShapeDtypeStruct((S, D_MODEL), BF16), jax.ShapeDtypeStruct((1, D_MODEL), F32)],
        compiler_params=_params("arbitrary"),
        name="ffn_bwd",
    )(_layer_index(l), x, g, dy, u, w1_all, w2_all)


def loss_head(y, target):
    S = y.shape[0]

    def body(y_ref, t_ref, loss_ref, dy_ref):
        @pl.when(pl.program_id(0) == 0)
        def _():
            loss_ref[0, 0] = 0.0

        err = y_ref[...] - t_ref[...]
        dy_ref[...] = err * (1.0 / D_MODEL)
        loss_ref[0, 0] += 0.5 * jnp.sum(jnp.mean(err * err, axis=-1))

    return pl.pallas_call(
        body,
        grid=(S // TM,),
        in_specs=[_rows(TM, D_MODEL), _rows(TM, D_MODEL)],
        out_specs=[pl.BlockSpec((1, 1), lambda i: (0, 0), memory_space=pltpu.SMEM), _rows(TM, D_MODEL)],
        out_shape=[jax.ShapeDtypeStruct((1, 1), F32), jax.ShapeDtypeStruct((S, D_MODEL), F32)],
        compiler_params=_params("arbitrary"),
        name="loss_head",
    )(y, target)


SPLIT = {
    "w_in": ((D_MODEL, D_IN), 1),
    "w_attn_out": ((D_ATTN, D_MODEL), 1),
    "w_conv_out": ((D_CONV, D_MODEL), 1),
    "pool_w": ((N_GROUPS, D_POOL // N_GROUPS, D_MODEL // N_GROUPS), 2),
    "w_o": ((D_MODEL, D_MODEL), 0),
    "w_ffn_in": ((D_MODEL, 2 * D_FF), 1),
    "w_ffn_out": ((D_FF, D_MODEL), 0),
}
SMALL = {"norm_mix_g": D_MODEL, "forget_b": N_HEADS, "q_norm_g": HEAD_DIM, "k_norm_g": HEAD_DIM,
         "pool_scale": D_MODEL, "norm_ffn_g": D_MODEL, "conv_w": CONV_K * D_CONV}
WEIGHTS = ["norm_mix_g", "w_in", "forget_b", "q_norm_g", "k_norm_g", "w_attn_out", "conv_w", "w_conv_out", "pool_w",
           "pool_scale", "w_o", "norm_ffn_g", "w_ffn_in", "w_ffn_out"]


MIX_W = ["w_in", "w_attn_out", "w_conv_out", "pool_w", "w_o"]
FFN_W = ["w_ffn_in", "w_ffn_out"]


def _row(a):
    return a.astype(F32).reshape(1, -1)


def mix_fwd(x, full, li, small):
    n = full["w_o"].shape[0]
    wa, wf, wb = w_in_prep(full["w_in"], li)
    wao, wco, wpool, convw = branch_w_prep(full["w_attn_out"], full["w_conv_out"], full["pool_w"], full["conv_w"], li)
    w = dict(wa=wa, wf=wf, wb=wb, wao=wao, wco=wco, wpool=wpool, convw=convw,
             w_o=full["w_o"].reshape(n, D_MODEL, D_MODEL), at=li,
             g1=_row(small["norm_mix_g"]), pscale=_row(small["pool_scale"]),
             gq=_row(jnp.tile(small["q_norm_g"], N_HEADS)), gk=_row(jnp.tile(small["k_norm_g"], N_HEADS)),
             fb=_row(jnp.pad(small["forget_b"], (0, D_F - N_HEADS))))
    h, pa, pf, pb = in_proj_fwd(x, w["g1"], wa, wf, wb)
    qx, kx, vx = qk_prep(pa, pf, w["gq"], w["gk"], w["fb"])
    ox, qxb = attn_fwd(qx, kx, vx)
    xm = mix_out_fwd(x, ox, pb, wao, convw, wco, wpool, w["pscale"], w["w_o"], li)
    return xm, w, dict(x=x, h=h, pa=pa, pf=pf, pb=pb, kx=kx, vx=vx, ox=ox, qxb=qxb)


def ffn_half_fwd(xm, full, li, g2):
    n = full["w_ffn_out"].shape[0]
    w = dict(w1=full["w_ffn_in"], w2=full["w_ffn_out"].reshape(n, D_FF, D_MODEL), at=li, g2=_row(g2))
    y, u = ffn_fwd(xm, w["g2"], w["w1"], w["w2"], li)
    return y, w, dict(xm=xm, u=u)


def ffn_half_bwd(dx, w, s, gi, big):
    n = big["w_ffn_out"].shape[0]
    big = dict(big)
    dxm, du, act, h2, dyb, dg2 = ffn_bwd(s["xm"], w["g2"], dx, s["u"], w["w1"], w["w2"], w["at"])
    big["w_ffn_out"] = wgrad_into(act, dyb, "wgrad_ffn_out", big["w_ffn_out"].reshape(n, D_FF, D_MODEL),
                                  gi).reshape(big["w_ffn_out"].shape)
    big["w_ffn_in"] = wgrad_into(h2, du, "wgrad_ffn_in", big["w_ffn_in"], gi)
    return dxm, big, dict(norm_ffn_g=dg2[0])


def mix_bwd_weights(dxm, w, s, gi, big):
    big = dict(big)
    dpb, dox, dwo, dwao, dwco, dwpool, dconvw, dpscale = mix_out_bwd(
        dxm, s["ox"], s["pb"], w["wao"], w["convw"], w["wco"], w["wpool"], w["pscale"], w["w_o"], w["at"])
    dqx, dkx, dvx = attn_bwd(s["qxb"], s["kx"], s["vx"], dox)
    dpa, dpf, dgq, dgk, dfb = attn_bwd_post(s["pa"], s["pf"], dqx, dkx, dvx, w["gq"], w["gk"], w["fb"])
    big["w_in"] = w_in_unprep(wgrad(s["h"], dpa, "wgrad_in_qkv"), wgrad(s["h"], dpf, "wgrad_in_f"),
                              wgrad(s["h"], dpb, "wgrad_in_b"), big["w_in"], gi)
    big["w_attn_out"], big["w_conv_out"], big["pool_w"], big["w_o"] = branch_g_place(
        dwao, dwco, dwpool, dwo, (big["w_attn_out"], big["w_conv_out"], big["pool_w"], big["w_o"]), gi)
    sm = dict(forget_b=dfb[0, 0:N_HEADS], q_norm_g=dgq.reshape(N_HEADS, HEAD_DIM).sum(0),
              k_norm_g=dgk.reshape(N_HEADS, HEAD_DIM).sum(0), pool_scale=dpscale[0],
              conv_w=dconvw[0:CONV_K].reshape(-1))
    return (dpa, dpf, dpb), big, sm


def mix_bwd_input(dproj, dxm, w, s):
    dx, dg1 = in_proj_bwd(s["x"], w["g1"], dxm, *dproj, w["wa"], w["wf"], w["wb"])
    return dx, dict(norm_mix_g=dg1[0])


def mix_bwd(dxm, w, s, gi, big):
    dproj, big, sm = mix_bwd_weights(dxm, w, s, gi, big)
    dx, g1 = mix_bwd_input(dproj, dxm, w, s)
    return dx, big, {**sm, **g1}


def grad_buffers(full):
    return {n: lax.empty(full[n].shape, F32) for n in SPLIT}


def local_step(x, target, gathered, small):
    n_layers = small["norm_mix_g"].shape[0]
    done = []
    for l in range(n_layers):
        xm, wm, sm = mix_fwd(x, gathered, l, {n: v[l] for n, v in small.items()})
        x, wf, sf = ffn_half_fwd(xm, gathered, l, small["norm_ffn_g"][l])
        done.append((wm, sm, wf, sf))
    loss, dx = loss_head(x, target)
    big = grad_buffers(gathered)
    small_grads = [None] * n_layers
    for l in reversed(range(n_layers)):
        wm, sm, wf, sf = done[l]
        dxm, ffn_big, g_ffn = ffn_half_bwd(dx, wf, sf, l, {n: big[n] for n in FFN_W})
        dx, mix_big, g_mix = mix_bwd(dxm, wm, sm, l, {n: big[n] for n in MIX_W})
        big = {**ffn_big, **mix_big}
        small_grads[l] = {**g_ffn, **g_mix}
    return loss, dx, big, {n: jnp.stack([g[n] for g in small_grads]) for n in SMALL}


def adamw(w, g, m, v, name):
    R, C = w.shape
    tm = 256 if R % 256 == 0 else R

    def body(w_ref, g_ref, m_ref, v_ref, d_ref, nm_ref, nv_ref):
        gr = g_ref[...]
        m_new = ADAM_B1 * m_ref[...] + (1.0 - ADAM_B1) * gr
        v_new = ADAM_B2 * v_ref[...] + (1.0 - ADAM_B2) * jnp.square(gr)
        nm_ref[...] = m_new
        nv_ref[...] = v_new
        m_hat = m_new / (1.0 - ADAM_B1 ** ADAM_STEP)
        v_hat = v_new / (1.0 - ADAM_B2 ** ADAM_STEP)
        d_ref[...] = -ADAM_LR * (m_hat / (jnp.sqrt(v_hat) + ADAM_EPS) + ADAM_WD * w_ref[...])

    spec = _rows(tm, C)
    out = jax.ShapeDtypeStruct((R, C), F32)
    return pl.pallas_call(
        body,
        grid=(R // tm,),
        in_specs=[spec] * 4,
        out_specs=[spec] * 3,
        out_shape=[out] * 3,
        compiler_params=_params("parallel"),
        name=name,
    )(w, g, m, v)


MESH = pl.DeviceIdType.MESH
HBM_REF = pl.BlockSpec(memory_space=pl.ANY)
N_CHIPS = 4
N_DEV = 8
SMALL_SHAPE = (128, LANES)


def _mesh_pos():
    return lax.axis_index("x"), lax.axis_index("y"), lax.axis_index("c")


def _other_chips(x, y):
    return [(1 - x, y), (x, 1 - y), (1 - x, 1 - y)]


def _remote(src, dst, send_sem, recv_sem, to):
    return pltpu.make_async_remote_copy(src_ref=src, dst_ref=dst, send_sem=send_sem, recv_sem=recv_sem,
                                        device_id=to, device_id_type=MESH)


def _row_tile(rows):
    for tm in (256, 176, 128):
        if rows % tm == 0:
            return tm
    return rows


def _as4(a):
    return a.reshape(a.shape[0], a.shape[1], -1, a.shape[-1])


def _core_rows(buf, c):
    R, C = buf.shape[-2:]
    if R % 2:
        whole = (pl.ds(0, R), pl.ds(0, C))
        return whole, whole, False
    return (pl.ds(c * (R // 2), R // 2), pl.ds(0, C)), (pl.ds((1 - c) * (R // 2), R // 2), pl.ds(0, C)), True


def _half_shape(g):
    return g.shape[:-2] + (g.shape[-2] // 2, g.shape[-1])


def place_shard(chip, w, lo, hi, dtype, name, after=None):
    w3 = w.reshape(w.shape[0], -1, w.shape[-1])
    _, R, C = w3.shape
    tm = _row_tile(R)
    idle = [] if after is None else [after]

    def body(chip_ref, w_ref, *rest):
        rest[-1][...] = w_ref[...].astype(dtype)

    return pl.pallas_call(
        body,
        grid_spec=pltpu.PrefetchScalarGridSpec(
            num_scalar_prefetch=1,
            grid=(hi - lo, R // tm),
            in_specs=[pl.BlockSpec((None, tm, C), lambda l, i, chip: (lo + l, i, 0))] + [_whole(a) for a in idle],
            out_specs=pl.BlockSpec((None, None, tm, C), lambda l, i, chip: (l, chip[0], i, 0)),
        ),
        out_shape=jax.ShapeDtypeStruct((hi - lo, N_CHIPS, R, C), dtype),
        compiler_params=_params("parallel", "parallel"),
        name=name,
    )(chip, w3, *idle)


HBM_SPACE = pl.BlockSpec(memory_space=pltpu.HBM)
SEM_SPACE = pl.BlockSpec(memory_space=pltpu.SEMAPHORE)
IN_FLIGHT = pltpu.SideEffectType.DATAFLOW_SIDE_EFFECTING


def _sem_table(send_sems, recv_sems):
    return lambda t, j: (send_sems.at[t, j], recv_sems.at[t, j])


def _sem_per_peer(send_sems, recv_sems):
    return lambda t, j: (send_sems[j], recv_sems[j])


def _gather_ici(bufs, sem, sends=True, lands=True):
    x, y, c = _mesh_pos()
    me = 2 * x + y
    out, into = [], []
    for t, buf in enumerate(bufs):
        mine, _, _ = _core_rows(buf, c)
        part = lambda k, buf=buf, mine=mine: buf.at[pl.ds(0, buf.shape[0]), k, *mine]
        for j, (px, py) in enumerate(_other_chips(x, y)):
            if sends:
                out.append(_remote(part(me), part(me), *sem(t, j), (px, py, c)))
            if lands:
                into.append(_remote(part(2 * px + py), part(2 * px + py), *sem(t, j), (px, py, c)))
    return out, into


def _gather_d2d(bufs, send_sems, recv_sems, first):
    x, y, c = _mesh_pos()
    sends, lands = [], []
    for t, buf in enumerate(bufs):
        mine, theirs, split = _core_rows(buf, c)
        if not split:
            continue
        for j, (px, py) in enumerate(_other_chips(x, y)):
            part = lambda half, buf=buf, k=2 * px + py: buf.at[pl.ds(0, buf.shape[0]), k, *half]
            sems = (send_sems.at[t, first + j], recv_sems.at[t, first + j], (x, y, 1 - c))
            sends.append(_remote(part(mine), part(mine), *sems))
            lands.append(_remote(part(theirs), part(theirs), *sems))
    return sends, lands


def gather_start(bufs):
    n = len(bufs)

    def body(*refs):
        send_sems, recv_sems = refs[n:n + 3], refs[n + 3:n + 6]
        outs = refs[n + 6:2 * n + 6]
        token = refs[2 * n + 6]
        for cp in _gather_ici(outs, _sem_per_peer(send_sems, recv_sems), lands=False)[0]:
            cp.start()
        token[...] = jnp.zeros_like(token)

    res = pl.pallas_call(
        body,
        in_specs=[HBM_SPACE] * n,
        out_specs=[SEM_SPACE] * 6 + [HBM_SPACE] * n + [pl.BlockSpec(memory_space=pltpu.VMEM)],
        out_shape=[pltpu.SemaphoreType.DMA(())] * 6
        + [pltpu.HBM(b.shape, b.dtype) for b in bufs] + [jax.ShapeDtypeStruct((8, LANES), F32)],
        input_output_aliases={t: t + 6 for t in range(n)},
        compiler_params=pltpu.CompilerParams(has_side_effects=IN_FLIGHT),
        name="gather_start",
    )(*[pltpu.with_memory_space_constraint(b, pltpu.HBM) for b in bufs])
    return list(res[0:3]), list(res[3:6]), list(res[6:n + 6]), res[n + 6]


def gather_wait(send_sems, recv_sems, bufs, after):
    n = len(bufs)

    def body(*refs):
        ins_send, ins_recv = refs[n:n + 3], refs[n + 3:n + 6]
        outs = refs[n + 7:]
        sends, lands = _gather_ici(outs, _sem_per_peer(ins_send, ins_recv))
        for cp in lands:
            cp.wait_recv()
        for cp in sends:
            cp.wait_send()

    return list(pl.pallas_call(
        body,
        in_specs=[HBM_SPACE] * n + [SEM_SPACE] * 6 + [HBM_REF],
        out_specs=[HBM_SPACE] * n,
        out_shape=[pltpu.HBM(b.shape, b.dtype) for b in bufs],
        input_output_aliases={t: t for t in range(n)},
        compiler_params=pltpu.CompilerParams(has_side_effects=IN_FLIGHT),
        name="gather_wait",
    )(*bufs, *send_sems, *recv_sems, after))


def gather_forward(bufs):
    n = len(bufs)

    def body(*refs):
        outs = refs[n:2 * n]
        send_sems, recv_sems = refs[2 * n:]
        sends, lands = _gather_d2d(outs, send_sems, recv_sems, 0)
        for cp in sends:
            cp.start()
        for cp in lands:
            cp.wait_recv()
        for cp in sends:
            cp.wait_send()

    return list(pl.pallas_call(
        body,
        in_specs=[HBM_REF] * n,
        out_specs=[HBM_REF] * n,
        out_shape=[jax.ShapeDtypeStruct(b.shape, b.dtype) for b in bufs],
        input_output_aliases={t: t for t in range(n)},
        scratch_shapes=[pltpu.SemaphoreType.DMA((n, 3)), pltpu.SemaphoreType.DMA((n, 3))],
        name="gather_forward",
    )(*bufs))


def pair_exchange(grads):
    n = len(grads)

    def body(*refs):
        ins, outs = refs[:n], refs[n:2 * n]
        send_sems, recv_sems = refs[2 * n:]
        cps = _pair_copies(ins, outs, lambda t: (send_sems.at[t], recv_sems.at[t]))
        for cp in cps:
            cp.start()
        for cp in cps:
            cp.wait()

    return list(pl.pallas_call(
        body,
        in_specs=[HBM_REF] * n,
        out_specs=[HBM_REF] * n,
        out_shape=[jax.ShapeDtypeStruct(_half_shape(g), g.dtype) for g in grads],
        scratch_shapes=[pltpu.SemaphoreType.DMA((n,)), pltpu.SemaphoreType.DMA((n,))],
        name="pair_exchange",
    )(*grads))


def _pair_copies(grads, lands, sem):
    x, y, c = _mesh_pos()
    cps = []
    for t in range(len(grads)):
        _, theirs, _ = _core_rows(grads[t], c)
        src = grads[t].at[pl.ds(0, grads[t].shape[0]), pl.ds(0, N_CHIPS), *theirs]
        cps.append(_remote(src, lands[t], *sem(t), (x, y, 1 - c)))
    return cps


def pair_exchange_start(grads):
    n = len(grads)
    lands = [lax.empty(_half_shape(g), g.dtype) for g in grads]

    def body(*refs):
        send_sem, recv_sem = refs[2 * n:2 * n + 2]
        outs = refs[2 * n + 2:4 * n + 2]
        token = refs[4 * n + 2]
        for cp in _pair_copies(outs[:n], outs[n:], lambda t: (send_sem, recv_sem)):
            cp.start()
        token[...] = jnp.zeros_like(token)

    res = pl.pallas_call(
        body,
        in_specs=[HBM_SPACE] * (2 * n),
        out_specs=[SEM_SPACE] * 2 + [HBM_SPACE] * (2 * n) + [pl.BlockSpec(memory_space=pltpu.VMEM)],
        out_shape=[pltpu.SemaphoreType.DMA(())] * 2 + [pltpu.HBM(a.shape, a.dtype) for a in list(grads) + lands]
        + [jax.ShapeDtypeStruct((8, LANES), F32)],
        input_output_aliases={t: t + 2 for t in range(2 * n)},
        compiler_params=pltpu.CompilerParams(has_side_effects=IN_FLIGHT),
        name="pair_exchange_start",
    )(*[pltpu.with_memory_space_constraint(a, pltpu.HBM) for a in list(grads) + lands])
    return res[0], res[1], list(res[2:n + 2]), list(res[n + 2:2 * n + 2]), res[2 * n + 2]


def pair_exchange_wait(send_sem, recv_sem, grads, lands, after):
    n = len(grads)

    def body(*refs):
        in_send, in_recv = refs[2 * n:2 * n + 2]
        outs = refs[2 * n + 3:]
        for cp in _pair_copies(outs[:n], outs[n:], lambda t: (in_send, in_recv)):
            cp.wait()

    res = pl.pallas_call(
        body,
        in_specs=[HBM_SPACE] * (2 * n) + [SEM_SPACE] * 2 + [HBM_REF],
        out_specs=[HBM_SPACE] * (2 * n),
        out_shape=[pltpu.HBM(a.shape, a.dtype) for a in list(grads) + list(lands)],
        input_output_aliases={t: t for t in range(2 * n)},
        compiler_params=pltpu.CompilerParams(has_side_effects=IN_FLIGHT),
        name="pair_exchange_wait",
    )(*grads, *lands, send_sem, recv_sem, after)
    return list(res[:n]), list(res[n:])


def pair_sum(core, g, t, name):
    n, _, hr, hc = t.shape
    R, C = g.shape[2:]
    tm = _row_tile(hr)
    per = hr // tm

    def body(c_ref, g_ref, t_ref, o_ref):
        o_ref[...] = (g_ref[...] + t_ref[...]).astype(BF16)

    tile = pl.BlockSpec((None, tm, hc), lambda a, i, c: (a, i, 0))
    out = pl.pallas_call(
        body,
        grid_spec=pltpu.PrefetchScalarGridSpec(
            num_scalar_prefetch=1,
            grid=(n * N_CHIPS, per),
            in_specs=[pl.BlockSpec((None, tm, hc), lambda a, i, c: (a, per * c[0] + i, 0)), tile],
            out_specs=tile,
        ),
        out_shape=jax.ShapeDtypeStruct((n * N_CHIPS, hr, hc), BF16),
        compiler_params=_params("parallel", "parallel"),
        name=name,
    )(core, g.reshape(n * N_CHIPS, R, C), t.reshape(n * N_CHIPS, hr, hc))
    return out.reshape(t.shape)


def _chip_copies(sums, recv, sem):
    x, y, c = _mesh_pos()
    cps = []
    for t in range(len(sums)):
        for j, (px, py) in enumerate(_other_chips(x, y)):
            src = sums[t].at[pl.ds(0, sums[t].shape[0]), 2 * px + py]
            cps.append(_remote(src, recv[t].at[j], *sem(t, j), (px, py, c)))
    return cps


def _recv_shape(s):
    return (N_CHIPS - 1, s.shape[0]) + s.shape[2:]


def chip_exchange(sums):
    n = len(sums)

    def body(*refs):
        ins, outs = refs[:n], refs[n:2 * n]
        cps = _chip_copies(ins, outs, _sem_table(*refs[2 * n:]))
        for cp in cps:
            cp.start()
        for cp in cps:
            cp.wait()

    return list(pl.pallas_call(
        body,
        in_specs=[HBM_REF] * n,
        out_specs=[HBM_REF] * n,
        out_shape=[jax.ShapeDtypeStruct(_recv_shape(s), s.dtype) for s in sums],
        scratch_shapes=[pltpu.SemaphoreType.DMA((n, 3)), pltpu.SemaphoreType.DMA((n, 3))],
        name="chip_exchange",
    )(*sums))


def chip_exchange_start(sums):
    n = len(sums)
    lands = [lax.empty(_recv_shape(s), s.dtype) for s in sums]

    def body(*refs):
        send_sems, recv_sems = refs[2 * n:2 * n + 3], refs[2 * n + 3:2 * n + 6]
        outs = refs[2 * n + 6:4 * n + 6]
        token = refs[4 * n + 6]
        for cp in _chip_copies(outs[:n], outs[n:2 * n], _sem_per_peer(send_sems, recv_sems)):
            cp.start()
        token[...] = jnp.zeros_like(token)

    res = pl.pallas_call(
        body,
        in_specs=[HBM_SPACE] * (2 * n),
        out_specs=[SEM_SPACE] * 6 + [HBM_SPACE] * (2 * n) + [pl.BlockSpec(memory_space=pltpu.VMEM)],
        out_shape=[pltpu.SemaphoreType.DMA(())] * 6 + [pltpu.HBM(a.shape, a.dtype) for a in list(sums) + lands]
        + [jax.ShapeDtypeStruct((8, LANES), F32)],
        input_output_aliases={t: t + 6 for t in range(2 * n)},
        compiler_params=pltpu.CompilerParams(has_side_effects=IN_FLIGHT),
        name="chip_exchange_start",
    )(*[pltpu.with_memory_space_constraint(a, pltpu.HBM) for a in list(sums) + lands])
    return list(res[0:3]), list(res[3:6]), list(res[6:n + 6]), list(res[n + 6:2 * n + 6]), res[2 * n + 6]


def chip_exchange_wait(send_sems, recv_sems, sums, lands, after):
    n = len(sums)

    def body(*refs):
        ins_send, ins_recv = refs[2 * n:2 * n + 3], refs[2 * n + 3:2 * n + 6]
        outs = refs[2 * n + 7:]
        for cp in _chip_copies(outs[:n], outs[n:], _sem_per_peer(ins_send, ins_recv)):
            cp.wait()

    res = pl.pallas_call(
        body,
        in_specs=[HBM_SPACE] * (2 * n) + [SEM_SPACE] * 6 + [HBM_REF],
        out_specs=[HBM_SPACE] * (2 * n),
        out_shape=[pltpu.HBM(a.shape, a.dtype) for a in list(sums) + list(lands)],
        input_output_aliases={t: t for t in range(2 * n)},
        compiler_params=pltpu.CompilerParams(has_side_effects=IN_FLIGHT),
        name="chip_exchange_wait",
    )(*sums, *lands, *send_sems, *recv_sems, after)
    return list(res[:n]), list(res[n:])


def chip_sum_into(core, chip, recv, sums, total, lo, name):
    _, n, hr, hc = recv.shape
    tm = _row_tile(hr)
    per = hr // tm

    def body(c_ref, k_ref, r_ref, s_ref, t_ref, o_ref):
        acc = s_ref[...].astype(F32)
        for j in range(N_CHIPS - 1):
            acc = acc + r_ref[j].astype(F32)
        o_ref[...] = acc

    return pl.pallas_call(
        body,
        grid_spec=pltpu.PrefetchScalarGridSpec(
            num_scalar_prefetch=2,
            grid=(n, per),
            in_specs=[pl.BlockSpec((N_CHIPS - 1, None, tm, hc), lambda a, i, c, k: (0, a, i, 0)),
                      pl.BlockSpec((None, None, tm, hc), lambda a, i, c, k: (a, k[0], i, 0)),
                      HBM_REF],
            out_specs=pl.BlockSpec((None, tm, hc), lambda a, i, c, k: (lo + a, per * c[0] + i, 0)),
        ),
        out_shape=jax.ShapeDtypeStruct(total.shape, F32),
        input_output_aliases={4: 0},
        compiler_params=_params("parallel", "parallel"),
        name=name,
    )(core, chip, recv, sums, total)


def sibling_share(totals):
    n = len(totals)

    def body(*refs):
        outs = refs[n:2 * n]
        send_sems, recv_sems = refs[2 * n:]
        x, y, c = _mesh_pos()
        half = lambda t, which: outs[t].at[pl.ds(0, DEPTH), *_core_rows(outs[t], c)[which]]
        sent = [_remote(half(t, 0), half(t, 0), send_sems.at[t], recv_sems.at[t], (x, y, 1 - c)) for t in range(n)]
        for cp in sent:
            cp.start()
        for t in range(n):
            _remote(half(t, 1), half(t, 1), send_sems.at[t], recv_sems.at[t], (x, y, 1 - c)).wait_recv()
        for cp in sent:
            cp.wait_send()

    return pl.pallas_call(
        body,
        in_specs=[HBM_REF] * n,
        out_specs=[HBM_REF] * n,
        out_shape=[jax.ShapeDtypeStruct(t.shape, t.dtype) for t in totals],
        input_output_aliases={t: t for t in range(n)},
        scratch_shapes=[pltpu.SemaphoreType.DMA((n,)), pltpu.SemaphoreType.DMA((n,))],
        name="sibling_share",
    )(*totals)


def small_allgather(small):
    def body(s_ref, a_ref, send_sems, recv_sems, local_sem):
        x, y, c = _mesh_pos()
        me = 4 * x + 2 * y + c
        own = pltpu.make_async_copy(s_ref, a_ref.at[me], local_sem)
        own.start()
        sent = []
        for k in range(1, N_DEV):
            peer = (x ^ (k >> 2), y ^ ((k >> 1) & 1), c ^ (k & 1))
            cp = _remote(s_ref, a_ref.at[me], send_sems.at[k - 1], recv_sems.at[k - 1], peer)
            cp.start()
            sent.append(cp)
        for k in range(1, N_DEV):
            px, py, pc = x ^ (k >> 2), y ^ ((k >> 1) & 1), c ^ (k & 1)
            _remote(s_ref, a_ref.at[4 * px + 2 * py + pc], send_sems.at[k - 1], recv_sems.at[k - 1], (px, py, pc)).wait_recv()
        for cp in sent:
            cp.wait_send()
        own.wait()

    return pl.pallas_call(
        body,
        in_specs=[HBM_REF],
        out_specs=HBM_REF,
        out_shape=jax.ShapeDtypeStruct((N_DEV,) + SMALL_SHAPE, small.dtype),
        scratch_shapes=[pltpu.SemaphoreType.DMA((N_DEV - 1,)), pltpu.SemaphoreType.DMA((N_DEV - 1,)), pltpu.SemaphoreType.DMA],
        name="small_allgather",
    )(small)


def small_sum(blocks):
    def body(a_ref, o_ref):
        acc = a_ref[0]
        for d in range(1, N_DEV):
            acc = acc + a_ref[d]
        o_ref[...] = acc

    return pl.pallas_call(
        body,
        in_specs=[pl.BlockSpec(memory_space=pltpu.VMEM)],
        out_specs=pl.BlockSpec(memory_space=pltpu.VMEM),
        out_shape=jax.ShapeDtypeStruct(SMALL_SHAPE, F32),
        name="small_sum",
    )(blocks)


def pack_small(grads, loss):
    flat = jnp.concatenate([grads[n].reshape(-1) for n in SMALL] + [loss.reshape(-1)])
    size = SMALL_SHAPE[0] * SMALL_SHAPE[1]
    return jnp.pad(flat, (0, size - flat.shape[0])).reshape(SMALL_SHAPE)


def unpack_small(packed):
    flat = packed.reshape(-1)
    out, off = {}, 0
    for n, size in SMALL.items():
        out[n] = flat[off:off + DEPTH * size].reshape(DEPTH, size)
        off += DEPTH * size
    return out, flat[off]


def kernel(x, norm_mix_g, w_in, forget_b, q_norm_g, k_norm_g, w_attn_out, conv_w, w_conv_out, pool_w, pool_scale, w_o, norm_ffn_g, w_ffn_in, w_ffn_out, loss_target, m_norm_mix_g, m_w_in, m_forget_b, m_q_norm_g, m_k_norm_g, m_w_attn_out, m_conv_w, m_w_conv_out, m_pool_w, m_pool_scale, m_w_o, m_norm_ffn_g, m_w_ffn_in, m_w_ffn_out, v_norm_mix_g, v_w_in, v_forget_b, v_q_norm_g, v_k_norm_g, v_w_attn_out, v_conv_w, v_w_conv_out, v_pool_w, v_pool_scale, v_w_o, v_norm_ffn_g, v_w_ffn_in, v_w_ffn_out):
    given = dict(locals())
    weights = {n: given[n] for n in WEIGHTS}

    core = lax.axis_index("c").astype(jnp.int32)
    chip = (2 * lax.axis_index("x") + lax.axis_index("y")).astype(jnp.int32)

    core, chip = core.reshape(1), chip.reshape(1)
    mix_names, all_names = MIX_W + ["conv_w"], MIX_W + FFN_W + ["conv_w"]

    def placed(lo, hi, names, after=None):
        piece = None
        if after is not None:
            piece = _as4(next(iter(after.values())))[0, 0, 0:HALO, 0:LANES] if isinstance(after, dict) else after
        return [place_shard(chip, weights[n], lo, hi, F32 if n == "conv_w" else BF16, "place_" + n, piece)
                for n in names]

    def as_weights(bufs, names):
        return {n: b.reshape(b.shape[:2] + weights[n].shape[1:]) for n, b in zip(names, bufs)}

    def landed(start, names, after):
        return as_weights(gather_forward(gather_wait(*start[:3], after)), names)

    def layer_small(l, *tokens):
        small = {n: weights[n][l] for n in SMALL if n != "conv_w"}
        for t in tokens:
            small["norm_mix_g"] = small["norm_mix_g"] + t[0, 0]
        return small

    done = [None] * DEPTH
    first = gather_start(placed(0, 1, mix_names))
    next_bufs = (placed(0, 1, FFN_W, after=first[3]), placed(1, 2, all_names, after=first[3]))
    first = landed(first, mix_names, next_bufs[1][0])
    (ffn0, layer1), _ = lax.optimization_barrier((next_bufs, first["conv_w"]))
    ffn0, layer1 = gather_start(ffn0), gather_start(layer1)
    xm, wm, sm = mix_fwd(x[0], first, 0, layer_small(0, ffn0[3], layer1[3]))
    xs, wf, sf = ffn_half_fwd(xm, landed(ffn0, FFN_W, xm), 0, weights["norm_ffn_g"][0])
    done[0] = (wm, sm, wf, sf)
    full = landed(layer1, all_names, xs)
    rest = gather_start(placed(2, DEPTH, all_names, after=full))
    xm, wm, sm = mix_fwd(xs, full, 0, layer_small(1, rest[3]))
    xs, wf, sf = ffn_half_fwd(xm, full, 0, weights["norm_ffn_g"][1])
    done[1] = (wm, sm, wf, sf)
    full = landed(rest, all_names, xs)
    for l in range(2, DEPTH):
        xm, wm, sm = mix_fwd(xs, full, l - 2, layer_small(l))
        xs, wf, sf = ffn_half_fwd(xm, full, l - 2, weights["norm_ffn_g"][l])
        done[l] = (wm, sm, wf, sf)
    loss, dx = loss_head(xs, loss_target[0])

    small_grads = [None] * DEPTH
    totals = {n: lax.empty((DEPTH,) + _as4(weights[n][None]).shape[2:], F32) for n in SPLIT}

    def buffers(names, n_layers):
        return {n: lax.empty((n_layers, N_CHIPS) + weights[n].shape[1:], F32) for n in names}

    def pair_sums(names, grads, theirs):
        return [pair_sum(core, a, t, "pair_sum_" + n) for n, a, t in zip(names, grads, theirs)]

    def add_chips(names, recv, sums, lo):
        for n, r, s in zip(names, recv, sums):
            totals[n] = chip_sum_into(core, chip, r, s, totals[n], lo, "chip_sum_" + n)

    big = buffers(SPLIT, DEPTH - 1)
    for l in reversed(range(1, DEPTH)):
        wm, sm, wf, sf = done[l]
        dxm, ffn_big, g_ffn = ffn_half_bwd(dx, wf, sf, l - 1, {n: big[n] for n in FFN_W})
        dx, mix_big, g_mix = mix_bwd(dxm, wm, sm, l - 1, {n: big[n] for n in MIX_W})
        big = {**ffn_big, **mix_big}
        small_grads[l] = {**g_ffn, **g_mix}
    pair = pair_exchange_start([_as4(big[n]) for n in SPLIT])

    wm, sm, wf, sf = done[0]
    wf = dict(wf, g2=wf["g2"] + pair[4][0:1, 0:1])
    dxm, ffn_big, g_ffn = ffn_half_bwd(dx, wf, sf, 0, buffers(FFN_W, 1))
    chips = chip_exchange_start(pair_sums(SPLIT, *pair_exchange_wait(*pair[:4], ffn_big["w_ffn_in"])))
    ffn_grads = [_as4(ffn_big[n]) for n in FFN_W]
    ffn_chips = chip_exchange_start(pair_sums(FFN_W, ffn_grads, pair_exchange(ffn_grads)))
    wm = dict(wm, pscale=wm["pscale"] + chips[4][0:1, 0:1] + ffn_chips[4][0:1, 0:1])
    dproj, mix_big, g_mix = mix_bwd_weights(dxm, wm, sm, 0, buffers(MIX_W, 1))
    mix_grads = [_as4(mix_big[n]) for n in MIX_W]
    mix_chips = chip_exchange_start(pair_sums(MIX_W, mix_grads, pair_exchange(mix_grads)))
    dx, g_in = mix_bwd_input(dproj, dxm, dict(wm, g1=wm["g1"] + mix_chips[4][0:1, 0:1]), sm)
    small_grads[0] = {**g_ffn, **g_mix, **g_in}

    reduced, deltas, new_m, new_v = {}, {}, {}, {}

    def update(names):
        for n in names:
            w = weights[n]
            flat = (-1, w.shape[-1])
            d, nm, nv = adamw(w.reshape(flat), reduced[n].reshape(flat), given["m_" + n].reshape(flat),
                              given["v_" + n].reshape(flat), "adamw_" + n)
            deltas[n], new_m[n], new_v[n] = d.reshape(w.shape), nm.reshape(w.shape), nv.reshape(w.shape)

    def share(names):
        for n, t in zip(names, sibling_share([totals[n] for n in names])):
            reduced[n] = t.reshape(weights[n].shape)

    sums, recv = chip_exchange_wait(*chips[:4], dx)
    add_chips(SPLIT, recv, sums, 1)
    sums, recv = chip_exchange_wait(*ffn_chips[:4], dx)
    add_chips(FFN_W, recv, sums, 0)
    share(FFN_W)
    update(FFN_W)
    sums, recv = chip_exchange_wait(*mix_chips[:4], deltas["w_ffn_in"])
    add_chips(MIX_W, recv, sums, 0)
    share(MIX_W)

    small_grads = {n: jnp.stack([g[n] for g in small_grads]) for n in SMALL}
    small_total, loss_sum = unpack_small(small_sum(small_allgather(pack_small(small_grads, loss))))
    chip = chip[0]
    cols = D_CONV // N_CHIPS
    small_total["conv_w"] = lax.dynamic_slice_in_dim(small_total["conv_w"].reshape(DEPTH, CONV_K, D_CONV), chip * cols, cols, axis=2)
    for n in SMALL:
        reduced[n] = small_total[n].reshape(weights[n].shape)
    update(MIX_W + list(SMALL))

    return (loss_sum, dx[None], *[reduced[n] for n in WEIGHTS], *[deltas[n] for n in WEIGHTS],
            *[new_m[n] for n in WEIGHTS], *[new_v[n] for n in WEIGHTS])
```
